```python
import math
import jax, jax.numpy as jnp
from jax import lax
import numpy as np

D_MODEL = 1024
BATCH = 8
SEQ = 4096
DEPTH = 1

CHUNK = 64
Q_BLOCK = 128
HEAD_DIM = 64
N_RET_HEADS = 8
N_FOX_HEADS = 8
RET_WIDTH = N_RET_HEADS * HEAD_DIM
FOX_WIDTH = N_FOX_HEADS * HEAD_DIM
MIX_WIDTH = RET_WIDTH + FOX_WIDTH
N_MEM = 256
N_XATTN_HEADS = 4
XATTN_HEAD_DIM = D_MODEL // N_XATTN_HEADS
D_FF = -(-8 * D_MODEL // (3 * 256)) * 256
ROPE_BASE = 10000.0
EPS = 1e-6
NEG_INF = -1e30
IN_SIZES = (RET_WIDTH, RET_WIDTH, RET_WIDTH, RET_WIDTH, FOX_WIDTH, FOX_WIDTH, FOX_WIDTH, N_FOX_HEADS)
IN_WIDTH = sum(IN_SIZES)

kernel_name = "hymba_retention_fox_hybrid_block"


def _split_points(sizes):
    pts, acc = [], 0
    for s in sizes[:-1]:
        acc += s
        pts.append(acc)
    return pts


def rmsnorm(x, g):
    xf = x.astype(jnp.float32)
    y = xf * lax.rsqrt(jnp.mean(xf * xf, axis=-1, keepdims=True) + EPS)
    return (y * g.astype(jnp.float32)).astype(x.dtype)


def head_group_norm(x, g):
    xf = x.astype(jnp.float32)
    mu = jnp.mean(xf, axis=-1, keepdims=True)
    xc = xf - mu
    var = jnp.mean(xc * xc, axis=-1, keepdims=True)
    return (xc * lax.rsqrt(var + EPS) * g.astype(jnp.float32)).astype(x.dtype)


def rotary(x, pos):
    d = x.shape[-1]
    inv_freq = ROPE_BASE ** (-jnp.arange(0, d, 2, dtype=jnp.float32) / d)
    ang = pos[:, None] * inv_freq[None, :]
    cos = jnp.cos(ang)[:, None, :].astype(x.dtype)
    sin = jnp.sin(ang)[:, None, :].astype(x.dtype)
    x1, x2 = x[..., : d // 2], x[..., d // 2:]
    return jnp.concatenate([x1 * cos - x2 * sin, x1 * sin + x2 * cos], axis=-1)


def chunk_retention(q, k, v):
    B, T, H, d = q.shape
    dv = v.shape[-1]
    nc = T // CHUNK
    dt = q.dtype
    log_g = jnp.log(1.0 - 2.0 ** (-5.0 - jnp.arange(H, dtype=jnp.float32)))
    idx = jnp.arange(CHUNK, dtype=jnp.float32)
    intra_decay = jnp.exp(log_g[:, None, None] * jnp.abs(idx[:, None] - idx[None, :])).astype(dt)
    q_decay = jnp.exp(log_g[None, :] * (idx[:, None] + 1.0)).astype(dt)
    k_decay = jnp.exp(log_g[None, :] * (CHUNK - 1.0 - idx[:, None])).astype(dt)
    chunk_decay = jnp.exp(log_g * CHUNK).astype(dt)[:, None, None]

    qc = (q * (d ** -0.5)).reshape(B, nc, CHUNK, H, d)
    kc = k.reshape(B, nc, CHUNK, H, d)
    vc = v.reshape(B, nc, CHUNK, H, dv)

    scores = jnp.einsum('bnihd,bnjhd->bnhij', qc, kc) * intra_decay
    intra = jnp.einsum('bnhij,bnjhe->bnihe', scores, vc)

    kv = jnp.einsum('bnjhd,bnjhe->nbhde', kc * k_decay[:, :, None], vc)

    def step(state, kv_c):
        return state * chunk_decay + kv_c, state

    _, s_prev = lax.scan(step, jnp.zeros((B, H, d, dv), kv.dtype), kv)
    inter = jnp.einsum('bnihd,nbhde->bnihe', qc * q_decay[:, :, None], s_prev)
    return (intra + inter).reshape(B, T, H, dv)


def forgetting_attention(q, k, v, log_f):
    B, T, H, d = q.shape
    F = jnp.cumsum(log_f, axis=1).transpose(0, 2, 1)
    qh = (q * (d ** -0.5)).transpose(0, 2, 1, 3)
    kh = k.transpose(0, 2, 1, 3)
    vh = v.transpose(0, 2, 1, 3)
    qpos = jnp.arange(Q_BLOCK)
    outs = []
    for blk in range(T // Q_BLOCK):
        q0 = blk * Q_BLOCK
        kend = q0 + Q_BLOCK
        logits = jnp.einsum('bhqd,bhkd->bhqk', qh[:, :, q0:kend], kh[:, :, :kend]).astype(jnp.float32)
        logits = logits + F[:, :, q0:kend, None] - F[:, :, None, :kend]
        causal = (q0 + qpos)[:, None] >= jnp.arange(kend)[None, :]
        logits = jnp.where(causal, logits, NEG_INF)
        p = jax.nn.softmax(logits, axis=-1).astype(v.dtype)
        outs.append(jnp.einsum('bhqk,bhkd->bhqd', p, vh[:, :, :kend]))
    return jnp.concatenate(outs, axis=2).transpose(0, 2, 1, 3)


def memory_cross_attention(hn, mem, w_xq, w_xkv, g_mem, g_xq, g_xk, w_xo):
    B, T, _ = hn.shape
    M = mem.shape[1]
    q = (hn @ w_xq).reshape(B, T, N_XATTN_HEADS, XATTN_HEAD_DIM)
    q = rmsnorm(q, g_xq)
    kv = rmsnorm(mem, g_mem) @ w_xkv
    k, v = jnp.split(kv, 2, axis=-1)
    k = rmsnorm(k.reshape(B, M, N_XATTN_HEADS, XATTN_HEAD_DIM), g_xk)
    v = v.reshape(B, M, N_XATTN_HEADS, XATTN_HEAD_DIM)
    logits = jnp.einsum('bthd,bmhd->bhtm', q, k).astype(jnp.float32) * (XATTN_HEAD_DIM ** -0.5)
    p = jax.nn.softmax(logits, axis=-1).astype(v.dtype)
    o = jnp.einsum('bhtm,bmhd->bthd', p, v).reshape(B, T, D_MODEL)
    return o @ w_xo


def _fwd_setup_inputs(seed: int = 0) -> dict:
    key = jax.random.key(seed)
    ks = jax.random.split(key, 24)
    f32 = jnp.float32
    D = D_MODEL

    def w(k, shape, fan_in):
        return jax.random.normal(k, shape, f32) * fan_in ** -0.5

    def gain(k, shape):
        return 1.0 + 0.02 * jax.random.normal(k, shape, f32)

    return {
        "x": jax.random.normal(ks[0], (BATCH, SEQ, D), f32),
        "mem": jax.random.normal(ks[1], (BATCH, N_MEM, D), f32),
        "g_mix": gain(ks[2], (DEPTH, D)),
        "w_in": w(ks[3], (DEPTH, D, IN_WIDTH), D),
        "b_forget": jax.random.uniform(ks[4], (DEPTH, N_FOX_HEADS), f32, 1.0, 4.0),
        "g_ret_out": gain(ks[5], (DEPTH, N_RET_HEADS, HEAD_DIM)),
        "g_fox_q": gain(ks[6], (DEPTH, HEAD_DIM)),
        "g_fox_k": gain(ks[7], (DEPTH, HEAD_DIM)),
        "w_out": w(ks[8], (DEPTH, MIX_WIDTH, D), MIX_WIDTH),
        "g_xattn": gain(ks[9], (DEPTH, D)),
        "w_xq": w(ks[10], (DEPTH, D, D), D),
        "w_xkv": w(ks[11], (DEPTH, D, 2 * D), D),
        "g_mem": gain(ks[12], (DEPTH, D)),
        "g_xq": gain(ks[13], (DEPTH, XATTN_HEAD_DIM)),
        "g_xk": gain(ks[14], (DEPTH, XATTN_HEAD_DIM)),
        "w_xo": w(ks[15], (DEPTH, D, D), D),
        "g_ffn": gain(ks[16], (DEPTH, D)),
        "w_gate": w(ks[17], (DEPTH, D, D_FF), D),
        "w_up": w(ks[18], (DEPTH, D, D_FF), D),
        "w_down": w(ks[19], (DEPTH, D_FF, D), D_FF),
    }


def _fwd_reference(x, mem, g_mix, w_in, b_forget, g_ret_out, g_fox_q, g_fox_k, w_out,
              g_xattn, w_xq, w_xkv, g_mem, g_xq, g_xk, w_xo,
              g_ffn, w_gate, w_up, w_down):
    B, T, _ = x.shape
    pos = jnp.arange(T, dtype=jnp.float32)
    splits = _split_points(IN_SIZES)
    h = x
    for l in range(DEPTH):
        hn = rmsnorm(h, g_mix[l])
        proj = hn @ w_in[l]
        rq, rk, rv, rg, fq, fk, fv, ff = jnp.split(proj, splits, axis=-1)

        rq = rotary(rq.reshape(B, T, N_RET_HEADS, HEAD_DIM), pos)
        rk = rotary(rk.reshape(B, T, N_RET_HEADS, HEAD_DIM), pos)
        ret = chunk_retention(rq, rk, rv.reshape(B, T, N_RET_HEADS, HEAD_DIM))
        ret = head_group_norm(ret, g_ret_out[l]).reshape(B, T, RET_WIDTH)
        ret = jax.nn.silu(rg) * ret

        fq = rmsnorm(fq.reshape(B, T, N_FOX_HEADS, HEAD_DIM), g_fox_q[l])
        fk = rmsnorm(fk.reshape(B, T, N_FOX_HEADS, HEAD_DIM), g_fox_k[l])
        log_f = jax.nn.log_sigmoid(ff.astype(jnp.float32) + b_forget[l].astype(jnp.float32))
        fox = forgetting_attention(fq, fk, fv.reshape(B, T, N_FOX_HEADS, HEAD_DIM), log_f)
        fox = fox.reshape(B, T, FOX_WIDTH)

        h = h + jnp.concatenate([ret, fox], axis=-1) @ w_out[l]

        h = h + memory_cross_attention(rmsnorm(h, g_xattn[l]), mem, w_xq[l], w_xkv[l],
                                       g_mem[l], g_xq[l], g_xk[l], w_xo[l])

        hn = rmsnorm(h, g_ffn[l])
        h = h + (jax.nn.silu(hn @ w_gate[l]) * (hn @ w_up[l])) @ w_down[l]
    return h


import jax as _jax
import jax.numpy as _jnp

TWIN_FORMAT = 'train_step'
FWD_PARAMS = ['x', 'mem', 'g_mix', 'w_in', 'b_forget', 'g_ret_out', 'g_fox_q', 'g_fox_k', 'w_out', 'g_xattn', 'w_xq', 'w_xkv', 'g_mem', 'g_xq', 'g_xk', 'w_xo', 'g_ffn', 'w_gate', 'w_up', 'w_down']
TWIN_WEIGHTS = ['g_mix', 'w_in', 'b_forget', 'g_ret_out', 'g_fox_q', 'g_fox_k', 'w_out', 'g_xattn', 'w_xq', 'w_xkv', 'g_mem', 'g_xq', 'g_xk', 'w_xo', 'g_ffn', 'w_gate', 'w_up', 'w_down']
TWIN_DIFF_INPUT = 'x'
TWIN_INPUTS = ['x', 'mem', 'g_mix', 'w_in', 'b_forget', 'g_ret_out', 'g_fox_q', 'g_fox_k', 'w_out', 'g_xattn', 'w_xq', 'w_xkv', 'g_mem', 'g_xq', 'g_xk', 'w_xo', 'g_ffn', 'w_gate', 'w_up', 'w_down', 'loss_target', 'm_g_mix', 'm_w_in', 'm_b_forget', 'm_g_ret_out', 'm_g_fox_q', 'm_g_fox_k', 'm_w_out', 'm_g_xattn', 'm_w_xq', 'm_w_xkv', 'm_g_mem', 'm_g_xq', 'm_g_xk', 'm_w_xo', 'm_g_ffn', 'm_w_gate', 'm_w_up', 'm_w_down', 'v_g_mix', 'v_w_in', 'v_b_forget', 'v_g_ret_out', 'v_g_fox_q', 'v_g_fox_k', 'v_w_out', 'v_g_xattn', 'v_w_xq', 'v_w_xkv', 'v_g_mem', 'v_g_xq', 'v_g_xk', 'v_w_xo', 'v_g_ffn', 'v_w_gate', 'v_w_up', 'v_w_down']
TWIN_OUTPUTS = ['loss', 'grad_x', 'grad_g_mix', 'grad_w_in', 'grad_b_forget', 'grad_g_ret_out', 'grad_g_fox_q', 'grad_g_fox_k', 'grad_w_out', 'grad_g_xattn', 'grad_w_xq', 'grad_w_xkv', 'grad_g_mem', 'grad_g_xq', 'grad_g_xk', 'grad_w_xo', 'grad_g_ffn', 'grad_w_gate', 'grad_w_up', 'grad_w_down', 'delta_g_mix', 'delta_w_in', 'delta_b_forget', 'delta_g_ret_out', 'delta_g_fox_q', 'delta_g_fox_k', 'delta_w_out', 'delta_g_xattn', 'delta_w_xq', 'delta_w_xkv', 'delta_g_mem', 'delta_g_xq', 'delta_g_xk', 'delta_w_xo', 'delta_g_ffn', 'delta_w_gate', 'delta_w_up', 'delta_w_down', 'new_m_g_mix', 'new_m_w_in', 'new_m_b_forget', 'new_m_g_ret_out', 'new_m_g_fox_q', 'new_m_g_fox_k', 'new_m_w_out', 'new_m_g_xattn', 'new_m_w_xq', 'new_m_w_xkv', 'new_m_g_mem', 'new_m_g_xq', 'new_m_g_xk', 'new_m_w_xo', 'new_m_g_ffn', 'new_m_w_gate', 'new_m_w_up', 'new_m_w_down', 'new_v_g_mix', 'new_v_w_in', 'new_v_b_forget', 'new_v_g_ret_out', 'new_v_g_fox_q', 'new_v_g_fox_k', 'new_v_w_out', 'new_v_g_xattn', 'new_v_w_xq', 'new_v_w_xkv', 'new_v_g_mem', 'new_v_g_xq', 'new_v_g_xk', 'new_v_w_xo', 'new_v_g_ffn', 'new_v_w_gate', 'new_v_w_up', 'new_v_w_down']
TWIN_LEAF_KINDS = {'loss': 'loss', 'grad_x': 'grad_x', 'grad_g_mix': 'grad_w', 'grad_w_in': 'grad_w', 'grad_b_forget': 'grad_w', 'grad_g_ret_out': 'grad_w', 'grad_g_fox_q': 'grad_w', 'grad_g_fox_k': 'grad_w', 'grad_w_out': 'grad_w', 'grad_g_xattn': 'grad_w', 'grad_w_xq': 'grad_w', 'grad_w_xkv': 'grad_w', 'grad_g_mem': 'grad_w', 'grad_g_xq': 'grad_w', 'grad_g_xk': 'grad_w', 'grad_w_xo': 'grad_w', 'grad_g_ffn': 'grad_w', 'grad_w_gate': 'grad_w', 'grad_w_up': 'grad_w', 'grad_w_down': 'grad_w', 'delta_g_mix': 'delta_w', 'delta_w_in': 'delta_w', 'delta_b_forget': 'delta_w', 'delta_g_ret_out': 'delta_w', 'delta_g_fox_q': 'delta_w', 'delta_g_fox_k': 'delta_w', 'delta_w_out': 'delta_w', 'delta_g_xattn': 'delta_w', 'delta_w_xq': 'delta_w', 'delta_w_xkv': 'delta_w', 'delta_g_mem': 'delta_w', 'delta_g_xq': 'delta_w', 'delta_g_xk': 'delta_w', 'delta_w_xo': 'delta_w', 'delta_g_ffn': 'delta_w', 'delta_w_gate': 'delta_w', 'delta_w_up': 'delta_w', 'delta_w_down': 'delta_w', 'new_m_g_mix': 'new_m', 'new_m_w_in': 'new_m', 'new_m_b_forget': 'new_m', 'new_m_g_ret_out': 'new_m', 'new_m_g_fox_q': 'new_m', 'new_m_g_fox_k': 'new_m', 'new_m_w_out': 'new_m', 'new_m_g_xattn': 'new_m', 'new_m_w_xq': 'new_m', 'new_m_w_xkv': 'new_m', 'new_m_g_mem': 'new_m', 'new_m_g_xq': 'new_m', 'new_m_g_xk': 'new_m', 'new_m_w_xo': 'new_m', 'new_m_g_ffn': 'new_m', 'new_m_w_gate': 'new_m', 'new_m_w_up': 'new_m', 'new_m_w_down': 'new_m', 'new_v_g_mix': 'new_v', 'new_v_w_in': 'new_v', 'new_v_b_forget': 'new_v', 'new_v_g_ret_out': 'new_v', 'new_v_g_fox_q': 'new_v', 'new_v_g_fox_k': 'new_v', 'new_v_w_out': 'new_v', 'new_v_g_xattn': 'new_v', 'new_v_w_xq': 'new_v', 'new_v_w_xkv': 'new_v', 'new_v_g_mem': 'new_v', 'new_v_g_xq': 'new_v', 'new_v_g_xk': 'new_v', 'new_v_w_xo': 'new_v', 'new_v_g_ffn': 'new_v', 'new_v_w_gate': 'new_v', 'new_v_w_up': 'new_v', 'new_v_w_down': 'new_v'}


def _forward(args):
    return _fwd_reference(*[args[k] for k in FWD_PARAMS])


def _output_shape():
    def fwd():
        inp = _fwd_setup_inputs(0)
        return _fwd_reference(*[inp[k] for k in FWD_PARAMS])
    out = _jax.eval_shape(fwd)
    return out.shape, out.dtype

N_MICROBATCH = 1
ADAM_LR = 0.001
ADAM_B1 = 0.9
ADAM_B2 = 0.999
ADAM_EPS = 1e-08
ADAM_WD = 0.01
ADAM_STEP = 10
PER_EXAMPLE_BATCH_AXIS = {'x': 0, 'mem': 0, 'loss_target': 0}
SHARED_INPUTS = []
_WEIGHT_DTYPES = {'g_mix': _jnp.float32, 'w_in': _jnp.float32, 'b_forget': _jnp.float32, 'g_ret_out': _jnp.float32, 'g_fox_q': _jnp.float32, 'g_fox_k': _jnp.float32, 'w_out': _jnp.float32, 'g_xattn': _jnp.float32, 'w_xq': _jnp.float32, 'w_xkv': _jnp.float32, 'g_mem': _jnp.float32, 'g_xq': _jnp.float32, 'g_xk': _jnp.float32, 'w_xo': _jnp.float32, 'g_ffn': _jnp.float32, 'w_gate': _jnp.float32, 'w_up': _jnp.float32, 'w_down': _jnp.float32}
MOMENT_SCALE = {'g_mix': 9.385557e+00, 'w_in': 2.846208e-01, 'b_forget': 9.747584e+01, 'g_ret_out': 1.119193e+01, 'g_fox_q': 1.364736e+01, 'g_fox_k': 1.365711e+01, 'w_out': 3.086316e-01, 'g_xattn': 5.371601e-02, 'w_xq': 5.244121e-02, 'w_xkv': 9.351224e-02, 'g_mem': 3.637935e-01, 'g_xq': 1.293149e+00, 'g_xk': 1.293219e+00, 'w_xo': 1.156377e-01, 'g_ffn': 2.474170e+01, 'w_gate': 1.750909e-01, 'w_up': 1.886271e-01, 'w_down': 2.959447e-01}


def _to_microbatches(a, axis):
    t = _jnp.moveaxis(a, axis, 0)
    t = t.reshape((N_MICROBATCH, t.shape[0] // N_MICROBATCH) + t.shape[1:])
    return _jnp.moveaxis(t, 1, axis + 1)


def setup_inputs(seed: int = 0) -> dict:
    inp = _fwd_setup_inputs(seed)
    key = _jax.random.fold_in(_jax.random.key(seed), 7919)
    shape, _ = _output_shape()
    out = dict(inp)
    out["loss_target"] = _jax.random.normal(_jax.random.fold_in(key, 0), shape, _jnp.float32)
    for i, name in enumerate(TWIN_WEIGHTS):
        w = inp[name].astype(_jnp.float32)
        if MOMENT_SCALE is None:
            s = _jnp.sqrt(_jnp.mean(_jnp.square(w)) + 1e-30)
        else:
            s = MOMENT_SCALE[name]
        km, kv = _jax.random.split(_jax.random.fold_in(key, i + 1))
        out[name] = w
        out["m_" + name] = s * _jax.random.normal(km, w.shape, _jnp.float32)
        out["v_" + name] = (s * s) * _jax.random.uniform(kv, w.shape, _jnp.float32, 0.5, 1.5)
    if N_MICROBATCH > 1:
        for name, axis in PER_EXAMPLE_BATCH_AXIS.items():
            out[name] = _to_microbatches(out[name], axis)
    return {'x': out['x'], 'mem': out['mem'], 'g_mix': out['g_mix'], 'w_in': out['w_in'], 'b_forget': out['b_forget'], 'g_ret_out': out['g_ret_out'], 'g_fox_q': out['g_fox_q'], 'g_fox_k': out['g_fox_k'], 'w_out': out['w_out'], 'g_xattn': out['g_xattn'], 'w_xq': out['w_xq'], 'w_xkv': out['w_xkv'], 'g_mem': out['g_mem'], 'g_xq': out['g_xq'], 'g_xk': out['g_xk'], 'w_xo': out['w_xo'], 'g_ffn': out['g_ffn'], 'w_gate': out['w_gate'], 'w_up': out['w_up'], 'w_down': out['w_down'], 'loss_target': out['loss_target'], 'm_g_mix': out['m_g_mix'], 'm_w_in': out['m_w_in'], 'm_b_forget': out['m_b_forget'], 'm_g_ret_out': out['m_g_ret_out'], 'm_g_fox_q': out['m_g_fox_q'], 'm_g_fox_k': out['m_g_fox_k'], 'm_w_out': out['m_w_out'], 'm_g_xattn': out['m_g_xattn'], 'm_w_xq': out['m_w_xq'], 'm_w_xkv': out['m_w_xkv'], 'm_g_mem': out['m_g_mem'], 'm_g_xq': out['m_g_xq'], 'm_g_xk': out['m_g_xk'], 'm_w_xo': out['m_w_xo'], 'm_g_ffn': out['m_g_ffn'], 'm_w_gate': out['m_w_gate'], 'm_w_up': out['m_w_up'], 'm_w_down': out['m_w_down'], 'v_g_mix': out['v_g_mix'], 'v_w_in': out['v_w_in'], 'v_b_forget': out['v_b_forget'], 'v_g_ret_out': out['v_g_ret_out'], 'v_g_fox_q': out['v_g_fox_q'], 'v_g_fox_k': out['v_g_fox_k'], 'v_w_out': out['v_w_out'], 'v_g_xattn': out['v_g_xattn'], 'v_w_xq': out['v_w_xq'], 'v_w_xkv': out['v_w_xkv'], 'v_g_mem': out['v_g_mem'], 'v_g_xq': out['v_g_xq'], 'v_g_xk': out['v_g_xk'], 'v_w_xo': out['v_w_xo'], 'v_g_ffn': out['v_g_ffn'], 'v_w_gate': out['v_w_gate'], 'v_w_up': out['v_w_up'], 'v_w_down': out['v_w_down']}


def _loss(weights, diff, rest, loss_target):
    with _jax.named_scope("forward"):
        args = {**rest, TWIN_DIFF_INPUT: diff, **{k: w.astype(_WEIGHT_DTYPES[k]) for k, w in weights.items()}}
        y = _forward(args)
    with _jax.named_scope("loss_head"):
        err = _jnp.square(y.astype(_jnp.float32) - loss_target)
        return 0.5 * _jnp.sum(_jnp.mean(err, axis=-1)) if err.ndim else 0.5 * err


def _adamw(w, g, m, v):
    m = ADAM_B1 * m + (1.0 - ADAM_B1) * g
    v = ADAM_B2 * v + (1.0 - ADAM_B2) * _jnp.square(g)
    m_hat = m / (1.0 - ADAM_B1 ** ADAM_STEP)
    v_hat = v / (1.0 - ADAM_B2 ** ADAM_STEP)
    delta = -ADAM_LR * (m_hat / (_jnp.sqrt(v_hat) + ADAM_EPS) + ADAM_WD * w)
    return delta, m, v


def reference(x, mem, g_mix, w_in, b_forget, g_ret_out, g_fox_q, g_fox_k, w_out, g_xattn, w_xq, w_xkv, g_mem, g_xq, g_xk, w_xo, g_ffn, w_gate, w_up, w_down, loss_target, m_g_mix, m_w_in, m_b_forget, m_g_ret_out, m_g_fox_q, m_g_fox_k, m_w_out, m_g_xattn, m_w_xq, m_w_xkv, m_g_mem, m_g_xq, m_g_xk, m_w_xo, m_g_ffn, m_w_gate, m_w_up, m_w_down, v_g_mix, v_w_in, v_b_forget, v_g_ret_out, v_g_fox_q, v_g_fox_k, v_w_out, v_g_xattn, v_w_xq, v_w_xkv, v_g_mem, v_g_xq, v_g_xk, v_w_xo, v_g_ffn, v_w_gate, v_w_up, v_w_down):
    given = dict(x=x, mem=mem, g_mix=g_mix, w_in=w_in, b_forget=b_forget, g_ret_out=g_ret_out, g_fox_q=g_fox_q, g_fox_k=g_fox_k, w_out=w_out, g_xattn=g_xattn, w_xq=w_xq, w_xkv=w_xkv, g_mem=g_mem, g_xq=g_xq, g_xk=g_xk, w_xo=w_xo, g_ffn=g_ffn, w_gate=w_gate, w_up=w_up, w_down=w_down, loss_target=loss_target, m_g_mix=m_g_mix, m_w_in=m_w_in, m_b_forget=m_b_forget, m_g_ret_out=m_g_ret_out, m_g_fox_q=m_g_fox_q, m_g_fox_k=m_g_fox_k, m_w_out=m_w_out, m_g_xattn=m_g_xattn, m_w_xq=m_w_xq, m_w_xkv=m_w_xkv, m_g_mem=m_g_mem, m_g_xq=m_g_xq, m_g_xk=m_g_xk, m_w_xo=m_w_xo, m_g_ffn=m_g_ffn, m_w_gate=m_w_gate, m_w_up=m_w_up, m_w_down=m_w_down, v_g_mix=v_g_mix, v_w_in=v_w_in, v_b_forget=v_b_forget, v_g_ret_out=v_g_ret_out, v_g_fox_q=v_g_fox_q, v_g_fox_k=v_g_fox_k, v_w_out=v_w_out, v_g_xattn=v_g_xattn, v_w_xq=v_w_xq, v_w_xkv=v_w_xkv, v_g_mem=v_g_mem, v_g_xq=v_g_xq, v_g_xk=v_g_xk, v_w_xo=v_w_xo, v_g_ffn=v_g_ffn, v_w_gate=v_w_gate, v_w_up=v_w_up, v_w_down=v_w_down)
    weights = {n: given[n] for n in TWIN_WEIGHTS}
    shared = {n: given[n] for n in SHARED_INPUTS}
    per_example = {n: given[n] for n in ['x', 'mem']}
    grad_fn = _jax.value_and_grad(_loss, argnums=(0, 1))

    def one_microbatch(ex, loss_target):
        ex = dict(ex)
        diff = ex.pop(TWIN_DIFF_INPUT)
        return grad_fn(weights, diff, {**shared, **ex}, loss_target)

    if N_MICROBATCH == 1:
        loss, (grad_w, grad_x) = one_microbatch(per_example, given["loss_target"])
    else:
        def body(carry, xs):
            loss_sum, grad_sum = carry
            l_k, (gw_k, gx_k) = one_microbatch(xs[0], xs[1])
            with _jax.named_scope("update"):
                return (loss_sum + l_k, _jax.tree.map(_jnp.add, grad_sum, gw_k)), gx_k

        init = (_jnp.zeros((), _jnp.float32), _jax.tree.map(_jnp.zeros_like, weights))
        (loss, grad_w), grad_x = _jax.lax.scan(body, init, (per_example, given["loss_target"]))
    with _jax.named_scope("update"):
        delta_w, new_m, new_v = {}, {}, {}
        for n in TWIN_WEIGHTS:
            delta_w[n], new_m[n], new_v[n] = _adamw(weights[n], grad_w[n], given["m_" + n], given["v_" + n])
    return (loss, grad_x, *[grad_w[n] for n in TWIN_WEIGHTS], *[delta_w[n] for n in TWIN_WEIGHTS],
            *[new_m[n] for n in TWIN_WEIGHTS], *[new_v[n] for n in TWIN_WEIGHTS])
```

```python
import functools
import math

import jax
import jax.numpy as jnp
import numpy as np
from jax import lax
from jax.experimental import pallas as pl
from jax.experimental.pallas import tpu as pltpu

F32 = jnp.float32
BF = jnp.bfloat16

D = 1024
HEAD = 64
CHUNK = 64
N_MEM = 256
XHEAD = 256
D_FF = 2816
EPS = 1e-6
NEG = -1e30
LANES = 128
N_DEV = 8
V7X_VMEM_BYTES = 64 * 1024 * 1024
VMEM_LIMIT = V7X_VMEM_BYTES - 8 * 1024 * 1024

ADAM_LR, ADAM_B1, ADAM_B2, ADAM_EPS, ADAM_WD, ADAM_STEP = 0.001, 0.9, 0.999, 1e-08, 0.01, 10

W_LAYOUT = (("w_in", 449, 464, True), ("w_out", 128, 128, False), ("w_xq", 128, 128, False), ("w_xkv", 256, 256, True),
            ("w_xo", 128, 128, False), ("w_gate", 352, 352, True), ("w_up", 352, 352, True), ("w_down", 352, 352, False))
W_ROWS = sum(w[2] for w in W_LAYOUT)
W_OFF = {}
_o = 0
for _n, _r, _p, _t in W_LAYOUT:
    W_OFF[_n] = _o
    _o += _p
SMALL_ROWS = 8

NT = (((1,), (1,)), ((), ()))
NN = (((1,), (0,)), ((), ()))
TN = (((0,), (0,)), ((), ()))
_DIMS = {"nn": NN, "nt": NT, "tn": TN}


def _params(sem):
    return pltpu.CompilerParams(dimension_semantics=sem, vmem_limit_bytes=VMEM_LIMIT)


def _mm(name, products, extras, epilogue, M, N, tm, tn, out_dtypes):
    flat = [t for p in products for t in p]
    counts = [len(p) for p in products]
    in_specs, args = [], []
    for a, b, form in flat:
        if form == "tn":
            in_specs.append(pl.BlockSpec((a.shape[0], tm), lambda i, j: (0, i)))
        else:
            in_specs.append(pl.BlockSpec((tm, a.shape[1]), lambda i, j: (i, 0)))
        if form == "nt":
            in_specs.append(pl.BlockSpec((tn, b.shape[1]), lambda i, j: (j, 0)))
        else:
            in_specs.append(pl.BlockSpec((b.shape[0], tn), lambda i, j: (0, j)))
        args += [a, b]
    for e in extras:
        in_specs.append(pl.BlockSpec((tm, tn), lambda i, j: (i, j)))
        args.append(e)
    n_in = len(args)

    def body(*refs):
        ins, outs = refs[:n_in], refs[n_in:]
        prods, p = [], 0
        for c in counts:
            acc = None
            for _ in range(c):
                a = ins[2 * p][...].astype(BF)
                b = ins[2 * p + 1][...].astype(BF)
                d = lax.dot_general(a, b, _DIMS[flat[p][2]], preferred_element_type=F32)
                acc = d if acc is None else acc + d
                p += 1
            prods.append(acc)
        ex = [r[...].astype(F32) for r in ins[2 * len(flat):]]
        res = epilogue(*prods, *ex)
        for o, r in zip(outs, res):
            o[...] = r.astype(o.dtype)

    return pl.pallas_call(
        body, name=name, grid=(M // tm, N // tn), in_specs=in_specs,
        out_specs=[pl.BlockSpec((tm, tn), lambda i, j: (i, j)) for _ in out_dtypes],
        out_shape=[jax.ShapeDtypeStruct((M, N), dt) for dt in out_dtypes],
        compiler_params=_params(("parallel", "arbitrary")),
    )(*args)


def _ident(x):
    return (x,)


def _add(x, r):
    return (x + r,)


def _spec(rows, w, off, per_j):
    if per_j:
        return pl.BlockSpec((rows, w), lambda j, i: (i, off + j))
    return pl.BlockSpec((rows, w), lambda j, i: (i, off))


def _pspec(rows, w, off, per_j):
    if per_j:
        return pl.BlockSpec((rows, w), lambda j, i: (0, off + j))
    return pl.BlockSpec((rows, w), lambda j, i: (0, off))


def _rw_fwd(name, fn, rows, params, outs, T, tm, nj, n_acc=0):
    in_specs = [_spec(tm, w, off, pj) for _, w, off, pj in rows] + [_pspec(a.shape[0], w, off, pj) for a, w, off, pj in params]
    args = [r[0] for r in rows] + [p[0] for p in params]
    n_in, n_out = len(args), len(outs)
    out_specs = [pl.BlockSpec((tm, w), lambda j, i: (i, j)) for _, w in outs]
    out_shape = [jax.ShapeDtypeStruct((T, nj * w), dt) for dt, w in outs]
    out_specs += [pl.BlockSpec((1, LANES), lambda j, i: (0, 0)) for _ in range(n_acc)]
    out_shape += [jax.ShapeDtypeStruct((1, LANES), F32) for _ in range(n_acc)]

    def body(*refs):
        vals = [r[...].astype(F32) for r in refs[:n_in]]
        res = fn(*vals)
        orefs = refs[n_in:]
        for k in range(n_out):
            orefs[k][...] = res[k].astype(orefs[k].dtype)
        first = (pl.program_id(0) == 0) & (pl.program_id(1) == 0)
        for k in range(n_acc):
            @pl.when(first)
            def _(k=k):
                orefs[n_out + k][...] = jnp.zeros((1, LANES), F32)
            orefs[n_out + k][...] += res[n_out + k]

    return pl.pallas_call(
        body, name=name, grid=(nj, T // tm), in_specs=in_specs, out_specs=out_specs, out_shape=out_shape,
        compiler_params=_params(("arbitrary", "arbitrary")),
    )(*args)


def _rw_bwd(name, fn, rows, params, cots, T, tm, nj, row_grads, param_grads):
    in_specs = ([_spec(tm, w, off, pj) for _, w, off, pj in rows] + [_pspec(a.shape[0], w, off, pj) for a, w, off, pj in params]
                + [_spec(tm, w, off, pj) for _, w, off, pj in cots])
    args = [r[0] for r in rows] + [p[0] for p in params] + [c[0] for c in cots]
    nr, npar, nc = len(rows), len(params), len(cots)
    out_specs, out_shape, kinds = [], [], []
    for k, dt in enumerate(row_grads):
        if dt is not None:
            w = rows[k][1]
            out_specs.append(pl.BlockSpec((tm, w), lambda j, i: (i, j)))
            out_shape.append(jax.ShapeDtypeStruct((T, nj * w), dt))
            kinds.append(("row", k))
    for k, need in enumerate(param_grads):
        if need:
            a, w, off, pj = params[k]
            out_specs.append(_pspec(a.shape[0], w, off, pj))
            out_shape.append(jax.ShapeDtypeStruct(a.shape, F32))
            kinds.append(("par", k))

    def body(*refs):
        vals = [r[...].astype(F32) for r in refs[:nr + npar]]
        ct = tuple(r[...].astype(F32) for r in refs[nr + npar:nr + npar + nc])
        _, vjp = jax.vjp(lambda *a: tuple(fn(*a)), *vals)
        grads = vjp(ct)
        orefs = refs[nr + npar + nc:]
        j, i = pl.program_id(0), pl.program_id(1)
        for o, (kind, k) in zip(orefs, kinds):
            if kind == "row":
                o[...] = grads[k].astype(o.dtype)
            else:
                first = (i == 0) if params[k][3] else ((i == 0) & (j == 0))

                @pl.when(first)
                def _(o=o):
                    o[...] = jnp.zeros(o.shape, F32)
                o[...] += grads[nr + k]

    return pl.pallas_call(
        body, name=name, grid=(nj, T // tm), in_specs=in_specs, out_specs=out_specs, out_shape=out_shape,
        compiler_params=_params(("arbitrary", "arbitrary")),
    )(*args)


def _rms(x, g):
    return x * lax.rsqrt(jnp.mean(x * x, axis=-1, keepdims=True) + EPS) * g


def _rms_fn(x, g):
    return (_rms(x, g),)


def _lo_mask():
    return lax.broadcasted_iota(jnp.int32, (1, LANES), 1) < HEAD


def _gmean(x, lo):
    s0 = jnp.sum(jnp.where(lo, x, 0.0), axis=-1, keepdims=True)
    s1 = jnp.sum(jnp.where(lo, 0.0, x), axis=-1, keepdims=True)
    return jnp.where(lo, s0, s1) * (1.0 / HEAD)


def _fox_prep_fn(fq, fk, gq, gk):
    lo = _lo_mask()
    qn = fq * lax.rsqrt(_gmean(fq * fq, lo) + EPS) * gq * (HEAD ** -0.5)
    kn = fk * lax.rsqrt(_gmean(fk * fk, lo) + EPS) * gk
    return qn, kn


def _cast_fn(v):
    return (v,)


@jax.custom_vjp
def _swap_halves(x):
    bit = (lax.broadcasted_iota(jnp.int32, (1, LANES), 1) & (HEAD // 2)) == 0
    return jnp.where(bit, pltpu.roll(x, LANES - HEAD // 2, 1), pltpu.roll(x, HEAD // 2, 1))


_swap_halves.defvjp(lambda x: (_swap_halves(x), None), lambda _, g: (_swap_halves(g),))


def _ret_fn(rq, rk, rv, rg, cos, sin, s_in, g, lg):
    tb = rq.shape[0]
    nc = tb // CHUNK
    lo = _lo_mask()
    row = lax.broadcasted_iota(jnp.int32, (LANES, 1), 0) < HEAD
    same_head = row == lo
    q = (rq * cos + _swap_halves(rq) * sin) * (HEAD ** -0.5)
    k = rk * cos + _swap_halves(rk) * sin
    q3, k3, v3 = q.reshape(nc, CHUNK, LANES), k.reshape(nc, CHUNK, LANES), rv.reshape(nc, CHUNK, LANES)
    pos = lax.broadcasted_iota(jnp.int32, (CHUNK, 1), 0).astype(F32)
    q_decay = jnp.exp(lg * (pos + 1.0))
    k_decay = jnp.exp(lg * (CHUNK - 1.0 - pos))
    chunk_decay = jnp.exp(lg * float(CHUNK))
    dist = jnp.abs(lax.broadcasted_iota(jnp.int32, (CHUNK, CHUNK), 0) - lax.broadcasted_iota(jnp.int32, (CHUNK, CHUNK), 1)).astype(F32)
    v3b = v3.astype(BF)
    intra = []
    for hh in range(2):
        hm = lo if hh == 0 else ~lo
        lg_h = lg[:, hh * HEAD:hh * HEAD + 1]
        qm = jnp.where(hm, q3, 0.0).astype(BF)
        sc = jnp.einsum("nid,njd->nij", qm, k3.astype(BF), preferred_element_type=F32) * jnp.exp(lg_h * dist)[None]
        intra.append(jnp.einsum("nij,nje->nie", sc.astype(BF), v3b, preferred_element_type=F32))
    o = jnp.where(lo, intra[0], intra[1])
    kv = jnp.einsum("njd,nje->nde", (k3 * k_decay[None]).astype(BF), v3b, preferred_element_type=F32)
    kv = jnp.where(same_head[None], kv, 0.0)
    state, states = s_in, []
    for n in range(nc):
        states.append(state)
        state = state * chunk_decay + kv[n]
    s_prev = jnp.stack(states, axis=0)
    o = o + jnp.einsum("nid,nde->nie", (q3 * q_decay[None]).astype(BF), s_prev.astype(BF), preferred_element_type=F32)
    o = o.reshape(tb, LANES)
    mu = _gmean(o, lo)
    oc = o - mu
    y = oc * lax.rsqrt(_gmean(oc * oc, lo) + EPS) * g
    return jax.nn.silu(rg) * y, state


def _xattn_fn(qx, gq, gk, kk, vv):
    q = _rms(qx, gq)
    k = _rms(kk, gk)
    logits = lax.dot_general(q.astype(BF), k.astype(BF), NT, preferred_element_type=F32) * (XHEAD ** -0.5)
    p = jax.nn.softmax(logits, axis=-1)
    return (jnp.dot(p.astype(BF), vv.astype(BF), preferred_element_type=F32),)


def _swiglu_fwd_epi(g, u):
    return g, u, jax.nn.silu(g) * u


def _swiglu_bwd_epi(dact, g, u):
    _, vjp = jax.vjp(lambda a, b: jax.nn.silu(a) * b, g, u)
    return vjp(dact)


def _loss_fn(h, target):
    err = h - target
    part = jnp.sum(jnp.sum(err * err, axis=0, keepdims=True), axis=-1, keepdims=True) * (0.5 / D)
    return err * (1.0 / D), part


def _ret_fwd(P, cos, sin, g_ret, lg, T, tb):
    nb = T // tb

    def body(rq, rk, rv, rg, c, s, g, l, o_ref, s0_ref, state):
        @pl.when(pl.program_id(1) == 0)
        def _():
            state[...] = jnp.zeros(state.shape, F32)
        s0_ref[0, 0] = state[...]
        out, s_new = _ret_fn(rq[...], rk[...], rv[...], rg[...], c[...], s[...], state[...], g[...], l[...])
        o_ref[...] = out
        state[...] = s_new

    sec = lambda off: pl.BlockSpec((tb, LANES), lambda j, i: (i, off + j))
    tab = pl.BlockSpec((tb, LANES), lambda j, i: (i, 0))
    par = pl.BlockSpec((1, LANES), lambda j, i: (0, j))
    return pl.pallas_call(
        body, name="ret_fwd", grid=(4, nb),
        in_specs=[sec(0), sec(4), sec(8), sec(12), tab, tab, par, par],
        out_specs=[pl.BlockSpec((tb, LANES), lambda j, i: (i, j)), pl.BlockSpec((1, 1, LANES, LANES), lambda j, i: (j, i, 0, 0))],
        out_shape=[jax.ShapeDtypeStruct((T, 4 * LANES), F32), jax.ShapeDtypeStruct((4, nb, LANES, LANES), F32)],
        scratch_shapes=[pltpu.VMEM((LANES, LANES), F32)],
        compiler_params=_params(("arbitrary", "arbitrary")),
    )(P, P, P, P, cos, sin, g_ret, lg)


def _ret_bwd(P, cos, sin, g_ret, lg, s0, dmix, T, tb):
    nb = T // tb

    def body(rq, rk, rv, rg, c, s, g, l, s0_ref, do, drq, drk, drv, drg, dg, dstate):
        i = pl.program_id(1)

        @pl.when(i == 0)
        def _():
            dstate[...] = jnp.zeros(dstate.shape, F32)
            dg[...] = jnp.zeros(dg.shape, F32)

        cc, ss, ll = c[...], s[...], l[...]
        _, vjp = jax.vjp(lambda a, b, v, gate, st, gg: _ret_fn(a, b, v, gate, cc, ss, st, gg, ll),
                         rq[...], rk[...], rv[...], rg[...], s0_ref[0, 0], g[...])
        ga, gb, gv, ggate, gst, ggain = vjp((do[...], dstate[...]))
        drq[...] = ga.astype(drq.dtype)
        drk[...] = gb.astype(drk.dtype)
        drv[...] = gv.astype(drv.dtype)
        drg[...] = ggate.astype(drg.dtype)
        dstate[...] = gst
        dg[...] += ggain

    rev = lambda i: nb - 1 - i
    sec = lambda off: pl.BlockSpec((tb, LANES), lambda j, i: (rev(i), off + j))
    tab = pl.BlockSpec((tb, LANES), lambda j, i: (rev(i), 0))
    par = pl.BlockSpec((1, LANES), lambda j, i: (0, j))
    outb = pl.BlockSpec((tb, LANES), lambda j, i: (rev(i), j))
    return pl.pallas_call(
        body, name="ret_bwd", grid=(4, nb),
        in_specs=[sec(0), sec(4), sec(8), sec(12), tab, tab, par, par,
                  pl.BlockSpec((1, 1, LANES, LANES), lambda j, i: (j, rev(i), 0, 0)), outb],
        out_specs=[outb, outb, outb, outb, par],
        out_shape=[jax.ShapeDtypeStruct((T, 4 * LANES), BF)] * 4 + [jax.ShapeDtypeStruct((1, 4 * LANES), F32)],
        scratch_shapes=[pltpu.VMEM((LANES, LANES), F32)],
        compiler_params=_params(("arbitrary", "arbitrary")),
    )(P, P, P, P, cos, sin, g_ret, lg, s0, dmix)


_FB = 128


def _tri(lower):
    r = lax.broadcasted_iota(jnp.int32, (_FB, _FB), 0)
    c = lax.broadcasted_iota(jnp.int32, (_FB, _FB), 1)
    return ((r >= c) if lower else (r <= c)).astype(F32)


def _fgate_fwd(ffp, bpad, T):
    def body(ff_ref, b_ref, fc_ref, fr_ref):
        lane = lax.broadcasted_iota(jnp.int32, (1, LANES), 1)
        tri = _tri(True)
        carry = jnp.zeros((1, LANES), F32)
        for blk in range(T // _FB):
            z = ff_ref[blk * _FB:(blk + 1) * _FB, :] + b_ref[...]
            lf = jnp.where(lane < 8, jax.nn.log_sigmoid(z), 0.0)
            f = jnp.dot(tri, lf, precision=lax.Precision.HIGHEST, preferred_element_type=F32) + carry
            carry = f[_FB - 1:_FB, :]
            fc_ref[blk * _FB:(blk + 1) * _FB, :] = f
            fr_ref[:, blk * _FB:(blk + 1) * _FB] = f.T[:8, :]

    return pl.pallas_call(
        body, name="fgate_fwd",
        out_shape=[jax.ShapeDtypeStruct((T, LANES), F32), jax.ShapeDtypeStruct((8, T), F32)],
        compiler_params=pltpu.CompilerParams(vmem_limit_bytes=VMEM_LIMIT),
    )(ffp, bpad)


def _fgate_bwd(ffp, bpad, dfr, T):
    def body(ff_ref, b_ref, dfr_ref, dff_ref, db_ref):
        lane = lax.broadcasted_iota(jnp.int32, (1, LANES), 1)
        tri = _tri(False)
        carry = jnp.zeros((1, LANES), F32)
        db = jnp.zeros((1, LANES), F32)
        for blk in reversed(range(T // _FB)):
            d8 = dfr_ref[:, blk * _FB:(blk + 1) * _FB]
            dcol = jnp.concatenate([d8, jnp.zeros((_FB - 8, _FB), F32)], axis=0).T
            dlf = jnp.dot(tri, dcol, precision=lax.Precision.HIGHEST, preferred_element_type=F32) + carry
            carry = dlf[0:1, :]
            z = ff_ref[blk * _FB:(blk + 1) * _FB, :] + b_ref[...]
            dz = jnp.where(lane < 8, dlf * jax.nn.sigmoid(-z), 0.0)
            dff_ref[blk * _FB:(blk + 1) * _FB, :] = dz.astype(dff_ref.dtype)
            db = db + jnp.sum(dz, axis=0, keepdims=True)
        db_ref[...] = db

    return pl.pallas_call(
        body, name="fgate_bwd",
        out_shape=[jax.ShapeDtypeStruct((T, LANES), BF), jax.ShapeDtypeStruct((1, LANES), F32)],
        compiler_params=pltpu.CompilerParams(vmem_limit_bytes=VMEM_LIMIT),
    )(ffp, bpad, dfr)


def _head_bias_col(fc, head):
    lane = lax.broadcasted_iota(jnp.int32, (1, LANES), 1)
    return jnp.sum(jnp.where(lane == head, fc, 0.0), axis=-1, keepdims=True)


def _head_bias_row(fr, head):
    sub = lax.broadcasted_iota(jnp.int32, (8, 1), 0)
    return jnp.sum(jnp.where(sub == head, fr, 0.0), axis=0, keepdims=True)


def _fox_fwd(qn, kn, vb, fc, fr, T, tq):
    nq = T // tq

    def body(q_ref, k_ref, v_ref, fc_ref, fr_ref, o_ref, c_ref):
        j, i = pl.program_id(0), pl.program_id(1)
        lane = lax.broadcasted_iota(jnp.int32, (1, LANES), 1)
        lo = lane < HEAD
        causal = lax.broadcasted_iota(jnp.int32, (tq, tq), 0) >= lax.broadcasted_iota(jnp.int32, (tq, tq), 1)
        q = q_ref[...]
        fcb = fc_ref[...]
        outs, cs = [], []
        for hh in range(2):
            hm = lo if hh == 0 else ~lo
            head = 2 * j + hh
            qh = jnp.where(hm, q, jnp.zeros_like(q))
            fq = _head_bias_col(fcb, head)

            def block(kb, carry, diag, qh=qh, fq=fq, head=head):
                m, l, acc = carry
                k0 = pl.multiple_of(kb * tq, tq)
                k = k_ref[pl.ds(k0, tq), :]
                v = v_ref[pl.ds(k0, tq), :]
                fk = _head_bias_row(fr_ref[:, pl.ds(k0, tq)], head)
                s = (lax.dot_general(qh, k, NT, preferred_element_type=F32) + fq) - fk
                if diag:
                    s = jnp.where(causal, s, NEG)
                m2 = jnp.maximum(m, jnp.max(s, axis=-1, keepdims=True))
                p = jnp.exp(s - m2)
                a = jnp.exp(m - m2)
                return m2, a * l + jnp.sum(p, axis=-1, keepdims=True), a * acc + jnp.dot(p.astype(BF), v, preferred_element_type=F32)

            init = (jnp.full((tq, 1), NEG, F32), jnp.zeros((tq, 1), F32), jnp.zeros((tq, LANES), F32))
            carry = lax.fori_loop(0, i, lambda kb, c: block(kb, c, False), init)
            m, l, acc = block(i, carry, True)
            outs.append(acc / l)
            cs.append(fq - (m + jnp.log(l)))
        o_ref[...] = jnp.where(lo, outs[0], outs[1])
        c_ref[0] = jnp.where(lane == 0, cs[0], jnp.where(lane == 1, cs[1], 0.0))

    full = lambda: pl.BlockSpec((T, LANES), lambda j, i: (0, j))
    return pl.pallas_call(
        body, name="fox_fwd", grid=(4, nq),
        in_specs=[pl.BlockSpec((tq, LANES), lambda j, i: (i, j)), full(), full(),
                  pl.BlockSpec((tq, LANES), lambda j, i: (i, 0)), pl.BlockSpec((8, T), lambda j, i: (0, 0))],
        out_specs=[pl.BlockSpec((tq, LANES), lambda j, i: (i, j)), pl.BlockSpec((1, tq, LANES), lambda j, i: (j, i, 0))],
        out_shape=[jax.ShapeDtypeStruct((T, 4 * LANES), F32), jax.ShapeDtypeStruct((4, T, LANES), F32)],
        compiler_params=_params(("parallel", "arbitrary")),
    )(qn, kn, vb, fc, fr)


def _fox_bwd_dq(qn, kn, vb, fr, cq, dmix, T, tq):
    nq = T // tq

    def body(q_ref, k_ref, v_ref, fr_ref, c_ref, do_ref, dq_ref, dl_ref, p_scr, dp_scr):
        j, i = pl.program_id(0), pl.program_id(1)
        lane = lax.broadcasted_iota(jnp.int32, (1, LANES), 1)
        lo = lane < HEAD
        causal = lax.broadcasted_iota(jnp.int32, (tq, tq), 0) >= lax.broadcasted_iota(jnp.int32, (tq, tq), 1)
        q, do, cb = q_ref[...], do_ref[...], c_ref[0]
        res, deltas = [], []
        for hh in range(2):
            hm = lo if hh == 0 else ~lo
            head = 2 * j + hh
            qh = jnp.where(hm, q, jnp.zeros_like(q))
            doh = jnp.where(hm, do, 0.0).astype(BF)
            c = cb[:, hh:hh + 1]

            def probs(kb, delta, diag, qh=qh, doh=doh, c=c, head=head):
                k0 = pl.multiple_of(kb * tq, tq)
                k = k_ref[pl.ds(k0, tq), :]
                v = v_ref[pl.ds(k0, tq), :]
                fk = _head_bias_row(fr_ref[:, pl.ds(k0, tq)], head)
                p = jnp.exp((lax.dot_general(qh, k, NT, preferred_element_type=F32) + c) - fk)
                if diag:
                    p = jnp.where(causal, p, 0.0)
                dp = lax.dot_general(doh, v, NT, preferred_element_type=F32)
                p_scr[:, pl.ds(k0, tq)] = p
                dp_scr[:, pl.ds(k0, tq)] = dp
                return delta + jnp.sum(p * dp, axis=-1, keepdims=True)

            delta = lax.fori_loop(0, i, lambda kb, d: probs(kb, d, False), jnp.zeros((tq, 1), F32))
            delta = probs(i, delta, True)

            def grad(kb, acc, delta=delta):
                k0 = pl.multiple_of(kb * tq, tq)
                ds = p_scr[:, pl.ds(k0, tq)] * (dp_scr[:, pl.ds(k0, tq)] - delta)
                return acc + jnp.dot(ds.astype(BF), k_ref[pl.ds(k0, tq), :], preferred_element_type=F32)

            res.append(lax.fori_loop(0, i + 1, grad, jnp.zeros((tq, LANES), F32)))
            deltas.append(delta)
        dq_ref[...] = jnp.where(lo, res[0], res[1])
        dl_ref[0] = jnp.where(lane == 0, deltas[0], jnp.where(lane == 1, deltas[1], 0.0))

    full = lambda: pl.BlockSpec((T, LANES), lambda j, i: (0, j))
    return pl.pallas_call(
        body, name="fox_bwd_dq", grid=(4, nq),
        in_specs=[pl.BlockSpec((tq, LANES), lambda j, i: (i, j)), full(), full(), pl.BlockSpec((8, T), lambda j, i: (0, 0)),
                  pl.BlockSpec((1, tq, LANES), lambda j, i: (j, i, 0)), pl.BlockSpec((tq, LANES), lambda j, i: (i, 4 + j))],
        out_specs=[pl.BlockSpec((tq, LANES), lambda j, i: (i, j)), pl.BlockSpec((1, tq, LANES), lambda j, i: (j, i, 0))],
        out_shape=[jax.ShapeDtypeStruct((T, 4 * LANES), F32), jax.ShapeDtypeStruct((4, T, LANES), F32)],
        scratch_shapes=[pltpu.VMEM((tq, T), F32), pltpu.VMEM((tq, T), F32)],
        compiler_params=_params(("parallel", "arbitrary")),
    )(qn, kn, vb, fr, cq, dmix)


def _fox_bwd_dkv(qn, kn, vb, fr, cq, dl, dmix, T, tq):
    nq = T // tq

    def body(q_ref, k_ref, v_ref, fr_ref, c_ref, dl_ref, do_ref, dk_ref, dv_ref, dfr_ref):
        j, kb = pl.program_id(0), pl.program_id(1)
        lo = _lo_mask()
        sub = lax.broadcasted_iota(jnp.int32, (8, 1), 0)
        causal = lax.broadcasted_iota(jnp.int32, (tq, tq), 0) >= lax.broadcasted_iota(jnp.int32, (tq, tq), 1)
        k, v, frb = k_ref[...], v_ref[...], fr_ref[...]
        dks, dvs, dfs = [], [], []
        for hh in range(2):
            hm = lo if hh == 0 else ~lo
            head = 2 * j + hh
            km = jnp.where(hm, k, jnp.zeros_like(k))
            vm = jnp.where(hm, v, jnp.zeros_like(v))
            fk = _head_bias_row(frb, head)

            def block(qi, carry, diag, km=km, vm=vm, fk=fk, hm=hm, hh=hh):
                dk, dv, df = carry
                q0 = pl.multiple_of(qi * tq, tq)
                q = q_ref[pl.ds(q0, tq), :]
                c = c_ref[0, pl.ds(q0, tq), :][:, hh:hh + 1]
                delta = dl_ref[0, pl.ds(q0, tq), :][:, hh:hh + 1]
                dob = do_ref[pl.ds(q0, tq), :].astype(BF)
                p = jnp.exp((lax.dot_general(q, km, NT, preferred_element_type=F32) + c) - fk)
                if diag:
                    p = jnp.where(causal, p, 0.0)
                dv = dv + lax.dot_general(p.astype(BF), dob, TN, preferred_element_type=F32)
                dp = lax.dot_general(dob, vm, NT, preferred_element_type=F32)
                ds = p * (dp - delta)
                dk = dk + lax.dot_general(ds.astype(BF), q, TN, preferred_element_type=F32)
                return dk, dv, df - jnp.sum(ds, axis=0, keepdims=True)

            init = (jnp.zeros((tq, LANES), F32), jnp.zeros((tq, LANES), F32), jnp.zeros((1, tq), F32))
            carry = block(kb, init, True)
            dk, dv, df = lax.fori_loop(kb + 1, nq, lambda qi, cr: block(qi, cr, False), carry)
            dks.append(dk)
            dvs.append(dv)
            dfs.append(df)
        dk_ref[...] = jnp.where(lo, dks[0], dks[1])
        dv_ref[...] = jnp.where(lo, dvs[0], dvs[1]).astype(dv_ref.dtype)
        dfr_ref[0] = jnp.where(sub == 0, dfs[0], jnp.where(sub == 1, dfs[1], 0.0))

    full = lambda off: pl.BlockSpec((T, LANES), lambda j, kb: (0, off + j))
    blk = lambda: pl.BlockSpec((tq, LANES), lambda j, kb: (kb, j))
    return pl.pallas_call(
        body, name="fox_bwd_dkv", grid=(4, nq),
        in_specs=[full(0), blk(), blk(), pl.BlockSpec((8, tq), lambda j, kb: (0, kb)),
                  pl.BlockSpec((1, T, LANES), lambda j, kb: (j, 0, 0)), pl.BlockSpec((1, T, LANES), lambda j, kb: (j, 0, 0)), full(4)],
        out_specs=[blk(), blk(), pl.BlockSpec((1, 8, tq), lambda j, kb: (j, 0, kb))],
        out_shape=[jax.ShapeDtypeStruct((T, 4 * LANES), F32), jax.ShapeDtypeStruct((T, 4 * LANES), BF),
                   jax.ShapeDtypeStruct((4, 8, T), F32)],
        compiler_params=_params(("parallel", "arbitrary")),
    )(qn, kn, vb, fr, cq, dl, dmix)


MESH = pl.DeviceIdType.MESH


def _place():
    return lax.axis_index("x"), lax.axis_index("y"), lax.axis_index("c")


def _all_gather(shard):
    R, W = shard.shape

    def body(x_ref, out_ref, send_sems, recv_sems, local_sem):
        x, y, c = _place()
        me, sibling = (x, y, c), (x, y, 1 - c)
        chips = [(1 - x, y), (x, 1 - y), (1 - x, 1 - y)]

        def slot(px, py, pc):
            return out_ref.at[4 * px + 2 * py + pc]

        def copy(k, block, to, src=None):
            return pltpu.make_async_remote_copy(
                src_ref=slot(*block) if src is None else src, dst_ref=slot(*block),
                send_sem=send_sems.at[k], recv_sem=recv_sems.at[k], device_id=to, device_id_type=MESH)

        mine = pltpu.make_async_copy(x_ref, slot(*me), local_sem)
        mine.start()
        first = [copy(0, me, sibling, src=x_ref)]
        first += [copy(1 + n, me, (*chip, c), src=x_ref) for n, chip in enumerate(chips)]
        for cp in first:
            cp.start()
        passed = [copy(4 + n, (*chip, c), sibling) for n, chip in enumerate(chips)]
        for n, chip in enumerate(chips):
            copy(1 + n, (*chip, c), me).wait_recv()
            passed[n].start()
        copy(0, sibling, me).wait_recv()
        for n, chip in enumerate(chips):
            copy(4 + n, (*chip, 1 - c), me).wait_recv()
        for cp in first + passed:
            cp.wait_send()
        mine.wait()

    return pl.pallas_call(
        body, name="all_gather_weights",
        out_shape=jax.ShapeDtypeStruct((N_DEV, R, W), shard.dtype),
        in_specs=[pl.BlockSpec(memory_space=pl.ANY)], out_specs=pl.BlockSpec(memory_space=pl.ANY),
        scratch_shapes=[pltpu.SemaphoreType.DMA((7,)), pltpu.SemaphoreType.DMA((7,)), pltpu.SemaphoreType.DMA],
    )(shard)


def _all_to_all(big, small):
    def body(big_ref, small_ref, rbig_ref, rsmall_ref, send_sems, recv_sems, local_sems):
        x, y, c = _place()
        me = 4 * x + 2 * y + c
        l0 = pltpu.make_async_copy(big_ref.at[me], rbig_ref.at[me], local_sems.at[0])
        l1 = pltpu.make_async_copy(small_ref, rsmall_ref.at[me], local_sems.at[1])
        l0.start()
        l1.start()
        copies = []
        for r in range(1, N_DEV):
            px, py, pc = x ^ (r >> 2), y ^ ((r >> 1) & 1), c ^ (r & 1)
            peer = 4 * px + 2 * py + pc
            copies.append(pltpu.make_async_remote_copy(
                src_ref=big_ref.at[peer], dst_ref=rbig_ref.at[me], send_sem=send_sems.at[2 * r], recv_sem=recv_sems.at[2 * r],
                device_id=(px, py, pc), device_id_type=MESH))
            copies.append(pltpu.make_async_remote_copy(
                src_ref=small_ref, dst_ref=rsmall_ref.at[me], send_sem=send_sems.at[2 * r + 1], recv_sem=recv_sems.at[2 * r + 1],
                device_id=(px, py, pc), device_id_type=MESH))
        for cp in copies:
            cp.start()
        for cp in copies:
            cp.wait_recv()
        for cp in copies:
            cp.wait_send()
        l0.wait()
        l1.wait()

    return pl.pallas_call(
        body, name="all_to_all_grads",
        out_shape=[jax.ShapeDtypeStruct(big.shape, big.dtype), jax.ShapeDtypeStruct((N_DEV,) + small.shape, small.dtype)],
        in_specs=[pl.BlockSpec(memory_space=pl.ANY)] * 2, out_specs=[pl.BlockSpec(memory_space=pl.ANY)] * 2,
        scratch_shapes=[pltpu.SemaphoreType.DMA((2 * N_DEV,)), pltpu.SemaphoreType.DMA((2 * N_DEV,)), pltpu.SemaphoreType.DMA((2,))],
    )(big, small)


def _adamw(name, slots, w, m, v, tr):
    R, W = w.shape

    def body(s_ref, w_ref, m_ref, v_ref, g_ref, d_ref, nm_ref, nv_ref):
        g = s_ref[0].astype(F32)
        for s in range(1, N_DEV):
            g = g + s_ref[s].astype(F32)
        m2 = ADAM_B1 * m_ref[...] + (1.0 - ADAM_B1) * g
        v2 = ADAM_B2 * v_ref[...] + (1.0 - ADAM_B2) * jnp.square(g)
        m_hat = m2 / (1.0 - ADAM_B1 ** ADAM_STEP)
        v_hat = v2 / (1.0 - ADAM_B2 ** ADAM_STEP)
        g_ref[...] = g
        d_ref[...] = -ADAM_LR * (m_hat / (jnp.sqrt(v_hat) + ADAM_EPS) + ADAM_WD * w_ref[...])
        nm_ref[...] = m2
        nv_ref[...] = v2

    row = lambda: pl.BlockSpec((tr, W), lambda i: (i, 0))
    return pl.pallas_call(
        body, name=name, grid=(R // tr,),
        in_specs=[pl.BlockSpec((N_DEV, tr, W), lambda i: (0, i, 0)), row(), row(), row()],
        out_specs=[row(), row(), row(), row()],
        out_shape=[jax.ShapeDtypeStruct((R, W), F32)] * 4,
        compiler_params=_params(("parallel",)),
    )(slots, w, m, v)


def _tables(T):
    pos = jnp.arange(T, dtype=F32)
    inv_freq = 10000.0 ** (-jnp.arange(0, HEAD, 2, dtype=F32) / HEAD)
    ang = pos[:, None] * inv_freq[None, :]
    cos, sin = jnp.cos(ang), jnp.sin(ang)
    cos4 = jnp.tile(cos, (1, 4))
    sin4 = jnp.tile(jnp.concatenate([-sin, sin], axis=1), (1, 2))
    log_g = jnp.log(1.0 - 2.0 ** (-5.0 - jnp.arange(8, dtype=F32)))
    return cos4, sin4, jnp.repeat(log_g, HEAD)[None, :]


def _local_step(x, mem, target, sp, W):
    T = x.shape[0]
    tm = min(512, T)
    tq = min(256, T)
    tb = min(1024, T)
    cos4, sin4, lg = _tables(T)
    g_fq2 = jnp.tile(sp["g_fox_q"], (1, 2))
    g_fk2 = jnp.tile(sp["g_fox_k"], (1, 2))
    g_ret = sp["g_ret_out"].reshape(1, 8 * HEAD)
    bpad = jnp.pad(sp["b_forget"], ((0, 0), (0, LANES - 8)))
    w_secs = [W["w_inT"][k * 512:(k + 1) * 512] for k in range(7)]
    w_ffT = jnp.pad(W["w_inT"][3584:3592], ((0, LANES - 8), (0, 0)))
    w_mainT = W["w_inT"][:3584]

    hn1, = _rw_fwd("rms_mix", _rms_fn, [(x, D, 0, False)], [(sp["g_mix"], D, 0, False)], [(BF, D)], T, tm, 1)
    P, = _mm("proj_in", [[(hn1, w_mainT, "nt")]], [], _ident, T, 3584, tm, 512, [F32])
    ffp, = _mm("proj_ff", [[(hn1, w_ffT, "nt")]], [], _ident, T, LANES, tm, LANES, [F32])
    ret, s0 = _ret_fwd(P, cos4, sin4, g_ret, lg, T, tb)
    qn, kn = _rw_fwd("fox_prep", _fox_prep_fn, [(P, LANES, 16, True), (P, LANES, 20, True)],
                     [(g_fq2, LANES, 0, False), (g_fk2, LANES, 0, False)], [(BF, LANES), (BF, LANES)], T, tm, 4)
    vb, = _rw_fwd("fox_v", _cast_fn, [(P, LANES, 24, True)], [], [(BF, LANES)], T, tm, 4)
    fc, fr = _fgate_fwd(ffp, bpad, T)
    fox, cq = _fox_fwd(qn, kn, vb, fc, fr, T, tq)
    mix = jnp.concatenate([ret, fox], axis=1)
    h1, = _mm("proj_out", [[(mix, W["w_out"], "nn")]], [x], _add, T, D, tm, 512, [F32])

    hn2, = _rw_fwd("rms_xattn", _rms_fn, [(h1, D, 0, False)], [(sp["g_xattn"], D, 0, False)], [(BF, D)], T, tm, 1)
    qx, = _mm("proj_xq", [[(hn2, W["w_xq"], "nn")]], [], _ident, T, D, tm, 512, [F32])
    memn, = _rw_fwd("rms_mem", _rms_fn, [(mem, D, 0, False)], [(sp["g_mem"], D, 0, False)], [(BF, D)], N_MEM, N_MEM, 1)
    kv, = _mm("proj_xkv", [[(memn, W["w_xkvT"], "nt")]], [], _ident, N_MEM, 2 * D, N_MEM, 512, [F32])
    xa_rows = [(qx, XHEAD, 0, True)]
    xa_params = [(sp["g_xq"], XHEAD, 0, False), (sp["g_xk"], XHEAD, 0, False), (kv, XHEAD, 0, True), (kv, XHEAD, 4, True)]
    xo, = _rw_fwd("xattn_fwd", _xattn_fn, xa_rows, xa_params, [(BF, XHEAD)], T, tm, 4)
    h2, = _mm("proj_xo", [[(xo, W["w_xo"], "nn")]], [h1], _add, T, D, tm, 512, [F32])

    hn3, = _rw_fwd("rms_ffn", _rms_fn, [(h2, D, 0, False)], [(sp["g_ffn"], D, 0, False)], [(BF, D)], T, tm, 1)
    gate, up, act = _mm("ffn_in", [[(hn3, W["w_gateT"], "nt")], [(hn3, W["w_upT"], "nt")]], [], _swiglu_fwd_epi,
                        T, D_FF, tm, 256, [F32, F32, BF])
    h3, = _mm("ffn_out", [[(act, W["w_down"], "nn")]], [h2], _add, T, D, tm, 512, [F32])
    dy, loss_part = _rw_fwd("loss", _loss_fn, [(h3, D, 0, False), (target, D, 0, False)], [], [(F32, D)], T, tm, 1, n_acc=1)

    dgate, dup = _mm("ffn_out_bwd", [[(dy, W["w_down"], "nt")]], [gate, up], _swiglu_bwd_epi, T, D_FF, tm, 256, [BF, BF])
    dhn3, = _mm("ffn_in_bwd", [[(dgate, W["w_gateT"], "nn"), (dup, W["w_upT"], "nn")]], [], _ident, T, D, tm, 512, [F32])
    gW = {}
    gW["w_gateT"], = _mm("dw_gate", [[(dgate, hn3, "tn")]], [], _ident, D_FF, D, 256, 512, [F32])
    gW["w_upT"], = _mm("dw_up", [[(dup, hn3, "tn")]], [], _ident, D_FF, D, 256, 512, [F32])
    gW["w_down"], = _mm("dw_down", [[(act, dy, "tn")]], [], _ident, D_FF, D, 256, 512, [F32])
    gs = {}
    dh2n, gs["g_ffn"] = _rw_bwd("rms_ffn_bwd", _rms_fn, [(h2, D, 0, False)], [(sp["g_ffn"], D, 0, False)], [(dhn3, D, 0, False)],
                                T, tm, 1, [F32], [True])
    dh2, = _rw_fwd("add_dh2", lambda a, b: (a + b,), [(dy, D, 0, False), (dh2n, D, 0, False)], [], [(F32, D)], T, tm, 1)

    dxo, = _mm("proj_xo_bwd", [[(dh2, W["w_xo"], "nt")]], [], _ident, T, D, tm, 512, [BF])
    gW["w_xo"], = _mm("dw_xo", [[(xo, dh2, "tn")]], [], _ident, D, D, 256, 512, [F32])
    dqx, gs["g_xq"], gs["g_xk"], dkv_k, dkv_v = _rw_bwd(
        "xattn_bwd", _xattn_fn, xa_rows, xa_params, [(dxo, XHEAD, 0, True)], T, tm, 4, [BF], [True, True, True, True])
    dkv = jnp.concatenate([dkv_k[:, :D], dkv_v[:, D:]], axis=1)
    dhn2, = _mm("proj_xq_bwd", [[(dqx, W["w_xq"], "nt")]], [], _ident, T, D, tm, 512, [F32])
    gW["w_xq"], = _mm("dw_xq", [[(hn2, dqx, "tn")]], [], _ident, D, D, 256, 512, [F32])
    dmemn, = _mm("proj_xkv_bwd", [[(dkv, W["w_xkvT"], "nn")]], [], _ident, N_MEM, D, N_MEM, 512, [F32])
    gW["w_xkvT"], = _mm("dw_xkv", [[(dkv, memn, "tn")]], [], _ident, 2 * D, D, 512, 512, [F32])
    gs["g_mem"], = _rw_bwd("rms_mem_bwd", _rms_fn, [(mem, D, 0, False)], [(sp["g_mem"], D, 0, False)], [(dmemn, D, 0, False)],
                           N_MEM, N_MEM, 1, [None], [True])
    dh1n, gs["g_xattn"] = _rw_bwd("rms_xattn_bwd", _rms_fn, [(h1, D, 0, False)], [(sp["g_xattn"], D, 0, False)],
                                  [(dhn2, D, 0, False)], T, tm, 1, [F32], [True])
    dh1, = _rw_fwd("add_dh1", lambda a, b: (a + b,), [(dh2, D, 0, False), (dh1n, D, 0, False)], [], [(F32, D)], T, tm, 1)

    dmix, = _mm("proj_out_bwd", [[(dh1, W["w_out"], "nt")]], [], _ident, T, D, tm, 512, [F32])
    gW["w_out"], = _mm("dw_out", [[(mix, dh1, "tn")]], [], _ident, D, D, 256, 512, [F32])
    dqn, dl = _fox_bwd_dq(qn, kn, vb, fr, cq, dmix, T, tq)
    dkn, dfv, dfr4 = _fox_bwd_dkv(qn, kn, vb, fr, cq, dl, dmix, T, tq)
    dfq, dfk, gq2, gk2 = _rw_bwd("fox_prep_bwd", _fox_prep_fn, [(P, LANES, 16, True), (P, LANES, 20, True)],
                                 [(g_fq2, LANES, 0, False), (g_fk2, LANES, 0, False)],
                                 [(dqn, LANES, 0, True), (dkn, LANES, 0, True)], T, tm, 4, [BF, BF], [True, True])
    gs["g_fox_q"] = gq2[:, :HEAD] + gq2[:, HEAD:]
    gs["g_fox_k"] = gk2[:, :HEAD] + gk2[:, HEAD:]
    dff, dbp = _fgate_bwd(ffp, bpad, dfr4[:, :2, :].reshape(8, T), T)
    gs["b_forget"] = dbp[:, :8]
    drq, drk, drv, drg, dg_ret = _ret_bwd(P, cos4, sin4, g_ret, lg, s0, dmix, T, tb)
    gs["g_ret_out"] = dg_ret
    dsecs = [drq, drk, drv, drg, dfq, dfk, dfv]
    dhn1, = _mm("proj_in_bwd", [[(d, w, "nn") for d, w in zip(dsecs, w_secs)] + [(dff, w_ffT, "nn")]], [], _ident,
                T, D, tm, 512, [F32])
    g_secs = [_mm("dw_in_%d" % k, [[(d, hn1, "tn")]], [], _ident, 512, D, 256, 512, [F32])[0] for k, d in enumerate(dsecs)]
    g_ff, = _mm("dw_in_ff", [[(dff, hn1, "tn")]], [], _ident, LANES, D, LANES, 512, [F32])
    gW["w_inT"] = jnp.concatenate(g_secs + [g_ff[:8]], axis=0)
    dxn, gs["g_mix"] = _rw_bwd("rms_mix_bwd", _rms_fn, [(x, D, 0, False)], [(sp["g_mix"], D, 0, False)], [(dhn1, D, 0, False)],
                               T, tm, 1, [F32], [True])
    grad_x, = _rw_fwd("add_dx", lambda a, b: (a + b,), [(dh1, D, 0, False), (dxn, D, 0, False)], [], [(F32, D)], T, tm, 1)
    return loss_part, grad_x, gW, gs


_CANON = {"w_in": "w_inT", "w_xkv": "w_xkvT", "w_gate": "w_gateT", "w_up": "w_upT"}
_SMALL = (("g_mix", 0, 0, 1024), ("g_xattn", 1, 0, 1024), ("g_mem", 2, 0, 1024), ("g_ffn", 3, 0, 1024),
          ("g_ret_out", 4, 0, 512), ("g_xq", 4, 512, 256), ("g_xk", 4, 768, 256),
          ("g_fox_q", 5, 0, 64), ("g_fox_k", 5, 64, 64), ("b_forget", 5, 128, 8))
_LOSS_AT = (5, 256)


def _pack_shards(tree, dtype):
    parts = []
    for name, rows, padded, transposed in W_LAYOUT:
        a = tree[name][0]
        a = a.T if transposed else a
        parts.append(jnp.pad(a, ((0, padded - rows), (0, 0))).astype(dtype))
    return jnp.concatenate(parts, axis=0)


def _unpack_shards(packed, like):
    out = {}
    for name, rows, padded, transposed in W_LAYOUT:
        a = packed[W_OFF[name]:W_OFF[name] + rows]
        out[name] = (a.T if transposed else a)[None].reshape(like[name].shape)
    return out


def _pack_small(tree):
    rows = [jnp.zeros((1, D), F32) for _ in range(SMALL_ROWS)]
    buf = jnp.concatenate(rows, axis=0)
    for name, r, c, n in _SMALL:
        buf = lax.dynamic_update_slice(buf, tree[name].reshape(1, n).astype(F32), (r, c))
    return buf


def _unpack_small(buf, like):
    return {name: buf[r:r + 1, c:c + n].reshape(like[name].shape) for name, r, c, n in _SMALL}


def kernel(x, mem, g_mix, w_in, b_forget, g_ret_out, g_fox_q, g_fox_k, w_out, g_xattn, w_xq, w_xkv, g_mem, g_xq, g_xk, w_xo, g_ffn, w_gate, w_up, w_down, loss_target, m_g_mix, m_w_in, m_b_forget, m_g_ret_out, m_g_fox_q, m_g_fox_k, m_w_out, m_g_xattn, m_w_xq, m_w_xkv, m_g_mem, m_g_xq, m_g_xk, m_w_xo, m_g_ffn, m_w_gate, m_w_up, m_w_down, v_g_mix, v_w_in, v_b_forget, v_g_ret_out, v_g_fox_q, v_g_fox_k, v_w_out, v_g_xattn, v_w_xq, v_w_xkv, v_g_mem, v_g_xq, v_g_xk, v_w_xo, v_g_ffn, v_w_gate, v_w_up, v_w_down):
    names = ("g_mix", "w_in", "b_forget", "g_ret_out", "g_fox_q", "g_fox_k", "w_out", "g_xattn", "w_xq", "w_xkv", "g_mem",
             "g_xq", "g_xk", "w_xo", "g_ffn", "w_gate", "w_up", "w_down")
    w = dict(zip(names, (g_mix, w_in, b_forget, g_ret_out, g_fox_q, g_fox_k, w_out, g_xattn, w_xq, w_xkv, g_mem, g_xq, g_xk,
                         w_xo, g_ffn, w_gate, w_up, w_down)))
    m = dict(zip(names, (m_g_mix, m_w_in, m_b_forget, m_g_ret_out, m_g_fox_q, m_g_fox_k, m_w_out, m_g_xattn, m_w_xq, m_w_xkv,
                         m_g_mem, m_g_xq, m_g_xk, m_w_xo, m_g_ffn, m_w_gate, m_w_up, m_w_down)))
    v = dict(zip(names, (v_g_mix, v_w_in, v_b_forget, v_g_ret_out, v_g_fox_q, v_g_fox_k, v_w_out, v_g_xattn, v_w_xq, v_w_xkv,
                         v_g_mem, v_g_xq, v_g_xk, v_w_xo, v_g_ffn, v_w_gate, v_w_up, v_w_down)))
    small_names = [s[0] for s in _SMALL]

    gathered = _all_gather(_pack_shards(w, BF))
    W = {}
    for name, rows, padded, transposed in W_LAYOUT:
        full = gathered[:, W_OFF[name]:W_OFF[name] + rows].reshape(N_DEV * rows, D)
        W[_CANON.get(name, name)] = full

    sp = {n: w[n].reshape(1, -1) for n in small_names}
    loss_part, grad_x, gW, gs = _local_step(x[0], mem[0], loss_target[0], sp, W)

    chunks = []
    for name, rows, padded, transposed in W_LAYOUT:
        g = gW[_CANON.get(name, name)].reshape(N_DEV, rows, D)
        chunks.append(jnp.pad(g, ((0, 0), (0, padded - rows), (0, 0))).astype(BF))
    send = jnp.concatenate(chunks, axis=1)
    small = _pack_small(gs)
    small = lax.dynamic_update_slice(small, loss_part[:, :1], _LOSS_AT)
    recv, recv_small = _all_to_all(send, small)

    g_big, d_big, m_big, v_big = _adamw("adamw_shards", recv, _pack_shards(w, F32), _pack_shards(m, F32), _pack_shards(v, F32), 240)
    g_sm, d_sm, m_sm, v_sm = _adamw("adamw_small", recv_small, _pack_small(w), _pack_small(m), _pack_small(v), SMALL_ROWS)
    loss = g_sm[_LOSS_AT[0], _LOSS_AT[1]]

    outs = []
    for big, sm in ((g_big, g_sm), (d_big, d_sm), (m_big, m_sm), (v_big, v_sm)):
        tree = {**_unpack_shards(big, w), **_unpack_small(sm, w)}
        outs += [tree[n] for n in names]
    return (loss, grad_x[None], *outs)
```

```python
import functools
import math

import jax
import jax.numpy as jnp
import numpy as np
from jax import lax
from jax.experimental import pallas as pl
from jax.experimental.pallas import tpu as pltpu

F32 = jnp.float32
BF = jnp.bfloat16

D = 1024
HEAD = 64
CHUNK = 64
N_MEM = 256
XHEAD = 256
D_FF = 2816
EPS = 1e-6
NEG = -1e30
LANES = 128
N_DEV = 8
V7X_VMEM_BYTES = 64 * 1024 * 1024
VMEM_LIMIT = V7X_VMEM_BYTES - 8 * 1024 * 1024

ADAM_LR, ADAM_B1, ADAM_B2, ADAM_EPS, ADAM_WD, ADAM_STEP = 0.001, 0.9, 0.999, 1e-08, 0.01, 10

W_LAYOUT = (("w_in", 449, 464, True), ("w_out", 128, 128, False), ("w_xq", 128, 128, False), ("w_xkv", 256, 256, True),
            ("w_xo", 128, 128, False), ("w_gate", 352, 352, True), ("w_up", 352, 352, True), ("w_down", 352, 352, False))
W_ROWS = sum(w[2] for w in W_LAYOUT)
W_OFF = {}
_o = 0
for _n, _r, _p, _t in W_LAYOUT:
    W_OFF[_n] = _o
    _o += _p
SMALL_ROWS = 8

NT = (((1,), (1,)), ((), ()))
NN = (((1,), (0,)), ((), ()))
TN = (((0,), (0,)), ((), ()))
_DIMS = {"nn": NN, "nt": NT, "tn": TN}


def _params(sem):
    return pltpu.CompilerParams(dimension_semantics=sem, vmem_limit_bytes=VMEM_LIMIT)


def _mm(name, products, extras, epilogue, M, N, tm, tn, out_dtypes):
    flat = [t for p in products for t in p]
    counts = [len(p) for p in products]
    in_specs, args = [], []
    for a, b, form in flat:
        if form == "tn":
            in_specs.append(pl.BlockSpec((a.shape[0], tm), lambda i, j: (0, i)))
        else:
            in_specs.append(pl.BlockSpec((tm, a.shape[1]), lambda i, j: (i, 0)))
        if form == "nt":
            in_specs.append(pl.BlockSpec((tn, b.shape[1]), lambda i, j: (j, 0)))
        else:
            in_specs.append(pl.BlockSpec((b.shape[0], tn), lambda i, j: (0, j)))
        args += [a, b]
    for e in extras:
        in_specs.append(pl.BlockSpec((tm, tn), lambda i, j: (i, j)))
        args.append(e)
    n_in = len(args)

    def body(*refs):
        ins, outs = refs[:n_in], refs[n_in:]
        prods, p = [], 0
        for c in counts:
            acc = None
            for _ in range(c):
                a = ins[2 * p][...].astype(BF)
                b = ins[2 * p + 1][...].astype(BF)
                d = lax.dot_general(a, b, _DIMS[flat[p][2]], preferred_element_type=F32)
                acc = d if acc is None else acc + d
                p += 1
            prods.append(acc)
        ex = [r[...].astype(F32) for r in ins[2 * len(flat):]]
        res = epilogue(*prods, *ex)
        for o, r in zip(outs, res):
            o[...] = r.astype(o.dtype)

    return pl.pallas_call(
        body, name=name, grid=(M // tm, N // tn), in_specs=in_specs,
        out_specs=[pl.BlockSpec((tm, tn), lambda i, j: (i, j)) for _ in out_dtypes],
        out_shape=[jax.ShapeDtypeStruct((M, N), dt) for dt in out_dtypes],
        compiler_params=_params(("parallel", "arbitrary")),
    )(*args)


def _ident(x):
    return (x,)


def _add(x, r):
    return (x + r,)


def _spec(rows, w, off, per_j):
    if per_j:
        return pl.BlockSpec((rows, w), lambda j, i: (i, off + j))
    return pl.BlockSpec((rows, w), lambda j, i: (i, off))


def _pspec(rows, w, off, per_j):
    if per_j:
        return pl.BlockSpec((rows, w), lambda j, i: (0, off + j))
    return pl.BlockSpec((rows, w), lambda j, i: (0, off))


def _rw_fwd(name, fn, rows, params, outs, T, tm, nj, n_acc=0):
    in_specs = [_spec(tm, w, off, pj) for _, w, off, pj in rows] + [_pspec(a.shape[0], w, off, pj) for a, w, off, pj in params]
    args = [r[0] for r in rows] + [p[0] for p in params]
    n_in, n_out = len(args), len(outs)
    out_specs = [pl.BlockSpec((tm, w), lambda j, i: (i, j)) for _, w in outs]
    out_shape = [jax.ShapeDtypeStruct((T, nj * w), dt) for dt, w in outs]
    out_specs += [pl.BlockSpec((1, LANES), lambda j, i: (0, 0)) for _ in range(n_acc)]
    out_shape += [jax.ShapeDtypeStruct((1, LANES), F32) for _ in range(n_acc)]

    def body(*refs):
        vals = [r[...].astype(F32) for r in refs[:n_in]]
        res = fn(*vals)
        orefs = refs[n_in:]
        for k in range(n_out):
            orefs[k][...] = res[k].astype(orefs[k].dtype)
        first = (pl.program_id(0) == 0) & (pl.program_id(1) == 0)
        for k in range(n_acc):
            @pl.when(first)
            def _(k=k):
                orefs[n_out + k][...] = jnp.zeros((1, LANES), F32)
            orefs[n_out + k][...] += res[n_out + k]

    return pl.pallas_call(
        body, name=name, grid=(nj, T // tm), in_specs=in_specs, out_specs=out_specs, out_shape=out_shape,
        compiler_params=_params(("arbitrary", "arbitrary")),
    )(*args)


def _rw_bwd(name, fn, rows, params, cots, T, tm, nj, row_grads, param_grads):
    in_specs = ([_spec(tm, w, off, pj) for _, w, off, pj in rows] + [_pspec(a.shape[0], w, off, pj) for a, w, off, pj in params]
                + [_spec(tm, w, off, pj) for _, w, off, pj in cots])
    args = [r[0] for r in rows] + [p[0] for p in params] + [c[0] for c in cots]
    nr, npar, nc = len(rows), len(params), len(cots)
    out_specs, out_shape, kinds = [], [], []
    for k, dt in enumerate(row_grads):
        if dt is not None:
            w = rows[k][1]
            out_specs.append(pl.BlockSpec((tm, w), lambda j, i: (i, j)))
            out_shape.append(jax.ShapeDtypeStruct((T, nj * w), dt))
            kinds.append(("row", k))
    for k, need in enumerate(param_grads):
        if need:
            a, w, off, pj = params[k]
            out_specs.append(_pspec(a.shape[0], w, off, pj))
            out_shape.append(jax.ShapeDtypeStruct(a.shape, F32))
            kinds.append(("par", k))

    def body(*refs):
        vals = [r[...].astype(F32) for r in refs[:nr + npar]]
        ct = tuple(r[...].astype(F32) for r in refs[nr + npar:nr + npar + nc])
        _, vjp = jax.vjp(lambda *a: tuple(fn(*a)), *vals)
        grads = vjp(ct)
        orefs = refs[nr + npar + nc:]
        j, i = pl.program_id(0), pl.program_id(1)
        for o, (kind, k) in zip(orefs, kinds):
            if kind == "row":
                o[...] = grads[k].astype(o.dtype)
            else:
                first = (i == 0) if params[k][3] else ((i == 0) & (j == 0))

                @pl.when(first)
                def _(o=o):
                    o[...] = jnp.zeros(o.shape, F32)
                o[...] += grads[nr + k]

    return pl.pallas_call(
        body, name=name, grid=(nj, T // tm), in_specs=in_specs, out_specs=out_specs, out_shape=out_shape,
        compiler_params=_params(("arbitrary", "arbitrary")),
    )(*args)


def _rms(x, g):
    return x * lax.rsqrt(jnp.mean(x * x, axis=-1, keepdims=True) + EPS) * g


def _rms_fn(x, g):
    return (_rms(x, g),)


def _lo_mask():
    return lax.broadcasted_iota(jnp.int32, (1, LANES), 1) < HEAD


def _gmean(x, lo):
    s0 = jnp.sum(jnp.where(lo, x, 0.0), axis=-1, keepdims=True)
    s1 = jnp.sum(jnp.where(lo, 0.0, x), axis=-1, keepdims=True)
    return jnp.where(lo, s0, s1) * (1.0 / HEAD)


def _fox_prep_fn(fq, fk, gq, gk):
    lo = _lo_mask()
    qn = fq * lax.rsqrt(_gmean(fq * fq, lo) + EPS) * gq * (HEAD ** -0.5)
    kn = fk * lax.rsqrt(_gmean(fk * fk, lo) + EPS) * gk
    return qn, kn


def _cast_fn(v):
    return (v,)


@jax.custom_vjp
def _swap_halves(x):
    bit = (lax.broadcasted_iota(jnp.int32, (1, LANES), 1) & (HEAD // 2)) == 0
    return jnp.where(bit, pltpu.roll(x, LANES - HEAD // 2, 1), pltpu.roll(x, HEAD // 2, 1))


_swap_halves.defvjp(lambda x: (_swap_halves(x), None), lambda _, g: (_swap_halves(g),))


def _ret_fn(rq, rk, rv, rg, cos, sin, s_in, g, lg):
    tb = rq.shape[0]
    nc = tb // CHUNK
    lo = _lo_mask()
    row = lax.broadcasted_iota(jnp.int32, (LANES, 1), 0) < HEAD
    same_head = row == lo
    q = (rq * cos + _swap_halves(rq) * sin) * (HEAD ** -0.5)
    k = rk * cos + _swap_halves(rk) * sin
    q3, k3, v3 = q.reshape(nc, CHUNK, LANES), k.reshape(nc, CHUNK, LANES), rv.reshape(nc, CHUNK, LANES)
    pos = lax.broadcasted_iota(jnp.int32, (CHUNK, 1), 0).astype(F32)
    q_decay = jnp.exp(lg * (pos + 1.0))
    k_decay = jnp.exp(lg * (CHUNK - 1.0 - pos))
    chunk_decay = jnp.exp(lg * float(CHUNK))
    dist = jnp.abs(lax.broadcasted_iota(jnp.int32, (CHUNK, CHUNK), 0) - lax.broadcasted_iota(jnp.int32, (CHUNK, CHUNK), 1)).astype(F32)
    v3b = v3.astype(BF)
    intra = []
    for hh in range(2):
        hm = lo if hh == 0 else ~lo
        lg_h = lg[:, hh * HEAD:hh * HEAD + 1]
        qm = jnp.where(hm, q3, 0.0).astype(BF)
        sc = jnp.einsum("nid,njd->nij", qm, k3.astype(BF), preferred_element_type=F32) * jnp.exp(lg_h * dist)[None]
        intra.append(jnp.einsum("nij,nje->nie", sc.astype(BF), v3b, preferred_element_type=F32))
    o = jnp.where(lo, intra[0], intra[1])
    kv = jnp.einsum("njd,nje->nde", (k3 * k_decay[None]).astype(BF), v3b, preferred_element_type=F32)
    kv = jnp.where(same_head[None], kv, 0.0)
    state, states = s_in, []
    for n in range(nc):
        states.append(state)
        state = state * chunk_decay + kv[n]
    s_prev = jnp.stack(states, axis=0)
    o = o + jnp.einsum("nid,nde->nie", (q3 * q_decay[None]).astype(BF), s_prev.astype(BF), preferred_element_type=F32)
    o = o.reshape(tb, LANES)
    mu = _gmean(o, lo)
    oc = o - mu
    y = oc * lax.rsqrt(_gmean(oc * oc, lo) + EPS) * g
    return jax.nn.silu(rg) * y, state


def _xattn_fn(qx, gq, gk, kk, vv):
    q = _rms(qx, gq)
    k = _rms(kk, gk)
    logits = lax.dot_general(q.astype(BF), k.astype(BF), NT, preferred_element_type=F32) * (XHEAD ** -0.5)
    p = jax.nn.softmax(logits, axis=-1)
    return (jnp.dot(p.astype(BF), vv.astype(BF), preferred_element_type=F32),)


def _swiglu_fwd_epi(g, u):
    return g, u, jax.nn.silu(g) * u


def _swiglu_bwd_epi(dact, g, u):
    _, vjp = jax.vjp(lambda a, b: jax.nn.silu(a) * b, g, u)
    return vjp(dact)


def _loss_fn(h, target):
    err = h - target
    part = jnp.sum(jnp.sum(err * err, axis=0, keepdims=True), axis=-1, keepdims=True) * (0.5 / D)
    return err * (1.0 / D), part


def _ret_fwd(P, cos, sin, g_ret, lg, T, tb):
    nb = T // tb

    def body(rq, rk, rv, rg, c, s, g, l, o_ref, s0_ref, state):
        @pl.when(pl.program_id(1) == 0)
        def _():
            state[...] = jnp.zeros(state.shape, F32)
        s0_ref[0, 0] = state[...]
        out, s_new = _ret_fn(rq[...], rk[...], rv[...], rg[...], c[...], s[...], state[...], g[...], l[...])
        o_ref[...] = out
        state[...] = s_new

    sec = lambda off: pl.BlockSpec((tb, LANES), lambda j, i: (i, off + j))
    tab = pl.BlockSpec((tb, LANES), lambda j, i: (i, 0))
    par = pl.BlockSpec((1, LANES), lambda j, i: (0, j))
    return pl.pallas_call(
        body, name="ret_fwd", grid=(4, nb),
        in_specs=[sec(0), sec(4), sec(8), sec(12), tab, tab, par, par],
        out_specs=[pl.BlockSpec((tb, LANES), lambda j, i: (i, j)), pl.BlockSpec((1, 1, LANES, LANES), lambda j, i: (j, i, 0, 0))],
        out_shape=[jax.ShapeDtypeStruct((T, 4 * LANES), F32), jax.ShapeDtypeStruct((4, nb, LANES, LANES), F32)],
        scratch_shapes=[pltpu.VMEM((LANES, LANES), F32)],
        compiler_params=_params(("arbitrary", "arbitrary")),
    )(P, P, P, P, cos, sin, g_ret, lg)


def _ret_bwd(P, cos, sin, g_ret, lg, s0, dmix, T, tb):
    nb = T // tb

    def body(rq, rk, rv, rg, c, s, g, l, s0_ref, do, drq, drk, drv, drg, dg, dstate):
        i = pl.program_id(1)

        @pl.when(i == 0)
        def _():
            dstate[...] = jnp.zeros(dstate.shape, F32)
            dg[...] = jnp.zeros(dg.shape, F32)

        cc, ss, ll = c[...], s[...], l[...]
        _, vjp = jax.vjp(lambda a, b, v, gate, st, gg: _ret_fn(a, b, v, gate, cc, ss, st, gg, ll),
                         rq[...], rk[...], rv[...], rg[...], s0_ref[0, 0], g[...])
        ga, gb, gv, ggate, gst, ggain = vjp((do[...], dstate[...]))
        drq[...] = ga.astype(drq.dtype)
        drk[...] = gb.astype(drk.dtype)
        drv[...] = gv.astype(drv.dtype)
        drg[...] = ggate.astype(drg.dtype)
        dstate[...] = gst
        dg[...] += ggain

    rev = lambda i: nb - 1 - i
    sec = lambda off: pl.BlockSpec((tb, LANES), lambda j, i: (rev(i), off + j))
    tab = pl.BlockSpec((tb, LANES), lambda j, i: (rev(i), 0))
    par = pl.BlockSpec((1, LANES), lambda j, i: (0, j))
    outb = pl.BlockSpec((tb, LANES), lambda j, i: (rev(i), j))
    return pl.pallas_call(
        body, name="ret_bwd", grid=(4, nb),
        in_specs=[sec(0), sec(4), sec(8), sec(12), tab, tab, par, par,
                  pl.BlockSpec((1, 1, LANES, LANES), lambda j, i: (j, rev(i), 0, 0)), outb],
        out_specs=[outb, outb, outb, outb, par],
        out_shape=[jax.ShapeDtypeStruct((T, 4 * LANES), BF)] * 4 + [jax.ShapeDtypeStruct((1, 4 * LANES), F32)],
        scratch_shapes=[pltpu.VMEM((LANES, LANES), F32)],
        compiler_params=_params(("arbitrary", "arbitrary")),
    )(P, P, P, P, cos, sin, g_ret, lg, s0, dmix)


_FB = 128


def _tri(lower):
    r = lax.broadcasted_iota(jnp.int32, (_FB, _FB), 0)
    c = lax.broadcasted_iota(jnp.int32, (_FB, _FB), 1)
    return ((r >= c) if lower else (r <= c)).astype(F32)


def _fgate_fwd(ffp, bpad, T):
    def body(ff_ref, b_ref, fc_ref, fr_ref):
        lane = lax.broadcasted_iota(jnp.int32, (1, LANES), 1)
        tri = _tri(True)
        carry = jnp.zeros((1, LANES), F32)
        for blk in range(T // _FB):
            z = ff_ref[blk * _FB:(blk + 1) * _FB, :] + b_ref[...]
            lf = jnp.where(lane < 8, jax.nn.log_sigmoid(z), 0.0)
            f = jnp.dot(tri, lf, precision=lax.Precision.HIGHEST, preferred_element_type=F32) + carry
            carry = f[_FB - 1:_FB, :]
            fc_ref[blk * _FB:(blk + 1) * _FB, :] = f
            fr_ref[:, blk * _FB:(blk + 1) * _FB] = f.T[:8, :]

    return pl.pallas_call(
        body, name="fgate_fwd",
        out_shape=[jax.ShapeDtypeStruct((T, LANES), F32), jax.ShapeDtypeStruct((8, T), F32)],
        compiler_params=pltpu.CompilerParams(vmem_limit_bytes=VMEM_LIMIT),
    )(ffp, bpad)


def _fgate_bwd(ffp, bpad, dfr, T):
    def body(ff_ref, b_ref, dfr_ref, dff_ref, db_ref):
        lane = lax.broadcasted_iota(jnp.int32, (1, LANES), 1)
        tri = _tri(False)
        carry = jnp.zeros((1, LANES), F32)
        db = jnp.zeros((1, LANES), F32)
        for blk in reversed(range(T // _FB)):
            d8 = dfr_ref[:, blk * _FB:(blk + 1) * _FB]
            dcol = jnp.concatenate([d8, jnp.zeros((_FB - 8, _FB), F32)], axis=0).T
            dlf = jnp.dot(tri, dcol, precision=lax.Precision.HIGHEST, preferred_element_type=F32) + carry
            carry = dlf[0:1, :]
            z = ff_ref[blk * _FB:(blk + 1) * _FB, :] + b_ref[...]
            dz = jnp.where(lane < 8, dlf * jax.nn.sigmoid(-z), 0.0)
            dff_ref[blk * _FB:(blk + 1) * _FB, :] = dz.astype(dff_ref.dtype)
            db = db + jnp.sum(dz, axis=0, keepdims=True)
        db_ref[...] = db

    return pl.pallas_call(
        body, name="fgate_bwd",
        out_shape=[jax.ShapeDtypeStruct((T, LANES), BF), jax.ShapeDtypeStruct((1, LANES), F32)],
        compiler_params=pltpu.CompilerParams(vmem_limit_bytes=VMEM_LIMIT),
    )(ffp, bpad, dfr)


def _head_bias_col(fc, head):
    lane = lax.broadcasted_iota(jnp.int32, (1, LANES), 1)
    return jnp.sum(jnp.where(lane == head, fc, 0.0), axis=-1, keepdims=True)


def _head_bias_row(fr, head):
    sub = lax.broadcasted_iota(jnp.int32, (8, 1), 0)
    return jnp.sum(jnp.where(sub == head, fr, 0.0), axis=0, keepdims=True)


def _fox_fwd(qn, kn, vb, fc, fr, T, tq):
    nq = T // tq

    def body(q_ref, k_ref, v_ref, fc_ref, fr_ref, o_ref, c_ref):
        j, i = pl.program_id(0), pl.program_id(1)
        lane = lax.broadcasted_iota(jnp.int32, (1, LANES), 1)
        lo = lane < HEAD
        causal = lax.broadcasted_iota(jnp.int32, (tq, tq), 0) >= lax.broadcasted_iota(jnp.int32, (tq, tq), 1)
        q = q_ref[...]
        fcb = fc_ref[...]
        outs, cs = [], []
        for hh in range(2):
            hm = lo if hh == 0 else ~lo
            head = 2 * j + hh
            qh = jnp.where(hm, q, jnp.zeros_like(q))
            fq = _head_bias_col(fcb, head)

            def block(kb, carry, diag, qh=qh, fq=fq, head=head):
                m, l, acc = carry
                k0 = pl.multiple_of(kb * tq, tq)
                k = k_ref[pl.ds(k0, tq), :]
                v = v_ref[pl.ds(k0, tq), :]
                fk = _head_bias_row(fr_ref[:, pl.ds(k0, tq)], head)
                s = (lax.dot_general(qh, k, NT, preferred_element_type=F32) + fq) - fk
                if diag:
                    s = jnp.where(causal, s, NEG)
                m2 = jnp.maximum(m, jnp.max(s, axis=-1, keepdims=True))
                p = jnp.exp(s - m2)
                a = jnp.exp(m - m2)
                return m2, a * l + jnp.sum(p, axis=-1, keepdims=True), a * acc + jnp.dot(p.astype(BF), v, preferred_element_type=F32)

            init = (jnp.full((tq, 1), NEG, F32), jnp.zeros((tq, 1), F32), jnp.zeros((tq, LANES), F32))
            carry = lax.fori_loop(0, i, lambda kb, c: block(kb, c, False), init)
            m, l, acc = block(i, carry, True)
            outs.append(acc / l)
            cs.append(fq - (m + jnp.log(l)))
        o_ref[...] = jnp.where(lo, outs[0], outs[1])
        c_ref[0] = jnp.where(lane == 0, cs[0], jnp.where(lane == 1, cs[1], 0.0))

    full = lambda: pl.BlockSpec((T, LANES), lambda j, i: (0, j))
    return pl.pallas_call(
        body, name="fox_fwd", grid=(4, nq),
        in_specs=[pl.BlockSpec((tq, LANES), lambda j, i: (i, j)), full(), full(),
                  pl.BlockSpec((tq, LANES), lambda j, i: (i, 0)), pl.BlockSpec((8, T), lambda j, i: (0, 0))],
        out_specs=[pl.BlockSpec((tq, LANES), lambda j, i: (i, j)), pl.BlockSpec((1, tq, LANES), lambda j, i: (j, i, 0))],
        out_shape=[jax.ShapeDtypeStruct((T, 4 * LANES), F32), jax.ShapeDtypeStruct((4, T, LANES), F32)],
        compiler_params=_params(("parallel", "arbitrary")),
    )(qn, kn, vb, fc, fr)


def _fox_bwd_dq(qn, kn, vb, fr, cq, dmix, T, tq):
    nq = T // tq

    def body(q_ref, k_ref, v_ref, fr_ref, c_ref, do_ref, dq_ref, dl_ref, p_scr, dp_scr):
        j, i = pl.program_id(0), pl.program_id(1)
        lane = lax.broadcasted_iota(jnp.int32, (1, LANES), 1)
        lo = lane < HEAD
        causal = lax.broadcasted_iota(jnp.int32, (tq, tq), 0) >= lax.broadcasted_iota(jnp.int32, (tq, tq), 1)
        q, do, cb = q_ref[...], do_ref[...], c_ref[0]
        res, deltas = [], []
        for hh in range(2):
            hm = lo if hh == 0 else ~lo
            head = 2 * j + hh
            qh = jnp.where(hm, q, jnp.zeros_like(q))
            doh = jnp.where(hm, do, 0.0).astype(BF)
            c = cb[:, hh:hh + 1]

            def probs(kb, delta, diag, qh=qh, doh=doh, c=c, head=head):
                k0 = pl.multiple_of(kb * tq, tq)
                k = k_ref[pl.ds(k0, tq), :]
                v = v_ref[pl.ds(k0, tq), :]
                fk = _head_bias_row(fr_ref[:, pl.ds(k0, tq)], head)
                p = jnp.exp((lax.dot_general(qh, k, NT, preferred_element_type=F32) + c) - fk)
                if diag:
                    p = jnp.where(causal, p, 0.0)
                dp = lax.dot_general(doh, v, NT, preferred_element_type=F32)
                p_scr[:, pl.ds(k0, tq)] = p
                dp_scr[:, pl.ds(k0, tq)] = dp
                return delta + jnp.sum(p * dp, axis=-1, keepdims=True)

            delta = lax.fori_loop(0, i, lambda kb, d: probs(kb, d, False), jnp.zeros((tq, 1), F32))
            delta = probs(i, delta, True)

            def grad(kb, acc, delta=delta):
                k0 = pl.multiple_of(kb * tq, tq)
                ds = p_scr[:, pl.ds(k0, tq)] * (dp_scr[:, pl.ds(k0, tq)] - delta)
                return acc + jnp.dot(ds.astype(BF), k_ref[pl.ds(k0, tq), :], preferred_element_type=F32)

            res.append(lax.fori_loop(0, i + 1, grad, jnp.zeros((tq, LANES), F32)))
            deltas.append(delta)
        dq_ref[...] = jnp.where(lo, res[0], res[1])
        dl_ref[0] = jnp.where(lane == 0, deltas[0], jnp.where(lane == 1, deltas[1], 0.0))

    full = lambda: pl.BlockSpec((T, LANES), lambda j, i: (0, j))
    return pl.pallas_call(
        body, name="fox_bwd_dq", grid=(4, nq),
        in_specs=[pl.BlockSpec((tq, LANES), lambda j, i: (i, j)), full(), full(), pl.BlockSpec((8, T), lambda j, i: (0, 0)),
                  pl.BlockSpec((1, tq, LANES), lambda j, i: (j, i, 0)), pl.BlockSpec((tq, LANES), lambda j, i: (i, 4 + j))],
        out_specs=[pl.BlockSpec((tq, LANES), lambda j, i: (i, j)), pl.BlockSpec((1, tq, LANES), lambda j, i: (j, i, 0))],
        out_shape=[jax.ShapeDtypeStruct((T, 4 * LANES), F32), jax.ShapeDtypeStruct((4, T, LANES), F32)],
        scratch_shapes=[pltpu.VMEM((tq, T), F32), pltpu.VMEM((tq, T), F32)],
        compiler_params=_params(("parallel", "arbitrary")),
    )(qn, kn, vb, fr, cq, dmix)


def _fox_bwd_dkv(qn, kn, vb, fr, cq, dl, dmix, T, tq):
    nq = T // tq

    def body(q_ref, k_ref, v_ref, fr_ref, c_ref, dl_ref, do_ref, dk_ref, dv_ref, dfr_ref):
        j, kb = pl.program_id(0), pl.program_id(1)
        lo = _lo_mask()
        sub = lax.broadcasted_iota(jnp.int32, (8, 1), 0)
        causal = lax.broadcasted_iota(jnp.int32, (tq, tq), 0) >= lax.broadcasted_iota(jnp.int32, (tq, tq), 1)
        k, v, frb = k_ref[...], v_ref[...], fr_ref[...]
        dks, dvs, dfs = [], [], []
        for hh in range(2):
            hm = lo if hh == 0 else ~lo
            head = 2 * j + hh
            km = jnp.where(hm, k, jnp.zeros_like(k))
            vm = jnp.where(hm, v, jnp.zeros_like(v))
            fk = _head_bias_row(frb, head)

            def block(qi, carry, diag, km=km, vm=vm, fk=fk, hm=hm, hh=hh):
                dk, dv, df = carry
                q0 = pl.multiple_of(qi * tq, tq)
                q = q_ref[pl.ds(q0, tq), :]
                c = c_ref[0, pl.ds(q0, tq), :][:, hh:hh + 1]
                delta = dl_ref[0, pl.ds(q0, tq), :][:, hh:hh + 1]
                dob = do_ref[pl.ds(q0, tq), :].astype(BF)
                p = jnp.exp((lax.dot_general(q, km, NT, preferred_element_type=F32) + c) - fk)
                if diag:
                    p = jnp.where(causal, p, 0.0)
                dv = dv + lax.dot_general(p.astype(BF), dob, TN, preferred_element_type=F32)
                dp = lax.dot_general(dob, vm, NT, preferred_element_type=F32)
                ds = p * (dp - delta)
                dk = dk + lax.dot_general(ds.astype(BF), q, TN, preferred_element_type=F32)
                return dk, dv, df - jnp.sum(ds, axis=0, keepdims=True)

            init = (jnp.zeros((tq, LANES), F32), jnp.zeros((tq, LANES), F32), jnp.zeros((1, tq), F32))
            carry = block(kb, init, True)
            dk, dv, df = lax.fori_loop(kb + 1, nq, lambda qi, cr: block(qi, cr, False), carry)
            dks.append(dk)
            dvs.append(dv)
            dfs.append(df)
        dk_ref[...] = jnp.where(lo, dks[0], dks[1])
        dv_ref[...] = jnp.where(lo, dvs[0], dvs[1]).astype(dv_ref.dtype)
        dfr_ref[0] = jnp.where(sub == 0, dfs[0], jnp.where(sub == 1, dfs[1], 0.0))

    full = lambda off: pl.BlockSpec((T, LANES), lambda j, kb: (0, off + j))
    blk = lambda: pl.BlockSpec((tq, LANES), lambda j, kb: (kb, j))
    return pl.pallas_call(
        body, name="fox_bwd_dkv", grid=(4, nq),
        in_specs=[full(0), blk(), blk(), pl.BlockSpec((8, tq), lambda j, kb: (0, kb)),
                  pl.BlockSpec((1, T, LANES), lambda j, kb: (j, 0, 0)), pl.BlockSpec((1, T, LANES), lambda j, kb: (j, 0, 0)), full(4)],
        out_specs=[blk(), blk(), pl.BlockSpec((1, 8, tq), lambda j, kb: (j, 0, kb))],
        out_shape=[jax.ShapeDtypeStruct((T, 4 * LANES), F32), jax.ShapeDtypeStruct((T, 4 * LANES), BF),
                   jax.ShapeDtypeStruct((4, 8, T), F32)],
        compiler_params=_params(("parallel", "arbitrary")),
    )(qn, kn, vb, fr, cq, dl, dmix)


_BIAS_LANE = HEAD


def _split3(f):
    hi = f.astype(BF).astype(F32)
    mid = (f - hi).astype(BF).astype(F32)
    lo = ((f - hi) - mid).astype(BF).astype(F32)
    return hi, mid, lo


def _fox_operands(P, fc, g_fq2, g_fk2, T, tm):
    def body(fq_ref, fk_ref, fv_ref, fc_ref, gq_ref, gk_ref, qa_ref, qat_ref, ka_ref, kat_ref, va_ref, vat_ref):
        j = pl.program_id(0)
        lane = lax.broadcasted_iota(jnp.int32, (1, LANES), 1)
        qn, kn = _fox_prep_fn(fq_ref[...], fk_ref[...], gq_ref[...], gk_ref[...])
        v = fv_ref[...]
        fcb = fc_ref[...]
        b = _BIAS_LANE
        for hh in range(2):
            hi, mid, lo = _split3(_head_bias_col(fcb, 2 * j + hh))
            take = (lambda a: a) if hh == 0 else (lambda a: pltpu.roll(a, HEAD, 1))
            qa = jnp.where(lane < HEAD, take(qn), jnp.where(lane == b, hi, jnp.where(lane == b + 1, mid, jnp.where(
                lane == b + 2, lo, jnp.where(lane < b + 6, 1.0, 0.0)))))
            ka = jnp.where(lane < HEAD, take(kn), jnp.where(lane < b + 3, 1.0, jnp.where(lane == b + 3, -hi, jnp.where(
                lane == b + 4, -mid, jnp.where(lane == b + 5, -lo, 0.0)))))
            va = jnp.where(lane < HEAD, take(v), 0.0)
            for val, ref, tref in ((qa, qa_ref, qat_ref), (ka, ka_ref, kat_ref), (va, va_ref, vat_ref)):
                ref[hh] = val.astype(BF)
                tref[hh] = val.T.astype(BF)

    sec = lambda off: pl.BlockSpec((tm, LANES), lambda j, i: (i, off + j))
    par = pl.BlockSpec((1, LANES), lambda j, i: (0, 0))
    nat = pl.BlockSpec((2, tm, LANES), lambda j, i: (j, i, 0))
    trn = pl.BlockSpec((2, LANES, tm), lambda j, i: (j, 0, i))
    return pl.pallas_call(
        body, name="fox_operands", grid=(4, T // tm),
        in_specs=[sec(16), sec(20), sec(24), pl.BlockSpec((tm, LANES), lambda j, i: (i, 0)), par, par],
        out_specs=[nat, trn, nat, trn, nat, trn],
        out_shape=[jax.ShapeDtypeStruct((8, T, LANES), BF), jax.ShapeDtypeStruct((8, LANES, T), BF)] * 3,
        compiler_params=_params(("parallel", "arbitrary")),
    )(P, P, P, fc, g_fq2, g_fk2)


def _fox_forward(qat, ka, vat, T, tq, tk):
    nq, per = T // tq, tq // tk

    def body(qat_ref, ka_ref, vat_ref, o_ref, lse_ref):
        i = pl.program_id(1)
        sub = lax.broadcasted_iota(jnp.int32, (8, 1), 0)
        krow = lax.broadcasted_iota(jnp.int32, (tk, tq), 0)
        qcol = lax.broadcasted_iota(jnp.int32, (tk, tq), 1)

        def scores(kb):
            k0 = pl.multiple_of(kb * tk, tk)
            return tuple(jnp.dot(ka_ref[hh, pl.ds(k0, tk), :], qat_ref[hh], preferred_element_type=F32) for hh in range(2))

        def step(kb, carry, mask, last=False):
            stats, s_now = carry
            s_next = s_now if last else scores(kb + 1)
            k0 = pl.multiple_of(kb * tk, tk)
            new = []
            for hh in range(2):
                m, l, acc = stats[hh]
                s = s_now[hh] if mask is None else jnp.where(mask, s_now[hh], NEG)
                m2 = jnp.maximum(m, jnp.max(s, axis=0, keepdims=True))
                p = jnp.exp(s - m2)
                a = jnp.exp(m - m2)
                pv = jnp.dot(vat_ref[hh, 0:HEAD, pl.ds(k0, tk)], p.astype(BF), preferred_element_type=F32)
                new.append((m2, a * l + jnp.sum(p, axis=0, keepdims=True), a * acc + pv))
            return tuple(new), s_next

        one = (jnp.full((1, tq), NEG, F32), jnp.zeros((1, tq), F32), jnp.zeros((HEAD, tq), F32))
        carry = lax.fori_loop(0, i * per, lambda kb, c: step(kb, c, None), ((one, one), scores(0)))
        for d in range(per):
            carry = step(i * per + d, carry, krow + d * tk <= qcol, last=(d == per - 1))
        stats = carry[0]
        o_ref[...] = jnp.concatenate([acc / l for _, l, acc in stats], axis=0).T
        lses = [m + jnp.log(l) for m, l, _ in stats]
        lse_ref[0] = jnp.where(sub == 0, lses[0], jnp.where(sub == 1, lses[1], 0.0))

    return pl.pallas_call(
        body, name="fox_forward", grid=(4, nq),
        in_specs=[pl.BlockSpec((2, LANES, tq), lambda j, i: (j, 0, i)), pl.BlockSpec((2, T, LANES), lambda j, i: (j, 0, 0)),
                  pl.BlockSpec((2, LANES, T), lambda j, i: (j, 0, 0))],
        out_specs=[pl.BlockSpec((tq, LANES), lambda j, i: (i, j)), pl.BlockSpec((1, 8, tq), lambda j, i: (j, 0, i))],
        out_shape=[jax.ShapeDtypeStruct((T, 4 * LANES), F32), jax.ShapeDtypeStruct((4, 8, T), F32)],
        compiler_params=_params(("parallel", "arbitrary")),
    )(qat, ka, vat)


def _fox_cotangent(dmix, fox, T, tm):
    def body(do_ref, o_ref, doa_ref, doat_ref, dl_ref):
        lane = lax.broadcasted_iota(jnp.int32, (1, LANES), 1)
        sub = lax.broadcasted_iota(jnp.int32, (8, 1), 0)
        dob = do_ref[...].astype(BF).astype(F32)
        prod_t = (dob * o_ref[...]).T
        d0 = jnp.sum(prod_t[:HEAD], axis=0, keepdims=True)
        d1 = jnp.sum(prod_t[HEAD:], axis=0, keepdims=True)
        dl_ref[0] = jnp.where(sub == 0, d0, jnp.where(sub == 1, d1, 0.0))
        for hh in range(2):
            val = jnp.where(lane < HEAD, dob if hh == 0 else pltpu.roll(dob, HEAD, 1), 0.0)
            doa_ref[hh] = val.astype(BF)
            doat_ref[hh] = val.T.astype(BF)

    return pl.pallas_call(
        body, name="fox_cotangent", grid=(4, T // tm),
        in_specs=[pl.BlockSpec((tm, LANES), lambda j, i: (i, 4 + j)), pl.BlockSpec((tm, LANES), lambda j, i: (i, j))],
        out_specs=[pl.BlockSpec((2, tm, LANES), lambda j, i: (j, i, 0)), pl.BlockSpec((2, LANES, tm), lambda j, i: (j, 0, i)),
                   pl.BlockSpec((1, 8, tm), lambda j, i: (j, 0, i))],
        out_shape=[jax.ShapeDtypeStruct((8, T, LANES), BF), jax.ShapeDtypeStruct((8, LANES, T), BF),
                   jax.ShapeDtypeStruct((4, 8, T), F32)],
        compiler_params=_params(("parallel", "arbitrary")),
    )(dmix, fox)


def _fox_backward(qa, qat, ka, kat, va, doa, doat, lse, dl, T, tq):
    nq = T // tq

    def body(qa_ref, qat_ref, ka_ref, kat_ref, va_ref, doa_ref, doat_ref, lse_ref, dl_ref,
             dq_ref, dk_ref, dv_ref, df_ref, dr_ref, dqt, dk_acc, dv_acc, df_acc):
        j, kb = pl.program_id(0), pl.program_id(1)
        lane = lax.broadcasted_iota(jnp.int32, (1, LANES), 1)
        mask = lax.broadcasted_iota(jnp.int32, (tq, tq), 0) <= lax.broadcasted_iota(jnp.int32, (tq, tq), 1)

        @pl.when(kb == 0)
        def _():
            dqt[...] = jnp.zeros(dqt.shape, F32)

        dk_acc[...] = jnp.zeros(dk_acc.shape, F32)
        dv_acc[...] = jnp.zeros(dv_acc.shape, F32)
        df_acc[...] = jnp.zeros(df_acc.shape, F32)

        def step(qi, diag):
            q0 = pl.multiple_of(qi * tq, tq)
            for hh in range(2):
                s = jnp.dot(ka_ref[hh], qat_ref[hh, :, pl.ds(q0, tq)], preferred_element_type=F32)
                p = jnp.exp(s - lse_ref[0, hh:hh + 1, pl.ds(q0, tq)])
                if diag:
                    p = jnp.where(mask, p, 0.0)
                dp = jnp.dot(va_ref[hh], doat_ref[hh, :, pl.ds(q0, tq)], preferred_element_type=F32)
                ds = p * (dp - dl_ref[0, hh:hh + 1, pl.ds(q0, tq)])
                pb, dsb = p.astype(BF), ds.astype(BF)
                dv_acc[hh] += jnp.dot(pb, doa_ref[hh, pl.ds(q0, tq), :], preferred_element_type=F32)
                dk_acc[hh] += jnp.dot(dsb, qa_ref[hh, pl.ds(q0, tq), :], preferred_element_type=F32)
                dqt[hh, 0:HEAD, pl.ds(q0, tq)] += jnp.dot(kat_ref[hh, 0:HEAD, :], dsb, preferred_element_type=F32)
                dqt[hh, HEAD:HEAD + 8, pl.ds(q0, tq)] += jnp.broadcast_to(jnp.sum(ds, axis=0, keepdims=True), (8, tq))
                part = ds[:, 0:LANES]
                for c in range(1, tq // LANES):
                    part = part + ds[:, c * LANES:(c + 1) * LANES]
                df_acc[hh] += part

        step(kb, True)

        @pl.loop(kb + 1, nq)
        def _(qi):
            step(qi, False)

        lo = lane < HEAD
        dk_ref[...] = jnp.where(lo, dk_acc[0], pltpu.roll(dk_acc[1], HEAD, 1))
        dv_ref[...] = jnp.where(lo, dv_acc[0], pltpu.roll(dv_acc[1], HEAD, 1)).astype(dv_ref.dtype)
        f0 = -jnp.sum(df_acc[0], axis=1, keepdims=True)
        f1 = -jnp.sum(df_acc[1], axis=1, keepdims=True)
        df_ref[0] = jnp.where(lane == 2 * j, f0, jnp.where(lane == 2 * j + 1, f1, 0.0))

        @pl.when(kb == nq - 1)
        def _():
            for t in range(nq):
                cols = slice(t * tq, (t + 1) * tq)
                dq_ref[cols, :] = jnp.concatenate([dqt[0, 0:HEAD, cols], dqt[1, 0:HEAD, cols]], axis=0).T
                rsum = jnp.concatenate([dqt[0, HEAD:HEAD + 8, cols], dqt[1, HEAD:HEAD + 8, cols],
                                        jnp.zeros((LANES - 16, tq), F32)], axis=0).T
                dr_ref[0, cols, :] = jnp.where(lane == 2 * j, rsum[:, 0:1], jnp.where(lane == 2 * j + 1, rsum[:, 8:9], 0.0))

    nat_full = pl.BlockSpec((2, T, LANES), lambda j, kb: (j, 0, 0))
    trn_full = pl.BlockSpec((2, LANES, T), lambda j, kb: (j, 0, 0))
    nat_blk = pl.BlockSpec((2, tq, LANES), lambda j, kb: (j, kb, 0))
    trn_blk = pl.BlockSpec((2, LANES, tq), lambda j, kb: (j, 0, kb))
    rows = pl.BlockSpec((1, 8, T), lambda j, kb: (j, 0, 0))
    blk = pl.BlockSpec((tq, LANES), lambda j, kb: (kb, j))
    return pl.pallas_call(
        body, name="fox_backward", grid=(4, nq),
        in_specs=[nat_full, trn_full, nat_blk, trn_blk, nat_blk, nat_full, trn_full, rows, rows],
        out_specs=[pl.BlockSpec((T, LANES), lambda j, kb: (0, j)), blk, blk, pl.BlockSpec((1, tq, LANES), lambda j, kb: (j, kb, 0)),
                   pl.BlockSpec((1, T, LANES), lambda j, kb: (j, 0, 0))],
        out_shape=[jax.ShapeDtypeStruct((T, 4 * LANES), F32), jax.ShapeDtypeStruct((T, 4 * LANES), F32),
                   jax.ShapeDtypeStruct((T, 4 * LANES), BF), jax.ShapeDtypeStruct((4, T, LANES), F32),
                   jax.ShapeDtypeStruct((4, T, LANES), F32)],
        scratch_shapes=[pltpu.VMEM((2, HEAD + 8, T), F32), pltpu.VMEM((2, tq, LANES), F32), pltpu.VMEM((2, tq, LANES), F32),
                        pltpu.VMEM((2, tq, LANES), F32)],
        compiler_params=_params(("arbitrary", "arbitrary")),
    )(qa, qat, ka, kat, va, doa, doat, lse, dl)


def _fgate_bwd_col(ffp, bpad, dfc, T):
    def body(ff_ref, b_ref, dfc_ref, dff_ref, db_ref):
        lane = lax.broadcasted_iota(jnp.int32, (1, LANES), 1)
        tri = _tri(False)
        carry = jnp.zeros((1, LANES), F32)
        db = jnp.zeros((1, LANES), F32)
        for blk in reversed(range(T // _FB)):
            dlf = jnp.dot(tri, dfc_ref[blk * _FB:(blk + 1) * _FB, :], precision=lax.Precision.HIGHEST,
                          preferred_element_type=F32) + carry
            carry = dlf[0:1, :]
            z = ff_ref[blk * _FB:(blk + 1) * _FB, :] + b_ref[...]
            dz = jnp.where(lane < 8, dlf * jax.nn.sigmoid(-z), 0.0)
            dff_ref[blk * _FB:(blk + 1) * _FB, :] = dz.astype(dff_ref.dtype)
            db = db + jnp.sum(dz, axis=0, keepdims=True)
        db_ref[...] = db

    return pl.pallas_call(
        body, name="fgate_bwd",
        out_shape=[jax.ShapeDtypeStruct((T, LANES), BF), jax.ShapeDtypeStruct((1, LANES), F32)],
        compiler_params=pltpu.CompilerParams(vmem_limit_bytes=VMEM_LIMIT),
    )(ffp, bpad, dfc)


MESH = pl.DeviceIdType.MESH


def _place():
    return lax.axis_index("x"), lax.axis_index("y"), lax.axis_index("c")


def _all_gather(shard):
    R, W = shard.shape

    def body(x_ref, out_ref, send_sems, recv_sems, local_sem):
        x, y, c = _place()
        me, sibling = (x, y, c), (x, y, 1 - c)
        chips = [(1 - x, y), (x, 1 - y), (1 - x, 1 - y)]

        def slot(px, py, pc):
            return out_ref.at[4 * px + 2 * py + pc]

        def copy(k, block, to, src=None):
            return pltpu.make_async_remote_copy(
                src_ref=slot(*block) if src is None else src, dst_ref=slot(*block),
                send_sem=send_sems.at[k], recv_sem=recv_sems.at[k], device_id=to, device_id_type=MESH)

        mine = pltpu.make_async_copy(x_ref, slot(*me), local_sem)
        mine.start()
        first = [copy(0, me, sibling, src=x_ref)]
        first += [copy(1 + n, me, (*chip, c), src=x_ref) for n, chip in enumerate(chips)]
        for cp in first:
            cp.start()
        passed = [copy(4 + n, (*chip, c), sibling) for n, chip in enumerate(chips)]
        for n, chip in enumerate(chips):
            copy(1 + n, (*chip, c), me).wait_recv()
            passed[n].start()
        copy(0, sibling, me).wait_recv()
        for n, chip in enumerate(chips):
            copy(4 + n, (*chip, 1 - c), me).wait_recv()
        for cp in first + passed:
            cp.wait_send()
        mine.wait()

    return pl.pallas_call(
        body, name="all_gather_weights",
        out_shape=jax.ShapeDtypeStruct((N_DEV, R, W), shard.dtype),
        in_specs=[pl.BlockSpec(memory_space=pl.ANY)], out_specs=pl.BlockSpec(memory_space=pl.ANY),
        scratch_shapes=[pltpu.SemaphoreType.DMA((7,)), pltpu.SemaphoreType.DMA((7,)), pltpu.SemaphoreType.DMA],
    )(shard)


def _all_to_all(big, small):
    def body(big_ref, small_ref, rbig_ref, rsmall_ref, send_sems, recv_sems, local_sems):
        x, y, c = _place()
        me = 4 * x + 2 * y + c
        l0 = pltpu.make_async_copy(big_ref.at[me], rbig_ref.at[me], local_sems.at[0])
        l1 = pltpu.make_async_copy(small_ref, rsmall_ref.at[me], local_sems.at[1])
        l0.start()
        l1.start()
        copies = []
        for r in range(1, N_DEV):
            px, py, pc = x ^ (r >> 2), y ^ ((r >> 1) & 1), c ^ (r & 1)
            peer = 4 * px + 2 * py + pc
            copies.append(pltpu.make_async_remote_copy(
                src_ref=big_ref.at[peer], dst_ref=rbig_ref.at[me], send_sem=send_sems.at[2 * r], recv_sem=recv_sems.at[2 * r],
                device_id=(px, py, pc), device_id_type=MESH))
            copies.append(pltpu.make_async_remote_copy(
                src_ref=small_ref, dst_ref=rsmall_ref.at[me], send_sem=send_sems.at[2 * r + 1], recv_sem=recv_sems.at[2 * r + 1],
                device_id=(px, py, pc), device_id_type=MESH))
        for cp in copies:
            cp.start()
        for cp in copies:
            cp.wait_recv()
        for cp in copies:
            cp.wait_send()
        l0.wait()
        l1.wait()

    return pl.pallas_call(
        body, name="all_to_all_grads",
        out_shape=[jax.ShapeDtypeStruct(big.shape, big.dtype), jax.ShapeDtypeStruct((N_DEV,) + small.shape, small.dtype)],
        in_specs=[pl.BlockSpec(memory_space=pl.ANY)] * 2, out_specs=[pl.BlockSpec(memory_space=pl.ANY)] * 2,
        scratch_shapes=[pltpu.SemaphoreType.DMA((2 * N_DEV,)), pltpu.SemaphoreType.DMA((2 * N_DEV,)), pltpu.SemaphoreType.DMA((2,))],
    )(big, small)


def _adamw(name, slots, w, m, v, tr):
    R, W = w.shape

    def body(s_ref, w_ref, m_ref, v_ref, g_ref, d_ref, nm_ref, nv_ref):
        g = s_ref[0].astype(F32)
        for s in range(1, N_DEV):
            g = g + s_ref[s].astype(F32)
        m2 = ADAM_B1 * m_ref[...] + (1.0 - ADAM_B1) * g
        v2 = ADAM_B2 * v_ref[...] + (1.0 - ADAM_B2) * jnp.square(g)
        m_hat = m2 / (1.0 - ADAM_B1 ** ADAM_STEP)
        v_hat = v2 / (1.0 - ADAM_B2 ** ADAM_STEP)
        g_ref[...] = g
        d_ref[...] = -ADAM_LR * (m_hat / (jnp.sqrt(v_hat) + ADAM_EPS) + ADAM_WD * w_ref[...])
        nm_ref[...] = m2
        nv_ref[...] = v2

    row = lambda: pl.BlockSpec((tr, W), lambda i: (i, 0))
    return pl.pallas_call(
        body, name=name, grid=(R // tr,),
        in_specs=[pl.BlockSpec((N_DEV, tr, W), lambda i: (0, i, 0)), row(), row(), row()],
        out_specs=[row(), row(), row(), row()],
        out_shape=[jax.ShapeDtypeStruct((R, W), F32)] * 4,
        compiler_params=_params(("parallel",)),
    )(slots, w, m, v)


def _tables(T):
    pos = jnp.arange(T, dtype=F32)
    inv_freq = 10000.0 ** (-jnp.arange(0, HEAD, 2, dtype=F32) / HEAD)
    ang = pos[:, None] * inv_freq[None, :]
    cos, sin = jnp.cos(ang), jnp.sin(ang)
    cos4 = jnp.tile(cos, (1, 4))
    sin4 = jnp.tile(jnp.concatenate([-sin, sin], axis=1), (1, 2))
    log_g = jnp.log(1.0 - 2.0 ** (-5.0 - jnp.arange(8, dtype=F32)))
    return cos4, sin4, jnp.repeat(log_g, HEAD)[None, :]


def _local_step(x, mem, target, sp, W):
    T = x.shape[0]
    tm = min(512, T)
    tq = min(256, T)
    tb = min(1024, T)
    cos4, sin4, lg = _tables(T)
    g_fq2 = jnp.tile(sp["g_fox_q"], (1, 2))
    g_fk2 = jnp.tile(sp["g_fox_k"], (1, 2))
    g_ret = sp["g_ret_out"].reshape(1, 8 * HEAD)
    bpad = jnp.pad(sp["b_forget"], ((0, 0), (0, LANES - 8)))
    w_secs = [W["w_inT"][k * 512:(k + 1) * 512] for k in range(7)]
    w_ffT = jnp.pad(W["w_inT"][3584:3592], ((0, LANES - 8), (0, 0)))
    w_mainT = W["w_inT"][:3584]

    hn1, = _rw_fwd("rms_mix", _rms_fn, [(x, D, 0, False)], [(sp["g_mix"], D, 0, False)], [(BF, D)], T, tm, 1)
    P, = _mm("proj_in", [[(hn1, w_mainT, "nt")]], [], _ident, T, 3584, tm, 512, [F32])
    ffp, = _mm("proj_ff", [[(hn1, w_ffT, "nt")]], [], _ident, T, LANES, tm, LANES, [F32])
    ret, s0 = _ret_fwd(P, cos4, sin4, g_ret, lg, T, tb)
    fc, _ = _fgate_fwd(ffp, bpad, T)
    qa, qat, ka, kat, va, vat = _fox_operands(P, fc, g_fq2, g_fk2, T, tm)
    fox, lse = _fox_forward(qat, ka, vat, T, tq, min(128, T))
    mix = jnp.concatenate([ret, fox], axis=1)
    h1, = _mm("proj_out", [[(mix, W["w_out"], "nn")]], [x], _add, T, D, tm, 512, [F32])

    hn2, = _rw_fwd("rms_xattn", _rms_fn, [(h1, D, 0, False)], [(sp["g_xattn"], D, 0, False)], [(BF, D)], T, tm, 1)
    qx, = _mm("proj_xq", [[(hn2, W["w_xq"], "nn")]], [], _ident, T, D, tm, 512, [F32])
    memn, = _rw_fwd("rms_mem", _rms_fn, [(mem, D, 0, False)], [(sp["g_mem"], D, 0, False)], [(BF, D)], N_MEM, N_MEM, 1)
    kv, = _mm("proj_xkv", [[(memn, W["w_xkvT"], "nt")]], [], _ident, N_MEM, 2 * D, N_MEM, 512, [F32])
    xa_rows = [(qx, XHEAD, 0, True)]
    xa_params = [(sp["g_xq"], XHEAD, 0, False), (sp["g_xk"], XHEAD, 0, False), (kv, XHEAD, 0, True), (kv, XHEAD, 4, True)]
    xo, = _rw_fwd("xattn_fwd", _xattn_fn, xa_rows, xa_params, [(BF, XHEAD)], T, tm, 4)
    h2, = _mm("proj_xo", [[(xo, W["w_xo"], "nn")]], [h1], _add, T, D, tm, 512, [F32])

    hn3, = _rw_fwd("rms_ffn", _rms_fn, [(h2, D, 0, False)], [(sp["g_ffn"], D, 0, False)], [(BF, D)], T, tm, 1)
    gate, up, act = _mm("ffn_in", [[(hn3, W["w_gateT"], "nt")], [(hn3, W["w_upT"], "nt")]], [], _swiglu_fwd_epi,
                        T, D_FF, tm, 256, [F32, F32, BF])
    h3, = _mm("ffn_out", [[(act, W["w_down"], "nn")]], [h2], _add, T, D, tm, 512, [F32])
    dy, loss_part = _rw_fwd("loss", _loss_fn, [(h3, D, 0, False), (target, D, 0, False)], [], [(F32, D)], T, tm, 1, n_acc=1)

    dgate, dup = _mm("ffn_out_bwd", [[(dy, W["w_down"], "nt")]], [gate, up], _swiglu_bwd_epi, T, D_FF, tm, 256, [BF, BF])
    dhn3, = _mm("ffn_in_bwd", [[(dgate, W["w_gateT"], "nn"), (dup, W["w_upT"], "nn")]], [], _ident, T, D, tm, 512, [F32])
    gW = {}
    gW["w_gateT"], = _mm("dw_gate", [[(dgate, hn3, "tn")]], [], _ident, D_FF, D, 256, 512, [F32])
    gW["w_upT"], = _mm("dw_up", [[(dup, hn3, "tn")]], [], _ident, D_FF, D, 256, 512, [F32])
    gW["w_down"], = _mm("dw_down", [[(act, dy, "tn")]], [], _ident, D_FF, D, 256, 512, [F32])
    gs = {}
    dh2n, gs["g_ffn"] = _rw_bwd("rms_ffn_bwd", _rms_fn, [(h2, D, 0, False)], [(sp["g_ffn"], D, 0, False)], [(dhn3, D, 0, False)],
                                T, tm, 1, [F32], [True])
    dh2, = _rw_fwd("add_dh2", lambda a, b: (a + b,), [(dy, D, 0, False), (dh2n, D, 0, False)], [], [(F32, D)], T, tm, 1)

    dxo, = _mm("proj_xo_bwd", [[(dh2, W["w_xo"], "nt")]], [], _ident, T, D, tm, 512, [BF])
    gW["w_xo"], = _mm("dw_xo", [[(xo, dh2, "tn")]], [], _ident, D, D, 256, 512, [F32])
    dqx, gs["g_xq"], gs["g_xk"], dkv_k, dkv_v = _rw_bwd(
        "xattn_bwd", _xattn_fn, xa_rows, xa_params, [(dxo, XHEAD, 0, True)], T, tm, 4, [BF], [True, True, True, True])
    dkv = jnp.concatenate([dkv_k[:, :D], dkv_v[:, D:]], axis=1)
    dhn2, = _mm("proj_xq_bwd", [[(dqx, W["w_xq"], "nt")]], [], _ident, T, D, tm, 512, [F32])
    gW["w_xq"], = _mm("dw_xq", [[(hn2, dqx, "tn")]], [], _ident, D, D, 256, 512, [F32])
    dmemn, = _mm("proj_xkv_bwd", [[(dkv, W["w_xkvT"], "nn")]], [], _ident, N_MEM, D, N_MEM, 512, [F32])
    gW["w_xkvT"], = _mm("dw_xkv", [[(dkv, memn, "tn")]], [], _ident, 2 * D, D, 512, 512, [F32])
    gs["g_mem"], = _rw_bwd("rms_mem_bwd", _rms_fn, [(mem, D, 0, False)], [(sp["g_mem"], D, 0, False)], [(dmemn, D, 0, False)],
                           N_MEM, N_MEM, 1, [None], [True])
    dh1n, gs["g_xattn"] = _rw_bwd("rms_xattn_bwd", _rms_fn, [(h1, D, 0, False)], [(sp["g_xattn"], D, 0, False)],
                                  [(dhn2, D, 0, False)], T, tm, 1, [F32], [True])
    dh1, = _rw_fwd("add_dh1", lambda a, b: (a + b,), [(dh2, D, 0, False), (dh1n, D, 0, False)], [], [(F32, D)], T, tm, 1)

    dmix, = _mm("proj_out_bwd", [[(dh1, W["w_out"], "nt")]], [], _ident, T, D, tm, 512, [F32])
    gW["w_out"], = _mm("dw_out", [[(mix, dh1, "tn")]], [], _ident, D, D, 256, 512, [F32])
    doa, doat, dl = _fox_cotangent(dmix, fox, T, tm)
    dqn, dkn, dfv, dfc4, drc4 = _fox_backward(qa, qat, ka, kat, va, doa, doat, lse, dl, T, tq)
    dfq, dfk, gq2, gk2 = _rw_bwd("fox_prep_bwd", _fox_prep_fn, [(P, LANES, 16, True), (P, LANES, 20, True)],
                                 [(g_fq2, LANES, 0, False), (g_fk2, LANES, 0, False)],
                                 [(dqn, LANES, 0, True), (dkn, LANES, 0, True)], T, tm, 4, [BF, BF], [True, True])
    gs["g_fox_q"] = gq2[:, :HEAD] + gq2[:, HEAD:]
    gs["g_fox_k"] = gk2[:, :HEAD] + gk2[:, HEAD:]
    dff, dbp = _fgate_bwd_col(ffp, bpad, jnp.sum(dfc4 + drc4, axis=0), T)
    gs["b_forget"] = dbp[:, :8]
    drq, drk, drv, drg, dg_ret = _ret_bwd(P, cos4, sin4, g_ret, lg, s0, dmix, T, tb)
    gs["g_ret_out"] = dg_ret
    dsecs = [drq, drk, drv, drg, dfq, dfk, dfv]
    dhn1, = _mm("proj_in_bwd", [[(d, w, "nn") for d, w in zip(dsecs, w_secs)] + [(dff, w_ffT, "nn")]], [], _ident,
                T, D, tm, 512, [F32])
    g_secs = [_mm("dw_in_%d" % k, [[(d, hn1, "tn")]], [], _ident, 512, D, 256, 512, [F32])[0] for k, d in enumerate(dsecs)]
    g_ff, = _mm("dw_in_ff", [[(dff, hn1, "tn")]], [], _ident, LANES, D, LANES, 512, [F32])
    gW["w_inT"] = jnp.concatenate(g_secs + [g_ff[:8]], axis=0)
    dxn, gs["g_mix"] = _rw_bwd("rms_mix_bwd", _rms_fn, [(x, D, 0, False)], [(sp["g_mix"], D, 0, False)], [(dhn1, D, 0, False)],
                               T, tm, 1, [F32], [True])
    grad_x, = _rw_fwd("add_dx", lambda a, b: (a + b,), [(dh1, D, 0, False), (dxn, D, 0, False)], [], [(F32, D)], T, tm, 1)
    return loss_part, grad_x, gW, gs


_CANON = {"w_in": "w_inT", "w_xkv": "w_xkvT", "w_gate": "w_gateT", "w_up": "w_upT"}
_SMALL = (("g_mix", 0, 0, 1024), ("g_xattn", 1, 0, 1024), ("g_mem", 2, 0, 1024), ("g_ffn", 3, 0, 1024),
          ("g_ret_out", 4, 0, 512), ("g_xq", 4, 512, 256), ("g_xk", 4, 768, 256),
          ("g_fox_q", 5, 0, 64), ("g_fox_k", 5, 64, 64), ("b_forget", 5, 128, 8))
_LOSS_AT = (5, 256)


def _pack_shards(tree, dtype):
    parts = []
    for name, rows, padded, transposed in W_LAYOUT:
        a = tree[name][0]
        a = a.T if transposed else a
        parts.append(jnp.pad(a, ((0, padded - rows), (0, 0))).astype(dtype))
    return jnp.concatenate(parts, axis=0)


def _unpack_shards(packed, like):
    out = {}
    for name, rows, padded, transposed in W_LAYOUT:
        a = packed[W_OFF[name]:W_OFF[name] + rows]
        out[name] = (a.T if transposed else a)[None].reshape(like[name].shape)
    return out


def _pack_small(tree):
    rows = [jnp.zeros((1, D), F32) for _ in range(SMALL_ROWS)]
    buf = jnp.concatenate(rows, axis=0)
    for name, r, c, n in _SMALL:
        buf = lax.dynamic_update_slice(buf, tree[name].reshape(1, n).astype(F32), (r, c))
    return buf


def _unpack_small(buf, like):
    return {name: buf[r:r + 1, c:c + n].reshape(like[name].shape) for name, r, c, n in _SMALL}


def kernel(x, mem, g_mix, w_in, b_forget, g_ret_out, g_fox_q, g_fox_k, w_out, g_xattn, w_xq, w_xkv, g_mem, g_xq, g_xk, w_xo, g_ffn, w_gate, w_up, w_down, loss_target, m_g_mix, m_w_in, m_b_forget, m_g_ret_out, m_g_fox_q, m_g_fox_k, m_w_out, m_g_xattn, m_w_xq, m_w_xkv, m_g_mem, m_g_xq, m_g_xk, m_w_xo, m_g_ffn, m_w_gate, m_w_up, m_w_down, v_g_mix, v_w_in, v_b_forget, v_g_ret_out, v_g_fox_q, v_g_fox_k, v_w_out, v_g_xattn, v_w_xq, v_w_xkv, v_g_mem, v_g_xq, v_g_xk, v_w_xo, v_g_ffn, v_w_gate, v_w_up, v_w_down):
    names = ("g_mix", "w_in", "b_forget", "g_ret_out", "g_fox_q", "g_fox_k", "w_out", "g_xattn", "w_xq", "w_xkv", "g_mem",
             "g_xq", "g_xk", "w_xo", "g_ffn", "w_gate", "w_up", "w_down")
    w = dict(zip(names, (g_mix, w_in, b_forget, g_ret_out, g_fox_q, g_fox_k, w_out, g_xattn, w_xq, w_xkv, g_mem, g_xq, g_xk,
                         w_xo, g_ffn, w_gate, w_up, w_down)))
    m = dict(zip(names, (m_g_mix, m_w_in, m_b_forget, m_g_ret_out, m_g_fox_q, m_g_fox_k, m_w_out, m_g_xattn, m_w_xq, m_w_xkv,
                         m_g_mem, m_g_xq, m_g_xk, m_w_xo, m_g_ffn, m_w_gate, m_w_up, m_w_down)))
    v = dict(zip(names, (v_g_mix, v_w_in, v_b_forget, v_g_ret_out, v_g_fox_q, v_g_fox_k, v_w_out, v_g_xattn, v_w_xq, v_w_xkv,
                         v_g_mem, v_g_xq, v_g_xk, v_w_xo, v_g_ffn, v_w_gate, v_w_up, v_w_down)))
    small_names = [s[0] for s in _SMALL]

    gathered = _all_gather(_pack_shards(w, BF))
    W = {}
    for name, rows, padded, transposed in W_LAYOUT:
        full = gathered[:, W_OFF[name]:W_OFF[name] + rows].reshape(N_DEV * rows, D)
        W[_CANON.get(name, name)] = full

    sp = {n: w[n].reshape(1, -1) for n in small_names}
    loss_part, grad_x, gW, gs = _local_step(x[0], mem[0], loss_target[0], sp, W)

    chunks = []
    for name, rows, padded, transposed in W_LAYOUT:
        g = gW[_CANON.get(name, name)].reshape(N_DEV, rows, D)
        chunks.append(jnp.pad(g, ((0, 0), (0, padded - rows), (0, 0))).astype(BF))
    send = jnp.concatenate(chunks, axis=1)
    small = _pack_small(gs)
    small = lax.dynamic_update_slice(small, loss_part[:, :1], _LOSS_AT)
    recv, recv_small = _all_to_all(send, small)

    g_big, d_big, m_big, v_big = _adamw("adamw_shards", recv, _pack_shards(w, F32), _pack_shards(m, F32), _pack_shards(v, F32), 240)
    g_sm, d_sm, m_sm, v_sm = _adamw("adamw_small", recv_small, _pack_small(w), _pack_small(m), _pack_small(v), SMALL_ROWS)
    loss = g_sm[_LOSS_AT[0], _LOSS_AT[1]]

    outs = []
    for big, sm in ((g_big, g_sm), (d_big, d_sm), (m_big, m_sm), (v_big, v_sm)):
        tree = {**_unpack_shards(big, w), **_unpack_small(sm, w)}
        outs += [tree[n] for n in names]
    return (loss, grad_x[None], *outs)
```

```python
import functools
import math

import jax
import jax.numpy as jnp
import numpy as np
from jax import lax
from jax.experimental import pallas as pl
from jax.experimental.pallas import tpu as pltpu

F32 = jnp.float32
BF = jnp.bfloat16

D = 1024
HEAD = 64
CHUNK = 64
N_MEM = 256
XHEAD = 256
D_FF = 2816
EPS = 1e-6
NEG = -1e30
LANES = 128
N_DEV = 8
V7X_VMEM_BYTES = 64 * 1024 * 1024
VMEM_LIMIT = V7X_VMEM_BYTES - 8 * 1024 * 1024

ADAM_LR, ADAM_B1, ADAM_B2, ADAM_EPS, ADAM_WD, ADAM_STEP = 0.001, 0.9, 0.999, 1e-08, 0.01, 10

W_LAYOUT = (("w_in", 449, 464, True), ("w_out", 128, 128, False), ("w_xq", 128, 128, False), ("w_xkv", 256, 256, True),
            ("w_xo", 128, 128, False), ("w_gate", 352, 352, True), ("w_up", 352, 352, True), ("w_down", 352, 352, False))
W_ROWS = sum(w[2] for w in W_LAYOUT)
W_OFF = {}
_o = 0
for _n, _r, _p, _t in W_LAYOUT:
    W_OFF[_n] = _o
    _o += _p
SMALL_ROWS = 8
W_SHARD = {"w_in": (449, 464, True), "w_out": (128, 128, False), "w_xq": (128, 128, False), "w_xkv": (256, 256, True),
           "w_xo": (128, 128, False), "w_gate": (352, 352, True), "w_up": (352, 352, True), "w_down": (352, 352, False)}
GATHER_FIRST = ("w_in",)
GATHER_REST = ("w_out", "w_xq", "w_xkv", "w_xo", "w_gate", "w_up", "w_down")
GRAD_GROUPS = {"ffn": ("w_gate", "w_up", "w_down"), "xattn": ("w_xq", "w_xkv", "w_xo"), "mix": ("w_in", "w_out")}

NT = (((1,), (1,)), ((), ()))
NN = (((1,), (0,)), ((), ()))
TN = (((0,), (0,)), ((), ()))
_DIMS = {"nn": NN, "nt": NT, "tn": TN}


def _params(sem):
    return pltpu.CompilerParams(dimension_semantics=sem, vmem_limit_bytes=VMEM_LIMIT)


def _mm(name, products, extras, epilogue, M, N, tm, tn, out_dtypes):
    flat = [t for p in products for t in p]
    counts = [len(p) for p in products]
    in_specs, args = [], []
    for a, b, form in flat:
        if form == "tn":
            in_specs.append(pl.BlockSpec((a.shape[0], tm), lambda i, j: (0, i)))
        else:
            in_specs.append(pl.BlockSpec((tm, a.shape[1]), lambda i, j: (i, 0)))
        if form == "nt":
            in_specs.append(pl.BlockSpec((tn, b.shape[1]), lambda i, j: (j, 0)))
        else:
            in_specs.append(pl.BlockSpec((b.shape[0], tn), lambda i, j: (0, j)))
        args += [a, b]
    for e in extras:
        in_specs.append(pl.BlockSpec((tm, tn), lambda i, j: (i, j)))
        args.append(e)
    n_in = len(args)

    def body(*refs):
        ins, outs = refs[:n_in], refs[n_in:]
        prods, p = [], 0
        for c in counts:
            acc = None
            for _ in range(c):
                a = ins[2 * p][...].astype(BF)
                b = ins[2 * p + 1][...].astype(BF)
                d = lax.dot_general(a, b, _DIMS[flat[p][2]], preferred_element_type=F32)
                acc = d if acc is None else acc + d
                p += 1
            prods.append(acc)
        ex = [r[...].astype(F32) for r in ins[2 * len(flat):]]
        res = epilogue(*prods, *ex)
        for o, r in zip(outs, res):
            o[...] = r.astype(o.dtype)

    return pl.pallas_call(
        body, name=name, grid=(M // tm, N // tn), in_specs=in_specs,
        out_specs=[pl.BlockSpec((tm, tn), lambda i, j: (i, j)) for _ in out_dtypes],
        out_shape=[jax.ShapeDtypeStruct((M, N), dt) for dt in out_dtypes],
        compiler_params=_params(("parallel", "arbitrary")),
    )(*args)


def _ident(x):
    return (x,)


def _add(x, r):
    return (x + r,)


def _spec(rows, w, off, per_j):
    if per_j:
        return pl.BlockSpec((rows, w), lambda j, i: (i, off + j))
    return pl.BlockSpec((rows, w), lambda j, i: (i, off))


def _pspec(rows, w, off, per_j):
    if per_j:
        return pl.BlockSpec((rows, w), lambda j, i: (0, off + j))
    return pl.BlockSpec((rows, w), lambda j, i: (0, off))


def _rw_fwd(name, fn, rows, params, outs, T, tm, nj, n_acc=0):
    in_specs = [_spec(tm, w, off, pj) for _, w, off, pj in rows] + [_pspec(a.shape[0], w, off, pj) for a, w, off, pj in params]
    args = [r[0] for r in rows] + [p[0] for p in params]
    n_in, n_out = len(args), len(outs)
    out_specs = [pl.BlockSpec((tm, w), lambda j, i: (i, j)) for _, w in outs]
    out_shape = [jax.ShapeDtypeStruct((T, nj * w), dt) for dt, w in outs]
    out_specs += [pl.BlockSpec((1, LANES), lambda j, i: (0, 0)) for _ in range(n_acc)]
    out_shape += [jax.ShapeDtypeStruct((1, LANES), F32) for _ in range(n_acc)]

    def body(*refs):
        vals = [r[...].astype(F32) for r in refs[:n_in]]
        res = fn(*vals)
        orefs = refs[n_in:]
        for k in range(n_out):
            orefs[k][...] = res[k].astype(orefs[k].dtype)
        first = (pl.program_id(0) == 0) & (pl.program_id(1) == 0)
        for k in range(n_acc):
            @pl.when(first)
            def _(k=k):
                orefs[n_out + k][...] = jnp.zeros((1, LANES), F32)
            orefs[n_out + k][...] += res[n_out + k]

    return pl.pallas_call(
        body, name=name, grid=(nj, T // tm), in_specs=in_specs, out_specs=out_specs, out_shape=out_shape,
        compiler_params=_params(("arbitrary", "arbitrary")),
    )(*args)


def _rw_bwd(name, fn, rows, params, cots, T, tm, nj, row_grads, param_grads, resid=None):
    in_specs = ([_spec(tm, w, off, pj) for _, w, off, pj in rows] + [_pspec(a.shape[0], w, off, pj) for a, w, off, pj in params]
                + [_spec(tm, w, off, pj) for _, w, off, pj in cots])
    args = [r[0] for r in rows] + [p[0] for p in params] + [c[0] for c in cots]
    if resid is not None:
        in_specs.append(_spec(tm, rows[0][1], rows[0][2], rows[0][3]))
        args.append(resid)
    nr, npar, nc = len(rows), len(params), len(cots)
    out_specs, out_shape, kinds = [], [], []
    for k, dt in enumerate(row_grads):
        if dt is not None:
            w = rows[k][1]
            out_specs.append(pl.BlockSpec((tm, w), lambda j, i: (i, j)))
            out_shape.append(jax.ShapeDtypeStruct((T, nj * w), dt))
            kinds.append(("row", k))
    for k, need in enumerate(param_grads):
        if need:
            a, w, off, pj = params[k]
            out_specs.append(_pspec(a.shape[0], w, off, pj))
            out_shape.append(jax.ShapeDtypeStruct(a.shape, F32))
            kinds.append(("par", k))

    def body(*refs):
        vals = [r[...].astype(F32) for r in refs[:nr + npar]]
        ct = tuple(r[...].astype(F32) for r in refs[nr + npar:nr + npar + nc])
        _, vjp = jax.vjp(lambda *a: tuple(fn(*a)), *vals)
        grads = list(vjp(ct))
        n_in = nr + npar + nc + (resid is not None)
        if resid is not None:
            grads[0] = grads[0] + refs[n_in - 1][...].astype(F32)
        orefs = refs[n_in:]
        j, i = pl.program_id(0), pl.program_id(1)
        for o, (kind, k) in zip(orefs, kinds):
            if kind == "row":
                o[...] = grads[k].astype(o.dtype)
            else:
                first = (i == 0) if params[k][3] else ((i == 0) & (j == 0))

                @pl.when(first)
                def _(o=o):
                    o[...] = jnp.zeros(o.shape, F32)
                o[...] += grads[nr + k]

    return pl.pallas_call(
        body, name=name, grid=(nj, T // tm), in_specs=in_specs, out_specs=out_specs, out_shape=out_shape,
        compiler_params=_params(("arbitrary", "arbitrary")),
    )(*args)


def _rms(x, g):
    return x * lax.rsqrt(jnp.mean(x * x, axis=-1, keepdims=True) + EPS) * g


def _rms_fn(x, g):
    return (_rms(x, g),)


def _lo_mask():
    return lax.broadcasted_iota(jnp.int32, (1, LANES), 1) < HEAD


def _gmean(x, lo):
    s0 = jnp.sum(jnp.where(lo, x, 0.0), axis=-1, keepdims=True)
    s1 = jnp.sum(jnp.where(lo, 0.0, x), axis=-1, keepdims=True)
    return jnp.where(lo, s0, s1) * (1.0 / HEAD)


def _fox_prep_fn(fq, fk, gq, gk):
    lo = _lo_mask()
    qn = fq * lax.rsqrt(_gmean(fq * fq, lo) + EPS) * gq * (HEAD ** -0.5)
    kn = fk * lax.rsqrt(_gmean(fk * fk, lo) + EPS) * gk
    return qn, kn


def _cast_fn(v):
    return (v,)


@jax.custom_vjp
def _swap_halves(x):
    bit = (lax.broadcasted_iota(jnp.int32, (1, LANES), 1) & (HEAD // 2)) == 0
    return jnp.where(bit, pltpu.roll(x, LANES - HEAD // 2, 1), pltpu.roll(x, HEAD // 2, 1))


_swap_halves.defvjp(lambda x: (_swap_halves(x), None), lambda _, g: (_swap_halves(g),))


def _ret_fn(rq, rk, rv, rg, cos, sin, s_in, g, lg):
    tb = rq.shape[0]
    nc = tb // CHUNK
    lo = _lo_mask()
    row = lax.broadcasted_iota(jnp.int32, (LANES, 1), 0) < HEAD
    same_head = row == lo
    q = (rq * cos + _swap_halves(rq) * sin) * (HEAD ** -0.5)
    k = rk * cos + _swap_halves(rk) * sin
    q3, k3, v3 = q.reshape(nc, CHUNK, LANES), k.reshape(nc, CHUNK, LANES), rv.reshape(nc, CHUNK, LANES)
    pos = lax.broadcasted_iota(jnp.int32, (CHUNK, 1), 0).astype(F32)
    q_decay = jnp.exp(lg * (pos + 1.0))
    k_decay = jnp.exp(lg * (CHUNK - 1.0 - pos))
    chunk_decay = jnp.exp(lg * float(CHUNK))
    dist = jnp.abs(lax.broadcasted_iota(jnp.int32, (CHUNK, CHUNK), 0) - lax.broadcasted_iota(jnp.int32, (CHUNK, CHUNK), 1)).astype(F32)
    v3b = v3.astype(BF)
    intra = []
    for hh in range(2):
        hm = lo if hh == 0 else ~lo
        lg_h = lg[:, hh * HEAD:hh * HEAD + 1]
        qm = jnp.where(hm, q3, 0.0).astype(BF)
        sc = jnp.einsum("nid,njd->nij", qm, k3.astype(BF), preferred_element_type=F32) * jnp.exp(lg_h * dist)[None]
        intra.append(jnp.einsum("nij,nje->nie", sc.astype(BF), v3b, preferred_element_type=F32))
    o = jnp.where(lo, intra[0], intra[1])
    kv = jnp.einsum("njd,nje->nde", (k3 * k_decay[None]).astype(BF), v3b, preferred_element_type=F32)
    kv = jnp.where(same_head[None], kv, 0.0)
    state, states = s_in, []
    for n in range(nc):
        states.append(state)
        state = state * chunk_decay + kv[n]
    s_prev = jnp.stack(states, axis=0)
    o = o + jnp.einsum("nid,nde->nie", (q3 * q_decay[None]).astype(BF), s_prev.astype(BF), preferred_element_type=F32)
    o = o.reshape(tb, LANES)
    mu = _gmean(o, lo)
    oc = o - mu
    y = oc * lax.rsqrt(_gmean(oc * oc, lo) + EPS) * g
    return jax.nn.silu(rg) * y, state


def _xattn_fn(qx, gq, gk, kk, vv):
    q = _rms(qx, gq)
    k = _rms(kk, gk)
    logits = lax.dot_general(q.astype(BF), k.astype(BF), NT, preferred_element_type=F32) * (XHEAD ** -0.5)
    p = jax.nn.softmax(logits, axis=-1)
    return (jnp.dot(p.astype(BF), vv.astype(BF), preferred_element_type=F32),)


def _swiglu_fwd_epi(g, u):
    return g, u, jax.nn.silu(g) * u


def _swiglu_bwd_epi(dact, g, u):
    _, vjp = jax.vjp(lambda a, b: jax.nn.silu(a) * b, g, u)
    return vjp(dact)


def _loss_fn(h, target):
    err = h - target
    part = jnp.sum(jnp.sum(err * err, axis=0, keepdims=True), axis=-1, keepdims=True) * (0.5 / D)
    return err * (1.0 / D), part


def _ret_fwd(P, cos, sin, g_ret, lg, T, tb):
    nb = T // tb

    def body(rq, rk, rv, rg, c, s, g, l, o_ref, s0_ref, state):
        @pl.when(pl.program_id(1) == 0)
        def _():
            state[...] = jnp.zeros(state.shape, F32)
        s0_ref[0, 0] = state[...]
        out, s_new = _ret_fn(rq[...], rk[...], rv[...], rg[...], c[...], s[...], state[...], g[...], l[...])
        o_ref[...] = out
        state[...] = s_new

    sec = lambda off: pl.BlockSpec((tb, LANES), lambda j, i: (i, off + j))
    tab = pl.BlockSpec((tb, LANES), lambda j, i: (i, 0))
    par = pl.BlockSpec((1, LANES), lambda j, i: (0, j))
    return pl.pallas_call(
        body, name="ret_fwd", grid=(4, nb),
        in_specs=[sec(0), sec(4), sec(8), sec(12), tab, tab, par, par],
        out_specs=[pl.BlockSpec((tb, LANES), lambda j, i: (i, j)), pl.BlockSpec((1, 1, LANES, LANES), lambda j, i: (j, i, 0, 0))],
        out_shape=[jax.ShapeDtypeStruct((T, 4 * LANES), F32), jax.ShapeDtypeStruct((4, nb, LANES, LANES), F32)],
        scratch_shapes=[pltpu.VMEM((LANES, LANES), F32)],
        compiler_params=_params(("arbitrary", "arbitrary")),
    )(P, P, P, P, cos, sin, g_ret, lg)


def _ret_bwd(P, cos, sin, g_ret, lg, s0, dmix, T, tb):
    nb = T // tb

    def body(rq, rk, rv, rg, c, s, g, l, s0_ref, do, drq, drk, drv, drg, dg, dstate):
        i = pl.program_id(1)

        @pl.when(i == 0)
        def _():
            dstate[...] = jnp.zeros(dstate.shape, F32)
            dg[...] = jnp.zeros(dg.shape, F32)

        cc, ss, ll = c[...], s[...], l[...]
        _, vjp = jax.vjp(lambda a, b, v, gate, st, gg: _ret_fn(a, b, v, gate, cc, ss, st, gg, ll),
                         rq[...], rk[...], rv[...], rg[...], s0_ref[0, 0], g[...])
        ga, gb, gv, ggate, gst, ggain = vjp((do[...], dstate[...]))
        drq[...] = ga.astype(drq.dtype)
        drk[...] = gb.astype(drk.dtype)
        drv[...] = gv.astype(drv.dtype)
        drg[...] = ggate.astype(drg.dtype)
        dstate[...] = gst
        dg[...] += ggain

    rev = lambda i: nb - 1 - i
    sec = lambda off: pl.BlockSpec((tb, LANES), lambda j, i: (rev(i), off + j))
    tab = pl.BlockSpec((tb, LANES), lambda j, i: (rev(i), 0))
    par = pl.BlockSpec((1, LANES), lambda j, i: (0, j))
    outb = pl.BlockSpec((tb, LANES), lambda j, i: (rev(i), j))
    return pl.pallas_call(
        body, name="ret_bwd", grid=(4, nb),
        in_specs=[sec(0), sec(4), sec(8), sec(12), tab, tab, par, par,
                  pl.BlockSpec((1, 1, LANES, LANES), lambda j, i: (j, rev(i), 0, 0)), outb],
        out_specs=[outb, outb, outb, outb, par],
        out_shape=[jax.ShapeDtypeStruct((T, 4 * LANES), BF)] * 4 + [jax.ShapeDtypeStruct((1, 4 * LANES), F32)],
        scratch_shapes=[pltpu.VMEM((LANES, LANES), F32)],
        compiler_params=_params(("arbitrary", "arbitrary")),
    )(P, P, P, P, cos, sin, g_ret, lg, s0, dmix)


_FB = 128


def _tri(lower):
    r = lax.broadcasted_iota(jnp.int32, (_FB, _FB), 0)
    c = lax.broadcasted_iota(jnp.int32, (_FB, _FB), 1)
    return ((r >= c) if lower else (r <= c)).astype(F32)


def _fgate_fwd(ffp, bpad, T):
    def body(ff_ref, b_ref, fc_ref, fr_ref):
        lane = lax.broadcasted_iota(jnp.int32, (1, LANES), 1)
        tri = _tri(True)
        carry = jnp.zeros((1, LANES), F32)
        for blk in range(T // _FB):
            z = ff_ref[blk * _FB:(blk + 1) * _FB, :] + b_ref[...]
            lf = jnp.where(lane < 8, jax.nn.log_sigmoid(z), 0.0)
            f = jnp.dot(tri, lf, precision=lax.Precision.HIGHEST, preferred_element_type=F32) + carry
            carry = f[_FB - 1:_FB, :]
            fc_ref[blk * _FB:(blk + 1) * _FB, :] = f
            fr_ref[:, blk * _FB:(blk + 1) * _FB] = f.T[:8, :]

    return pl.pallas_call(
        body, name="fgate_fwd",
        out_shape=[jax.ShapeDtypeStruct((T, LANES), F32), jax.ShapeDtypeStruct((8, T), F32)],
        compiler_params=pltpu.CompilerParams(vmem_limit_bytes=VMEM_LIMIT),
    )(ffp, bpad)


def _fgate_bwd(ffp, bpad, dfr, T):
    def body(ff_ref, b_ref, dfr_ref, dff_ref, db_ref):
        lane = lax.broadcasted_iota(jnp.int32, (1, LANES), 1)
        tri = _tri(False)
        carry = jnp.zeros((1, LANES), F32)
        db = jnp.zeros((1, LANES), F32)
        for blk in reversed(range(T // _FB)):
            d8 = dfr_ref[:, blk * _FB:(blk + 1) * _FB]
            dcol = jnp.concatenate([d8, jnp.zeros((_FB - 8, _FB), F32)], axis=0).T
            dlf = jnp.dot(tri, dcol, precision=lax.Precision.HIGHEST, preferred_element_type=F32) + carry
            carry = dlf[0:1, :]
            z = ff_ref[blk * _FB:(blk + 1) * _FB, :] + b_ref[...]
            dz = jnp.where(lane < 8, dlf * jax.nn.sigmoid(-z), 0.0)
            dff_ref[blk * _FB:(blk + 1) * _FB, :] = dz.astype(dff_ref.dtype)
            db = db + jnp.sum(dz, axis=0, keepdims=True)
        db_ref[...] = db

    return pl.pallas_call(
        body, name="fgate_bwd",
        out_shape=[jax.ShapeDtypeStruct((T, LANES), BF), jax.ShapeDtypeStruct((1, LANES), F32)],
        compiler_params=pltpu.CompilerParams(vmem_limit_bytes=VMEM_LIMIT),
    )(ffp, bpad, dfr)


def _head_bias_col(fc, head):
    lane = lax.broadcasted_iota(jnp.int32, (1, LANES), 1)
    return jnp.sum(jnp.where(lane == head, fc, 0.0), axis=-1, keepdims=True)


def _head_bias_row(fr, head):
    sub = lax.broadcasted_iota(jnp.int32, (8, 1), 0)
    return jnp.sum(jnp.where(sub == head, fr, 0.0), axis=0, keepdims=True)


def _fox_fwd(qn, kn, vb, fc, fr, T, tq):
    nq = T // tq

    def body(q_ref, k_ref, v_ref, fc_ref, fr_ref, o_ref, c_ref):
        j, i = pl.program_id(0), pl.program_id(1)
        lane = lax.broadcasted_iota(jnp.int32, (1, LANES), 1)
        lo = lane < HEAD
        causal = lax.broadcasted_iota(jnp.int32, (tq, tq), 0) >= lax.broadcasted_iota(jnp.int32, (tq, tq), 1)
        q = q_ref[...]
        fcb = fc_ref[...]
        outs, cs = [], []
        for hh in range(2):
            hm = lo if hh == 0 else ~lo
            head = 2 * j + hh
            qh = jnp.where(hm, q, jnp.zeros_like(q))
            fq = _head_bias_col(fcb, head)

            def block(kb, carry, diag, qh=qh, fq=fq, head=head):
                m, l, acc = carry
                k0 = pl.multiple_of(kb * tq, tq)
                k = k_ref[pl.ds(k0, tq), :]
                v = v_ref[pl.ds(k0, tq), :]
                fk = _head_bias_row(fr_ref[:, pl.ds(k0, tq)], head)
                s = (lax.dot_general(qh, k, NT, preferred_element_type=F32) + fq) - fk
                if diag:
                    s = jnp.where(causal, s, NEG)
                m2 = jnp.maximum(m, jnp.max(s, axis=-1, keepdims=True))
                p = jnp.exp(s - m2)
                a = jnp.exp(m - m2)
                return m2, a * l + jnp.sum(p, axis=-1, keepdims=True), a * acc + jnp.dot(p.astype(BF), v, preferred_element_type=F32)

            init = (jnp.full((tq, 1), NEG, F32), jnp.zeros((tq, 1), F32), jnp.zeros((tq, LANES), F32))
            carry = lax.fori_loop(0, i, lambda kb, c: block(kb, c, False), init)
            m, l, acc = block(i, carry, True)
            outs.append(acc / l)
            cs.append(fq - (m + jnp.log(l)))
        o_ref[...] = jnp.where(lo, outs[0], outs[1])
        c_ref[0] = jnp.where(lane == 0, cs[0], jnp.where(lane == 1, cs[1], 0.0))

    full = lambda: pl.BlockSpec((T, LANES), lambda j, i: (0, j))
    return pl.pallas_call(
        body, name="fox_fwd", grid=(4, nq),
        in_specs=[pl.BlockSpec((tq, LANES), lambda j, i: (i, j)), full(), full(),
                  pl.BlockSpec((tq, LANES), lambda j, i: (i, 0)), pl.BlockSpec((8, T), lambda j, i: (0, 0))],
        out_specs=[pl.BlockSpec((tq, LANES), lambda j, i: (i, j)), pl.BlockSpec((1, tq, LANES), lambda j, i: (j, i, 0))],
        out_shape=[jax.ShapeDtypeStruct((T, 4 * LANES), F32), jax.ShapeDtypeStruct((4, T, LANES), F32)],
        compiler_params=_params(("parallel", "arbitrary")),
    )(qn, kn, vb, fc, fr)


def _fox_bwd_dq(qn, kn, vb, fr, cq, dmix, T, tq):
    nq = T // tq

    def body(q_ref, k_ref, v_ref, fr_ref, c_ref, do_ref, dq_ref, dl_ref, p_scr, dp_scr):
        j, i = pl.program_id(0), pl.program_id(1)
        lane = lax.broadcasted_iota(jnp.int32, (1, LANES), 1)
        lo = lane < HEAD
        causal = lax.broadcasted_iota(jnp.int32, (tq, tq), 0) >= lax.broadcasted_iota(jnp.int32, (tq, tq), 1)
        q, do, cb = q_ref[...], do_ref[...], c_ref[0]
        res, deltas = [], []
        for hh in range(2):
            hm = lo if hh == 0 else ~lo
            head = 2 * j + hh
            qh = jnp.where(hm, q, jnp.zeros_like(q))
            doh = jnp.where(hm, do, 0.0).astype(BF)
            c = cb[:, hh:hh + 1]

            def probs(kb, delta, diag, qh=qh, doh=doh, c=c, head=head):
                k0 = pl.multiple_of(kb * tq, tq)
                k = k_ref[pl.ds(k0, tq), :]
                v = v_ref[pl.ds(k0, tq), :]
                fk = _head_bias_row(fr_ref[:, pl.ds(k0, tq)], head)
                p = jnp.exp((lax.dot_general(qh, k, NT, preferred_element_type=F32) + c) - fk)
                if diag:
                    p = jnp.where(causal, p, 0.0)
                dp = lax.dot_general(doh, v, NT, preferred_element_type=F32)
                p_scr[:, pl.ds(k0, tq)] = p
                dp_scr[:, pl.ds(k0, tq)] = dp
                return delta + jnp.sum(p * dp, axis=-1, keepdims=True)

            delta = lax.fori_loop(0, i, lambda kb, d: probs(kb, d, False), jnp.zeros((tq, 1), F32))
            delta = probs(i, delta, True)

            def grad(kb, acc, delta=delta):
                k0 = pl.multiple_of(kb * tq, tq)
                ds = p_scr[:, pl.ds(k0, tq)] * (dp_scr[:, pl.ds(k0, tq)] - delta)
                return acc + jnp.dot(ds.astype(BF), k_ref[pl.ds(k0, tq), :], preferred_element_type=F32)

            res.append(lax.fori_loop(0, i + 1, grad, jnp.zeros((tq, LANES), F32)))
            deltas.append(delta)
        dq_ref[...] = jnp.where(lo, res[0], res[1])
        dl_ref[0] = jnp.where(lane == 0, deltas[0], jnp.where(lane == 1, deltas[1], 0.0))

    full = lambda: pl.BlockSpec((T, LANES), lambda j, i: (0, j))
    return pl.pallas_call(
        body, name="fox_bwd_dq", grid=(4, nq),
        in_specs=[pl.BlockSpec((tq, LANES), lambda j, i: (i, j)), full(), full(), pl.BlockSpec((8, T), lambda j, i: (0, 0)),
                  pl.BlockSpec((1, tq, LANES), lambda j, i: (j, i, 0)), pl.BlockSpec((tq, LANES), lambda j, i: (i, 4 + j))],
        out_specs=[pl.BlockSpec((tq, LANES), lambda j, i: (i, j)), pl.BlockSpec((1, tq, LANES), lambda j, i: (j, i, 0))],
        out_shape=[jax.ShapeDtypeStruct((T, 4 * LANES), F32), jax.ShapeDtypeStruct((4, T, LANES), F32)],
        scratch_shapes=[pltpu.VMEM((tq, T), F32), pltpu.VMEM((tq, T), F32)],
        compiler_params=_params(("parallel", "arbitrary")),
    )(qn, kn, vb, fr, cq, dmix)


def _fox_bwd_dkv(qn, kn, vb, fr, cq, dl, dmix, T, tq):
    nq = T // tq

    def body(q_ref, k_ref, v_ref, fr_ref, c_ref, dl_ref, do_ref, dk_ref, dv_ref, dfr_ref):
        j, kb = pl.program_id(0), pl.program_id(1)
        lo = _lo_mask()
        sub = lax.broadcasted_iota(jnp.int32, (8, 1), 0)
        causal = lax.broadcasted_iota(jnp.int32, (tq, tq), 0) >= lax.broadcasted_iota(jnp.int32, (tq, tq), 1)
        k, v, frb = k_ref[...], v_ref[...], fr_ref[...]
        dks, dvs, dfs = [], [], []
        for hh in range(2):
            hm = lo if hh == 0 else ~lo
            head = 2 * j + hh
            km = jnp.where(hm, k, jnp.zeros_like(k))
            vm = jnp.where(hm, v, jnp.zeros_like(v))
            fk = _head_bias_row(frb, head)

            def block(qi, carry, diag, km=km, vm=vm, fk=fk, hm=hm, hh=hh):
                dk, dv, df = carry
                q0 = pl.multiple_of(qi * tq, tq)
                q = q_ref[pl.ds(q0, tq), :]
                c = c_ref[0, pl.ds(q0, tq), :][:, hh:hh + 1]
                delta = dl_ref[0, pl.ds(q0, tq), :][:, hh:hh + 1]
                dob = do_ref[pl.ds(q0, tq), :].astype(BF)
                p = jnp.exp((lax.dot_general(q, km, NT, preferred_element_type=F32) + c) - fk)
                if diag:
                    p = jnp.where(causal, p, 0.0)
                dv = dv + lax.dot_general(p.astype(BF), dob, TN, preferred_element_type=F32)
                dp = lax.dot_general(dob, vm, NT, preferred_element_type=F32)
                ds = p * (dp - delta)
                dk = dk + lax.dot_general(ds.astype(BF), q, TN, preferred_element_type=F32)
                return dk, dv, df - jnp.sum(ds, axis=0, keepdims=True)

            init = (jnp.zeros((tq, LANES), F32), jnp.zeros((tq, LANES), F32), jnp.zeros((1, tq), F32))
            carry = block(kb, init, True)
            dk, dv, df = lax.fori_loop(kb + 1, nq, lambda qi, cr: block(qi, cr, False), carry)
            dks.append(dk)
            dvs.append(dv)
            dfs.append(df)
        dk_ref[...] = jnp.where(lo, dks[0], dks[1])
        dv_ref[...] = jnp.where(lo, dvs[0], dvs[1]).astype(dv_ref.dtype)
        dfr_ref[0] = jnp.where(sub == 0, dfs[0], jnp.where(sub == 1, dfs[1], 0.0))

    full = lambda off: pl.BlockSpec((T, LANES), lambda j, kb: (0, off + j))
    blk = lambda: pl.BlockSpec((tq, LANES), lambda j, kb: (kb, j))
    return pl.pallas_call(
        body, name="fox_bwd_dkv", grid=(4, nq),
        in_specs=[full(0), blk(), blk(), pl.BlockSpec((8, tq), lambda j, kb: (0, kb)),
                  pl.BlockSpec((1, T, LANES), lambda j, kb: (j, 0, 0)), pl.BlockSpec((1, T, LANES), lambda j, kb: (j, 0, 0)), full(4)],
        out_specs=[blk(), blk(), pl.BlockSpec((1, 8, tq), lambda j, kb: (j, 0, kb))],
        out_shape=[jax.ShapeDtypeStruct((T, 4 * LANES), F32), jax.ShapeDtypeStruct((T, 4 * LANES), BF),
                   jax.ShapeDtypeStruct((4, 8, T), F32)],
        compiler_params=_params(("parallel", "arbitrary")),
    )(qn, kn, vb, fr, cq, dl, dmix)


_BIAS_LANE = HEAD


def _split3(f):
    hi = f.astype(BF).astype(F32)
    mid = (f - hi).astype(BF).astype(F32)
    lo = ((f - hi) - mid).astype(BF).astype(F32)
    return hi, mid, lo


def _fox_operands(P, fc, g_fq2, g_fk2, T, tm):
    def body(fq_ref, fk_ref, fv_ref, fc_ref, gq_ref, gk_ref, qa_ref, qat_ref, ka_ref, kat_ref, va_ref, vat_ref):
        j = pl.program_id(0)
        lane = lax.broadcasted_iota(jnp.int32, (1, LANES), 1)
        qn, kn = _fox_prep_fn(fq_ref[...], fk_ref[...], gq_ref[...], gk_ref[...])
        v = fv_ref[...]
        fcb = fc_ref[...]
        b = _BIAS_LANE
        for hh in range(2):
            hi, mid, lo = _split3(_head_bias_col(fcb, 2 * j + hh))
            take = (lambda a: a) if hh == 0 else (lambda a: pltpu.roll(a, HEAD, 1))
            qa = jnp.where(lane < HEAD, take(qn), jnp.where(lane == b, hi, jnp.where(lane == b + 1, mid, jnp.where(
                lane == b + 2, lo, jnp.where(lane < b + 6, 1.0, 0.0)))))
            ka = jnp.where(lane < HEAD, take(kn), jnp.where(lane < b + 3, 1.0, jnp.where(lane == b + 3, -hi, jnp.where(
                lane == b + 4, -mid, jnp.where(lane == b + 5, -lo, 0.0)))))
            va = jnp.where(lane < HEAD, take(v), 0.0)
            for val, ref, tref in ((qa, qa_ref, qat_ref), (ka, ka_ref, kat_ref), (va, va_ref, vat_ref)):
                ref[hh] = val.astype(BF)
                tref[hh] = val.T.astype(BF)

    sec = lambda off: pl.BlockSpec((tm, LANES), lambda j, i: (i, off + j))
    par = pl.BlockSpec((1, LANES), lambda j, i: (0, 0))
    nat = pl.BlockSpec((2, tm, LANES), lambda j, i: (j, i, 0))
    trn = pl.BlockSpec((2, LANES, tm), lambda j, i: (j, 0, i))
    return pl.pallas_call(
        body, name="fox_operands", grid=(4, T // tm),
        in_specs=[sec(16), sec(20), sec(24), pl.BlockSpec((tm, LANES), lambda j, i: (i, 0)), par, par],
        out_specs=[nat, trn, nat, trn, nat, trn],
        out_shape=[jax.ShapeDtypeStruct((8, T, LANES), BF), jax.ShapeDtypeStruct((8, LANES, T), BF)] * 3,
        compiler_params=_params(("parallel", "arbitrary")),
    )(P, P, P, fc, g_fq2, g_fk2)


def _fox_forward(qat, ka, vat, T, tq, tk):
    nq, per = T // tq, tq // tk

    def body(qat_ref, ka_ref, vat_ref, o_ref, lse_ref):
        i = pl.program_id(1)
        sub = lax.broadcasted_iota(jnp.int32, (8, 1), 0)
        krow = lax.broadcasted_iota(jnp.int32, (tk, tq), 0)
        qcol = lax.broadcasted_iota(jnp.int32, (tk, tq), 1)

        def scores(kb):
            k0 = pl.multiple_of(kb * tk, tk)
            return tuple(jnp.dot(ka_ref[hh, pl.ds(k0, tk), :], qat_ref[hh], preferred_element_type=F32) for hh in range(2))

        def step(kb, carry, mask, last=False):
            stats, s_now = carry
            s_next = s_now if last else scores(kb + 1)
            k0 = pl.multiple_of(kb * tk, tk)
            new = []
            for hh in range(2):
                m, l, acc = stats[hh]
                s = s_now[hh] if mask is None else jnp.where(mask, s_now[hh], NEG)
                m2 = jnp.maximum(m, jnp.max(s, axis=0, keepdims=True))
                p = jnp.exp(s - m2)
                a = jnp.exp(m - m2)
                pv = jnp.dot(vat_ref[hh, 0:HEAD, pl.ds(k0, tk)], p.astype(BF), preferred_element_type=F32)
                new.append((m2, a * l + jnp.sum(p, axis=0, keepdims=True), a * acc + pv))
            return tuple(new), s_next

        one = (jnp.full((1, tq), NEG, F32), jnp.zeros((1, tq), F32), jnp.zeros((HEAD, tq), F32))
        carry = lax.fori_loop(0, i * per, lambda kb, c: step(kb, c, None), ((one, one), scores(0)))
        for d in range(per):
            carry = step(i * per + d, carry, krow + d * tk <= qcol, last=(d == per - 1))
        stats = carry[0]
        o_ref[...] = jnp.concatenate([acc / l for _, l, acc in stats], axis=0).T
        lses = [m + jnp.log(l) for m, l, _ in stats]
        lse_ref[0] = jnp.where(sub == 0, lses[0], jnp.where(sub == 1, lses[1], 0.0))

    return pl.pallas_call(
        body, name="fox_forward", grid=(4, nq),
        in_specs=[pl.BlockSpec((2, LANES, tq), lambda j, i: (j, 0, i)), pl.BlockSpec((2, T, LANES), lambda j, i: (j, 0, 0)),
                  pl.BlockSpec((2, LANES, T), lambda j, i: (j, 0, 0))],
        out_specs=[pl.BlockSpec((tq, LANES), lambda j, i: (i, j)), pl.BlockSpec((1, 8, tq), lambda j, i: (j, 0, i))],
        out_shape=[jax.ShapeDtypeStruct((T, 4 * LANES), F32), jax.ShapeDtypeStruct((4, 8, T), F32)],
        compiler_params=_params(("parallel", "arbitrary")),
    )(qat, ka, vat)


def _fox_cotangent(dmix, fox, T, tm):
    def body(do_ref, o_ref, doa_ref, doat_ref, dl_ref):
        lane = lax.broadcasted_iota(jnp.int32, (1, LANES), 1)
        sub = lax.broadcasted_iota(jnp.int32, (8, 1), 0)
        dob = do_ref[...].astype(BF).astype(F32)
        prod_t = (dob * o_ref[...]).T
        d0 = jnp.sum(prod_t[:HEAD], axis=0, keepdims=True)
        d1 = jnp.sum(prod_t[HEAD:], axis=0, keepdims=True)
        dl_ref[0] = jnp.where(sub == 0, d0, jnp.where(sub == 1, d1, 0.0))
        for hh in range(2):
            val = jnp.where(lane < HEAD, dob if hh == 0 else pltpu.roll(dob, HEAD, 1), 0.0)
            doa_ref[hh] = val.astype(BF)
            doat_ref[hh] = val.T.astype(BF)

    return pl.pallas_call(
        body, name="fox_cotangent", grid=(4, T // tm),
        in_specs=[pl.BlockSpec((tm, LANES), lambda j, i: (i, 4 + j)), pl.BlockSpec((tm, LANES), lambda j, i: (i, j))],
        out_specs=[pl.BlockSpec((2, tm, LANES), lambda j, i: (j, i, 0)), pl.BlockSpec((2, LANES, tm), lambda j, i: (j, 0, i)),
                   pl.BlockSpec((1, 8, tm), lambda j, i: (j, 0, i))],
        out_shape=[jax.ShapeDtypeStruct((8, T, LANES), BF), jax.ShapeDtypeStruct((8, LANES, T), BF),
                   jax.ShapeDtypeStruct((4, 8, T), F32)],
        compiler_params=_params(("parallel", "arbitrary")),
    )(dmix, fox)


def _fox_backward(qa, qat, ka, kat, va, doa, doat, lse, dl, T, tq):
    nq = T // tq

    def body(qa_ref, qat_ref, ka_ref, kat_ref, va_ref, doa_ref, doat_ref, lse_ref, dl_ref,
             dq_ref, dk_ref, dv_ref, df_ref, dr_ref, dqt, dk_acc, dv_acc, df_acc):
        j, kb = pl.program_id(0), pl.program_id(1)
        lane = lax.broadcasted_iota(jnp.int32, (1, LANES), 1)
        mask = lax.broadcasted_iota(jnp.int32, (tq, tq), 0) <= lax.broadcasted_iota(jnp.int32, (tq, tq), 1)

        @pl.when(kb == 0)
        def _():
            dqt[...] = jnp.zeros(dqt.shape, F32)

        dk_acc[...] = jnp.zeros(dk_acc.shape, F32)
        dv_acc[...] = jnp.zeros(dv_acc.shape, F32)
        df_acc[...] = jnp.zeros(df_acc.shape, F32)

        def step(qi, diag):
            q0 = pl.multiple_of(qi * tq, tq)
            for hh in range(2):
                s = jnp.dot(ka_ref[hh], qat_ref[hh, :, pl.ds(q0, tq)], preferred_element_type=F32)
                p = jnp.exp(s - lse_ref[0, hh:hh + 1, pl.ds(q0, tq)])
                if diag:
                    p = jnp.where(mask, p, 0.0)
                dp = jnp.dot(va_ref[hh], doat_ref[hh, :, pl.ds(q0, tq)], preferred_element_type=F32)
                ds = p * (dp - dl_ref[0, hh:hh + 1, pl.ds(q0, tq)])
                pb, dsb = p.astype(BF), ds.astype(BF)
                dv_acc[hh] += jnp.dot(pb, doa_ref[hh, pl.ds(q0, tq), :], preferred_element_type=F32)
                dk_acc[hh] += jnp.dot(dsb, qa_ref[hh, pl.ds(q0, tq), :], preferred_element_type=F32)
                dqt[hh, 0:HEAD, pl.ds(q0, tq)] += jnp.dot(kat_ref[hh, 0:HEAD, :], dsb, preferred_element_type=F32)
                dqt[hh, HEAD:HEAD + 8, pl.ds(q0, tq)] += jnp.broadcast_to(jnp.sum(ds, axis=0, keepdims=True), (8, tq))
                part = ds[:, 0:LANES]
                for c in range(1, tq // LANES):
                    part = part + ds[:, c * LANES:(c + 1) * LANES]
                df_acc[hh] += part

        step(kb, True)

        @pl.loop(kb + 1, nq)
        def _(qi):
            step(qi, False)

        lo = lane < HEAD
        dk_ref[...] = jnp.where(lo, dk_acc[0], pltpu.roll(dk_acc[1], HEAD, 1))
        dv_ref[...] = jnp.where(lo, dv_acc[0], pltpu.roll(dv_acc[1], HEAD, 1)).astype(dv_ref.dtype)
        f0 = -jnp.sum(df_acc[0], axis=1, keepdims=True)
        f1 = -jnp.sum(df_acc[1], axis=1, keepdims=True)
        df_ref[0] = jnp.where(lane == 2 * j, f0, jnp.where(lane == 2 * j + 1, f1, 0.0))

        @pl.when(kb == nq - 1)
        def _():
            for t in range(nq):
                cols = slice(t * tq, (t + 1) * tq)
                dq_ref[cols, :] = jnp.concatenate([dqt[0, 0:HEAD, cols], dqt[1, 0:HEAD, cols]], axis=0).T
                rsum = jnp.concatenate([dqt[0, HEAD:HEAD + 8, cols], dqt[1, HEAD:HEAD + 8, cols],
                                        jnp.zeros((LANES - 16, tq), F32)], axis=0).T
                dr_ref[0, cols, :] = jnp.where(lane == 2 * j, rsum[:, 0:1], jnp.where(lane == 2 * j + 1, rsum[:, 8:9], 0.0))

    nat_full = pl.BlockSpec((2, T, LANES), lambda j, kb: (j, 0, 0))
    trn_full = pl.BlockSpec((2, LANES, T), lambda j, kb: (j, 0, 0))
    nat_blk = pl.BlockSpec((2, tq, LANES), lambda j, kb: (j, kb, 0))
    trn_blk = pl.BlockSpec((2, LANES, tq), lambda j, kb: (j, 0, kb))
    rows = pl.BlockSpec((1, 8, T), lambda j, kb: (j, 0, 0))
    blk = pl.BlockSpec((tq, LANES), lambda j, kb: (kb, j))
    return pl.pallas_call(
        body, name="fox_backward", grid=(4, nq),
        in_specs=[nat_full, trn_full, nat_blk, trn_blk, nat_blk, nat_full, trn_full, rows, rows],
        out_specs=[pl.BlockSpec((T, LANES), lambda j, kb: (0, j)), blk, blk, pl.BlockSpec((1, tq, LANES), lambda j, kb: (j, kb, 0)),
                   pl.BlockSpec((1, T, LANES), lambda j, kb: (j, 0, 0))],
        out_shape=[jax.ShapeDtypeStruct((T, 4 * LANES), F32), jax.ShapeDtypeStruct((T, 4 * LANES), F32),
                   jax.ShapeDtypeStruct((T, 4 * LANES), BF), jax.ShapeDtypeStruct((4, T, LANES), F32),
                   jax.ShapeDtypeStruct((4, T, LANES), F32)],
        scratch_shapes=[pltpu.VMEM((2, HEAD + 8, T), F32), pltpu.VMEM((2, tq, LANES), F32), pltpu.VMEM((2, tq, LANES), F32),
                        pltpu.VMEM((2, tq, LANES), F32)],
        compiler_params=_params(("arbitrary", "arbitrary")),
    )(qa, qat, ka, kat, va, doa, doat, lse, dl)


def _fgate_bwd_col(ffp, bpad, dfc, T):
    def body(ff_ref, b_ref, dfc_ref, dff_ref, db_ref):
        lane = lax.broadcasted_iota(jnp.int32, (1, LANES), 1)
        tri = _tri(False)
        carry = jnp.zeros((1, LANES), F32)
        db = jnp.zeros((1, LANES), F32)
        for blk in reversed(range(T // _FB)):
            dlf = jnp.dot(tri, dfc_ref[blk * _FB:(blk + 1) * _FB, :], precision=lax.Precision.HIGHEST,
                          preferred_element_type=F32) + carry
            carry = dlf[0:1, :]
            z = ff_ref[blk * _FB:(blk + 1) * _FB, :] + b_ref[...]
            dz = jnp.where(lane < 8, dlf * jax.nn.sigmoid(-z), 0.0)
            dff_ref[blk * _FB:(blk + 1) * _FB, :] = dz.astype(dff_ref.dtype)
            db = db + jnp.sum(dz, axis=0, keepdims=True)
        db_ref[...] = db

    return pl.pallas_call(
        body, name="fgate_bwd",
        out_shape=[jax.ShapeDtypeStruct((T, LANES), BF), jax.ShapeDtypeStruct((1, LANES), F32)],
        compiler_params=pltpu.CompilerParams(vmem_limit_bytes=VMEM_LIMIT),
    )(ffp, bpad, dfc)


MESH = pl.DeviceIdType.MESH


def _place():
    return lax.axis_index("x"), lax.axis_index("y"), lax.axis_index("c")


def _all_gather(shard):
    R, W = shard.shape

    def body(x_ref, out_ref, send_sems, recv_sems, local_sem):
        x, y, c = _place()
        me, sibling = (x, y, c), (x, y, 1 - c)
        chips = [(1 - x, y), (x, 1 - y), (1 - x, 1 - y)]

        def slot(px, py, pc):
            return out_ref.at[4 * px + 2 * py + pc]

        def copy(k, block, to, src=None):
            return pltpu.make_async_remote_copy(
                src_ref=slot(*block) if src is None else src, dst_ref=slot(*block),
                send_sem=send_sems.at[k], recv_sem=recv_sems.at[k], device_id=to, device_id_type=MESH)

        mine = pltpu.make_async_copy(x_ref, slot(*me), local_sem)
        mine.start()
        first = [copy(0, me, sibling, src=x_ref)]
        first += [copy(1 + n, me, (*chip, c), src=x_ref) for n, chip in enumerate(chips)]
        for cp in first:
            cp.start()
        passed = [copy(4 + n, (*chip, c), sibling) for n, chip in enumerate(chips)]
        for n, chip in enumerate(chips):
            copy(1 + n, (*chip, c), me).wait_recv()
            passed[n].start()
        copy(0, sibling, me).wait_recv()
        for n, chip in enumerate(chips):
            copy(4 + n, (*chip, 1 - c), me).wait_recv()
        for cp in first + passed:
            cp.wait_send()
        mine.wait()

    return pl.pallas_call(
        body, name="all_gather_weights",
        out_shape=jax.ShapeDtypeStruct((N_DEV, R, W), shard.dtype),
        in_specs=[pl.BlockSpec(memory_space=pl.ANY)], out_specs=pl.BlockSpec(memory_space=pl.ANY),
        scratch_shapes=[pltpu.SemaphoreType.DMA((7,)), pltpu.SemaphoreType.DMA((7,)), pltpu.SemaphoreType.DMA],
    )(shard)


def _all_to_all(big, small):
    def body(big_ref, small_ref, rbig_ref, rsmall_ref, send_sems, recv_sems, local_sems):
        x, y, c = _place()
        me = 4 * x + 2 * y + c
        l0 = pltpu.make_async_copy(big_ref.at[me], rbig_ref.at[me], local_sems.at[0])
        l1 = pltpu.make_async_copy(small_ref, rsmall_ref.at[me], local_sems.at[1])
        l0.start()
        l1.start()
        copies = []
        for r in range(1, N_DEV):
            px, py, pc = x ^ (r >> 2), y ^ ((r >> 1) & 1), c ^ (r & 1)
            peer = 4 * px + 2 * py + pc
            copies.append(pltpu.make_async_remote_copy(
                src_ref=big_ref.at[peer], dst_ref=rbig_ref.at[me], send_sem=send_sems.at[2 * r], recv_sem=recv_sems.at[2 * r],
                device_id=(px, py, pc), device_id_type=MESH))
            copies.append(pltpu.make_async_remote_copy(
                src_ref=small_ref, dst_ref=rsmall_ref.at[me], send_sem=send_sems.at[2 * r + 1], recv_sem=recv_sems.at[2 * r + 1],
                device_id=(px, py, pc), device_id_type=MESH))
        for cp in copies:
            cp.start()
        for cp in copies:
            cp.wait_recv()
        for cp in copies:
            cp.wait_send()
        l0.wait()
        l1.wait()

    return pl.pallas_call(
        body, name="all_to_all_grads",
        out_shape=[jax.ShapeDtypeStruct(big.shape, big.dtype), jax.ShapeDtypeStruct((N_DEV,) + small.shape, small.dtype)],
        in_specs=[pl.BlockSpec(memory_space=pl.ANY)] * 2, out_specs=[pl.BlockSpec(memory_space=pl.ANY)] * 2,
        scratch_shapes=[pltpu.SemaphoreType.DMA((2 * N_DEV,)), pltpu.SemaphoreType.DMA((2 * N_DEV,)), pltpu.SemaphoreType.DMA((2,))],
    )(big, small)


def _exchange_copies(src_ref, land_ref, send_sems, recv_sems, scatter):
    x, y, c = _place()
    me = 4 * x + 2 * y + c
    copies = []
    for r in range(1, N_DEV):
        px, py, pc = x ^ (r >> 2), y ^ ((r >> 1) & 1), c ^ (r & 1)
        copies.append(pltpu.make_async_remote_copy(
            src_ref=src_ref.at[4 * px + 2 * py + pc] if scatter else src_ref, dst_ref=land_ref.at[me],
            send_sem=send_sems.at[r - 1], recv_sem=recv_sems.at[r - 1], device_id=(px, py, pc), device_id_type=MESH))
    return copies


_HBM = pl.BlockSpec(memory_space=pltpu.HBM)
_SEM = pl.BlockSpec(memory_space=pltpu.SEMAPHORE)
_EFFECT = pltpu.SideEffectType.DATAFLOW_SIDE_EFFECTING


def _exchange_start(name, src, land, scatter):
    def body(src_ref, land_ref, send_sems, recv_sems, src_thru, land_thru, token):
        for cp in _exchange_copies(src_ref, land_ref, send_sems, recv_sems, scatter):
            cp.start()
        token[...] = jnp.zeros(token.shape, F32)

    return pl.pallas_call(
        body, name=name,
        out_shape=(pltpu.SemaphoreType.DMA((N_DEV - 1,)), pltpu.SemaphoreType.DMA((N_DEV - 1,)),
                   pltpu.HBM(src.shape, src.dtype), pltpu.HBM(land.shape, land.dtype), jax.ShapeDtypeStruct((8, LANES), F32)),
        in_specs=(_HBM, _HBM), out_specs=(_SEM, _SEM, _HBM, _HBM, pl.BlockSpec(memory_space=pltpu.VMEM)),
        input_output_aliases={0: 2, 1: 3},
        compiler_params=pltpu.CompilerParams(has_side_effects=_EFFECT),
    )(pltpu.with_memory_space_constraint(src, pltpu.HBM), pltpu.with_memory_space_constraint(land, pltpu.HBM))


def _exchange_wait(name, started, after, scatter):
    send_sems, recv_sems, src_thru, land_thru, _ = started

    def body(src_ref, land_ref, send_sems, recv_sems, after_ref, src_dead, got_ref):
        copies = _exchange_copies(src_ref, land_ref, send_sems, recv_sems, scatter)
        for cp in copies:
            cp.wait_send()
        for cp in copies:
            cp.wait_recv()

    return pl.pallas_call(
        body, name=name,
        out_shape=(pltpu.HBM(src_thru.shape, src_thru.dtype), pltpu.HBM(land_thru.shape, land_thru.dtype)),
        in_specs=(_HBM, _HBM, _SEM, _SEM, pl.BlockSpec(memory_space=pl.ANY)), out_specs=(_HBM, _HBM),
        input_output_aliases={0: 0, 1: 1},
        compiler_params=pltpu.CompilerParams(has_side_effects=_EFFECT),
    )(src_thru, land_thru, send_sems, recv_sems, after)


def _adamw(name, slots, w, m, v, tr, own=None):
    R, W = w.shape

    def body(s_ref, *refs):
        if own is not None:
            own_ref, refs = refs[0], refs[1:]
        w_ref, m_ref, v_ref, g_ref, d_ref, nm_ref, nv_ref = refs
        g = s_ref[0].astype(F32)
        for s in range(1, N_DEV):
            g = g + s_ref[s].astype(F32)
        if own is not None:
            g = g + own_ref[...].astype(F32)
        m2 = ADAM_B1 * m_ref[...] + (1.0 - ADAM_B1) * g
        v2 = ADAM_B2 * v_ref[...] + (1.0 - ADAM_B2) * jnp.square(g)
        m_hat = m2 / (1.0 - ADAM_B1 ** ADAM_STEP)
        v_hat = v2 / (1.0 - ADAM_B2 ** ADAM_STEP)
        g_ref[...] = g
        d_ref[...] = -ADAM_LR * (m_hat / (jnp.sqrt(v_hat) + ADAM_EPS) + ADAM_WD * w_ref[...])
        nm_ref[...] = m2
        nv_ref[...] = v2

    row = lambda: pl.BlockSpec((tr, W), lambda i: (i, 0))
    return pl.pallas_call(
        body, name=name, grid=(R // tr,),
        in_specs=[pl.BlockSpec((N_DEV, tr, W), lambda i: (0, i, 0))] + [row() for _ in range(3 + (own is not None))],
        out_specs=[row(), row(), row(), row()],
        out_shape=[jax.ShapeDtypeStruct((R, W), F32)] * 4,
        compiler_params=_params(("parallel",)),
    )(slots, *([own] if own is not None else []), w, m, v)


def _tables(T):
    pos = jnp.arange(T, dtype=F32)
    inv_freq = 10000.0 ** (-jnp.arange(0, HEAD, 2, dtype=F32) / HEAD)
    ang = pos[:, None] * inv_freq[None, :]
    cos, sin = jnp.cos(ang), jnp.sin(ang)
    cos4 = jnp.tile(cos, (1, 4))
    sin4 = jnp.tile(jnp.concatenate([-sin, sin], axis=1), (1, 2))
    log_g = jnp.log(1.0 - 2.0 ** (-5.0 - jnp.arange(8, dtype=F32)))
    return cos4, sin4, jnp.repeat(log_g, HEAD)[None, :]


def _local_step(x, mem, target, sp, w_inT, token, fetch_rest, push):
    T = x.shape[0]
    tm = min(512, T)
    tq = min(256, T)
    tb = min(1024, T)
    cos4, sin4, lg = _tables(T)
    g_fq2 = jnp.tile(sp["g_fox_q"], (1, 2))
    g_fk2 = jnp.tile(sp["g_fox_k"], (1, 2))
    g_ret = sp["g_ret_out"].reshape(1, 8 * HEAD)
    bpad = jnp.pad(sp["b_forget"], ((0, 0), (0, LANES - 8)))
    w_secs = [w_inT[k * 512:(k + 1) * 512] for k in range(7)]
    w_ffT = jnp.pad(w_inT[3584:3592], ((0, LANES - 8), (0, 0)))
    w_mainT = w_inT[:3584]
    tie = lambda p, tok: p + tok[0:1, 0:1]

    hn1, = _rw_fwd("rms_mix", _rms_fn, [(x, D, 0, False)], [(tie(sp["g_mix"], token), D, 0, False)], [(BF, D)], T, tm, 1)
    P, = _mm("proj_in", [[(hn1, w_mainT, "nt")]], [], _ident, T, 3584, tm, 512, [F32])
    ffp, = _mm("proj_ff", [[(hn1, w_ffT, "nt")]], [], _ident, T, LANES, tm, LANES, [F32])
    ret, s0 = _ret_fwd(P, cos4, sin4, g_ret, lg, T, tb)
    fc, _ = _fgate_fwd(ffp, bpad, T)
    qa, qat, ka, kat, va, vat = _fox_operands(P, fc, g_fq2, g_fk2, T, tm)
    fox, lse = _fox_forward(qat, ka, vat, T, tq, min(128, T))
    mix = jnp.concatenate([ret, fox], axis=1)
    W = fetch_rest(fox)
    h1, = _mm("proj_out", [[(mix, W["w_out"], "nn")]], [x], _add, T, D, tm, 512, [F32])

    hn2, = _rw_fwd("rms_xattn", _rms_fn, [(h1, D, 0, False)], [(sp["g_xattn"], D, 0, False)], [(BF, D)], T, tm, 1)
    qx, = _mm("proj_xq", [[(hn2, W["w_xq"], "nn")]], [], _ident, T, D, tm, 512, [F32])
    memn, = _rw_fwd("rms_mem", _rms_fn, [(mem, D, 0, False)], [(sp["g_mem"], D, 0, False)], [(BF, D)], N_MEM, N_MEM, 1)
    kv, = _mm("proj_xkv", [[(memn, W["w_xkvT"], "nt")]], [], _ident, N_MEM, 2 * D, N_MEM, 512, [F32])
    xa_rows = [(qx, XHEAD, 0, True)]
    xa_params = [(sp["g_xq"], XHEAD, 0, False), (sp["g_xk"], XHEAD, 0, False), (kv, XHEAD, 0, True), (kv, XHEAD, 4, True)]
    xo, = _rw_fwd("xattn_fwd", _xattn_fn, xa_rows, xa_params, [(BF, XHEAD)], T, tm, 4)
    h2, = _mm("proj_xo", [[(xo, W["w_xo"], "nn")]], [h1], _add, T, D, tm, 512, [F32])

    hn3, = _rw_fwd("rms_ffn", _rms_fn, [(h2, D, 0, False)], [(sp["g_ffn"], D, 0, False)], [(BF, D)], T, tm, 1)
    gate, up, act = _mm("ffn_in", [[(hn3, W["w_gateT"], "nt")], [(hn3, W["w_upT"], "nt")]], [], _swiglu_fwd_epi,
                        T, D_FF, tm, 256, [F32, F32, BF])
    h3, = _mm("ffn_out", [[(act, W["w_down"], "nn")]], [h2], _add, T, D, tm, 512, [F32])
    dy, loss_part = _rw_fwd("loss", _loss_fn, [(h3, D, 0, False), (target, D, 0, False)], [], [(F32, D)], T, tm, 1, n_acc=1)

    dgate, dup = _mm("ffn_out_bwd", [[(dy, W["w_down"], "nt")]], [gate, up], _swiglu_bwd_epi, T, D_FF, tm, 256, [BF, BF])
    dhn3, = _mm("ffn_in_bwd", [[(dgate, W["w_gateT"], "nn"), (dup, W["w_upT"], "nn")]], [], _ident, T, D, tm, 512, [F32])
    gW = {}
    gW["w_gateT"], = _mm("dw_gate", [[(dgate, hn3, "tn")]], [], _ident, D_FF, D, 256, 512, [F32])
    gW["w_upT"], = _mm("dw_up", [[(dup, hn3, "tn")]], [], _ident, D_FF, D, 256, 512, [F32])
    gW["w_down"], = _mm("dw_down", [[(act, dy, "tn")]], [], _ident, D_FF, D, 256, 512, [F32])
    tok = push("ffn", gW)
    gs = {}
    dh2, gs["g_ffn"] = _rw_bwd("rms_ffn_bwd", _rms_fn, [(h2, D, 0, False)], [(tie(sp["g_ffn"], tok), D, 0, False)],
                               [(dhn3, D, 0, False)], T, tm, 1, [F32], [True], resid=dy)

    dxo, = _mm("proj_xo_bwd", [[(dh2, W["w_xo"], "nt")]], [], _ident, T, D, tm, 512, [BF])
    gW["w_xo"], = _mm("dw_xo", [[(xo, dh2, "tn")]], [], _ident, D, D, 256, 512, [F32])
    dqx, gs["g_xq"], gs["g_xk"], dkv_k, dkv_v = _rw_bwd(
        "xattn_bwd", _xattn_fn, xa_rows, xa_params, [(dxo, XHEAD, 0, True)], T, tm, 4, [BF], [True, True, True, True])
    dkv = jnp.concatenate([dkv_k[:, :D], dkv_v[:, D:]], axis=1)
    dhn2, = _mm("proj_xq_bwd", [[(dqx, W["w_xq"], "nt")]], [], _ident, T, D, tm, 512, [F32])
    gW["w_xq"], = _mm("dw_xq", [[(hn2, dqx, "tn")]], [], _ident, D, D, 256, 512, [F32])
    dmemn, = _mm("proj_xkv_bwd", [[(dkv, W["w_xkvT"], "nn")]], [], _ident, N_MEM, D, N_MEM, 512, [F32])
    gW["w_xkvT"], = _mm("dw_xkv", [[(dkv, memn, "tn")]], [], _ident, 2 * D, D, 512, 512, [F32])
    tok = push("xattn", gW)
    gs["g_mem"], = _rw_bwd("rms_mem_bwd", _rms_fn, [(mem, D, 0, False)], [(sp["g_mem"], D, 0, False)], [(dmemn, D, 0, False)],
                           N_MEM, N_MEM, 1, [None], [True])
    dh1, gs["g_xattn"] = _rw_bwd("rms_xattn_bwd", _rms_fn, [(h1, D, 0, False)], [(tie(sp["g_xattn"], tok), D, 0, False)],
                                 [(dhn2, D, 0, False)], T, tm, 1, [F32], [True], resid=dh2)

    dmix, = _mm("proj_out_bwd", [[(dh1, W["w_out"], "nt")]], [], _ident, T, D, tm, 512, [F32])
    gW["w_out"], = _mm("dw_out", [[(mix, dh1, "tn")]], [], _ident, D, D, 256, 512, [F32])
    doa, doat, dl = _fox_cotangent(dmix, fox, T, tm)
    dqn, dkn, dfv, dfc4, drc4 = _fox_backward(qa, qat, ka, kat, va, doa, doat, lse, dl, T, tq)
    dfq, dfk, gq2, gk2 = _rw_bwd("fox_prep_bwd", _fox_prep_fn, [(P, LANES, 16, True), (P, LANES, 20, True)],
                                 [(g_fq2, LANES, 0, False), (g_fk2, LANES, 0, False)],
                                 [(dqn, LANES, 0, True), (dkn, LANES, 0, True)], T, tm, 4, [BF, BF], [True, True])
    gs["g_fox_q"] = gq2[:, :HEAD] + gq2[:, HEAD:]
    gs["g_fox_k"] = gk2[:, :HEAD] + gk2[:, HEAD:]
    dff, dbp = _fgate_bwd_col(ffp, bpad, jnp.sum(dfc4 + drc4, axis=0), T)
    gs["b_forget"] = dbp[:, :8]
    drq, drk, drv, drg, dg_ret = _ret_bwd(P, cos4, sin4, g_ret, lg, s0, dmix, T, tb)
    gs["g_ret_out"] = dg_ret
    dsecs = [drq, drk, drv, drg, dfq, dfk, dfv]
    dhn1, = _mm("proj_in_bwd", [[(d, w, "nn") for d, w in zip(dsecs, w_secs)] + [(dff, w_ffT, "nn")]], [], _ident,
                T, D, tm, 512, [F32])
    g_secs = [_mm("dw_in_%d" % k, [[(d, hn1, "tn")]], [], _ident, 512, D, 256, 512, [F32])[0] for k, d in enumerate(dsecs)]
    g_ff, = _mm("dw_in_ff", [[(dff, hn1, "tn")]], [], _ident, LANES, D, LANES, 512, [F32])
    gW["w_inT"] = jnp.concatenate(g_secs + [g_ff[:8]], axis=0)
    grad_x, gs["g_mix"] = _rw_bwd("rms_mix_bwd", _rms_fn, [(x, D, 0, False)], [(sp["g_mix"], D, 0, False)], [(dhn1, D, 0, False)],
                                  T, tm, 1, [F32], [True], resid=dh1)
    return loss_part, grad_x, gW, gs


_CANON = {"w_in": "w_inT", "w_xkv": "w_xkvT", "w_gate": "w_gateT", "w_up": "w_upT"}
_SMALL = (("g_mix", 0, 0, 1024), ("g_xattn", 1, 0, 1024), ("g_mem", 2, 0, 1024), ("g_ffn", 3, 0, 1024),
          ("g_ret_out", 4, 0, 512), ("g_xq", 4, 512, 256), ("g_xk", 4, 768, 256),
          ("g_fox_q", 5, 0, 64), ("g_fox_k", 5, 64, 64), ("b_forget", 5, 128, 8))
_LOSS_AT = (5, 256)


def _pack_shards(tree, dtype):
    parts = []
    for name, rows, padded, transposed in W_LAYOUT:
        a = tree[name][0]
        a = a.T if transposed else a
        parts.append(jnp.pad(a, ((0, padded - rows), (0, 0))).astype(dtype))
    return jnp.concatenate(parts, axis=0)


def _unpack_shards(packed, like):
    out = {}
    for name, rows, padded, transposed in W_LAYOUT:
        a = packed[W_OFF[name]:W_OFF[name] + rows]
        out[name] = (a.T if transposed else a)[None].reshape(like[name].shape)
    return out


def _pack_small(tree):
    rows = [jnp.zeros((1, D), F32) for _ in range(SMALL_ROWS)]
    buf = jnp.concatenate(rows, axis=0)
    for name, r, c, n in _SMALL:
        buf = lax.dynamic_update_slice(buf, tree[name].reshape(1, n).astype(F32), (r, c))
    return buf


def _unpack_small(buf, like):
    return {name: buf[r:r + 1, c:c + n].reshape(like[name].shape) for name, r, c, n in _SMALL}


def kernel(x, mem, g_mix, w_in, b_forget, g_ret_out, g_fox_q, g_fox_k, w_out, g_xattn, w_xq, w_xkv, g_mem, g_xq, g_xk, w_xo, g_ffn, w_gate, w_up, w_down, loss_target, m_g_mix, m_w_in, m_b_forget, m_g_ret_out, m_g_fox_q, m_g_fox_k, m_w_out, m_g_xattn, m_w_xq, m_w_xkv, m_g_mem, m_g_xq, m_g_xk, m_w_xo, m_g_ffn, m_w_gate, m_w_up, m_w_down, v_g_mix, v_w_in, v_b_forget, v_g_ret_out, v_g_fox_q, v_g_fox_k, v_w_out, v_g_xattn, v_w_xq, v_w_xkv, v_g_mem, v_g_xq, v_g_xk, v_w_xo, v_g_ffn, v_w_gate, v_w_up, v_w_down):
    names = ("g_mix", "w_in", "b_forget", "g_ret_out", "g_fox_q", "g_fox_k", "w_out", "g_xattn", "w_xq", "w_xkv", "g_mem",
             "g_xq", "g_xk", "w_xo", "g_ffn", "w_gate", "w_up", "w_down")
    w = dict(zip(names, (g_mix, w_in, b_forget, g_ret_out, g_fox_q, g_fox_k, w_out, g_xattn, w_xq, w_xkv, g_mem, g_xq, g_xk,
                         w_xo, g_ffn, w_gate, w_up, w_down)))
    m = dict(zip(names, (m_g_mix, m_w_in, m_b_forget, m_g_ret_out, m_g_fox_q, m_g_fox_k, m_w_out, m_g_xattn, m_w_xq, m_w_xkv,
                         m_g_mem, m_g_xq, m_g_xk, m_w_xo, m_g_ffn, m_w_gate, m_w_up, m_w_down)))
    v = dict(zip(names, (v_g_mix, v_w_in, v_b_forget, v_g_ret_out, v_g_fox_q, v_g_fox_k, v_w_out, v_g_xattn, v_w_xq, v_w_xkv,
                         v_g_mem, v_g_xq, v_g_xk, v_w_xo, v_g_ffn, v_w_gate, v_w_up, v_w_down)))
    small_names = [s[0] for s in _SMALL]

    gathered = _all_gather(_pack_shards(w, BF))
    W = {}
    for name, rows, padded, transposed in W_LAYOUT:
        full = gathered[:, W_OFF[name]:W_OFF[name] + rows].reshape(N_DEV * rows, D)
        W[_CANON.get(name, name)] = full

    sp = {n: w[n].reshape(1, -1) for n in small_names}
    loss_part, grad_x, gW, gs = _local_step(x[0], mem[0], loss_target[0], sp, W)

    chunks = []
    for name, rows, padded, transposed in W_LAYOUT:
        g = gW[_CANON.get(name, name)].reshape(N_DEV, rows, D)
        chunks.append(jnp.pad(g, ((0, 0), (0, padded - rows), (0, 0))).astype(BF))
    send = jnp.concatenate(chunks, axis=1)
    small = _pack_small(gs)
    small = lax.dynamic_update_slice(small, loss_part[:, :1], _LOSS_AT)
    recv, recv_small = _all_to_all(send, small)

    g_big, d_big, m_big, v_big = _adamw("adamw_shards", recv, _pack_shards(w, F32), _pack_shards(m, F32), _pack_shards(v, F32), 240)
    g_sm, d_sm, m_sm, v_sm = _adamw("adamw_small", recv_small, _pack_small(w), _pack_small(m), _pack_small(v), SMALL_ROWS)
    loss = g_sm[_LOSS_AT[0], _LOSS_AT[1]]

    outs = []
    for big, sm in ((g_big, g_sm), (d_big, d_sm), (m_big, m_sm), (v_big, v_sm)):
        tree = {**_unpack_shards(big, w), **_unpack_small(sm, w)}
        outs += [tree[n] for n in names]
    return (loss, grad_x[None], *outs)


def _pack_shards(tree, names, dtype):
    parts = []
    for name in names:
        rows, padded, transposed = W_SHARD[name]
        a = tree[name][0]
        a = a.T if transposed else a
        parts.append(jnp.pad(a, ((0, padded - rows), (0, 0))).astype(dtype))
    return jnp.concatenate(parts, axis=0)


def _unpack_shards(packed, names, like):
    out, off = {}, 0
    for name in names:
        rows, padded, transposed = W_SHARD[name]
        a = packed[off:off + rows]
        out[name] = (a.T if transposed else a)[None].reshape(like[name].shape)
        off += padded
    return out


def _unpack_gathered(gathered, names):
    out, off = {}, 0
    for name in names:
        rows, padded, _ = W_SHARD[name]
        out[_CANON.get(name, name)] = gathered[:, off:off + rows].reshape(N_DEV * rows, D)
        off += padded
    return out


def _pack_chunks(grads, names):
    chunks = []
    for name in names:
        rows, padded, _ = W_SHARD[name]
        g = grads[_CANON.get(name, name)].reshape(N_DEV, rows, D)
        chunks.append(jnp.pad(g, ((0, 0), (0, padded - rows), (0, 0))).astype(BF))
    return jnp.concatenate(chunks, axis=1)


def kernel(x, mem, g_mix, w_in, b_forget, g_ret_out, g_fox_q, g_fox_k, w_out, g_xattn, w_xq, w_xkv, g_mem, g_xq, g_xk, w_xo, g_ffn, w_gate, w_up, w_down, loss_target, m_g_mix, m_w_in, m_b_forget, m_g_ret_out, m_g_fox_q, m_g_fox_k, m_w_out, m_g_xattn, m_w_xq, m_w_xkv, m_g_mem, m_g_xq, m_g_xk, m_w_xo, m_g_ffn, m_w_gate, m_w_up, m_w_down, v_g_mix, v_w_in, v_b_forget, v_g_ret_out, v_g_fox_q, v_g_fox_k, v_w_out, v_g_xattn, v_w_xq, v_w_xkv, v_g_mem, v_g_xq, v_g_xk, v_w_xo, v_g_ffn, v_w_gate, v_w_up, v_w_down):
    names = ("g_mix", "w_in", "b_forget", "g_ret_out", "g_fox_q", "g_fox_k", "w_out", "g_xattn", "w_xq", "w_xkv", "g_mem",
             "g_xq", "g_xk", "w_xo", "g_ffn", "w_gate", "w_up", "w_down")
    w = dict(zip(names, (g_mix, w_in, b_forget, g_ret_out, g_fox_q, g_fox_k, w_out, g_xattn, w_xq, w_xkv, g_mem, g_xq, g_xk,
                         w_xo, g_ffn, w_gate, w_up, w_down)))
    m = dict(zip(names, (m_g_mix, m_w_in, m_b_forget, m_g_ret_out, m_g_fox_q, m_g_fox_k, m_w_out, m_g_xattn, m_w_xq, m_w_xkv,
                         m_g_mem, m_g_xq, m_g_xk, m_w_xo, m_g_ffn, m_w_gate, m_w_up, m_w_down)))
    v = dict(zip(names, (v_g_mix, v_w_in, v_b_forget, v_g_ret_out, v_g_fox_q, v_g_fox_k, v_w_out, v_g_xattn, v_w_xq, v_w_xkv,
                         v_g_mem, v_g_xq, v_g_xk, v_w_xo, v_g_ffn, v_w_gate, v_w_up, v_w_down)))
    small_names = [s[0] for s in _SMALL]
    me = 4 * lax.axis_index("x") + 2 * lax.axis_index("y") + lax.axis_index("c")

    first = _all_gather(_pack_shards(w, GATHER_FIRST, BF))
    first, rest_shard = lax.optimization_barrier((first, _pack_shards(w, GATHER_REST, BF)))
    rest_started = _exchange_start("gather_rest_start", rest_shard,
                                   jnp.broadcast_to(rest_shard[None], (N_DEV,) + rest_shard.shape), scatter=False)

    def fetch_rest(after):
        return _unpack_gathered(_exchange_wait("gather_rest_wait", rest_started, after, scatter=False)[1], GATHER_REST)

    pushed = {}

    def push(group, grads):
        send = _pack_chunks(grads, GRAD_GROUPS[group])
        pushed[group] = _exchange_start("scatter_%s_start" % group, send, jnp.zeros(send.shape, BF), scatter=True)
        return pushed[group][4]

    sp = {n: w[n].reshape(1, -1) for n in small_names}
    loss_part, grad_x, g_last, gs = _local_step(x[0], mem[0], loss_target[0], sp, _unpack_gathered(first, GATHER_FIRST)["w_inT"],
                                                rest_started[4], fetch_rest, push)

    small = lax.dynamic_update_slice(_pack_small(gs), loss_part[:, :1], _LOSS_AT)
    recv_mix, recv_small = _all_to_all(_pack_chunks(g_last, GRAD_GROUPS["mix"]), small)

    results = {}
    for group in ("ffn", "xattn", "mix"):
        gnames = GRAD_GROUPS[group]
        wp, mp, vp = (_pack_shards(t, gnames, F32) for t in (w, m, v))
        if group == "mix":
            res = _adamw("adamw_mix", recv_mix, wp, mp, vp, 16)
        else:
            sent, recv = _exchange_wait("scatter_%s_wait" % group, pushed[group], recv_small, scatter=True)
            own = lax.dynamic_index_in_dim(sent, me, axis=0, keepdims=False)
            res = _adamw("adamw_%s" % group, recv, wp, mp, vp, 32, own=own)
        results[group] = [_unpack_shards(r, gnames, w) for r in res]
    g_sm, d_sm, m_sm, v_sm = _adamw("adamw_small", recv_small, _pack_small(w), _pack_small(m), _pack_small(v), SMALL_ROWS)
    loss = g_sm[_LOSS_AT[0], _LOSS_AT[1]]

    outs = []
    for k, sm in enumerate((g_sm, d_sm, m_sm, v_sm)):
        tree = _unpack_small(sm, w)
        for group in results:
            tree.update(results[group][k])
        outs += [tree[n] for n in names]
    return (loss, grad_x[None], *outs)
```

```python
import functools
import math

import jax
import jax.numpy as jnp
import numpy as np
from jax import lax
from jax.experimental import pallas as pl
from jax.experimental.pallas import tpu as pltpu

F32 = jnp.float32
BF = jnp.bfloat16

D = 1024
HEAD = 64
CHUNK = 64
N_MEM = 256
XHEAD = 256
D_FF = 2816
EPS = 1e-6
NEG = -1e30
LANES = 128
N_DEV = 8
V7X_VMEM_BYTES = 64 * 1024 * 1024
VMEM_LIMIT = V7X_VMEM_BYTES - 8 * 1024 * 1024

ADAM_LR, ADAM_B1, ADAM_B2, ADAM_EPS, ADAM_WD, ADAM_STEP = 0.001, 0.9, 0.999, 1e-08, 0.01, 10

W_LAYOUT = (("w_in", 449, 464, True), ("w_out", 128, 128, False), ("w_xq", 128, 128, False), ("w_xkv", 256, 256, True),
            ("w_xo", 128, 128, False), ("w_gate", 352, 352, True), ("w_up", 352, 352, True), ("w_down", 352, 352, False))
W_ROWS = sum(w[2] for w in W_LAYOUT)
W_OFF = {}
_o = 0
for _n, _r, _p, _t in W_LAYOUT:
    W_OFF[_n] = _o
    _o += _p
SMALL_ROWS = 8
W_SHARD = {"w_in": (449, 464, True), "w_out": (128, 128, False), "w_xq": (128, 128, False), "w_xkv": (256, 256, True),
           "w_xo": (128, 128, False), "w_gate": (352, 352, True), "w_up": (352, 352, True), "w_down": (352, 352, False)}
GATHER_FIRST = ("w_in",)
GATHER_REST = ("w_out", "w_xq", "w_xkv", "w_xo", "w_gate", "w_up", "w_down")
GRAD_GROUPS = {"ffn": ("w_gate", "w_up", "w_down"), "xattn": ("w_xq", "w_xkv", "w_xo"), "mix": ("w_in", "w_out")}

NT = (((1,), (1,)), ((), ()))
NN = (((1,), (0,)), ((), ()))
TN = (((0,), (0,)), ((), ()))
_DIMS = {"nn": NN, "nt": NT, "tn": TN}


def _params(sem):
    return pltpu.CompilerParams(dimension_semantics=sem, vmem_limit_bytes=VMEM_LIMIT)


def _mm(name, products, extras, epilogue, M, N, tm, tn, out_dtypes):
    flat = [t for p in products for t in p]
    counts = [len(p) for p in products]
    in_specs, args = [], []
    for a, b, form in flat:
        if form == "tn":
            in_specs.append(pl.BlockSpec((a.shape[0], tm), lambda i, j: (0, i)))
        else:
            in_specs.append(pl.BlockSpec((tm, a.shape[1]), lambda i, j: (i, 0)))
        if form == "nt":
            in_specs.append(pl.BlockSpec((tn, b.shape[1]), lambda i, j: (j, 0)))
        else:
            in_specs.append(pl.BlockSpec((b.shape[0], tn), lambda i, j: (0, j)))
        args += [a, b]
    for e in extras:
        in_specs.append(pl.BlockSpec((tm, tn), lambda i, j: (i, j)))
        args.append(e)
    n_in = len(args)

    def body(*refs):
        ins, outs = refs[:n_in], refs[n_in:]
        prods, p = [], 0
        for c in counts:
            acc = None
            for _ in range(c):
                a = ins[2 * p][...].astype(BF)
                b = ins[2 * p + 1][...].astype(BF)
                d = lax.dot_general(a, b, _DIMS[flat[p][2]], preferred_element_type=F32)
                acc = d if acc is None else acc + d
                p += 1
            prods.append(acc)
        ex = [r[...].astype(F32) for r in ins[2 * len(flat):]]
        res = epilogue(*prods, *ex)
        for o, r in zip(outs, res):
            o[...] = r.astype(o.dtype)

    return pl.pallas_call(
        body, name=name, grid=(M // tm, N // tn), in_specs=in_specs,
        out_specs=[pl.BlockSpec((tm, tn), lambda i, j: (i, j)) for _ in out_dtypes],
        out_shape=[jax.ShapeDtypeStruct((M, N), dt) for dt in out_dtypes],
        compiler_params=_params(("parallel", "arbitrary")),
    )(*args)


def _ident(x):
    return (x,)


def _add(x, r):
    return (x + r,)


def _spec(rows, w, off, per_j):
    if per_j:
        return pl.BlockSpec((rows, w), lambda j, i: (i, off + j))
    return pl.BlockSpec((rows, w), lambda j, i: (i, off))


def _pspec(rows, w, off, per_j):
    if per_j:
        return pl.BlockSpec((rows, w), lambda j, i: (0, off + j))
    return pl.BlockSpec((rows, w), lambda j, i: (0, off))


def _rw_fwd(name, fn, rows, params, outs, T, tm, nj, n_acc=0):
    in_specs = [_spec(tm, w, off, pj) for _, w, off, pj in rows] + [_pspec(a.shape[0], w, off, pj) for a, w, off, pj in params]
    args = [r[0] for r in rows] + [p[0] for p in params]
    n_in, n_out = len(args), len(outs)
    out_specs = [pl.BlockSpec((tm, w), lambda j, i: (i, j)) for _, w in outs]
    out_shape = [jax.ShapeDtypeStruct((T, nj * w), dt) for dt, w in outs]
    out_specs += [pl.BlockSpec((1, LANES), lambda j, i: (0, 0)) for _ in range(n_acc)]
    out_shape += [jax.ShapeDtypeStruct((1, LANES), F32) for _ in range(n_acc)]

    def body(*refs):
        vals = [r[...].astype(F32) for r in refs[:n_in]]
        res = fn(*vals)
        orefs = refs[n_in:]
        for k in range(n_out):
            orefs[k][...] = res[k].astype(orefs[k].dtype)
        first = (pl.program_id(0) == 0) & (pl.program_id(1) == 0)
        for k in range(n_acc):
            @pl.when(first)
            def _(k=k):
                orefs[n_out + k][...] = jnp.zeros((1, LANES), F32)
            orefs[n_out + k][...] += res[n_out + k]

    return pl.pallas_call(
        body, name=name, grid=(nj, T // tm), in_specs=in_specs, out_specs=out_specs, out_shape=out_shape,
        compiler_params=_params(("arbitrary", "arbitrary")),
    )(*args)


def _rw_bwd(name, fn, rows, params, cots, T, tm, nj, row_grads, param_grads, resid=None):
    in_specs = ([_spec(tm, w, off, pj) for _, w, off, pj in rows] + [_pspec(a.shape[0], w, off, pj) for a, w, off, pj in params]
                + [_spec(tm, w, off, pj) for _, w, off, pj in cots])
    args = [r[0] for r in rows] + [p[0] for p in params] + [c[0] for c in cots]
    if resid is not None:
        in_specs.append(_spec(tm, rows[0][1], rows[0][2], rows[0][3]))
        args.append(resid)
    nr, npar, nc = len(rows), len(params), len(cots)
    out_specs, out_shape, kinds = [], [], []
    for k, dts in enumerate(row_grads):
        for dt in (dts if isinstance(dts, (list, tuple)) else [dts]):
            if dt is not None:
                w = rows[k][1]
                out_specs.append(pl.BlockSpec((tm, w), lambda j, i: (i, j)))
                out_shape.append(jax.ShapeDtypeStruct((T, nj * w), dt))
                kinds.append(("row", k))
    for k, need in enumerate(param_grads):
        if need:
            a, w, off, pj = params[k]
            out_specs.append(_pspec(a.shape[0], w, off, pj))
            out_shape.append(jax.ShapeDtypeStruct(a.shape, F32))
            kinds.append(("par", k))

    def body(*refs):
        vals = [r[...].astype(F32) for r in refs[:nr + npar]]
        ct = tuple(r[...].astype(F32) for r in refs[nr + npar:nr + npar + nc])
        _, vjp = jax.vjp(lambda *a: tuple(fn(*a)), *vals)
        grads = list(vjp(ct))
        n_in = nr + npar + nc + (resid is not None)
        if resid is not None:
            grads[0] = grads[0] + refs[n_in - 1][...].astype(F32)
        orefs = refs[n_in:]
        j, i = pl.program_id(0), pl.program_id(1)
        for o, (kind, k) in zip(orefs, kinds):
            if kind == "row":
                o[...] = grads[k].astype(o.dtype)
            else:
                first = (i == 0) if params[k][3] else ((i == 0) & (j == 0))

                @pl.when(first)
                def _(o=o):
                    o[...] = jnp.zeros(o.shape, F32)
                o[...] += grads[nr + k]

    return pl.pallas_call(
        body, name=name, grid=(nj, T // tm), in_specs=in_specs, out_specs=out_specs, out_shape=out_shape,
        compiler_params=_params(("arbitrary", "arbitrary")),
    )(*args)


def _rms(x, g):
    return x * lax.rsqrt(jnp.mean(x * x, axis=-1, keepdims=True) + EPS) * g


def _rms_fn(x, g):
    return (_rms(x, g),)


def _lo_mask():
    return lax.broadcasted_iota(jnp.int32, (1, LANES), 1) < HEAD


def _gmean(x, lo):
    s0 = jnp.sum(jnp.where(lo, x, 0.0), axis=-1, keepdims=True)
    s1 = jnp.sum(jnp.where(lo, 0.0, x), axis=-1, keepdims=True)
    return jnp.where(lo, s0, s1) * (1.0 / HEAD)


def _fox_prep_fn(fq, fk, gq, gk):
    lo = _lo_mask()
    qn = fq * lax.rsqrt(_gmean(fq * fq, lo) + EPS) * gq * (HEAD ** -0.5)
    kn = fk * lax.rsqrt(_gmean(fk * fk, lo) + EPS) * gk
    return qn, kn


def _cast_fn(v):
    return (v,)


@jax.custom_vjp
def _swap_halves(x):
    bit = (lax.broadcasted_iota(jnp.int32, (1, LANES), 1) & (HEAD // 2)) == 0
    return jnp.where(bit, pltpu.roll(x, LANES - HEAD // 2, 1), pltpu.roll(x, HEAD // 2, 1))


_swap_halves.defvjp(lambda x: (_swap_halves(x), None), lambda _, g: (_swap_halves(g),))


def _ret_fn(rq, rk, rv, rg, cos, sin, s_in, g, lg):
    tb = rq.shape[0]
    nc = tb // CHUNK
    lo = _lo_mask()
    row = lax.broadcasted_iota(jnp.int32, (LANES, 1), 0) < HEAD
    same_head = row == lo
    q = (rq * cos + _swap_halves(rq) * sin) * (HEAD ** -0.5)
    k = rk * cos + _swap_halves(rk) * sin
    q3, k3, v3 = q.reshape(nc, CHUNK, LANES), k.reshape(nc, CHUNK, LANES), rv.reshape(nc, CHUNK, LANES)
    pos = lax.broadcasted_iota(jnp.int32, (CHUNK, 1), 0).astype(F32)
    q_decay = jnp.exp(lg * (pos + 1.0))
    k_decay = jnp.exp(lg * (CHUNK - 1.0 - pos))
    chunk_decay = jnp.exp(lg * float(CHUNK))
    dist = jnp.abs(lax.broadcasted_iota(jnp.int32, (CHUNK, CHUNK), 0) - lax.broadcasted_iota(jnp.int32, (CHUNK, CHUNK), 1)).astype(F32)
    v3b = v3.astype(BF)
    intra = []
    for hh in range(2):
        hm = lo if hh == 0 else ~lo
        lg_h = lg[:, hh * HEAD:hh * HEAD + 1]
        qm = jnp.where(hm, q3, 0.0).astype(BF)
        sc = jnp.einsum("nid,njd->nij", qm, k3.astype(BF), preferred_element_type=F32) * jnp.exp(lg_h * dist)[None]
        intra.append(jnp.einsum("nij,nje->nie", sc.astype(BF), v3b, preferred_element_type=F32))
    o = jnp.where(lo, intra[0], intra[1])
    kv = jnp.einsum("njd,nje->nde", (k3 * k_decay[None]).astype(BF), v3b, preferred_element_type=F32)
    kv = jnp.where(same_head[None], kv, 0.0)
    state, states = s_in, []
    for n in range(nc):
        states.append(state)
        state = state * chunk_decay + kv[n]
    s_prev = jnp.stack(states, axis=0)
    o = o + jnp.einsum("nid,nde->nie", (q3 * q_decay[None]).astype(BF), s_prev.astype(BF), preferred_element_type=F32)
    o = o.reshape(tb, LANES)
    mu = _gmean(o, lo)
    oc = o - mu
    y = oc * lax.rsqrt(_gmean(oc * oc, lo) + EPS) * g
    return jax.nn.silu(rg) * y, state


def _xattn_fn(qx, gq, gk, kk, vv):
    q = _rms(qx, gq)
    k = _rms(kk, gk)
    logits = lax.dot_general(q.astype(BF), k.astype(BF), NT, preferred_element_type=F32) * (XHEAD ** -0.5)
    p = jax.nn.softmax(logits, axis=-1)
    return (jnp.dot(p.astype(BF), vv.astype(BF), preferred_element_type=F32),)


def _swiglu_fwd_epi(g, u):
    return g, u, jax.nn.silu(g) * u


def _swiglu_bwd_epi(dact, g, u):
    _, vjp = jax.vjp(lambda a, b: jax.nn.silu(a) * b, g, u)
    return vjp(dact)


def _loss_fn(h, target):
    err = h - target
    part = jnp.sum(jnp.sum(err * err, axis=0, keepdims=True), axis=-1, keepdims=True) * (0.5 / D)
    dy = err * (1.0 / D)
    return dy, dy, part


def _ret_fwd(P, cos, sin, g_ret, lg, T, tb):
    nb = T // tb

    def body(rq, rk, rv, rg, c, s, g, l, o_ref, s0_ref, state):
        @pl.when(pl.program_id(1) == 0)
        def _():
            state[...] = jnp.zeros(state.shape, F32)
        s0_ref[0, 0] = state[...]
        out, s_new = _ret_fn(rq[...], rk[...], rv[...], rg[...], c[...], s[...], state[...], g[...], l[...])
        o_ref[...] = out
        state[...] = s_new

    sec = lambda off: pl.BlockSpec((tb, LANES), lambda j, i: (i, off + j))
    tab = pl.BlockSpec((tb, LANES), lambda j, i: (i, 0))
    par = pl.BlockSpec((1, LANES), lambda j, i: (0, j))
    return pl.pallas_call(
        body, name="ret_fwd", grid=(4, nb),
        in_specs=[sec(0), sec(4), sec(8), sec(12), tab, tab, par, par],
        out_specs=[pl.BlockSpec((tb, LANES), lambda j, i: (i, j)), pl.BlockSpec((1, 1, LANES, LANES), lambda j, i: (j, i, 0, 0))],
        out_shape=[jax.ShapeDtypeStruct((T, 4 * LANES), F32), jax.ShapeDtypeStruct((4, nb, LANES, LANES), F32)],
        scratch_shapes=[pltpu.VMEM((LANES, LANES), F32)],
        compiler_params=_params(("arbitrary", "arbitrary")),
    )(P, P, P, P, cos, sin, g_ret, lg)


def _ret_bwd(P, cos, sin, g_ret, lg, s0, dmix, T, tb):
    nb = T // tb

    def body(rq, rk, rv, rg, c, s, g, l, s0_ref, do, drq, drk, drv, drg, dg, dstate):
        i = pl.program_id(1)

        @pl.when(i == 0)
        def _():
            dstate[...] = jnp.zeros(dstate.shape, F32)
            dg[...] = jnp.zeros(dg.shape, F32)

        cc, ss, ll = c[...], s[...], l[...]
        _, vjp = jax.vjp(lambda a, b, v, gate, st, gg: _ret_fn(a, b, v, gate, cc, ss, st, gg, ll),
                         rq[...], rk[...], rv[...], rg[...], s0_ref[0, 0], g[...])
        ga, gb, gv, ggate, gst, ggain = vjp((do[...], dstate[...]))
        drq[...] = ga.astype(drq.dtype)
        drk[...] = gb.astype(drk.dtype)
        drv[...] = gv.astype(drv.dtype)
        drg[...] = ggate.astype(drg.dtype)
        dstate[...] = gst
        dg[...] += ggain

    rev = lambda i: nb - 1 - i
    sec = lambda off: pl.BlockSpec((tb, LANES), lambda j, i: (rev(i), off + j))
    tab = pl.BlockSpec((tb, LANES), lambda j, i: (rev(i), 0))
    par = pl.BlockSpec((1, LANES), lambda j, i: (0, j))
    outb = pl.BlockSpec((tb, LANES), lambda j, i: (rev(i), j))
    return pl.pallas_call(
        body, name="ret_bwd", grid=(4, nb),
        in_specs=[sec(0), sec(4), sec(8), sec(12), tab, tab, par, par,
                  pl.BlockSpec((1, 1, LANES, LANES), lambda j, i: (j, rev(i), 0, 0)), outb],
        out_specs=[outb, outb, outb, outb, par],
        out_shape=[jax.ShapeDtypeStruct((T, 4 * LANES), BF)] * 4 + [jax.ShapeDtypeStruct((1, 4 * LANES), F32)],
        scratch_shapes=[pltpu.VMEM((LANES, LANES), F32)],
        compiler_params=_params(("arbitrary", "arbitrary")),
    )(P, P, P, P, cos, sin, g_ret, lg, s0, dmix)


_FB = 128


def _tri(lower):
    r = lax.broadcasted_iota(jnp.int32, (_FB, _FB), 0)
    c = lax.broadcasted_iota(jnp.int32, (_FB, _FB), 1)
    return ((r >= c) if lower else (r <= c)).astype(F32)


def _fgate_fwd(ffp, bpad, T):
    def body(ff_ref, b_ref, fc_ref, fr_ref):
        lane = lax.broadcasted_iota(jnp.int32, (1, LANES), 1)
        tri = _tri(True)
        carry = jnp.zeros((1, LANES), F32)
        for blk in range(T // _FB):
            z = ff_ref[blk * _FB:(blk + 1) * _FB, :] + b_ref[...]
            lf = jnp.where(lane < 8, jax.nn.log_sigmoid(z), 0.0)
            f = jnp.dot(tri, lf, precision=lax.Precision.HIGHEST, preferred_element_type=F32) + carry
            carry = f[_FB - 1:_FB, :]
            fc_ref[blk * _FB:(blk + 1) * _FB, :] = f
            fr_ref[:, blk * _FB:(blk + 1) * _FB] = f.T[:8, :]

    return pl.pallas_call(
        body, name="fgate_fwd",
        out_shape=[jax.ShapeDtypeStruct((T, LANES), F32), jax.ShapeDtypeStruct((8, T), F32)],
        compiler_params=pltpu.CompilerParams(vmem_limit_bytes=VMEM_LIMIT),
    )(ffp, bpad)


def _fgate_bwd(ffp, bpad, dfr, T):
    def body(ff_ref, b_ref, dfr_ref, dff_ref, db_ref):
        lane = lax.broadcasted_iota(jnp.int32, (1, LANES), 1)
        tri = _tri(False)
        carry = jnp.zeros((1, LANES), F32)
        db = jnp.zeros((1, LANES), F32)
        for blk in reversed(range(T // _FB)):
            d8 = dfr_ref[:, blk * _FB:(blk + 1) * _FB]
            dcol = jnp.concatenate([d8, jnp.zeros((_FB - 8, _FB), F32)], axis=0).T
            dlf = jnp.dot(tri, dcol, precision=lax.Precision.HIGHEST, preferred_element_type=F32) + carry
            carry = dlf[0:1, :]
            z = ff_ref[blk * _FB:(blk + 1) * _FB, :] + b_ref[...]
            dz = jnp.where(lane < 8, dlf * jax.nn.sigmoid(-z), 0.0)
            dff_ref[blk * _FB:(blk + 1) * _FB, :] = dz.astype(dff_ref.dtype)
            db = db + jnp.sum(dz, axis=0, keepdims=True)
        db_ref[...] = db

    return pl.pallas_call(
        body, name="fgate_bwd",
        out_shape=[jax.ShapeDtypeStruct((T, LANES), BF), jax.ShapeDtypeStruct((1, LANES), F32)],
        compiler_params=pltpu.CompilerParams(vmem_limit_bytes=VMEM_LIMIT),
    )(ffp, bpad, dfr)


def _head_bias_col(fc, head):
    lane = lax.broadcasted_iota(jnp.int32, (1, LANES), 1)
    return jnp.sum(jnp.where(lane == head, fc, 0.0), axis=-1, keepdims=True)


def _head_bias_row(fr, head):
    sub = lax.broadcasted_iota(jnp.int32, (8, 1), 0)
    return jnp.sum(jnp.where(sub == head, fr, 0.0), axis=0, keepdims=True)


def _fox_fwd(qn, kn, vb, fc, fr, T, tq):
    nq = T // tq

    def body(q_ref, k_ref, v_ref, fc_ref, fr_ref, o_ref, c_ref):
        j, i = pl.program_id(0), pl.program_id(1)
        lane = lax.broadcasted_iota(jnp.int32, (1, LANES), 1)
        lo = lane < HEAD
        causal = lax.broadcasted_iota(jnp.int32, (tq, tq), 0) >= lax.broadcasted_iota(jnp.int32, (tq, tq), 1)
        q = q_ref[...]
        fcb = fc_ref[...]
        outs, cs = [], []
        for hh in range(2):
            hm = lo if hh == 0 else ~lo
            head = 2 * j + hh
            qh = jnp.where(hm, q, jnp.zeros_like(q))
            fq = _head_bias_col(fcb, head)

            def block(kb, carry, diag, qh=qh, fq=fq, head=head):
                m, l, acc = carry
                k0 = pl.multiple_of(kb * tq, tq)
                k = k_ref[pl.ds(k0, tq), :]
                v = v_ref[pl.ds(k0, tq), :]
                fk = _head_bias_row(fr_ref[:, pl.ds(k0, tq)], head)
                s = (lax.dot_general(qh, k, NT, preferred_element_type=F32) + fq) - fk
                if diag:
                    s = jnp.where(causal, s, NEG)
                m2 = jnp.maximum(m, jnp.max(s, axis=-1, keepdims=True))
                p = jnp.exp(s - m2)
                a = jnp.exp(m - m2)
                return m2, a * l + jnp.sum(p, axis=-1, keepdims=True), a * acc + jnp.dot(p.astype(BF), v, preferred_element_type=F32)

            init = (jnp.full((tq, 1), NEG, F32), jnp.zeros((tq, 1), F32), jnp.zeros((tq, LANES), F32))
            carry = lax.fori_loop(0, i, lambda kb, c: block(kb, c, False), init)
            m, l, acc = block(i, carry, True)
            outs.append(acc / l)
            cs.append(fq - (m + jnp.log(l)))
        o_ref[...] = jnp.where(lo, outs[0], outs[1])
        c_ref[0] = jnp.where(lane == 0, cs[0], jnp.where(lane == 1, cs[1], 0.0))

    full = lambda: pl.BlockSpec((T, LANES), lambda j, i: (0, j))
    return pl.pallas_call(
        body, name="fox_fwd", grid=(4, nq),
        in_specs=[pl.BlockSpec((tq, LANES), lambda j, i: (i, j)), full(), full(),
                  pl.BlockSpec((tq, LANES), lambda j, i: (i, 0)), pl.BlockSpec((8, T), lambda j, i: (0, 0))],
        out_specs=[pl.BlockSpec((tq, LANES), lambda j, i: (i, j)), pl.BlockSpec((1, tq, LANES), lambda j, i: (j, i, 0))],
        out_shape=[jax.ShapeDtypeStruct((T, 4 * LANES), F32), jax.ShapeDtypeStruct((4, T, LANES), F32)],
        compiler_params=_params(("parallel", "arbitrary")),
    )(qn, kn, vb, fc, fr)


def _fox_bwd_dq(qn, kn, vb, fr, cq, dmix, T, tq):
    nq = T // tq

    def body(q_ref, k_ref, v_ref, fr_ref, c_ref, do_ref, dq_ref, dl_ref, p_scr, dp_scr):
        j, i = pl.program_id(0), pl.program_id(1)
        lane = lax.broadcasted_iota(jnp.int32, (1, LANES), 1)
        lo = lane < HEAD
        causal = lax.broadcasted_iota(jnp.int32, (tq, tq), 0) >= lax.broadcasted_iota(jnp.int32, (tq, tq), 1)
        q, do, cb = q_ref[...], do_ref[...], c_ref[0]
        res, deltas = [], []
        for hh in range(2):
            hm = lo if hh == 0 else ~lo
            head = 2 * j + hh
            qh = jnp.where(hm, q, jnp.zeros_like(q))
            doh = jnp.where(hm, do, 0.0).astype(BF)
            c = cb[:, hh:hh + 1]

            def probs(kb, delta, diag, qh=qh, doh=doh, c=c, head=head):
                k0 = pl.multiple_of(kb * tq, tq)
                k = k_ref[pl.ds(k0, tq), :]
                v = v_ref[pl.ds(k0, tq), :]
                fk = _head_bias_row(fr_ref[:, pl.ds(k0, tq)], head)
                p = jnp.exp((lax.dot_general(qh, k, NT, preferred_element_type=F32) + c) - fk)
                if diag:
                    p = jnp.where(causal, p, 0.0)
                dp = lax.dot_general(doh, v, NT, preferred_element_type=F32)
                p_scr[:, pl.ds(k0, tq)] = p
                dp_scr[:, pl.ds(k0, tq)] = dp
                return delta + jnp.sum(p * dp, axis=-1, keepdims=True)

            delta = lax.fori_loop(0, i, lambda kb, d: probs(kb, d, False), jnp.zeros((tq, 1), F32))
            delta = probs(i, delta, True)

            def grad(kb, acc, delta=delta):
                k0 = pl.multiple_of(kb * tq, tq)
                ds = p_scr[:, pl.ds(k0, tq)] * (dp_scr[:, pl.ds(k0, tq)] - delta)
                return acc + jnp.dot(ds.astype(BF), k_ref[pl.ds(k0, tq), :], preferred_element_type=F32)

            res.append(lax.fori_loop(0, i + 1, grad, jnp.zeros((tq, LANES), F32)))
            deltas.append(delta)
        dq_ref[...] = jnp.where(lo, res[0], res[1])
        dl_ref[0] = jnp.where(lane == 0, deltas[0], jnp.where(lane == 1, deltas[1], 0.0))

    full = lambda: pl.BlockSpec((T, LANES), lambda j, i: (0, j))
    return pl.pallas_call(
        body, name="fox_bwd_dq", grid=(4, nq),
        in_specs=[pl.BlockSpec((tq, LANES), lambda j, i: (i, j)), full(), full(), pl.BlockSpec((8, T), lambda j, i: (0, 0)),
                  pl.BlockSpec((1, tq, LANES), lambda j, i: (j, i, 0)), pl.BlockSpec((tq, LANES), lambda j, i: (i, 4 + j))],
        out_specs=[pl.BlockSpec((tq, LANES), lambda j, i: (i, j)), pl.BlockSpec((1, tq, LANES), lambda j, i: (j, i, 0))],
        out_shape=[jax.ShapeDtypeStruct((T, 4 * LANES), F32), jax.ShapeDtypeStruct((4, T, LANES), F32)],
        scratch_shapes=[pltpu.VMEM((tq, T), F32), pltpu.VMEM((tq, T), F32)],
        compiler_params=_params(("parallel", "arbitrary")),
    )(qn, kn, vb, fr, cq, dmix)


def _fox_bwd_dkv(qn, kn, vb, fr, cq, dl, dmix, T, tq):
    nq = T // tq

    def body(q_ref, k_ref, v_ref, fr_ref, c_ref, dl_ref, do_ref, dk_ref, dv_ref, dfr_ref):
        j, kb = pl.program_id(0), pl.program_id(1)
        lo = _lo_mask()
        sub = lax.broadcasted_iota(jnp.int32, (8, 1), 0)
        causal = lax.broadcasted_iota(jnp.int32, (tq, tq), 0) >= lax.broadcasted_iota(jnp.int32, (tq, tq), 1)
        k, v, frb = k_ref[...], v_ref[...], fr_ref[...]
        dks, dvs, dfs = [], [], []
        for hh in range(2):
            hm = lo if hh == 0 else ~lo
            head = 2 * j + hh
            km = jnp.where(hm, k, jnp.zeros_like(k))
            vm = jnp.where(hm, v, jnp.zeros_like(v))
            fk = _head_bias_row(frb, head)

            def block(qi, carry, diag, km=km, vm=vm, fk=fk, hm=hm, hh=hh):
                dk, dv, df = carry
                q0 = pl.multiple_of(qi * tq, tq)
                q = q_ref[pl.ds(q0, tq), :]
                c = c_ref[0, pl.ds(q0, tq), :][:, hh:hh + 1]
                delta = dl_ref[0, pl.ds(q0, tq), :][:, hh:hh + 1]
                dob = do_ref[pl.ds(q0, tq), :].astype(BF)
                p = jnp.exp((lax.dot_general(q, km, NT, preferred_element_type=F32) + c) - fk)
                if diag:
                    p = jnp.where(causal, p, 0.0)
                dv = dv + lax.dot_general(p.astype(BF), dob, TN, preferred_element_type=F32)
                dp = lax.dot_general(dob, vm, NT, preferred_element_type=F32)
                ds = p * (dp - delta)
                dk = dk + lax.dot_general(ds.astype(BF), q, TN, preferred_element_type=F32)
                return dk, dv, df - jnp.sum(ds, axis=0, keepdims=True)

            init = (jnp.zeros((tq, LANES), F32), jnp.zeros((tq, LANES), F32), jnp.zeros((1, tq), F32))
            carry = block(kb, init, True)
            dk, dv, df = lax.fori_loop(kb + 1, nq, lambda qi, cr: block(qi, cr, False), carry)
            dks.append(dk)
            dvs.append(dv)
            dfs.append(df)
        dk_ref[...] = jnp.where(lo, dks[0], dks[1])
        dv_ref[...] = jnp.where(lo, dvs[0], dvs[1]).astype(dv_ref.dtype)
        dfr_ref[0] = jnp.where(sub == 0, dfs[0], jnp.where(sub == 1, dfs[1], 0.0))

    full = lambda off: pl.BlockSpec((T, LANES), lambda j, kb: (0, off + j))
    blk = lambda: pl.BlockSpec((tq, LANES), lambda j, kb: (kb, j))
    return pl.pallas_call(
        body, name="fox_bwd_dkv", grid=(4, nq),
        in_specs=[full(0), blk(), blk(), pl.BlockSpec((8, tq), lambda j, kb: (0, kb)),
                  pl.BlockSpec((1, T, LANES), lambda j, kb: (j, 0, 0)), pl.BlockSpec((1, T, LANES), lambda j, kb: (j, 0, 0)), full(4)],
        out_specs=[blk(), blk(), pl.BlockSpec((1, 8, tq), lambda j, kb: (j, 0, kb))],
        out_shape=[jax.ShapeDtypeStruct((T, 4 * LANES), F32), jax.ShapeDtypeStruct((T, 4 * LANES), BF),
                   jax.ShapeDtypeStruct((4, 8, T), F32)],
        compiler_params=_params(("parallel", "arbitrary")),
    )(qn, kn, vb, fr, cq, dl, dmix)


_BIAS_LANE = HEAD


def _split3(f):
    hi = f.astype(BF).astype(F32)
    mid = (f - hi).astype(BF).astype(F32)
    lo = ((f - hi) - mid).astype(BF).astype(F32)
    return hi, mid, lo


def _fox_operands(P, fc, g_fq2, g_fk2, T, tm):
    def body(fq_ref, fk_ref, fv_ref, fc_ref, gq_ref, gk_ref, qa_ref, qat_ref, ka_ref, kat_ref, va_ref, vat_ref):
        j = pl.program_id(0)
        lane = lax.broadcasted_iota(jnp.int32, (1, LANES), 1)
        qn, kn = _fox_prep_fn(fq_ref[...], fk_ref[...], gq_ref[...], gk_ref[...])
        v = fv_ref[...]
        fcb = fc_ref[...]
        b = _BIAS_LANE
        for hh in range(2):
            hi, mid, lo = _split3(_head_bias_col(fcb, 2 * j + hh))
            take = (lambda a: a) if hh == 0 else (lambda a: pltpu.roll(a, HEAD, 1))
            qa = jnp.where(lane < HEAD, take(qn), jnp.where(lane == b, hi, jnp.where(lane == b + 1, mid, jnp.where(
                lane == b + 2, lo, jnp.where(lane < b + 6, 1.0, 0.0)))))
            ka = jnp.where(lane < HEAD, take(kn), jnp.where(lane < b + 3, 1.0, jnp.where(lane == b + 3, -hi, jnp.where(
                lane == b + 4, -mid, jnp.where(lane == b + 5, -lo, 0.0)))))
            va = jnp.where(lane < HEAD, take(v), 0.0)
            for val, ref, tref in ((qa, qa_ref, qat_ref), (ka, ka_ref, kat_ref), (va, va_ref, vat_ref)):
                ref[hh] = val.astype(BF)
                tref[hh] = val.T.astype(BF)

    sec = lambda off: pl.BlockSpec((tm, LANES), lambda j, i: (i, off + j))
    par = pl.BlockSpec((1, LANES), lambda j, i: (0, 0))
    nat = pl.BlockSpec((2, tm, LANES), lambda j, i: (j, i, 0))
    trn = pl.BlockSpec((2, LANES, tm), lambda j, i: (j, 0, i))
    return pl.pallas_call(
        body, name="fox_operands", grid=(4, T // tm),
        in_specs=[sec(16), sec(20), sec(24), pl.BlockSpec((tm, LANES), lambda j, i: (i, 0)), par, par],
        out_specs=[nat, trn, nat, trn, nat, trn],
        out_shape=[jax.ShapeDtypeStruct((8, T, LANES), BF), jax.ShapeDtypeStruct((8, LANES, T), BF)] * 3,
        compiler_params=_params(("parallel", "arbitrary")),
    )(P, P, P, fc, g_fq2, g_fk2)


def _fox_forward(qat, ka, vat, T, tq, tk):
    nq, per = T // tq, tq // tk

    def body(qat_ref, ka_ref, vat_ref, o_ref, lse_ref):
        i = pl.program_id(1)
        sub = lax.broadcasted_iota(jnp.int32, (8, 1), 0)
        krow = lax.broadcasted_iota(jnp.int32, (tk, tq), 0)
        qcol = lax.broadcasted_iota(jnp.int32, (tk, tq), 1)

        def scores(kb):
            k0 = pl.multiple_of(kb * tk, tk)
            return tuple(jnp.dot(ka_ref[hh, pl.ds(k0, tk), :], qat_ref[hh], preferred_element_type=F32) for hh in range(2))

        def step(kb, carry, mask, last=False):
            stats, s_now = carry
            s_next = s_now if last else scores(kb + 1)
            k0 = pl.multiple_of(kb * tk, tk)
            new = []
            for hh in range(2):
                m, l, acc = stats[hh]
                s = s_now[hh] if mask is None else jnp.where(mask, s_now[hh], NEG)
                m2 = jnp.maximum(m, jnp.max(s, axis=0, keepdims=True))
                p = jnp.exp(s - m2)
                a = jnp.exp(m - m2)
                pv = jnp.dot(vat_ref[hh, 0:HEAD, pl.ds(k0, tk)], p.astype(BF), preferred_element_type=F32)
                new.append((m2, a * l + jnp.sum(p, axis=0, keepdims=True), a * acc + pv))
            return tuple(new), s_next

        one = (jnp.full((1, tq), NEG, F32), jnp.zeros((1, tq), F32), jnp.zeros((HEAD, tq), F32))
        carry = lax.fori_loop(0, i * per, lambda kb, c: step(kb, c, None), ((one, one), scores(0)))
        for d in range(per):
            carry = step(i * per + d, carry, krow + d * tk <= qcol, last=(d == per - 1))
        stats = carry[0]
        o_ref[...] = jnp.concatenate([acc / l for _, l, acc in stats], axis=0).T
        lses = [m + jnp.log(l) for m, l, _ in stats]
        lse_ref[0] = jnp.where(sub == 0, lses[0], jnp.where(sub == 1, lses[1], 0.0))

    return pl.pallas_call(
        body, name="fox_forward", grid=(4, nq),
        in_specs=[pl.BlockSpec((2, LANES, tq), lambda j, i: (j, 0, i)), pl.BlockSpec((2, T, LANES), lambda j, i: (j, 0, 0)),
                  pl.BlockSpec((2, LANES, T), lambda j, i: (j, 0, 0))],
        out_specs=[pl.BlockSpec((tq, LANES), lambda j, i: (i, j)), pl.BlockSpec((1, 8, tq), lambda j, i: (j, 0, i))],
        out_shape=[jax.ShapeDtypeStruct((T, 4 * LANES), F32), jax.ShapeDtypeStruct((4, 8, T), F32)],
        compiler_params=_params(("parallel", "arbitrary")),
    )(qat, ka, vat)


def _fox_cotangent(dmix, fox, T, tm):
    def body(do_ref, o_ref, doa_ref, doat_ref, dl_ref):
        lane = lax.broadcasted_iota(jnp.int32, (1, LANES), 1)
        sub = lax.broadcasted_iota(jnp.int32, (8, 1), 0)
        dob = do_ref[...].astype(BF).astype(F32)
        prod_t = (dob * o_ref[...]).T
        d0 = jnp.sum(prod_t[:HEAD], axis=0, keepdims=True)
        d1 = jnp.sum(prod_t[HEAD:], axis=0, keepdims=True)
        dl_ref[0] = jnp.where(sub == 0, d0, jnp.where(sub == 1, d1, 0.0))
        for hh in range(2):
            val = jnp.where(lane < HEAD, dob if hh == 0 else pltpu.roll(dob, HEAD, 1), 0.0)
            doa_ref[hh] = val.astype(BF)
            doat_ref[hh] = val.T.astype(BF)

    return pl.pallas_call(
        body, name="fox_cotangent", grid=(4, T // tm),
        in_specs=[pl.BlockSpec((tm, LANES), lambda j, i: (i, 4 + j)), pl.BlockSpec((tm, LANES), lambda j, i: (i, j))],
        out_specs=[pl.BlockSpec((2, tm, LANES), lambda j, i: (j, i, 0)), pl.BlockSpec((2, LANES, tm), lambda j, i: (j, 0, i)),
                   pl.BlockSpec((1, 8, tm), lambda j, i: (j, 0, i))],
        out_shape=[jax.ShapeDtypeStruct((8, T, LANES), BF), jax.ShapeDtypeStruct((8, LANES, T), BF),
                   jax.ShapeDtypeStruct((4, 8, T), F32)],
        compiler_params=_params(("parallel", "arbitrary")),
    )(dmix, fox)


def _fox_backward(qa, qat, ka, kat, va, doa, doat, lse, dl, T, tq):
    nq = T // tq

    def body(qa_ref, qat_ref, ka_ref, kat_ref, va_ref, doa_ref, doat_ref, lse_ref, dl_ref,
             dq_ref, dk_ref, dv_ref, df_ref, dr_ref, dqt, dk_acc, dv_acc, df_acc):
        j, kb = pl.program_id(0), pl.program_id(1)
        lane = lax.broadcasted_iota(jnp.int32, (1, LANES), 1)
        mask = lax.broadcasted_iota(jnp.int32, (tq, tq), 0) <= lax.broadcasted_iota(jnp.int32, (tq, tq), 1)

        @pl.when(kb == 0)
        def _():
            dqt[...] = jnp.zeros(dqt.shape, F32)

        dk_acc[...] = jnp.zeros(dk_acc.shape, F32)
        dv_acc[...] = jnp.zeros(dv_acc.shape, F32)
        df_acc[...] = jnp.zeros(df_acc.shape, F32)

        def step(qi, diag):
            q0 = pl.multiple_of(qi * tq, tq)
            for hh in range(2):
                s = jnp.dot(ka_ref[hh], qat_ref[hh, :, pl.ds(q0, tq)], preferred_element_type=F32)
                p = jnp.exp(s - lse_ref[0, hh:hh + 1, pl.ds(q0, tq)])
                if diag:
                    p = jnp.where(mask, p, 0.0)
                dp = jnp.dot(va_ref[hh], doat_ref[hh, :, pl.ds(q0, tq)], preferred_element_type=F32)
                ds = p * (dp - dl_ref[0, hh:hh + 1, pl.ds(q0, tq)])
                pb, dsb = p.astype(BF), ds.astype(BF)
                dv_acc[hh] += jnp.dot(pb, doa_ref[hh, pl.ds(q0, tq), :], preferred_element_type=F32)
                dk_acc[hh] += jnp.dot(dsb, qa_ref[hh, pl.ds(q0, tq), :], preferred_element_type=F32)
                dqt[hh, 0:HEAD, pl.ds(q0, tq)] += jnp.dot(kat_ref[hh, 0:HEAD, :], dsb, preferred_element_type=F32)
                dqt[hh, HEAD:HEAD + 8, pl.ds(q0, tq)] += jnp.broadcast_to(jnp.sum(ds, axis=0, keepdims=True), (8, tq))
                part = ds[:, 0:LANES]
                for c in range(1, tq // LANES):
                    part = part + ds[:, c * LANES:(c + 1) * LANES]
                df_acc[hh] += part

        step(kb, True)

        @pl.loop(kb + 1, nq)
        def _(qi):
            step(qi, False)

        lo = lane < HEAD
        dk_ref[...] = jnp.where(lo, dk_acc[0], pltpu.roll(dk_acc[1], HEAD, 1))
        dv_ref[...] = jnp.where(lo, dv_acc[0], pltpu.roll(dv_acc[1], HEAD, 1)).astype(dv_ref.dtype)
        f0 = -jnp.sum(df_acc[0], axis=1, keepdims=True)
        f1 = -jnp.sum(df_acc[1], axis=1, keepdims=True)
        df_ref[0] = jnp.where(lane == 2 * j, f0, jnp.where(lane == 2 * j + 1, f1, 0.0))

        @pl.when(kb == nq - 1)
        def _():
            for t in range(nq):
                cols = slice(t * tq, (t + 1) * tq)
                dq_ref[cols, :] = jnp.concatenate([dqt[0, 0:HEAD, cols], dqt[1, 0:HEAD, cols]], axis=0).T
                rsum = jnp.concatenate([dqt[0, HEAD:HEAD + 8, cols], dqt[1, HEAD:HEAD + 8, cols],
                                        jnp.zeros((LANES - 16, tq), F32)], axis=0).T
                dr_ref[0, cols, :] = jnp.where(lane == 2 * j, rsum[:, 0:1], jnp.where(lane == 2 * j + 1, rsum[:, 8:9], 0.0))

    nat_full = pl.BlockSpec((2, T, LANES), lambda j, kb: (j, 0, 0))
    trn_full = pl.BlockSpec((2, LANES, T), lambda j, kb: (j, 0, 0))
    nat_blk = pl.BlockSpec((2, tq, LANES), lambda j, kb: (j, kb, 0))
    trn_blk = pl.BlockSpec((2, LANES, tq), lambda j, kb: (j, 0, kb))
    rows = pl.BlockSpec((1, 8, T), lambda j, kb: (j, 0, 0))
    blk = pl.BlockSpec((tq, LANES), lambda j, kb: (kb, j))
    return pl.pallas_call(
        body, name="fox_backward", grid=(4, nq),
        in_specs=[nat_full, trn_full, nat_blk, trn_blk, nat_blk, nat_full, trn_full, rows, rows],
        out_specs=[pl.BlockSpec((T, LANES), lambda j, kb: (0, j)), blk, blk, pl.BlockSpec((1, tq, LANES), lambda j, kb: (j, kb, 0)),
                   pl.BlockSpec((1, T, LANES), lambda j, kb: (j, 0, 0))],
        out_shape=[jax.ShapeDtypeStruct((T, 4 * LANES), F32), jax.ShapeDtypeStruct((T, 4 * LANES), F32),
                   jax.ShapeDtypeStruct((T, 4 * LANES), BF), jax.ShapeDtypeStruct((4, T, LANES), F32),
                   jax.ShapeDtypeStruct((4, T, LANES), F32)],
        scratch_shapes=[pltpu.VMEM((2, HEAD + 8, T), F32), pltpu.VMEM((2, tq, LANES), F32), pltpu.VMEM((2, tq, LANES), F32),
                        pltpu.VMEM((2, tq, LANES), F32)],
        compiler_params=_params(("arbitrary", "arbitrary")),
    )(qa, qat, ka, kat, va, doa, doat, lse, dl)


def _fgate_bwd_col(ffp, bpad, dfc, T):
    def body(ff_ref, b_ref, dfc_ref, dff_ref, db_ref):
        lane = lax.broadcasted_iota(jnp.int32, (1, LANES), 1)
        tri = _tri(False)
        carry = jnp.zeros((1, LANES), F32)
        db = jnp.zeros((1, LANES), F32)
        for blk in reversed(range(T // _FB)):
            dlf = jnp.dot(tri, dfc_ref[blk * _FB:(blk + 1) * _FB, :], precision=lax.Precision.HIGHEST,
                          preferred_element_type=F32) + carry
            carry = dlf[0:1, :]
            z = ff_ref[blk * _FB:(blk + 1) * _FB, :] + b_ref[...]
            dz = jnp.where(lane < 8, dlf * jax.nn.sigmoid(-z), 0.0)
            dff_ref[blk * _FB:(blk + 1) * _FB, :] = dz.astype(dff_ref.dtype)
            db = db + jnp.sum(dz, axis=0, keepdims=True)
        db_ref[...] = db

    return pl.pallas_call(
        body, name="fgate_bwd",
        out_shape=[jax.ShapeDtypeStruct((T, LANES), BF), jax.ShapeDtypeStruct((1, LANES), F32)],
        compiler_params=pltpu.CompilerParams(vmem_limit_bytes=VMEM_LIMIT),
    )(ffp, bpad, dfc)


MESH = pl.DeviceIdType.MESH


def _place():
    return lax.axis_index("x"), lax.axis_index("y"), lax.axis_index("c")


def _all_gather(shard):
    R, W = shard.shape

    def body(x_ref, out_ref, send_sems, recv_sems, local_sem):
        x, y, c = _place()
        me, sibling = (x, y, c), (x, y, 1 - c)
        chips = [(1 - x, y), (x, 1 - y), (1 - x, 1 - y)]

        def slot(px, py, pc):
            return out_ref.at[4 * px + 2 * py + pc]

        def copy(k, block, to, src=None):
            return pltpu.make_async_remote_copy(
                src_ref=slot(*block) if src is None else src, dst_ref=slot(*block),
                send_sem=send_sems.at[k], recv_sem=recv_sems.at[k], device_id=to, device_id_type=MESH)

        mine = pltpu.make_async_copy(x_ref, slot(*me), local_sem)
        mine.start()
        first = [copy(0, me, sibling, src=x_ref)]
        first += [copy(1 + n, me, (*chip, c), src=x_ref) for n, chip in enumerate(chips)]
        for cp in first:
            cp.start()
        passed = [copy(4 + n, (*chip, c), sibling) for n, chip in enumerate(chips)]
        for n, chip in enumerate(chips):
            copy(1 + n, (*chip, c), me).wait_recv()
            passed[n].start()
        copy(0, sibling, me).wait_recv()
        for n, chip in enumerate(chips):
            copy(4 + n, (*chip, 1 - c), me).wait_recv()
        for cp in first + passed:
            cp.wait_send()
        mine.wait()

    return pl.pallas_call(
        body, name="all_gather_weights",
        out_shape=jax.ShapeDtypeStruct((N_DEV, R, W), shard.dtype),
        in_specs=[pl.BlockSpec(memory_space=pl.ANY)], out_specs=pl.BlockSpec(memory_space=pl.ANY),
        scratch_shapes=[pltpu.SemaphoreType.DMA((7,)), pltpu.SemaphoreType.DMA((7,)), pltpu.SemaphoreType.DMA],
    )(shard)


def _all_to_all(big, small):
    def body(big_ref, small_ref, rbig_ref, rsmall_ref, send_sems, recv_sems, local_sems):
        x, y, c = _place()
        me = 4 * x + 2 * y + c
        l0 = pltpu.make_async_copy(big_ref.at[me], rbig_ref.at[me], local_sems.at[0])
        l1 = pltpu.make_async_copy(small_ref, rsmall_ref.at[me], local_sems.at[1])
        l0.start()
        l1.start()
        copies = []
        for r in range(1, N_DEV):
            px, py, pc = x ^ (r >> 2), y ^ ((r >> 1) & 1), c ^ (r & 1)
            peer = 4 * px + 2 * py + pc
            copies.append(pltpu.make_async_remote_copy(
                src_ref=big_ref.at[peer], dst_ref=rbig_ref.at[me], send_sem=send_sems.at[2 * r], recv_sem=recv_sems.at[2 * r],
                device_id=(px, py, pc), device_id_type=MESH))
            copies.append(pltpu.make_async_remote_copy(
                src_ref=small_ref, dst_ref=rsmall_ref.at[me], send_sem=send_sems.at[2 * r + 1], recv_sem=recv_sems.at[2 * r + 1],
                device_id=(px, py, pc), device_id_type=MESH))
        for cp in copies:
            cp.start()
        for cp in copies:
            cp.wait_recv()
        for cp in copies:
            cp.wait_send()
        l0.wait()
        l1.wait()

    return pl.pallas_call(
        body, name="all_to_all_grads",
        out_shape=[jax.ShapeDtypeStruct(big.shape, big.dtype), jax.ShapeDtypeStruct((N_DEV,) + small.shape, small.dtype)],
        in_specs=[pl.BlockSpec(memory_space=pl.ANY)] * 2, out_specs=[pl.BlockSpec(memory_space=pl.ANY)] * 2,
        scratch_shapes=[pltpu.SemaphoreType.DMA((2 * N_DEV,)), pltpu.SemaphoreType.DMA((2 * N_DEV,)), pltpu.SemaphoreType.DMA((2,))],
    )(big, small)


def _exchange_copies(src_ref, land_ref, send_sems, recv_sems, scatter):
    x, y, c = _place()
    me = 4 * x + 2 * y + c
    copies = []
    for r in range(1, N_DEV):
        px, py, pc = x ^ (r >> 2), y ^ ((r >> 1) & 1), c ^ (r & 1)
        copies.append(pltpu.make_async_remote_copy(
            src_ref=src_ref.at[4 * px + 2 * py + pc] if scatter else src_ref, dst_ref=land_ref.at[me],
            send_sem=send_sems.at[r - 1], recv_sem=recv_sems.at[r - 1], device_id=(px, py, pc), device_id_type=MESH))
    return copies


_HBM = pl.BlockSpec(memory_space=pltpu.HBM)
_SEM = pl.BlockSpec(memory_space=pltpu.SEMAPHORE)
_EFFECT = pltpu.SideEffectType.DATAFLOW_SIDE_EFFECTING


def _exchange_start(name, src, land, scatter):
    def body(src_ref, land_ref, send_sems, recv_sems, src_thru, land_thru, token):
        for cp in _exchange_copies(src_ref, land_ref, send_sems, recv_sems, scatter):
            cp.start()
        token[...] = jnp.zeros(token.shape, F32)

    return pl.pallas_call(
        body, name=name,
        out_shape=(pltpu.SemaphoreType.DMA((N_DEV - 1,)), pltpu.SemaphoreType.DMA((N_DEV - 1,)),
                   pltpu.HBM(src.shape, src.dtype), pltpu.HBM(land.shape, land.dtype), jax.ShapeDtypeStruct((8, LANES), F32)),
        in_specs=(_HBM, _HBM), out_specs=(_SEM, _SEM, _HBM, _HBM, pl.BlockSpec(memory_space=pltpu.VMEM)),
        input_output_aliases={0: 2, 1: 3},
        compiler_params=pltpu.CompilerParams(has_side_effects=_EFFECT),
    )(pltpu.with_memory_space_constraint(src, pltpu.HBM), pltpu.with_memory_space_constraint(land, pltpu.HBM))


def _exchange_wait(name, started, after, scatter):
    send_sems, recv_sems, src_thru, land_thru, _ = started

    def body(src_ref, land_ref, send_sems, recv_sems, after_ref, src_dead, got_ref):
        copies = _exchange_copies(src_ref, land_ref, send_sems, recv_sems, scatter)
        for cp in copies:
            cp.wait_send()
        for cp in copies:
            cp.wait_recv()

    return pl.pallas_call(
        body, name=name,
        out_shape=(pltpu.HBM(src_thru.shape, src_thru.dtype), pltpu.HBM(land_thru.shape, land_thru.dtype)),
        in_specs=(_HBM, _HBM, _SEM, _SEM, pl.BlockSpec(memory_space=pl.ANY)), out_specs=(_HBM, _HBM),
        input_output_aliases={0: 0, 1: 1},
        compiler_params=pltpu.CompilerParams(has_side_effects=_EFFECT),
    )(src_thru, land_thru, send_sems, recv_sems, after)


def _adamw(name, slots, w, m, v, tr, own=None):
    R, W = w.shape

    def body(s_ref, *refs):
        if own is not None:
            own_ref, refs = refs[0], refs[1:]
        w_ref, m_ref, v_ref, g_ref, d_ref, nm_ref, nv_ref = refs
        g = s_ref[0].astype(F32)
        for s in range(1, N_DEV):
            g = g + s_ref[s].astype(F32)
        if own is not None:
            g = g + own_ref[...].astype(F32)
        m2 = ADAM_B1 * m_ref[...] + (1.0 - ADAM_B1) * g
        v2 = ADAM_B2 * v_ref[...] + (1.0 - ADAM_B2) * jnp.square(g)
        m_hat = m2 / (1.0 - ADAM_B1 ** ADAM_STEP)
        v_hat = v2 / (1.0 - ADAM_B2 ** ADAM_STEP)
        g_ref[...] = g
        d_ref[...] = -ADAM_LR * (m_hat / (jnp.sqrt(v_hat) + ADAM_EPS) + ADAM_WD * w_ref[...])
        nm_ref[...] = m2
        nv_ref[...] = v2

    row = lambda: pl.BlockSpec((tr, W), lambda i: (i, 0))
    return pl.pallas_call(
        body, name=name, grid=(R // tr,),
        in_specs=[pl.BlockSpec((N_DEV, tr, W), lambda i: (0, i, 0))] + [row() for _ in range(3 + (own is not None))],
        out_specs=[row(), row(), row(), row()],
        out_shape=[jax.ShapeDtypeStruct((R, W), F32)] * 4,
        compiler_params=_params(("parallel",)),
    )(slots, *([own] if own is not None else []), w, m, v)


def _tables(T):
    pos = jnp.arange(T, dtype=F32)
    inv_freq = 10000.0 ** (-jnp.arange(0, HEAD, 2, dtype=F32) / HEAD)
    ang = pos[:, None] * inv_freq[None, :]
    cos, sin = jnp.cos(ang), jnp.sin(ang)
    cos4 = jnp.tile(cos, (1, 4))
    sin4 = jnp.tile(jnp.concatenate([-sin, sin], axis=1), (1, 2))
    log_g = jnp.log(1.0 - 2.0 ** (-5.0 - jnp.arange(8, dtype=F32)))
    return cos4, sin4, jnp.repeat(log_g, HEAD)[None, :]


def _local_step(x, mem, target, sp, w_inT, token, fetch_rest, push):
    T = x.shape[0]
    tm = min(512, T)
    tq = min(256, T)
    tb = min(1024, T)
    cos4, sin4, lg = _tables(T)
    g_fq2 = jnp.tile(sp["g_fox_q"], (1, 2))
    g_fk2 = jnp.tile(sp["g_fox_k"], (1, 2))
    g_ret = sp["g_ret_out"].reshape(1, 8 * HEAD)
    bpad = jnp.pad(sp["b_forget"], ((0, 0), (0, LANES - 8)))
    w_secs = [w_inT[k * 512:(k + 1) * 512] for k in range(7)]
    w_ffT = jnp.pad(w_inT[3584:3592], ((0, LANES - 8), (0, 0)))
    w_mainT = w_inT[:3584]
    tie = lambda p, tok: p + tok[0:1, 0:1]
    tm2, tm4 = min(1024, T), min(2048, T)

    hn1, = _rw_fwd("rms_mix", _rms_fn, [(x, D, 0, False)], [(tie(sp["g_mix"], token), D, 0, False)], [(BF, D)], T, tm, 1)
    P, = _mm("proj_in", [[(hn1, w_mainT, "nt")]], [], _ident, T, 3584, tm4, 512, [F32])
    ffp, = _mm("proj_ff", [[(hn1, w_ffT, "nt")]], [], _ident, T, LANES, tm, LANES, [F32])
    ret, s0 = _ret_fwd(P, cos4, sin4, g_ret, lg, T, tb)
    fc, _ = _fgate_fwd(ffp, bpad, T)
    qa, qat, ka, kat, va, vat = _fox_operands(P, fc, g_fq2, g_fk2, T, tm)
    fox, lse = _fox_forward(qat, ka, vat, T, tq, min(128, T))
    mix = jnp.concatenate([ret, fox], axis=1)
    W = fetch_rest(fox)
    h1, = _mm("proj_out", [[(mix, W["w_out"], "nn")]], [x], _add, T, D, tm2, D, [F32])

    hn2, = _rw_fwd("rms_xattn", _rms_fn, [(h1, D, 0, False)], [(sp["g_xattn"], D, 0, False)], [(BF, D)], T, tm, 1)
    qx, = _mm("proj_xq", [[(hn2, W["w_xq"], "nn")]], [], _ident, T, D, tm2, D, [F32])
    memn, = _rw_fwd("rms_mem", _rms_fn, [(mem, D, 0, False)], [(sp["g_mem"], D, 0, False)], [(BF, D)], N_MEM, N_MEM, 1)
    kv, = _mm("proj_xkv", [[(memn, W["w_xkvT"], "nt")]], [], _ident, N_MEM, 2 * D, N_MEM, 512, [F32])
    xa_rows = [(qx, XHEAD, 0, True)]
    xa_params = [(sp["g_xq"], XHEAD, 0, False), (sp["g_xk"], XHEAD, 0, False), (kv, XHEAD, 0, True), (kv, XHEAD, 4, True)]
    xo, = _rw_fwd("xattn_fwd", _xattn_fn, xa_rows, xa_params, [(BF, XHEAD)], T, tm, 4)
    h2, = _mm("proj_xo", [[(xo, W["w_xo"], "nn")]], [h1], _add, T, D, tm2, D, [F32])

    hn3, = _rw_fwd("rms_ffn", _rms_fn, [(h2, D, 0, False)], [(sp["g_ffn"], D, 0, False)], [(BF, D)], T, tm, 1)
    gate, up, act = _mm("ffn_in", [[(hn3, W["w_gateT"], "nt")], [(hn3, W["w_upT"], "nt")]], [], _swiglu_fwd_epi,
                        T, D_FF, tm4, 256, [BF, BF, BF])
    h3, = _mm("ffn_out", [[(act, W["w_down"], "nn")]], [h2], _add, T, D, tm, D, [F32])
    dy, dyb, loss_part = _rw_fwd("loss", _loss_fn, [(h3, D, 0, False), (target, D, 0, False)], [], [(F32, D), (BF, D)], T, tm, 1,
                                 n_acc=1)

    dgate, dup = _mm("ffn_out_bwd", [[(dyb, W["w_down"], "nt")]], [gate, up], _swiglu_bwd_epi, T, D_FF, tm4, 256, [BF, BF])
    dhn3, = _mm("ffn_in_bwd", [[(dgate, W["w_gateT"], "nn"), (dup, W["w_upT"], "nn")]], [], _ident, T, D, tm, D, [F32])
    gW = {}
    gW["w_gateT"], = _mm("dw_gate", [[(dgate, hn3, "tn")]], [], _ident, D_FF, D, 256, D, [F32])
    gW["w_upT"], = _mm("dw_up", [[(dup, hn3, "tn")]], [], _ident, D_FF, D, 256, D, [F32])
    gW["w_down"], = _mm("dw_down", [[(act, dyb, "tn")]], [], _ident, D_FF, D, 256, D, [F32])
    tok = push("ffn", gW)
    gs = {}
    dh2, dh2b, gs["g_ffn"] = _rw_bwd("rms_ffn_bwd", _rms_fn, [(h2, D, 0, False)], [(tie(sp["g_ffn"], tok), D, 0, False)],
                                     [(dhn3, D, 0, False)], T, tm, 1, [[F32, BF]], [True], resid=dy)

    dxo, = _mm("proj_xo_bwd", [[(dh2b, W["w_xo"], "nt")]], [], _ident, T, D, tm2, D, [BF])
    gW["w_xo"], = _mm("dw_xo", [[(xo, dh2b, "tn")]], [], _ident, D, D, 256, D, [F32])
    dqx, gs["g_xq"], gs["g_xk"], dkv_k, dkv_v = _rw_bwd(
        "xattn_bwd", _xattn_fn, xa_rows, xa_params, [(dxo, XHEAD, 0, True)], T, tm, 4, [BF], [True, True, True, True])
    dkv = jnp.concatenate([dkv_k[:, :D], dkv_v[:, D:]], axis=1)
    dhn2, = _mm("proj_xq_bwd", [[(dqx, W["w_xq"], "nt")]], [], _ident, T, D, tm2, D, [F32])
    gW["w_xq"], = _mm("dw_xq", [[(hn2, dqx, "tn")]], [], _ident, D, D, 256, D, [F32])
    dmemn, = _mm("proj_xkv_bwd", [[(dkv, W["w_xkvT"], "nn")]], [], _ident, N_MEM, D, N_MEM, 512, [F32])
    gW["w_xkvT"], = _mm("dw_xkv", [[(dkv, memn, "tn")]], [], _ident, 2 * D, D, 512, 512, [F32])
    tok = push("xattn", gW)
    gs["g_mem"], = _rw_bwd("rms_mem_bwd", _rms_fn, [(mem, D, 0, False)], [(sp["g_mem"], D, 0, False)], [(dmemn, D, 0, False)],
                           N_MEM, N_MEM, 1, [None], [True])
    dh1, dh1b, gs["g_xattn"] = _rw_bwd("rms_xattn_bwd", _rms_fn, [(h1, D, 0, False)], [(tie(sp["g_xattn"], tok), D, 0, False)],
                                       [(dhn2, D, 0, False)], T, tm, 1, [[F32, BF]], [True], resid=dh2)

    dmix, = _mm("proj_out_bwd", [[(dh1b, W["w_out"], "nt")]], [], _ident, T, D, tm2, D, [F32])
    gW["w_out"], = _mm("dw_out", [[(mix, dh1b, "tn")]], [], _ident, D, D, 256, D, [F32])
    doa, doat, dl = _fox_cotangent(dmix, fox, T, tm)
    dqn, dkn, dfv, dfc4, drc4 = _fox_backward(qa, qat, ka, kat, va, doa, doat, lse, dl, T, tq)
    dfq, dfk, gq2, gk2 = _rw_bwd("fox_prep_bwd", _fox_prep_fn, [(P, LANES, 16, True), (P, LANES, 20, True)],
                                 [(g_fq2, LANES, 0, False), (g_fk2, LANES, 0, False)],
                                 [(dqn, LANES, 0, True), (dkn, LANES, 0, True)], T, tm, 4, [BF, BF], [True, True])
    gs["g_fox_q"] = gq2[:, :HEAD] + gq2[:, HEAD:]
    gs["g_fox_k"] = gk2[:, :HEAD] + gk2[:, HEAD:]
    dff, dbp = _fgate_bwd_col(ffp, bpad, jnp.sum(dfc4 + drc4, axis=0), T)
    gs["b_forget"] = dbp[:, :8]
    drq, drk, drv, drg, dg_ret = _ret_bwd(P, cos4, sin4, g_ret, lg, s0, dmix, T, tb)
    gs["g_ret_out"] = dg_ret
    dsecs = [drq, drk, drv, drg, dfq, dfk, dfv]
    dhn1, = _mm("proj_in_bwd", [[(d, w, "nn") for d, w in zip(dsecs, w_secs)] + [(dff, w_ffT, "nn")]], [], _ident,
                T, D, tm, D, [F32])
    g_secs = [_mm("dw_in_%d" % k, [[(d, hn1, "tn")]], [], _ident, 512, D, 256, D, [F32])[0] for k, d in enumerate(dsecs)]
    g_ff, = _mm("dw_in_ff", [[(dff, hn1, "tn")]], [], _ident, LANES, D, LANES, D, [F32])
    gW["w_inT"] = jnp.concatenate(g_secs + [g_ff[:8]], axis=0)
    grad_x, gs["g_mix"] = _rw_bwd("rms_mix_bwd", _rms_fn, [(x, D, 0, False)], [(sp["g_mix"], D, 0, False)], [(dhn1, D, 0, False)],
                                  T, tm, 1, [F32], [True], resid=dh1)
    return loss_part, grad_x, gW, gs


_CANON = {"w_in": "w_inT", "w_xkv": "w_xkvT", "w_gate": "w_gateT", "w_up": "w_upT"}
_SMALL = (("g_mix", 0, 0, 1024), ("g_xattn", 1, 0, 1024), ("g_mem", 2, 0, 1024), ("g_ffn", 3, 0, 1024),
          ("g_ret_out", 4, 0, 512), ("g_xq", 4, 512, 256), ("g_xk", 4, 768, 256),
          ("g_fox_q", 5, 0, 64), ("g_fox_k", 5, 64, 64), ("b_forget", 5, 128, 8))
_LOSS_AT = (5, 256)


def _pack_shards(tree, dtype):
    parts = []
    for name, rows, padded, transposed in W_LAYOUT:
        a = tree[name][0]
        a = a.T if transposed else a
        parts.append(jnp.pad(a, ((0, padded - rows), (0, 0))).astype(dtype))
    return jnp.concatenate(parts, axis=0)


def _unpack_shards(packed, like):
    out = {}
    for name, rows, padded, transposed in W_LAYOUT:
        a = packed[W_OFF[name]:W_OFF[name] + rows]
        out[name] = (a.T if transposed else a)[None].reshape(like[name].shape)
    return out


def _pack_small(tree):
    rows = [jnp.zeros((1, D), F32) for _ in range(SMALL_ROWS)]
    buf = jnp.concatenate(rows, axis=0)
    for name, r, c, n in _SMALL:
        buf = lax.dynamic_update_slice(buf, tree[name].reshape(1, n).astype(F32), (r, c))
    return buf


def _unpack_small(buf, like):
    return {name: buf[r:r + 1, c:c + n].reshape(like[name].shape) for name, r, c, n in _SMALL}


def kernel(x, mem, g_mix, w_in, b_forget, g_ret_out, g_fox_q, g_fox_k, w_out, g_xattn, w_xq, w_xkv, g_mem, g_xq, g_xk, w_xo, g_ffn, w_gate, w_up, w_down, loss_target, m_g_mix, m_w_in, m_b_forget, m_g_ret_out, m_g_fox_q, m_g_fox_k, m_w_out, m_g_xattn, m_w_xq, m_w_xkv, m_g_mem, m_g_xq, m_g_xk, m_w_xo, m_g_ffn, m_w_gate, m_w_up, m_w_down, v_g_mix, v_w_in, v_b_forget, v_g_ret_out, v_g_fox_q, v_g_fox_k, v_w_out, v_g_xattn, v_w_xq, v_w_xkv, v_g_mem, v_g_xq, v_g_xk, v_w_xo, v_g_ffn, v_w_gate, v_w_up, v_w_down):
    names = ("g_mix", "w_in", "b_forget", "g_ret_out", "g_fox_q", "g_fox_k", "w_out", "g_xattn", "w_xq", "w_xkv", "g_mem",
             "g_xq", "g_xk", "w_xo", "g_ffn", "w_gate", "w_up", "w_down")
    w = dict(zip(names, (g_mix, w_in, b_forget, g_ret_out, g_fox_q, g_fox_k, w_out, g_xattn, w_xq, w_xkv, g_mem, g_xq, g_xk,
                         w_xo, g_ffn, w_gate, w_up, w_down)))
    m = dict(zip(names, (m_g_mix, m_w_in, m_b_forget, m_g_ret_out, m_g_fox_q, m_g_fox_k, m_w_out, m_g_xattn, m_w_xq, m_w_xkv,
                         m_g_mem, m_g_xq, m_g_xk, m_w_xo, m_g_ffn, m_w_gate, m_w_up, m_w_down)))
    v = dict(zip(names, (v_g_mix, v_w_in, v_b_forget, v_g_ret_out, v_g_fox_q, v_g_fox_k, v_w_out, v_g_xattn, v_w_xq, v_w_xkv,
                         v_g_mem, v_g_xq, v_g_xk, v_w_xo, v_g_ffn, v_w_gate, v_w_up, v_w_down)))
    small_names = [s[0] for s in _SMALL]

    gathered = _all_gather(_pack_shards(w, BF))
    W = {}
    for name, rows, padded, transposed in W_LAYOUT:
        full = gathered[:, W_OFF[name]:W_OFF[name] + rows].reshape(N_DEV * rows, D)
        W[_CANON.get(name, name)] = full

    sp = {n: w[n].reshape(1, -1) for n in small_names}
    loss_part, grad_x, gW, gs = _local_step(x[0], mem[0], loss_target[0], sp, W)

    chunks = []
    for name, rows, padded, transposed in W_LAYOUT:
        g = gW[_CANON.get(name, name)].reshape(N_DEV, rows, D)
        chunks.append(jnp.pad(g, ((0, 0), (0, padded - rows), (0, 0))).astype(BF))
    send = jnp.concatenate(chunks, axis=1)
    small = _pack_small(gs)
    small = lax.dynamic_update_slice(small, loss_part[:, :1], _LOSS_AT)
    recv, recv_small = _all_to_all(send, small)

    g_big, d_big, m_big, v_big = _adamw("adamw_shards", recv, _pack_shards(w, F32), _pack_shards(m, F32), _pack_shards(v, F32), 240)
    g_sm, d_sm, m_sm, v_sm = _adamw("adamw_small", recv_small, _pack_small(w), _pack_small(m), _pack_small(v), SMALL_ROWS)
    loss = g_sm[_LOSS_AT[0], _LOSS_AT[1]]

    outs = []
    for big, sm in ((g_big, g_sm), (d_big, d_sm), (m_big, m_sm), (v_big, v_sm)):
        tree = {**_unpack_shards(big, w), **_unpack_small(sm, w)}
        outs += [tree[n] for n in names]
    return (loss, grad_x[None], *outs)


def _pack_shards(tree, names, dtype):
    parts = []
    for name in names:
        rows, padded, transposed = W_SHARD[name]
        a = tree[name][0]
        a = a.T if transposed else a
        parts.append(jnp.pad(a, ((0, padded - rows), (0, 0))).astype(dtype))
    return jnp.concatenate(parts, axis=0)


def _unpack_shards(packed, names, like):
    out, off = {}, 0
    for name in names:
        rows, padded, transposed = W_SHARD[name]
        a = packed[off:off + rows]
        out[name] = (a.T if transposed else a)[None].reshape(like[name].shape)
        off += padded
    return out


def _unpack_gathered(gathered, names):
    out, off = {}, 0
    for name in names:
        rows, padded, _ = W_SHARD[name]
        out[_CANON.get(name, name)] = gathered[:, off:off + rows].reshape(N_DEV * rows, D)
        off += padded
    return out


def _pack_chunks(grads, names):
    chunks = []
    for name in names:
        rows, padded, _ = W_SHARD[name]
        g = grads[_CANON.get(name, name)].reshape(N_DEV, rows, D)
        chunks.append(jnp.pad(g, ((0, 0), (0, padded - rows), (0, 0))).astype(BF))
    return jnp.concatenate(chunks, axis=1)


def kernel(x, mem, g_mix, w_in, b_forget, g_ret_out, g_fox_q, g_fox_k, w_out, g_xattn, w_xq, w_xkv, g_mem, g_xq, g_xk, w_xo, g_ffn, w_gate, w_up, w_down, loss_target, m_g_mix, m_w_in, m_b_forget, m_g_ret_out, m_g_fox_q, m_g_fox_k, m_w_out, m_g_xattn, m_w_xq, m_w_xkv, m_g_mem, m_g_xq, m_g_xk, m_w_xo, m_g_ffn, m_w_gate, m_w_up, m_w_down, v_g_mix, v_w_in, v_b_forget, v_g_ret_out, v_g_fox_q, v_g_fox_k, v_w_out, v_g_xattn, v_w_xq, v_w_xkv, v_g_mem, v_g_xq, v_g_xk, v_w_xo, v_g_ffn, v_w_gate, v_w_up, v_w_down):
    names = ("g_mix", "w_in", "b_forget", "g_ret_out", "g_fox_q", "g_fox_k", "w_out", "g_xattn", "w_xq", "w_xkv", "g_mem",
             "g_xq", "g_xk", "w_xo", "g_ffn", "w_gate", "w_up", "w_down")
    w = dict(zip(names, (g_mix, w_in, b_forget, g_ret_out, g_fox_q, g_fox_k, w_out, g_xattn, w_xq, w_xkv, g_mem, g_xq, g_xk,
                         w_xo, g_ffn, w_gate, w_up, w_down)))
    m = dict(zip(names, (m_g_mix, m_w_in, m_b_forget, m_g_ret_out, m_g_fox_q, m_g_fox_k, m_w_out, m_g_xattn, m_w_xq, m_w_xkv,
                         m_g_mem, m_g_xq, m_g_xk, m_w_xo, m_g_ffn, m_w_gate, m_w_up, m_w_down)))
    v = dict(zip(names, (v_g_mix, v_w_in, v_b_forget, v_g_ret_out, v_g_fox_q, v_g_fox_k, v_w_out, v_g_xattn, v_w_xq, v_w_xkv,
                         v_g_mem, v_g_xq, v_g_xk, v_w_xo, v_g_ffn, v_w_gate, v_w_up, v_w_down)))
    small_names = [s[0] for s in _SMALL]
    me = 4 * lax.axis_index("x") + 2 * lax.axis_index("y") + lax.axis_index("c")

    first = _all_gather(_pack_shards(w, GATHER_FIRST, BF))
    first, rest_shard = lax.optimization_barrier((first, _pack_shards(w, GATHER_REST, BF)))
    rest_started = _exchange_start("gather_rest_start", rest_shard,
                                   jnp.broadcast_to(rest_shard[None], (N_DEV,) + rest_shard.shape), scatter=False)

    def fetch_rest(after):
        return _unpack_gathered(_exchange_wait("gather_rest_wait", rest_started, after, scatter=False)[1], GATHER_REST)

    pushed = {}

    def push(group, grads):
        send = _pack_chunks(grads, GRAD_GROUPS[group])
        pushed[group] = _exchange_start("scatter_%s_start" % group, send, jnp.zeros(send.shape, BF), scatter=True)
        return pushed[group][4]

    sp = {n: w[n].reshape(1, -1) for n in small_names}
    loss_part, grad_x, g_last, gs = _local_step(x[0], mem[0], loss_target[0], sp, _unpack_gathered(first, GATHER_FIRST)["w_inT"],
                                                rest_started[4], fetch_rest, push)

    small = lax.dynamic_update_slice(_pack_small(gs), loss_part[:, :1], _LOSS_AT)
    recv_mix, recv_small = _all_to_all(_pack_chunks(g_last, GRAD_GROUPS["mix"]), small)

    results = {}
    for group in ("ffn", "xattn", "mix"):
        gnames = GRAD_GROUPS[group]
        wp, mp, vp = (_pack_shards(t, gnames, F32) for t in (w, m, v))
        if group == "mix":
            res = _adamw("adamw_mix", recv_mix, wp, mp, vp, 16)
        else:
            sent, recv = _exchange_wait("scatter_%s_wait" % group, pushed[group], recv_small, scatter=True)
            own = lax.dynamic_index_in_dim(sent, me, axis=0, keepdims=False)
            res = _adamw("adamw_%s" % group, recv, wp, mp, vp, {"ffn": 176, "xattn": 128}[group], own=own)
        results[group] = [_unpack_shards(r, gnames, w) for r in res]
    g_sm, d_sm, m_sm, v_sm = _adamw("adamw_small", recv_small, _pack_small(w), _pack_small(m), _pack_small(v), SMALL_ROWS)
    loss = g_sm[_LOSS_AT[0], _LOSS_AT[1]]

    outs = []
    for k, sm in enumerate((g_sm, d_sm, m_sm, v_sm)):
        tree = _unpack_small(sm, w)
        for group in results:
            tree.update(results[group][k])
        outs += [tree[n] for n in names]
    return (loss, grad_x[None], *outs)
```

```python
import functools
import math

import jax
import jax.numpy as jnp
import numpy as np
from jax import lax
from jax.experimental import pallas as pl
from jax.experimental.pallas import tpu as pltpu

F32 = jnp.float32
BF = jnp.bfloat16

D = 1024
HEAD = 64
CHUNK = 64
N_MEM = 256
XHEAD = 256
D_FF = 2816
EPS = 1e-6
NEG = -1e30
LANES = 128
N_DEV = 8
V7X_VMEM_BYTES = 64 * 1024 * 1024
VMEM_LIMIT = V7X_VMEM_BYTES - 8 * 1024 * 1024

ADAM_LR, ADAM_B1, ADAM_B2, ADAM_EPS, ADAM_WD, ADAM_STEP = 0.001, 0.9, 0.999, 1e-08, 0.01, 10

W_LAYOUT = (("w_in", 449, 464, True), ("w_out", 128, 128, False), ("w_xq", 128, 128, False), ("w_xkv", 256, 256, True),
            ("w_xo", 128, 128, False), ("w_gate", 352, 352, True), ("w_up", 352, 352, True), ("w_down", 352, 352, False))
W_ROWS = sum(w[2] for w in W_LAYOUT)
W_OFF = {}
_o = 0
for _n, _r, _p, _t in W_LAYOUT:
    W_OFF[_n] = _o
    _o += _p
SMALL_ROWS = 8
W_SHARD = {"w_in": (449, 464, True), "w_out": (128, 128, False), "w_xq": (128, 128, False), "w_xkv": (256, 256, True),
           "w_xo": (128, 128, False), "w_gate": (352, 352, True), "w_up": (352, 352, True), "w_down": (352, 352, False)}
GATHER_FIRST = ("w_in",)
GATHER_REST = ("w_out", "w_xq", "w_xkv", "w_xo", "w_gate", "w_up", "w_down")
GRAD_GROUPS = {"ffn": ("w_gate", "w_up", "w_down"), "xattn": ("w_xq", "w_xkv", "w_xo"), "mix": ("w_in", "w_out")}

NT = (((1,), (1,)), ((), ()))
NN = (((1,), (0,)), ((), ()))
TN = (((0,), (0,)), ((), ()))
_DIMS = {"nn": NN, "nt": NT, "tn": TN}


def _params(sem):
    return pltpu.CompilerParams(dimension_semantics=sem, vmem_limit_bytes=VMEM_LIMIT)


def _mm(name, products, extras, epilogue, M, N, tm, tn, out_dtypes):
    flat = [t for p in products for t in p]
    counts = [len(p) for p in products]
    in_specs, args = [], []
    for a, b, form in flat:
        if form == "tn":
            in_specs.append(pl.BlockSpec((a.shape[0], tm), lambda i, j: (0, i)))
        else:
            in_specs.append(pl.BlockSpec((tm, a.shape[1]), lambda i, j: (i, 0)))
        if form == "nt":
            in_specs.append(pl.BlockSpec((tn, b.shape[1]), lambda i, j: (j, 0)))
        else:
            in_specs.append(pl.BlockSpec((b.shape[0], tn), lambda i, j: (0, j)))
        args += [a, b]
    for e in extras:
        in_specs.append(pl.BlockSpec((tm, tn), lambda i, j: (i, j)))
        args.append(e)
    n_in = len(args)

    def body(*refs):
        ins, outs = refs[:n_in], refs[n_in:]
        prods, p = [], 0
        for c in counts:
            acc = None
            for _ in range(c):
                a = ins[2 * p][...].astype(BF)
                b = ins[2 * p + 1][...].astype(BF)
                d = lax.dot_general(a, b, _DIMS[flat[p][2]], preferred_element_type=F32)
                acc = d if acc is None else acc + d
                p += 1
            prods.append(acc)
        ex = [r[...].astype(F32) for r in ins[2 * len(flat):]]
        res = epilogue(*prods, *ex)
        for o, r in zip(outs, res):
            o[...] = r.astype(o.dtype)

    return pl.pallas_call(
        body, name=name, grid=(M // tm, N // tn), in_specs=in_specs,
        out_specs=[pl.BlockSpec((tm, tn), lambda i, j: (i, j)) for _ in out_dtypes],
        out_shape=[jax.ShapeDtypeStruct((M, N), dt) for dt in out_dtypes],
        compiler_params=_params(("parallel", "arbitrary")),
    )(*args)


def _ident(x):
    return (x,)


def _add(x, r):
    return (x + r,)


def _spec(rows, w, off, per_j):
    if per_j:
        return pl.BlockSpec((rows, w), lambda j, i: (i, off + j))
    return pl.BlockSpec((rows, w), lambda j, i: (i, off))


def _pspec(rows, w, off, per_j):
    if per_j:
        return pl.BlockSpec((rows, w), lambda j, i: (0, off + j))
    return pl.BlockSpec((rows, w), lambda j, i: (0, off))


def _rw_fwd(name, fn, rows, params, outs, T, tm, nj, n_acc=0):
    in_specs = [_spec(tm, w, off, pj) for _, w, off, pj in rows] + [_pspec(a.shape[0], w, off, pj) for a, w, off, pj in params]
    args = [r[0] for r in rows] + [p[0] for p in params]
    n_in, n_out = len(args), len(outs)
    out_specs = [pl.BlockSpec((tm, w), lambda j, i: (i, j)) for _, w in outs]
    out_shape = [jax.ShapeDtypeStruct((T, nj * w), dt) for dt, w in outs]
    out_specs += [pl.BlockSpec((1, LANES), lambda j, i: (0, 0)) for _ in range(n_acc)]
    out_shape += [jax.ShapeDtypeStruct((1, LANES), F32) for _ in range(n_acc)]

    def body(*refs):
        vals = [r[...].astype(F32) for r in refs[:n_in]]
        res = fn(*vals)
        orefs = refs[n_in:]
        for k in range(n_out):
            orefs[k][...] = res[k].astype(orefs[k].dtype)
        first = (pl.program_id(0) == 0) & (pl.program_id(1) == 0)
        for k in range(n_acc):
            @pl.when(first)
            def _(k=k):
                orefs[n_out + k][...] = jnp.zeros((1, LANES), F32)
            orefs[n_out + k][...] += res[n_out + k]

    return pl.pallas_call(
        body, name=name, grid=(nj, T // tm), in_specs=in_specs, out_specs=out_specs, out_shape=out_shape,
        compiler_params=_params(("arbitrary", "arbitrary")),
    )(*args)


def _rw_bwd(name, fn, rows, params, cots, T, tm, nj, row_grads, param_grads, resid=None):
    in_specs = ([_spec(tm, w, off, pj) for _, w, off, pj in rows] + [_pspec(a.shape[0], w, off, pj) for a, w, off, pj in params]
                + [_spec(tm, w, off, pj) for _, w, off, pj in cots])
    args = [r[0] for r in rows] + [p[0] for p in params] + [c[0] for c in cots]
    if resid is not None:
        in_specs.append(_spec(tm, rows[0][1], rows[0][2], rows[0][3]))
        args.append(resid)
    nr, npar, nc = len(rows), len(params), len(cots)
    out_specs, out_shape, kinds = [], [], []
    for k, dts in enumerate(row_grads):
        for dt in (dts if isinstance(dts, (list, tuple)) else [dts]):
            if dt is not None:
                w = rows[k][1]
                out_specs.append(pl.BlockSpec((tm, w), lambda j, i: (i, j)))
                out_shape.append(jax.ShapeDtypeStruct((T, nj * w), dt))
                kinds.append(("row", k))
    for k, need in enumerate(param_grads):
        if need:
            a, w, off, pj = params[k]
            out_specs.append(_pspec(a.shape[0], w, off, pj))
            out_shape.append(jax.ShapeDtypeStruct(a.shape, F32))
            kinds.append(("par", k))

    def body(*refs):
        vals = [r[...].astype(F32) for r in refs[:nr + npar]]
        ct = tuple(r[...].astype(F32) for r in refs[nr + npar:nr + npar + nc])
        _, vjp = jax.vjp(lambda *a: tuple(fn(*a)), *vals)
        grads = list(vjp(ct))
        n_in = nr + npar + nc + (resid is not None)
        if resid is not None:
            grads[0] = grads[0] + refs[n_in - 1][...].astype(F32)
        orefs = refs[n_in:]
        j, i = pl.program_id(0), pl.program_id(1)
        for o, (kind, k) in zip(orefs, kinds):
            if kind == "row":
                o[...] = grads[k].astype(o.dtype)
            else:
                first = (i == 0) if params[k][3] else ((i == 0) & (j == 0))

                @pl.when(first)
                def _(o=o):
                    o[...] = jnp.zeros(o.shape, F32)
                o[...] += grads[nr + k]

    return pl.pallas_call(
        body, name=name, grid=(nj, T // tm), in_specs=in_specs, out_specs=out_specs, out_shape=out_shape,
        compiler_params=_params(("arbitrary", "arbitrary")),
    )(*args)


def _rms(x, g):
    return x * lax.rsqrt(jnp.mean(x * x, axis=-1, keepdims=True) + EPS) * g


def _rms_fn(x, g):
    return (_rms(x, g),)


def _lo_mask():
    return lax.broadcasted_iota(jnp.int32, (1, LANES), 1) < HEAD


def _gmean(x, lo):
    s0 = jnp.sum(jnp.where(lo, x, 0.0), axis=-1, keepdims=True)
    s1 = jnp.sum(jnp.where(lo, 0.0, x), axis=-1, keepdims=True)
    return jnp.where(lo, s0, s1) * (1.0 / HEAD)


def _fox_prep_fn(fq, fk, gq, gk):
    lo = _lo_mask()
    qn = fq * lax.rsqrt(_gmean(fq * fq, lo) + EPS) * gq * (HEAD ** -0.5)
    kn = fk * lax.rsqrt(_gmean(fk * fk, lo) + EPS) * gk
    return qn, kn


def _cast_fn(v):
    return (v,)


@jax.custom_vjp
def _swap_halves(x):
    bit = (lax.broadcasted_iota(jnp.int32, (1, LANES), 1) & (HEAD // 2)) == 0
    return jnp.where(bit, pltpu.roll(x, LANES - HEAD // 2, 1), pltpu.roll(x, HEAD // 2, 1))


_swap_halves.defvjp(lambda x: (_swap_halves(x), None), lambda _, g: (_swap_halves(g),))


def _ret_fn(rq, rk, rv, rg, cos, sin, s_in, g, lg):
    tb = rq.shape[0]
    nc = tb // CHUNK
    lo = _lo_mask()
    row = lax.broadcasted_iota(jnp.int32, (LANES, 1), 0) < HEAD
    same_head = row == lo
    q = (rq * cos + _swap_halves(rq) * sin) * (HEAD ** -0.5)
    k = rk * cos + _swap_halves(rk) * sin
    q3, k3, v3 = q.reshape(nc, CHUNK, LANES), k.reshape(nc, CHUNK, LANES), rv.reshape(nc, CHUNK, LANES)
    pos = lax.broadcasted_iota(jnp.int32, (CHUNK, 1), 0).astype(F32)
    q_decay = jnp.exp(lg * (pos + 1.0))
    k_decay = jnp.exp(lg * (CHUNK - 1.0 - pos))
    chunk_decay = jnp.exp(lg * float(CHUNK))
    dist = jnp.abs(lax.broadcasted_iota(jnp.int32, (CHUNK, CHUNK), 0) - lax.broadcasted_iota(jnp.int32, (CHUNK, CHUNK), 1)).astype(F32)
    v3b = v3.astype(BF)
    intra = []
    for hh in range(2):
        hm = lo if hh == 0 else ~lo
        lg_h = lg[:, hh * HEAD:hh * HEAD + 1]
        qm = jnp.where(hm, q3, 0.0).astype(BF)
        sc = jnp.einsum("nid,njd->nij", qm, k3.astype(BF), preferred_element_type=F32) * jnp.exp(lg_h * dist)[None]
        intra.append(jnp.einsum("nij,nje->nie", sc.astype(BF), v3b, preferred_element_type=F32))
    o = jnp.where(lo, intra[0], intra[1])
    kv = jnp.einsum("njd,nje->nde", (k3 * k_decay[None]).astype(BF), v3b, preferred_element_type=F32)
    kv = jnp.where(same_head[None], kv, 0.0)
    state, states = s_in, []
    for n in range(nc):
        states.append(state)
        state = state * chunk_decay + kv[n]
    s_prev = jnp.stack(states, axis=0)
    o = o + jnp.einsum("nid,nde->nie", (q3 * q_decay[None]).astype(BF), s_prev.astype(BF), preferred_element_type=F32)
    o = o.reshape(tb, LANES)
    mu = _gmean(o, lo)
    oc = o - mu
    y = oc * lax.rsqrt(_gmean(oc * oc, lo) + EPS) * g
    return jax.nn.silu(rg) * y, state


def _xattn_fn(qx, gq, gk, kk, vv):
    q = _rms(qx, gq)
    k = _rms(kk, gk)
    logits = lax.dot_general(q.astype(BF), k.astype(BF), NT, preferred_element_type=F32) * (XHEAD ** -0.5)
    p = jax.nn.softmax(logits, axis=-1)
    return (jnp.dot(p.astype(BF), vv.astype(BF), preferred_element_type=F32),)


def _swiglu_fwd_epi(g, u):
    return g, u, jax.nn.silu(g) * u


def _swiglu_bwd_epi(dact, g, u):
    _, vjp = jax.vjp(lambda a, b: jax.nn.silu(a) * b, g, u)
    return vjp(dact)


def _loss_fn(h, target):
    err = h - target
    part = jnp.sum(jnp.sum(err * err, axis=0, keepdims=True), axis=-1, keepdims=True) * (0.5 / D)
    dy = err * (1.0 / D)
    return dy, dy, part


def _ret_fwd(P, cos, sin, g_ret, lg, T, tb):
    nb = T // tb

    def body(rq, rk, rv, rg, c, s, g, l, o_ref, s0_ref, state):
        @pl.when(pl.program_id(1) == 0)
        def _():
            state[...] = jnp.zeros(state.shape, F32)
        s0_ref[0, 0] = state[...]
        out, s_new = _ret_fn(rq[...], rk[...], rv[...], rg[...], c[...], s[...], state[...], g[...], l[...])
        o_ref[...] = out
        state[...] = s_new

    sec = lambda off: pl.BlockSpec((tb, LANES), lambda j, i: (i, off + j))
    tab = pl.BlockSpec((tb, LANES), lambda j, i: (i, 0))
    par = pl.BlockSpec((1, LANES), lambda j, i: (0, j))
    return pl.pallas_call(
        body, name="ret_fwd", grid=(4, nb),
        in_specs=[sec(0), sec(4), sec(8), sec(12), tab, tab, par, par],
        out_specs=[pl.BlockSpec((tb, LANES), lambda j, i: (i, j)), pl.BlockSpec((1, 1, LANES, LANES), lambda j, i: (j, i, 0, 0))],
        out_shape=[jax.ShapeDtypeStruct((T, 4 * LANES), F32), jax.ShapeDtypeStruct((4, nb, LANES, LANES), F32)],
        scratch_shapes=[pltpu.VMEM((LANES, LANES), F32)],
        compiler_params=_params(("arbitrary", "arbitrary")),
    )(P, P, P, P, cos, sin, g_ret, lg)


def _ret_bwd(P, cos, sin, g_ret, lg, s0, dmix, T, tb):
    nb = T // tb

    def body(rq, rk, rv, rg, c, s, g, l, s0_ref, do, drq, drk, drv, drg, dg, dstate):
        i = pl.program_id(1)

        @pl.when(i == 0)
        def _():
            dstate[...] = jnp.zeros(dstate.shape, F32)
            dg[...] = jnp.zeros(dg.shape, F32)

        cc, ss, ll = c[...], s[...], l[...]
        _, vjp = jax.vjp(lambda a, b, v, gate, st, gg: _ret_fn(a, b, v, gate, cc, ss, st, gg, ll),
                         rq[...], rk[...], rv[...], rg[...], s0_ref[0, 0], g[...])
        ga, gb, gv, ggate, gst, ggain = vjp((do[...], dstate[...]))
        drq[...] = ga.astype(drq.dtype)
        drk[...] = gb.astype(drk.dtype)
        drv[...] = gv.astype(drv.dtype)
        drg[...] = ggate.astype(drg.dtype)
        dstate[...] = gst
        dg[...] += ggain

    rev = lambda i: nb - 1 - i
    sec = lambda off: pl.BlockSpec((tb, LANES), lambda j, i: (rev(i), off + j))
    tab = pl.BlockSpec((tb, LANES), lambda j, i: (rev(i), 0))
    par = pl.BlockSpec((1, LANES), lambda j, i: (0, j))
    outb = pl.BlockSpec((tb, LANES), lambda j, i: (rev(i), j))
    return pl.pallas_call(
        body, name="ret_bwd", grid=(4, nb),
        in_specs=[sec(0), sec(4), sec(8), sec(12), tab, tab, par, par,
                  pl.BlockSpec((1, 1, LANES, LANES), lambda j, i: (j, rev(i), 0, 0)), outb],
        out_specs=[outb, outb, outb, outb, par],
        out_shape=[jax.ShapeDtypeStruct((T, 4 * LANES), BF)] * 4 + [jax.ShapeDtypeStruct((1, 4 * LANES), F32)],
        scratch_shapes=[pltpu.VMEM((LANES, LANES), F32)],
        compiler_params=_params(("arbitrary", "arbitrary")),
    )(P, P, P, P, cos, sin, g_ret, lg, s0, dmix)


_FB = 128


def _tri(lower):
    r = lax.broadcasted_iota(jnp.int32, (_FB, _FB), 0)
    c = lax.broadcasted_iota(jnp.int32, (_FB, _FB), 1)
    return ((r >= c) if lower else (r <= c)).astype(F32)


def _fgate_fwd(ffp, bpad, T):
    def body(ff_ref, b_ref, fc_ref, fr_ref):
        lane = lax.broadcasted_iota(jnp.int32, (1, LANES), 1)
        tri = _tri(True)
        carry = jnp.zeros((1, LANES), F32)
        for blk in range(T // _FB):
            z = ff_ref[blk * _FB:(blk + 1) * _FB, :] + b_ref[...]
            lf = jnp.where(lane < 8, jax.nn.log_sigmoid(z), 0.0)
            f = jnp.dot(tri, lf, precision=lax.Precision.HIGHEST, preferred_element_type=F32) + carry
            carry = f[_FB - 1:_FB, :]
            fc_ref[blk * _FB:(blk + 1) * _FB, :] = f
            fr_ref[:, blk * _FB:(blk + 1) * _FB] = f.T[:8, :]

    return pl.pallas_call(
        body, name="fgate_fwd",
        out_shape=[jax.ShapeDtypeStruct((T, LANES), F32), jax.ShapeDtypeStruct((8, T), F32)],
        compiler_params=pltpu.CompilerParams(vmem_limit_bytes=VMEM_LIMIT),
    )(ffp, bpad)


def _fgate_bwd(ffp, bpad, dfr, T):
    def body(ff_ref, b_ref, dfr_ref, dff_ref, db_ref):
        lane = lax.broadcasted_iota(jnp.int32, (1, LANES), 1)
        tri = _tri(False)
        carry = jnp.zeros((1, LANES), F32)
        db = jnp.zeros((1, LANES), F32)
        for blk in reversed(range(T // _FB)):
            d8 = dfr_ref[:, blk * _FB:(blk + 1) * _FB]
            dcol = jnp.concatenate([d8, jnp.zeros((_FB - 8, _FB), F32)], axis=0).T
            dlf = jnp.dot(tri, dcol, precision=lax.Precision.HIGHEST, preferred_element_type=F32) + carry
            carry = dlf[0:1, :]
            z = ff_ref[blk * _FB:(blk + 1) * _FB, :] + b_ref[...]
            dz = jnp.where(lane < 8, dlf * jax.nn.sigmoid(-z), 0.0)
            dff_ref[blk * _FB:(blk + 1) * _FB, :] = dz.astype(dff_ref.dtype)
            db = db + jnp.sum(dz, axis=0, keepdims=True)
        db_ref[...] = db

    return pl.pallas_call(
        body, name="fgate_bwd",
        out_shape=[jax.ShapeDtypeStruct((T, LANES), BF), jax.ShapeDtypeStruct((1, LANES), F32)],
        compiler_params=pltpu.CompilerParams(vmem_limit_bytes=VMEM_LIMIT),
    )(ffp, bpad, dfr)


def _head_bias_col(fc, head):
    lane = lax.broadcasted_iota(jnp.int32, (1, LANES), 1)
    return jnp.sum(jnp.where(lane == head, fc, 0.0), axis=-1, keepdims=True)


def _head_bias_row(fr, head):
    sub = lax.broadcasted_iota(jnp.int32, (8, 1), 0)
    return jnp.sum(jnp.where(sub == head, fr, 0.0), axis=0, keepdims=True)


def _fox_fwd(qn, kn, vb, fc, fr, T, tq):
    nq = T // tq

    def body(q_ref, k_ref, v_ref, fc_ref, fr_ref, o_ref, c_ref):
        j, i = pl.program_id(0), pl.program_id(1)
        lane = lax.broadcasted_iota(jnp.int32, (1, LANES), 1)
        lo = lane < HEAD
        causal = lax.broadcasted_iota(jnp.int32, (tq, tq), 0) >= lax.broadcasted_iota(jnp.int32, (tq, tq), 1)
        q = q_ref[...]
        fcb = fc_ref[...]
        outs, cs = [], []
        for hh in range(2):
            hm = lo if hh == 0 else ~lo
            head = 2 * j + hh
            qh = jnp.where(hm, q, jnp.zeros_like(q))
            fq = _head_bias_col(fcb, head)

            def block(kb, carry, diag, qh=qh, fq=fq, head=head):
                m, l, acc = carry
                k0 = pl.multiple_of(kb * tq, tq)
                k = k_ref[pl.ds(k0, tq), :]
                v = v_ref[pl.ds(k0, tq), :]
                fk = _head_bias_row(fr_ref[:, pl.ds(k0, tq)], head)
                s = (lax.dot_general(qh, k, NT, preferred_element_type=F32) + fq) - fk
                if diag:
                    s = jnp.where(causal, s, NEG)
                m2 = jnp.maximum(m, jnp.max(s, axis=-1, keepdims=True))
                p = jnp.exp(s - m2)
                a = jnp.exp(m - m2)
                return m2, a * l + jnp.sum(p, axis=-1, keepdims=True), a * acc + jnp.dot(p.astype(BF), v, preferred_element_type=F32)

            init = (jnp.full((tq, 1), NEG, F32), jnp.zeros((tq, 1), F32), jnp.zeros((tq, LANES), F32))
            carry = lax.fori_loop(0, i, lambda kb, c: block(kb, c, False), init)
            m, l, acc = block(i, carry, True)
            outs.append(acc / l)
            cs.append(fq - (m + jnp.log(l)))
        o_ref[...] = jnp.where(lo, outs[0], outs[1])
        c_ref[0] = jnp.where(lane == 0, cs[0], jnp.where(lane == 1, cs[1], 0.0))

    full = lambda: pl.BlockSpec((T, LANES), lambda j, i: (0, j))
    return pl.pallas_call(
        body, name="fox_fwd", grid=(4, nq),
        in_specs=[pl.BlockSpec((tq, LANES), lambda j, i: (i, j)), full(), full(),
                  pl.BlockSpec((tq, LANES), lambda j, i: (i, 0)), pl.BlockSpec((8, T), lambda j, i: (0, 0))],
        out_specs=[pl.BlockSpec((tq, LANES), lambda j, i: (i, j)), pl.BlockSpec((1, tq, LANES), lambda j, i: (j, i, 0))],
        out_shape=[jax.ShapeDtypeStruct((T, 4 * LANES), F32), jax.ShapeDtypeStruct((4, T, LANES), F32)],
        compiler_params=_params(("parallel", "arbitrary")),
    )(qn, kn, vb, fc, fr)


def _fox_bwd_dq(qn, kn, vb, fr, cq, dmix, T, tq):
    nq = T // tq

    def body(q_ref, k_ref, v_ref, fr_ref, c_ref, do_ref, dq_ref, dl_ref, p_scr, dp_scr):
        j, i = pl.program_id(0), pl.program_id(1)
        lane = lax.broadcasted_iota(jnp.int32, (1, LANES), 1)
        lo = lane < HEAD
        causal = lax.broadcasted_iota(jnp.int32, (tq, tq), 0) >= lax.broadcasted_iota(jnp.int32, (tq, tq), 1)
        q, do, cb = q_ref[...], do_ref[...], c_ref[0]
        res, deltas = [], []
        for hh in range(2):
            hm = lo if hh == 0 else ~lo
            head = 2 * j + hh
            qh = jnp.where(hm, q, jnp.zeros_like(q))
            doh = jnp.where(hm, do, 0.0).astype(BF)
            c = cb[:, hh:hh + 1]

            def probs(kb, delta, diag, qh=qh, doh=doh, c=c, head=head):
                k0 = pl.multiple_of(kb * tq, tq)
                k = k_ref[pl.ds(k0, tq), :]
                v = v_ref[pl.ds(k0, tq), :]
                fk = _head_bias_row(fr_ref[:, pl.ds(k0, tq)], head)
                p = jnp.exp((lax.dot_general(qh, k, NT, preferred_element_type=F32) + c) - fk)
                if diag:
                    p = jnp.where(causal, p, 0.0)
                dp = lax.dot_general(doh, v, NT, preferred_element_type=F32)
                p_scr[:, pl.ds(k0, tq)] = p
                dp_scr[:, pl.ds(k0, tq)] = dp
                return delta + jnp.sum(p * dp, axis=-1, keepdims=True)

            delta = lax.fori_loop(0, i, lambda kb, d: probs(kb, d, False), jnp.zeros((tq, 1), F32))
            delta = probs(i, delta, True)

            def grad(kb, acc, delta=delta):
                k0 = pl.multiple_of(kb * tq, tq)
                ds = p_scr[:, pl.ds(k0, tq)] * (dp_scr[:, pl.ds(k0, tq)] - delta)
                return acc + jnp.dot(ds.astype(BF), k_ref[pl.ds(k0, tq), :], preferred_element_type=F32)

            res.append(lax.fori_loop(0, i + 1, grad, jnp.zeros((tq, LANES), F32)))
            deltas.append(delta)
        dq_ref[...] = jnp.where(lo, res[0], res[1])
        dl_ref[0] = jnp.where(lane == 0, deltas[0], jnp.where(lane == 1, deltas[1], 0.0))

    full = lambda: pl.BlockSpec((T, LANES), lambda j, i: (0, j))
    return pl.pallas_call(
        body, name="fox_bwd_dq", grid=(4, nq),
        in_specs=[pl.BlockSpec((tq, LANES), lambda j, i: (i, j)), full(), full(), pl.BlockSpec((8, T), lambda j, i: (0, 0)),
                  pl.BlockSpec((1, tq, LANES), lambda j, i: (j, i, 0)), pl.BlockSpec((tq, LANES), lambda j, i: (i, 4 + j))],
        out_specs=[pl.BlockSpec((tq, LANES), lambda j, i: (i, j)), pl.BlockSpec((1, tq, LANES), lambda j, i: (j, i, 0))],
        out_shape=[jax.ShapeDtypeStruct((T, 4 * LANES), F32), jax.ShapeDtypeStruct((4, T, LANES), F32)],
        scratch_shapes=[pltpu.VMEM((tq, T), F32), pltpu.VMEM((tq, T), F32)],
        compiler_params=_params(("parallel", "arbitrary")),
    )(qn, kn, vb, fr, cq, dmix)


def _fox_bwd_dkv(qn, kn, vb, fr, cq, dl, dmix, T, tq):
    nq = T // tq

    def body(q_ref, k_ref, v_ref, fr_ref, c_ref, dl_ref, do_ref, dk_ref, dv_ref, dfr_ref):
        j, kb = pl.program_id(0), pl.program_id(1)
        lo = _lo_mask()
        sub = lax.broadcasted_iota(jnp.int32, (8, 1), 0)
        causal = lax.broadcasted_iota(jnp.int32, (tq, tq), 0) >= lax.broadcasted_iota(jnp.int32, (tq, tq), 1)
        k, v, frb = k_ref[...], v_ref[...], fr_ref[...]
        dks, dvs, dfs = [], [], []
        for hh in range(2):
            hm = lo if hh == 0 else ~lo
            head = 2 * j + hh
            km = jnp.where(hm, k, jnp.zeros_like(k))
            vm = jnp.where(hm, v, jnp.zeros_like(v))
            fk = _head_bias_row(frb, head)

            def block(qi, carry, diag, km=km, vm=vm, fk=fk, hm=hm, hh=hh):
                dk, dv, df = carry
                q0 = pl.multiple_of(qi * tq, tq)
                q = q_ref[pl.ds(q0, tq), :]
                c = c_ref[0, pl.ds(q0, tq), :][:, hh:hh + 1]
                delta = dl_ref[0, pl.ds(q0, tq), :][:, hh:hh + 1]
                dob = do_ref[pl.ds(q0, tq), :].astype(BF)
                p = jnp.exp((lax.dot_general(q, km, NT, preferred_element_type=F32) + c) - fk)
                if diag:
                    p = jnp.where(causal, p, 0.0)
                dv = dv + lax.dot_general(p.astype(BF), dob, TN, preferred_element_type=F32)
                dp = lax.dot_general(dob, vm, NT, preferred_element_type=F32)
                ds = p * (dp - delta)
                dk = dk + lax.dot_general(ds.astype(BF), q, TN, preferred_element_type=F32)
                return dk, dv, df - jnp.sum(ds, axis=0, keepdims=True)

            init = (jnp.zeros((tq, LANES), F32), jnp.zeros((tq, LANES), F32), jnp.zeros((1, tq), F32))
            carry = block(kb, init, True)
            dk, dv, df = lax.fori_loop(kb + 1, nq, lambda qi, cr: block(qi, cr, False), carry)
            dks.append(dk)
            dvs.append(dv)
            dfs.append(df)
        dk_ref[...] = jnp.where(lo, dks[0], dks[1])
        dv_ref[...] = jnp.where(lo, dvs[0], dvs[1]).astype(dv_ref.dtype)
        dfr_ref[0] = jnp.where(sub == 0, dfs[0], jnp.where(sub == 1, dfs[1], 0.0))

    full = lambda off: pl.BlockSpec((T, LANES), lambda j, kb: (0, off + j))
    blk = lambda: pl.BlockSpec((tq, LANES), lambda j, kb: (kb, j))
    return pl.pallas_call(
        body, name="fox_bwd_dkv", grid=(4, nq),
        in_specs=[full(0), blk(), blk(), pl.BlockSpec((8, tq), lambda j, kb: (0, kb)),
                  pl.BlockSpec((1, T, LANES), lambda j, kb: (j, 0, 0)), pl.BlockSpec((1, T, LANES), lambda j, kb: (j, 0, 0)), full(4)],
        out_specs=[blk(), blk(), pl.BlockSpec((1, 8, tq), lambda j, kb: (j, 0, kb))],
        out_shape=[jax.ShapeDtypeStruct((T, 4 * LANES), F32), jax.ShapeDtypeStruct((T, 4 * LANES), BF),
                   jax.ShapeDtypeStruct((4, 8, T), F32)],
        compiler_params=_params(("parallel", "arbitrary")),
    )(qn, kn, vb, fr, cq, dl, dmix)


_BIAS_LANE = HEAD


def _split3(f):
    hi = f.astype(BF).astype(F32)
    mid = (f - hi).astype(BF).astype(F32)
    lo = ((f - hi) - mid).astype(BF).astype(F32)
    return hi, mid, lo


def _fox_operands(P, fc, g_fq2, g_fk2, T, tm):
    def body(fq_ref, fk_ref, fv_ref, fc_ref, gq_ref, gk_ref, qa_ref, qat_ref, ka_ref, kat_ref, va_ref, vat_ref):
        j = pl.program_id(0)
        lane = lax.broadcasted_iota(jnp.int32, (1, LANES), 1)
        qn, kn = _fox_prep_fn(fq_ref[...], fk_ref[...], gq_ref[...], gk_ref[...])
        v = fv_ref[...]
        fcb = fc_ref[...]
        b = _BIAS_LANE
        for hh in range(2):
            hi, mid, lo = _split3(_head_bias_col(fcb, 2 * j + hh))
            take = (lambda a: a) if hh == 0 else (lambda a: pltpu.roll(a, HEAD, 1))
            qa = jnp.where(lane < HEAD, take(qn), jnp.where(lane == b, hi, jnp.where(lane == b + 1, mid, jnp.where(
                lane == b + 2, lo, jnp.where(lane < b + 6, 1.0, 0.0)))))
            ka = jnp.where(lane < HEAD, take(kn), jnp.where(lane < b + 3, 1.0, jnp.where(lane == b + 3, -hi, jnp.where(
                lane == b + 4, -mid, jnp.where(lane == b + 5, -lo, 0.0)))))
            va = jnp.where(lane < HEAD, take(v), 0.0)
            for val, ref, tref in ((qa, qa_ref, qat_ref), (ka, ka_ref, kat_ref), (va, va_ref, vat_ref)):
                ref[hh] = val.astype(BF)
                tref[hh] = val.T.astype(BF)

    sec = lambda off: pl.BlockSpec((tm, LANES), lambda j, i: (i, off + j))
    par = pl.BlockSpec((1, LANES), lambda j, i: (0, 0))
    nat = pl.BlockSpec((2, tm, LANES), lambda j, i: (j, i, 0))
    trn = pl.BlockSpec((2, LANES, tm), lambda j, i: (j, 0, i))
    return pl.pallas_call(
        body, name="fox_operands", grid=(4, T // tm),
        in_specs=[sec(16), sec(20), sec(24), pl.BlockSpec((tm, LANES), lambda j, i: (i, 0)), par, par],
        out_specs=[nat, trn, nat, trn, nat, trn],
        out_shape=[jax.ShapeDtypeStruct((8, T, LANES), BF), jax.ShapeDtypeStruct((8, LANES, T), BF)] * 3,
        compiler_params=_params(("parallel", "arbitrary")),
    )(P, P, P, fc, g_fq2, g_fk2)


def _fox_forward(qat, ka, vat, T, tq, tk):
    nq, per = T // tq, tq // tk

    def body(qat_ref, ka_ref, vat_ref, o_ref, lse_ref):
        i = pl.program_id(1)
        sub = lax.broadcasted_iota(jnp.int32, (8, 1), 0)
        krow = lax.broadcasted_iota(jnp.int32, (tk, tq), 0)
        qcol = lax.broadcasted_iota(jnp.int32, (tk, tq), 1)

        def scores(kb):
            k0 = pl.multiple_of(kb * tk, tk)
            return tuple(jnp.dot(ka_ref[hh, pl.ds(k0, tk), :], qat_ref[hh], preferred_element_type=F32) for hh in range(2))

        def step(kb, carry, mask, last=False):
            stats, s_now = carry
            s_next = s_now if last else scores(kb + 1)
            k0 = pl.multiple_of(kb * tk, tk)
            new = []
            for hh in range(2):
                m, l, acc = stats[hh]
                s = s_now[hh] if mask is None else jnp.where(mask, s_now[hh], NEG)
                m2 = jnp.maximum(m, jnp.max(s, axis=0, keepdims=True))
                p = jnp.exp(s - m2)
                a = jnp.exp(m - m2)
                pv = jnp.dot(vat_ref[hh, 0:HEAD, pl.ds(k0, tk)], p.astype(BF), preferred_element_type=F32)
                new.append((m2, a * l + jnp.sum(p, axis=0, keepdims=True), a * acc + pv))
            return tuple(new), s_next

        one = (jnp.full((1, tq), NEG, F32), jnp.zeros((1, tq), F32), jnp.zeros((HEAD, tq), F32))
        carry = lax.fori_loop(0, i * per, lambda kb, c: step(kb, c, None), ((one, one), scores(0)))
        for d in range(per):
            carry = step(i * per + d, carry, krow + d * tk <= qcol, last=(d == per - 1))
        stats = carry[0]
        o_ref[...] = jnp.concatenate([acc / l for _, l, acc in stats], axis=0).T
        lses = [m + jnp.log(l) for m, l, _ in stats]
        lse_ref[0] = jnp.where(sub == 0, lses[0], jnp.where(sub == 1, lses[1], 0.0))

    return pl.pallas_call(
        body, name="fox_forward", grid=(4, nq),
        in_specs=[pl.BlockSpec((2, LANES, tq), lambda j, i: (j, 0, i)), pl.BlockSpec((2, T, LANES), lambda j, i: (j, 0, 0)),
                  pl.BlockSpec((2, LANES, T), lambda j, i: (j, 0, 0))],
        out_specs=[pl.BlockSpec((tq, LANES), lambda j, i: (i, j)), pl.BlockSpec((1, 8, tq), lambda j, i: (j, 0, i))],
        out_shape=[jax.ShapeDtypeStruct((T, 4 * LANES), F32), jax.ShapeDtypeStruct((4, 8, T), F32)],
        compiler_params=_params(("parallel", "arbitrary")),
    )(qat, ka, vat)


def _fox_cotangent(dmix, fox, T, tm):
    def body(do_ref, o_ref, doa_ref, doat_ref, dl_ref):
        lane = lax.broadcasted_iota(jnp.int32, (1, LANES), 1)
        sub = lax.broadcasted_iota(jnp.int32, (8, 1), 0)
        dob = do_ref[...].astype(BF).astype(F32)
        prod_t = (dob * o_ref[...]).T
        d0 = jnp.sum(prod_t[:HEAD], axis=0, keepdims=True)
        d1 = jnp.sum(prod_t[HEAD:], axis=0, keepdims=True)
        dl_ref[0] = jnp.where(sub == 0, d0, jnp.where(sub == 1, d1, 0.0))
        for hh in range(2):
            val = jnp.where(lane < HEAD, dob if hh == 0 else pltpu.roll(dob, HEAD, 1), 0.0)
            doa_ref[hh] = val.astype(BF)
            doat_ref[hh] = val.T.astype(BF)

    return pl.pallas_call(
        body, name="fox_cotangent", grid=(4, T // tm),
        in_specs=[pl.BlockSpec((tm, LANES), lambda j, i: (i, 4 + j)), pl.BlockSpec((tm, LANES), lambda j, i: (i, j))],
        out_specs=[pl.BlockSpec((2, tm, LANES), lambda j, i: (j, i, 0)), pl.BlockSpec((2, LANES, tm), lambda j, i: (j, 0, i)),
                   pl.BlockSpec((1, 8, tm), lambda j, i: (j, 0, i))],
        out_shape=[jax.ShapeDtypeStruct((8, T, LANES), BF), jax.ShapeDtypeStruct((8, LANES, T), BF),
                   jax.ShapeDtypeStruct((4, 8, T), F32)],
        compiler_params=_params(("parallel", "arbitrary")),
    )(dmix, fox)


def _fox_backward(qa, qat, ka, kat, va, doa, doat, lse, dl, T, tq):
    nq = T // tq

    def body(qa_ref, qat_ref, ka_ref, kat_ref, va_ref, doa_ref, doat_ref, lse_ref, dl_ref,
             dq_ref, dk_ref, dv_ref, df_ref, dr_ref, dqt, dk_acc, dv_acc, df_acc):
        j, kb = pl.program_id(0), pl.program_id(1)
        lane = lax.broadcasted_iota(jnp.int32, (1, LANES), 1)
        mask = lax.broadcasted_iota(jnp.int32, (tq, tq), 0) <= lax.broadcasted_iota(jnp.int32, (tq, tq), 1)

        @pl.when(kb == 0)
        def _():
            dqt[...] = jnp.zeros(dqt.shape, F32)

        dk_acc[...] = jnp.zeros(dk_acc.shape, F32)
        dv_acc[...] = jnp.zeros(dv_acc.shape, F32)
        df_acc[...] = jnp.zeros(df_acc.shape, F32)

        def products(qi):
            q0 = pl.multiple_of(qi * tq, tq)
            return tuple((jnp.dot(ka_ref[hh], qat_ref[hh, :, pl.ds(q0, tq)], preferred_element_type=F32),
                          jnp.dot(va_ref[hh], doat_ref[hh, :, pl.ds(q0, tq)], preferred_element_type=F32)) for hh in range(2))

        def step(qi, now, diag):
            ahead = products(jnp.minimum(qi + 1, nq - 1))
            q0 = pl.multiple_of(qi * tq, tq)
            for hh in range(2):
                s, dp = now[hh]
                p = jnp.exp(s - lse_ref[0, hh:hh + 1, pl.ds(q0, tq)])
                if diag:
                    p = jnp.where(mask, p, 0.0)
                ds = p * (dp - dl_ref[0, hh:hh + 1, pl.ds(q0, tq)])
                pb, dsb = p.astype(BF), ds.astype(BF)
                dv_acc[hh] += jnp.dot(pb, doa_ref[hh, pl.ds(q0, tq), :], preferred_element_type=F32)
                dk_acc[hh] += jnp.dot(dsb, qa_ref[hh, pl.ds(q0, tq), :], preferred_element_type=F32)
                dqt[hh, 0:HEAD, pl.ds(q0, tq)] += jnp.dot(kat_ref[hh, 0:HEAD, :], dsb, preferred_element_type=F32)
                dqt[hh, HEAD:HEAD + 8, pl.ds(q0, tq)] += jnp.broadcast_to(jnp.sum(ds, axis=0, keepdims=True), (8, tq))
                part = ds[:, 0:LANES]
                for c in range(1, tq // LANES):
                    part = part + ds[:, c * LANES:(c + 1) * LANES]
                df_acc[hh] += part
            return ahead

        lax.fori_loop(kb + 1, nq, lambda qi, now: step(qi, now, False), step(kb, products(kb), True))

        lo = lane < HEAD
        dk_ref[...] = jnp.where(lo, dk_acc[0], pltpu.roll(dk_acc[1], HEAD, 1))
        dv_ref[...] = jnp.where(lo, dv_acc[0], pltpu.roll(dv_acc[1], HEAD, 1)).astype(dv_ref.dtype)
        f0 = -jnp.sum(df_acc[0], axis=1, keepdims=True)
        f1 = -jnp.sum(df_acc[1], axis=1, keepdims=True)
        df_ref[0] = jnp.where(lane == 2 * j, f0, jnp.where(lane == 2 * j + 1, f1, 0.0))

        @pl.when(kb == nq - 1)
        def _():
            for t in range(nq):
                cols = slice(t * tq, (t + 1) * tq)
                dq_ref[cols, :] = jnp.concatenate([dqt[0, 0:HEAD, cols], dqt[1, 0:HEAD, cols]], axis=0).T
                rsum = jnp.concatenate([dqt[0, HEAD:HEAD + 8, cols], dqt[1, HEAD:HEAD + 8, cols],
                                        jnp.zeros((LANES - 16, tq), F32)], axis=0).T
                dr_ref[0, cols, :] = jnp.where(lane == 2 * j, rsum[:, 0:1], jnp.where(lane == 2 * j + 1, rsum[:, 8:9], 0.0))

    nat_full = pl.BlockSpec((2, T, LANES), lambda j, kb: (j, 0, 0))
    trn_full = pl.BlockSpec((2, LANES, T), lambda j, kb: (j, 0, 0))
    nat_blk = pl.BlockSpec((2, tq, LANES), lambda j, kb: (j, kb, 0))
    trn_blk = pl.BlockSpec((2, LANES, tq), lambda j, kb: (j, 0, kb))
    rows = pl.BlockSpec((1, 8, T), lambda j, kb: (j, 0, 0))
    blk = pl.BlockSpec((tq, LANES), lambda j, kb: (kb, j))
    return pl.pallas_call(
        body, name="fox_backward", grid=(4, nq),
        in_specs=[nat_full, trn_full, nat_blk, trn_blk, nat_blk, nat_full, trn_full, rows, rows],
        out_specs=[pl.BlockSpec((T, LANES), lambda j, kb: (0, j)), blk, blk, pl.BlockSpec((1, tq, LANES), lambda j, kb: (j, kb, 0)),
                   pl.BlockSpec((1, T, LANES), lambda j, kb: (j, 0, 0))],
        out_shape=[jax.ShapeDtypeStruct((T, 4 * LANES), F32), jax.ShapeDtypeStruct((T, 4 * LANES), F32),
                   jax.ShapeDtypeStruct((T, 4 * LANES), BF), jax.ShapeDtypeStruct((4, T, LANES), F32),
                   jax.ShapeDtypeStruct((4, T, LANES), F32)],
        scratch_shapes=[pltpu.VMEM((2, HEAD + 8, T), F32), pltpu.VMEM((2, tq, LANES), F32), pltpu.VMEM((2, tq, LANES), F32),
                        pltpu.VMEM((2, tq, LANES), F32)],
        compiler_params=_params(("arbitrary", "arbitrary")),
    )(qa, qat, ka, kat, va, doa, doat, lse, dl)


def _fgate_bwd_col(ffp, bpad, dfc, T):
    def body(ff_ref, b_ref, dfc_ref, dff_ref, db_ref):
        lane = lax.broadcasted_iota(jnp.int32, (1, LANES), 1)
        tri = _tri(False)
        carry = jnp.zeros((1, LANES), F32)
        db = jnp.zeros((1, LANES), F32)
        for blk in reversed(range(T // _FB)):
            dlf = jnp.dot(tri, dfc_ref[blk * _FB:(blk + 1) * _FB, :], precision=lax.Precision.HIGHEST,
                          preferred_element_type=F32) + carry
            carry = dlf[0:1, :]
            z = ff_ref[blk * _FB:(blk + 1) * _FB, :] + b_ref[...]
            dz = jnp.where(lane < 8, dlf * jax.nn.sigmoid(-z), 0.0)
            dff_ref[blk * _FB:(blk + 1) * _FB, :] = dz.astype(dff_ref.dtype)
            db = db + jnp.sum(dz, axis=0, keepdims=True)
        db_ref[...] = db

    return pl.pallas_call(
        body, name="fgate_bwd",
        out_shape=[jax.ShapeDtypeStruct((T, LANES), BF), jax.ShapeDtypeStruct((1, LANES), F32)],
        compiler_params=pltpu.CompilerParams(vmem_limit_bytes=VMEM_LIMIT),
    )(ffp, bpad, dfc)


MESH = pl.DeviceIdType.MESH


def _place():
    return lax.axis_index("x"), lax.axis_index("y"), lax.axis_index("c")


def _all_gather(shard):
    R, W = shard.shape

    def body(x_ref, out_ref, send_sems, recv_sems, local_sem):
        x, y, c = _place()
        me, sibling = (x, y, c), (x, y, 1 - c)
        chips = [(1 - x, y), (x, 1 - y), (1 - x, 1 - y)]

        def slot(px, py, pc):
            return out_ref.at[4 * px + 2 * py + pc]

        def copy(k, block, to, src=None):
            return pltpu.make_async_remote_copy(
                src_ref=slot(*block) if src is None else src, dst_ref=slot(*block),
                send_sem=send_sems.at[k], recv_sem=recv_sems.at[k], device_id=to, device_id_type=MESH)

        mine = pltpu.make_async_copy(x_ref, slot(*me), local_sem)
        mine.start()
        first = [copy(0, me, sibling, src=x_ref)]
        first += [copy(1 + n, me, (*chip, c), src=x_ref) for n, chip in enumerate(chips)]
        for cp in first:
            cp.start()
        passed = [copy(4 + n, (*chip, c), sibling) for n, chip in enumerate(chips)]
        for n, chip in enumerate(chips):
            copy(1 + n, (*chip, c), me).wait_recv()
            passed[n].start()
        copy(0, sibling, me).wait_recv()
        for n, chip in enumerate(chips):
            copy(4 + n, (*chip, 1 - c), me).wait_recv()
        for cp in first + passed:
            cp.wait_send()
        mine.wait()

    return pl.pallas_call(
        body, name="all_gather_weights",
        out_shape=jax.ShapeDtypeStruct((N_DEV, R, W), shard.dtype),
        in_specs=[pl.BlockSpec(memory_space=pl.ANY)], out_specs=pl.BlockSpec(memory_space=pl.ANY),
        scratch_shapes=[pltpu.SemaphoreType.DMA((7,)), pltpu.SemaphoreType.DMA((7,)), pltpu.SemaphoreType.DMA],
    )(shard)


def _all_to_all(big, small):
    def body(big_ref, small_ref, rbig_ref, rsmall_ref, send_sems, recv_sems, local_sems):
        x, y, c = _place()
        me = 4 * x + 2 * y + c
        l0 = pltpu.make_async_copy(big_ref.at[me], rbig_ref.at[me], local_sems.at[0])
        l1 = pltpu.make_async_copy(small_ref, rsmall_ref.at[me], local_sems.at[1])
        l0.start()
        l1.start()
        copies = []
        for r in range(1, N_DEV):
            px, py, pc = x ^ (r >> 2), y ^ ((r >> 1) & 1), c ^ (r & 1)
            peer = 4 * px + 2 * py + pc
            copies.append(pltpu.make_async_remote_copy(
                src_ref=big_ref.at[peer], dst_ref=rbig_ref.at[me], send_sem=send_sems.at[2 * r], recv_sem=recv_sems.at[2 * r],
                device_id=(px, py, pc), device_id_type=MESH))
            copies.append(pltpu.make_async_remote_copy(
                src_ref=small_ref, dst_ref=rsmall_ref.at[me], send_sem=send_sems.at[2 * r + 1], recv_sem=recv_sems.at[2 * r + 1],
                device_id=(px, py, pc), device_id_type=MESH))
        for cp in copies:
            cp.start()
        for cp in copies:
            cp.wait_recv()
        for cp in copies:
            cp.wait_send()
        l0.wait()
        l1.wait()

    return pl.pallas_call(
        body, name="all_to_all_grads",
        out_shape=[jax.ShapeDtypeStruct(big.shape, big.dtype), jax.ShapeDtypeStruct((N_DEV,) + small.shape, small.dtype)],
        in_specs=[pl.BlockSpec(memory_space=pl.ANY)] * 2, out_specs=[pl.BlockSpec(memory_space=pl.ANY)] * 2,
        scratch_shapes=[pltpu.SemaphoreType.DMA((2 * N_DEV,)), pltpu.SemaphoreType.DMA((2 * N_DEV,)), pltpu.SemaphoreType.DMA((2,))],
    )(big, small)


def _exchange_copies(src_ref, land_ref, send_sems, recv_sems, scatter):
    x, y, c = _place()
    me = 4 * x + 2 * y + c
    copies = []
    for r in range(1, N_DEV):
        px, py, pc = x ^ (r >> 2), y ^ ((r >> 1) & 1), c ^ (r & 1)
        copies.append(pltpu.make_async_remote_copy(
            src_ref=src_ref.at[4 * px + 2 * py + pc] if scatter else src_ref, dst_ref=land_ref.at[me],
            send_sem=send_sems.at[r - 1], recv_sem=recv_sems.at[r - 1], device_id=(px, py, pc), device_id_type=MESH))
    return copies


_HBM = pl.BlockSpec(memory_space=pltpu.HBM)
_SEM = pl.BlockSpec(memory_space=pltpu.SEMAPHORE)
_EFFECT = pltpu.SideEffectType.DATAFLOW_SIDE_EFFECTING


def _exchange_start(name, src, land, scatter):
    def body(src_ref, land_ref, send_sems, recv_sems, src_thru, land_thru, token):
        for cp in _exchange_copies(src_ref, land_ref, send_sems, recv_sems, scatter):
            cp.start()
        token[...] = jnp.zeros(token.shape, F32)

    return pl.pallas_call(
        body, name=name,
        out_shape=(pltpu.SemaphoreType.DMA((N_DEV - 1,)), pltpu.SemaphoreType.DMA((N_DEV - 1,)),
                   pltpu.HBM(src.shape, src.dtype), pltpu.HBM(land.shape, land.dtype), jax.ShapeDtypeStruct((8, LANES), F32)),
        in_specs=(_HBM, _HBM), out_specs=(_SEM, _SEM, _HBM, _HBM, pl.BlockSpec(memory_space=pltpu.VMEM)),
        input_output_aliases={0: 2, 1: 3},
        compiler_params=pltpu.CompilerParams(has_side_effects=_EFFECT),
    )(pltpu.with_memory_space_constraint(src, pltpu.HBM), pltpu.with_memory_space_constraint(land, pltpu.HBM))


def _exchange_wait(name, started, after, scatter):
    send_sems, recv_sems, src_thru, land_thru, _ = started

    def body(src_ref, land_ref, send_sems, recv_sems, after_ref, src_dead, got_ref):
        copies = _exchange_copies(src_ref, land_ref, send_sems, recv_sems, scatter)
        for cp in copies:
            cp.wait_send()
        for cp in copies:
            cp.wait_recv()

    return pl.pallas_call(
        body, name=name,
        out_shape=(pltpu.HBM(src_thru.shape, src_thru.dtype), pltpu.HBM(land_thru.shape, land_thru.dtype)),
        in_specs=(_HBM, _HBM, _SEM, _SEM, pl.BlockSpec(memory_space=pl.ANY)), out_specs=(_HBM, _HBM),
        input_output_aliases={0: 0, 1: 1},
        compiler_params=pltpu.CompilerParams(has_side_effects=_EFFECT),
    )(src_thru, land_thru, send_sems, recv_sems, after)


def _adamw(name, slots, w, m, v, tr, own=None):
    R, W = w.shape

    def body(s_ref, *refs):
        if own is not None:
            own_ref, refs = refs[0], refs[1:]
        w_ref, m_ref, v_ref, g_ref, d_ref, nm_ref, nv_ref = refs
        g = s_ref[0].astype(F32)
        for s in range(1, N_DEV):
            g = g + s_ref[s].astype(F32)
        if own is not None:
            g = g + own_ref[...].astype(F32)
        m2 = ADAM_B1 * m_ref[...] + (1.0 - ADAM_B1) * g
        v2 = ADAM_B2 * v_ref[...] + (1.0 - ADAM_B2) * jnp.square(g)
        m_hat = m2 / (1.0 - ADAM_B1 ** ADAM_STEP)
        v_hat = v2 / (1.0 - ADAM_B2 ** ADAM_STEP)
        g_ref[...] = g
        d_ref[...] = -ADAM_LR * (m_hat / (jnp.sqrt(v_hat) + ADAM_EPS) + ADAM_WD * w_ref[...])
        nm_ref[...] = m2
        nv_ref[...] = v2

    row = lambda: pl.BlockSpec((tr, W), lambda i: (i, 0))
    return pl.pallas_call(
        body, name=name, grid=(R // tr,),
        in_specs=[pl.BlockSpec((N_DEV, tr, W), lambda i: (0, i, 0))] + [row() for _ in range(3 + (own is not None))],
        out_specs=[row(), row(), row(), row()],
        out_shape=[jax.ShapeDtypeStruct((R, W), F32)] * 4,
        compiler_params=_params(("parallel",)),
    )(slots, *([own] if own is not None else []), w, m, v)


def _tables(T):
    pos = jnp.arange(T, dtype=F32)
    inv_freq = 10000.0 ** (-jnp.arange(0, HEAD, 2, dtype=F32) / HEAD)
    ang = pos[:, None] * inv_freq[None, :]
    cos, sin = jnp.cos(ang), jnp.sin(ang)
    cos4 = jnp.tile(cos, (1, 4))
    sin4 = jnp.tile(jnp.concatenate([-sin, sin], axis=1), (1, 2))
    log_g = jnp.log(1.0 - 2.0 ** (-5.0 - jnp.arange(8, dtype=F32)))
    return cos4, sin4, jnp.repeat(log_g, HEAD)[None, :]


def _local_step(x, mem, target, sp, w_inT, token, fetch_rest, push):
    T = x.shape[0]
    tm = min(512, T)
    tq = min(256, T)
    tb = min(1024, T)
    cos4, sin4, lg = _tables(T)
    g_fq2 = jnp.tile(sp["g_fox_q"], (1, 2))
    g_fk2 = jnp.tile(sp["g_fox_k"], (1, 2))
    g_ret = sp["g_ret_out"].reshape(1, 8 * HEAD)
    bpad = jnp.pad(sp["b_forget"], ((0, 0), (0, LANES - 8)))
    w_secs = [w_inT[k * 512:(k + 1) * 512] for k in range(7)]
    w_ffT = jnp.pad(w_inT[3584:3592], ((0, LANES - 8), (0, 0)))
    w_mainT = w_inT[:3584]
    tie = lambda p, tok: p + tok[0:1, 0:1]
    tm2, tm4 = min(1024, T), min(2048, T)

    hn1, = _rw_fwd("rms_mix", _rms_fn, [(x, D, 0, False)], [(tie(sp["g_mix"], token), D, 0, False)], [(BF, D)], T, tm, 1)
    P, = _mm("proj_in", [[(hn1, w_mainT, "nt")]], [], _ident, T, 3584, tm4, 512, [F32])
    ffp, = _mm("proj_ff", [[(hn1, w_ffT, "nt")]], [], _ident, T, LANES, tm, LANES, [F32])
    ret, s0 = _ret_fwd(P, cos4, sin4, g_ret, lg, T, tb)
    fc, _ = _fgate_fwd(ffp, bpad, T)
    qa, qat, ka, kat, va, vat = _fox_operands(P, fc, g_fq2, g_fk2, T, tm)
    fox, lse = _fox_forward(qat, ka, vat, T, tq, min(128, T))
    W = fetch_rest(fox)
    w_out_halves = (W["w_out"][:4 * LANES], W["w_out"][4 * LANES:])
    h1, = _mm("proj_out", [[(ret, w_out_halves[0], "nn"), (fox, w_out_halves[1], "nn")]], [x], _add, T, D, tm2, D, [F32])

    hn2, = _rw_fwd("rms_xattn", _rms_fn, [(h1, D, 0, False)], [(sp["g_xattn"], D, 0, False)], [(BF, D)], T, tm, 1)
    qx, = _mm("proj_xq", [[(hn2, W["w_xq"], "nn")]], [], _ident, T, D, tm2, D, [F32])
    memn, = _rw_fwd("rms_mem", _rms_fn, [(mem, D, 0, False)], [(sp["g_mem"], D, 0, False)], [(BF, D)], N_MEM, N_MEM, 1)
    kv, = _mm("proj_xkv", [[(memn, W["w_xkvT"], "nt")]], [], _ident, N_MEM, 2 * D, N_MEM, 512, [F32])
    xa_rows = [(qx, XHEAD, 0, True)]
    xa_params = [(sp["g_xq"], XHEAD, 0, False), (sp["g_xk"], XHEAD, 0, False), (kv, XHEAD, 0, True), (kv, XHEAD, 4, True)]
    xo, = _rw_fwd("xattn_fwd", _xattn_fn, xa_rows, xa_params, [(BF, XHEAD)], T, tm, 4)
    h2, = _mm("proj_xo", [[(xo, W["w_xo"], "nn")]], [h1], _add, T, D, tm2, D, [F32])

    hn3, = _rw_fwd("rms_ffn", _rms_fn, [(h2, D, 0, False)], [(sp["g_ffn"], D, 0, False)], [(BF, D)], T, tm, 1)
    gate, up, act = _mm("ffn_in", [[(hn3, W["w_gateT"], "nt")], [(hn3, W["w_upT"], "nt")]], [], _swiglu_fwd_epi,
                        T, D_FF, tm4, 256, [BF, BF, BF])
    h3, = _mm("ffn_out", [[(act, W["w_down"], "nn")]], [h2], _add, T, D, tm, D, [F32])
    dy, dyb, loss_part = _rw_fwd("loss", _loss_fn, [(h3, D, 0, False), (target, D, 0, False)], [], [(F32, D), (BF, D)], T, tm, 1,
                                 n_acc=1)

    dgate, dup = _mm("ffn_out_bwd", [[(dyb, W["w_down"], "nt")]], [gate, up], _swiglu_bwd_epi, T, D_FF, tm4, 256, [BF, BF])
    dhn3, = _mm("ffn_in_bwd", [[(dgate, W["w_gateT"], "nn"), (dup, W["w_upT"], "nn")]], [], _ident, T, D, tm, D, [F32])
    gW = {}
    gW["w_gateT"], = _mm("dw_gate", [[(dgate, hn3, "tn")]], [], _ident, D_FF, D, 256, D, [BF])
    gW["w_upT"], = _mm("dw_up", [[(dup, hn3, "tn")]], [], _ident, D_FF, D, 256, D, [BF])
    gW["w_down"], = _mm("dw_down", [[(act, dyb, "tn")]], [], _ident, D_FF, D, 256, D, [BF])
    tok = push("ffn", gW)
    gs = {}
    dh2, dh2b, gs["g_ffn"] = _rw_bwd("rms_ffn_bwd", _rms_fn, [(h2, D, 0, False)], [(tie(sp["g_ffn"], tok), D, 0, False)],
                                     [(dhn3, D, 0, False)], T, tm, 1, [[F32, BF]], [True], resid=dy)

    dxo, = _mm("proj_xo_bwd", [[(dh2b, W["w_xo"], "nt")]], [], _ident, T, D, tm2, D, [BF])
    gW["w_xo"], = _mm("dw_xo", [[(xo, dh2b, "tn")]], [], _ident, D, D, 256, D, [BF])
    dqx, gs["g_xq"], gs["g_xk"], dkv_k, dkv_v = _rw_bwd(
        "xattn_bwd", _xattn_fn, xa_rows, xa_params, [(dxo, XHEAD, 0, True)], T, tm, 4, [BF], [True, True, True, True])
    dkv = jnp.concatenate([dkv_k[:, :D], dkv_v[:, D:]], axis=1)
    dhn2, = _mm("proj_xq_bwd", [[(dqx, W["w_xq"], "nt")]], [], _ident, T, D, tm2, D, [F32])
    gW["w_xq"], = _mm("dw_xq", [[(hn2, dqx, "tn")]], [], _ident, D, D, 256, D, [BF])
    dmemn, = _mm("proj_xkv_bwd", [[(dkv, W["w_xkvT"], "nn")]], [], _ident, N_MEM, D, N_MEM, 512, [F32])
    gW["w_xkvT"], = _mm("dw_xkv", [[(dkv, memn, "tn")]], [], _ident, 2 * D, D, 512, D, [BF])
    tok = push("xattn", gW)
    gs["g_mem"], = _rw_bwd("rms_mem_bwd", _rms_fn, [(mem, D, 0, False)], [(sp["g_mem"], D, 0, False)], [(dmemn, D, 0, False)],
                           N_MEM, N_MEM, 1, [None], [True])
    dh1, dh1b, gs["g_xattn"] = _rw_bwd("rms_xattn_bwd", _rms_fn, [(h1, D, 0, False)], [(tie(sp["g_xattn"], tok), D, 0, False)],
                                       [(dhn2, D, 0, False)], T, tm, 1, [[F32, BF]], [True], resid=dh2)

    dmix, = _mm("proj_out_bwd", [[(dh1b, W["w_out"], "nt")]], [], _ident, T, D, tm2, D, [F32])
    gW["w_out"] = jnp.concatenate([_mm("dw_out_%d" % k, [[(a, dh1b, "tn")]], [], _ident, 4 * LANES, D, 256, D, [BF])[0]
                                   for k, a in enumerate((ret, fox))], axis=0)
    tok = push("out", gW)
    doa, doat, dl = _fox_cotangent(dmix, fox, T, tm)
    dqn, dkn, dfv, dfc4, drc4 = _fox_backward(qa, qat, ka, kat, va, doa, doat, lse + tok[0:1, 0:1], dl, T, tq)
    dfq, dfk, gq2, gk2 = _rw_bwd("fox_prep_bwd", _fox_prep_fn, [(P, LANES, 16, True), (P, LANES, 20, True)],
                                 [(g_fq2, LANES, 0, False), (g_fk2, LANES, 0, False)],
                                 [(dqn, LANES, 0, True), (dkn, LANES, 0, True)], T, tm, 4, [BF, BF], [True, True])
    gs["g_fox_q"] = gq2[:, :HEAD] + gq2[:, HEAD:]
    gs["g_fox_k"] = gk2[:, :HEAD] + gk2[:, HEAD:]
    dff, dbp = _fgate_bwd_col(ffp, bpad, jnp.sum(dfc4 + drc4, axis=0), T)
    gs["b_forget"] = dbp[:, :8]
    drq, drk, drv, drg, dg_ret = _ret_bwd(P, cos4, sin4, g_ret, lg, s0, dmix, T, tb)
    gs["g_ret_out"] = dg_ret
    dsecs = [drq, drk, drv, drg, dfq, dfk, dfv]
    dhn1, = _mm("proj_in_bwd", [[(d, w, "nn") for d, w in zip(dsecs, w_secs)] + [(dff, w_ffT, "nn")]], [], _ident,
                T, D, tm, D, [F32])
    g_secs = [_mm("dw_in_%d" % k, [[(d, hn1, "tn")]], [], _ident, 512, D, 256, D, [BF])[0] for k, d in enumerate(dsecs)]
    g_ff, = _mm("dw_in_ff", [[(dff, hn1, "tn")]], [], _ident, LANES, D, LANES, D, [BF])
    gW["w_inT"] = jnp.concatenate(g_secs + [g_ff[:8]], axis=0)
    tok = push("in", gW)
    grad_x, gs["g_mix"] = _rw_bwd("rms_mix_bwd", _rms_fn, [(x, D, 0, False)], [(tie(sp["g_mix"], tok), D, 0, False)],
                                  [(dhn1, D, 0, False)], T, tm, 1, [F32], [True], resid=dh1)
    return loss_part, grad_x, gs


_CANON = {"w_in": "w_inT", "w_xkv": "w_xkvT", "w_gate": "w_gateT", "w_up": "w_upT"}
_SMALL = (("g_mix", 0, 0, 1024), ("g_xattn", 1, 0, 1024), ("g_mem", 2, 0, 1024), ("g_ffn", 3, 0, 1024),
          ("g_ret_out", 4, 0, 512), ("g_xq", 4, 512, 256), ("g_xk", 4, 768, 256),
          ("g_fox_q", 5, 0, 64), ("g_fox_k", 5, 64, 64), ("b_forget", 5, 128, 8))
_LOSS_AT = (5, 256)


def _pack_shards(tree, dtype):
    parts = []
    for name, rows, padded, transposed in W_LAYOUT:
        a = tree[name][0]
        a = a.T if transposed else a
        parts.append(jnp.pad(a, ((0, padded - rows), (0, 0))).astype(dtype))
    return jnp.concatenate(parts, axis=0)


def _unpack_shards(packed, like):
    out = {}
    for name, rows, padded, transposed in W_LAYOUT:
        a = packed[W_OFF[name]:W_OFF[name] + rows]
        out[name] = (a.T if transposed else a)[None].reshape(like[name].shape)
    return out


def _pack_small(tree):
    rows = [jnp.zeros((1, D), F32) for _ in range(SMALL_ROWS)]
    buf = jnp.concatenate(rows, axis=0)
    for name, r, c, n in _SMALL:
        buf = lax.dynamic_update_slice(buf, tree[name].reshape(1, n).astype(F32), (r, c))
    return buf


def _unpack_small(buf, like):
    return {name: buf[r:r + 1, c:c + n].reshape(like[name].shape) for name, r, c, n in _SMALL}


def kernel(x, mem, g_mix, w_in, b_forget, g_ret_out, g_fox_q, g_fox_k, w_out, g_xattn, w_xq, w_xkv, g_mem, g_xq, g_xk, w_xo, g_ffn, w_gate, w_up, w_down, loss_target, m_g_mix, m_w_in, m_b_forget, m_g_ret_out, m_g_fox_q, m_g_fox_k, m_w_out, m_g_xattn, m_w_xq, m_w_xkv, m_g_mem, m_g_xq, m_g_xk, m_w_xo, m_g_ffn, m_w_gate, m_w_up, m_w_down, v_g_mix, v_w_in, v_b_forget, v_g_ret_out, v_g_fox_q, v_g_fox_k, v_w_out, v_g_xattn, v_w_xq, v_w_xkv, v_g_mem, v_g_xq, v_g_xk, v_w_xo, v_g_ffn, v_w_gate, v_w_up, v_w_down):
    names = ("g_mix", "w_in", "b_forget", "g_ret_out", "g_fox_q", "g_fox_k", "w_out", "g_xattn", "w_xq", "w_xkv", "g_mem",
             "g_xq", "g_xk", "w_xo", "g_ffn", "w_gate", "w_up", "w_down")
    w = dict(zip(names, (g_mix, w_in, b_forget, g_ret_out, g_fox_q, g_fox_k, w_out, g_xattn, w_xq, w_xkv, g_mem, g_xq, g_xk,
                         w_xo, g_ffn, w_gate, w_up, w_down)))
    m = dict(zip(names, (m_g_mix, m_w_in, m_b_forget, m_g_ret_out, m_g_fox_q, m_g_fox_k, m_w_out, m_g_xattn, m_w_xq, m_w_xkv,
                         m_g_mem, m_g_xq, m_g_xk, m_w_xo, m_g_ffn, m_w_gate, m_w_up, m_w_down)))
    v = dict(zip(names, (v_g_mix, v_w_in, v_b_forget, v_g_ret_out, v_g_fox_q, v_g_fox_k, v_w_out, v_g_xattn, v_w_xq, v_w_xkv,
                         v_g_mem, v_g_xq, v_g_xk, v_w_xo, v_g_ffn, v_w_gate, v_w_up, v_w_down)))
    small_names = [s[0] for s in _SMALL]

    gathered = _all_gather(_pack_shards(w, BF))
    W = {}
    for name, rows, padded, transposed in W_LAYOUT:
        full = gathered[:, W_OFF[name]:W_OFF[name] + rows].reshape(N_DEV * rows, D)
        W[_CANON.get(name, name)] = full

    sp = {n: w[n].reshape(1, -1) for n in small_names}
    loss_part, grad_x, gW, gs = _local_step(x[0], mem[0], loss_target[0], sp, W)

    chunks = []
    for name, rows, padded, transposed in W_LAYOUT:
        g = gW[_CANON.get(name, name)].reshape(N_DEV, rows, D)
        chunks.append(jnp.pad(g, ((0, 0), (0, padded - rows), (0, 0))).astype(BF))
    send = jnp.concatenate(chunks, axis=1)
    small = _pack_small(gs)
    small = lax.dynamic_update_slice(small, loss_part[:, :1], _LOSS_AT)
    recv, recv_small = _all_to_all(send, small)

    g_big, d_big, m_big, v_big = _adamw("adamw_shards", recv, _pack_shards(w, F32), _pack_shards(m, F32), _pack_shards(v, F32), 240)
    g_sm, d_sm, m_sm, v_sm = _adamw("adamw_small", recv_small, _pack_small(w), _pack_small(m), _pack_small(v), SMALL_ROWS)
    loss = g_sm[_LOSS_AT[0], _LOSS_AT[1]]

    outs = []
    for big, sm in ((g_big, g_sm), (d_big, d_sm), (m_big, m_sm), (v_big, v_sm)):
        tree = {**_unpack_shards(big, w), **_unpack_small(sm, w)}
        outs += [tree[n] for n in names]
    return (loss, grad_x[None], *outs)


def _pack_shards(tree, names, dtype):
    parts = []
    for name in names:
        rows, padded, transposed = W_SHARD[name]
        a = tree[name][0]
        a = a.T if transposed else a
        parts.append(jnp.pad(a, ((0, padded - rows), (0, 0))).astype(dtype))
    return jnp.concatenate(parts, axis=0)


def _unpack_shards(packed, names, like):
    out, off = {}, 0
    for name in names:
        rows, padded, transposed = W_SHARD[name]
        a = packed[off:off + rows]
        out[name] = (a.T if transposed else a)[None].reshape(like[name].shape)
        off += padded
    return out


def _unpack_gathered(gathered, names):
    out, off = {}, 0
    for name in names:
        rows, padded, _ = W_SHARD[name]
        out[_CANON.get(name, name)] = gathered[:, off:off + rows].reshape(N_DEV * rows, D)
        off += padded
    return out


def _pack_chunks(grads, names):
    chunks = []
    for name in names:
        rows, padded, _ = W_SHARD[name]
        g = grads[_CANON.get(name, name)].reshape(N_DEV, rows, D)
        chunks.append(jnp.pad(g, ((0, 0), (0, padded - rows), (0, 0))).astype(BF))
    return jnp.concatenate(chunks, axis=1)


def kernel(x, mem, g_mix, w_in, b_forget, g_ret_out, g_fox_q, g_fox_k, w_out, g_xattn, w_xq, w_xkv, g_mem, g_xq, g_xk, w_xo, g_ffn, w_gate, w_up, w_down, loss_target, m_g_mix, m_w_in, m_b_forget, m_g_ret_out, m_g_fox_q, m_g_fox_k, m_w_out, m_g_xattn, m_w_xq, m_w_xkv, m_g_mem, m_g_xq, m_g_xk, m_w_xo, m_g_ffn, m_w_gate, m_w_up, m_w_down, v_g_mix, v_w_in, v_b_forget, v_g_ret_out, v_g_fox_q, v_g_fox_k, v_w_out, v_g_xattn, v_w_xq, v_w_xkv, v_g_mem, v_g_xq, v_g_xk, v_w_xo, v_g_ffn, v_w_gate, v_w_up, v_w_down):
    names = ("g_mix", "w_in", "b_forget", "g_ret_out", "g_fox_q", "g_fox_k", "w_out", "g_xattn", "w_xq", "w_xkv", "g_mem",
             "g_xq", "g_xk", "w_xo", "g_ffn", "w_gate", "w_up", "w_down")
    w = dict(zip(names, (g_mix, w_in, b_forget, g_ret_out, g_fox_q, g_fox_k, w_out, g_xattn, w_xq, w_xkv, g_mem, g_xq, g_xk,
                         w_xo, g_ffn, w_gate, w_up, w_down)))
    m = dict(zip(names, (m_g_mix, m_w_in, m_b_forget, m_g_ret_out, m_g_fox_q, m_g_fox_k, m_w_out, m_g_xattn, m_w_xq, m_w_xkv,
                         m_g_mem, m_g_xq, m_g_xk, m_w_xo, m_g_ffn, m_w_gate, m_w_up, m_w_down)))
    v = dict(zip(names, (v_g_mix, v_w_in, v_b_forget, v_g_ret_out, v_g_fox_q, v_g_fox_k, v_w_out, v_g_xattn, v_w_xq, v_w_xkv,
                         v_g_mem, v_g_xq, v_g_xk, v_w_xo, v_g_ffn, v_w_gate, v_w_up, v_w_down)))
    small_names = [s[0] for s in _SMALL]
    me = 4 * lax.axis_index("x") + 2 * lax.axis_index("y") + lax.axis_index("c")

    first = _all_gather(_pack_shards(w, GATHER_FIRST, BF))
    first, rest_shard = lax.optimization_barrier((first, _pack_shards(w, GATHER_REST, BF)))
    rest_started = _exchange_start("gather_rest_start", rest_shard,
                                   jnp.broadcast_to(rest_shard[None], (N_DEV,) + rest_shard.shape), scatter=False)

    def fetch_rest(after):
        return _unpack_gathered(_exchange_wait("gather_rest_wait", rest_started, after, scatter=False)[1], GATHER_REST)

    pushed = {}

    def push(group, grads):
        send = _pack_chunks(grads, GRAD_GROUPS[group])
        pushed[group] = _exchange_start("scatter_%s_start" % group, send, jnp.zeros(send.shape, BF), scatter=True)
        return pushed[group][4]

    sp = {n: w[n].reshape(1, -1) for n in small_names}
    loss_part, grad_x, g_last, gs = _local_step(x[0], mem[0], loss_target[0], sp, _unpack_gathered(first, GATHER_FIRST)["w_inT"],
                                                rest_started[4], fetch_rest, push)

    small = lax.dynamic_update_slice(_pack_small(gs), loss_part[:, :1], _LOSS_AT)
    recv_mix, recv_small = _all_to_all(_pack_chunks(g_last, GRAD_GROUPS["mix"]), small)

    results = {}
    for group in ("ffn", "xattn", "mix"):
        gnames = GRAD_GROUPS[group]
        wp, mp, vp = (_pack_shards(t, gnames, F32) for t in (w, m, v))
        if group == "mix":
            res = _adamw("adamw_mix", recv_mix, wp, mp, vp, 16)
        else:
            sent, recv = _exchange_wait("scatter_%s_wait" % group, pushed[group], recv_small, scatter=True)
            own = lax.dynamic_index_in_dim(sent, me, axis=0, keepdims=False)
            res = _adamw("adamw_%s" % group, recv, wp, mp, vp, {"ffn": 176, "xattn": 128}[group], own=own)
        results[group] = [_unpack_shards(r, gnames, w) for r in res]
    g_sm, d_sm, m_sm, v_sm = _adamw("adamw_small", recv_small, _pack_small(w), _pack_small(m), _pack_small(v), SMALL_ROWS)
    loss = g_sm[_LOSS_AT[0], _LOSS_AT[1]]

    outs = []
    for k, sm in enumerate((g_sm, d_sm, m_sm, v_sm)):
        tree = _unpack_small(sm, w)
        for group in results:
            tree.update(results[group][k])
        outs += [tree[n] for n in names]
    return (loss, grad_x[None], *outs)


SCATTER_GROUPS = {"ffn": ("w_gate", "w_up", "w_down"), "xattn": ("w_xq", "w_xo", "w_xkv"), "out": ("w_out",), "in": ("w_in",)}


def _adam_update(g, w, m, v):
    m2 = ADAM_B1 * m + (1.0 - ADAM_B1) * g
    v2 = ADAM_B2 * v + (1.0 - ADAM_B2) * jnp.square(g)
    m_hat = m2 / (1.0 - ADAM_B1 ** ADAM_STEP)
    v_hat = v2 / (1.0 - ADAM_B2 ** ADAM_STEP)
    return g, -ADAM_LR * (m_hat / (jnp.sqrt(v_hat) + ADAM_EPS) + ADAM_WD * w), m2, v2


def _adamw_shard(name, recv, own, off, w, m, v):
    rows, padded, transposed = W_SHARD[name.split(":")[1]]
    assert off % padded == 0
    blk = off // padded

    def total(s_ref, own_ref):
        g = own_ref[...].astype(F32)
        for s in range(N_DEV):
            g = g + s_ref[s].astype(F32)
        return g

    if not transposed:
        def body(s_ref, own_ref, w_ref, m_ref, v_ref, *outs):
            for o, r in zip(outs, _adam_update(total(s_ref, own_ref), w_ref[0], m_ref[0], v_ref[0])):
                o[0] = r

        full = pl.BlockSpec((1, rows, D), lambda i: (0, 0, 0))
        return pl.pallas_call(
            body, name=name.replace(":", "_"), grid=(1,),
            in_specs=[pl.BlockSpec((N_DEV, padded, D), lambda i: (0, blk, 0)), pl.BlockSpec((padded, D), lambda i: (blk, 0)),
                      full, full, full],
            out_specs=[full] * 4, out_shape=[jax.ShapeDtypeStruct((1, rows, D), F32)] * 4,
            compiler_params=_params(("arbitrary",)),
        )(recv, own, w, m, v)

    wide = -(-padded // LANES) * LANES

    def body(s_ref, own_ref, w_ref, m_ref, v_ref, *outs):
        g = total(s_ref, own_ref)
        if wide > padded:
            g = jnp.concatenate([g, jnp.zeros((wide - padded, LANES), F32)], axis=0)
        g = g.T[:, :rows]
        for o, r in zip(outs, _adam_update(g, w_ref[0], m_ref[0], v_ref[0])):
            o[0] = r

    cols = pl.BlockSpec((1, LANES, rows), lambda c: (0, c, 0))
    return pl.pallas_call(
        body, name=name.replace(":", "_"), grid=(D // LANES,),
        in_specs=[pl.BlockSpec((N_DEV, padded, LANES), lambda c: (0, blk, c)), pl.BlockSpec((padded, LANES), lambda c: (blk, c)),
                  cols, cols, cols],
        out_specs=[cols] * 4, out_shape=[jax.ShapeDtypeStruct((1, D, rows), F32)] * 4,
        compiler_params=_params(("arbitrary",)),
    )(recv, own, w, m, v)


def _gather_small(small):
    def body(small_ref, out_ref, send_sems, recv_sems, local_sem):
        x, y, c = _place()
        me = 4 * x + 2 * y + c
        mine = pltpu.make_async_copy(small_ref, out_ref.at[me], local_sem)
        mine.start()
        copies = []
        for r in range(1, N_DEV):
            px, py, pc = x ^ (r >> 2), y ^ ((r >> 1) & 1), c ^ (r & 1)
            copies.append(pltpu.make_async_remote_copy(
                src_ref=small_ref, dst_ref=out_ref.at[me], send_sem=send_sems.at[r - 1], recv_sem=recv_sems.at[r - 1],
                device_id=(px, py, pc), device_id_type=MESH))
        for cp in copies:
            cp.start()
        for cp in copies:
            cp.wait_recv()
        for cp in copies:
            cp.wait_send()
        mine.wait()

    return pl.pallas_call(
        body, name="gather_small",
        out_shape=jax.ShapeDtypeStruct((N_DEV,) + small.shape, small.dtype),
        in_specs=[pl.BlockSpec(memory_space=pl.ANY)], out_specs=pl.BlockSpec(memory_space=pl.ANY),
        scratch_shapes=[pltpu.SemaphoreType.DMA((N_DEV - 1,)), pltpu.SemaphoreType.DMA((N_DEV - 1,)), pltpu.SemaphoreType.DMA],
    )(small)


def _pack_chunks(grads, names):
    chunks = []
    for name in names:
        rows, padded, _ = W_SHARD[name]
        g = grads[_CANON.get(name, name)].reshape(N_DEV, rows, D)
        chunks.append(jnp.pad(g, ((0, 0), (0, padded - rows), (0, 0))).astype(BF))
    return chunks[0] if len(chunks) == 1 else jnp.concatenate(chunks, axis=1)


def kernel(x, mem, g_mix, w_in, b_forget, g_ret_out, g_fox_q, g_fox_k, w_out, g_xattn, w_xq, w_xkv, g_mem, g_xq, g_xk, w_xo, g_ffn, w_gate, w_up, w_down, loss_target, m_g_mix, m_w_in, m_b_forget, m_g_ret_out, m_g_fox_q, m_g_fox_k, m_w_out, m_g_xattn, m_w_xq, m_w_xkv, m_g_mem, m_g_xq, m_g_xk, m_w_xo, m_g_ffn, m_w_gate, m_w_up, m_w_down, v_g_mix, v_w_in, v_b_forget, v_g_ret_out, v_g_fox_q, v_g_fox_k, v_w_out, v_g_xattn, v_w_xq, v_w_xkv, v_g_mem, v_g_xq, v_g_xk, v_w_xo, v_g_ffn, v_w_gate, v_w_up, v_w_down):
    names = ("g_mix", "w_in", "b_forget", "g_ret_out", "g_fox_q", "g_fox_k", "w_out", "g_xattn", "w_xq", "w_xkv", "g_mem",
             "g_xq", "g_xk", "w_xo", "g_ffn", "w_gate", "w_up", "w_down")
    w = dict(zip(names, (g_mix, w_in, b_forget, g_ret_out, g_fox_q, g_fox_k, w_out, g_xattn, w_xq, w_xkv, g_mem, g_xq, g_xk,
                         w_xo, g_ffn, w_gate, w_up, w_down)))
    m = dict(zip(names, (m_g_mix, m_w_in, m_b_forget, m_g_ret_out, m_g_fox_q, m_g_fox_k, m_w_out, m_g_xattn, m_w_xq, m_w_xkv,
                         m_g_mem, m_g_xq, m_g_xk, m_w_xo, m_g_ffn, m_w_gate, m_w_up, m_w_down)))
    v = dict(zip(names, (v_g_mix, v_w_in, v_b_forget, v_g_ret_out, v_g_fox_q, v_g_fox_k, v_w_out, v_g_xattn, v_w_xq, v_w_xkv,
                         v_g_mem, v_g_xq, v_g_xk, v_w_xo, v_g_ffn, v_w_gate, v_w_up, v_w_down)))
    small_names = [s[0] for s in _SMALL]
    me = 4 * lax.axis_index("x") + 2 * lax.axis_index("y") + lax.axis_index("c")

    first = _all_gather(_pack_shards(w, GATHER_FIRST, BF))
    first, rest_shard = lax.optimization_barrier((first, _pack_shards(w, GATHER_REST, BF)))
    rest_started = _exchange_start("gather_rest_start", rest_shard,
                                   jnp.broadcast_to(rest_shard[None], (N_DEV,) + rest_shard.shape), scatter=False)

    def fetch_rest(after):
        return _unpack_gathered(_exchange_wait("gather_rest_wait", rest_started, after, scatter=False)[1], GATHER_REST)

    pushed = {}

    def push(group, grads):
        send = _pack_chunks(grads, SCATTER_GROUPS[group])
        pushed[group] = _exchange_start("scatter_%s_start" % group, send, jnp.zeros(send.shape, BF), scatter=True)
        return pushed[group][4]

    sp = {n: w[n].reshape(1, -1) for n in small_names}
    loss_part, grad_x, gs = _local_step(x[0], mem[0], loss_target[0], sp, _unpack_gathered(first, GATHER_FIRST)["w_inT"],
                                        rest_started[4], fetch_rest, push)

    small = lax.dynamic_update_slice(_pack_small(gs), loss_part[:, :1], _LOSS_AT)
    recv_small = _gather_small(small)
    g_sm, d_sm, m_sm, v_sm = _adamw("adamw_small", recv_small, _pack_small(w), _pack_small(m), _pack_small(v), SMALL_ROWS)
    loss = g_sm[_LOSS_AT[0], _LOSS_AT[1]]

    results, after = {}, recv_small
    for group in ("ffn", "xattn", "out", "in"):
        sent, recv = _exchange_wait("scatter_%s_wait" % group, pushed[group], after, scatter=True)
        own = lax.dynamic_index_in_dim(sent, me, axis=0, keepdims=False)
        off = 0
        for name in SCATTER_GROUPS[group]:
            results[name] = _adamw_shard("adamw:" + name, recv, own, off, w[name], m[name], v[name])
            off += W_SHARD[name][1]
        after = results[SCATTER_GROUPS[group][-1]][0]

    outs = []
    for k, sm in enumerate((g_sm, d_sm, m_sm, v_sm)):
        tree = _unpack_small(sm, w)
        tree.update({name: res[k] for name, res in results.items()})
        outs += [tree[n] for n in names]
    return (loss, grad_x[None], *outs)
```

```python
import functools
import math

import jax
import jax.numpy as jnp
import numpy as np
from jax import lax
from jax.experimental import pallas as pl
from jax.experimental.pallas import tpu as pltpu

F32 = jnp.float32
BF = jnp.bfloat16

D = 1024
HEAD = 64
CHUNK = 64
N_MEM = 256
XHEAD = 256
D_FF = 2816
EPS = 1e-6
NEG = -1e30
LANES = 128
N_DEV = 8
V7X_VMEM_BYTES = 64 * 1024 * 1024
VMEM_LIMIT = V7X_VMEM_BYTES - 8 * 1024 * 1024

ADAM_LR, ADAM_B1, ADAM_B2, ADAM_EPS, ADAM_WD, ADAM_STEP = 0.001, 0.9, 0.999, 1e-08, 0.01, 10

W_LAYOUT = (("w_in", 449, 464, True), ("w_out", 128, 128, False), ("w_xq", 128, 128, False), ("w_xkv", 256, 256, True),
            ("w_xo", 128, 128, False), ("w_gate", 352, 352, True), ("w_up", 352, 352, True), ("w_down", 352, 352, False))
W_ROWS = sum(w[2] for w in W_LAYOUT)
W_OFF = {}
_o = 0
for _n, _r, _p, _t in W_LAYOUT:
    W_OFF[_n] = _o
    _o += _p
SMALL_ROWS = 8
W_SHARD = {"w_in": (449, 449, True), "w_out": (128, 128, False), "w_xq": (128, 128, False), "w_xkv": (256, 256, True),
           "w_xo": (128, 128, False), "w_gate": (352, 352, True), "w_up": (352, 352, True), "w_down": (352, 352, False)}
GATHER_FIRST = ("w_in",)
GATHER_REST = ("w_out", "w_xq", "w_xkv", "w_xo", "w_gate", "w_up", "w_down")
GRAD_GROUPS = {"ffn": ("w_gate", "w_up", "w_down"), "xattn": ("w_xq", "w_xkv", "w_xo"), "mix": ("w_in", "w_out")}

NT = (((1,), (1,)), ((), ()))
NN = (((1,), (0,)), ((), ()))
TN = (((0,), (0,)), ((), ()))
_DIMS = {"nn": NN, "nt": NT, "tn": TN}


def _params(sem):
    return pltpu.CompilerParams(dimension_semantics=sem, vmem_limit_bytes=VMEM_LIMIT)


def _mm(name, products, extras, epilogue, M, N, tm, tn, out_dtypes):
    flat = [t for p in products for t in p]
    counts = [len(p) for p in products]
    in_specs, args = [], []
    for a, b, form in flat:
        if form == "tn":
            in_specs.append(pl.BlockSpec((a.shape[0], tm), lambda i, j: (0, i)))
        else:
            in_specs.append(pl.BlockSpec((tm, a.shape[1]), lambda i, j: (i, 0)))
        if form == "nt":
            in_specs.append(pl.BlockSpec((tn, b.shape[1]), lambda i, j: (j, 0)))
        else:
            in_specs.append(pl.BlockSpec((b.shape[0], tn), lambda i, j: (0, j)))
        args += [a, b]
    for e in extras:
        in_specs.append(pl.BlockSpec((tm, tn), lambda i, j: (i, j)))
        args.append(e)
    n_in = len(args)

    def body(*refs):
        ins, outs = refs[:n_in], refs[n_in:]
        prods, p = [], 0
        for c in counts:
            acc = None
            for _ in range(c):
                a = ins[2 * p][...].astype(BF)
                b = ins[2 * p + 1][...].astype(BF)
                d = lax.dot_general(a, b, _DIMS[flat[p][2]], preferred_element_type=F32)
                acc = d if acc is None else acc + d
                p += 1
            prods.append(acc)
        ex = [r[...].astype(F32) for r in ins[2 * len(flat):]]
        res = epilogue(*prods, *ex)
        for o, r in zip(outs, res):
            o[...] = r.astype(o.dtype)

    return pl.pallas_call(
        body, name=name, grid=(M // tm, N // tn), in_specs=in_specs,
        out_specs=[pl.BlockSpec((tm, tn), lambda i, j: (i, j)) for _ in out_dtypes],
        out_shape=[jax.ShapeDtypeStruct((M, N), dt) for dt in out_dtypes],
        compiler_params=_params(("parallel", "arbitrary")),
    )(*args)


def _ident(x):
    return (x,)


def _add(x, r):
    return (x + r,)


def _spec(rows, w, off, per_j):
    if per_j:
        return pl.BlockSpec((rows, w), lambda j, i: (i, off + j))
    return pl.BlockSpec((rows, w), lambda j, i: (i, off))


def _pspec(rows, w, off, per_j):
    if per_j:
        return pl.BlockSpec((rows, w), lambda j, i: (0, off + j))
    return pl.BlockSpec((rows, w), lambda j, i: (0, off))


def _rw_fwd(name, fn, rows, params, outs, T, tm, nj, n_acc=0):
    in_specs = [_spec(tm, w, off, pj) for _, w, off, pj in rows] + [_pspec(a.shape[0], w, off, pj) for a, w, off, pj in params]
    args = [r[0] for r in rows] + [p[0] for p in params]
    n_in, n_out = len(args), len(outs)
    out_specs = [pl.BlockSpec((tm, w), lambda j, i: (i, j)) for _, w in outs]
    out_shape = [jax.ShapeDtypeStruct((T, nj * w), dt) for dt, w in outs]
    out_specs += [pl.BlockSpec((1, LANES), lambda j, i: (0, 0)) for _ in range(n_acc)]
    out_shape += [jax.ShapeDtypeStruct((1, LANES), F32) for _ in range(n_acc)]

    def body(*refs):
        vals = [r[...].astype(F32) for r in refs[:n_in]]
        res = fn(*vals)
        orefs = refs[n_in:]
        for k in range(n_out):
            orefs[k][...] = res[k].astype(orefs[k].dtype)
        first = (pl.program_id(0) == 0) & (pl.program_id(1) == 0)
        for k in range(n_acc):
            @pl.when(first)
            def _(k=k):
                orefs[n_out + k][...] = jnp.zeros((1, LANES), F32)
            orefs[n_out + k][...] += res[n_out + k]

    return pl.pallas_call(
        body, name=name, grid=(nj, T // tm), in_specs=in_specs, out_specs=out_specs, out_shape=out_shape,
        compiler_params=_params(("arbitrary", "arbitrary")),
    )(*args)


def _rw_bwd(name, fn, rows, params, cots, T, tm, nj, row_grads, param_grads, resid=None):
    in_specs = ([_spec(tm, w, off, pj) for _, w, off, pj in rows] + [_pspec(a.shape[0], w, off, pj) for a, w, off, pj in params]
                + [_spec(tm, w, off, pj) for _, w, off, pj in cots])
    args = [r[0] for r in rows] + [p[0] for p in params] + [c[0] for c in cots]
    if resid is not None:
        in_specs.append(_spec(tm, rows[0][1], rows[0][2], rows[0][3]))
        args.append(resid)
    nr, npar, nc = len(rows), len(params), len(cots)
    out_specs, out_shape, kinds = [], [], []
    for k, dts in enumerate(row_grads):
        for dt in (dts if isinstance(dts, (list, tuple)) else [dts]):
            if dt is not None:
                w = rows[k][1]
                out_specs.append(pl.BlockSpec((tm, w), lambda j, i: (i, j)))
                out_shape.append(jax.ShapeDtypeStruct((T, nj * w), dt))
                kinds.append(("row", k))
    for k, need in enumerate(param_grads):
        if need:
            a, w, off, pj = params[k]
            out_specs.append(_pspec(a.shape[0], w, off, pj))
            out_shape.append(jax.ShapeDtypeStruct(a.shape, F32))
            kinds.append(("par", k))

    def body(*refs):
        vals = [r[...].astype(F32) for r in refs[:nr + npar]]
        ct = tuple(r[...].astype(F32) for r in refs[nr + npar:nr + npar + nc])
        _, vjp = jax.vjp(lambda *a: tuple(fn(*a)), *vals)
        grads = list(vjp(ct))
        n_in = nr + npar + nc + (resid is not None)
        if resid is not None:
            grads[0] = grads[0] + refs[n_in - 1][...].astype(F32)
        orefs = refs[n_in:]
        j, i = pl.program_id(0), pl.program_id(1)
        for o, (kind, k) in zip(orefs, kinds):
            if kind == "row":
                o[...] = grads[k].astype(o.dtype)
            else:
                first = (i == 0) if params[k][3] else ((i == 0) & (j == 0))

                @pl.when(first)
                def _(o=o):
                    o[...] = jnp.zeros(o.shape, F32)
                o[...] += grads[nr + k]

    return pl.pallas_call(
        body, name=name, grid=(nj, T // tm), in_specs=in_specs, out_specs=out_specs, out_shape=out_shape,
        compiler_params=_params(("arbitrary", "arbitrary")),
    )(*args)


def _rms(x, g):
    return x * lax.rsqrt(jnp.mean(x * x, axis=-1, keepdims=True) + EPS) * g


def _rms_fn(x, g):
    return (_rms(x, g),)


def _lo_mask():
    return lax.broadcasted_iota(jnp.int32, (1, LANES), 1) < HEAD


def _gmean(x, lo):
    s0 = jnp.sum(jnp.where(lo, x, 0.0), axis=-1, keepdims=True)
    s1 = jnp.sum(jnp.where(lo, 0.0, x), axis=-1, keepdims=True)
    return jnp.where(lo, s0, s1) * (1.0 / HEAD)


def _fox_prep_fn(fq, fk, gq, gk):
    lo = _lo_mask()
    qn = fq * lax.rsqrt(_gmean(fq * fq, lo) + EPS) * gq * (HEAD ** -0.5)
    kn = fk * lax.rsqrt(_gmean(fk * fk, lo) + EPS) * gk
    return qn, kn


def _cast_fn(v):
    return (v,)


@jax.custom_vjp
def _swap_halves(x):
    bit = (lax.broadcasted_iota(jnp.int32, (1, LANES), 1) & (HEAD // 2)) == 0
    return jnp.where(bit, pltpu.roll(x, LANES - HEAD // 2, 1), pltpu.roll(x, HEAD // 2, 1))


_swap_halves.defvjp(lambda x: (_swap_halves(x), None), lambda _, g: (_swap_halves(g),))


def _ret_fn(rq, rk, rv, rg, cos, sin, s_in, g, lg):
    tb = rq.shape[0]
    nc = tb // CHUNK
    lo = _lo_mask()
    row = lax.broadcasted_iota(jnp.int32, (LANES, 1), 0) < HEAD
    same_head = row == lo
    q = (rq * cos + _swap_halves(rq) * sin) * (HEAD ** -0.5)
    k = rk * cos + _swap_halves(rk) * sin
    q3, k3, v3 = q.reshape(nc, CHUNK, LANES), k.reshape(nc, CHUNK, LANES), rv.reshape(nc, CHUNK, LANES)
    pos = lax.broadcasted_iota(jnp.int32, (CHUNK, 1), 0).astype(F32)
    q_decay = jnp.exp(lg * (pos + 1.0))
    k_decay = jnp.exp(lg * (CHUNK - 1.0 - pos))
    chunk_decay = jnp.exp(lg * float(CHUNK))
    dist = jnp.abs(lax.broadcasted_iota(jnp.int32, (CHUNK, CHUNK), 0) - lax.broadcasted_iota(jnp.int32, (CHUNK, CHUNK), 1)).astype(F32)
    v3b = v3.astype(BF)
    intra = []
    for hh in range(2):
        hm = lo if hh == 0 else ~lo
        lg_h = lg[:, hh * HEAD:hh * HEAD + 1]
        qm = jnp.where(hm, q3, 0.0).astype(BF)
        sc = jnp.einsum("nid,njd->nij", qm, k3.astype(BF), preferred_element_type=F32) * jnp.exp(lg_h * dist)[None]
        intra.append(jnp.einsum("nij,nje->nie", sc.astype(BF), v3b, preferred_element_type=F32))
    o = jnp.where(lo, intra[0], intra[1])
    kv = jnp.einsum("njd,nje->nde", (k3 * k_decay[None]).astype(BF), v3b, preferred_element_type=F32)
    kv = jnp.where(same_head[None], kv, 0.0)
    state, states = s_in, []
    for n in range(nc):
        states.append(state)
        state = state * chunk_decay + kv[n]
    s_prev = jnp.stack(states, axis=0)
    o = o + jnp.einsum("nid,nde->nie", (q3 * q_decay[None]).astype(BF), s_prev.astype(BF), preferred_element_type=F32)
    o = o.reshape(tb, LANES)
    mu = _gmean(o, lo)
    oc = o - mu
    y = oc * lax.rsqrt(_gmean(oc * oc, lo) + EPS) * g
    return jax.nn.silu(rg) * y, state


def _xattn_fn(qx, gq, gk, kk, vv):
    q = _rms(qx, gq)
    k = _rms(kk, gk)
    logits = lax.dot_general(q.astype(BF), k.astype(BF), NT, preferred_element_type=F32) * (XHEAD ** -0.5)
    p = jax.nn.softmax(logits, axis=-1)
    return (jnp.dot(p.astype(BF), vv.astype(BF), preferred_element_type=F32),)


def _swiglu_fwd_epi(g, u):
    return g, u, jax.nn.silu(g) * u


def _swiglu_bwd_epi(dact, g, u):
    _, vjp = jax.vjp(lambda a, b: jax.nn.silu(a) * b, g, u)
    return vjp(dact)


def _loss_fn(h, target):
    err = h - target
    part = jnp.sum(jnp.sum(err * err, axis=0, keepdims=True), axis=-1, keepdims=True) * (0.5 / D)
    dy = err * (1.0 / D)
    return dy, dy, part


def _ret_fwd(P, cos, sin, g_ret, lg, T, tb):
    nb = T // tb

    def body(rq, rk, rv, rg, c, s, g, l, o_ref, s0_ref, state):
        @pl.when(pl.program_id(1) == 0)
        def _():
            state[...] = jnp.zeros(state.shape, F32)
        s0_ref[0, 0] = state[...]
        out, s_new = _ret_fn(rq[...], rk[...], rv[...], rg[...], c[...], s[...], state[...], g[...], l[...])
        o_ref[...] = out
        state[...] = s_new

    sec = lambda off: pl.BlockSpec((tb, LANES), lambda j, i: (i, off + j))
    tab = pl.BlockSpec((tb, LANES), lambda j, i: (i, 0))
    par = pl.BlockSpec((1, LANES), lambda j, i: (0, j))
    return pl.pallas_call(
        body, name="ret_fwd", grid=(4, nb),
        in_specs=[sec(0), sec(4), sec(8), sec(12), tab, tab, par, par],
        out_specs=[pl.BlockSpec((tb, LANES), lambda j, i: (i, j)), pl.BlockSpec((1, 1, LANES, LANES), lambda j, i: (j, i, 0, 0))],
        out_shape=[jax.ShapeDtypeStruct((T, 4 * LANES), F32), jax.ShapeDtypeStruct((4, nb, LANES, LANES), F32)],
        scratch_shapes=[pltpu.VMEM((LANES, LANES), F32)],
        compiler_params=_params(("arbitrary", "arbitrary")),
    )(P, P, P, P, cos, sin, g_ret, lg)


def _ret_bwd(P, cos, sin, g_ret, lg, s0, dmix, T, tb):
    nb = T // tb

    def body(rq, rk, rv, rg, c, s, g, l, s0_ref, do, drq, drk, drv, drg, dg, dstate):
        i = pl.program_id(1)

        @pl.when(i == 0)
        def _():
            dstate[...] = jnp.zeros(dstate.shape, F32)
            dg[...] = jnp.zeros(dg.shape, F32)

        cc, ss, ll = c[...], s[...], l[...]
        _, vjp = jax.vjp(lambda a, b, v, gate, st, gg: _ret_fn(a, b, v, gate, cc, ss, st, gg, ll),
                         rq[...], rk[...], rv[...], rg[...], s0_ref[0, 0], g[...])
        ga, gb, gv, ggate, gst, ggain = vjp((do[...], dstate[...]))
        drq[...] = ga.astype(drq.dtype)
        drk[...] = gb.astype(drk.dtype)
        drv[...] = gv.astype(drv.dtype)
        drg[...] = ggate.astype(drg.dtype)
        dstate[...] = gst
        dg[...] += ggain

    rev = lambda i: nb - 1 - i
    sec = lambda off: pl.BlockSpec((tb, LANES), lambda j, i: (rev(i), off + j))
    tab = pl.BlockSpec((tb, LANES), lambda j, i: (rev(i), 0))
    par = pl.BlockSpec((1, LANES), lambda j, i: (0, j))
    outb = pl.BlockSpec((tb, LANES), lambda j, i: (rev(i), j))
    return pl.pallas_call(
        body, name="ret_bwd", grid=(4, nb),
        in_specs=[sec(0), sec(4), sec(8), sec(12), tab, tab, par, par,
                  pl.BlockSpec((1, 1, LANES, LANES), lambda j, i: (j, rev(i), 0, 0)), outb],
        out_specs=[outb, outb, outb, outb, par],
        out_shape=[jax.ShapeDtypeStruct((T, 4 * LANES), BF)] * 4 + [jax.ShapeDtypeStruct((1, 4 * LANES), F32)],
        scratch_shapes=[pltpu.VMEM((LANES, LANES), F32)],
        compiler_params=_params(("arbitrary", "arbitrary")),
    )(P, P, P, P, cos, sin, g_ret, lg, s0, dmix)


_FB = 128


def _tri(lower):
    r = lax.broadcasted_iota(jnp.int32, (_FB, _FB), 0)
    c = lax.broadcasted_iota(jnp.int32, (_FB, _FB), 1)
    return ((r >= c) if lower else (r <= c)).astype(F32)


def _fgate_fwd(ffp, bpad, T):
    def body(ff_ref, b_ref, fc_ref, fr_ref):
        lane = lax.broadcasted_iota(jnp.int32, (1, LANES), 1)
        tri = _tri(True)
        carry = jnp.zeros((1, LANES), F32)
        for blk in range(T // _FB):
            z = ff_ref[blk * _FB:(blk + 1) * _FB, :] + b_ref[...]
            lf = jnp.where(lane < 8, jax.nn.log_sigmoid(z), 0.0)
            f = jnp.dot(tri, lf, precision=lax.Precision.HIGHEST, preferred_element_type=F32) + carry
            carry = f[_FB - 1:_FB, :]
            fc_ref[blk * _FB:(blk + 1) * _FB, :] = f
            fr_ref[:, blk * _FB:(blk + 1) * _FB] = f.T[:8, :]

    return pl.pallas_call(
        body, name="fgate_fwd",
        out_shape=[jax.ShapeDtypeStruct((T, LANES), F32), jax.ShapeDtypeStruct((8, T), F32)],
        compiler_params=pltpu.CompilerParams(vmem_limit_bytes=VMEM_LIMIT),
    )(ffp, bpad)


def _fgate_bwd(ffp, bpad, dfr, T):
    def body(ff_ref, b_ref, dfr_ref, dff_ref, db_ref):
        lane = lax.broadcasted_iota(jnp.int32, (1, LANES), 1)
        tri = _tri(False)
        carry = jnp.zeros((1, LANES), F32)
        db = jnp.zeros((1, LANES), F32)
        for blk in reversed(range(T // _FB)):
            d8 = dfr_ref[:, blk * _FB:(blk + 1) * _FB]
            dcol = jnp.concatenate([d8, jnp.zeros((_FB - 8, _FB), F32)], axis=0).T
            dlf = jnp.dot(tri, dcol, precision=lax.Precision.HIGHEST, preferred_element_type=F32) + carry
            carry = dlf[0:1, :]
            z = ff_ref[blk * _FB:(blk + 1) * _FB, :] + b_ref[...]
            dz = jnp.where(lane < 8, dlf * jax.nn.sigmoid(-z), 0.0)
            dff_ref[blk * _FB:(blk + 1) * _FB, :] = dz.astype(dff_ref.dtype)
            db = db + jnp.sum(dz, axis=0, keepdims=True)
        db_ref[...] = db

    return pl.pallas_call(
        body, name="fgate_bwd",
        out_shape=[jax.ShapeDtypeStruct((T, LANES), BF), jax.ShapeDtypeStruct((1, LANES), F32)],
        compiler_params=pltpu.CompilerParams(vmem_limit_bytes=VMEM_LIMIT),
    )(ffp, bpad, dfr)


def _head_bias_col(fc, head):
    lane = lax.broadcasted_iota(jnp.int32, (1, LANES), 1)
    return jnp.sum(jnp.where(lane == head, fc, 0.0), axis=-1, keepdims=True)


def _head_bias_row(fr, head):
    sub = lax.broadcasted_iota(jnp.int32, (8, 1), 0)
    return jnp.sum(jnp.where(sub == head, fr, 0.0), axis=0, keepdims=True)


def _fox_fwd(qn, kn, vb, fc, fr, T, tq):
    nq = T // tq

    def body(q_ref, k_ref, v_ref, fc_ref, fr_ref, o_ref, c_ref):
        j, i = pl.program_id(0), pl.program_id(1)
        lane = lax.broadcasted_iota(jnp.int32, (1, LANES), 1)
        lo = lane < HEAD
        causal = lax.broadcasted_iota(jnp.int32, (tq, tq), 0) >= lax.broadcasted_iota(jnp.int32, (tq, tq), 1)
        q = q_ref[...]
        fcb = fc_ref[...]
        outs, cs = [], []
        for hh in range(2):
            hm = lo if hh == 0 else ~lo
            head = 2 * j + hh
            qh = jnp.where(hm, q, jnp.zeros_like(q))
            fq = _head_bias_col(fcb, head)

            def block(kb, carry, diag, qh=qh, fq=fq, head=head):
                m, l, acc = carry
                k0 = pl.multiple_of(kb * tq, tq)
                k = k_ref[pl.ds(k0, tq), :]
                v = v_ref[pl.ds(k0, tq), :]
                fk = _head_bias_row(fr_ref[:, pl.ds(k0, tq)], head)
                s = (lax.dot_general(qh, k, NT, preferred_element_type=F32) + fq) - fk
                if diag:
                    s = jnp.where(causal, s, NEG)
                m2 = jnp.maximum(m, jnp.max(s, axis=-1, keepdims=True))
                p = jnp.exp(s - m2)
                a = jnp.exp(m - m2)
                return m2, a * l + jnp.sum(p, axis=-1, keepdims=True), a * acc + jnp.dot(p.astype(BF), v, preferred_element_type=F32)

            init = (jnp.full((tq, 1), NEG, F32), jnp.zeros((tq, 1), F32), jnp.zeros((tq, LANES), F32))
            carry = lax.fori_loop(0, i, lambda kb, c: block(kb, c, False), init)
            m, l, acc = block(i, carry, True)
            outs.append(acc / l)
            cs.append(fq - (m + jnp.log(l)))
        o_ref[...] = jnp.where(lo, outs[0], outs[1])
        c_ref[0] = jnp.where(lane == 0, cs[0], jnp.where(lane == 1, cs[1], 0.0))

    full = lambda: pl.BlockSpec((T, LANES), lambda j, i: (0, j))
    return pl.pallas_call(
        body, name="fox_fwd", grid=(4, nq),
        in_specs=[pl.BlockSpec((tq, LANES), lambda j, i: (i, j)), full(), full(),
                  pl.BlockSpec((tq, LANES), lambda j, i: (i, 0)), pl.BlockSpec((8, T), lambda j, i: (0, 0))],
        out_specs=[pl.BlockSpec((tq, LANES), lambda j, i: (i, j)), pl.BlockSpec((1, tq, LANES), lambda j, i: (j, i, 0))],
        out_shape=[jax.ShapeDtypeStruct((T, 4 * LANES), F32), jax.ShapeDtypeStruct((4, T, LANES), F32)],
        compiler_params=_params(("parallel", "arbitrary")),
    )(qn, kn, vb, fc, fr)


def _fox_bwd_dq(qn, kn, vb, fr, cq, dmix, T, tq):
    nq = T // tq

    def body(q_ref, k_ref, v_ref, fr_ref, c_ref, do_ref, dq_ref, dl_ref, p_scr, dp_scr):
        j, i = pl.program_id(0), pl.program_id(1)
        lane = lax.broadcasted_iota(jnp.int32, (1, LANES), 1)
        lo = lane < HEAD
        causal = lax.broadcasted_iota(jnp.int32, (tq, tq), 0) >= lax.broadcasted_iota(jnp.int32, (tq, tq), 1)
        q, do, cb = q_ref[...], do_ref[...], c_ref[0]
        res, deltas = [], []
        for hh in range(2):
            hm = lo if hh == 0 else ~lo
            head = 2 * j + hh
            qh = jnp.where(hm, q, jnp.zeros_like(q))
            doh = jnp.where(hm, do, 0.0).astype(BF)
            c = cb[:, hh:hh + 1]

            def probs(kb, delta, diag, qh=qh, doh=doh, c=c, head=head):
                k0 = pl.multiple_of(kb * tq, tq)
                k = k_ref[pl.ds(k0, tq), :]
                v = v_ref[pl.ds(k0, tq), :]
                fk = _head_bias_row(fr_ref[:, pl.ds(k0, tq)], head)
                p = jnp.exp((lax.dot_general(qh, k, NT, preferred_element_type=F32) + c) - fk)
                if diag:
                    p = jnp.where(causal, p, 0.0)
                dp = lax.dot_general(doh, v, NT, preferred_element_type=F32)
                p_scr[:, pl.ds(k0, tq)] = p
                dp_scr[:, pl.ds(k0, tq)] = dp
                return delta + jnp.sum(p * dp, axis=-1, keepdims=True)

            delta = lax.fori_loop(0, i, lambda kb, d: probs(kb, d, False), jnp.zeros((tq, 1), F32))
            delta = probs(i, delta, True)

            def grad(kb, acc, delta=delta):
                k0 = pl.multiple_of(kb * tq, tq)
                ds = p_scr[:, pl.ds(k0, tq)] * (dp_scr[:, pl.ds(k0, tq)] - delta)
                return acc + jnp.dot(ds.astype(BF), k_ref[pl.ds(k0, tq), :], preferred_element_type=F32)

            res.append(lax.fori_loop(0, i + 1, grad, jnp.zeros((tq, LANES), F32)))
            deltas.append(delta)
        dq_ref[...] = jnp.where(lo, res[0], res[1])
        dl_ref[0] = jnp.where(lane == 0, deltas[0], jnp.where(lane == 1, deltas[1], 0.0))

    full = lambda: pl.BlockSpec((T, LANES), lambda j, i: (0, j))
    return pl.pallas_call(
        body, name="fox_bwd_dq", grid=(4, nq),
        in_specs=[pl.BlockSpec((tq, LANES), lambda j, i: (i, j)), full(), full(), pl.BlockSpec((8, T), lambda j, i: (0, 0)),
                  pl.BlockSpec((1, tq, LANES), lambda j, i: (j, i, 0)), pl.BlockSpec((tq, LANES), lambda j, i: (i, 4 + j))],
        out_specs=[pl.BlockSpec((tq, LANES), lambda j, i: (i, j)), pl.BlockSpec((1, tq, LANES), lambda j, i: (j, i, 0))],
        out_shape=[jax.ShapeDtypeStruct((T, 4 * LANES), F32), jax.ShapeDtypeStruct((4, T, LANES), F32)],
        scratch_shapes=[pltpu.VMEM((tq, T), F32), pltpu.VMEM((tq, T), F32)],
        compiler_params=_params(("parallel", "arbitrary")),
    )(qn, kn, vb, fr, cq, dmix)


def _fox_bwd_dkv(qn, kn, vb, fr, cq, dl, dmix, T, tq):
    nq = T // tq

    def body(q_ref, k_ref, v_ref, fr_ref, c_ref, dl_ref, do_ref, dk_ref, dv_ref, dfr_ref):
        j, kb = pl.program_id(0), pl.program_id(1)
        lo = _lo_mask()
        sub = lax.broadcasted_iota(jnp.int32, (8, 1), 0)
        causal = lax.broadcasted_iota(jnp.int32, (tq, tq), 0) >= lax.broadcasted_iota(jnp.int32, (tq, tq), 1)
        k, v, frb = k_ref[...], v_ref[...], fr_ref[...]
        dks, dvs, dfs = [], [], []
        for hh in range(2):
            hm = lo if hh == 0 else ~lo
            head = 2 * j + hh
            km = jnp.where(hm, k, jnp.zeros_like(k))
            vm = jnp.where(hm, v, jnp.zeros_like(v))
            fk = _head_bias_row(frb, head)

            def block(qi, carry, diag, km=km, vm=vm, fk=fk, hm=hm, hh=hh):
                dk, dv, df = carry
                q0 = pl.multiple_of(qi * tq, tq)
                q = q_ref[pl.ds(q0, tq), :]
                c = c_ref[0, pl.ds(q0, tq), :][:, hh:hh + 1]
                delta = dl_ref[0, pl.ds(q0, tq), :][:, hh:hh + 1]
                dob = do_ref[pl.ds(q0, tq), :].astype(BF)
                p = jnp.exp((lax.dot_general(q, km, NT, preferred_element_type=F32) + c) - fk)
                if diag:
                    p = jnp.where(causal, p, 0.0)
                dv = dv + lax.dot_general(p.astype(BF), dob, TN, preferred_element_type=F32)
                dp = lax.dot_general(dob, vm, NT, preferred_element_type=F32)
                ds = p * (dp - delta)
                dk = dk + lax.dot_general(ds.astype(BF), q, TN, preferred_element_type=F32)
                return dk, dv, df - jnp.sum(ds, axis=0, keepdims=True)

            init = (jnp.zeros((tq, LANES), F32), jnp.zeros((tq, LANES), F32), jnp.zeros((1, tq), F32))
            carry = block(kb, init, True)
            dk, dv, df = lax.fori_loop(kb + 1, nq, lambda qi, cr: block(qi, cr, False), carry)
            dks.append(dk)
            dvs.append(dv)
            dfs.append(df)
        dk_ref[...] = jnp.where(lo, dks[0], dks[1])
        dv_ref[...] = jnp.where(lo, dvs[0], dvs[1]).astype(dv_ref.dtype)
        dfr_ref[0] = jnp.where(sub == 0, dfs[0], jnp.where(sub == 1, dfs[1], 0.0))

    full = lambda off: pl.BlockSpec((T, LANES), lambda j, kb: (0, off + j))
    blk = lambda: pl.BlockSpec((tq, LANES), lambda j, kb: (kb, j))
    return pl.pallas_call(
        body, name="fox_bwd_dkv", grid=(4, nq),
        in_specs=[full(0), blk(), blk(), pl.BlockSpec((8, tq), lambda j, kb: (0, kb)),
                  pl.BlockSpec((1, T, LANES), lambda j, kb: (j, 0, 0)), pl.BlockSpec((1, T, LANES), lambda j, kb: (j, 0, 0)), full(4)],
        out_specs=[blk(), blk(), pl.BlockSpec((1, 8, tq), lambda j, kb: (j, 0, kb))],
        out_shape=[jax.ShapeDtypeStruct((T, 4 * LANES), F32), jax.ShapeDtypeStruct((T, 4 * LANES), BF),
                   jax.ShapeDtypeStruct((4, 8, T), F32)],
        compiler_params=_params(("parallel", "arbitrary")),
    )(qn, kn, vb, fr, cq, dl, dmix)


_BIAS_LANE = HEAD


def _split3(f):
    hi = f.astype(BF).astype(F32)
    mid = (f - hi).astype(BF).astype(F32)
    lo = ((f - hi) - mid).astype(BF).astype(F32)
    return hi, mid, lo


def _fox_operands(P, fc, g_fq2, g_fk2, T, tm):
    def body(fq_ref, fk_ref, fv_ref, fc_ref, gq_ref, gk_ref, qa_ref, qat_ref, ka_ref, kat_ref, va_ref, vat_ref):
        j = pl.program_id(0)
        lane = lax.broadcasted_iota(jnp.int32, (1, LANES), 1)
        qn, kn = _fox_prep_fn(fq_ref[...], fk_ref[...], gq_ref[...], gk_ref[...])
        v = fv_ref[...]
        fcb = fc_ref[...]
        b = _BIAS_LANE
        for hh in range(2):
            hi, mid, lo = _split3(_head_bias_col(fcb, 2 * j + hh))
            take = (lambda a: a) if hh == 0 else (lambda a: pltpu.roll(a, HEAD, 1))
            qa = jnp.where(lane < HEAD, take(qn), jnp.where(lane == b, hi, jnp.where(lane == b + 1, mid, jnp.where(
                lane == b + 2, lo, jnp.where(lane < b + 6, 1.0, 0.0)))))
            ka = jnp.where(lane < HEAD, take(kn), jnp.where(lane < b + 3, 1.0, jnp.where(lane == b + 3, -hi, jnp.where(
                lane == b + 4, -mid, jnp.where(lane == b + 5, -lo, 0.0)))))
            va = jnp.where(lane < HEAD, take(v), 0.0)
            for val, ref, tref in ((qa, qa_ref, qat_ref), (ka, ka_ref, kat_ref), (va, va_ref, vat_ref)):
                ref[hh] = val.astype(BF)
                tref[hh] = val.T.astype(BF)

    sec = lambda off: pl.BlockSpec((tm, LANES), lambda j, i: (i, off + j))
    par = pl.BlockSpec((1, LANES), lambda j, i: (0, 0))
    nat = pl.BlockSpec((2, tm, LANES), lambda j, i: (j, i, 0))
    trn = pl.BlockSpec((2, LANES, tm), lambda j, i: (j, 0, i))
    return pl.pallas_call(
        body, name="fox_operands", grid=(4, T // tm),
        in_specs=[sec(16), sec(20), sec(24), pl.BlockSpec((tm, LANES), lambda j, i: (i, 0)), par, par],
        out_specs=[nat, trn, nat, trn, nat, trn],
        out_shape=[jax.ShapeDtypeStruct((8, T, LANES), BF), jax.ShapeDtypeStruct((8, LANES, T), BF)] * 3,
        compiler_params=_params(("parallel", "arbitrary")),
    )(P, P, P, fc, g_fq2, g_fk2)


def _fox_forward(qat, ka, vat, T, tq, tk):
    nq, per = T // tq, tq // tk

    def body(qat_ref, ka_ref, vat_ref, o_ref, lse_ref):
        i = pl.program_id(1)
        sub = lax.broadcasted_iota(jnp.int32, (8, 1), 0)
        krow = lax.broadcasted_iota(jnp.int32, (tk, tq), 0)
        qcol = lax.broadcasted_iota(jnp.int32, (tk, tq), 1)

        def scores(kb):
            k0 = pl.multiple_of(kb * tk, tk)
            return tuple(jnp.dot(ka_ref[hh, pl.ds(k0, tk), :], qat_ref[hh], preferred_element_type=F32) for hh in range(2))

        def step(kb, carry, mask, last=False):
            stats, s_now = carry
            s_next = s_now if last else scores(kb + 1)
            k0 = pl.multiple_of(kb * tk, tk)
            new = []
            for hh in range(2):
                m, l, acc = stats[hh]
                s = s_now[hh] if mask is None else jnp.where(mask, s_now[hh], NEG)
                m2 = jnp.maximum(m, jnp.max(s, axis=0, keepdims=True))
                p = jnp.exp(s - m2)
                a = jnp.exp(m - m2)
                pv = jnp.dot(vat_ref[hh, 0:HEAD, pl.ds(k0, tk)], p.astype(BF), preferred_element_type=F32)
                new.append((m2, a * l + jnp.sum(p, axis=0, keepdims=True), a * acc + pv))
            return tuple(new), s_next

        one = (jnp.full((1, tq), NEG, F32), jnp.zeros((1, tq), F32), jnp.zeros((HEAD, tq), F32))
        carry = lax.fori_loop(0, i * per, lambda kb, c: step(kb, c, None), ((one, one), scores(0)))
        for d in range(per):
            carry = step(i * per + d, carry, krow + d * tk <= qcol, last=(d == per - 1))
        stats = carry[0]
        o_ref[...] = jnp.concatenate([acc / l for _, l, acc in stats], axis=0).T
        lses = [m + jnp.log(l) for m, l, _ in stats]
        lse_ref[0] = jnp.where(sub == 0, lses[0], jnp.where(sub == 1, lses[1], 0.0))

    return pl.pallas_call(
        body, name="fox_forward", grid=(4, nq),
        in_specs=[pl.BlockSpec((2, LANES, tq), lambda j, i: (j, 0, i)), pl.BlockSpec((2, T, LANES), lambda j, i: (j, 0, 0)),
                  pl.BlockSpec((2, LANES, T), lambda j, i: (j, 0, 0))],
        out_specs=[pl.BlockSpec((tq, LANES), lambda j, i: (i, j)), pl.BlockSpec((1, 8, tq), lambda j, i: (j, 0, i))],
        out_shape=[jax.ShapeDtypeStruct((T, 4 * LANES), F32), jax.ShapeDtypeStruct((4, 8, T), F32)],
        compiler_params=_params(("parallel", "arbitrary")),
    )(qat, ka, vat)


def _fox_cotangent(dmix, fox, T, tm):
    def body(do_ref, o_ref, doa_ref, doat_ref, dl_ref):
        lane = lax.broadcasted_iota(jnp.int32, (1, LANES), 1)
        sub = lax.broadcasted_iota(jnp.int32, (8, 1), 0)
        dob = do_ref[...].astype(BF).astype(F32)
        prod_t = (dob * o_ref[...]).T
        d0 = jnp.sum(prod_t[:HEAD], axis=0, keepdims=True)
        d1 = jnp.sum(prod_t[HEAD:], axis=0, keepdims=True)
        dl_ref[0] = jnp.where(sub == 0, d0, jnp.where(sub == 1, d1, 0.0))
        for hh in range(2):
            val = jnp.where(lane < HEAD, dob if hh == 0 else pltpu.roll(dob, HEAD, 1), 0.0)
            doa_ref[hh] = val.astype(BF)
            doat_ref[hh] = val.T.astype(BF)

    return pl.pallas_call(
        body, name="fox_cotangent", grid=(4, T // tm),
        in_specs=[pl.BlockSpec((tm, LANES), lambda j, i: (i, 4 + j)), pl.BlockSpec((tm, LANES), lambda j, i: (i, j))],
        out_specs=[pl.BlockSpec((2, tm, LANES), lambda j, i: (j, i, 0)), pl.BlockSpec((2, LANES, tm), lambda j, i: (j, 0, i)),
                   pl.BlockSpec((1, 8, tm), lambda j, i: (j, 0, i))],
        out_shape=[jax.ShapeDtypeStruct((8, T, LANES), BF), jax.ShapeDtypeStruct((8, LANES, T), BF),
                   jax.ShapeDtypeStruct((4, 8, T), F32)],
        compiler_params=_params(("parallel", "arbitrary")),
    )(dmix, fox)


def _fox_backward(qa, qat, ka, kat, va, doa, doat, lse, dl, T, tq):
    nq = T // tq

    def body(qa_ref, qat_ref, ka_ref, kat_ref, va_ref, doa_ref, doat_ref, lse_ref, dl_ref,
             dq_ref, dk_ref, dv_ref, df_ref, dr_ref, dqt, dk_acc, dv_acc, df_acc):
        j, kb = pl.program_id(0), pl.program_id(1)
        lane = lax.broadcasted_iota(jnp.int32, (1, LANES), 1)
        mask = lax.broadcasted_iota(jnp.int32, (tq, tq), 0) <= lax.broadcasted_iota(jnp.int32, (tq, tq), 1)

        @pl.when(kb == 0)
        def _():
            dqt[...] = jnp.zeros(dqt.shape, F32)

        dk_acc[...] = jnp.zeros(dk_acc.shape, F32)
        dv_acc[...] = jnp.zeros(dv_acc.shape, F32)
        df_acc[...] = jnp.zeros(df_acc.shape, F32)

        def products(qi):
            q0 = pl.multiple_of(qi * tq, tq)
            return tuple((jnp.dot(ka_ref[hh], qat_ref[hh, :, pl.ds(q0, tq)], preferred_element_type=F32),
                          jnp.dot(va_ref[hh], doat_ref[hh, :, pl.ds(q0, tq)], preferred_element_type=F32)) for hh in range(2))

        def step(qi, now, diag):
            ahead = products(jnp.minimum(qi + 1, nq - 1))
            q0 = pl.multiple_of(qi * tq, tq)
            for hh in range(2):
                s, dp = now[hh]
                p = jnp.exp(s - lse_ref[0, hh:hh + 1, pl.ds(q0, tq)])
                if diag:
                    p = jnp.where(mask, p, 0.0)
                ds = p * (dp - dl_ref[0, hh:hh + 1, pl.ds(q0, tq)])
                pb, dsb = p.astype(BF), ds.astype(BF)
                dv_acc[hh] += jnp.dot(pb, doa_ref[hh, pl.ds(q0, tq), :], preferred_element_type=F32)
                dk_acc[hh] += jnp.dot(dsb, qa_ref[hh, pl.ds(q0, tq), :], preferred_element_type=F32)
                dqt[hh, 0:HEAD, pl.ds(q0, tq)] += jnp.dot(kat_ref[hh, 0:HEAD, :], dsb, preferred_element_type=F32)
                dqt[hh, HEAD:HEAD + 8, pl.ds(q0, tq)] += jnp.broadcast_to(jnp.sum(ds, axis=0, keepdims=True), (8, tq))
                part = ds[:, 0:LANES]
                for c in range(1, tq // LANES):
                    part = part + ds[:, c * LANES:(c + 1) * LANES]
                df_acc[hh] += part
            return ahead

        lax.fori_loop(kb + 1, nq, lambda qi, now: step(qi, now, False), step(kb, products(kb), True))

        lo = lane < HEAD
        dk_ref[...] = jnp.where(lo, dk_acc[0], pltpu.roll(dk_acc[1], HEAD, 1))
        dv_ref[...] = jnp.where(lo, dv_acc[0], pltpu.roll(dv_acc[1], HEAD, 1)).astype(dv_ref.dtype)
        f0 = -jnp.sum(df_acc[0], axis=1, keepdims=True)
        f1 = -jnp.sum(df_acc[1], axis=1, keepdims=True)
        df_ref[0] = jnp.where(lane == 2 * j, f0, jnp.where(lane == 2 * j + 1, f1, 0.0))

        @pl.when(kb == nq - 1)
        def _():
            for t in range(nq):
                cols = slice(t * tq, (t + 1) * tq)
                dq_ref[cols, :] = jnp.concatenate([dqt[0, 0:HEAD, cols], dqt[1, 0:HEAD, cols]], axis=0).T
                rsum = jnp.concatenate([dqt[0, HEAD:HEAD + 8, cols], dqt[1, HEAD:HEAD + 8, cols],
                                        jnp.zeros((LANES - 16, tq), F32)], axis=0).T
                dr_ref[0, cols, :] = jnp.where(lane == 2 * j, rsum[:, 0:1], jnp.where(lane == 2 * j + 1, rsum[:, 8:9], 0.0))

    nat_full = pl.BlockSpec((2, T, LANES), lambda j, kb: (j, 0, 0))
    trn_full = pl.BlockSpec((2, LANES, T), lambda j, kb: (j, 0, 0))
    nat_blk = pl.BlockSpec((2, tq, LANES), lambda j, kb: (j, kb, 0))
    trn_blk = pl.BlockSpec((2, LANES, tq), lambda j, kb: (j, 0, kb))
    rows = pl.BlockSpec((1, 8, T), lambda j, kb: (j, 0, 0))
    blk = pl.BlockSpec((tq, LANES), lambda j, kb: (kb, j))
    return pl.pallas_call(
        body, name="fox_backward", grid=(4, nq),
        in_specs=[nat_full, trn_full, nat_blk, trn_blk, nat_blk, nat_full, trn_full, rows, rows],
        out_specs=[pl.BlockSpec((T, LANES), lambda j, kb: (0, j)), blk, blk, pl.BlockSpec((1, tq, LANES), lambda j, kb: (j, kb, 0)),
                   pl.BlockSpec((1, T, LANES), lambda j, kb: (j, 0, 0))],
        out_shape=[jax.ShapeDtypeStruct((T, 4 * LANES), F32), jax.ShapeDtypeStruct((T, 4 * LANES), F32),
                   jax.ShapeDtypeStruct((T, 4 * LANES), BF), jax.ShapeDtypeStruct((4, T, LANES), F32),
                   jax.ShapeDtypeStruct((4, T, LANES), F32)],
        scratch_shapes=[pltpu.VMEM((2, HEAD + 8, T), F32), pltpu.VMEM((2, tq, LANES), F32), pltpu.VMEM((2, tq, LANES), F32),
                        pltpu.VMEM((2, tq, LANES), F32)],
        compiler_params=_params(("arbitrary", "arbitrary")),
    )(qa, qat, ka, kat, va, doa, doat, lse, dl)


def _fgate_bwd_col(ffp, bpad, dfc, T):
    def body(ff_ref, b_ref, dfc_ref, dff_ref, db_ref):
        lane = lax.broadcasted_iota(jnp.int32, (1, LANES), 1)
        tri = _tri(False)
        carry = jnp.zeros((1, LANES), F32)
        db = jnp.zeros((1, LANES), F32)
        for blk in reversed(range(T // _FB)):
            dlf = jnp.dot(tri, dfc_ref[blk * _FB:(blk + 1) * _FB, :], precision=lax.Precision.HIGHEST,
                          preferred_element_type=F32) + carry
            carry = dlf[0:1, :]
            z = ff_ref[blk * _FB:(blk + 1) * _FB, :] + b_ref[...]
            dz = jnp.where(lane < 8, dlf * jax.nn.sigmoid(-z), 0.0)
            dff_ref[blk * _FB:(blk + 1) * _FB, :] = dz.astype(dff_ref.dtype)
            db = db + jnp.sum(dz, axis=0, keepdims=True)
        db_ref[...] = db

    return pl.pallas_call(
        body, name="fgate_bwd",
        out_shape=[jax.ShapeDtypeStruct((T, LANES), BF), jax.ShapeDtypeStruct((1, LANES), F32)],
        compiler_params=pltpu.CompilerParams(vmem_limit_bytes=VMEM_LIMIT),
    )(ffp, bpad, dfc)


MESH = pl.DeviceIdType.MESH


def _place():
    return lax.axis_index("x"), lax.axis_index("y"), lax.axis_index("c")


def _all_gather(shard):
    R, W = shard.shape

    def body(x_ref, out_ref, send_sems, recv_sems, local_sem):
        x, y, c = _place()
        me, sibling = (x, y, c), (x, y, 1 - c)
        chips = [(1 - x, y), (x, 1 - y), (1 - x, 1 - y)]

        def slot(px, py, pc):
            return out_ref.at[4 * px + 2 * py + pc]

        def copy(k, block, to, src=None):
            return pltpu.make_async_remote_copy(
                src_ref=slot(*block) if src is None else src, dst_ref=slot(*block),
                send_sem=send_sems.at[k], recv_sem=recv_sems.at[k], device_id=to, device_id_type=MESH)

        mine = pltpu.make_async_copy(x_ref, slot(*me), local_sem)
        mine.start()
        first = [copy(0, me, sibling, src=x_ref)]
        first += [copy(1 + n, me, (*chip, c), src=x_ref) for n, chip in enumerate(chips)]
        for cp in first:
            cp.start()
        passed = [copy(4 + n, (*chip, c), sibling) for n, chip in enumerate(chips)]
        for n, chip in enumerate(chips):
            copy(1 + n, (*chip, c), me).wait_recv()
            passed[n].start()
        copy(0, sibling, me).wait_recv()
        for n, chip in enumerate(chips):
            copy(4 + n, (*chip, 1 - c), me).wait_recv()
        for cp in first + passed:
            cp.wait_send()
        mine.wait()

    return pl.pallas_call(
        body, name="all_gather_weights",
        out_shape=jax.ShapeDtypeStruct((N_DEV, R, W), shard.dtype),
        in_specs=[pl.BlockSpec(memory_space=pl.ANY)], out_specs=pl.BlockSpec(memory_space=pl.ANY),
        scratch_shapes=[pltpu.SemaphoreType.DMA((7,)), pltpu.SemaphoreType.DMA((7,)), pltpu.SemaphoreType.DMA],
    )(shard)


def _all_to_all(big, small):
    def body(big_ref, small_ref, rbig_ref, rsmall_ref, send_sems, recv_sems, local_sems):
        x, y, c = _place()
        me = 4 * x + 2 * y + c
        l0 = pltpu.make_async_copy(big_ref.at[me], rbig_ref.at[me], local_sems.at[0])
        l1 = pltpu.make_async_copy(small_ref, rsmall_ref.at[me], local_sems.at[1])
        l0.start()
        l1.start()
        copies = []
        for r in range(1, N_DEV):
            px, py, pc = x ^ (r >> 2), y ^ ((r >> 1) & 1), c ^ (r & 1)
            peer = 4 * px + 2 * py + pc
            copies.append(pltpu.make_async_remote_copy(
                src_ref=big_ref.at[peer], dst_ref=rbig_ref.at[me], send_sem=send_sems.at[2 * r], recv_sem=recv_sems.at[2 * r],
                device_id=(px, py, pc), device_id_type=MESH))
            copies.append(pltpu.make_async_remote_copy(
                src_ref=small_ref, dst_ref=rsmall_ref.at[me], send_sem=send_sems.at[2 * r + 1], recv_sem=recv_sems.at[2 * r + 1],
                device_id=(px, py, pc), device_id_type=MESH))
        for cp in copies:
            cp.start()
        for cp in copies:
            cp.wait_recv()
        for cp in copies:
            cp.wait_send()
        l0.wait()
        l1.wait()

    return pl.pallas_call(
        body, name="all_to_all_grads",
        out_shape=[jax.ShapeDtypeStruct(big.shape, big.dtype), jax.ShapeDtypeStruct((N_DEV,) + small.shape, small.dtype)],
        in_specs=[pl.BlockSpec(memory_space=pl.ANY)] * 2, out_specs=[pl.BlockSpec(memory_space=pl.ANY)] * 2,
        scratch_shapes=[pltpu.SemaphoreType.DMA((2 * N_DEV,)), pltpu.SemaphoreType.DMA((2 * N_DEV,)), pltpu.SemaphoreType.DMA((2,))],
    )(big, small)


def _exchange_copies(src_ref, land_ref, send_sems, recv_sems, scatter):
    x, y, c = _place()
    me = 4 * x + 2 * y + c
    copies = []
    for r in range(1, N_DEV):
        px, py, pc = x ^ (r >> 2), y ^ ((r >> 1) & 1), c ^ (r & 1)
        copies.append(pltpu.make_async_remote_copy(
            src_ref=src_ref.at[4 * px + 2 * py + pc] if scatter else src_ref, dst_ref=land_ref.at[me],
            send_sem=send_sems.at[r - 1], recv_sem=recv_sems.at[r - 1], device_id=(px, py, pc), device_id_type=MESH))
    return copies


_HBM = pl.BlockSpec(memory_space=pltpu.HBM)
_SEM = pl.BlockSpec(memory_space=pltpu.SEMAPHORE)
_EFFECT = pltpu.SideEffectType.DATAFLOW_SIDE_EFFECTING


def _exchange_start(name, src, land, scatter):
    def body(src_ref, land_ref, send_sems, recv_sems, src_thru, land_thru, token):
        for cp in _exchange_copies(src_ref, land_ref, send_sems, recv_sems, scatter):
            cp.start()
        token[...] = jnp.zeros(token.shape, F32)

    return pl.pallas_call(
        body, name=name,
        out_shape=(pltpu.SemaphoreType.DMA((N_DEV - 1,)), pltpu.SemaphoreType.DMA((N_DEV - 1,)),
                   pltpu.HBM(src.shape, src.dtype), pltpu.HBM(land.shape, land.dtype), jax.ShapeDtypeStruct((8, LANES), F32)),
        in_specs=(_HBM, _HBM), out_specs=(_SEM, _SEM, _HBM, _HBM, pl.BlockSpec(memory_space=pltpu.VMEM)),
        input_output_aliases={0: 2, 1: 3},
        compiler_params=pltpu.CompilerParams(has_side_effects=_EFFECT),
    )(pltpu.with_memory_space_constraint(src, pltpu.HBM), pltpu.with_memory_space_constraint(land, pltpu.HBM))


def _exchange_wait(name, started, after, scatter):
    send_sems, recv_sems, src_thru, land_thru, _ = started

    def body(src_ref, land_ref, send_sems, recv_sems, after_ref, src_dead, got_ref):
        copies = _exchange_copies(src_ref, land_ref, send_sems, recv_sems, scatter)
        for cp in copies:
            cp.wait_send()
        for cp in copies:
            cp.wait_recv()

    return pl.pallas_call(
        body, name=name,
        out_shape=(pltpu.HBM(src_thru.shape, src_thru.dtype), pltpu.HBM(land_thru.shape, land_thru.dtype)),
        in_specs=(_HBM, _HBM, _SEM, _SEM, pl.BlockSpec(memory_space=pl.ANY)), out_specs=(_HBM, _HBM),
        input_output_aliases={0: 0, 1: 1},
        compiler_params=pltpu.CompilerParams(has_side_effects=_EFFECT),
    )(src_thru, land_thru, send_sems, recv_sems, after)


def _adamw(name, slots, w, m, v, tr, own=None):
    R, W = w.shape

    def body(s_ref, *refs):
        if own is not None:
            own_ref, refs = refs[0], refs[1:]
        w_ref, m_ref, v_ref, g_ref, d_ref, nm_ref, nv_ref = refs
        g = s_ref[0].astype(F32)
        for s in range(1, N_DEV):
            g = g + s_ref[s].astype(F32)
        if own is not None:
            g = g + own_ref[...].astype(F32)
        m2 = ADAM_B1 * m_ref[...] + (1.0 - ADAM_B1) * g
        v2 = ADAM_B2 * v_ref[...] + (1.0 - ADAM_B2) * jnp.square(g)
        m_hat = m2 / (1.0 - ADAM_B1 ** ADAM_STEP)
        v_hat = v2 / (1.0 - ADAM_B2 ** ADAM_STEP)
        g_ref[...] = g
        d_ref[...] = -ADAM_LR * (m_hat / (jnp.sqrt(v_hat) + ADAM_EPS) + ADAM_WD * w_ref[...])
        nm_ref[...] = m2
        nv_ref[...] = v2

    row = lambda: pl.BlockSpec((tr, W), lambda i: (i, 0))
    return pl.pallas_call(
        body, name=name, grid=(R // tr,),
        in_specs=[pl.BlockSpec((N_DEV, tr, W), lambda i: (0, i, 0))] + [row() for _ in range(3 + (own is not None))],
        out_specs=[row(), row(), row(), row()],
        out_shape=[jax.ShapeDtypeStruct((R, W), F32)] * 4,
        compiler_params=_params(("parallel",)),
    )(slots, *([own] if own is not None else []), w, m, v)


def _tables(T):
    pos = jnp.arange(T, dtype=F32)
    inv_freq = 10000.0 ** (-jnp.arange(0, HEAD, 2, dtype=F32) / HEAD)
    ang = pos[:, None] * inv_freq[None, :]
    cos, sin = jnp.cos(ang), jnp.sin(ang)
    cos4 = jnp.tile(cos, (1, 4))
    sin4 = jnp.tile(jnp.concatenate([-sin, sin], axis=1), (1, 2))
    log_g = jnp.log(1.0 - 2.0 ** (-5.0 - jnp.arange(8, dtype=F32)))
    return cos4, sin4, jnp.repeat(log_g, HEAD)[None, :]


def _local_step(x, mem, target, sp, w_inT, token, fetch_rest, push):
    T = x.shape[0]
    tm = min(512, T)
    tq = min(256, T)
    tb = min(1024, T)
    cos4, sin4, lg = _tables(T)
    g_fq2 = jnp.tile(sp["g_fox_q"], (1, 2))
    g_fk2 = jnp.tile(sp["g_fox_k"], (1, 2))
    g_ret = sp["g_ret_out"].reshape(1, 8 * HEAD)
    bpad = jnp.pad(sp["b_forget"], ((0, 0), (0, LANES - 8)))
    w_secs = [w_inT[k * 512:(k + 1) * 512] for k in range(7)]
    w_ffT = jnp.pad(w_inT[3584:3592], ((0, LANES - 8), (0, 0)))
    w_mainT = w_inT[:3584]
    tie = lambda p, tok: p + tok[0:1, 0:1]
    tm2, tm4 = min(1024, T), min(2048, T)

    hn1, = _rw_fwd("rms_mix", _rms_fn, [(x, D, 0, False)], [(tie(sp["g_mix"], token), D, 0, False)], [(BF, D)], T, tm, 1)
    P, = _mm("proj_in", [[(hn1, w_mainT, "nt")]], [], _ident, T, 3584, tm4, 512, [F32])
    ffp, = _mm("proj_ff", [[(hn1, w_ffT, "nt")]], [], _ident, T, LANES, tm, LANES, [F32])
    ret, s0 = _ret_fwd(P, cos4, sin4, g_ret, lg, T, tb)
    fc, _ = _fgate_fwd(ffp, bpad, T)
    qa, qat, ka, kat, va, vat = _fox_operands(P, fc, g_fq2, g_fk2, T, tm)
    fox, lse = _fox_forward(qat, ka, vat, T, tq, min(128, T))
    W = fetch_rest(fox)
    w_out_halves = (W["w_out"][:4 * LANES], W["w_out"][4 * LANES:])
    h1, = _mm("proj_out", [[(ret, w_out_halves[0], "nn"), (fox, w_out_halves[1], "nn")]], [x], _add, T, D, tm2, D, [F32])

    hn2, = _rw_fwd("rms_xattn", _rms_fn, [(h1, D, 0, False)], [(sp["g_xattn"], D, 0, False)], [(BF, D)], T, tm, 1)
    qx, = _mm("proj_xq", [[(hn2, W["w_xq"], "nn")]], [], _ident, T, D, tm2, D, [F32])
    memn, = _rw_fwd("rms_mem", _rms_fn, [(mem, D, 0, False)], [(sp["g_mem"], D, 0, False)], [(BF, D)], N_MEM, N_MEM, 1)
    kv, = _mm("proj_xkv", [[(memn, W["w_xkvT"], "nt")]], [], _ident, N_MEM, 2 * D, N_MEM, 512, [F32])
    xa_rows = [(qx, XHEAD, 0, True)]
    xa_params = [(sp["g_xq"], XHEAD, 0, False), (sp["g_xk"], XHEAD, 0, False), (kv, XHEAD, 0, True), (kv, XHEAD, 4, True)]
    xo, = _rw_fwd("xattn_fwd", _xattn_fn, xa_rows, xa_params, [(BF, XHEAD)], T, tm, 4)
    h2, = _mm("proj_xo", [[(xo, W["w_xo"], "nn")]], [h1], _add, T, D, tm2, D, [F32])

    hn3, = _rw_fwd("rms_ffn", _rms_fn, [(h2, D, 0, False)], [(sp["g_ffn"], D, 0, False)], [(BF, D)], T, tm, 1)
    gate, up, act = _mm("ffn_in", [[(hn3, W["w_gateT"], "nt")], [(hn3, W["w_upT"], "nt")]], [], _swiglu_fwd_epi,
                        T, D_FF, tm4, 256, [BF, BF, BF])
    h3, = _mm("ffn_out", [[(act, W["w_down"], "nn")]], [h2], _add, T, D, tm, D, [F32])
    dy, dyb, loss_part = _rw_fwd("loss", _loss_fn, [(h3, D, 0, False), (target, D, 0, False)], [], [(F32, D), (BF, D)], T, tm, 1,
                                 n_acc=1)

    dgate, dup = _mm("ffn_out_bwd", [[(dyb, W["w_down"], "nt")]], [gate, up], _swiglu_bwd_epi, T, D_FF, tm4, 256, [BF, BF])
    dhn3, = _mm("ffn_in_bwd", [[(dgate, W["w_gateT"], "nn"), (dup, W["w_upT"], "nn")]], [], _ident, T, D, tm, D, [F32])
    gW = {}
    gW["w_gateT"], = _mm("dw_gate", [[(dgate, hn3, "tn")]], [], _ident, D_FF, D, 256, D, [BF])
    gW["w_upT"], = _mm("dw_up", [[(dup, hn3, "tn")]], [], _ident, D_FF, D, 256, D, [BF])
    gW["w_down"], = _mm("dw_down", [[(act, dyb, "tn")]], [], _ident, D_FF, D, 256, D, [BF])
    tok = push("ffn", gW)
    gs = {}
    dh2, dh2b, gs["g_ffn"] = _rw_bwd("rms_ffn_bwd", _rms_fn, [(h2, D, 0, False)], [(tie(sp["g_ffn"], tok), D, 0, False)],
                                     [(dhn3, D, 0, False)], T, tm, 1, [[F32, BF]], [True], resid=dy)

    dxo, = _mm("proj_xo_bwd", [[(dh2b, W["w_xo"], "nt")]], [], _ident, T, D, tm2, D, [BF])
    gW["w_xo"], = _mm("dw_xo", [[(xo, dh2b, "tn")]], [], _ident, D, D, 256, D, [BF])
    dqx, gs["g_xq"], gs["g_xk"], dkv_k, dkv_v = _rw_bwd(
        "xattn_bwd", _xattn_fn, xa_rows, xa_params, [(dxo, XHEAD, 0, True)], T, tm, 4, [BF], [True, True, True, True])
    dkv = jnp.concatenate([dkv_k[:, :D], dkv_v[:, D:]], axis=1)
    dhn2, = _mm("proj_xq_bwd", [[(dqx, W["w_xq"], "nt")]], [], _ident, T, D, tm2, D, [F32])
    gW["w_xq"], = _mm("dw_xq", [[(hn2, dqx, "tn")]], [], _ident, D, D, 256, D, [BF])
    dmemn, = _mm("proj_xkv_bwd", [[(dkv, W["w_xkvT"], "nn")]], [], _ident, N_MEM, D, N_MEM, 512, [F32])
    gW["w_xkvT"], = _mm("dw_xkv", [[(dkv, memn, "tn")]], [], _ident, 2 * D, D, 512, D, [BF])
    tok = push("xattn", gW)
    gs["g_mem"], = _rw_bwd("rms_mem_bwd", _rms_fn, [(mem, D, 0, False)], [(sp["g_mem"], D, 0, False)], [(dmemn, D, 0, False)],
                           N_MEM, N_MEM, 1, [None], [True])
    dh1, dh1b, gs["g_xattn"] = _rw_bwd("rms_xattn_bwd", _rms_fn, [(h1, D, 0, False)], [(tie(sp["g_xattn"], tok), D, 0, False)],
                                       [(dhn2, D, 0, False)], T, tm, 1, [[F32, BF]], [True], resid=dh2)

    dmix, = _mm("proj_out_bwd", [[(dh1b, W["w_out"], "nt")]], [], _ident, T, D, tm2, D, [F32])
    gW["w_out"] = jnp.concatenate([_mm("dw_out_%d" % k, [[(a, dh1b, "tn")]], [], _ident, 4 * LANES, D, 256, D, [BF])[0]
                                   for k, a in enumerate((ret, fox))], axis=0)
    tok = push("out", gW)
    doa, doat, dl = _fox_cotangent(dmix, fox, T, tm)
    dqn, dkn, dfv, dfc4, drc4 = _fox_backward(qa, qat, ka, kat, va, doa, doat, lse + tok[0:1, 0:1], dl, T, tq)
    dfq, dfk, gq2, gk2 = _rw_bwd("fox_prep_bwd", _fox_prep_fn, [(P, LANES, 16, True), (P, LANES, 20, True)],
                                 [(g_fq2, LANES, 0, False), (g_fk2, LANES, 0, False)],
                                 [(dqn, LANES, 0, True), (dkn, LANES, 0, True)], T, tm, 4, [BF, BF], [True, True])
    gs["g_fox_q"] = gq2[:, :HEAD] + gq2[:, HEAD:]
    gs["g_fox_k"] = gk2[:, :HEAD] + gk2[:, HEAD:]
    dff, dbp = _fgate_bwd_col(ffp, bpad, jnp.sum(dfc4 + drc4, axis=0), T)
    gs["b_forget"] = dbp[:, :8]
    drq, drk, drv, drg, dg_ret = _ret_bwd(P, cos4, sin4, g_ret, lg, s0, dmix, T, tb)
    gs["g_ret_out"] = dg_ret
    dsecs = [drq, drk, drv, drg, dfq, dfk, dfv]
    dhn1, = _mm("proj_in_bwd", [[(d, w, "nn") for d, w in zip(dsecs, w_secs)] + [(dff, w_ffT, "nn")]], [], _ident,
                T, D, tm, D, [F32])
    g_secs = [_mm("dw_in_%d" % k, [[(d, hn1, "tn")]], [], _ident, 512, D, 256, D, [BF])[0] for k, d in enumerate(dsecs)]
    g_ff, = _mm("dw_in_ff", [[(dff, hn1, "tn")]], [], _ident, LANES, D, LANES, D, [BF])
    gW["w_inT"] = jnp.concatenate(g_secs + [g_ff[:8]], axis=0)
    tok = push("in", gW)
    grad_x, gs["g_mix"] = _rw_bwd("rms_mix_bwd", _rms_fn, [(x, D, 0, False)], [(tie(sp["g_mix"], tok), D, 0, False)],
                                  [(dhn1, D, 0, False)], T, tm, 1, [F32], [True], resid=dh1)
    return loss_part, grad_x, gs


_CANON = {"w_in": "w_inT", "w_xkv": "w_xkvT", "w_gate": "w_gateT", "w_up": "w_upT"}
_SMALL = (("g_mix", 0, 0, 1024), ("g_xattn", 1, 0, 1024), ("g_mem", 2, 0, 1024), ("g_ffn", 3, 0, 1024),
          ("g_ret_out", 4, 0, 512), ("g_xq", 4, 512, 256), ("g_xk", 4, 768, 256),
          ("g_fox_q", 5, 0, 64), ("g_fox_k", 5, 64, 64), ("b_forget", 5, 128, 8))
_LOSS_AT = (5, 256)


def _pack_shards(tree, dtype):
    parts = []
    for name, rows, padded, transposed in W_LAYOUT:
        a = tree[name][0]
        a = a.T if transposed else a
        parts.append(jnp.pad(a, ((0, padded - rows), (0, 0))).astype(dtype))
    return jnp.concatenate(parts, axis=0)


def _unpack_shards(packed, like):
    out = {}
    for name, rows, padded, transposed in W_LAYOUT:
        a = packed[W_OFF[name]:W_OFF[name] + rows]
        out[name] = (a.T if transposed else a)[None].reshape(like[name].shape)
    return out


def _pack_small(tree):
    rows = [jnp.zeros((1, D), F32) for _ in range(SMALL_ROWS)]
    buf = jnp.concatenate(rows, axis=0)
    for name, r, c, n in _SMALL:
        buf = lax.dynamic_update_slice(buf, tree[name].reshape(1, n).astype(F32), (r, c))
    return buf


def _unpack_small(buf, like):
    return {name: buf[r:r + 1, c:c + n].reshape(like[name].shape) for name, r, c, n in _SMALL}


def kernel(x, mem, g_mix, w_in, b_forget, g_ret_out, g_fox_q, g_fox_k, w_out, g_xattn, w_xq, w_xkv, g_mem, g_xq, g_xk, w_xo, g_ffn, w_gate, w_up, w_down, loss_target, m_g_mix, m_w_in, m_b_forget, m_g_ret_out, m_g_fox_q, m_g_fox_k, m_w_out, m_g_xattn, m_w_xq, m_w_xkv, m_g_mem, m_g_xq, m_g_xk, m_w_xo, m_g_ffn, m_w_gate, m_w_up, m_w_down, v_g_mix, v_w_in, v_b_forget, v_g_ret_out, v_g_fox_q, v_g_fox_k, v_w_out, v_g_xattn, v_w_xq, v_w_xkv, v_g_mem, v_g_xq, v_g_xk, v_w_xo, v_g_ffn, v_w_gate, v_w_up, v_w_down):
    names = ("g_mix", "w_in", "b_forget", "g_ret_out", "g_fox_q", "g_fox_k", "w_out", "g_xattn", "w_xq", "w_xkv", "g_mem",
             "g_xq", "g_xk", "w_xo", "g_ffn", "w_gate", "w_up", "w_down")
    w = dict(zip(names, (g_mix, w_in, b_forget, g_ret_out, g_fox_q, g_fox_k, w_out, g_xattn, w_xq, w_xkv, g_mem, g_xq, g_xk,
                         w_xo, g_ffn, w_gate, w_up, w_down)))
    m = dict(zip(names, (m_g_mix, m_w_in, m_b_forget, m_g_ret_out, m_g_fox_q, m_g_fox_k, m_w_out, m_g_xattn, m_w_xq, m_w_xkv,
                         m_g_mem, m_g_xq, m_g_xk, m_w_xo, m_g_ffn, m_w_gate, m_w_up, m_w_down)))
    v = dict(zip(names, (v_g_mix, v_w_in, v_b_forget, v_g_ret_out, v_g_fox_q, v_g_fox_k, v_w_out, v_g_xattn, v_w_xq, v_w_xkv,
                         v_g_mem, v_g_xq, v_g_xk, v_w_xo, v_g_ffn, v_w_gate, v_w_up, v_w_down)))
    small_names = [s[0] for s in _SMALL]

    gathered = _all_gather(_pack_shards(w, BF))
    W = {}
    for name, rows, padded, transposed in W_LAYOUT:
        full = gathered[:, W_OFF[name]:W_OFF[name] + rows].reshape(N_DEV * rows, D)
        W[_CANON.get(name, name)] = full

    sp = {n: w[n].reshape(1, -1) for n in small_names}
    loss_part, grad_x, gW, gs = _local_step(x[0], mem[0], loss_target[0], sp, W)

    chunks = []
    for name, rows, padded, transposed in W_LAYOUT:
        g = gW[_CANON.get(name, name)].reshape(N_DEV, rows, D)
        chunks.append(jnp.pad(g, ((0, 0), (0, padded - rows), (0, 0))).astype(BF))
    send = jnp.concatenate(chunks, axis=1)
    small = _pack_small(gs)
    small = lax.dynamic_update_slice(small, loss_part[:, :1], _LOSS_AT)
    recv, recv_small = _all_to_all(send, small)

    g_big, d_big, m_big, v_big = _adamw("adamw_shards", recv, _pack_shards(w, F32), _pack_shards(m, F32), _pack_shards(v, F32), 240)
    g_sm, d_sm, m_sm, v_sm = _adamw("adamw_small", recv_small, _pack_small(w), _pack_small(m), _pack_small(v), SMALL_ROWS)
    loss = g_sm[_LOSS_AT[0], _LOSS_AT[1]]

    outs = []
    for big, sm in ((g_big, g_sm), (d_big, d_sm), (m_big, m_sm), (v_big, v_sm)):
        tree = {**_unpack_shards(big, w), **_unpack_small(sm, w)}
        outs += [tree[n] for n in names]
    return (loss, grad_x[None], *outs)


def _pack_shards(tree, names, dtype):
    parts = []
    for name in names:
        rows, padded, transposed = W_SHARD[name]
        a = tree[name][0]
        a = a.T if transposed else a
        parts.append(jnp.pad(a, ((0, padded - rows), (0, 0))).astype(dtype))
    return jnp.concatenate(parts, axis=0)


def _unpack_shards(packed, names, like):
    out, off = {}, 0
    for name in names:
        rows, padded, transposed = W_SHARD[name]
        a = packed[off:off + rows]
        out[name] = (a.T if transposed else a)[None].reshape(like[name].shape)
        off += padded
    return out


def _unpack_gathered(gathered, names):
    out, off = {}, 0
    for name in names:
        rows, padded, _ = W_SHARD[name]
        out[_CANON.get(name, name)] = gathered[:, off:off + rows].reshape(N_DEV * rows, D)
        off += padded
    return out


def _pack_chunks(grads, names):
    chunks = []
    for name in names:
        rows, padded, _ = W_SHARD[name]
        g = grads[_CANON.get(name, name)].reshape(N_DEV, rows, D)
        chunks.append(jnp.pad(g, ((0, 0), (0, padded - rows), (0, 0))).astype(BF))
    return jnp.concatenate(chunks, axis=1)


def kernel(x, mem, g_mix, w_in, b_forget, g_ret_out, g_fox_q, g_fox_k, w_out, g_xattn, w_xq, w_xkv, g_mem, g_xq, g_xk, w_xo, g_ffn, w_gate, w_up, w_down, loss_target, m_g_mix, m_w_in, m_b_forget, m_g_ret_out, m_g_fox_q, m_g_fox_k, m_w_out, m_g_xattn, m_w_xq, m_w_xkv, m_g_mem, m_g_xq, m_g_xk, m_w_xo, m_g_ffn, m_w_gate, m_w_up, m_w_down, v_g_mix, v_w_in, v_b_forget, v_g_ret_out, v_g_fox_q, v_g_fox_k, v_w_out, v_g_xattn, v_w_xq, v_w_xkv, v_g_mem, v_g_xq, v_g_xk, v_w_xo, v_g_ffn, v_w_gate, v_w_up, v_w_down):
    names = ("g_mix", "w_in", "b_forget", "g_ret_out", "g_fox_q", "g_fox_k", "w_out", "g_xattn", "w_xq", "w_xkv", "g_mem",
             "g_xq", "g_xk", "w_xo", "g_ffn", "w_gate", "w_up", "w_down")
    w = dict(zip(names, (g_mix, w_in, b_forget, g_ret_out, g_fox_q, g_fox_k, w_out, g_xattn, w_xq, w_xkv, g_mem, g_xq, g_xk,
                         w_xo, g_ffn, w_gate, w_up, w_down)))
    m = dict(zip(names, (m_g_mix, m_w_in, m_b_forget, m_g_ret_out, m_g_fox_q, m_g_fox_k, m_w_out, m_g_xattn, m_w_xq, m_w_xkv,
                         m_g_mem, m_g_xq, m_g_xk, m_w_xo, m_g_ffn, m_w_gate, m_w_up, m_w_down)))
    v = dict(zip(names, (v_g_mix, v_w_in, v_b_forget, v_g_ret_out, v_g_fox_q, v_g_fox_k, v_w_out, v_g_xattn, v_w_xq, v_w_xkv,
                         v_g_mem, v_g_xq, v_g_xk, v_w_xo, v_g_ffn, v_w_gate, v_w_up, v_w_down)))
    small_names = [s[0] for s in _SMALL]
    me = 4 * lax.axis_index("x") + 2 * lax.axis_index("y") + lax.axis_index("c")

    first = _all_gather(_pack_shards(w, GATHER_FIRST, BF))
    first, rest_shard = lax.optimization_barrier((first, _pack_shards(w, GATHER_REST, BF)))
    rest_started = _exchange_start("gather_rest_start", rest_shard,
                                   jnp.broadcast_to(rest_shard[None], (N_DEV,) + rest_shard.shape), scatter=False)

    def fetch_rest(after):
        return _unpack_gathered(_exchange_wait("gather_rest_wait", rest_started, after, scatter=False)[1], GATHER_REST)

    pushed = {}

    def push(group, grads):
        send = _pack_chunks(grads, GRAD_GROUPS[group])
        pushed[group] = _exchange_start("scatter_%s_start" % group, send, jnp.zeros(send.shape, BF), scatter=True)
        return pushed[group][4]

    sp = {n: w[n].reshape(1, -1) for n in small_names}
    loss_part, grad_x, g_last, gs = _local_step(x[0], mem[0], loss_target[0], sp, _unpack_gathered(first, GATHER_FIRST)["w_inT"],
                                                rest_started[4], fetch_rest, push)

    small = lax.dynamic_update_slice(_pack_small(gs), loss_part[:, :1], _LOSS_AT)
    recv_mix, recv_small = _all_to_all(_pack_chunks(g_last, GRAD_GROUPS["mix"]), small)

    results = {}
    for group in ("ffn", "xattn", "mix"):
        gnames = GRAD_GROUPS[group]
        wp, mp, vp = (_pack_shards(t, gnames, F32) for t in (w, m, v))
        if group == "mix":
            res = _adamw("adamw_mix", recv_mix, wp, mp, vp, 16)
        else:
            sent, recv = _exchange_wait("scatter_%s_wait" % group, pushed[group], recv_small, scatter=True)
            own = lax.dynamic_index_in_dim(sent, me, axis=0, keepdims=False)
            res = _adamw("adamw_%s" % group, recv, wp, mp, vp, {"ffn": 176, "xattn": 128}[group], own=own)
        results[group] = [_unpack_shards(r, gnames, w) for r in res]
    g_sm, d_sm, m_sm, v_sm = _adamw("adamw_small", recv_small, _pack_small(w), _pack_small(m), _pack_small(v), SMALL_ROWS)
    loss = g_sm[_LOSS_AT[0], _LOSS_AT[1]]

    outs = []
    for k, sm in enumerate((g_sm, d_sm, m_sm, v_sm)):
        tree = _unpack_small(sm, w)
        for group in results:
            tree.update(results[group][k])
        outs += [tree[n] for n in names]
    return (loss, grad_x[None], *outs)


SCATTER_GROUPS = {"ffn": ("w_gate", "w_up", "w_down"), "xattn": ("w_xq", "w_xo", "w_xkv"), "out": ("w_out",), "in": ("w_in",)}


def _adam_update(g, w, m, v):
    m2 = ADAM_B1 * m + (1.0 - ADAM_B1) * g
    v2 = ADAM_B2 * v + (1.0 - ADAM_B2) * jnp.square(g)
    m_hat = m2 / (1.0 - ADAM_B1 ** ADAM_STEP)
    v_hat = v2 / (1.0 - ADAM_B2 ** ADAM_STEP)
    return g, -ADAM_LR * (m_hat / (jnp.sqrt(v_hat) + ADAM_EPS) + ADAM_WD * w), m2, v2


def _adamw_shard(name, recv, own, off, w, m, v):
    rows, padded, transposed = W_SHARD[name.split(":")[1]]
    assert off % padded == 0
    blk = off // padded

    def total(s_ref, own_ref):
        g = own_ref[...].astype(F32)
        for s in range(N_DEV):
            g = g + s_ref[s].astype(F32)
        return g

    canonical_view = name.endswith(":")
    if transposed and rows == padded and not canonical_view:
        res = _adamw_shard(name + ":", recv, own, off, *(jnp.swapaxes(a, 1, 2) for a in (w, m, v)))
        return [jnp.swapaxes(r, 1, 2) for r in res]

    if canonical_view or not transposed:
        def body(s_ref, own_ref, w_ref, m_ref, v_ref, *outs):
            for o, r in zip(outs, _adam_update(total(s_ref, own_ref), w_ref[0], m_ref[0], v_ref[0])):
                o[0] = r

        full = pl.BlockSpec((1, rows, D), lambda i: (0, 0, 0))
        return pl.pallas_call(
            body, name=name.replace(":", "_"), grid=(1,),
            in_specs=[pl.BlockSpec((N_DEV, padded, D), lambda i: (0, blk, 0)), pl.BlockSpec((padded, D), lambda i: (blk, 0)),
                      full, full, full],
            out_specs=[full] * 4, out_shape=[jax.ShapeDtypeStruct((1, rows, D), F32)] * 4,
            compiler_params=_params(("arbitrary",)),
        )(recv, own, w, m, v)

    wide = -(-padded // LANES) * LANES

    def body(s_ref, own_ref, w_ref, m_ref, v_ref, *outs):
        g = total(s_ref, own_ref)
        if wide > padded:
            g = jnp.concatenate([g, jnp.zeros((wide - padded, LANES), F32)], axis=0)
        g = g.T[:, :rows]
        for o, r in zip(outs, _adam_update(g, w_ref[0], m_ref[0], v_ref[0])):
            o[0] = r

    cols = pl.BlockSpec((1, LANES, rows), lambda c: (0, c, 0))
    return pl.pallas_call(
        body, name=name.replace(":", "_"), grid=(D // LANES,),
        in_specs=[pl.BlockSpec((N_DEV, padded, LANES), lambda c: (0, blk, c)), pl.BlockSpec((padded, LANES), lambda c: (blk, c)),
                  cols, cols, cols],
        out_specs=[cols] * 4, out_shape=[jax.ShapeDtypeStruct((1, D, rows), F32)] * 4,
        compiler_params=_params(("arbitrary",)),
    )(recv, own, w, m, v)


def _gather_small(small):
    def body(small_ref, out_ref, send_sems, recv_sems, local_sem):
        x, y, c = _place()
        me = 4 * x + 2 * y + c
        mine = pltpu.make_async_copy(small_ref, out_ref.at[me], local_sem)
        mine.start()
        copies = []
        for r in range(1, N_DEV):
            px, py, pc = x ^ (r >> 2), y ^ ((r >> 1) & 1), c ^ (r & 1)
            copies.append(pltpu.make_async_remote_copy(
                src_ref=small_ref, dst_ref=out_ref.at[me], send_sem=send_sems.at[r - 1], recv_sem=recv_sems.at[r - 1],
                device_id=(px, py, pc), device_id_type=MESH))
        for cp in copies:
            cp.start()
        for cp in copies:
            cp.wait_recv()
        for cp in copies:
            cp.wait_send()
        mine.wait()

    return pl.pallas_call(
        body, name="gather_small",
        out_shape=jax.ShapeDtypeStruct((N_DEV,) + small.shape, small.dtype),
        in_specs=[pl.BlockSpec(memory_space=pl.ANY)], out_specs=pl.BlockSpec(memory_space=pl.ANY),
        scratch_shapes=[pltpu.SemaphoreType.DMA((N_DEV - 1,)), pltpu.SemaphoreType.DMA((N_DEV - 1,)), pltpu.SemaphoreType.DMA],
    )(small)


def _pack_chunks(grads, names):
    chunks = []
    for name in names:
        rows, padded, _ = W_SHARD[name]
        g = grads[_CANON.get(name, name)].reshape(N_DEV, rows, D)
        chunks.append(jnp.pad(g, ((0, 0), (0, padded - rows), (0, 0))).astype(BF))
    return chunks[0] if len(chunks) == 1 else jnp.concatenate(chunks, axis=1)


def kernel(x, mem, g_mix, w_in, b_forget, g_ret_out, g_fox_q, g_fox_k, w_out, g_xattn, w_xq, w_xkv, g_mem, g_xq, g_xk, w_xo, g_ffn, w_gate, w_up, w_down, loss_target, m_g_mix, m_w_in, m_b_forget, m_g_ret_out, m_g_fox_q, m_g_fox_k, m_w_out, m_g_xattn, m_w_xq, m_w_xkv, m_g_mem, m_g_xq, m_g_xk, m_w_xo, m_g_ffn, m_w_gate, m_w_up, m_w_down, v_g_mix, v_w_in, v_b_forget, v_g_ret_out, v_g_fox_q, v_g_fox_k, v_w_out, v_g_xattn, v_w_xq, v_w_xkv, v_g_mem, v_g_xq, v_g_xk, v_w_xo, v_g_ffn, v_w_gate, v_w_up, v_w_down):
    names = ("g_mix", "w_in", "b_forget", "g_ret_out", "g_fox_q", "g_fox_k", "w_out", "g_xattn", "w_xq", "w_xkv", "g_mem",
             "g_xq", "g_xk", "w_xo", "g_ffn", "w_gate", "w_up", "w_down")
    w = dict(zip(names, (g_mix, w_in, b_forget, g_ret_out, g_fox_q, g_fox_k, w_out, g_xattn, w_xq, w_xkv, g_mem, g_xq, g_xk,
                         w_xo, g_ffn, w_gate, w_up, w_down)))
    m = dict(zip(names, (m_g_mix, m_w_in, m_b_forget, m_g_ret_out, m_g_fox_q, m_g_fox_k, m_w_out, m_g_xattn, m_w_xq, m_w_xkv,
                         m_g_mem, m_g_xq, m_g_xk, m_w_xo, m_g_ffn, m_w_gate, m_w_up, m_w_down)))
    v = dict(zip(names, (v_g_mix, v_w_in, v_b_forget, v_g_ret_out, v_g_fox_q, v_g_fox_k, v_w_out, v_g_xattn, v_w_xq, v_w_xkv,
                         v_g_mem, v_g_xq, v_g_xk, v_w_xo, v_g_ffn, v_w_gate, v_w_up, v_w_down)))
    small_names = [s[0] for s in _SMALL]
    me = 4 * lax.axis_index("x") + 2 * lax.axis_index("y") + lax.axis_index("c")

    first = _all_gather(_pack_shards(w, GATHER_FIRST, BF))
    first, rest_shard = lax.optimization_barrier((first, _pack_shards(w, GATHER_REST, BF)))
    rest_started = _exchange_start("gather_rest_start", rest_shard,
                                   jnp.broadcast_to(rest_shard[None], (N_DEV,) + rest_shard.shape), scatter=False)

    def fetch_rest(after):
        return _unpack_gathered(_exchange_wait("gather_rest_wait", rest_started, after, scatter=False)[1], GATHER_REST)

    pushed = {}

    def push(group, grads):
        send = _pack_chunks(grads, SCATTER_GROUPS[group])
        pushed[group] = _exchange_start("scatter_%s_start" % group, send, jnp.zeros(send.shape, BF), scatter=True)
        return pushed[group][4]

    sp = {n: w[n].reshape(1, -1) for n in small_names}
    loss_part, grad_x, gs = _local_step(x[0], mem[0], loss_target[0], sp, _unpack_gathered(first, GATHER_FIRST)["w_inT"],
                                        rest_started[4], fetch_rest, push)

    small = lax.dynamic_update_slice(_pack_small(gs), loss_part[:, :1], _LOSS_AT)
    recv_small = _gather_small(small)
    g_sm, d_sm, m_sm, v_sm = _adamw("adamw_small", recv_small, _pack_small(w), _pack_small(m), _pack_small(v), SMALL_ROWS)
    loss = g_sm[_LOSS_AT[0], _LOSS_AT[1]]

    results, after = {}, recv_small
    for group in ("ffn", "xattn", "out", "in"):
        sent, recv = _exchange_wait("scatter_%s_wait" % group, pushed[group], after, scatter=True)
        own = lax.dynamic_index_in_dim(sent, me, axis=0, keepdims=False)
        off = 0
        for name in SCATTER_GROUPS[group]:
            results[name] = _adamw_shard("adamw:" + name, recv, own, off, w[name], m[name], v[name])
            off += W_SHARD[name][1]
        after = results[SCATTER_GROUPS[group][-1]][0]

    outs = []
    for k, sm in enumerate((g_sm, d_sm, m_sm, v_sm)):
        tree = _unpack_small(sm, w)
        tree.update({name: res[k] for name, res in results.items()})
        outs += [tree[n] for n in names]
    return (loss, grad_x[None], *outs)
```

```python
import functools
import math

import jax
import jax.numpy as jnp
import numpy as np
from jax import lax
from jax.experimental import pallas as pl
from jax.experimental.pallas import tpu as pltpu

F32 = jnp.float32
BF = jnp.bfloat16

D = 1024
HEAD = 64
CHUNK = 64
N_MEM = 256
XHEAD = 256
D_FF = 2816
EPS = 1e-6
NEG = -1e30
LANES = 128
N_DEV = 8
V7X_VMEM_BYTES = 64 * 1024 * 1024
VMEM_LIMIT = V7X_VMEM_BYTES - 8 * 1024 * 1024

ADAM_LR, ADAM_B1, ADAM_B2, ADAM_EPS, ADAM_WD, ADAM_STEP = 0.001, 0.9, 0.999, 1e-08, 0.01, 10

W_LAYOUT = (("w_in", 449, 464, True), ("w_out", 128, 128, False), ("w_xq", 128, 128, False), ("w_xkv", 256, 256, True),
            ("w_xo", 128, 128, False), ("w_gate", 352, 352, True), ("w_up", 352, 352, True), ("w_down", 352, 352, False))
W_ROWS = sum(w[2] for w in W_LAYOUT)
W_OFF = {}
_o = 0
for _n, _r, _p, _t in W_LAYOUT:
    W_OFF[_n] = _o
    _o += _p
SMALL_ROWS = 8
W_SHARD = {"w_in": (449, 449, True), "w_out": (128, 128, False), "w_xq": (128, 128, False), "w_xkv": (256, 256, True),
           "w_xo": (128, 128, False), "w_gate": (352, 352, True), "w_up": (352, 352, True), "w_down": (352, 352, False)}
GATHER_FIRST = ("w_in",)
GATHER_REST = ("w_out", "w_xq", "w_xkv", "w_xo", "w_gate", "w_up", "w_down")
GRAD_GROUPS = {"ffn": ("w_gate", "w_up", "w_down"), "xattn": ("w_xq", "w_xkv", "w_xo"), "mix": ("w_in", "w_out")}

NT = (((1,), (1,)), ((), ()))
NN = (((1,), (0,)), ((), ()))
TN = (((0,), (0,)), ((), ()))
_DIMS = {"nn": NN, "nt": NT, "tn": TN}


def _params(sem):
    return pltpu.CompilerParams(dimension_semantics=sem, vmem_limit_bytes=VMEM_LIMIT)


def _mm(name, products, extras, epilogue, M, N, tm, tn, out_dtypes):
    flat = [t for p in products for t in p]
    counts = [len(p) for p in products]
    in_specs, args = [], []
    for a, b, form in flat:
        if form == "tn":
            in_specs.append(pl.BlockSpec((a.shape[0], tm), lambda i, j: (0, i)))
        else:
            in_specs.append(pl.BlockSpec((tm, a.shape[1]), lambda i, j: (i, 0)))
        if form == "nt":
            in_specs.append(pl.BlockSpec((tn, b.shape[1]), lambda i, j: (j, 0)))
        else:
            in_specs.append(pl.BlockSpec((b.shape[0], tn), lambda i, j: (0, j)))
        args += [a, b]
    for e in extras:
        in_specs.append(pl.BlockSpec((tm, tn), lambda i, j: (i, j)))
        args.append(e)
    n_in = len(args)

    def body(*refs):
        ins, outs = refs[:n_in], refs[n_in:]
        prods, p = [], 0
        for c in counts:
            acc = None
            for _ in range(c):
                a = ins[2 * p][...].astype(BF)
                b = ins[2 * p + 1][...].astype(BF)
                d = lax.dot_general(a, b, _DIMS[flat[p][2]], preferred_element_type=F32)
                acc = d if acc is None else acc + d
                p += 1
            prods.append(acc)
        ex = [r[...].astype(F32) for r in ins[2 * len(flat):]]
        res = epilogue(*prods, *ex)
        for o, r in zip(outs, res):
            o[...] = r.astype(o.dtype)

    return pl.pallas_call(
        body, name=name, grid=(M // tm, N // tn), in_specs=in_specs,
        out_specs=[pl.BlockSpec((tm, tn), lambda i, j: (i, j)) for _ in out_dtypes],
        out_shape=[jax.ShapeDtypeStruct((M, N), dt) for dt in out_dtypes],
        compiler_params=_params(("parallel", "arbitrary")),
    )(*args)


def _ident(x):
    return (x,)


def _add(x, r):
    return (x + r,)


def _spec(rows, w, off, per_j):
    if per_j:
        return pl.BlockSpec((rows, w), lambda j, i: (i, off + j))
    return pl.BlockSpec((rows, w), lambda j, i: (i, off))


def _pspec(rows, w, off, per_j):
    if per_j:
        return pl.BlockSpec((rows, w), lambda j, i: (0, off + j))
    return pl.BlockSpec((rows, w), lambda j, i: (0, off))


def _rw_fwd(name, fn, rows, params, outs, T, tm, nj, n_acc=0):
    in_specs = [_spec(tm, w, off, pj) for _, w, off, pj in rows] + [_pspec(a.shape[0], w, off, pj) for a, w, off, pj in params]
    args = [r[0] for r in rows] + [p[0] for p in params]
    n_in, n_out = len(args), len(outs)
    out_specs = [pl.BlockSpec((tm, w), lambda j, i: (i, j)) for _, w in outs]
    out_shape = [jax.ShapeDtypeStruct((T, nj * w), dt) for dt, w in outs]
    out_specs += [pl.BlockSpec((1, LANES), lambda j, i: (0, 0)) for _ in range(n_acc)]
    out_shape += [jax.ShapeDtypeStruct((1, LANES), F32) for _ in range(n_acc)]

    def body(*refs):
        vals = [r[...].astype(F32) for r in refs[:n_in]]
        res = fn(*vals)
        orefs = refs[n_in:]
        for k in range(n_out):
            orefs[k][...] = res[k].astype(orefs[k].dtype)
        first = (pl.program_id(0) == 0) & (pl.program_id(1) == 0)
        for k in range(n_acc):
            @pl.when(first)
            def _(k=k):
                orefs[n_out + k][...] = jnp.zeros((1, LANES), F32)
            orefs[n_out + k][...] += res[n_out + k]

    return pl.pallas_call(
        body, name=name, grid=(nj, T // tm), in_specs=in_specs, out_specs=out_specs, out_shape=out_shape,
        compiler_params=_params(("arbitrary", "arbitrary")),
    )(*args)


def _rw_bwd(name, fn, rows, params, cots, T, tm, nj, row_grads, param_grads, resid=None):
    in_specs = ([_spec(tm, w, off, pj) for _, w, off, pj in rows] + [_pspec(a.shape[0], w, off, pj) for a, w, off, pj in params]
                + [_spec(tm, w, off, pj) for _, w, off, pj in cots])
    args = [r[0] for r in rows] + [p[0] for p in params] + [c[0] for c in cots]
    if resid is not None:
        in_specs.append(_spec(tm, rows[0][1], rows[0][2], rows[0][3]))
        args.append(resid)
    nr, npar, nc = len(rows), len(params), len(cots)
    out_specs, out_shape, kinds = [], [], []
    for k, dts in enumerate(row_grads):
        for dt in (dts if isinstance(dts, (list, tuple)) else [dts]):
            if dt is not None:
                w = rows[k][1]
                out_specs.append(pl.BlockSpec((tm, w), lambda j, i: (i, j)))
                out_shape.append(jax.ShapeDtypeStruct((T, nj * w), dt))
                kinds.append(("row", k))
    for k, need in enumerate(param_grads):
        if need:
            a, w, off, pj = params[k]
            out_specs.append(_pspec(a.shape[0], w, off, pj))
            out_shape.append(jax.ShapeDtypeStruct(a.shape, F32))
            kinds.append(("par", k))

    def body(*refs):
        vals = [r[...].astype(F32) for r in refs[:nr + npar]]
        ct = tuple(r[...].astype(F32) for r in refs[nr + npar:nr + npar + nc])
        _, vjp = jax.vjp(lambda *a: tuple(fn(*a)), *vals)
        grads = list(vjp(ct))
        n_in = nr + npar + nc + (resid is not None)
        if resid is not None:
            grads[0] = grads[0] + refs[n_in - 1][...].astype(F32)
        orefs = refs[n_in:]
        j, i = pl.program_id(0), pl.program_id(1)
        for o, (kind, k) in zip(orefs, kinds):
            if kind == "row":
                o[...] = grads[k].astype(o.dtype)
            else:
                first = (i == 0) if params[k][3] else ((i == 0) & (j == 0))

                @pl.when(first)
                def _(o=o):
                    o[...] = jnp.zeros(o.shape, F32)
                o[...] += grads[nr + k]

    return pl.pallas_call(
        body, name=name, grid=(nj, T // tm), in_specs=in_specs, out_specs=out_specs, out_shape=out_shape,
        compiler_params=_params(("arbitrary", "arbitrary")),
    )(*args)


def _rms(x, g):
    return x * lax.rsqrt(jnp.mean(x * x, axis=-1, keepdims=True) + EPS) * g


def _rms_fn(x, g):
    return (_rms(x, g),)


def _lo_mask():
    return lax.broadcasted_iota(jnp.int32, (1, LANES), 1) < HEAD


def _gmean(x, lo):
    s0 = jnp.sum(jnp.where(lo, x, 0.0), axis=-1, keepdims=True)
    s1 = jnp.sum(jnp.where(lo, 0.0, x), axis=-1, keepdims=True)
    return jnp.where(lo, s0, s1) * (1.0 / HEAD)


def _fox_prep_fn(fq, fk, gq, gk):
    lo = _lo_mask()
    qn = fq * lax.rsqrt(_gmean(fq * fq, lo) + EPS) * gq * (HEAD ** -0.5)
    kn = fk * lax.rsqrt(_gmean(fk * fk, lo) + EPS) * gk
    return qn, kn


def _cast_fn(v):
    return (v,)


@jax.custom_vjp
def _swap_halves(x):
    bit = (lax.broadcasted_iota(jnp.int32, (1, LANES), 1) & (HEAD // 2)) == 0
    return jnp.where(bit, pltpu.roll(x, LANES - HEAD // 2, 1), pltpu.roll(x, HEAD // 2, 1))


_swap_halves.defvjp(lambda x: (_swap_halves(x), None), lambda _, g: (_swap_halves(g),))


def _ret_fn(rq, rk, rv, rg, cos, sin, s_in, g, lg):
    tb = rq.shape[0]
    nc = tb // CHUNK
    lo = _lo_mask()
    row = lax.broadcasted_iota(jnp.int32, (LANES, 1), 0) < HEAD
    same_head = row == lo
    q = (rq * cos + _swap_halves(rq) * sin) * (HEAD ** -0.5)
    k = rk * cos + _swap_halves(rk) * sin
    q3, k3, v3 = q.reshape(nc, CHUNK, LANES), k.reshape(nc, CHUNK, LANES), rv.reshape(nc, CHUNK, LANES)
    pos = lax.broadcasted_iota(jnp.int32, (CHUNK, 1), 0).astype(F32)
    q_decay = jnp.exp(lg * (pos + 1.0))
    k_decay = jnp.exp(lg * (CHUNK - 1.0 - pos))
    chunk_decay = jnp.exp(lg * float(CHUNK))
    dist = jnp.abs(lax.broadcasted_iota(jnp.int32, (CHUNK, CHUNK), 0) - lax.broadcasted_iota(jnp.int32, (CHUNK, CHUNK), 1)).astype(F32)
    v3b = v3.astype(BF)
    intra = []
    for hh in range(2):
        hm = lo if hh == 0 else ~lo
        lg_h = lg[:, hh * HEAD:hh * HEAD + 1]
        qm = jnp.where(hm, q3, 0.0).astype(BF)
        sc = jnp.einsum("nid,njd->nij", qm, k3.astype(BF), preferred_element_type=F32) * jnp.exp(lg_h * dist)[None]
        intra.append(jnp.einsum("nij,nje->nie", sc.astype(BF), v3b, preferred_element_type=F32))
    o = jnp.where(lo, intra[0], intra[1])
    kv = jnp.einsum("njd,nje->nde", (k3 * k_decay[None]).astype(BF), v3b, preferred_element_type=F32)
    kv = jnp.where(same_head[None], kv, 0.0)
    state, states = s_in, []
    for n in range(nc):
        states.append(state)
        state = state * chunk_decay + kv[n]
    s_prev = jnp.stack(states, axis=0)
    o = o + jnp.einsum("nid,nde->nie", (q3 * q_decay[None]).astype(BF), s_prev.astype(BF), preferred_element_type=F32)
    o = o.reshape(tb, LANES)
    mu = _gmean(o, lo)
    oc = o - mu
    y = oc * lax.rsqrt(_gmean(oc * oc, lo) + EPS) * g
    return jax.nn.silu(rg) * y, state


def _xattn_fn(qx, gq, gk, kk, vv):
    q = _rms(qx, gq)
    k = _rms(kk, gk)
    logits = lax.dot_general(q.astype(BF), k.astype(BF), NT, preferred_element_type=F32) * (XHEAD ** -0.5)
    p = jax.nn.softmax(logits, axis=-1)
    return (jnp.dot(p.astype(BF), vv.astype(BF), preferred_element_type=F32),)


def _swiglu_fwd_epi(g, u):
    return g, u, jax.nn.silu(g) * u


def _swiglu_bwd_epi(dact, g, u):
    _, vjp = jax.vjp(lambda a, b: jax.nn.silu(a) * b, g, u)
    return vjp(dact)


def _loss_fn(h, target):
    err = h - target
    part = jnp.sum(jnp.sum(err * err, axis=0, keepdims=True), axis=-1, keepdims=True) * (0.5 / D)
    dy = err * (1.0 / D)
    return dy, dy, part


def _ret_fwd(P, cos, sin, g_ret, lg, T, tb):
    nb = T // tb

    def body(rq, rk, rv, rg, c, s, g, l, o_ref, s0_ref, state):
        @pl.when(pl.program_id(1) == 0)
        def _():
            state[...] = jnp.zeros(state.shape, F32)
        s0_ref[0, 0] = state[...]
        out, s_new = _ret_fn(rq[...], rk[...], rv[...], rg[...], c[...], s[...], state[...], g[...], l[...])
        o_ref[...] = out
        state[...] = s_new

    sec = lambda off: pl.BlockSpec((tb, LANES), lambda j, i: (i, off + j))
    tab = pl.BlockSpec((tb, LANES), lambda j, i: (i, 0))
    par = pl.BlockSpec((1, LANES), lambda j, i: (0, j))
    return pl.pallas_call(
        body, name="ret_fwd", grid=(4, nb),
        in_specs=[sec(0), sec(4), sec(8), sec(12), tab, tab, par, par],
        out_specs=[pl.BlockSpec((tb, LANES), lambda j, i: (i, j)), pl.BlockSpec((1, 1, LANES, LANES), lambda j, i: (j, i, 0, 0))],
        out_shape=[jax.ShapeDtypeStruct((T, 4 * LANES), F32), jax.ShapeDtypeStruct((4, nb, LANES, LANES), F32)],
        scratch_shapes=[pltpu.VMEM((LANES, LANES), F32)],
        compiler_params=_params(("arbitrary", "arbitrary")),
    )(P, P, P, P, cos, sin, g_ret, lg)


def _ret_bwd(P, cos, sin, g_ret, lg, s0, dmix, T, tb):
    nb = T // tb

    def body(rq, rk, rv, rg, c, s, g, l, s0_ref, do, drq, drk, drv, drg, dg, dstate):
        i = pl.program_id(1)

        @pl.when(i == 0)
        def _():
            dstate[...] = jnp.zeros(dstate.shape, F32)
            dg[...] = jnp.zeros(dg.shape, F32)

        cc, ss, ll = c[...], s[...], l[...]
        _, vjp = jax.vjp(lambda a, b, v, gate, st, gg: _ret_fn(a, b, v, gate, cc, ss, st, gg, ll),
                         rq[...], rk[...], rv[...], rg[...], s0_ref[0, 0], g[...])
        ga, gb, gv, ggate, gst, ggain = vjp((do[...], dstate[...]))
        drq[...] = ga.astype(drq.dtype)
        drk[...] = gb.astype(drk.dtype)
        drv[...] = gv.astype(drv.dtype)
        drg[...] = ggate.astype(drg.dtype)
        dstate[...] = gst
        dg[...] += ggain

    rev = lambda i: nb - 1 - i
    sec = lambda off: pl.BlockSpec((tb, LANES), lambda j, i: (rev(i), off + j))
    tab = pl.BlockSpec((tb, LANES), lambda j, i: (rev(i), 0))
    par = pl.BlockSpec((1, LANES), lambda j, i: (0, j))
    outb = pl.BlockSpec((tb, LANES), lambda j, i: (rev(i), j))
    return pl.pallas_call(
        body, name="ret_bwd", grid=(4, nb),
        in_specs=[sec(0), sec(4), sec(8), sec(12), tab, tab, par, par,
                  pl.BlockSpec((1, 1, LANES, LANES), lambda j, i: (j, rev(i), 0, 0)), outb],
        out_specs=[outb, outb, outb, outb, par],
        out_shape=[jax.ShapeDtypeStruct((T, 4 * LANES), BF)] * 4 + [jax.ShapeDtypeStruct((1, 4 * LANES), F32)],
        scratch_shapes=[pltpu.VMEM((LANES, LANES), F32)],
        compiler_params=_params(("arbitrary", "arbitrary")),
    )(P, P, P, P, cos, sin, g_ret, lg, s0, dmix)


_FB = 128


def _tri(lower):
    r = lax.broadcasted_iota(jnp.int32, (_FB, _FB), 0)
    c = lax.broadcasted_iota(jnp.int32, (_FB, _FB), 1)
    return ((r >= c) if lower else (r <= c)).astype(F32)


def _fgate_fwd(ffp, bpad, T):
    def body(ff_ref, b_ref, fc_ref, fr_ref):
        lane = lax.broadcasted_iota(jnp.int32, (1, LANES), 1)
        tri = _tri(True)
        carry = jnp.zeros((1, LANES), F32)
        for blk in range(T // _FB):
            z = ff_ref[blk * _FB:(blk + 1) * _FB, :] + b_ref[...]
            lf = jnp.where(lane < 8, jax.nn.log_sigmoid(z), 0.0)
            f = jnp.dot(tri, lf, precision=lax.Precision.HIGHEST, preferred_element_type=F32) + carry
            carry = f[_FB - 1:_FB, :]
            fc_ref[blk * _FB:(blk + 1) * _FB, :] = f
            fr_ref[:, blk * _FB:(blk + 1) * _FB] = f.T[:8, :]

    return pl.pallas_call(
        body, name="fgate_fwd",
        out_shape=[jax.ShapeDtypeStruct((T, LANES), F32), jax.ShapeDtypeStruct((8, T), F32)],
        compiler_params=pltpu.CompilerParams(vmem_limit_bytes=VMEM_LIMIT),
    )(ffp, bpad)


def _fgate_bwd(ffp, bpad, dfr, T):
    def body(ff_ref, b_ref, dfr_ref, dff_ref, db_ref):
        lane = lax.broadcasted_iota(jnp.int32, (1, LANES), 1)
        tri = _tri(False)
        carry = jnp.zeros((1, LANES), F32)
        db = jnp.zeros((1, LANES), F32)
        for blk in reversed(range(T // _FB)):
            d8 = dfr_ref[:, blk * _FB:(blk + 1) * _FB]
            dcol = jnp.concatenate([d8, jnp.zeros((_FB - 8, _FB), F32)], axis=0).T
            dlf = jnp.dot(tri, dcol, precision=lax.Precision.HIGHEST, preferred_element_type=F32) + carry
            carry = dlf[0:1, :]
            z = ff_ref[blk * _FB:(blk + 1) * _FB, :] + b_ref[...]
            dz = jnp.where(lane < 8, dlf * jax.nn.sigmoid(-z), 0.0)
            dff_ref[blk * _FB:(blk + 1) * _FB, :] = dz.astype(dff_ref.dtype)
            db = db + jnp.sum(dz, axis=0, keepdims=True)
        db_ref[...] = db

    return pl.pallas_call(
        body, name="fgate_bwd",
        out_shape=[jax.ShapeDtypeStruct((T, LANES), BF), jax.ShapeDtypeStruct((1, LANES), F32)],
        compiler_params=pltpu.CompilerParams(vmem_limit_bytes=VMEM_LIMIT),
    )(ffp, bpad, dfr)


def _head_bias_col(fc, head):
    lane = lax.broadcasted_iota(jnp.int32, (1, LANES), 1)
    return jnp.sum(jnp.where(lane == head, fc, 0.0), axis=-1, keepdims=True)


def _head_bias_row(fr, head):
    sub = lax.broadcasted_iota(jnp.int32, (8, 1), 0)
    return jnp.sum(jnp.where(sub == head, fr, 0.0), axis=0, keepdims=True)


def _fox_fwd(qn, kn, vb, fc, fr, T, tq):
    nq = T // tq

    def body(q_ref, k_ref, v_ref, fc_ref, fr_ref, o_ref, c_ref):
        j, i = pl.program_id(0), pl.program_id(1)
        lane = lax.broadcasted_iota(jnp.int32, (1, LANES), 1)
        lo = lane < HEAD
        causal = lax.broadcasted_iota(jnp.int32, (tq, tq), 0) >= lax.broadcasted_iota(jnp.int32, (tq, tq), 1)
        q = q_ref[...]
        fcb = fc_ref[...]
        outs, cs = [], []
        for hh in range(2):
            hm = lo if hh == 0 else ~lo
            head = 2 * j + hh
            qh = jnp.where(hm, q, jnp.zeros_like(q))
            fq = _head_bias_col(fcb, head)

            def block(kb, carry, diag, qh=qh, fq=fq, head=head):
                m, l, acc = carry
                k0 = pl.multiple_of(kb * tq, tq)
                k = k_ref[pl.ds(k0, tq), :]
                v = v_ref[pl.ds(k0, tq), :]
                fk = _head_bias_row(fr_ref[:, pl.ds(k0, tq)], head)
                s = (lax.dot_general(qh, k, NT, preferred_element_type=F32) + fq) - fk
                if diag:
                    s = jnp.where(causal, s, NEG)
                m2 = jnp.maximum(m, jnp.max(s, axis=-1, keepdims=True))
                p = jnp.exp(s - m2)
                a = jnp.exp(m - m2)
                return m2, a * l + jnp.sum(p, axis=-1, keepdims=True), a * acc + jnp.dot(p.astype(BF), v, preferred_element_type=F32)

            init = (jnp.full((tq, 1), NEG, F32), jnp.zeros((tq, 1), F32), jnp.zeros((tq, LANES), F32))
            carry = lax.fori_loop(0, i, lambda kb, c: block(kb, c, False), init)
            m, l, acc = block(i, carry, True)
            outs.append(acc / l)
            cs.append(fq - (m + jnp.log(l)))
        o_ref[...] = jnp.where(lo, outs[0], outs[1])
        c_ref[0] = jnp.where(lane == 0, cs[0], jnp.where(lane == 1, cs[1], 0.0))

    full = lambda: pl.BlockSpec((T, LANES), lambda j, i: (0, j))
    return pl.pallas_call(
        body, name="fox_fwd", grid=(4, nq),
        in_specs=[pl.BlockSpec((tq, LANES), lambda j, i: (i, j)), full(), full(),
                  pl.BlockSpec((tq, LANES), lambda j, i: (i, 0)), pl.BlockSpec((8, T), lambda j, i: (0, 0))],
        out_specs=[pl.BlockSpec((tq, LANES), lambda j, i: (i, j)), pl.BlockSpec((1, tq, LANES), lambda j, i: (j, i, 0))],
        out_shape=[jax.ShapeDtypeStruct((T, 4 * LANES), F32), jax.ShapeDtypeStruct((4, T, LANES), F32)],
        compiler_params=_params(("parallel", "arbitrary")),
    )(qn, kn, vb, fc, fr)


def _fox_bwd_dq(qn, kn, vb, fr, cq, dmix, T, tq):
    nq = T // tq

    def body(q_ref, k_ref, v_ref, fr_ref, c_ref, do_ref, dq_ref, dl_ref, p_scr, dp_scr):
        j, i = pl.program_id(0), pl.program_id(1)
        lane = lax.broadcasted_iota(jnp.int32, (1, LANES), 1)
        lo = lane < HEAD
        causal = lax.broadcasted_iota(jnp.int32, (tq, tq), 0) >= lax.broadcasted_iota(jnp.int32, (tq, tq), 1)
        q, do, cb = q_ref[...], do_ref[...], c_ref[0]
        res, deltas = [], []
        for hh in range(2):
            hm = lo if hh == 0 else ~lo
            head = 2 * j + hh
            qh = jnp.where(hm, q, jnp.zeros_like(q))
            doh = jnp.where(hm, do, 0.0).astype(BF)
            c = cb[:, hh:hh + 1]

            def probs(kb, delta, diag, qh=qh, doh=doh, c=c, head=head):
                k0 = pl.multiple_of(kb * tq, tq)
                k = k_ref[pl.ds(k0, tq), :]
                v = v_ref[pl.ds(k0, tq), :]
                fk = _head_bias_row(fr_ref[:, pl.ds(k0, tq)], head)
                p = jnp.exp((lax.dot_general(qh, k, NT, preferred_element_type=F32) + c) - fk)
                if diag:
                    p = jnp.where(causal, p, 0.0)
                dp = lax.dot_general(doh, v, NT, preferred_element_type=F32)
                p_scr[:, pl.ds(k0, tq)] = p
                dp_scr[:, pl.ds(k0, tq)] = dp
                return delta + jnp.sum(p * dp, axis=-1, keepdims=True)

            delta = lax.fori_loop(0, i, lambda kb, d: probs(kb, d, False), jnp.zeros((tq, 1), F32))
            delta = probs(i, delta, True)

            def grad(kb, acc, delta=delta):
                k0 = pl.multiple_of(kb * tq, tq)
                ds = p_scr[:, pl.ds(k0, tq)] * (dp_scr[:, pl.ds(k0, tq)] - delta)
                return acc + jnp.dot(ds.astype(BF), k_ref[pl.ds(k0, tq), :], preferred_element_type=F32)

            res.append(lax.fori_loop(0, i + 1, grad, jnp.zeros((tq, LANES), F32)))
            deltas.append(delta)
        dq_ref[...] = jnp.where(lo, res[0], res[1])
        dl_ref[0] = jnp.where(lane == 0, deltas[0], jnp.where(lane == 1, deltas[1], 0.0))

    full = lambda: pl.BlockSpec((T, LANES), lambda j, i: (0, j))
    return pl.pallas_call(
        body, name="fox_bwd_dq", grid=(4, nq),
        in_specs=[pl.BlockSpec((tq, LANES), lambda j, i: (i, j)), full(), full(), pl.BlockSpec((8, T), lambda j, i: (0, 0)),
                  pl.BlockSpec((1, tq, LANES), lambda j, i: (j, i, 0)), pl.BlockSpec((tq, LANES), lambda j, i: (i, 4 + j))],
        out_specs=[pl.BlockSpec((tq, LANES), lambda j, i: (i, j)), pl.BlockSpec((1, tq, LANES), lambda j, i: (j, i, 0))],
        out_shape=[jax.ShapeDtypeStruct((T, 4 * LANES), F32), jax.ShapeDtypeStruct((4, T, LANES), F32)],
        scratch_shapes=[pltpu.VMEM((tq, T), F32), pltpu.VMEM((tq, T), F32)],
        compiler_params=_params(("parallel", "arbitrary")),
    )(qn, kn, vb, fr, cq, dmix)


def _fox_bwd_dkv(qn, kn, vb, fr, cq, dl, dmix, T, tq):
    nq = T // tq

    def body(q_ref, k_ref, v_ref, fr_ref, c_ref, dl_ref, do_ref, dk_ref, dv_ref, dfr_ref):
        j, kb = pl.program_id(0), pl.program_id(1)
        lo = _lo_mask()
        sub = lax.broadcasted_iota(jnp.int32, (8, 1), 0)
        causal = lax.broadcasted_iota(jnp.int32, (tq, tq), 0) >= lax.broadcasted_iota(jnp.int32, (tq, tq), 1)
        k, v, frb = k_ref[...], v_ref[...], fr_ref[...]
        dks, dvs, dfs = [], [], []
        for hh in range(2):
            hm = lo if hh == 0 else ~lo
            head = 2 * j + hh
            km = jnp.where(hm, k, jnp.zeros_like(k))
            vm = jnp.where(hm, v, jnp.zeros_like(v))
            fk = _head_bias_row(frb, head)

            def block(qi, carry, diag, km=km, vm=vm, fk=fk, hm=hm, hh=hh):
                dk, dv, df = carry
                q0 = pl.multiple_of(qi * tq, tq)
                q = q_ref[pl.ds(q0, tq), :]
                c = c_ref[0, pl.ds(q0, tq), :][:, hh:hh + 1]
                delta = dl_ref[0, pl.ds(q0, tq), :][:, hh:hh + 1]
                dob = do_ref[pl.ds(q0, tq), :].astype(BF)
                p = jnp.exp((lax.dot_general(q, km, NT, preferred_element_type=F32) + c) - fk)
                if diag:
                    p = jnp.where(causal, p, 0.0)
                dv = dv + lax.dot_general(p.astype(BF), dob, TN, preferred_element_type=F32)
                dp = lax.dot_general(dob, vm, NT, preferred_element_type=F32)
                ds = p * (dp - delta)
                dk = dk + lax.dot_general(ds.astype(BF), q, TN, preferred_element_type=F32)
                return dk, dv, df - jnp.sum(ds, axis=0, keepdims=True)

            init = (jnp.zeros((tq, LANES), F32), jnp.zeros((tq, LANES), F32), jnp.zeros((1, tq), F32))
            carry = block(kb, init, True)
            dk, dv, df = lax.fori_loop(kb + 1, nq, lambda qi, cr: block(qi, cr, False), carry)
            dks.append(dk)
            dvs.append(dv)
            dfs.append(df)
        dk_ref[...] = jnp.where(lo, dks[0], dks[1])
        dv_ref[...] = jnp.where(lo, dvs[0], dvs[1]).astype(dv_ref.dtype)
        dfr_ref[0] = jnp.where(sub == 0, dfs[0], jnp.where(sub == 1, dfs[1], 0.0))

    full = lambda off: pl.BlockSpec((T, LANES), lambda j, kb: (0, off + j))
    blk = lambda: pl.BlockSpec((tq, LANES), lambda j, kb: (kb, j))
    return pl.pallas_call(
        body, name="fox_bwd_dkv", grid=(4, nq),
        in_specs=[full(0), blk(), blk(), pl.BlockSpec((8, tq), lambda j, kb: (0, kb)),
                  pl.BlockSpec((1, T, LANES), lambda j, kb: (j, 0, 0)), pl.BlockSpec((1, T, LANES), lambda j, kb: (j, 0, 0)), full(4)],
        out_specs=[blk(), blk(), pl.BlockSpec((1, 8, tq), lambda j, kb: (j, 0, kb))],
        out_shape=[jax.ShapeDtypeStruct((T, 4 * LANES), F32), jax.ShapeDtypeStruct((T, 4 * LANES), BF),
                   jax.ShapeDtypeStruct((4, 8, T), F32)],
        compiler_params=_params(("parallel", "arbitrary")),
    )(qn, kn, vb, fr, cq, dl, dmix)


_BIAS_LANE = HEAD


def _split3(f):
    hi = f.astype(BF).astype(F32)
    mid = (f - hi).astype(BF).astype(F32)
    lo = ((f - hi) - mid).astype(BF).astype(F32)
    return hi, mid, lo


def _fox_operands(P, fc, g_fq2, g_fk2, T, tm):
    def body(fq_ref, fk_ref, fv_ref, fc_ref, gq_ref, gk_ref, qa_ref, qat_ref, ka_ref, kat_ref, va_ref, vat_ref):
        j = pl.program_id(0)
        lane = lax.broadcasted_iota(jnp.int32, (1, LANES), 1)
        qn, kn = _fox_prep_fn(fq_ref[...], fk_ref[...], gq_ref[...], gk_ref[...])
        v = fv_ref[...]
        fcb = fc_ref[...]
        b = _BIAS_LANE
        for hh in range(2):
            hi, mid, lo = _split3(_head_bias_col(fcb, 2 * j + hh))
            take = (lambda a: a) if hh == 0 else (lambda a: pltpu.roll(a, HEAD, 1))
            qa = jnp.where(lane < HEAD, take(qn), jnp.where(lane == b, hi, jnp.where(lane == b + 1, mid, jnp.where(
                lane == b + 2, lo, jnp.where(lane < b + 6, 1.0, 0.0)))))
            ka = jnp.where(lane < HEAD, take(kn), jnp.where(lane < b + 3, 1.0, jnp.where(lane == b + 3, -hi, jnp.where(
                lane == b + 4, -mid, jnp.where(lane == b + 5, -lo, 0.0)))))
            va = jnp.where(lane < HEAD, take(v), 0.0)
            for val, ref, tref in ((qa, qa_ref, qat_ref), (ka, ka_ref, kat_ref), (va, va_ref, vat_ref)):
                ref[hh] = val.astype(BF)
                tref[hh] = val.T.astype(BF)

    sec = lambda off: pl.BlockSpec((tm, LANES), lambda j, i: (i, off + j))
    par = pl.BlockSpec((1, LANES), lambda j, i: (0, 0))
    nat = pl.BlockSpec((2, tm, LANES), lambda j, i: (j, i, 0))
    trn = pl.BlockSpec((2, LANES, tm), lambda j, i: (j, 0, i))
    return pl.pallas_call(
        body, name="fox_operands", grid=(4, T // tm),
        in_specs=[sec(16), sec(20), sec(24), pl.BlockSpec((tm, LANES), lambda j, i: (i, 0)), par, par],
        out_specs=[nat, trn, nat, trn, nat, trn],
        out_shape=[jax.ShapeDtypeStruct((8, T, LANES), BF), jax.ShapeDtypeStruct((8, LANES, T), BF)] * 3,
        compiler_params=_params(("parallel", "arbitrary")),
    )(P, P, P, fc, g_fq2, g_fk2)


def _fox_forward(qat, ka, vat, T, tq, tk):
    nq, per = T // tq, tq // tk

    def body(qat_ref, ka_ref, vat_ref, o_ref, lse_ref):
        i = pl.program_id(1)
        sub = lax.broadcasted_iota(jnp.int32, (8, 1), 0)
        krow = lax.broadcasted_iota(jnp.int32, (tk, tq), 0)
        qcol = lax.broadcasted_iota(jnp.int32, (tk, tq), 1)

        def scores(kb):
            k0 = pl.multiple_of(kb * tk, tk)
            return tuple(jnp.dot(ka_ref[hh, pl.ds(k0, tk), :], qat_ref[hh], preferred_element_type=F32) for hh in range(2))

        def step(kb, carry, mask, last=False):
            stats, s_now = carry
            s_next = s_now if last else scores(kb + 1)
            k0 = pl.multiple_of(kb * tk, tk)
            new = []
            for hh in range(2):
                m, l, acc = stats[hh]
                s = s_now[hh] if mask is None else jnp.where(mask, s_now[hh], NEG)
                m2 = jnp.maximum(m, jnp.max(s, axis=0, keepdims=True))
                p = jnp.exp(s - m2)
                a = jnp.exp(m - m2)
                pv = jnp.dot(vat_ref[hh, 0:HEAD, pl.ds(k0, tk)], p.astype(BF), preferred_element_type=F32)
                new.append((m2, a * l + jnp.sum(p, axis=0, keepdims=True), a * acc + pv))
            return tuple(new), s_next

        one = (jnp.full((1, tq), NEG, F32), jnp.zeros((1, tq), F32), jnp.zeros((HEAD, tq), F32))
        carry = lax.fori_loop(0, i * per, lambda kb, c: step(kb, c, None), ((one, one), scores(0)))
        for d in range(per):
            carry = step(i * per + d, carry, krow + d * tk <= qcol, last=(d == per - 1))
        stats = carry[0]
        o_ref[...] = jnp.concatenate([acc / l for _, l, acc in stats], axis=0).T
        lses = [m + jnp.log(l) for m, l, _ in stats]
        lse_ref[0] = jnp.where(sub == 0, lses[0], jnp.where(sub == 1, lses[1], 0.0))

    return pl.pallas_call(
        body, name="fox_forward", grid=(4, nq),
        in_specs=[pl.BlockSpec((2, LANES, tq), lambda j, i: (j, 0, i)), pl.BlockSpec((2, T, LANES), lambda j, i: (j, 0, 0)),
                  pl.BlockSpec((2, LANES, T), lambda j, i: (j, 0, 0))],
        out_specs=[pl.BlockSpec((tq, LANES), lambda j, i: (i, j)), pl.BlockSpec((1, 8, tq), lambda j, i: (j, 0, i))],
        out_shape=[jax.ShapeDtypeStruct((T, 4 * LANES), F32), jax.ShapeDtypeStruct((4, 8, T), F32)],
        compiler_params=_params(("parallel", "arbitrary")),
    )(qat, ka, vat)


def _fox_cotangent(dmix, fox, T, tm):
    def body(do_ref, o_ref, doa_ref, doat_ref, dl_ref):
        lane = lax.broadcasted_iota(jnp.int32, (1, LANES), 1)
        sub = lax.broadcasted_iota(jnp.int32, (8, 1), 0)
        dob = do_ref[...].astype(BF).astype(F32)
        prod_t = (dob * o_ref[...]).T
        d0 = jnp.sum(prod_t[:HEAD], axis=0, keepdims=True)
        d1 = jnp.sum(prod_t[HEAD:], axis=0, keepdims=True)
        dl_ref[0] = jnp.where(sub == 0, d0, jnp.where(sub == 1, d1, 0.0))
        for hh in range(2):
            val = jnp.where(lane < HEAD, dob if hh == 0 else pltpu.roll(dob, HEAD, 1), 0.0)
            doa_ref[hh] = val.astype(BF)
            doat_ref[hh] = val.T.astype(BF)

    return pl.pallas_call(
        body, name="fox_cotangent", grid=(4, T // tm),
        in_specs=[pl.BlockSpec((tm, LANES), lambda j, i: (i, 4 + j)), pl.BlockSpec((tm, LANES), lambda j, i: (i, j))],
        out_specs=[pl.BlockSpec((2, tm, LANES), lambda j, i: (j, i, 0)), pl.BlockSpec((2, LANES, tm), lambda j, i: (j, 0, i)),
                   pl.BlockSpec((1, 8, tm), lambda j, i: (j, 0, i))],
        out_shape=[jax.ShapeDtypeStruct((8, T, LANES), BF), jax.ShapeDtypeStruct((8, LANES, T), BF),
                   jax.ShapeDtypeStruct((4, 8, T), F32)],
        compiler_params=_params(("parallel", "arbitrary")),
    )(dmix, fox)


def _fox_backward(qa, qat, ka, kat, va, doa, doat, lse, dl, T, tq, tk):
    nq, nk = T // tq, T // tk

    def body(qa_ref, qat_ref, ka_ref, kat_ref, va_ref, doa_ref, doat_ref, lse_ref, dl_ref,
             dq_ref, dk_ref, dv_ref, df_ref, dr_ref, dqt, dk_acc, dv_acc, df_acc):
        j, kb = pl.program_id(0), pl.program_id(1)
        lane = lax.broadcasted_iota(jnp.int32, (1, LANES), 1)
        first = (kb * tk) // tq
        mask = (lax.broadcasted_iota(jnp.int32, (tk, tq), 0) + (kb * tk - first * tq)
                <= lax.broadcasted_iota(jnp.int32, (tk, tq), 1))

        @pl.when(kb == 0)
        def _():
            dqt[...] = jnp.zeros(dqt.shape, F32)

        dk_acc[...] = jnp.zeros(dk_acc.shape, F32)
        dv_acc[...] = jnp.zeros(dv_acc.shape, F32)
        df_acc[...] = jnp.zeros(df_acc.shape, F32)

        def products(qi):
            q0 = pl.multiple_of(qi * tq, tq)
            return tuple((jnp.dot(ka_ref[hh], qat_ref[hh, :, pl.ds(q0, tq)], preferred_element_type=F32),
                          jnp.dot(va_ref[hh], doat_ref[hh, :, pl.ds(q0, tq)], preferred_element_type=F32)) for hh in range(2))

        def step(qi, now, diag):
            ahead = products(jnp.minimum(qi + 1, nq - 1))
            q0 = pl.multiple_of(qi * tq, tq)
            for hh in range(2):
                s, dp = now[hh]
                p = jnp.exp(s - lse_ref[0, hh:hh + 1, pl.ds(q0, tq)])
                if diag:
                    p = jnp.where(mask, p, 0.0)
                ds = p * (dp - dl_ref[0, hh:hh + 1, pl.ds(q0, tq)])
                pb, dsb = p.astype(BF), ds.astype(BF)
                dv_acc[hh] += jnp.dot(pb, doa_ref[hh, pl.ds(q0, tq), :], preferred_element_type=F32)
                dk_acc[hh] += jnp.dot(dsb, qa_ref[hh, pl.ds(q0, tq), :], preferred_element_type=F32)
                dqt[hh, 0:HEAD, pl.ds(q0, tq)] += jnp.dot(kat_ref[hh, 0:HEAD, :], dsb, preferred_element_type=F32)
                dqt[hh, HEAD:HEAD + 8, pl.ds(q0, tq)] += jnp.broadcast_to(jnp.sum(ds, axis=0, keepdims=True), (8, tq))
                part = ds[:, 0:LANES]
                for c in range(1, tq // LANES):
                    part = part + ds[:, c * LANES:(c + 1) * LANES]
                df_acc[hh] += part
            return ahead

        lax.fori_loop(first + 1, nq, lambda qi, now: step(qi, now, False), step(first, products(first), True))

        lo = lane < HEAD
        dk_ref[...] = jnp.where(lo, dk_acc[0], pltpu.roll(dk_acc[1], HEAD, 1))
        dv_ref[...] = jnp.where(lo, dv_acc[0], pltpu.roll(dv_acc[1], HEAD, 1)).astype(dv_ref.dtype)
        f0 = -jnp.sum(df_acc[0], axis=1, keepdims=True)
        f1 = -jnp.sum(df_acc[1], axis=1, keepdims=True)
        df_ref[0] = jnp.where(lane == 2 * j, f0, jnp.where(lane == 2 * j + 1, f1, 0.0))

        @pl.when(kb == nk - 1)
        def _():
            for t in range(nq):
                cols = slice(t * tq, (t + 1) * tq)
                dq_ref[cols, :] = jnp.concatenate([dqt[0, 0:HEAD, cols], dqt[1, 0:HEAD, cols]], axis=0).T
                rsum = jnp.concatenate([dqt[0, HEAD:HEAD + 8, cols], dqt[1, HEAD:HEAD + 8, cols],
                                        jnp.zeros((LANES - 16, tq), F32)], axis=0).T
                dr_ref[0, cols, :] = jnp.where(lane == 2 * j, rsum[:, 0:1], jnp.where(lane == 2 * j + 1, rsum[:, 8:9], 0.0))

    nat_full = pl.BlockSpec((2, T, LANES), lambda j, kb: (j, 0, 0))
    trn_full = pl.BlockSpec((2, LANES, T), lambda j, kb: (j, 0, 0))
    nat_blk = pl.BlockSpec((2, tk, LANES), lambda j, kb: (j, kb, 0))
    trn_blk = pl.BlockSpec((2, LANES, tk), lambda j, kb: (j, 0, kb))
    rows = pl.BlockSpec((1, 8, T), lambda j, kb: (j, 0, 0))
    blk = pl.BlockSpec((tk, LANES), lambda j, kb: (kb, j))
    return pl.pallas_call(
        body, name="fox_backward", grid=(4, nk),
        in_specs=[nat_full, trn_full, nat_blk, trn_blk, nat_blk, nat_full, trn_full, rows, rows],
        out_specs=[pl.BlockSpec((T, LANES), lambda j, kb: (0, j)), blk, blk, pl.BlockSpec((1, tk, LANES), lambda j, kb: (j, kb, 0)),
                   pl.BlockSpec((1, T, LANES), lambda j, kb: (j, 0, 0))],
        out_shape=[jax.ShapeDtypeStruct((T, 4 * LANES), F32), jax.ShapeDtypeStruct((T, 4 * LANES), F32),
                   jax.ShapeDtypeStruct((T, 4 * LANES), BF), jax.ShapeDtypeStruct((4, T, LANES), F32),
                   jax.ShapeDtypeStruct((4, T, LANES), F32)],
        scratch_shapes=[pltpu.VMEM((2, HEAD + 8, T), F32), pltpu.VMEM((2, tk, LANES), F32), pltpu.VMEM((2, tk, LANES), F32),
                        pltpu.VMEM((2, tk, LANES), F32)],
        compiler_params=_params(("arbitrary", "arbitrary")),
    )(qa, qat, ka, kat, va, doa, doat, lse, dl)


def _fgate_bwd_col(ffp, bpad, dfc, T):
    def body(ff_ref, b_ref, dfc_ref, dff_ref, db_ref):
        lane = lax.broadcasted_iota(jnp.int32, (1, LANES), 1)
        tri = _tri(False)
        carry = jnp.zeros((1, LANES), F32)
        db = jnp.zeros((1, LANES), F32)
        for blk in reversed(range(T // _FB)):
            dlf = jnp.dot(tri, dfc_ref[blk * _FB:(blk + 1) * _FB, :], precision=lax.Precision.HIGHEST,
                          preferred_element_type=F32) + carry
            carry = dlf[0:1, :]
            z = ff_ref[blk * _FB:(blk + 1) * _FB, :] + b_ref[...]
            dz = jnp.where(lane < 8, dlf * jax.nn.sigmoid(-z), 0.0)
            dff_ref[blk * _FB:(blk + 1) * _FB, :] = dz.astype(dff_ref.dtype)
            db = db + jnp.sum(dz, axis=0, keepdims=True)
        db_ref[...] = db

    return pl.pallas_call(
        body, name="fgate_bwd",
        out_shape=[jax.ShapeDtypeStruct((T, LANES), BF), jax.ShapeDtypeStruct((1, LANES), F32)],
        compiler_params=pltpu.CompilerParams(vmem_limit_bytes=VMEM_LIMIT),
    )(ffp, bpad, dfc)


MESH = pl.DeviceIdType.MESH


def _place():
    return lax.axis_index("x"), lax.axis_index("y"), lax.axis_index("c")


def _all_gather(shard):
    R, W = shard.shape

    def body(x_ref, out_ref, send_sems, recv_sems, local_sem):
        x, y, c = _place()
        me, sibling = (x, y, c), (x, y, 1 - c)
        chips = [(1 - x, y), (x, 1 - y), (1 - x, 1 - y)]

        def slot(px, py, pc):
            return out_ref.at[4 * px + 2 * py + pc]

        def copy(k, block, to, src=None):
            return pltpu.make_async_remote_copy(
                src_ref=slot(*block) if src is None else src, dst_ref=slot(*block),
                send_sem=send_sems.at[k], recv_sem=recv_sems.at[k], device_id=to, device_id_type=MESH)

        mine = pltpu.make_async_copy(x_ref, slot(*me), local_sem)
        mine.start()
        first = [copy(0, me, sibling, src=x_ref)]
        first += [copy(1 + n, me, (*chip, c), src=x_ref) for n, chip in enumerate(chips)]
        for cp in first:
            cp.start()
        passed = [copy(4 + n, (*chip, c), sibling) for n, chip in enumerate(chips)]
        for n, chip in enumerate(chips):
            copy(1 + n, (*chip, c), me).wait_recv()
            passed[n].start()
        copy(0, sibling, me).wait_recv()
        for n, chip in enumerate(chips):
            copy(4 + n, (*chip, 1 - c), me).wait_recv()
        for cp in first + passed:
            cp.wait_send()
        mine.wait()

    return pl.pallas_call(
        body, name="all_gather_weights",
        out_shape=jax.ShapeDtypeStruct((N_DEV, R, W), shard.dtype),
        in_specs=[pl.BlockSpec(memory_space=pl.ANY)], out_specs=pl.BlockSpec(memory_space=pl.ANY),
        scratch_shapes=[pltpu.SemaphoreType.DMA((7,)), pltpu.SemaphoreType.DMA((7,)), pltpu.SemaphoreType.DMA],
    )(shard)


def _all_to_all(big, small):
    def body(big_ref, small_ref, rbig_ref, rsmall_ref, send_sems, recv_sems, local_sems):
        x, y, c = _place()
        me = 4 * x + 2 * y + c
        l0 = pltpu.make_async_copy(big_ref.at[me], rbig_ref.at[me], local_sems.at[0])
        l1 = pltpu.make_async_copy(small_ref, rsmall_ref.at[me], local_sems.at[1])
        l0.start()
        l1.start()
        copies = []
        for r in range(1, N_DEV):
            px, py, pc = x ^ (r >> 2), y ^ ((r >> 1) & 1), c ^ (r & 1)
            peer = 4 * px + 2 * py + pc
            copies.append(pltpu.make_async_remote_copy(
                src_ref=big_ref.at[peer], dst_ref=rbig_ref.at[me], send_sem=send_sems.at[2 * r], recv_sem=recv_sems.at[2 * r],
                device_id=(px, py, pc), device_id_type=MESH))
            copies.append(pltpu.make_async_remote_copy(
                src_ref=small_ref, dst_ref=rsmall_ref.at[me], send_sem=send_sems.at[2 * r + 1], recv_sem=recv_sems.at[2 * r + 1],
                device_id=(px, py, pc), device_id_type=MESH))
        for cp in copies:
            cp.start()
        for cp in copies:
            cp.wait_recv()
        for cp in copies:
            cp.wait_send()
        l0.wait()
        l1.wait()

    return pl.pallas_call(
        body, name="all_to_all_grads",
        out_shape=[jax.ShapeDtypeStruct(big.shape, big.dtype), jax.ShapeDtypeStruct((N_DEV,) + small.shape, small.dtype)],
        in_specs=[pl.BlockSpec(memory_space=pl.ANY)] * 2, out_specs=[pl.BlockSpec(memory_space=pl.ANY)] * 2,
        scratch_shapes=[pltpu.SemaphoreType.DMA((2 * N_DEV,)), pltpu.SemaphoreType.DMA((2 * N_DEV,)), pltpu.SemaphoreType.DMA((2,))],
    )(big, small)


def _exchange_copies(src_ref, land_ref, send_sems, recv_sems, scatter):
    x, y, c = _place()
    me = 4 * x + 2 * y + c
    copies = []
    for r in range(1, N_DEV):
        px, py, pc = x ^ (r >> 2), y ^ ((r >> 1) & 1), c ^ (r & 1)
        copies.append(pltpu.make_async_remote_copy(
            src_ref=src_ref.at[4 * px + 2 * py + pc] if scatter else src_ref, dst_ref=land_ref.at[me],
            send_sem=send_sems.at[r - 1], recv_sem=recv_sems.at[r - 1], device_id=(px, py, pc), device_id_type=MESH))
    return copies


_HBM = pl.BlockSpec(memory_space=pltpu.HBM)
_SEM = pl.BlockSpec(memory_space=pltpu.SEMAPHORE)
_EFFECT = pltpu.SideEffectType.DATAFLOW_SIDE_EFFECTING


def _exchange_start(name, src, land, scatter):
    def body(src_ref, land_ref, send_sems, recv_sems, src_thru, land_thru, token):
        for cp in _exchange_copies(src_ref, land_ref, send_sems, recv_sems, scatter):
            cp.start()
        token[...] = jnp.zeros(token.shape, F32)

    return pl.pallas_call(
        body, name=name,
        out_shape=(pltpu.SemaphoreType.DMA((N_DEV - 1,)), pltpu.SemaphoreType.DMA((N_DEV - 1,)),
                   pltpu.HBM(src.shape, src.dtype), pltpu.HBM(land.shape, land.dtype), jax.ShapeDtypeStruct((8, LANES), F32)),
        in_specs=(_HBM, _HBM), out_specs=(_SEM, _SEM, _HBM, _HBM, pl.BlockSpec(memory_space=pltpu.VMEM)),
        input_output_aliases={0: 2, 1: 3},
        compiler_params=pltpu.CompilerParams(has_side_effects=_EFFECT),
    )(pltpu.with_memory_space_constraint(src, pltpu.HBM), pltpu.with_memory_space_constraint(land, pltpu.HBM))


def _exchange_wait(name, started, after, scatter):
    send_sems, recv_sems, src_thru, land_thru, _ = started

    def body(src_ref, land_ref, send_sems, recv_sems, after_ref, src_dead, got_ref):
        copies = _exchange_copies(src_ref, land_ref, send_sems, recv_sems, scatter)
        for cp in copies:
            cp.wait_send()
        for cp in copies:
            cp.wait_recv()

    return pl.pallas_call(
        body, name=name,
        out_shape=(pltpu.HBM(src_thru.shape, src_thru.dtype), pltpu.HBM(land_thru.shape, land_thru.dtype)),
        in_specs=(_HBM, _HBM, _SEM, _SEM, pl.BlockSpec(memory_space=pl.ANY)), out_specs=(_HBM, _HBM),
        input_output_aliases={0: 0, 1: 1},
        compiler_params=pltpu.CompilerParams(has_side_effects=_EFFECT),
    )(src_thru, land_thru, send_sems, recv_sems, after)


def _adamw(name, slots, w, m, v, tr, own=None):
    R, W = w.shape

    def body(s_ref, *refs):
        if own is not None:
            own_ref, refs = refs[0], refs[1:]
        w_ref, m_ref, v_ref, g_ref, d_ref, nm_ref, nv_ref = refs
        g = s_ref[0].astype(F32)
        for s in range(1, N_DEV):
            g = g + s_ref[s].astype(F32)
        if own is not None:
            g = g + own_ref[...].astype(F32)
        m2 = ADAM_B1 * m_ref[...] + (1.0 - ADAM_B1) * g
        v2 = ADAM_B2 * v_ref[...] + (1.0 - ADAM_B2) * jnp.square(g)
        m_hat = m2 / (1.0 - ADAM_B1 ** ADAM_STEP)
        v_hat = v2 / (1.0 - ADAM_B2 ** ADAM_STEP)
        g_ref[...] = g
        d_ref[...] = -ADAM_LR * (m_hat / (jnp.sqrt(v_hat) + ADAM_EPS) + ADAM_WD * w_ref[...])
        nm_ref[...] = m2
        nv_ref[...] = v2

    row = lambda: pl.BlockSpec((tr, W), lambda i: (i, 0))
    return pl.pallas_call(
        body, name=name, grid=(R // tr,),
        in_specs=[pl.BlockSpec((N_DEV, tr, W), lambda i: (0, i, 0))] + [row() for _ in range(3 + (own is not None))],
        out_specs=[row(), row(), row(), row()],
        out_shape=[jax.ShapeDtypeStruct((R, W), F32)] * 4,
        compiler_params=_params(("parallel",)),
    )(slots, *([own] if own is not None else []), w, m, v)


def _tables(T):
    pos = jnp.arange(T, dtype=F32)
    inv_freq = 10000.0 ** (-jnp.arange(0, HEAD, 2, dtype=F32) / HEAD)
    ang = pos[:, None] * inv_freq[None, :]
    cos, sin = jnp.cos(ang), jnp.sin(ang)
    cos4 = jnp.tile(cos, (1, 4))
    sin4 = jnp.tile(jnp.concatenate([-sin, sin], axis=1), (1, 2))
    log_g = jnp.log(1.0 - 2.0 ** (-5.0 - jnp.arange(8, dtype=F32)))
    return cos4, sin4, jnp.repeat(log_g, HEAD)[None, :]


def _local_step(x, mem, target, sp, w_inT, token, fetch_rest, push):
    T = x.shape[0]
    tm = min(512, T)
    tq = min(256, T)
    tb = min(1024, T)
    cos4, sin4, lg = _tables(T)
    g_fq2 = jnp.tile(sp["g_fox_q"], (1, 2))
    g_fk2 = jnp.tile(sp["g_fox_k"], (1, 2))
    g_ret = sp["g_ret_out"].reshape(1, 8 * HEAD)
    bpad = jnp.pad(sp["b_forget"], ((0, 0), (0, LANES - 8)))
    w_secs = [w_inT[k * 512:(k + 1) * 512] for k in range(7)]
    w_ffT = jnp.pad(w_inT[3584:3592], ((0, LANES - 8), (0, 0)))
    w_mainT = w_inT[:3584]
    tie = lambda p, tok: p + tok[0:1, 0:1]
    tm2, tm4 = min(1024, T), min(2048, T)

    hn1, = _rw_fwd("rms_mix", _rms_fn, [(x, D, 0, False)], [(tie(sp["g_mix"], token), D, 0, False)], [(BF, D)], T, tm, 1)
    P, = _mm("proj_in", [[(hn1, w_mainT, "nt")]], [], _ident, T, 3584, tm4, 512, [F32])
    ffp, = _mm("proj_ff", [[(hn1, w_ffT, "nt")]], [], _ident, T, LANES, tm, LANES, [F32])
    ret, s0 = _ret_fwd(P, cos4, sin4, g_ret, lg, T, tb)
    fc, _ = _fgate_fwd(ffp, bpad, T)
    qa, qat, ka, kat, va, vat = _fox_operands(P, fc, g_fq2, g_fk2, T, tm)
    fox, lse = _fox_forward(qat, ka, vat, T, min(512, T), tq)
    W = fetch_rest(fox)
    w_out_halves = (W["w_out"][:4 * LANES], W["w_out"][4 * LANES:])
    h1, = _mm("proj_out", [[(ret, w_out_halves[0], "nn"), (fox, w_out_halves[1], "nn")]], [x], _add, T, D, tm2, D, [F32])

    hn2, = _rw_fwd("rms_xattn", _rms_fn, [(h1, D, 0, False)], [(sp["g_xattn"], D, 0, False)], [(BF, D)], T, tm, 1)
    qx, = _mm("proj_xq", [[(hn2, W["w_xq"], "nn")]], [], _ident, T, D, tm2, D, [F32])
    memn, = _rw_fwd("rms_mem", _rms_fn, [(mem, D, 0, False)], [(sp["g_mem"], D, 0, False)], [(BF, D)], N_MEM, N_MEM, 1)
    kv, = _mm("proj_xkv", [[(memn, W["w_xkvT"], "nt")]], [], _ident, N_MEM, 2 * D, N_MEM, 512, [F32])
    xa_rows = [(qx, XHEAD, 0, True)]
    xa_params = [(sp["g_xq"], XHEAD, 0, False), (sp["g_xk"], XHEAD, 0, False), (kv, XHEAD, 0, True), (kv, XHEAD, 4, True)]
    xo, = _rw_fwd("xattn_fwd", _xattn_fn, xa_rows, xa_params, [(BF, XHEAD)], T, tm, 4)
    h2, = _mm("proj_xo", [[(xo, W["w_xo"], "nn")]], [h1], _add, T, D, tm2, D, [F32])

    hn3, = _rw_fwd("rms_ffn", _rms_fn, [(h2, D, 0, False)], [(sp["g_ffn"], D, 0, False)], [(BF, D)], T, tm, 1)
    gate, up, act = _mm("ffn_in", [[(hn3, W["w_gateT"], "nt")], [(hn3, W["w_upT"], "nt")]], [], _swiglu_fwd_epi,
                        T, D_FF, tm4, 256, [BF, BF, BF])
    h3, = _mm("ffn_out", [[(act, W["w_down"], "nn")]], [h2], _add, T, D, tm, D, [F32])
    dy, dyb, loss_part = _rw_fwd("loss", _loss_fn, [(h3, D, 0, False), (target, D, 0, False)], [], [(F32, D), (BF, D)], T, tm, 1,
                                 n_acc=1)

    dgate, dup = _mm("ffn_out_bwd", [[(dyb, W["w_down"], "nt")]], [gate, up], _swiglu_bwd_epi, T, D_FF, tm4, 256, [BF, BF])
    dhn3, = _mm("ffn_in_bwd", [[(dgate, W["w_gateT"], "nn"), (dup, W["w_upT"], "nn")]], [], _ident, T, D, tm, D, [F32])
    gW = {}
    gW["w_gateT"], = _mm("dw_gate", [[(dgate, hn3, "tn")]], [], _ident, D_FF, D, 256, D, [BF])
    gW["w_upT"], = _mm("dw_up", [[(dup, hn3, "tn")]], [], _ident, D_FF, D, 256, D, [BF])
    gW["w_down"], = _mm("dw_down", [[(act, dyb, "tn")]], [], _ident, D_FF, D, 256, D, [BF])
    tok = push("ffn", gW)
    gs = {}
    dh2, dh2b, gs["g_ffn"] = _rw_bwd("rms_ffn_bwd", _rms_fn, [(h2, D, 0, False)], [(tie(sp["g_ffn"], tok), D, 0, False)],
                                     [(dhn3, D, 0, False)], T, tm, 1, [[F32, BF]], [True], resid=dy)

    dxo, = _mm("proj_xo_bwd", [[(dh2b, W["w_xo"], "nt")]], [], _ident, T, D, tm2, D, [BF])
    gW["w_xo"], = _mm("dw_xo", [[(xo, dh2b, "tn")]], [], _ident, D, D, 256, D, [BF])
    dqx, gs["g_xq"], gs["g_xk"], dkv_k, dkv_v = _rw_bwd(
        "xattn_bwd", _xattn_fn, xa_rows, xa_params, [(dxo, XHEAD, 0, True)], T, tm, 4, [BF], [True, True, True, True])
    dkv = jnp.concatenate([dkv_k[:, :D], dkv_v[:, D:]], axis=1)
    dhn2, = _mm("proj_xq_bwd", [[(dqx, W["w_xq"], "nt")]], [], _ident, T, D, tm2, D, [F32])
    gW["w_xq"], = _mm("dw_xq", [[(hn2, dqx, "tn")]], [], _ident, D, D, 256, D, [BF])
    dmemn, = _mm("proj_xkv_bwd", [[(dkv, W["w_xkvT"], "nn")]], [], _ident, N_MEM, D, N_MEM, 512, [F32])
    gW["w_xkvT"], = _mm("dw_xkv", [[(dkv, memn, "tn")]], [], _ident, 2 * D, D, 512, D, [BF])
    tok = push("xattn", gW)
    gs["g_mem"], = _rw_bwd("rms_mem_bwd", _rms_fn, [(mem, D, 0, False)], [(sp["g_mem"], D, 0, False)], [(dmemn, D, 0, False)],
                           N_MEM, N_MEM, 1, [None], [True])
    dh1, dh1b, gs["g_xattn"] = _rw_bwd("rms_xattn_bwd", _rms_fn, [(h1, D, 0, False)], [(tie(sp["g_xattn"], tok), D, 0, False)],
                                       [(dhn2, D, 0, False)], T, tm, 1, [[F32, BF]], [True], resid=dh2)

    dmix, = _mm("proj_out_bwd", [[(dh1b, W["w_out"], "nt")]], [], _ident, T, D, tm2, D, [F32])
    gW["w_out"] = jnp.concatenate([_mm("dw_out_%d" % k, [[(a, dh1b, "tn")]], [], _ident, 4 * LANES, D, 256, D, [BF])[0]
                                   for k, a in enumerate((ret, fox))], axis=0)
    tok = push("out", gW)
    doa, doat, dl = _fox_cotangent(dmix, fox, T, tm)
    dqn, dkn, dfv, dfc4, drc4 = _fox_backward(qa, qat, ka, kat, va, doa, doat, lse + tok[0:1, 0:1], dl, T, tq, tq)
    dfq, dfk, gq2, gk2 = _rw_bwd("fox_prep_bwd", _fox_prep_fn, [(P, LANES, 16, True), (P, LANES, 20, True)],
                                 [(g_fq2, LANES, 0, False), (g_fk2, LANES, 0, False)],
                                 [(dqn, LANES, 0, True), (dkn, LANES, 0, True)], T, tm, 4, [BF, BF], [True, True])
    gs["g_fox_q"] = gq2[:, :HEAD] + gq2[:, HEAD:]
    gs["g_fox_k"] = gk2[:, :HEAD] + gk2[:, HEAD:]
    dff, dbp = _fgate_bwd_col(ffp, bpad, jnp.sum(dfc4 + drc4, axis=0), T)
    gs["b_forget"] = dbp[:, :8]
    drq, drk, drv, drg, dg_ret = _ret_bwd(P, cos4, sin4, g_ret, lg, s0, dmix, T, tb)
    gs["g_ret_out"] = dg_ret
    dsecs = [drq, drk, drv, drg, dfq, dfk, dfv]
    dhn1, = _mm("proj_in_bwd", [[(d, w, "nn") for d, w in zip(dsecs, w_secs)] + [(dff, w_ffT, "nn")]], [], _ident,
                T, D, tm, D, [F32])
    g_secs = [_mm("dw_in_%d" % k, [[(d, hn1, "tn")]], [], _ident, 512, D, 256, D, [BF])[0] for k, d in enumerate(dsecs)]
    g_ff, = _mm("dw_in_ff", [[(dff, hn1, "tn")]], [], _ident, LANES, D, LANES, D, [BF])
    gW["w_inT"] = jnp.concatenate(g_secs + [g_ff[:8]], axis=0)
    tok = push("in", gW)
    grad_x, gs["g_mix"] = _rw_bwd("rms_mix_bwd", _rms_fn, [(x, D, 0, False)], [(tie(sp["g_mix"], tok), D, 0, False)],
                                  [(dhn1, D, 0, False)], T, tm, 1, [F32], [True], resid=dh1)
    return loss_part, grad_x, gs


_CANON = {"w_in": "w_inT", "w_xkv": "w_xkvT", "w_gate": "w_gateT", "w_up": "w_upT"}
_SMALL = (("g_mix", 0, 0, 1024), ("g_xattn", 1, 0, 1024), ("g_mem", 2, 0, 1024), ("g_ffn", 3, 0, 1024),
          ("g_ret_out", 4, 0, 512), ("g_xq", 4, 512, 256), ("g_xk", 4, 768, 256),
          ("g_fox_q", 5, 0, 64), ("g_fox_k", 5, 64, 64), ("b_forget", 5, 128, 8))
_LOSS_AT = (5, 256)


def _pack_shards(tree, dtype):
    parts = []
    for name, rows, padded, transposed in W_LAYOUT:
        a = tree[name][0]
        a = a.T if transposed else a
        parts.append(jnp.pad(a, ((0, padded - rows), (0, 0))).astype(dtype))
    return jnp.concatenate(parts, axis=0)


def _unpack_shards(packed, like):
    out = {}
    for name, rows, padded, transposed in W_LAYOUT:
        a = packed[W_OFF[name]:W_OFF[name] + rows]
        out[name] = (a.T if transposed else a)[None].reshape(like[name].shape)
    return out


def _pack_small(tree):
    rows = [jnp.zeros((1, D), F32) for _ in range(SMALL_ROWS)]
    buf = jnp.concatenate(rows, axis=0)
    for name, r, c, n in _SMALL:
        buf = lax.dynamic_update_slice(buf, tree[name].reshape(1, n).astype(F32), (r, c))
    return buf


def _unpack_small(buf, like):
    return {name: buf[r:r + 1, c:c + n].reshape(like[name].shape) for name, r, c, n in _SMALL}


def kernel(x, mem, g_mix, w_in, b_forget, g_ret_out, g_fox_q, g_fox_k, w_out, g_xattn, w_xq, w_xkv, g_mem, g_xq, g_xk, w_xo, g_ffn, w_gate, w_up, w_down, loss_target, m_g_mix, m_w_in, m_b_forget, m_g_ret_out, m_g_fox_q, m_g_fox_k, m_w_out, m_g_xattn, m_w_xq, m_w_xkv, m_g_mem, m_g_xq, m_g_xk, m_w_xo, m_g_ffn, m_w_gate, m_w_up, m_w_down, v_g_mix, v_w_in, v_b_forget, v_g_ret_out, v_g_fox_q, v_g_fox_k, v_w_out, v_g_xattn, v_w_xq, v_w_xkv, v_g_mem, v_g_xq, v_g_xk, v_w_xo, v_g_ffn, v_w_gate, v_w_up, v_w_down):
    names = ("g_mix", "w_in", "b_forget", "g_ret_out", "g_fox_q", "g_fox_k", "w_out", "g_xattn", "w_xq", "w_xkv", "g_mem",
             "g_xq", "g_xk", "w_xo", "g_ffn", "w_gate", "w_up", "w_down")
    w = dict(zip(names, (g_mix, w_in, b_forget, g_ret_out, g_fox_q, g_fox_k, w_out, g_xattn, w_xq, w_xkv, g_mem, g_xq, g_xk,
                         w_xo, g_ffn, w_gate, w_up, w_down)))
    m = dict(zip(names, (m_g_mix, m_w_in, m_b_forget, m_g_ret_out, m_g_fox_q, m_g_fox_k, m_w_out, m_g_xattn, m_w_xq, m_w_xkv,
                         m_g_mem, m_g_xq, m_g_xk, m_w_xo, m_g_ffn, m_w_gate, m_w_up, m_w_down)))
    v = dict(zip(names, (v_g_mix, v_w_in, v_b_forget, v_g_ret_out, v_g_fox_q, v_g_fox_k, v_w_out, v_g_xattn, v_w_xq, v_w_xkv,
                         v_g_mem, v_g_xq, v_g_xk, v_w_xo, v_g_ffn, v_w_gate, v_w_up, v_w_down)))
    small_names = [s[0] for s in _SMALL]

    gathered = _all_gather(_pack_shards(w, BF))
    W = {}
    for name, rows, padded, transposed in W_LAYOUT:
        full = gathered[:, W_OFF[name]:W_OFF[name] + rows].reshape(N_DEV * rows, D)
        W[_CANON.get(name, name)] = full

    sp = {n: w[n].reshape(1, -1) for n in small_names}
    loss_part, grad_x, gW, gs = _local_step(x[0], mem[0], loss_target[0], sp, W)

    chunks = []
    for name, rows, padded, transposed in W_LAYOUT:
        g = gW[_CANON.get(name, name)].reshape(N_DEV, rows, D)
        chunks.append(jnp.pad(g, ((0, 0), (0, padded - rows), (0, 0))).astype(BF))
    send = jnp.concatenate(chunks, axis=1)
    small = _pack_small(gs)
    small = lax.dynamic_update_slice(small, loss_part[:, :1], _LOSS_AT)
    recv, recv_small = _all_to_all(send, small)

    g_big, d_big, m_big, v_big = _adamw("adamw_shards", recv, _pack_shards(w, F32), _pack_shards(m, F32), _pack_shards(v, F32), 240)
    g_sm, d_sm, m_sm, v_sm = _adamw("adamw_small", recv_small, _pack_small(w), _pack_small(m), _pack_small(v), SMALL_ROWS)
    loss = g_sm[_LOSS_AT[0], _LOSS_AT[1]]

    outs = []
    for big, sm in ((g_big, g_sm), (d_big, d_sm), (m_big, m_sm), (v_big, v_sm)):
        tree = {**_unpack_shards(big, w), **_unpack_small(sm, w)}
        outs += [tree[n] for n in names]
    return (loss, grad_x[None], *outs)


def _pack_shards(tree, names, dtype):
    parts = []
    for name in names:
        rows, padded, transposed = W_SHARD[name]
        a = tree[name][0]
        a = a.T if transposed else a
        parts.append(jnp.pad(a, ((0, padded - rows), (0, 0))).astype(dtype))
    return jnp.concatenate(parts, axis=0)


def _unpack_shards(packed, names, like):
    out, off = {}, 0
    for name in names:
        rows, padded, transposed = W_SHARD[name]
        a = packed[off:off + rows]
        out[name] = (a.T if transposed else a)[None].reshape(like[name].shape)
        off += padded
    return out


def _unpack_gathered(gathered, names):
    out, off = {}, 0
    for name in names:
        rows, padded, _ = W_SHARD[name]
        out[_CANON.get(name, name)] = gathered[:, off:off + rows].reshape(N_DEV * rows, D)
        off += padded
    return out


def _pack_chunks(grads, names):
    chunks = []
    for name in names:
        rows, padded, _ = W_SHARD[name]
        g = grads[_CANON.get(name, name)].reshape(N_DEV, rows, D)
        chunks.append(jnp.pad(g, ((0, 0), (0, padded - rows), (0, 0))).astype(BF))
    return jnp.concatenate(chunks, axis=1)


def kernel(x, mem, g_mix, w_in, b_forget, g_ret_out, g_fox_q, g_fox_k, w_out, g_xattn, w_xq, w_xkv, g_mem, g_xq, g_xk, w_xo, g_ffn, w_gate, w_up, w_down, loss_target, m_g_mix, m_w_in, m_b_forget, m_g_ret_out, m_g_fox_q, m_g_fox_k, m_w_out, m_g_xattn, m_w_xq, m_w_xkv, m_g_mem, m_g_xq, m_g_xk, m_w_xo, m_g_ffn, m_w_gate, m_w_up, m_w_down, v_g_mix, v_w_in, v_b_forget, v_g_ret_out, v_g_fox_q, v_g_fox_k, v_w_out, v_g_xattn, v_w_xq, v_w_xkv, v_g_mem, v_g_xq, v_g_xk, v_w_xo, v_g_ffn, v_w_gate, v_w_up, v_w_down):
    names = ("g_mix", "w_in", "b_forget", "g_ret_out", "g_fox_q", "g_fox_k", "w_out", "g_xattn", "w_xq", "w_xkv", "g_mem",
             "g_xq", "g_xk", "w_xo", "g_ffn", "w_gate", "w_up", "w_down")
    w = dict(zip(names, (g_mix, w_in, b_forget, g_ret_out, g_fox_q, g_fox_k, w_out, g_xattn, w_xq, w_xkv, g_mem, g_xq, g_xk,
                         w_xo, g_ffn, w_gate, w_up, w_down)))
    m = dict(zip(names, (m_g_mix, m_w_in, m_b_forget, m_g_ret_out, m_g_fox_q, m_g_fox_k, m_w_out, m_g_xattn, m_w_xq, m_w_xkv,
                         m_g_mem, m_g_xq, m_g_xk, m_w_xo, m_g_ffn, m_w_gate, m_w_up, m_w_down)))
    v = dict(zip(names, (v_g_mix, v_w_in, v_b_forget, v_g_ret_out, v_g_fox_q, v_g_fox_k, v_w_out, v_g_xattn, v_w_xq, v_w_xkv,
                         v_g_mem, v_g_xq, v_g_xk, v_w_xo, v_g_ffn, v_w_gate, v_w_up, v_w_down)))
    small_names = [s[0] for s in _SMALL]
    me = 4 * lax.axis_index("x") + 2 * lax.axis_index("y") + lax.axis_index("c")

    first = _all_gather(_pack_shards(w, GATHER_FIRST, BF))
    first, rest_shard = lax.optimization_barrier((first, _pack_shards(w, GATHER_REST, BF)))
    rest_started = _exchange_start("gather_rest_start", rest_shard,
                                   jnp.broadcast_to(rest_shard[None], (N_DEV,) + rest_shard.shape), scatter=False)

    def fetch_rest(after):
        return _unpack_gathered(_exchange_wait("gather_rest_wait", rest_started, after, scatter=False)[1], GATHER_REST)

    pushed = {}

    def push(group, grads):
        send = _pack_chunks(grads, GRAD_GROUPS[group])
        pushed[group] = _exchange_start("scatter_%s_start" % group, send, jnp.zeros(send.shape, BF), scatter=True)
        return pushed[group][4]

    sp = {n: w[n].reshape(1, -1) for n in small_names}
    loss_part, grad_x, g_last, gs = _local_step(x[0], mem[0], loss_target[0], sp, _unpack_gathered(first, GATHER_FIRST)["w_inT"],
                                                rest_started[4], fetch_rest, push)

    small = lax.dynamic_update_slice(_pack_small(gs), loss_part[:, :1], _LOSS_AT)
    recv_mix, recv_small = _all_to_all(_pack_chunks(g_last, GRAD_GROUPS["mix"]), small)

    results = {}
    for group in ("ffn", "xattn", "mix"):
        gnames = GRAD_GROUPS[group]
        wp, mp, vp = (_pack_shards(t, gnames, F32) for t in (w, m, v))
        if group == "mix":
            res = _adamw("adamw_mix", recv_mix, wp, mp, vp, 16)
        else:
            sent, recv = _exchange_wait("scatter_%s_wait" % group, pushed[group], recv_small, scatter=True)
            own = lax.dynamic_index_in_dim(sent, me, axis=0, keepdims=False)
            res = _adamw("adamw_%s" % group, recv, wp, mp, vp, {"ffn": 176, "xattn": 128}[group], own=own)
        results[group] = [_unpack_shards(r, gnames, w) for r in res]
    g_sm, d_sm, m_sm, v_sm = _adamw("adamw_small", recv_small, _pack_small(w), _pack_small(m), _pack_small(v), SMALL_ROWS)
    loss = g_sm[_LOSS_AT[0], _LOSS_AT[1]]

    outs = []
    for k, sm in enumerate((g_sm, d_sm, m_sm, v_sm)):
        tree = _unpack_small(sm, w)
        for group in results:
            tree.update(results[group][k])
        outs += [tree[n] for n in names]
    return (loss, grad_x[None], *outs)


SCATTER_GROUPS = {"ffn": ("w_gate", "w_up", "w_down"), "xattn": ("w_xq", "w_xo", "w_xkv"), "out": ("w_out",), "in": ("w_in",)}


def _adam_update(g, w, m, v):
    m2 = ADAM_B1 * m + (1.0 - ADAM_B1) * g
    v2 = ADAM_B2 * v + (1.0 - ADAM_B2) * jnp.square(g)
    m_hat = m2 / (1.0 - ADAM_B1 ** ADAM_STEP)
    v_hat = v2 / (1.0 - ADAM_B2 ** ADAM_STEP)
    return g, -ADAM_LR * (m_hat / (jnp.sqrt(v_hat) + ADAM_EPS) + ADAM_WD * w), m2, v2


def _adamw_shard(name, recv, own, off, w, m, v):
    rows, padded, transposed = W_SHARD[name.split(":")[1]]
    assert off % padded == 0
    blk = off // padded

    def total(s_ref, own_ref):
        g = own_ref[...].astype(F32)
        for s in range(N_DEV):
            g = g + s_ref[s].astype(F32)
        return g

    canonical_view = name.endswith(":")
    if transposed and rows == padded and not canonical_view:
        res = _adamw_shard(name + ":", recv, own, off, *(jnp.swapaxes(a, 1, 2) for a in (w, m, v)))
        return [jnp.swapaxes(r, 1, 2) for r in res]

    if canonical_view or not transposed:
        def body(s_ref, own_ref, w_ref, m_ref, v_ref, *outs):
            for o, r in zip(outs, _adam_update(total(s_ref, own_ref), w_ref[0], m_ref[0], v_ref[0])):
                o[0] = r

        full = pl.BlockSpec((1, rows, D), lambda i: (0, 0, 0))
        return pl.pallas_call(
            body, name=name.replace(":", "_"), grid=(1,),
            in_specs=[pl.BlockSpec((N_DEV, padded, D), lambda i: (0, blk, 0)), pl.BlockSpec((padded, D), lambda i: (blk, 0)),
                      full, full, full],
            out_specs=[full] * 4, out_shape=[jax.ShapeDtypeStruct((1, rows, D), F32)] * 4,
            compiler_params=_params(("arbitrary",)),
        )(recv, own, w, m, v)

    wide = -(-padded // LANES) * LANES

    def body(s_ref, own_ref, w_ref, m_ref, v_ref, *outs):
        g = total(s_ref, own_ref)
        if wide > padded:
            g = jnp.concatenate([g, jnp.zeros((wide - padded, LANES), F32)], axis=0)
        g = g.T[:, :rows]
        for o, r in zip(outs, _adam_update(g, w_ref[0], m_ref[0], v_ref[0])):
            o[0] = r

    cols = pl.BlockSpec((1, LANES, rows), lambda c: (0, c, 0))
    return pl.pallas_call(
        body, name=name.replace(":", "_"), grid=(D // LANES,),
        in_specs=[pl.BlockSpec((N_DEV, padded, LANES), lambda c: (0, blk, c)), pl.BlockSpec((padded, LANES), lambda c: (blk, c)),
                  cols, cols, cols],
        out_specs=[cols] * 4, out_shape=[jax.ShapeDtypeStruct((1, D, rows), F32)] * 4,
        compiler_params=_params(("arbitrary",)),
    )(recv, own, w, m, v)


def _gather_small(small):
    def body(small_ref, out_ref, send_sems, recv_sems, local_sem):
        x, y, c = _place()
        me = 4 * x + 2 * y + c
        mine = pltpu.make_async_copy(small_ref, out_ref.at[me], local_sem)
        mine.start()
        copies = []
        for r in range(1, N_DEV):
            px, py, pc = x ^ (r >> 2), y ^ ((r >> 1) & 1), c ^ (r & 1)
            copies.append(pltpu.make_async_remote_copy(
                src_ref=small_ref, dst_ref=out_ref.at[me], send_sem=send_sems.at[r - 1], recv_sem=recv_sems.at[r - 1],
                device_id=(px, py, pc), device_id_type=MESH))
        for cp in copies:
            cp.start()
        for cp in copies:
            cp.wait_recv()
        for cp in copies:
            cp.wait_send()
        mine.wait()

    return pl.pallas_call(
        body, name="gather_small",
        out_shape=jax.ShapeDtypeStruct((N_DEV,) + small.shape, small.dtype),
        in_specs=[pl.BlockSpec(memory_space=pl.ANY)], out_specs=pl.BlockSpec(memory_space=pl.ANY),
        scratch_shapes=[pltpu.SemaphoreType.DMA((N_DEV - 1,)), pltpu.SemaphoreType.DMA((N_DEV - 1,)), pltpu.SemaphoreType.DMA],
    )(small)


def _pack_chunks(grads, names):
    chunks = []
    for name in names:
        rows, padded, _ = W_SHARD[name]
        g = grads[_CANON.get(name, name)].reshape(N_DEV, rows, D)
        chunks.append(jnp.pad(g, ((0, 0), (0, padded - rows), (0, 0))).astype(BF))
    return chunks[0] if len(chunks) == 1 else jnp.concatenate(chunks, axis=1)


def kernel(x, mem, g_mix, w_in, b_forget, g_ret_out, g_fox_q, g_fox_k, w_out, g_xattn, w_xq, w_xkv, g_mem, g_xq, g_xk, w_xo, g_ffn, w_gate, w_up, w_down, loss_target, m_g_mix, m_w_in, m_b_forget, m_g_ret_out, m_g_fox_q, m_g_fox_k, m_w_out, m_g_xattn, m_w_xq, m_w_xkv, m_g_mem, m_g_xq, m_g_xk, m_w_xo, m_g_ffn, m_w_gate, m_w_up, m_w_down, v_g_mix, v_w_in, v_b_forget, v_g_ret_out, v_g_fox_q, v_g_fox_k, v_w_out, v_g_xattn, v_w_xq, v_w_xkv, v_g_mem, v_g_xq, v_g_xk, v_w_xo, v_g_ffn, v_w_gate, v_w_up, v_w_down):
    names = ("g_mix", "w_in", "b_forget", "g_ret_out", "g_fox_q", "g_fox_k", "w_out", "g_xattn", "w_xq", "w_xkv", "g_mem",
             "g_xq", "g_xk", "w_xo", "g_ffn", "w_gate", "w_up", "w_down")
    w = dict(zip(names, (g_mix, w_in, b_forget, g_ret_out, g_fox_q, g_fox_k, w_out, g_xattn, w_xq, w_xkv, g_mem, g_xq, g_xk,
                         w_xo, g_ffn, w_gate, w_up, w_down)))
    m = dict(zip(names, (m_g_mix, m_w_in, m_b_forget, m_g_ret_out, m_g_fox_q, m_g_fox_k, m_w_out, m_g_xattn, m_w_xq, m_w_xkv,
                         m_g_mem, m_g_xq, m_g_xk, m_w_xo, m_g_ffn, m_w_gate, m_w_up, m_w_down)))
    v = dict(zip(names, (v_g_mix, v_w_in, v_b_forget, v_g_ret_out, v_g_fox_q, v_g_fox_k, v_w_out, v_g_xattn, v_w_xq, v_w_xkv,
                         v_g_mem, v_g_xq, v_g_xk, v_w_xo, v_g_ffn, v_w_gate, v_w_up, v_w_down)))
    small_names = [s[0] for s in _SMALL]
    me = 4 * lax.axis_index("x") + 2 * lax.axis_index("y") + lax.axis_index("c")

    first = _all_gather(_pack_shards(w, GATHER_FIRST, BF))
    first, rest_shard = lax.optimization_barrier((first, _pack_shards(w, GATHER_REST, BF)))
    rest_started = _exchange_start("gather_rest_start", rest_shard,
                                   jnp.broadcast_to(rest_shard[None], (N_DEV,) + rest_shard.shape), scatter=False)

    def fetch_rest(after):
        return _unpack_gathered(_exchange_wait("gather_rest_wait", rest_started, after, scatter=False)[1], GATHER_REST)

    pushed = {}

    def push(group, grads):
        send = _pack_chunks(grads, SCATTER_GROUPS[group])
        pushed[group] = _exchange_start("scatter_%s_start" % group, send, jnp.zeros(send.shape, BF), scatter=True)
        return pushed[group][4]

    sp = {n: w[n].reshape(1, -1) for n in small_names}
    loss_part, grad_x, gs = _local_step(x[0], mem[0], loss_target[0], sp, _unpack_gathered(first, GATHER_FIRST)["w_inT"],
                                        rest_started[4], fetch_rest, push)

    small = lax.dynamic_update_slice(_pack_small(gs), loss_part[:, :1], _LOSS_AT)
    recv_small = _gather_small(small)
    g_sm, d_sm, m_sm, v_sm = _adamw("adamw_small", recv_small, _pack_small(w), _pack_small(m), _pack_small(v), SMALL_ROWS)
    loss = g_sm[_LOSS_AT[0], _LOSS_AT[1]]

    results, after = {}, recv_small
    for group in ("ffn", "xattn", "out", "in"):
        sent, recv = _exchange_wait("scatter_%s_wait" % group, pushed[group], after, scatter=True)
        own = lax.dynamic_index_in_dim(sent, me, axis=0, keepdims=False)
        off = 0
        for name in SCATTER_GROUPS[group]:
            results[name] = _adamw_shard("adamw:" + name, recv, own, off, w[name], m[name], v[name])
            off += W_SHARD[name][1]
        after = results[SCATTER_GROUPS[group][-1]][0]

    outs = []
    for k, sm in enumerate((g_sm, d_sm, m_sm, v_sm)):
        tree = _unpack_small(sm, w)
        tree.update({name: res[k] for name, res in results.items()})
        outs += [tree[n] for n in names]
    return (loss, grad_x[None], *outs)
```

```python
import functools
import math

import jax
import jax.numpy as jnp
import numpy as np
from jax import lax
from jax.experimental import pallas as pl
from jax.experimental.pallas import tpu as pltpu

F32 = jnp.float32
BF = jnp.bfloat16

D = 1024
HEAD = 64
CHUNK = 64
N_MEM = 256
XHEAD = 256
D_FF = 2816
EPS = 1e-6
NEG = -1e30
LANES = 128
N_DEV = 8
V7X_VMEM_BYTES = 64 * 1024 * 1024
VMEM_LIMIT = V7X_VMEM_BYTES - 8 * 1024 * 1024

ADAM_LR, ADAM_B1, ADAM_B2, ADAM_EPS, ADAM_WD, ADAM_STEP = 0.001, 0.9, 0.999, 1e-08, 0.01, 10

W_LAYOUT = (("w_in", 449, 464, True), ("w_out", 128, 128, False), ("w_xq", 128, 128, False), ("w_xkv", 256, 256, True),
            ("w_xo", 128, 128, False), ("w_gate", 352, 352, True), ("w_up", 352, 352, True), ("w_down", 352, 352, False))
W_ROWS = sum(w[2] for w in W_LAYOUT)
W_OFF = {}
_o = 0
for _n, _r, _p, _t in W_LAYOUT:
    W_OFF[_n] = _o
    _o += _p
SMALL_ROWS = 8
W_SHARD = {"w_in": (449, 449, True), "w_out": (128, 128, False), "w_xq": (128, 128, False), "w_xkv": (256, 256, True),
           "w_xo": (128, 128, False), "w_gate": (352, 352, True), "w_up": (352, 352, True), "w_down": (352, 352, False)}
GATHER_FIRST = ("w_in",)
GATHER_REST = ("w_out", "w_xq", "w_xkv", "w_xo", "w_gate", "w_up", "w_down")
GRAD_GROUPS = {"ffn": ("w_gate", "w_up", "w_down"), "xattn": ("w_xq", "w_xkv", "w_xo"), "mix": ("w_in", "w_out")}

NT = (((1,), (1,)), ((), ()))
NN = (((1,), (0,)), ((), ()))
TN = (((0,), (0,)), ((), ()))
_DIMS = {"nn": NN, "nt": NT, "tn": TN}


def _params(sem):
    return pltpu.CompilerParams(dimension_semantics=sem, vmem_limit_bytes=VMEM_LIMIT)


def _mm(name, products, extras, epilogue, M, N, tm, tn, out_dtypes, params=(), n_acc=0):
    assert n_acc == 0 or tn == N
    flat = [t for p in products for t in p]
    counts = [len(p) for p in products]
    in_specs, args = [], []
    for a, b, form in flat:
        if form == "tn":
            in_specs.append(pl.BlockSpec((a.shape[0], tm), lambda i, j: (0, i)))
        else:
            in_specs.append(pl.BlockSpec((tm, a.shape[1]), lambda i, j: (i, 0)))
        if form == "nt":
            in_specs.append(pl.BlockSpec((tn, b.shape[1]), lambda i, j: (j, 0)))
        else:
            in_specs.append(pl.BlockSpec((b.shape[0], tn), lambda i, j: (0, j)))
        args += [a, b]
    for e in extras:
        in_specs.append(pl.BlockSpec((tm, tn), lambda i, j: (i, j)))
        args.append(e)
    for p in params:
        in_specs.append(pl.BlockSpec((1, tn), lambda i, j: (0, j)))
        args.append(p)
    n_in = len(args)
    n_out = len(out_dtypes)

    def body(*refs):
        ins, outs = refs[:n_in], refs[n_in:]
        prods, p = [], 0
        for c in counts:
            acc = None
            for _ in range(c):
                a = ins[2 * p][...].astype(BF)
                b = ins[2 * p + 1][...].astype(BF)
                d = lax.dot_general(a, b, _DIMS[flat[p][2]], preferred_element_type=F32)
                acc = d if acc is None else acc + d
                p += 1
            prods.append(acc)
        ex = [r[...].astype(F32) for r in ins[2 * len(flat):]]
        res = epilogue(*prods, *ex)
        for o, r in zip(outs[:n_out], res[:n_out]):
            o[...] = r.astype(o.dtype)
        for o, r in zip(outs[n_out:], res[n_out:]):
            @pl.when(pl.program_id(0) == 0)
            def _(o=o):
                o[...] = jnp.zeros(o.shape, F32)
            o[...] += r

    return pl.pallas_call(
        body, name=name, grid=(M // tm, N // tn), in_specs=in_specs,
        out_specs=[pl.BlockSpec((tm, tn), lambda i, j: (i, j)) for _ in out_dtypes]
        + [pl.BlockSpec((1, tn), lambda i, j: (0, j)) for _ in range(n_acc)],
        out_shape=[jax.ShapeDtypeStruct((M, N), dt) for dt in out_dtypes] + [jax.ShapeDtypeStruct((1, N), F32)] * n_acc,
        compiler_params=_params(("arbitrary", "arbitrary")),
    )(*args)


def _ident(x):
    return (x,)


def _add(x, r):
    return (x + r,)


def _spec(rows, w, off, per_j):
    if per_j:
        return pl.BlockSpec((rows, w), lambda j, i: (i, off + j))
    return pl.BlockSpec((rows, w), lambda j, i: (i, off))


def _pspec(rows, w, off, per_j):
    if per_j:
        return pl.BlockSpec((rows, w), lambda j, i: (0, off + j))
    return pl.BlockSpec((rows, w), lambda j, i: (0, off))


def _rw_fwd(name, fn, rows, params, outs, T, tm, nj, n_acc=0):
    in_specs = [_spec(tm, w, off, pj) for _, w, off, pj in rows] + [_pspec(a.shape[0], w, off, pj) for a, w, off, pj in params]
    args = [r[0] for r in rows] + [p[0] for p in params]
    n_in, n_out = len(args), len(outs)
    out_specs = [pl.BlockSpec((tm, w), lambda j, i: (i, j)) for _, w in outs]
    out_shape = [jax.ShapeDtypeStruct((T, nj * w), dt) for dt, w in outs]
    out_specs += [pl.BlockSpec((1, LANES), lambda j, i: (0, 0)) for _ in range(n_acc)]
    out_shape += [jax.ShapeDtypeStruct((1, LANES), F32) for _ in range(n_acc)]

    def body(*refs):
        vals = [r[...].astype(F32) for r in refs[:n_in]]
        res = fn(*vals)
        orefs = refs[n_in:]
        for k in range(n_out):
            orefs[k][...] = res[k].astype(orefs[k].dtype)
        first = (pl.program_id(0) == 0) & (pl.program_id(1) == 0)
        for k in range(n_acc):
            @pl.when(first)
            def _(k=k):
                orefs[n_out + k][...] = jnp.zeros((1, LANES), F32)
            orefs[n_out + k][...] += res[n_out + k]

    return pl.pallas_call(
        body, name=name, grid=(nj, T // tm), in_specs=in_specs, out_specs=out_specs, out_shape=out_shape,
        compiler_params=_params(("arbitrary", "arbitrary")),
    )(*args)


def _rw_bwd(name, fn, rows, params, cots, T, tm, nj, row_grads, param_grads, resid=None):
    in_specs = ([_spec(tm, w, off, pj) for _, w, off, pj in rows] + [_pspec(a.shape[0], w, off, pj) for a, w, off, pj in params]
                + [_spec(tm, w, off, pj) for _, w, off, pj in cots])
    args = [r[0] for r in rows] + [p[0] for p in params] + [c[0] for c in cots]
    if resid is not None:
        in_specs.append(_spec(tm, rows[0][1], rows[0][2], rows[0][3]))
        args.append(resid)
    nr, npar, nc = len(rows), len(params), len(cots)
    out_specs, out_shape, kinds = [], [], []
    for k, dts in enumerate(row_grads):
        for dt in (dts if isinstance(dts, (list, tuple)) else [dts]):
            if dt is not None:
                w = rows[k][1]
                out_specs.append(pl.BlockSpec((tm, w), lambda j, i: (i, j)))
                out_shape.append(jax.ShapeDtypeStruct((T, nj * w), dt))
                kinds.append(("row", k))
    for k, need in enumerate(param_grads):
        if need:
            a, w, off, pj = params[k]
            out_specs.append(_pspec(a.shape[0], w, off, pj))
            out_shape.append(jax.ShapeDtypeStruct(a.shape, F32))
            kinds.append(("par", k))

    def body(*refs):
        vals = [r[...].astype(F32) for r in refs[:nr + npar]]
        ct = tuple(r[...].astype(F32) for r in refs[nr + npar:nr + npar + nc])
        _, vjp = jax.vjp(lambda *a: tuple(fn(*a)), *vals)
        grads = list(vjp(ct))
        n_in = nr + npar + nc + (resid is not None)
        if resid is not None:
            grads[0] = grads[0] + refs[n_in - 1][...].astype(F32)
        orefs = refs[n_in:]
        j, i = pl.program_id(0), pl.program_id(1)
        for o, (kind, k) in zip(orefs, kinds):
            if kind == "row":
                o[...] = grads[k].astype(o.dtype)
            else:
                first = (i == 0) if params[k][3] else ((i == 0) & (j == 0))

                @pl.when(first)
                def _(o=o):
                    o[...] = jnp.zeros(o.shape, F32)
                o[...] += grads[nr + k]

    return pl.pallas_call(
        body, name=name, grid=(nj, T // tm), in_specs=in_specs, out_specs=out_specs, out_shape=out_shape,
        compiler_params=_params(("arbitrary", "arbitrary")),
    )(*args)


def _rms(x, g):
    return x * lax.rsqrt(jnp.mean(x * x, axis=-1, keepdims=True) + EPS) * g


def _rms_fn(x, g):
    return (_rms(x, g),)


def _lo_mask():
    return lax.broadcasted_iota(jnp.int32, (1, LANES), 1) < HEAD


def _gmean(x, lo):
    s0 = jnp.sum(jnp.where(lo, x, 0.0), axis=-1, keepdims=True)
    s1 = jnp.sum(jnp.where(lo, 0.0, x), axis=-1, keepdims=True)
    return jnp.where(lo, s0, s1) * (1.0 / HEAD)


def _fox_prep_fn(fq, fk, gq, gk):
    lo = _lo_mask()
    qn = fq * lax.rsqrt(_gmean(fq * fq, lo) + EPS) * gq * (HEAD ** -0.5)
    kn = fk * lax.rsqrt(_gmean(fk * fk, lo) + EPS) * gk
    return qn, kn


def _cast_fn(v):
    return (v,)


@jax.custom_vjp
def _swap_halves(x):
    bit = (lax.broadcasted_iota(jnp.int32, (1, LANES), 1) & (HEAD // 2)) == 0
    return jnp.where(bit, pltpu.roll(x, LANES - HEAD // 2, 1), pltpu.roll(x, HEAD // 2, 1))


_swap_halves.defvjp(lambda x: (_swap_halves(x), None), lambda _, g: (_swap_halves(g),))


def _ret_fn(rq, rk, rv, rg, cos, sin, s_in, g, lg):
    tb = rq.shape[0]
    nc = tb // CHUNK
    lo = _lo_mask()
    row = lax.broadcasted_iota(jnp.int32, (LANES, 1), 0) < HEAD
    same_head = row == lo
    q = (rq * cos + _swap_halves(rq) * sin) * (HEAD ** -0.5)
    k = rk * cos + _swap_halves(rk) * sin
    q3, k3, v3 = q.reshape(nc, CHUNK, LANES), k.reshape(nc, CHUNK, LANES), rv.reshape(nc, CHUNK, LANES)
    pos = lax.broadcasted_iota(jnp.int32, (CHUNK, 1), 0).astype(F32)
    q_decay = jnp.exp(lg * (pos + 1.0))
    k_decay = jnp.exp(lg * (CHUNK - 1.0 - pos))
    chunk_decay = jnp.exp(lg * float(CHUNK))
    dist = jnp.abs(lax.broadcasted_iota(jnp.int32, (CHUNK, CHUNK), 0) - lax.broadcasted_iota(jnp.int32, (CHUNK, CHUNK), 1)).astype(F32)
    v3b = v3.astype(BF)
    intra = []
    for hh in range(2):
        hm = lo if hh == 0 else ~lo
        lg_h = lg[:, hh * HEAD:hh * HEAD + 1]
        qm = jnp.where(hm, q3, 0.0).astype(BF)
        sc = jnp.einsum("nid,njd->nij", qm, k3.astype(BF), preferred_element_type=F32) * jnp.exp(lg_h * dist)[None]
        intra.append(jnp.einsum("nij,nje->nie", sc.astype(BF), v3b, preferred_element_type=F32))
    o = jnp.where(lo, intra[0], intra[1])
    kv = jnp.einsum("njd,nje->nde", (k3 * k_decay[None]).astype(BF), v3b, preferred_element_type=F32)
    kv = jnp.where(same_head[None], kv, 0.0)
    state, states = s_in, []
    for n in range(nc):
        states.append(state)
        state = state * chunk_decay + kv[n]
    s_prev = jnp.stack(states, axis=0)
    o = o + jnp.einsum("nid,nde->nie", (q3 * q_decay[None]).astype(BF), s_prev.astype(BF), preferred_element_type=F32)
    o = o.reshape(tb, LANES)
    mu = _gmean(o, lo)
    oc = o - mu
    y = oc * lax.rsqrt(_gmean(oc * oc, lo) + EPS) * g
    return jax.nn.silu(rg) * y, state


def _xattn_fn(qx, gq, gk, kk, vv):
    q = _rms(qx, gq)
    k = _rms(kk, gk)
    logits = lax.dot_general(q.astype(BF), k.astype(BF), NT, preferred_element_type=F32) * (XHEAD ** -0.5)
    p = jax.nn.softmax(logits, axis=-1)
    return (jnp.dot(p.astype(BF), vv.astype(BF), preferred_element_type=F32),)


def _swiglu_fwd_epi(g, u):
    return g, u, jax.nn.silu(g) * u


def _swiglu_bwd_epi(dact, g, u):
    _, vjp = jax.vjp(lambda a, b: jax.nn.silu(a) * b, g, u)
    return vjp(dact)


def _add_rms_epi(acc, resid, g):
    h = acc + resid
    return h, _rms(h, g)


def _add_loss_epi(acc, resid, target):
    err = (acc + resid) - target
    dy = err * (1.0 / D)
    part = jnp.sum(jnp.sum(err * err, axis=0, keepdims=True), axis=1, keepdims=True) * (0.5 / D)
    return dy, dy, jnp.broadcast_to(part, (1, err.shape[1]))


def _rms_bwd_epi(dhn, h, skip, g):
    _, vjp = jax.vjp(_rms, h, g)
    dh, dg = vjp(dhn)
    dh = dh + skip
    return dh, dh, dg


def _loss_fn(h, target):
    err = h - target
    part = jnp.sum(jnp.sum(err * err, axis=0, keepdims=True), axis=-1, keepdims=True) * (0.5 / D)
    dy = err * (1.0 / D)
    return dy, dy, part


def _ret_fwd(P, cos, sin, g_ret, lg, T, tb):
    nb = T // tb

    def body(rq, rk, rv, rg, c, s, g, l, o_ref, s0_ref, state):
        @pl.when(pl.program_id(1) == 0)
        def _():
            state[...] = jnp.zeros(state.shape, F32)
        s0_ref[0, 0] = state[...]
        out, s_new = _ret_fn(rq[...], rk[...], rv[...], rg[...], c[...], s[...], state[...], g[...], l[...])
        o_ref[...] = out
        state[...] = s_new

    sec = lambda off: pl.BlockSpec((tb, LANES), lambda j, i: (i, off + j))
    tab = pl.BlockSpec((tb, LANES), lambda j, i: (i, 0))
    par = pl.BlockSpec((1, LANES), lambda j, i: (0, j))
    return pl.pallas_call(
        body, name="ret_fwd", grid=(4, nb),
        in_specs=[sec(0), sec(4), sec(8), sec(12), tab, tab, par, par],
        out_specs=[pl.BlockSpec((tb, LANES), lambda j, i: (i, j)), pl.BlockSpec((1, 1, LANES, LANES), lambda j, i: (j, i, 0, 0))],
        out_shape=[jax.ShapeDtypeStruct((T, 4 * LANES), F32), jax.ShapeDtypeStruct((4, nb, LANES, LANES), F32)],
        scratch_shapes=[pltpu.VMEM((LANES, LANES), F32)],
        compiler_params=_params(("arbitrary", "arbitrary")),
    )(P, P, P, P, cos, sin, g_ret, lg)


def _ret_bwd(P, cos, sin, g_ret, lg, s0, dmix, T, tb):
    nb = T // tb

    def body(rq, rk, rv, rg, c, s, g, l, s0_ref, do, drq, drk, drv, drg, dg, dstate):
        i = pl.program_id(1)

        @pl.when(i == 0)
        def _():
            dstate[...] = jnp.zeros(dstate.shape, F32)
            dg[...] = jnp.zeros(dg.shape, F32)

        cc, ss, ll = c[...], s[...], l[...]
        _, vjp = jax.vjp(lambda a, b, v, gate, st, gg: _ret_fn(a, b, v, gate, cc, ss, st, gg, ll),
                         rq[...], rk[...], rv[...], rg[...], s0_ref[0, 0], g[...])
        ga, gb, gv, ggate, gst, ggain = vjp((do[...], dstate[...]))
        drq[...] = ga.astype(drq.dtype)
        drk[...] = gb.astype(drk.dtype)
        drv[...] = gv.astype(drv.dtype)
        drg[...] = ggate.astype(drg.dtype)
        dstate[...] = gst
        dg[...] += ggain

    rev = lambda i: nb - 1 - i
    sec = lambda off: pl.BlockSpec((tb, LANES), lambda j, i: (rev(i), off + j))
    tab = pl.BlockSpec((tb, LANES), lambda j, i: (rev(i), 0))
    par = pl.BlockSpec((1, LANES), lambda j, i: (0, j))
    outb = pl.BlockSpec((tb, LANES), lambda j, i: (rev(i), j))
    return pl.pallas_call(
        body, name="ret_bwd", grid=(4, nb),
        in_specs=[sec(0), sec(4), sec(8), sec(12), tab, tab, par, par,
                  pl.BlockSpec((1, 1, LANES, LANES), lambda j, i: (j, rev(i), 0, 0)), outb],
        out_specs=[outb, outb, outb, outb, par],
        out_shape=[jax.ShapeDtypeStruct((T, 4 * LANES), BF)] * 4 + [jax.ShapeDtypeStruct((1, 4 * LANES), F32)],
        scratch_shapes=[pltpu.VMEM((LANES, LANES), F32)],
        compiler_params=_params(("arbitrary", "arbitrary")),
    )(P, P, P, P, cos, sin, g_ret, lg, s0, dmix)


_FB = 128


def _tri(lower):
    r = lax.broadcasted_iota(jnp.int32, (_FB, _FB), 0)
    c = lax.broadcasted_iota(jnp.int32, (_FB, _FB), 1)
    return ((r >= c) if lower else (r <= c)).astype(F32)


def _fgate_fwd(ffp, bpad, T):
    def body(ff_ref, b_ref, fc_ref, fr_ref):
        lane = lax.broadcasted_iota(jnp.int32, (1, LANES), 1)
        tri = _tri(True)
        carry = jnp.zeros((1, LANES), F32)
        for blk in range(T // _FB):
            z = ff_ref[blk * _FB:(blk + 1) * _FB, :] + b_ref[...]
            lf = jnp.where(lane < 8, jax.nn.log_sigmoid(z), 0.0)
            f = jnp.dot(tri, lf, precision=lax.Precision.HIGHEST, preferred_element_type=F32) + carry
            carry = f[_FB - 1:_FB, :]
            fc_ref[blk * _FB:(blk + 1) * _FB, :] = f
            fr_ref[:, blk * _FB:(blk + 1) * _FB] = f.T[:8, :]

    return pl.pallas_call(
        body, name="fgate_fwd",
        out_shape=[jax.ShapeDtypeStruct((T, LANES), F32), jax.ShapeDtypeStruct((8, T), F32)],
        compiler_params=pltpu.CompilerParams(vmem_limit_bytes=VMEM_LIMIT),
    )(ffp, bpad)


def _fgate_bwd(ffp, bpad, dfr, T):
    def body(ff_ref, b_ref, dfr_ref, dff_ref, db_ref):
        lane = lax.broadcasted_iota(jnp.int32, (1, LANES), 1)
        tri = _tri(False)
        carry = jnp.zeros((1, LANES), F32)
        db = jnp.zeros((1, LANES), F32)
        for blk in reversed(range(T // _FB)):
            d8 = dfr_ref[:, blk * _FB:(blk + 1) * _FB]
            dcol = jnp.concatenate([d8, jnp.zeros((_FB - 8, _FB), F32)], axis=0).T
            dlf = jnp.dot(tri, dcol, precision=lax.Precision.HIGHEST, preferred_element_type=F32) + carry
            carry = dlf[0:1, :]
            z = ff_ref[blk * _FB:(blk + 1) * _FB, :] + b_ref[...]
            dz = jnp.where(lane < 8, dlf * jax.nn.sigmoid(-z), 0.0)
            dff_ref[blk * _FB:(blk + 1) * _FB, :] = dz.astype(dff_ref.dtype)
            db = db + jnp.sum(dz, axis=0, keepdims=True)
        db_ref[...] = db

    return pl.pallas_call(
        body, name="fgate_bwd",
        out_shape=[jax.ShapeDtypeStruct((T, LANES), BF), jax.ShapeDtypeStruct((1, LANES), F32)],
        compiler_params=pltpu.CompilerParams(vmem_limit_bytes=VMEM_LIMIT),
    )(ffp, bpad, dfr)


def _head_bias_col(fc, head):
    lane = lax.broadcasted_iota(jnp.int32, (1, LANES), 1)
    return jnp.sum(jnp.where(lane == head, fc, 0.0), axis=-1, keepdims=True)


def _head_bias_row(fr, head):
    sub = lax.broadcasted_iota(jnp.int32, (8, 1), 0)
    return jnp.sum(jnp.where(sub == head, fr, 0.0), axis=0, keepdims=True)


def _fox_fwd(qn, kn, vb, fc, fr, T, tq):
    nq = T // tq

    def body(q_ref, k_ref, v_ref, fc_ref, fr_ref, o_ref, c_ref):
        j, i = pl.program_id(0), pl.program_id(1)
        lane = lax.broadcasted_iota(jnp.int32, (1, LANES), 1)
        lo = lane < HEAD
        causal = lax.broadcasted_iota(jnp.int32, (tq, tq), 0) >= lax.broadcasted_iota(jnp.int32, (tq, tq), 1)
        q = q_ref[...]
        fcb = fc_ref[...]
        outs, cs = [], []
        for hh in range(2):
            hm = lo if hh == 0 else ~lo
            head = 2 * j + hh
            qh = jnp.where(hm, q, jnp.zeros_like(q))
            fq = _head_bias_col(fcb, head)

            def block(kb, carry, diag, qh=qh, fq=fq, head=head):
                m, l, acc = carry
                k0 = pl.multiple_of(kb * tq, tq)
                k = k_ref[pl.ds(k0, tq), :]
                v = v_ref[pl.ds(k0, tq), :]
                fk = _head_bias_row(fr_ref[:, pl.ds(k0, tq)], head)
                s = (lax.dot_general(qh, k, NT, preferred_element_type=F32) + fq) - fk
                if diag:
                    s = jnp.where(causal, s, NEG)
                m2 = jnp.maximum(m, jnp.max(s, axis=-1, keepdims=True))
                p = jnp.exp(s - m2)
                a = jnp.exp(m - m2)
                return m2, a * l + jnp.sum(p, axis=-1, keepdims=True), a * acc + jnp.dot(p.astype(BF), v, preferred_element_type=F32)

            init = (jnp.full((tq, 1), NEG, F32), jnp.zeros((tq, 1), F32), jnp.zeros((tq, LANES), F32))
            carry = lax.fori_loop(0, i, lambda kb, c: block(kb, c, False), init)
            m, l, acc = block(i, carry, True)
            outs.append(acc / l)
            cs.append(fq - (m + jnp.log(l)))
        o_ref[...] = jnp.where(lo, outs[0], outs[1])
        c_ref[0] = jnp.where(lane == 0, cs[0], jnp.where(lane == 1, cs[1], 0.0))

    full = lambda: pl.BlockSpec((T, LANES), lambda j, i: (0, j))
    return pl.pallas_call(
        body, name="fox_fwd", grid=(4, nq),
        in_specs=[pl.BlockSpec((tq, LANES), lambda j, i: (i, j)), full(), full(),
                  pl.BlockSpec((tq, LANES), lambda j, i: (i, 0)), pl.BlockSpec((8, T), lambda j, i: (0, 0))],
        out_specs=[pl.BlockSpec((tq, LANES), lambda j, i: (i, j)), pl.BlockSpec((1, tq, LANES), lambda j, i: (j, i, 0))],
        out_shape=[jax.ShapeDtypeStruct((T, 4 * LANES), F32), jax.ShapeDtypeStruct((4, T, LANES), F32)],
        compiler_params=_params(("parallel", "arbitrary")),
    )(qn, kn, vb, fc, fr)


def _fox_bwd_dq(qn, kn, vb, fr, cq, dmix, T, tq):
    nq = T // tq

    def body(q_ref, k_ref, v_ref, fr_ref, c_ref, do_ref, dq_ref, dl_ref, p_scr, dp_scr):
        j, i = pl.program_id(0), pl.program_id(1)
        lane = lax.broadcasted_iota(jnp.int32, (1, LANES), 1)
        lo = lane < HEAD
        causal = lax.broadcasted_iota(jnp.int32, (tq, tq), 0) >= lax.broadcasted_iota(jnp.int32, (tq, tq), 1)
        q, do, cb = q_ref[...], do_ref[...], c_ref[0]
        res, deltas = [], []
        for hh in range(2):
            hm = lo if hh == 0 else ~lo
            head = 2 * j + hh
            qh = jnp.where(hm, q, jnp.zeros_like(q))
            doh = jnp.where(hm, do, 0.0).astype(BF)
            c = cb[:, hh:hh + 1]

            def probs(kb, delta, diag, qh=qh, doh=doh, c=c, head=head):
                k0 = pl.multiple_of(kb * tq, tq)
                k = k_ref[pl.ds(k0, tq), :]
                v = v_ref[pl.ds(k0, tq), :]
                fk = _head_bias_row(fr_ref[:, pl.ds(k0, tq)], head)
                p = jnp.exp((lax.dot_general(qh, k, NT, preferred_element_type=F32) + c) - fk)
                if diag:
                    p = jnp.where(causal, p, 0.0)
                dp = lax.dot_general(doh, v, NT, preferred_element_type=F32)
                p_scr[:, pl.ds(k0, tq)] = p
                dp_scr[:, pl.ds(k0, tq)] = dp
                return delta + jnp.sum(p * dp, axis=-1, keepdims=True)

            delta = lax.fori_loop(0, i, lambda kb, d: probs(kb, d, False), jnp.zeros((tq, 1), F32))
            delta = probs(i, delta, True)

            def grad(kb, acc, delta=delta):
                k0 = pl.multiple_of(kb * tq, tq)
                ds = p_scr[:, pl.ds(k0, tq)] * (dp_scr[:, pl.ds(k0, tq)] - delta)
                return acc + jnp.dot(ds.astype(BF), k_ref[pl.ds(k0, tq), :], preferred_element_type=F32)

            res.append(lax.fori_loop(0, i + 1, grad, jnp.zeros((tq, LANES), F32)))
            deltas.append(delta)
        dq_ref[...] = jnp.where(lo, res[0], res[1])
        dl_ref[0] = jnp.where(lane == 0, deltas[0], jnp.where(lane == 1, deltas[1], 0.0))

    full = lambda: pl.BlockSpec((T, LANES), lambda j, i: (0, j))
    return pl.pallas_call(
        body, name="fox_bwd_dq", grid=(4, nq),
        in_specs=[pl.BlockSpec((tq, LANES), lambda j, i: (i, j)), full(), full(), pl.BlockSpec((8, T), lambda j, i: (0, 0)),
                  pl.BlockSpec((1, tq, LANES), lambda j, i: (j, i, 0)), pl.BlockSpec((tq, LANES), lambda j, i: (i, 4 + j))],
        out_specs=[pl.BlockSpec((tq, LANES), lambda j, i: (i, j)), pl.BlockSpec((1, tq, LANES), lambda j, i: (j, i, 0))],
        out_shape=[jax.ShapeDtypeStruct((T, 4 * LANES), F32), jax.ShapeDtypeStruct((4, T, LANES), F32)],
        scratch_shapes=[pltpu.VMEM((tq, T), F32), pltpu.VMEM((tq, T), F32)],
        compiler_params=_params(("parallel", "arbitrary")),
    )(qn, kn, vb, fr, cq, dmix)


def _fox_bwd_dkv(qn, kn, vb, fr, cq, dl, dmix, T, tq):
    nq = T // tq

    def body(q_ref, k_ref, v_ref, fr_ref, c_ref, dl_ref, do_ref, dk_ref, dv_ref, dfr_ref):
        j, kb = pl.program_id(0), pl.program_id(1)
        lo = _lo_mask()
        sub = lax.broadcasted_iota(jnp.int32, (8, 1), 0)
        causal = lax.broadcasted_iota(jnp.int32, (tq, tq), 0) >= lax.broadcasted_iota(jnp.int32, (tq, tq), 1)
        k, v, frb = k_ref[...], v_ref[...], fr_ref[...]
        dks, dvs, dfs = [], [], []
        for hh in range(2):
            hm = lo if hh == 0 else ~lo
            head = 2 * j + hh
            km = jnp.where(hm, k, jnp.zeros_like(k))
            vm = jnp.where(hm, v, jnp.zeros_like(v))
            fk = _head_bias_row(frb, head)

            def block(qi, carry, diag, km=km, vm=vm, fk=fk, hm=hm, hh=hh):
                dk, dv, df = carry
                q0 = pl.multiple_of(qi * tq, tq)
                q = q_ref[pl.ds(q0, tq), :]
                c = c_ref[0, pl.ds(q0, tq), :][:, hh:hh + 1]
                delta = dl_ref[0, pl.ds(q0, tq), :][:, hh:hh + 1]
                dob = do_ref[pl.ds(q0, tq), :].astype(BF)
                p = jnp.exp((lax.dot_general(q, km, NT, preferred_element_type=F32) + c) - fk)
                if diag:
                    p = jnp.where(causal, p, 0.0)
                dv = dv + lax.dot_general(p.astype(BF), dob, TN, preferred_element_type=F32)
                dp = lax.dot_general(dob, vm, NT, preferred_element_type=F32)
                ds = p * (dp - delta)
                dk = dk + lax.dot_general(ds.astype(BF), q, TN, preferred_element_type=F32)
                return dk, dv, df - jnp.sum(ds, axis=0, keepdims=True)

            init = (jnp.zeros((tq, LANES), F32), jnp.zeros((tq, LANES), F32), jnp.zeros((1, tq), F32))
            carry = block(kb, init, True)
            dk, dv, df = lax.fori_loop(kb + 1, nq, lambda qi, cr: block(qi, cr, False), carry)
            dks.append(dk)
            dvs.append(dv)
            dfs.append(df)
        dk_ref[...] = jnp.where(lo, dks[0], dks[1])
        dv_ref[...] = jnp.where(lo, dvs[0], dvs[1]).astype(dv_ref.dtype)
        dfr_ref[0] = jnp.where(sub == 0, dfs[0], jnp.where(sub == 1, dfs[1], 0.0))

    full = lambda off: pl.BlockSpec((T, LANES), lambda j, kb: (0, off + j))
    blk = lambda: pl.BlockSpec((tq, LANES), lambda j, kb: (kb, j))
    return pl.pallas_call(
        body, name="fox_bwd_dkv", grid=(4, nq),
        in_specs=[full(0), blk(), blk(), pl.BlockSpec((8, tq), lambda j, kb: (0, kb)),
                  pl.BlockSpec((1, T, LANES), lambda j, kb: (j, 0, 0)), pl.BlockSpec((1, T, LANES), lambda j, kb: (j, 0, 0)), full(4)],
        out_specs=[blk(), blk(), pl.BlockSpec((1, 8, tq), lambda j, kb: (j, 0, kb))],
        out_shape=[jax.ShapeDtypeStruct((T, 4 * LANES), F32), jax.ShapeDtypeStruct((T, 4 * LANES), BF),
                   jax.ShapeDtypeStruct((4, 8, T), F32)],
        compiler_params=_params(("parallel", "arbitrary")),
    )(qn, kn, vb, fr, cq, dl, dmix)


_BIAS_LANE = HEAD


def _split3(f):
    hi = f.astype(BF).astype(F32)
    mid = (f - hi).astype(BF).astype(F32)
    lo = ((f - hi) - mid).astype(BF).astype(F32)
    return hi, mid, lo


def _fox_operands(P, fc, g_fq2, g_fk2, T, tm):
    def body(fq_ref, fk_ref, fv_ref, fc_ref, gq_ref, gk_ref, qa_ref, qat_ref, ka_ref, kat_ref, va_ref, vat_ref):
        j = pl.program_id(0)
        lane = lax.broadcasted_iota(jnp.int32, (1, LANES), 1)
        qn, kn = _fox_prep_fn(fq_ref[...], fk_ref[...], gq_ref[...], gk_ref[...])
        v = fv_ref[...]
        fcb = fc_ref[...]
        b = _BIAS_LANE
        for hh in range(2):
            hi, mid, lo = _split3(_head_bias_col(fcb, 2 * j + hh))
            take = (lambda a: a) if hh == 0 else (lambda a: pltpu.roll(a, HEAD, 1))
            qa = jnp.where(lane < HEAD, take(qn), jnp.where(lane == b, hi, jnp.where(lane == b + 1, mid, jnp.where(
                lane == b + 2, lo, jnp.where(lane < b + 6, 1.0, 0.0)))))
            ka = jnp.where(lane < HEAD, take(kn), jnp.where(lane < b + 3, 1.0, jnp.where(lane == b + 3, -hi, jnp.where(
                lane == b + 4, -mid, jnp.where(lane == b + 5, -lo, 0.0)))))
            va = jnp.where(lane < HEAD, take(v), 0.0)
            for val, ref, tref in ((qa, qa_ref, qat_ref), (ka, ka_ref, kat_ref), (va, va_ref, vat_ref)):
                ref[hh] = val.astype(BF)
                tref[hh] = val.T.astype(BF)

    sec = lambda off: pl.BlockSpec((tm, LANES), lambda j, i: (i, off + j))
    par = pl.BlockSpec((1, LANES), lambda j, i: (0, 0))
    nat = pl.BlockSpec((2, tm, LANES), lambda j, i: (j, i, 0))
    trn = pl.BlockSpec((2, LANES, tm), lambda j, i: (j, 0, i))
    return pl.pallas_call(
        body, name="fox_operands", grid=(4, T // tm),
        in_specs=[sec(16), sec(20), sec(24), pl.BlockSpec((tm, LANES), lambda j, i: (i, 0)), par, par],
        out_specs=[nat, trn, nat, trn, nat, trn],
        out_shape=[jax.ShapeDtypeStruct((8, T, LANES), BF), jax.ShapeDtypeStruct((8, LANES, T), BF)] * 3,
        compiler_params=_params(("parallel", "arbitrary")),
    )(P, P, P, fc, g_fq2, g_fk2)


def _fox_forward(qat, ka, vat, T, tq, tk):
    nq, per = T // tq, tq // tk

    def body(qat_ref, ka_ref, vat_ref, o_ref, lse_ref):
        i = pl.program_id(1)
        sub = lax.broadcasted_iota(jnp.int32, (8, 1), 0)
        krow = lax.broadcasted_iota(jnp.int32, (tk, tq), 0)
        qcol = lax.broadcasted_iota(jnp.int32, (tk, tq), 1)

        def scores(kb):
            k0 = pl.multiple_of(kb * tk, tk)
            return tuple(jnp.dot(ka_ref[hh, pl.ds(k0, tk), :], qat_ref[hh], preferred_element_type=F32) for hh in range(2))

        def step(kb, carry, mask, last=False):
            stats, s_now = carry
            s_next = s_now if last else scores(kb + 1)
            k0 = pl.multiple_of(kb * tk, tk)
            new = []
            for hh in range(2):
                m, l, acc = stats[hh]
                s = s_now[hh] if mask is None else jnp.where(mask, s_now[hh], NEG)
                m2 = jnp.maximum(m, jnp.max(s, axis=0, keepdims=True))
                p = jnp.exp(s - m2)
                a = jnp.exp(m - m2)
                pv = jnp.dot(vat_ref[hh, 0:HEAD, pl.ds(k0, tk)], p.astype(BF), preferred_element_type=F32)
                new.append((m2, a * l + jnp.sum(p, axis=0, keepdims=True), a * acc + pv))
            return tuple(new), s_next

        one = (jnp.full((1, tq), NEG, F32), jnp.zeros((1, tq), F32), jnp.zeros((HEAD, tq), F32))
        carry = lax.fori_loop(0, i * per, lambda kb, c: step(kb, c, None), ((one, one), scores(0)))
        for d in range(per):
            carry = step(i * per + d, carry, krow + d * tk <= qcol, last=(d == per - 1))
        stats = carry[0]
        o_ref[...] = jnp.concatenate([acc / l for _, l, acc in stats], axis=0).T
        lses = [m + jnp.log(l) for m, l, _ in stats]
        lse_ref[0] = jnp.where(sub == 0, lses[0], jnp.where(sub == 1, lses[1], 0.0))

    return pl.pallas_call(
        body, name="fox_forward", grid=(4, nq),
        in_specs=[pl.BlockSpec((2, LANES, tq), lambda j, i: (j, 0, i)), pl.BlockSpec((2, T, LANES), lambda j, i: (j, 0, 0)),
                  pl.BlockSpec((2, LANES, T), lambda j, i: (j, 0, 0))],
        out_specs=[pl.BlockSpec((tq, LANES), lambda j, i: (i, j)), pl.BlockSpec((1, 8, tq), lambda j, i: (j, 0, i))],
        out_shape=[jax.ShapeDtypeStruct((T, 4 * LANES), F32), jax.ShapeDtypeStruct((4, 8, T), F32)],
        compiler_params=_params(("parallel", "arbitrary")),
    )(qat, ka, vat)


def _fox_cotangent(dmix, fox, T, tm):
    def body(do_ref, o_ref, doa_ref, doat_ref, dl_ref):
        lane = lax.broadcasted_iota(jnp.int32, (1, LANES), 1)
        sub = lax.broadcasted_iota(jnp.int32, (8, 1), 0)
        dob = do_ref[...].astype(BF).astype(F32)
        prod_t = (dob * o_ref[...]).T
        d0 = jnp.sum(prod_t[:HEAD], axis=0, keepdims=True)
        d1 = jnp.sum(prod_t[HEAD:], axis=0, keepdims=True)
        dl_ref[0] = jnp.where(sub == 0, d0, jnp.where(sub == 1, d1, 0.0))
        for hh in range(2):
            val = jnp.where(lane < HEAD, dob if hh == 0 else pltpu.roll(dob, HEAD, 1), 0.0)
            doa_ref[hh] = val.astype(BF)
            doat_ref[hh] = val.T.astype(BF)

    return pl.pallas_call(
        body, name="fox_cotangent", grid=(4, T // tm),
        in_specs=[pl.BlockSpec((tm, LANES), lambda j, i: (i, 4 + j)), pl.BlockSpec((tm, LANES), lambda j, i: (i, j))],
        out_specs=[pl.BlockSpec((2, tm, LANES), lambda j, i: (j, i, 0)), pl.BlockSpec((2, LANES, tm), lambda j, i: (j, 0, i)),
                   pl.BlockSpec((1, 8, tm), lambda j, i: (j, 0, i))],
        out_shape=[jax.ShapeDtypeStruct((8, T, LANES), BF), jax.ShapeDtypeStruct((8, LANES, T), BF),
                   jax.ShapeDtypeStruct((4, 8, T), F32)],
        compiler_params=_params(("parallel", "arbitrary")),
    )(dmix, fox)


def _fox_backward(qa, qat, ka, kat, va, doa, doat, lse, dl, T, tq, tk):
    nq, nk = T // tq, T // tk
    HEAD_GROUPS = ((0, 1),)

    def body(qa_ref, qat_ref, ka_ref, kat_ref, va_ref, doa_ref, doat_ref, lse_ref, dl_ref,
             dq_ref, dk_ref, dv_ref, df_ref, dr_ref, dqt, dk_acc, dv_acc, df_acc):
        j, kb = pl.program_id(0), pl.program_id(1)
        lane = lax.broadcasted_iota(jnp.int32, (1, LANES), 1)
        first = (kb * tk) // tq
        mask = (lax.broadcasted_iota(jnp.int32, (tk, tq), 0) + (kb * tk - first * tq)
                <= lax.broadcasted_iota(jnp.int32, (tk, tq), 1))

        @pl.when(kb == 0)
        def _():
            dqt[...] = jnp.zeros(dqt.shape, F32)

        dk_acc[...] = jnp.zeros(dk_acc.shape, F32)
        dv_acc[...] = jnp.zeros(dv_acc.shape, F32)
        df_acc[...] = jnp.zeros(df_acc.shape, F32)

        def products(qi, heads):
            q0 = pl.multiple_of(qi * tq, tq)
            return tuple((jnp.dot(ka_ref[hh], qat_ref[hh, :, pl.ds(q0, tq)], preferred_element_type=F32),
                          jnp.dot(va_ref[hh], doat_ref[hh, :, pl.ds(q0, tq)], preferred_element_type=F32)) for hh in heads)

        def block(qi, now, keep, heads):
            q0 = pl.multiple_of(qi * tq, tq)
            for n, hh in enumerate(heads):
                s, dp = now[n]
                p = jnp.exp(s - lse_ref[0, hh:hh + 1, pl.ds(q0, tq)])
                if keep is not None:
                    p = jnp.where(keep, p, 0.0)
                ds = p * (dp - dl_ref[0, hh:hh + 1, pl.ds(q0, tq)])
                pb, dsb = p.astype(BF), ds.astype(BF)
                dv_acc[hh] += jnp.dot(pb, doa_ref[hh, pl.ds(q0, tq), :], preferred_element_type=F32)
                dk_acc[hh] += jnp.dot(dsb, qa_ref[hh, pl.ds(q0, tq), :], preferred_element_type=F32)
                dqt[hh, 0:HEAD, pl.ds(q0, tq)] += jnp.dot(kat_ref[hh, 0:HEAD, :], dsb, preferred_element_type=F32)
                dqt[hh, HEAD:HEAD + 8, pl.ds(q0, tq)] += jnp.broadcast_to(jnp.sum(ds, axis=0, keepdims=True), (8, tq))
                part = ds[:, 0:LANES]
                for c in range(1, tq // LANES):
                    part = part + ds[:, c * LANES:(c + 1) * LANES]
                df_acc[hh] += part

        def step(qi, now, keep, heads):
            ahead = products(jnp.minimum(qi + 1, nq - 1), heads)
            block(qi, now, keep, heads)
            return ahead

        for heads in HEAD_GROUPS:
            lax.fori_loop(first + 1, nq, lambda qi, now, heads=heads: step(qi, now, None, heads),
                          step(first, products(first, heads), mask, heads))

        lo = lane < HEAD
        dk_ref[...] = jnp.where(lo, dk_acc[0], pltpu.roll(dk_acc[1], HEAD, 1))
        dv_ref[...] = jnp.where(lo, dv_acc[0], pltpu.roll(dv_acc[1], HEAD, 1)).astype(dv_ref.dtype)
        f0 = -jnp.sum(df_acc[0], axis=1, keepdims=True)
        f1 = -jnp.sum(df_acc[1], axis=1, keepdims=True)
        df_ref[0] = jnp.where(lane == 2 * j, f0, jnp.where(lane == 2 * j + 1, f1, 0.0))

        @pl.when(kb == nk - 1)
        def _():
            for t in range(nq):
                cols = slice(t * tq, (t + 1) * tq)
                dq_ref[cols, :] = jnp.concatenate([dqt[0, 0:HEAD, cols], dqt[1, 0:HEAD, cols]], axis=0).T
                rsum = jnp.concatenate([dqt[0, HEAD:HEAD + 8, cols], dqt[1, HEAD:HEAD + 8, cols],
                                        jnp.zeros((LANES - 16, tq), F32)], axis=0).T
                dr_ref[0, cols, :] = jnp.where(lane == 2 * j, rsum[:, 0:1], jnp.where(lane == 2 * j + 1, rsum[:, 8:9], 0.0))

    nat_full = pl.BlockSpec((2, T, LANES), lambda j, kb: (j, 0, 0))
    trn_full = pl.BlockSpec((2, LANES, T), lambda j, kb: (j, 0, 0))
    nat_blk = pl.BlockSpec((2, tk, LANES), lambda j, kb: (j, kb, 0))
    trn_blk = pl.BlockSpec((2, LANES, tk), lambda j, kb: (j, 0, kb))
    rows = pl.BlockSpec((1, 8, T), lambda j, kb: (j, 0, 0))
    blk = pl.BlockSpec((tk, LANES), lambda j, kb: (kb, j))
    return pl.pallas_call(
        body, name="fox_backward", grid=(4, nk),
        in_specs=[nat_full, trn_full, nat_blk, trn_blk, nat_blk, nat_full, trn_full, rows, rows],
        out_specs=[pl.BlockSpec((T, LANES), lambda j, kb: (0, j)), blk, blk, pl.BlockSpec((1, tk, LANES), lambda j, kb: (j, kb, 0)),
                   pl.BlockSpec((1, T, LANES), lambda j, kb: (j, 0, 0))],
        out_shape=[jax.ShapeDtypeStruct((T, 4 * LANES), F32), jax.ShapeDtypeStruct((T, 4 * LANES), F32),
                   jax.ShapeDtypeStruct((T, 4 * LANES), BF), jax.ShapeDtypeStruct((4, T, LANES), F32),
                   jax.ShapeDtypeStruct((4, T, LANES), F32)],
        scratch_shapes=[pltpu.VMEM((2, HEAD + 8, T), F32), pltpu.VMEM((2, tk, LANES), F32), pltpu.VMEM((2, tk, LANES), F32),
                        pltpu.VMEM((2, tk, LANES), F32)],
        compiler_params=_params(("arbitrary", "arbitrary")),
    )(qa, qat, ka, kat, va, doa, doat, lse, dl)


def _fgate_bwd_col(ffp, bpad, dfc, T):
    def body(ff_ref, b_ref, dfc_ref, dff_ref, db_ref):
        lane = lax.broadcasted_iota(jnp.int32, (1, LANES), 1)
        tri = _tri(False)
        carry = jnp.zeros((1, LANES), F32)
        db = jnp.zeros((1, LANES), F32)
        for blk in reversed(range(T // _FB)):
            dlf = jnp.dot(tri, dfc_ref[blk * _FB:(blk + 1) * _FB, :], precision=lax.Precision.HIGHEST,
                          preferred_element_type=F32) + carry
            carry = dlf[0:1, :]
            z = ff_ref[blk * _FB:(blk + 1) * _FB, :] + b_ref[...]
            dz = jnp.where(lane < 8, dlf * jax.nn.sigmoid(-z), 0.0)
            dff_ref[blk * _FB:(blk + 1) * _FB, :] = dz.astype(dff_ref.dtype)
            db = db + jnp.sum(dz, axis=0, keepdims=True)
        db_ref[...] = db

    return pl.pallas_call(
        body, name="fgate_bwd",
        out_shape=[jax.ShapeDtypeStruct((T, LANES), BF), jax.ShapeDtypeStruct((1, LANES), F32)],
        compiler_params=pltpu.CompilerParams(vmem_limit_bytes=VMEM_LIMIT),
    )(ffp, bpad, dfc)


MESH = pl.DeviceIdType.MESH


def _place():
    return lax.axis_index("x"), lax.axis_index("y"), lax.axis_index("c")


def _all_gather(shard):
    R, W = shard.shape

    def body(x_ref, out_ref, send_sems, recv_sems, local_sem):
        x, y, c = _place()
        me, sibling = (x, y, c), (x, y, 1 - c)
        chips = [(1 - x, y), (x, 1 - y), (1 - x, 1 - y)]

        def slot(px, py, pc):
            return out_ref.at[4 * px + 2 * py + pc]

        def copy(k, block, to, src=None):
            return pltpu.make_async_remote_copy(
                src_ref=slot(*block) if src is None else src, dst_ref=slot(*block),
                send_sem=send_sems.at[k], recv_sem=recv_sems.at[k], device_id=to, device_id_type=MESH)

        mine = pltpu.make_async_copy(x_ref, slot(*me), local_sem)
        mine.start()
        first = [copy(0, me, sibling, src=x_ref)]
        first += [copy(1 + n, me, (*chip, c), src=x_ref) for n, chip in enumerate(chips)]
        for cp in first:
            cp.start()
        passed = [copy(4 + n, (*chip, c), sibling) for n, chip in enumerate(chips)]
        for n, chip in enumerate(chips):
            copy(1 + n, (*chip, c), me).wait_recv()
            passed[n].start()
        copy(0, sibling, me).wait_recv()
        for n, chip in enumerate(chips):
            copy(4 + n, (*chip, 1 - c), me).wait_recv()
        for cp in first + passed:
            cp.wait_send()
        mine.wait()

    return pl.pallas_call(
        body, name="all_gather_weights",
        out_shape=jax.ShapeDtypeStruct((N_DEV, R, W), shard.dtype),
        in_specs=[pl.BlockSpec(memory_space=pl.ANY)], out_specs=pl.BlockSpec(memory_space=pl.ANY),
        scratch_shapes=[pltpu.SemaphoreType.DMA((7,)), pltpu.SemaphoreType.DMA((7,)), pltpu.SemaphoreType.DMA],
    )(shard)


def _all_to_all(big, small):
    def body(big_ref, small_ref, rbig_ref, rsmall_ref, send_sems, recv_sems, local_sems):
        x, y, c = _place()
        me = 4 * x + 2 * y + c
        l0 = pltpu.make_async_copy(big_ref.at[me], rbig_ref.at[me], local_sems.at[0])
        l1 = pltpu.make_async_copy(small_ref, rsmall_ref.at[me], local_sems.at[1])
        l0.start()
        l1.start()
        copies = []
        for r in range(1, N_DEV):
            px, py, pc = x ^ (r >> 2), y ^ ((r >> 1) & 1), c ^ (r & 1)
            peer = 4 * px + 2 * py + pc
            copies.append(pltpu.make_async_remote_copy(
                src_ref=big_ref.at[peer], dst_ref=rbig_ref.at[me], send_sem=send_sems.at[2 * r], recv_sem=recv_sems.at[2 * r],
                device_id=(px, py, pc), device_id_type=MESH))
            copies.append(pltpu.make_async_remote_copy(
                src_ref=small_ref, dst_ref=rsmall_ref.at[me], send_sem=send_sems.at[2 * r + 1], recv_sem=recv_sems.at[2 * r + 1],
                device_id=(px, py, pc), device_id_type=MESH))
        for cp in copies:
            cp.start()
        for cp in copies:
            cp.wait_recv()
        for cp in copies:
            cp.wait_send()
        l0.wait()
        l1.wait()

    return pl.pallas_call(
        body, name="all_to_all_grads",
        out_shape=[jax.ShapeDtypeStruct(big.shape, big.dtype), jax.ShapeDtypeStruct((N_DEV,) + small.shape, small.dtype)],
        in_specs=[pl.BlockSpec(memory_space=pl.ANY)] * 2, out_specs=[pl.BlockSpec(memory_space=pl.ANY)] * 2,
        scratch_shapes=[pltpu.SemaphoreType.DMA((2 * N_DEV,)), pltpu.SemaphoreType.DMA((2 * N_DEV,)), pltpu.SemaphoreType.DMA((2,))],
    )(big, small)


def _exchange_copies(src_ref, land_ref, send_sems, recv_sems, scatter):
    x, y, c = _place()
    me = 4 * x + 2 * y + c
    copies = []
    for r in range(1, N_DEV):
        px, py, pc = x ^ (r >> 2), y ^ ((r >> 1) & 1), c ^ (r & 1)
        copies.append(pltpu.make_async_remote_copy(
            src_ref=src_ref.at[4 * px + 2 * py + pc] if scatter else src_ref, dst_ref=land_ref.at[me],
            send_sem=send_sems.at[r - 1], recv_sem=recv_sems.at[r - 1], device_id=(px, py, pc), device_id_type=MESH))
    return copies


_HBM = pl.BlockSpec(memory_space=pltpu.HBM)
_SEM = pl.BlockSpec(memory_space=pltpu.SEMAPHORE)
_EFFECT = pltpu.SideEffectType.DATAFLOW_SIDE_EFFECTING


def _exchange_start(name, src, land, scatter):
    def body(src_ref, land_ref, send_sems, recv_sems, src_thru, land_thru, token):
        for cp in _exchange_copies(src_ref, land_ref, send_sems, recv_sems, scatter):
            cp.start()
        token[...] = jnp.zeros(token.shape, F32)

    return pl.pallas_call(
        body, name=name,
        out_shape=(pltpu.SemaphoreType.DMA((N_DEV - 1,)), pltpu.SemaphoreType.DMA((N_DEV - 1,)),
                   pltpu.HBM(src.shape, src.dtype), pltpu.HBM(land.shape, land.dtype), jax.ShapeDtypeStruct((8, LANES), F32)),
        in_specs=(_HBM, _HBM), out_specs=(_SEM, _SEM, _HBM, _HBM, pl.BlockSpec(memory_space=pltpu.VMEM)),
        input_output_aliases={0: 2, 1: 3},
        compiler_params=pltpu.CompilerParams(has_side_effects=_EFFECT),
    )(pltpu.with_memory_space_constraint(src, pltpu.HBM), pltpu.with_memory_space_constraint(land, pltpu.HBM))


def _exchange_wait(name, started, after, scatter):
    send_sems, recv_sems, src_thru, land_thru, _ = started

    def body(src_ref, land_ref, send_sems, recv_sems, after_ref, src_dead, got_ref):
        copies = _exchange_copies(src_ref, land_ref, send_sems, recv_sems, scatter)
        for cp in copies:
            cp.wait_send()
        for cp in copies:
            cp.wait_recv()

    return pl.pallas_call(
        body, name=name,
        out_shape=(pltpu.HBM(src_thru.shape, src_thru.dtype), pltpu.HBM(land_thru.shape, land_thru.dtype)),
        in_specs=(_HBM, _HBM, _SEM, _SEM, pl.BlockSpec(memory_space=pl.ANY)), out_specs=(_HBM, _HBM),
        input_output_aliases={0: 0, 1: 1},
        compiler_params=pltpu.CompilerParams(has_side_effects=_EFFECT),
    )(src_thru, land_thru, send_sems, recv_sems, after)


def _adamw(name, slots, w, m, v, tr, own=None):
    R, W = w.shape

    def body(s_ref, *refs):
        if own is not None:
            own_ref, refs = refs[0], refs[1:]
        w_ref, m_ref, v_ref, g_ref, d_ref, nm_ref, nv_ref = refs
        g = s_ref[0].astype(F32)
        for s in range(1, N_DEV):
            g = g + s_ref[s].astype(F32)
        if own is not None:
            g = g + own_ref[...].astype(F32)
        m2 = ADAM_B1 * m_ref[...] + (1.0 - ADAM_B1) * g
        v2 = ADAM_B2 * v_ref[...] + (1.0 - ADAM_B2) * jnp.square(g)
        m_hat = m2 / (1.0 - ADAM_B1 ** ADAM_STEP)
        v_hat = v2 / (1.0 - ADAM_B2 ** ADAM_STEP)
        g_ref[...] = g
        d_ref[...] = -ADAM_LR * (m_hat / (jnp.sqrt(v_hat) + ADAM_EPS) + ADAM_WD * w_ref[...])
        nm_ref[...] = m2
        nv_ref[...] = v2

    row = lambda: pl.BlockSpec((tr, W), lambda i: (i, 0))
    return pl.pallas_call(
        body, name=name, grid=(R // tr,),
        in_specs=[pl.BlockSpec((N_DEV, tr, W), lambda i: (0, i, 0))] + [row() for _ in range(3 + (own is not None))],
        out_specs=[row(), row(), row(), row()],
        out_shape=[jax.ShapeDtypeStruct((R, W), F32)] * 4,
        compiler_params=_params(("parallel",)),
    )(slots, *([own] if own is not None else []), w, m, v)


def _tables(T):
    pos = jnp.arange(T, dtype=F32)
    inv_freq = 10000.0 ** (-jnp.arange(0, HEAD, 2, dtype=F32) / HEAD)
    ang = pos[:, None] * inv_freq[None, :]
    cos, sin = jnp.cos(ang), jnp.sin(ang)
    cos4 = jnp.tile(cos, (1, 4))
    sin4 = jnp.tile(jnp.concatenate([-sin, sin], axis=1), (1, 2))
    log_g = jnp.log(1.0 - 2.0 ** (-5.0 - jnp.arange(8, dtype=F32)))
    return cos4, sin4, jnp.repeat(log_g, HEAD)[None, :]


def _local_step(x, mem, target, sp, w_inT, token, fetch_rest, push):
    T = x.shape[0]
    tm = min(512, T)
    tq = min(256, T)
    tb = min(1024, T)
    cos4, sin4, lg = _tables(T)
    g_fq2 = jnp.tile(sp["g_fox_q"], (1, 2))
    g_fk2 = jnp.tile(sp["g_fox_k"], (1, 2))
    g_ret = sp["g_ret_out"].reshape(1, 8 * HEAD)
    bpad = jnp.pad(sp["b_forget"], ((0, 0), (0, LANES - 8)))
    w_secs = [w_inT[k * 512:(k + 1) * 512] for k in range(7)]
    w_ffT = jnp.pad(w_inT[3584:3592], ((0, LANES - 8), (0, 0)))
    w_mainT = w_inT[:3584]
    tie = lambda p, tok: p + tok[0:1, 0:1]
    tm2, tm4 = min(1024, T), min(2048, T)

    hn1, = _rw_fwd("rms_mix", _rms_fn, [(x, D, 0, False)], [(tie(sp["g_mix"], token), D, 0, False)], [(BF, D)], T, tm, 1)
    P, = _mm("proj_in", [[(hn1, w_mainT, "nt")]], [], _ident, T, 3584, tm4, 512, [F32])
    ffp, = _mm("proj_ff", [[(hn1, w_ffT, "nt")]], [], _ident, T, LANES, tm, LANES, [F32])
    ret, s0 = _ret_fwd(P, cos4, sin4, g_ret, lg, T, tb)
    fc, _ = _fgate_fwd(ffp, bpad, T)
    qa, qat, ka, kat, va, vat = _fox_operands(P, fc, g_fq2, g_fk2, T, tm)
    fox, lse = _fox_forward(qat, ka, vat, T, min(512, T), tq)
    W = fetch_rest(fox)
    w_out_halves = (W["w_out"][:4 * LANES], W["w_out"][4 * LANES:])
    h1, hn2 = _mm("proj_out", [[(ret, w_out_halves[0], "nn"), (fox, w_out_halves[1], "nn")]], [x], _add_rms_epi, T, D, tm2, D,
                  [F32, BF], params=[sp["g_xattn"]])

    qx, = _mm("proj_xq", [[(hn2, W["w_xq"], "nn")]], [], _ident, T, D, tm2, D, [F32])
    memn, = _rw_fwd("rms_mem", _rms_fn, [(mem, D, 0, False)], [(sp["g_mem"], D, 0, False)], [(BF, D)], N_MEM, N_MEM, 1)
    kv, = _mm("proj_xkv", [[(memn, W["w_xkvT"], "nt")]], [], _ident, N_MEM, 2 * D, N_MEM, 512, [F32])
    xa_rows = [(qx, XHEAD, 0, True)]
    xa_params = [(sp["g_xq"], XHEAD, 0, False), (sp["g_xk"], XHEAD, 0, False), (kv, XHEAD, 0, True), (kv, XHEAD, 4, True)]
    xo, = _rw_fwd("xattn_fwd", _xattn_fn, xa_rows, xa_params, [(BF, XHEAD)], T, tm, 4)
    h2, hn3 = _mm("proj_xo", [[(xo, W["w_xo"], "nn")]], [h1], _add_rms_epi, T, D, tm2, D, [F32, BF], params=[sp["g_ffn"]])

    gate, up, act = _mm("ffn_in", [[(hn3, W["w_gateT"], "nt")], [(hn3, W["w_upT"], "nt")]], [], _swiglu_fwd_epi,
                        T, D_FF, tm4, 256, [BF, BF, BF])
    dy, dyb, loss_part = _mm("ffn_out", [[(act, W["w_down"], "nn")]], [h2, target], _add_loss_epi, T, D, tm, D, [F32, BF], n_acc=1)

    dgate, dup = _mm("ffn_out_bwd", [[(dyb, W["w_down"], "nt")]], [gate, up], _swiglu_bwd_epi, T, D_FF, tm4, 256, [BF, BF])
    gW = {}
    gW["w_gateT"], = _mm("dw_gate", [[(dgate, hn3, "tn")]], [], _ident, D_FF, D, 256, D, [BF])
    gW["w_upT"], = _mm("dw_up", [[(dup, hn3, "tn")]], [], _ident, D_FF, D, 256, D, [BF])
    gW["w_down"], = _mm("dw_down", [[(act, dyb, "tn")]], [], _ident, D_FF, D, 256, D, [BF])
    tok = push("ffn", gW)
    gs = {}
    dh2, dh2b, gs["g_ffn"] = _mm("ffn_in_bwd", [[(dgate, W["w_gateT"], "nn"), (dup, W["w_upT"], "nn")]], [h2, dy], _rms_bwd_epi,
                                 T, D, min(256, T), D, [F32, BF], params=[tie(sp["g_ffn"], tok)], n_acc=1)

    dxo, = _mm("proj_xo_bwd", [[(dh2b, W["w_xo"], "nt")]], [], _ident, T, D, tm2, D, [BF])
    gW["w_xo"], = _mm("dw_xo", [[(xo, dh2b, "tn")]], [], _ident, D, D, 256, D, [BF])
    dqx, gs["g_xq"], gs["g_xk"], dkv_k, dkv_v = _rw_bwd(
        "xattn_bwd", _xattn_fn, xa_rows, xa_params, [(dxo, XHEAD, 0, True)], T, tm, 4, [BF], [True, True, True, True])
    dkv = jnp.concatenate([dkv_k[:, :D], dkv_v[:, D:]], axis=1)
    gW["w_xq"], = _mm("dw_xq", [[(hn2, dqx, "tn")]], [], _ident, D, D, 256, D, [BF])
    dmemn, = _mm("proj_xkv_bwd", [[(dkv, W["w_xkvT"], "nn")]], [], _ident, N_MEM, D, N_MEM, 512, [F32])
    gW["w_xkvT"], = _mm("dw_xkv", [[(dkv, memn, "tn")]], [], _ident, 2 * D, D, 512, D, [BF])
    tok = push("xattn", gW)
    gs["g_mem"], = _rw_bwd("rms_mem_bwd", _rms_fn, [(mem, D, 0, False)], [(sp["g_mem"], D, 0, False)], [(dmemn, D, 0, False)],
                           N_MEM, N_MEM, 1, [None], [True])
    dh1, dh1b, gs["g_xattn"] = _mm("proj_xq_bwd", [[(dqx, W["w_xq"], "nt")]], [h1, dh2], _rms_bwd_epi, T, D, tm, D, [F32, BF],
                                   params=[tie(sp["g_xattn"], tok)], n_acc=1)

    dmix, = _mm("proj_out_bwd", [[(dh1b, W["w_out"], "nt")]], [], _ident, T, D, tm2, D, [F32])
    gW["w_out"] = jnp.concatenate([_mm("dw_out_%d" % k, [[(a, dh1b, "tn")]], [], _ident, 4 * LANES, D, 256, D, [BF])[0]
                                   for k, a in enumerate((ret, fox))], axis=0)
    tok = push("out", gW)
    doa, doat, dl = _fox_cotangent(dmix, fox, T, tm)
    dqn, dkn, dfv, dfc4, drc4 = _fox_backward(qa, qat, ka, kat, va, doa, doat, lse + tok[0:1, 0:1], dl, T, tq, tq)
    dfq, dfk, gq2, gk2 = _rw_bwd("fox_prep_bwd", _fox_prep_fn, [(P, LANES, 16, True), (P, LANES, 20, True)],
                                 [(g_fq2, LANES, 0, False), (g_fk2, LANES, 0, False)],
                                 [(dqn, LANES, 0, True), (dkn, LANES, 0, True)], T, tm, 4, [BF, BF], [True, True])
    gs["g_fox_q"] = gq2[:, :HEAD] + gq2[:, HEAD:]
    gs["g_fox_k"] = gk2[:, :HEAD] + gk2[:, HEAD:]
    dff, dbp = _fgate_bwd_col(ffp, bpad, jnp.sum(dfc4 + drc4, axis=0), T)
    gs["b_forget"] = dbp[:, :8]
    drq, drk, drv, drg, dg_ret = _ret_bwd(P, cos4, sin4, g_ret, lg, s0, dmix, T, tb)
    gs["g_ret_out"] = dg_ret
    dsecs = [drq, drk, drv, drg, dfq, dfk, dfv]
    g_secs = [_mm("dw_in_%d" % k, [[(d, hn1, "tn")]], [], _ident, 512, D, 256, D, [BF])[0] for k, d in enumerate(dsecs)]
    g_ff, = _mm("dw_in_ff", [[(dff, hn1, "tn")]], [], _ident, LANES, D, LANES, D, [BF])
    gW["w_inT"] = jnp.concatenate(g_secs + [g_ff[:8]], axis=0)
    tok = push("in", gW)
    grad_x, _, gs["g_mix"] = _mm("proj_in_bwd", [[(d, w, "nn") for d, w in zip(dsecs, w_secs)] + [(dff, w_ffT, "nn")]], [x, dh1],
                                 _rms_bwd_epi, T, D, tm, D, [F32, BF], params=[tie(sp["g_mix"], tok)], n_acc=1)
    return loss_part, grad_x, gs


_CANON = {"w_in": "w_inT", "w_xkv": "w_xkvT", "w_gate": "w_gateT", "w_up": "w_upT"}
_SMALL = (("g_mix", 0, 0, 1024), ("g_xattn", 1, 0, 1024), ("g_mem", 2, 0, 1024), ("g_ffn", 3, 0, 1024),
          ("g_ret_out", 4, 0, 512), ("g_xq", 4, 512, 256), ("g_xk", 4, 768, 256),
          ("g_fox_q", 5, 0, 64), ("g_fox_k", 5, 64, 64), ("b_forget", 5, 128, 8))
_LOSS_AT = (5, 256)


def _pack_shards(tree, dtype):
    parts = []
    for name, rows, padded, transposed in W_LAYOUT:
        a = tree[name][0]
        a = a.T if transposed else a
        parts.append(jnp.pad(a, ((0, padded - rows), (0, 0))).astype(dtype))
    return jnp.concatenate(parts, axis=0)


def _unpack_shards(packed, like):
    out = {}
    for name, rows, padded, transposed in W_LAYOUT:
        a = packed[W_OFF[name]:W_OFF[name] + rows]
        out[name] = (a.T if transposed else a)[None].reshape(like[name].shape)
    return out


def _pack_small(tree):
    rows = [jnp.zeros((1, D), F32) for _ in range(SMALL_ROWS)]
    buf = jnp.concatenate(rows, axis=0)
    for name, r, c, n in _SMALL:
        buf = lax.dynamic_update_slice(buf, tree[name].reshape(1, n).astype(F32), (r, c))
    return buf


def _unpack_small(buf, like):
    return {name: buf[r:r + 1, c:c + n].reshape(like[name].shape) for name, r, c, n in _SMALL}


def kernel(x, mem, g_mix, w_in, b_forget, g_ret_out, g_fox_q, g_fox_k, w_out, g_xattn, w_xq, w_xkv, g_mem, g_xq, g_xk, w_xo, g_ffn, w_gate, w_up, w_down, loss_target, m_g_mix, m_w_in, m_b_forget, m_g_ret_out, m_g_fox_q, m_g_fox_k, m_w_out, m_g_xattn, m_w_xq, m_w_xkv, m_g_mem, m_g_xq, m_g_xk, m_w_xo, m_g_ffn, m_w_gate, m_w_up, m_w_down, v_g_mix, v_w_in, v_b_forget, v_g_ret_out, v_g_fox_q, v_g_fox_k, v_w_out, v_g_xattn, v_w_xq, v_w_xkv, v_g_mem, v_g_xq, v_g_xk, v_w_xo, v_g_ffn, v_w_gate, v_w_up, v_w_down):
    names = ("g_mix", "w_in", "b_forget", "g_ret_out", "g_fox_q", "g_fox_k", "w_out", "g_xattn", "w_xq", "w_xkv", "g_mem",
             "g_xq", "g_xk", "w_xo", "g_ffn", "w_gate", "w_up", "w_down")
    w = dict(zip(names, (g_mix, w_in, b_forget, g_ret_out, g_fox_q, g_fox_k, w_out, g_xattn, w_xq, w_xkv, g_mem, g_xq, g_xk,
                         w_xo, g_ffn, w_gate, w_up, w_down)))
    m = dict(zip(names, (m_g_mix, m_w_in, m_b_forget, m_g_ret_out, m_g_fox_q, m_g_fox_k, m_w_out, m_g_xattn, m_w_xq, m_w_xkv,
                         m_g_mem, m_g_xq, m_g_xk, m_w_xo, m_g_ffn, m_w_gate, m_w_up, m_w_down)))
    v = dict(zip(names, (v_g_mix, v_w_in, v_b_forget, v_g_ret_out, v_g_fox_q, v_g_fox_k, v_w_out, v_g_xattn, v_w_xq, v_w_xkv,
                         v_g_mem, v_g_xq, v_g_xk, v_w_xo, v_g_ffn, v_w_gate, v_w_up, v_w_down)))
    small_names = [s[0] for s in _SMALL]

    gathered = _all_gather(_pack_shards(w, BF))
    W = {}
    for name, rows, padded, transposed in W_LAYOUT:
        full = gathered[:, W_OFF[name]:W_OFF[name] + rows].reshape(N_DEV * rows, D)
        W[_CANON.get(name, name)] = full

    sp = {n: w[n].reshape(1, -1) for n in small_names}
    loss_part, grad_x, gW, gs = _local_step(x[0], mem[0], loss_target[0], sp, W)

    chunks = []
    for name, rows, padded, transposed in W_LAYOUT:
        g = gW[_CANON.get(name, name)].reshape(N_DEV, rows, D)
        chunks.append(jnp.pad(g, ((0, 0), (0, padded - rows), (0, 0))).astype(BF))
    send = jnp.concatenate(chunks, axis=1)
    small = _pack_small(gs)
    small = lax.dynamic_update_slice(small, loss_part[:, :1], _LOSS_AT)
    recv, recv_small = _all_to_all(send, small)

    g_big, d_big, m_big, v_big = _adamw("adamw_shards", recv, _pack_shards(w, F32), _pack_shards(m, F32), _pack_shards(v, F32), 240)
    g_sm, d_sm, m_sm, v_sm = _adamw("adamw_small", recv_small, _pack_small(w), _pack_small(m), _pack_small(v), SMALL_ROWS)
    loss = g_sm[_LOSS_AT[0], _LOSS_AT[1]]

    outs = []
    for big, sm in ((g_big, g_sm), (d_big, d_sm), (m_big, m_sm), (v_big, v_sm)):
        tree = {**_unpack_shards(big, w), **_unpack_small(sm, w)}
        outs += [tree[n] for n in names]
    return (loss, grad_x[None], *outs)


def _pack_shards(tree, names, dtype):
    parts = []
    for name in names:
        rows, padded, transposed = W_SHARD[name]
        a = tree[name][0]
        a = a.T if transposed else a
        parts.append(jnp.pad(a, ((0, padded - rows), (0, 0))).astype(dtype))
    return jnp.concatenate(parts, axis=0)


def _unpack_shards(packed, names, like):
    out, off = {}, 0
    for name in names:
        rows, padded, transposed = W_SHARD[name]
        a = packed[off:off + rows]
        out[name] = (a.T if transposed else a)[None].reshape(like[name].shape)
        off += padded
    return out


def _unpack_gathered(gathered, names):
    out, off = {}, 0
    for name in names:
        rows, padded, _ = W_SHARD[name]
        out[_CANON.get(name, name)] = gathered[:, off:off + rows].reshape(N_DEV * rows, D)
        off += padded
    return out


def _pack_chunks(grads, names):
    chunks = []
    for name in names:
        rows, padded, _ = W_SHARD[name]
        g = grads[_CANON.get(name, name)].reshape(N_DEV, rows, D)
        chunks.append(jnp.pad(g, ((0, 0), (0, padded - rows), (0, 0))).astype(BF))
    return jnp.concatenate(chunks, axis=1)


def kernel(x, mem, g_mix, w_in, b_forget, g_ret_out, g_fox_q, g_fox_k, w_out, g_xattn, w_xq, w_xkv, g_mem, g_xq, g_xk, w_xo, g_ffn, w_gate, w_up, w_down, loss_target, m_g_mix, m_w_in, m_b_forget, m_g_ret_out, m_g_fox_q, m_g_fox_k, m_w_out, m_g_xattn, m_w_xq, m_w_xkv, m_g_mem, m_g_xq, m_g_xk, m_w_xo, m_g_ffn, m_w_gate, m_w_up, m_w_down, v_g_mix, v_w_in, v_b_forget, v_g_ret_out, v_g_fox_q, v_g_fox_k, v_w_out, v_g_xattn, v_w_xq, v_w_xkv, v_g_mem, v_g_xq, v_g_xk, v_w_xo, v_g_ffn, v_w_gate, v_w_up, v_w_down):
    names = ("g_mix", "w_in", "b_forget", "g_ret_out", "g_fox_q", "g_fox_k", "w_out", "g_xattn", "w_xq", "w_xkv", "g_mem",
             "g_xq", "g_xk", "w_xo", "g_ffn", "w_gate", "w_up", "w_down")
    w = dict(zip(names, (g_mix, w_in, b_forget, g_ret_out, g_fox_q, g_fox_k, w_out, g_xattn, w_xq, w_xkv, g_mem, g_xq, g_xk,
                         w_xo, g_ffn, w_gate, w_up, w_down)))
    m = dict(zip(names, (m_g_mix, m_w_in, m_b_forget, m_g_ret_out, m_g_fox_q, m_g_fox_k, m_w_out, m_g_xattn, m_w_xq, m_w_xkv,
                         m_g_mem, m_g_xq, m_g_xk, m_w_xo, m_g_ffn, m_w_gate, m_w_up, m_w_down)))
    v = dict(zip(names, (v_g_mix, v_w_in, v_b_forget, v_g_ret_out, v_g_fox_q, v_g_fox_k, v_w_out, v_g_xattn, v_w_xq, v_w_xkv,
                         v_g_mem, v_g_xq, v_g_xk, v_w_xo, v_g_ffn, v_w_gate, v_w_up, v_w_down)))
    small_names = [s[0] for s in _SMALL]
    me = 4 * lax.axis_index("x") + 2 * lax.axis_index("y") + lax.axis_index("c")

    first = _all_gather(_pack_shards(w, GATHER_FIRST, BF))
    first, rest_shard = lax.optimization_barrier((first, _pack_shards(w, GATHER_REST, BF)))
    rest_started = _exchange_start("gather_rest_start", rest_shard,
                                   jnp.broadcast_to(rest_shard[None], (N_DEV,) + rest_shard.shape), scatter=False)

    def fetch_rest(after):
        return _unpack_gathered(_exchange_wait("gather_rest_wait", rest_started, after, scatter=False)[1], GATHER_REST)

    pushed = {}

    def push(group, grads):
        send = _pack_chunks(grads, GRAD_GROUPS[group])
        pushed[group] = _exchange_start("scatter_%s_start" % group, send, jnp.zeros(send.shape, BF), scatter=True)
        return pushed[group][4]

    sp = {n: w[n].reshape(1, -1) for n in small_names}
    loss_part, grad_x, g_last, gs = _local_step(x[0], mem[0], loss_target[0], sp, _unpack_gathered(first, GATHER_FIRST)["w_inT"],
                                                rest_started[4], fetch_rest, push)

    small = lax.dynamic_update_slice(_pack_small(gs), loss_part[:, :1], _LOSS_AT)
    recv_mix, recv_small = _all_to_all(_pack_chunks(g_last, GRAD_GROUPS["mix"]), small)

    results = {}
    for group in ("ffn", "xattn", "mix"):
        gnames = GRAD_GROUPS[group]
        wp, mp, vp = (_pack_shards(t, gnames, F32) for t in (w, m, v))
        if group == "mix":
            res = _adamw("adamw_mix", recv_mix, wp, mp, vp, 16)
        else:
            sent, recv = _exchange_wait("scatter_%s_wait" % group, pushed[group], recv_small, scatter=True)
            own = lax.dynamic_index_in_dim(sent, me, axis=0, keepdims=False)
            res = _adamw("adamw_%s" % group, recv, wp, mp, vp, {"ffn": 176, "xattn": 128}[group], own=own)
        results[group] = [_unpack_shards(r, gnames, w) for r in res]
    g_sm, d_sm, m_sm, v_sm = _adamw("adamw_small", recv_small, _pack_small(w), _pack_small(m), _pack_small(v), SMALL_ROWS)
    loss = g_sm[_LOSS_AT[0], _LOSS_AT[1]]

    outs = []
    for k, sm in enumerate((g_sm, d_sm, m_sm, v_sm)):
        tree = _unpack_small(sm, w)
        for group in results:
            tree.update(results[group][k])
        outs += [tree[n] for n in names]
    return (loss, grad_x[None], *outs)


SCATTER_GROUPS = {"ffn": ("w_gate", "w_up", "w_down"), "xattn": ("w_xq", "w_xo", "w_xkv"), "out": ("w_out",), "in": ("w_in",)}


def _adam_update(g, w, m, v):
    m2 = ADAM_B1 * m + (1.0 - ADAM_B1) * g
    v2 = ADAM_B2 * v + (1.0 - ADAM_B2) * jnp.square(g)
    m_hat = m2 / (1.0 - ADAM_B1 ** ADAM_STEP)
    v_hat = v2 / (1.0 - ADAM_B2 ** ADAM_STEP)
    return g, -ADAM_LR * (m_hat / (jnp.sqrt(v_hat) + ADAM_EPS) + ADAM_WD * w), m2, v2


def _adamw_shard(name, recv, own, off, w, m, v):
    rows, padded, transposed = W_SHARD[name.split(":")[1]]
    assert off % padded == 0
    blk = off // padded

    def total(s_ref, own_ref):
        g = own_ref[...].astype(F32)
        for s in range(N_DEV):
            g = g + s_ref[s].astype(F32)
        return g

    canonical_view = name.endswith(":")
    if transposed and rows == padded and not canonical_view:
        res = _adamw_shard(name + ":", recv, own, off, *(jnp.swapaxes(a, 1, 2) for a in (w, m, v)))
        return [jnp.swapaxes(r, 1, 2) for r in res]

    if canonical_view or not transposed:
        def body(s_ref, own_ref, w_ref, m_ref, v_ref, *outs):
            for o, r in zip(outs, _adam_update(total(s_ref, own_ref), w_ref[0], m_ref[0], v_ref[0])):
                o[0] = r

        full = pl.BlockSpec((1, rows, D), lambda i: (0, 0, 0))
        return pl.pallas_call(
            body, name=name.replace(":", "_"), grid=(1,),
            in_specs=[pl.BlockSpec((N_DEV, padded, D), lambda i: (0, blk, 0)), pl.BlockSpec((padded, D), lambda i: (blk, 0)),
                      full, full, full],
            out_specs=[full] * 4, out_shape=[jax.ShapeDtypeStruct((1, rows, D), F32)] * 4,
            compiler_params=_params(("arbitrary",)),
        )(recv, own, w, m, v)

    wide = -(-padded // LANES) * LANES

    def body(s_ref, own_ref, w_ref, m_ref, v_ref, *outs):
        g = total(s_ref, own_ref)
        if wide > padded:
            g = jnp.concatenate([g, jnp.zeros((wide - padded, LANES), F32)], axis=0)
        g = g.T[:, :rows]
        for o, r in zip(outs, _adam_update(g, w_ref[0], m_ref[0], v_ref[0])):
            o[0] = r

    cols = pl.BlockSpec((1, LANES, rows), lambda c: (0, c, 0))
    return pl.pallas_call(
        body, name=name.replace(":", "_"), grid=(D // LANES,),
        in_specs=[pl.BlockSpec((N_DEV, padded, LANES), lambda c: (0, blk, c)), pl.BlockSpec((padded, LANES), lambda c: (blk, c)),
                  cols, cols, cols],
        out_specs=[cols] * 4, out_shape=[jax.ShapeDtypeStruct((1, D, rows), F32)] * 4,
        compiler_params=_params(("arbitrary",)),
    )(recv, own, w, m, v)


def _gather_small(small):
    def body(small_ref, out_ref, send_sems, recv_sems, local_sem):
        x, y, c = _place()
        me = 4 * x + 2 * y + c
        mine = pltpu.make_async_copy(small_ref, out_ref.at[me], local_sem)
        mine.start()
        copies = []
        for r in range(1, N_DEV):
            px, py, pc = x ^ (r >> 2), y ^ ((r >> 1) & 1), c ^ (r & 1)
            copies.append(pltpu.make_async_remote_copy(
                src_ref=small_ref, dst_ref=out_ref.at[me], send_sem=send_sems.at[r - 1], recv_sem=recv_sems.at[r - 1],
                device_id=(px, py, pc), device_id_type=MESH))
        for cp in copies:
            cp.start()
        for cp in copies:
            cp.wait_recv()
        for cp in copies:
            cp.wait_send()
        mine.wait()

    return pl.pallas_call(
        body, name="gather_small",
        out_shape=jax.ShapeDtypeStruct((N_DEV,) + small.shape, small.dtype),
        in_specs=[pl.BlockSpec(memory_space=pl.ANY)], out_specs=pl.BlockSpec(memory_space=pl.ANY),
        scratch_shapes=[pltpu.SemaphoreType.DMA((N_DEV - 1,)), pltpu.SemaphoreType.DMA((N_DEV - 1,)), pltpu.SemaphoreType.DMA],
    )(small)


def _pack_chunks(grads, names):
    chunks = []
    for name in names:
        rows, padded, _ = W_SHARD[name]
        g = grads[_CANON.get(name, name)].reshape(N_DEV, rows, D)
        chunks.append(jnp.pad(g, ((0, 0), (0, padded - rows), (0, 0))).astype(BF))
    return chunks[0] if len(chunks) == 1 else jnp.concatenate(chunks, axis=1)


def kernel(x, mem, g_mix, w_in, b_forget, g_ret_out, g_fox_q, g_fox_k, w_out, g_xattn, w_xq, w_xkv, g_mem, g_xq, g_xk, w_xo, g_ffn, w_gate, w_up, w_down, loss_target, m_g_mix, m_w_in, m_b_forget, m_g_ret_out, m_g_fox_q, m_g_fox_k, m_w_out, m_g_xattn, m_w_xq, m_w_xkv, m_g_mem, m_g_xq, m_g_xk, m_w_xo, m_g_ffn, m_w_gate, m_w_up, m_w_down, v_g_mix, v_w_in, v_b_forget, v_g_ret_out, v_g_fox_q, v_g_fox_k, v_w_out, v_g_xattn, v_w_xq, v_w_xkv, v_g_mem, v_g_xq, v_g_xk, v_w_xo, v_g_ffn, v_w_gate, v_w_up, v_w_down):
    names = ("g_mix", "w_in", "b_forget", "g_ret_out", "g_fox_q", "g_fox_k", "w_out", "g_xattn", "w_xq", "w_xkv", "g_mem",
             "g_xq", "g_xk", "w_xo", "g_ffn", "w_gate", "w_up", "w_down")
    w = dict(zip(names, (g_mix, w_in, b_forget, g_ret_out, g_fox_q, g_fox_k, w_out, g_xattn, w_xq, w_xkv, g_mem, g_xq, g_xk,
                         w_xo, g_ffn, w_gate, w_up, w_down)))
    m = dict(zip(names, (m_g_mix, m_w_in, m_b_forget, m_g_ret_out, m_g_fox_q, m_g_fox_k, m_w_out, m_g_xattn, m_w_xq, m_w_xkv,
                         m_g_mem, m_g_xq, m_g_xk, m_w_xo, m_g_ffn, m_w_gate, m_w_up, m_w_down)))
    v = dict(zip(names, (v_g_mix, v_w_in, v_b_forget, v_g_ret_out, v_g_fox_q, v_g_fox_k, v_w_out, v_g_xattn, v_w_xq, v_w_xkv,
                         v_g_mem, v_g_xq, v_g_xk, v_w_xo, v_g_ffn, v_w_gate, v_w_up, v_w_down)))
    small_names = [s[0] for s in _SMALL]
    me = 4 * lax.axis_index("x") + 2 * lax.axis_index("y") + lax.axis_index("c")

    first = _all_gather(_pack_shards(w, GATHER_FIRST, BF))
    first, rest_shard = lax.optimization_barrier((first, _pack_shards(w, GATHER_REST, BF)))
    rest_started = _exchange_start("gather_rest_start", rest_shard,
                                   jnp.broadcast_to(rest_shard[None], (N_DEV,) + rest_shard.shape), scatter=False)

    def fetch_rest(after):
        return _unpack_gathered(_exchange_wait("gather_rest_wait", rest_started, after, scatter=False)[1], GATHER_REST)

    pushed = {}

    def push(group, grads):
        send = _pack_chunks(grads, SCATTER_GROUPS[group])
        pushed[group] = _exchange_start("scatter_%s_start" % group, send, jnp.zeros(send.shape, BF), scatter=True)
        return pushed[group][4]

    sp = {n: w[n].reshape(1, -1) for n in small_names}
    loss_part, grad_x, gs = _local_step(x[0], mem[0], loss_target[0], sp, _unpack_gathered(first, GATHER_FIRST)["w_inT"],
                                        rest_started[4], fetch_rest, push)

    small = lax.dynamic_update_slice(_pack_small(gs), loss_part[:, :1], _LOSS_AT)
    recv_small = _gather_small(small)
    g_sm, d_sm, m_sm, v_sm = _adamw("adamw_small", recv_small, _pack_small(w), _pack_small(m), _pack_small(v), SMALL_ROWS)
    loss = g_sm[_LOSS_AT[0], _LOSS_AT[1]]

    results, after = {}, recv_small
    for group in ("ffn", "xattn", "out", "in"):
        sent, recv = _exchange_wait("scatter_%s_wait" % group, pushed[group], after, scatter=True)
        own = lax.dynamic_index_in_dim(sent, me, axis=0, keepdims=False)
        off = 0
        for name in SCATTER_GROUPS[group]:
            results[name] = _adamw_shard("adamw:" + name, recv, own, off, w[name], m[name], v[name])
            off += W_SHARD[name][1]
        after = results[SCATTER_GROUPS[group][-1]][0]

    outs = []
    for k, sm in enumerate((g_sm, d_sm, m_sm, v_sm)):
        tree = _unpack_small(sm, w)
        tree.update({name: res[k] for name, res in results.items()})
        outs += [tree[n] for n in names]
    return (loss, grad_x[None], *outs)
```

```python
import functools
import math

import jax
import jax.numpy as jnp
import numpy as np
from jax import lax
from jax.experimental import pallas as pl
from jax.experimental.pallas import tpu as pltpu

F32 = jnp.float32
BF = jnp.bfloat16

D = 1024
HEAD = 64
CHUNK = 64
N_MEM = 256
XHEAD = 256
D_FF = 2816
EPS = 1e-6
NEG = -1e30
LANES = 128
N_DEV = 8
V7X_VMEM_BYTES = 64 * 1024 * 1024
VMEM_LIMIT = V7X_VMEM_BYTES - 8 * 1024 * 1024

ADAM_LR, ADAM_B1, ADAM_B2, ADAM_EPS, ADAM_WD, ADAM_STEP = 0.001, 0.9, 0.999, 1e-08, 0.01, 10

W_LAYOUT = (("w_in", 449, 464, True), ("w_out", 128, 128, False), ("w_xq", 128, 128, False), ("w_xkv", 256, 256, True),
            ("w_xo", 128, 128, False), ("w_gate", 352, 352, True), ("w_up", 352, 352, True), ("w_down", 352, 352, False))
W_ROWS = sum(w[2] for w in W_LAYOUT)
W_OFF = {}
_o = 0
for _n, _r, _p, _t in W_LAYOUT:
    W_OFF[_n] = _o
    _o += _p
SMALL_ROWS = 8
W_SHARD = {"w_in": (449, 449, True), "w_out": (128, 128, False), "w_xq": (128, 128, False), "w_xkv": (256, 256, True),
           "w_xo": (128, 128, False), "w_gate": (352, 352, True), "w_up": (352, 352, True), "w_down": (352, 352, False)}
GATHER_FIRST = ("w_in",)
GATHER_REST = ("w_out", "w_xq", "w_xkv", "w_xo", "w_gate", "w_up", "w_down")
GRAD_GROUPS = {"ffn": ("w_gate", "w_up", "w_down"), "xattn": ("w_xq", "w_xkv", "w_xo"), "mix": ("w_in", "w_out")}

NT = (((1,), (1,)), ((), ()))
NN = (((1,), (0,)), ((), ()))
TN = (((0,), (0,)), ((), ()))
_DIMS = {"nn": NN, "nt": NT, "tn": TN}


def _params(sem):
    return pltpu.CompilerParams(dimension_semantics=sem, vmem_limit_bytes=VMEM_LIMIT)


def _mm(name, products, extras, epilogue, M, N, tm, tn, out_dtypes, params=(), n_acc=0):
    assert n_acc == 0 or tn == N
    flat = [t for p in products for t in p]
    counts = [len(p) for p in products]
    in_specs, args = [], []
    for a, b, form in flat:
        if form == "tn":
            in_specs.append(pl.BlockSpec((a.shape[0], tm), lambda i, j: (0, i)))
        else:
            in_specs.append(pl.BlockSpec((tm, a.shape[1]), lambda i, j: (i, 0)))
        if form == "nt":
            in_specs.append(pl.BlockSpec((tn, b.shape[1]), lambda i, j: (j, 0)))
        else:
            in_specs.append(pl.BlockSpec((b.shape[0], tn), lambda i, j: (0, j)))
        args += [a, b]
    for e in extras:
        in_specs.append(pl.BlockSpec((tm, tn), lambda i, j: (i, j)))
        args.append(e)
    for p in params:
        in_specs.append(pl.BlockSpec((1, tn), lambda i, j: (0, j)))
        args.append(p)
    n_in = len(args)
    n_out = len(out_dtypes)

    def body(*refs):
        ins, outs = refs[:n_in], refs[n_in:]
        prods, p = [], 0
        for c in counts:
            acc = None
            for _ in range(c):
                a = ins[2 * p][...].astype(BF)
                b = ins[2 * p + 1][...].astype(BF)
                d = lax.dot_general(a, b, _DIMS[flat[p][2]], preferred_element_type=F32)
                acc = d if acc is None else acc + d
                p += 1
            prods.append(acc)
        ex = [r[...].astype(F32) for r in ins[2 * len(flat):]]
        res = epilogue(*prods, *ex)
        for o, r in zip(outs[:n_out], res[:n_out]):
            o[...] = r.astype(o.dtype)
        for o, r in zip(outs[n_out:], res[n_out:]):
            @pl.when(pl.program_id(0) == 0)
            def _(o=o):
                o[...] = jnp.zeros(o.shape, F32)
            o[...] += r

    return pl.pallas_call(
        body, name=name, grid=(M // tm, N // tn), in_specs=in_specs,
        out_specs=[pl.BlockSpec((tm, tn), lambda i, j: (i, j)) for _ in out_dtypes]
        + [pl.BlockSpec((1, tn), lambda i, j: (0, j)) for _ in range(n_acc)],
        out_shape=[jax.ShapeDtypeStruct((M, N), dt) for dt in out_dtypes] + [jax.ShapeDtypeStruct((1, N), F32)] * n_acc,
        compiler_params=_params(("arbitrary", "arbitrary")),
    )(*args)


def _ident(x):
    return (x,)


def _add(x, r):
    return (x + r,)


def _spec(rows, w, off, per_j):
    if per_j:
        return pl.BlockSpec((rows, w), lambda j, i: (i, off + j))
    return pl.BlockSpec((rows, w), lambda j, i: (i, off))


def _pspec(rows, w, off, per_j):
    if per_j:
        return pl.BlockSpec((rows, w), lambda j, i: (0, off + j))
    return pl.BlockSpec((rows, w), lambda j, i: (0, off))


def _rw_fwd(name, fn, rows, params, outs, T, tm, nj, n_acc=0):
    in_specs = [_spec(tm, w, off, pj) for _, w, off, pj in rows] + [_pspec(a.shape[0], w, off, pj) for a, w, off, pj in params]
    args = [r[0] for r in rows] + [p[0] for p in params]
    n_in, n_out = len(args), len(outs)
    out_specs = [pl.BlockSpec((tm, w), lambda j, i: (i, j)) for _, w in outs]
    out_shape = [jax.ShapeDtypeStruct((T, nj * w), dt) for dt, w in outs]
    out_specs += [pl.BlockSpec((1, LANES), lambda j, i: (0, 0)) for _ in range(n_acc)]
    out_shape += [jax.ShapeDtypeStruct((1, LANES), F32) for _ in range(n_acc)]

    def body(*refs):
        vals = [r[...].astype(F32) for r in refs[:n_in]]
        res = fn(*vals)
        orefs = refs[n_in:]
        for k in range(n_out):
            orefs[k][...] = res[k].astype(orefs[k].dtype)
        first = (pl.program_id(0) == 0) & (pl.program_id(1) == 0)
        for k in range(n_acc):
            @pl.when(first)
            def _(k=k):
                orefs[n_out + k][...] = jnp.zeros((1, LANES), F32)
            orefs[n_out + k][...] += res[n_out + k]

    return pl.pallas_call(
        body, name=name, grid=(nj, T // tm), in_specs=in_specs, out_specs=out_specs, out_shape=out_shape,
        compiler_params=_params(("arbitrary", "arbitrary")),
    )(*args)


def _rw_bwd(name, fn, rows, params, cots, T, tm, nj, row_grads, param_grads, resid=None):
    in_specs = ([_spec(tm, w, off, pj) for _, w, off, pj in rows] + [_pspec(a.shape[0], w, off, pj) for a, w, off, pj in params]
                + [_spec(tm, w, off, pj) for _, w, off, pj in cots])
    args = [r[0] for r in rows] + [p[0] for p in params] + [c[0] for c in cots]
    if resid is not None:
        in_specs.append(_spec(tm, rows[0][1], rows[0][2], rows[0][3]))
        args.append(resid)
    nr, npar, nc = len(rows), len(params), len(cots)
    out_specs, out_shape, kinds = [], [], []
    for k, dts in enumerate(row_grads):
        for dt in (dts if isinstance(dts, (list, tuple)) else [dts]):
            if dt is not None:
                w = rows[k][1]
                out_specs.append(pl.BlockSpec((tm, w), lambda j, i: (i, j)))
                out_shape.append(jax.ShapeDtypeStruct((T, nj * w), dt))
                kinds.append(("row", k))
    for k, need in enumerate(param_grads):
        if need:
            a, w, off, pj = params[k]
            out_specs.append(_pspec(a.shape[0], w, off, pj))
            out_shape.append(jax.ShapeDtypeStruct(a.shape, F32))
            kinds.append(("par", k))

    def body(*refs):
        vals = [r[...].astype(F32) for r in refs[:nr + npar]]
        ct = tuple(r[...].astype(F32) for r in refs[nr + npar:nr + npar + nc])
        _, vjp = jax.vjp(lambda *a: tuple(fn(*a)), *vals)
        grads = list(vjp(ct))
        n_in = nr + npar + nc + (resid is not None)
        if resid is not None:
            grads[0] = grads[0] + refs[n_in - 1][...].astype(F32)
        orefs = refs[n_in:]
        j, i = pl.program_id(0), pl.program_id(1)
        for o, (kind, k) in zip(orefs, kinds):
            if kind == "row":
                o[...] = grads[k].astype(o.dtype)
            else:
                first = (i == 0) if params[k][3] else ((i == 0) & (j == 0))

                @pl.when(first)
                def _(o=o):
                    o[...] = jnp.zeros(o.shape, F32)
                o[...] += grads[nr + k]

    return pl.pallas_call(
        body, name=name, grid=(nj, T // tm), in_specs=in_specs, out_specs=out_specs, out_shape=out_shape,
        compiler_params=_params(("arbitrary", "arbitrary")),
    )(*args)


def _rms(x, g):
    return x * lax.rsqrt(jnp.mean(x * x, axis=-1, keepdims=True) + EPS) * g


def _rms_fn(x, g):
    return (_rms(x, g),)


def _lo_mask():
    return lax.broadcasted_iota(jnp.int32, (1, LANES), 1) < HEAD


def _gmean(x, lo):
    s0 = jnp.sum(jnp.where(lo, x, 0.0), axis=-1, keepdims=True)
    s1 = jnp.sum(jnp.where(lo, 0.0, x), axis=-1, keepdims=True)
    return jnp.where(lo, s0, s1) * (1.0 / HEAD)


def _fox_prep_fn(fq, fk, gq, gk):
    lo = _lo_mask()
    qn = fq * lax.rsqrt(_gmean(fq * fq, lo) + EPS) * gq * (HEAD ** -0.5)
    kn = fk * lax.rsqrt(_gmean(fk * fk, lo) + EPS) * gk
    return qn, kn


def _cast_fn(v):
    return (v,)


@jax.custom_vjp
def _swap_halves(x):
    bit = (lax.broadcasted_iota(jnp.int32, (1, LANES), 1) & (HEAD // 2)) == 0
    return jnp.where(bit, pltpu.roll(x, LANES - HEAD // 2, 1), pltpu.roll(x, HEAD // 2, 1))


_swap_halves.defvjp(lambda x: (_swap_halves(x), None), lambda _, g: (_swap_halves(g),))


def _ret_fn(rq, rk, rv, rg, cos, sin, s_in, g, lg):
    tb = rq.shape[0]
    nc = tb // CHUNK
    lo = _lo_mask()
    row = lax.broadcasted_iota(jnp.int32, (LANES, 1), 0) < HEAD
    same_head = row == lo
    q = (rq * cos + _swap_halves(rq) * sin) * (HEAD ** -0.5)
    k = rk * cos + _swap_halves(rk) * sin
    q3, k3, v3 = q.reshape(nc, CHUNK, LANES), k.reshape(nc, CHUNK, LANES), rv.reshape(nc, CHUNK, LANES)
    pos = lax.broadcasted_iota(jnp.int32, (CHUNK, 1), 0).astype(F32)
    q_decay = jnp.exp(lg * (pos + 1.0))
    k_decay = jnp.exp(lg * (CHUNK - 1.0 - pos))
    chunk_decay = jnp.exp(lg * float(CHUNK))
    dist = jnp.abs(lax.broadcasted_iota(jnp.int32, (CHUNK, CHUNK), 0) - lax.broadcasted_iota(jnp.int32, (CHUNK, CHUNK), 1)).astype(F32)
    v3b = v3.astype(BF)
    intra = []
    for hh in range(2):
        hm = lo if hh == 0 else ~lo
        lg_h = lg[:, hh * HEAD:hh * HEAD + 1]
        qm = jnp.where(hm, q3, 0.0).astype(BF)
        sc = jnp.einsum("nid,njd->nij", qm, k3.astype(BF), preferred_element_type=F32) * jnp.exp(lg_h * dist)[None]
        intra.append(jnp.einsum("nij,nje->nie", sc.astype(BF), v3b, preferred_element_type=F32))
    o = jnp.where(lo, intra[0], intra[1])
    kv = jnp.einsum("njd,nje->nde", (k3 * k_decay[None]).astype(BF), v3b, preferred_element_type=F32)
    kv = jnp.where(same_head[None], kv, 0.0)
    state, states = s_in, []
    for n in range(nc):
        states.append(state)
        state = state * chunk_decay + kv[n]
    s_prev = jnp.stack(states, axis=0)
    o = o + jnp.einsum("nid,nde->nie", (q3 * q_decay[None]).astype(BF), s_prev.astype(BF), preferred_element_type=F32)
    o = o.reshape(tb, LANES)
    mu = _gmean(o, lo)
    oc = o - mu
    y = oc * lax.rsqrt(_gmean(oc * oc, lo) + EPS) * g
    return jax.nn.silu(rg) * y, state


def _xattn_fn(qx, gq, gk, kk, vv):
    q = _rms(qx, gq)
    k = _rms(kk, gk)
    logits = lax.dot_general(q.astype(BF), k.astype(BF), NT, preferred_element_type=F32) * (XHEAD ** -0.5)
    p = jax.nn.softmax(logits, axis=-1)
    return (jnp.dot(p.astype(BF), vv.astype(BF), preferred_element_type=F32),)


def _swiglu_fwd_epi(g, u):
    return g, u, jax.nn.silu(g) * u


def _swiglu_bwd_epi(dact, g, u):
    _, vjp = jax.vjp(lambda a, b: jax.nn.silu(a) * b, g, u)
    return vjp(dact)


def _add_rms_epi(acc, resid, g):
    h = acc + resid
    return h, _rms(h, g)


def _add_loss_epi(acc, resid, target):
    err = (acc + resid) - target
    dy = err * (1.0 / D)
    part = jnp.sum(jnp.sum(err * err, axis=0, keepdims=True), axis=1, keepdims=True) * (0.5 / D)
    return dy, dy, jnp.broadcast_to(part, (1, err.shape[1]))


def _rms_bwd_epi(dhn, h, skip, g):
    _, vjp = jax.vjp(_rms, h, g)
    dh, dg = vjp(dhn)
    dh = dh + skip
    return dh, dh, dg


def _loss_fn(h, target):
    err = h - target
    part = jnp.sum(jnp.sum(err * err, axis=0, keepdims=True), axis=-1, keepdims=True) * (0.5 / D)
    dy = err * (1.0 / D)
    return dy, dy, part


def _ret_fwd(P, cos, sin, g_ret, lg, T, tb):
    nb = T // tb

    def body(rq, rk, rv, rg, c, s, g, l, o_ref, s0_ref, state):
        @pl.when(pl.program_id(1) == 0)
        def _():
            state[...] = jnp.zeros(state.shape, F32)
        s0_ref[0, 0] = state[...]
        out, s_new = _ret_fn(rq[...], rk[...], rv[...], rg[...], c[...], s[...], state[...], g[...], l[...])
        o_ref[...] = out
        state[...] = s_new

    sec = lambda off: pl.BlockSpec((tb, LANES), lambda j, i: (i, off + j))
    tab = pl.BlockSpec((tb, LANES), lambda j, i: (i, 0))
    par = pl.BlockSpec((1, LANES), lambda j, i: (0, j))
    return pl.pallas_call(
        body, name="ret_fwd", grid=(4, nb),
        in_specs=[sec(0), sec(4), sec(8), sec(12), tab, tab, par, par],
        out_specs=[pl.BlockSpec((tb, LANES), lambda j, i: (i, j)), pl.BlockSpec((1, 1, LANES, LANES), lambda j, i: (j, i, 0, 0))],
        out_shape=[jax.ShapeDtypeStruct((T, 4 * LANES), F32), jax.ShapeDtypeStruct((4, nb, LANES, LANES), F32)],
        scratch_shapes=[pltpu.VMEM((LANES, LANES), F32)],
        compiler_params=_params(("arbitrary", "arbitrary")),
    )(P, P, P, P, cos, sin, g_ret, lg)


def _ret_bwd(P, cos, sin, g_ret, lg, s0, dmix, T, tb):
    nb = T // tb

    def body(rq, rk, rv, rg, c, s, g, l, s0_ref, do, drq, drk, drv, drg, dg, dstate):
        i = pl.program_id(1)

        @pl.when(i == 0)
        def _():
            dstate[...] = jnp.zeros(dstate.shape, F32)
            dg[...] = jnp.zeros(dg.shape, F32)

        cc, ss, ll = c[...], s[...], l[...]
        _, vjp = jax.vjp(lambda a, b, v, gate, st, gg: _ret_fn(a, b, v, gate, cc, ss, st, gg, ll),
                         rq[...], rk[...], rv[...], rg[...], s0_ref[0, 0], g[...])
        ga, gb, gv, ggate, gst, ggain = vjp((do[...], dstate[...]))
        drq[...] = ga.astype(drq.dtype)
        drk[...] = gb.astype(drk.dtype)
        drv[...] = gv.astype(drv.dtype)
        drg[...] = ggate.astype(drg.dtype)
        dstate[...] = gst
        dg[...] += ggain

    rev = lambda i: nb - 1 - i
    sec = lambda off: pl.BlockSpec((tb, LANES), lambda j, i: (rev(i), off + j))
    tab = pl.BlockSpec((tb, LANES), lambda j, i: (rev(i), 0))
    par = pl.BlockSpec((1, LANES), lambda j, i: (0, j))
    outb = pl.BlockSpec((tb, LANES), lambda j, i: (rev(i), j))
    return pl.pallas_call(
        body, name="ret_bwd", grid=(4, nb),
        in_specs=[sec(0), sec(4), sec(8), sec(12), tab, tab, par, par,
                  pl.BlockSpec((1, 1, LANES, LANES), lambda j, i: (j, rev(i), 0, 0)), outb],
        out_specs=[outb, outb, outb, outb, par],
        out_shape=[jax.ShapeDtypeStruct((T, 4 * LANES), BF)] * 4 + [jax.ShapeDtypeStruct((1, 4 * LANES), F32)],
        scratch_shapes=[pltpu.VMEM((LANES, LANES), F32)],
        compiler_params=_params(("arbitrary", "arbitrary")),
    )(P, P, P, P, cos, sin, g_ret, lg, s0, dmix)


_FB = 128


def _tri(lower):
    r = lax.broadcasted_iota(jnp.int32, (_FB, _FB), 0)
    c = lax.broadcasted_iota(jnp.int32, (_FB, _FB), 1)
    return ((r >= c) if lower else (r <= c)).astype(F32)


def _fgate_fwd(ffp, bpad, T):
    def body(ff_ref, b_ref, fc_ref, fr_ref):
        lane = lax.broadcasted_iota(jnp.int32, (1, LANES), 1)
        tri = _tri(True)
        carry = jnp.zeros((1, LANES), F32)
        for blk in range(T // _FB):
            z = ff_ref[blk * _FB:(blk + 1) * _FB, :] + b_ref[...]
            lf = jnp.where(lane < 8, jax.nn.log_sigmoid(z), 0.0)
            f = jnp.dot(tri, lf, precision=lax.Precision.HIGHEST, preferred_element_type=F32) + carry
            carry = f[_FB - 1:_FB, :]
            fc_ref[blk * _FB:(blk + 1) * _FB, :] = f
            fr_ref[:, blk * _FB:(blk + 1) * _FB] = f.T[:8, :]

    return pl.pallas_call(
        body, name="fgate_fwd",
        out_shape=[jax.ShapeDtypeStruct((T, LANES), F32), jax.ShapeDtypeStruct((8, T), F32)],
        compiler_params=pltpu.CompilerParams(vmem_limit_bytes=VMEM_LIMIT),
    )(ffp, bpad)


def _fgate_bwd(ffp, bpad, dfr, T):
    def body(ff_ref, b_ref, dfr_ref, dff_ref, db_ref):
        lane = lax.broadcasted_iota(jnp.int32, (1, LANES), 1)
        tri = _tri(False)
        carry = jnp.zeros((1, LANES), F32)
        db = jnp.zeros((1, LANES), F32)
        for blk in reversed(range(T // _FB)):
            d8 = dfr_ref[:, blk * _FB:(blk + 1) * _FB]
            dcol = jnp.concatenate([d8, jnp.zeros((_FB - 8, _FB), F32)], axis=0).T
            dlf = jnp.dot(tri, dcol, precision=lax.Precision.HIGHEST, preferred_element_type=F32) + carry
            carry = dlf[0:1, :]
            z = ff_ref[blk * _FB:(blk + 1) * _FB, :] + b_ref[...]
            dz = jnp.where(lane < 8, dlf * jax.nn.sigmoid(-z), 0.0)
            dff_ref[blk * _FB:(blk + 1) * _FB, :] = dz.astype(dff_ref.dtype)
            db = db + jnp.sum(dz, axis=0, keepdims=True)
        db_ref[...] = db

    return pl.pallas_call(
        body, name="fgate_bwd",
        out_shape=[jax.ShapeDtypeStruct((T, LANES), BF), jax.ShapeDtypeStruct((1, LANES), F32)],
        compiler_params=pltpu.CompilerParams(vmem_limit_bytes=VMEM_LIMIT),
    )(ffp, bpad, dfr)


def _head_bias_col(fc, head):
    lane = lax.broadcasted_iota(jnp.int32, (1, LANES), 1)
    return jnp.sum(jnp.where(lane == head, fc, 0.0), axis=-1, keepdims=True)


def _head_bias_row(fr, head):
    sub = lax.broadcasted_iota(jnp.int32, (8, 1), 0)
    return jnp.sum(jnp.where(sub == head, fr, 0.0), axis=0, keepdims=True)


def _fox_fwd(qn, kn, vb, fc, fr, T, tq):
    nq = T // tq

    def body(q_ref, k_ref, v_ref, fc_ref, fr_ref, o_ref, c_ref):
        j, i = pl.program_id(0), pl.program_id(1)
        lane = lax.broadcasted_iota(jnp.int32, (1, LANES), 1)
        lo = lane < HEAD
        causal = lax.broadcasted_iota(jnp.int32, (tq, tq), 0) >= lax.broadcasted_iota(jnp.int32, (tq, tq), 1)
        q = q_ref[...]
        fcb = fc_ref[...]
        outs, cs = [], []
        for hh in range(2):
            hm = lo if hh == 0 else ~lo
            head = 2 * j + hh
            qh = jnp.where(hm, q, jnp.zeros_like(q))
            fq = _head_bias_col(fcb, head)

            def block(kb, carry, diag, qh=qh, fq=fq, head=head):
                m, l, acc = carry
                k0 = pl.multiple_of(kb * tq, tq)
                k = k_ref[pl.ds(k0, tq), :]
                v = v_ref[pl.ds(k0, tq), :]
                fk = _head_bias_row(fr_ref[:, pl.ds(k0, tq)], head)
                s = (lax.dot_general(qh, k, NT, preferred_element_type=F32) + fq) - fk
                if diag:
                    s = jnp.where(causal, s, NEG)
                m2 = jnp.maximum(m, jnp.max(s, axis=-1, keepdims=True))
                p = jnp.exp(s - m2)
                a = jnp.exp(m - m2)
                return m2, a * l + jnp.sum(p, axis=-1, keepdims=True), a * acc + jnp.dot(p.astype(BF), v, preferred_element_type=F32)

            init = (jnp.full((tq, 1), NEG, F32), jnp.zeros((tq, 1), F32), jnp.zeros((tq, LANES), F32))
            carry = lax.fori_loop(0, i, lambda kb, c: block(kb, c, False), init)
            m, l, acc = block(i, carry, True)
            outs.append(acc / l)
            cs.append(fq - (m + jnp.log(l)))
        o_ref[...] = jnp.where(lo, outs[0], outs[1])
        c_ref[0] = jnp.where(lane == 0, cs[0], jnp.where(lane == 1, cs[1], 0.0))

    full = lambda: pl.BlockSpec((T, LANES), lambda j, i: (0, j))
    return pl.pallas_call(
        body, name="fox_fwd", grid=(4, nq),
        in_specs=[pl.BlockSpec((tq, LANES), lambda j, i: (i, j)), full(), full(),
                  pl.BlockSpec((tq, LANES), lambda j, i: (i, 0)), pl.BlockSpec((8, T), lambda j, i: (0, 0))],
        out_specs=[pl.BlockSpec((tq, LANES), lambda j, i: (i, j)), pl.BlockSpec((1, tq, LANES), lambda j, i: (j, i, 0))],
        out_shape=[jax.ShapeDtypeStruct((T, 4 * LANES), F32), jax.ShapeDtypeStruct((4, T, LANES), F32)],
        compiler_params=_params(("parallel", "arbitrary")),
    )(qn, kn, vb, fc, fr)


def _fox_bwd_dq(qn, kn, vb, fr, cq, dmix, T, tq):
    nq = T // tq

    def body(q_ref, k_ref, v_ref, fr_ref, c_ref, do_ref, dq_ref, dl_ref, p_scr, dp_scr):
        j, i = pl.program_id(0), pl.program_id(1)
        lane = lax.broadcasted_iota(jnp.int32, (1, LANES), 1)
        lo = lane < HEAD
        causal = lax.broadcasted_iota(jnp.int32, (tq, tq), 0) >= lax.broadcasted_iota(jnp.int32, (tq, tq), 1)
        q, do, cb = q_ref[...], do_ref[...], c_ref[0]
        res, deltas = [], []
        for hh in range(2):
            hm = lo if hh == 0 else ~lo
            head = 2 * j + hh
            qh = jnp.where(hm, q, jnp.zeros_like(q))
            doh = jnp.where(hm, do, 0.0).astype(BF)
            c = cb[:, hh:hh + 1]

            def probs(kb, delta, diag, qh=qh, doh=doh, c=c, head=head):
                k0 = pl.multiple_of(kb * tq, tq)
                k = k_ref[pl.ds(k0, tq), :]
                v = v_ref[pl.ds(k0, tq), :]
                fk = _head_bias_row(fr_ref[:, pl.ds(k0, tq)], head)
                p = jnp.exp((lax.dot_general(qh, k, NT, preferred_element_type=F32) + c) - fk)
                if diag:
                    p = jnp.where(causal, p, 0.0)
                dp = lax.dot_general(doh, v, NT, preferred_element_type=F32)
                p_scr[:, pl.ds(k0, tq)] = p
                dp_scr[:, pl.ds(k0, tq)] = dp
                return delta + jnp.sum(p * dp, axis=-1, keepdims=True)

            delta = lax.fori_loop(0, i, lambda kb, d: probs(kb, d, False), jnp.zeros((tq, 1), F32))
            delta = probs(i, delta, True)

            def grad(kb, acc, delta=delta):
                k0 = pl.multiple_of(kb * tq, tq)
                ds = p_scr[:, pl.ds(k0, tq)] * (dp_scr[:, pl.ds(k0, tq)] - delta)
                return acc + jnp.dot(ds.astype(BF), k_ref[pl.ds(k0, tq), :], preferred_element_type=F32)

            res.append(lax.fori_loop(0, i + 1, grad, jnp.zeros((tq, LANES), F32)))
            deltas.append(delta)
        dq_ref[...] = jnp.where(lo, res[0], res[1])
        dl_ref[0] = jnp.where(lane == 0, deltas[0], jnp.where(lane == 1, deltas[1], 0.0))

    full = lambda: pl.BlockSpec((T, LANES), lambda j, i: (0, j))
    return pl.pallas_call(
        body, name="fox_bwd_dq", grid=(4, nq),
        in_specs=[pl.BlockSpec((tq, LANES), lambda j, i: (i, j)), full(), full(), pl.BlockSpec((8, T), lambda j, i: (0, 0)),
                  pl.BlockSpec((1, tq, LANES), lambda j, i: (j, i, 0)), pl.BlockSpec((tq, LANES), lambda j, i: (i, 4 + j))],
        out_specs=[pl.BlockSpec((tq, LANES), lambda j, i: (i, j)), pl.BlockSpec((1, tq, LANES), lambda j, i: (j, i, 0))],
        out_shape=[jax.ShapeDtypeStruct((T, 4 * LANES), F32), jax.ShapeDtypeStruct((4, T, LANES), F32)],
        scratch_shapes=[pltpu.VMEM((tq, T), F32), pltpu.VMEM((tq, T), F32)],
        compiler_params=_params(("parallel", "arbitrary")),
    )(qn, kn, vb, fr, cq, dmix)


def _fox_bwd_dkv(qn, kn, vb, fr, cq, dl, dmix, T, tq):
    nq = T // tq

    def body(q_ref, k_ref, v_ref, fr_ref, c_ref, dl_ref, do_ref, dk_ref, dv_ref, dfr_ref):
        j, kb = pl.program_id(0), pl.program_id(1)
        lo = _lo_mask()
        sub = lax.broadcasted_iota(jnp.int32, (8, 1), 0)
        causal = lax.broadcasted_iota(jnp.int32, (tq, tq), 0) >= lax.broadcasted_iota(jnp.int32, (tq, tq), 1)
        k, v, frb = k_ref[...], v_ref[...], fr_ref[...]
        dks, dvs, dfs = [], [], []
        for hh in range(2):
            hm = lo if hh == 0 else ~lo
            head = 2 * j + hh
            km = jnp.where(hm, k, jnp.zeros_like(k))
            vm = jnp.where(hm, v, jnp.zeros_like(v))
            fk = _head_bias_row(frb, head)

            def block(qi, carry, diag, km=km, vm=vm, fk=fk, hm=hm, hh=hh):
                dk, dv, df = carry
                q0 = pl.multiple_of(qi * tq, tq)
                q = q_ref[pl.ds(q0, tq), :]
                c = c_ref[0, pl.ds(q0, tq), :][:, hh:hh + 1]
                delta = dl_ref[0, pl.ds(q0, tq), :][:, hh:hh + 1]
                dob = do_ref[pl.ds(q0, tq), :].astype(BF)
                p = jnp.exp((lax.dot_general(q, km, NT, preferred_element_type=F32) + c) - fk)
                if diag:
                    p = jnp.where(causal, p, 0.0)
                dv = dv + lax.dot_general(p.astype(BF), dob, TN, preferred_element_type=F32)
                dp = lax.dot_general(dob, vm, NT, preferred_element_type=F32)
                ds = p * (dp - delta)
                dk = dk + lax.dot_general(ds.astype(BF), q, TN, preferred_element_type=F32)
                return dk, dv, df - jnp.sum(ds, axis=0, keepdims=True)

            init = (jnp.zeros((tq, LANES), F32), jnp.zeros((tq, LANES), F32), jnp.zeros((1, tq), F32))
            carry = block(kb, init, True)
            dk, dv, df = lax.fori_loop(kb + 1, nq, lambda qi, cr: block(qi, cr, False), carry)
            dks.append(dk)
            dvs.append(dv)
            dfs.append(df)
        dk_ref[...] = jnp.where(lo, dks[0], dks[1])
        dv_ref[...] = jnp.where(lo, dvs[0], dvs[1]).astype(dv_ref.dtype)
        dfr_ref[0] = jnp.where(sub == 0, dfs[0], jnp.where(sub == 1, dfs[1], 0.0))

    full = lambda off: pl.BlockSpec((T, LANES), lambda j, kb: (0, off + j))
    blk = lambda: pl.BlockSpec((tq, LANES), lambda j, kb: (kb, j))
    return pl.pallas_call(
        body, name="fox_bwd_dkv", grid=(4, nq),
        in_specs=[full(0), blk(), blk(), pl.BlockSpec((8, tq), lambda j, kb: (0, kb)),
                  pl.BlockSpec((1, T, LANES), lambda j, kb: (j, 0, 0)), pl.BlockSpec((1, T, LANES), lambda j, kb: (j, 0, 0)), full(4)],
        out_specs=[blk(), blk(), pl.BlockSpec((1, 8, tq), lambda j, kb: (j, 0, kb))],
        out_shape=[jax.ShapeDtypeStruct((T, 4 * LANES), F32), jax.ShapeDtypeStruct((T, 4 * LANES), BF),
                   jax.ShapeDtypeStruct((4, 8, T), F32)],
        compiler_params=_params(("parallel", "arbitrary")),
    )(qn, kn, vb, fr, cq, dl, dmix)


_BIAS_LANE = HEAD


def _split3(f):
    hi = f.astype(BF).astype(F32)
    mid = (f - hi).astype(BF).astype(F32)
    lo = ((f - hi) - mid).astype(BF).astype(F32)
    return hi, mid, lo


def _fox_operands(P, fc, g_fq2, g_fk2, T, tm):
    def body(fq_ref, fk_ref, fv_ref, fc_ref, gq_ref, gk_ref, qa_ref, qat_ref, ka_ref, kat_ref, va_ref, vat_ref):
        j = pl.program_id(0)
        lane = lax.broadcasted_iota(jnp.int32, (1, LANES), 1)
        qn, kn = _fox_prep_fn(fq_ref[...], fk_ref[...], gq_ref[...], gk_ref[...])
        v = fv_ref[...]
        fcb = fc_ref[...]
        b = _BIAS_LANE
        for hh in range(2):
            hi, mid, lo = _split3(_head_bias_col(fcb, 2 * j + hh))
            take = (lambda a: a) if hh == 0 else (lambda a: pltpu.roll(a, HEAD, 1))
            qa = jnp.where(lane < HEAD, take(qn), jnp.where(lane == b, hi, jnp.where(lane == b + 1, mid, jnp.where(
                lane == b + 2, lo, jnp.where(lane < b + 6, 1.0, 0.0)))))
            ka = jnp.where(lane < HEAD, take(kn), jnp.where(lane < b + 3, 1.0, jnp.where(lane == b + 3, -hi, jnp.where(
                lane == b + 4, -mid, jnp.where(lane == b + 5, -lo, 0.0)))))
            va = jnp.where(lane < HEAD, take(v), 0.0)
            for val, ref, tref in ((qa, qa_ref, qat_ref), (ka, ka_ref, kat_ref), (va, va_ref, vat_ref)):
                ref[hh] = val.astype(BF)
                tref[hh] = val.T.astype(BF)

    sec = lambda off: pl.BlockSpec((tm, LANES), lambda j, i: (i, off + j))
    par = pl.BlockSpec((1, LANES), lambda j, i: (0, 0))
    nat = pl.BlockSpec((2, tm, LANES), lambda j, i: (j, i, 0))
    trn = pl.BlockSpec((2, LANES, tm), lambda j, i: (j, 0, i))
    return pl.pallas_call(
        body, name="fox_operands", grid=(4, T // tm),
        in_specs=[sec(16), sec(20), sec(24), pl.BlockSpec((tm, LANES), lambda j, i: (i, 0)), par, par],
        out_specs=[nat, trn, nat, trn, nat, trn],
        out_shape=[jax.ShapeDtypeStruct((8, T, LANES), BF), jax.ShapeDtypeStruct((8, LANES, T), BF)] * 3,
        compiler_params=_params(("parallel", "arbitrary")),
    )(P, P, P, fc, g_fq2, g_fk2)


def _fox_forward(qat, ka, vat, T, tq, tk):
    nq, per = T // tq, tq // tk

    def body(qat_ref, ka_ref, vat_ref, o_ref, lse_ref):
        i = pl.program_id(1)
        sub = lax.broadcasted_iota(jnp.int32, (8, 1), 0)
        krow = lax.broadcasted_iota(jnp.int32, (tk, tq), 0)
        qcol = lax.broadcasted_iota(jnp.int32, (tk, tq), 1)

        def scores(kb):
            k0 = pl.multiple_of(kb * tk, tk)
            return tuple(jnp.dot(ka_ref[hh, pl.ds(k0, tk), :], qat_ref[hh], preferred_element_type=F32) for hh in range(2))

        def step(kb, carry, mask, last=False):
            stats, s_now = carry
            s_next = s_now if last else scores(kb + 1)
            k0 = pl.multiple_of(kb * tk, tk)
            new = []
            for hh in range(2):
                m, l, acc = stats[hh]
                s = s_now[hh] if mask is None else jnp.where(mask, s_now[hh], NEG)
                m2 = jnp.maximum(m, jnp.max(s, axis=0, keepdims=True))
                p = jnp.exp(s - m2)
                a = jnp.exp(m - m2)
                pv = jnp.dot(vat_ref[hh, 0:HEAD, pl.ds(k0, tk)], p.astype(BF), preferred_element_type=F32)
                new.append((m2, a * l + jnp.sum(p, axis=0, keepdims=True), a * acc + pv))
            return tuple(new), s_next

        one = (jnp.full((1, tq), NEG, F32), jnp.zeros((1, tq), F32), jnp.zeros((HEAD, tq), F32))
        carry = lax.fori_loop(0, i * per, lambda kb, c: step(kb, c, None), ((one, one), scores(0)))
        for d in range(per):
            carry = step(i * per + d, carry, krow + d * tk <= qcol, last=(d == per - 1))
        stats = carry[0]
        o_ref[...] = jnp.concatenate([acc / l for _, l, acc in stats], axis=0).T
        lses = [m + jnp.log(l) for m, l, _ in stats]
        lse_ref[0] = jnp.where(sub == 0, lses[0], jnp.where(sub == 1, lses[1], 0.0))

    return pl.pallas_call(
        body, name="fox_forward", grid=(4, nq),
        in_specs=[pl.BlockSpec((2, LANES, tq), lambda j, i: (j, 0, i)), pl.BlockSpec((2, T, LANES), lambda j, i: (j, 0, 0)),
                  pl.BlockSpec((2, LANES, T), lambda j, i: (j, 0, 0))],
        out_specs=[pl.BlockSpec((tq, LANES), lambda j, i: (i, j)), pl.BlockSpec((1, 8, tq), lambda j, i: (j, 0, i))],
        out_shape=[jax.ShapeDtypeStruct((T, 4 * LANES), F32), jax.ShapeDtypeStruct((4, 8, T), F32)],
        compiler_params=_params(("parallel", "arbitrary")),
    )(qat, ka, vat)


def _fox_cotangent(dmix, fox, T, tm):
    def body(do_ref, o_ref, doa_ref, doat_ref, dl_ref):
        lane = lax.broadcasted_iota(jnp.int32, (1, LANES), 1)
        sub = lax.broadcasted_iota(jnp.int32, (8, 1), 0)
        dob = do_ref[...].astype(BF).astype(F32)
        prod_t = (dob * o_ref[...]).T
        d0 = jnp.sum(prod_t[:HEAD], axis=0, keepdims=True)
        d1 = jnp.sum(prod_t[HEAD:], axis=0, keepdims=True)
        dl_ref[0] = jnp.where(sub == 0, d0, jnp.where(sub == 1, d1, 0.0))
        for hh in range(2):
            val = jnp.where(lane < HEAD, dob if hh == 0 else pltpu.roll(dob, HEAD, 1), 0.0)
            doa_ref[hh] = val.astype(BF)
            doat_ref[hh] = val.T.astype(BF)

    return pl.pallas_call(
        body, name="fox_cotangent", grid=(4, T // tm),
        in_specs=[pl.BlockSpec((tm, LANES), lambda j, i: (i, 4 + j)), pl.BlockSpec((tm, LANES), lambda j, i: (i, j))],
        out_specs=[pl.BlockSpec((2, tm, LANES), lambda j, i: (j, i, 0)), pl.BlockSpec((2, LANES, tm), lambda j, i: (j, 0, i)),
                   pl.BlockSpec((1, 8, tm), lambda j, i: (j, 0, i))],
        out_shape=[jax.ShapeDtypeStruct((8, T, LANES), BF), jax.ShapeDtypeStruct((8, LANES, T), BF),
                   jax.ShapeDtypeStruct((4, 8, T), F32)],
        compiler_params=_params(("parallel", "arbitrary")),
    )(dmix, fox)


def _fox_backward(qa, qat, ka, kat, va, doa, doat, lse, dl, T, tq, tk):
    nq, nk = T // tq, T // tk
    HEAD_GROUPS = ((0, 1),)

    def body(qa_ref, qat_ref, ka_ref, kat_ref, va_ref, doa_ref, doat_ref, lse_ref, dl_ref,
             dq_ref, dk_ref, dv_ref, df_ref, dr_ref, dqt, dk_acc, dv_acc, df_acc):
        j, kb = pl.program_id(0), pl.program_id(1)
        lane = lax.broadcasted_iota(jnp.int32, (1, LANES), 1)
        first = (kb * tk) // tq
        mask = (lax.broadcasted_iota(jnp.int32, (tk, tq), 0) + (kb * tk - first * tq)
                <= lax.broadcasted_iota(jnp.int32, (tk, tq), 1))

        @pl.when(kb == 0)
        def _():
            dqt[...] = jnp.zeros(dqt.shape, F32)

        dk_acc[...] = jnp.zeros(dk_acc.shape, F32)
        dv_acc[...] = jnp.zeros(dv_acc.shape, F32)
        df_acc[...] = jnp.zeros(df_acc.shape, F32)

        def products(qi, heads):
            q0 = pl.multiple_of(qi * tq, tq)
            return tuple((jnp.dot(ka_ref[hh], qat_ref[hh, :, pl.ds(q0, tq)], preferred_element_type=F32),
                          jnp.dot(va_ref[hh], doat_ref[hh, :, pl.ds(q0, tq)], preferred_element_type=F32)) for hh in heads)

        def block(qi, now, keep, heads):
            q0 = pl.multiple_of(qi * tq, tq)
            for n, hh in enumerate(heads):
                s, dp = now[n]
                p = jnp.exp(s - lse_ref[0, hh:hh + 1, pl.ds(q0, tq)])
                if keep is not None:
                    p = jnp.where(keep, p, 0.0)
                ds = p * (dp - dl_ref[0, hh:hh + 1, pl.ds(q0, tq)])
                pb, dsb = p.astype(BF), ds.astype(BF)
                dv_acc[hh] += jnp.dot(pb, doa_ref[hh, pl.ds(q0, tq), :], preferred_element_type=F32)
                dk_acc[hh] += jnp.dot(dsb, qa_ref[hh, pl.ds(q0, tq), :], preferred_element_type=F32)
                dqt[hh, 0:HEAD, pl.ds(q0, tq)] += jnp.dot(kat_ref[hh, 0:HEAD, :], dsb, preferred_element_type=F32)
                dqt[hh, HEAD:HEAD + 8, pl.ds(q0, tq)] += jnp.broadcast_to(jnp.sum(ds, axis=0, keepdims=True), (8, tq))
                part = ds[:, 0:LANES]
                for c in range(1, tq // LANES):
                    part = part + ds[:, c * LANES:(c + 1) * LANES]
                df_acc[hh] += part

        def step(qi, now, keep, heads):
            ahead = products(jnp.minimum(qi + 1, nq - 1), heads)
            block(qi, now, keep, heads)
            return ahead

        for heads in HEAD_GROUPS:
            lax.fori_loop(first + 1, nq, lambda qi, now, heads=heads: step(qi, now, None, heads),
                          step(first, products(first, heads), mask, heads))

        lo = lane < HEAD
        dk_ref[...] = jnp.where(lo, dk_acc[0], pltpu.roll(dk_acc[1], HEAD, 1))
        dv_ref[...] = jnp.where(lo, dv_acc[0], pltpu.roll(dv_acc[1], HEAD, 1)).astype(dv_ref.dtype)
        f0 = -jnp.sum(df_acc[0], axis=1, keepdims=True)
        f1 = -jnp.sum(df_acc[1], axis=1, keepdims=True)
        df_ref[0] = jnp.where(lane == 2 * j, f0, jnp.where(lane == 2 * j + 1, f1, 0.0))

        @pl.when(kb == nk - 1)
        def _():
            for t in range(nq):
                cols = slice(t * tq, (t + 1) * tq)
                dq_ref[cols, :] = jnp.concatenate([dqt[0, 0:HEAD, cols], dqt[1, 0:HEAD, cols]], axis=0).T
                rsum = jnp.concatenate([dqt[0, HEAD:HEAD + 8, cols], dqt[1, HEAD:HEAD + 8, cols],
                                        jnp.zeros((LANES - 16, tq), F32)], axis=0).T
                dr_ref[0, cols, :] = jnp.where(lane == 2 * j, rsum[:, 0:1], jnp.where(lane == 2 * j + 1, rsum[:, 8:9], 0.0))

    nat_full = pl.BlockSpec((2, T, LANES), lambda j, kb: (j, 0, 0))
    trn_full = pl.BlockSpec((2, LANES, T), lambda j, kb: (j, 0, 0))
    nat_blk = pl.BlockSpec((2, tk, LANES), lambda j, kb: (j, kb, 0))
    trn_blk = pl.BlockSpec((2, LANES, tk), lambda j, kb: (j, 0, kb))
    rows = pl.BlockSpec((1, 8, T), lambda j, kb: (j, 0, 0))
    blk = pl.BlockSpec((tk, LANES), lambda j, kb: (kb, j))
    return pl.pallas_call(
        body, name="fox_backward", grid=(4, nk),
        in_specs=[nat_full, trn_full, nat_blk, trn_blk, nat_blk, nat_full, trn_full, rows, rows],
        out_specs=[pl.BlockSpec((T, LANES), lambda j, kb: (0, j)), blk, blk, pl.BlockSpec((1, tk, LANES), lambda j, kb: (j, kb, 0)),
                   pl.BlockSpec((1, T, LANES), lambda j, kb: (j, 0, 0))],
        out_shape=[jax.ShapeDtypeStruct((T, 4 * LANES), F32), jax.ShapeDtypeStruct((T, 4 * LANES), F32),
                   jax.ShapeDtypeStruct((T, 4 * LANES), BF), jax.ShapeDtypeStruct((4, T, LANES), F32),
                   jax.ShapeDtypeStruct((4, T, LANES), F32)],
        scratch_shapes=[pltpu.VMEM((2, HEAD + 8, T), F32), pltpu.VMEM((2, tk, LANES), F32), pltpu.VMEM((2, tk, LANES), F32),
                        pltpu.VMEM((2, tk, LANES), F32)],
        compiler_params=_params(("arbitrary", "arbitrary")),
    )(qa, qat, ka, kat, va, doa, doat, lse, dl)


def _fgate_bwd_col(ffp, bpad, dfc4, drc4, T):
    def body(ff_ref, b_ref, dfc_ref, drc_ref, dff_ref, db_ref):
        lane = lax.broadcasted_iota(jnp.int32, (1, LANES), 1)
        tri = _tri(False)
        carry = jnp.zeros((1, LANES), F32)
        db = jnp.zeros((1, LANES), F32)
        for blk in reversed(range(T // _FB)):
            rows = slice(blk * _FB, (blk + 1) * _FB)
            dcol = dfc_ref[0, rows, :] + drc_ref[0, rows, :]
            for pair in range(1, 4):
                dcol = dcol + (dfc_ref[pair, rows, :] + drc_ref[pair, rows, :])
            dlf = jnp.dot(tri, dcol, precision=lax.Precision.HIGHEST, preferred_element_type=F32) + carry
            carry = dlf[0:1, :]
            z = ff_ref[blk * _FB:(blk + 1) * _FB, :] + b_ref[...]
            dz = jnp.where(lane < 8, dlf * jax.nn.sigmoid(-z), 0.0)
            dff_ref[blk * _FB:(blk + 1) * _FB, :] = dz.astype(dff_ref.dtype)
            db = db + jnp.sum(dz, axis=0, keepdims=True)
        db_ref[...] = db

    return pl.pallas_call(
        body, name="fgate_bwd",
        out_shape=[jax.ShapeDtypeStruct((T, LANES), BF), jax.ShapeDtypeStruct((1, LANES), F32)],
        compiler_params=pltpu.CompilerParams(vmem_limit_bytes=VMEM_LIMIT),
    )(ffp, bpad, dfc4, drc4)


MESH = pl.DeviceIdType.MESH


def _place():
    return lax.axis_index("x"), lax.axis_index("y"), lax.axis_index("c")


def _all_gather(shard):
    R, W = shard.shape

    def body(x_ref, out_ref, send_sems, recv_sems, local_sem):
        x, y, c = _place()
        me, sibling = (x, y, c), (x, y, 1 - c)
        chips = [(1 - x, y), (x, 1 - y), (1 - x, 1 - y)]

        def slot(px, py, pc):
            return out_ref.at[4 * px + 2 * py + pc]

        def copy(k, block, to, src=None):
            return pltpu.make_async_remote_copy(
                src_ref=slot(*block) if src is None else src, dst_ref=slot(*block),
                send_sem=send_sems.at[k], recv_sem=recv_sems.at[k], device_id=to, device_id_type=MESH)

        mine = pltpu.make_async_copy(x_ref, slot(*me), local_sem)
        mine.start()
        first = [copy(0, me, sibling, src=x_ref)]
        first += [copy(1 + n, me, (*chip, c), src=x_ref) for n, chip in enumerate(chips)]
        for cp in first:
            cp.start()
        passed = [copy(4 + n, (*chip, c), sibling) for n, chip in enumerate(chips)]
        for n, chip in enumerate(chips):
            copy(1 + n, (*chip, c), me).wait_recv()
            passed[n].start()
        copy(0, sibling, me).wait_recv()
        for n, chip in enumerate(chips):
            copy(4 + n, (*chip, 1 - c), me).wait_recv()
        for cp in first + passed:
            cp.wait_send()
        mine.wait()

    return pl.pallas_call(
        body, name="all_gather_weights",
        out_shape=jax.ShapeDtypeStruct((N_DEV, R, W), shard.dtype),
        in_specs=[pl.BlockSpec(memory_space=pl.ANY)], out_specs=pl.BlockSpec(memory_space=pl.ANY),
        scratch_shapes=[pltpu.SemaphoreType.DMA((7,)), pltpu.SemaphoreType.DMA((7,)), pltpu.SemaphoreType.DMA],
    )(shard)


def _all_to_all(big, small):
    def body(big_ref, small_ref, rbig_ref, rsmall_ref, send_sems, recv_sems, local_sems):
        x, y, c = _place()
        me = 4 * x + 2 * y + c
        l0 = pltpu.make_async_copy(big_ref.at[me], rbig_ref.at[me], local_sems.at[0])
        l1 = pltpu.make_async_copy(small_ref, rsmall_ref.at[me], local_sems.at[1])
        l0.start()
        l1.start()
        copies = []
        for r in range(1, N_DEV):
            px, py, pc = x ^ (r >> 2), y ^ ((r >> 1) & 1), c ^ (r & 1)
            peer = 4 * px + 2 * py + pc
            copies.append(pltpu.make_async_remote_copy(
                src_ref=big_ref.at[peer], dst_ref=rbig_ref.at[me], send_sem=send_sems.at[2 * r], recv_sem=recv_sems.at[2 * r],
                device_id=(px, py, pc), device_id_type=MESH))
            copies.append(pltpu.make_async_remote_copy(
                src_ref=small_ref, dst_ref=rsmall_ref.at[me], send_sem=send_sems.at[2 * r + 1], recv_sem=recv_sems.at[2 * r + 1],
                device_id=(px, py, pc), device_id_type=MESH))
        for cp in copies:
            cp.start()
        for cp in copies:
            cp.wait_recv()
        for cp in copies:
            cp.wait_send()
        l0.wait()
        l1.wait()

    return pl.pallas_call(
        body, name="all_to_all_grads",
        out_shape=[jax.ShapeDtypeStruct(big.shape, big.dtype), jax.ShapeDtypeStruct((N_DEV,) + small.shape, small.dtype)],
        in_specs=[pl.BlockSpec(memory_space=pl.ANY)] * 2, out_specs=[pl.BlockSpec(memory_space=pl.ANY)] * 2,
        scratch_shapes=[pltpu.SemaphoreType.DMA((2 * N_DEV,)), pltpu.SemaphoreType.DMA((2 * N_DEV,)), pltpu.SemaphoreType.DMA((2,))],
    )(big, small)


def _exchange_copies(src_ref, land_ref, send_sems, recv_sems, scatter):
    x, y, c = _place()
    me = 4 * x + 2 * y + c
    copies = []
    for r in range(1, N_DEV):
        px, py, pc = x ^ (r >> 2), y ^ ((r >> 1) & 1), c ^ (r & 1)
        copies.append(pltpu.make_async_remote_copy(
            src_ref=src_ref.at[4 * px + 2 * py + pc] if scatter else src_ref, dst_ref=land_ref.at[me],
            send_sem=send_sems.at[r - 1], recv_sem=recv_sems.at[r - 1], device_id=(px, py, pc), device_id_type=MESH))
    return copies


_HBM = pl.BlockSpec(memory_space=pltpu.HBM)
_SEM = pl.BlockSpec(memory_space=pltpu.SEMAPHORE)
_EFFECT = pltpu.SideEffectType.DATAFLOW_SIDE_EFFECTING


def _exchange_start(name, src, land, scatter):
    def body(src_ref, land_ref, send_sems, recv_sems, src_thru, land_thru, token):
        for cp in _exchange_copies(src_ref, land_ref, send_sems, recv_sems, scatter):
            cp.start()
        token[...] = jnp.zeros(token.shape, F32)

    return pl.pallas_call(
        body, name=name,
        out_shape=(pltpu.SemaphoreType.DMA((N_DEV - 1,)), pltpu.SemaphoreType.DMA((N_DEV - 1,)),
                   pltpu.HBM(src.shape, src.dtype), pltpu.HBM(land.shape, land.dtype), jax.ShapeDtypeStruct((8, LANES), F32)),
        in_specs=(_HBM, _HBM), out_specs=(_SEM, _SEM, _HBM, _HBM, pl.BlockSpec(memory_space=pltpu.VMEM)),
        input_output_aliases={0: 2, 1: 3},
        compiler_params=pltpu.CompilerParams(has_side_effects=_EFFECT),
    )(pltpu.with_memory_space_constraint(src, pltpu.HBM), pltpu.with_memory_space_constraint(land, pltpu.HBM))


def _exchange_wait(name, started, after, scatter):
    send_sems, recv_sems, src_thru, land_thru, _ = started

    def body(src_ref, land_ref, send_sems, recv_sems, after_ref, src_dead, got_ref):
        copies = _exchange_copies(src_ref, land_ref, send_sems, recv_sems, scatter)
        for cp in copies:
            cp.wait_send()
        for cp in copies:
            cp.wait_recv()

    return pl.pallas_call(
        body, name=name,
        out_shape=(pltpu.HBM(src_thru.shape, src_thru.dtype), pltpu.HBM(land_thru.shape, land_thru.dtype)),
        in_specs=(_HBM, _HBM, _SEM, _SEM, pl.BlockSpec(memory_space=pl.ANY)), out_specs=(_HBM, _HBM),
        input_output_aliases={0: 0, 1: 1},
        compiler_params=pltpu.CompilerParams(has_side_effects=_EFFECT),
    )(src_thru, land_thru, send_sems, recv_sems, after)


def _adamw(name, slots, w, m, v, tr, own=None):
    R, W = w.shape

    def body(s_ref, *refs):
        if own is not None:
            own_ref, refs = refs[0], refs[1:]
        w_ref, m_ref, v_ref, g_ref, d_ref, nm_ref, nv_ref = refs
        g = s_ref[0].astype(F32)
        for s in range(1, N_DEV):
            g = g + s_ref[s].astype(F32)
        if own is not None:
            g = g + own_ref[...].astype(F32)
        m2 = ADAM_B1 * m_ref[...] + (1.0 - ADAM_B1) * g
        v2 = ADAM_B2 * v_ref[...] + (1.0 - ADAM_B2) * jnp.square(g)
        m_hat = m2 / (1.0 - ADAM_B1 ** ADAM_STEP)
        v_hat = v2 / (1.0 - ADAM_B2 ** ADAM_STEP)
        g_ref[...] = g
        d_ref[...] = -ADAM_LR * (m_hat / (jnp.sqrt(v_hat) + ADAM_EPS) + ADAM_WD * w_ref[...])
        nm_ref[...] = m2
        nv_ref[...] = v2

    row = lambda: pl.BlockSpec((tr, W), lambda i: (i, 0))
    return pl.pallas_call(
        body, name=name, grid=(R // tr,),
        in_specs=[pl.BlockSpec((N_DEV, tr, W), lambda i: (0, i, 0))] + [row() for _ in range(3 + (own is not None))],
        out_specs=[row(), row(), row(), row()],
        out_shape=[jax.ShapeDtypeStruct((R, W), F32)] * 4,
        compiler_params=_params(("parallel",)),
    )(slots, *([own] if own is not None else []), w, m, v)


def _tables(T):
    pos = jnp.arange(T, dtype=F32)
    inv_freq = 10000.0 ** (-jnp.arange(0, HEAD, 2, dtype=F32) / HEAD)
    ang = pos[:, None] * inv_freq[None, :]
    cos, sin = jnp.cos(ang), jnp.sin(ang)
    cos4 = jnp.tile(cos, (1, 4))
    sin4 = jnp.tile(jnp.concatenate([-sin, sin], axis=1), (1, 2))
    log_g = jnp.log(1.0 - 2.0 ** (-5.0 - jnp.arange(8, dtype=F32)))
    return cos4, sin4, jnp.repeat(log_g, HEAD)[None, :]


def _local_step(x, mem, target, sp, w_inT, token, fetch_rest, push, push_small):
    T = x.shape[0]
    tm = min(512, T)
    tq = min(256, T)
    tb = min(1024, T)
    cos4, sin4, lg = _tables(T)
    g_fq2 = jnp.tile(sp["g_fox_q"], (1, 2))
    g_fk2 = jnp.tile(sp["g_fox_k"], (1, 2))
    g_ret = sp["g_ret_out"].reshape(1, 8 * HEAD)
    bpad = jnp.pad(sp["b_forget"], ((0, 0), (0, LANES - 8)))
    w_secs = [w_inT[k * 512:(k + 1) * 512] for k in range(7)]
    w_ffT = jnp.pad(w_inT[3584:3592], ((0, LANES - 8), (0, 0)))
    w_mainT = w_inT[:3584]
    tie = lambda p, tok: p + tok[0:1, 0:1]
    tm2, tm4 = min(1024, T), min(2048, T)

    hn1, = _rw_fwd("rms_mix", _rms_fn, [(x, D, 0, False)], [(tie(sp["g_mix"], token), D, 0, False)], [(BF, D)], T, tm, 1)
    P, = _mm("proj_in", [[(hn1, w_mainT, "nt")]], [], _ident, T, 3584, tm4, 512, [F32])
    ffp, = _mm("proj_ff", [[(hn1, w_ffT, "nt")]], [], _ident, T, LANES, tm, LANES, [F32])
    ret, s0 = _ret_fwd(P, cos4, sin4, g_ret, lg, T, tb)
    fc, _ = _fgate_fwd(ffp, bpad, T)
    qa, qat, ka, kat, va, vat = _fox_operands(P, fc, g_fq2, g_fk2, T, tm)
    fox, lse = _fox_forward(qat, ka, vat, T, min(512, T), tq)
    W = fetch_rest(fox)
    w_out_halves = (W["w_out"][:4 * LANES], W["w_out"][4 * LANES:])
    h1, hn2 = _mm("proj_out", [[(ret, w_out_halves[0], "nn"), (fox, w_out_halves[1], "nn")]], [x], _add_rms_epi, T, D, tm2, D,
                  [F32, BF], params=[sp["g_xattn"]])

    qx, = _mm("proj_xq", [[(hn2, W["w_xq"], "nn")]], [], _ident, T, D, tm2, D, [F32])
    memn, = _rw_fwd("rms_mem", _rms_fn, [(mem, D, 0, False)], [(sp["g_mem"], D, 0, False)], [(BF, D)], N_MEM, N_MEM, 1)
    kv, = _mm("proj_xkv", [[(memn, W["w_xkvT"], "nt")]], [], _ident, N_MEM, 2 * D, N_MEM, 512, [F32])
    xa_rows = [(qx, XHEAD, 0, True)]
    xa_params = [(sp["g_xq"], XHEAD, 0, False), (sp["g_xk"], XHEAD, 0, False), (kv, XHEAD, 0, True), (kv, XHEAD, 4, True)]
    xo, = _rw_fwd("xattn_fwd", _xattn_fn, xa_rows, xa_params, [(BF, XHEAD)], T, tm, 4)
    h2, hn3 = _mm("proj_xo", [[(xo, W["w_xo"], "nn")]], [h1], _add_rms_epi, T, D, tm2, D, [F32, BF], params=[sp["g_ffn"]])

    gate, up, act = _mm("ffn_in", [[(hn3, W["w_gateT"], "nt")], [(hn3, W["w_upT"], "nt")]], [], _swiglu_fwd_epi,
                        T, D_FF, tm4, 256, [BF, BF, BF])
    dy, dyb, loss_part = _mm("ffn_out", [[(act, W["w_down"], "nn")]], [h2, target], _add_loss_epi, T, D, tm, D, [F32, BF], n_acc=1)

    dgate, dup = _mm("ffn_out_bwd", [[(dyb, W["w_down"], "nt")]], [gate, up], _swiglu_bwd_epi, T, D_FF, tm4, 256, [BF, BF])
    gW = {}
    gW["w_gateT"], = _mm("dw_gate", [[(dgate, hn3, "tn")]], [], _ident, D_FF, D, 256, D, [BF])
    gW["w_upT"], = _mm("dw_up", [[(dup, hn3, "tn")]], [], _ident, D_FF, D, 256, D, [BF])
    gW["w_down"], = _mm("dw_down", [[(act, dyb, "tn")]], [], _ident, D_FF, D, 256, D, [BF])
    tok = push("ffn", gW)
    gs = {}
    dh2, dh2b, gs["g_ffn"] = _mm("ffn_in_bwd", [[(dgate, W["w_gateT"], "nn"), (dup, W["w_upT"], "nn")]], [h2, dy], _rms_bwd_epi,
                                 T, D, min(256, T), D, [F32, BF], params=[tie(sp["g_ffn"], tok)], n_acc=1)

    dxo, = _mm("proj_xo_bwd", [[(dh2b, W["w_xo"], "nt")]], [], _ident, T, D, tm2, D, [BF])
    gW["w_xo"], = _mm("dw_xo", [[(xo, dh2b, "tn")]], [], _ident, D, D, 256, D, [BF])
    dqx, gs["g_xq"], gs["g_xk"], dkv_k, dkv_v = _rw_bwd(
        "xattn_bwd", _xattn_fn, xa_rows, xa_params, [(dxo, XHEAD, 0, True)], T, tm, 4, [BF], [True, True, True, True])
    dkv = jnp.concatenate([dkv_k[:, :D], dkv_v[:, D:]], axis=1)
    gW["w_xq"], = _mm("dw_xq", [[(hn2, dqx, "tn")]], [], _ident, D, D, 256, D, [BF])
    dmemn, = _mm("proj_xkv_bwd", [[(dkv, W["w_xkvT"], "nn")]], [], _ident, N_MEM, D, N_MEM, 512, [F32])
    gW["w_xkvT"], = _mm("dw_xkv", [[(dkv, memn, "tn")]], [], _ident, 2 * D, D, 512, D, [BF])
    tok = push("xattn", gW)
    gs["g_mem"], = _rw_bwd("rms_mem_bwd", _rms_fn, [(mem, D, 0, False)], [(sp["g_mem"], D, 0, False)], [(dmemn, D, 0, False)],
                           N_MEM, N_MEM, 1, [None], [True])
    dh1, dh1b, gs["g_xattn"] = _mm("proj_xq_bwd", [[(dqx, W["w_xq"], "nt")]], [h1, dh2], _rms_bwd_epi, T, D, tm, D, [F32, BF],
                                   params=[tie(sp["g_xattn"], tok)], n_acc=1)

    dmix, = _mm("proj_out_bwd", [[(dh1b, W["w_out"], "nt")]], [], _ident, T, D, tm2, D, [F32])
    gW["w_out"] = jnp.concatenate([_mm("dw_out_%d" % k, [[(a, dh1b, "tn")]], [], _ident, 4 * LANES, D, 256, D, [BF])[0]
                                   for k, a in enumerate((ret, fox))], axis=0)
    tok = push("out", gW)
    doa, doat, dl = _fox_cotangent(dmix, fox, T, tm)
    dqn, dkn, dfv, dfc4, drc4 = _fox_backward(qa, qat, ka, kat, va, doa, doat, lse + tok[0:1, 0:1], dl, T, tq, tq)
    dfq, dfk, gq2, gk2 = _rw_bwd("fox_prep_bwd", _fox_prep_fn, [(P, LANES, 16, True), (P, LANES, 20, True)],
                                 [(g_fq2, LANES, 0, False), (g_fk2, LANES, 0, False)],
                                 [(dqn, LANES, 0, True), (dkn, LANES, 0, True)], T, tm, 4, [BF, BF], [True, True])
    gs["g_fox_q"] = gq2[:, :HEAD] + gq2[:, HEAD:]
    gs["g_fox_k"] = gk2[:, :HEAD] + gk2[:, HEAD:]
    dff, dbp = _fgate_bwd_col(ffp, bpad, dfc4, drc4, T)
    gs["b_forget"] = dbp[:, :8]
    drq, drk, drv, drg, dg_ret = _ret_bwd(P, cos4, sin4, g_ret, lg, s0, dmix, T, tb)
    gs["g_ret_out"] = dg_ret
    dsecs = [drq, drk, drv, drg, dfq, dfk, dfv]
    g_secs = [_mm("dw_in_%d" % k, [[(d, hn1, "tn")]], [], _ident, 512, D, 256, D, [BF])[0] for k, d in enumerate(dsecs)]
    g_ff, = _mm("dw_in_ff", [[(dff, hn1, "tn")]], [], _ident, LANES, D, LANES, D, [BF])
    gW["w_inT"] = jnp.concatenate(g_secs + [g_ff[:8]], axis=0)
    tok = push("in", gW)
    grad_x, _, gs["g_mix"] = _mm("proj_in_bwd", [[(d, w, "nn") for d, w in zip(dsecs, w_secs)] + [(dff, w_ffT, "nn")]], [x, dh1],
                                 _rms_bwd_epi, T, D, tm, D, [F32, BF], params=[tie(sp["g_mix"], tok)], n_acc=1)
    return grad_x, push_small(gs, loss_part)


_CANON = {"w_in": "w_inT", "w_xkv": "w_xkvT", "w_gate": "w_gateT", "w_up": "w_upT"}
_SMALL = (("g_mix", 0, 0, 1024), ("g_xattn", 1, 0, 1024), ("g_mem", 2, 0, 1024), ("g_ffn", 3, 0, 1024),
          ("g_ret_out", 4, 0, 512), ("g_xq", 4, 512, 256), ("g_xk", 4, 768, 256),
          ("g_fox_q", 5, 0, 64), ("g_fox_k", 5, 64, 64), ("b_forget", 5, 128, 8))
_LOSS_AT = (5, 256)


def _pack_shards(tree, dtype):
    parts = []
    for name, rows, padded, transposed in W_LAYOUT:
        a = tree[name][0]
        a = a.T if transposed else a
        parts.append(jnp.pad(a, ((0, padded - rows), (0, 0))).astype(dtype))
    return jnp.concatenate(parts, axis=0)


def _unpack_shards(packed, like):
    out = {}
    for name, rows, padded, transposed in W_LAYOUT:
        a = packed[W_OFF[name]:W_OFF[name] + rows]
        out[name] = (a.T if transposed else a)[None].reshape(like[name].shape)
    return out


def _pack_small(tree):
    rows = [jnp.zeros((1, D), F32) for _ in range(SMALL_ROWS)]
    buf = jnp.concatenate(rows, axis=0)
    for name, r, c, n in _SMALL:
        buf = lax.dynamic_update_slice(buf, tree[name].reshape(1, n).astype(F32), (r, c))
    return buf


def _unpack_small(buf, like):
    return {name: buf[r:r + 1, c:c + n].reshape(like[name].shape) for name, r, c, n in _SMALL}


def kernel(x, mem, g_mix, w_in, b_forget, g_ret_out, g_fox_q, g_fox_k, w_out, g_xattn, w_xq, w_xkv, g_mem, g_xq, g_xk, w_xo, g_ffn, w_gate, w_up, w_down, loss_target, m_g_mix, m_w_in, m_b_forget, m_g_ret_out, m_g_fox_q, m_g_fox_k, m_w_out, m_g_xattn, m_w_xq, m_w_xkv, m_g_mem, m_g_xq, m_g_xk, m_w_xo, m_g_ffn, m_w_gate, m_w_up, m_w_down, v_g_mix, v_w_in, v_b_forget, v_g_ret_out, v_g_fox_q, v_g_fox_k, v_w_out, v_g_xattn, v_w_xq, v_w_xkv, v_g_mem, v_g_xq, v_g_xk, v_w_xo, v_g_ffn, v_w_gate, v_w_up, v_w_down):
    names = ("g_mix", "w_in", "b_forget", "g_ret_out", "g_fox_q", "g_fox_k", "w_out", "g_xattn", "w_xq", "w_xkv", "g_mem",
             "g_xq", "g_xk", "w_xo", "g_ffn", "w_gate", "w_up", "w_down")
    w = dict(zip(names, (g_mix, w_in, b_forget, g_ret_out, g_fox_q, g_fox_k, w_out, g_xattn, w_xq, w_xkv, g_mem, g_xq, g_xk,
                         w_xo, g_ffn, w_gate, w_up, w_down)))
    m = dict(zip(names, (m_g_mix, m_w_in, m_b_forget, m_g_ret_out, m_g_fox_q, m_g_fox_k, m_w_out, m_g_xattn, m_w_xq, m_w_xkv,
                         m_g_mem, m_g_xq, m_g_xk, m_w_xo, m_g_ffn, m_w_gate, m_w_up, m_w_down)))
    v = dict(zip(names, (v_g_mix, v_w_in, v_b_forget, v_g_ret_out, v_g_fox_q, v_g_fox_k, v_w_out, v_g_xattn, v_w_xq, v_w_xkv,
                         v_g_mem, v_g_xq, v_g_xk, v_w_xo, v_g_ffn, v_w_gate, v_w_up, v_w_down)))
    small_names = [s[0] for s in _SMALL]

    gathered = _all_gather(_pack_shards(w, BF))
    W = {}
    for name, rows, padded, transposed in W_LAYOUT:
        full = gathered[:, W_OFF[name]:W_OFF[name] + rows].reshape(N_DEV * rows, D)
        W[_CANON.get(name, name)] = full

    sp = {n: w[n].reshape(1, -1) for n in small_names}
    loss_part, grad_x, gW, gs = _local_step(x[0], mem[0], loss_target[0], sp, W)

    chunks = []
    for name, rows, padded, transposed in W_LAYOUT:
        g = gW[_CANON.get(name, name)].reshape(N_DEV, rows, D)
        chunks.append(jnp.pad(g, ((0, 0), (0, padded - rows), (0, 0))).astype(BF))
    send = jnp.concatenate(chunks, axis=1)
    small = _pack_small(gs)
    small = lax.dynamic_update_slice(small, loss_part[:, :1], _LOSS_AT)
    recv, recv_small = _all_to_all(send, small)

    g_big, d_big, m_big, v_big = _adamw("adamw_shards", recv, _pack_shards(w, F32), _pack_shards(m, F32), _pack_shards(v, F32), 240)
    g_sm, d_sm, m_sm, v_sm = _adamw("adamw_small", recv_small, _pack_small(w), _pack_small(m), _pack_small(v), SMALL_ROWS)
    loss = g_sm[_LOSS_AT[0], _LOSS_AT[1]]

    outs = []
    for big, sm in ((g_big, g_sm), (d_big, d_sm), (m_big, m_sm), (v_big, v_sm)):
        tree = {**_unpack_shards(big, w), **_unpack_small(sm, w)}
        outs += [tree[n] for n in names]
    return (loss, grad_x[None], *outs)


def _pack_shards(tree, names, dtype):
    parts = []
    for name in names:
        rows, padded, transposed = W_SHARD[name]
        a = tree[name][0]
        a = a.T if transposed else a
        parts.append(jnp.pad(a, ((0, padded - rows), (0, 0))).astype(dtype))
    return jnp.concatenate(parts, axis=0)


def _unpack_shards(packed, names, like):
    out, off = {}, 0
    for name in names:
        rows, padded, transposed = W_SHARD[name]
        a = packed[off:off + rows]
        out[name] = (a.T if transposed else a)[None].reshape(like[name].shape)
        off += padded
    return out


def _unpack_gathered(gathered, names):
    out, off = {}, 0
    for name in names:
        rows, padded, _ = W_SHARD[name]
        out[_CANON.get(name, name)] = gathered[:, off:off + rows].reshape(N_DEV * rows, D)
        off += padded
    return out


def _pack_chunks(grads, names):
    chunks = []
    for name in names:
        rows, padded, _ = W_SHARD[name]
        g = grads[_CANON.get(name, name)].reshape(N_DEV, rows, D)
        chunks.append(jnp.pad(g, ((0, 0), (0, padded - rows), (0, 0))).astype(BF))
    return jnp.concatenate(chunks, axis=1)


def kernel(x, mem, g_mix, w_in, b_forget, g_ret_out, g_fox_q, g_fox_k, w_out, g_xattn, w_xq, w_xkv, g_mem, g_xq, g_xk, w_xo, g_ffn, w_gate, w_up, w_down, loss_target, m_g_mix, m_w_in, m_b_forget, m_g_ret_out, m_g_fox_q, m_g_fox_k, m_w_out, m_g_xattn, m_w_xq, m_w_xkv, m_g_mem, m_g_xq, m_g_xk, m_w_xo, m_g_ffn, m_w_gate, m_w_up, m_w_down, v_g_mix, v_w_in, v_b_forget, v_g_ret_out, v_g_fox_q, v_g_fox_k, v_w_out, v_g_xattn, v_w_xq, v_w_xkv, v_g_mem, v_g_xq, v_g_xk, v_w_xo, v_g_ffn, v_w_gate, v_w_up, v_w_down):
    names = ("g_mix", "w_in", "b_forget", "g_ret_out", "g_fox_q", "g_fox_k", "w_out", "g_xattn", "w_xq", "w_xkv", "g_mem",
             "g_xq", "g_xk", "w_xo", "g_ffn", "w_gate", "w_up", "w_down")
    w = dict(zip(names, (g_mix, w_in, b_forget, g_ret_out, g_fox_q, g_fox_k, w_out, g_xattn, w_xq, w_xkv, g_mem, g_xq, g_xk,
                         w_xo, g_ffn, w_gate, w_up, w_down)))
    m = dict(zip(names, (m_g_mix, m_w_in, m_b_forget, m_g_ret_out, m_g_fox_q, m_g_fox_k, m_w_out, m_g_xattn, m_w_xq, m_w_xkv,
                         m_g_mem, m_g_xq, m_g_xk, m_w_xo, m_g_ffn, m_w_gate, m_w_up, m_w_down)))
    v = dict(zip(names, (v_g_mix, v_w_in, v_b_forget, v_g_ret_out, v_g_fox_q, v_g_fox_k, v_w_out, v_g_xattn, v_w_xq, v_w_xkv,
                         v_g_mem, v_g_xq, v_g_xk, v_w_xo, v_g_ffn, v_w_gate, v_w_up, v_w_down)))
    small_names = [s[0] for s in _SMALL]
    me = 4 * lax.axis_index("x") + 2 * lax.axis_index("y") + lax.axis_index("c")

    first = _all_gather(_pack_shards(w, GATHER_FIRST, BF))
    first, rest_shard = lax.optimization_barrier((first, _pack_shards(w, GATHER_REST, BF)))
    rest_started = _exchange_start("gather_rest_start", rest_shard,
                                   jnp.broadcast_to(rest_shard[None], (N_DEV,) + rest_shard.shape), scatter=False)

    def fetch_rest(after):
        return _unpack_gathered(_exchange_wait("gather_rest_wait", rest_started, after, scatter=False)[1], GATHER_REST)

    pushed = {}

    def push(group, grads):
        send = _pack_chunks(grads, GRAD_GROUPS[group])
        pushed[group] = _exchange_start("scatter_%s_start" % group, send, jnp.zeros(send.shape, BF), scatter=True)
        return pushed[group][4]

    sp = {n: w[n].reshape(1, -1) for n in small_names}
    loss_part, grad_x, g_last, gs = _local_step(x[0], mem[0], loss_target[0], sp, _unpack_gathered(first, GATHER_FIRST)["w_inT"],
                                                rest_started[4], fetch_rest, push)

    small = lax.dynamic_update_slice(_pack_small(gs), loss_part[:, :1], _LOSS_AT)
    recv_mix, recv_small = _all_to_all(_pack_chunks(g_last, GRAD_GROUPS["mix"]), small)

    results = {}
    for group in ("ffn", "xattn", "mix"):
        gnames = GRAD_GROUPS[group]
        wp, mp, vp = (_pack_shards(t, gnames, F32) for t in (w, m, v))
        if group == "mix":
            res = _adamw("adamw_mix", recv_mix, wp, mp, vp, 16)
        else:
            sent, recv = _exchange_wait("scatter_%s_wait" % group, pushed[group], recv_small, scatter=True)
            own = lax.dynamic_index_in_dim(sent, me, axis=0, keepdims=False)
            res = _adamw("adamw_%s" % group, recv, wp, mp, vp, {"ffn": 176, "xattn": 128}[group], own=own)
        results[group] = [_unpack_shards(r, gnames, w) for r in res]
    g_sm, d_sm, m_sm, v_sm = _adamw("adamw_small", recv_small, _pack_small(w), _pack_small(m), _pack_small(v), SMALL_ROWS)
    loss = g_sm[_LOSS_AT[0], _LOSS_AT[1]]

    outs = []
    for k, sm in enumerate((g_sm, d_sm, m_sm, v_sm)):
        tree = _unpack_small(sm, w)
        for group in results:
            tree.update(results[group][k])
        outs += [tree[n] for n in names]
    return (loss, grad_x[None], *outs)


SCATTER_GROUPS = {"ffn": ("w_gate", "w_up", "w_down"), "xattn": ("w_xq", "w_xo", "w_xkv"), "out": ("w_out",), "in": ("w_in",)}


def _adam_update(g, w, m, v):
    m2 = ADAM_B1 * m + (1.0 - ADAM_B1) * g
    v2 = ADAM_B2 * v + (1.0 - ADAM_B2) * jnp.square(g)
    m_hat = m2 / (1.0 - ADAM_B1 ** ADAM_STEP)
    v_hat = v2 / (1.0 - ADAM_B2 ** ADAM_STEP)
    return g, -ADAM_LR * (m_hat / (jnp.sqrt(v_hat) + ADAM_EPS) + ADAM_WD * w), m2, v2


def _adamw_shard(name, recv, own, off, w, m, v):
    rows, padded, transposed = W_SHARD[name.split(":")[1]]
    assert off % padded == 0
    blk = off // padded

    def total(s_ref, own_ref):
        g = own_ref[...].astype(F32)
        for s in range(N_DEV):
            g = g + s_ref[s].astype(F32)
        return g

    canonical_view = name.endswith(":")
    if transposed and rows == padded and not canonical_view:
        res = _adamw_shard(name + ":", recv, own, off, *(jnp.swapaxes(a, 1, 2) for a in (w, m, v)))
        return [jnp.swapaxes(r, 1, 2) for r in res]

    if canonical_view or not transposed:
        def body(s_ref, own_ref, w_ref, m_ref, v_ref, *outs):
            for o, r in zip(outs, _adam_update(total(s_ref, own_ref), w_ref[0], m_ref[0], v_ref[0])):
                o[0] = r

        full = pl.BlockSpec((1, rows, D), lambda i: (0, 0, 0))
        return pl.pallas_call(
            body, name=name.replace(":", "_"), grid=(1,),
            in_specs=[pl.BlockSpec((N_DEV, padded, D), lambda i: (0, blk, 0)), pl.BlockSpec((padded, D), lambda i: (blk, 0)),
                      full, full, full],
            out_specs=[full] * 4, out_shape=[jax.ShapeDtypeStruct((1, rows, D), F32)] * 4,
            compiler_params=_params(("arbitrary",)),
        )(recv, own, w, m, v)

    wide = -(-padded // LANES) * LANES

    def body(s_ref, own_ref, w_ref, m_ref, v_ref, *outs):
        g = total(s_ref, own_ref)
        if wide > padded:
            g = jnp.concatenate([g, jnp.zeros((wide - padded, LANES), F32)], axis=0)
        g = g.T[:, :rows]
        for o, r in zip(outs, _adam_update(g, w_ref[0], m_ref[0], v_ref[0])):
            o[0] = r

    cols = pl.BlockSpec((1, LANES, rows), lambda c: (0, c, 0))
    return pl.pallas_call(
        body, name=name.replace(":", "_"), grid=(D // LANES,),
        in_specs=[pl.BlockSpec((N_DEV, padded, LANES), lambda c: (0, blk, c)), pl.BlockSpec((padded, LANES), lambda c: (blk, c)),
                  cols, cols, cols],
        out_specs=[cols] * 4, out_shape=[jax.ShapeDtypeStruct((1, D, rows), F32)] * 4,
        compiler_params=_params(("arbitrary",)),
    )(recv, own, w, m, v)


def _gather_small(small):
    def body(small_ref, out_ref, send_sems, recv_sems, local_sem):
        x, y, c = _place()
        me = 4 * x + 2 * y + c
        mine = pltpu.make_async_copy(small_ref, out_ref.at[me], local_sem)
        mine.start()
        copies = []
        for r in range(1, N_DEV):
            px, py, pc = x ^ (r >> 2), y ^ ((r >> 1) & 1), c ^ (r & 1)
            copies.append(pltpu.make_async_remote_copy(
                src_ref=small_ref, dst_ref=out_ref.at[me], send_sem=send_sems.at[r - 1], recv_sem=recv_sems.at[r - 1],
                device_id=(px, py, pc), device_id_type=MESH))
        for cp in copies:
            cp.start()
        for cp in copies:
            cp.wait_recv()
        for cp in copies:
            cp.wait_send()
        mine.wait()

    return pl.pallas_call(
        body, name="gather_small",
        out_shape=jax.ShapeDtypeStruct((N_DEV,) + small.shape, small.dtype),
        in_specs=[pl.BlockSpec(memory_space=pl.ANY)], out_specs=pl.BlockSpec(memory_space=pl.ANY),
        scratch_shapes=[pltpu.SemaphoreType.DMA((N_DEV - 1,)), pltpu.SemaphoreType.DMA((N_DEV - 1,)), pltpu.SemaphoreType.DMA],
    )(small)


def _pack_chunks(grads, names):
    chunks = []
    for name in names:
        rows, padded, _ = W_SHARD[name]
        g = grads[_CANON.get(name, name)].reshape(N_DEV, rows, D)
        chunks.append(jnp.pad(g, ((0, 0), (0, padded - rows), (0, 0))).astype(BF))
    return chunks[0] if len(chunks) == 1 else jnp.concatenate(chunks, axis=1)


def kernel(x, mem, g_mix, w_in, b_forget, g_ret_out, g_fox_q, g_fox_k, w_out, g_xattn, w_xq, w_xkv, g_mem, g_xq, g_xk, w_xo, g_ffn, w_gate, w_up, w_down, loss_target, m_g_mix, m_w_in, m_b_forget, m_g_ret_out, m_g_fox_q, m_g_fox_k, m_w_out, m_g_xattn, m_w_xq, m_w_xkv, m_g_mem, m_g_xq, m_g_xk, m_w_xo, m_g_ffn, m_w_gate, m_w_up, m_w_down, v_g_mix, v_w_in, v_b_forget, v_g_ret_out, v_g_fox_q, v_g_fox_k, v_w_out, v_g_xattn, v_w_xq, v_w_xkv, v_g_mem, v_g_xq, v_g_xk, v_w_xo, v_g_ffn, v_w_gate, v_w_up, v_w_down):
    names = ("g_mix", "w_in", "b_forget", "g_ret_out", "g_fox_q", "g_fox_k", "w_out", "g_xattn", "w_xq", "w_xkv", "g_mem",
             "g_xq", "g_xk", "w_xo", "g_ffn", "w_gate", "w_up", "w_down")
    w = dict(zip(names, (g_mix, w_in, b_forget, g_ret_out, g_fox_q, g_fox_k, w_out, g_xattn, w_xq, w_xkv, g_mem, g_xq, g_xk,
                         w_xo, g_ffn, w_gate, w_up, w_down)))
    m = dict(zip(names, (m_g_mix, m_w_in, m_b_forget, m_g_ret_out, m_g_fox_q, m_g_fox_k, m_w_out, m_g_xattn, m_w_xq, m_w_xkv,
                         m_g_mem, m_g_xq, m_g_xk, m_w_xo, m_g_ffn, m_w_gate, m_w_up, m_w_down)))
    v = dict(zip(names, (v_g_mix, v_w_in, v_b_forget, v_g_ret_out, v_g_fox_q, v_g_fox_k, v_w_out, v_g_xattn, v_w_xq, v_w_xkv,
                         v_g_mem, v_g_xq, v_g_xk, v_w_xo, v_g_ffn, v_w_gate, v_w_up, v_w_down)))
    small_names = [s[0] for s in _SMALL]
    me = 4 * lax.axis_index("x") + 2 * lax.axis_index("y") + lax.axis_index("c")

    first = _all_gather(_pack_shards(w, GATHER_FIRST, BF))
    first, rest_shard = lax.optimization_barrier((first, _pack_shards(w, GATHER_REST, BF)))
    rest_started = _exchange_start("gather_rest_start", rest_shard,
                                   jnp.broadcast_to(rest_shard[None], (N_DEV,) + rest_shard.shape), scatter=False)

    def fetch_rest(after):
        return _unpack_gathered(_exchange_wait("gather_rest_wait", rest_started, after, scatter=False)[1], GATHER_REST)

    pushed = {}

    def push(group, grads):
        send = _pack_chunks(grads, SCATTER_GROUPS[group])
        pushed[group] = _exchange_start("scatter_%s_start" % group, send, jnp.zeros(send.shape, BF), scatter=True)
        return pushed[group][4]

    def push_small(gs, loss_part):
        small = lax.dynamic_update_slice(_pack_small(gs), loss_part[:, :1], _LOSS_AT)
        pushed["small"] = _exchange_start("gather_small_start", small, jnp.broadcast_to(small[None], (N_DEV,) + small.shape),
                                          scatter=False)
        return pushed["small"][4]

    sp = {n: w[n].reshape(1, -1) for n in small_names}
    grad_x, done = _local_step(x[0], mem[0], loss_target[0], sp, _unpack_gathered(first, GATHER_FIRST)["w_inT"],
                               rest_started[4], fetch_rest, push, push_small)

    results, after = {}, done
    for group in ("ffn", "xattn", "out", "small", "in"):
        if group == "small":
            recv_small = _exchange_wait("gather_small_wait", pushed["small"], after, scatter=False)[1]
            g_sm, d_sm, m_sm, v_sm = _adamw("adamw_small", recv_small, _pack_small(w), _pack_small(m), _pack_small(v), SMALL_ROWS)
            after = g_sm
            continue
        sent, recv = _exchange_wait("scatter_%s_wait" % group, pushed[group], after, scatter=True)
        own = lax.dynamic_index_in_dim(sent, me, axis=0, keepdims=False)
        off = 0
        for name in SCATTER_GROUPS[group]:
            results[name] = _adamw_shard("adamw:" + name, recv, own, off, w[name], m[name], v[name])
            off += W_SHARD[name][1]
        after = results[SCATTER_GROUPS[group][-1]][0]
    loss = g_sm[_LOSS_AT[0], _LOSS_AT[1]]

    outs = []
    for k, sm in enumerate((g_sm, d_sm, m_sm, v_sm)):
        tree = _unpack_small(sm, w)
        tree.update({name: res[k] for name, res in results.items()})
        outs += [tree[n] for n in names]
    return (loss, grad_x[None], *outs)
```

```python
import functools
import math

import jax
import jax.numpy as jnp
import numpy as np
from jax import lax
from jax.experimental import pallas as pl
from jax.experimental.pallas import tpu as pltpu

F32 = jnp.float32
BF = jnp.bfloat16

D = 1024
HEAD = 64
CHUNK = 64
N_MEM = 256
XHEAD = 256
D_FF = 2816
EPS = 1e-6
NEG = -1e30
LANES = 128
N_DEV = 8
V7X_VMEM_BYTES = 64 * 1024 * 1024
VMEM_LIMIT = V7X_VMEM_BYTES - 8 * 1024 * 1024

ADAM_LR, ADAM_B1, ADAM_B2, ADAM_EPS, ADAM_WD, ADAM_STEP = 0.001, 0.9, 0.999, 1e-08, 0.01, 10

W_LAYOUT = (("w_in", 449, 464, True), ("w_out", 128, 128, False), ("w_xq", 128, 128, False), ("w_xkv", 256, 256, True),
            ("w_xo", 128, 128, False), ("w_gate", 352, 352, True), ("w_up", 352, 352, True), ("w_down", 352, 352, False))
W_ROWS = sum(w[2] for w in W_LAYOUT)
W_OFF = {}
_o = 0
for _n, _r, _p, _t in W_LAYOUT:
    W_OFF[_n] = _o
    _o += _p
SMALL_ROWS = 8
W_SHARD = {"w_in": (449, 449, True), "w_out": (128, 128, False), "w_xq": (128, 128, False), "w_xkv": (256, 256, True),
           "w_xo": (128, 128, False), "w_gate": (352, 352, True), "w_up": (352, 352, True), "w_down": (352, 352, False)}
GATHER_FIRST = ("w_in",)
GATHER_REST = ("w_out", "w_xq", "w_xkv", "w_xo", "w_gate", "w_up", "w_down")
GRAD_GROUPS = {"ffn": ("w_gate", "w_up", "w_down"), "xattn": ("w_xq", "w_xkv", "w_xo"), "mix": ("w_in", "w_out")}

NT = (((1,), (1,)), ((), ()))
NN = (((1,), (0,)), ((), ()))
TN = (((0,), (0,)), ((), ()))
_DIMS = {"nn": NN, "nt": NT, "tn": TN}


def _params(sem):
    return pltpu.CompilerParams(dimension_semantics=sem, vmem_limit_bytes=VMEM_LIMIT)


def _mm(name, products, extras, epilogue, M, N, tm, tn, out_dtypes, params=(), n_acc=0):
    assert n_acc == 0 or tn == N
    flat = [t for p in products for t in p]
    counts = [len(p) for p in products]
    in_specs, args = [], []
    for a, b, form in flat:
        if form == "tn":
            in_specs.append(pl.BlockSpec((a.shape[0], tm), lambda i, j: (0, i)))
        else:
            in_specs.append(pl.BlockSpec((tm, a.shape[1]), lambda i, j: (i, 0)))
        if form == "nt":
            in_specs.append(pl.BlockSpec((tn, b.shape[1]), lambda i, j: (j, 0)))
        else:
            in_specs.append(pl.BlockSpec((b.shape[0], tn), lambda i, j: (0, j)))
        args += [a, b]
    for e in extras:
        in_specs.append(pl.BlockSpec((tm, tn), lambda i, j: (i, j)))
        args.append(e)
    for p in params:
        in_specs.append(pl.BlockSpec((1, tn), lambda i, j: (0, j)))
        args.append(p)
    n_in = len(args)
    n_out = len(out_dtypes)

    def body(*refs):
        ins, outs = refs[:n_in], refs[n_in:]
        prods, p = [], 0
        for c in counts:
            acc = None
            for _ in range(c):
                a = ins[2 * p][...].astype(BF)
                b = ins[2 * p + 1][...].astype(BF)
                d = lax.dot_general(a, b, _DIMS[flat[p][2]], preferred_element_type=F32)
                acc = d if acc is None else acc + d
                p += 1
            prods.append(acc)
        ex = [r[...].astype(F32) for r in ins[2 * len(flat):]]
        res = epilogue(*prods, *ex)
        for o, r in zip(outs[:n_out], res[:n_out]):
            o[...] = r.astype(o.dtype)
        for o, r in zip(outs[n_out:], res[n_out:]):
            @pl.when(pl.program_id(0) == 0)
            def _(o=o):
                o[...] = jnp.zeros(o.shape, F32)
            o[...] += r

    return pl.pallas_call(
        body, name=name, grid=(M // tm, N // tn), in_specs=in_specs,
        out_specs=[pl.BlockSpec((tm, tn), lambda i, j: (i, j)) for _ in out_dtypes]
        + [pl.BlockSpec((1, tn), lambda i, j: (0, j)) for _ in range(n_acc)],
        out_shape=[jax.ShapeDtypeStruct((M, N), dt) for dt in out_dtypes] + [jax.ShapeDtypeStruct((1, N), F32)] * n_acc,
        compiler_params=_params(("arbitrary", "arbitrary")),
    )(*args)


def _ident(x):
    return (x,)


def _add(x, r):
    return (x + r,)


def _spec(rows, w, off, per_j):
    if per_j:
        return pl.BlockSpec((rows, w), lambda j, i: (i, off + j))
    return pl.BlockSpec((rows, w), lambda j, i: (i, off))


def _pspec(rows, w, off, per_j):
    if per_j:
        return pl.BlockSpec((rows, w), lambda j, i: (0, off + j))
    return pl.BlockSpec((rows, w), lambda j, i: (0, off))


def _rw_fwd(name, fn, rows, params, outs, T, tm, nj, n_acc=0):
    in_specs = [_spec(tm, w, off, pj) for _, w, off, pj in rows] + [_pspec(a.shape[0], w, off, pj) for a, w, off, pj in params]
    args = [r[0] for r in rows] + [p[0] for p in params]
    n_in, n_out = len(args), len(outs)
    out_specs = [pl.BlockSpec((tm, w), lambda j, i: (i, j)) for _, w in outs]
    out_shape = [jax.ShapeDtypeStruct((T, nj * w), dt) for dt, w in outs]
    out_specs += [pl.BlockSpec((1, LANES), lambda j, i: (0, 0)) for _ in range(n_acc)]
    out_shape += [jax.ShapeDtypeStruct((1, LANES), F32) for _ in range(n_acc)]

    def body(*refs):
        vals = [r[...].astype(F32) for r in refs[:n_in]]
        res = fn(*vals)
        orefs = refs[n_in:]
        for k in range(n_out):
            orefs[k][...] = res[k].astype(orefs[k].dtype)
        first = (pl.program_id(0) == 0) & (pl.program_id(1) == 0)
        for k in range(n_acc):
            @pl.when(first)
            def _(k=k):
                orefs[n_out + k][...] = jnp.zeros((1, LANES), F32)
            orefs[n_out + k][...] += res[n_out + k]

    return pl.pallas_call(
        body, name=name, grid=(nj, T // tm), in_specs=in_specs, out_specs=out_specs, out_shape=out_shape,
        compiler_params=_params(("arbitrary", "arbitrary")),
    )(*args)


def _rw_bwd(name, fn, rows, params, cots, T, tm, nj, row_grads, param_grads, resid=None):
    in_specs = ([_spec(tm, w, off, pj) for _, w, off, pj in rows] + [_pspec(a.shape[0], w, off, pj) for a, w, off, pj in params]
                + [_spec(tm, w, off, pj) for _, w, off, pj in cots])
    args = [r[0] for r in rows] + [p[0] for p in params] + [c[0] for c in cots]
    if resid is not None:
        in_specs.append(_spec(tm, rows[0][1], rows[0][2], rows[0][3]))
        args.append(resid)
    nr, npar, nc = len(rows), len(params), len(cots)
    out_specs, out_shape, kinds = [], [], []
    for k, dts in enumerate(row_grads):
        for dt in (dts if isinstance(dts, (list, tuple)) else [dts]):
            if dt is not None:
                w = rows[k][1]
                out_specs.append(pl.BlockSpec((tm, w), lambda j, i: (i, j)))
                out_shape.append(jax.ShapeDtypeStruct((T, nj * w), dt))
                kinds.append(("row", k))
    for k, need in enumerate(param_grads):
        if need:
            a, w, off, pj = params[k]
            out_specs.append(_pspec(a.shape[0], w, off, pj))
            out_shape.append(jax.ShapeDtypeStruct(a.shape, F32))
            kinds.append(("par", k))

    def body(*refs):
        vals = [r[...].astype(F32) for r in refs[:nr + npar]]
        ct = tuple(r[...].astype(F32) for r in refs[nr + npar:nr + npar + nc])
        _, vjp = jax.vjp(lambda *a: tuple(fn(*a)), *vals)
        grads = list(vjp(ct))
        n_in = nr + npar + nc + (resid is not None)
        if resid is not None:
            grads[0] = grads[0] + refs[n_in - 1][...].astype(F32)
        orefs = refs[n_in:]
        j, i = pl.program_id(0), pl.program_id(1)
        for o, (kind, k) in zip(orefs, kinds):
            if kind == "row":
                o[...] = grads[k].astype(o.dtype)
            else:
                first = (i == 0) if params[k][3] else ((i == 0) & (j == 0))

                @pl.when(first)
                def _(o=o):
                    o[...] = jnp.zeros(o.shape, F32)
                o[...] += grads[nr + k]

    return pl.pallas_call(
        body, name=name, grid=(nj, T // tm), in_specs=in_specs, out_specs=out_specs, out_shape=out_shape,
        compiler_params=_params(("arbitrary", "arbitrary")),
    )(*args)


def _rms(x, g):
    return x * lax.rsqrt(jnp.mean(x * x, axis=-1, keepdims=True) + EPS) * g


def _rms_fn(x, g):
    return (_rms(x, g),)


def _lo_mask():
    return lax.broadcasted_iota(jnp.int32, (1, LANES), 1) < HEAD


def _gmean(x, lo):
    s0 = jnp.sum(jnp.where(lo, x, 0.0), axis=-1, keepdims=True)
    s1 = jnp.sum(jnp.where(lo, 0.0, x), axis=-1, keepdims=True)
    return jnp.where(lo, s0, s1) * (1.0 / HEAD)


def _fox_prep_fn(fq, fk, gq, gk):
    lo = _lo_mask()
    qn = fq * lax.rsqrt(_gmean(fq * fq, lo) + EPS) * gq * (HEAD ** -0.5)
    kn = fk * lax.rsqrt(_gmean(fk * fk, lo) + EPS) * gk
    return qn, kn


def _cast_fn(v):
    return (v,)


@jax.custom_vjp
def _swap_halves(x):
    bit = (lax.broadcasted_iota(jnp.int32, (1, LANES), 1) & (HEAD // 2)) == 0
    return jnp.where(bit, pltpu.roll(x, LANES - HEAD // 2, 1), pltpu.roll(x, HEAD // 2, 1))


_swap_halves.defvjp(lambda x: (_swap_halves(x), None), lambda _, g: (_swap_halves(g),))


def _ret_fn(rq, rk, rv, rg, cos, sin, s_in, g, lg):
    tb = rq.shape[0]
    nc = tb // CHUNK
    lo = _lo_mask()
    row = lax.broadcasted_iota(jnp.int32, (LANES, 1), 0) < HEAD
    same_head = row == lo
    q = (rq * cos + _swap_halves(rq) * sin) * (HEAD ** -0.5)
    k = rk * cos + _swap_halves(rk) * sin
    q3, k3, v3 = q.reshape(nc, CHUNK, LANES), k.reshape(nc, CHUNK, LANES), rv.reshape(nc, CHUNK, LANES)
    pos = lax.broadcasted_iota(jnp.int32, (CHUNK, 1), 0).astype(F32)
    q_decay = jnp.exp(lg * (pos + 1.0))
    k_decay = jnp.exp(lg * (CHUNK - 1.0 - pos))
    chunk_decay = jnp.exp(lg * float(CHUNK))
    dist = jnp.abs(lax.broadcasted_iota(jnp.int32, (CHUNK, CHUNK), 0) - lax.broadcasted_iota(jnp.int32, (CHUNK, CHUNK), 1)).astype(F32)
    v3b = v3.astype(BF)
    intra = []
    for hh in range(2):
        hm = lo if hh == 0 else ~lo
        lg_h = lg[:, hh * HEAD:hh * HEAD + 1]
        qm = jnp.where(hm, q3, 0.0).astype(BF)
        sc = jnp.einsum("nid,njd->nij", qm, k3.astype(BF), preferred_element_type=F32) * jnp.exp(lg_h * dist)[None]
        intra.append(jnp.einsum("nij,nje->nie", sc.astype(BF), v3b, preferred_element_type=F32))
    o = jnp.where(lo, intra[0], intra[1])
    kv = jnp.einsum("njd,nje->nde", (k3 * k_decay[None]).astype(BF), v3b, preferred_element_type=F32)
    kv = jnp.where(same_head[None], kv, 0.0)
    state, states = s_in, []
    for n in range(nc):
        states.append(state)
        state = state * chunk_decay + kv[n]
    s_prev = jnp.stack(states, axis=0)
    o = o + jnp.einsum("nid,nde->nie", (q3 * q_decay[None]).astype(BF), s_prev.astype(BF), preferred_element_type=F32)
    o = o.reshape(tb, LANES)
    mu = _gmean(o, lo)
    oc = o - mu
    y = oc * lax.rsqrt(_gmean(oc * oc, lo) + EPS) * g
    return jax.nn.silu(rg) * y, state


def _xattn_fn(qx, gq, gk, kk, vv):
    q = _rms(qx, gq)
    k = _rms(kk, gk)
    logits = lax.dot_general(q.astype(BF), k.astype(BF), NT, preferred_element_type=F32) * (XHEAD ** -0.5)
    p = jax.nn.softmax(logits, axis=-1)
    return (jnp.dot(p.astype(BF), vv.astype(BF), preferred_element_type=F32),)


def _swiglu_fwd_epi(g, u):
    return g, u, jax.nn.silu(g) * u


def _swiglu_bwd_epi(dact, g, u):
    _, vjp = jax.vjp(lambda a, b: jax.nn.silu(a) * b, g, u)
    return vjp(dact)


def _add_rms_epi(acc, resid, g):
    h = acc + resid
    return h, _rms(h, g)


def _add_loss_epi(acc, resid, target):
    err = (acc + resid) - target
    dy = err * (1.0 / D)
    part = jnp.sum(jnp.sum(err * err, axis=0, keepdims=True), axis=1, keepdims=True) * (0.5 / D)
    return dy, dy, jnp.broadcast_to(part, (1, err.shape[1]))


def _rms_bwd_epi(dhn, h, skip, g):
    _, vjp = jax.vjp(_rms, h, g)
    dh, dg = vjp(dhn)
    dh = dh + skip
    return dh, dh, dg


def _loss_fn(h, target):
    err = h - target
    part = jnp.sum(jnp.sum(err * err, axis=0, keepdims=True), axis=-1, keepdims=True) * (0.5 / D)
    dy = err * (1.0 / D)
    return dy, dy, part


def _ret_fwd(P, cos, sin, g_ret, lg, T, tb):
    nb = T // tb

    def body(rq, rk, rv, rg, c, s, g, l, o_ref, s0_ref, state):
        @pl.when(pl.program_id(1) == 0)
        def _():
            state[...] = jnp.zeros(state.shape, F32)
        s0_ref[0, 0] = state[...]
        out, s_new = _ret_fn(rq[...], rk[...], rv[...], rg[...], c[...], s[...], state[...], g[...], l[...])
        o_ref[...] = out
        state[...] = s_new

    sec = lambda off: pl.BlockSpec((tb, LANES), lambda j, i: (i, off + j))
    tab = pl.BlockSpec((tb, LANES), lambda j, i: (i, 0))
    par = pl.BlockSpec((1, LANES), lambda j, i: (0, j))
    return pl.pallas_call(
        body, name="ret_fwd", grid=(4, nb),
        in_specs=[sec(0), sec(4), sec(8), sec(12), tab, tab, par, par],
        out_specs=[pl.BlockSpec((tb, LANES), lambda j, i: (i, j)), pl.BlockSpec((1, 1, LANES, LANES), lambda j, i: (j, i, 0, 0))],
        out_shape=[jax.ShapeDtypeStruct((T, 4 * LANES), F32), jax.ShapeDtypeStruct((4, nb, LANES, LANES), F32)],
        scratch_shapes=[pltpu.VMEM((LANES, LANES), F32)],
        compiler_params=_params(("arbitrary", "arbitrary")),
    )(P, P, P, P, cos, sin, g_ret, lg)


def _ret_bwd(P, cos, sin, g_ret, lg, s0, dmix, T, tb):
    nb = T // tb

    def body(rq, rk, rv, rg, c, s, g, l, s0_ref, do, drq, drk, drv, drg, dg, dstate):
        i = pl.program_id(1)

        @pl.when(i == 0)
        def _():
            dstate[...] = jnp.zeros(dstate.shape, F32)
            dg[...] = jnp.zeros(dg.shape, F32)

        cc, ss, ll = c[...], s[...], l[...]
        _, vjp = jax.vjp(lambda a, b, v, gate, st, gg: _ret_fn(a, b, v, gate, cc, ss, st, gg, ll),
                         rq[...], rk[...], rv[...], rg[...], s0_ref[0, 0], g[...])
        ga, gb, gv, ggate, gst, ggain = vjp((do[...], dstate[...]))
        drq[...] = ga.astype(drq.dtype)
        drk[...] = gb.astype(drk.dtype)
        drv[...] = gv.astype(drv.dtype)
        drg[...] = ggate.astype(drg.dtype)
        dstate[...] = gst
        dg[...] += ggain

    rev = lambda i: nb - 1 - i
    sec = lambda off: pl.BlockSpec((tb, LANES), lambda j, i: (rev(i), off + j))
    tab = pl.BlockSpec((tb, LANES), lambda j, i: (rev(i), 0))
    par = pl.BlockSpec((1, LANES), lambda j, i: (0, j))
    outb = pl.BlockSpec((tb, LANES), lambda j, i: (rev(i), j))
    return pl.pallas_call(
        body, name="ret_bwd", grid=(4, nb),
        in_specs=[sec(0), sec(4), sec(8), sec(12), tab, tab, par, par,
                  pl.BlockSpec((1, 1, LANES, LANES), lambda j, i: (j, rev(i), 0, 0)), outb],
        out_specs=[outb, outb, outb, outb, par],
        out_shape=[jax.ShapeDtypeStruct((T, 4 * LANES), BF)] * 4 + [jax.ShapeDtypeStruct((1, 4 * LANES), F32)],
        scratch_shapes=[pltpu.VMEM((LANES, LANES), F32)],
        compiler_params=_params(("arbitrary", "arbitrary")),
    )(P, P, P, P, cos, sin, g_ret, lg, s0, dmix)


_FB = 128


def _tri(lower):
    r = lax.broadcasted_iota(jnp.int32, (_FB, _FB), 0)
    c = lax.broadcasted_iota(jnp.int32, (_FB, _FB), 1)
    return ((r >= c) if lower else (r <= c)).astype(F32)


def _fgate_fwd(ffp, bpad, T):
    def body(ff_ref, b_ref, fc_ref, fr_ref):
        lane = lax.broadcasted_iota(jnp.int32, (1, LANES), 1)
        tri = _tri(True)
        carry = jnp.zeros((1, LANES), F32)
        for blk in range(T // _FB):
            z = ff_ref[blk * _FB:(blk + 1) * _FB, :] + b_ref[...]
            lf = jnp.where(lane < 8, jax.nn.log_sigmoid(z), 0.0)
            f = jnp.dot(tri, lf, precision=lax.Precision.HIGHEST, preferred_element_type=F32) + carry
            carry = f[_FB - 1:_FB, :]
            fc_ref[blk * _FB:(blk + 1) * _FB, :] = f
            fr_ref[:, blk * _FB:(blk + 1) * _FB] = f.T[:8, :]

    return pl.pallas_call(
        body, name="fgate_fwd",
        out_shape=[jax.ShapeDtypeStruct((T, LANES), F32), jax.ShapeDtypeStruct((8, T), F32)],
        compiler_params=pltpu.CompilerParams(vmem_limit_bytes=VMEM_LIMIT),
    )(ffp, bpad)


def _fgate_bwd(ffp, bpad, dfr, T):
    def body(ff_ref, b_ref, dfr_ref, dff_ref, db_ref):
        lane = lax.broadcasted_iota(jnp.int32, (1, LANES), 1)
        tri = _tri(False)
        carry = jnp.zeros((1, LANES), F32)
        db = jnp.zeros((1, LANES), F32)
        for blk in reversed(range(T // _FB)):
            d8 = dfr_ref[:, blk * _FB:(blk + 1) * _FB]
            dcol = jnp.concatenate([d8, jnp.zeros((_FB - 8, _FB), F32)], axis=0).T
            dlf = jnp.dot(tri, dcol, precision=lax.Precision.HIGHEST, preferred_element_type=F32) + carry
            carry = dlf[0:1, :]
            z = ff_ref[blk * _FB:(blk + 1) * _FB, :] + b_ref[...]
            dz = jnp.where(lane < 8, dlf * jax.nn.sigmoid(-z), 0.0)
            dff_ref[blk * _FB:(blk + 1) * _FB, :] = dz.astype(dff_ref.dtype)
            db = db + jnp.sum(dz, axis=0, keepdims=True)
        db_ref[...] = db

    return pl.pallas_call(
        body, name="fgate_bwd",
        out_shape=[jax.ShapeDtypeStruct((T, LANES), BF), jax.ShapeDtypeStruct((1, LANES), F32)],
        compiler_params=pltpu.CompilerParams(vmem_limit_bytes=VMEM_LIMIT),
    )(ffp, bpad, dfr)


def _head_bias_col(fc, head):
    lane = lax.broadcasted_iota(jnp.int32, (1, LANES), 1)
    return jnp.sum(jnp.where(lane == head, fc, 0.0), axis=-1, keepdims=True)


def _head_bias_row(fr, head):
    sub = lax.broadcasted_iota(jnp.int32, (8, 1), 0)
    return jnp.sum(jnp.where(sub == head, fr, 0.0), axis=0, keepdims=True)


def _fox_fwd(qn, kn, vb, fc, fr, T, tq):
    nq = T // tq

    def body(q_ref, k_ref, v_ref, fc_ref, fr_ref, o_ref, c_ref):
        j, i = pl.program_id(0), pl.program_id(1)
        lane = lax.broadcasted_iota(jnp.int32, (1, LANES), 1)
        lo = lane < HEAD
        causal = lax.broadcasted_iota(jnp.int32, (tq, tq), 0) >= lax.broadcasted_iota(jnp.int32, (tq, tq), 1)
        q = q_ref[...]
        fcb = fc_ref[...]
        outs, cs = [], []
        for hh in range(2):
            hm = lo if hh == 0 else ~lo
            head = 2 * j + hh
            qh = jnp.where(hm, q, jnp.zeros_like(q))
            fq = _head_bias_col(fcb, head)

            def block(kb, carry, diag, qh=qh, fq=fq, head=head):
                m, l, acc = carry
                k0 = pl.multiple_of(kb * tq, tq)
                k = k_ref[pl.ds(k0, tq), :]
                v = v_ref[pl.ds(k0, tq), :]
                fk = _head_bias_row(fr_ref[:, pl.ds(k0, tq)], head)
                s = (lax.dot_general(qh, k, NT, preferred_element_type=F32) + fq) - fk
                if diag:
                    s = jnp.where(causal, s, NEG)
                m2 = jnp.maximum(m, jnp.max(s, axis=-1, keepdims=True))
                p = jnp.exp(s - m2)
                a = jnp.exp(m - m2)
                return m2, a * l + jnp.sum(p, axis=-1, keepdims=True), a * acc + jnp.dot(p.astype(BF), v, preferred_element_type=F32)

            init = (jnp.full((tq, 1), NEG, F32), jnp.zeros((tq, 1), F32), jnp.zeros((tq, LANES), F32))
            carry = lax.fori_loop(0, i, lambda kb, c: block(kb, c, False), init)
            m, l, acc = block(i, carry, True)
            outs.append(acc / l)
            cs.append(fq - (m + jnp.log(l)))
        o_ref[...] = jnp.where(lo, outs[0], outs[1])
        c_ref[0] = jnp.where(lane == 0, cs[0], jnp.where(lane == 1, cs[1], 0.0))

    full = lambda: pl.BlockSpec((T, LANES), lambda j, i: (0, j))
    return pl.pallas_call(
        body, name="fox_fwd", grid=(4, nq),
        in_specs=[pl.BlockSpec((tq, LANES), lambda j, i: (i, j)), full(), full(),
                  pl.BlockSpec((tq, LANES), lambda j, i: (i, 0)), pl.BlockSpec((8, T), lambda j, i: (0, 0))],
        out_specs=[pl.BlockSpec((tq, LANES), lambda j, i: (i, j)), pl.BlockSpec((1, tq, LANES), lambda j, i: (j, i, 0))],
        out_shape=[jax.ShapeDtypeStruct((T, 4 * LANES), F32), jax.ShapeDtypeStruct((4, T, LANES), F32)],
        compiler_params=_params(("parallel", "arbitrary")),
    )(qn, kn, vb, fc, fr)


def _fox_bwd_dq(qn, kn, vb, fr, cq, dmix, T, tq):
    nq = T // tq

    def body(q_ref, k_ref, v_ref, fr_ref, c_ref, do_ref, dq_ref, dl_ref, p_scr, dp_scr):
        j, i = pl.program_id(0), pl.program_id(1)
        lane = lax.broadcasted_iota(jnp.int32, (1, LANES), 1)
        lo = lane < HEAD
        causal = lax.broadcasted_iota(jnp.int32, (tq, tq), 0) >= lax.broadcasted_iota(jnp.int32, (tq, tq), 1)
        q, do, cb = q_ref[...], do_ref[...], c_ref[0]
        res, deltas = [], []
        for hh in range(2):
            hm = lo if hh == 0 else ~lo
            head = 2 * j + hh
            qh = jnp.where(hm, q, jnp.zeros_like(q))
            doh = jnp.where(hm, do, 0.0).astype(BF)
            c = cb[:, hh:hh + 1]

            def probs(kb, delta, diag, qh=qh, doh=doh, c=c, head=head):
                k0 = pl.multiple_of(kb * tq, tq)
                k = k_ref[pl.ds(k0, tq), :]
                v = v_ref[pl.ds(k0, tq), :]
                fk = _head_bias_row(fr_ref[:, pl.ds(k0, tq)], head)
                p = jnp.exp((lax.dot_general(qh, k, NT, preferred_element_type=F32) + c) - fk)
                if diag:
                    p = jnp.where(causal, p, 0.0)
                dp = lax.dot_general(doh, v, NT, preferred_element_type=F32)
                p_scr[:, pl.ds(k0, tq)] = p
                dp_scr[:, pl.ds(k0, tq)] = dp
                return delta + jnp.sum(p * dp, axis=-1, keepdims=True)

            delta = lax.fori_loop(0, i, lambda kb, d: probs(kb, d, False), jnp.zeros((tq, 1), F32))
            delta = probs(i, delta, True)

            def grad(kb, acc, delta=delta):
                k0 = pl.multiple_of(kb * tq, tq)
                ds = p_scr[:, pl.ds(k0, tq)] * (dp_scr[:, pl.ds(k0, tq)] - delta)
                return acc + jnp.dot(ds.astype(BF), k_ref[pl.ds(k0, tq), :], preferred_element_type=F32)

            res.append(lax.fori_loop(0, i + 1, grad, jnp.zeros((tq, LANES), F32)))
            deltas.append(delta)
        dq_ref[...] = jnp.where(lo, res[0], res[1])
        dl_ref[0] = jnp.where(lane == 0, deltas[0], jnp.where(lane == 1, deltas[1], 0.0))

    full = lambda: pl.BlockSpec((T, LANES), lambda j, i: (0, j))
    return pl.pallas_call(
        body, name="fox_bwd_dq", grid=(4, nq),
        in_specs=[pl.BlockSpec((tq, LANES), lambda j, i: (i, j)), full(), full(), pl.BlockSpec((8, T), lambda j, i: (0, 0)),
                  pl.BlockSpec((1, tq, LANES), lambda j, i: (j, i, 0)), pl.BlockSpec((tq, LANES), lambda j, i: (i, 4 + j))],
        out_specs=[pl.BlockSpec((tq, LANES), lambda j, i: (i, j)), pl.BlockSpec((1, tq, LANES), lambda j, i: (j, i, 0))],
        out_shape=[jax.ShapeDtypeStruct((T, 4 * LANES), F32), jax.ShapeDtypeStruct((4, T, LANES), F32)],
        scratch_shapes=[pltpu.VMEM((tq, T), F32), pltpu.VMEM((tq, T), F32)],
        compiler_params=_params(("parallel", "arbitrary")),
    )(qn, kn, vb, fr, cq, dmix)


def _fox_bwd_dkv(qn, kn, vb, fr, cq, dl, dmix, T, tq):
    nq = T // tq

    def body(q_ref, k_ref, v_ref, fr_ref, c_ref, dl_ref, do_ref, dk_ref, dv_ref, dfr_ref):
        j, kb = pl.program_id(0), pl.program_id(1)
        lo = _lo_mask()
        sub = lax.broadcasted_iota(jnp.int32, (8, 1), 0)
        causal = lax.broadcasted_iota(jnp.int32, (tq, tq), 0) >= lax.broadcasted_iota(jnp.int32, (tq, tq), 1)
        k, v, frb = k_ref[...], v_ref[...], fr_ref[...]
        dks, dvs, dfs = [], [], []
        for hh in range(2):
            hm = lo if hh == 0 else ~lo
            head = 2 * j + hh
            km = jnp.where(hm, k, jnp.zeros_like(k))
            vm = jnp.where(hm, v, jnp.zeros_like(v))
            fk = _head_bias_row(frb, head)

            def block(qi, carry, diag, km=km, vm=vm, fk=fk, hm=hm, hh=hh):
                dk, dv, df = carry
                q0 = pl.multiple_of(qi * tq, tq)
                q = q_ref[pl.ds(q0, tq), :]
                c = c_ref[0, pl.ds(q0, tq), :][:, hh:hh + 1]
                delta = dl_ref[0, pl.ds(q0, tq), :][:, hh:hh + 1]
                dob = do_ref[pl.ds(q0, tq), :].astype(BF)
                p = jnp.exp((lax.dot_general(q, km, NT, preferred_element_type=F32) + c) - fk)
                if diag:
                    p = jnp.where(causal, p, 0.0)
                dv = dv + lax.dot_general(p.astype(BF), dob, TN, preferred_element_type=F32)
                dp = lax.dot_general(dob, vm, NT, preferred_element_type=F32)
                ds = p * (dp - delta)
                dk = dk + lax.dot_general(ds.astype(BF), q, TN, preferred_element_type=F32)
                return dk, dv, df - jnp.sum(ds, axis=0, keepdims=True)

            init = (jnp.zeros((tq, LANES), F32), jnp.zeros((tq, LANES), F32), jnp.zeros((1, tq), F32))
            carry = block(kb, init, True)
            dk, dv, df = lax.fori_loop(kb + 1, nq, lambda qi, cr: block(qi, cr, False), carry)
            dks.append(dk)
            dvs.append(dv)
            dfs.append(df)
        dk_ref[...] = jnp.where(lo, dks[0], dks[1])
        dv_ref[...] = jnp.where(lo, dvs[0], dvs[1]).astype(dv_ref.dtype)
        dfr_ref[0] = jnp.where(sub == 0, dfs[0], jnp.where(sub == 1, dfs[1], 0.0))

    full = lambda off: pl.BlockSpec((T, LANES), lambda j, kb: (0, off + j))
    blk = lambda: pl.BlockSpec((tq, LANES), lambda j, kb: (kb, j))
    return pl.pallas_call(
        body, name="fox_bwd_dkv", grid=(4, nq),
        in_specs=[full(0), blk(), blk(), pl.BlockSpec((8, tq), lambda j, kb: (0, kb)),
                  pl.BlockSpec((1, T, LANES), lambda j, kb: (j, 0, 0)), pl.BlockSpec((1, T, LANES), lambda j, kb: (j, 0, 0)), full(4)],
        out_specs=[blk(), blk(), pl.BlockSpec((1, 8, tq), lambda j, kb: (j, 0, kb))],
        out_shape=[jax.ShapeDtypeStruct((T, 4 * LANES), F32), jax.ShapeDtypeStruct((T, 4 * LANES), BF),
                   jax.ShapeDtypeStruct((4, 8, T), F32)],
        compiler_params=_params(("parallel", "arbitrary")),
    )(qn, kn, vb, fr, cq, dl, dmix)


_BIAS_LANE = HEAD


def _split3(f):
    hi = f.astype(BF).astype(F32)
    mid = (f - hi).astype(BF).astype(F32)
    lo = ((f - hi) - mid).astype(BF).astype(F32)
    return hi, mid, lo


def _fox_operands(P, fc, g_fq2, g_fk2, T, tm):
    def body(fq_ref, fk_ref, fv_ref, fc_ref, gq_ref, gk_ref, qa_ref, qat_ref, ka_ref, kat_ref, va_ref, vat_ref):
        j = pl.program_id(0)
        lane = lax.broadcasted_iota(jnp.int32, (1, LANES), 1)
        qn, kn = _fox_prep_fn(fq_ref[...], fk_ref[...], gq_ref[...], gk_ref[...])
        v = fv_ref[...]
        fcb = fc_ref[...]
        b = _BIAS_LANE
        for hh in range(2):
            hi, mid, lo = _split3(_head_bias_col(fcb, 2 * j + hh))
            take = (lambda a: a) if hh == 0 else (lambda a: pltpu.roll(a, HEAD, 1))
            qa = jnp.where(lane < HEAD, take(qn), jnp.where(lane == b, hi, jnp.where(lane == b + 1, mid, jnp.where(
                lane == b + 2, lo, jnp.where(lane < b + 6, 1.0, 0.0)))))
            ka = jnp.where(lane < HEAD, take(kn), jnp.where(lane < b + 3, 1.0, jnp.where(lane == b + 3, -hi, jnp.where(
                lane == b + 4, -mid, jnp.where(lane == b + 5, -lo, 0.0)))))
            va = jnp.where(lane < HEAD, take(v), 0.0)
            for val, ref, tref in ((qa, qa_ref, qat_ref), (ka, ka_ref, kat_ref), (va, va_ref, vat_ref)):
                ref[hh] = val.astype(BF)
                tref[hh] = val.T.astype(BF)

    sec = lambda off: pl.BlockSpec((tm, LANES), lambda j, i: (i, off + j))
    par = pl.BlockSpec((1, LANES), lambda j, i: (0, 0))
    nat = pl.BlockSpec((2, tm, LANES), lambda j, i: (j, i, 0))
    trn = pl.BlockSpec((2, LANES, tm), lambda j, i: (j, 0, i))
    return pl.pallas_call(
        body, name="fox_operands", grid=(4, T // tm),
        in_specs=[sec(16), sec(20), sec(24), pl.BlockSpec((tm, LANES), lambda j, i: (i, 0)), par, par],
        out_specs=[nat, trn, nat, trn, nat, trn],
        out_shape=[jax.ShapeDtypeStruct((8, T, LANES), BF), jax.ShapeDtypeStruct((8, LANES, T), BF)] * 3,
        compiler_params=_params(("parallel", "arbitrary")),
    )(P, P, P, fc, g_fq2, g_fk2)


def _fox_forward(qat, ka, vat, T, tq, tk):
    nq, per = T // tq, tq // tk

    def body(qat_ref, ka_ref, vat_ref, o_ref, lse_ref):
        i = pl.program_id(1)
        sub = lax.broadcasted_iota(jnp.int32, (8, 1), 0)
        krow = lax.broadcasted_iota(jnp.int32, (tk, tq), 0)
        qcol = lax.broadcasted_iota(jnp.int32, (tk, tq), 1)

        def scores(kb):
            k0 = pl.multiple_of(kb * tk, tk)
            return tuple(jnp.dot(ka_ref[hh, pl.ds(k0, tk), :], qat_ref[hh], preferred_element_type=F32) for hh in range(2))

        def step(kb, carry, mask, last=False):
            stats, s_now = carry
            s_next = s_now if last else scores(kb + 1)
            k0 = pl.multiple_of(kb * tk, tk)
            new = []
            for hh in range(2):
                m, l, acc = stats[hh]
                s = s_now[hh] if mask is None else jnp.where(mask, s_now[hh], NEG)
                m2 = jnp.maximum(m, jnp.max(s, axis=0, keepdims=True))
                p = jnp.exp(s - m2)
                a = jnp.exp(m - m2)
                pv = jnp.dot(vat_ref[hh, 0:HEAD, pl.ds(k0, tk)], p.astype(BF), preferred_element_type=F32)
                new.append((m2, a * l + jnp.sum(p, axis=0, keepdims=True), a * acc + pv))
            return tuple(new), s_next

        one = (jnp.full((1, tq), NEG, F32), jnp.zeros((1, tq), F32), jnp.zeros((HEAD, tq), F32))
        carry = lax.fori_loop(0, i * per, lambda kb, c: step(kb, c, None), ((one, one), scores(0)))
        for d in range(per):
            carry = step(i * per + d, carry, krow + d * tk <= qcol, last=(d == per - 1))
        stats = carry[0]
        o_ref[...] = jnp.concatenate([acc / l for _, l, acc in stats], axis=0).T
        lses = [m + jnp.log(l) for m, l, _ in stats]
        lse_ref[0] = jnp.where(sub == 0, lses[0], jnp.where(sub == 1, lses[1], 0.0))

    return pl.pallas_call(
        body, name="fox_forward", grid=(4, nq),
        in_specs=[pl.BlockSpec((2, LANES, tq), lambda j, i: (j, 0, i)), pl.BlockSpec((2, T, LANES), lambda j, i: (j, 0, 0)),
                  pl.BlockSpec((2, LANES, T), lambda j, i: (j, 0, 0))],
        out_specs=[pl.BlockSpec((tq, LANES), lambda j, i: (i, j)), pl.BlockSpec((1, 8, tq), lambda j, i: (j, 0, i))],
        out_shape=[jax.ShapeDtypeStruct((T, 4 * LANES), F32), jax.ShapeDtypeStruct((4, 8, T), F32)],
        compiler_params=_params(("parallel", "arbitrary")),
    )(qat, ka, vat)


def _fox_cotangent(dmix, fox, T, tm):
    def body(do_ref, o_ref, doa_ref, doat_ref, dl_ref):
        lane = lax.broadcasted_iota(jnp.int32, (1, LANES), 1)
        sub = lax.broadcasted_iota(jnp.int32, (8, 1), 0)
        dob = do_ref[...].astype(BF).astype(F32)
        prod_t = (dob * o_ref[...]).T
        d0 = jnp.sum(prod_t[:HEAD], axis=0, keepdims=True)
        d1 = jnp.sum(prod_t[HEAD:], axis=0, keepdims=True)
        dl_ref[0] = jnp.where(sub == 0, d0, jnp.where(sub == 1, d1, 0.0))
        for hh in range(2):
            val = jnp.where(lane < HEAD, dob if hh == 0 else pltpu.roll(dob, HEAD, 1), 0.0)
            doa_ref[hh] = val.astype(BF)
            doat_ref[hh] = val.T.astype(BF)

    return pl.pallas_call(
        body, name="fox_cotangent", grid=(4, T // tm),
        in_specs=[pl.BlockSpec((tm, LANES), lambda j, i: (i, 4 + j)), pl.BlockSpec((tm, LANES), lambda j, i: (i, j))],
        out_specs=[pl.BlockSpec((2, tm, LANES), lambda j, i: (j, i, 0)), pl.BlockSpec((2, LANES, tm), lambda j, i: (j, 0, i)),
                   pl.BlockSpec((1, 8, tm), lambda j, i: (j, 0, i))],
        out_shape=[jax.ShapeDtypeStruct((8, T, LANES), BF), jax.ShapeDtypeStruct((8, LANES, T), BF),
                   jax.ShapeDtypeStruct((4, 8, T), F32)],
        compiler_params=_params(("parallel", "arbitrary")),
    )(dmix, fox)


def _fox_backward(qa, qat, ka, kat, va, doa, doat, lse, dl, T, tq, tk):
    nq, nk = T // tq, T // tk

    def body(qa_ref, qat_ref, ka_ref, kat_ref, va_ref, doa_ref, doat_ref, lse_ref, dl_ref,
             dq_ref, dk_ref, dv_ref, df_ref, dr_ref, dqt, dk_acc, dv_acc, df_acc, sdp, pds):
        j, kb = pl.program_id(0), pl.program_id(1)
        lane = lax.broadcasted_iota(jnp.int32, (1, LANES), 1)
        first = (kb * tk) // tq

        @pl.when(kb == 0)
        def _():
            dqt[...] = jnp.zeros(dqt.shape, F32)

        dk_acc[...] = jnp.zeros(dk_acc.shape, F32)
        dv_acc[...] = jnp.zeros(dv_acc.shape, F32)
        df_acc[...] = jnp.zeros(df_acc.shape, F32)

        RC = 64
        last = nq - 1

        def products(slot, qi):
            q0 = pl.multiple_of(qi * tq, tq)
            for hh in range(2):
                sdp[slot, hh, 0] = jnp.dot(ka_ref[hh], qat_ref[hh, :, pl.ds(q0, tq)], preferred_element_type=F32)
                sdp[slot, hh, 1] = jnp.dot(va_ref[hh], doat_ref[hh, :, pl.ds(q0, tq)], preferred_element_type=F32)

        def softmax_bwd(slot, qi, diagonal, valid):
            q0 = pl.multiple_of(qi * tq, tq)
            shift = kb * tk - first * tq
            col = lax.broadcasted_iota(jnp.int32, (RC, tq), 1)
            row = lax.broadcasted_iota(jnp.int32, (RC, tq), 0)
            for hh in range(2):
                lse_row = lse_ref[0, hh:hh + 1, pl.ds(q0, tq)]
                dl_row = dl_ref[0, hh:hh + 1, pl.ds(q0, tq)]
                rsum = jnp.zeros((1, tq), F32)
                for r in range(tk // RC):
                    rows = slice(r * RC, (r + 1) * RC)
                    p = jnp.exp(sdp[slot, hh, 0, rows, :] - lse_row)
                    p = jnp.where((row + (r * RC + shift) <= col) if diagonal else valid, p, 0.0)
                    ds = p * (sdp[slot, hh, 1, rows, :] - dl_row)
                    pds[slot, hh, 0, rows, :] = p.astype(BF)
                    pds[slot, hh, 1, rows, :] = ds.astype(BF)
                    rsum = rsum + jnp.sum(ds, axis=0, keepdims=True)
                    part = ds[:, 0:LANES]
                    for c in range(1, tq // LANES):
                        part = part + ds[:, c * LANES:(c + 1) * LANES]
                    df_acc[hh, rows, :] += part
                dqt[hh, HEAD:HEAD + 8, pl.ds(q0, tq)] += jnp.broadcast_to(rsum, (8, tq))

        def accumulate(slot, qi):
            q0 = pl.multiple_of(qi * tq, tq)
            for hh in range(2):
                dv_acc[hh] += jnp.dot(pds[slot, hh, 0], doa_ref[hh, pl.ds(q0, tq), :], preferred_element_type=F32)
                dk_acc[hh] += jnp.dot(pds[slot, hh, 1], qa_ref[hh, pl.ds(q0, tq), :], preferred_element_type=F32)
                dqt[hh, 0:HEAD, pl.ds(q0, tq)] += jnp.dot(kat_ref[hh, 0:HEAD, :], pds[slot, hh, 1], preferred_element_type=F32)

        products(0, first)
        products(1, jnp.minimum(first + 1, last))
        softmax_bwd(0, first, True, None)

        @pl.loop(0, (nq - first + 1) // 2)
        def _(t):
            qi = first + 2 * t
            products(0, jnp.minimum(qi + 2, last))
            softmax_bwd(1, jnp.minimum(qi + 1, last), False, qi + 1 <= last)
            accumulate(0, qi)
            products(1, jnp.minimum(qi + 3, last))
            softmax_bwd(0, jnp.minimum(qi + 2, last), False, qi + 2 <= last)
            accumulate(1, jnp.minimum(qi + 1, last))

        lo = lane < HEAD
        dk_ref[...] = jnp.where(lo, dk_acc[0], pltpu.roll(dk_acc[1], HEAD, 1))
        dv_ref[...] = jnp.where(lo, dv_acc[0], pltpu.roll(dv_acc[1], HEAD, 1)).astype(dv_ref.dtype)
        f0 = -jnp.sum(df_acc[0], axis=1, keepdims=True)
        f1 = -jnp.sum(df_acc[1], axis=1, keepdims=True)
        df_ref[0] = jnp.where(lane == 2 * j, f0, jnp.where(lane == 2 * j + 1, f1, 0.0))

        @pl.when(kb == nk - 1)
        def _():
            for t in range(nq):
                cols = slice(t * tq, (t + 1) * tq)
                dq_ref[cols, :] = jnp.concatenate([dqt[0, 0:HEAD, cols], dqt[1, 0:HEAD, cols]], axis=0).T
                rsum = jnp.concatenate([dqt[0, HEAD:HEAD + 8, cols], dqt[1, HEAD:HEAD + 8, cols],
                                        jnp.zeros((LANES - 16, tq), F32)], axis=0).T
                dr_ref[0, cols, :] = jnp.where(lane == 2 * j, rsum[:, 0:1], jnp.where(lane == 2 * j + 1, rsum[:, 8:9], 0.0))

    nat_full = pl.BlockSpec((2, T, LANES), lambda j, kb: (j, 0, 0))
    trn_full = pl.BlockSpec((2, LANES, T), lambda j, kb: (j, 0, 0))
    nat_blk = pl.BlockSpec((2, tk, LANES), lambda j, kb: (j, kb, 0))
    trn_blk = pl.BlockSpec((2, LANES, tk), lambda j, kb: (j, 0, kb))
    rows = pl.BlockSpec((1, 8, T), lambda j, kb: (j, 0, 0))
    blk = pl.BlockSpec((tk, LANES), lambda j, kb: (kb, j))
    return pl.pallas_call(
        body, name="fox_backward", grid=(4, nk),
        in_specs=[nat_full, trn_full, nat_blk, trn_blk, nat_blk, nat_full, trn_full, rows, rows],
        out_specs=[pl.BlockSpec((T, LANES), lambda j, kb: (0, j)), blk, blk, pl.BlockSpec((1, tk, LANES), lambda j, kb: (j, kb, 0)),
                   pl.BlockSpec((1, T, LANES), lambda j, kb: (j, 0, 0))],
        out_shape=[jax.ShapeDtypeStruct((T, 4 * LANES), F32), jax.ShapeDtypeStruct((T, 4 * LANES), F32),
                   jax.ShapeDtypeStruct((T, 4 * LANES), BF), jax.ShapeDtypeStruct((4, T, LANES), F32),
                   jax.ShapeDtypeStruct((4, T, LANES), F32)],
        scratch_shapes=[pltpu.VMEM((2, HEAD + 8, T), F32), pltpu.VMEM((2, tk, LANES), F32), pltpu.VMEM((2, tk, LANES), F32),
                        pltpu.VMEM((2, tk, LANES), F32), pltpu.VMEM((2, 2, 2, tk, tq), F32), pltpu.VMEM((2, 2, 2, tk, tq), BF)],
        compiler_params=_params(("arbitrary", "arbitrary")),
    )(qa, qat, ka, kat, va, doa, doat, lse, dl)


def _fgate_bwd_col(ffp, bpad, dfc4, drc4, T):
    def body(ff_ref, b_ref, dfc_ref, drc_ref, dff_ref, db_ref):
        lane = lax.broadcasted_iota(jnp.int32, (1, LANES), 1)
        tri = _tri(False)
        carry = jnp.zeros((1, LANES), F32)
        db = jnp.zeros((1, LANES), F32)
        for blk in reversed(range(T // _FB)):
            rows = slice(blk * _FB, (blk + 1) * _FB)
            dcol = dfc_ref[0, rows, :] + drc_ref[0, rows, :]
            for pair in range(1, 4):
                dcol = dcol + (dfc_ref[pair, rows, :] + drc_ref[pair, rows, :])
            dlf = jnp.dot(tri, dcol, precision=lax.Precision.HIGHEST, preferred_element_type=F32) + carry
            carry = dlf[0:1, :]
            z = ff_ref[blk * _FB:(blk + 1) * _FB, :] + b_ref[...]
            dz = jnp.where(lane < 8, dlf * jax.nn.sigmoid(-z), 0.0)
            dff_ref[blk * _FB:(blk + 1) * _FB, :] = dz.astype(dff_ref.dtype)
            db = db + jnp.sum(dz, axis=0, keepdims=True)
        db_ref[...] = db

    return pl.pallas_call(
        body, name="fgate_bwd",
        out_shape=[jax.ShapeDtypeStruct((T, LANES), BF), jax.ShapeDtypeStruct((1, LANES), F32)],
        compiler_params=pltpu.CompilerParams(vmem_limit_bytes=VMEM_LIMIT),
    )(ffp, bpad, dfc4, drc4)


MESH = pl.DeviceIdType.MESH


def _place():
    return lax.axis_index("x"), lax.axis_index("y"), lax.axis_index("c")


def _all_gather(shard):
    R, W = shard.shape

    def body(x_ref, out_ref, send_sems, recv_sems, local_sem):
        x, y, c = _place()
        me, sibling = (x, y, c), (x, y, 1 - c)
        chips = [(1 - x, y), (x, 1 - y), (1 - x, 1 - y)]

        def slot(px, py, pc):
            return out_ref.at[4 * px + 2 * py + pc]

        def copy(k, block, to, src=None):
            return pltpu.make_async_remote_copy(
                src_ref=slot(*block) if src is None else src, dst_ref=slot(*block),
                send_sem=send_sems.at[k], recv_sem=recv_sems.at[k], device_id=to, device_id_type=MESH)

        mine = pltpu.make_async_copy(x_ref, slot(*me), local_sem)
        mine.start()
        first = [copy(0, me, sibling, src=x_ref)]
        first += [copy(1 + n, me, (*chip, c), src=x_ref) for n, chip in enumerate(chips)]
        for cp in first:
            cp.start()
        passed = [copy(4 + n, (*chip, c), sibling) for n, chip in enumerate(chips)]
        for n, chip in enumerate(chips):
            copy(1 + n, (*chip, c), me).wait_recv()
            passed[n].start()
        copy(0, sibling, me).wait_recv()
        for n, chip in enumerate(chips):
            copy(4 + n, (*chip, 1 - c), me).wait_recv()
        for cp in first + passed:
            cp.wait_send()
        mine.wait()

    return pl.pallas_call(
        body, name="all_gather_weights",
        out_shape=jax.ShapeDtypeStruct((N_DEV, R, W), shard.dtype),
        in_specs=[pl.BlockSpec(memory_space=pl.ANY)], out_specs=pl.BlockSpec(memory_space=pl.ANY),
        scratch_shapes=[pltpu.SemaphoreType.DMA((7,)), pltpu.SemaphoreType.DMA((7,)), pltpu.SemaphoreType.DMA],
    )(shard)


def _all_to_all(big, small):
    def body(big_ref, small_ref, rbig_ref, rsmall_ref, send_sems, recv_sems, local_sems):
        x, y, c = _place()
        me = 4 * x + 2 * y + c
        l0 = pltpu.make_async_copy(big_ref.at[me], rbig_ref.at[me], local_sems.at[0])
        l1 = pltpu.make_async_copy(small_ref, rsmall_ref.at[me], local_sems.at[1])
        l0.start()
        l1.start()
        copies = []
        for r in range(1, N_DEV):
            px, py, pc = x ^ (r >> 2), y ^ ((r >> 1) & 1), c ^ (r & 1)
            peer = 4 * px + 2 * py + pc
            copies.append(pltpu.make_async_remote_copy(
                src_ref=big_ref.at[peer], dst_ref=rbig_ref.at[me], send_sem=send_sems.at[2 * r], recv_sem=recv_sems.at[2 * r],
                device_id=(px, py, pc), device_id_type=MESH))
            copies.append(pltpu.make_async_remote_copy(
                src_ref=small_ref, dst_ref=rsmall_ref.at[me], send_sem=send_sems.at[2 * r + 1], recv_sem=recv_sems.at[2 * r + 1],
                device_id=(px, py, pc), device_id_type=MESH))
        for cp in copies:
            cp.start()
        for cp in copies:
            cp.wait_recv()
        for cp in copies:
            cp.wait_send()
        l0.wait()
        l1.wait()

    return pl.pallas_call(
        body, name="all_to_all_grads",
        out_shape=[jax.ShapeDtypeStruct(big.shape, big.dtype), jax.ShapeDtypeStruct((N_DEV,) + small.shape, small.dtype)],
        in_specs=[pl.BlockSpec(memory_space=pl.ANY)] * 2, out_specs=[pl.BlockSpec(memory_space=pl.ANY)] * 2,
        scratch_shapes=[pltpu.SemaphoreType.DMA((2 * N_DEV,)), pltpu.SemaphoreType.DMA((2 * N_DEV,)), pltpu.SemaphoreType.DMA((2,))],
    )(big, small)


def _exchange_copies(src_ref, land_ref, send_sems, recv_sems, scatter):
    x, y, c = _place()
    me = 4 * x + 2 * y + c
    copies = []
    for r in range(1, N_DEV):
        px, py, pc = x ^ (r >> 2), y ^ ((r >> 1) & 1), c ^ (r & 1)
        copies.append(pltpu.make_async_remote_copy(
            src_ref=src_ref.at[4 * px + 2 * py + pc] if scatter else src_ref, dst_ref=land_ref.at[me],
            send_sem=send_sems.at[r - 1], recv_sem=recv_sems.at[r - 1], device_id=(px, py, pc), device_id_type=MESH))
    return copies


_HBM = pl.BlockSpec(memory_space=pltpu.HBM)
_SEM = pl.BlockSpec(memory_space=pltpu.SEMAPHORE)
_EFFECT = pltpu.SideEffectType.DATAFLOW_SIDE_EFFECTING


def _exchange_start(name, src, land, scatter):
    def body(src_ref, land_ref, send_sems, recv_sems, src_thru, land_thru, token):
        for cp in _exchange_copies(src_ref, land_ref, send_sems, recv_sems, scatter):
            cp.start()
        token[...] = jnp.zeros(token.shape, F32)

    return pl.pallas_call(
        body, name=name,
        out_shape=(pltpu.SemaphoreType.DMA((N_DEV - 1,)), pltpu.SemaphoreType.DMA((N_DEV - 1,)),
                   pltpu.HBM(src.shape, src.dtype), pltpu.HBM(land.shape, land.dtype), jax.ShapeDtypeStruct((8, LANES), F32)),
        in_specs=(_HBM, _HBM), out_specs=(_SEM, _SEM, _HBM, _HBM, pl.BlockSpec(memory_space=pltpu.VMEM)),
        input_output_aliases={0: 2, 1: 3},
        compiler_params=pltpu.CompilerParams(has_side_effects=_EFFECT),
    )(pltpu.with_memory_space_constraint(src, pltpu.HBM), pltpu.with_memory_space_constraint(land, pltpu.HBM))


def _exchange_wait(name, started, after, scatter):
    send_sems, recv_sems, src_thru, land_thru, _ = started

    def body(src_ref, land_ref, send_sems, recv_sems, after_ref, src_dead, got_ref):
        copies = _exchange_copies(src_ref, land_ref, send_sems, recv_sems, scatter)
        for cp in copies:
            cp.wait_send()
        for cp in copies:
            cp.wait_recv()

    return pl.pallas_call(
        body, name=name,
        out_shape=(pltpu.HBM(src_thru.shape, src_thru.dtype), pltpu.HBM(land_thru.shape, land_thru.dtype)),
        in_specs=(_HBM, _HBM, _SEM, _SEM, pl.BlockSpec(memory_space=pl.ANY)), out_specs=(_HBM, _HBM),
        input_output_aliases={0: 0, 1: 1},
        compiler_params=pltpu.CompilerParams(has_side_effects=_EFFECT),
    )(src_thru, land_thru, send_sems, recv_sems, after)


def _adamw(name, slots, w, m, v, tr, own=None):
    R, W = w.shape

    def body(s_ref, *refs):
        if own is not None:
            own_ref, refs = refs[0], refs[1:]
        w_ref, m_ref, v_ref, g_ref, d_ref, nm_ref, nv_ref = refs
        g = s_ref[0].astype(F32)
        for s in range(1, N_DEV):
            g = g + s_ref[s].astype(F32)
        if own is not None:
            g = g + own_ref[...].astype(F32)
        m2 = ADAM_B1 * m_ref[...] + (1.0 - ADAM_B1) * g
        v2 = ADAM_B2 * v_ref[...] + (1.0 - ADAM_B2) * jnp.square(g)
        m_hat = m2 / (1.0 - ADAM_B1 ** ADAM_STEP)
        v_hat = v2 / (1.0 - ADAM_B2 ** ADAM_STEP)
        g_ref[...] = g
        d_ref[...] = -ADAM_LR * (m_hat / (jnp.sqrt(v_hat) + ADAM_EPS) + ADAM_WD * w_ref[...])
        nm_ref[...] = m2
        nv_ref[...] = v2

    row = lambda: pl.BlockSpec((tr, W), lambda i: (i, 0))
    return pl.pallas_call(
        body, name=name, grid=(R // tr,),
        in_specs=[pl.BlockSpec((N_DEV, tr, W), lambda i: (0, i, 0))] + [row() for _ in range(3 + (own is not None))],
        out_specs=[row(), row(), row(), row()],
        out_shape=[jax.ShapeDtypeStruct((R, W), F32)] * 4,
        compiler_params=_params(("parallel",)),
    )(slots, *([own] if own is not None else []), w, m, v)


def _tables(T):
    pos = jnp.arange(T, dtype=F32)
    inv_freq = 10000.0 ** (-jnp.arange(0, HEAD, 2, dtype=F32) / HEAD)
    ang = pos[:, None] * inv_freq[None, :]
    cos, sin = jnp.cos(ang), jnp.sin(ang)
    cos4 = jnp.tile(cos, (1, 4))
    sin4 = jnp.tile(jnp.concatenate([-sin, sin], axis=1), (1, 2))
    log_g = jnp.log(1.0 - 2.0 ** (-5.0 - jnp.arange(8, dtype=F32)))
    return cos4, sin4, jnp.repeat(log_g, HEAD)[None, :]


def _local_step(x, mem, target, sp, w_inT, token, fetch_rest, push, push_small):
    T = x.shape[0]
    tm = min(512, T)
    tq = min(256, T)
    tb = min(1024, T)
    cos4, sin4, lg = _tables(T)
    g_fq2 = jnp.tile(sp["g_fox_q"], (1, 2))
    g_fk2 = jnp.tile(sp["g_fox_k"], (1, 2))
    g_ret = sp["g_ret_out"].reshape(1, 8 * HEAD)
    bpad = jnp.pad(sp["b_forget"], ((0, 0), (0, LANES - 8)))
    w_secs = [w_inT[k * 512:(k + 1) * 512] for k in range(7)]
    w_ffT = jnp.pad(w_inT[3584:3592], ((0, LANES - 8), (0, 0)))
    w_mainT = w_inT[:3584]
    tie = lambda p, tok: p + tok[0:1, 0:1]
    tm2, tm4 = min(1024, T), min(2048, T)

    hn1, = _rw_fwd("rms_mix", _rms_fn, [(x, D, 0, False)], [(tie(sp["g_mix"], token), D, 0, False)], [(BF, D)], T, tm, 1)
    P, = _mm("proj_in", [[(hn1, w_mainT, "nt")]], [], _ident, T, 3584, tm4, 512, [F32])
    ffp, = _mm("proj_ff", [[(hn1, w_ffT, "nt")]], [], _ident, T, LANES, tm, LANES, [F32])
    ret, s0 = _ret_fwd(P, cos4, sin4, g_ret, lg, T, tb)
    fc, _ = _fgate_fwd(ffp, bpad, T)
    qa, qat, ka, kat, va, vat = _fox_operands(P, fc, g_fq2, g_fk2, T, tm)
    fox, lse = _fox_forward(qat, ka, vat, T, min(512, T), tq)
    W = fetch_rest(fox)
    w_out_halves = (W["w_out"][:4 * LANES], W["w_out"][4 * LANES:])
    h1, hn2 = _mm("proj_out", [[(ret, w_out_halves[0], "nn"), (fox, w_out_halves[1], "nn")]], [x], _add_rms_epi, T, D, tm2, D,
                  [F32, BF], params=[sp["g_xattn"]])

    qx, = _mm("proj_xq", [[(hn2, W["w_xq"], "nn")]], [], _ident, T, D, tm2, D, [F32])
    memn, = _rw_fwd("rms_mem", _rms_fn, [(mem, D, 0, False)], [(sp["g_mem"], D, 0, False)], [(BF, D)], N_MEM, N_MEM, 1)
    kv, = _mm("proj_xkv", [[(memn, W["w_xkvT"], "nt")]], [], _ident, N_MEM, 2 * D, N_MEM, 512, [F32])
    xa_rows = [(qx, XHEAD, 0, True)]
    xa_params = [(sp["g_xq"], XHEAD, 0, False), (sp["g_xk"], XHEAD, 0, False), (kv, XHEAD, 0, True), (kv, XHEAD, 4, True)]
    xo, = _rw_fwd("xattn_fwd", _xattn_fn, xa_rows, xa_params, [(BF, XHEAD)], T, tm, 4)
    h2, hn3 = _mm("proj_xo", [[(xo, W["w_xo"], "nn")]], [h1], _add_rms_epi, T, D, tm2, D, [F32, BF], params=[sp["g_ffn"]])

    gate, up, act = _mm("ffn_in", [[(hn3, W["w_gateT"], "nt")], [(hn3, W["w_upT"], "nt")]], [], _swiglu_fwd_epi,
                        T, D_FF, tm4, 256, [BF, BF, BF])
    dy, dyb, loss_part = _mm("ffn_out", [[(act, W["w_down"], "nn")]], [h2, target], _add_loss_epi, T, D, tm, D, [F32, BF], n_acc=1)

    dgate, dup = _mm("ffn_out_bwd", [[(dyb, W["w_down"], "nt")]], [gate, up], _swiglu_bwd_epi, T, D_FF, tm4, 256, [BF, BF])
    gW = {}
    gW["w_gateT"], = _mm("dw_gate", [[(dgate, hn3, "tn")]], [], _ident, D_FF, D, 256, D, [BF])
    gW["w_upT"], = _mm("dw_up", [[(dup, hn3, "tn")]], [], _ident, D_FF, D, 256, D, [BF])
    gW["w_down"], = _mm("dw_down", [[(act, dyb, "tn")]], [], _ident, D_FF, D, 256, D, [BF])
    tok = push("ffn", gW)
    gs = {}
    dh2, dh2b, gs["g_ffn"] = _mm("ffn_in_bwd", [[(dgate, W["w_gateT"], "nn"), (dup, W["w_upT"], "nn")]], [h2, dy], _rms_bwd_epi,
                                 T, D, min(256, T), D, [F32, BF], params=[tie(sp["g_ffn"], tok)], n_acc=1)

    dxo, = _mm("proj_xo_bwd", [[(dh2b, W["w_xo"], "nt")]], [], _ident, T, D, tm2, D, [BF])
    gW["w_xo"], = _mm("dw_xo", [[(xo, dh2b, "tn")]], [], _ident, D, D, 256, D, [BF])
    dqx, gs["g_xq"], gs["g_xk"], dkv_k, dkv_v = _rw_bwd(
        "xattn_bwd", _xattn_fn, xa_rows, xa_params, [(dxo, XHEAD, 0, True)], T, tm, 4, [BF], [True, True, True, True])
    dkv = jnp.concatenate([dkv_k[:, :D], dkv_v[:, D:]], axis=1)
    gW["w_xq"], = _mm("dw_xq", [[(hn2, dqx, "tn")]], [], _ident, D, D, 256, D, [BF])
    dmemn, = _mm("proj_xkv_bwd", [[(dkv, W["w_xkvT"], "nn")]], [], _ident, N_MEM, D, N_MEM, 512, [F32])
    gW["w_xkvT"], = _mm("dw_xkv", [[(dkv, memn, "tn")]], [], _ident, 2 * D, D, 512, D, [BF])
    tok = push("xattn", gW)
    gs["g_mem"], = _rw_bwd("rms_mem_bwd", _rms_fn, [(mem, D, 0, False)], [(sp["g_mem"], D, 0, False)], [(dmemn, D, 0, False)],
                           N_MEM, N_MEM, 1, [None], [True])
    dh1, dh1b, gs["g_xattn"] = _mm("proj_xq_bwd", [[(dqx, W["w_xq"], "nt")]], [h1, dh2], _rms_bwd_epi, T, D, tm, D, [F32, BF],
                                   params=[tie(sp["g_xattn"], tok)], n_acc=1)

    dmix, = _mm("proj_out_bwd", [[(dh1b, W["w_out"], "nt")]], [], _ident, T, D, tm2, D, [F32])
    gW["w_out"] = jnp.concatenate([_mm("dw_out_%d" % k, [[(a, dh1b, "tn")]], [], _ident, 4 * LANES, D, 256, D, [BF])[0]
                                   for k, a in enumerate((ret, fox))], axis=0)
    tok = push("out", gW)
    doa, doat, dl = _fox_cotangent(dmix, fox, T, tm)
    dqn, dkn, dfv, dfc4, drc4 = _fox_backward(qa, qat, ka, kat, va, doa, doat, lse + tok[0:1, 0:1], dl, T, tq, tq)
    dfq, dfk, gq2, gk2 = _rw_bwd("fox_prep_bwd", _fox_prep_fn, [(P, LANES, 16, True), (P, LANES, 20, True)],
                                 [(g_fq2, LANES, 0, False), (g_fk2, LANES, 0, False)],
                                 [(dqn, LANES, 0, True), (dkn, LANES, 0, True)], T, tm, 4, [BF, BF], [True, True])
    gs["g_fox_q"] = gq2[:, :HEAD] + gq2[:, HEAD:]
    gs["g_fox_k"] = gk2[:, :HEAD] + gk2[:, HEAD:]
    dff, dbp = _fgate_bwd_col(ffp, bpad, dfc4, drc4, T)
    gs["b_forget"] = dbp[:, :8]
    drq, drk, drv, drg, dg_ret = _ret_bwd(P, cos4, sin4, g_ret, lg, s0, dmix, T, tb)
    gs["g_ret_out"] = dg_ret
    dsecs = [drq, drk, drv, drg, dfq, dfk, dfv]
    g_secs = [_mm("dw_in_%d" % k, [[(d, hn1, "tn")]], [], _ident, 512, D, 256, D, [BF])[0] for k, d in enumerate(dsecs)]
    g_ff, = _mm("dw_in_ff", [[(dff, hn1, "tn")]], [], _ident, LANES, D, LANES, D, [BF])
    gW["w_inT"] = jnp.concatenate(g_secs + [g_ff[:8]], axis=0)
    tok = push("in", gW)
    grad_x, _, gs["g_mix"] = _mm("proj_in_bwd", [[(d, w, "nn") for d, w in zip(dsecs, w_secs)] + [(dff, w_ffT, "nn")]], [x, dh1],
                                 _rms_bwd_epi, T, D, tm, D, [F32, BF], params=[tie(sp["g_mix"], tok)], n_acc=1)
    return grad_x, push_small(gs, loss_part)


_CANON = {"w_in": "w_inT", "w_xkv": "w_xkvT", "w_gate": "w_gateT", "w_up": "w_upT"}
_SMALL = (("g_mix", 0, 0, 1024), ("g_xattn", 1, 0, 1024), ("g_mem", 2, 0, 1024), ("g_ffn", 3, 0, 1024),
          ("g_ret_out", 4, 0, 512), ("g_xq", 4, 512, 256), ("g_xk", 4, 768, 256),
          ("g_fox_q", 5, 0, 64), ("g_fox_k", 5, 64, 64), ("b_forget", 5, 128, 8))
_LOSS_AT = (5, 256)


def _pack_shards(tree, dtype):
    parts = []
    for name, rows, padded, transposed in W_LAYOUT:
        a = tree[name][0]
        a = a.T if transposed else a
        parts.append(jnp.pad(a, ((0, padded - rows), (0, 0))).astype(dtype))
    return jnp.concatenate(parts, axis=0)


def _unpack_shards(packed, like):
    out = {}
    for name, rows, padded, transposed in W_LAYOUT:
        a = packed[W_OFF[name]:W_OFF[name] + rows]
        out[name] = (a.T if transposed else a)[None].reshape(like[name].shape)
    return out


def _pack_small(tree):
    rows = [jnp.zeros((1, D), F32) for _ in range(SMALL_ROWS)]
    buf = jnp.concatenate(rows, axis=0)
    for name, r, c, n in _SMALL:
        buf = lax.dynamic_update_slice(buf, tree[name].reshape(1, n).astype(F32), (r, c))
    return buf


def _unpack_small(buf, like):
    return {name: buf[r:r + 1, c:c + n].reshape(like[name].shape) for name, r, c, n in _SMALL}


def kernel(x, mem, g_mix, w_in, b_forget, g_ret_out, g_fox_q, g_fox_k, w_out, g_xattn, w_xq, w_xkv, g_mem, g_xq, g_xk, w_xo, g_ffn, w_gate, w_up, w_down, loss_target, m_g_mix, m_w_in, m_b_forget, m_g_ret_out, m_g_fox_q, m_g_fox_k, m_w_out, m_g_xattn, m_w_xq, m_w_xkv, m_g_mem, m_g_xq, m_g_xk, m_w_xo, m_g_ffn, m_w_gate, m_w_up, m_w_down, v_g_mix, v_w_in, v_b_forget, v_g_ret_out, v_g_fox_q, v_g_fox_k, v_w_out, v_g_xattn, v_w_xq, v_w_xkv, v_g_mem, v_g_xq, v_g_xk, v_w_xo, v_g_ffn, v_w_gate, v_w_up, v_w_down):
    names = ("g_mix", "w_in", "b_forget", "g_ret_out", "g_fox_q", "g_fox_k", "w_out", "g_xattn", "w_xq", "w_xkv", "g_mem",
             "g_xq", "g_xk", "w_xo", "g_ffn", "w_gate", "w_up", "w_down")
    w = dict(zip(names, (g_mix, w_in, b_forget, g_ret_out, g_fox_q, g_fox_k, w_out, g_xattn, w_xq, w_xkv, g_mem, g_xq, g_xk,
                         w_xo, g_ffn, w_gate, w_up, w_down)))
    m = dict(zip(names, (m_g_mix, m_w_in, m_b_forget, m_g_ret_out, m_g_fox_q, m_g_fox_k, m_w_out, m_g_xattn, m_w_xq, m_w_xkv,
                         m_g_mem, m_g_xq, m_g_xk, m_w_xo, m_g_ffn, m_w_gate, m_w_up, m_w_down)))
    v = dict(zip(names, (v_g_mix, v_w_in, v_b_forget, v_g_ret_out, v_g_fox_q, v_g_fox_k, v_w_out, v_g_xattn, v_w_xq, v_w_xkv,
                         v_g_mem, v_g_xq, v_g_xk, v_w_xo, v_g_ffn, v_w_gate, v_w_up, v_w_down)))
    small_names = [s[0] for s in _SMALL]

    gathered = _all_gather(_pack_shards(w, BF))
    W = {}
    for name, rows, padded, transposed in W_LAYOUT:
        full = gathered[:, W_OFF[name]:W_OFF[name] + rows].reshape(N_DEV * rows, D)
        W[_CANON.get(name, name)] = full

    sp = {n: w[n].reshape(1, -1) for n in small_names}
    loss_part, grad_x, gW, gs = _local_step(x[0], mem[0], loss_target[0], sp, W)

    chunks = []
    for name, rows, padded, transposed in W_LAYOUT:
        g = gW[_CANON.get(name, name)].reshape(N_DEV, rows, D)
        chunks.append(jnp.pad(g, ((0, 0), (0, padded - rows), (0, 0))).astype(BF))
    send = jnp.concatenate(chunks, axis=1)
    small = _pack_small(gs)
    small = lax.dynamic_update_slice(small, loss_part[:, :1], _LOSS_AT)
    recv, recv_small = _all_to_all(send, small)

    g_big, d_big, m_big, v_big = _adamw("adamw_shards", recv, _pack_shards(w, F32), _pack_shards(m, F32), _pack_shards(v, F32), 240)
    g_sm, d_sm, m_sm, v_sm = _adamw("adamw_small", recv_small, _pack_small(w), _pack_small(m), _pack_small(v), SMALL_ROWS)
    loss = g_sm[_LOSS_AT[0], _LOSS_AT[1]]

    outs = []
    for big, sm in ((g_big, g_sm), (d_big, d_sm), (m_big, m_sm), (v_big, v_sm)):
        tree = {**_unpack_shards(big, w), **_unpack_small(sm, w)}
        outs += [tree[n] for n in names]
    return (loss, grad_x[None], *outs)


def _pack_shards(tree, names, dtype):
    parts = []
    for name in names:
        rows, padded, transposed = W_SHARD[name]
        a = tree[name][0]
        a = a.T if transposed else a
        parts.append(jnp.pad(a, ((0, padded - rows), (0, 0))).astype(dtype))
    return jnp.concatenate(parts, axis=0)


def _unpack_shards(packed, names, like):
    out, off = {}, 0
    for name in names:
        rows, padded, transposed = W_SHARD[name]
        a = packed[off:off + rows]
        out[name] = (a.T if transposed else a)[None].reshape(like[name].shape)
        off += padded
    return out


def _unpack_gathered(gathered, names):
    out, off = {}, 0
    for name in names:
        rows, padded, _ = W_SHARD[name]
        out[_CANON.get(name, name)] = gathered[:, off:off + rows].reshape(N_DEV * rows, D)
        off += padded
    return out


def _pack_chunks(grads, names):
    chunks = []
    for name in names:
        rows, padded, _ = W_SHARD[name]
        g = grads[_CANON.get(name, name)].reshape(N_DEV, rows, D)
        chunks.append(jnp.pad(g, ((0, 0), (0, padded - rows), (0, 0))).astype(BF))
    return jnp.concatenate(chunks, axis=1)


def kernel(x, mem, g_mix, w_in, b_forget, g_ret_out, g_fox_q, g_fox_k, w_out, g_xattn, w_xq, w_xkv, g_mem, g_xq, g_xk, w_xo, g_ffn, w_gate, w_up, w_down, loss_target, m_g_mix, m_w_in, m_b_forget, m_g_ret_out, m_g_fox_q, m_g_fox_k, m_w_out, m_g_xattn, m_w_xq, m_w_xkv, m_g_mem, m_g_xq, m_g_xk, m_w_xo, m_g_ffn, m_w_gate, m_w_up, m_w_down, v_g_mix, v_w_in, v_b_forget, v_g_ret_out, v_g_fox_q, v_g_fox_k, v_w_out, v_g_xattn, v_w_xq, v_w_xkv, v_g_mem, v_g_xq, v_g_xk, v_w_xo, v_g_ffn, v_w_gate, v_w_up, v_w_down):
    names = ("g_mix", "w_in", "b_forget", "g_ret_out", "g_fox_q", "g_fox_k", "w_out", "g_xattn", "w_xq", "w_xkv", "g_mem",
             "g_xq", "g_xk", "w_xo", "g_ffn", "w_gate", "w_up", "w_down")
    w = dict(zip(names, (g_mix, w_in, b_forget, g_ret_out, g_fox_q, g_fox_k, w_out, g_xattn, w_xq, w_xkv, g_mem, g_xq, g_xk,
                         w_xo, g_ffn, w_gate, w_up, w_down)))
    m = dict(zip(names, (m_g_mix, m_w_in, m_b_forget, m_g_ret_out, m_g_fox_q, m_g_fox_k, m_w_out, m_g_xattn, m_w_xq, m_w_xkv,
                         m_g_mem, m_g_xq, m_g_xk, m_w_xo, m_g_ffn, m_w_gate, m_w_up, m_w_down)))
    v = dict(zip(names, (v_g_mix, v_w_in, v_b_forget, v_g_ret_out, v_g_fox_q, v_g_fox_k, v_w_out, v_g_xattn, v_w_xq, v_w_xkv,
                         v_g_mem, v_g_xq, v_g_xk, v_w_xo, v_g_ffn, v_w_gate, v_w_up, v_w_down)))
    small_names = [s[0] for s in _SMALL]
    me = 4 * lax.axis_index("x") + 2 * lax.axis_index("y") + lax.axis_index("c")

    first = _all_gather(_pack_shards(w, GATHER_FIRST, BF))
    first, rest_shard = lax.optimization_barrier((first, _pack_shards(w, GATHER_REST, BF)))
    rest_started = _exchange_start("gather_rest_start", rest_shard,
                                   jnp.broadcast_to(rest_shard[None], (N_DEV,) + rest_shard.shape), scatter=False)

    def fetch_rest(after):
        return _unpack_gathered(_exchange_wait("gather_rest_wait", rest_started, after, scatter=False)[1], GATHER_REST)

    pushed = {}

    def push(group, grads):
        send = _pack_chunks(grads, GRAD_GROUPS[group])
        pushed[group] = _exchange_start("scatter_%s_start" % group, send, jnp.zeros(send.shape, BF), scatter=True)
        return pushed[group][4]

    sp = {n: w[n].reshape(1, -1) for n in small_names}
    loss_part, grad_x, g_last, gs = _local_step(x[0], mem[0], loss_target[0], sp, _unpack_gathered(first, GATHER_FIRST)["w_inT"],
                                                rest_started[4], fetch_rest, push)

    small = lax.dynamic_update_slice(_pack_small(gs), loss_part[:, :1], _LOSS_AT)
    recv_mix, recv_small = _all_to_all(_pack_chunks(g_last, GRAD_GROUPS["mix"]), small)

    results = {}
    for group in ("ffn", "xattn", "mix"):
        gnames = GRAD_GROUPS[group]
        wp, mp, vp = (_pack_shards(t, gnames, F32) for t in (w, m, v))
        if group == "mix":
            res = _adamw("adamw_mix", recv_mix, wp, mp, vp, 16)
        else:
            sent, recv = _exchange_wait("scatter_%s_wait" % group, pushed[group], recv_small, scatter=True)
            own = lax.dynamic_index_in_dim(sent, me, axis=0, keepdims=False)
            res = _adamw("adamw_%s" % group, recv, wp, mp, vp, {"ffn": 176, "xattn": 128}[group], own=own)
        results[group] = [_unpack_shards(r, gnames, w) for r in res]
    g_sm, d_sm, m_sm, v_sm = _adamw("adamw_small", recv_small, _pack_small(w), _pack_small(m), _pack_small(v), SMALL_ROWS)
    loss = g_sm[_LOSS_AT[0], _LOSS_AT[1]]

    outs = []
    for k, sm in enumerate((g_sm, d_sm, m_sm, v_sm)):
        tree = _unpack_small(sm, w)
        for group in results:
            tree.update(results[group][k])
        outs += [tree[n] for n in names]
    return (loss, grad_x[None], *outs)


SCATTER_GROUPS = {"ffn": ("w_gate", "w_up", "w_down"), "xattn": ("w_xq", "w_xo", "w_xkv"), "out": ("w_out",), "in": ("w_in",)}


def _adam_update(g, w, m, v):
    m2 = ADAM_B1 * m + (1.0 - ADAM_B1) * g
    v2 = ADAM_B2 * v + (1.0 - ADAM_B2) * jnp.square(g)
    m_hat = m2 / (1.0 - ADAM_B1 ** ADAM_STEP)
    v_hat = v2 / (1.0 - ADAM_B2 ** ADAM_STEP)
    return g, -ADAM_LR * (m_hat / (jnp.sqrt(v_hat) + ADAM_EPS) + ADAM_WD * w), m2, v2


def _adamw_shard(name, recv, own, off, w, m, v):
    rows, padded, transposed = W_SHARD[name.split(":")[1]]
    assert off % padded == 0
    blk = off // padded

    def total(s_ref, own_ref):
        g = own_ref[...].astype(F32)
        for s in range(N_DEV):
            g = g + s_ref[s].astype(F32)
        return g

    canonical_view = name.endswith(":")
    if transposed and rows == padded and not canonical_view:
        res = _adamw_shard(name + ":", recv, own, off, *(jnp.swapaxes(a, 1, 2) for a in (w, m, v)))
        return [jnp.swapaxes(r, 1, 2) for r in res]

    if canonical_view or not transposed:
        def body(s_ref, own_ref, w_ref, m_ref, v_ref, *outs):
            for o, r in zip(outs, _adam_update(total(s_ref, own_ref), w_ref[0], m_ref[0], v_ref[0])):
                o[0] = r

        full = pl.BlockSpec((1, rows, D), lambda i: (0, 0, 0))
        return pl.pallas_call(
            body, name=name.replace(":", "_"), grid=(1,),
            in_specs=[pl.BlockSpec((N_DEV, padded, D), lambda i: (0, blk, 0)), pl.BlockSpec((padded, D), lambda i: (blk, 0)),
                      full, full, full],
            out_specs=[full] * 4, out_shape=[jax.ShapeDtypeStruct((1, rows, D), F32)] * 4,
            compiler_params=_params(("arbitrary",)),
        )(recv, own, w, m, v)

    wide = -(-padded // LANES) * LANES

    def body(s_ref, own_ref, w_ref, m_ref, v_ref, *outs):
        g = total(s_ref, own_ref)
        if wide > padded:
            g = jnp.concatenate([g, jnp.zeros((wide - padded, LANES), F32)], axis=0)
        g = g.T[:, :rows]
        for o, r in zip(outs, _adam_update(g, w_ref[0], m_ref[0], v_ref[0])):
            o[0] = r

    cols = pl.BlockSpec((1, LANES, rows), lambda c: (0, c, 0))
    return pl.pallas_call(
        body, name=name.replace(":", "_"), grid=(D // LANES,),
        in_specs=[pl.BlockSpec((N_DEV, padded, LANES), lambda c: (0, blk, c)), pl.BlockSpec((padded, LANES), lambda c: (blk, c)),
                  cols, cols, cols],
        out_specs=[cols] * 4, out_shape=[jax.ShapeDtypeStruct((1, D, rows), F32)] * 4,
        compiler_params=_params(("arbitrary",)),
    )(recv, own, w, m, v)


def _gather_small(small):
    def body(small_ref, out_ref, send_sems, recv_sems, local_sem):
        x, y, c = _place()
        me = 4 * x + 2 * y + c
        mine = pltpu.make_async_copy(small_ref, out_ref.at[me], local_sem)
        mine.start()
        copies = []
        for r in range(1, N_DEV):
            px, py, pc = x ^ (r >> 2), y ^ ((r >> 1) & 1), c ^ (r & 1)
            copies.append(pltpu.make_async_remote_copy(
                src_ref=small_ref, dst_ref=out_ref.at[me], send_sem=send_sems.at[r - 1], recv_sem=recv_sems.at[r - 1],
                device_id=(px, py, pc), device_id_type=MESH))
        for cp in copies:
            cp.start()
        for cp in copies:
            cp.wait_recv()
        for cp in copies:
            cp.wait_send()
        mine.wait()

    return pl.pallas_call(
        body, name="gather_small",
        out_shape=jax.ShapeDtypeStruct((N_DEV,) + small.shape, small.dtype),
        in_specs=[pl.BlockSpec(memory_space=pl.ANY)], out_specs=pl.BlockSpec(memory_space=pl.ANY),
        scratch_shapes=[pltpu.SemaphoreType.DMA((N_DEV - 1,)), pltpu.SemaphoreType.DMA((N_DEV - 1,)), pltpu.SemaphoreType.DMA],
    )(small)


def _pack_chunks(grads, names):
    chunks = []
    for name in names:
        rows, padded, _ = W_SHARD[name]
        g = grads[_CANON.get(name, name)].reshape(N_DEV, rows, D)
        chunks.append(jnp.pad(g, ((0, 0), (0, padded - rows), (0, 0))).astype(BF))
    return chunks[0] if len(chunks) == 1 else jnp.concatenate(chunks, axis=1)


def kernel(x, mem, g_mix, w_in, b_forget, g_ret_out, g_fox_q, g_fox_k, w_out, g_xattn, w_xq, w_xkv, g_mem, g_xq, g_xk, w_xo, g_ffn, w_gate, w_up, w_down, loss_target, m_g_mix, m_w_in, m_b_forget, m_g_ret_out, m_g_fox_q, m_g_fox_k, m_w_out, m_g_xattn, m_w_xq, m_w_xkv, m_g_mem, m_g_xq, m_g_xk, m_w_xo, m_g_ffn, m_w_gate, m_w_up, m_w_down, v_g_mix, v_w_in, v_b_forget, v_g_ret_out, v_g_fox_q, v_g_fox_k, v_w_out, v_g_xattn, v_w_xq, v_w_xkv, v_g_mem, v_g_xq, v_g_xk, v_w_xo, v_g_ffn, v_w_gate, v_w_up, v_w_down):
    names = ("g_mix", "w_in", "b_forget", "g_ret_out", "g_fox_q", "g_fox_k", "w_out", "g_xattn", "w_xq", "w_xkv", "g_mem",
             "g_xq", "g_xk", "w_xo", "g_ffn", "w_gate", "w_up", "w_down")
    w = dict(zip(names, (g_mix, w_in, b_forget, g_ret_out, g_fox_q, g_fox_k, w_out, g_xattn, w_xq, w_xkv, g_mem, g_xq, g_xk,
                         w_xo, g_ffn, w_gate, w_up, w_down)))
    m = dict(zip(names, (m_g_mix, m_w_in, m_b_forget, m_g_ret_out, m_g_fox_q, m_g_fox_k, m_w_out, m_g_xattn, m_w_xq, m_w_xkv,
                         m_g_mem, m_g_xq, m_g_xk, m_w_xo, m_g_ffn, m_w_gate, m_w_up, m_w_down)))
    v = dict(zip(names, (v_g_mix, v_w_in, v_b_forget, v_g_ret_out, v_g_fox_q, v_g_fox_k, v_w_out, v_g_xattn, v_w_xq, v_w_xkv,
                         v_g_mem, v_g_xq, v_g_xk, v_w_xo, v_g_ffn, v_w_gate, v_w_up, v_w_down)))
    small_names = [s[0] for s in _SMALL]
    me = 4 * lax.axis_index("x") + 2 * lax.axis_index("y") + lax.axis_index("c")

    first = _all_gather(_pack_shards(w, GATHER_FIRST, BF))
    first, rest_shard = lax.optimization_barrier((first, _pack_shards(w, GATHER_REST, BF)))
    rest_started = _exchange_start("gather_rest_start", rest_shard,
                                   jnp.broadcast_to(rest_shard[None], (N_DEV,) + rest_shard.shape), scatter=False)

    def fetch_rest(after):
        return _unpack_gathered(_exchange_wait("gather_rest_wait", rest_started, after, scatter=False)[1], GATHER_REST)

    pushed = {}

    def push(group, grads):
        send = _pack_chunks(grads, SCATTER_GROUPS[group])
        pushed[group] = _exchange_start("scatter_%s_start" % group, send, jnp.zeros(send.shape, BF), scatter=True)
        return pushed[group][4]

    def push_small(gs, loss_part):
        small = lax.dynamic_update_slice(_pack_small(gs), loss_part[:, :1], _LOSS_AT)
        pushed["small"] = _exchange_start("gather_small_start", small, jnp.broadcast_to(small[None], (N_DEV,) + small.shape),
                                          scatter=False)
        return pushed["small"][4]

    sp = {n: w[n].reshape(1, -1) for n in small_names}
    grad_x, done = _local_step(x[0], mem[0], loss_target[0], sp, _unpack_gathered(first, GATHER_FIRST)["w_inT"],
                               rest_started[4], fetch_rest, push, push_small)

    results, after = {}, done
    for group in ("ffn", "xattn", "out", "small", "in"):
        if group == "small":
            recv_small = _exchange_wait("gather_small_wait", pushed["small"], after, scatter=False)[1]
            g_sm, d_sm, m_sm, v_sm = _adamw("adamw_small", recv_small, _pack_small(w), _pack_small(m), _pack_small(v), SMALL_ROWS)
            after = g_sm
            continue
        sent, recv = _exchange_wait("scatter_%s_wait" % group, pushed[group], after, scatter=True)
        own = lax.dynamic_index_in_dim(sent, me, axis=0, keepdims=False)
        off = 0
        for name in SCATTER_GROUPS[group]:
            results[name] = _adamw_shard("adamw:" + name, recv, own, off, w[name], m[name], v[name])
            off += W_SHARD[name][1]
        after = results[SCATTER_GROUPS[group][-1]][0]
    loss = g_sm[_LOSS_AT[0], _LOSS_AT[1]]

    outs = []
    for k, sm in enumerate((g_sm, d_sm, m_sm, v_sm)):
        tree = _unpack_small(sm, w)
        tree.update({name: res[k] for name, res in results.items()})
        outs += [tree[n] for n in names]
    return (loss, grad_x[None], *outs)
```

```python
import functools
import math

import jax
import jax.numpy as jnp
import numpy as np
from jax import lax
from jax.experimental import pallas as pl
from jax.experimental.pallas import tpu as pltpu

F32 = jnp.float32
BF = jnp.bfloat16

D = 1024
HEAD = 64
CHUNK = 64
N_MEM = 256
XHEAD = 256
D_FF = 2816
EPS = 1e-6
NEG = -1e30
LANES = 128
N_DEV = 8
V7X_VMEM_BYTES = 64 * 1024 * 1024
VMEM_LIMIT = V7X_VMEM_BYTES - 8 * 1024 * 1024

ADAM_LR, ADAM_B1, ADAM_B2, ADAM_EPS, ADAM_WD, ADAM_STEP = 0.001, 0.9, 0.999, 1e-08, 0.01, 10

W_LAYOUT = (("w_in", 449, 464, True), ("w_out", 128, 128, False), ("w_xq", 128, 128, False), ("w_xkv", 256, 256, True),
            ("w_xo", 128, 128, False), ("w_gate", 352, 352, True), ("w_up", 352, 352, True), ("w_down", 352, 352, False))
W_ROWS = sum(w[2] for w in W_LAYOUT)
W_OFF = {}
_o = 0
for _n, _r, _p, _t in W_LAYOUT:
    W_OFF[_n] = _o
    _o += _p
SMALL_ROWS = 8
W_SHARD = {"w_in": (449, 449, True), "w_out": (128, 128, False), "w_xq": (128, 128, False), "w_xkv": (256, 256, True),
           "w_xo": (128, 128, False), "w_gate": (352, 352, True), "w_up": (352, 352, True), "w_down": (352, 352, False)}
GATHER_FIRST = ("w_in",)
GATHER_REST = ("w_out", "w_xq", "w_xkv", "w_xo", "w_gate", "w_up", "w_down")
GRAD_GROUPS = {"ffn": ("w_gate", "w_up", "w_down"), "xattn": ("w_xq", "w_xkv", "w_xo"), "mix": ("w_in", "w_out")}

NT = (((1,), (1,)), ((), ()))
NN = (((1,), (0,)), ((), ()))
TN = (((0,), (0,)), ((), ()))
_DIMS = {"nn": NN, "nt": NT, "tn": TN}


def _params(sem):
    return pltpu.CompilerParams(dimension_semantics=sem, vmem_limit_bytes=VMEM_LIMIT)


def _mm(name, products, extras, epilogue, M, N, tm, tn, out_dtypes, params=(), n_acc=0):
    assert n_acc == 0 or tn == N
    flat = [t for p in products for t in p]
    counts = [len(p) for p in products]
    in_specs, args = [], []
    for a, b, form in flat:
        if form == "tn":
            in_specs.append(pl.BlockSpec((a.shape[0], tm), lambda i, j: (0, i)))
        else:
            in_specs.append(pl.BlockSpec((tm, a.shape[1]), lambda i, j: (i, 0)))
        if form == "nt":
            in_specs.append(pl.BlockSpec((tn, b.shape[1]), lambda i, j: (j, 0)))
        else:
            in_specs.append(pl.BlockSpec((b.shape[0], tn), lambda i, j: (0, j)))
        args += [a, b]
    for e in extras:
        in_specs.append(pl.BlockSpec((tm, tn), lambda i, j: (i, j)))
        args.append(e)
    for p in params:
        in_specs.append(pl.BlockSpec((1, tn), lambda i, j: (0, j)))
        args.append(p)
    n_in = len(args)
    n_out = len(out_dtypes)

    def body(*refs):
        ins, outs = refs[:n_in], refs[n_in:]
        prods, p = [], 0
        for c in counts:
            acc = None
            for _ in range(c):
                a = ins[2 * p][...].astype(BF)
                b = ins[2 * p + 1][...].astype(BF)
                d = lax.dot_general(a, b, _DIMS[flat[p][2]], preferred_element_type=F32)
                acc = d if acc is None else acc + d
                p += 1
            prods.append(acc)
        ex = [r[...].astype(F32) for r in ins[2 * len(flat):]]
        res = epilogue(*prods, *ex)
        for o, r in zip(outs[:n_out], res[:n_out]):
            o[...] = r.astype(o.dtype)
        for o, r in zip(outs[n_out:], res[n_out:]):
            @pl.when(pl.program_id(0) == 0)
            def _(o=o):
                o[...] = jnp.zeros(o.shape, F32)
            o[...] += r

    return pl.pallas_call(
        body, name=name, grid=(M // tm, N // tn), in_specs=in_specs,
        out_specs=[pl.BlockSpec((tm, tn), lambda i, j: (i, j)) for _ in out_dtypes]
        + [pl.BlockSpec((1, tn), lambda i, j: (0, j)) for _ in range(n_acc)],
        out_shape=[jax.ShapeDtypeStruct((M, N), dt) for dt in out_dtypes] + [jax.ShapeDtypeStruct((1, N), F32)] * n_acc,
        compiler_params=_params(("arbitrary", "arbitrary")),
    )(*args)


def _ident(x):
    return (x,)


def _add(x, r):
    return (x + r,)


def _spec(rows, w, off, per_j):
    if per_j:
        return pl.BlockSpec((rows, w), lambda j, i: (i, off + j))
    return pl.BlockSpec((rows, w), lambda j, i: (i, off))


def _pspec(rows, w, off, per_j):
    if per_j:
        return pl.BlockSpec((rows, w), lambda j, i: (0, off + j))
    return pl.BlockSpec((rows, w), lambda j, i: (0, off))


def _rw_fwd(name, fn, rows, params, outs, T, tm, nj, n_acc=0):
    in_specs = [_spec(tm, w, off, pj) for _, w, off, pj in rows] + [_pspec(a.shape[0], w, off, pj) for a, w, off, pj in params]
    args = [r[0] for r in rows] + [p[0] for p in params]
    n_in, n_out = len(args), len(outs)
    out_specs = [pl.BlockSpec((tm, w), lambda j, i: (i, j)) for _, w in outs]
    out_shape = [jax.ShapeDtypeStruct((T, nj * w), dt) for dt, w in outs]
    out_specs += [pl.BlockSpec((1, LANES), lambda j, i: (0, 0)) for _ in range(n_acc)]
    out_shape += [jax.ShapeDtypeStruct((1, LANES), F32) for _ in range(n_acc)]

    def body(*refs):
        vals = [r[...].astype(F32) for r in refs[:n_in]]
        res = fn(*vals)
        orefs = refs[n_in:]
        for k in range(n_out):
            orefs[k][...] = res[k].astype(orefs[k].dtype)
        first = (pl.program_id(0) == 0) & (pl.program_id(1) == 0)
        for k in range(n_acc):
            @pl.when(first)
            def _(k=k):
                orefs[n_out + k][...] = jnp.zeros((1, LANES), F32)
            orefs[n_out + k][...] += res[n_out + k]

    return pl.pallas_call(
        body, name=name, grid=(nj, T // tm), in_specs=in_specs, out_specs=out_specs, out_shape=out_shape,
        compiler_params=_params(("arbitrary", "arbitrary")),
    )(*args)


def _rw_bwd(name, fn, rows, params, cots, T, tm, nj, row_grads, param_grads, resid=None):
    in_specs = ([_spec(tm, w, off, pj) for _, w, off, pj in rows] + [_pspec(a.shape[0], w, off, pj) for a, w, off, pj in params]
                + [_spec(tm, w, off, pj) for _, w, off, pj in cots])
    args = [r[0] for r in rows] + [p[0] for p in params] + [c[0] for c in cots]
    if resid is not None:
        in_specs.append(_spec(tm, rows[0][1], rows[0][2], rows[0][3]))
        args.append(resid)
    nr, npar, nc = len(rows), len(params), len(cots)
    out_specs, out_shape, kinds = [], [], []
    for k, dts in enumerate(row_grads):
        for dt in (dts if isinstance(dts, (list, tuple)) else [dts]):
            if dt is not None:
                w = rows[k][1]
                out_specs.append(pl.BlockSpec((tm, w), lambda j, i: (i, j)))
                out_shape.append(jax.ShapeDtypeStruct((T, nj * w), dt))
                kinds.append(("row", k))
    for k, need in enumerate(param_grads):
        if need:
            a, w, off, pj = params[k]
            out_specs.append(_pspec(a.shape[0], w, off, pj))
            out_shape.append(jax.ShapeDtypeStruct(a.shape, F32))
            kinds.append(("par", k))

    def body(*refs):
        vals = [r[...].astype(F32) for r in refs[:nr + npar]]
        ct = tuple(r[...].astype(F32) for r in refs[nr + npar:nr + npar + nc])
        _, vjp = jax.vjp(lambda *a: tuple(fn(*a)), *vals)
        grads = list(vjp(ct))
        n_in = nr + npar + nc + (resid is not None)
        if resid is not None:
            grads[0] = grads[0] + refs[n_in - 1][...].astype(F32)
        orefs = refs[n_in:]
        j, i = pl.program_id(0), pl.program_id(1)
        for o, (kind, k) in zip(orefs, kinds):
            if kind == "row":
                o[...] = grads[k].astype(o.dtype)
            else:
                first = (i == 0) if params[k][3] else ((i == 0) & (j == 0))

                @pl.when(first)
                def _(o=o):
                    o[...] = jnp.zeros(o.shape, F32)
                o[...] += grads[nr + k]

    return pl.pallas_call(
        body, name=name, grid=(nj, T // tm), in_specs=in_specs, out_specs=out_specs, out_shape=out_shape,
        compiler_params=_params(("arbitrary", "arbitrary")),
    )(*args)


def _rms(x, g):
    return x * lax.rsqrt(jnp.mean(x * x, axis=-1, keepdims=True) + EPS) * g


def _rms_fn(x, g):
    return (_rms(x, g),)


def _lo_mask():
    return lax.broadcasted_iota(jnp.int32, (1, LANES), 1) < HEAD


def _gmean(x, lo):
    s0 = jnp.sum(jnp.where(lo, x, 0.0), axis=-1, keepdims=True)
    s1 = jnp.sum(jnp.where(lo, 0.0, x), axis=-1, keepdims=True)
    return jnp.where(lo, s0, s1) * (1.0 / HEAD)


def _fox_prep_fn(fq, fk, gq, gk):
    lo = _lo_mask()
    qn = fq * lax.rsqrt(_gmean(fq * fq, lo) + EPS) * gq * (HEAD ** -0.5)
    kn = fk * lax.rsqrt(_gmean(fk * fk, lo) + EPS) * gk
    return qn, kn


def _cast_fn(v):
    return (v,)


@jax.custom_vjp
def _swap_halves(x):
    bit = (lax.broadcasted_iota(jnp.int32, (1, LANES), 1) & (HEAD // 2)) == 0
    return jnp.where(bit, pltpu.roll(x, LANES - HEAD // 2, 1), pltpu.roll(x, HEAD // 2, 1))


_swap_halves.defvjp(lambda x: (_swap_halves(x), None), lambda _, g: (_swap_halves(g),))


def _ret_fn(rq, rk, rv, rg, cos, sin, s_in, g, lg):
    tb = rq.shape[0]
    nc = tb // CHUNK
    lo = _lo_mask()
    row = lax.broadcasted_iota(jnp.int32, (LANES, 1), 0) < HEAD
    same_head = row == lo
    q = (rq * cos + _swap_halves(rq) * sin) * (HEAD ** -0.5)
    k = rk * cos + _swap_halves(rk) * sin
    q3, k3, v3 = q.reshape(nc, CHUNK, LANES), k.reshape(nc, CHUNK, LANES), rv.reshape(nc, CHUNK, LANES)
    pos = lax.broadcasted_iota(jnp.int32, (CHUNK, 1), 0).astype(F32)
    q_decay = jnp.exp(lg * (pos + 1.0))
    k_decay = jnp.exp(lg * (CHUNK - 1.0 - pos))
    chunk_decay = jnp.exp(lg * float(CHUNK))
    dist = jnp.abs(lax.broadcasted_iota(jnp.int32, (CHUNK, CHUNK), 0) - lax.broadcasted_iota(jnp.int32, (CHUNK, CHUNK), 1)).astype(F32)
    v3b = v3.astype(BF)
    intra = []
    for hh in range(2):
        hm = lo if hh == 0 else ~lo
        lg_h = lg[:, hh * HEAD:hh * HEAD + 1]
        qm = jnp.where(hm, q3, 0.0).astype(BF)
        sc = jnp.einsum("nid,njd->nij", qm, k3.astype(BF), preferred_element_type=F32) * jnp.exp(lg_h * dist)[None]
        intra.append(jnp.einsum("nij,nje->nie", sc.astype(BF), v3b, preferred_element_type=F32))
    o = jnp.where(lo, intra[0], intra[1])
    kv = jnp.einsum("njd,nje->nde", (k3 * k_decay[None]).astype(BF), v3b, preferred_element_type=F32)
    kv = jnp.where(same_head[None], kv, 0.0)
    state, states = s_in, []
    for n in range(nc):
        states.append(state)
        state = state * chunk_decay + kv[n]
    s_prev = jnp.stack(states, axis=0)
    o = o + jnp.einsum("nid,nde->nie", (q3 * q_decay[None]).astype(BF), s_prev.astype(BF), preferred_element_type=F32)
    o = o.reshape(tb, LANES)
    mu = _gmean(o, lo)
    oc = o - mu
    y = oc * lax.rsqrt(_gmean(oc * oc, lo) + EPS) * g
    return jax.nn.silu(rg) * y, state


def _xattn_fn(qx, gq, gk, kk, vv):
    q = _rms(qx, gq)
    k = _rms(kk, gk)
    logits = lax.dot_general(q.astype(BF), k.astype(BF), NT, preferred_element_type=F32) * (XHEAD ** -0.5)
    p = jax.nn.softmax(logits, axis=-1)
    return (jnp.dot(p.astype(BF), vv.astype(BF), preferred_element_type=F32),)


def _swiglu_fwd_epi(g, u):
    return g, u, jax.nn.silu(g) * u


def _swiglu_bwd_epi(dact, g, u):
    _, vjp = jax.vjp(lambda a, b: jax.nn.silu(a) * b, g, u)
    return vjp(dact)


def _add_rms_epi(acc, resid, g):
    h = acc + resid
    return h, _rms(h, g)


def _add_loss_epi(acc, resid, target):
    err = (acc + resid) - target
    dy = err * (1.0 / D)
    part = jnp.sum(jnp.sum(err * err, axis=0, keepdims=True), axis=1, keepdims=True) * (0.5 / D)
    return dy, dy, jnp.broadcast_to(part, (1, err.shape[1]))


def _rms_bwd_epi(dhn, h, skip, g):
    _, vjp = jax.vjp(_rms, h, g)
    dh, dg = vjp(dhn)
    dh = dh + skip
    return dh, dh, dg


def _loss_fn(h, target):
    err = h - target
    part = jnp.sum(jnp.sum(err * err, axis=0, keepdims=True), axis=-1, keepdims=True) * (0.5 / D)
    dy = err * (1.0 / D)
    return dy, dy, part


def _ret_fwd(P, cos, sin, g_ret, lg, T, tb):
    nb = T // tb

    def body(rq, rk, rv, rg, c, s, g, l, o_ref, s0_ref, state):
        @pl.when(pl.program_id(1) == 0)
        def _():
            state[...] = jnp.zeros(state.shape, F32)
        s0_ref[0, 0] = state[...]
        out, s_new = _ret_fn(rq[...], rk[...], rv[...], rg[...], c[...], s[...], state[...], g[...], l[...])
        o_ref[...] = out
        state[...] = s_new

    sec = lambda off: pl.BlockSpec((tb, LANES), lambda j, i: (i, off + j))
    tab = pl.BlockSpec((tb, LANES), lambda j, i: (i, 0))
    par = pl.BlockSpec((1, LANES), lambda j, i: (0, j))
    return pl.pallas_call(
        body, name="ret_fwd", grid=(4, nb),
        in_specs=[sec(0), sec(4), sec(8), sec(12), tab, tab, par, par],
        out_specs=[pl.BlockSpec((tb, LANES), lambda j, i: (i, j)), pl.BlockSpec((1, 1, LANES, LANES), lambda j, i: (j, i, 0, 0))],
        out_shape=[jax.ShapeDtypeStruct((T, 4 * LANES), F32), jax.ShapeDtypeStruct((4, nb, LANES, LANES), F32)],
        scratch_shapes=[pltpu.VMEM((LANES, LANES), F32)],
        compiler_params=_params(("arbitrary", "arbitrary")),
    )(P, P, P, P, cos, sin, g_ret, lg)


def _ret_bwd(P, cos, sin, g_ret, lg, s0, dmix, T, tb):
    nb = T // tb

    def body(rq, rk, rv, rg, c, s, g, l, s0_ref, do, drq, drk, drv, drg, dg, dstate):
        i = pl.program_id(1)

        @pl.when(i == 0)
        def _():
            dstate[...] = jnp.zeros(dstate.shape, F32)
            dg[...] = jnp.zeros(dg.shape, F32)

        cc, ss, ll = c[...], s[...], l[...]
        _, vjp = jax.vjp(lambda a, b, v, gate, st, gg: _ret_fn(a, b, v, gate, cc, ss, st, gg, ll),
                         rq[...], rk[...], rv[...], rg[...], s0_ref[0, 0], g[...])
        ga, gb, gv, ggate, gst, ggain = vjp((do[...], dstate[...]))
        drq[...] = ga.astype(drq.dtype)
        drk[...] = gb.astype(drk.dtype)
        drv[...] = gv.astype(drv.dtype)
        drg[...] = ggate.astype(drg.dtype)
        dstate[...] = gst
        dg[...] += ggain

    rev = lambda i: nb - 1 - i
    sec = lambda off: pl.BlockSpec((tb, LANES), lambda j, i: (rev(i), off + j))
    tab = pl.BlockSpec((tb, LANES), lambda j, i: (rev(i), 0))
    par = pl.BlockSpec((1, LANES), lambda j, i: (0, j))
    outb = pl.BlockSpec((tb, LANES), lambda j, i: (rev(i), j))
    return pl.pallas_call(
        body, name="ret_bwd", grid=(4, nb),
        in_specs=[sec(0), sec(4), sec(8), sec(12), tab, tab, par, par,
                  pl.BlockSpec((1, 1, LANES, LANES), lambda j, i: (j, rev(i), 0, 0)), outb],
        out_specs=[outb, outb, outb, outb, par],
        out_shape=[jax.ShapeDtypeStruct((T, 4 * LANES), BF)] * 4 + [jax.ShapeDtypeStruct((1, 4 * LANES), F32)],
        scratch_shapes=[pltpu.VMEM((LANES, LANES), F32)],
        compiler_params=_params(("arbitrary", "arbitrary")),
    )(P, P, P, P, cos, sin, g_ret, lg, s0, dmix)


_FB = 128


def _tri(lower):
    r = lax.broadcasted_iota(jnp.int32, (_FB, _FB), 0)
    c = lax.broadcasted_iota(jnp.int32, (_FB, _FB), 1)
    return ((r >= c) if lower else (r <= c)).astype(F32)


def _fgate_fwd(ffp, bpad, T):
    def body(ff_ref, b_ref, fc_ref, fr_ref):
        lane = lax.broadcasted_iota(jnp.int32, (1, LANES), 1)
        tri = _tri(True)
        carry = jnp.zeros((1, LANES), F32)
        for blk in range(T // _FB):
            z = ff_ref[blk * _FB:(blk + 1) * _FB, :] + b_ref[...]
            lf = jnp.where(lane < 8, jax.nn.log_sigmoid(z), 0.0)
            f = jnp.dot(tri, lf, precision=lax.Precision.HIGHEST, preferred_element_type=F32) + carry
            carry = f[_FB - 1:_FB, :]
            fc_ref[blk * _FB:(blk + 1) * _FB, :] = f
            fr_ref[:, blk * _FB:(blk + 1) * _FB] = f.T[:8, :]

    return pl.pallas_call(
        body, name="fgate_fwd",
        out_shape=[jax.ShapeDtypeStruct((T, LANES), F32), jax.ShapeDtypeStruct((8, T), F32)],
        compiler_params=pltpu.CompilerParams(vmem_limit_bytes=VMEM_LIMIT),
    )(ffp, bpad)


def _fgate_bwd(ffp, bpad, dfr, T):
    def body(ff_ref, b_ref, dfr_ref, dff_ref, db_ref):
        lane = lax.broadcasted_iota(jnp.int32, (1, LANES), 1)
        tri = _tri(False)
        carry = jnp.zeros((1, LANES), F32)
        db = jnp.zeros((1, LANES), F32)
        for blk in reversed(range(T // _FB)):
            d8 = dfr_ref[:, blk * _FB:(blk + 1) * _FB]
            dcol = jnp.concatenate([d8, jnp.zeros((_FB - 8, _FB), F32)], axis=0).T
            dlf = jnp.dot(tri, dcol, precision=lax.Precision.HIGHEST, preferred_element_type=F32) + carry
            carry = dlf[0:1, :]
            z = ff_ref[blk * _FB:(blk + 1) * _FB, :] + b_ref[...]
            dz = jnp.where(lane < 8, dlf * jax.nn.sigmoid(-z), 0.0)
            dff_ref[blk * _FB:(blk + 1) * _FB, :] = dz.astype(dff_ref.dtype)
            db = db + jnp.sum(dz, axis=0, keepdims=True)
        db_ref[...] = db

    return pl.pallas_call(
        body, name="fgate_bwd",
        out_shape=[jax.ShapeDtypeStruct((T, LANES), BF), jax.ShapeDtypeStruct((1, LANES), F32)],
        compiler_params=pltpu.CompilerParams(vmem_limit_bytes=VMEM_LIMIT),
    )(ffp, bpad, dfr)


def _head_bias_col(fc, head):
    lane = lax.broadcasted_iota(jnp.int32, (1, LANES), 1)
    return jnp.sum(jnp.where(lane == head, fc, 0.0), axis=-1, keepdims=True)


def _head_bias_row(fr, head):
    sub = lax.broadcasted_iota(jnp.int32, (8, 1), 0)
    return jnp.sum(jnp.where(sub == head, fr, 0.0), axis=0, keepdims=True)


def _fox_fwd(qn, kn, vb, fc, fr, T, tq):
    nq = T // tq

    def body(q_ref, k_ref, v_ref, fc_ref, fr_ref, o_ref, c_ref):
        j, i = pl.program_id(0), pl.program_id(1)
        lane = lax.broadcasted_iota(jnp.int32, (1, LANES), 1)
        lo = lane < HEAD
        causal = lax.broadcasted_iota(jnp.int32, (tq, tq), 0) >= lax.broadcasted_iota(jnp.int32, (tq, tq), 1)
        q = q_ref[...]
        fcb = fc_ref[...]
        outs, cs = [], []
        for hh in range(2):
            hm = lo if hh == 0 else ~lo
            head = 2 * j + hh
            qh = jnp.where(hm, q, jnp.zeros_like(q))
            fq = _head_bias_col(fcb, head)

            def block(kb, carry, diag, qh=qh, fq=fq, head=head):
                m, l, acc = carry
                k0 = pl.multiple_of(kb * tq, tq)
                k = k_ref[pl.ds(k0, tq), :]
                v = v_ref[pl.ds(k0, tq), :]
                fk = _head_bias_row(fr_ref[:, pl.ds(k0, tq)], head)
                s = (lax.dot_general(qh, k, NT, preferred_element_type=F32) + fq) - fk
                if diag:
                    s = jnp.where(causal, s, NEG)
                m2 = jnp.maximum(m, jnp.max(s, axis=-1, keepdims=True))
                p = jnp.exp(s - m2)
                a = jnp.exp(m - m2)
                return m2, a * l + jnp.sum(p, axis=-1, keepdims=True), a * acc + jnp.dot(p.astype(BF), v, preferred_element_type=F32)

            init = (jnp.full((tq, 1), NEG, F32), jnp.zeros((tq, 1), F32), jnp.zeros((tq, LANES), F32))
            carry = lax.fori_loop(0, i, lambda kb, c: block(kb, c, False), init)
            m, l, acc = block(i, carry, True)
            outs.append(acc / l)
            cs.append(fq - (m + jnp.log(l)))
        o_ref[...] = jnp.where(lo, outs[0], outs[1])
        c_ref[0] = jnp.where(lane == 0, cs[0], jnp.where(lane == 1, cs[1], 0.0))

    full = lambda: pl.BlockSpec((T, LANES), lambda j, i: (0, j))
    return pl.pallas_call(
        body, name="fox_fwd", grid=(4, nq),
        in_specs=[pl.BlockSpec((tq, LANES), lambda j, i: (i, j)), full(), full(),
                  pl.BlockSpec((tq, LANES), lambda j, i: (i, 0)), pl.BlockSpec((8, T), lambda j, i: (0, 0))],
        out_specs=[pl.BlockSpec((tq, LANES), lambda j, i: (i, j)), pl.BlockSpec((1, tq, LANES), lambda j, i: (j, i, 0))],
        out_shape=[jax.ShapeDtypeStruct((T, 4 * LANES), F32), jax.ShapeDtypeStruct((4, T, LANES), F32)],
        compiler_params=_params(("parallel", "arbitrary")),
    )(qn, kn, vb, fc, fr)


def _fox_bwd_dq(qn, kn, vb, fr, cq, dmix, T, tq):
    nq = T // tq

    def body(q_ref, k_ref, v_ref, fr_ref, c_ref, do_ref, dq_ref, dl_ref, p_scr, dp_scr):
        j, i = pl.program_id(0), pl.program_id(1)
        lane = lax.broadcasted_iota(jnp.int32, (1, LANES), 1)
        lo = lane < HEAD
        causal = lax.broadcasted_iota(jnp.int32, (tq, tq), 0) >= lax.broadcasted_iota(jnp.int32, (tq, tq), 1)
        q, do, cb = q_ref[...], do_ref[...], c_ref[0]
        res, deltas = [], []
        for hh in range(2):
            hm = lo if hh == 0 else ~lo
            head = 2 * j + hh
            qh = jnp.where(hm, q, jnp.zeros_like(q))
            doh = jnp.where(hm, do, 0.0).astype(BF)
            c = cb[:, hh:hh + 1]

            def probs(kb, delta, diag, qh=qh, doh=doh, c=c, head=head):
                k0 = pl.multiple_of(kb * tq, tq)
                k = k_ref[pl.ds(k0, tq), :]
                v = v_ref[pl.ds(k0, tq), :]
                fk = _head_bias_row(fr_ref[:, pl.ds(k0, tq)], head)
                p = jnp.exp((lax.dot_general(qh, k, NT, preferred_element_type=F32) + c) - fk)
                if diag:
                    p = jnp.where(causal, p, 0.0)
                dp = lax.dot_general(doh, v, NT, preferred_element_type=F32)
                p_scr[:, pl.ds(k0, tq)] = p
                dp_scr[:, pl.ds(k0, tq)] = dp
                return delta + jnp.sum(p * dp, axis=-1, keepdims=True)

            delta = lax.fori_loop(0, i, lambda kb, d: probs(kb, d, False), jnp.zeros((tq, 1), F32))
            delta = probs(i, delta, True)

            def grad(kb, acc, delta=delta):
                k0 = pl.multiple_of(kb * tq, tq)
                ds = p_scr[:, pl.ds(k0, tq)] * (dp_scr[:, pl.ds(k0, tq)] - delta)
                return acc + jnp.dot(ds.astype(BF), k_ref[pl.ds(k0, tq), :], preferred_element_type=F32)

            res.append(lax.fori_loop(0, i + 1, grad, jnp.zeros((tq, LANES), F32)))
            deltas.append(delta)
        dq_ref[...] = jnp.where(lo, res[0], res[1])
        dl_ref[0] = jnp.where(lane == 0, deltas[0], jnp.where(lane == 1, deltas[1], 0.0))

    full = lambda: pl.BlockSpec((T, LANES), lambda j, i: (0, j))
    return pl.pallas_call(
        body, name="fox_bwd_dq", grid=(4, nq),
        in_specs=[pl.BlockSpec((tq, LANES), lambda j, i: (i, j)), full(), full(), pl.BlockSpec((8, T), lambda j, i: (0, 0)),
                  pl.BlockSpec((1, tq, LANES), lambda j, i: (j, i, 0)), pl.BlockSpec((tq, LANES), lambda j, i: (i, 4 + j))],
        out_specs=[pl.BlockSpec((tq, LANES), lambda j, i: (i, j)), pl.BlockSpec((1, tq, LANES), lambda j, i: (j, i, 0))],
        out_shape=[jax.ShapeDtypeStruct((T, 4 * LANES), F32), jax.ShapeDtypeStruct((4, T, LANES), F32)],
        scratch_shapes=[pltpu.VMEM((tq, T), F32), pltpu.VMEM((tq, T), F32)],
        compiler_params=_params(("parallel", "arbitrary")),
    )(qn, kn, vb, fr, cq, dmix)


def _fox_bwd_dkv(qn, kn, vb, fr, cq, dl, dmix, T, tq):
    nq = T // tq

    def body(q_ref, k_ref, v_ref, fr_ref, c_ref, dl_ref, do_ref, dk_ref, dv_ref, dfr_ref):
        j, kb = pl.program_id(0), pl.program_id(1)
        lo = _lo_mask()
        sub = lax.broadcasted_iota(jnp.int32, (8, 1), 0)
        causal = lax.broadcasted_iota(jnp.int32, (tq, tq), 0) >= lax.broadcasted_iota(jnp.int32, (tq, tq), 1)
        k, v, frb = k_ref[...], v_ref[...], fr_ref[...]
        dks, dvs, dfs = [], [], []
        for hh in range(2):
            hm = lo if hh == 0 else ~lo
            head = 2 * j + hh
            km = jnp.where(hm, k, jnp.zeros_like(k))
            vm = jnp.where(hm, v, jnp.zeros_like(v))
            fk = _head_bias_row(frb, head)

            def block(qi, carry, diag, km=km, vm=vm, fk=fk, hm=hm, hh=hh):
                dk, dv, df = carry
                q0 = pl.multiple_of(qi * tq, tq)
                q = q_ref[pl.ds(q0, tq), :]
                c = c_ref[0, pl.ds(q0, tq), :][:, hh:hh + 1]
                delta = dl_ref[0, pl.ds(q0, tq), :][:, hh:hh + 1]
                dob = do_ref[pl.ds(q0, tq), :].astype(BF)
                p = jnp.exp((lax.dot_general(q, km, NT, preferred_element_type=F32) + c) - fk)
                if diag:
                    p = jnp.where(causal, p, 0.0)
                dv = dv + lax.dot_general(p.astype(BF), dob, TN, preferred_element_type=F32)
                dp = lax.dot_general(dob, vm, NT, preferred_element_type=F32)
                ds = p * (dp - delta)
                dk = dk + lax.dot_general(ds.astype(BF), q, TN, preferred_element_type=F32)
                return dk, dv, df - jnp.sum(ds, axis=0, keepdims=True)

            init = (jnp.zeros((tq, LANES), F32), jnp.zeros((tq, LANES), F32), jnp.zeros((1, tq), F32))
            carry = block(kb, init, True)
            dk, dv, df = lax.fori_loop(kb + 1, nq, lambda qi, cr: block(qi, cr, False), carry)
            dks.append(dk)
            dvs.append(dv)
            dfs.append(df)
        dk_ref[...] = jnp.where(lo, dks[0], dks[1])
        dv_ref[...] = jnp.where(lo, dvs[0], dvs[1]).astype(dv_ref.dtype)
        dfr_ref[0] = jnp.where(sub == 0, dfs[0], jnp.where(sub == 1, dfs[1], 0.0))

    full = lambda off: pl.BlockSpec((T, LANES), lambda j, kb: (0, off + j))
    blk = lambda: pl.BlockSpec((tq, LANES), lambda j, kb: (kb, j))
    return pl.pallas_call(
        body, name="fox_bwd_dkv", grid=(4, nq),
        in_specs=[full(0), blk(), blk(), pl.BlockSpec((8, tq), lambda j, kb: (0, kb)),
                  pl.BlockSpec((1, T, LANES), lambda j, kb: (j, 0, 0)), pl.BlockSpec((1, T, LANES), lambda j, kb: (j, 0, 0)), full(4)],
        out_specs=[blk(), blk(), pl.BlockSpec((1, 8, tq), lambda j, kb: (j, 0, kb))],
        out_shape=[jax.ShapeDtypeStruct((T, 4 * LANES), F32), jax.ShapeDtypeStruct((T, 4 * LANES), BF),
                   jax.ShapeDtypeStruct((4, 8, T), F32)],
        compiler_params=_params(("parallel", "arbitrary")),
    )(qn, kn, vb, fr, cq, dl, dmix)


_BIAS_LANE = HEAD


def _split3(f):
    hi = f.astype(BF).astype(F32)
    mid = (f - hi).astype(BF).astype(F32)
    lo = ((f - hi) - mid).astype(BF).astype(F32)
    return hi, mid, lo


def _fox_operands(P, fc, g_fq2, g_fk2, T, tm):
    def body(fq_ref, fk_ref, fv_ref, fc_ref, gq_ref, gk_ref, qa_ref, qat_ref, ka_ref, kat_ref, va_ref, vat_ref):
        j = pl.program_id(0)
        lane = lax.broadcasted_iota(jnp.int32, (1, LANES), 1)
        qn, kn = _fox_prep_fn(fq_ref[...], fk_ref[...], gq_ref[...], gk_ref[...])
        v = fv_ref[...]
        fcb = fc_ref[...]
        b = _BIAS_LANE
        for hh in range(2):
            hi, mid, lo = _split3(_head_bias_col(fcb, 2 * j + hh))
            take = (lambda a: a) if hh == 0 else (lambda a: pltpu.roll(a, HEAD, 1))
            qa = jnp.where(lane < HEAD, take(qn), jnp.where(lane == b, hi, jnp.where(lane == b + 1, mid, jnp.where(
                lane == b + 2, lo, jnp.where(lane < b + 6, 1.0, 0.0)))))
            ka = jnp.where(lane < HEAD, take(kn), jnp.where(lane < b + 3, 1.0, jnp.where(lane == b + 3, -hi, jnp.where(
                lane == b + 4, -mid, jnp.where(lane == b + 5, -lo, 0.0)))))
            va = jnp.where(lane < HEAD, take(v), 0.0)
            for val, ref, tref in ((qa, qa_ref, qat_ref), (ka, ka_ref, kat_ref), (va, va_ref, vat_ref)):
                ref[hh] = val.astype(BF)
                tref[hh] = val.T.astype(BF)

    sec = lambda off: pl.BlockSpec((tm, LANES), lambda j, i: (i, off + j))
    par = pl.BlockSpec((1, LANES), lambda j, i: (0, 0))
    nat = pl.BlockSpec((2, tm, LANES), lambda j, i: (j, i, 0))
    trn = pl.BlockSpec((2, LANES, tm), lambda j, i: (j, 0, i))
    return pl.pallas_call(
        body, name="fox_operands", grid=(4, T // tm),
        in_specs=[sec(16), sec(20), sec(24), pl.BlockSpec((tm, LANES), lambda j, i: (i, 0)), par, par],
        out_specs=[nat, trn, nat, trn, nat, trn],
        out_shape=[jax.ShapeDtypeStruct((8, T, LANES), BF), jax.ShapeDtypeStruct((8, LANES, T), BF)] * 3,
        compiler_params=_params(("parallel", "arbitrary")),
    )(P, P, P, fc, g_fq2, g_fk2)


def _fox_forward(qat, ka, vat, T, tq, tk):
    nq, per = T // tq, tq // tk
    assert per == 2
    RC = 64

    def body(qat_ref, ka_ref, vat_ref, o_ref, lse_ref, s_scr, p_scr, a_scr, m_scr, l_scr, acc_scr):
        i = pl.program_id(1)
        sub = lax.broadcasted_iota(jnp.int32, (8, 1), 0)
        row = lax.broadcasted_iota(jnp.int32, (RC, tq), 0)
        col = lax.broadcasted_iota(jnp.int32, (RC, tq), 1)
        m_scr[...] = jnp.full(m_scr.shape, NEG, F32)
        l_scr[...] = jnp.zeros(l_scr.shape, F32)
        acc_scr[...] = jnp.zeros(acc_scr.shape, F32)

        def scores(slot, kb):
            k0 = pl.multiple_of(kb * tk, tk)
            for hh in range(2):
                s_scr[slot, hh] = jnp.dot(ka_ref[hh, pl.ds(k0, tk), :], qat_ref[hh], preferred_element_type=F32)

        def softmax(slot, kb, diagonal):
            shift = kb * tk - i * tq
            for hh in range(2):
                def masked(r):
                    tile = s_scr[slot, hh, r * RC:(r + 1) * RC, :]
                    return jnp.where(row + (r * RC + shift) <= col, tile, NEG) if diagonal else tile

                mx = jnp.max(masked(0), axis=0, keepdims=True)
                for r in range(1, tk // RC):
                    mx = jnp.maximum(mx, jnp.max(masked(r), axis=0, keepdims=True))
                m_old = m_scr[hh, 0:1, :]
                m2 = jnp.maximum(m_old, mx)
                a = jnp.exp(m_old - m2)
                lsum = jnp.zeros((1, tq), F32)
                for r in range(tk // RC):
                    p = jnp.exp(masked(r) - m2)
                    p_scr[slot, hh, r * RC:(r + 1) * RC, :] = p.astype(BF)
                    lsum = lsum + jnp.sum(p, axis=0, keepdims=True)
                m_scr[hh] = jnp.broadcast_to(m2, (8, tq))
                l_scr[hh] = jnp.broadcast_to(a * l_scr[hh, 0:1, :] + lsum, (8, tq))
                a_scr[slot, hh] = jnp.broadcast_to(a, (8, tq))

        def values(slot, kb):
            k0 = pl.multiple_of(kb * tk, tk)
            for hh in range(2):
                pv = jnp.dot(vat_ref[hh, 0:HEAD, pl.ds(k0, tk)], p_scr[slot, hh], preferred_element_type=F32)
                acc_scr[hh] = a_scr[slot, hh, 0:1, :] * acc_scr[hh] + pv

        def pair(kb, diag_first, diag_second, more):
            if more:
                scores(0, kb + 2)
            softmax(1, kb + 1, diag_first)
            values(0, kb)
            if more:
                scores(1, kb + 3)
                softmax(0, kb + 2, diag_second)
            values(1, kb + 1)

        scores(0, 0)
        scores(1, 1)
        softmax(0, 0, True)

        @pl.loop(0, jnp.maximum(i - 1, 0))
        def _(t):
            pair(2 * t, False, False, True)

        @pl.when(i >= 1)
        def _():
            pair(2 * (i - 1), False, True, True)

        pair(2 * i, True, False, False)

        o_ref[...] = jnp.concatenate([acc_scr[hh] / l_scr[hh, 0:1, :] for hh in range(2)], axis=0).T
        lses = [m_scr[hh, 0:1, :] + jnp.log(l_scr[hh, 0:1, :]) for hh in range(2)]
        lse_ref[0] = jnp.where(sub == 0, lses[0], jnp.where(sub == 1, lses[1], 0.0))

    return pl.pallas_call(
        body, name="fox_forward", grid=(4, nq),
        in_specs=[pl.BlockSpec((2, LANES, tq), lambda j, i: (j, 0, i)), pl.BlockSpec((2, T, LANES), lambda j, i: (j, 0, 0)),
                  pl.BlockSpec((2, LANES, T), lambda j, i: (j, 0, 0))],
        out_specs=[pl.BlockSpec((tq, LANES), lambda j, i: (i, j)), pl.BlockSpec((1, 8, tq), lambda j, i: (j, 0, i))],
        out_shape=[jax.ShapeDtypeStruct((T, 4 * LANES), F32), jax.ShapeDtypeStruct((4, 8, T), F32)],
        scratch_shapes=[pltpu.VMEM((2, 2, tk, tq), F32), pltpu.VMEM((2, 2, tk, tq), BF), pltpu.VMEM((2, 2, 8, tq), F32),
                        pltpu.VMEM((2, 8, tq), F32), pltpu.VMEM((2, 8, tq), F32), pltpu.VMEM((2, HEAD, tq), F32)],
        compiler_params=_params(("parallel", "arbitrary")),
    )(qat, ka, vat)


def _fox_cotangent(dmix, fox, T, tm):
    def body(do_ref, o_ref, doa_ref, doat_ref, dl_ref):
        lane = lax.broadcasted_iota(jnp.int32, (1, LANES), 1)
        sub = lax.broadcasted_iota(jnp.int32, (8, 1), 0)
        dob = do_ref[...].astype(BF).astype(F32)
        prod_t = (dob * o_ref[...]).T
        d0 = jnp.sum(prod_t[:HEAD], axis=0, keepdims=True)
        d1 = jnp.sum(prod_t[HEAD:], axis=0, keepdims=True)
        dl_ref[0] = jnp.where(sub == 0, d0, jnp.where(sub == 1, d1, 0.0))
        for hh in range(2):
            val = jnp.where(lane < HEAD, dob if hh == 0 else pltpu.roll(dob, HEAD, 1), 0.0)
            doa_ref[hh] = val.astype(BF)
            doat_ref[hh] = val.T.astype(BF)

    return pl.pallas_call(
        body, name="fox_cotangent", grid=(4, T // tm),
        in_specs=[pl.BlockSpec((tm, LANES), lambda j, i: (i, 4 + j)), pl.BlockSpec((tm, LANES), lambda j, i: (i, j))],
        out_specs=[pl.BlockSpec((2, tm, LANES), lambda j, i: (j, i, 0)), pl.BlockSpec((2, LANES, tm), lambda j, i: (j, 0, i)),
                   pl.BlockSpec((1, 8, tm), lambda j, i: (j, 0, i))],
        out_shape=[jax.ShapeDtypeStruct((8, T, LANES), BF), jax.ShapeDtypeStruct((8, LANES, T), BF),
                   jax.ShapeDtypeStruct((4, 8, T), F32)],
        compiler_params=_params(("parallel", "arbitrary")),
    )(dmix, fox)


def _fox_backward(qa, qat, ka, kat, va, doa, doat, lse, dl, T, tq, tk):
    nq, nk = T // tq, T // tk

    def body(qa_ref, qat_ref, ka_ref, kat_ref, va_ref, doa_ref, doat_ref, lse_ref, dl_ref,
             dq_ref, dk_ref, dv_ref, df_ref, dr_ref, dqt, dk_acc, dv_acc, df_acc, sdp, pds):
        j, kb = pl.program_id(0), pl.program_id(1)
        lane = lax.broadcasted_iota(jnp.int32, (1, LANES), 1)
        first = (kb * tk) // tq

        @pl.when(kb == 0)
        def _():
            dqt[...] = jnp.zeros(dqt.shape, F32)

        dk_acc[...] = jnp.zeros(dk_acc.shape, F32)
        dv_acc[...] = jnp.zeros(dv_acc.shape, F32)
        df_acc[...] = jnp.zeros(df_acc.shape, F32)

        RC = 64
        last = nq - 1

        def products(slot, qi):
            q0 = pl.multiple_of(qi * tq, tq)
            for hh in range(2):
                sdp[slot, hh, 0] = jnp.dot(ka_ref[hh], qat_ref[hh, :, pl.ds(q0, tq)], preferred_element_type=F32)
                sdp[slot, hh, 1] = jnp.dot(va_ref[hh], doat_ref[hh, :, pl.ds(q0, tq)], preferred_element_type=F32)

        def softmax_bwd(slot, qi, diagonal, valid):
            q0 = pl.multiple_of(qi * tq, tq)
            shift = kb * tk - first * tq
            col = lax.broadcasted_iota(jnp.int32, (RC, tq), 1)
            row = lax.broadcasted_iota(jnp.int32, (RC, tq), 0)
            for hh in range(2):
                lse_row = lse_ref[0, hh:hh + 1, pl.ds(q0, tq)]
                dl_row = dl_ref[0, hh:hh + 1, pl.ds(q0, tq)]
                rsum = jnp.zeros((1, tq), F32)
                for r in range(tk // RC):
                    rows = slice(r * RC, (r + 1) * RC)
                    p = jnp.exp(sdp[slot, hh, 0, rows, :] - lse_row)
                    p = jnp.where((row + (r * RC + shift) <= col) if diagonal else valid, p, 0.0)
                    ds = p * (sdp[slot, hh, 1, rows, :] - dl_row)
                    pds[slot, hh, 0, rows, :] = p.astype(BF)
                    pds[slot, hh, 1, rows, :] = ds.astype(BF)
                    rsum = rsum + jnp.sum(ds, axis=0, keepdims=True)
                    part = ds[:, 0:LANES]
                    for c in range(1, tq // LANES):
                        part = part + ds[:, c * LANES:(c + 1) * LANES]
                    df_acc[hh, rows, :] += part
                dqt[hh, HEAD:HEAD + 8, pl.ds(q0, tq)] += jnp.broadcast_to(rsum, (8, tq))

        def accumulate(slot, qi):
            q0 = pl.multiple_of(qi * tq, tq)
            for hh in range(2):
                dv_acc[hh] += jnp.dot(pds[slot, hh, 0], doa_ref[hh, pl.ds(q0, tq), :], preferred_element_type=F32)
                dk_acc[hh] += jnp.dot(pds[slot, hh, 1], qa_ref[hh, pl.ds(q0, tq), :], preferred_element_type=F32)
                dqt[hh, 0:HEAD, pl.ds(q0, tq)] += jnp.dot(kat_ref[hh, 0:HEAD, :], pds[slot, hh, 1], preferred_element_type=F32)

        products(0, first)
        products(1, jnp.minimum(first + 1, last))
        softmax_bwd(0, first, True, None)

        @pl.loop(0, (nq - first + 1) // 2)
        def _(t):
            qi = first + 2 * t
            products(0, jnp.minimum(qi + 2, last))
            softmax_bwd(1, jnp.minimum(qi + 1, last), False, qi + 1 <= last)
            accumulate(0, qi)
            products(1, jnp.minimum(qi + 3, last))
            softmax_bwd(0, jnp.minimum(qi + 2, last), False, qi + 2 <= last)
            accumulate(1, jnp.minimum(qi + 1, last))

        lo = lane < HEAD
        dk_ref[...] = jnp.where(lo, dk_acc[0], pltpu.roll(dk_acc[1], HEAD, 1))
        dv_ref[...] = jnp.where(lo, dv_acc[0], pltpu.roll(dv_acc[1], HEAD, 1)).astype(dv_ref.dtype)
        f0 = -jnp.sum(df_acc[0], axis=1, keepdims=True)
        f1 = -jnp.sum(df_acc[1], axis=1, keepdims=True)
        df_ref[0] = jnp.where(lane == 2 * j, f0, jnp.where(lane == 2 * j + 1, f1, 0.0))

        @pl.when(kb == nk - 1)
        def _():
            for t in range(nq):
                cols = slice(t * tq, (t + 1) * tq)
                dq_ref[cols, :] = jnp.concatenate([dqt[0, 0:HEAD, cols], dqt[1, 0:HEAD, cols]], axis=0).T
                rsum = jnp.concatenate([dqt[0, HEAD:HEAD + 8, cols], dqt[1, HEAD:HEAD + 8, cols],
                                        jnp.zeros((LANES - 16, tq), F32)], axis=0).T
                dr_ref[0, cols, :] = jnp.where(lane == 2 * j, rsum[:, 0:1], jnp.where(lane == 2 * j + 1, rsum[:, 8:9], 0.0))

    nat_full = pl.BlockSpec((2, T, LANES), lambda j, kb: (j, 0, 0))
    trn_full = pl.BlockSpec((2, LANES, T), lambda j, kb: (j, 0, 0))
    nat_blk = pl.BlockSpec((2, tk, LANES), lambda j, kb: (j, kb, 0))
    trn_blk = pl.BlockSpec((2, LANES, tk), lambda j, kb: (j, 0, kb))
    rows = pl.BlockSpec((1, 8, T), lambda j, kb: (j, 0, 0))
    blk = pl.BlockSpec((tk, LANES), lambda j, kb: (kb, j))
    return pl.pallas_call(
        body, name="fox_backward", grid=(4, nk),
        in_specs=[nat_full, trn_full, nat_blk, trn_blk, nat_blk, nat_full, trn_full, rows, rows],
        out_specs=[pl.BlockSpec((T, LANES), lambda j, kb: (0, j)), blk, blk, pl.BlockSpec((1, tk, LANES), lambda j, kb: (j, kb, 0)),
                   pl.BlockSpec((1, T, LANES), lambda j, kb: (j, 0, 0))],
        out_shape=[jax.ShapeDtypeStruct((T, 4 * LANES), F32), jax.ShapeDtypeStruct((T, 4 * LANES), F32),
                   jax.ShapeDtypeStruct((T, 4 * LANES), BF), jax.ShapeDtypeStruct((4, T, LANES), F32),
                   jax.ShapeDtypeStruct((4, T, LANES), F32)],
        scratch_shapes=[pltpu.VMEM((2, HEAD + 8, T), F32), pltpu.VMEM((2, tk, LANES), F32), pltpu.VMEM((2, tk, LANES), F32),
                        pltpu.VMEM((2, tk, LANES), F32), pltpu.VMEM((2, 2, 2, tk, tq), F32), pltpu.VMEM((2, 2, 2, tk, tq), BF)],
        compiler_params=_params(("arbitrary", "arbitrary")),
    )(qa, qat, ka, kat, va, doa, doat, lse, dl)


def _fgate_bwd_col(ffp, bpad, dfc4, drc4, T):
    def body(ff_ref, b_ref, dfc_ref, drc_ref, dff_ref, db_ref):
        lane = lax.broadcasted_iota(jnp.int32, (1, LANES), 1)
        tri = _tri(False)
        carry = jnp.zeros((1, LANES), F32)
        db = jnp.zeros((1, LANES), F32)
        for blk in reversed(range(T // _FB)):
            rows = slice(blk * _FB, (blk + 1) * _FB)
            dcol = dfc_ref[0, rows, :] + drc_ref[0, rows, :]
            for pair in range(1, 4):
                dcol = dcol + (dfc_ref[pair, rows, :] + drc_ref[pair, rows, :])
            dlf = jnp.dot(tri, dcol, precision=lax.Precision.HIGHEST, preferred_element_type=F32) + carry
            carry = dlf[0:1, :]
            z = ff_ref[blk * _FB:(blk + 1) * _FB, :] + b_ref[...]
            dz = jnp.where(lane < 8, dlf * jax.nn.sigmoid(-z), 0.0)
            dff_ref[blk * _FB:(blk + 1) * _FB, :] = dz.astype(dff_ref.dtype)
            db = db + jnp.sum(dz, axis=0, keepdims=True)
        db_ref[...] = db

    return pl.pallas_call(
        body, name="fgate_bwd",
        out_shape=[jax.ShapeDtypeStruct((T, LANES), BF), jax.ShapeDtypeStruct((1, LANES), F32)],
        compiler_params=pltpu.CompilerParams(vmem_limit_bytes=VMEM_LIMIT),
    )(ffp, bpad, dfc4, drc4)


MESH = pl.DeviceIdType.MESH


def _place():
    return lax.axis_index("x"), lax.axis_index("y"), lax.axis_index("c")


def _all_gather(shard):
    R, W = shard.shape

    def body(x_ref, out_ref, send_sems, recv_sems, local_sem):
        x, y, c = _place()
        me, sibling = (x, y, c), (x, y, 1 - c)
        chips = [(1 - x, y), (x, 1 - y), (1 - x, 1 - y)]

        def slot(px, py, pc):
            return out_ref.at[4 * px + 2 * py + pc]

        def copy(k, block, to, src=None):
            return pltpu.make_async_remote_copy(
                src_ref=slot(*block) if src is None else src, dst_ref=slot(*block),
                send_sem=send_sems.at[k], recv_sem=recv_sems.at[k], device_id=to, device_id_type=MESH)

        mine = pltpu.make_async_copy(x_ref, slot(*me), local_sem)
        mine.start()
        first = [copy(0, me, sibling, src=x_ref)]
        first += [copy(1 + n, me, (*chip, c), src=x_ref) for n, chip in enumerate(chips)]
        for cp in first:
            cp.start()
        passed = [copy(4 + n, (*chip, c), sibling) for n, chip in enumerate(chips)]
        for n, chip in enumerate(chips):
            copy(1 + n, (*chip, c), me).wait_recv()
            passed[n].start()
        copy(0, sibling, me).wait_recv()
        for n, chip in enumerate(chips):
            copy(4 + n, (*chip, 1 - c), me).wait_recv()
        for cp in first + passed:
            cp.wait_send()
        mine.wait()

    return pl.pallas_call(
        body, name="all_gather_weights",
        out_shape=jax.ShapeDtypeStruct((N_DEV, R, W), shard.dtype),
        in_specs=[pl.BlockSpec(memory_space=pl.ANY)], out_specs=pl.BlockSpec(memory_space=pl.ANY),
        scratch_shapes=[pltpu.SemaphoreType.DMA((7,)), pltpu.SemaphoreType.DMA((7,)), pltpu.SemaphoreType.DMA],
    )(shard)


def _all_to_all(big, small):
    def body(big_ref, small_ref, rbig_ref, rsmall_ref, send_sems, recv_sems, local_sems):
        x, y, c = _place()
        me = 4 * x + 2 * y + c
        l0 = pltpu.make_async_copy(big_ref.at[me], rbig_ref.at[me], local_sems.at[0])
        l1 = pltpu.make_async_copy(small_ref, rsmall_ref.at[me], local_sems.at[1])
        l0.start()
        l1.start()
        copies = []
        for r in range(1, N_DEV):
            px, py, pc = x ^ (r >> 2), y ^ ((r >> 1) & 1), c ^ (r & 1)
            peer = 4 * px + 2 * py + pc
            copies.append(pltpu.make_async_remote_copy(
                src_ref=big_ref.at[peer], dst_ref=rbig_ref.at[me], send_sem=send_sems.at[2 * r], recv_sem=recv_sems.at[2 * r],
                device_id=(px, py, pc), device_id_type=MESH))
            copies.append(pltpu.make_async_remote_copy(
                src_ref=small_ref, dst_ref=rsmall_ref.at[me], send_sem=send_sems.at[2 * r + 1], recv_sem=recv_sems.at[2 * r + 1],
                device_id=(px, py, pc), device_id_type=MESH))
        for cp in copies:
            cp.start()
        for cp in copies:
            cp.wait_recv()
        for cp in copies:
            cp.wait_send()
        l0.wait()
        l1.wait()

    return pl.pallas_call(
        body, name="all_to_all_grads",
        out_shape=[jax.ShapeDtypeStruct(big.shape, big.dtype), jax.ShapeDtypeStruct((N_DEV,) + small.shape, small.dtype)],
        in_specs=[pl.BlockSpec(memory_space=pl.ANY)] * 2, out_specs=[pl.BlockSpec(memory_space=pl.ANY)] * 2,
        scratch_shapes=[pltpu.SemaphoreType.DMA((2 * N_DEV,)), pltpu.SemaphoreType.DMA((2 * N_DEV,)), pltpu.SemaphoreType.DMA((2,))],
    )(big, small)


def _exchange_copies(src_ref, land_ref, send_sems, recv_sems, scatter):
    x, y, c = _place()
    me = 4 * x + 2 * y + c
    copies = []
    for r in range(1, N_DEV):
        px, py, pc = x ^ (r >> 2), y ^ ((r >> 1) & 1), c ^ (r & 1)
        copies.append(pltpu.make_async_remote_copy(
            src_ref=src_ref.at[4 * px + 2 * py + pc] if scatter else src_ref, dst_ref=land_ref.at[me],
            send_sem=send_sems.at[r - 1], recv_sem=recv_sems.at[r - 1], device_id=(px, py, pc), device_id_type=MESH))
    return copies


_HBM = pl.BlockSpec(memory_space=pltpu.HBM)
_SEM = pl.BlockSpec(memory_space=pltpu.SEMAPHORE)
_EFFECT = pltpu.SideEffectType.DATAFLOW_SIDE_EFFECTING


def _exchange_start(name, src, land, scatter):
    def body(src_ref, land_ref, send_sems, recv_sems, src_thru, land_thru, token):
        for cp in _exchange_copies(src_ref, land_ref, send_sems, recv_sems, scatter):
            cp.start()
        token[...] = jnp.zeros(token.shape, F32)

    return pl.pallas_call(
        body, name=name,
        out_shape=(pltpu.SemaphoreType.DMA((N_DEV - 1,)), pltpu.SemaphoreType.DMA((N_DEV - 1,)),
                   pltpu.HBM(src.shape, src.dtype), pltpu.HBM(land.shape, land.dtype), jax.ShapeDtypeStruct((8, LANES), F32)),
        in_specs=(_HBM, _HBM), out_specs=(_SEM, _SEM, _HBM, _HBM, pl.BlockSpec(memory_space=pltpu.VMEM)),
        input_output_aliases={0: 2, 1: 3},
        compiler_params=pltpu.CompilerParams(has_side_effects=_EFFECT),
    )(pltpu.with_memory_space_constraint(src, pltpu.HBM), pltpu.with_memory_space_constraint(land, pltpu.HBM))


def _exchange_wait(name, started, after, scatter):
    send_sems, recv_sems, src_thru, land_thru, _ = started

    def body(src_ref, land_ref, send_sems, recv_sems, after_ref, src_dead, got_ref):
        copies = _exchange_copies(src_ref, land_ref, send_sems, recv_sems, scatter)
        for cp in copies:
            cp.wait_send()
        for cp in copies:
            cp.wait_recv()

    return pl.pallas_call(
        body, name=name,
        out_shape=(pltpu.HBM(src_thru.shape, src_thru.dtype), pltpu.HBM(land_thru.shape, land_thru.dtype)),
        in_specs=(_HBM, _HBM, _SEM, _SEM, pl.BlockSpec(memory_space=pl.ANY)), out_specs=(_HBM, _HBM),
        input_output_aliases={0: 0, 1: 1},
        compiler_params=pltpu.CompilerParams(has_side_effects=_EFFECT),
    )(src_thru, land_thru, send_sems, recv_sems, after)


def _adamw(name, slots, w, m, v, tr, own=None):
    R, W = w.shape

    def body(s_ref, *refs):
        if own is not None:
            own_ref, refs = refs[0], refs[1:]
        w_ref, m_ref, v_ref, g_ref, d_ref, nm_ref, nv_ref = refs
        g = s_ref[0].astype(F32)
        for s in range(1, N_DEV):
            g = g + s_ref[s].astype(F32)
        if own is not None:
            g = g + own_ref[...].astype(F32)
        m2 = ADAM_B1 * m_ref[...] + (1.0 - ADAM_B1) * g
        v2 = ADAM_B2 * v_ref[...] + (1.0 - ADAM_B2) * jnp.square(g)
        m_hat = m2 / (1.0 - ADAM_B1 ** ADAM_STEP)
        v_hat = v2 / (1.0 - ADAM_B2 ** ADAM_STEP)
        g_ref[...] = g
        d_ref[...] = -ADAM_LR * (m_hat / (jnp.sqrt(v_hat) + ADAM_EPS) + ADAM_WD * w_ref[...])
        nm_ref[...] = m2
        nv_ref[...] = v2

    row = lambda: pl.BlockSpec((tr, W), lambda i: (i, 0))
    return pl.pallas_call(
        body, name=name, grid=(R // tr,),
        in_specs=[pl.BlockSpec((N_DEV, tr, W), lambda i: (0, i, 0))] + [row() for _ in range(3 + (own is not None))],
        out_specs=[row(), row(), row(), row()],
        out_shape=[jax.ShapeDtypeStruct((R, W), F32)] * 4,
        compiler_params=_params(("parallel",)),
    )(slots, *([own] if own is not None else []), w, m, v)


def _tables(T):
    pos = jnp.arange(T, dtype=F32)
    inv_freq = 10000.0 ** (-jnp.arange(0, HEAD, 2, dtype=F32) / HEAD)
    ang = pos[:, None] * inv_freq[None, :]
    cos, sin = jnp.cos(ang), jnp.sin(ang)
    cos4 = jnp.tile(cos, (1, 4))
    sin4 = jnp.tile(jnp.concatenate([-sin, sin], axis=1), (1, 2))
    log_g = jnp.log(1.0 - 2.0 ** (-5.0 - jnp.arange(8, dtype=F32)))
    return cos4, sin4, jnp.repeat(log_g, HEAD)[None, :]


def _local_step(x, mem, target, sp, w_inT, token, fetch_rest, push, push_small):
    T = x.shape[0]
    tm = min(512, T)
    tq = min(256, T)
    tb = min(1024, T)
    cos4, sin4, lg = _tables(T)
    g_fq2 = jnp.tile(sp["g_fox_q"], (1, 2))
    g_fk2 = jnp.tile(sp["g_fox_k"], (1, 2))
    g_ret = sp["g_ret_out"].reshape(1, 8 * HEAD)
    bpad = jnp.pad(sp["b_forget"], ((0, 0), (0, LANES - 8)))
    w_secs = [w_inT[k * 512:(k + 1) * 512] for k in range(7)]
    w_ffT = jnp.pad(w_inT[3584:3592], ((0, LANES - 8), (0, 0)))
    w_mainT = w_inT[:3584]
    tie = lambda p, tok: p + tok[0:1, 0:1]
    tm2, tm4 = min(1024, T), min(2048, T)

    hn1, = _rw_fwd("rms_mix", _rms_fn, [(x, D, 0, False)], [(tie(sp["g_mix"], token), D, 0, False)], [(BF, D)], T, tm, 1)
    P, = _mm("proj_in", [[(hn1, w_mainT, "nt")]], [], _ident, T, 3584, tm4, 512, [F32])
    ffp, = _mm("proj_ff", [[(hn1, w_ffT, "nt")]], [], _ident, T, LANES, tm, LANES, [F32])
    ret, s0 = _ret_fwd(P, cos4, sin4, g_ret, lg, T, tb)
    fc, _ = _fgate_fwd(ffp, bpad, T)
    qa, qat, ka, kat, va, vat = _fox_operands(P, fc, g_fq2, g_fk2, T, tm)
    fox, lse = _fox_forward(qat, ka, vat, T, min(512, T), tq)
    W = fetch_rest(fox)
    w_out_halves = (W["w_out"][:4 * LANES], W["w_out"][4 * LANES:])
    h1, hn2 = _mm("proj_out", [[(ret, w_out_halves[0], "nn"), (fox, w_out_halves[1], "nn")]], [x], _add_rms_epi, T, D, tm2, D,
                  [F32, BF], params=[sp["g_xattn"]])

    qx, = _mm("proj_xq", [[(hn2, W["w_xq"], "nn")]], [], _ident, T, D, tm2, D, [F32])
    memn, = _rw_fwd("rms_mem", _rms_fn, [(mem, D, 0, False)], [(sp["g_mem"], D, 0, False)], [(BF, D)], N_MEM, N_MEM, 1)
    kv, = _mm("proj_xkv", [[(memn, W["w_xkvT"], "nt")]], [], _ident, N_MEM, 2 * D, N_MEM, 512, [F32])
    xa_rows = [(qx, XHEAD, 0, True)]
    xa_params = [(sp["g_xq"], XHEAD, 0, False), (sp["g_xk"], XHEAD, 0, False), (kv, XHEAD, 0, True), (kv, XHEAD, 4, True)]
    xo, = _rw_fwd("xattn_fwd", _xattn_fn, xa_rows, xa_params, [(BF, XHEAD)], T, tm, 4)
    h2, hn3 = _mm("proj_xo", [[(xo, W["w_xo"], "nn")]], [h1], _add_rms_epi, T, D, tm2, D, [F32, BF], params=[sp["g_ffn"]])

    gate, up, act = _mm("ffn_in", [[(hn3, W["w_gateT"], "nt")], [(hn3, W["w_upT"], "nt")]], [], _swiglu_fwd_epi,
                        T, D_FF, tm4, 256, [BF, BF, BF])
    dy, dyb, loss_part = _mm("ffn_out", [[(act, W["w_down"], "nn")]], [h2, target], _add_loss_epi, T, D, tm, D, [F32, BF], n_acc=1)

    dgate, dup = _mm("ffn_out_bwd", [[(dyb, W["w_down"], "nt")]], [gate, up], _swiglu_bwd_epi, T, D_FF, tm4, 256, [BF, BF])
    gW = {}
    gW["w_gateT"], = _mm("dw_gate", [[(dgate, hn3, "tn")]], [], _ident, D_FF, D, 256, D, [BF])
    gW["w_upT"], = _mm("dw_up", [[(dup, hn3, "tn")]], [], _ident, D_FF, D, 256, D, [BF])
    gW["w_down"], = _mm("dw_down", [[(act, dyb, "tn")]], [], _ident, D_FF, D, 256, D, [BF])
    tok = push("ffn", gW)
    gs = {}
    dh2, dh2b, gs["g_ffn"] = _mm("ffn_in_bwd", [[(dgate, W["w_gateT"], "nn"), (dup, W["w_upT"], "nn")]], [h2, dy], _rms_bwd_epi,
                                 T, D, min(256, T), D, [F32, BF], params=[tie(sp["g_ffn"], tok)], n_acc=1)

    dxo, = _mm("proj_xo_bwd", [[(dh2b, W["w_xo"], "nt")]], [], _ident, T, D, tm2, D, [BF])
    gW["w_xo"], = _mm("dw_xo", [[(xo, dh2b, "tn")]], [], _ident, D, D, 256, D, [BF])
    dqx, gs["g_xq"], gs["g_xk"], dkv_k, dkv_v = _rw_bwd(
        "xattn_bwd", _xattn_fn, xa_rows, xa_params, [(dxo, XHEAD, 0, True)], T, tm, 4, [BF], [True, True, True, True])
    dkv = jnp.concatenate([dkv_k[:, :D], dkv_v[:, D:]], axis=1)
    gW["w_xq"], = _mm("dw_xq", [[(hn2, dqx, "tn")]], [], _ident, D, D, 256, D, [BF])
    dmemn, = _mm("proj_xkv_bwd", [[(dkv, W["w_xkvT"], "nn")]], [], _ident, N_MEM, D, N_MEM, 512, [F32])
    gW["w_xkvT"], = _mm("dw_xkv", [[(dkv, memn, "tn")]], [], _ident, 2 * D, D, 512, D, [BF])
    tok = push("xattn", gW)
    gs["g_mem"], = _rw_bwd("rms_mem_bwd", _rms_fn, [(mem, D, 0, False)], [(sp["g_mem"], D, 0, False)], [(dmemn, D, 0, False)],
                           N_MEM, N_MEM, 1, [None], [True])
    dh1, dh1b, gs["g_xattn"] = _mm("proj_xq_bwd", [[(dqx, W["w_xq"], "nt")]], [h1, dh2], _rms_bwd_epi, T, D, tm, D, [F32, BF],
                                   params=[tie(sp["g_xattn"], tok)], n_acc=1)

    dmix, = _mm("proj_out_bwd", [[(dh1b, W["w_out"], "nt")]], [], _ident, T, D, tm2, D, [F32])
    gW["w_out"] = jnp.concatenate([_mm("dw_out_%d" % k, [[(a, dh1b, "tn")]], [], _ident, 4 * LANES, D, 256, D, [BF])[0]
                                   for k, a in enumerate((ret, fox))], axis=0)
    tok = push("out", gW)
    doa, doat, dl = _fox_cotangent(dmix, fox, T, tm)
    dqn, dkn, dfv, dfc4, drc4 = _fox_backward(qa, qat, ka, kat, va, doa, doat, lse + tok[0:1, 0:1], dl, T, tq, tq)
    dfq, dfk, gq2, gk2 = _rw_bwd("fox_prep_bwd", _fox_prep_fn, [(P, LANES, 16, True), (P, LANES, 20, True)],
                                 [(g_fq2, LANES, 0, False), (g_fk2, LANES, 0, False)],
                                 [(dqn, LANES, 0, True), (dkn, LANES, 0, True)], T, tm, 4, [BF, BF], [True, True])
    gs["g_fox_q"] = gq2[:, :HEAD] + gq2[:, HEAD:]
    gs["g_fox_k"] = gk2[:, :HEAD] + gk2[:, HEAD:]
    dff, dbp = _fgate_bwd_col(ffp, bpad, dfc4, drc4, T)
    gs["b_forget"] = dbp[:, :8]
    drq, drk, drv, drg, dg_ret = _ret_bwd(P, cos4, sin4, g_ret, lg, s0, dmix, T, tb)
    gs["g_ret_out"] = dg_ret
    dsecs = [drq, drk, drv, drg, dfq, dfk, dfv]
    g_secs = [_mm("dw_in_%d" % k, [[(d, hn1, "tn")]], [], _ident, 512, D, 256, D, [BF])[0] for k, d in enumerate(dsecs)]
    g_ff, = _mm("dw_in_ff", [[(dff, hn1, "tn")]], [], _ident, LANES, D, LANES, D, [BF])
    gW["w_inT"] = jnp.concatenate(g_secs + [g_ff[:8]], axis=0)
    tok = push("in", gW)
    grad_x, _, gs["g_mix"] = _mm("proj_in_bwd", [[(d, w, "nn") for d, w in zip(dsecs, w_secs)] + [(dff, w_ffT, "nn")]], [x, dh1],
                                 _rms_bwd_epi, T, D, tm, D, [F32, BF], params=[tie(sp["g_mix"], tok)], n_acc=1)
    return grad_x, push_small(gs, loss_part)


_CANON = {"w_in": "w_inT", "w_xkv": "w_xkvT", "w_gate": "w_gateT", "w_up": "w_upT"}
_SMALL = (("g_mix", 0, 0, 1024), ("g_xattn", 1, 0, 1024), ("g_mem", 2, 0, 1024), ("g_ffn", 3, 0, 1024),
          ("g_ret_out", 4, 0, 512), ("g_xq", 4, 512, 256), ("g_xk", 4, 768, 256),
          ("g_fox_q", 5, 0, 64), ("g_fox_k", 5, 64, 64), ("b_forget", 5, 128, 8))
_LOSS_AT = (5, 256)


def _pack_shards(tree, dtype):
    parts = []
    for name, rows, padded, transposed in W_LAYOUT:
        a = tree[name][0]
        a = a.T if transposed else a
        parts.append(jnp.pad(a, ((0, padded - rows), (0, 0))).astype(dtype))
    return jnp.concatenate(parts, axis=0)


def _unpack_shards(packed, like):
    out = {}
    for name, rows, padded, transposed in W_LAYOUT:
        a = packed[W_OFF[name]:W_OFF[name] + rows]
        out[name] = (a.T if transposed else a)[None].reshape(like[name].shape)
    return out


def _pack_small(tree):
    rows = [jnp.zeros((1, D), F32) for _ in range(SMALL_ROWS)]
    buf = jnp.concatenate(rows, axis=0)
    for name, r, c, n in _SMALL:
        buf = lax.dynamic_update_slice(buf, tree[name].reshape(1, n).astype(F32), (r, c))
    return buf


def _unpack_small(buf, like):
    return {name: buf[r:r + 1, c:c + n].reshape(like[name].shape) for name, r, c, n in _SMALL}


def kernel(x, mem, g_mix, w_in, b_forget, g_ret_out, g_fox_q, g_fox_k, w_out, g_xattn, w_xq, w_xkv, g_mem, g_xq, g_xk, w_xo, g_ffn, w_gate, w_up, w_down, loss_target, m_g_mix, m_w_in, m_b_forget, m_g_ret_out, m_g_fox_q, m_g_fox_k, m_w_out, m_g_xattn, m_w_xq, m_w_xkv, m_g_mem, m_g_xq, m_g_xk, m_w_xo, m_g_ffn, m_w_gate, m_w_up, m_w_down, v_g_mix, v_w_in, v_b_forget, v_g_ret_out, v_g_fox_q, v_g_fox_k, v_w_out, v_g_xattn, v_w_xq, v_w_xkv, v_g_mem, v_g_xq, v_g_xk, v_w_xo, v_g_ffn, v_w_gate, v_w_up, v_w_down):
    names = ("g_mix", "w_in", "b_forget", "g_ret_out", "g_fox_q", "g_fox_k", "w_out", "g_xattn", "w_xq", "w_xkv", "g_mem",
             "g_xq", "g_xk", "w_xo", "g_ffn", "w_gate", "w_up", "w_down")
    w = dict(zip(names, (g_mix, w_in, b_forget, g_ret_out, g_fox_q, g_fox_k, w_out, g_xattn, w_xq, w_xkv, g_mem, g_xq, g_xk,
                         w_xo, g_ffn, w_gate, w_up, w_down)))
    m = dict(zip(names, (m_g_mix, m_w_in, m_b_forget, m_g_ret_out, m_g_fox_q, m_g_fox_k, m_w_out, m_g_xattn, m_w_xq, m_w_xkv,
                         m_g_mem, m_g_xq, m_g_xk, m_w_xo, m_g_ffn, m_w_gate, m_w_up, m_w_down)))
    v = dict(zip(names, (v_g_mix, v_w_in, v_b_forget, v_g_ret_out, v_g_fox_q, v_g_fox_k, v_w_out, v_g_xattn, v_w_xq, v_w_xkv,
                         v_g_mem, v_g_xq, v_g_xk, v_w_xo, v_g_ffn, v_w_gate, v_w_up, v_w_down)))
    small_names = [s[0] for s in _SMALL]

    gathered = _all_gather(_pack_shards(w, BF))
    W = {}
    for name, rows, padded, transposed in W_LAYOUT:
        full = gathered[:, W_OFF[name]:W_OFF[name] + rows].reshape(N_DEV * rows, D)
        W[_CANON.get(name, name)] = full

    sp = {n: w[n].reshape(1, -1) for n in small_names}
    loss_part, grad_x, gW, gs = _local_step(x[0], mem[0], loss_target[0], sp, W)

    chunks = []
    for name, rows, padded, transposed in W_LAYOUT:
        g = gW[_CANON.get(name, name)].reshape(N_DEV, rows, D)
        chunks.append(jnp.pad(g, ((0, 0), (0, padded - rows), (0, 0))).astype(BF))
    send = jnp.concatenate(chunks, axis=1)
    small = _pack_small(gs)
    small = lax.dynamic_update_slice(small, loss_part[:, :1], _LOSS_AT)
    recv, recv_small = _all_to_all(send, small)

    g_big, d_big, m_big, v_big = _adamw("adamw_shards", recv, _pack_shards(w, F32), _pack_shards(m, F32), _pack_shards(v, F32), 240)
    g_sm, d_sm, m_sm, v_sm = _adamw("adamw_small", recv_small, _pack_small(w), _pack_small(m), _pack_small(v), SMALL_ROWS)
    loss = g_sm[_LOSS_AT[0], _LOSS_AT[1]]

    outs = []
    for big, sm in ((g_big, g_sm), (d_big, d_sm), (m_big, m_sm), (v_big, v_sm)):
        tree = {**_unpack_shards(big, w), **_unpack_small(sm, w)}
        outs += [tree[n] for n in names]
    return (loss, grad_x[None], *outs)


def _pack_shards(tree, names, dtype):
    parts = []
    for name in names:
        rows, padded, transposed = W_SHARD[name]
        a = tree[name][0]
        a = a.T if transposed else a
        parts.append(jnp.pad(a, ((0, padded - rows), (0, 0))).astype(dtype))
    return jnp.concatenate(parts, axis=0)


def _unpack_shards(packed, names, like):
    out, off = {}, 0
    for name in names:
        rows, padded, transposed = W_SHARD[name]
        a = packed[off:off + rows]
        out[name] = (a.T if transposed else a)[None].reshape(like[name].shape)
        off += padded
    return out


def _unpack_gathered(gathered, names):
    out, off = {}, 0
    for name in names:
        rows, padded, _ = W_SHARD[name]
        out[_CANON.get(name, name)] = gathered[:, off:off + rows].reshape(N_DEV * rows, D)
        off += padded
    return out


def _pack_chunks(grads, names):
    chunks = []
    for name in names:
        rows, padded, _ = W_SHARD[name]
        g = grads[_CANON.get(name, name)].reshape(N_DEV, rows, D)
        chunks.append(jnp.pad(g, ((0, 0), (0, padded - rows), (0, 0))).astype(BF))
    return jnp.concatenate(chunks, axis=1)


def kernel(x, mem, g_mix, w_in, b_forget, g_ret_out, g_fox_q, g_fox_k, w_out, g_xattn, w_xq, w_xkv, g_mem, g_xq, g_xk, w_xo, g_ffn, w_gate, w_up, w_down, loss_target, m_g_mix, m_w_in, m_b_forget, m_g_ret_out, m_g_fox_q, m_g_fox_k, m_w_out, m_g_xattn, m_w_xq, m_w_xkv, m_g_mem, m_g_xq, m_g_xk, m_w_xo, m_g_ffn, m_w_gate, m_w_up, m_w_down, v_g_mix, v_w_in, v_b_forget, v_g_ret_out, v_g_fox_q, v_g_fox_k, v_w_out, v_g_xattn, v_w_xq, v_w_xkv, v_g_mem, v_g_xq, v_g_xk, v_w_xo, v_g_ffn, v_w_gate, v_w_up, v_w_down):
    names = ("g_mix", "w_in", "b_forget", "g_ret_out", "g_fox_q", "g_fox_k", "w_out", "g_xattn", "w_xq", "w_xkv", "g_mem",
             "g_xq", "g_xk", "w_xo", "g_ffn", "w_gate", "w_up", "w_down")
    w = dict(zip(names, (g_mix, w_in, b_forget, g_ret_out, g_fox_q, g_fox_k, w_out, g_xattn, w_xq, w_xkv, g_mem, g_xq, g_xk,
                         w_xo, g_ffn, w_gate, w_up, w_down)))
    m = dict(zip(names, (m_g_mix, m_w_in, m_b_forget, m_g_ret_out, m_g_fox_q, m_g_fox_k, m_w_out, m_g_xattn, m_w_xq, m_w_xkv,
                         m_g_mem, m_g_xq, m_g_xk, m_w_xo, m_g_ffn, m_w_gate, m_w_up, m_w_down)))
    v = dict(zip(names, (v_g_mix, v_w_in, v_b_forget, v_g_ret_out, v_g_fox_q, v_g_fox_k, v_w_out, v_g_xattn, v_w_xq, v_w_xkv,
                         v_g_mem, v_g_xq, v_g_xk, v_w_xo, v_g_ffn, v_w_gate, v_w_up, v_w_down)))
    small_names = [s[0] for s in _SMALL]
    me = 4 * lax.axis_index("x") + 2 * lax.axis_index("y") + lax.axis_index("c")

    first = _all_gather(_pack_shards(w, GATHER_FIRST, BF))
    first, rest_shard = lax.optimization_barrier((first, _pack_shards(w, GATHER_REST, BF)))
    rest_started = _exchange_start("gather_rest_start", rest_shard,
                                   jnp.broadcast_to(rest_shard[None], (N_DEV,) + rest_shard.shape), scatter=False)

    def fetch_rest(after):
        return _unpack_gathered(_exchange_wait("gather_rest_wait", rest_started, after, scatter=False)[1], GATHER_REST)

    pushed = {}

    def push(group, grads):
        send = _pack_chunks(grads, GRAD_GROUPS[group])
        pushed[group] = _exchange_start("scatter_%s_start" % group, send, jnp.zeros(send.shape, BF), scatter=True)
        return pushed[group][4]

    sp = {n: w[n].reshape(1, -1) for n in small_names}
    loss_part, grad_x, g_last, gs = _local_step(x[0], mem[0], loss_target[0], sp, _unpack_gathered(first, GATHER_FIRST)["w_inT"],
                                                rest_started[4], fetch_rest, push)

    small = lax.dynamic_update_slice(_pack_small(gs), loss_part[:, :1], _LOSS_AT)
    recv_mix, recv_small = _all_to_all(_pack_chunks(g_last, GRAD_GROUPS["mix"]), small)

    results = {}
    for group in ("ffn", "xattn", "mix"):
        gnames = GRAD_GROUPS[group]
        wp, mp, vp = (_pack_shards(t, gnames, F32) for t in (w, m, v))
        if group == "mix":
            res = _adamw("adamw_mix", recv_mix, wp, mp, vp, 16)
        else:
            sent, recv = _exchange_wait("scatter_%s_wait" % group, pushed[group], recv_small, scatter=True)
            own = lax.dynamic_index_in_dim(sent, me, axis=0, keepdims=False)
            res = _adamw("adamw_%s" % group, recv, wp, mp, vp, {"ffn": 176, "xattn": 128}[group], own=own)
        results[group] = [_unpack_shards(r, gnames, w) for r in res]
    g_sm, d_sm, m_sm, v_sm = _adamw("adamw_small", recv_small, _pack_small(w), _pack_small(m), _pack_small(v), SMALL_ROWS)
    loss = g_sm[_LOSS_AT[0], _LOSS_AT[1]]

    outs = []
    for k, sm in enumerate((g_sm, d_sm, m_sm, v_sm)):
        tree = _unpack_small(sm, w)
        for group in results:
            tree.update(results[group][k])
        outs += [tree[n] for n in names]
    return (loss, grad_x[None], *outs)


SCATTER_GROUPS = {"ffn": ("w_gate", "w_up", "w_down"), "xattn": ("w_xq", "w_xo", "w_xkv"), "out": ("w_out",), "in": ("w_in",)}


def _adam_update(g, w, m, v):
    m2 = ADAM_B1 * m + (1.0 - ADAM_B1) * g
    v2 = ADAM_B2 * v + (1.0 - ADAM_B2) * jnp.square(g)
    m_hat = m2 / (1.0 - ADAM_B1 ** ADAM_STEP)
    v_hat = v2 / (1.0 - ADAM_B2 ** ADAM_STEP)
    return g, -ADAM_LR * (m_hat / (jnp.sqrt(v_hat) + ADAM_EPS) + ADAM_WD * w), m2, v2


def _adamw_shard(name, recv, own, off, w, m, v):
    rows, padded, transposed = W_SHARD[name.split(":")[1]]
    assert off % padded == 0
    blk = off // padded

    def total(s_ref, own_ref):
        g = own_ref[...].astype(F32)
        for s in range(N_DEV):
            g = g + s_ref[s].astype(F32)
        return g

    canonical_view = name.endswith(":")
    if transposed and rows == padded and not canonical_view:
        res = _adamw_shard(name + ":", recv, own, off, *(jnp.swapaxes(a, 1, 2) for a in (w, m, v)))
        return [jnp.swapaxes(r, 1, 2) for r in res]

    if canonical_view or not transposed:
        def body(s_ref, own_ref, w_ref, m_ref, v_ref, *outs):
            for o, r in zip(outs, _adam_update(total(s_ref, own_ref), w_ref[0], m_ref[0], v_ref[0])):
                o[0] = r

        full = pl.BlockSpec((1, rows, D), lambda i: (0, 0, 0))
        return pl.pallas_call(
            body, name=name.replace(":", "_"), grid=(1,),
            in_specs=[pl.BlockSpec((N_DEV, padded, D), lambda i: (0, blk, 0)), pl.BlockSpec((padded, D), lambda i: (blk, 0)),
                      full, full, full],
            out_specs=[full] * 4, out_shape=[jax.ShapeDtypeStruct((1, rows, D), F32)] * 4,
            compiler_params=_params(("arbitrary",)),
        )(recv, own, w, m, v)

    wide = -(-padded // LANES) * LANES

    def body(s_ref, own_ref, w_ref, m_ref, v_ref, *outs):
        g = total(s_ref, own_ref)
        if wide > padded:
            g = jnp.concatenate([g, jnp.zeros((wide - padded, LANES), F32)], axis=0)
        g = g.T[:, :rows]
        for o, r in zip(outs, _adam_update(g, w_ref[0], m_ref[0], v_ref[0])):
            o[0] = r

    cols = pl.BlockSpec((1, LANES, rows), lambda c: (0, c, 0))
    return pl.pallas_call(
        body, name=name.replace(":", "_"), grid=(D // LANES,),
        in_specs=[pl.BlockSpec((N_DEV, padded, LANES), lambda c: (0, blk, c)), pl.BlockSpec((padded, LANES), lambda c: (blk, c)),
                  cols, cols, cols],
        out_specs=[cols] * 4, out_shape=[jax.ShapeDtypeStruct((1, D, rows), F32)] * 4,
        compiler_params=_params(("arbitrary",)),
    )(recv, own, w, m, v)


def _gather_small(small):
    def body(small_ref, out_ref, send_sems, recv_sems, local_sem):
        x, y, c = _place()
        me = 4 * x + 2 * y + c
        mine = pltpu.make_async_copy(small_ref, out_ref.at[me], local_sem)
        mine.start()
        copies = []
        for r in range(1, N_DEV):
            px, py, pc = x ^ (r >> 2), y ^ ((r >> 1) & 1), c ^ (r & 1)
            copies.append(pltpu.make_async_remote_copy(
                src_ref=small_ref, dst_ref=out_ref.at[me], send_sem=send_sems.at[r - 1], recv_sem=recv_sems.at[r - 1],
                device_id=(px, py, pc), device_id_type=MESH))
        for cp in copies:
            cp.start()
        for cp in copies:
            cp.wait_recv()
        for cp in copies:
            cp.wait_send()
        mine.wait()

    return pl.pallas_call(
        body, name="gather_small",
        out_shape=jax.ShapeDtypeStruct((N_DEV,) + small.shape, small.dtype),
        in_specs=[pl.BlockSpec(memory_space=pl.ANY)], out_specs=pl.BlockSpec(memory_space=pl.ANY),
        scratch_shapes=[pltpu.SemaphoreType.DMA((N_DEV - 1,)), pltpu.SemaphoreType.DMA((N_DEV - 1,)), pltpu.SemaphoreType.DMA],
    )(small)


def _pack_chunks(grads, names):
    chunks = []
    for name in names:
        rows, padded, _ = W_SHARD[name]
        g = grads[_CANON.get(name, name)].reshape(N_DEV, rows, D)
        chunks.append(jnp.pad(g, ((0, 0), (0, padded - rows), (0, 0))).astype(BF))
    return chunks[0] if len(chunks) == 1 else jnp.concatenate(chunks, axis=1)


def kernel(x, mem, g_mix, w_in, b_forget, g_ret_out, g_fox_q, g_fox_k, w_out, g_xattn, w_xq, w_xkv, g_mem, g_xq, g_xk, w_xo, g_ffn, w_gate, w_up, w_down, loss_target, m_g_mix, m_w_in, m_b_forget, m_g_ret_out, m_g_fox_q, m_g_fox_k, m_w_out, m_g_xattn, m_w_xq, m_w_xkv, m_g_mem, m_g_xq, m_g_xk, m_w_xo, m_g_ffn, m_w_gate, m_w_up, m_w_down, v_g_mix, v_w_in, v_b_forget, v_g_ret_out, v_g_fox_q, v_g_fox_k, v_w_out, v_g_xattn, v_w_xq, v_w_xkv, v_g_mem, v_g_xq, v_g_xk, v_w_xo, v_g_ffn, v_w_gate, v_w_up, v_w_down):
    names = ("g_mix", "w_in", "b_forget", "g_ret_out", "g_fox_q", "g_fox_k", "w_out", "g_xattn", "w_xq", "w_xkv", "g_mem",
             "g_xq", "g_xk", "w_xo", "g_ffn", "w_gate", "w_up", "w_down")
    w = dict(zip(names, (g_mix, w_in, b_forget, g_ret_out, g_fox_q, g_fox_k, w_out, g_xattn, w_xq, w_xkv, g_mem, g_xq, g_xk,
                         w_xo, g_ffn, w_gate, w_up, w_down)))
    m = dict(zip(names, (m_g_mix, m_w_in, m_b_forget, m_g_ret_out, m_g_fox_q, m_g_fox_k, m_w_out, m_g_xattn, m_w_xq, m_w_xkv,
                         m_g_mem, m_g_xq, m_g_xk, m_w_xo, m_g_ffn, m_w_gate, m_w_up, m_w_down)))
    v = dict(zip(names, (v_g_mix, v_w_in, v_b_forget, v_g_ret_out, v_g_fox_q, v_g_fox_k, v_w_out, v_g_xattn, v_w_xq, v_w_xkv,
                         v_g_mem, v_g_xq, v_g_xk, v_w_xo, v_g_ffn, v_w_gate, v_w_up, v_w_down)))
    small_names = [s[0] for s in _SMALL]
    me = 4 * lax.axis_index("x") + 2 * lax.axis_index("y") + lax.axis_index("c")

    first = _all_gather(_pack_shards(w, GATHER_FIRST, BF))
    first, rest_shard = lax.optimization_barrier((first, _pack_shards(w, GATHER_REST, BF)))
    rest_started = _exchange_start("gather_rest_start", rest_shard,
                                   jnp.broadcast_to(rest_shard[None], (N_DEV,) + rest_shard.shape), scatter=False)

    def fetch_rest(after):
        return _unpack_gathered(_exchange_wait("gather_rest_wait", rest_started, after, scatter=False)[1], GATHER_REST)

    pushed = {}

    def push(group, grads):
        send = _pack_chunks(grads, SCATTER_GROUPS[group])
        pushed[group] = _exchange_start("scatter_%s_start" % group, send, jnp.zeros(send.shape, BF), scatter=True)
        return pushed[group][4]

    def push_small(gs, loss_part):
        small = lax.dynamic_update_slice(_pack_small(gs), loss_part[:, :1], _LOSS_AT)
        pushed["small"] = _exchange_start("gather_small_start", small, jnp.broadcast_to(small[None], (N_DEV,) + small.shape),
                                          scatter=False)
        return pushed["small"][4]

    sp = {n: w[n].reshape(1, -1) for n in small_names}
    grad_x, done = _local_step(x[0], mem[0], loss_target[0], sp, _unpack_gathered(first, GATHER_FIRST)["w_inT"],
                               rest_started[4], fetch_rest, push, push_small)

    results, after = {}, done
    for group in ("ffn", "xattn", "out", "small", "in"):
        if group == "small":
            recv_small = _exchange_wait("gather_small_wait", pushed["small"], after, scatter=False)[1]
            g_sm, d_sm, m_sm, v_sm = _adamw("adamw_small", recv_small, _pack_small(w), _pack_small(m), _pack_small(v), SMALL_ROWS)
            after = g_sm
            continue
        sent, recv = _exchange_wait("scatter_%s_wait" % group, pushed[group], after, scatter=True)
        own = lax.dynamic_index_in_dim(sent, me, axis=0, keepdims=False)
        off = 0
        for name in SCATTER_GROUPS[group]:
            results[name] = _adamw_shard("adamw:" + name, recv, own, off, w[name], m[name], v[name])
            off += W_SHARD[name][1]
        after = results[SCATTER_GROUPS[group][-1]][0]
    loss = g_sm[_LOSS_AT[0], _LOSS_AT[1]]

    outs = []
    for k, sm in enumerate((g_sm, d_sm, m_sm, v_sm)):
        tree = _unpack_small(sm, w)
        tree.update({name: res[k] for name, res in results.items()})
        outs += [tree[n] for n in names]
    return (loss, grad_x[None], *outs)
```

```python
import jax
import jax.numpy as jnp
from jax import lax
from jax.experimental import pallas as pl
from jax.experimental.pallas import tpu as pltpu

F32 = jnp.float32
BF = jnp.bfloat16

D = 1024
HEAD = 64
CHUNK = 64
N_MEM = 256
XHEAD = 256
D_FF = 2816
EPS = 1e-6
NEG = -1e30
LANES = 128
N_DEV = 8
V7X_VMEM_BYTES = 64 * 1024 * 1024
VMEM_LIMIT = V7X_VMEM_BYTES - 8 * 1024 * 1024

ADAM_LR, ADAM_B1, ADAM_B2, ADAM_EPS, ADAM_WD, ADAM_STEP = 0.001, 0.9, 0.999, 1e-08, 0.01, 10

W_SHARD = {"w_in": (449, True), "w_out": (128, False), "w_xq": (128, False), "w_xkv": (256, True),
           "w_xo": (128, False), "w_gate": (352, True), "w_up": (352, True), "w_down": (352, False)}
GATHER_REST = ("w_out", "w_xq", "w_xkv", "w_xo", "w_gate", "w_up", "w_down")
SCATTER_GROUPS = {"ffn": ("w_gate", "w_up", "w_down"), "xattn": ("w_xq", "w_xo", "w_xkv"), "out": ("w_out",), "in": ("w_in",)}
SMALL_ROWS = 8

NT = (((1,), (1,)), ((), ()))
NN = (((1,), (0,)), ((), ()))
TN = (((0,), (0,)), ((), ()))
_DIMS = {"nn": NN, "nt": NT, "tn": TN}


def _params(sem):
    return pltpu.CompilerParams(dimension_semantics=sem, vmem_limit_bytes=VMEM_LIMIT)


def _mm(name, products, extras, epilogue, M, N, tm, tn, out_dtypes, params=(), n_acc=0):
    assert n_acc == 0 or tn == N
    flat = [t for p in products for t in p]
    counts = [len(p) for p in products]
    in_specs, args = [], []
    for a, b, form in flat:
        if form == "tn":
            in_specs.append(pl.BlockSpec((a.shape[0], tm), lambda i, j: (0, i)))
        else:
            in_specs.append(pl.BlockSpec((tm, a.shape[1]), lambda i, j: (i, 0)))
        if form == "nt":
            in_specs.append(pl.BlockSpec((tn, b.shape[1]), lambda i, j: (j, 0)))
        else:
            in_specs.append(pl.BlockSpec((b.shape[0], tn), lambda i, j: (0, j)))
        args += [a, b]
    for e in extras:
        in_specs.append(pl.BlockSpec((tm, tn), lambda i, j: (i, j)))
        args.append(e)
    for p in params:
        in_specs.append(pl.BlockSpec((1, tn), lambda i, j: (0, j)))
        args.append(p)
    n_in = len(args)
    n_out = len(out_dtypes)

    def body(*refs):
        ins, outs = refs[:n_in], refs[n_in:]
        prods, p = [], 0
        for c in counts:
            acc = None
            for _ in range(c):
                a = ins[2 * p][...].astype(BF)
                b = ins[2 * p + 1][...].astype(BF)
                d = lax.dot_general(a, b, _DIMS[flat[p][2]], preferred_element_type=F32)
                acc = d if acc is None else acc + d
                p += 1
            prods.append(acc)
        ex = [r[...].astype(F32) for r in ins[2 * len(flat):]]
        res = epilogue(*prods, *ex)
        for o, r in zip(outs[:n_out], res[:n_out]):
            o[...] = r.astype(o.dtype)
        for o, r in zip(outs[n_out:], res[n_out:]):
            @pl.when(pl.program_id(0) == 0)
            def _(o=o):
                o[...] = jnp.zeros(o.shape, F32)
            o[...] += r

    return pl.pallas_call(
        body, name=name, grid=(M // tm, N // tn), in_specs=in_specs,
        out_specs=[pl.BlockSpec((tm, tn), lambda i, j: (i, j)) for _ in out_dtypes]
        + [pl.BlockSpec((1, tn), lambda i, j: (0, j)) for _ in range(n_acc)],
        out_shape=[jax.ShapeDtypeStruct((M, N), dt) for dt in out_dtypes] + [jax.ShapeDtypeStruct((1, N), F32)] * n_acc,
        compiler_params=_params(("arbitrary", "arbitrary")),
    )(*args)


def _ident(x):
    return (x,)


def _spec(rows, w, off, per_j):
    if per_j:
        return pl.BlockSpec((rows, w), lambda j, i: (i, off + j))
    return pl.BlockSpec((rows, w), lambda j, i: (i, off))


def _pspec(rows, w, off, per_j):
    if per_j:
        return pl.BlockSpec((rows, w), lambda j, i: (0, off + j))
    return pl.BlockSpec((rows, w), lambda j, i: (0, off))


def _rw_fwd(name, fn, rows, params, outs, T, tm, nj, n_acc=0):
    in_specs = [_spec(tm, w, off, pj) for _, w, off, pj in rows] + [_pspec(a.shape[0], w, off, pj) for a, w, off, pj in params]
    args = [r[0] for r in rows] + [p[0] for p in params]
    n_in, n_out = len(args), len(outs)
    out_specs = [pl.BlockSpec((tm, w), lambda j, i: (i, j)) for _, w in outs]
    out_shape = [jax.ShapeDtypeStruct((T, nj * w), dt) for dt, w in outs]
    out_specs += [pl.BlockSpec((1, LANES), lambda j, i: (0, 0)) for _ in range(n_acc)]
    out_shape += [jax.ShapeDtypeStruct((1, LANES), F32) for _ in range(n_acc)]

    def body(*refs):
        vals = [r[...].astype(F32) for r in refs[:n_in]]
        res = fn(*vals)
        orefs = refs[n_in:]
        for k in range(n_out):
            orefs[k][...] = res[k].astype(orefs[k].dtype)
        first = (pl.program_id(0) == 0) & (pl.program_id(1) == 0)
        for k in range(n_acc):
            @pl.when(first)
            def _(k=k):
                orefs[n_out + k][...] = jnp.zeros((1, LANES), F32)
            orefs[n_out + k][...] += res[n_out + k]

    return pl.pallas_call(
        body, name=name, grid=(nj, T // tm), in_specs=in_specs, out_specs=out_specs, out_shape=out_shape,
        compiler_params=_params(("arbitrary", "arbitrary")),
    )(*args)


def _rw_bwd(name, fn, rows, params, cots, T, tm, nj, row_grads, param_grads, resid=None):
    in_specs = ([_spec(tm, w, off, pj) for _, w, off, pj in rows] + [_pspec(a.shape[0], w, off, pj) for a, w, off, pj in params]
                + [_spec(tm, w, off, pj) for _, w, off, pj in cots])
    args = [r[0] for r in rows] + [p[0] for p in params] + [c[0] for c in cots]
    if resid is not None:
        in_specs.append(_spec(tm, rows[0][1], rows[0][2], rows[0][3]))
        args.append(resid)
    nr, npar, nc = len(rows), len(params), len(cots)
    out_specs, out_shape, kinds = [], [], []
    for k, dts in enumerate(row_grads):
        for dt in (dts if isinstance(dts, (list, tuple)) else [dts]):
            if dt is not None:
                w = rows[k][1]
                out_specs.append(pl.BlockSpec((tm, w), lambda j, i: (i, j)))
                out_shape.append(jax.ShapeDtypeStruct((T, nj * w), dt))
                kinds.append(("row", k))
    for k, need in enumerate(param_grads):
        if need:
            a, w, off, pj = params[k]
            out_specs.append(_pspec(a.shape[0], w, off, pj))
            out_shape.append(jax.ShapeDtypeStruct(a.shape, F32))
            kinds.append(("par", k))

    def body(*refs):
        vals = [r[...].astype(F32) for r in refs[:nr + npar]]
        ct = tuple(r[...].astype(F32) for r in refs[nr + npar:nr + npar + nc])
        _, vjp = jax.vjp(lambda *a: tuple(fn(*a)), *vals)
        grads = list(vjp(ct))
        n_in = nr + npar + nc + (resid is not None)
        if resid is not None:
            grads[0] = grads[0] + refs[n_in - 1][...].astype(F32)
        orefs = refs[n_in:]
        j, i = pl.program_id(0), pl.program_id(1)
        for o, (kind, k) in zip(orefs, kinds):
            if kind == "row":
                o[...] = grads[k].astype(o.dtype)
            else:
                first = (i == 0) if params[k][3] else ((i == 0) & (j == 0))

                @pl.when(first)
                def _(o=o):
                    o[...] = jnp.zeros(o.shape, F32)
                o[...] += grads[nr + k]

    return pl.pallas_call(
        body, name=name, grid=(nj, T // tm), in_specs=in_specs, out_specs=out_specs, out_shape=out_shape,
        compiler_params=_params(("arbitrary", "arbitrary")),
    )(*args)


def _rms(x, g):
    return x * lax.rsqrt(jnp.mean(x * x, axis=-1, keepdims=True) + EPS) * g


def _rms_fn(x, g):
    return (_rms(x, g),)


def _lo_mask():
    return lax.broadcasted_iota(jnp.int32, (1, LANES), 1) < HEAD


def _gmean(x, lo):
    s0 = jnp.sum(jnp.where(lo, x, 0.0), axis=-1, keepdims=True)
    s1 = jnp.sum(jnp.where(lo, 0.0, x), axis=-1, keepdims=True)
    return jnp.where(lo, s0, s1) * (1.0 / HEAD)


def _fox_prep_fn(fq, fk, gq, gk):
    lo = _lo_mask()
    qn = fq * lax.rsqrt(_gmean(fq * fq, lo) + EPS) * gq * (HEAD ** -0.5)
    kn = fk * lax.rsqrt(_gmean(fk * fk, lo) + EPS) * gk
    return qn, kn


@jax.custom_vjp
def _swap_halves(x):
    bit = (lax.broadcasted_iota(jnp.int32, (1, LANES), 1) & (HEAD // 2)) == 0
    return jnp.where(bit, pltpu.roll(x, LANES - HEAD // 2, 1), pltpu.roll(x, HEAD // 2, 1))


_swap_halves.defvjp(lambda x: (_swap_halves(x), None), lambda _, g: (_swap_halves(g),))


def _ret_fn(rq, rk, rv, rg, cos, sin, s_in, g, lg):
    tb = rq.shape[0]
    nc = tb // CHUNK
    lo = _lo_mask()
    row = lax.broadcasted_iota(jnp.int32, (LANES, 1), 0) < HEAD
    same_head = row == lo
    q = (rq * cos + _swap_halves(rq) * sin) * (HEAD ** -0.5)
    k = rk * cos + _swap_halves(rk) * sin
    q3, k3, v3 = q.reshape(nc, CHUNK, LANES), k.reshape(nc, CHUNK, LANES), rv.reshape(nc, CHUNK, LANES)
    pos = lax.broadcasted_iota(jnp.int32, (CHUNK, 1), 0).astype(F32)
    q_decay = jnp.exp(lg * (pos + 1.0))
    k_decay = jnp.exp(lg * (CHUNK - 1.0 - pos))
    chunk_decay = jnp.exp(lg * float(CHUNK))
    dist = jnp.abs(lax.broadcasted_iota(jnp.int32, (CHUNK, CHUNK), 0) - lax.broadcasted_iota(jnp.int32, (CHUNK, CHUNK), 1)).astype(F32)
    v3b = v3.astype(BF)
    intra = []
    for hh in range(2):
        hm = lo if hh == 0 else ~lo
        lg_h = lg[:, hh * HEAD:hh * HEAD + 1]
        qm = jnp.where(hm, q3, 0.0).astype(BF)
        sc = jnp.einsum("nid,njd->nij", qm, k3.astype(BF), preferred_element_type=F32) * jnp.exp(lg_h * dist)[None]
        intra.append(jnp.einsum("nij,nje->nie", sc.astype(BF), v3b, preferred_element_type=F32))
    o = jnp.where(lo, intra[0], intra[1])
    kv = jnp.einsum("njd,nje->nde", (k3 * k_decay[None]).astype(BF), v3b, preferred_element_type=F32)
    kv = jnp.where(same_head[None], kv, 0.0)
    state, states = s_in, []
    for n in range(nc):
        states.append(state)
        state = state * chunk_decay + kv[n]
    s_prev = jnp.stack(states, axis=0)
    o = o + jnp.einsum("nid,nde->nie", (q3 * q_decay[None]).astype(BF), s_prev.astype(BF), preferred_element_type=F32)
    o = o.reshape(tb, LANES)
    mu = _gmean(o, lo)
    oc = o - mu
    y = oc * lax.rsqrt(_gmean(oc * oc, lo) + EPS) * g
    return jax.nn.silu(rg) * y, state


def _xattn_fn(qx, gq, gk, kk, vv):
    q = _rms(qx, gq)
    k = _rms(kk, gk)
    logits = lax.dot_general(q.astype(BF), k.astype(BF), NT, preferred_element_type=F32) * (XHEAD ** -0.5)
    p = jax.nn.softmax(logits, axis=-1)
    return (jnp.dot(p.astype(BF), vv.astype(BF), preferred_element_type=F32),)


def _swiglu_fwd_epi(g, u):
    return g, u, jax.nn.silu(g) * u


def _swiglu_bwd_epi(dact, g, u):
    _, vjp = jax.vjp(lambda a, b: jax.nn.silu(a) * b, g, u)
    return vjp(dact)


def _add_rms_epi(acc, resid, g):
    h = acc + resid
    return h, _rms(h, g)


def _add_loss_epi(acc, resid, target):
    err = (acc + resid) - target
    dy = err * (1.0 / D)
    part = jnp.sum(jnp.sum(err * err, axis=0, keepdims=True), axis=1, keepdims=True) * (0.5 / D)
    return dy, dy, jnp.broadcast_to(part, (1, err.shape[1]))


def _rms_bwd_epi(dhn, h, skip, g):
    _, vjp = jax.vjp(_rms, h, g)
    dh, dg = vjp(dhn)
    dh = dh + skip
    return dh, dh, dg


def _ret_fwd(P, cos, sin, g_ret, lg, T, tb):
    nb = T // tb

    def body(rq, rk, rv, rg, c, s, g, l, o_ref, s0_ref, state):
        @pl.when(pl.program_id(1) == 0)
        def _():
            state[...] = jnp.zeros(state.shape, F32)
        s0_ref[0, 0] = state[...]
        out, s_new = _ret_fn(rq[...], rk[...], rv[...], rg[...], c[...], s[...], state[...], g[...], l[...])
        o_ref[...] = out
        state[...] = s_new

    sec = lambda off: pl.BlockSpec((tb, LANES), lambda j, i: (i, off + j))
    tab = pl.BlockSpec((tb, LANES), lambda j, i: (i, 0))
    par = pl.BlockSpec((1, LANES), lambda j, i: (0, j))
    return pl.pallas_call(
        body, name="ret_fwd", grid=(4, nb),
        in_specs=[sec(0), sec(4), sec(8), sec(12), tab, tab, par, par],
        out_specs=[pl.BlockSpec((tb, LANES), lambda j, i: (i, j)), pl.BlockSpec((1, 1, LANES, LANES), lambda j, i: (j, i, 0, 0))],
        out_shape=[jax.ShapeDtypeStruct((T, 4 * LANES), F32), jax.ShapeDtypeStruct((4, nb, LANES, LANES), F32)],
        scratch_shapes=[pltpu.VMEM((LANES, LANES), F32)],
        compiler_params=_params(("arbitrary", "arbitrary")),
    )(P, P, P, P, cos, sin, g_ret, lg)


def _ret_bwd(P, cos, sin, g_ret, lg, s0, dmix, T, tb):
    nb = T // tb

    def body(rq, rk, rv, rg, c, s, g, l, s0_ref, do, drq, drk, drv, drg, dg, dstate):
        i = pl.program_id(1)

        @pl.when(i == 0)
        def _():
            dstate[...] = jnp.zeros(dstate.shape, F32)
            dg[...] = jnp.zeros(dg.shape, F32)

        cc, ss, ll = c[...], s[...], l[...]
        _, vjp = jax.vjp(lambda a, b, v, gate, st, gg: _ret_fn(a, b, v, gate, cc, ss, st, gg, ll),
                         rq[...], rk[...], rv[...], rg[...], s0_ref[0, 0], g[...])
        ga, gb, gv, ggate, gst, ggain = vjp((do[...], dstate[...]))
        drq[...] = ga.astype(drq.dtype)
        drk[...] = gb.astype(drk.dtype)
        drv[...] = gv.astype(drv.dtype)
        drg[...] = ggate.astype(drg.dtype)
        dstate[...] = gst
        dg[...] += ggain

    rev = lambda i: nb - 1 - i
    sec = lambda off: pl.BlockSpec((tb, LANES), lambda j, i: (rev(i), off + j))
    tab = pl.BlockSpec((tb, LANES), lambda j, i: (rev(i), 0))
    par = pl.BlockSpec((1, LANES), lambda j, i: (0, j))
    outb = pl.BlockSpec((tb, LANES), lambda j, i: (rev(i), j))
    return pl.pallas_call(
        body, name="ret_bwd", grid=(4, nb),
        in_specs=[sec(0), sec(4), sec(8), sec(12), tab, tab, par, par,
                  pl.BlockSpec((1, 1, LANES, LANES), lambda j, i: (j, rev(i), 0, 0)), outb],
        out_specs=[outb, outb, outb, outb, par],
        out_shape=[jax.ShapeDtypeStruct((T, 4 * LANES), BF)] * 4 + [jax.ShapeDtypeStruct((1, 4 * LANES), F32)],
        scratch_shapes=[pltpu.VMEM((LANES, LANES), F32)],
        compiler_params=_params(("arbitrary", "arbitrary")),
    )(P, P, P, P, cos, sin, g_ret, lg, s0, dmix)


_FB = 128


def _tri(lower):
    r = lax.broadcasted_iota(jnp.int32, (_FB, _FB), 0)
    c = lax.broadcasted_iota(jnp.int32, (_FB, _FB), 1)
    return ((r >= c) if lower else (r <= c)).astype(F32)


def _fgate_fwd(ffp, bpad, T):
    def body(ff_ref, b_ref, fc_ref, fr_ref):
        lane = lax.broadcasted_iota(jnp.int32, (1, LANES), 1)
        tri = _tri(True)
        carry = jnp.zeros((1, LANES), F32)
        for blk in range(T // _FB):
            z = ff_ref[blk * _FB:(blk + 1) * _FB, :] + b_ref[...]
            lf = jnp.where(lane < 8, jax.nn.log_sigmoid(z), 0.0)
            f = jnp.dot(tri, lf, precision=lax.Precision.HIGHEST, preferred_element_type=F32) + carry
            carry = f[_FB - 1:_FB, :]
            fc_ref[blk * _FB:(blk + 1) * _FB, :] = f
            fr_ref[:, blk * _FB:(blk + 1) * _FB] = f.T[:8, :]

    return pl.pallas_call(
        body, name="fgate_fwd",
        out_shape=[jax.ShapeDtypeStruct((T, LANES), F32), jax.ShapeDtypeStruct((8, T), F32)],
        compiler_params=pltpu.CompilerParams(vmem_limit_bytes=VMEM_LIMIT),
    )(ffp, bpad)


_BIAS_LANE = HEAD


def _head_bias_col(fc, head):
    lane = lax.broadcasted_iota(jnp.int32, (1, LANES), 1)
    return jnp.sum(jnp.where(lane == head, fc, 0.0), axis=-1, keepdims=True)


def _split3(f):
    hi = f.astype(BF).astype(F32)
    mid = (f - hi).astype(BF).astype(F32)
    lo = ((f - hi) - mid).astype(BF).astype(F32)
    return hi, mid, lo


def _fox_operands(P, fc, g_fq2, g_fk2, T, tm):
    def body(fq_ref, fk_ref, fv_ref, fc_ref, gq_ref, gk_ref, qa_ref, qat_ref, ka_ref, kat_ref, va_ref, vat_ref):
        j = pl.program_id(0)
        lane = lax.broadcasted_iota(jnp.int32, (1, LANES), 1)
        qn, kn = _fox_prep_fn(fq_ref[...], fk_ref[...], gq_ref[...], gk_ref[...])
        v = fv_ref[...]
        fcb = fc_ref[...]
        b = _BIAS_LANE
        for hh in range(2):
            hi, mid, lo = _split3(_head_bias_col(fcb, 2 * j + hh))
            take = (lambda a: a) if hh == 0 else (lambda a: pltpu.roll(a, HEAD, 1))
            qa = jnp.where(lane < HEAD, take(qn), jnp.where(lane == b, hi, jnp.where(lane == b + 1, mid, jnp.where(
                lane == b + 2, lo, jnp.where(lane < b + 6, 1.0, 0.0)))))
            ka = jnp.where(lane < HEAD, take(kn), jnp.where(lane < b + 3, 1.0, jnp.where(lane == b + 3, -hi, jnp.where(
                lane == b + 4, -mid, jnp.where(lane == b + 5, -lo, 0.0)))))
            va = jnp.where(lane < HEAD, take(v), 0.0)
            for val, ref, tref in ((qa, qa_ref, qat_ref), (ka, ka_ref, kat_ref), (va, va_ref, vat_ref)):
                ref[hh] = val.astype(BF)
                tref[hh] = val.T.astype(BF)

    sec = lambda off: pl.BlockSpec((tm, LANES), lambda j, i: (i, off + j))
    par = pl.BlockSpec((1, LANES), lambda j, i: (0, 0))
    nat = pl.BlockSpec((2, tm, LANES), lambda j, i: (j, i, 0))
    trn = pl.BlockSpec((2, LANES, tm), lambda j, i: (j, 0, i))
    return pl.pallas_call(
        body, name="fox_operands", grid=(4, T // tm),
        in_specs=[sec(16), sec(20), sec(24), pl.BlockSpec((tm, LANES), lambda j, i: (i, 0)), par, par],
        out_specs=[nat, trn, nat, trn, nat, trn],
        out_shape=[jax.ShapeDtypeStruct((8, T, LANES), BF), jax.ShapeDtypeStruct((8, LANES, T), BF)] * 3,
        compiler_params=_params(("parallel", "arbitrary")),
    )(P, P, P, fc, g_fq2, g_fk2)


def _fox_forward(qat, ka, vat, T, tq, tk):
    nq, per = T // tq, tq // tk
    assert per == 2
    RC = 64

    def body(qat_ref, ka_ref, vat_ref, o_ref, lse_ref, s_scr, p_scr, a_scr, m_scr, l_scr, acc_scr):
        i = pl.program_id(1)
        sub = lax.broadcasted_iota(jnp.int32, (8, 1), 0)
        row = lax.broadcasted_iota(jnp.int32, (RC, tq), 0)
        col = lax.broadcasted_iota(jnp.int32, (RC, tq), 1)
        m_scr[...] = jnp.full(m_scr.shape, NEG, F32)
        l_scr[...] = jnp.zeros(l_scr.shape, F32)
        acc_scr[...] = jnp.zeros(acc_scr.shape, F32)

        def scores(slot, kb):
            k0 = pl.multiple_of(kb * tk, tk)
            for hh in range(2):
                s_scr[slot, hh] = jnp.dot(ka_ref[hh, pl.ds(k0, tk), :], qat_ref[hh], preferred_element_type=F32)

        def softmax(slot, kb, diagonal):
            shift = kb * tk - i * tq
            for hh in range(2):
                def masked(r):
                    tile = s_scr[slot, hh, r * RC:(r + 1) * RC, :]
                    return jnp.where(row + (r * RC + shift) <= col, tile, NEG) if diagonal else tile

                mx = jnp.max(masked(0), axis=0, keepdims=True)
                for r in range(1, tk // RC):
                    mx = jnp.maximum(mx, jnp.max(masked(r), axis=0, keepdims=True))
                m_old = m_scr[hh, 0:1, :]
                m2 = jnp.maximum(m_old, mx)
                a = jnp.exp(m_old - m2)
                lsum = jnp.zeros((1, tq), F32)
                for r in range(tk // RC):
                    p = jnp.exp(masked(r) - m2)
                    p_scr[slot, hh, r * RC:(r + 1) * RC, :] = p.astype(BF)
                    lsum = lsum + jnp.sum(p, axis=0, keepdims=True)
                m_scr[hh] = jnp.broadcast_to(m2, (8, tq))
                l_scr[hh] = jnp.broadcast_to(a * l_scr[hh, 0:1, :] + lsum, (8, tq))
                a_scr[slot, hh] = jnp.broadcast_to(a, (8, tq))

        def values(slot, kb):
            k0 = pl.multiple_of(kb * tk, tk)
            for hh in range(2):
                pv = jnp.dot(vat_ref[hh, 0:HEAD, pl.ds(k0, tk)], p_scr[slot, hh], preferred_element_type=F32)
                acc_scr[hh] = a_scr[slot, hh, 0:1, :] * acc_scr[hh] + pv

        def pair(kb, diag_first, diag_second, more):
            if more:
                scores(0, kb + 2)
            softmax(1, kb + 1, diag_first)
            values(0, kb)
            if more:
                scores(1, kb + 3)
                softmax(0, kb + 2, diag_second)
            values(1, kb + 1)

        scores(0, 0)
        scores(1, 1)
        softmax(0, 0, True)

        @pl.loop(0, jnp.maximum(i - 1, 0))
        def _(t):
            pair(2 * t, False, False, True)

        @pl.when(i >= 1)
        def _():
            pair(2 * (i - 1), False, True, True)

        pair(2 * i, True, False, False)

        o_ref[...] = jnp.concatenate([acc_scr[hh] / l_scr[hh, 0:1, :] for hh in range(2)], axis=0).T
        lses = [m_scr[hh, 0:1, :] + jnp.log(l_scr[hh, 0:1, :]) for hh in range(2)]
        lse_ref[0] = jnp.where(sub == 0, lses[0], jnp.where(sub == 1, lses[1], 0.0))

    return pl.pallas_call(
        body, name="fox_forward", grid=(4, nq),
        in_specs=[pl.BlockSpec((2, LANES, tq), lambda j, i: (j, 0, i)), pl.BlockSpec((2, T, LANES), lambda j, i: (j, 0, 0)),
                  pl.BlockSpec((2, LANES, T), lambda j, i: (j, 0, 0))],
        out_specs=[pl.BlockSpec((tq, LANES), lambda j, i: (i, j)), pl.BlockSpec((1, 8, tq), lambda j, i: (j, 0, i))],
        out_shape=[jax.ShapeDtypeStruct((T, 4 * LANES), F32), jax.ShapeDtypeStruct((4, 8, T), F32)],
        scratch_shapes=[pltpu.VMEM((2, 2, tk, tq), F32), pltpu.VMEM((2, 2, tk, tq), BF), pltpu.VMEM((2, 2, 8, tq), F32),
                        pltpu.VMEM((2, 8, tq), F32), pltpu.VMEM((2, 8, tq), F32), pltpu.VMEM((2, HEAD, tq), F32)],
        compiler_params=_params(("parallel", "arbitrary")),
    )(qat, ka, vat)


def _fox_cotangent(dmix, fox, T, tm):
    def body(do_ref, o_ref, doa_ref, doat_ref, dl_ref):
        lane = lax.broadcasted_iota(jnp.int32, (1, LANES), 1)
        sub = lax.broadcasted_iota(jnp.int32, (8, 1), 0)
        dob = do_ref[...].astype(BF).astype(F32)
        prod_t = (dob * o_ref[...]).T
        d0 = jnp.sum(prod_t[:HEAD], axis=0, keepdims=True)
        d1 = jnp.sum(prod_t[HEAD:], axis=0, keepdims=True)
        dl_ref[0] = jnp.where(sub == 0, d0, jnp.where(sub == 1, d1, 0.0))
        for hh in range(2):
            val = jnp.where(lane < HEAD, dob if hh == 0 else pltpu.roll(dob, HEAD, 1), 0.0)
            doa_ref[hh] = val.astype(BF)
            doat_ref[hh] = val.T.astype(BF)

    return pl.pallas_call(
        body, name="fox_cotangent", grid=(4, T // tm),
        in_specs=[pl.BlockSpec((tm, LANES), lambda j, i: (i, 4 + j)), pl.BlockSpec((tm, LANES), lambda j, i: (i, j))],
        out_specs=[pl.BlockSpec((2, tm, LANES), lambda j, i: (j, i, 0)), pl.BlockSpec((2, LANES, tm), lambda j, i: (j, 0, i)),
                   pl.BlockSpec((1, 8, tm), lambda j, i: (j, 0, i))],
        out_shape=[jax.ShapeDtypeStruct((8, T, LANES), BF), jax.ShapeDtypeStruct((8, LANES, T), BF),
                   jax.ShapeDtypeStruct((4, 8, T), F32)],
        compiler_params=_params(("parallel", "arbitrary")),
    )(dmix, fox)


def _fox_backward(qa, qat, ka, kat, va, doa, doat, lse, dl, T, tq, tk):
    nq, nk = T // tq, T // tk

    def body(qa_ref, qat_ref, ka_ref, kat_ref, va_ref, doa_ref, doat_ref, lse_ref, dl_ref,
             dq_ref, dk_ref, dv_ref, df_ref, dr_ref, dqt, dk_acc, dv_acc, df_acc, sdp, pds):
        j, kb = pl.program_id(0), pl.program_id(1)
        lane = lax.broadcasted_iota(jnp.int32, (1, LANES), 1)
        first = (kb * tk) // tq

        @pl.when(kb == 0)
        def _():
            dqt[...] = jnp.zeros(dqt.shape, F32)

        dk_acc[...] = jnp.zeros(dk_acc.shape, F32)
        dv_acc[...] = jnp.zeros(dv_acc.shape, F32)
        df_acc[...] = jnp.zeros(df_acc.shape, F32)

        RC = 64
        last = nq - 1

        def products(slot, qi):
            q0 = pl.multiple_of(qi * tq, tq)
            for hh in range(2):
                sdp[slot, hh, 0] = jnp.dot(ka_ref[hh], qat_ref[hh, :, pl.ds(q0, tq)], preferred_element_type=F32)
                sdp[slot, hh, 1] = jnp.dot(va_ref[hh], doat_ref[hh, :, pl.ds(q0, tq)], preferred_element_type=F32)

        def softmax_bwd(slot, qi, diagonal, valid):
            q0 = pl.multiple_of(qi * tq, tq)
            shift = kb * tk - first * tq
            col = lax.broadcasted_iota(jnp.int32, (RC, tq), 1)
            row = lax.broadcasted_iota(jnp.int32, (RC, tq), 0)
            for hh in range(2):
                lse_row = lse_ref[0, hh:hh + 1, pl.ds(q0, tq)]
                dl_row = dl_ref[0, hh:hh + 1, pl.ds(q0, tq)]
                rsum = jnp.zeros((1, tq), F32)
                for r in range(tk // RC):
                    rows = slice(r * RC, (r + 1) * RC)
                    p = jnp.exp(sdp[slot, hh, 0, rows, :] - lse_row)
                    p = jnp.where((row + (r * RC + shift) <= col) if diagonal else valid, p, 0.0)
                    ds = p * (sdp[slot, hh, 1, rows, :] - dl_row)
                    pds[slot, hh, 0, rows, :] = p.astype(BF)
                    pds[slot, hh, 1, rows, :] = ds.astype(BF)
                    rsum = rsum + jnp.sum(ds, axis=0, keepdims=True)
                    part = ds[:, 0:LANES]
                    for c in range(1, tq // LANES):
                        part = part + ds[:, c * LANES:(c + 1) * LANES]
                    df_acc[hh, rows, :] += part
                dqt[hh, HEAD:HEAD + 8, pl.ds(q0, tq)] += jnp.broadcast_to(rsum, (8, tq))

        def accumulate(slot, qi):
            q0 = pl.multiple_of(qi * tq, tq)
            for hh in range(2):
                dv_acc[hh] += jnp.dot(pds[slot, hh, 0], doa_ref[hh, pl.ds(q0, tq), :], preferred_element_type=F32)
                dk_acc[hh] += jnp.dot(pds[slot, hh, 1], qa_ref[hh, pl.ds(q0, tq), :], preferred_element_type=F32)
                dqt[hh, 0:HEAD, pl.ds(q0, tq)] += jnp.dot(kat_ref[hh, 0:HEAD, :], pds[slot, hh, 1], preferred_element_type=F32)

        products(0, first)
        products(1, jnp.minimum(first + 1, last))
        softmax_bwd(0, first, True, None)

        @pl.loop(0, (nq - first + 1) // 2)
        def _(t):
            qi = first + 2 * t
            products(0, jnp.minimum(qi + 2, last))
            softmax_bwd(1, jnp.minimum(qi + 1, last), False, qi + 1 <= last)
            accumulate(0, qi)
            products(1, jnp.minimum(qi + 3, last))
            softmax_bwd(0, jnp.minimum(qi + 2, last), False, qi + 2 <= last)
            accumulate(1, jnp.minimum(qi + 1, last))

        lo = lane < HEAD
        dk_ref[...] = jnp.where(lo, dk_acc[0], pltpu.roll(dk_acc[1], HEAD, 1))
        dv_ref[...] = jnp.where(lo, dv_acc[0], pltpu.roll(dv_acc[1], HEAD, 1)).astype(dv_ref.dtype)
        f0 = -jnp.sum(df_acc[0], axis=1, keepdims=True)
        f1 = -jnp.sum(df_acc[1], axis=1, keepdims=True)
        df_ref[0] = jnp.where(lane == 2 * j, f0, jnp.where(lane == 2 * j + 1, f1, 0.0))

        @pl.when(kb == nk - 1)
        def _():
            for t in range(nq):
                cols = slice(t * tq, (t + 1) * tq)
                dq_ref[cols, :] = jnp.concatenate([dqt[0, 0:HEAD, cols], dqt[1, 0:HEAD, cols]], axis=0).T
                rsum = jnp.concatenate([dqt[0, HEAD:HEAD + 8, cols], dqt[1, HEAD:HEAD + 8, cols],
                                        jnp.zeros((LANES - 16, tq), F32)], axis=0).T
                dr_ref[0, cols, :] = jnp.where(lane == 2 * j, rsum[:, 0:1], jnp.where(lane == 2 * j + 1, rsum[:, 8:9], 0.0))

    nat_full = pl.BlockSpec((2, T, LANES), lambda j, kb: (j, 0, 0))
    trn_full = pl.BlockSpec((2, LANES, T), lambda j, kb: (j, 0, 0))
    nat_blk = pl.BlockSpec((2, tk, LANES), lambda j, kb: (j, kb, 0))
    trn_blk = pl.BlockSpec((2, LANES, tk), lambda j, kb: (j, 0, kb))
    rows = pl.BlockSpec((1, 8, T), lambda j, kb: (j, 0, 0))
    blk = pl.BlockSpec((tk, LANES), lambda j, kb: (kb, j))
    return pl.pallas_call(
        body, name="fox_backward", grid=(4, nk),
        in_specs=[nat_full, trn_full, nat_blk, trn_blk, nat_blk, nat_full, trn_full, rows, rows],
        out_specs=[pl.BlockSpec((T, LANES), lambda j, kb: (0, j)), blk, blk, pl.BlockSpec((1, tk, LANES), lambda j, kb: (j, kb, 0)),
                   pl.BlockSpec((1, T, LANES), lambda j, kb: (j, 0, 0))],
        out_shape=[jax.ShapeDtypeStruct((T, 4 * LANES), F32), jax.ShapeDtypeStruct((T, 4 * LANES), F32),
                   jax.ShapeDtypeStruct((T, 4 * LANES), BF), jax.ShapeDtypeStruct((4, T, LANES), F32),
                   jax.ShapeDtypeStruct((4, T, LANES), F32)],
        scratch_shapes=[pltpu.VMEM((2, HEAD + 8, T), F32), pltpu.VMEM((2, tk, LANES), F32), pltpu.VMEM((2, tk, LANES), F32),
                        pltpu.VMEM((2, tk, LANES), F32), pltpu.VMEM((2, 2, 2, tk, tq), F32), pltpu.VMEM((2, 2, 2, tk, tq), BF)],
        compiler_params=_params(("arbitrary", "arbitrary")),
    )(qa, qat, ka, kat, va, doa, doat, lse, dl)


def _fgate_bwd_col(ffp, bpad, dfc4, drc4, T):
    def body(ff_ref, b_ref, dfc_ref, drc_ref, dff_ref, db_ref):
        lane = lax.broadcasted_iota(jnp.int32, (1, LANES), 1)
        tri = _tri(False)
        carry = jnp.zeros((1, LANES), F32)
        db = jnp.zeros((1, LANES), F32)
        for blk in reversed(range(T // _FB)):
            rows = slice(blk * _FB, (blk + 1) * _FB)
            dcol = dfc_ref[0, rows, :] + drc_ref[0, rows, :]
            for pair in range(1, 4):
                dcol = dcol + (dfc_ref[pair, rows, :] + drc_ref[pair, rows, :])
            dlf = jnp.dot(tri, dcol, precision=lax.Precision.HIGHEST, preferred_element_type=F32) + carry
            carry = dlf[0:1, :]
            z = ff_ref[blk * _FB:(blk + 1) * _FB, :] + b_ref[...]
            dz = jnp.where(lane < 8, dlf * jax.nn.sigmoid(-z), 0.0)
            dff_ref[blk * _FB:(blk + 1) * _FB, :] = dz.astype(dff_ref.dtype)
            db = db + jnp.sum(dz, axis=0, keepdims=True)
        db_ref[...] = db

    return pl.pallas_call(
        body, name="fgate_bwd",
        out_shape=[jax.ShapeDtypeStruct((T, LANES), BF), jax.ShapeDtypeStruct((1, LANES), F32)],
        compiler_params=pltpu.CompilerParams(vmem_limit_bytes=VMEM_LIMIT),
    )(ffp, bpad, dfc4, drc4)


MESH = pl.DeviceIdType.MESH
N_PEERS = N_DEV - 1


def _place():
    return lax.axis_index("x"), lax.axis_index("y"), lax.axis_index("c")


def _all_gather(shard):
    R, W = shard.shape

    def body(x_ref, out_ref, send_sems, recv_sems, local_sem):
        x, y, c = _place()
        me, sibling = (x, y, c), (x, y, 1 - c)
        chips = [(1 - x, y), (x, 1 - y), (1 - x, 1 - y)]

        def slot(px, py, pc):
            return out_ref.at[4 * px + 2 * py + pc]

        def copy(k, block, to, src=None):
            return pltpu.make_async_remote_copy(
                src_ref=slot(*block) if src is None else src, dst_ref=slot(*block),
                send_sem=send_sems.at[k], recv_sem=recv_sems.at[k], device_id=to, device_id_type=MESH)

        mine = pltpu.make_async_copy(x_ref, slot(*me), local_sem)
        mine.start()
        first = [copy(0, me, sibling, src=x_ref)]
        first += [copy(1 + n, me, (*chip, c), src=x_ref) for n, chip in enumerate(chips)]
        for cp in first:
            cp.start()
        passed = [copy(4 + n, (*chip, c), sibling) for n, chip in enumerate(chips)]
        for n, chip in enumerate(chips):
            copy(1 + n, (*chip, c), me).wait_recv()
            passed[n].start()
        copy(0, sibling, me).wait_recv()
        for n, chip in enumerate(chips):
            copy(4 + n, (*chip, 1 - c), me).wait_recv()
        for cp in first + passed:
            cp.wait_send()
        mine.wait()

    return pl.pallas_call(
        body, name="all_gather_weights",
        out_shape=jax.ShapeDtypeStruct((N_DEV, R, W), shard.dtype),
        in_specs=[pl.BlockSpec(memory_space=pl.ANY)], out_specs=pl.BlockSpec(memory_space=pl.ANY),
        scratch_shapes=[pltpu.SemaphoreType.DMA((N_PEERS,)), pltpu.SemaphoreType.DMA((N_PEERS,)), pltpu.SemaphoreType.DMA],
    )(shard)


def _exchange_copies(src_refs, land_refs, send_sems, recv_sems, scatter):
    x, y, c = _place()
    me = 4 * x + 2 * y + c
    copies = []
    for k, (src_ref, land_ref) in enumerate(zip(src_refs, land_refs)):
        for r in range(1, N_DEV):
            px, py, pc = x ^ (r >> 2), y ^ ((r >> 1) & 1), c ^ (r & 1)
            copies.append(pltpu.make_async_remote_copy(
                src_ref=src_ref.at[4 * px + 2 * py + pc] if scatter else src_ref, dst_ref=land_ref.at[me],
                send_sem=send_sems.at[k * N_PEERS + r - 1], recv_sem=recv_sems.at[k * N_PEERS + r - 1],
                device_id=(px, py, pc), device_id_type=MESH))
    return copies


_HBM = pl.BlockSpec(memory_space=pltpu.HBM)
_SEM = pl.BlockSpec(memory_space=pltpu.SEMAPHORE)
_EFFECT = pltpu.SideEffectType.DATAFLOW_SIDE_EFFECTING


def _exchange_start(name, srcs, lands, scatter):
    n = len(srcs)

    def body(*refs):
        send_sems, recv_sems = refs[2 * n], refs[2 * n + 1]
        for cp in _exchange_copies(refs[:n], refs[n:2 * n], send_sems, recv_sems, scatter):
            cp.start()
        token = refs[-1]
        token[...] = jnp.zeros(token.shape, F32)

    arrays = list(srcs) + list(lands)
    out = pl.pallas_call(
        body, name=name,
        out_shape=(pltpu.SemaphoreType.DMA((n * N_PEERS,)), pltpu.SemaphoreType.DMA((n * N_PEERS,)))
        + tuple(pltpu.HBM(a.shape, a.dtype) for a in arrays) + (jax.ShapeDtypeStruct((8, LANES), F32),),
        in_specs=(_HBM,) * (2 * n), out_specs=(_SEM, _SEM) + (_HBM,) * (2 * n) + (pl.BlockSpec(memory_space=pltpu.VMEM),),
        input_output_aliases={k: 2 + k for k in range(2 * n)},
        compiler_params=pltpu.CompilerParams(has_side_effects=_EFFECT),
    )(*(pltpu.with_memory_space_constraint(a, pltpu.HBM) for a in arrays))
    return out[0], out[1], out[2:2 + n], out[2 + n:2 + 2 * n], out[-1]


def _exchange_wait(name, started, after, scatter):
    send_sems, recv_sems, srcs, lands, _ = started
    n = len(srcs)

    def body(*refs):
        copies = _exchange_copies(refs[:n], refs[n:2 * n], refs[2 * n], refs[2 * n + 1], scatter)
        for cp in copies:
            cp.wait_send()
        for cp in copies:
            cp.wait_recv()

    arrays = list(srcs) + list(lands)
    out = pl.pallas_call(
        body, name=name,
        out_shape=tuple(pltpu.HBM(a.shape, a.dtype) for a in arrays),
        in_specs=(_HBM,) * (2 * n) + (_SEM, _SEM, pl.BlockSpec(memory_space=pl.ANY)), out_specs=(_HBM,) * (2 * n),
        input_output_aliases={k: k for k in range(2 * n)},
        compiler_params=pltpu.CompilerParams(has_side_effects=_EFFECT),
    )(*arrays, send_sems, recv_sems, after)
    return out[:n], out[n:]


def _adam_update(g, w, m, v):
    m2 = ADAM_B1 * m + (1.0 - ADAM_B1) * g
    v2 = ADAM_B2 * v + (1.0 - ADAM_B2) * jnp.square(g)
    m_hat = m2 / (1.0 - ADAM_B1 ** ADAM_STEP)
    v_hat = v2 / (1.0 - ADAM_B2 ** ADAM_STEP)
    return g, -ADAM_LR * (m_hat / (jnp.sqrt(v_hat) + ADAM_EPS) + ADAM_WD * w), m2, v2


def _adamw(name, slots, own, w, m, v):
    R, W = w.shape

    def body(s_ref, *refs):
        if own is not None:
            g = refs[0][...].astype(F32)
            refs = refs[1:]
        else:
            g = jnp.zeros((R, W), F32)
        for s in range(N_DEV):
            g = g + s_ref[s].astype(F32)
        w_ref, m_ref, v_ref = refs[:3]
        for o, r in zip(refs[3:], _adam_update(g, w_ref[...], m_ref[...], v_ref[...])):
            o[...] = r

    full = pl.BlockSpec((R, W), lambda i: (0, 0))
    args = [slots] + ([own] if own is not None else []) + [w, m, v]
    return pl.pallas_call(
        body, name=name, grid=(1,),
        in_specs=[pl.BlockSpec((N_DEV, R, W), lambda i: (0, 0, 0))] + [full] * (len(args) - 1),
        out_specs=[full] * 4, out_shape=[jax.ShapeDtypeStruct((R, W), F32)] * 4,
        compiler_params=_params(("arbitrary",)),
    )(*args)


def _tables(T):
    pos = jnp.arange(T, dtype=F32)
    inv_freq = 10000.0 ** (-jnp.arange(0, HEAD, 2, dtype=F32) / HEAD)
    ang = pos[:, None] * inv_freq[None, :]
    cos, sin = jnp.cos(ang), jnp.sin(ang)
    cos4 = jnp.tile(cos, (1, 4))
    sin4 = jnp.tile(jnp.concatenate([-sin, sin], axis=1), (1, 2))
    log_g = jnp.log(1.0 - 2.0 ** (-5.0 - jnp.arange(8, dtype=F32)))
    return cos4, sin4, jnp.repeat(log_g, HEAD)[None, :]


def _local_step(x, mem, target, sp, w_inT, token, fetch_rest, push, push_small):
    T = x.shape[0]
    tm = min(512, T)
    tq = min(256, T)
    tb = min(1024, T)
    cos4, sin4, lg = _tables(T)
    g_fq2 = jnp.tile(sp["g_fox_q"], (1, 2))
    g_fk2 = jnp.tile(sp["g_fox_k"], (1, 2))
    g_ret = sp["g_ret_out"].reshape(1, 8 * HEAD)
    bpad = jnp.pad(sp["b_forget"], ((0, 0), (0, LANES - 8)))
    w_secs = [w_inT[k * 512:(k + 1) * 512] for k in range(7)]
    w_ffT = jnp.pad(w_inT[3584:3592], ((0, LANES - 8), (0, 0)))
    w_mainT = w_inT[:3584]
    tie = lambda p, tok: p + tok[0:1, 0:1]
    tm2, tm4 = min(1024, T), min(2048, T)

    hn1, = _rw_fwd("rms_mix", _rms_fn, [(x, D, 0, False)], [(tie(sp["g_mix"], token), D, 0, False)], [(BF, D)], T, tm, 1)
    P, = _mm("proj_in", [[(hn1, w_mainT, "nt")]], [], _ident, T, 3584, tm4, 512, [F32])
    ffp, = _mm("proj_ff", [[(hn1, w_ffT, "nt")]], [], _ident, T, LANES, tm, LANES, [F32])
    ret, s0 = _ret_fwd(P, cos4, sin4, g_ret, lg, T, tb)
    fc, _ = _fgate_fwd(ffp, bpad, T)
    qa, qat, ka, kat, va, vat = _fox_operands(P, fc, g_fq2, g_fk2, T, tm)
    fox, lse = _fox_forward(qat, ka, vat, T, min(512, T), tq)
    W = fetch_rest(fox)
    w_out_halves = (W["w_out"][:4 * LANES], W["w_out"][4 * LANES:])
    h1, hn2 = _mm("proj_out", [[(ret, w_out_halves[0], "nn"), (fox, w_out_halves[1], "nn")]], [x], _add_rms_epi, T, D, tm2, D,
                  [F32, BF], params=[sp["g_xattn"]])

    qx, = _mm("proj_xq", [[(hn2, W["w_xq"], "nn")]], [], _ident, T, D, tm2, D, [F32])
    memn, = _rw_fwd("rms_mem", _rms_fn, [(mem, D, 0, False)], [(sp["g_mem"], D, 0, False)], [(BF, D)], N_MEM, N_MEM, 1)
    kv, = _mm("proj_xkv", [[(memn, W["w_xkvT"], "nt")]], [], _ident, N_MEM, 2 * D, N_MEM, 512, [F32])
    xa_rows = [(qx, XHEAD, 0, True)]
    xa_params = [(sp["g_xq"], XHEAD, 0, False), (sp["g_xk"], XHEAD, 0, False), (kv, XHEAD, 0, True), (kv, XHEAD, 4, True)]
    xo, = _rw_fwd("xattn_fwd", _xattn_fn, xa_rows, xa_params, [(BF, XHEAD)], T, tm, 4)
    h2, hn3 = _mm("proj_xo", [[(xo, W["w_xo"], "nn")]], [h1], _add_rms_epi, T, D, tm2, D, [F32, BF], params=[sp["g_ffn"]])

    gate, up, act = _mm("ffn_in", [[(hn3, W["w_gateT"], "nt")], [(hn3, W["w_upT"], "nt")]], [], _swiglu_fwd_epi,
                        T, D_FF, tm4, 256, [BF, BF, BF])
    dy, dyb, loss_part = _mm("ffn_out", [[(act, W["w_down"], "nn")]], [h2, target], _add_loss_epi, T, D, tm, D, [F32, BF], n_acc=1)

    dgate, dup = _mm("ffn_out_bwd", [[(dyb, W["w_down"], "nt")]], [gate, up], _swiglu_bwd_epi, T, D_FF, tm4, 256, [BF, BF])
    gW = {}
    gW["w_gateT"], = _mm("dw_gate", [[(dgate, hn3, "tn")]], [], _ident, D_FF, D, 256, D, [BF])
    gW["w_upT"], = _mm("dw_up", [[(dup, hn3, "tn")]], [], _ident, D_FF, D, 256, D, [BF])
    gW["w_down"], = _mm("dw_down", [[(act, dyb, "tn")]], [], _ident, D_FF, D, 256, D, [BF])
    tok = push("ffn", gW)
    gs = {}
    dh2, dh2b, gs["g_ffn"] = _mm("ffn_in_bwd", [[(dgate, W["w_gateT"], "nn"), (dup, W["w_upT"], "nn")]], [h2, dy], _rms_bwd_epi,
                                 T, D, min(256, T), D, [F32, BF], params=[tie(sp["g_ffn"], tok)], n_acc=1)

    dxo, = _mm("proj_xo_bwd", [[(dh2b, W["w_xo"], "nt")]], [], _ident, T, D, tm2, D, [BF])
    gW["w_xo"], = _mm("dw_xo", [[(xo, dh2b, "tn")]], [], _ident, D, D, 256, D, [BF])
    dqx, gs["g_xq"], gs["g_xk"], dkv_k, dkv_v = _rw_bwd(
        "xattn_bwd", _xattn_fn, xa_rows, xa_params, [(dxo, XHEAD, 0, True)], T, tm, 4, [BF], [True, True, True, True])
    dkv = jnp.concatenate([dkv_k[:, :D], dkv_v[:, D:]], axis=1)
    gW["w_xq"], = _mm("dw_xq", [[(hn2, dqx, "tn")]], [], _ident, D, D, 256, D, [BF])
    dmemn, = _mm("proj_xkv_bwd", [[(dkv, W["w_xkvT"], "nn")]], [], _ident, N_MEM, D, N_MEM, 512, [F32])
    gW["w_xkvT"], = _mm("dw_xkv", [[(dkv, memn, "tn")]], [], _ident, 2 * D, D, 512, D, [BF])
    tok = push("xattn", gW)
    gs["g_mem"], = _rw_bwd("rms_mem_bwd", _rms_fn, [(mem, D, 0, False)], [(sp["g_mem"], D, 0, False)], [(dmemn, D, 0, False)],
                           N_MEM, N_MEM, 1, [None], [True])
    dh1, dh1b, gs["g_xattn"] = _mm("proj_xq_bwd", [[(dqx, W["w_xq"], "nt")]], [h1, dh2], _rms_bwd_epi, T, D, tm, D, [F32, BF],
                                   params=[tie(sp["g_xattn"], tok)], n_acc=1)

    dmix, = _mm("proj_out_bwd", [[(dh1b, W["w_out"], "nt")]], [], _ident, T, D, tm2, D, [F32])
    gW["w_out"] = jnp.concatenate([_mm("dw_out_%d" % k, [[(a, dh1b, "tn")]], [], _ident, 4 * LANES, D, 256, D, [BF])[0]
                                   for k, a in enumerate((ret, fox))], axis=0)
    tok = push("out", gW)
    doa, doat, dl = _fox_cotangent(dmix, fox, T, tm)
    dqn, dkn, dfv, dfc4, drc4 = _fox_backward(qa, qat, ka, kat, va, doa, doat, lse + tok[0:1, 0:1], dl, T, tq, tq)
    dfq, dfk, gq2, gk2 = _rw_bwd("fox_prep_bwd", _fox_prep_fn, [(P, LANES, 16, True), (P, LANES, 20, True)],
                                 [(g_fq2, LANES, 0, False), (g_fk2, LANES, 0, False)],
                                 [(dqn, LANES, 0, True), (dkn, LANES, 0, True)], T, tm, 4, [BF, BF], [True, True])
    gs["g_fox_q"] = gq2[:, :HEAD] + gq2[:, HEAD:]
    gs["g_fox_k"] = gk2[:, :HEAD] + gk2[:, HEAD:]
    dff, dbp = _fgate_bwd_col(ffp, bpad, dfc4, drc4, T)
    gs["b_forget"] = dbp[:, :8]
    drq, drk, drv, drg, dg_ret = _ret_bwd(P, cos4, sin4, g_ret, lg, s0, dmix, T, tb)
    gs["g_ret_out"] = dg_ret
    dsecs = [drq, drk, drv, drg, dfq, dfk, dfv]
    g_secs = [_mm("dw_in_%d" % k, [[(d, hn1, "tn")]], [], _ident, 512, D, 256, D, [BF])[0] for k, d in enumerate(dsecs)]
    g_ff, = _mm("dw_in_ff", [[(dff, hn1, "tn")]], [], _ident, LANES, D, LANES, D, [BF])
    gW["w_inT"] = jnp.concatenate(g_secs + [g_ff[:8]], axis=0)
    tok = push("in", gW)
    grad_x, _, gs["g_mix"] = _mm("proj_in_bwd", [[(d, w, "nn") for d, w in zip(dsecs, w_secs)] + [(dff, w_ffT, "nn")]], [x, dh1],
                                 _rms_bwd_epi, T, D, tm, D, [F32, BF], params=[tie(sp["g_mix"], tok)], n_acc=1)
    return grad_x, push_small(gs, loss_part)


_CANON = {"w_in": "w_inT", "w_xkv": "w_xkvT", "w_gate": "w_gateT", "w_up": "w_upT"}
_SMALL = (("g_mix", 0, 0, 1024), ("g_xattn", 1, 0, 1024), ("g_mem", 2, 0, 1024), ("g_ffn", 3, 0, 1024),
          ("g_ret_out", 4, 0, 512), ("g_xq", 4, 512, 256), ("g_xk", 4, 768, 256),
          ("g_fox_q", 5, 0, 64), ("g_fox_k", 5, 64, 64), ("b_forget", 5, 128, 8))
_LOSS_AT = (5, 256)


def _pack_small(tree):
    buf = jnp.zeros((SMALL_ROWS, D), F32)
    for name, r, c, n in _SMALL:
        buf = lax.dynamic_update_slice(buf, tree[name].reshape(1, n).astype(F32), (r, c))
    return buf


def _unpack_small(buf, like):
    return {name: buf[r:r + 1, c:c + n].reshape(like[name].shape) for name, r, c, n in _SMALL}


def _canonical(tree, name):
    a = tree[name][0]
    return a.T if W_SHARD[name][1] else a


def _from_canonical(a, name):
    return (a.T if W_SHARD[name][1] else a)[None]


def kernel(x, mem, g_mix, w_in, b_forget, g_ret_out, g_fox_q, g_fox_k, w_out, g_xattn, w_xq, w_xkv, g_mem, g_xq, g_xk, w_xo, g_ffn, w_gate, w_up, w_down, loss_target, m_g_mix, m_w_in, m_b_forget, m_g_ret_out, m_g_fox_q, m_g_fox_k, m_w_out, m_g_xattn, m_w_xq, m_w_xkv, m_g_mem, m_g_xq, m_g_xk, m_w_xo, m_g_ffn, m_w_gate, m_w_up, m_w_down, v_g_mix, v_w_in, v_b_forget, v_g_ret_out, v_g_fox_q, v_g_fox_k, v_w_out, v_g_xattn, v_w_xq, v_w_xkv, v_g_mem, v_g_xq, v_g_xk, v_w_xo, v_g_ffn, v_w_gate, v_w_up, v_w_down):
    names = ("g_mix", "w_in", "b_forget", "g_ret_out", "g_fox_q", "g_fox_k", "w_out", "g_xattn", "w_xq", "w_xkv", "g_mem",
             "g_xq", "g_xk", "w_xo", "g_ffn", "w_gate", "w_up", "w_down")
    w = dict(zip(names, (g_mix, w_in, b_forget, g_ret_out, g_fox_q, g_fox_k, w_out, g_xattn, w_xq, w_xkv, g_mem, g_xq, g_xk,
                         w_xo, g_ffn, w_gate, w_up, w_down)))
    m = dict(zip(names, (m_g_mix, m_w_in, m_b_forget, m_g_ret_out, m_g_fox_q, m_g_fox_k, m_w_out, m_g_xattn, m_w_xq, m_w_xkv,
                         m_g_mem, m_g_xq, m_g_xk, m_w_xo, m_g_ffn, m_w_gate, m_w_up, m_w_down)))
    v = dict(zip(names, (v_g_mix, v_w_in, v_b_forget, v_g_ret_out, v_g_fox_q, v_g_fox_k, v_w_out, v_g_xattn, v_w_xq, v_w_xkv,
                         v_g_mem, v_g_xq, v_g_xk, v_w_xo, v_g_ffn, v_w_gate, v_w_up, v_w_down)))
    small_names = [s[0] for s in _SMALL]
    me = 4 * lax.axis_index("x") + 2 * lax.axis_index("y") + lax.axis_index("c")

    first = _all_gather(_canonical(w, "w_in").astype(BF))
    first, rest = lax.optimization_barrier((first, [_canonical(w, n).astype(BF) for n in GATHER_REST]))
    rest_started = _exchange_start("gather_rest_start", rest, [jnp.broadcast_to(a[None], (N_DEV,) + a.shape) for a in rest],
                                   scatter=False)

    def fetch_rest(after):
        lands = _exchange_wait("gather_rest_wait", rest_started, after, scatter=False)[1]
        return {_CANON.get(n, n): a.reshape(N_DEV * a.shape[1], D) for n, a in zip(GATHER_REST, lands)}

    pushed = {}

    def push(group, grads):
        srcs = [grads[_CANON.get(n, n)].reshape(N_DEV, W_SHARD[n][0], D) for n in SCATTER_GROUPS[group]]
        pushed[group] = _exchange_start("scatter_%s_start" % group, srcs, [jnp.zeros(a.shape, BF) for a in srcs], scatter=True)
        return pushed[group][4]

    def push_small(gs, loss_part):
        small = lax.dynamic_update_slice(_pack_small(gs), loss_part[:, :1], _LOSS_AT)
        pushed["small"] = _exchange_start("gather_small_start", [small], [jnp.broadcast_to(small[None], (N_DEV,) + small.shape)],
                                          scatter=False)
        return pushed["small"][4]

    sp = {n: w[n].reshape(1, -1) for n in small_names}
    grad_x, done = _local_step(x[0], mem[0], loss_target[0], sp, first.reshape(N_DEV * W_SHARD["w_in"][0], D),
                               rest_started[4], fetch_rest, push, push_small)

    results, after = {}, done
    for group in ("ffn", "xattn", "out", "small", "in"):
        if group == "small":
            recv_small = _exchange_wait("gather_small_wait", pushed["small"], after, scatter=False)[1][0]
            g_sm, d_sm, m_sm, v_sm = _adamw("adamw_small", recv_small, None, _pack_small(w), _pack_small(m), _pack_small(v))
            after = g_sm
            continue
        sents, recvs = _exchange_wait("scatter_%s_wait" % group, pushed[group], after, scatter=True)
        for name, sent, recv in zip(SCATTER_GROUPS[group], sents, recvs):
            own = lax.dynamic_index_in_dim(sent, me, axis=0, keepdims=False)
            res = _adamw("adamw_" + name, recv, own, *(_canonical(t, name) for t in (w, m, v)))
            results[name] = [_from_canonical(r, name) for r in res]
        after = results[SCATTER_GROUPS[group][-1]][0]
    loss = g_sm[_LOSS_AT[0], _LOSS_AT[1]]

    outs = []
    for k, sm in enumerate((g_sm, d_sm, m_sm, v_sm)):
        tree = _unpack_small(sm, w)
        tree.update({name: res[k] for name, res in results.items()})
        outs += [tree[n] for n in names]
    return (loss, grad_x[None], *outs)
```

```python
import jax
import jax.numpy as jnp
from jax import lax
from jax.experimental import pallas as pl
from jax.experimental.pallas import tpu as pltpu

F32 = jnp.float32
BF = jnp.bfloat16

D = 1024
HEAD = 64
CHUNK = 64
N_MEM = 256
XHEAD = 256
D_FF = 2816
EPS = 1e-6
NEG = -1e30
LANES = 128
N_DEV = 8
V7X_VMEM_BYTES = 64 * 1024 * 1024
VMEM_LIMIT = V7X_VMEM_BYTES - 8 * 1024 * 1024

ADAM_LR, ADAM_B1, ADAM_B2, ADAM_EPS, ADAM_WD, ADAM_STEP = 0.001, 0.9, 0.999, 1e-08, 0.01, 10

W_SHARD = {"w_in": (449, True), "w_out": (128, False), "w_xq": (128, False), "w_xkv": (256, True),
           "w_xo": (128, False), "w_gate": (352, True), "w_up": (352, True), "w_down": (352, False)}
GATHER_REST = ("w_out", "w_xq", "w_xkv", "w_xo", "w_gate", "w_up", "w_down")
SCATTER_GROUPS = {"ffn": ("w_gate", "w_up", "w_down"), "xattn": ("w_xq", "w_xo", "w_xkv"), "out": ("w_out",), "in": ("w_in",)}
SMALL_ROWS = 8

NT = (((1,), (1,)), ((), ()))
NN = (((1,), (0,)), ((), ()))
TN = (((0,), (0,)), ((), ()))
_DIMS = {"nn": NN, "nt": NT, "tn": TN}


def _params(sem):
    return pltpu.CompilerParams(dimension_semantics=sem, vmem_limit_bytes=VMEM_LIMIT)


def _mm(name, products, extras, epilogue, M, N, tm, tn, out_dtypes, params=(), n_acc=0):
    assert n_acc == 0 or tn == N
    flat = [t for p in products for t in p]
    counts = [len(p) for p in products]
    in_specs, args = [], []
    for a, b, form in flat:
        if form == "tn":
            in_specs.append(pl.BlockSpec((a.shape[0], tm), lambda i, j: (0, i)))
        else:
            in_specs.append(pl.BlockSpec((tm, a.shape[1]), lambda i, j: (i, 0)))
        if form == "nt":
            in_specs.append(pl.BlockSpec((tn, b.shape[1]), lambda i, j: (j, 0)))
        else:
            in_specs.append(pl.BlockSpec((b.shape[0], tn), lambda i, j: (0, j)))
        args += [a, b]
    for e in extras:
        in_specs.append(pl.BlockSpec((tm, tn), lambda i, j: (i, j)))
        args.append(e)
    for p in params:
        in_specs.append(pl.BlockSpec((1, tn), lambda i, j: (0, j)))
        args.append(p)
    n_in = len(args)
    n_out = len(out_dtypes)

    def body(*refs):
        ins, outs = refs[:n_in], refs[n_in:]
        prods, p = [], 0
        for c in counts:
            acc = None
            for _ in range(c):
                a = ins[2 * p][...].astype(BF)
                b = ins[2 * p + 1][...].astype(BF)
                d = lax.dot_general(a, b, _DIMS[flat[p][2]], preferred_element_type=F32)
                acc = d if acc is None else acc + d
                p += 1
            prods.append(acc)
        ex = [r[...].astype(F32) for r in ins[2 * len(flat):]]
        res = epilogue(*prods, *ex)
        for o, r in zip(outs[:n_out], res[:n_out]):
            o[...] = r.astype(o.dtype)
        for o, r in zip(outs[n_out:], res[n_out:]):
            @pl.when(pl.program_id(0) == 0)
            def _(o=o):
                o[...] = jnp.zeros(o.shape, F32)
            o[...] += r

    return pl.pallas_call(
        body, name=name, grid=(M // tm, N // tn), in_specs=in_specs,
        out_specs=[pl.BlockSpec((tm, tn), lambda i, j: (i, j)) for _ in out_dtypes]
        + [pl.BlockSpec((1, tn), lambda i, j: (0, j)) for _ in range(n_acc)],
        out_shape=[jax.ShapeDtypeStruct((M, N), dt) for dt in out_dtypes] + [jax.ShapeDtypeStruct((1, N), F32)] * n_acc,
        compiler_params=_params(("arbitrary", "arbitrary")),
    )(*args)


def _ident(x):
    return (x,)


def _spec(rows, w, off, per_j):
    if per_j:
        return pl.BlockSpec((rows, w), lambda j, i: (i, off + j))
    return pl.BlockSpec((rows, w), lambda j, i: (i, off))


def _pspec(rows, w, off, per_j):
    if per_j:
        return pl.BlockSpec((rows, w), lambda j, i: (0, off + j))
    return pl.BlockSpec((rows, w), lambda j, i: (0, off))


def _rw_fwd(name, fn, rows, params, outs, T, tm, nj, n_acc=0):
    in_specs = [_spec(tm, w, off, pj) for _, w, off, pj in rows] + [_pspec(a.shape[0], w, off, pj) for a, w, off, pj in params]
    args = [r[0] for r in rows] + [p[0] for p in params]
    n_in, n_out = len(args), len(outs)
    out_specs = [pl.BlockSpec((tm, w), lambda j, i: (i, j)) for _, w in outs]
    out_shape = [jax.ShapeDtypeStruct((T, nj * w), dt) for dt, w in outs]
    out_specs += [pl.BlockSpec((1, LANES), lambda j, i: (0, 0)) for _ in range(n_acc)]
    out_shape += [jax.ShapeDtypeStruct((1, LANES), F32) for _ in range(n_acc)]

    def body(*refs):
        vals = [r[...].astype(F32) for r in refs[:n_in]]
        res = fn(*vals)
        orefs = refs[n_in:]
        for k in range(n_out):
            orefs[k][...] = res[k].astype(orefs[k].dtype)
        first = (pl.program_id(0) == 0) & (pl.program_id(1) == 0)
        for k in range(n_acc):
            @pl.when(first)
            def _(k=k):
                orefs[n_out + k][...] = jnp.zeros((1, LANES), F32)
            orefs[n_out + k][...] += res[n_out + k]

    return pl.pallas_call(
        body, name=name, grid=(nj, T // tm), in_specs=in_specs, out_specs=out_specs, out_shape=out_shape,
        compiler_params=_params(("arbitrary", "arbitrary")),
    )(*args)


def _rw_bwd(name, fn, rows, params, cots, T, tm, nj, row_grads, param_grads, resid=None):
    in_specs = ([_spec(tm, w, off, pj) for _, w, off, pj in rows] + [_pspec(a.shape[0], w, off, pj) for a, w, off, pj in params]
                + [_spec(tm, w, off, pj) for _, w, off, pj in cots])
    args = [r[0] for r in rows] + [p[0] for p in params] + [c[0] for c in cots]
    if resid is not None:
        in_specs.append(_spec(tm, rows[0][1], rows[0][2], rows[0][3]))
        args.append(resid)
    nr, npar, nc = len(rows), len(params), len(cots)
    out_specs, out_shape, kinds = [], [], []
    for k, dts in enumerate(row_grads):
        for dt in (dts if isinstance(dts, (list, tuple)) else [dts]):
            if dt is not None:
                w = rows[k][1]
                out_specs.append(pl.BlockSpec((tm, w), lambda j, i: (i, j)))
                out_shape.append(jax.ShapeDtypeStruct((T, nj * w), dt))
                kinds.append(("row", k))
    for k, need in enumerate(param_grads):
        if need:
            a, w, off, pj = params[k]
            out_specs.append(_pspec(a.shape[0], w, off, pj))
            out_shape.append(jax.ShapeDtypeStruct(a.shape, F32))
            kinds.append(("par", k))

    def body(*refs):
        vals = [r[...].astype(F32) for r in refs[:nr + npar]]
        ct = tuple(r[...].astype(F32) for r in refs[nr + npar:nr + npar + nc])
        _, vjp = jax.vjp(lambda *a: tuple(fn(*a)), *vals)
        grads = list(vjp(ct))
        n_in = nr + npar + nc + (resid is not None)
        if resid is not None:
            grads[0] = grads[0] + refs[n_in - 1][...].astype(F32)
        orefs = refs[n_in:]
        j, i = pl.program_id(0), pl.program_id(1)
        for o, (kind, k) in zip(orefs, kinds):
            if kind == "row":
                o[...] = grads[k].astype(o.dtype)
            else:
                first = (i == 0) if params[k][3] else ((i == 0) & (j == 0))

                @pl.when(first)
                def _(o=o):
                    o[...] = jnp.zeros(o.shape, F32)
                o[...] += grads[nr + k]

    return pl.pallas_call(
        body, name=name, grid=(nj, T // tm), in_specs=in_specs, out_specs=out_specs, out_shape=out_shape,
        compiler_params=_params(("arbitrary", "arbitrary")),
    )(*args)


def _rms(x, g):
    return x * lax.rsqrt(jnp.mean(x * x, axis=-1, keepdims=True) + EPS) * g


def _rms_fn(x, g):
    return (_rms(x, g),)


def _lo_mask():
    return lax.broadcasted_iota(jnp.int32, (1, LANES), 1) < HEAD


def _gmean(x, lo):
    s0 = jnp.sum(jnp.where(lo, x, 0.0), axis=-1, keepdims=True)
    s1 = jnp.sum(jnp.where(lo, 0.0, x), axis=-1, keepdims=True)
    return jnp.where(lo, s0, s1) * (1.0 / HEAD)


def _fox_prep_fn(fq, fk, gq, gk):
    lo = _lo_mask()
    qn = fq * lax.rsqrt(_gmean(fq * fq, lo) + EPS) * gq * (HEAD ** -0.5)
    kn = fk * lax.rsqrt(_gmean(fk * fk, lo) + EPS) * gk
    return qn, kn


@jax.custom_vjp
def _swap_halves(x):
    bit = (lax.broadcasted_iota(jnp.int32, (1, LANES), 1) & (HEAD // 2)) == 0
    return jnp.where(bit, pltpu.roll(x, LANES - HEAD // 2, 1), pltpu.roll(x, HEAD // 2, 1))


_swap_halves.defvjp(lambda x: (_swap_halves(x), None), lambda _, g: (_swap_halves(g),))


def _ret_fn(rq, rk, rv, rg, cos, sin, s_in, g, lg):
    tb = rq.shape[0]
    nc = tb // CHUNK
    lo = _lo_mask()
    row = lax.broadcasted_iota(jnp.int32, (LANES, 1), 0) < HEAD
    same_head = row == lo
    q = (rq * cos + _swap_halves(rq) * sin) * (HEAD ** -0.5)
    k = rk * cos + _swap_halves(rk) * sin
    q3, k3, v3 = q.reshape(nc, CHUNK, LANES), k.reshape(nc, CHUNK, LANES), rv.reshape(nc, CHUNK, LANES)
    pos = lax.broadcasted_iota(jnp.int32, (CHUNK, 1), 0).astype(F32)
    q_decay = jnp.exp(lg * (pos + 1.0))
    k_decay = jnp.exp(lg * (CHUNK - 1.0 - pos))
    chunk_decay = jnp.exp(lg * float(CHUNK))
    dist = jnp.abs(lax.broadcasted_iota(jnp.int32, (CHUNK, CHUNK), 0) - lax.broadcasted_iota(jnp.int32, (CHUNK, CHUNK), 1)).astype(F32)
    v3b = v3.astype(BF)
    intra = []
    for hh in range(2):
        hm = lo if hh == 0 else ~lo
        lg_h = lg[:, hh * HEAD:hh * HEAD + 1]
        qm = jnp.where(hm, q3, 0.0).astype(BF)
        sc = jnp.einsum("nid,njd->nij", qm, k3.astype(BF), preferred_element_type=F32) * jnp.exp(lg_h * dist)[None]
        intra.append(jnp.einsum("nij,nje->nie", sc.astype(BF), v3b, preferred_element_type=F32))
    o = jnp.where(lo, intra[0], intra[1])
    kv = jnp.einsum("njd,nje->nde", (k3 * k_decay[None]).astype(BF), v3b, preferred_element_type=F32)
    kv = jnp.where(same_head[None], kv, 0.0)
    state, states = s_in, []
    for n in range(nc):
        states.append(state)
        state = state * chunk_decay + kv[n]
    s_prev = jnp.stack(states, axis=0)
    o = o + jnp.einsum("nid,nde->nie", (q3 * q_decay[None]).astype(BF), s_prev.astype(BF), preferred_element_type=F32)
    o = o.reshape(tb, LANES)
    mu = _gmean(o, lo)
    oc = o - mu
    y = oc * lax.rsqrt(_gmean(oc * oc, lo) + EPS) * g
    return jax.nn.silu(rg) * y, state


def _xattn_fn(qx, gq, gk, kk, vv):
    q = _rms(qx, gq)
    k = _rms(kk, gk)
    logits = lax.dot_general(q.astype(BF), k.astype(BF), NT, preferred_element_type=F32) * (XHEAD ** -0.5)
    p = jax.nn.softmax(logits, axis=-1)
    return (jnp.dot(p.astype(BF), vv.astype(BF), preferred_element_type=F32),)


def _swiglu_fwd_epi(g, u):
    return g, u, jax.nn.silu(g) * u


def _swiglu_bwd_epi(dact, g, u):
    _, vjp = jax.vjp(lambda a, b: jax.nn.silu(a) * b, g, u)
    return vjp(dact)


def _add_rms_epi(acc, resid, g):
    h = acc + resid
    return h, _rms(h, g)


def _add_loss_epi(acc, resid, target):
    err = (acc + resid) - target
    dy = err * (1.0 / D)
    part = jnp.sum(jnp.sum(err * err, axis=0, keepdims=True), axis=1, keepdims=True) * (0.5 / D)
    return dy, dy, jnp.broadcast_to(part, (1, err.shape[1]))


def _rms_bwd_epi(dhn, h, skip, g):
    _, vjp = jax.vjp(_rms, h, g)
    dh, dg = vjp(dhn)
    dh = dh + skip
    return dh, dh, dg


def _ret_fwd(P, cos, sin, g_ret, lg, T, tb):
    nb = T // tb

    def body(rq, rk, rv, rg, c, s, g, l, o_ref, s0_ref, state):
        @pl.when(pl.program_id(1) == 0)
        def _():
            state[...] = jnp.zeros(state.shape, F32)
        s0_ref[0, 0] = state[...]
        out, s_new = _ret_fn(rq[...], rk[...], rv[...], rg[...], c[...], s[...], state[...], g[...], l[...])
        o_ref[...] = out.astype(o_ref.dtype)
        state[...] = s_new

    sec = lambda off: pl.BlockSpec((tb, LANES), lambda j, i: (i, off + j))
    tab = pl.BlockSpec((tb, LANES), lambda j, i: (i, 0))
    par = pl.BlockSpec((1, LANES), lambda j, i: (0, j))
    return pl.pallas_call(
        body, name="ret_fwd", grid=(4, nb),
        in_specs=[sec(0), sec(4), sec(8), sec(12), tab, tab, par, par],
        out_specs=[pl.BlockSpec((tb, LANES), lambda j, i: (i, j)), pl.BlockSpec((1, 1, LANES, LANES), lambda j, i: (j, i, 0, 0))],
        out_shape=[jax.ShapeDtypeStruct((T, 4 * LANES), BF), jax.ShapeDtypeStruct((4, nb, LANES, LANES), F32)],
        scratch_shapes=[pltpu.VMEM((LANES, LANES), F32)],
        compiler_params=_params(("arbitrary", "arbitrary")),
    )(P, P, P, P, cos, sin, g_ret, lg)


def _ret_bwd(P, cos, sin, g_ret, lg, s0, dmix, T, tb):
    nb = T // tb

    def body(rq, rk, rv, rg, c, s, g, l, s0_ref, do, drq, drk, drv, drg, dg, dstate):
        i = pl.program_id(1)

        @pl.when(i == 0)
        def _():
            dstate[...] = jnp.zeros(dstate.shape, F32)
            dg[...] = jnp.zeros(dg.shape, F32)

        cc, ss, ll = c[...], s[...], l[...]
        _, vjp = jax.vjp(lambda a, b, v, gate, st, gg: _ret_fn(a, b, v, gate, cc, ss, st, gg, ll),
                         rq[...], rk[...], rv[...], rg[...], s0_ref[0, 0], g[...])
        ga, gb, gv, ggate, gst, ggain = vjp((do[...], dstate[...]))
        drq[...] = ga.astype(drq.dtype)
        drk[...] = gb.astype(drk.dtype)
        drv[...] = gv.astype(drv.dtype)
        drg[...] = ggate.astype(drg.dtype)
        dstate[...] = gst
        dg[...] += ggain

    rev = lambda i: nb - 1 - i
    sec = lambda off: pl.BlockSpec((tb, LANES), lambda j, i: (rev(i), off + j))
    tab = pl.BlockSpec((tb, LANES), lambda j, i: (rev(i), 0))
    par = pl.BlockSpec((1, LANES), lambda j, i: (0, j))
    outb = pl.BlockSpec((tb, LANES), lambda j, i: (rev(i), j))
    return pl.pallas_call(
        body, name="ret_bwd", grid=(4, nb),
        in_specs=[sec(0), sec(4), sec(8), sec(12), tab, tab, par, par,
                  pl.BlockSpec((1, 1, LANES, LANES), lambda j, i: (j, rev(i), 0, 0)), outb],
        out_specs=[outb, outb, outb, outb, par],
        out_shape=[jax.ShapeDtypeStruct((T, 4 * LANES), BF)] * 4 + [jax.ShapeDtypeStruct((1, 4 * LANES), F32)],
        scratch_shapes=[pltpu.VMEM((LANES, LANES), F32)],
        compiler_params=_params(("arbitrary", "arbitrary")),
    )(P, P, P, P, cos, sin, g_ret, lg, s0, dmix)


_FB = 128


def _tri(lower):
    r = lax.broadcasted_iota(jnp.int32, (_FB, _FB), 0)
    c = lax.broadcasted_iota(jnp.int32, (_FB, _FB), 1)
    return ((r >= c) if lower else (r <= c)).astype(F32)


def _fgate_fwd(ffp, bpad, T):
    def body(ff_ref, b_ref, fc_ref, fr_ref):
        lane = lax.broadcasted_iota(jnp.int32, (1, LANES), 1)
        tri = _tri(True)
        carry = jnp.zeros((1, LANES), F32)
        for blk in range(T // _FB):
            z = ff_ref[blk * _FB:(blk + 1) * _FB, :] + b_ref[...]
            lf = jnp.where(lane < 8, jax.nn.log_sigmoid(z), 0.0)
            f = jnp.dot(tri, lf, precision=lax.Precision.HIGHEST, preferred_element_type=F32) + carry
            carry = f[_FB - 1:_FB, :]
            fc_ref[blk * _FB:(blk + 1) * _FB, :] = f
            fr_ref[:, blk * _FB:(blk + 1) * _FB] = f.T[:8, :]

    return pl.pallas_call(
        body, name="fgate_fwd",
        out_shape=[jax.ShapeDtypeStruct((T, LANES), F32), jax.ShapeDtypeStruct((8, T), F32)],
        compiler_params=pltpu.CompilerParams(vmem_limit_bytes=VMEM_LIMIT),
    )(ffp, bpad)


_BIAS_LANE = HEAD


def _head_bias_col(fc, head):
    lane = lax.broadcasted_iota(jnp.int32, (1, LANES), 1)
    return jnp.sum(jnp.where(lane == head, fc, 0.0), axis=-1, keepdims=True)


def _split3(f):
    hi = f.astype(BF).astype(F32)
    mid = (f - hi).astype(BF).astype(F32)
    lo = ((f - hi) - mid).astype(BF).astype(F32)
    return hi, mid, lo


def _fox_operands(P, fc, g_fq2, g_fk2, T, tm):
    def body(fq_ref, fk_ref, fv_ref, fc_ref, gq_ref, gk_ref, qa_ref, qat_ref, ka_ref, kat_ref, va_ref, vat_ref):
        j = pl.program_id(0)
        lane = lax.broadcasted_iota(jnp.int32, (1, LANES), 1)
        qn, kn = _fox_prep_fn(fq_ref[...], fk_ref[...], gq_ref[...], gk_ref[...])
        v = fv_ref[...]
        fcb = fc_ref[...]
        b = _BIAS_LANE
        for hh in range(2):
            hi, mid, lo = _split3(_head_bias_col(fcb, 2 * j + hh))
            take = (lambda a: a) if hh == 0 else (lambda a: pltpu.roll(a, HEAD, 1))
            qa = jnp.where(lane < HEAD, take(qn), jnp.where(lane == b, hi, jnp.where(lane == b + 1, mid, jnp.where(
                lane == b + 2, lo, jnp.where(lane < b + 6, 1.0, 0.0)))))
            ka = jnp.where(lane < HEAD, take(kn), jnp.where(lane < b + 3, 1.0, jnp.where(lane == b + 3, -hi, jnp.where(
                lane == b + 4, -mid, jnp.where(lane == b + 5, -lo, 0.0)))))
            va = jnp.where(lane < HEAD, take(v), 0.0)
            for val, ref, tref in ((qa, qa_ref, qat_ref), (ka, ka_ref, kat_ref), (va, va_ref, vat_ref)):
                ref[hh] = val.astype(BF)
                tref[hh] = val.T.astype(BF)

    sec = lambda off: pl.BlockSpec((tm, LANES), lambda j, i: (i, off + j))
    par = pl.BlockSpec((1, LANES), lambda j, i: (0, 0))
    nat = pl.BlockSpec((2, tm, LANES), lambda j, i: (j, i, 0))
    trn = pl.BlockSpec((2, LANES, tm), lambda j, i: (j, 0, i))
    return pl.pallas_call(
        body, name="fox_operands", grid=(4, T // tm),
        in_specs=[sec(16), sec(20), sec(24), pl.BlockSpec((tm, LANES), lambda j, i: (i, 0)), par, par],
        out_specs=[nat, trn, nat, trn, nat, trn],
        out_shape=[jax.ShapeDtypeStruct((8, T, LANES), BF), jax.ShapeDtypeStruct((8, LANES, T), BF)] * 3,
        compiler_params=_params(("parallel", "arbitrary")),
    )(P, P, P, fc, g_fq2, g_fk2)


def _fox_forward(qat, ka, vat, T, tq, tk):
    nq, per = T // tq, tq // tk
    assert per == 2
    RC = 64

    def body(qat_ref, ka_ref, vat_ref, o_ref, lse_ref, s_scr, p_scr, a_scr, m_scr, l_scr, acc_scr):
        i = pl.program_id(1)
        sub = lax.broadcasted_iota(jnp.int32, (8, 1), 0)
        row = lax.broadcasted_iota(jnp.int32, (RC, tq), 0)
        col = lax.broadcasted_iota(jnp.int32, (RC, tq), 1)
        m_scr[...] = jnp.full(m_scr.shape, NEG, F32)
        l_scr[...] = jnp.zeros(l_scr.shape, F32)
        acc_scr[...] = jnp.zeros(acc_scr.shape, F32)

        def scores(slot, kb):
            k0 = pl.multiple_of(kb * tk, tk)
            for hh in range(2):
                s_scr[slot, hh] = jnp.dot(ka_ref[hh, pl.ds(k0, tk), :], qat_ref[hh], preferred_element_type=F32)

        def softmax(slot, kb, diagonal):
            shift = kb * tk - i * tq
            for hh in range(2):
                def masked(r):
                    tile = s_scr[slot, hh, r * RC:(r + 1) * RC, :]
                    return jnp.where(row + (r * RC + shift) <= col, tile, NEG) if diagonal else tile

                mx = jnp.max(masked(0), axis=0, keepdims=True)
                for r in range(1, tk // RC):
                    mx = jnp.maximum(mx, jnp.max(masked(r), axis=0, keepdims=True))
                m_old = m_scr[hh, 0:1, :]
                m2 = jnp.maximum(m_old, mx)
                a = jnp.exp(m_old - m2)
                lsum = jnp.zeros((1, tq), F32)
                for r in range(tk // RC):
                    p = jnp.exp(masked(r) - m2)
                    p_scr[slot, hh, r * RC:(r + 1) * RC, :] = p.astype(BF)
                    lsum = lsum + jnp.sum(p, axis=0, keepdims=True)
                m_scr[hh] = jnp.broadcast_to(m2, (8, tq))
                l_scr[hh] = jnp.broadcast_to(a * l_scr[hh, 0:1, :] + lsum, (8, tq))
                a_scr[slot, hh] = jnp.broadcast_to(a, (8, tq))

        def values(slot, kb):
            k0 = pl.multiple_of(kb * tk, tk)
            for hh in range(2):
                pv = jnp.dot(vat_ref[hh, 0:HEAD, pl.ds(k0, tk)], p_scr[slot, hh], preferred_element_type=F32)
                acc_scr[hh] = a_scr[slot, hh, 0:1, :] * acc_scr[hh] + pv

        def pair(kb, diag_first, diag_second, more):
            if more:
                scores(0, kb + 2)
            softmax(1, kb + 1, diag_first)
            values(0, kb)
            if more:
                scores(1, kb + 3)
                softmax(0, kb + 2, diag_second)
            values(1, kb + 1)

        scores(0, 0)
        scores(1, 1)
        softmax(0, 0, True)

        @pl.loop(0, jnp.maximum(i - 1, 0))
        def _(t):
            pair(2 * t, False, False, True)

        @pl.when(i >= 1)
        def _():
            pair(2 * (i - 1), False, True, True)

        pair(2 * i, True, False, False)

        o_ref[...] = jnp.concatenate([acc_scr[hh] / l_scr[hh, 0:1, :] for hh in range(2)], axis=0).T
        lses = [m_scr[hh, 0:1, :] + jnp.log(l_scr[hh, 0:1, :]) for hh in range(2)]
        lse_ref[0] = jnp.where(sub == 0, lses[0], jnp.where(sub == 1, lses[1], 0.0))

    return pl.pallas_call(
        body, name="fox_forward", grid=(4, nq),
        in_specs=[pl.BlockSpec((2, LANES, tq), lambda j, i: (j, 0, i)), pl.BlockSpec((2, T, LANES), lambda j, i: (j, 0, 0)),
                  pl.BlockSpec((2, LANES, T), lambda j, i: (j, 0, 0))],
        out_specs=[pl.BlockSpec((tq, LANES), lambda j, i: (i, j)), pl.BlockSpec((1, 8, tq), lambda j, i: (j, 0, i))],
        out_shape=[jax.ShapeDtypeStruct((T, 4 * LANES), F32), jax.ShapeDtypeStruct((4, 8, T), F32)],
        scratch_shapes=[pltpu.VMEM((2, 2, tk, tq), F32), pltpu.VMEM((2, 2, tk, tq), BF), pltpu.VMEM((2, 2, 8, tq), F32),
                        pltpu.VMEM((2, 8, tq), F32), pltpu.VMEM((2, 8, tq), F32), pltpu.VMEM((2, HEAD, tq), F32)],
        compiler_params=_params(("parallel", "arbitrary")),
    )(qat, ka, vat)


def _fox_cotangent(dmix, fox, T, tm):
    def body(do_ref, o_ref, doa_ref, doat_ref, dl_ref):
        lane = lax.broadcasted_iota(jnp.int32, (1, LANES), 1)
        sub = lax.broadcasted_iota(jnp.int32, (8, 1), 0)
        dob = do_ref[...].astype(BF).astype(F32)
        prod_t = (dob * o_ref[...]).T
        d0 = jnp.sum(prod_t[:HEAD], axis=0, keepdims=True)
        d1 = jnp.sum(prod_t[HEAD:], axis=0, keepdims=True)
        dl_ref[0] = jnp.where(sub == 0, d0, jnp.where(sub == 1, d1, 0.0))
        for hh in range(2):
            val = jnp.where(lane < HEAD, dob if hh == 0 else pltpu.roll(dob, HEAD, 1), 0.0)
            doa_ref[hh] = val.astype(BF)
            doat_ref[hh] = val.T.astype(BF)

    return pl.pallas_call(
        body, name="fox_cotangent", grid=(4, T // tm),
        in_specs=[pl.BlockSpec((tm, LANES), lambda j, i: (i, 4 + j)), pl.BlockSpec((tm, LANES), lambda j, i: (i, j))],
        out_specs=[pl.BlockSpec((2, tm, LANES), lambda j, i: (j, i, 0)), pl.BlockSpec((2, LANES, tm), lambda j, i: (j, 0, i)),
                   pl.BlockSpec((1, 8, tm), lambda j, i: (j, 0, i))],
        out_shape=[jax.ShapeDtypeStruct((8, T, LANES), BF), jax.ShapeDtypeStruct((8, LANES, T), BF),
                   jax.ShapeDtypeStruct((4, 8, T), F32)],
        compiler_params=_params(("parallel", "arbitrary")),
    )(dmix, fox)


def _fox_backward(qa, qat, ka, kat, va, doa, doat, lse, dl, T, tq, tk):
    nq, nk = T // tq, T // tk

    def body(qa_ref, qat_ref, ka_ref, kat_ref, va_ref, doa_ref, doat_ref, lse_ref, dl_ref,
             dq_ref, dk_ref, dv_ref, df_ref, dr_ref, dqt, dk_acc, dv_acc, df_acc, sdp, pds):
        j, kb = pl.program_id(0), pl.program_id(1)
        lane = lax.broadcasted_iota(jnp.int32, (1, LANES), 1)
        first = (kb * tk) // tq

        @pl.when(kb == 0)
        def _():
            dqt[...] = jnp.zeros(dqt.shape, F32)

        dk_acc[...] = jnp.zeros(dk_acc.shape, F32)
        dv_acc[...] = jnp.zeros(dv_acc.shape, F32)
        df_acc[...] = jnp.zeros(df_acc.shape, F32)

        RC = 64
        last = nq - 1

        def products(slot, qi):
            q0 = pl.multiple_of(qi * tq, tq)
            for hh in range(2):
                sdp[slot, hh, 0] = jnp.dot(ka_ref[hh], qat_ref[hh, :, pl.ds(q0, tq)], preferred_element_type=F32)
                sdp[slot, hh, 1] = jnp.dot(va_ref[hh], doat_ref[hh, :, pl.ds(q0, tq)], preferred_element_type=F32)

        def softmax_bwd(slot, qi, diagonal, valid):
            q0 = pl.multiple_of(qi * tq, tq)
            shift = kb * tk - first * tq
            col = lax.broadcasted_iota(jnp.int32, (RC, tq), 1)
            row = lax.broadcasted_iota(jnp.int32, (RC, tq), 0)
            for hh in range(2):
                lse_row = lse_ref[0, hh:hh + 1, pl.ds(q0, tq)]
                dl_row = dl_ref[0, hh:hh + 1, pl.ds(q0, tq)]
                rsum = jnp.zeros((1, tq), F32)
                for r in range(tk // RC):
                    rows = slice(r * RC, (r + 1) * RC)
                    p = jnp.exp(sdp[slot, hh, 0, rows, :] - lse_row)
                    p = jnp.where((row + (r * RC + shift) <= col) if diagonal else valid, p, 0.0)
                    ds = p * (sdp[slot, hh, 1, rows, :] - dl_row)
                    pds[slot, hh, 0, rows, :] = p.astype(BF)
                    pds[slot, hh, 1, rows, :] = ds.astype(BF)
                    rsum = rsum + jnp.sum(ds, axis=0, keepdims=True)
                    part = ds[:, 0:LANES]
                    for c in range(1, tq // LANES):
                        part = part + ds[:, c * LANES:(c + 1) * LANES]
                    df_acc[hh, rows, :] += part
                dqt[hh, HEAD:HEAD + 8, pl.ds(q0, tq)] += jnp.broadcast_to(rsum, (8, tq))

        def accumulate(slot, qi):
            q0 = pl.multiple_of(qi * tq, tq)
            for hh in range(2):
                dv_acc[hh] += jnp.dot(pds[slot, hh, 0], doa_ref[hh, pl.ds(q0, tq), :], preferred_element_type=F32)
                dk_acc[hh] += jnp.dot(pds[slot, hh, 1], qa_ref[hh, pl.ds(q0, tq), :], preferred_element_type=F32)
                dqt[hh, 0:HEAD, pl.ds(q0, tq)] += jnp.dot(kat_ref[hh, 0:HEAD, :], pds[slot, hh, 1], preferred_element_type=F32)

        products(0, first)
        products(1, jnp.minimum(first + 1, last))
        softmax_bwd(0, first, True, None)

        @pl.loop(0, (nq - first + 1) // 2)
        def _(t):
            qi = first + 2 * t
            products(0, jnp.minimum(qi + 2, last))
            softmax_bwd(1, jnp.minimum(qi + 1, last), False, qi + 1 <= last)
            accumulate(0, qi)
            products(1, jnp.minimum(qi + 3, last))
            softmax_bwd(0, jnp.minimum(qi + 2, last), False, qi + 2 <= last)
            accumulate(1, jnp.minimum(qi + 1, last))

        lo = lane < HEAD
        dk_ref[...] = jnp.where(lo, dk_acc[0], pltpu.roll(dk_acc[1], HEAD, 1))
        dv_ref[...] = jnp.where(lo, dv_acc[0], pltpu.roll(dv_acc[1], HEAD, 1)).astype(dv_ref.dtype)
        f0 = -jnp.sum(df_acc[0], axis=1, keepdims=True)
        f1 = -jnp.sum(df_acc[1], axis=1, keepdims=True)
        df_ref[0] = jnp.where(lane == 2 * j, f0, jnp.where(lane == 2 * j + 1, f1, 0.0))

        @pl.when(kb == nk - 1)
        def _():
            for t in range(nq):
                cols = slice(t * tq, (t + 1) * tq)
                dq_ref[cols, :] = jnp.concatenate([dqt[0, 0:HEAD, cols], dqt[1, 0:HEAD, cols]], axis=0).T
                rsum = jnp.concatenate([dqt[0, HEAD:HEAD + 8, cols], dqt[1, HEAD:HEAD + 8, cols],
                                        jnp.zeros((LANES - 16, tq), F32)], axis=0).T
                dr_ref[0, cols, :] = jnp.where(lane == 2 * j, rsum[:, 0:1], jnp.where(lane == 2 * j + 1, rsum[:, 8:9], 0.0))

    nat_full = pl.BlockSpec((2, T, LANES), lambda j, kb: (j, 0, 0))
    trn_full = pl.BlockSpec((2, LANES, T), lambda j, kb: (j, 0, 0))
    nat_blk = pl.BlockSpec((2, tk, LANES), lambda j, kb: (j, kb, 0))
    trn_blk = pl.BlockSpec((2, LANES, tk), lambda j, kb: (j, 0, kb))
    rows = pl.BlockSpec((1, 8, T), lambda j, kb: (j, 0, 0))
    blk = pl.BlockSpec((tk, LANES), lambda j, kb: (kb, j))
    return pl.pallas_call(
        body, name="fox_backward", grid=(4, nk),
        in_specs=[nat_full, trn_full, nat_blk, trn_blk, nat_blk, nat_full, trn_full, rows, rows],
        out_specs=[pl.BlockSpec((T, LANES), lambda j, kb: (0, j)), blk, blk, pl.BlockSpec((1, tk, LANES), lambda j, kb: (j, kb, 0)),
                   pl.BlockSpec((1, T, LANES), lambda j, kb: (j, 0, 0))],
        out_shape=[jax.ShapeDtypeStruct((T, 4 * LANES), F32), jax.ShapeDtypeStruct((T, 4 * LANES), F32),
                   jax.ShapeDtypeStruct((T, 4 * LANES), BF), jax.ShapeDtypeStruct((4, T, LANES), F32),
                   jax.ShapeDtypeStruct((4, T, LANES), F32)],
        scratch_shapes=[pltpu.VMEM((2, HEAD + 8, T), F32), pltpu.VMEM((2, tk, LANES), F32), pltpu.VMEM((2, tk, LANES), F32),
                        pltpu.VMEM((2, tk, LANES), F32), pltpu.VMEM((2, 2, 2, tk, tq), F32), pltpu.VMEM((2, 2, 2, tk, tq), BF)],
        compiler_params=_params(("arbitrary", "arbitrary")),
    )(qa, qat, ka, kat, va, doa, doat, lse, dl)


def _fgate_bwd_col(ffp, bpad, dfc4, drc4, T):
    def body(ff_ref, b_ref, dfc_ref, drc_ref, dff_ref, db_ref):
        lane = lax.broadcasted_iota(jnp.int32, (1, LANES), 1)
        tri = _tri(False)
        carry = jnp.zeros((1, LANES), F32)
        db = jnp.zeros((1, LANES), F32)
        for blk in reversed(range(T // _FB)):
            rows = slice(blk * _FB, (blk + 1) * _FB)
            dcol = dfc_ref[0, rows, :] + drc_ref[0, rows, :]
            for pair in range(1, 4):
                dcol = dcol + (dfc_ref[pair, rows, :] + drc_ref[pair, rows, :])
            dlf = jnp.dot(tri, dcol, precision=lax.Precision.HIGHEST, preferred_element_type=F32) + carry
            carry = dlf[0:1, :]
            z = ff_ref[blk * _FB:(blk + 1) * _FB, :] + b_ref[...]
            dz = jnp.where(lane < 8, dlf * jax.nn.sigmoid(-z), 0.0)
            dff_ref[blk * _FB:(blk + 1) * _FB, :] = dz.astype(dff_ref.dtype)
            db = db + jnp.sum(dz, axis=0, keepdims=True)
        db_ref[...] = db

    return pl.pallas_call(
        body, name="fgate_bwd",
        out_shape=[jax.ShapeDtypeStruct((T, LANES), BF), jax.ShapeDtypeStruct((1, LANES), F32)],
        compiler_params=pltpu.CompilerParams(vmem_limit_bytes=VMEM_LIMIT),
    )(ffp, bpad, dfc4, drc4)


MESH = pl.DeviceIdType.MESH
N_PEERS = N_DEV - 1


def _place():
    return lax.axis_index("x"), lax.axis_index("y"), lax.axis_index("c")


def _all_gather(shard):
    R, W = shard.shape

    def body(x_ref, out_ref, send_sems, recv_sems, local_sem):
        x, y, c = _place()
        me, sibling = (x, y, c), (x, y, 1 - c)
        chips = [(1 - x, y), (x, 1 - y), (1 - x, 1 - y)]

        def slot(px, py, pc):
            return out_ref.at[4 * px + 2 * py + pc]

        def copy(k, block, to, src=None):
            return pltpu.make_async_remote_copy(
                src_ref=slot(*block) if src is None else src, dst_ref=slot(*block),
                send_sem=send_sems.at[k], recv_sem=recv_sems.at[k], device_id=to, device_id_type=MESH)

        mine = pltpu.make_async_copy(x_ref, slot(*me), local_sem)
        mine.start()
        first = [copy(0, me, sibling, src=x_ref)]
        first += [copy(1 + n, me, (*chip, c), src=x_ref) for n, chip in enumerate(chips)]
        for cp in first:
            cp.start()
        passed = [copy(4 + n, (*chip, c), sibling) for n, chip in enumerate(chips)]
        for n, chip in enumerate(chips):
            copy(1 + n, (*chip, c), me).wait_recv()
            passed[n].start()
        copy(0, sibling, me).wait_recv()
        for n, chip in enumerate(chips):
            copy(4 + n, (*chip, 1 - c), me).wait_recv()
        for cp in first + passed:
            cp.wait_send()
        mine.wait()

    return pl.pallas_call(
        body, name="all_gather_weights",
        out_shape=jax.ShapeDtypeStruct((N_DEV, R, W), shard.dtype),
        in_specs=[pl.BlockSpec(memory_space=pl.ANY)], out_specs=pl.BlockSpec(memory_space=pl.ANY),
        scratch_shapes=[pltpu.SemaphoreType.DMA((N_PEERS,)), pltpu.SemaphoreType.DMA((N_PEERS,)), pltpu.SemaphoreType.DMA],
    )(shard)


def _exchange_copies(src_refs, land_refs, send_sems, recv_sems, scatter):
    x, y, c = _place()
    me = 4 * x + 2 * y + c
    copies = []
    for k, (src_ref, land_ref) in enumerate(zip(src_refs, land_refs)):
        for r in range(1, N_DEV):
            px, py, pc = x ^ (r >> 2), y ^ ((r >> 1) & 1), c ^ (r & 1)
            copies.append(pltpu.make_async_remote_copy(
                src_ref=src_ref.at[4 * px + 2 * py + pc] if scatter else src_ref, dst_ref=land_ref.at[me],
                send_sem=send_sems.at[k * N_PEERS + r - 1], recv_sem=recv_sems.at[k * N_PEERS + r - 1],
                device_id=(px, py, pc), device_id_type=MESH))
    return copies


_HBM = pl.BlockSpec(memory_space=pltpu.HBM)
_SEM = pl.BlockSpec(memory_space=pltpu.SEMAPHORE)
_EFFECT = pltpu.SideEffectType.DATAFLOW_SIDE_EFFECTING


def _exchange_start(name, srcs, lands, scatter):
    n = len(srcs)

    def body(*refs):
        send_sems, recv_sems = refs[2 * n], refs[2 * n + 1]
        for cp in _exchange_copies(refs[:n], refs[n:2 * n], send_sems, recv_sems, scatter):
            cp.start()
        token = refs[-1]
        token[...] = jnp.zeros(token.shape, F32)

    arrays = list(srcs) + list(lands)
    out = pl.pallas_call(
        body, name=name,
        out_shape=(pltpu.SemaphoreType.DMA((n * N_PEERS,)), pltpu.SemaphoreType.DMA((n * N_PEERS,)))
        + tuple(pltpu.HBM(a.shape, a.dtype) for a in arrays) + (jax.ShapeDtypeStruct((8, LANES), F32),),
        in_specs=(_HBM,) * (2 * n), out_specs=(_SEM, _SEM) + (_HBM,) * (2 * n) + (pl.BlockSpec(memory_space=pltpu.VMEM),),
        input_output_aliases={k: 2 + k for k in range(2 * n)},
        compiler_params=pltpu.CompilerParams(has_side_effects=_EFFECT),
    )(*(pltpu.with_memory_space_constraint(a, pltpu.HBM) for a in arrays))
    return out[0], out[1], out[2:2 + n], out[2 + n:2 + 2 * n], out[-1]


def _exchange_wait(name, started, after, scatter):
    send_sems, recv_sems, srcs, lands, _ = started
    n = len(srcs)

    def body(*refs):
        copies = _exchange_copies(refs[:n], refs[n:2 * n], refs[2 * n], refs[2 * n + 1], scatter)
        for cp in copies:
            cp.wait_send()
        for cp in copies:
            cp.wait_recv()

    arrays = list(srcs) + list(lands)
    out = pl.pallas_call(
        body, name=name,
        out_shape=tuple(pltpu.HBM(a.shape, a.dtype) for a in arrays),
        in_specs=(_HBM,) * (2 * n) + (_SEM, _SEM, pl.BlockSpec(memory_space=pl.ANY)), out_specs=(_HBM,) * (2 * n),
        input_output_aliases={k: k for k in range(2 * n)},
        compiler_params=pltpu.CompilerParams(has_side_effects=_EFFECT),
    )(*arrays, send_sems, recv_sems, after)
    return out[:n], out[n:]


def _adam_update(g, w, m, v):
    m2 = ADAM_B1 * m + (1.0 - ADAM_B1) * g
    v2 = ADAM_B2 * v + (1.0 - ADAM_B2) * jnp.square(g)
    m_hat = m2 / (1.0 - ADAM_B1 ** ADAM_STEP)
    v_hat = v2 / (1.0 - ADAM_B2 ** ADAM_STEP)
    return g, -ADAM_LR * (m_hat / (jnp.sqrt(v_hat) + ADAM_EPS) + ADAM_WD * w), m2, v2


def _adamw(name, slots, own, w, m, v):
    R, W = w.shape

    def body(s_ref, *refs):
        if own is not None:
            g = refs[0][...].astype(F32)
            refs = refs[1:]
        else:
            g = jnp.zeros((R, W), F32)
        for s in range(N_DEV):
            g = g + s_ref[s].astype(F32)
        w_ref, m_ref, v_ref = refs[:3]
        for o, r in zip(refs[3:], _adam_update(g, w_ref[...], m_ref[...], v_ref[...])):
            o[...] = r

    full = pl.BlockSpec((R, W), lambda i: (0, 0))
    args = [slots] + ([own] if own is not None else []) + [w, m, v]
    return pl.pallas_call(
        body, name=name, grid=(1,),
        in_specs=[pl.BlockSpec((N_DEV, R, W), lambda i: (0, 0, 0))] + [full] * (len(args) - 1),
        out_specs=[full] * 4, out_shape=[jax.ShapeDtypeStruct((R, W), F32)] * 4,
        compiler_params=_params(("arbitrary",)),
    )(*args)


def _tables(T):
    pos = jnp.arange(T, dtype=F32)
    inv_freq = 10000.0 ** (-jnp.arange(0, HEAD, 2, dtype=F32) / HEAD)
    ang = pos[:, None] * inv_freq[None, :]
    cos, sin = jnp.cos(ang), jnp.sin(ang)
    cos4 = jnp.tile(cos, (1, 4))
    sin4 = jnp.tile(jnp.concatenate([-sin, sin], axis=1), (1, 2))
    log_g = jnp.log(1.0 - 2.0 ** (-5.0 - jnp.arange(8, dtype=F32)))
    return cos4, sin4, jnp.repeat(log_g, HEAD)[None, :]


def _local_step(x, mem, target, sp, w_inT, token, fetch_rest, push, push_small):
    T = x.shape[0]
    tm = min(512, T)
    tq = min(256, T)
    tb = min(1024, T)
    cos4, sin4, lg = _tables(T)
    g_fq2 = jnp.tile(sp["g_fox_q"], (1, 2))
    g_fk2 = jnp.tile(sp["g_fox_k"], (1, 2))
    g_ret = sp["g_ret_out"].reshape(1, 8 * HEAD)
    bpad = jnp.pad(sp["b_forget"], ((0, 0), (0, LANES - 8)))
    w_secs = [w_inT[k * 512:(k + 1) * 512] for k in range(7)]
    w_ffT = jnp.pad(w_inT[3584:3592], ((0, LANES - 8), (0, 0)))
    w_mainT = w_inT[:3584]
    tie = lambda p, tok: p + tok[0:1, 0:1]
    tm2, tm4 = min(1024, T), min(2048, T)

    hn1, = _rw_fwd("rms_mix", _rms_fn, [(x, D, 0, False)], [(tie(sp["g_mix"], token), D, 0, False)], [(BF, D)], T, tm, 1)
    P, = _mm("proj_in", [[(hn1, w_mainT, "nt")]], [], _ident, T, 3584, tm4, 512, [F32])
    ffp, = _mm("proj_ff", [[(hn1, w_ffT, "nt")]], [], _ident, T, LANES, tm, LANES, [F32])
    ret, s0 = _ret_fwd(P, cos4, sin4, g_ret, lg, T, tb)
    fc, _ = _fgate_fwd(ffp, bpad, T)
    qa, qat, ka, kat, va, vat = _fox_operands(P, fc, g_fq2, g_fk2, T, tm2)
    fox, lse = _fox_forward(qat, ka, vat, T, min(512, T), tq)
    W = fetch_rest(fox)
    w_out_halves = (W["w_out"][:4 * LANES], W["w_out"][4 * LANES:])
    h1, hn2 = _mm("proj_out", [[(ret, w_out_halves[0], "nn"), (fox, w_out_halves[1], "nn")]], [x], _add_rms_epi, T, D, tm2, D,
                  [F32, BF], params=[sp["g_xattn"]])

    qx, = _mm("proj_xq", [[(hn2, W["w_xq"], "nn")]], [], _ident, T, D, tm2, D, [F32])
    memn, = _rw_fwd("rms_mem", _rms_fn, [(mem, D, 0, False)], [(sp["g_mem"], D, 0, False)], [(BF, D)], N_MEM, N_MEM, 1)
    kv, = _mm("proj_xkv", [[(memn, W["w_xkvT"], "nt")]], [], _ident, N_MEM, 2 * D, N_MEM, 512, [F32])
    xa_rows = [(qx, XHEAD, 0, True)]
    xa_params = [(sp["g_xq"], XHEAD, 0, False), (sp["g_xk"], XHEAD, 0, False), (kv, XHEAD, 0, True), (kv, XHEAD, 4, True)]
    xo, = _rw_fwd("xattn_fwd", _xattn_fn, xa_rows, xa_params, [(BF, XHEAD)], T, tm2, 4)
    h2, hn3 = _mm("proj_xo", [[(xo, W["w_xo"], "nn")]], [h1], _add_rms_epi, T, D, tm2, D, [F32, BF], params=[sp["g_ffn"]])

    gate, up, act = _mm("ffn_in", [[(hn3, W["w_gateT"], "nt")], [(hn3, W["w_upT"], "nt")]], [], _swiglu_fwd_epi,
                        T, D_FF, tm4, 256, [BF, BF, BF])
    dy, dyb, loss_part = _mm("ffn_out", [[(act, W["w_down"], "nn")]], [h2, target], _add_loss_epi, T, D, tm, D, [F32, BF], n_acc=1)

    dgate, dup = _mm("ffn_out_bwd", [[(dyb, W["w_down"], "nt")]], [gate, up], _swiglu_bwd_epi, T, D_FF, tm4, 256, [BF, BF])
    gW = {}
    gW["w_gateT"], = _mm("dw_gate", [[(dgate, hn3, "tn")]], [], _ident, D_FF, D, 256, D, [BF])
    gW["w_upT"], = _mm("dw_up", [[(dup, hn3, "tn")]], [], _ident, D_FF, D, 256, D, [BF])
    gW["w_down"], = _mm("dw_down", [[(act, dyb, "tn")]], [], _ident, D_FF, D, 256, D, [BF])
    tok = push("ffn", gW)
    gs = {}
    dh2, dh2b, gs["g_ffn"] = _mm("ffn_in_bwd", [[(dgate, W["w_gateT"], "nn"), (dup, W["w_upT"], "nn")]], [h2, dy], _rms_bwd_epi,
                                 T, D, min(256, T), D, [F32, BF], params=[tie(sp["g_ffn"], tok)], n_acc=1)

    dxo, = _mm("proj_xo_bwd", [[(dh2b, W["w_xo"], "nt")]], [], _ident, T, D, tm2, D, [BF])
    gW["w_xo"], = _mm("dw_xo", [[(xo, dh2b, "tn")]], [], _ident, D, D, 256, D, [BF])
    dqx, gs["g_xq"], gs["g_xk"], dkv_k, dkv_v = _rw_bwd(
        "xattn_bwd", _xattn_fn, xa_rows, xa_params, [(dxo, XHEAD, 0, True)], T, tm2, 4, [BF], [True, True, True, True])
    dkv = jnp.concatenate([dkv_k[:, :D], dkv_v[:, D:]], axis=1)
    gW["w_xq"], = _mm("dw_xq", [[(hn2, dqx, "tn")]], [], _ident, D, D, 256, D, [BF])
    dmemn, = _mm("proj_xkv_bwd", [[(dkv, W["w_xkvT"], "nn")]], [], _ident, N_MEM, D, N_MEM, 512, [F32])
    gW["w_xkvT"], = _mm("dw_xkv", [[(dkv, memn, "tn")]], [], _ident, 2 * D, D, 512, D, [BF])
    tok = push("xattn", gW)
    gs["g_mem"], = _rw_bwd("rms_mem_bwd", _rms_fn, [(mem, D, 0, False)], [(sp["g_mem"], D, 0, False)], [(dmemn, D, 0, False)],
                           N_MEM, N_MEM, 1, [None], [True])
    dh1, dh1b, gs["g_xattn"] = _mm("proj_xq_bwd", [[(dqx, W["w_xq"], "nt")]], [h1, dh2], _rms_bwd_epi, T, D, tm, D, [F32, BF],
                                   params=[tie(sp["g_xattn"], tok)], n_acc=1)

    dmix, = _mm("proj_out_bwd", [[(dh1b, W["w_out"], "nt")]], [], _ident, T, D, tm2, D, [F32])
    gW["w_out"] = jnp.concatenate([_mm("dw_out_%d" % k, [[(a, dh1b, "tn")]], [], _ident, 4 * LANES, D, 256, D, [BF])[0]
                                   for k, a in enumerate((ret, fox))], axis=0)
    tok = push("out", gW)
    doa, doat, dl = _fox_cotangent(dmix, fox, T, tm2)
    dqn, dkn, dfv, dfc4, drc4 = _fox_backward(qa, qat, ka, kat, va, doa, doat, lse + tok[0:1, 0:1], dl, T, tq, tq)
    dfq, dfk, gq2, gk2 = _rw_bwd("fox_prep_bwd", _fox_prep_fn, [(P, LANES, 16, True), (P, LANES, 20, True)],
                                 [(g_fq2, LANES, 0, False), (g_fk2, LANES, 0, False)],
                                 [(dqn, LANES, 0, True), (dkn, LANES, 0, True)], T, tm2, 4, [BF, BF], [True, True])
    gs["g_fox_q"] = gq2[:, :HEAD] + gq2[:, HEAD:]
    gs["g_fox_k"] = gk2[:, :HEAD] + gk2[:, HEAD:]
    dff, dbp = _fgate_bwd_col(ffp, bpad, dfc4, drc4, T)
    gs["b_forget"] = dbp[:, :8]
    drq, drk, drv, drg, dg_ret = _ret_bwd(P, cos4, sin4, g_ret, lg, s0, dmix, T, tb)
    gs["g_ret_out"] = dg_ret
    dsecs = [drq, drk, drv, drg, dfq, dfk, dfv]
    g_secs = [_mm("dw_in_%d" % k, [[(d, hn1, "tn")]], [], _ident, 512, D, 256, D, [BF])[0] for k, d in enumerate(dsecs)]
    g_ff, = _mm("dw_in_ff", [[(dff, hn1, "tn")]], [], _ident, LANES, D, LANES, D, [BF])
    gW["w_inT"] = jnp.concatenate(g_secs + [g_ff[:8]], axis=0)
    tok = push("in", gW)
    grad_x, _, gs["g_mix"] = _mm("proj_in_bwd", [[(d, w, "nn") for d, w in zip(dsecs, w_secs)] + [(dff, w_ffT, "nn")]], [x, dh1],
                                 _rms_bwd_epi, T, D, tm, D, [F32, BF], params=[tie(sp["g_mix"], tok)], n_acc=1)
    return grad_x, push_small(gs, loss_part)


_CANON = {"w_in": "w_inT", "w_xkv": "w_xkvT", "w_gate": "w_gateT", "w_up": "w_upT"}
_SMALL = (("g_mix", 0, 0, 1024), ("g_xattn", 1, 0, 1024), ("g_mem", 2, 0, 1024), ("g_ffn", 3, 0, 1024),
          ("g_ret_out", 4, 0, 512), ("g_xq", 4, 512, 256), ("g_xk", 4, 768, 256),
          ("g_fox_q", 5, 0, 64), ("g_fox_k", 5, 64, 64), ("b_forget", 5, 128, 8))
_LOSS_AT = (5, 256)


def _pack_small(tree):
    buf = jnp.zeros((SMALL_ROWS, D), F32)
    for name, r, c, n in _SMALL:
        buf = lax.dynamic_update_slice(buf, tree[name].reshape(1, n).astype(F32), (r, c))
    return buf


def _unpack_small(buf, like):
    return {name: buf[r:r + 1, c:c + n].reshape(like[name].shape) for name, r, c, n in _SMALL}


def _canonical(tree, name):
    a = tree[name][0]
    return a.T if W_SHARD[name][1] else a


def _from_canonical(a, name):
    return (a.T if W_SHARD[name][1] else a)[None]


def kernel(x, mem, g_mix, w_in, b_forget, g_ret_out, g_fox_q, g_fox_k, w_out, g_xattn, w_xq, w_xkv, g_mem, g_xq, g_xk, w_xo, g_ffn, w_gate, w_up, w_down, loss_target, m_g_mix, m_w_in, m_b_forget, m_g_ret_out, m_g_fox_q, m_g_fox_k, m_w_out, m_g_xattn, m_w_xq, m_w_xkv, m_g_mem, m_g_xq, m_g_xk, m_w_xo, m_g_ffn, m_w_gate, m_w_up, m_w_down, v_g_mix, v_w_in, v_b_forget, v_g_ret_out, v_g_fox_q, v_g_fox_k, v_w_out, v_g_xattn, v_w_xq, v_w_xkv, v_g_mem, v_g_xq, v_g_xk, v_w_xo, v_g_ffn, v_w_gate, v_w_up, v_w_down):
    names = ("g_mix", "w_in", "b_forget", "g_ret_out", "g_fox_q", "g_fox_k", "w_out", "g_xattn", "w_xq", "w_xkv", "g_mem",
             "g_xq", "g_xk", "w_xo", "g_ffn", "w_gate", "w_up", "w_down")
    w = dict(zip(names, (g_mix, w_in, b_forget, g_ret_out, g_fox_q, g_fox_k, w_out, g_xattn, w_xq, w_xkv, g_mem, g_xq, g_xk,
                         w_xo, g_ffn, w_gate, w_up, w_down)))
    m = dict(zip(names, (m_g_mix, m_w_in, m_b_forget, m_g_ret_out, m_g_fox_q, m_g_fox_k, m_w_out, m_g_xattn, m_w_xq, m_w_xkv,
                         m_g_mem, m_g_xq, m_g_xk, m_w_xo, m_g_ffn, m_w_gate, m_w_up, m_w_down)))
    v = dict(zip(names, (v_g_mix, v_w_in, v_b_forget, v_g_ret_out, v_g_fox_q, v_g_fox_k, v_w_out, v_g_xattn, v_w_xq, v_w_xkv,
                         v_g_mem, v_g_xq, v_g_xk, v_w_xo, v_g_ffn, v_w_gate, v_w_up, v_w_down)))
    small_names = [s[0] for s in _SMALL]
    me = 4 * lax.axis_index("x") + 2 * lax.axis_index("y") + lax.axis_index("c")

    first = _all_gather(_canonical(w, "w_in").astype(BF))
    first, rest = lax.optimization_barrier((first, [_canonical(w, n).astype(BF) for n in GATHER_REST]))
    rest_started = _exchange_start("gather_rest_start", rest, [jnp.broadcast_to(a[None], (N_DEV,) + a.shape) for a in rest],
                                   scatter=False)

    def fetch_rest(after):
        lands = _exchange_wait("gather_rest_wait", rest_started, after, scatter=False)[1]
        return {_CANON.get(n, n): a.reshape(N_DEV * a.shape[1], D) for n, a in zip(GATHER_REST, lands)}

    pushed = {}

    def push(group, grads):
        srcs = [grads[_CANON.get(n, n)].reshape(N_DEV, W_SHARD[n][0], D) for n in SCATTER_GROUPS[group]]
        pushed[group] = _exchange_start("scatter_%s_start" % group, srcs, [jnp.zeros(a.shape, BF) for a in srcs], scatter=True)
        return pushed[group][4]

    def push_small(gs, loss_part):
        small = lax.dynamic_update_slice(_pack_small(gs), loss_part[:, :1], _LOSS_AT)
        pushed["small"] = _exchange_start("gather_small_start", [small], [jnp.broadcast_to(small[None], (N_DEV,) + small.shape)],
                                          scatter=False)
        return pushed["small"][4]

    sp = {n: w[n].reshape(1, -1) for n in small_names}
    grad_x, done = _local_step(x[0], mem[0], loss_target[0], sp, first.reshape(N_DEV * W_SHARD["w_in"][0], D),
                               rest_started[4], fetch_rest, push, push_small)

    results, after = {}, done
    for group in ("ffn", "xattn", "out", "small", "in"):
        if group == "small":
            recv_small = _exchange_wait("gather_small_wait", pushed["small"], after, scatter=False)[1][0]
            g_sm, d_sm, m_sm, v_sm = _adamw("adamw_small", recv_small, None, _pack_small(w), _pack_small(m), _pack_small(v))
            after = g_sm
            continue
        sents, recvs = _exchange_wait("scatter_%s_wait" % group, pushed[group], after, scatter=True)
        for name, sent, recv in zip(SCATTER_GROUPS[group], sents, recvs):
            own = lax.dynamic_index_in_dim(sent, me, axis=0, keepdims=False)
            res = _adamw("adamw_" + name, recv, own, *(_canonical(t, name) for t in (w, m, v)))
            results[name] = [_from_canonical(r, name) for r in res]
        after = results[SCATTER_GROUPS[group][-1]][0]
    loss = g_sm[_LOSS_AT[0], _LOSS_AT[1]]

    outs = []
    for k, sm in enumerate((g_sm, d_sm, m_sm, v_sm)):
        tree = _unpack_small(sm, w)
        tree.update({name: res[k] for name, res in results.items()})
        outs += [tree[n] for n in names]
    return (loss, grad_x[None], *outs)
```

```python
import jax
import jax.numpy as jnp
from jax import lax
from jax.experimental import pallas as pl
from jax.experimental.pallas import tpu as pltpu

F32 = jnp.float32
BF = jnp.bfloat16

D = 1024
HEAD = 64
CHUNK = 64
N_MEM = 256
XHEAD = 256
D_FF = 2816
EPS = 1e-6
NEG = -1e30
LANES = 128
N_DEV = 8
V7X_VMEM_BYTES = 64 * 1024 * 1024
VMEM_LIMIT = V7X_VMEM_BYTES - 8 * 1024 * 1024

ADAM_LR, ADAM_B1, ADAM_B2, ADAM_EPS, ADAM_WD, ADAM_STEP = 0.001, 0.9, 0.999, 1e-08, 0.01, 10

W_SHARD = {"w_in": (449, True), "w_out": (128, False), "w_xq": (128, False), "w_xkv": (256, True),
           "w_xo": (128, False), "w_gate": (352, True), "w_up": (352, True), "w_down": (352, False)}
GATHER_REST = ("w_out", "w_xq", "w_xkv", "w_xo", "w_gate", "w_up", "w_down")
SCATTER_GROUPS = {"ffn": ("w_gate", "w_up", "w_down"), "xattn": ("w_xq", "w_xo", "w_xkv"), "out": ("w_out",), "in": ("w_in",)}
SMALL_ROWS = 8

NT = (((1,), (1,)), ((), ()))
NN = (((1,), (0,)), ((), ()))
TN = (((0,), (0,)), ((), ()))
_DIMS = {"nn": NN, "nt": NT, "tn": TN}


def _params(sem):
    return pltpu.CompilerParams(dimension_semantics=sem, vmem_limit_bytes=VMEM_LIMIT)


def _mm(name, products, extras, epilogue, M, N, tm, tn, out_dtypes, params=(), n_acc=0):
    assert n_acc == 0 or tn == N
    flat = [t for p in products for t in p]
    counts = [len(p) for p in products]
    in_specs, args = [], []
    for a, b, form in flat:
        if form == "tn":
            in_specs.append(pl.BlockSpec((a.shape[0], tm), lambda i, j: (0, i)))
        else:
            in_specs.append(pl.BlockSpec((tm, a.shape[1]), lambda i, j: (i, 0)))
        if form == "nt":
            in_specs.append(pl.BlockSpec((tn, b.shape[1]), lambda i, j: (j, 0)))
        else:
            in_specs.append(pl.BlockSpec((b.shape[0], tn), lambda i, j: (0, j)))
        args += [a, b]
    for e in extras:
        in_specs.append(pl.BlockSpec((tm, tn), lambda i, j: (i, j)))
        args.append(e)
    for p in params:
        in_specs.append(pl.BlockSpec((1, tn), lambda i, j: (0, j)))
        args.append(p)
    n_in = len(args)
    n_out = len(out_dtypes)

    def body(*refs):
        ins, outs = refs[:n_in], refs[n_in:]
        prods, p = [], 0
        for c in counts:
            acc = None
            for _ in range(c):
                a = ins[2 * p][...].astype(BF)
                b = ins[2 * p + 1][...].astype(BF)
                d = lax.dot_general(a, b, _DIMS[flat[p][2]], preferred_element_type=F32)
                acc = d if acc is None else acc + d
                p += 1
            prods.append(acc)
        ex = [r[...].astype(F32) for r in ins[2 * len(flat):]]
        res = epilogue(*prods, *ex)
        for o, r in zip(outs[:n_out], res[:n_out]):
            o[...] = r.astype(o.dtype)
        for o, r in zip(outs[n_out:], res[n_out:]):
            @pl.when(pl.program_id(0) == 0)
            def _(o=o):
                o[...] = jnp.zeros(o.shape, F32)
            o[...] += r

    return pl.pallas_call(
        body, name=name, grid=(M // tm, N // tn), in_specs=in_specs,
        out_specs=[pl.BlockSpec((tm, tn), lambda i, j: (i, j)) for _ in out_dtypes]
        + [pl.BlockSpec((1, tn), lambda i, j: (0, j)) for _ in range(n_acc)],
        out_shape=[jax.ShapeDtypeStruct((M, N), dt) for dt in out_dtypes] + [jax.ShapeDtypeStruct((1, N), F32)] * n_acc,
        compiler_params=_params(("arbitrary", "arbitrary")),
    )(*args)


def _ident(x):
    return (x,)


def _spec(rows, w, off, per_j):
    if per_j:
        return pl.BlockSpec((rows, w), lambda j, i: (i, off + j))
    return pl.BlockSpec((rows, w), lambda j, i: (i, off))


def _pspec(rows, w, off, per_j):
    if per_j:
        return pl.BlockSpec((rows, w), lambda j, i: (0, off + j))
    return pl.BlockSpec((rows, w), lambda j, i: (0, off))


def _rw_fwd(name, fn, rows, params, outs, T, tm, nj, n_acc=0):
    in_specs = [_spec(tm, w, off, pj) for _, w, off, pj in rows] + [_pspec(a.shape[0], w, off, pj) for a, w, off, pj in params]
    args = [r[0] for r in rows] + [p[0] for p in params]
    n_in, n_out = len(args), len(outs)
    out_specs = [pl.BlockSpec((tm, w), lambda j, i: (i, j)) for _, w in outs]
    out_shape = [jax.ShapeDtypeStruct((T, nj * w), dt) for dt, w in outs]
    out_specs += [pl.BlockSpec((1, LANES), lambda j, i: (0, 0)) for _ in range(n_acc)]
    out_shape += [jax.ShapeDtypeStruct((1, LANES), F32) for _ in range(n_acc)]

    def body(*refs):
        vals = [r[...].astype(F32) for r in refs[:n_in]]
        res = fn(*vals)
        orefs = refs[n_in:]
        for k in range(n_out):
            orefs[k][...] = res[k].astype(orefs[k].dtype)
        first = (pl.program_id(0) == 0) & (pl.program_id(1) == 0)
        for k in range(n_acc):
            @pl.when(first)
            def _(k=k):
                orefs[n_out + k][...] = jnp.zeros((1, LANES), F32)
            orefs[n_out + k][...] += res[n_out + k]

    return pl.pallas_call(
        body, name=name, grid=(nj, T // tm), in_specs=in_specs, out_specs=out_specs, out_shape=out_shape,
        compiler_params=_params(("arbitrary", "arbitrary")),
    )(*args)


def _rw_bwd(name, fn, rows, params, cots, T, tm, nj, row_grads, param_grads, resid=None):
    in_specs = ([_spec(tm, w, off, pj) for _, w, off, pj in rows] + [_pspec(a.shape[0], w, off, pj) for a, w, off, pj in params]
                + [_spec(tm, w, off, pj) for _, w, off, pj in cots])
    args = [r[0] for r in rows] + [p[0] for p in params] + [c[0] for c in cots]
    if resid is not None:
        in_specs.append(_spec(tm, rows[0][1], rows[0][2], rows[0][3]))
        args.append(resid)
    nr, npar, nc = len(rows), len(params), len(cots)
    out_specs, out_shape, kinds = [], [], []
    for k, dts in enumerate(row_grads):
        for dt in (dts if isinstance(dts, (list, tuple)) else [dts]):
            if dt is not None:
                w = rows[k][1]
                out_specs.append(pl.BlockSpec((tm, w), lambda j, i: (i, j)))
                out_shape.append(jax.ShapeDtypeStruct((T, nj * w), dt))
                kinds.append(("row", k))
    for k, need in enumerate(param_grads):
        if need:
            a, w, off, pj = params[k]
            out_specs.append(_pspec(a.shape[0], w, off, pj))
            out_shape.append(jax.ShapeDtypeStruct(a.shape, F32))
            kinds.append(("par", k))

    def body(*refs):
        vals = [r[...].astype(F32) for r in refs[:nr + npar]]
        ct = tuple(r[...].astype(F32) for r in refs[nr + npar:nr + npar + nc])
        _, vjp = jax.vjp(lambda *a: tuple(fn(*a)), *vals)
        grads = list(vjp(ct))
        n_in = nr + npar + nc + (resid is not None)
        if resid is not None:
            grads[0] = grads[0] + refs[n_in - 1][...].astype(F32)
        orefs = refs[n_in:]
        j, i = pl.program_id(0), pl.program_id(1)
        for o, (kind, k) in zip(orefs, kinds):
            if kind == "row":
                o[...] = grads[k].astype(o.dtype)
            else:
                first = (i == 0) if params[k][3] else ((i == 0) & (j == 0))

                @pl.when(first)
                def _(o=o):
                    o[...] = jnp.zeros(o.shape, F32)
                o[...] += grads[nr + k]

    return pl.pallas_call(
        body, name=name, grid=(nj, T // tm), in_specs=in_specs, out_specs=out_specs, out_shape=out_shape,
        compiler_params=_params(("arbitrary", "arbitrary")),
    )(*args)


def _rms(x, g):
    return x * lax.rsqrt(jnp.mean(x * x, axis=-1, keepdims=True) + EPS) * g


def _rms_fn(x, g):
    return (_rms(x, g),)


def _lo_mask():
    return lax.broadcasted_iota(jnp.int32, (1, LANES), 1) < HEAD


def _gmean(x, lo):
    s0 = jnp.sum(jnp.where(lo, x, 0.0), axis=-1, keepdims=True)
    s1 = jnp.sum(jnp.where(lo, 0.0, x), axis=-1, keepdims=True)
    return jnp.where(lo, s0, s1) * (1.0 / HEAD)


def _fox_prep_fn(fq, fk, gq, gk):
    lo = _lo_mask()
    qn = fq * lax.rsqrt(_gmean(fq * fq, lo) + EPS) * gq * (HEAD ** -0.5)
    kn = fk * lax.rsqrt(_gmean(fk * fk, lo) + EPS) * gk
    return qn, kn


@jax.custom_vjp
def _swap_halves(x):
    bit = (lax.broadcasted_iota(jnp.int32, (1, LANES), 1) & (HEAD // 2)) == 0
    return jnp.where(bit, pltpu.roll(x, LANES - HEAD // 2, 1), pltpu.roll(x, HEAD // 2, 1))


_swap_halves.defvjp(lambda x: (_swap_halves(x), None), lambda _, g: (_swap_halves(g),))


def _ret_fn(rq, rk, rv, rg, cos, sin, s_in, g, lg):
    tb = rq.shape[0]
    nc = tb // CHUNK
    lo = _lo_mask()
    row = lax.broadcasted_iota(jnp.int32, (LANES, 1), 0) < HEAD
    same_head = row == lo
    q = (rq * cos + _swap_halves(rq) * sin) * (HEAD ** -0.5)
    k = rk * cos + _swap_halves(rk) * sin
    q3, k3, v3 = q.reshape(nc, CHUNK, LANES), k.reshape(nc, CHUNK, LANES), rv.reshape(nc, CHUNK, LANES)
    pos = lax.broadcasted_iota(jnp.int32, (CHUNK, 1), 0).astype(F32)
    q_decay = jnp.exp(lg * (pos + 1.0))
    k_decay = jnp.exp(lg * (CHUNK - 1.0 - pos))
    chunk_decay = jnp.exp(lg * float(CHUNK))
    dist = jnp.abs(lax.broadcasted_iota(jnp.int32, (CHUNK, CHUNK), 0) - lax.broadcasted_iota(jnp.int32, (CHUNK, CHUNK), 1)).astype(F32)
    v3b = v3.astype(BF)
    intra = []
    for hh in range(2):
        hm = lo if hh == 0 else ~lo
        lg_h = lg[:, hh * HEAD:hh * HEAD + 1]
        qm = jnp.where(hm, q3, 0.0).astype(BF)
        sc = jnp.einsum("nid,njd->nij", qm, k3.astype(BF), preferred_element_type=F32) * jnp.exp(lg_h * dist)[None]
        intra.append(jnp.einsum("nij,nje->nie", sc.astype(BF), v3b, preferred_element_type=F32))
    o = jnp.where(lo, intra[0], intra[1])
    kv = jnp.einsum("njd,nje->nde", (k3 * k_decay[None]).astype(BF), v3b, preferred_element_type=F32)
    kv = jnp.where(same_head[None], kv, 0.0)
    state, states = s_in, []
    for n in range(nc):
        states.append(state)
        state = state * chunk_decay + kv[n]
    s_prev = jnp.stack(states, axis=0)
    o = o + jnp.einsum("nid,nde->nie", (q3 * q_decay[None]).astype(BF), s_prev.astype(BF), preferred_element_type=F32)
    o = o.reshape(tb, LANES)
    mu = _gmean(o, lo)
    oc = o - mu
    y = oc * lax.rsqrt(_gmean(oc * oc, lo) + EPS) * g
    return jax.nn.silu(rg) * y, state


def _xattn_fn(qx, gq, gk, kk, vv):
    q = _rms(qx, gq)
    k = _rms(kk, gk)
    logits = lax.dot_general(q.astype(BF), k.astype(BF), NT, preferred_element_type=F32) * (XHEAD ** -0.5)
    p = jax.nn.softmax(logits, axis=-1)
    return (jnp.dot(p.astype(BF), vv.astype(BF), preferred_element_type=F32),)


def _swiglu_fwd_epi(g, u):
    return g, u, jax.nn.silu(g) * u


def _swiglu_bwd_epi(dact, g, u):
    _, vjp = jax.vjp(lambda a, b: jax.nn.silu(a) * b, g, u)
    return vjp(dact)


def _add_rms_epi(acc, resid, g):
    h = acc + resid
    return h, _rms(h, g)


def _add_loss_epi(acc, resid, target):
    err = (acc + resid) - target
    dy = err * (1.0 / D)
    part = jnp.sum(jnp.sum(err * err, axis=0, keepdims=True), axis=1, keepdims=True) * (0.5 / D)
    return dy, dy, jnp.broadcast_to(part, (1, err.shape[1]))


def _rms_bwd_epi(dhn, h, skip, g):
    _, vjp = jax.vjp(_rms, h, g)
    dh, dg = vjp(dhn)
    dh = dh + skip
    return dh, dh, dg


def _ret_fwd(P, cos, sin, g_ret, lg, T, tb):
    nb = T // tb

    def body(rq, rk, rv, rg, c, s, g, l, o_ref, s0_ref, state):
        @pl.when(pl.program_id(1) == 0)
        def _():
            state[...] = jnp.zeros(state.shape, F32)
        s0_ref[0, 0] = state[...]
        out, s_new = _ret_fn(rq[...], rk[...], rv[...], rg[...], c[...], s[...], state[...], g[...], l[...])
        o_ref[...] = out.astype(o_ref.dtype)
        state[...] = s_new

    sec = lambda off: pl.BlockSpec((tb, LANES), lambda j, i: (i, off + j))
    tab = pl.BlockSpec((tb, LANES), lambda j, i: (i, 0))
    par = pl.BlockSpec((1, LANES), lambda j, i: (0, j))
    return pl.pallas_call(
        body, name="ret_fwd", grid=(4, nb),
        in_specs=[sec(0), sec(4), sec(8), sec(12), tab, tab, par, par],
        out_specs=[pl.BlockSpec((tb, LANES), lambda j, i: (i, j)), pl.BlockSpec((1, 1, LANES, LANES), lambda j, i: (j, i, 0, 0))],
        out_shape=[jax.ShapeDtypeStruct((T, 4 * LANES), BF), jax.ShapeDtypeStruct((4, nb, LANES, LANES), F32)],
        scratch_shapes=[pltpu.VMEM((LANES, LANES), F32)],
        compiler_params=_params(("arbitrary", "arbitrary")),
    )(P, P, P, P, cos, sin, g_ret, lg)


def _ret_bwd(P, cos, sin, g_ret, lg, s0, dmix, T, tb):
    nb = T // tb

    def body(rq, rk, rv, rg, c, s, g, l, s0_ref, do, drq, drk, drv, drg, dg, dstate):
        i = pl.program_id(1)

        @pl.when(i == 0)
        def _():
            dstate[...] = jnp.zeros(dstate.shape, F32)
            dg[...] = jnp.zeros(dg.shape, F32)

        cc, ss, ll = c[...], s[...], l[...]
        _, vjp = jax.vjp(lambda a, b, v, gate, st, gg: _ret_fn(a, b, v, gate, cc, ss, st, gg, ll),
                         rq[...], rk[...], rv[...], rg[...], s0_ref[0, 0], g[...])
        ga, gb, gv, ggate, gst, ggain = vjp((do[...], dstate[...]))
        drq[...] = ga.astype(drq.dtype)
        drk[...] = gb.astype(drk.dtype)
        drv[...] = gv.astype(drv.dtype)
        drg[...] = ggate.astype(drg.dtype)
        dstate[...] = gst
        dg[...] += ggain

    rev = lambda i: nb - 1 - i
    sec = lambda off: pl.BlockSpec((tb, LANES), lambda j, i: (rev(i), off + j))
    tab = pl.BlockSpec((tb, LANES), lambda j, i: (rev(i), 0))
    par = pl.BlockSpec((1, LANES), lambda j, i: (0, j))
    outb = pl.BlockSpec((tb, LANES), lambda j, i: (rev(i), j))
    return pl.pallas_call(
        body, name="ret_bwd", grid=(4, nb),
        in_specs=[sec(0), sec(4), sec(8), sec(12), tab, tab, par, par,
                  pl.BlockSpec((1, 1, LANES, LANES), lambda j, i: (j, rev(i), 0, 0)), outb],
        out_specs=[outb, outb, outb, outb, par],
        out_shape=[jax.ShapeDtypeStruct((T, 4 * LANES), BF)] * 4 + [jax.ShapeDtypeStruct((1, 4 * LANES), F32)],
        scratch_shapes=[pltpu.VMEM((LANES, LANES), F32)],
        compiler_params=_params(("arbitrary", "arbitrary")),
    )(P, P, P, P, cos, sin, g_ret, lg, s0, dmix)


_FB = 128


def _tri(lower):
    r = lax.broadcasted_iota(jnp.int32, (_FB, _FB), 0)
    c = lax.broadcasted_iota(jnp.int32, (_FB, _FB), 1)
    return ((r >= c) if lower else (r <= c)).astype(F32)


def _fgate_fwd(ffp, bpad, T):
    def body(ff_ref, b_ref, fc_ref, fr_ref):
        lane = lax.broadcasted_iota(jnp.int32, (1, LANES), 1)
        tri = _tri(True)
        carry = jnp.zeros((1, LANES), F32)
        for blk in range(T // _FB):
            z = ff_ref[blk * _FB:(blk + 1) * _FB, :] + b_ref[...]
            lf = jnp.where(lane < 8, jax.nn.log_sigmoid(z), 0.0)
            f = jnp.dot(tri, lf, precision=lax.Precision.HIGHEST, preferred_element_type=F32) + carry
            carry = f[_FB - 1:_FB, :]
            fc_ref[blk * _FB:(blk + 1) * _FB, :] = f
            fr_ref[:, blk * _FB:(blk + 1) * _FB] = f.T[:8, :]

    return pl.pallas_call(
        body, name="fgate_fwd",
        out_shape=[jax.ShapeDtypeStruct((T, LANES), F32), jax.ShapeDtypeStruct((8, T), F32)],
        compiler_params=pltpu.CompilerParams(vmem_limit_bytes=VMEM_LIMIT),
    )(ffp, bpad)


_BIAS_LANE = HEAD


def _head_bias_col(fc, head):
    lane = lax.broadcasted_iota(jnp.int32, (1, LANES), 1)
    return jnp.sum(jnp.where(lane == head, fc, 0.0), axis=-1, keepdims=True)


def _split3(f):
    hi = f.astype(BF).astype(F32)
    mid = (f - hi).astype(BF).astype(F32)
    lo = ((f - hi) - mid).astype(BF).astype(F32)
    return hi, mid, lo


def _fox_operands(P, fc, g_fq2, g_fk2, T, tm):
    def body(fq_ref, fk_ref, fv_ref, fc_ref, gq_ref, gk_ref, qa_ref, qat_ref, ka_ref, kat_ref, va_ref, vat_ref):
        j = pl.program_id(0)
        lane = lax.broadcasted_iota(jnp.int32, (1, LANES), 1)
        qn, kn = _fox_prep_fn(fq_ref[...], fk_ref[...], gq_ref[...], gk_ref[...])
        v = fv_ref[...]
        fcb = fc_ref[...]
        b = _BIAS_LANE
        for hh in range(2):
            hi, mid, lo = _split3(_head_bias_col(fcb, 2 * j + hh))
            take = (lambda a: a) if hh == 0 else (lambda a: pltpu.roll(a, HEAD, 1))
            qa = jnp.where(lane < HEAD, take(qn), jnp.where(lane == b, hi, jnp.where(lane == b + 1, mid, jnp.where(
                lane == b + 2, lo, jnp.where(lane < b + 6, 1.0, 0.0)))))
            ka = jnp.where(lane < HEAD, take(kn), jnp.where(lane < b + 3, 1.0, jnp.where(lane == b + 3, -hi, jnp.where(
                lane == b + 4, -mid, jnp.where(lane == b + 5, -lo, 0.0)))))
            va = jnp.where(lane < HEAD, take(v), 0.0)
            for val, ref, tref in ((qa, qa_ref, qat_ref), (ka, ka_ref, kat_ref), (va, va_ref, vat_ref)):
                ref[hh] = val.astype(BF)
                tref[hh] = val.T.astype(BF)

    sec = lambda off: pl.BlockSpec((tm, LANES), lambda j, i: (i, off + j))
    par = pl.BlockSpec((1, LANES), lambda j, i: (0, 0))
    nat = pl.BlockSpec((2, tm, LANES), lambda j, i: (j, i, 0))
    trn = pl.BlockSpec((2, LANES, tm), lambda j, i: (j, 0, i))
    return pl.pallas_call(
        body, name="fox_operands", grid=(4, T // tm),
        in_specs=[sec(16), sec(20), sec(24), pl.BlockSpec((tm, LANES), lambda j, i: (i, 0)), par, par],
        out_specs=[nat, trn, nat, trn, nat, trn],
        out_shape=[jax.ShapeDtypeStruct((8, T, LANES), BF), jax.ShapeDtypeStruct((8, LANES, T), BF)] * 3,
        compiler_params=_params(("parallel", "arbitrary")),
    )(P, P, P, fc, g_fq2, g_fk2)


def _fox_forward(qat, ka, vat, T, tq, tk):
    nq, per = T // tq, tq // tk
    assert per == 2
    RC = 64

    def body(qat_ref, ka_ref, vat_ref, o_ref, lse_ref, s_scr, p_scr, a_scr, m_scr, l_scr, acc_scr):
        i = pl.program_id(1)
        sub = lax.broadcasted_iota(jnp.int32, (8, 1), 0)
        row = lax.broadcasted_iota(jnp.int32, (RC, tq), 0)
        col = lax.broadcasted_iota(jnp.int32, (RC, tq), 1)
        m_scr[...] = jnp.full(m_scr.shape, NEG, F32)
        l_scr[...] = jnp.zeros(l_scr.shape, F32)
        acc_scr[...] = jnp.zeros(acc_scr.shape, F32)

        def scores(slot, kb):
            k0 = pl.multiple_of(kb * tk, tk)
            for hh in range(2):
                s_scr[slot, hh] = jnp.dot(ka_ref[hh, pl.ds(k0, tk), :], qat_ref[hh], preferred_element_type=F32)

        def softmax(slot, kb, diagonal):
            shift = kb * tk - i * tq
            for hh in range(2):
                def masked(r):
                    tile = s_scr[slot, hh, r * RC:(r + 1) * RC, :]
                    return jnp.where(row + (r * RC + shift) <= col, tile, NEG) if diagonal else tile

                mx = jnp.max(masked(0), axis=0, keepdims=True)
                for r in range(1, tk // RC):
                    mx = jnp.maximum(mx, jnp.max(masked(r), axis=0, keepdims=True))
                m_old = m_scr[hh, 0:1, :]
                m2 = jnp.maximum(m_old, mx)
                a = jnp.exp(m_old - m2)
                lsum = jnp.zeros((1, tq), F32)
                for r in range(tk // RC):
                    p = jnp.exp(masked(r) - m2)
                    p_scr[slot, hh, r * RC:(r + 1) * RC, :] = p.astype(BF)
                    lsum = lsum + jnp.sum(p, axis=0, keepdims=True)
                m_scr[hh] = jnp.broadcast_to(m2, (8, tq))
                l_scr[hh] = jnp.broadcast_to(a * l_scr[hh, 0:1, :] + lsum, (8, tq))
                a_scr[slot, hh] = jnp.broadcast_to(a, (8, tq))

        def values(slot, kb):
            k0 = pl.multiple_of(kb * tk, tk)
            for hh in range(2):
                pv = jnp.dot(vat_ref[hh, 0:HEAD, pl.ds(k0, tk)], p_scr[slot, hh], preferred_element_type=F32)
                acc_scr[hh] = a_scr[slot, hh, 0:1, :] * acc_scr[hh] + pv

        def pair(kb, diag_first, diag_second, more):
            if more:
                scores(0, kb + 2)
            softmax(1, kb + 1, diag_first)
            values(0, kb)
            if more:
                scores(1, kb + 3)
                softmax(0, kb + 2, diag_second)
            values(1, kb + 1)

        scores(0, 0)
        scores(1, 1)
        softmax(0, 0, True)

        @pl.loop(0, jnp.maximum(i - 1, 0))
        def _(t):
            pair(2 * t, False, False, True)

        @pl.when(i >= 1)
        def _():
            pair(2 * (i - 1), False, True, True)

        pair(2 * i, True, False, False)

        o_ref[...] = jnp.concatenate([acc_scr[hh] / l_scr[hh, 0:1, :] for hh in range(2)], axis=0).T
        lses = [m_scr[hh, 0:1, :] + jnp.log(l_scr[hh, 0:1, :]) for hh in range(2)]
        lse_ref[0] = jnp.where(sub == 0, lses[0], jnp.where(sub == 1, lses[1], 0.0))

    return pl.pallas_call(
        body, name="fox_forward", grid=(4, nq),
        in_specs=[pl.BlockSpec((2, LANES, tq), lambda j, i: (j, 0, i)), pl.BlockSpec((2, T, LANES), lambda j, i: (j, 0, 0)),
                  pl.BlockSpec((2, LANES, T), lambda j, i: (j, 0, 0))],
        out_specs=[pl.BlockSpec((tq, LANES), lambda j, i: (i, j)), pl.BlockSpec((1, 8, tq), lambda j, i: (j, 0, i))],
        out_shape=[jax.ShapeDtypeStruct((T, 4 * LANES), F32), jax.ShapeDtypeStruct((4, 8, T), F32)],
        scratch_shapes=[pltpu.VMEM((2, 2, tk, tq), F32), pltpu.VMEM((2, 2, tk, tq), BF), pltpu.VMEM((2, 2, 8, tq), F32),
                        pltpu.VMEM((2, 8, tq), F32), pltpu.VMEM((2, 8, tq), F32), pltpu.VMEM((2, HEAD, tq), F32)],
        compiler_params=_params(("parallel", "arbitrary")),
    )(qat, ka, vat)


def _fox_cotangent(dmix, fox, T, tm):
    def body(do_ref, o_ref, doa_ref, doat_ref, dl_ref):
        lane = lax.broadcasted_iota(jnp.int32, (1, LANES), 1)
        sub = lax.broadcasted_iota(jnp.int32, (8, 1), 0)
        dob = do_ref[...].astype(BF).astype(F32)
        prod_t = (dob * o_ref[...]).T
        d0 = jnp.sum(prod_t[:HEAD], axis=0, keepdims=True)
        d1 = jnp.sum(prod_t[HEAD:], axis=0, keepdims=True)
        dl_ref[0] = jnp.where(sub == 0, d0, jnp.where(sub == 1, d1, 0.0))
        for hh in range(2):
            val = jnp.where(lane < HEAD, dob if hh == 0 else pltpu.roll(dob, HEAD, 1), 0.0)
            doa_ref[hh] = val.astype(BF)
            doat_ref[hh] = val.T.astype(BF)

    return pl.pallas_call(
        body, name="fox_cotangent", grid=(4, T // tm),
        in_specs=[pl.BlockSpec((tm, LANES), lambda j, i: (i, 4 + j)), pl.BlockSpec((tm, LANES), lambda j, i: (i, j))],
        out_specs=[pl.BlockSpec((2, tm, LANES), lambda j, i: (j, i, 0)), pl.BlockSpec((2, LANES, tm), lambda j, i: (j, 0, i)),
                   pl.BlockSpec((1, 8, tm), lambda j, i: (j, 0, i))],
        out_shape=[jax.ShapeDtypeStruct((8, T, LANES), BF), jax.ShapeDtypeStruct((8, LANES, T), BF),
                   jax.ShapeDtypeStruct((4, 8, T), F32)],
        compiler_params=_params(("parallel", "arbitrary")),
    )(dmix, fox)


def _fox_backward(qa, qat, ka, kat, va, doa, doat, lse, dl, T, tq, tk):
    nq, nk = T // tq, T // tk

    def body(qa_ref, qat_ref, ka_ref, kat_ref, va_ref, doa_ref, doat_ref, lse_ref, dl_ref,
             dq_ref, dk_ref, dv_ref, df_ref, dr_ref, dqt, dk_acc, dv_acc, df_acc, sdp, pds):
        j, kb = pl.program_id(0), pl.program_id(1)
        lane = lax.broadcasted_iota(jnp.int32, (1, LANES), 1)
        first = (kb * tk) // tq

        @pl.when(kb == 0)
        def _():
            dqt[...] = jnp.zeros(dqt.shape, F32)

        dk_acc[...] = jnp.zeros(dk_acc.shape, F32)
        dv_acc[...] = jnp.zeros(dv_acc.shape, F32)
        df_acc[...] = jnp.zeros(df_acc.shape, F32)

        RC = 64
        last = nq - 1

        def products(slot, qi):
            q0 = pl.multiple_of(qi * tq, tq)
            for hh in range(2):
                sdp[slot, hh, 0] = jnp.dot(ka_ref[hh], qat_ref[hh, :, pl.ds(q0, tq)], preferred_element_type=F32)
                sdp[slot, hh, 1] = jnp.dot(va_ref[hh], doat_ref[hh, :, pl.ds(q0, tq)], preferred_element_type=F32)

        def softmax_bwd(slot, qi, diagonal, valid):
            q0 = pl.multiple_of(qi * tq, tq)
            shift = kb * tk - first * tq
            col = lax.broadcasted_iota(jnp.int32, (RC, tq), 1)
            row = lax.broadcasted_iota(jnp.int32, (RC, tq), 0)
            for hh in range(2):
                lse_row = lse_ref[0, hh:hh + 1, pl.ds(q0, tq)]
                dl_row = dl_ref[0, hh:hh + 1, pl.ds(q0, tq)]
                rsum = jnp.zeros((1, tq), F32)
                for r in range(tk // RC):
                    rows = slice(r * RC, (r + 1) * RC)
                    p = jnp.exp(sdp[slot, hh, 0, rows, :] - lse_row)
                    p = jnp.where((row + (r * RC + shift) <= col) if diagonal else valid, p, 0.0)
                    ds = p * (sdp[slot, hh, 1, rows, :] - dl_row)
                    pds[slot, hh, 0, rows, :] = p.astype(BF)
                    pds[slot, hh, 1, rows, :] = ds.astype(BF)
                    rsum = rsum + jnp.sum(ds, axis=0, keepdims=True)
                    part = ds[:, 0:LANES]
                    for c in range(1, tq // LANES):
                        part = part + ds[:, c * LANES:(c + 1) * LANES]
                    df_acc[hh, rows, :] += part
                dqt[hh, HEAD:HEAD + 8, pl.ds(q0, tq)] += jnp.broadcast_to(rsum, (8, tq))

        def accumulate(slot, qi):
            q0 = pl.multiple_of(qi * tq, tq)
            for hh in range(2):
                dv_acc[hh] += jnp.dot(pds[slot, hh, 0], doa_ref[hh, pl.ds(q0, tq), :], preferred_element_type=F32)
                dk_acc[hh] += jnp.dot(pds[slot, hh, 1], qa_ref[hh, pl.ds(q0, tq), :], preferred_element_type=F32)
                dqt[hh, 0:HEAD, pl.ds(q0, tq)] += jnp.dot(kat_ref[hh, 0:HEAD, :], pds[slot, hh, 1], preferred_element_type=F32)

        products(0, first)
        products(1, jnp.minimum(first + 1, last))
        softmax_bwd(0, first, True, None)

        @pl.loop(0, (nq - first + 1) // 2)
        def _(t):
            qi = first + 2 * t
            products(0, jnp.minimum(qi + 2, last))
            softmax_bwd(1, jnp.minimum(qi + 1, last), False, qi + 1 <= last)
            accumulate(0, qi)
            products(1, jnp.minimum(qi + 3, last))
            softmax_bwd(0, jnp.minimum(qi + 2, last), False, qi + 2 <= last)
            accumulate(1, jnp.minimum(qi + 1, last))

        lo = lane < HEAD
        dk_ref[...] = jnp.where(lo, dk_acc[0], pltpu.roll(dk_acc[1], HEAD, 1))
        dv_ref[...] = jnp.where(lo, dv_acc[0], pltpu.roll(dv_acc[1], HEAD, 1)).astype(dv_ref.dtype)
        f0 = -jnp.sum(df_acc[0], axis=1, keepdims=True)
        f1 = -jnp.sum(df_acc[1], axis=1, keepdims=True)
        df_ref[0] = jnp.where(lane == 2 * j, f0, jnp.where(lane == 2 * j + 1, f1, 0.0))

        @pl.when(kb == nk - 1)
        def _():
            for t in range(nq):
                cols = slice(t * tq, (t + 1) * tq)
                dq_ref[cols, :] = jnp.concatenate([dqt[0, 0:HEAD, cols], dqt[1, 0:HEAD, cols]], axis=0).T
                rsum = jnp.concatenate([dqt[0, HEAD:HEAD + 8, cols], dqt[1, HEAD:HEAD + 8, cols],
                                        jnp.zeros((LANES - 16, tq), F32)], axis=0).T
                dr_ref[0, cols, :] = jnp.where(lane == 2 * j, rsum[:, 0:1], jnp.where(lane == 2 * j + 1, rsum[:, 8:9], 0.0))

    nat_full = pl.BlockSpec((2, T, LANES), lambda j, kb: (j, 0, 0))
    trn_full = pl.BlockSpec((2, LANES, T), lambda j, kb: (j, 0, 0))
    nat_blk = pl.BlockSpec((2, tk, LANES), lambda j, kb: (j, kb, 0))
    trn_blk = pl.BlockSpec((2, LANES, tk), lambda j, kb: (j, 0, kb))
    rows = pl.BlockSpec((1, 8, T), lambda j, kb: (j, 0, 0))
    blk = pl.BlockSpec((tk, LANES), lambda j, kb: (kb, j))
    return pl.pallas_call(
        body, name="fox_backward", grid=(4, nk),
        in_specs=[nat_full, trn_full, nat_blk, trn_blk, nat_blk, nat_full, trn_full, rows, rows],
        out_specs=[pl.BlockSpec((T, LANES), lambda j, kb: (0, j)), blk, blk, pl.BlockSpec((1, tk, LANES), lambda j, kb: (j, kb, 0)),
                   pl.BlockSpec((1, T, LANES), lambda j, kb: (j, 0, 0))],
        out_shape=[jax.ShapeDtypeStruct((T, 4 * LANES), F32), jax.ShapeDtypeStruct((T, 4 * LANES), F32),
                   jax.ShapeDtypeStruct((T, 4 * LANES), BF), jax.ShapeDtypeStruct((4, T, LANES), F32),
                   jax.ShapeDtypeStruct((4, T, LANES), F32)],
        scratch_shapes=[pltpu.VMEM((2, HEAD + 8, T), F32), pltpu.VMEM((2, tk, LANES), F32), pltpu.VMEM((2, tk, LANES), F32),
                        pltpu.VMEM((2, tk, LANES), F32), pltpu.VMEM((2, 2, 2, tk, tq), F32), pltpu.VMEM((2, 2, 2, tk, tq), BF)],
        compiler_params=_params(("arbitrary", "arbitrary")),
    )(qa, qat, ka, kat, va, doa, doat, lse, dl)


def _fgate_bwd_col(ffp, bpad, dfc4, drc4, T):
    def body(ff_ref, b_ref, dfc_ref, drc_ref, dff_ref, db_ref):
        lane = lax.broadcasted_iota(jnp.int32, (1, LANES), 1)
        tri = _tri(False)
        carry = jnp.zeros((1, LANES), F32)
        db = jnp.zeros((1, LANES), F32)
        for blk in reversed(range(T // _FB)):
            rows = slice(blk * _FB, (blk + 1) * _FB)
            dcol = dfc_ref[0, rows, :] + drc_ref[0, rows, :]
            for pair in range(1, 4):
                dcol = dcol + (dfc_ref[pair, rows, :] + drc_ref[pair, rows, :])
            dlf = jnp.dot(tri, dcol, precision=lax.Precision.HIGHEST, preferred_element_type=F32) + carry
            carry = dlf[0:1, :]
            z = ff_ref[blk * _FB:(blk + 1) * _FB, :] + b_ref[...]
            dz = jnp.where(lane < 8, dlf * jax.nn.sigmoid(-z), 0.0)
            dff_ref[blk * _FB:(blk + 1) * _FB, :] = dz.astype(dff_ref.dtype)
            db = db + jnp.sum(dz, axis=0, keepdims=True)
        db_ref[...] = db

    return pl.pallas_call(
        body, name="fgate_bwd",
        out_shape=[jax.ShapeDtypeStruct((T, LANES), BF), jax.ShapeDtypeStruct((1, LANES), F32)],
        compiler_params=pltpu.CompilerParams(vmem_limit_bytes=VMEM_LIMIT),
    )(ffp, bpad, dfc4, drc4)


MESH = pl.DeviceIdType.MESH
N_PEERS = N_DEV - 1


def _place():
    return lax.axis_index("x"), lax.axis_index("y"), lax.axis_index("c")


def _all_gather(shard):
    R, W = shard.shape

    def body(x_ref, out_ref, send_sems, recv_sems, local_sem):
        x, y, c = _place()
        me, sibling = (x, y, c), (x, y, 1 - c)
        chips = [(1 - x, y), (x, 1 - y), (1 - x, 1 - y)]

        def slot(px, py, pc):
            return out_ref.at[4 * px + 2 * py + pc]

        def copy(k, block, to, src=None):
            return pltpu.make_async_remote_copy(
                src_ref=slot(*block) if src is None else src, dst_ref=slot(*block),
                send_sem=send_sems.at[k], recv_sem=recv_sems.at[k], device_id=to, device_id_type=MESH)

        mine = pltpu.make_async_copy(x_ref, slot(*me), local_sem)
        mine.start()
        first = [copy(0, me, sibling, src=x_ref)]
        first += [copy(1 + n, me, (*chip, c), src=x_ref) for n, chip in enumerate(chips)]
        for cp in first:
            cp.start()
        passed = [copy(4 + n, (*chip, c), sibling) for n, chip in enumerate(chips)]
        for n, chip in enumerate(chips):
            copy(1 + n, (*chip, c), me).wait_recv()
            passed[n].start()
        copy(0, sibling, me).wait_recv()
        for n, chip in enumerate(chips):
            copy(4 + n, (*chip, 1 - c), me).wait_recv()
        for cp in first + passed:
            cp.wait_send()
        mine.wait()

    return pl.pallas_call(
        body, name="all_gather_weights",
        out_shape=jax.ShapeDtypeStruct((N_DEV, R, W), shard.dtype),
        in_specs=[pl.BlockSpec(memory_space=pl.ANY)], out_specs=pl.BlockSpec(memory_space=pl.ANY),
        scratch_shapes=[pltpu.SemaphoreType.DMA((N_PEERS,)), pltpu.SemaphoreType.DMA((N_PEERS,)), pltpu.SemaphoreType.DMA],
    )(shard)


def _exchange_copies(src_refs, land_refs, send_sems, recv_sems, scatter):
    x, y, c = _place()
    me = 4 * x + 2 * y + c
    copies = []
    for k, (src_ref, land_ref) in enumerate(zip(src_refs, land_refs)):
        for r in range(1, N_DEV):
            px, py, pc = x ^ (r >> 2), y ^ ((r >> 1) & 1), c ^ (r & 1)
            copies.append(pltpu.make_async_remote_copy(
                src_ref=src_ref.at[4 * px + 2 * py + pc] if scatter else src_ref, dst_ref=land_ref.at[me],
                send_sem=send_sems.at[k * N_PEERS + r - 1], recv_sem=recv_sems.at[k * N_PEERS + r - 1],
                device_id=(px, py, pc), device_id_type=MESH))
    return copies


_HBM = pl.BlockSpec(memory_space=pltpu.HBM)
_SEM = pl.BlockSpec(memory_space=pltpu.SEMAPHORE)
_EFFECT = pltpu.SideEffectType.DATAFLOW_SIDE_EFFECTING


def _exchange_start(name, srcs, lands, scatter):
    n = len(srcs)

    def body(*refs):
        send_sems, recv_sems = refs[2 * n], refs[2 * n + 1]
        for cp in _exchange_copies(refs[:n], refs[n:2 * n], send_sems, recv_sems, scatter):
            cp.start()
        token = refs[-1]
        token[...] = jnp.zeros(token.shape, F32)

    arrays = list(srcs) + list(lands)
    out = pl.pallas_call(
        body, name=name,
        out_shape=(pltpu.SemaphoreType.DMA((n * N_PEERS,)), pltpu.SemaphoreType.DMA((n * N_PEERS,)))
        + tuple(pltpu.HBM(a.shape, a.dtype) for a in arrays) + (jax.ShapeDtypeStruct((8, LANES), F32),),
        in_specs=(_HBM,) * (2 * n), out_specs=(_SEM, _SEM) + (_HBM,) * (2 * n) + (pl.BlockSpec(memory_space=pltpu.VMEM),),
        input_output_aliases={k: 2 + k for k in range(2 * n)},
        compiler_params=pltpu.CompilerParams(has_side_effects=_EFFECT),
    )(*(pltpu.with_memory_space_constraint(a, pltpu.HBM) for a in arrays))
    return out[0], out[1], out[2:2 + n], out[2 + n:2 + 2 * n], out[-1]


def _exchange_wait(name, started, after, scatter):
    send_sems, recv_sems, srcs, lands, _ = started
    n = len(srcs)

    def body(*refs):
        copies = _exchange_copies(refs[:n], refs[n:2 * n], refs[2 * n], refs[2 * n + 1], scatter)
        for cp in copies:
            cp.wait_send()
        for cp in copies:
            cp.wait_recv()

    arrays = list(srcs) + list(lands)
    out = pl.pallas_call(
        body, name=name,
        out_shape=tuple(pltpu.HBM(a.shape, a.dtype) for a in arrays),
        in_specs=(_HBM,) * (2 * n) + (_SEM, _SEM, pl.BlockSpec(memory_space=pl.ANY)), out_specs=(_HBM,) * (2 * n),
        input_output_aliases={k: k for k in range(2 * n)},
        compiler_params=pltpu.CompilerParams(has_side_effects=_EFFECT),
    )(*arrays, send_sems, recv_sems, after)
    return out[:n], out[n:]


def _adam_update(g, w, m, v):
    m2 = ADAM_B1 * m + (1.0 - ADAM_B1) * g
    v2 = ADAM_B2 * v + (1.0 - ADAM_B2) * jnp.square(g)
    m_hat = m2 / (1.0 - ADAM_B1 ** ADAM_STEP)
    v_hat = v2 / (1.0 - ADAM_B2 ** ADAM_STEP)
    return g, -ADAM_LR * (m_hat / (jnp.sqrt(v_hat) + ADAM_EPS) + ADAM_WD * w), m2, v2


def _adamw(name, slots, own, w, m, v):
    R, W = w.shape

    def body(s_ref, *refs):
        if own is not None:
            g = refs[0][...].astype(F32)
            refs = refs[1:]
        else:
            g = jnp.zeros((R, W), F32)
        for s in range(N_DEV):
            g = g + s_ref[s].astype(F32)
        w_ref, m_ref, v_ref = refs[:3]
        for o, r in zip(refs[3:], _adam_update(g, w_ref[...], m_ref[...], v_ref[...])):
            o[...] = r

    full = pl.BlockSpec((R, W), lambda i: (0, 0))
    args = [slots] + ([own] if own is not None else []) + [w, m, v]
    return pl.pallas_call(
        body, name=name, grid=(1,),
        in_specs=[pl.BlockSpec((N_DEV, R, W), lambda i: (0, 0, 0))] + [full] * (len(args) - 1),
        out_specs=[full] * 4, out_shape=[jax.ShapeDtypeStruct((R, W), F32)] * 4,
        compiler_params=_params(("arbitrary",)),
    )(*args)


def _tables(T):
    pos = jnp.arange(T, dtype=F32)
    inv_freq = 10000.0 ** (-jnp.arange(0, HEAD, 2, dtype=F32) / HEAD)
    ang = pos[:, None] * inv_freq[None, :]
    cos, sin = jnp.cos(ang), jnp.sin(ang)
    cos4 = jnp.tile(cos, (1, 4))
    sin4 = jnp.tile(jnp.concatenate([-sin, sin], axis=1), (1, 2))
    log_g = jnp.log(1.0 - 2.0 ** (-5.0 - jnp.arange(8, dtype=F32)))
    return cos4, sin4, jnp.repeat(log_g, HEAD)[None, :]


def _local_step(x, mem, target, sp, w_inT, token, fetch_rest, push, push_small):
    T = x.shape[0]
    tm = min(512, T)
    tq = min(256, T)
    tb = min(1024, T)
    cos4, sin4, lg = _tables(T)
    g_fq2 = jnp.tile(sp["g_fox_q"], (1, 2))
    g_fk2 = jnp.tile(sp["g_fox_k"], (1, 2))
    g_ret = sp["g_ret_out"].reshape(1, 8 * HEAD)
    bpad = jnp.pad(sp["b_forget"], ((0, 0), (0, LANES - 8)))
    w_secs = [w_inT[k * 512:(k + 1) * 512] for k in range(7)]
    w_ffT = jnp.pad(w_inT[3584:3592], ((0, LANES - 8), (0, 0)))
    w_mainT = w_inT[:3584]
    tie = lambda p, tok: p + tok[0:1, 0:1]
    tm2, tm4 = min(1024, T), min(2048, T)

    hn1, = _rw_fwd("rms_mix", _rms_fn, [(x, D, 0, False)], [(tie(sp["g_mix"], token), D, 0, False)], [(BF, D)], T, tm4, 1)
    P, = _mm("proj_in", [[(hn1, w_mainT, "nt")]], [], _ident, T, 3584, tm4, 512, [F32])
    ffp, = _mm("proj_ff", [[(hn1, w_ffT, "nt")]], [], _ident, T, LANES, tm, LANES, [F32])
    ret, s0 = _ret_fwd(P, cos4, sin4, g_ret, lg, T, tb)
    fc, _ = _fgate_fwd(ffp, bpad, T)
    qa, qat, ka, kat, va, vat = _fox_operands(P, fc, g_fq2, g_fk2, T, tm4)
    fox, lse = _fox_forward(qat, ka, vat, T, min(512, T), tq)
    W = fetch_rest(fox)
    w_out_halves = (W["w_out"][:4 * LANES], W["w_out"][4 * LANES:])
    h1, hn2 = _mm("proj_out", [[(ret, w_out_halves[0], "nn"), (fox, w_out_halves[1], "nn")]], [x], _add_rms_epi, T, D, tm2, D,
                  [F32, BF], params=[sp["g_xattn"]])

    qx, = _mm("proj_xq", [[(hn2, W["w_xq"], "nn")]], [], _ident, T, D, tm2, D, [F32])
    memn, = _rw_fwd("rms_mem", _rms_fn, [(mem, D, 0, False)], [(sp["g_mem"], D, 0, False)], [(BF, D)], N_MEM, N_MEM, 1)
    kv, = _mm("proj_xkv", [[(memn, W["w_xkvT"], "nt")]], [], _ident, N_MEM, 2 * D, N_MEM, 512, [F32])
    xa_rows = [(qx, XHEAD, 0, True)]
    xa_params = [(sp["g_xq"], XHEAD, 0, False), (sp["g_xk"], XHEAD, 0, False), (kv, XHEAD, 0, True), (kv, XHEAD, 4, True)]
    xo, = _rw_fwd("xattn_fwd", _xattn_fn, xa_rows, xa_params, [(BF, XHEAD)], T, tm4, 4)
    h2, hn3 = _mm("proj_xo", [[(xo, W["w_xo"], "nn")]], [h1], _add_rms_epi, T, D, tm2, D, [F32, BF], params=[sp["g_ffn"]])

    gate, up, act = _mm("ffn_in", [[(hn3, W["w_gateT"], "nt")], [(hn3, W["w_upT"], "nt")]], [], _swiglu_fwd_epi,
                        T, D_FF, tm4, 256, [BF, BF, BF])
    dy, dyb, loss_part = _mm("ffn_out", [[(act, W["w_down"], "nn")]], [h2, target], _add_loss_epi, T, D, tm, D, [F32, BF], n_acc=1)

    dgate, dup = _mm("ffn_out_bwd", [[(dyb, W["w_down"], "nt")]], [gate, up], _swiglu_bwd_epi, T, D_FF, tm4, 256, [BF, BF])
    gW = {}
    gW["w_gateT"], = _mm("dw_gate", [[(dgate, hn3, "tn")]], [], _ident, D_FF, D, 256, D, [BF])
    gW["w_upT"], = _mm("dw_up", [[(dup, hn3, "tn")]], [], _ident, D_FF, D, 256, D, [BF])
    gW["w_down"], = _mm("dw_down", [[(act, dyb, "tn")]], [], _ident, D_FF, D, 256, D, [BF])
    tok = push("ffn", gW)
    gs = {}
    dh2, dh2b, gs["g_ffn"] = _mm("ffn_in_bwd", [[(dgate, W["w_gateT"], "nn"), (dup, W["w_upT"], "nn")]], [h2, dy], _rms_bwd_epi,
                                 T, D, min(256, T), D, [F32, BF], params=[tie(sp["g_ffn"], tok)], n_acc=1)

    dxo, = _mm("proj_xo_bwd", [[(dh2b, W["w_xo"], "nt")]], [], _ident, T, D, tm2, D, [BF])
    gW["w_xo"], = _mm("dw_xo", [[(xo, dh2b, "tn")]], [], _ident, D, D, 256, D, [BF])
    dqx, gs["g_xq"], gs["g_xk"], dkv_k, dkv_v = _rw_bwd(
        "xattn_bwd", _xattn_fn, xa_rows, xa_params, [(dxo, XHEAD, 0, True)], T, tm4, 4, [BF], [True, True, True, True])
    dkv = jnp.concatenate([dkv_k[:, :D], dkv_v[:, D:]], axis=1)
    gW["w_xq"], = _mm("dw_xq", [[(hn2, dqx, "tn")]], [], _ident, D, D, 256, D, [BF])
    dmemn, = _mm("proj_xkv_bwd", [[(dkv, W["w_xkvT"], "nn")]], [], _ident, N_MEM, D, N_MEM, 512, [F32])
    gW["w_xkvT"], = _mm("dw_xkv", [[(dkv, memn, "tn")]], [], _ident, 2 * D, D, 512, D, [BF])
    tok = push("xattn", gW)
    gs["g_mem"], = _rw_bwd("rms_mem_bwd", _rms_fn, [(mem, D, 0, False)], [(sp["g_mem"], D, 0, False)], [(dmemn, D, 0, False)],
                           N_MEM, N_MEM, 1, [None], [True])
    dh1, dh1b, gs["g_xattn"] = _mm("proj_xq_bwd", [[(dqx, W["w_xq"], "nt")]], [h1, dh2], _rms_bwd_epi, T, D, tm, D, [F32, BF],
                                   params=[tie(sp["g_xattn"], tok)], n_acc=1)

    dmix, = _mm("proj_out_bwd", [[(dh1b, W["w_out"], "nt")]], [], _ident, T, D, tm2, D, [F32])
    gW["w_out"] = jnp.concatenate([_mm("dw_out_%d" % k, [[(a, dh1b, "tn")]], [], _ident, 4 * LANES, D, 256, D, [BF])[0]
                                   for k, a in enumerate((ret, fox))], axis=0)
    tok = push("out", gW)
    doa, doat, dl = _fox_cotangent(dmix, fox, T, tm4)
    dqn, dkn, dfv, dfc4, drc4 = _fox_backward(qa, qat, ka, kat, va, doa, doat, lse + tok[0:1, 0:1], dl, T, tq, tq)
    dfq, dfk, gq2, gk2 = _rw_bwd("fox_prep_bwd", _fox_prep_fn, [(P, LANES, 16, True), (P, LANES, 20, True)],
                                 [(g_fq2, LANES, 0, False), (g_fk2, LANES, 0, False)],
                                 [(dqn, LANES, 0, True), (dkn, LANES, 0, True)], T, tm4, 4, [BF, BF], [True, True])
    gs["g_fox_q"] = gq2[:, :HEAD] + gq2[:, HEAD:]
    gs["g_fox_k"] = gk2[:, :HEAD] + gk2[:, HEAD:]
    dff, dbp = _fgate_bwd_col(ffp, bpad, dfc4, drc4, T)
    gs["b_forget"] = dbp[:, :8]
    drq, drk, drv, drg, dg_ret = _ret_bwd(P, cos4, sin4, g_ret, lg, s0, dmix, T, tb)
    gs["g_ret_out"] = dg_ret
    dsecs = [drq, drk, drv, drg, dfq, dfk, dfv]
    g_secs = [_mm("dw_in_%d" % k, [[(d, hn1, "tn")]], [], _ident, 512, D, 256, D, [BF])[0] for k, d in enumerate(dsecs)]
    g_ff, = _mm("dw_in_ff", [[(dff, hn1, "tn")]], [], _ident, LANES, D, LANES, D, [BF])
    gW["w_inT"] = jnp.concatenate(g_secs + [g_ff[:8]], axis=0)
    tok = push("in", gW)
    grad_x, _, gs["g_mix"] = _mm("proj_in_bwd", [[(d, w, "nn") for d, w in zip(dsecs, w_secs)] + [(dff, w_ffT, "nn")]], [x, dh1],
                                 _rms_bwd_epi, T, D, tm, D, [F32, BF], params=[tie(sp["g_mix"], tok)], n_acc=1)
    return grad_x, push_small(gs, loss_part)


_CANON = {"w_in": "w_inT", "w_xkv": "w_xkvT", "w_gate": "w_gateT", "w_up": "w_upT"}
_SMALL = (("g_mix", 0, 0, 1024), ("g_xattn", 1, 0, 1024), ("g_mem", 2, 0, 1024), ("g_ffn", 3, 0, 1024),
          ("g_ret_out", 4, 0, 512), ("g_xq", 4, 512, 256), ("g_xk", 4, 768, 256),
          ("g_fox_q", 5, 0, 64), ("g_fox_k", 5, 64, 64), ("b_forget", 5, 128, 8))
_LOSS_AT = (5, 256)


def _pack_small(tree):
    buf = jnp.zeros((SMALL_ROWS, D), F32)
    for name, r, c, n in _SMALL:
        buf = lax.dynamic_update_slice(buf, tree[name].reshape(1, n).astype(F32), (r, c))
    return buf


def _unpack_small(buf, like):
    return {name: buf[r:r + 1, c:c + n].reshape(like[name].shape) for name, r, c, n in _SMALL}


def _canonical(tree, name):
    a = tree[name][0]
    return a.T if W_SHARD[name][1] else a


def _from_canonical(a, name):
    return (a.T if W_SHARD[name][1] else a)[None]


def kernel(x, mem, g_mix, w_in, b_forget, g_ret_out, g_fox_q, g_fox_k, w_out, g_xattn, w_xq, w_xkv, g_mem, g_xq, g_xk, w_xo, g_ffn, w_gate, w_up, w_down, loss_target, m_g_mix, m_w_in, m_b_forget, m_g_ret_out, m_g_fox_q, m_g_fox_k, m_w_out, m_g_xattn, m_w_xq, m_w_xkv, m_g_mem, m_g_xq, m_g_xk, m_w_xo, m_g_ffn, m_w_gate, m_w_up, m_w_down, v_g_mix, v_w_in, v_b_forget, v_g_ret_out, v_g_fox_q, v_g_fox_k, v_w_out, v_g_xattn, v_w_xq, v_w_xkv, v_g_mem, v_g_xq, v_g_xk, v_w_xo, v_g_ffn, v_w_gate, v_w_up, v_w_down):
    names = ("g_mix", "w_in", "b_forget", "g_ret_out", "g_fox_q", "g_fox_k", "w_out", "g_xattn", "w_xq", "w_xkv", "g_mem",
             "g_xq", "g_xk", "w_xo", "g_ffn", "w_gate", "w_up", "w_down")
    w = dict(zip(names, (g_mix, w_in, b_forget, g_ret_out, g_fox_q, g_fox_k, w_out, g_xattn, w_xq, w_xkv, g_mem, g_xq, g_xk,
                         w_xo, g_ffn, w_gate, w_up, w_down)))
    m = dict(zip(names, (m_g_mix, m_w_in, m_b_forget, m_g_ret_out, m_g_fox_q, m_g_fox_k, m_w_out, m_g_xattn, m_w_xq, m_w_xkv,
                         m_g_mem, m_g_xq, m_g_xk, m_w_xo, m_g_ffn, m_w_gate, m_w_up, m_w_down)))
    v = dict(zip(names, (v_g_mix, v_w_in, v_b_forget, v_g_ret_out, v_g_fox_q, v_g_fox_k, v_w_out, v_g_xattn, v_w_xq, v_w_xkv,
                         v_g_mem, v_g_xq, v_g_xk, v_w_xo, v_g_ffn, v_w_gate, v_w_up, v_w_down)))
    small_names = [s[0] for s in _SMALL]
    me = 4 * lax.axis_index("x") + 2 * lax.axis_index("y") + lax.axis_index("c")

    first = _all_gather(_canonical(w, "w_in").astype(BF))
    first, rest = lax.optimization_barrier((first, [_canonical(w, n).astype(BF) for n in GATHER_REST]))
    rest_started = _exchange_start("gather_rest_start", rest, [jnp.broadcast_to(a[None], (N_DEV,) + a.shape) for a in rest],
                                   scatter=False)

    def fetch_rest(after):
        lands = _exchange_wait("gather_rest_wait", rest_started, after, scatter=False)[1]
        return {_CANON.get(n, n): a.reshape(N_DEV * a.shape[1], D) for n, a in zip(GATHER_REST, lands)}

    pushed = {}

    def push(group, grads):
        srcs = [grads[_CANON.get(n, n)].reshape(N_DEV, W_SHARD[n][0], D) for n in SCATTER_GROUPS[group]]
        pushed[group] = _exchange_start("scatter_%s_start" % group, srcs, [jnp.zeros(a.shape, BF) for a in srcs], scatter=True)
        return pushed[group][4]

    def push_small(gs, loss_part):
        small = lax.dynamic_update_slice(_pack_small(gs), loss_part[:, :1], _LOSS_AT)
        pushed["small"] = _exchange_start("gather_small_start", [small], [jnp.broadcast_to(small[None], (N_DEV,) + small.shape)],
                                          scatter=False)
        return pushed["small"][4]

    sp = {n: w[n].reshape(1, -1) for n in small_names}
    grad_x, done = _local_step(x[0], mem[0], loss_target[0], sp, first.reshape(N_DEV * W_SHARD["w_in"][0], D),
                               rest_started[4], fetch_rest, push, push_small)

    results, after = {}, done
    for group in ("ffn", "xattn", "out", "small", "in"):
        if group == "small":
            recv_small = _exchange_wait("gather_small_wait", pushed["small"], after, scatter=False)[1][0]
            g_sm, d_sm, m_sm, v_sm = _adamw("adamw_small", recv_small, None, _pack_small(w), _pack_small(m), _pack_small(v))
            after = g_sm
            continue
        sents, recvs = _exchange_wait("scatter_%s_wait" % group, pushed[group], after, scatter=True)
        for name, sent, recv in zip(SCATTER_GROUPS[group], sents, recvs):
            own = lax.dynamic_index_in_dim(sent, me, axis=0, keepdims=False)
            res = _adamw("adamw_" + name, recv, own, *(_canonical(t, name) for t in (w, m, v)))
            results[name] = [_from_canonical(r, name) for r in res]
        after = results[SCATTER_GROUPS[group][-1]][0]
    loss = g_sm[_LOSS_AT[0], _LOSS_AT[1]]

    outs = []
    for k, sm in enumerate((g_sm, d_sm, m_sm, v_sm)):
        tree = _unpack_small(sm, w)
        tree.update({name: res[k] for name, res in results.items()})
        outs += [tree[n] for n in names]
    return (loss, grad_x[None], *outs)
```

```python
import jax
import jax.numpy as jnp
from jax import lax
from jax.experimental import pallas as pl
from jax.experimental.pallas import tpu as pltpu

F32 = jnp.float32
BF = jnp.bfloat16

D = 1024
HEAD = 64
CHUNK = 64
N_MEM = 256
XHEAD = 256
D_FF = 2816
EPS = 1e-6
NEG = -1e30
LANES = 128
N_DEV = 8
V7X_VMEM_BYTES = 64 * 1024 * 1024
VMEM_LIMIT = V7X_VMEM_BYTES - 8 * 1024 * 1024

ADAM_LR, ADAM_B1, ADAM_B2, ADAM_EPS, ADAM_WD, ADAM_STEP = 0.001, 0.9, 0.999, 1e-08, 0.01, 10

W_SHARD = {"w_in": (449, True), "w_out": (128, False), "w_xq": (128, False), "w_xkv": (256, True),
           "w_xo": (128, False), "w_gate": (352, True), "w_up": (352, True), "w_down": (352, False)}
GATHER_REST = ("w_out", "w_xq", "w_xkv", "w_xo", "w_gate", "w_up", "w_down")
SCATTER_GROUPS = {"ffn": ("w_gate", "w_up", "w_down"), "xattn": ("w_xq", "w_xo", "w_xkv"), "out": ("w_out",), "in": ("w_in",)}
SMALL_ROWS = 8

NT = (((1,), (1,)), ((), ()))
NN = (((1,), (0,)), ((), ()))
TN = (((0,), (0,)), ((), ()))
_DIMS = {"nn": NN, "nt": NT, "tn": TN}


def _params(sem):
    return pltpu.CompilerParams(dimension_semantics=sem, vmem_limit_bytes=VMEM_LIMIT)


def _mm(name, products, extras, epilogue, M, N, tm, tn, out_dtypes, params=(), n_acc=0):
    assert n_acc == 0 or tn == N
    flat = [t for p in products for t in p]
    counts = [len(p) for p in products]
    in_specs, args, where, slots = [], [], {}, []

    def operand(arr, spec, kind):
        key = (id(arr), kind)
        if key not in where:
            where[key] = len(args)
            args.append(arr)
            in_specs.append(spec)
        return where[key]

    for a, b, form in flat:
        if form == "tn":
            ia = operand(a, pl.BlockSpec((a.shape[0], tm), lambda i, j: (0, i)), "a_tn")
        else:
            ia = operand(a, pl.BlockSpec((tm, a.shape[1]), lambda i, j: (i, 0)), "a")
        if form == "nt":
            ib = operand(b, pl.BlockSpec((tn, b.shape[1]), lambda i, j: (j, 0)), "b_nt")
        else:
            ib = operand(b, pl.BlockSpec((b.shape[0], tn), lambda i, j: (0, j)), "b")
        slots.append((ia, ib))
    n_mm = len(args)
    for e in extras:
        in_specs.append(pl.BlockSpec((tm, tn), lambda i, j: (i, j)))
        args.append(e)
    for p in params:
        in_specs.append(pl.BlockSpec((1, tn), lambda i, j: (0, j)))
        args.append(p)
    n_in = len(args)
    n_out = len(out_dtypes)

    def body(*refs):
        ins, outs = refs[:n_in], refs[n_in:]
        prods, p = [], 0
        for c in counts:
            acc = None
            for _ in range(c):
                a = ins[slots[p][0]][...].astype(BF)
                b = ins[slots[p][1]][...].astype(BF)
                d = lax.dot_general(a, b, _DIMS[flat[p][2]], preferred_element_type=F32)
                acc = d if acc is None else acc + d
                p += 1
            prods.append(acc)
        ex = [r[...].astype(F32) for r in ins[n_mm:]]
        res = epilogue(*prods, *ex)
        for o, r in zip(outs[:n_out], res[:n_out]):
            o[...] = r.astype(o.dtype)
        for o, r in zip(outs[n_out:], res[n_out:]):
            @pl.when(pl.program_id(0) == 0)
            def _(o=o):
                o[...] = jnp.zeros(o.shape, F32)
            o[...] += r

    return pl.pallas_call(
        body, name=name, grid=(M // tm, N // tn), in_specs=in_specs,
        out_specs=[pl.BlockSpec((tm, tn), lambda i, j: (i, j)) for _ in out_dtypes]
        + [pl.BlockSpec((1, tn), lambda i, j: (0, j)) for _ in range(n_acc)],
        out_shape=[jax.ShapeDtypeStruct((M, N), dt) for dt in out_dtypes] + [jax.ShapeDtypeStruct((1, N), F32)] * n_acc,
        compiler_params=_params(("arbitrary", "arbitrary")),
    )(*args)


def _ident(x):
    return (x,)


def _each(*xs):
    return xs


def _spec(rows, w, off, per_j):
    if per_j:
        return pl.BlockSpec((rows, w), lambda j, i: (i, off + j))
    return pl.BlockSpec((rows, w), lambda j, i: (i, off))


def _pspec(rows, w, off, per_j):
    if per_j:
        return pl.BlockSpec((rows, w), lambda j, i: (0, off + j))
    return pl.BlockSpec((rows, w), lambda j, i: (0, off))


def _rw_fwd(name, fn, rows, params, outs, T, tm, nj, n_acc=0):
    in_specs = [_spec(tm, w, off, pj) for _, w, off, pj in rows] + [_pspec(a.shape[0], w, off, pj) for a, w, off, pj in params]
    args = [r[0] for r in rows] + [p[0] for p in params]
    n_in, n_out = len(args), len(outs)
    out_specs = [pl.BlockSpec((tm, w), lambda j, i: (i, j)) for _, w in outs]
    out_shape = [jax.ShapeDtypeStruct((T, nj * w), dt) for dt, w in outs]
    out_specs += [pl.BlockSpec((1, LANES), lambda j, i: (0, 0)) for _ in range(n_acc)]
    out_shape += [jax.ShapeDtypeStruct((1, LANES), F32) for _ in range(n_acc)]

    def body(*refs):
        vals = [r[...].astype(F32) for r in refs[:n_in]]
        res = fn(*vals)
        orefs = refs[n_in:]
        for k in range(n_out):
            orefs[k][...] = res[k].astype(orefs[k].dtype)
        first = (pl.program_id(0) == 0) & (pl.program_id(1) == 0)
        for k in range(n_acc):
            @pl.when(first)
            def _(k=k):
                orefs[n_out + k][...] = jnp.zeros((1, LANES), F32)
            orefs[n_out + k][...] += res[n_out + k]

    return pl.pallas_call(
        body, name=name, grid=(nj, T // tm), in_specs=in_specs, out_specs=out_specs, out_shape=out_shape,
        compiler_params=_params(("arbitrary", "arbitrary")),
    )(*args)


def _rw_bwd(name, fn, rows, params, cots, T, tm, nj, row_grads, param_grads, resid=None):
    in_specs = ([_spec(tm, w, off, pj) for _, w, off, pj in rows] + [_pspec(a.shape[0], w, off, pj) for a, w, off, pj in params]
                + [_spec(tm, w, off, pj) for _, w, off, pj in cots])
    args = [r[0] for r in rows] + [p[0] for p in params] + [c[0] for c in cots]
    if resid is not None:
        in_specs.append(_spec(tm, rows[0][1], rows[0][2], rows[0][3]))
        args.append(resid)
    nr, npar, nc = len(rows), len(params), len(cots)
    out_specs, out_shape, kinds = [], [], []
    for k, dts in enumerate(row_grads):
        for dt in (dts if isinstance(dts, (list, tuple)) else [dts]):
            if dt is not None:
                w = rows[k][1]
                out_specs.append(pl.BlockSpec((tm, w), lambda j, i: (i, j)))
                out_shape.append(jax.ShapeDtypeStruct((T, nj * w), dt))
                kinds.append(("row", k))
    for k, need in enumerate(param_grads):
        if need:
            a, w, off, pj = params[k]
            out_specs.append(_pspec(a.shape[0], w, off, pj))
            out_shape.append(jax.ShapeDtypeStruct(a.shape, F32))
            kinds.append(("par", k))

    def body(*refs):
        vals = [r[...].astype(F32) for r in refs[:nr + npar]]
        ct = tuple(r[...].astype(F32) for r in refs[nr + npar:nr + npar + nc])
        _, vjp = jax.vjp(lambda *a: tuple(fn(*a)), *vals)
        grads = list(vjp(ct))
        n_in = nr + npar + nc + (resid is not None)
        if resid is not None:
            grads[0] = grads[0] + refs[n_in - 1][...].astype(F32)
        orefs = refs[n_in:]
        j, i = pl.program_id(0), pl.program_id(1)
        for o, (kind, k) in zip(orefs, kinds):
            if kind == "row":
                o[...] = grads[k].astype(o.dtype)
            else:
                first = (i == 0) if params[k][3] else ((i == 0) & (j == 0))

                @pl.when(first)
                def _(o=o):
                    o[...] = jnp.zeros(o.shape, F32)
                o[...] += grads[nr + k]

    return pl.pallas_call(
        body, name=name, grid=(nj, T // tm), in_specs=in_specs, out_specs=out_specs, out_shape=out_shape,
        compiler_params=_params(("arbitrary", "arbitrary")),
    )(*args)


def _rms(x, g):
    return x * lax.rsqrt(jnp.mean(x * x, axis=-1, keepdims=True) + EPS) * g


def _rms_fn(x, g):
    return (_rms(x, g),)


def _lo_mask():
    return lax.broadcasted_iota(jnp.int32, (1, LANES), 1) < HEAD


def _gmean(x, lo):
    s0 = jnp.sum(jnp.where(lo, x, 0.0), axis=-1, keepdims=True)
    s1 = jnp.sum(jnp.where(lo, 0.0, x), axis=-1, keepdims=True)
    return jnp.where(lo, s0, s1) * (1.0 / HEAD)


def _fox_prep_fn(fq, fk, gq, gk):
    lo = _lo_mask()
    qn = fq * lax.rsqrt(_gmean(fq * fq, lo) + EPS) * gq * (HEAD ** -0.5)
    kn = fk * lax.rsqrt(_gmean(fk * fk, lo) + EPS) * gk
    return qn, kn


@jax.custom_vjp
def _swap_halves(x):
    bit = (lax.broadcasted_iota(jnp.int32, (1, LANES), 1) & (HEAD // 2)) == 0
    return jnp.where(bit, pltpu.roll(x, LANES - HEAD // 2, 1), pltpu.roll(x, HEAD // 2, 1))


_swap_halves.defvjp(lambda x: (_swap_halves(x), None), lambda _, g: (_swap_halves(g),))


def _ret_fn(rq, rk, rv, rg, cos, sin, s_in, g, lg):
    tb = rq.shape[0]
    nc = tb // CHUNK
    lo = _lo_mask()
    row = lax.broadcasted_iota(jnp.int32, (LANES, 1), 0) < HEAD
    same_head = row == lo
    q = (rq * cos + _swap_halves(rq) * sin) * (HEAD ** -0.5)
    k = rk * cos + _swap_halves(rk) * sin
    q3, k3, v3 = q.reshape(nc, CHUNK, LANES), k.reshape(nc, CHUNK, LANES), rv.reshape(nc, CHUNK, LANES)
    pos = lax.broadcasted_iota(jnp.int32, (CHUNK, 1), 0).astype(F32)
    q_decay = jnp.exp(lg * (pos + 1.0))
    k_decay = jnp.exp(lg * (CHUNK - 1.0 - pos))
    chunk_decay = jnp.exp(lg * float(CHUNK))
    dist = jnp.abs(lax.broadcasted_iota(jnp.int32, (CHUNK, CHUNK), 0) - lax.broadcasted_iota(jnp.int32, (CHUNK, CHUNK), 1)).astype(F32)
    v3b = v3.astype(BF)
    intra = []
    for hh in range(2):
        hm = lo if hh == 0 else ~lo
        lg_h = lg[:, hh * HEAD:hh * HEAD + 1]
        qm = jnp.where(hm, q3, 0.0).astype(BF)
        sc = jnp.einsum("nid,njd->nij", qm, k3.astype(BF), preferred_element_type=F32) * jnp.exp(lg_h * dist)[None]
        intra.append(jnp.einsum("nij,nje->nie", sc.astype(BF), v3b, preferred_element_type=F32))
    o = jnp.where(lo, intra[0], intra[1])
    kv = jnp.einsum("njd,nje->nde", (k3 * k_decay[None]).astype(BF), v3b, preferred_element_type=F32)
    kv = jnp.where(same_head[None], kv, 0.0)
    state, states = s_in, []
    for n in range(nc):
        states.append(state)
        state = state * chunk_decay + kv[n]
    s_prev = jnp.stack(states, axis=0)
    o = o + jnp.einsum("nid,nde->nie", (q3 * q_decay[None]).astype(BF), s_prev.astype(BF), preferred_element_type=F32)
    o = o.reshape(tb, LANES)
    mu = _gmean(o, lo)
    oc = o - mu
    y = oc * lax.rsqrt(_gmean(oc * oc, lo) + EPS) * g
    return jax.nn.silu(rg) * y, state


def _xattn_fn(qx, gq, gk, kk, vv):
    q = _rms(qx, gq)
    k = _rms(kk, gk)
    logits = lax.dot_general(q.astype(BF), k.astype(BF), NT, preferred_element_type=F32) * (XHEAD ** -0.5)
    p = jax.nn.softmax(logits, axis=-1)
    return (jnp.dot(p.astype(BF), vv.astype(BF), preferred_element_type=F32),)


def _swiglu_fwd_epi(g, u):
    return g, u, jax.nn.silu(g) * u


def _swiglu_bwd_epi(dact, g, u):
    _, vjp = jax.vjp(lambda a, b: jax.nn.silu(a) * b, g, u)
    return vjp(dact)


def _add_rms_epi(acc, resid, g):
    h = acc + resid
    return h, _rms(h, g)


def _add_loss_epi(acc, resid, target):
    err = (acc + resid) - target
    dy = err * (1.0 / D)
    part = jnp.sum(jnp.sum(err * err, axis=0, keepdims=True), axis=1, keepdims=True) * (0.5 / D)
    return dy, dy, jnp.broadcast_to(part, (1, err.shape[1]))


def _rms_bwd_epi(dhn, h, skip, g):
    _, vjp = jax.vjp(_rms, h, g)
    dh, dg = vjp(dhn)
    dh = dh + skip
    return dh, dh, dg


def _ret_fwd(P, cos, sin, g_ret, lg, T, tb):
    nb = T // tb

    def body(rq, rk, rv, rg, c, s, g, l, o_ref, s0_ref, state):
        @pl.when(pl.program_id(1) == 0)
        def _():
            state[...] = jnp.zeros(state.shape, F32)
        s0_ref[0, 0] = state[...]
        out, s_new = _ret_fn(rq[...], rk[...], rv[...], rg[...], c[...], s[...], state[...], g[...], l[...])
        o_ref[...] = out.astype(o_ref.dtype)
        state[...] = s_new

    sec = lambda off: pl.BlockSpec((tb, LANES), lambda j, i: (i, off + j))
    tab = pl.BlockSpec((tb, LANES), lambda j, i: (i, 0))
    par = pl.BlockSpec((1, LANES), lambda j, i: (0, j))
    return pl.pallas_call(
        body, name="ret_fwd", grid=(4, nb),
        in_specs=[sec(0), sec(4), sec(8), sec(12), tab, tab, par, par],
        out_specs=[pl.BlockSpec((tb, LANES), lambda j, i: (i, j)), pl.BlockSpec((1, 1, LANES, LANES), lambda j, i: (j, i, 0, 0))],
        out_shape=[jax.ShapeDtypeStruct((T, 4 * LANES), BF), jax.ShapeDtypeStruct((4, nb, LANES, LANES), F32)],
        scratch_shapes=[pltpu.VMEM((LANES, LANES), F32)],
        compiler_params=_params(("arbitrary", "arbitrary")),
    )(P, P, P, P, cos, sin, g_ret, lg)


def _ret_bwd(P, cos, sin, g_ret, lg, s0, dmix, T, tb):
    nb = T // tb

    def body(rq, rk, rv, rg, c, s, g, l, s0_ref, do, drq, drk, drv, drg, dg, dstate):
        i = pl.program_id(1)

        @pl.when(i == 0)
        def _():
            dstate[...] = jnp.zeros(dstate.shape, F32)
            dg[...] = jnp.zeros(dg.shape, F32)

        cc, ss, ll = c[...], s[...], l[...]
        _, vjp = jax.vjp(lambda a, b, v, gate, st, gg: _ret_fn(a, b, v, gate, cc, ss, st, gg, ll),
                         rq[...], rk[...], rv[...], rg[...], s0_ref[0, 0], g[...])
        ga, gb, gv, ggate, gst, ggain = vjp((do[...], dstate[...]))
        drq[...] = ga.astype(drq.dtype)
        drk[...] = gb.astype(drk.dtype)
        drv[...] = gv.astype(drv.dtype)
        drg[...] = ggate.astype(drg.dtype)
        dstate[...] = gst
        dg[...] += ggain

    rev = lambda i: nb - 1 - i
    sec = lambda off: pl.BlockSpec((tb, LANES), lambda j, i: (rev(i), off + j))
    tab = pl.BlockSpec((tb, LANES), lambda j, i: (rev(i), 0))
    par = pl.BlockSpec((1, LANES), lambda j, i: (0, j))
    outb = pl.BlockSpec((tb, LANES), lambda j, i: (rev(i), j))
    return pl.pallas_call(
        body, name="ret_bwd", grid=(4, nb),
        in_specs=[sec(0), sec(4), sec(8), sec(12), tab, tab, par, par,
                  pl.BlockSpec((1, 1, LANES, LANES), lambda j, i: (j, rev(i), 0, 0)), outb],
        out_specs=[outb, outb, outb, outb, par],
        out_shape=[jax.ShapeDtypeStruct((T, 4 * LANES), BF)] * 4 + [jax.ShapeDtypeStruct((1, 4 * LANES), F32)],
        scratch_shapes=[pltpu.VMEM((LANES, LANES), F32)],
        compiler_params=_params(("arbitrary", "arbitrary")),
    )(P, P, P, P, cos, sin, g_ret, lg, s0, dmix)


_FB = 128


def _tri(lower):
    r = lax.broadcasted_iota(jnp.int32, (_FB, _FB), 0)
    c = lax.broadcasted_iota(jnp.int32, (_FB, _FB), 1)
    return ((r >= c) if lower else (r <= c)).astype(F32)


def _fgate_fwd(ffp, bpad, T):
    def body(ff_ref, b_ref, fc_ref, fr_ref):
        lane = lax.broadcasted_iota(jnp.int32, (1, LANES), 1)
        tri = _tri(True)
        carry = jnp.zeros((1, LANES), F32)
        for blk in range(T // _FB):
            z = ff_ref[blk * _FB:(blk + 1) * _FB, :] + b_ref[...]
            lf = jnp.where(lane < 8, jax.nn.log_sigmoid(z), 0.0)
            f = jnp.dot(tri, lf, precision=lax.Precision.HIGHEST, preferred_element_type=F32) + carry
            carry = f[_FB - 1:_FB, :]
            fc_ref[blk * _FB:(blk + 1) * _FB, :] = f
            fr_ref[:, blk * _FB:(blk + 1) * _FB] = f.T[:8, :]

    return pl.pallas_call(
        body, name="fgate_fwd",
        out_shape=[jax.ShapeDtypeStruct((T, LANES), F32), jax.ShapeDtypeStruct((8, T), F32)],
        compiler_params=pltpu.CompilerParams(vmem_limit_bytes=VMEM_LIMIT),
    )(ffp, bpad)


_BIAS_LANE = HEAD


def _head_bias_col(fc, head):
    lane = lax.broadcasted_iota(jnp.int32, (1, LANES), 1)
    return jnp.sum(jnp.where(lane == head, fc, 0.0), axis=-1, keepdims=True)


def _split3(f):
    hi = f.astype(BF).astype(F32)
    mid = (f - hi).astype(BF).astype(F32)
    lo = ((f - hi) - mid).astype(BF).astype(F32)
    return hi, mid, lo


def _fox_operands(P, fc, g_fq2, g_fk2, T, tm):
    def body(fq_ref, fk_ref, fv_ref, fc_ref, gq_ref, gk_ref, qa_ref, qat_ref, ka_ref, kat_ref, va_ref, vat_ref):
        j = pl.program_id(0)
        lane = lax.broadcasted_iota(jnp.int32, (1, LANES), 1)
        qn, kn = _fox_prep_fn(fq_ref[...], fk_ref[...], gq_ref[...], gk_ref[...])
        v = fv_ref[...]
        fcb = fc_ref[...]
        b = _BIAS_LANE
        for hh in range(2):
            hi, mid, lo = _split3(_head_bias_col(fcb, 2 * j + hh))
            take = (lambda a: a) if hh == 0 else (lambda a: pltpu.roll(a, HEAD, 1))
            qa = jnp.where(lane < HEAD, take(qn), jnp.where(lane == b, hi, jnp.where(lane == b + 1, mid, jnp.where(
                lane == b + 2, lo, jnp.where(lane < b + 6, 1.0, 0.0)))))
            ka = jnp.where(lane < HEAD, take(kn), jnp.where(lane < b + 3, 1.0, jnp.where(lane == b + 3, -hi, jnp.where(
                lane == b + 4, -mid, jnp.where(lane == b + 5, -lo, 0.0)))))
            va = jnp.where(lane < HEAD, take(v), 0.0)
            for val, ref, tref in ((qa, qa_ref, qat_ref), (ka, ka_ref, kat_ref), (va, va_ref, vat_ref)):
                ref[hh] = val.astype(BF)
                tref[hh] = val.T.astype(BF)

    sec = lambda off: pl.BlockSpec((tm, LANES), lambda j, i: (i, off + j))
    par = pl.BlockSpec((1, LANES), lambda j, i: (0, 0))
    nat = pl.BlockSpec((2, tm, LANES), lambda j, i: (j, i, 0))
    trn = pl.BlockSpec((2, LANES, tm), lambda j, i: (j, 0, i))
    return pl.pallas_call(
        body, name="fox_operands", grid=(4, T // tm),
        in_specs=[sec(16), sec(20), sec(24), pl.BlockSpec((tm, LANES), lambda j, i: (i, 0)), par, par],
        out_specs=[nat, trn, nat, trn, nat, trn],
        out_shape=[jax.ShapeDtypeStruct((8, T, LANES), BF), jax.ShapeDtypeStruct((8, LANES, T), BF)] * 3,
        compiler_params=_params(("parallel", "arbitrary")),
    )(P, P, P, fc, g_fq2, g_fk2)


def _fox_forward(qat, ka, vat, T, tq, tk):
    nq, per = T // tq, tq // tk
    assert per == 2
    RC = 64

    def body(qat_ref, ka_ref, vat_ref, o_ref, lse_ref, s_scr, p_scr, a_scr, m_scr, l_scr, acc_scr):
        i = pl.program_id(1)
        sub = lax.broadcasted_iota(jnp.int32, (8, 1), 0)
        row = lax.broadcasted_iota(jnp.int32, (RC, tq), 0)
        col = lax.broadcasted_iota(jnp.int32, (RC, tq), 1)
        m_scr[...] = jnp.full(m_scr.shape, NEG, F32)
        l_scr[...] = jnp.zeros(l_scr.shape, F32)
        acc_scr[...] = jnp.zeros(acc_scr.shape, F32)

        def scores(slot, kb):
            k0 = pl.multiple_of(kb * tk, tk)
            for hh in range(2):
                s_scr[slot, hh] = jnp.dot(ka_ref[hh, pl.ds(k0, tk), :], qat_ref[hh], preferred_element_type=F32)

        def softmax(slot, kb, diagonal):
            shift = kb * tk - i * tq
            for hh in range(2):
                def masked(r):
                    tile = s_scr[slot, hh, r * RC:(r + 1) * RC, :]
                    return jnp.where(row + (r * RC + shift) <= col, tile, NEG) if diagonal else tile

                mx = jnp.max(masked(0), axis=0, keepdims=True)
                for r in range(1, tk // RC):
                    mx = jnp.maximum(mx, jnp.max(masked(r), axis=0, keepdims=True))
                m_old = m_scr[hh, 0:1, :]
                m2 = jnp.maximum(m_old, mx)
                a = jnp.exp(m_old - m2)
                lsum = jnp.zeros((1, tq), F32)
                for r in range(tk // RC):
                    p = jnp.exp(masked(r) - m2)
                    p_scr[slot, hh, r * RC:(r + 1) * RC, :] = p.astype(BF)
                    lsum = lsum + jnp.sum(p, axis=0, keepdims=True)
                m_scr[hh] = jnp.broadcast_to(m2, (8, tq))
                l_scr[hh] = jnp.broadcast_to(a * l_scr[hh, 0:1, :] + lsum, (8, tq))
                a_scr[slot, hh] = jnp.broadcast_to(a, (8, tq))

        def values(slot, kb):
            k0 = pl.multiple_of(kb * tk, tk)
            for hh in range(2):
                pv = jnp.dot(vat_ref[hh, 0:HEAD, pl.ds(k0, tk)], p_scr[slot, hh], preferred_element_type=F32)
                acc_scr[hh] = a_scr[slot, hh, 0:1, :] * acc_scr[hh] + pv

        def pair(kb, diag_first, diag_second, more):
            if more:
                scores(0, kb + 2)
            softmax(1, kb + 1, diag_first)
            values(0, kb)
            if more:
                scores(1, kb + 3)
                softmax(0, kb + 2, diag_second)
            values(1, kb + 1)

        scores(0, 0)
        scores(1, 1)
        softmax(0, 0, True)

        @pl.loop(0, jnp.maximum(i - 1, 0))
        def _(t):
            pair(2 * t, False, False, True)

        @pl.when(i >= 1)
        def _():
            pair(2 * (i - 1), False, True, True)

        pair(2 * i, True, False, False)

        o_ref[...] = jnp.concatenate([acc_scr[hh] / l_scr[hh, 0:1, :] for hh in range(2)], axis=0).T
        lses = [m_scr[hh, 0:1, :] + jnp.log(l_scr[hh, 0:1, :]) for hh in range(2)]
        lse_ref[0] = jnp.where(sub == 0, lses[0], jnp.where(sub == 1, lses[1], 0.0))

    return pl.pallas_call(
        body, name="fox_forward", grid=(4, nq),
        in_specs=[pl.BlockSpec((2, LANES, tq), lambda j, i: (j, 0, i)), pl.BlockSpec((2, T, LANES), lambda j, i: (j, 0, 0)),
                  pl.BlockSpec((2, LANES, T), lambda j, i: (j, 0, 0))],
        out_specs=[pl.BlockSpec((tq, LANES), lambda j, i: (i, j)), pl.BlockSpec((1, 8, tq), lambda j, i: (j, 0, i))],
        out_shape=[jax.ShapeDtypeStruct((T, 4 * LANES), F32), jax.ShapeDtypeStruct((4, 8, T), F32)],
        scratch_shapes=[pltpu.VMEM((2, 2, tk, tq), F32), pltpu.VMEM((2, 2, tk, tq), BF), pltpu.VMEM((2, 2, 8, tq), F32),
                        pltpu.VMEM((2, 8, tq), F32), pltpu.VMEM((2, 8, tq), F32), pltpu.VMEM((2, HEAD, tq), F32)],
        compiler_params=_params(("parallel", "arbitrary")),
    )(qat, ka, vat)


def _fox_cotangent(dmix, fox, T, tm):
    def body(do_ref, o_ref, doa_ref, doat_ref, dl_ref):
        lane = lax.broadcasted_iota(jnp.int32, (1, LANES), 1)
        sub = lax.broadcasted_iota(jnp.int32, (8, 1), 0)
        dob = do_ref[...].astype(BF).astype(F32)
        prod_t = (dob * o_ref[...]).T
        d0 = jnp.sum(prod_t[:HEAD], axis=0, keepdims=True)
        d1 = jnp.sum(prod_t[HEAD:], axis=0, keepdims=True)
        dl_ref[0] = jnp.where(sub == 0, d0, jnp.where(sub == 1, d1, 0.0))
        for hh in range(2):
            val = jnp.where(lane < HEAD, dob if hh == 0 else pltpu.roll(dob, HEAD, 1), 0.0)
            doa_ref[hh] = val.astype(BF)
            doat_ref[hh] = val.T.astype(BF)

    return pl.pallas_call(
        body, name="fox_cotangent", grid=(4, T // tm),
        in_specs=[pl.BlockSpec((tm, LANES), lambda j, i: (i, 4 + j)), pl.BlockSpec((tm, LANES), lambda j, i: (i, j))],
        out_specs=[pl.BlockSpec((2, tm, LANES), lambda j, i: (j, i, 0)), pl.BlockSpec((2, LANES, tm), lambda j, i: (j, 0, i)),
                   pl.BlockSpec((1, 8, tm), lambda j, i: (j, 0, i))],
        out_shape=[jax.ShapeDtypeStruct((8, T, LANES), BF), jax.ShapeDtypeStruct((8, LANES, T), BF),
                   jax.ShapeDtypeStruct((4, 8, T), F32)],
        compiler_params=_params(("parallel", "arbitrary")),
    )(dmix, fox)


def _fox_backward(qa, qat, ka, kat, va, doa, doat, lse, dl, T, tq, tk):
    nq, nk = T // tq, T // tk

    def body(qa_ref, qat_ref, ka_ref, kat_ref, va_ref, doa_ref, doat_ref, lse_ref, dl_ref,
             dq_ref, dk_ref, dv_ref, df_ref, dr_ref, dqt, dk_acc, dv_acc, df_acc, sdp, pds):
        j, kb = pl.program_id(0), pl.program_id(1)
        lane = lax.broadcasted_iota(jnp.int32, (1, LANES), 1)
        first = (kb * tk) // tq

        @pl.when(kb == 0)
        def _():
            dqt[...] = jnp.zeros(dqt.shape, F32)

        dk_acc[...] = jnp.zeros(dk_acc.shape, F32)
        dv_acc[...] = jnp.zeros(dv_acc.shape, F32)
        df_acc[...] = jnp.zeros(df_acc.shape, F32)

        RC = 64
        last = nq - 1

        def products(slot, qi):
            q0 = pl.multiple_of(qi * tq, tq)
            for hh in range(2):
                sdp[slot, hh, 0] = jnp.dot(ka_ref[hh], qat_ref[hh, :, pl.ds(q0, tq)], preferred_element_type=F32)
                sdp[slot, hh, 1] = jnp.dot(va_ref[hh], doat_ref[hh, :, pl.ds(q0, tq)], preferred_element_type=F32)

        def softmax_bwd(slot, qi, diagonal, valid):
            q0 = pl.multiple_of(qi * tq, tq)
            shift = kb * tk - first * tq
            col = lax.broadcasted_iota(jnp.int32, (RC, tq), 1)
            row = lax.broadcasted_iota(jnp.int32, (RC, tq), 0)
            for hh in range(2):
                lse_row = lse_ref[0, hh:hh + 1, pl.ds(q0, tq)]
                dl_row = dl_ref[0, hh:hh + 1, pl.ds(q0, tq)]
                rsum = jnp.zeros((1, tq), F32)
                for r in range(tk // RC):
                    rows = slice(r * RC, (r + 1) * RC)
                    p = jnp.exp(sdp[slot, hh, 0, rows, :] - lse_row)
                    p = jnp.where((row + (r * RC + shift) <= col) if diagonal else valid, p, 0.0)
                    ds = p * (sdp[slot, hh, 1, rows, :] - dl_row)
                    pds[slot, hh, 0, rows, :] = p.astype(BF)
                    pds[slot, hh, 1, rows, :] = ds.astype(BF)
                    rsum = rsum + jnp.sum(ds, axis=0, keepdims=True)
                    part = ds[:, 0:LANES]
                    for c in range(1, tq // LANES):
                        part = part + ds[:, c * LANES:(c + 1) * LANES]
                    df_acc[hh, rows, :] += part
                dqt[hh, HEAD:HEAD + 8, pl.ds(q0, tq)] += jnp.broadcast_to(rsum, (8, tq))

        def accumulate(slot, qi):
            q0 = pl.multiple_of(qi * tq, tq)
            for hh in range(2):
                dv_acc[hh] += jnp.dot(pds[slot, hh, 0], doa_ref[hh, pl.ds(q0, tq), :], preferred_element_type=F32)
                dk_acc[hh] += jnp.dot(pds[slot, hh, 1], qa_ref[hh, pl.ds(q0, tq), :], preferred_element_type=F32)
                dqt[hh, 0:HEAD, pl.ds(q0, tq)] += jnp.dot(kat_ref[hh, 0:HEAD, :], pds[slot, hh, 1], preferred_element_type=F32)

        products(0, first)
        products(1, jnp.minimum(first + 1, last))
        softmax_bwd(0, first, True, None)

        @pl.loop(0, (nq - first + 1) // 2)
        def _(t):
            qi = first + 2 * t
            products(0, jnp.minimum(qi + 2, last))
            softmax_bwd(1, jnp.minimum(qi + 1, last), False, qi + 1 <= last)
            accumulate(0, qi)
            products(1, jnp.minimum(qi + 3, last))
            softmax_bwd(0, jnp.minimum(qi + 2, last), False, qi + 2 <= last)
            accumulate(1, jnp.minimum(qi + 1, last))

        lo = lane < HEAD
        dk_ref[...] = jnp.where(lo, dk_acc[0], pltpu.roll(dk_acc[1], HEAD, 1))
        dv_ref[...] = jnp.where(lo, dv_acc[0], pltpu.roll(dv_acc[1], HEAD, 1)).astype(dv_ref.dtype)
        f0 = -jnp.sum(df_acc[0], axis=1, keepdims=True)
        f1 = -jnp.sum(df_acc[1], axis=1, keepdims=True)
        df_ref[0] = jnp.where(lane == 2 * j, f0, jnp.where(lane == 2 * j + 1, f1, 0.0))

        @pl.when(kb == nk - 1)
        def _():
            for t in range(nq):
                cols = slice(t * tq, (t + 1) * tq)
                dq_ref[cols, :] = jnp.concatenate([dqt[0, 0:HEAD, cols], dqt[1, 0:HEAD, cols]], axis=0).T
                rsum = jnp.concatenate([dqt[0, HEAD:HEAD + 8, cols], dqt[1, HEAD:HEAD + 8, cols],
                                        jnp.zeros((LANES - 16, tq), F32)], axis=0).T
                dr_ref[0, cols, :] = jnp.where(lane == 2 * j, rsum[:, 0:1], jnp.where(lane == 2 * j + 1, rsum[:, 8:9], 0.0))

    nat_full = pl.BlockSpec((2, T, LANES), lambda j, kb: (j, 0, 0))
    trn_full = pl.BlockSpec((2, LANES, T), lambda j, kb: (j, 0, 0))
    nat_blk = pl.BlockSpec((2, tk, LANES), lambda j, kb: (j, kb, 0))
    trn_blk = pl.BlockSpec((2, LANES, tk), lambda j, kb: (j, 0, kb))
    rows = pl.BlockSpec((1, 8, T), lambda j, kb: (j, 0, 0))
    blk = pl.BlockSpec((tk, LANES), lambda j, kb: (kb, j))
    return pl.pallas_call(
        body, name="fox_backward", grid=(4, nk),
        in_specs=[nat_full, trn_full, nat_blk, trn_blk, nat_blk, nat_full, trn_full, rows, rows],
        out_specs=[pl.BlockSpec((T, LANES), lambda j, kb: (0, j)), blk, blk, pl.BlockSpec((1, tk, LANES), lambda j, kb: (j, kb, 0)),
                   pl.BlockSpec((1, T, LANES), lambda j, kb: (j, 0, 0))],
        out_shape=[jax.ShapeDtypeStruct((T, 4 * LANES), F32), jax.ShapeDtypeStruct((T, 4 * LANES), F32),
                   jax.ShapeDtypeStruct((T, 4 * LANES), BF), jax.ShapeDtypeStruct((4, T, LANES), F32),
                   jax.ShapeDtypeStruct((4, T, LANES), F32)],
        scratch_shapes=[pltpu.VMEM((2, HEAD + 8, T), F32), pltpu.VMEM((2, tk, LANES), F32), pltpu.VMEM((2, tk, LANES), F32),
                        pltpu.VMEM((2, tk, LANES), F32), pltpu.VMEM((2, 2, 2, tk, tq), F32), pltpu.VMEM((2, 2, 2, tk, tq), BF)],
        compiler_params=_params(("arbitrary", "arbitrary")),
    )(qa, qat, ka, kat, va, doa, doat, lse, dl)


def _fgate_bwd_col(ffp, bpad, dfc4, drc4, T):
    def body(ff_ref, b_ref, dfc_ref, drc_ref, dff_ref, db_ref):
        lane = lax.broadcasted_iota(jnp.int32, (1, LANES), 1)
        tri = _tri(False)
        carry = jnp.zeros((1, LANES), F32)
        db = jnp.zeros((1, LANES), F32)
        for blk in reversed(range(T // _FB)):
            rows = slice(blk * _FB, (blk + 1) * _FB)
            dcol = dfc_ref[0, rows, :] + drc_ref[0, rows, :]
            for pair in range(1, 4):
                dcol = dcol + (dfc_ref[pair, rows, :] + drc_ref[pair, rows, :])
            dlf = jnp.dot(tri, dcol, precision=lax.Precision.HIGHEST, preferred_element_type=F32) + carry
            carry = dlf[0:1, :]
            z = ff_ref[blk * _FB:(blk + 1) * _FB, :] + b_ref[...]
            dz = jnp.where(lane < 8, dlf * jax.nn.sigmoid(-z), 0.0)
            dff_ref[blk * _FB:(blk + 1) * _FB, :] = dz.astype(dff_ref.dtype)
            db = db + jnp.sum(dz, axis=0, keepdims=True)
        db_ref[...] = db

    return pl.pallas_call(
        body, name="fgate_bwd",
        out_shape=[jax.ShapeDtypeStruct((T, LANES), BF), jax.ShapeDtypeStruct((1, LANES), F32)],
        compiler_params=pltpu.CompilerParams(vmem_limit_bytes=VMEM_LIMIT),
    )(ffp, bpad, dfc4, drc4)


MESH = pl.DeviceIdType.MESH
N_PEERS = N_DEV - 1


def _place():
    return lax.axis_index("x"), lax.axis_index("y"), lax.axis_index("c")


def _all_gather(shard):
    R, W = shard.shape

    def body(x_ref, out_ref, send_sems, recv_sems, local_sem):
        x, y, c = _place()
        me, sibling = (x, y, c), (x, y, 1 - c)
        chips = [(1 - x, y), (x, 1 - y), (1 - x, 1 - y)]

        def slot(px, py, pc):
            return out_ref.at[4 * px + 2 * py + pc]

        def copy(k, block, to, src=None):
            return pltpu.make_async_remote_copy(
                src_ref=slot(*block) if src is None else src, dst_ref=slot(*block),
                send_sem=send_sems.at[k], recv_sem=recv_sems.at[k], device_id=to, device_id_type=MESH)

        mine = pltpu.make_async_copy(x_ref, slot(*me), local_sem)
        mine.start()
        first = [copy(0, me, sibling, src=x_ref)]
        first += [copy(1 + n, me, (*chip, c), src=x_ref) for n, chip in enumerate(chips)]
        for cp in first:
            cp.start()
        passed = [copy(4 + n, (*chip, c), sibling) for n, chip in enumerate(chips)]
        for n, chip in enumerate(chips):
            copy(1 + n, (*chip, c), me).wait_recv()
            passed[n].start()
        copy(0, sibling, me).wait_recv()
        for n, chip in enumerate(chips):
            copy(4 + n, (*chip, 1 - c), me).wait_recv()
        for cp in first + passed:
            cp.wait_send()
        mine.wait()

    return pl.pallas_call(
        body, name="all_gather_weights",
        out_shape=jax.ShapeDtypeStruct((N_DEV, R, W), shard.dtype),
        in_specs=[pl.BlockSpec(memory_space=pl.ANY)], out_specs=pl.BlockSpec(memory_space=pl.ANY),
        scratch_shapes=[pltpu.SemaphoreType.DMA((N_PEERS,)), pltpu.SemaphoreType.DMA((N_PEERS,)), pltpu.SemaphoreType.DMA],
    )(shard)


def _exchange_copies(src_refs, land_refs, send_sems, recv_sems, scatter):
    x, y, c = _place()
    me = 4 * x + 2 * y + c
    copies = []
    for k, (src_ref, land_ref) in enumerate(zip(src_refs, land_refs)):
        for r in range(1, N_DEV):
            px, py, pc = x ^ (r >> 2), y ^ ((r >> 1) & 1), c ^ (r & 1)
            copies.append(pltpu.make_async_remote_copy(
                src_ref=src_ref.at[4 * px + 2 * py + pc] if scatter else src_ref, dst_ref=land_ref.at[me],
                send_sem=send_sems.at[k * N_PEERS + r - 1], recv_sem=recv_sems.at[k * N_PEERS + r - 1],
                device_id=(px, py, pc), device_id_type=MESH))
    return copies


_HBM = pl.BlockSpec(memory_space=pltpu.HBM)
_SEM = pl.BlockSpec(memory_space=pltpu.SEMAPHORE)
_EFFECT = pltpu.SideEffectType.DATAFLOW_SIDE_EFFECTING


def _exchange_start(name, srcs, lands, scatter):
    n = len(srcs)

    def body(*refs):
        send_sems, recv_sems = refs[2 * n], refs[2 * n + 1]
        for cp in _exchange_copies(refs[:n], refs[n:2 * n], send_sems, recv_sems, scatter):
            cp.start()
        token = refs[-1]
        token[...] = jnp.zeros(token.shape, F32)

    arrays = list(srcs) + list(lands)
    out = pl.pallas_call(
        body, name=name,
        out_shape=(pltpu.SemaphoreType.DMA((n * N_PEERS,)), pltpu.SemaphoreType.DMA((n * N_PEERS,)))
        + tuple(pltpu.HBM(a.shape, a.dtype) for a in arrays) + (jax.ShapeDtypeStruct((8, LANES), F32),),
        in_specs=(_HBM,) * (2 * n), out_specs=(_SEM, _SEM) + (_HBM,) * (2 * n) + (pl.BlockSpec(memory_space=pltpu.VMEM),),
        input_output_aliases={k: 2 + k for k in range(2 * n)},
        compiler_params=pltpu.CompilerParams(has_side_effects=_EFFECT),
    )(*(pltpu.with_memory_space_constraint(a, pltpu.HBM) for a in arrays))
    return out[0], out[1], out[2:2 + n], out[2 + n:2 + 2 * n], out[-1]


def _exchange_wait(name, started, after, scatter):
    send_sems, recv_sems, srcs, lands, _ = started
    n = len(srcs)

    def body(*refs):
        copies = _exchange_copies(refs[:n], refs[n:2 * n], refs[2 * n], refs[2 * n + 1], scatter)
        for cp in copies:
            cp.wait_send()
        for cp in copies:
            cp.wait_recv()

    arrays = list(srcs) + list(lands)
    out = pl.pallas_call(
        body, name=name,
        out_shape=tuple(pltpu.HBM(a.shape, a.dtype) for a in arrays),
        in_specs=(_HBM,) * (2 * n) + (_SEM, _SEM, pl.BlockSpec(memory_space=pl.ANY)), out_specs=(_HBM,) * (2 * n),
        input_output_aliases={k: k for k in range(2 * n)},
        compiler_params=pltpu.CompilerParams(has_side_effects=_EFFECT),
    )(*arrays, send_sems, recv_sems, after)
    return out[:n], out[n:]


def _adam_update(g, w, m, v):
    m2 = ADAM_B1 * m + (1.0 - ADAM_B1) * g
    v2 = ADAM_B2 * v + (1.0 - ADAM_B2) * jnp.square(g)
    m_hat = m2 / (1.0 - ADAM_B1 ** ADAM_STEP)
    v_hat = v2 / (1.0 - ADAM_B2 ** ADAM_STEP)
    return g, -ADAM_LR * (m_hat / (jnp.sqrt(v_hat) + ADAM_EPS) + ADAM_WD * w), m2, v2


def _adamw(name, slots, own, w, m, v):
    R, W = w.shape

    def body(s_ref, *refs):
        if own is not None:
            g = refs[0][...].astype(F32)
            refs = refs[1:]
        else:
            g = jnp.zeros((R, W), F32)
        for s in range(N_DEV):
            g = g + s_ref[s].astype(F32)
        w_ref, m_ref, v_ref = refs[:3]
        for o, r in zip(refs[3:], _adam_update(g, w_ref[...], m_ref[...], v_ref[...])):
            o[...] = r

    full = pl.BlockSpec((R, W), lambda i: (0, 0))
    args = [slots] + ([own] if own is not None else []) + [w, m, v]
    return pl.pallas_call(
        body, name=name, grid=(1,),
        in_specs=[pl.BlockSpec((N_DEV, R, W), lambda i: (0, 0, 0))] + [full] * (len(args) - 1),
        out_specs=[full] * 4, out_shape=[jax.ShapeDtypeStruct((R, W), F32)] * 4,
        compiler_params=_params(("arbitrary",)),
    )(*args)


def _tables(T):
    pos = jnp.arange(T, dtype=F32)
    inv_freq = 10000.0 ** (-jnp.arange(0, HEAD, 2, dtype=F32) / HEAD)
    ang = pos[:, None] * inv_freq[None, :]
    cos, sin = jnp.cos(ang), jnp.sin(ang)
    cos4 = jnp.tile(cos, (1, 4))
    sin4 = jnp.tile(jnp.concatenate([-sin, sin], axis=1), (1, 2))
    log_g = jnp.log(1.0 - 2.0 ** (-5.0 - jnp.arange(8, dtype=F32)))
    return cos4, sin4, jnp.repeat(log_g, HEAD)[None, :]


def _local_step(x, mem, target, sp, w_inT, token, fetch_rest, push, push_small):
    T = x.shape[0]
    tm = min(512, T)
    tq = min(256, T)
    tb = min(1024, T)
    cos4, sin4, lg = _tables(T)
    g_fq2 = jnp.tile(sp["g_fox_q"], (1, 2))
    g_fk2 = jnp.tile(sp["g_fox_k"], (1, 2))
    g_ret = sp["g_ret_out"].reshape(1, 8 * HEAD)
    bpad = jnp.pad(sp["b_forget"], ((0, 0), (0, LANES - 8)))
    w_secs = [w_inT[k * 512:(k + 1) * 512] for k in range(7)]
    w_ffT = jnp.pad(w_inT[3584:3592], ((0, LANES - 8), (0, 0)))
    w_mainT = w_inT[:3584]
    tie = lambda p, tok: p + tok[0:1, 0:1]
    tm2, tm4 = min(1024, T), min(2048, T)

    hn1, = _rw_fwd("rms_mix", _rms_fn, [(x, D, 0, False)], [(tie(sp["g_mix"], token), D, 0, False)], [(BF, D)], T, tm4, 1)
    P, = _mm("proj_in", [[(hn1, w_mainT, "nt")]], [], _ident, T, 3584, tm4, 512, [F32])
    ffp, = _mm("proj_ff", [[(hn1, w_ffT, "nt")]], [], _ident, T, LANES, tm, LANES, [F32])
    ret, s0 = _ret_fwd(P, cos4, sin4, g_ret, lg, T, tb)
    fc, _ = _fgate_fwd(ffp, bpad, T)
    qa, qat, ka, kat, va, vat = _fox_operands(P, fc, g_fq2, g_fk2, T, tm4)
    fox, lse = _fox_forward(qat, ka, vat, T, min(512, T), tq)
    W = fetch_rest(fox)
    w_out_halves = (W["w_out"][:4 * LANES], W["w_out"][4 * LANES:])
    h1, hn2 = _mm("proj_out", [[(ret, w_out_halves[0], "nn"), (fox, w_out_halves[1], "nn")]], [x], _add_rms_epi, T, D, tm2, D,
                  [F32, BF], params=[sp["g_xattn"]])

    qx, = _mm("proj_xq", [[(hn2, W["w_xq"], "nn")]], [], _ident, T, D, tm2, D, [F32])
    memn, = _rw_fwd("rms_mem", _rms_fn, [(mem, D, 0, False)], [(sp["g_mem"], D, 0, False)], [(BF, D)], N_MEM, N_MEM, 1)
    kv, = _mm("proj_xkv", [[(memn, W["w_xkvT"], "nt")]], [], _ident, N_MEM, 2 * D, N_MEM, 512, [F32])
    xa_rows = [(qx, XHEAD, 0, True)]
    xa_params = [(sp["g_xq"], XHEAD, 0, False), (sp["g_xk"], XHEAD, 0, False), (kv, XHEAD, 0, True), (kv, XHEAD, 4, True)]
    xo, = _rw_fwd("xattn_fwd", _xattn_fn, xa_rows, xa_params, [(BF, XHEAD)], T, tm4, 4)
    h2, hn3 = _mm("proj_xo", [[(xo, W["w_xo"], "nn")]], [h1], _add_rms_epi, T, D, tm2, D, [F32, BF], params=[sp["g_ffn"]])

    gate, up, act = _mm("ffn_in", [[(hn3, W["w_gateT"], "nt")], [(hn3, W["w_upT"], "nt")]], [], _swiglu_fwd_epi,
                        T, D_FF, tm4, 256, [BF, BF, BF])
    dy, dyb, loss_part = _mm("ffn_out", [[(act, W["w_down"], "nn")]], [h2, target], _add_loss_epi, T, D, tm, D, [F32, BF], n_acc=1)

    dgate, dup = _mm("ffn_out_bwd", [[(dyb, W["w_down"], "nt")]], [gate, up], _swiglu_bwd_epi, T, D_FF, tm4, 256, [BF, BF])
    gW = {}
    gW["w_gateT"], gW["w_upT"] = _mm("dw_gate_up", [[(dgate, hn3, "tn")], [(dup, hn3, "tn")]], [], _each, D_FF, D, 256, D, [BF, BF])
    gW["w_down"], = _mm("dw_down", [[(act, dyb, "tn")]], [], _ident, D_FF, D, 256, D, [BF])
    tok = push("ffn", gW)
    gs = {}
    dh2, dh2b, gs["g_ffn"] = _mm("ffn_in_bwd", [[(dgate, W["w_gateT"], "nn"), (dup, W["w_upT"], "nn")]], [h2, dy], _rms_bwd_epi,
                                 T, D, min(256, T), D, [F32, BF], params=[tie(sp["g_ffn"], tok)], n_acc=1)

    dxo, = _mm("proj_xo_bwd", [[(dh2b, W["w_xo"], "nt")]], [], _ident, T, D, tm2, D, [BF])
    gW["w_xo"], = _mm("dw_xo", [[(xo, dh2b, "tn")]], [], _ident, D, D, 256, D, [BF])
    dqx, gs["g_xq"], gs["g_xk"], dkv_k, dkv_v = _rw_bwd(
        "xattn_bwd", _xattn_fn, xa_rows, xa_params, [(dxo, XHEAD, 0, True)], T, tm4, 4, [BF], [True, True, True, True])
    dkv = jnp.concatenate([dkv_k[:, :D], dkv_v[:, D:]], axis=1)
    gW["w_xq"], = _mm("dw_xq", [[(hn2, dqx, "tn")]], [], _ident, D, D, 256, D, [BF])
    dmemn, = _mm("proj_xkv_bwd", [[(dkv, W["w_xkvT"], "nn")]], [], _ident, N_MEM, D, N_MEM, 512, [F32])
    gW["w_xkvT"], = _mm("dw_xkv", [[(dkv, memn, "tn")]], [], _ident, 2 * D, D, 512, D, [BF])
    tok = push("xattn", gW)
    gs["g_mem"], = _rw_bwd("rms_mem_bwd", _rms_fn, [(mem, D, 0, False)], [(sp["g_mem"], D, 0, False)], [(dmemn, D, 0, False)],
                           N_MEM, N_MEM, 1, [None], [True])
    dh1, dh1b, gs["g_xattn"] = _mm("proj_xq_bwd", [[(dqx, W["w_xq"], "nt")]], [h1, dh2], _rms_bwd_epi, T, D, tm, D, [F32, BF],
                                   params=[tie(sp["g_xattn"], tok)], n_acc=1)

    dmix, = _mm("proj_out_bwd", [[(dh1b, W["w_out"], "nt")]], [], _ident, T, D, tm2, D, [F32])
    gW["w_out"] = jnp.concatenate(_mm("dw_out", [[(ret, dh1b, "tn")], [(fox, dh1b, "tn")]], [], _each, 4 * LANES, D, 256, D,
                                      [BF, BF]), axis=0)
    tok = push("out", gW)
    doa, doat, dl = _fox_cotangent(dmix, fox, T, tm4)
    dqn, dkn, dfv, dfc4, drc4 = _fox_backward(qa, qat, ka, kat, va, doa, doat, lse + tok[0:1, 0:1], dl, T, tq, tq)
    dfq, dfk, gq2, gk2 = _rw_bwd("fox_prep_bwd", _fox_prep_fn, [(P, LANES, 16, True), (P, LANES, 20, True)],
                                 [(g_fq2, LANES, 0, False), (g_fk2, LANES, 0, False)],
                                 [(dqn, LANES, 0, True), (dkn, LANES, 0, True)], T, tm4, 4, [BF, BF], [True, True])
    gs["g_fox_q"] = gq2[:, :HEAD] + gq2[:, HEAD:]
    gs["g_fox_k"] = gk2[:, :HEAD] + gk2[:, HEAD:]
    dff, dbp = _fgate_bwd_col(ffp, bpad, dfc4, drc4, T)
    gs["b_forget"] = dbp[:, :8]
    drq, drk, drv, drg, dg_ret = _ret_bwd(P, cos4, sin4, g_ret, lg, s0, dmix, T, tb)
    gs["g_ret_out"] = dg_ret
    dsecs = [drq, drk, drv, drg, dfq, dfk, dfv]
    g_secs = list(_mm("dw_in", [[(d, hn1, "tn")] for d in dsecs], [], _each, 512, D, LANES, D, [BF] * len(dsecs)))
    g_ff, = _mm("dw_in_ff", [[(dff, hn1, "tn")]], [], _ident, LANES, D, LANES, D, [BF])
    gW["w_inT"] = jnp.concatenate(g_secs + [g_ff[:8]], axis=0)
    tok = push("in", gW)
    grad_x, _, gs["g_mix"] = _mm("proj_in_bwd", [[(d, w, "nn") for d, w in zip(dsecs, w_secs)] + [(dff, w_ffT, "nn")]], [x, dh1],
                                 _rms_bwd_epi, T, D, tm, D, [F32, BF], params=[tie(sp["g_mix"], tok)], n_acc=1)
    return grad_x, push_small(gs, loss_part)


_CANON = {"w_in": "w_inT", "w_xkv": "w_xkvT", "w_gate": "w_gateT", "w_up": "w_upT"}
_SMALL = (("g_mix", 0, 0, 1024), ("g_xattn", 1, 0, 1024), ("g_mem", 2, 0, 1024), ("g_ffn", 3, 0, 1024),
          ("g_ret_out", 4, 0, 512), ("g_xq", 4, 512, 256), ("g_xk", 4, 768, 256),
          ("g_fox_q", 5, 0, 64), ("g_fox_k", 5, 64, 64), ("b_forget", 5, 128, 8))
_LOSS_AT = (5, 256)


def _pack_small(tree):
    buf = jnp.zeros((SMALL_ROWS, D), F32)
    for name, r, c, n in _SMALL:
        buf = lax.dynamic_update_slice(buf, tree[name].reshape(1, n).astype(F32), (r, c))
    return buf


def _unpack_small(buf, like):
    return {name: buf[r:r + 1, c:c + n].reshape(like[name].shape) for name, r, c, n in _SMALL}


def _canonical(tree, name):
    a = tree[name][0]
    return a.T if W_SHARD[name][1] else a


def _from_canonical(a, name):
    return (a.T if W_SHARD[name][1] else a)[None]


def kernel(x, mem, g_mix, w_in, b_forget, g_ret_out, g_fox_q, g_fox_k, w_out, g_xattn, w_xq, w_xkv, g_mem, g_xq, g_xk, w_xo, g_ffn, w_gate, w_up, w_down, loss_target, m_g_mix, m_w_in, m_b_forget, m_g_ret_out, m_g_fox_q, m_g_fox_k, m_w_out, m_g_xattn, m_w_xq, m_w_xkv, m_g_mem, m_g_xq, m_g_xk, m_w_xo, m_g_ffn, m_w_gate, m_w_up, m_w_down, v_g_mix, v_w_in, v_b_forget, v_g_ret_out, v_g_fox_q, v_g_fox_k, v_w_out, v_g_xattn, v_w_xq, v_w_xkv, v_g_mem, v_g_xq, v_g_xk, v_w_xo, v_g_ffn, v_w_gate, v_w_up, v_w_down):
    names = ("g_mix", "w_in", "b_forget", "g_ret_out", "g_fox_q", "g_fox_k", "w_out", "g_xattn", "w_xq", "w_xkv", "g_mem",
             "g_xq", "g_xk", "w_xo", "g_ffn", "w_gate", "w_up", "w_down")
    w = dict(zip(names, (g_mix, w_in, b_forget, g_ret_out, g_fox_q, g_fox_k, w_out, g_xattn, w_xq, w_xkv, g_mem, g_xq, g_xk,
                         w_xo, g_ffn, w_gate, w_up, w_down)))
    m = dict(zip(names, (m_g_mix, m_w_in, m_b_forget, m_g_ret_out, m_g_fox_q, m_g_fox_k, m_w_out, m_g_xattn, m_w_xq, m_w_xkv,
                         m_g_mem, m_g_xq, m_g_xk, m_w_xo, m_g_ffn, m_w_gate, m_w_up, m_w_down)))
    v = dict(zip(names, (v_g_mix, v_w_in, v_b_forget, v_g_ret_out, v_g_fox_q, v_g_fox_k, v_w_out, v_g_xattn, v_w_xq, v_w_xkv,
                         v_g_mem, v_g_xq, v_g_xk, v_w_xo, v_g_ffn, v_w_gate, v_w_up, v_w_down)))
    small_names = [s[0] for s in _SMALL]
    me = 4 * lax.axis_index("x") + 2 * lax.axis_index("y") + lax.axis_index("c")

    first = _all_gather(_canonical(w, "w_in").astype(BF))
    first, rest = lax.optimization_barrier((first, [_canonical(w, n).astype(BF) for n in GATHER_REST]))
    rest_started = _exchange_start("gather_rest_start", rest, [jnp.broadcast_to(a[None], (N_DEV,) + a.shape) for a in rest],
                                   scatter=False)

    def fetch_rest(after):
        lands = _exchange_wait("gather_rest_wait", rest_started, after, scatter=False)[1]
        return {_CANON.get(n, n): a.reshape(N_DEV * a.shape[1], D) for n, a in zip(GATHER_REST, lands)}

    pushed = {}

    def push(group, grads):
        srcs = [grads[_CANON.get(n, n)].reshape(N_DEV, W_SHARD[n][0], D) for n in SCATTER_GROUPS[group]]
        pushed[group] = _exchange_start("scatter_%s_start" % group, srcs, [jnp.zeros(a.shape, BF) for a in srcs], scatter=True)
        return pushed[group][4]

    def push_small(gs, loss_part):
        small = lax.dynamic_update_slice(_pack_small(gs), loss_part[:, :1], _LOSS_AT)
        pushed["small"] = _exchange_start("gather_small_start", [small], [jnp.broadcast_to(small[None], (N_DEV,) + small.shape)],
                                          scatter=False)
        return pushed["small"][4]

    sp = {n: w[n].reshape(1, -1) for n in small_names}
    grad_x, done = _local_step(x[0], mem[0], loss_target[0], sp, first.reshape(N_DEV * W_SHARD["w_in"][0], D),
                               rest_started[4], fetch_rest, push, push_small)

    results, after = {}, done
    for group in ("ffn", "xattn", "out", "small", "in"):
        if group == "small":
            recv_small = _exchange_wait("gather_small_wait", pushed["small"], after, scatter=False)[1][0]
            g_sm, d_sm, m_sm, v_sm = _adamw("adamw_small", recv_small, None, _pack_small(w), _pack_small(m), _pack_small(v))
            after = g_sm
            continue
        sents, recvs = _exchange_wait("scatter_%s_wait" % group, pushed[group], after, scatter=True)
        for name, sent, recv in zip(SCATTER_GROUPS[group], sents, recvs):
            own = lax.dynamic_index_in_dim(sent, me, axis=0, keepdims=False)
            res = _adamw("adamw_" + name, recv, own, *(_canonical(t, name) for t in (w, m, v)))
            results[name] = [_from_canonical(r, name) for r in res]
        after = results[SCATTER_GROUPS[group][-1]][0]
    loss = g_sm[_LOSS_AT[0], _LOSS_AT[1]]

    outs = []
    for k, sm in enumerate((g_sm, d_sm, m_sm, v_sm)):
        tree = _unpack_small(sm, w)
        tree.update({name: res[k] for name, res in results.items()})
        outs += [tree[n] for n in names]
    return (loss, grad_x[None], *outs)
```

```python
import jax
import jax.numpy as jnp
from jax import lax
from jax.experimental import pallas as pl
from jax.experimental.pallas import tpu as pltpu

F32 = jnp.float32
BF = jnp.bfloat16

D = 1024
HEAD = 64
CHUNK = 64
N_MEM = 256
XHEAD = 256
D_FF = 2816
EPS = 1e-6
NEG = -1e30
LANES = 128
N_DEV = 8
V7X_VMEM_BYTES = 64 * 1024 * 1024
VMEM_LIMIT = V7X_VMEM_BYTES - 8 * 1024 * 1024

ADAM_LR, ADAM_B1, ADAM_B2, ADAM_EPS, ADAM_WD, ADAM_STEP = 0.001, 0.9, 0.999, 1e-08, 0.01, 10

W_SHARD = {"w_in": (449, True), "w_out": (128, False), "w_xq": (128, False), "w_xkv": (256, True),
           "w_xo": (128, False), "w_gate": (352, True), "w_up": (352, True), "w_down": (352, False)}
GATHER_REST = ("w_out", "w_xq", "w_xkv", "w_xo", "w_gate", "w_up", "w_down")
SCATTER_GROUPS = {"ffn": ("w_gate", "w_up", "w_down"), "xattn": ("w_xq", "w_xo", "w_xkv"), "out": ("w_out",), "in": ("w_in",)}
SMALL_ROWS = 8

NT = (((1,), (1,)), ((), ()))
NN = (((1,), (0,)), ((), ()))
TN = (((0,), (0,)), ((), ()))
_DIMS = {"nn": NN, "nt": NT, "tn": TN}


def _params(sem):
    return pltpu.CompilerParams(dimension_semantics=sem, vmem_limit_bytes=VMEM_LIMIT)


def _mm(name, products, extras, epilogue, M, N, tm, tn, out_dtypes, params=(), n_acc=0):
    assert n_acc == 0 or tn == N
    flat = [t for p in products for t in p]
    counts = [len(p) for p in products]
    in_specs, args, where, slots = [], [], {}, []

    def operand(arr, spec, kind):
        key = (id(arr), kind)
        if key not in where:
            where[key] = len(args)
            args.append(arr)
            in_specs.append(spec)
        return where[key]

    for a, b, form in flat:
        if form == "tn":
            ia = operand(a, pl.BlockSpec((a.shape[0], tm), lambda i, j: (0, i)), "a_tn")
        else:
            ia = operand(a, pl.BlockSpec((tm, a.shape[1]), lambda i, j: (i, 0)), "a")
        if form == "nt":
            ib = operand(b, pl.BlockSpec((tn, b.shape[1]), lambda i, j: (j, 0)), "b_nt")
        else:
            ib = operand(b, pl.BlockSpec((b.shape[0], tn), lambda i, j: (0, j)), "b")
        slots.append((ia, ib))
    n_mm = len(args)
    for e in extras:
        in_specs.append(pl.BlockSpec((tm, tn), lambda i, j: (i, j)))
        args.append(e)
    for p in params:
        in_specs.append(pl.BlockSpec((1, tn), lambda i, j: (0, j)))
        args.append(p)
    n_in = len(args)
    n_out = len(out_dtypes)

    def body(*refs):
        ins, outs = refs[:n_in], refs[n_in:]
        prods, p = [], 0
        for c in counts:
            acc = None
            for _ in range(c):
                a = ins[slots[p][0]][...].astype(BF)
                b = ins[slots[p][1]][...].astype(BF)
                d = lax.dot_general(a, b, _DIMS[flat[p][2]], preferred_element_type=F32)
                acc = d if acc is None else acc + d
                p += 1
            prods.append(acc)
        ex = [r[...].astype(F32) for r in ins[n_mm:]]
        res = epilogue(*prods, *ex)
        for o, r in zip(outs[:n_out], res[:n_out]):
            o[...] = r.astype(o.dtype)
        for o, r in zip(outs[n_out:], res[n_out:]):
            @pl.when(pl.program_id(0) == 0)
            def _(o=o):
                o[...] = jnp.zeros(o.shape, F32)
            o[...] += r

    return pl.pallas_call(
        body, name=name, grid=(M // tm, N // tn), in_specs=in_specs,
        out_specs=[pl.BlockSpec((tm, tn), lambda i, j: (i, j)) for _ in out_dtypes]
        + [pl.BlockSpec((1, tn), lambda i, j: (0, j)) for _ in range(n_acc)],
        out_shape=[jax.ShapeDtypeStruct((M, N), dt) for dt in out_dtypes] + [jax.ShapeDtypeStruct((1, N), F32)] * n_acc,
        compiler_params=_params(("arbitrary", "arbitrary")),
    )(*args)


def _ident(x):
    return (x,)


def _each(*xs):
    return xs


def _spec(rows, w, off, per_j):
    if per_j:
        return pl.BlockSpec((rows, w), lambda j, i: (i, off + j))
    return pl.BlockSpec((rows, w), lambda j, i: (i, off))


def _pspec(rows, w, off, per_j):
    if per_j:
        return pl.BlockSpec((rows, w), lambda j, i: (0, off + j))
    return pl.BlockSpec((rows, w), lambda j, i: (0, off))


def _rw_fwd(name, fn, rows, params, outs, T, tm, nj, n_acc=0):
    in_specs = [_spec(tm, w, off, pj) for _, w, off, pj in rows] + [_pspec(a.shape[0], w, off, pj) for a, w, off, pj in params]
    args = [r[0] for r in rows] + [p[0] for p in params]
    n_in, n_out = len(args), len(outs)
    out_specs = [pl.BlockSpec((tm, w), lambda j, i: (i, j)) for _, w in outs]
    out_shape = [jax.ShapeDtypeStruct((T, nj * w), dt) for dt, w in outs]
    out_specs += [pl.BlockSpec((1, LANES), lambda j, i: (0, 0)) for _ in range(n_acc)]
    out_shape += [jax.ShapeDtypeStruct((1, LANES), F32) for _ in range(n_acc)]

    def body(*refs):
        vals = [r[...].astype(F32) for r in refs[:n_in]]
        res = fn(*vals)
        orefs = refs[n_in:]
        for k in range(n_out):
            orefs[k][...] = res[k].astype(orefs[k].dtype)
        first = (pl.program_id(0) == 0) & (pl.program_id(1) == 0)
        for k in range(n_acc):
            @pl.when(first)
            def _(k=k):
                orefs[n_out + k][...] = jnp.zeros((1, LANES), F32)
            orefs[n_out + k][...] += res[n_out + k]

    return pl.pallas_call(
        body, name=name, grid=(nj, T // tm), in_specs=in_specs, out_specs=out_specs, out_shape=out_shape,
        compiler_params=_params(("arbitrary", "arbitrary")),
    )(*args)


def _rw_bwd(name, fn, rows, params, cots, T, tm, nj, row_grads, param_grads, resid=None):
    in_specs = ([_spec(tm, w, off, pj) for _, w, off, pj in rows] + [_pspec(a.shape[0], w, off, pj) for a, w, off, pj in params]
                + [_spec(tm, w, off, pj) for _, w, off, pj in cots])
    args = [r[0] for r in rows] + [p[0] for p in params] + [c[0] for c in cots]
    if resid is not None:
        in_specs.append(_spec(tm, rows[0][1], rows[0][2], rows[0][3]))
        args.append(resid)
    nr, npar, nc = len(rows), len(params), len(cots)
    out_specs, out_shape, kinds = [], [], []
    for k, dts in enumerate(row_grads):
        for dt in (dts if isinstance(dts, (list, tuple)) else [dts]):
            if dt is not None:
                w = rows[k][1]
                out_specs.append(pl.BlockSpec((tm, w), lambda j, i: (i, j)))
                out_shape.append(jax.ShapeDtypeStruct((T, nj * w), dt))
                kinds.append(("row", k))
    for k, need in enumerate(param_grads):
        if need:
            a, w, off, pj = params[k]
            out_specs.append(_pspec(a.shape[0], w, off, pj))
            out_shape.append(jax.ShapeDtypeStruct(a.shape, F32))
            kinds.append(("par", k))

    def body(*refs):
        vals = [r[...].astype(F32) for r in refs[:nr + npar]]
        ct = tuple(r[...].astype(F32) for r in refs[nr + npar:nr + npar + nc])
        _, vjp = jax.vjp(lambda *a: tuple(fn(*a)), *vals)
        grads = list(vjp(ct))
        n_in = nr + npar + nc + (resid is not None)
        if resid is not None:
            grads[0] = grads[0] + refs[n_in - 1][...].astype(F32)
        orefs = refs[n_in:]
        j, i = pl.program_id(0), pl.program_id(1)
        for o, (kind, k) in zip(orefs, kinds):
            if kind == "row":
                o[...] = grads[k].astype(o.dtype)
            else:
                first = (i == 0) if params[k][3] else ((i == 0) & (j == 0))

                @pl.when(first)
                def _(o=o):
                    o[...] = jnp.zeros(o.shape, F32)
                o[...] += grads[nr + k]

    return pl.pallas_call(
        body, name=name, grid=(nj, T // tm), in_specs=in_specs, out_specs=out_specs, out_shape=out_shape,
        compiler_params=_params(("arbitrary", "arbitrary")),
    )(*args)


def _rms(x, g):
    return x * lax.rsqrt(jnp.mean(x * x, axis=-1, keepdims=True) + EPS) * g


def _rms_fn(x, g):
    return (_rms(x, g),)


def _lo_mask():
    return lax.broadcasted_iota(jnp.int32, (1, LANES), 1) < HEAD


def _gmean(x, lo):
    s0 = jnp.sum(jnp.where(lo, x, 0.0), axis=-1, keepdims=True)
    s1 = jnp.sum(jnp.where(lo, 0.0, x), axis=-1, keepdims=True)
    return jnp.where(lo, s0, s1) * (1.0 / HEAD)


def _fox_prep_fn(fq, fk, gq, gk):
    lo = _lo_mask()
    qn = fq * lax.rsqrt(_gmean(fq * fq, lo) + EPS) * gq * (HEAD ** -0.5)
    kn = fk * lax.rsqrt(_gmean(fk * fk, lo) + EPS) * gk
    return qn, kn


@jax.custom_vjp
def _swap_halves(x):
    bit = (lax.broadcasted_iota(jnp.int32, (1, LANES), 1) & (HEAD // 2)) == 0
    return jnp.where(bit, pltpu.roll(x, LANES - HEAD // 2, 1), pltpu.roll(x, HEAD // 2, 1))


_swap_halves.defvjp(lambda x: (_swap_halves(x), None), lambda _, g: (_swap_halves(g),))


def _ret_fn(rq, rk, rv, rg, cos, sin, s_in, g, lg):
    tb = rq.shape[0]
    nc = tb // CHUNK
    lo = _lo_mask()
    row = lax.broadcasted_iota(jnp.int32, (LANES, 1), 0) < HEAD
    same_head = row == lo
    q = (rq * cos + _swap_halves(rq) * sin) * (HEAD ** -0.5)
    k = rk * cos + _swap_halves(rk) * sin
    q3, k3, v3 = q.reshape(nc, CHUNK, LANES), k.reshape(nc, CHUNK, LANES), rv.reshape(nc, CHUNK, LANES)
    pos = lax.broadcasted_iota(jnp.int32, (CHUNK, 1), 0).astype(F32)
    q_decay = jnp.exp(lg * (pos + 1.0))
    k_decay = jnp.exp(lg * (CHUNK - 1.0 - pos))
    chunk_decay = jnp.exp(lg * float(CHUNK))
    dist = jnp.abs(lax.broadcasted_iota(jnp.int32, (CHUNK, CHUNK), 0) - lax.broadcasted_iota(jnp.int32, (CHUNK, CHUNK), 1)).astype(F32)
    v3b = v3.astype(BF)
    intra = []
    for hh in range(2):
        hm = lo if hh == 0 else ~lo
        lg_h = lg[:, hh * HEAD:hh * HEAD + 1]
        qm = jnp.where(hm, q3, 0.0).astype(BF)
        sc = jnp.einsum("nid,njd->nij", qm, k3.astype(BF), preferred_element_type=F32) * jnp.exp(lg_h * dist)[None]
        intra.append(jnp.einsum("nij,nje->nie", sc.astype(BF), v3b, preferred_element_type=F32))
    o = jnp.where(lo, intra[0], intra[1])
    kv = jnp.einsum("njd,nje->nde", (k3 * k_decay[None]).astype(BF), v3b, preferred_element_type=F32)
    kv = jnp.where(same_head[None], kv, 0.0)
    state, states = s_in, []
    for n in range(nc):
        states.append(state)
        state = state * chunk_decay + kv[n]
    s_prev = jnp.stack(states, axis=0)
    o = o + jnp.einsum("nid,nde->nie", (q3 * q_decay[None]).astype(BF), s_prev.astype(BF), preferred_element_type=F32)
    o = o.reshape(tb, LANES)
    mu = _gmean(o, lo)
    oc = o - mu
    y = oc * lax.rsqrt(_gmean(oc * oc, lo) + EPS) * g
    return jax.nn.silu(rg) * y, state


def _xattn_fn(qx, gq, gk, kk, vv):
    q = _rms(qx, gq)
    k = _rms(kk, gk)
    logits = lax.dot_general(q.astype(BF), k.astype(BF), NT, preferred_element_type=F32) * (XHEAD ** -0.5)
    p = jax.nn.softmax(logits, axis=-1)
    return (jnp.dot(p.astype(BF), vv.astype(BF), preferred_element_type=F32),)


def _swiglu_fwd_epi(g, u):
    return g, u, jax.nn.silu(g) * u


def _swiglu_bwd_epi(dact, g, u):
    _, vjp = jax.vjp(lambda a, b: jax.nn.silu(a) * b, g, u)
    return vjp(dact)


def _add_rms_epi(acc, resid, g):
    h = acc + resid
    return h, _rms(h, g)


def _add_loss_epi(acc, resid, target):
    err = (acc + resid) - target
    dy = err * (1.0 / D)
    part = jnp.sum(jnp.sum(err * err, axis=0, keepdims=True), axis=1, keepdims=True) * (0.5 / D)
    return dy, dy, jnp.broadcast_to(part, (1, err.shape[1]))


def _rms_bwd_epi(dhn, h, skip, g):
    _, vjp = jax.vjp(_rms, h, g)
    dh, dg = vjp(dhn)
    dh = dh + skip
    return dh, dh, dg


def _ret_fwd(P, cos, sin, g_ret, lg, T, tb):
    nb = T // tb

    def body(rq, rk, rv, rg, c, s, g, l, o_ref, s0_ref, state):
        @pl.when(pl.program_id(1) == 0)
        def _():
            state[...] = jnp.zeros(state.shape, F32)
        s0_ref[0, 0] = state[...]
        out, s_new = _ret_fn(rq[...], rk[...], rv[...], rg[...], c[...], s[...], state[...], g[...], l[...])
        o_ref[...] = out.astype(o_ref.dtype)
        state[...] = s_new

    sec = lambda off: pl.BlockSpec((tb, LANES), lambda j, i: (i, off + j))
    tab = pl.BlockSpec((tb, LANES), lambda j, i: (i, 0))
    par = pl.BlockSpec((1, LANES), lambda j, i: (0, j))
    return pl.pallas_call(
        body, name="ret_fwd", grid=(4, nb),
        in_specs=[sec(0), sec(4), sec(8), sec(12), tab, tab, par, par],
        out_specs=[pl.BlockSpec((tb, LANES), lambda j, i: (i, j)), pl.BlockSpec((1, 1, LANES, LANES), lambda j, i: (j, i, 0, 0))],
        out_shape=[jax.ShapeDtypeStruct((T, 4 * LANES), BF), jax.ShapeDtypeStruct((4, nb, LANES, LANES), F32)],
        scratch_shapes=[pltpu.VMEM((LANES, LANES), F32)],
        compiler_params=_params(("arbitrary", "arbitrary")),
    )(P, P, P, P, cos, sin, g_ret, lg)


def _ret_bwd(P, cos, sin, g_ret, lg, s0, dmix, T, tb):
    nb = T // tb

    def body(rq, rk, rv, rg, c, s, g, l, s0_ref, do, drq, drk, drv, drg, dg, dstate):
        i = pl.program_id(1)

        @pl.when(i == 0)
        def _():
            dstate[...] = jnp.zeros(dstate.shape, F32)
            dg[...] = jnp.zeros(dg.shape, F32)

        cc, ss, ll = c[...], s[...], l[...]
        _, vjp = jax.vjp(lambda a, b, v, gate, st, gg: _ret_fn(a, b, v, gate, cc, ss, st, gg, ll),
                         rq[...], rk[...], rv[...], rg[...], s0_ref[0, 0], g[...])
        ga, gb, gv, ggate, gst, ggain = vjp((do[...], dstate[...]))
        drq[...] = ga.astype(drq.dtype)
        drk[...] = gb.astype(drk.dtype)
        drv[...] = gv.astype(drv.dtype)
        drg[...] = ggate.astype(drg.dtype)
        dstate[...] = gst
        dg[...] += ggain

    rev = lambda i: nb - 1 - i
    sec = lambda off: pl.BlockSpec((tb, LANES), lambda j, i: (rev(i), off + j))
    tab = pl.BlockSpec((tb, LANES), lambda j, i: (rev(i), 0))
    par = pl.BlockSpec((1, LANES), lambda j, i: (0, j))
    outb = pl.BlockSpec((tb, LANES), lambda j, i: (rev(i), j))
    return pl.pallas_call(
        body, name="ret_bwd", grid=(4, nb),
        in_specs=[sec(0), sec(4), sec(8), sec(12), tab, tab, par, par,
                  pl.BlockSpec((1, 1, LANES, LANES), lambda j, i: (j, rev(i), 0, 0)), outb],
        out_specs=[outb, outb, outb, outb, par],
        out_shape=[jax.ShapeDtypeStruct((T, 4 * LANES), BF)] * 4 + [jax.ShapeDtypeStruct((1, 4 * LANES), F32)],
        scratch_shapes=[pltpu.VMEM((LANES, LANES), F32)],
        compiler_params=_params(("arbitrary", "arbitrary")),
    )(P, P, P, P, cos, sin, g_ret, lg, s0, dmix)


_FB = 128


def _tri(lower):
    r = lax.broadcasted_iota(jnp.int32, (_FB, _FB), 0)
    c = lax.broadcasted_iota(jnp.int32, (_FB, _FB), 1)
    return ((r >= c) if lower else (r <= c)).astype(F32)


def _fgate_fwd(ffp, bpad, T):
    def body(ff_ref, b_ref, fc_ref, fr_ref):
        lane = lax.broadcasted_iota(jnp.int32, (1, LANES), 1)
        tri = _tri(True)
        carry = jnp.zeros((1, LANES), F32)
        for blk in range(T // _FB):
            z = ff_ref[blk * _FB:(blk + 1) * _FB, :] + b_ref[...]
            lf = jnp.where(lane < 8, jax.nn.log_sigmoid(z), 0.0)
            f = jnp.dot(tri, lf, precision=lax.Precision.HIGHEST, preferred_element_type=F32) + carry
            carry = f[_FB - 1:_FB, :]
            fc_ref[blk * _FB:(blk + 1) * _FB, :] = f
            fr_ref[:, blk * _FB:(blk + 1) * _FB] = f.T[:8, :]

    return pl.pallas_call(
        body, name="fgate_fwd",
        out_shape=[jax.ShapeDtypeStruct((T, LANES), F32), jax.ShapeDtypeStruct((8, T), F32)],
        compiler_params=pltpu.CompilerParams(vmem_limit_bytes=VMEM_LIMIT),
    )(ffp, bpad)


_BIAS_LANE = HEAD


def _head_bias_col(fc, head):
    lane = lax.broadcasted_iota(jnp.int32, (1, LANES), 1)
    return jnp.sum(jnp.where(lane == head, fc, 0.0), axis=-1, keepdims=True)


def _split3(f):
    hi = f.astype(BF).astype(F32)
    mid = (f - hi).astype(BF).astype(F32)
    lo = ((f - hi) - mid).astype(BF).astype(F32)
    return hi, mid, lo


def _fox_operands(P, fc, g_fq2, g_fk2, T, tm):
    def body(fq_ref, fk_ref, fv_ref, fc_ref, gq_ref, gk_ref, qa_ref, qat_ref, ka_ref, kat_ref, va_ref, vat_ref):
        j = pl.program_id(0)
        lane = lax.broadcasted_iota(jnp.int32, (1, LANES), 1)
        qn, kn = _fox_prep_fn(fq_ref[...], fk_ref[...], gq_ref[...], gk_ref[...])
        v = fv_ref[...]
        fcb = fc_ref[...]
        b = _BIAS_LANE
        for hh in range(2):
            hi, mid, lo = _split3(_head_bias_col(fcb, 2 * j + hh))
            take = (lambda a: a) if hh == 0 else (lambda a: pltpu.roll(a, HEAD, 1))
            qa = jnp.where(lane < HEAD, take(qn), jnp.where(lane == b, hi, jnp.where(lane == b + 1, mid, jnp.where(
                lane == b + 2, lo, jnp.where(lane < b + 6, 1.0, 0.0)))))
            ka = jnp.where(lane < HEAD, take(kn), jnp.where(lane < b + 3, 1.0, jnp.where(lane == b + 3, -hi, jnp.where(
                lane == b + 4, -mid, jnp.where(lane == b + 5, -lo, 0.0)))))
            va = jnp.where(lane < HEAD, take(v), 0.0)
            for val, ref, tref in ((qa, qa_ref, qat_ref), (ka, ka_ref, kat_ref), (va, va_ref, vat_ref)):
                ref[hh] = val.astype(BF)
                tref[hh] = val.T.astype(BF)

    sec = lambda off: pl.BlockSpec((tm, LANES), lambda j, i: (i, off + j))
    par = pl.BlockSpec((1, LANES), lambda j, i: (0, 0))
    nat = pl.BlockSpec((2, tm, LANES), lambda j, i: (j, i, 0))
    trn = pl.BlockSpec((2, LANES, tm), lambda j, i: (j, 0, i))
    return pl.pallas_call(
        body, name="fox_operands", grid=(4, T // tm),
        in_specs=[sec(16), sec(20), sec(24), pl.BlockSpec((tm, LANES), lambda j, i: (i, 0)), par, par],
        out_specs=[nat, trn, nat, trn, nat, trn],
        out_shape=[jax.ShapeDtypeStruct((8, T, LANES), BF), jax.ShapeDtypeStruct((8, LANES, T), BF)] * 3,
        compiler_params=_params(("parallel", "arbitrary")),
    )(P, P, P, fc, g_fq2, g_fk2)


def _fox_forward(qat, ka, vat, T, tq, tk):
    nq, per = T // tq, tq // tk
    assert per == 2
    RC = 64

    def body(qat_ref, ka_ref, vat_ref, o_ref, lse_ref, s_scr, p_scr, a_scr, m_scr, l_scr, acc_scr):
        i = pl.program_id(1)
        sub = lax.broadcasted_iota(jnp.int32, (8, 1), 0)
        row = lax.broadcasted_iota(jnp.int32, (RC, tq), 0)
        col = lax.broadcasted_iota(jnp.int32, (RC, tq), 1)
        m_scr[...] = jnp.full(m_scr.shape, NEG, F32)
        l_scr[...] = jnp.zeros(l_scr.shape, F32)
        acc_scr[...] = jnp.zeros(acc_scr.shape, F32)

        def scores(slot, kb):
            k0 = pl.multiple_of(kb * tk, tk)
            for hh in range(2):
                s_scr[slot, hh] = jnp.dot(ka_ref[hh, pl.ds(k0, tk), :], qat_ref[hh], preferred_element_type=F32)

        def softmax(slot, kb, diagonal):
            shift = kb * tk - i * tq
            for hh in range(2):
                def masked(r):
                    tile = s_scr[slot, hh, r * RC:(r + 1) * RC, :]
                    return jnp.where(row + (r * RC + shift) <= col, tile, NEG) if diagonal else tile

                mx = jnp.max(masked(0), axis=0, keepdims=True)
                for r in range(1, tk // RC):
                    mx = jnp.maximum(mx, jnp.max(masked(r), axis=0, keepdims=True))
                m_old = m_scr[hh, 0:1, :]
                m2 = jnp.maximum(m_old, mx)
                a = jnp.exp(m_old - m2)
                lsum = jnp.zeros((1, tq), F32)
                for r in range(tk // RC):
                    p = jnp.exp(masked(r) - m2)
                    p_scr[slot, hh, r * RC:(r + 1) * RC, :] = p.astype(BF)
                    lsum = lsum + jnp.sum(p, axis=0, keepdims=True)
                m_scr[hh] = jnp.broadcast_to(m2, (8, tq))
                l_scr[hh] = jnp.broadcast_to(a * l_scr[hh, 0:1, :] + lsum, (8, tq))
                a_scr[slot, hh] = jnp.broadcast_to(a, (8, tq))

        def values(slot, kb):
            k0 = pl.multiple_of(kb * tk, tk)
            for hh in range(2):
                pv = jnp.dot(vat_ref[hh, 0:HEAD, pl.ds(k0, tk)], p_scr[slot, hh], preferred_element_type=F32)
                acc_scr[hh] = a_scr[slot, hh, 0:1, :] * acc_scr[hh] + pv

        def pair(kb, diag_first, diag_second, more):
            if more:
                scores(0, kb + 2)
            softmax(1, kb + 1, diag_first)
            values(0, kb)
            if more:
                scores(1, kb + 3)
                softmax(0, kb + 2, diag_second)
            values(1, kb + 1)

        scores(0, 0)
        scores(1, 1)
        softmax(0, 0, True)

        @pl.loop(0, jnp.maximum(i - 1, 0))
        def _(t):
            pair(2 * t, False, False, True)

        @pl.when(i >= 1)
        def _():
            pair(2 * (i - 1), False, True, True)

        pair(2 * i, True, False, False)

        o_ref[...] = jnp.concatenate([acc_scr[hh] / l_scr[hh, 0:1, :] for hh in range(2)], axis=0).T
        lses = [m_scr[hh, 0:1, :] + jnp.log(l_scr[hh, 0:1, :]) for hh in range(2)]
        lse_ref[0] = jnp.where(sub == 0, lses[0], jnp.where(sub == 1, lses[1], 0.0))

    return pl.pallas_call(
        body, name="fox_forward", grid=(4, nq),
        in_specs=[pl.BlockSpec((2, LANES, tq), lambda j, i: (j, 0, i)), pl.BlockSpec((2, T, LANES), lambda j, i: (j, 0, 0)),
                  pl.BlockSpec((2, LANES, T), lambda j, i: (j, 0, 0))],
        out_specs=[pl.BlockSpec((tq, LANES), lambda j, i: (i, j)), pl.BlockSpec((1, 8, tq), lambda j, i: (j, 0, i))],
        out_shape=[jax.ShapeDtypeStruct((T, 4 * LANES), F32), jax.ShapeDtypeStruct((4, 8, T), F32)],
        scratch_shapes=[pltpu.VMEM((2, 2, tk, tq), F32), pltpu.VMEM((2, 2, tk, tq), BF), pltpu.VMEM((2, 2, 8, tq), F32),
                        pltpu.VMEM((2, 8, tq), F32), pltpu.VMEM((2, 8, tq), F32), pltpu.VMEM((2, HEAD, tq), F32)],
        compiler_params=_params(("parallel", "arbitrary")),
    )(qat, ka, vat)


def _fox_cotangent(dmix, fox, T, tm):
    def body(do_ref, o_ref, doa_ref, doat_ref, dl_ref):
        lane = lax.broadcasted_iota(jnp.int32, (1, LANES), 1)
        sub = lax.broadcasted_iota(jnp.int32, (8, 1), 0)
        dob = do_ref[...].astype(BF).astype(F32)
        prod_t = (dob * o_ref[...]).T
        d0 = jnp.sum(prod_t[:HEAD], axis=0, keepdims=True)
        d1 = jnp.sum(prod_t[HEAD:], axis=0, keepdims=True)
        dl_ref[0] = jnp.where(sub == 0, d0, jnp.where(sub == 1, d1, 0.0))
        for hh in range(2):
            val = jnp.where(lane < HEAD, dob if hh == 0 else pltpu.roll(dob, HEAD, 1), 0.0)
            doa_ref[hh] = val.astype(BF)
            doat_ref[hh] = val.T.astype(BF)

    return pl.pallas_call(
        body, name="fox_cotangent", grid=(4, T // tm),
        in_specs=[pl.BlockSpec((tm, LANES), lambda j, i: (i, 4 + j)), pl.BlockSpec((tm, LANES), lambda j, i: (i, j))],
        out_specs=[pl.BlockSpec((2, tm, LANES), lambda j, i: (j, i, 0)), pl.BlockSpec((2, LANES, tm), lambda j, i: (j, 0, i)),
                   pl.BlockSpec((1, 8, tm), lambda j, i: (j, 0, i))],
        out_shape=[jax.ShapeDtypeStruct((8, T, LANES), BF), jax.ShapeDtypeStruct((8, LANES, T), BF),
                   jax.ShapeDtypeStruct((4, 8, T), F32)],
        compiler_params=_params(("parallel", "arbitrary")),
    )(dmix, fox)


def _fox_backward(qa, qat, ka, kat, va, doa, doat, lse, dl, T, tq, tk):
    nq, nk = T // tq, T // tk

    def body(qa_ref, qat_ref, ka_ref, kat_ref, va_ref, doa_ref, doat_ref, lse_ref, dl_ref,
             dq_ref, dk_ref, dv_ref, df_ref, dr_ref, dqt, dk_acc, dv_acc, df_acc, sdp, pds):
        j, kb = pl.program_id(0), pl.program_id(1)
        lane = lax.broadcasted_iota(jnp.int32, (1, LANES), 1)
        first = (kb * tk) // tq

        @pl.when(kb == 0)
        def _():
            dqt[...] = jnp.zeros(dqt.shape, F32)

        dk_acc[...] = jnp.zeros(dk_acc.shape, F32)
        dv_acc[...] = jnp.zeros(dv_acc.shape, F32)
        df_acc[...] = jnp.zeros(df_acc.shape, F32)

        RC = 64
        last = nq - 1

        def products(slot, qi):
            q0 = pl.multiple_of(qi * tq, tq)
            for hh in range(2):
                sdp[slot, hh, 0] = jnp.dot(ka_ref[hh], qat_ref[hh, :, pl.ds(q0, tq)], preferred_element_type=F32)
                sdp[slot, hh, 1] = jnp.dot(va_ref[hh], doat_ref[hh, :, pl.ds(q0, tq)], preferred_element_type=F32)

        def softmax_bwd(slot, qi, diagonal, valid):
            q0 = pl.multiple_of(qi * tq, tq)
            shift = kb * tk - first * tq
            col = lax.broadcasted_iota(jnp.int32, (RC, tq), 1)
            row = lax.broadcasted_iota(jnp.int32, (RC, tq), 0)
            for hh in range(2):
                lse_row = lse_ref[0, hh:hh + 1, pl.ds(q0, tq)]
                dl_row = dl_ref[0, hh:hh + 1, pl.ds(q0, tq)]
                rsum = jnp.zeros((1, tq), F32)
                for r in range(tk // RC):
                    rows = slice(r * RC, (r + 1) * RC)
                    p = jnp.exp(sdp[slot, hh, 0, rows, :] - lse_row)
                    p = jnp.where((row + (r * RC + shift) <= col) if diagonal else valid, p, 0.0)
                    ds = p * (sdp[slot, hh, 1, rows, :] - dl_row)
                    pds[slot, hh, 0, rows, :] = p.astype(BF)
                    pds[slot, hh, 1, rows, :] = ds.astype(BF)
                    rsum = rsum + jnp.sum(ds, axis=0, keepdims=True)
                    part = ds[:, 0:LANES]
                    for c in range(1, tq // LANES):
                        part = part + ds[:, c * LANES:(c + 1) * LANES]
                    df_acc[hh, rows, :] += part
                dqt[hh, HEAD:HEAD + 8, pl.ds(q0, tq)] += jnp.broadcast_to(rsum, (8, tq))

        def accumulate(slot, qi):
            q0 = pl.multiple_of(qi * tq, tq)
            for hh in range(2):
                dv_acc[hh] += jnp.dot(pds[slot, hh, 0], doa_ref[hh, pl.ds(q0, tq), :], preferred_element_type=F32)
                dk_acc[hh] += jnp.dot(pds[slot, hh, 1], qa_ref[hh, pl.ds(q0, tq), :], preferred_element_type=F32)
                dqt[hh, 0:HEAD, pl.ds(q0, tq)] += jnp.dot(kat_ref[hh, 0:HEAD, :], pds[slot, hh, 1], preferred_element_type=F32)

        products(0, first)
        products(1, jnp.minimum(first + 1, last))
        softmax_bwd(0, first, True, None)

        @pl.loop(0, (nq - first + 1) // 2)
        def _(t):
            qi = first + 2 * t
            products(0, jnp.minimum(qi + 2, last))
            softmax_bwd(1, jnp.minimum(qi + 1, last), False, qi + 1 <= last)
            accumulate(0, qi)
            products(1, jnp.minimum(qi + 3, last))
            softmax_bwd(0, jnp.minimum(qi + 2, last), False, qi + 2 <= last)
            accumulate(1, jnp.minimum(qi + 1, last))

        lo = lane < HEAD
        dk_ref[...] = jnp.where(lo, dk_acc[0], pltpu.roll(dk_acc[1], HEAD, 1))
        dv_ref[...] = jnp.where(lo, dv_acc[0], pltpu.roll(dv_acc[1], HEAD, 1)).astype(dv_ref.dtype)
        f0 = -jnp.sum(df_acc[0], axis=1, keepdims=True)
        f1 = -jnp.sum(df_acc[1], axis=1, keepdims=True)
        df_ref[0] = jnp.where(lane == 2 * j, f0, jnp.where(lane == 2 * j + 1, f1, 0.0))

        @pl.when(kb == nk - 1)
        def _():
            for t in range(nq):
                cols = slice(t * tq, (t + 1) * tq)
                dq_ref[cols, :] = jnp.concatenate([dqt[0, 0:HEAD, cols], dqt[1, 0:HEAD, cols]], axis=0).T
                rsum = jnp.concatenate([dqt[0, HEAD:HEAD + 8, cols], dqt[1, HEAD:HEAD + 8, cols],
                                        jnp.zeros((LANES - 16, tq), F32)], axis=0).T
                dr_ref[0, cols, :] = jnp.where(lane == 2 * j, rsum[:, 0:1], jnp.where(lane == 2 * j + 1, rsum[:, 8:9], 0.0))

    nat_full = pl.BlockSpec((2, T, LANES), lambda j, kb: (j, 0, 0))
    trn_full = pl.BlockSpec((2, LANES, T), lambda j, kb: (j, 0, 0))
    nat_blk = pl.BlockSpec((2, tk, LANES), lambda j, kb: (j, kb, 0))
    trn_blk = pl.BlockSpec((2, LANES, tk), lambda j, kb: (j, 0, kb))
    rows = pl.BlockSpec((1, 8, T), lambda j, kb: (j, 0, 0))
    blk = pl.BlockSpec((tk, LANES), lambda j, kb: (kb, j))
    return pl.pallas_call(
        body, name="fox_backward", grid=(4, nk),
        in_specs=[nat_full, trn_full, nat_blk, trn_blk, nat_blk, nat_full, trn_full, rows, rows],
        out_specs=[pl.BlockSpec((T, LANES), lambda j, kb: (0, j)), blk, blk, pl.BlockSpec((1, tk, LANES), lambda j, kb: (j, kb, 0)),
                   pl.BlockSpec((1, T, LANES), lambda j, kb: (j, 0, 0))],
        out_shape=[jax.ShapeDtypeStruct((T, 4 * LANES), F32), jax.ShapeDtypeStruct((T, 4 * LANES), F32),
                   jax.ShapeDtypeStruct((T, 4 * LANES), BF), jax.ShapeDtypeStruct((4, T, LANES), F32),
                   jax.ShapeDtypeStruct((4, T, LANES), F32)],
        scratch_shapes=[pltpu.VMEM((2, HEAD + 8, T), F32), pltpu.VMEM((2, tk, LANES), F32), pltpu.VMEM((2, tk, LANES), F32),
                        pltpu.VMEM((2, tk, LANES), F32), pltpu.VMEM((2, 2, 2, tk, tq), F32), pltpu.VMEM((2, 2, 2, tk, tq), BF)],
        compiler_params=_params(("arbitrary", "arbitrary")),
    )(qa, qat, ka, kat, va, doa, doat, lse, dl)


def _fgate_bwd_col(ffp, bpad, dfc4, drc4, T):
    def body(ff_ref, b_ref, dfc_ref, drc_ref, dff_ref, db_ref):
        lane = lax.broadcasted_iota(jnp.int32, (1, LANES), 1)
        tri = _tri(False)
        carry = jnp.zeros((1, LANES), F32)
        db = jnp.zeros((1, LANES), F32)
        for blk in reversed(range(T // _FB)):
            rows = slice(blk * _FB, (blk + 1) * _FB)
            dcol = dfc_ref[0, rows, :] + drc_ref[0, rows, :]
            for pair in range(1, 4):
                dcol = dcol + (dfc_ref[pair, rows, :] + drc_ref[pair, rows, :])
            dlf = jnp.dot(tri, dcol, precision=lax.Precision.HIGHEST, preferred_element_type=F32) + carry
            carry = dlf[0:1, :]
            z = ff_ref[blk * _FB:(blk + 1) * _FB, :] + b_ref[...]
            dz = jnp.where(lane < 8, dlf * jax.nn.sigmoid(-z), 0.0)
            dff_ref[blk * _FB:(blk + 1) * _FB, :] = dz.astype(dff_ref.dtype)
            db = db + jnp.sum(dz, axis=0, keepdims=True)
        db_ref[...] = db

    return pl.pallas_call(
        body, name="fgate_bwd",
        out_shape=[jax.ShapeDtypeStruct((T, LANES), BF), jax.ShapeDtypeStruct((1, LANES), F32)],
        compiler_params=pltpu.CompilerParams(vmem_limit_bytes=VMEM_LIMIT),
    )(ffp, bpad, dfc4, drc4)


MESH = pl.DeviceIdType.MESH
N_PEERS = N_DEV - 1


def _place():
    return lax.axis_index("x"), lax.axis_index("y"), lax.axis_index("c")


def _all_gather(shard):
    R, W = shard.shape

    def body(x_ref, out_ref, send_sems, recv_sems, local_sem):
        x, y, c = _place()
        me, sibling = (x, y, c), (x, y, 1 - c)
        chips = [(1 - x, y), (x, 1 - y), (1 - x, 1 - y)]

        def slot(px, py, pc):
            return out_ref.at[4 * px + 2 * py + pc]

        def copy(k, block, to, src=None):
            return pltpu.make_async_remote_copy(
                src_ref=slot(*block) if src is None else src, dst_ref=slot(*block),
                send_sem=send_sems.at[k], recv_sem=recv_sems.at[k], device_id=to, device_id_type=MESH)

        mine = pltpu.make_async_copy(x_ref, slot(*me), local_sem)
        mine.start()
        first = [copy(0, me, sibling, src=x_ref)]
        first += [copy(1 + n, me, (*chip, c), src=x_ref) for n, chip in enumerate(chips)]
        for cp in first:
            cp.start()
        passed = [copy(4 + n, (*chip, c), sibling) for n, chip in enumerate(chips)]
        for n, chip in enumerate(chips):
            copy(1 + n, (*chip, c), me).wait_recv()
            passed[n].start()
        copy(0, sibling, me).wait_recv()
        for n, chip in enumerate(chips):
            copy(4 + n, (*chip, 1 - c), me).wait_recv()
        for cp in first + passed:
            cp.wait_send()
        mine.wait()

    return pl.pallas_call(
        body, name="all_gather_weights",
        out_shape=jax.ShapeDtypeStruct((N_DEV, R, W), shard.dtype),
        in_specs=[pl.BlockSpec(memory_space=pl.ANY)], out_specs=pl.BlockSpec(memory_space=pl.ANY),
        scratch_shapes=[pltpu.SemaphoreType.DMA((N_PEERS,)), pltpu.SemaphoreType.DMA((N_PEERS,)), pltpu.SemaphoreType.DMA],
    )(shard)


def _exchange_copies(src_refs, land_refs, send_sems, recv_sems, scatter):
    x, y, c = _place()
    me = 4 * x + 2 * y + c
    copies = []
    for k, (src_ref, land_ref) in enumerate(zip(src_refs, land_refs)):
        for r in range(1, N_DEV):
            px, py, pc = x ^ (r >> 2), y ^ ((r >> 1) & 1), c ^ (r & 1)
            copies.append(pltpu.make_async_remote_copy(
                src_ref=src_ref.at[4 * px + 2 * py + pc] if scatter else src_ref, dst_ref=land_ref.at[me],
                send_sem=send_sems.at[k * N_PEERS + r - 1], recv_sem=recv_sems.at[k * N_PEERS + r - 1],
                device_id=(px, py, pc), device_id_type=MESH))
    return copies


_HBM = pl.BlockSpec(memory_space=pltpu.HBM)
_SEM = pl.BlockSpec(memory_space=pltpu.SEMAPHORE)
_EFFECT = pltpu.SideEffectType.DATAFLOW_SIDE_EFFECTING


def _exchange_start(name, srcs, lands, scatter):
    n = len(srcs)

    def body(*refs):
        send_sems, recv_sems = refs[2 * n], refs[2 * n + 1]
        for cp in _exchange_copies(refs[:n], refs[n:2 * n], send_sems, recv_sems, scatter):
            cp.start()
        token = refs[-1]
        token[...] = jnp.zeros(token.shape, F32)

    arrays = list(srcs) + list(lands)
    out = pl.pallas_call(
        body, name=name,
        out_shape=(pltpu.SemaphoreType.DMA((n * N_PEERS,)), pltpu.SemaphoreType.DMA((n * N_PEERS,)))
        + tuple(pltpu.HBM(a.shape, a.dtype) for a in arrays) + (jax.ShapeDtypeStruct((8, LANES), F32),),
        in_specs=(_HBM,) * (2 * n), out_specs=(_SEM, _SEM) + (_HBM,) * (2 * n) + (pl.BlockSpec(memory_space=pltpu.VMEM),),
        input_output_aliases={k: 2 + k for k in range(2 * n)},
        compiler_params=pltpu.CompilerParams(has_side_effects=_EFFECT),
    )(*(pltpu.with_memory_space_constraint(a, pltpu.HBM) for a in arrays))
    return out[0], out[1], out[2:2 + n], out[2 + n:2 + 2 * n], out[-1]


def _exchange_wait(name, started, after, scatter):
    send_sems, recv_sems, srcs, lands, _ = started
    n = len(srcs)

    def body(*refs):
        copies = _exchange_copies(refs[:n], refs[n:2 * n], refs[2 * n], refs[2 * n + 1], scatter)
        for cp in copies:
            cp.wait_send()
        for cp in copies:
            cp.wait_recv()

    arrays = list(srcs) + list(lands)
    out = pl.pallas_call(
        body, name=name,
        out_shape=tuple(pltpu.HBM(a.shape, a.dtype) for a in arrays),
        in_specs=(_HBM,) * (2 * n) + (_SEM, _SEM, pl.BlockSpec(memory_space=pl.ANY)), out_specs=(_HBM,) * (2 * n),
        input_output_aliases={k: k for k in range(2 * n)},
        compiler_params=pltpu.CompilerParams(has_side_effects=_EFFECT),
    )(*arrays, send_sems, recv_sems, after)
    return out[:n], out[n:]


def _adam_update(g, w, m, v):
    m2 = ADAM_B1 * m + (1.0 - ADAM_B1) * g
    v2 = ADAM_B2 * v + (1.0 - ADAM_B2) * jnp.square(g)
    m_hat = m2 / (1.0 - ADAM_B1 ** ADAM_STEP)
    v_hat = v2 / (1.0 - ADAM_B2 ** ADAM_STEP)
    return g, -ADAM_LR * (m_hat / (jnp.sqrt(v_hat) + ADAM_EPS) + ADAM_WD * w), m2, v2


def _adamw(name, slots, own, w, m, v):
    R, W = w.shape

    def body(s_ref, *refs):
        if own is not None:
            x, y, c = _place()
            me = 4 * x + 2 * y + c
            g = refs[0][...].astype(F32)
            refs = refs[1:]
        else:
            g = jnp.zeros((R, W), F32)
        for s in range(N_DEV):
            part = s_ref[s].astype(F32)
            g = g + (part if own is None else jnp.where(me == s, 0.0, part))
        w_ref, m_ref, v_ref = refs[:3]
        for o, r in zip(refs[3:], _adam_update(g, w_ref[...], m_ref[...], v_ref[...])):
            o[...] = r

    full = pl.BlockSpec((R, W), lambda i: (0, 0))
    args = [slots] + ([own] if own is not None else []) + [w, m, v]
    return pl.pallas_call(
        body, name=name, grid=(1,),
        in_specs=[pl.BlockSpec((N_DEV, R, W), lambda i: (0, 0, 0))] + [full] * (len(args) - 1),
        out_specs=[full] * 4, out_shape=[jax.ShapeDtypeStruct((R, W), F32)] * 4,
        compiler_params=_params(("arbitrary",)),
    )(*args)


def _tables(T):
    pos = jnp.arange(T, dtype=F32)
    inv_freq = 10000.0 ** (-jnp.arange(0, HEAD, 2, dtype=F32) / HEAD)
    ang = pos[:, None] * inv_freq[None, :]
    cos, sin = jnp.cos(ang), jnp.sin(ang)
    cos4 = jnp.tile(cos, (1, 4))
    sin4 = jnp.tile(jnp.concatenate([-sin, sin], axis=1), (1, 2))
    log_g = jnp.log(1.0 - 2.0 ** (-5.0 - jnp.arange(8, dtype=F32)))
    return cos4, sin4, jnp.repeat(log_g, HEAD)[None, :]


def _local_step(x, mem, target, sp, w_inT, token, fetch_rest, push, push_small):
    T = x.shape[0]
    tm = min(512, T)
    tq = min(256, T)
    tb = min(1024, T)
    cos4, sin4, lg = _tables(T)
    g_fq2 = jnp.tile(sp["g_fox_q"], (1, 2))
    g_fk2 = jnp.tile(sp["g_fox_k"], (1, 2))
    g_ret = sp["g_ret_out"].reshape(1, 8 * HEAD)
    bpad = jnp.pad(sp["b_forget"], ((0, 0), (0, LANES - 8)))
    w_secs = [w_inT[k * 512:(k + 1) * 512] for k in range(7)]
    w_ffT = jnp.pad(w_inT[3584:3592], ((0, LANES - 8), (0, 0)))
    w_mainT = w_inT[:3584]
    tie = lambda p, tok: p + tok[0:1, 0:1]
    tm2, tm4 = min(1024, T), min(2048, T)

    hn1, = _rw_fwd("rms_mix", _rms_fn, [(x, D, 0, False)], [(tie(sp["g_mix"], token), D, 0, False)], [(BF, D)], T, tm4, 1)
    P, = _mm("proj_in", [[(hn1, w_mainT, "nt")]], [], _ident, T, 3584, tm4, 512, [F32])
    ffp, = _mm("proj_ff", [[(hn1, w_ffT, "nt")]], [], _ident, T, LANES, tm, LANES, [F32])
    ret, s0 = _ret_fwd(P, cos4, sin4, g_ret, lg, T, tb)
    fc, _ = _fgate_fwd(ffp, bpad, T)
    qa, qat, ka, kat, va, vat = _fox_operands(P, fc, g_fq2, g_fk2, T, tm4)
    fox, lse = _fox_forward(qat, ka, vat, T, min(512, T), tq)
    W = fetch_rest(fox)
    w_out_halves = (W["w_out"][:4 * LANES], W["w_out"][4 * LANES:])
    h1, hn2 = _mm("proj_out", [[(ret, w_out_halves[0], "nn"), (fox, w_out_halves[1], "nn")]], [x], _add_rms_epi, T, D, tm2, D,
                  [F32, BF], params=[sp["g_xattn"]])

    qx, = _mm("proj_xq", [[(hn2, W["w_xq"], "nn")]], [], _ident, T, D, tm2, D, [F32])
    memn, = _rw_fwd("rms_mem", _rms_fn, [(mem, D, 0, False)], [(sp["g_mem"], D, 0, False)], [(BF, D)], N_MEM, N_MEM, 1)
    kv, = _mm("proj_xkv", [[(memn, W["w_xkvT"], "nt")]], [], _ident, N_MEM, 2 * D, N_MEM, 512, [F32])
    xa_rows = [(qx, XHEAD, 0, True)]
    xa_params = [(sp["g_xq"], XHEAD, 0, False), (sp["g_xk"], XHEAD, 0, False), (kv, XHEAD, 0, True), (kv, XHEAD, 4, True)]
    xo, = _rw_fwd("xattn_fwd", _xattn_fn, xa_rows, xa_params, [(BF, XHEAD)], T, tm4, 4)
    h2, hn3 = _mm("proj_xo", [[(xo, W["w_xo"], "nn")]], [h1], _add_rms_epi, T, D, tm2, D, [F32, BF], params=[sp["g_ffn"]])

    gate, up, act = _mm("ffn_in", [[(hn3, W["w_gateT"], "nt")], [(hn3, W["w_upT"], "nt")]], [], _swiglu_fwd_epi,
                        T, D_FF, tm4, 256, [BF, BF, BF])
    dy, dyb, loss_part = _mm("ffn_out", [[(act, W["w_down"], "nn")]], [h2, target], _add_loss_epi, T, D, tm, D, [F32, BF], n_acc=1)

    dgate, dup = _mm("ffn_out_bwd", [[(dyb, W["w_down"], "nt")]], [gate, up], _swiglu_bwd_epi, T, D_FF, tm4, 256, [BF, BF])
    gW = {}
    gW["w_gateT"], gW["w_upT"] = _mm("dw_gate_up", [[(dgate, hn3, "tn")], [(dup, hn3, "tn")]], [], _each, D_FF, D, 256, D, [BF, BF])
    gW["w_down"], = _mm("dw_down", [[(act, dyb, "tn")]], [], _ident, D_FF, D, 256, D, [BF])
    tok = push("ffn", gW)
    gs = {}
    dh2, dh2b, gs["g_ffn"] = _mm("ffn_in_bwd", [[(dgate, W["w_gateT"], "nn"), (dup, W["w_upT"], "nn")]], [h2, dy], _rms_bwd_epi,
                                 T, D, min(256, T), D, [F32, BF], params=[tie(sp["g_ffn"], tok)], n_acc=1)

    dxo, = _mm("proj_xo_bwd", [[(dh2b, W["w_xo"], "nt")]], [], _ident, T, D, tm2, D, [BF])
    gW["w_xo"], = _mm("dw_xo", [[(xo, dh2b, "tn")]], [], _ident, D, D, 256, D, [BF])
    dqx, gs["g_xq"], gs["g_xk"], dkv_k, dkv_v = _rw_bwd(
        "xattn_bwd", _xattn_fn, xa_rows, xa_params, [(dxo, XHEAD, 0, True)], T, tm4, 4, [BF], [True, True, True, True])
    dkv = jnp.concatenate([dkv_k[:, :D], dkv_v[:, D:]], axis=1)
    gW["w_xq"], = _mm("dw_xq", [[(hn2, dqx, "tn")]], [], _ident, D, D, 256, D, [BF])
    dmemn, = _mm("proj_xkv_bwd", [[(dkv, W["w_xkvT"], "nn")]], [], _ident, N_MEM, D, N_MEM, 512, [F32])
    gW["w_xkvT"], = _mm("dw_xkv", [[(dkv, memn, "tn")]], [], _ident, 2 * D, D, 512, D, [BF])
    tok = push("xattn", gW)
    gs["g_mem"], = _rw_bwd("rms_mem_bwd", _rms_fn, [(mem, D, 0, False)], [(sp["g_mem"], D, 0, False)], [(dmemn, D, 0, False)],
                           N_MEM, N_MEM, 1, [None], [True])
    dh1, dh1b, gs["g_xattn"] = _mm("proj_xq_bwd", [[(dqx, W["w_xq"], "nt")]], [h1, dh2], _rms_bwd_epi, T, D, tm, D, [F32, BF],
                                   params=[tie(sp["g_xattn"], tok)], n_acc=1)

    dmix, = _mm("proj_out_bwd", [[(dh1b, W["w_out"], "nt")]], [], _ident, T, D, tm2, D, [F32])
    gW["w_out"] = jnp.concatenate(_mm("dw_out", [[(ret, dh1b, "tn")], [(fox, dh1b, "tn")]], [], _each, 4 * LANES, D, 256, D,
                                      [BF, BF]), axis=0)
    tok = push("out", gW)
    doa, doat, dl = _fox_cotangent(dmix, fox, T, tm4)
    dqn, dkn, dfv, dfc4, drc4 = _fox_backward(qa, qat, ka, kat, va, doa, doat, lse + tok[0:1, 0:1], dl, T, tq, tq)
    dfq, dfk, gq2, gk2 = _rw_bwd("fox_prep_bwd", _fox_prep_fn, [(P, LANES, 16, True), (P, LANES, 20, True)],
                                 [(g_fq2, LANES, 0, False), (g_fk2, LANES, 0, False)],
                                 [(dqn, LANES, 0, True), (dkn, LANES, 0, True)], T, tm4, 4, [BF, BF], [True, True])
    gs["g_fox_q"] = gq2[:, :HEAD] + gq2[:, HEAD:]
    gs["g_fox_k"] = gk2[:, :HEAD] + gk2[:, HEAD:]
    dff, dbp = _fgate_bwd_col(ffp, bpad, dfc4, drc4, T)
    gs["b_forget"] = dbp[:, :8]
    drq, drk, drv, drg, dg_ret = _ret_bwd(P, cos4, sin4, g_ret, lg, s0, dmix, T, tb)
    gs["g_ret_out"] = dg_ret
    dsecs = [drq, drk, drv, drg, dfq, dfk, dfv]
    g_secs = list(_mm("dw_in", [[(d, hn1, "tn")] for d in dsecs], [], _each, 512, D, LANES, D, [BF] * len(dsecs)))
    g_ff, = _mm("dw_in_ff", [[(dff, hn1, "tn")]], [], _ident, LANES, D, LANES, D, [BF])
    gW["w_inT"] = jnp.concatenate(g_secs + [g_ff[:8]], axis=0)
    tok = push("in", gW)
    grad_x, _, gs["g_mix"] = _mm("proj_in_bwd", [[(d, w, "nn") for d, w in zip(dsecs, w_secs)] + [(dff, w_ffT, "nn")]], [x, dh1],
                                 _rms_bwd_epi, T, D, tm, D, [F32, BF], params=[tie(sp["g_mix"], tok)], n_acc=1)
    return grad_x, push_small(gs, loss_part)


_CANON = {"w_in": "w_inT", "w_xkv": "w_xkvT", "w_gate": "w_gateT", "w_up": "w_upT"}
_SMALL = (("g_mix", 0, 0, 1024), ("g_xattn", 1, 0, 1024), ("g_mem", 2, 0, 1024), ("g_ffn", 3, 0, 1024),
          ("g_ret_out", 4, 0, 512), ("g_xq", 4, 512, 256), ("g_xk", 4, 768, 256),
          ("g_fox_q", 5, 0, 64), ("g_fox_k", 5, 64, 64), ("b_forget", 5, 128, 8))
_LOSS_AT = (5, 256)


def _pack_small(tree):
    buf = jnp.zeros((SMALL_ROWS, D), F32)
    for name, r, c, n in _SMALL:
        buf = lax.dynamic_update_slice(buf, tree[name].reshape(1, n).astype(F32), (r, c))
    return buf


def _unpack_small(buf, like):
    return {name: buf[r:r + 1, c:c + n].reshape(like[name].shape) for name, r, c, n in _SMALL}


def _canonical(tree, name):
    a = tree[name][0]
    return a.T if W_SHARD[name][1] else a


def _from_canonical(a, name):
    return (a.T if W_SHARD[name][1] else a)[None]


def kernel(x, mem, g_mix, w_in, b_forget, g_ret_out, g_fox_q, g_fox_k, w_out, g_xattn, w_xq, w_xkv, g_mem, g_xq, g_xk, w_xo, g_ffn, w_gate, w_up, w_down, loss_target, m_g_mix, m_w_in, m_b_forget, m_g_ret_out, m_g_fox_q, m_g_fox_k, m_w_out, m_g_xattn, m_w_xq, m_w_xkv, m_g_mem, m_g_xq, m_g_xk, m_w_xo, m_g_ffn, m_w_gate, m_w_up, m_w_down, v_g_mix, v_w_in, v_b_forget, v_g_ret_out, v_g_fox_q, v_g_fox_k, v_w_out, v_g_xattn, v_w_xq, v_w_xkv, v_g_mem, v_g_xq, v_g_xk, v_w_xo, v_g_ffn, v_w_gate, v_w_up, v_w_down):
    names = ("g_mix", "w_in", "b_forget", "g_ret_out", "g_fox_q", "g_fox_k", "w_out", "g_xattn", "w_xq", "w_xkv", "g_mem",
             "g_xq", "g_xk", "w_xo", "g_ffn", "w_gate", "w_up", "w_down")
    w = dict(zip(names, (g_mix, w_in, b_forget, g_ret_out, g_fox_q, g_fox_k, w_out, g_xattn, w_xq, w_xkv, g_mem, g_xq, g_xk,
                         w_xo, g_ffn, w_gate, w_up, w_down)))
    m = dict(zip(names, (m_g_mix, m_w_in, m_b_forget, m_g_ret_out, m_g_fox_q, m_g_fox_k, m_w_out, m_g_xattn, m_w_xq, m_w_xkv,
                         m_g_mem, m_g_xq, m_g_xk, m_w_xo, m_g_ffn, m_w_gate, m_w_up, m_w_down)))
    v = dict(zip(names, (v_g_mix, v_w_in, v_b_forget, v_g_ret_out, v_g_fox_q, v_g_fox_k, v_w_out, v_g_xattn, v_w_xq, v_w_xkv,
                         v_g_mem, v_g_xq, v_g_xk, v_w_xo, v_g_ffn, v_w_gate, v_w_up, v_w_down)))
    small_names = [s[0] for s in _SMALL]
    me = 4 * lax.axis_index("x") + 2 * lax.axis_index("y") + lax.axis_index("c")

    first = _all_gather(_canonical(w, "w_in").astype(BF))
    first, rest = lax.optimization_barrier((first, [_canonical(w, n).astype(BF) for n in GATHER_REST]))
    rest_started = _exchange_start("gather_rest_start", rest, [jnp.broadcast_to(a[None], (N_DEV,) + a.shape) for a in rest],
                                   scatter=False)

    def fetch_rest(after):
        lands = _exchange_wait("gather_rest_wait", rest_started, after, scatter=False)[1]
        return {_CANON.get(n, n): a.reshape(N_DEV * a.shape[1], D) for n, a in zip(GATHER_REST, lands)}

    pushed = {}

    def push(group, grads):
        srcs = [grads[_CANON.get(n, n)].reshape(N_DEV, W_SHARD[n][0], D) for n in SCATTER_GROUPS[group]]
        pushed[group] = _exchange_start("scatter_%s_start" % group, srcs, [lax.empty(a.shape, BF) for a in srcs], scatter=True)
        return pushed[group][4]

    def push_small(gs, loss_part):
        small = lax.dynamic_update_slice(_pack_small(gs), loss_part[:, :1], _LOSS_AT)
        pushed["small"] = _exchange_start("gather_small_start", [small], [jnp.broadcast_to(small[None], (N_DEV,) + small.shape)],
                                          scatter=False)
        return pushed["small"][4]

    sp = {n: w[n].reshape(1, -1) for n in small_names}
    grad_x, done = _local_step(x[0], mem[0], loss_target[0], sp, first.reshape(N_DEV * W_SHARD["w_in"][0], D),
                               rest_started[4], fetch_rest, push, push_small)

    results, after = {}, done
    for group in ("ffn", "xattn", "out", "small", "in"):
        if group == "small":
            recv_small = _exchange_wait("gather_small_wait", pushed["small"], after, scatter=False)[1][0]
            g_sm, d_sm, m_sm, v_sm = _adamw("adamw_small", recv_small, None, _pack_small(w), _pack_small(m), _pack_small(v))
            after = g_sm
            continue
        sents, recvs = _exchange_wait("scatter_%s_wait" % group, pushed[group], after, scatter=True)
        for name, sent, recv in zip(SCATTER_GROUPS[group], sents, recvs):
            own = lax.dynamic_index_in_dim(sent, me, axis=0, keepdims=False)
            res = _adamw("adamw_" + name, recv, own, *(_canonical(t, name) for t in (w, m, v)))
            results[name] = [_from_canonical(r, name) for r in res]
        after = results[SCATTER_GROUPS[group][-1]][0]
    loss = g_sm[_LOSS_AT[0], _LOSS_AT[1]]

    outs = []
    for k, sm in enumerate((g_sm, d_sm, m_sm, v_sm)):
        tree = _unpack_small(sm, w)
        tree.update({name: res[k] for name, res in results.items()})
        outs += [tree[n] for n in names]
    return (loss, grad_x[None], *outs)
```

```python
import jax
import jax.numpy as jnp
from jax import lax
from jax.experimental import pallas as pl
from jax.experimental.pallas import tpu as pltpu

F32 = jnp.float32
BF = jnp.bfloat16

D = 1024
HEAD = 64
CHUNK = 64
N_MEM = 256
XHEAD = 256
D_FF = 2816
EPS = 1e-6
NEG = -1e30
LANES = 128
N_DEV = 8
V7X_VMEM_BYTES = 64 * 1024 * 1024
VMEM_LIMIT = V7X_VMEM_BYTES - 8 * 1024 * 1024

ADAM_LR, ADAM_B1, ADAM_B2, ADAM_EPS, ADAM_WD, ADAM_STEP = 0.001, 0.9, 0.999, 1e-08, 0.01, 10

W_SHARD = {"w_in": (449, True), "w_out": (128, False), "w_xq": (128, False), "w_xkv": (256, True),
           "w_xo": (128, False), "w_gate": (352, True), "w_up": (352, True), "w_down": (352, False)}
GATHER_REST = ("w_out", "w_xq", "w_xkv", "w_xo", "w_gate", "w_up", "w_down")
SCATTER_GROUPS = {"ffn": ("w_gate", "w_up", "w_down"), "xattn": ("w_xq", "w_xo", "w_xkv"), "out": ("w_out",), "in": ("w_in",)}
SMALL_ROWS = 8

NT = (((1,), (1,)), ((), ()))
NN = (((1,), (0,)), ((), ()))
TN = (((0,), (0,)), ((), ()))
_DIMS = {"nn": NN, "nt": NT, "tn": TN}


def _params(sem):
    return pltpu.CompilerParams(dimension_semantics=sem, vmem_limit_bytes=VMEM_LIMIT)


def _mm(name, products, extras, epilogue, M, N, tm, tn, out_dtypes, params=(), n_acc=0):
    assert n_acc == 0 or tn == N
    flat = [t for p in products for t in p]
    counts = [len(p) for p in products]
    in_specs, args, where, slots = [], [], {}, []

    def operand(arr, spec, kind):
        key = (id(arr), kind)
        if key not in where:
            where[key] = len(args)
            args.append(arr)
            in_specs.append(spec)
        return where[key]

    for a, b, form in flat:
        if form == "tn":
            ia = operand(a, pl.BlockSpec((a.shape[0], tm), lambda i, j: (0, i)), "a_tn")
        else:
            ia = operand(a, pl.BlockSpec((tm, a.shape[1]), lambda i, j: (i, 0)), "a")
        if form == "nt":
            ib = operand(b, pl.BlockSpec((tn, b.shape[1]), lambda i, j: (j, 0)), "b_nt")
        else:
            ib = operand(b, pl.BlockSpec((b.shape[0], tn), lambda i, j: (0, j)), "b")
        slots.append((ia, ib))
    n_mm = len(args)
    for e in extras:
        in_specs.append(pl.BlockSpec((tm, tn), lambda i, j: (i, j)))
        args.append(e)
    for p in params:
        in_specs.append(pl.BlockSpec((1, tn), lambda i, j: (0, j)))
        args.append(p)
    n_in = len(args)
    n_out = len(out_dtypes)

    def body(*refs):
        ins, outs = refs[:n_in], refs[n_in:]
        prods, p = [], 0
        for c in counts:
            acc = None
            for _ in range(c):
                a = ins[slots[p][0]][...].astype(BF)
                b = ins[slots[p][1]][...].astype(BF)
                d = lax.dot_general(a, b, _DIMS[flat[p][2]], preferred_element_type=F32)
                acc = d if acc is None else acc + d
                p += 1
            prods.append(acc)
        ex = [r[...].astype(F32) for r in ins[n_mm:]]
        res = epilogue(*prods, *ex)
        for o, r in zip(outs[:n_out], res[:n_out]):
            o[...] = r.astype(o.dtype)
        for o, r in zip(outs[n_out:], res[n_out:]):
            @pl.when(pl.program_id(0) == 0)
            def _(o=o):
                o[...] = jnp.zeros(o.shape, F32)
            o[...] += r

    return pl.pallas_call(
        body, name=name, grid=(M // tm, N // tn), in_specs=in_specs,
        out_specs=[pl.BlockSpec((tm, tn), lambda i, j: (i, j)) for _ in out_dtypes]
        + [pl.BlockSpec((1, tn), lambda i, j: (0, j)) for _ in range(n_acc)],
        out_shape=[jax.ShapeDtypeStruct((M, N), dt) for dt in out_dtypes] + [jax.ShapeDtypeStruct((1, N), F32)] * n_acc,
        compiler_params=_params(("arbitrary", "arbitrary")),
    )(*args)


def _ident(x):
    return (x,)


def _each(*xs):
    return xs


def _spec(rows, w, off, per_j):
    if per_j:
        return pl.BlockSpec((rows, w), lambda j, i: (i, off + j))
    return pl.BlockSpec((rows, w), lambda j, i: (i, off))


def _pspec(rows, w, off, per_j):
    if per_j:
        return pl.BlockSpec((rows, w), lambda j, i: (0, off + j))
    return pl.BlockSpec((rows, w), lambda j, i: (0, off))


def _rw_fwd(name, fn, rows, params, outs, T, tm, nj, n_acc=0):
    in_specs = [_spec(tm, w, off, pj) for _, w, off, pj in rows] + [_pspec(a.shape[0], w, off, pj) for a, w, off, pj in params]
    args = [r[0] for r in rows] + [p[0] for p in params]
    n_in, n_out = len(args), len(outs)
    out_specs = [pl.BlockSpec((tm, w), lambda j, i: (i, j)) for _, w in outs]
    out_shape = [jax.ShapeDtypeStruct((T, nj * w), dt) for dt, w in outs]
    out_specs += [pl.BlockSpec((1, LANES), lambda j, i: (0, 0)) for _ in range(n_acc)]
    out_shape += [jax.ShapeDtypeStruct((1, LANES), F32) for _ in range(n_acc)]

    def body(*refs):
        vals = [r[...].astype(F32) for r in refs[:n_in]]
        res = fn(*vals)
        orefs = refs[n_in:]
        for k in range(n_out):
            orefs[k][...] = res[k].astype(orefs[k].dtype)
        first = (pl.program_id(0) == 0) & (pl.program_id(1) == 0)
        for k in range(n_acc):
            @pl.when(first)
            def _(k=k):
                orefs[n_out + k][...] = jnp.zeros((1, LANES), F32)
            orefs[n_out + k][...] += res[n_out + k]

    return pl.pallas_call(
        body, name=name, grid=(nj, T // tm), in_specs=in_specs, out_specs=out_specs, out_shape=out_shape,
        compiler_params=_params(("arbitrary", "arbitrary")),
    )(*args)


def _rw_bwd(name, fn, rows, params, cots, T, tm, nj, row_grads, param_grads, resid=None):
    in_specs = ([_spec(tm, w, off, pj) for _, w, off, pj in rows] + [_pspec(a.shape[0], w, off, pj) for a, w, off, pj in params]
                + [_spec(tm, w, off, pj) for _, w, off, pj in cots])
    args = [r[0] for r in rows] + [p[0] for p in params] + [c[0] for c in cots]
    if resid is not None:
        in_specs.append(_spec(tm, rows[0][1], rows[0][2], rows[0][3]))
        args.append(resid)
    nr, npar, nc = len(rows), len(params), len(cots)
    out_specs, out_shape, kinds = [], [], []
    for k, dts in enumerate(row_grads):
        for dt in (dts if isinstance(dts, (list, tuple)) else [dts]):
            if dt is not None:
                w = rows[k][1]
                out_specs.append(pl.BlockSpec((tm, w), lambda j, i: (i, j)))
                out_shape.append(jax.ShapeDtypeStruct((T, nj * w), dt))
                kinds.append(("row", k))
    for k, need in enumerate(param_grads):
        if need:
            a, w, off, pj = params[k]
            out_specs.append(_pspec(a.shape[0], w, off, pj))
            out_shape.append(jax.ShapeDtypeStruct(a.shape, F32))
            kinds.append(("par", k))

    def body(*refs):
        vals = [r[...].astype(F32) for r in refs[:nr + npar]]
        ct = tuple(r[...].astype(F32) for r in refs[nr + npar:nr + npar + nc])
        _, vjp = jax.vjp(lambda *a: tuple(fn(*a)), *vals)
        grads = list(vjp(ct))
        n_in = nr + npar + nc + (resid is not None)
        if resid is not None:
            grads[0] = grads[0] + refs[n_in - 1][...].astype(F32)
        orefs = refs[n_in:]
        j, i = pl.program_id(0), pl.program_id(1)
        for o, (kind, k) in zip(orefs, kinds):
            if kind == "row":
                o[...] = grads[k].astype(o.dtype)
            else:
                first = (i == 0) if params[k][3] else ((i == 0) & (j == 0))

                @pl.when(first)
                def _(o=o):
                    o[...] = jnp.zeros(o.shape, F32)
                o[...] += grads[nr + k]

    return pl.pallas_call(
        body, name=name, grid=(nj, T // tm), in_specs=in_specs, out_specs=out_specs, out_shape=out_shape,
        compiler_params=_params(("arbitrary", "arbitrary")),
    )(*args)


def _rms(x, g):
    return x * lax.rsqrt(jnp.mean(x * x, axis=-1, keepdims=True) + EPS) * g


def _rms_fn(x, g):
    return (_rms(x, g),)


def _lo_mask():
    return lax.broadcasted_iota(jnp.int32, (1, LANES), 1) < HEAD


def _gmean(x, lo):
    s0 = jnp.sum(jnp.where(lo, x, 0.0), axis=-1, keepdims=True)
    s1 = jnp.sum(jnp.where(lo, 0.0, x), axis=-1, keepdims=True)
    return jnp.where(lo, s0, s1) * (1.0 / HEAD)


def _fox_prep_fn(fq, fk, gq, gk):
    lo = _lo_mask()
    qn = fq * lax.rsqrt(_gmean(fq * fq, lo) + EPS) * gq * (HEAD ** -0.5)
    kn = fk * lax.rsqrt(_gmean(fk * fk, lo) + EPS) * gk
    return qn, kn


@jax.custom_vjp
def _swap_halves(x):
    bit = (lax.broadcasted_iota(jnp.int32, (1, LANES), 1) & (HEAD // 2)) == 0
    return jnp.where(bit, pltpu.roll(x, LANES - HEAD // 2, 1), pltpu.roll(x, HEAD // 2, 1))


_swap_halves.defvjp(lambda x: (_swap_halves(x), None), lambda _, g: (_swap_halves(g),))


def _ret_fn(rq, rk, rv, rg, cos, sin, s_in, g, lg):
    tb = rq.shape[0]
    nc = tb // CHUNK
    lo = _lo_mask()
    row = lax.broadcasted_iota(jnp.int32, (LANES, 1), 0) < HEAD
    same_head = row == lo
    q = (rq * cos + _swap_halves(rq) * sin) * (HEAD ** -0.5)
    k = rk * cos + _swap_halves(rk) * sin
    q3, k3, v3 = q.reshape(nc, CHUNK, LANES), k.reshape(nc, CHUNK, LANES), rv.reshape(nc, CHUNK, LANES)
    pos = lax.broadcasted_iota(jnp.int32, (CHUNK, 1), 0).astype(F32)
    q_decay = jnp.exp(lg * (pos + 1.0))
    k_decay = jnp.exp(lg * (CHUNK - 1.0 - pos))
    chunk_decay = jnp.exp(lg * float(CHUNK))
    dist = jnp.abs(lax.broadcasted_iota(jnp.int32, (CHUNK, CHUNK), 0) - lax.broadcasted_iota(jnp.int32, (CHUNK, CHUNK), 1)).astype(F32)
    v3b = v3.astype(BF)
    intra = []
    for hh in range(2):
        hm = lo if hh == 0 else ~lo
        lg_h = lg[:, hh * HEAD:hh * HEAD + 1]
        qm = jnp.where(hm, q3, 0.0).astype(BF)
        sc = jnp.einsum("nid,njd->nij", qm, k3.astype(BF), preferred_element_type=F32) * jnp.exp(lg_h * dist)[None]
        intra.append(jnp.einsum("nij,nje->nie", sc.astype(BF), v3b, preferred_element_type=F32))
    o = jnp.where(lo, intra[0], intra[1])
    kv = jnp.einsum("njd,nje->nde", (k3 * k_decay[None]).astype(BF), v3b, preferred_element_type=F32)
    kv = jnp.where(same_head[None], kv, 0.0)
    state, states = s_in, []
    for n in range(nc):
        states.append(state)
        state = state * chunk_decay + kv[n]
    s_prev = jnp.stack(states, axis=0)
    o = o + jnp.einsum("nid,nde->nie", (q3 * q_decay[None]).astype(BF), s_prev.astype(BF), preferred_element_type=F32)
    o = o.reshape(tb, LANES)
    mu = _gmean(o, lo)
    oc = o - mu
    y = oc * lax.rsqrt(_gmean(oc * oc, lo) + EPS) * g
    return jax.nn.silu(rg) * y, state


def _xattn_fn(qx, gq, gk, kk, vv):
    q = _rms(qx, gq)
    k = _rms(kk, gk)
    logits = lax.dot_general(q.astype(BF), k.astype(BF), NT, preferred_element_type=F32) * (XHEAD ** -0.5)
    p = jax.nn.softmax(logits, axis=-1)
    return (jnp.dot(p.astype(BF), vv.astype(BF), preferred_element_type=F32),)


def _swiglu_fwd_epi(g, u):
    return g, u, jax.nn.silu(g) * u


def _swiglu_bwd_epi(dact, g, u):
    _, vjp = jax.vjp(lambda a, b: jax.nn.silu(a) * b, g, u)
    return vjp(dact)


def _add_rms_epi(acc, resid, g):
    h = acc + resid
    return h, _rms(h, g)


def _add_loss_epi(acc, resid, target):
    err = (acc + resid) - target
    dy = err * (1.0 / D)
    part = jnp.sum(jnp.sum(err * err, axis=0, keepdims=True), axis=1, keepdims=True) * (0.5 / D)
    return dy, dy, jnp.broadcast_to(part, (1, err.shape[1]))


def _rms_bwd_epi(dhn, h, skip, g):
    _, vjp = jax.vjp(_rms, h, g)
    dh, dg = vjp(dhn)
    dh = dh + skip
    return dh, dh, dg


def _ret_fwd(P, cos, sin, g_ret, lg, T, tb):
    nb = T // tb

    def body(rq, rk, rv, rg, c, s, g, l, o_ref, s0_ref, state):
        @pl.when(pl.program_id(1) == 0)
        def _():
            state[...] = jnp.zeros(state.shape, F32)
        s0_ref[0, 0] = state[...]
        out, s_new = _ret_fn(rq[...], rk[...], rv[...], rg[...], c[...], s[...], state[...], g[...], l[...])
        o_ref[...] = out.astype(o_ref.dtype)
        state[...] = s_new

    sec = lambda off: pl.BlockSpec((tb, LANES), lambda j, i: (i, off + j))
    tab = pl.BlockSpec((tb, LANES), lambda j, i: (i, 0))
    par = pl.BlockSpec((1, LANES), lambda j, i: (0, j))
    return pl.pallas_call(
        body, name="ret_fwd", grid=(4, nb),
        in_specs=[sec(0), sec(4), sec(8), sec(12), tab, tab, par, par],
        out_specs=[pl.BlockSpec((tb, LANES), lambda j, i: (i, j)), pl.BlockSpec((1, 1, LANES, LANES), lambda j, i: (j, i, 0, 0))],
        out_shape=[jax.ShapeDtypeStruct((T, 4 * LANES), BF), jax.ShapeDtypeStruct((4, nb, LANES, LANES), F32)],
        scratch_shapes=[pltpu.VMEM((LANES, LANES), F32)],
        compiler_params=_params(("arbitrary", "arbitrary")),
    )(P, P, P, P, cos, sin, g_ret, lg)


def _ret_bwd(P, cos, sin, g_ret, lg, s0, dmix, T, tb):
    nb = T // tb

    def body(rq, rk, rv, rg, c, s, g, l, s0_ref, do, drq, drk, drv, drg, dg, dstate):
        i = pl.program_id(1)

        @pl.when(i == 0)
        def _():
            dstate[...] = jnp.zeros(dstate.shape, F32)
            dg[...] = jnp.zeros(dg.shape, F32)

        cc, ss, ll = c[...], s[...], l[...]
        _, vjp = jax.vjp(lambda a, b, v, gate, st, gg: _ret_fn(a, b, v, gate, cc, ss, st, gg, ll),
                         rq[...], rk[...], rv[...], rg[...], s0_ref[0, 0], g[...])
        ga, gb, gv, ggate, gst, ggain = vjp((do[...], dstate[...]))
        drq[...] = ga.astype(drq.dtype)
        drk[...] = gb.astype(drk.dtype)
        drv[...] = gv.astype(drv.dtype)
        drg[...] = ggate.astype(drg.dtype)
        dstate[...] = gst
        dg[...] += ggain

    rev = lambda i: nb - 1 - i
    sec = lambda off: pl.BlockSpec((tb, LANES), lambda j, i: (rev(i), off + j))
    tab = pl.BlockSpec((tb, LANES), lambda j, i: (rev(i), 0))
    par = pl.BlockSpec((1, LANES), lambda j, i: (0, j))
    outb = pl.BlockSpec((tb, LANES), lambda j, i: (rev(i), j))
    return pl.pallas_call(
        body, name="ret_bwd", grid=(4, nb),
        in_specs=[sec(0), sec(4), sec(8), sec(12), tab, tab, par, par,
                  pl.BlockSpec((1, 1, LANES, LANES), lambda j, i: (j, rev(i), 0, 0)), outb],
        out_specs=[outb, outb, outb, outb, par],
        out_shape=[jax.ShapeDtypeStruct((T, 4 * LANES), BF)] * 4 + [jax.ShapeDtypeStruct((1, 4 * LANES), F32)],
        scratch_shapes=[pltpu.VMEM((LANES, LANES), F32)],
        compiler_params=_params(("arbitrary", "arbitrary")),
    )(P, P, P, P, cos, sin, g_ret, lg, s0, dmix)


_FB = 128


def _tri(lower):
    r = lax.broadcasted_iota(jnp.int32, (_FB, _FB), 0)
    c = lax.broadcasted_iota(jnp.int32, (_FB, _FB), 1)
    return ((r >= c) if lower else (r <= c)).astype(F32)


def _fgate_fwd(ffp, bpad, T):
    def body(ff_ref, b_ref, fc_ref, fr_ref):
        lane = lax.broadcasted_iota(jnp.int32, (1, LANES), 1)
        tri = _tri(True)
        carry = jnp.zeros((1, LANES), F32)
        for blk in range(T // _FB):
            z = ff_ref[blk * _FB:(blk + 1) * _FB, :] + b_ref[...]
            lf = jnp.where(lane < 8, jax.nn.log_sigmoid(z), 0.0)
            f = jnp.dot(tri, lf, precision=lax.Precision.HIGHEST, preferred_element_type=F32) + carry
            carry = f[_FB - 1:_FB, :]
            fc_ref[blk * _FB:(blk + 1) * _FB, :] = f
            fr_ref[:, blk * _FB:(blk + 1) * _FB] = f.T[:8, :]

    return pl.pallas_call(
        body, name="fgate_fwd",
        out_shape=[jax.ShapeDtypeStruct((T, LANES), F32), jax.ShapeDtypeStruct((8, T), F32)],
        compiler_params=pltpu.CompilerParams(vmem_limit_bytes=VMEM_LIMIT),
    )(ffp, bpad)


_BIAS_LANE = HEAD


def _head_bias_col(fc, head):
    lane = lax.broadcasted_iota(jnp.int32, (1, LANES), 1)
    return jnp.sum(jnp.where(lane == head, fc, 0.0), axis=-1, keepdims=True)


def _split3(f):
    hi = f.astype(BF).astype(F32)
    mid = (f - hi).astype(BF).astype(F32)
    lo = ((f - hi) - mid).astype(BF).astype(F32)
    return hi, mid, lo


def _fox_operands(P, fc, g_fq2, g_fk2, T, tm):
    def body(fq_ref, fk_ref, fv_ref, fc_ref, gq_ref, gk_ref, qa_ref, qat_ref, ka_ref, kat_ref, va_ref, vat_ref):
        j = pl.program_id(0)
        lane = lax.broadcasted_iota(jnp.int32, (1, LANES), 1)
        qn, kn = _fox_prep_fn(fq_ref[...], fk_ref[...], gq_ref[...], gk_ref[...])
        v = fv_ref[...]
        fcb = fc_ref[...]
        b = _BIAS_LANE
        for hh in range(2):
            hi, mid, lo = _split3(_head_bias_col(fcb, 2 * j + hh))
            take = (lambda a: a) if hh == 0 else (lambda a: pltpu.roll(a, HEAD, 1))
            qa = jnp.where(lane < HEAD, take(qn), jnp.where(lane == b, hi, jnp.where(lane == b + 1, mid, jnp.where(
                lane == b + 2, lo, jnp.where(lane < b + 6, 1.0, 0.0)))))
            ka = jnp.where(lane < HEAD, take(kn), jnp.where(lane < b + 3, 1.0, jnp.where(lane == b + 3, -hi, jnp.where(
                lane == b + 4, -mid, jnp.where(lane == b + 5, -lo, 0.0)))))
            va = jnp.where(lane < HEAD, take(v), 0.0)
            for val, ref, tref in ((qa, qa_ref, qat_ref), (ka, ka_ref, kat_ref), (va, va_ref, vat_ref)):
                ref[hh] = val.astype(BF)
                tref[hh] = val.T.astype(BF)

    sec = lambda off: pl.BlockSpec((tm, LANES), lambda j, i: (i, off + j))
    par = pl.BlockSpec((1, LANES), lambda j, i: (0, 0))
    nat = pl.BlockSpec((2, tm, LANES), lambda j, i: (j, i, 0))
    trn = pl.BlockSpec((2, LANES, tm), lambda j, i: (j, 0, i))
    return pl.pallas_call(
        body, name="fox_operands", grid=(4, T // tm),
        in_specs=[sec(16), sec(20), sec(24), pl.BlockSpec((tm, LANES), lambda j, i: (i, 0)), par, par],
        out_specs=[nat, trn, nat, trn, nat, trn],
        out_shape=[jax.ShapeDtypeStruct((8, T, LANES), BF), jax.ShapeDtypeStruct((8, LANES, T), BF)] * 3,
        compiler_params=_params(("parallel", "arbitrary")),
    )(P, P, P, fc, g_fq2, g_fk2)


def _fox_forward(qat, ka, vat, T, tq, tk):
    nq, per = T // tq, tq // tk
    assert per == 2
    RC = 64

    def body(qat_ref, ka_ref, vat_ref, o_ref, lse_ref, s_scr, p_scr, a_scr, m_scr, l_scr, acc_scr):
        i = pl.program_id(1)
        sub = lax.broadcasted_iota(jnp.int32, (8, 1), 0)
        row = lax.broadcasted_iota(jnp.int32, (RC, tq), 0)
        col = lax.broadcasted_iota(jnp.int32, (RC, tq), 1)
        m_scr[...] = jnp.full(m_scr.shape, NEG, F32)
        l_scr[...] = jnp.zeros(l_scr.shape, F32)
        acc_scr[...] = jnp.zeros(acc_scr.shape, F32)

        def scores(slot, kb):
            k0 = pl.multiple_of(kb * tk, tk)
            for hh in range(2):
                s_scr[slot, hh] = jnp.dot(ka_ref[hh, pl.ds(k0, tk), :], qat_ref[hh], preferred_element_type=F32)

        def softmax(slot, kb, diagonal):
            shift = kb * tk - i * tq
            for hh in range(2):
                def masked(r):
                    tile = s_scr[slot, hh, r * RC:(r + 1) * RC, :]
                    return jnp.where(row + (r * RC + shift) <= col, tile, NEG) if diagonal else tile

                mx = jnp.max(masked(0), axis=0, keepdims=True)
                for r in range(1, tk // RC):
                    mx = jnp.maximum(mx, jnp.max(masked(r), axis=0, keepdims=True))
                m_old = m_scr[hh, 0:1, :]
                m2 = jnp.maximum(m_old, mx)
                a = jnp.exp(m_old - m2)
                lsum = jnp.zeros((1, tq), F32)
                for r in range(tk // RC):
                    p = jnp.exp(masked(r) - m2)
                    p_scr[slot, hh, r * RC:(r + 1) * RC, :] = p.astype(BF)
                    lsum = lsum + jnp.sum(p, axis=0, keepdims=True)
                m_scr[hh] = jnp.broadcast_to(m2, (8, tq))
                l_scr[hh] = jnp.broadcast_to(a * l_scr[hh, 0:1, :] + lsum, (8, tq))
                a_scr[slot, hh] = jnp.broadcast_to(a, (8, tq))

        def values(slot, kb):
            k0 = pl.multiple_of(kb * tk, tk)
            for hh in range(2):
                pv = jnp.dot(vat_ref[hh, 0:HEAD, pl.ds(k0, tk)], p_scr[slot, hh], preferred_element_type=F32)
                acc_scr[hh] = a_scr[slot, hh, 0:1, :] * acc_scr[hh] + pv

        def pair(kb, diag_first, diag_second, more):
            if more:
                scores(0, kb + 2)
            softmax(1, kb + 1, diag_first)
            values(0, kb)
            if more:
                scores(1, kb + 3)
                softmax(0, kb + 2, diag_second)
            values(1, kb + 1)

        scores(0, 0)
        scores(1, 1)
        softmax(0, 0, True)

        @pl.loop(0, jnp.maximum(i - 1, 0))
        def _(t):
            pair(2 * t, False, False, True)

        @pl.when(i >= 1)
        def _():
            pair(2 * (i - 1), False, True, True)

        pair(2 * i, True, False, False)

        o_ref[...] = jnp.concatenate([acc_scr[hh] / l_scr[hh, 0:1, :] for hh in range(2)], axis=0).T
        lses = [m_scr[hh, 0:1, :] + jnp.log(l_scr[hh, 0:1, :]) for hh in range(2)]
        lse_ref[0] = jnp.where(sub == 0, lses[0], jnp.where(sub == 1, lses[1], 0.0))

    return pl.pallas_call(
        body, name="fox_forward", grid=(4, nq),
        in_specs=[pl.BlockSpec((2, LANES, tq), lambda j, i: (j, 0, i)), pl.BlockSpec((2, T, LANES), lambda j, i: (j, 0, 0)),
                  pl.BlockSpec((2, LANES, T), lambda j, i: (j, 0, 0))],
        out_specs=[pl.BlockSpec((tq, LANES), lambda j, i: (i, j)), pl.BlockSpec((1, 8, tq), lambda j, i: (j, 0, i))],
        out_shape=[jax.ShapeDtypeStruct((T, 4 * LANES), F32), jax.ShapeDtypeStruct((4, 8, T), F32)],
        scratch_shapes=[pltpu.VMEM((2, 2, tk, tq), F32), pltpu.VMEM((2, 2, tk, tq), BF), pltpu.VMEM((2, 2, 8, tq), F32),
                        pltpu.VMEM((2, 8, tq), F32), pltpu.VMEM((2, 8, tq), F32), pltpu.VMEM((2, HEAD, tq), F32)],
        compiler_params=_params(("parallel", "arbitrary")),
    )(qat, ka, vat)


def _fox_cotangent(dmix, fox, T, tm):
    def body(do_ref, o_ref, doa_ref, doat_ref, dl_ref):
        lane = lax.broadcasted_iota(jnp.int32, (1, LANES), 1)
        sub = lax.broadcasted_iota(jnp.int32, (8, 1), 0)
        dob = do_ref[...].astype(BF).astype(F32)
        prod_t = (dob * o_ref[...]).T
        d0 = jnp.sum(prod_t[:HEAD], axis=0, keepdims=True)
        d1 = jnp.sum(prod_t[HEAD:], axis=0, keepdims=True)
        dl_ref[0] = jnp.where(sub == 0, d0, jnp.where(sub == 1, d1, 0.0))
        for hh in range(2):
            val = jnp.where(lane < HEAD, dob if hh == 0 else pltpu.roll(dob, HEAD, 1), 0.0)
            doa_ref[hh] = val.astype(BF)
            doat_ref[hh] = val.T.astype(BF)

    return pl.pallas_call(
        body, name="fox_cotangent", grid=(4, T // tm),
        in_specs=[pl.BlockSpec((tm, LANES), lambda j, i: (i, 4 + j)), pl.BlockSpec((tm, LANES), lambda j, i: (i, j))],
        out_specs=[pl.BlockSpec((2, tm, LANES), lambda j, i: (j, i, 0)), pl.BlockSpec((2, LANES, tm), lambda j, i: (j, 0, i)),
                   pl.BlockSpec((1, 8, tm), lambda j, i: (j, 0, i))],
        out_shape=[jax.ShapeDtypeStruct((8, T, LANES), BF), jax.ShapeDtypeStruct((8, LANES, T), BF),
                   jax.ShapeDtypeStruct((4, 8, T), F32)],
        compiler_params=_params(("parallel", "arbitrary")),
    )(dmix, fox)


def _fox_backward(qa, qat, ka, kat, va, doa, doat, lse, dl, T, tq, tk):
    nq, nk = T // tq, T // tk

    def body(qa_ref, qat_ref, ka_ref, kat_ref, va_ref, doa_ref, doat_ref, lse_ref, dl_ref,
             dq_ref, dk_ref, dv_ref, df_ref, dr_ref, dqt, dk_acc, dv_acc, df_acc, sdp, pds):
        j, kb = pl.program_id(0), pl.program_id(1)
        lane = lax.broadcasted_iota(jnp.int32, (1, LANES), 1)
        first = (kb * tk) // tq

        @pl.when(kb == 0)
        def _():
            dqt[...] = jnp.zeros(dqt.shape, F32)

        dk_acc[...] = jnp.zeros(dk_acc.shape, F32)
        dv_acc[...] = jnp.zeros(dv_acc.shape, F32)
        df_acc[...] = jnp.zeros(df_acc.shape, F32)

        RC = 64
        last = nq - 1

        def products(slot, qi):
            q0 = pl.multiple_of(qi * tq, tq)
            for hh in range(2):
                sdp[slot, hh, 0] = jnp.dot(ka_ref[hh], qat_ref[hh, :, pl.ds(q0, tq)], preferred_element_type=F32)
                sdp[slot, hh, 1] = jnp.dot(va_ref[hh], doat_ref[hh, :, pl.ds(q0, tq)], preferred_element_type=F32)

        def softmax_bwd(slot, qi, diagonal, valid):
            q0 = pl.multiple_of(qi * tq, tq)
            shift = kb * tk - first * tq
            col = lax.broadcasted_iota(jnp.int32, (RC, tq), 1)
            row = lax.broadcasted_iota(jnp.int32, (RC, tq), 0)
            for hh in range(2):
                lse_row = lse_ref[0, hh:hh + 1, pl.ds(q0, tq)]
                dl_row = dl_ref[0, hh:hh + 1, pl.ds(q0, tq)]
                rsum = jnp.zeros((1, tq), F32)
                for r in range(tk // RC):
                    rows = slice(r * RC, (r + 1) * RC)
                    p = jnp.exp(sdp[slot, hh, 0, rows, :] - lse_row)
                    p = jnp.where((row + (r * RC + shift) <= col) if diagonal else valid, p, 0.0)
                    ds = p * (sdp[slot, hh, 1, rows, :] - dl_row)
                    pds[slot, hh, 0, rows, :] = p.astype(BF)
                    pds[slot, hh, 1, rows, :] = ds.astype(BF)
                    rsum = rsum + jnp.sum(ds, axis=0, keepdims=True)
                    part = ds[:, 0:LANES]
                    for c in range(1, tq // LANES):
                        part = part + ds[:, c * LANES:(c + 1) * LANES]
                    df_acc[hh, rows, :] += part
                dqt[hh, HEAD:HEAD + 8, pl.ds(q0, tq)] += jnp.broadcast_to(rsum, (8, tq))

        def accumulate(slot, qi):
            q0 = pl.multiple_of(qi * tq, tq)
            for hh in range(2):
                dv_acc[hh] += jnp.dot(pds[slot, hh, 0], doa_ref[hh, pl.ds(q0, tq), :], preferred_element_type=F32)
                dk_acc[hh] += jnp.dot(pds[slot, hh, 1], qa_ref[hh, pl.ds(q0, tq), :], preferred_element_type=F32)
                dqt[hh, 0:HEAD, pl.ds(q0, tq)] += jnp.dot(kat_ref[hh, 0:HEAD, :], pds[slot, hh, 1], preferred_element_type=F32)

        products(0, first)
        products(1, jnp.minimum(first + 1, last))
        softmax_bwd(0, first, True, None)

        @pl.loop(0, (nq - first + 1) // 2)
        def _(t):
            qi = first + 2 * t
            products(0, jnp.minimum(qi + 2, last))
            softmax_bwd(1, jnp.minimum(qi + 1, last), False, qi + 1 <= last)
            accumulate(0, qi)
            products(1, jnp.minimum(qi + 3, last))
            softmax_bwd(0, jnp.minimum(qi + 2, last), False, qi + 2 <= last)
            accumulate(1, jnp.minimum(qi + 1, last))

        lo = lane < HEAD
        dk_ref[...] = jnp.where(lo, dk_acc[0], pltpu.roll(dk_acc[1], HEAD, 1))
        dv_ref[...] = jnp.where(lo, dv_acc[0], pltpu.roll(dv_acc[1], HEAD, 1)).astype(dv_ref.dtype)
        f0 = -jnp.sum(df_acc[0], axis=1, keepdims=True)
        f1 = -jnp.sum(df_acc[1], axis=1, keepdims=True)
        df_ref[0] = jnp.where(lane == 2 * j, f0, jnp.where(lane == 2 * j + 1, f1, 0.0))

        @pl.when(kb == nk - 1)
        def _():
            for t in range(nq):
                cols = slice(t * tq, (t + 1) * tq)
                dq_ref[cols, :] = jnp.concatenate([dqt[0, 0:HEAD, cols], dqt[1, 0:HEAD, cols]], axis=0).T
                rsum = jnp.concatenate([dqt[0, HEAD:HEAD + 8, cols], dqt[1, HEAD:HEAD + 8, cols],
                                        jnp.zeros((LANES - 16, tq), F32)], axis=0).T
                dr_ref[0, cols, :] = jnp.where(lane == 2 * j, rsum[:, 0:1], jnp.where(lane == 2 * j + 1, rsum[:, 8:9], 0.0))

    nat_full = pl.BlockSpec((2, T, LANES), lambda j, kb: (j, 0, 0))
    trn_full = pl.BlockSpec((2, LANES, T), lambda j, kb: (j, 0, 0))
    nat_blk = pl.BlockSpec((2, tk, LANES), lambda j, kb: (j, kb, 0))
    trn_blk = pl.BlockSpec((2, LANES, tk), lambda j, kb: (j, 0, kb))
    rows = pl.BlockSpec((1, 8, T), lambda j, kb: (j, 0, 0))
    blk = pl.BlockSpec((tk, LANES), lambda j, kb: (kb, j))
    return pl.pallas_call(
        body, name="fox_backward", grid=(4, nk),
        in_specs=[nat_full, trn_full, nat_blk, trn_blk, nat_blk, nat_full, trn_full, rows, rows],
        out_specs=[pl.BlockSpec((T, LANES), lambda j, kb: (0, j)), blk, blk, pl.BlockSpec((1, tk, LANES), lambda j, kb: (j, kb, 0)),
                   pl.BlockSpec((1, T, LANES), lambda j, kb: (j, 0, 0))],
        out_shape=[jax.ShapeDtypeStruct((T, 4 * LANES), F32), jax.ShapeDtypeStruct((T, 4 * LANES), F32),
                   jax.ShapeDtypeStruct((T, 4 * LANES), BF), jax.ShapeDtypeStruct((4, T, LANES), F32),
                   jax.ShapeDtypeStruct((4, T, LANES), F32)],
        scratch_shapes=[pltpu.VMEM((2, HEAD + 8, T), F32), pltpu.VMEM((2, tk, LANES), F32), pltpu.VMEM((2, tk, LANES), F32),
                        pltpu.VMEM((2, tk, LANES), F32), pltpu.VMEM((2, 2, 2, tk, tq), F32), pltpu.VMEM((2, 2, 2, tk, tq), BF)],
        compiler_params=_params(("arbitrary", "arbitrary")),
    )(qa, qat, ka, kat, va, doa, doat, lse, dl)


def _fgate_bwd_col(ffp, bpad, dfc4, drc4, T):
    def body(ff_ref, b_ref, dfc_ref, drc_ref, dff_ref, db_ref):
        lane = lax.broadcasted_iota(jnp.int32, (1, LANES), 1)
        tri = _tri(False)
        carry = jnp.zeros((1, LANES), F32)
        db = jnp.zeros((1, LANES), F32)
        for blk in reversed(range(T // _FB)):
            rows = slice(blk * _FB, (blk + 1) * _FB)
            dcol = dfc_ref[0, rows, :] + drc_ref[0, rows, :]
            for pair in range(1, 4):
                dcol = dcol + (dfc_ref[pair, rows, :] + drc_ref[pair, rows, :])
            dlf = jnp.dot(tri, dcol, precision=lax.Precision.HIGHEST, preferred_element_type=F32) + carry
            carry = dlf[0:1, :]
            z = ff_ref[blk * _FB:(blk + 1) * _FB, :] + b_ref[...]
            dz = jnp.where(lane < 8, dlf * jax.nn.sigmoid(-z), 0.0)
            dff_ref[blk * _FB:(blk + 1) * _FB, :] = dz.astype(dff_ref.dtype)
            db = db + jnp.sum(dz, axis=0, keepdims=True)
        db_ref[...] = db

    return pl.pallas_call(
        body, name="fgate_bwd",
        out_shape=[jax.ShapeDtypeStruct((T, LANES), BF), jax.ShapeDtypeStruct((1, LANES), F32)],
        compiler_params=pltpu.CompilerParams(vmem_limit_bytes=VMEM_LIMIT),
    )(ffp, bpad, dfc4, drc4)


MESH = pl.DeviceIdType.MESH
N_PEERS = N_DEV - 1


def _place():
    return lax.axis_index("x"), lax.axis_index("y"), lax.axis_index("c")


def _all_gather(shard):
    R, W = shard.shape

    def body(x_ref, out_ref, send_sems, recv_sems, local_sem):
        x, y, c = _place()
        me, sibling = (x, y, c), (x, y, 1 - c)
        chips = [(1 - x, y), (x, 1 - y), (1 - x, 1 - y)]

        def slot(px, py, pc):
            return out_ref.at[4 * px + 2 * py + pc]

        def copy(k, block, to, src=None):
            return pltpu.make_async_remote_copy(
                src_ref=slot(*block) if src is None else src, dst_ref=slot(*block),
                send_sem=send_sems.at[k], recv_sem=recv_sems.at[k], device_id=to, device_id_type=MESH)

        mine = pltpu.make_async_copy(x_ref, slot(*me), local_sem)
        mine.start()
        first = [copy(0, me, sibling, src=x_ref)]
        first += [copy(1 + n, me, (*chip, c), src=x_ref) for n, chip in enumerate(chips)]
        for cp in first:
            cp.start()
        passed = [copy(4 + n, (*chip, c), sibling) for n, chip in enumerate(chips)]
        for n, chip in enumerate(chips):
            copy(1 + n, (*chip, c), me).wait_recv()
            passed[n].start()
        copy(0, sibling, me).wait_recv()
        for n, chip in enumerate(chips):
            copy(4 + n, (*chip, 1 - c), me).wait_recv()
        for cp in first + passed:
            cp.wait_send()
        mine.wait()

    return pl.pallas_call(
        body, name="all_gather_weights",
        out_shape=jax.ShapeDtypeStruct((N_DEV, R, W), shard.dtype),
        in_specs=[pl.BlockSpec(memory_space=pl.ANY)], out_specs=pl.BlockSpec(memory_space=pl.ANY),
        scratch_shapes=[pltpu.SemaphoreType.DMA((N_PEERS,)), pltpu.SemaphoreType.DMA((N_PEERS,)), pltpu.SemaphoreType.DMA],
    )(shard)


def _exchange_copies(src_refs, land_refs, send_sems, recv_sems, scatter):
    x, y, c = _place()
    me = 4 * x + 2 * y + c
    copies = []
    for k, (src_ref, land_ref) in enumerate(zip(src_refs, land_refs)):
        for r in range(1, N_DEV):
            px, py, pc = x ^ (r >> 2), y ^ ((r >> 1) & 1), c ^ (r & 1)
            copies.append(pltpu.make_async_remote_copy(
                src_ref=src_ref.at[4 * px + 2 * py + pc] if scatter else src_ref, dst_ref=land_ref.at[me],
                send_sem=send_sems.at[k * N_PEERS + r - 1], recv_sem=recv_sems.at[k * N_PEERS + r - 1],
                device_id=(px, py, pc), device_id_type=MESH))
    return copies


_HBM = pl.BlockSpec(memory_space=pltpu.HBM)
_SEM = pl.BlockSpec(memory_space=pltpu.SEMAPHORE)
_EFFECT = pltpu.SideEffectType.DATAFLOW_SIDE_EFFECTING


def _exchange_start(name, srcs, lands, scatter):
    n = len(srcs)

    def body(*refs):
        send_sems, recv_sems = refs[2 * n], refs[2 * n + 1]
        for cp in _exchange_copies(refs[:n], refs[n:2 * n], send_sems, recv_sems, scatter):
            cp.start()
        token = refs[-1]
        token[...] = jnp.zeros(token.shape, F32)

    arrays = list(srcs) + list(lands)
    out = pl.pallas_call(
        body, name=name,
        out_shape=(pltpu.SemaphoreType.DMA((n * N_PEERS,)), pltpu.SemaphoreType.DMA((n * N_PEERS,)))
        + tuple(pltpu.HBM(a.shape, a.dtype) for a in arrays) + (jax.ShapeDtypeStruct((8, LANES), F32),),
        in_specs=(_HBM,) * (2 * n), out_specs=(_SEM, _SEM) + (_HBM,) * (2 * n) + (pl.BlockSpec(memory_space=pltpu.VMEM),),
        input_output_aliases={k: 2 + k for k in range(2 * n)},
        compiler_params=pltpu.CompilerParams(has_side_effects=_EFFECT),
    )(*(pltpu.with_memory_space_constraint(a, pltpu.HBM) for a in arrays))
    return out[0], out[1], out[2:2 + n], out[2 + n:2 + 2 * n], out[-1]


def _exchange_wait(name, started, after, scatter):
    send_sems, recv_sems, srcs, lands, _ = started
    n = len(srcs)

    def body(*refs):
        copies = _exchange_copies(refs[:n], refs[n:2 * n], refs[2 * n], refs[2 * n + 1], scatter)
        for cp in copies:
            cp.wait_send()
        for cp in copies:
            cp.wait_recv()

    arrays = list(srcs) + list(lands)
    out = pl.pallas_call(
        body, name=name,
        out_shape=tuple(pltpu.HBM(a.shape, a.dtype) for a in arrays),
        in_specs=(_HBM,) * (2 * n) + (_SEM, _SEM, pl.BlockSpec(memory_space=pl.ANY)), out_specs=(_HBM,) * (2 * n),
        input_output_aliases={k: k for k in range(2 * n)},
        compiler_params=pltpu.CompilerParams(has_side_effects=_EFFECT),
    )(*arrays, send_sems, recv_sems, after)
    return out[:n], out[n:]


def _adam_update(g, w, m, v):
    m2 = ADAM_B1 * m + (1.0 - ADAM_B1) * g
    v2 = ADAM_B2 * v + (1.0 - ADAM_B2) * jnp.square(g)
    m_hat = m2 / (1.0 - ADAM_B1 ** ADAM_STEP)
    v_hat = v2 / (1.0 - ADAM_B2 ** ADAM_STEP)
    return g, -ADAM_LR * (m_hat / (jnp.sqrt(v_hat) + ADAM_EPS) + ADAM_WD * w), m2, v2


def _adamw(name, slots, own, w, m, v):
    R, W = w.shape

    def body(s_ref, *refs):
        if own is not None:
            x, y, c = _place()
            me = 4 * x + 2 * y + c
            g = refs[0][...].astype(F32)
            refs = refs[1:]
        else:
            g = jnp.zeros((R, W), F32)
        for s in range(N_DEV):
            part = s_ref[s].astype(F32)
            g = g + (part if own is None else jnp.where(me == s, 0.0, part))
        w_ref, m_ref, v_ref = refs[:3]
        for o, r in zip(refs[3:], _adam_update(g, w_ref[...], m_ref[...], v_ref[...])):
            o[...] = r

    full = pl.BlockSpec((R, W), lambda i: (0, 0))
    args = [slots] + ([own] if own is not None else []) + [w, m, v]
    return pl.pallas_call(
        body, name=name, grid=(1,),
        in_specs=[pl.BlockSpec((N_DEV, R, W), lambda i: (0, 0, 0))] + [full] * (len(args) - 1),
        out_specs=[full] * 4, out_shape=[jax.ShapeDtypeStruct((R, W), F32)] * 4,
        compiler_params=_params(("arbitrary",)),
    )(*args)


def _tables(T):
    pos = jnp.arange(T, dtype=F32)
    inv_freq = 10000.0 ** (-jnp.arange(0, HEAD, 2, dtype=F32) / HEAD)
    ang = pos[:, None] * inv_freq[None, :]
    cos, sin = jnp.cos(ang), jnp.sin(ang)
    cos4 = jnp.tile(cos, (1, 4))
    sin4 = jnp.tile(jnp.concatenate([-sin, sin], axis=1), (1, 2))
    log_g = jnp.log(1.0 - 2.0 ** (-5.0 - jnp.arange(8, dtype=F32)))
    return cos4, sin4, jnp.repeat(log_g, HEAD)[None, :]


def _local_step(x, mem, target, sp, w_inT, token, fetch_rest, push, push_small):
    T = x.shape[0]
    tm = min(512, T)
    tq = min(256, T)
    tb = min(1024, T)
    cos4, sin4, lg = _tables(T)
    g_fq2 = jnp.tile(sp["g_fox_q"], (1, 2))
    g_fk2 = jnp.tile(sp["g_fox_k"], (1, 2))
    g_ret = sp["g_ret_out"].reshape(1, 8 * HEAD)
    bpad = jnp.pad(sp["b_forget"], ((0, 0), (0, LANES - 8)))
    w_secs = [w_inT[k * 512:(k + 1) * 512] for k in range(7)]
    w_ffT = jnp.pad(w_inT[3584:3592], ((0, LANES - 8), (0, 0)))
    w_mainT = w_inT[:3584]
    tie = lambda p, tok: p + tok[0:1, 0:1]
    tm2, tm4 = min(1024, T), min(2048, T)

    hn1, = _rw_fwd("rms_mix", _rms_fn, [(x, D, 0, False)], [(tie(sp["g_mix"], token), D, 0, False)], [(BF, D)], T, tm4, 1)
    P, = _mm("proj_in", [[(hn1, w_mainT, "nt")]], [], _ident, T, 3584, tm4, 512, [F32])
    ffp, = _mm("proj_ff", [[(hn1, w_ffT, "nt")]], [], _ident, T, LANES, tm, LANES, [F32])
    ret, s0 = _ret_fwd(P, cos4, sin4, g_ret, lg, T, tb)
    fc, _ = _fgate_fwd(ffp, bpad, T)
    qa, qat, ka, kat, va, vat = _fox_operands(P, fc, g_fq2, g_fk2, T, tm4)
    fox, lse = _fox_forward(qat, ka, vat, T, min(512, T), tq)
    W = fetch_rest(fox)
    w_out_halves = (W["w_out"][:4 * LANES], W["w_out"][4 * LANES:])
    h1, hn2 = _mm("proj_out", [[(ret, w_out_halves[0], "nn"), (fox, w_out_halves[1], "nn")]], [x], _add_rms_epi, T, D, tm2, D,
                  [F32, BF], params=[sp["g_xattn"]])

    qx, = _mm("proj_xq", [[(hn2, W["w_xq"], "nn")]], [], _ident, T, D, tm2, D, [F32])
    memn, = _rw_fwd("rms_mem", _rms_fn, [(mem, D, 0, False)], [(sp["g_mem"], D, 0, False)], [(BF, D)], N_MEM, N_MEM, 1)
    kv, = _mm("proj_xkv", [[(memn, W["w_xkvT"], "nt")]], [], _ident, N_MEM, 2 * D, N_MEM, 512, [F32])
    xa_rows = [(qx, XHEAD, 0, True)]
    xa_params = [(sp["g_xq"], XHEAD, 0, False), (sp["g_xk"], XHEAD, 0, False), (kv, XHEAD, 0, True), (kv, XHEAD, 4, True)]
    xo, = _rw_fwd("xattn_fwd", _xattn_fn, xa_rows, xa_params, [(BF, XHEAD)], T, tm4, 4)
    h2, hn3 = _mm("proj_xo", [[(xo, W["w_xo"], "nn")]], [h1], _add_rms_epi, T, D, tm2, D, [F32, BF], params=[sp["g_ffn"]])

    gate, up, act = _mm("ffn_in", [[(hn3, W["w_gateT"], "nt")], [(hn3, W["w_upT"], "nt")]], [], _swiglu_fwd_epi,
                        T, D_FF, tm4, 256, [BF, BF, BF])
    dy, dyb, loss_part = _mm("ffn_out", [[(act, W["w_down"], "nn")]], [h2, target], _add_loss_epi, T, D, tm, D, [F32, BF], n_acc=1)

    dgate, dup = _mm("ffn_out_bwd", [[(dyb, W["w_down"], "nt")]], [gate, up], _swiglu_bwd_epi, T, D_FF, tm4, 256, [BF, BF])
    gW = {}
    gW["w_gateT"], gW["w_upT"] = _mm("dw_gate_up", [[(dgate, hn3, "tn")], [(dup, hn3, "tn")]], [], _each, D_FF, D, 256, D, [BF, BF])
    gW["w_down"], = _mm("dw_down", [[(act, dyb, "tn")]], [], _ident, D_FF, D, 256, D, [BF])
    tok = push("ffn", gW)
    gs = {}
    dh2, dh2b, gs["g_ffn"] = _mm("ffn_in_bwd", [[(dgate, W["w_gateT"], "nn"), (dup, W["w_upT"], "nn")]], [h2, dy], _rms_bwd_epi,
                                 T, D, min(256, T), D, [F32, BF], params=[tie(sp["g_ffn"], tok)], n_acc=1)

    dxo, = _mm("proj_xo_bwd", [[(dh2b, W["w_xo"], "nt")]], [], _ident, T, D, tm2, D, [BF])
    gW["w_xo"], = _mm("dw_xo", [[(xo, dh2b, "tn")]], [], _ident, D, D, 256, D, [BF])
    dqx, gs["g_xq"], gs["g_xk"], dkv_k, dkv_v = _rw_bwd(
        "xattn_bwd", _xattn_fn, xa_rows, xa_params, [(dxo, XHEAD, 0, True)], T, tm4, 4, [BF], [True, True, True, True])
    dkv = jnp.concatenate([dkv_k[:, :D], dkv_v[:, D:]], axis=1)
    gW["w_xq"], = _mm("dw_xq", [[(hn2, dqx, "tn")]], [], _ident, D, D, 256, D, [BF])
    dmemn, = _mm("proj_xkv_bwd", [[(dkv, W["w_xkvT"], "nn")]], [], _ident, N_MEM, D, N_MEM, 512, [F32])
    gW["w_xkvT"], = _mm("dw_xkv", [[(dkv, memn, "tn")]], [], _ident, 2 * D, D, 512, D, [BF])
    tok = push("xattn", gW)
    gs["g_mem"], = _rw_bwd("rms_mem_bwd", _rms_fn, [(mem, D, 0, False)], [(sp["g_mem"], D, 0, False)], [(dmemn, D, 0, False)],
                           N_MEM, N_MEM, 1, [None], [True])
    dh1, dh1b, gs["g_xattn"] = _mm("proj_xq_bwd", [[(dqx, W["w_xq"], "nt")]], [h1, dh2], _rms_bwd_epi, T, D, tm, D, [F32, BF],
                                   params=[tie(sp["g_xattn"], tok)], n_acc=1)

    dmix, = _mm("proj_out_bwd", [[(dh1b, W["w_out"], "nt")]], [], _ident, T, D, tm2, D, [F32])
    gW["w_out"] = jnp.concatenate(_mm("dw_out", [[(ret, dh1b, "tn")], [(fox, dh1b, "tn")]], [], _each, 4 * LANES, D, 256, D,
                                      [BF, BF]), axis=0)
    tok = push("out", gW)
    doa, doat, dl = _fox_cotangent(dmix, fox, T, tm4)
    dqn, dkn, dfv, dfc4, drc4 = _fox_backward(qa, qat, ka, kat, va, doa, doat, lse + tok[0:1, 0:1], dl, T, tq, tq)
    dfq, dfk, gq2, gk2 = _rw_bwd("fox_prep_bwd", _fox_prep_fn, [(P, LANES, 16, True), (P, LANES, 20, True)],
                                 [(g_fq2, LANES, 0, False), (g_fk2, LANES, 0, False)],
                                 [(dqn, LANES, 0, True), (dkn, LANES, 0, True)], T, tm4, 4, [BF, BF], [True, True])
    gs["g_fox_q"] = gq2[:, :HEAD] + gq2[:, HEAD:]
    gs["g_fox_k"] = gk2[:, :HEAD] + gk2[:, HEAD:]
    dff, dbp = _fgate_bwd_col(ffp, bpad, dfc4, drc4, T)
    gs["b_forget"] = dbp[:, :8]
    drq, drk, drv, drg, dg_ret = _ret_bwd(P, cos4, sin4, g_ret, lg, s0, dmix, T, tb)
    gs["g_ret_out"] = dg_ret
    dsecs = [drq, drk, drv, drg, dfq, dfk, dfv]
    g_secs = list(_mm("dw_in", [[(d, hn1, "tn")] for d in dsecs], [], _each, 512, D, LANES, D, [BF] * len(dsecs)))
    g_ff, = _mm("dw_in_ff", [[(dff, hn1, "tn")]], [], _ident, LANES, D, LANES, D, [BF])
    gW["w_inT"] = jnp.concatenate(g_secs + [g_ff[:8]], axis=0)
    tok = push("in", gW)
    grad_x, _, gs["g_mix"] = _mm("proj_in_bwd", [[(d, w, "nn") for d, w in zip(dsecs, w_secs)] + [(dff, w_ffT, "nn")]], [x, dh1],
                                 _rms_bwd_epi, T, D, tm, D, [F32, BF], params=[tie(sp["g_mix"], tok)], n_acc=1)
    return grad_x, push_small(gs, loss_part)


_CANON = {"w_in": "w_inT", "w_xkv": "w_xkvT", "w_gate": "w_gateT", "w_up": "w_upT"}
_SMALL = (("g_mix", 0, 0, 1024), ("g_xattn", 1, 0, 1024), ("g_mem", 2, 0, 1024), ("g_ffn", 3, 0, 1024),
          ("g_ret_out", 4, 0, 512), ("g_xq", 4, 512, 256), ("g_xk", 4, 768, 256),
          ("g_fox_q", 5, 0, 64), ("g_fox_k", 5, 64, 64), ("b_forget", 5, 128, 8))
_LOSS_AT = (5, 256)


def _pack_small(tree):
    buf = jnp.zeros((SMALL_ROWS, D), F32)
    for name, r, c, n in _SMALL:
        buf = lax.dynamic_update_slice(buf, tree[name].reshape(1, n).astype(F32), (r, c))
    return buf


def _unpack_small(buf, like):
    return {name: buf[r:r + 1, c:c + n].reshape(like[name].shape) for name, r, c, n in _SMALL}


def _canonical(tree, name):
    a = tree[name][0]
    return a.T if W_SHARD[name][1] else a


def _from_canonical(a, name):
    return (a.T if W_SHARD[name][1] else a)[None]


def kernel(x, mem, g_mix, w_in, b_forget, g_ret_out, g_fox_q, g_fox_k, w_out, g_xattn, w_xq, w_xkv, g_mem, g_xq, g_xk, w_xo, g_ffn, w_gate, w_up, w_down, loss_target, m_g_mix, m_w_in, m_b_forget, m_g_ret_out, m_g_fox_q, m_g_fox_k, m_w_out, m_g_xattn, m_w_xq, m_w_xkv, m_g_mem, m_g_xq, m_g_xk, m_w_xo, m_g_ffn, m_w_gate, m_w_up, m_w_down, v_g_mix, v_w_in, v_b_forget, v_g_ret_out, v_g_fox_q, v_g_fox_k, v_w_out, v_g_xattn, v_w_xq, v_w_xkv, v_g_mem, v_g_xq, v_g_xk, v_w_xo, v_g_ffn, v_w_gate, v_w_up, v_w_down):
    names = ("g_mix", "w_in", "b_forget", "g_ret_out", "g_fox_q", "g_fox_k", "w_out", "g_xattn", "w_xq", "w_xkv", "g_mem",
             "g_xq", "g_xk", "w_xo", "g_ffn", "w_gate", "w_up", "w_down")
    w = dict(zip(names, (g_mix, w_in, b_forget, g_ret_out, g_fox_q, g_fox_k, w_out, g_xattn, w_xq, w_xkv, g_mem, g_xq, g_xk,
                         w_xo, g_ffn, w_gate, w_up, w_down)))
    m = dict(zip(names, (m_g_mix, m_w_in, m_b_forget, m_g_ret_out, m_g_fox_q, m_g_fox_k, m_w_out, m_g_xattn, m_w_xq, m_w_xkv,
                         m_g_mem, m_g_xq, m_g_xk, m_w_xo, m_g_ffn, m_w_gate, m_w_up, m_w_down)))
    v = dict(zip(names, (v_g_mix, v_w_in, v_b_forget, v_g_ret_out, v_g_fox_q, v_g_fox_k, v_w_out, v_g_xattn, v_w_xq, v_w_xkv,
                         v_g_mem, v_g_xq, v_g_xk, v_w_xo, v_g_ffn, v_w_gate, v_w_up, v_w_down)))
    small_names = [s[0] for s in _SMALL]
    me = 4 * lax.axis_index("x") + 2 * lax.axis_index("y") + lax.axis_index("c")

    first = _all_gather(_canonical(w, "w_in").astype(BF))
    first, rest = lax.optimization_barrier((first, [_canonical(w, n).astype(BF) for n in GATHER_REST]))
    rest_started = _exchange_start("gather_rest_start", rest, [lax.empty((N_DEV,) + a.shape, BF) for a in rest], scatter=False)

    def fetch_rest(after):
        srcs, lands = _exchange_wait("gather_rest_wait", rest_started, after, scatter=False)
        lands = [lax.dynamic_update_index_in_dim(a, own, me, axis=0) for a, own in zip(lands, srcs)]
        return {_CANON.get(n, n): a.reshape(N_DEV * a.shape[1], D) for n, a in zip(GATHER_REST, lands)}

    pushed = {}

    def push(group, grads):
        srcs = [grads[_CANON.get(n, n)].reshape(N_DEV, W_SHARD[n][0], D) for n in SCATTER_GROUPS[group]]
        pushed[group] = _exchange_start("scatter_%s_start" % group, srcs, [lax.empty(a.shape, BF) for a in srcs], scatter=True)
        return pushed[group][4]

    def push_small(gs, loss_part):
        small = lax.dynamic_update_slice(_pack_small(gs), loss_part[:, :1], _LOSS_AT)
        pushed["small"] = _exchange_start("gather_small_start", [small], [jnp.broadcast_to(small[None], (N_DEV,) + small.shape)],
                                          scatter=False)
        return pushed["small"][4]

    sp = {n: w[n].reshape(1, -1) for n in small_names}
    grad_x, done = _local_step(x[0], mem[0], loss_target[0], sp, first.reshape(N_DEV * W_SHARD["w_in"][0], D),
                               rest_started[4], fetch_rest, push, push_small)

    results, after = {}, done
    for group in ("ffn", "xattn", "out", "small", "in"):
        if group == "small":
            recv_small = _exchange_wait("gather_small_wait", pushed["small"], after, scatter=False)[1][0]
            g_sm, d_sm, m_sm, v_sm = _adamw("adamw_small", recv_small, None, _pack_small(w), _pack_small(m), _pack_small(v))
            after = g_sm
            continue
        sents, recvs = _exchange_wait("scatter_%s_wait" % group, pushed[group], after, scatter=True)
        for name, sent, recv in zip(SCATTER_GROUPS[group], sents, recvs):
            own = lax.dynamic_index_in_dim(sent, me, axis=0, keepdims=False)
            res = _adamw("adamw_" + name, recv, own, *(_canonical(t, name) for t in (w, m, v)))
            results[name] = [_from_canonical(r, name) for r in res]
        after = results[SCATTER_GROUPS[group][-1]][0]
    loss = g_sm[_LOSS_AT[0], _LOSS_AT[1]]

    outs = []
    for k, sm in enumerate((g_sm, d_sm, m_sm, v_sm)):
        tree = _unpack_small(sm, w)
        tree.update({name: res[k] for name, res in results.items()})
        outs += [tree[n] for n in names]
    return (loss, grad_x[None], *outs)
```

```python
import jax
import jax.numpy as jnp
from jax import lax
from jax.experimental import pallas as pl
from jax.experimental.pallas import tpu as pltpu

F32 = jnp.float32
BF = jnp.bfloat16

D = 1024
HEAD = 64
CHUNK = 64
N_MEM = 256
XHEAD = 256
D_FF = 2816
EPS = 1e-6
NEG = -1e30
LANES = 128
N_DEV = 8
V7X_VMEM_BYTES = 64 * 1024 * 1024
VMEM_LIMIT = V7X_VMEM_BYTES - 8 * 1024 * 1024

ADAM_LR, ADAM_B1, ADAM_B2, ADAM_EPS, ADAM_WD, ADAM_STEP = 0.001, 0.9, 0.999, 1e-08, 0.01, 10

W_SHARD = {"w_in": (449, True), "w_out": (128, False), "w_xq": (128, False), "w_xkv": (256, True),
           "w_xo": (128, False), "w_gate": (352, True), "w_up": (352, True), "w_down": (352, False)}
GATHER_REST = ("w_out", "w_xq", "w_xkv", "w_xo", "w_gate", "w_up", "w_down")
SCATTER_GROUPS = {"ffn": ("w_gate", "w_up", "w_down"), "xattn": ("w_xq", "w_xo", "w_xkv"), "out": ("w_out",), "in": ("w_in",)}
SMALL_ROWS = 8

NT = (((1,), (1,)), ((), ()))
NN = (((1,), (0,)), ((), ()))
TN = (((0,), (0,)), ((), ()))
_DIMS = {"nn": NN, "nt": NT, "tn": TN}


def _params(sem):
    return pltpu.CompilerParams(dimension_semantics=sem, vmem_limit_bytes=VMEM_LIMIT)


def _mm(name, products, extras, epilogue, M, N, tm, tn, out_dtypes, params=(), n_acc=0):
    assert n_acc == 0 or tn == N
    flat = [t for p in products for t in p]
    counts = [len(p) for p in products]
    in_specs, args, where, slots = [], [], {}, []

    def operand(arr, spec, kind):
        key = (id(arr), kind)
        if key not in where:
            where[key] = len(args)
            args.append(arr)
            in_specs.append(spec)
        return where[key]

    for a, b, form in flat:
        if form == "tn":
            ia = operand(a, pl.BlockSpec((a.shape[0], tm), lambda i, j: (0, i)), "a_tn")
        else:
            ia = operand(a, pl.BlockSpec((tm, a.shape[1]), lambda i, j: (i, 0)), "a")
        if form == "nt":
            ib = operand(b, pl.BlockSpec((tn, b.shape[1]), lambda i, j: (j, 0)), "b_nt")
        else:
            ib = operand(b, pl.BlockSpec((b.shape[0], tn), lambda i, j: (0, j)), "b")
        slots.append((ia, ib))
    n_mm = len(args)
    for e in extras:
        in_specs.append(pl.BlockSpec((tm, tn), lambda i, j: (i, j)))
        args.append(e)
    for p in params:
        in_specs.append(pl.BlockSpec((1, tn), lambda i, j: (0, j)))
        args.append(p)
    n_in = len(args)
    n_out = len(out_dtypes)

    def body(*refs):
        ins, outs = refs[:n_in], refs[n_in:]
        prods, p = [], 0
        for c in counts:
            acc = None
            for _ in range(c):
                a = ins[slots[p][0]][...].astype(BF)
                b = ins[slots[p][1]][...].astype(BF)
                d = lax.dot_general(a, b, _DIMS[flat[p][2]], preferred_element_type=F32)
                acc = d if acc is None else acc + d
                p += 1
            prods.append(acc)
        ex = [r[...].astype(F32) for r in ins[n_mm:]]
        res = epilogue(*prods, *ex)
        for o, r in zip(outs[:n_out], res[:n_out]):
            o[...] = r.astype(o.dtype)
        for o, r in zip(outs[n_out:], res[n_out:]):
            @pl.when(pl.program_id(0) == 0)
            def _(o=o):
                o[...] = jnp.zeros(o.shape, F32)
            o[...] += r

    return pl.pallas_call(
        body, name=name, grid=(M // tm, N // tn), in_specs=in_specs,
        out_specs=[pl.BlockSpec((tm, tn), lambda i, j: (i, j)) for _ in out_dtypes]
        + [pl.BlockSpec((1, tn), lambda i, j: (0, j)) for _ in range(n_acc)],
        out_shape=[jax.ShapeDtypeStruct((M, N), dt) for dt in out_dtypes] + [jax.ShapeDtypeStruct((1, N), F32)] * n_acc,
        compiler_params=_params(("arbitrary", "arbitrary")),
    )(*args)


def _ident(x):
    return (x,)


def _each(*xs):
    return xs


def _spec(rows, w, off, per_j):
    if per_j:
        return pl.BlockSpec((rows, w), lambda j, i: (i, off + j))
    return pl.BlockSpec((rows, w), lambda j, i: (i, off))


def _pspec(rows, w, off, per_j):
    if per_j:
        return pl.BlockSpec((rows, w), lambda j, i: (0, off + j))
    return pl.BlockSpec((rows, w), lambda j, i: (0, off))


def _rw_fwd(name, fn, rows, params, outs, T, tm, nj, n_acc=0):
    in_specs = [_spec(tm, w, off, pj) for _, w, off, pj in rows] + [_pspec(a.shape[0], w, off, pj) for a, w, off, pj in params]
    args = [r[0] for r in rows] + [p[0] for p in params]
    n_in, n_out = len(args), len(outs)
    out_specs = [pl.BlockSpec((tm, w), lambda j, i: (i, j)) for _, w in outs]
    out_shape = [jax.ShapeDtypeStruct((T, nj * w), dt) for dt, w in outs]
    out_specs += [pl.BlockSpec((1, LANES), lambda j, i: (0, 0)) for _ in range(n_acc)]
    out_shape += [jax.ShapeDtypeStruct((1, LANES), F32) for _ in range(n_acc)]

    def body(*refs):
        vals = [r[...].astype(F32) for r in refs[:n_in]]
        res = fn(*vals)
        orefs = refs[n_in:]
        for k in range(n_out):
            orefs[k][...] = res[k].astype(orefs[k].dtype)
        first = (pl.program_id(0) == 0) & (pl.program_id(1) == 0)
        for k in range(n_acc):
            @pl.when(first)
            def _(k=k):
                orefs[n_out + k][...] = jnp.zeros((1, LANES), F32)
            orefs[n_out + k][...] += res[n_out + k]

    return pl.pallas_call(
        body, name=name, grid=(nj, T // tm), in_specs=in_specs, out_specs=out_specs, out_shape=out_shape,
        compiler_params=_params(("arbitrary", "arbitrary")),
    )(*args)


def _rw_bwd(name, fn, rows, params, cots, T, tm, nj, row_grads, param_grads, resid=None):
    in_specs = ([_spec(tm, w, off, pj) for _, w, off, pj in rows] + [_pspec(a.shape[0], w, off, pj) for a, w, off, pj in params]
                + [_spec(tm, w, off, pj) for _, w, off, pj in cots])
    args = [r[0] for r in rows] + [p[0] for p in params] + [c[0] for c in cots]
    if resid is not None:
        in_specs.append(_spec(tm, rows[0][1], rows[0][2], rows[0][3]))
        args.append(resid)
    nr, npar, nc = len(rows), len(params), len(cots)
    out_specs, out_shape, kinds = [], [], []
    for k, dts in enumerate(row_grads):
        for dt in (dts if isinstance(dts, (list, tuple)) else [dts]):
            if dt is not None:
                w = rows[k][1]
                out_specs.append(pl.BlockSpec((tm, w), lambda j, i: (i, j)))
                out_shape.append(jax.ShapeDtypeStruct((T, nj * w), dt))
                kinds.append(("row", k))
    for k, need in enumerate(param_grads):
        if need:
            a, w, off, pj = params[k]
            out_specs.append(_pspec(a.shape[0], w, off, pj))
            out_shape.append(jax.ShapeDtypeStruct(a.shape, F32))
            kinds.append(("par", k))

    def body(*refs):
        vals = [r[...].astype(F32) for r in refs[:nr + npar]]
        ct = tuple(r[...].astype(F32) for r in refs[nr + npar:nr + npar + nc])
        _, vjp = jax.vjp(lambda *a: tuple(fn(*a)), *vals)
        grads = list(vjp(ct))
        n_in = nr + npar + nc + (resid is not None)
        if resid is not None:
            grads[0] = grads[0] + refs[n_in - 1][...].astype(F32)
        orefs = refs[n_in:]
        j, i = pl.program_id(0), pl.program_id(1)
        for o, (kind, k) in zip(orefs, kinds):
            if kind == "row":
                o[...] = grads[k].astype(o.dtype)
            else:
                first = (i == 0) if params[k][3] else ((i == 0) & (j == 0))

                @pl.when(first)
                def _(o=o):
                    o[...] = jnp.zeros(o.shape, F32)
                o[...] += grads[nr + k]

    return pl.pallas_call(
        body, name=name, grid=(nj, T // tm), in_specs=in_specs, out_specs=out_specs, out_shape=out_shape,
        compiler_params=_params(("arbitrary", "arbitrary")),
    )(*args)


def _rms(x, g):
    return x * lax.rsqrt(jnp.mean(x * x, axis=-1, keepdims=True) + EPS) * g


def _rms_fn(x, g):
    return (_rms(x, g),)


def _lo_mask():
    return lax.broadcasted_iota(jnp.int32, (1, LANES), 1) < HEAD


def _gmean(x, lo):
    s0 = jnp.sum(jnp.where(lo, x, 0.0), axis=-1, keepdims=True)
    s1 = jnp.sum(jnp.where(lo, 0.0, x), axis=-1, keepdims=True)
    return jnp.where(lo, s0, s1) * (1.0 / HEAD)


def _fox_prep_fn(fq, fk, gq, gk):
    lo = _lo_mask()
    qn = fq * lax.rsqrt(_gmean(fq * fq, lo) + EPS) * gq * (HEAD ** -0.5)
    kn = fk * lax.rsqrt(_gmean(fk * fk, lo) + EPS) * gk
    return qn, kn


@jax.custom_vjp
def _swap_halves(x):
    bit = (lax.broadcasted_iota(jnp.int32, (1, LANES), 1) & (HEAD // 2)) == 0
    return jnp.where(bit, pltpu.roll(x, LANES - HEAD // 2, 1), pltpu.roll(x, HEAD // 2, 1))


_swap_halves.defvjp(lambda x: (_swap_halves(x), None), lambda _, g: (_swap_halves(g),))


def _ret_fn(rq, rk, rv, rg, cos, sin, s_in, g, lg):
    tb = rq.shape[0]
    nc = tb // CHUNK
    lo = _lo_mask()
    row = lax.broadcasted_iota(jnp.int32, (LANES, 1), 0) < HEAD
    same_head = row == lo
    q = (rq * cos + _swap_halves(rq) * sin) * (HEAD ** -0.5)
    k = rk * cos + _swap_halves(rk) * sin
    q3, k3, v3 = q.reshape(nc, CHUNK, LANES), k.reshape(nc, CHUNK, LANES), rv.reshape(nc, CHUNK, LANES)
    pos = lax.broadcasted_iota(jnp.int32, (CHUNK, 1), 0).astype(F32)
    q_decay = jnp.exp(lg * (pos + 1.0))
    k_decay = jnp.exp(lg * (CHUNK - 1.0 - pos))
    chunk_decay = jnp.exp(lg * float(CHUNK))
    dist = jnp.abs(lax.broadcasted_iota(jnp.int32, (CHUNK, CHUNK), 0) - lax.broadcasted_iota(jnp.int32, (CHUNK, CHUNK), 1)).astype(F32)
    v3b = v3.astype(BF)
    intra = []
    for hh in range(2):
        hm = lo if hh == 0 else ~lo
        lg_h = lg[:, hh * HEAD:hh * HEAD + 1]
        qm = jnp.where(hm, q3, 0.0).astype(BF)
        sc = jnp.einsum("nid,njd->nij", qm, k3.astype(BF), preferred_element_type=F32) * jnp.exp(lg_h * dist)[None]
        intra.append(jnp.einsum("nij,nje->nie", sc.astype(BF), v3b, preferred_element_type=F32))
    o = jnp.where(lo, intra[0], intra[1])
    kv = jnp.einsum("njd,nje->nde", (k3 * k_decay[None]).astype(BF), v3b, preferred_element_type=F32)
    kv = jnp.where(same_head[None], kv, 0.0)
    state, states = s_in, []
    for n in range(nc):
        states.append(state)
        state = state * chunk_decay + kv[n]
    s_prev = jnp.stack(states, axis=0)
    o = o + jnp.einsum("nid,nde->nie", (q3 * q_decay[None]).astype(BF), s_prev.astype(BF), preferred_element_type=F32)
    o = o.reshape(tb, LANES)
    mu = _gmean(o, lo)
    oc = o - mu
    y = oc * lax.rsqrt(_gmean(oc * oc, lo) + EPS) * g
    return jax.nn.silu(rg) * y, state


def _xattn_fn(qx, gq, gk, kk, vv):
    q = _rms(qx, gq)
    k = _rms(kk, gk)
    logits = lax.dot_general(q.astype(BF), k.astype(BF), NT, preferred_element_type=F32) * (XHEAD ** -0.5)
    p = jax.nn.softmax(logits, axis=-1)
    return (jnp.dot(p.astype(BF), vv.astype(BF), preferred_element_type=F32),)


def _swiglu_fwd_epi(g, u):
    return g, u, jax.nn.silu(g) * u


def _swiglu_bwd_epi(dact, g, u):
    _, vjp = jax.vjp(lambda a, b: jax.nn.silu(a) * b, g, u)
    return vjp(dact)


def _add_rms_epi(acc, resid, g):
    h = acc + resid
    return h, _rms(h, g)


def _add_loss_epi(acc, resid, target):
    err = (acc + resid) - target
    dy = err * (1.0 / D)
    part = jnp.sum(jnp.sum(err * err, axis=0, keepdims=True), axis=1, keepdims=True) * (0.5 / D)
    return dy, dy, jnp.broadcast_to(part, (1, err.shape[1]))


def _rms_bwd_epi(dhn, h, skip, g):
    _, vjp = jax.vjp(_rms, h, g)
    dh, dg = vjp(dhn)
    dh = dh + skip
    return dh, dh, dg


def _ret_fwd(P, cos, sin, g_ret, lg, T, tb):
    nb = T // tb

    def body(rq, rk, rv, rg, c, s, g, l, o_ref, s0_ref, state):
        @pl.when(pl.program_id(1) == 0)
        def _():
            state[...] = jnp.zeros(state.shape, F32)
        s0_ref[0, 0] = state[...]
        out, s_new = _ret_fn(rq[...], rk[...], rv[...], rg[...], c[...], s[...], state[...], g[...], l[...])
        o_ref[...] = out.astype(o_ref.dtype)
        state[...] = s_new

    sec = lambda off: pl.BlockSpec((tb, LANES), lambda j, i: (i, off + j))
    tab = pl.BlockSpec((tb, LANES), lambda j, i: (i, 0))
    par = pl.BlockSpec((1, LANES), lambda j, i: (0, j))
    return pl.pallas_call(
        body, name="ret_fwd", grid=(4, nb),
        in_specs=[sec(0), sec(4), sec(8), sec(12), tab, tab, par, par],
        out_specs=[pl.BlockSpec((tb, LANES), lambda j, i: (i, j)), pl.BlockSpec((1, 1, LANES, LANES), lambda j, i: (j, i, 0, 0))],
        out_shape=[jax.ShapeDtypeStruct((T, 4 * LANES), BF), jax.ShapeDtypeStruct((4, nb, LANES, LANES), F32)],
        scratch_shapes=[pltpu.VMEM((LANES, LANES), F32)],
        compiler_params=_params(("arbitrary", "arbitrary")),
    )(P, P, P, P, cos, sin, g_ret, lg)


def _ret_bwd(P, cos, sin, g_ret, lg, s0, dmix, T, tb):
    nb = T // tb

    def body(rq, rk, rv, rg, c, s, g, l, s0_ref, do, drq, drk, drv, drg, dg, dstate):
        i = pl.program_id(1)

        @pl.when(i == 0)
        def _():
            dstate[...] = jnp.zeros(dstate.shape, F32)
            dg[...] = jnp.zeros(dg.shape, F32)

        cc, ss, ll = c[...], s[...], l[...]
        _, vjp = jax.vjp(lambda a, b, v, gate, st, gg: _ret_fn(a, b, v, gate, cc, ss, st, gg, ll),
                         rq[...], rk[...], rv[...], rg[...], s0_ref[0, 0], g[...])
        ga, gb, gv, ggate, gst, ggain = vjp((do[...], dstate[...]))
        drq[...] = ga.astype(drq.dtype)
        drk[...] = gb.astype(drk.dtype)
        drv[...] = gv.astype(drv.dtype)
        drg[...] = ggate.astype(drg.dtype)
        dstate[...] = gst
        dg[...] += ggain

    rev = lambda i: nb - 1 - i
    sec = lambda off: pl.BlockSpec((tb, LANES), lambda j, i: (rev(i), off + j))
    tab = pl.BlockSpec((tb, LANES), lambda j, i: (rev(i), 0))
    par = pl.BlockSpec((1, LANES), lambda j, i: (0, j))
    outb = pl.BlockSpec((tb, LANES), lambda j, i: (rev(i), j))
    return pl.pallas_call(
        body, name="ret_bwd", grid=(4, nb),
        in_specs=[sec(0), sec(4), sec(8), sec(12), tab, tab, par, par,
                  pl.BlockSpec((1, 1, LANES, LANES), lambda j, i: (j, rev(i), 0, 0)), outb],
        out_specs=[outb, outb, outb, outb, par],
        out_shape=[jax.ShapeDtypeStruct((T, 4 * LANES), BF)] * 4 + [jax.ShapeDtypeStruct((1, 4 * LANES), F32)],
        scratch_shapes=[pltpu.VMEM((LANES, LANES), F32)],
        compiler_params=_params(("arbitrary", "arbitrary")),
    )(P, P, P, P, cos, sin, g_ret, lg, s0, dmix)


_FB = 128


def _tri(lower):
    r = lax.broadcasted_iota(jnp.int32, (_FB, _FB), 0)
    c = lax.broadcasted_iota(jnp.int32, (_FB, _FB), 1)
    return ((r >= c) if lower else (r <= c)).astype(F32)


def _fgate_fwd(ffp, bpad, T):
    def body(ff_ref, b_ref, fc_ref, fr_ref):
        lane = lax.broadcasted_iota(jnp.int32, (1, LANES), 1)
        tri = _tri(True)
        carry = jnp.zeros((1, LANES), F32)
        for blk in range(T // _FB):
            z = ff_ref[blk * _FB:(blk + 1) * _FB, :] + b_ref[...]
            lf = jnp.where(lane < 8, jax.nn.log_sigmoid(z), 0.0)
            f = jnp.dot(tri, lf, precision=lax.Precision.HIGHEST, preferred_element_type=F32) + carry
            carry = f[_FB - 1:_FB, :]
            fc_ref[blk * _FB:(blk + 1) * _FB, :] = f
            fr_ref[:, blk * _FB:(blk + 1) * _FB] = f.T[:8, :]

    return pl.pallas_call(
        body, name="fgate_fwd",
        out_shape=[jax.ShapeDtypeStruct((T, LANES), F32), jax.ShapeDtypeStruct((8, T), F32)],
        compiler_params=pltpu.CompilerParams(vmem_limit_bytes=VMEM_LIMIT),
    )(ffp, bpad)


_BIAS_LANE = HEAD


def _head_bias_col(fc, head):
    lane = lax.broadcasted_iota(jnp.int32, (1, LANES), 1)
    return jnp.sum(jnp.where(lane == head, fc, 0.0), axis=-1, keepdims=True)


def _split3(f):
    hi = f.astype(BF).astype(F32)
    mid = (f - hi).astype(BF).astype(F32)
    lo = ((f - hi) - mid).astype(BF).astype(F32)
    return hi, mid, lo


def _fox_operands(P, fc, g_fq2, g_fk2, T, tm):
    def body(fq_ref, fk_ref, fv_ref, fc_ref, gq_ref, gk_ref, qa_ref, qat_ref, ka_ref, kat_ref, va_ref, vat_ref):
        j = pl.program_id(0)
        lane = lax.broadcasted_iota(jnp.int32, (1, LANES), 1)
        qn, kn = _fox_prep_fn(fq_ref[...], fk_ref[...], gq_ref[...], gk_ref[...])
        v = fv_ref[...]
        fcb = fc_ref[...]
        b = _BIAS_LANE
        for hh in range(2):
            hi, mid, lo = _split3(_head_bias_col(fcb, 2 * j + hh))
            take = (lambda a: a) if hh == 0 else (lambda a: pltpu.roll(a, HEAD, 1))
            qa = jnp.where(lane < HEAD, take(qn), jnp.where(lane == b, hi, jnp.where(lane == b + 1, mid, jnp.where(
                lane == b + 2, lo, jnp.where(lane < b + 6, 1.0, 0.0)))))
            ka = jnp.where(lane < HEAD, take(kn), jnp.where(lane < b + 3, 1.0, jnp.where(lane == b + 3, -hi, jnp.where(
                lane == b + 4, -mid, jnp.where(lane == b + 5, -lo, 0.0)))))
            va = jnp.where(lane < HEAD, take(v), 0.0)
            for val, ref, tref in ((qa, qa_ref, qat_ref), (ka, ka_ref, kat_ref), (va, va_ref, vat_ref)):
                ref[hh] = val.astype(BF)
                tref[hh] = val.T.astype(BF)

    sec = lambda off: pl.BlockSpec((tm, LANES), lambda j, i: (i, off + j))
    par = pl.BlockSpec((1, LANES), lambda j, i: (0, 0))
    nat = pl.BlockSpec((2, tm, LANES), lambda j, i: (j, i, 0))
    trn = pl.BlockSpec((2, LANES, tm), lambda j, i: (j, 0, i))
    return pl.pallas_call(
        body, name="fox_operands", grid=(4, T // tm),
        in_specs=[sec(16), sec(20), sec(24), pl.BlockSpec((tm, LANES), lambda j, i: (i, 0)), par, par],
        out_specs=[nat, trn, nat, trn, nat, trn],
        out_shape=[jax.ShapeDtypeStruct((8, T, LANES), BF), jax.ShapeDtypeStruct((8, LANES, T), BF)] * 3,
        compiler_params=_params(("parallel", "arbitrary")),
    )(P, P, P, fc, g_fq2, g_fk2)


def _fox_forward(qat, ka, vat, T, tq, tk):
    nq, per = T // tq, tq // tk
    assert per == 2
    RC = 64

    def body(qat_ref, ka_ref, vat_ref, o_ref, lse_ref, s_scr, p_scr, a_scr, m_scr, l_scr, acc_scr):
        i = pl.program_id(1)
        sub = lax.broadcasted_iota(jnp.int32, (8, 1), 0)
        row = lax.broadcasted_iota(jnp.int32, (RC, tq), 0)
        col = lax.broadcasted_iota(jnp.int32, (RC, tq), 1)
        m_scr[...] = jnp.full(m_scr.shape, NEG, F32)
        l_scr[...] = jnp.zeros(l_scr.shape, F32)
        acc_scr[...] = jnp.zeros(acc_scr.shape, F32)

        def scores(slot, kb):
            k0 = pl.multiple_of(kb * tk, tk)
            for hh in range(2):
                s_scr[slot, hh] = jnp.dot(ka_ref[hh, pl.ds(k0, tk), :], qat_ref[hh], preferred_element_type=F32)

        def softmax(slot, kb, diagonal):
            shift = kb * tk - i * tq
            for hh in range(2):
                def masked(r):
                    tile = s_scr[slot, hh, r * RC:(r + 1) * RC, :]
                    return jnp.where(row + (r * RC + shift) <= col, tile, NEG) if diagonal else tile

                mx = jnp.max(masked(0), axis=0, keepdims=True)
                for r in range(1, tk // RC):
                    mx = jnp.maximum(mx, jnp.max(masked(r), axis=0, keepdims=True))
                m_old = m_scr[hh, 0:1, :]
                m2 = jnp.maximum(m_old, mx)
                a = jnp.exp(m_old - m2)
                lsum = jnp.zeros((1, tq), F32)
                for r in range(tk // RC):
                    p = jnp.exp(masked(r) - m2)
                    p_scr[slot, hh, r * RC:(r + 1) * RC, :] = p.astype(BF)
                    lsum = lsum + jnp.sum(p, axis=0, keepdims=True)
                m_scr[hh] = jnp.broadcast_to(m2, (8, tq))
                l_scr[hh] = jnp.broadcast_to(a * l_scr[hh, 0:1, :] + lsum, (8, tq))
                a_scr[slot, hh] = jnp.broadcast_to(a, (8, tq))

        def values(slot, kb):
            k0 = pl.multiple_of(kb * tk, tk)
            for hh in range(2):
                pv = jnp.dot(vat_ref[hh, 0:HEAD, pl.ds(k0, tk)], p_scr[slot, hh], preferred_element_type=F32)
                acc_scr[hh] = a_scr[slot, hh, 0:1, :] * acc_scr[hh] + pv

        def pair(kb, diag_first, diag_second, more):
            if more:
                scores(0, kb + 2)
            softmax(1, kb + 1, diag_first)
            values(0, kb)
            if more:
                scores(1, kb + 3)
                softmax(0, kb + 2, diag_second)
            values(1, kb + 1)

        scores(0, 0)
        scores(1, 1)
        softmax(0, 0, True)

        @pl.loop(0, jnp.maximum(i - 1, 0))
        def _(t):
            pair(2 * t, False, False, True)

        @pl.when(i >= 1)
        def _():
            pair(2 * (i - 1), False, True, True)

        pair(2 * i, True, False, False)

        o_ref[...] = jnp.concatenate([acc_scr[hh] / l_scr[hh, 0:1, :] for hh in range(2)], axis=0).T
        lses = [m_scr[hh, 0:1, :] + jnp.log(l_scr[hh, 0:1, :]) for hh in range(2)]
        lse_ref[0] = jnp.where(sub == 0, lses[0], jnp.where(sub == 1, lses[1], 0.0))

    return pl.pallas_call(
        body, name="fox_forward", grid=(4, nq),
        in_specs=[pl.BlockSpec((2, LANES, tq), lambda j, i: (j, 0, i)), pl.BlockSpec((2, T, LANES), lambda j, i: (j, 0, 0)),
                  pl.BlockSpec((2, LANES, T), lambda j, i: (j, 0, 0))],
        out_specs=[pl.BlockSpec((tq, LANES), lambda j, i: (i, j)), pl.BlockSpec((1, 8, tq), lambda j, i: (j, 0, i))],
        out_shape=[jax.ShapeDtypeStruct((T, 4 * LANES), F32), jax.ShapeDtypeStruct((4, 8, T), F32)],
        scratch_shapes=[pltpu.VMEM((2, 2, tk, tq), F32), pltpu.VMEM((2, 2, tk, tq), BF), pltpu.VMEM((2, 2, 8, tq), F32),
                        pltpu.VMEM((2, 8, tq), F32), pltpu.VMEM((2, 8, tq), F32), pltpu.VMEM((2, HEAD, tq), F32)],
        compiler_params=_params(("parallel", "arbitrary")),
    )(qat, ka, vat)


def _fox_cotangent(dmix, fox, T, tm):
    def body(do_ref, o_ref, doa_ref, doat_ref, dl_ref):
        lane = lax.broadcasted_iota(jnp.int32, (1, LANES), 1)
        sub = lax.broadcasted_iota(jnp.int32, (8, 1), 0)
        dob = do_ref[...].astype(BF).astype(F32)
        prod_t = (dob * o_ref[...]).T
        d0 = jnp.sum(prod_t[:HEAD], axis=0, keepdims=True)
        d1 = jnp.sum(prod_t[HEAD:], axis=0, keepdims=True)
        dl_ref[0] = jnp.where(sub == 0, d0, jnp.where(sub == 1, d1, 0.0))
        for hh in range(2):
            val = jnp.where(lane < HEAD, dob if hh == 0 else pltpu.roll(dob, HEAD, 1), 0.0)
            doa_ref[hh] = val.astype(BF)
            doat_ref[hh] = val.T.astype(BF)

    return pl.pallas_call(
        body, name="fox_cotangent", grid=(4, T // tm),
        in_specs=[pl.BlockSpec((tm, LANES), lambda j, i: (i, 4 + j)), pl.BlockSpec((tm, LANES), lambda j, i: (i, j))],
        out_specs=[pl.BlockSpec((2, tm, LANES), lambda j, i: (j, i, 0)), pl.BlockSpec((2, LANES, tm), lambda j, i: (j, 0, i)),
                   pl.BlockSpec((1, 8, tm), lambda j, i: (j, 0, i))],
        out_shape=[jax.ShapeDtypeStruct((8, T, LANES), BF), jax.ShapeDtypeStruct((8, LANES, T), BF),
                   jax.ShapeDtypeStruct((4, 8, T), F32)],
        compiler_params=_params(("parallel", "arbitrary")),
    )(dmix, fox)


def _fox_backward(qa, qat, ka, kat, va, doa, doat, lse, dl, T, tq, tk):
    nq, nk = T // tq, T // tk

    def body(qa_ref, qat_ref, ka_ref, kat_ref, va_ref, doa_ref, doat_ref, lse_ref, dl_ref,
             dq_ref, dk_ref, dv_ref, df_ref, dr_ref, dqt, dk_acc, dv_acc, df_acc, sdp, pds):
        j, kb = pl.program_id(0), pl.program_id(1)
        lane = lax.broadcasted_iota(jnp.int32, (1, LANES), 1)
        first = (kb * tk) // tq

        @pl.when(kb == 0)
        def _():
            dqt[...] = jnp.zeros(dqt.shape, F32)

        dk_acc[...] = jnp.zeros(dk_acc.shape, F32)
        dv_acc[...] = jnp.zeros(dv_acc.shape, F32)
        df_acc[...] = jnp.zeros(df_acc.shape, F32)

        RC = 64
        last = nq - 1

        def products(slot, qi):
            q0 = pl.multiple_of(qi * tq, tq)
            for hh in range(2):
                sdp[slot, hh, 0] = jnp.dot(ka_ref[hh], qat_ref[hh, :, pl.ds(q0, tq)], preferred_element_type=F32)
                sdp[slot, hh, 1] = jnp.dot(va_ref[hh], doat_ref[hh, :, pl.ds(q0, tq)], preferred_element_type=F32)

        def softmax_bwd(slot, qi, diagonal, valid):
            q0 = pl.multiple_of(qi * tq, tq)
            shift = kb * tk - first * tq
            col = lax.broadcasted_iota(jnp.int32, (RC, tq), 1)
            row = lax.broadcasted_iota(jnp.int32, (RC, tq), 0)
            for hh in range(2):
                lse_row = lse_ref[0, hh:hh + 1, pl.ds(q0, tq)]
                dl_row = dl_ref[0, hh:hh + 1, pl.ds(q0, tq)]
                rsum = jnp.zeros((1, tq), F32)
                for r in range(tk // RC):
                    rows = slice(r * RC, (r + 1) * RC)
                    p = jnp.exp(sdp[slot, hh, 0, rows, :] - lse_row)
                    p = jnp.where((row + (r * RC + shift) <= col) if diagonal else valid, p, 0.0)
                    ds = p * (sdp[slot, hh, 1, rows, :] - dl_row)
                    pds[slot, hh, 0, rows, :] = p.astype(BF)
                    pds[slot, hh, 1, rows, :] = ds.astype(BF)
                    rsum = rsum + jnp.sum(ds, axis=0, keepdims=True)
                    part = ds[:, 0:LANES]
                    for c in range(1, tq // LANES):
                        part = part + ds[:, c * LANES:(c + 1) * LANES]
                    df_acc[hh, rows, :] += part
                dqt[hh, HEAD:HEAD + 8, pl.ds(q0, tq)] += jnp.broadcast_to(rsum, (8, tq))

        def accumulate(slot, qi):
            q0 = pl.multiple_of(qi * tq, tq)
            for hh in range(2):
                dv_acc[hh] += jnp.dot(pds[slot, hh, 0], doa_ref[hh, pl.ds(q0, tq), :], preferred_element_type=F32)
                dk_acc[hh] += jnp.dot(pds[slot, hh, 1], qa_ref[hh, pl.ds(q0, tq), :], preferred_element_type=F32)
                dqt[hh, 0:HEAD, pl.ds(q0, tq)] += jnp.dot(kat_ref[hh, 0:HEAD, :], pds[slot, hh, 1], preferred_element_type=F32)

        products(0, first)
        products(1, jnp.minimum(first + 1, last))
        softmax_bwd(0, first, True, None)

        @pl.loop(0, (nq - first + 1) // 2)
        def _(t):
            qi = first + 2 * t
            products(0, jnp.minimum(qi + 2, last))
            softmax_bwd(1, jnp.minimum(qi + 1, last), False, qi + 1 <= last)
            accumulate(0, qi)
            products(1, jnp.minimum(qi + 3, last))
            softmax_bwd(0, jnp.minimum(qi + 2, last), False, qi + 2 <= last)
            accumulate(1, jnp.minimum(qi + 1, last))

        lo = lane < HEAD
        dk_ref[...] = jnp.where(lo, dk_acc[0], pltpu.roll(dk_acc[1], HEAD, 1))
        dv_ref[...] = jnp.where(lo, dv_acc[0], pltpu.roll(dv_acc[1], HEAD, 1)).astype(dv_ref.dtype)
        f0 = -jnp.sum(df_acc[0], axis=1, keepdims=True)
        f1 = -jnp.sum(df_acc[1], axis=1, keepdims=True)
        df_ref[0] = jnp.where(lane == 2 * j, f0, jnp.where(lane == 2 * j + 1, f1, 0.0))

        @pl.when(kb == nk - 1)
        def _():
            for t in range(nq):
                cols = slice(t * tq, (t + 1) * tq)
                dq_ref[cols, :] = jnp.concatenate([dqt[0, 0:HEAD, cols], dqt[1, 0:HEAD, cols]], axis=0).T
                rsum = jnp.concatenate([dqt[0, HEAD:HEAD + 8, cols], dqt[1, HEAD:HEAD + 8, cols],
                                        jnp.zeros((LANES - 16, tq), F32)], axis=0).T
                dr_ref[0, cols, :] = jnp.where(lane == 2 * j, rsum[:, 0:1], jnp.where(lane == 2 * j + 1, rsum[:, 8:9], 0.0))

    nat_full = pl.BlockSpec((2, T, LANES), lambda j, kb: (j, 0, 0))
    trn_full = pl.BlockSpec((2, LANES, T), lambda j, kb: (j, 0, 0))
    nat_blk = pl.BlockSpec((2, tk, LANES), lambda j, kb: (j, kb, 0))
    trn_blk = pl.BlockSpec((2, LANES, tk), lambda j, kb: (j, 0, kb))
    rows = pl.BlockSpec((1, 8, T), lambda j, kb: (j, 0, 0))
    blk = pl.BlockSpec((tk, LANES), lambda j, kb: (kb, j))
    return pl.pallas_call(
        body, name="fox_backward", grid=(4, nk),
        in_specs=[nat_full, trn_full, nat_blk, trn_blk, nat_blk, nat_full, trn_full, rows, rows],
        out_specs=[pl.BlockSpec((T, LANES), lambda j, kb: (0, j)), blk, blk, pl.BlockSpec((1, tk, LANES), lambda j, kb: (j, kb, 0)),
                   pl.BlockSpec((1, T, LANES), lambda j, kb: (j, 0, 0))],
        out_shape=[jax.ShapeDtypeStruct((T, 4 * LANES), F32), jax.ShapeDtypeStruct((T, 4 * LANES), F32),
                   jax.ShapeDtypeStruct((T, 4 * LANES), BF), jax.ShapeDtypeStruct((4, T, LANES), F32),
                   jax.ShapeDtypeStruct((4, T, LANES), F32)],
        scratch_shapes=[pltpu.VMEM((2, HEAD + 8, T), F32), pltpu.VMEM((2, tk, LANES), F32), pltpu.VMEM((2, tk, LANES), F32),
                        pltpu.VMEM((2, tk, LANES), F32), pltpu.VMEM((2, 2, 2, tk, tq), F32), pltpu.VMEM((2, 2, 2, tk, tq), BF)],
        compiler_params=_params(("arbitrary", "arbitrary")),
    )(qa, qat, ka, kat, va, doa, doat, lse, dl)


def _fgate_bwd_col(ffp, bpad, dfc4, drc4, T):
    def body(ff_ref, b_ref, dfc_ref, drc_ref, dff_ref, db_ref):
        lane = lax.broadcasted_iota(jnp.int32, (1, LANES), 1)
        tri = _tri(False)
        carry = jnp.zeros((1, LANES), F32)
        db = jnp.zeros((1, LANES), F32)
        for blk in reversed(range(T // _FB)):
            rows = slice(blk * _FB, (blk + 1) * _FB)
            dcol = dfc_ref[0, rows, :] + drc_ref[0, rows, :]
            for pair in range(1, 4):
                dcol = dcol + (dfc_ref[pair, rows, :] + drc_ref[pair, rows, :])
            dlf = jnp.dot(tri, dcol, precision=lax.Precision.HIGHEST, preferred_element_type=F32) + carry
            carry = dlf[0:1, :]
            z = ff_ref[blk * _FB:(blk + 1) * _FB, :] + b_ref[...]
            dz = jnp.where(lane < 8, dlf * jax.nn.sigmoid(-z), 0.0)
            dff_ref[blk * _FB:(blk + 1) * _FB, :] = dz.astype(dff_ref.dtype)
            db = db + jnp.sum(dz, axis=0, keepdims=True)
        db_ref[...] = db

    return pl.pallas_call(
        body, name="fgate_bwd",
        out_shape=[jax.ShapeDtypeStruct((T, LANES), BF), jax.ShapeDtypeStruct((1, LANES), F32)],
        compiler_params=pltpu.CompilerParams(vmem_limit_bytes=VMEM_LIMIT),
    )(ffp, bpad, dfc4, drc4)


MESH = pl.DeviceIdType.MESH
N_PEERS = N_DEV - 1


def _place():
    return lax.axis_index("x"), lax.axis_index("y"), lax.axis_index("c")


def _all_gather(shard):
    R, W = shard.shape

    def body(x_ref, out_ref, send_sems, recv_sems, local_sem):
        x, y, c = _place()
        me, sibling = (x, y, c), (x, y, 1 - c)
        chips = [(1 - x, y), (x, 1 - y), (1 - x, 1 - y)]

        def slot(px, py, pc):
            return out_ref.at[4 * px + 2 * py + pc]

        def copy(k, block, to, src=None):
            return pltpu.make_async_remote_copy(
                src_ref=slot(*block) if src is None else src, dst_ref=slot(*block),
                send_sem=send_sems.at[k], recv_sem=recv_sems.at[k], device_id=to, device_id_type=MESH)

        mine = pltpu.make_async_copy(x_ref, slot(*me), local_sem)
        mine.start()
        first = [copy(0, me, sibling, src=x_ref)]
        first += [copy(1 + n, me, (*chip, c), src=x_ref) for n, chip in enumerate(chips)]
        for cp in first:
            cp.start()
        passed = [copy(4 + n, (*chip, c), sibling) for n, chip in enumerate(chips)]
        for n, chip in enumerate(chips):
            copy(1 + n, (*chip, c), me).wait_recv()
            passed[n].start()
        copy(0, sibling, me).wait_recv()
        for n, chip in enumerate(chips):
            copy(4 + n, (*chip, 1 - c), me).wait_recv()
        for cp in first + passed:
            cp.wait_send()
        mine.wait()

    return pl.pallas_call(
        body, name="all_gather_weights",
        out_shape=jax.ShapeDtypeStruct((N_DEV, R, W), shard.dtype),
        in_specs=[pl.BlockSpec(memory_space=pl.ANY)], out_specs=pl.BlockSpec(memory_space=pl.ANY),
        scratch_shapes=[pltpu.SemaphoreType.DMA((N_PEERS,)), pltpu.SemaphoreType.DMA((N_PEERS,)), pltpu.SemaphoreType.DMA],
    )(shard)


def _exchange_copies(src_refs, land_refs, send_sems, recv_sems, scatter):
    x, y, c = _place()
    me = 4 * x + 2 * y + c
    copies = []
    for k, (src_ref, land_ref) in enumerate(zip(src_refs, land_refs)):
        for r in range(1, N_DEV):
            px, py, pc = x ^ (r >> 2), y ^ ((r >> 1) & 1), c ^ (r & 1)
            copies.append(pltpu.make_async_remote_copy(
                src_ref=src_ref.at[4 * px + 2 * py + pc] if scatter else src_ref, dst_ref=land_ref.at[me],
                send_sem=send_sems.at[k * N_PEERS + r - 1], recv_sem=recv_sems.at[k * N_PEERS + r - 1],
                device_id=(px, py, pc), device_id_type=MESH))
    return copies


_HBM = pl.BlockSpec(memory_space=pltpu.HBM)
_SEM = pl.BlockSpec(memory_space=pltpu.SEMAPHORE)
_EFFECT = pltpu.SideEffectType.DATAFLOW_SIDE_EFFECTING


def _exchange_start(name, srcs, lands, scatter):
    n = len(srcs)

    def body(*refs):
        send_sems, recv_sems = refs[2 * n], refs[2 * n + 1]
        for cp in _exchange_copies(refs[:n], refs[n:2 * n], send_sems, recv_sems, scatter):
            cp.start()
        token = refs[-1]
        token[...] = jnp.zeros(token.shape, F32)

    arrays = list(srcs) + list(lands)
    out = pl.pallas_call(
        body, name=name,
        out_shape=(pltpu.SemaphoreType.DMA((n * N_PEERS,)), pltpu.SemaphoreType.DMA((n * N_PEERS,)))
        + tuple(pltpu.HBM(a.shape, a.dtype) for a in arrays) + (jax.ShapeDtypeStruct((8, LANES), F32),),
        in_specs=(_HBM,) * (2 * n), out_specs=(_SEM, _SEM) + (_HBM,) * (2 * n) + (pl.BlockSpec(memory_space=pltpu.VMEM),),
        input_output_aliases={k: 2 + k for k in range(2 * n)},
        compiler_params=pltpu.CompilerParams(has_side_effects=_EFFECT),
    )(*(pltpu.with_memory_space_constraint(a, pltpu.HBM) for a in arrays))
    return out[0], out[1], out[2:2 + n], out[2 + n:2 + 2 * n], out[-1]


def _exchange_wait(name, started, after, scatter):
    send_sems, recv_sems, srcs, lands, _ = started
    n = len(srcs)

    def body(*refs):
        copies = _exchange_copies(refs[:n], refs[n:2 * n], refs[2 * n], refs[2 * n + 1], scatter)
        for cp in copies:
            cp.wait_send()
        for cp in copies:
            cp.wait_recv()

    arrays = list(srcs) + list(lands)
    out = pl.pallas_call(
        body, name=name,
        out_shape=tuple(pltpu.HBM(a.shape, a.dtype) for a in arrays),
        in_specs=(_HBM,) * (2 * n) + (_SEM, _SEM, pl.BlockSpec(memory_space=pl.ANY)), out_specs=(_HBM,) * (2 * n),
        input_output_aliases={k: k for k in range(2 * n)},
        compiler_params=pltpu.CompilerParams(has_side_effects=_EFFECT),
    )(*arrays, send_sems, recv_sems, after)
    return out[:n], out[n:]


def _adam_update(g, w, m, v):
    m2 = ADAM_B1 * m + (1.0 - ADAM_B1) * g
    v2 = ADAM_B2 * v + (1.0 - ADAM_B2) * jnp.square(g)
    m_hat = m2 / (1.0 - ADAM_B1 ** ADAM_STEP)
    v_hat = v2 / (1.0 - ADAM_B2 ** ADAM_STEP)
    return g, -ADAM_LR * (m_hat / (jnp.sqrt(v_hat) + ADAM_EPS) + ADAM_WD * w), m2, v2


def _adamw(name, slots, own, w, m, v):
    R, W = w.shape

    def body(s_ref, *refs):
        if own is not None:
            x, y, c = _place()
            me = 4 * x + 2 * y + c
            g = refs[0][...].astype(F32)
            refs = refs[1:]
        else:
            g = jnp.zeros((R, W), F32)
        for s in range(N_DEV):
            part = s_ref[s].astype(F32)
            g = g + (part if own is None else jnp.where(me == s, 0.0, part))
        w_ref, m_ref, v_ref = refs[:3]
        for o, r in zip(refs[3:], _adam_update(g, w_ref[...], m_ref[...], v_ref[...])):
            o[...] = r

    full = pl.BlockSpec((R, W), lambda i: (0, 0))
    args = [slots] + ([own] if own is not None else []) + [w, m, v]
    return pl.pallas_call(
        body, name=name, grid=(1,),
        in_specs=[pl.BlockSpec((N_DEV, R, W), lambda i: (0, 0, 0))] + [full] * (len(args) - 1),
        out_specs=[full] * 4, out_shape=[jax.ShapeDtypeStruct((R, W), F32)] * 4,
        compiler_params=_params(("arbitrary",)),
    )(*args)


def _tables(T):
    pos = jnp.arange(T, dtype=F32)
    inv_freq = 10000.0 ** (-jnp.arange(0, HEAD, 2, dtype=F32) / HEAD)
    ang = pos[:, None] * inv_freq[None, :]
    cos, sin = jnp.cos(ang), jnp.sin(ang)
    cos4 = jnp.tile(cos, (1, 4))
    sin4 = jnp.tile(jnp.concatenate([-sin, sin], axis=1), (1, 2))
    log_g = jnp.log(1.0 - 2.0 ** (-5.0 - jnp.arange(8, dtype=F32)))
    return cos4, sin4, jnp.repeat(log_g, HEAD)[None, :]


def _local_step(x, mem, target, sp, w_inT, token, fetch_rest, push, push_small):
    T = x.shape[0]
    tm = min(512, T)
    tq = min(256, T)
    tb = min(1024, T)
    cos4, sin4, lg = _tables(T)
    g_fq2 = jnp.tile(sp["g_fox_q"], (1, 2))
    g_fk2 = jnp.tile(sp["g_fox_k"], (1, 2))
    g_ret = sp["g_ret_out"].reshape(1, 8 * HEAD)
    bpad = jnp.pad(sp["b_forget"], ((0, 0), (0, LANES - 8)))
    w_secs = [w_inT[k * 512:(k + 1) * 512] for k in range(7)]
    w_ffT = jnp.pad(w_inT[3584:3592], ((0, LANES - 8), (0, 0)))
    w_mainT = w_inT[:3584]
    tie = lambda p, tok: p + tok[0:1, 0:1]
    tm2, tm4 = min(1024, T), min(2048, T)

    hn1, = _rw_fwd("rms_mix", _rms_fn, [(x, D, 0, False)], [(tie(sp["g_mix"], token), D, 0, False)], [(BF, D)], T, tm4, 1)
    P, = _mm("proj_in", [[(hn1, w_mainT, "nt")]], [], _ident, T, 3584, tm4, 896, [F32])
    ffp, = _mm("proj_ff", [[(hn1, w_ffT, "nt")]], [], _ident, T, LANES, tm, LANES, [F32])
    ret, s0 = _ret_fwd(P, cos4, sin4, g_ret, lg, T, tb)
    fc, _ = _fgate_fwd(ffp, bpad, T)
    qa, qat, ka, kat, va, vat = _fox_operands(P, fc, g_fq2, g_fk2, T, tm4)
    fox, lse = _fox_forward(qat, ka, vat, T, min(512, T), tq)
    W = fetch_rest(fox)
    w_out_halves = (W["w_out"][:4 * LANES], W["w_out"][4 * LANES:])
    h1, hn2 = _mm("proj_out", [[(ret, w_out_halves[0], "nn"), (fox, w_out_halves[1], "nn")]], [x], _add_rms_epi, T, D, tm2, D,
                  [F32, BF], params=[sp["g_xattn"]])

    qx, = _mm("proj_xq", [[(hn2, W["w_xq"], "nn")]], [], _ident, T, D, tm2, D, [F32])
    memn, = _rw_fwd("rms_mem", _rms_fn, [(mem, D, 0, False)], [(sp["g_mem"], D, 0, False)], [(BF, D)], N_MEM, N_MEM, 1)
    kv, = _mm("proj_xkv", [[(memn, W["w_xkvT"], "nt")]], [], _ident, N_MEM, 2 * D, N_MEM, 512, [F32])
    xa_rows = [(qx, XHEAD, 0, True)]
    xa_params = [(sp["g_xq"], XHEAD, 0, False), (sp["g_xk"], XHEAD, 0, False), (kv, XHEAD, 0, True), (kv, XHEAD, 4, True)]
    xo, = _rw_fwd("xattn_fwd", _xattn_fn, xa_rows, xa_params, [(BF, XHEAD)], T, tm4, 4)
    h2, hn3 = _mm("proj_xo", [[(xo, W["w_xo"], "nn")]], [h1], _add_rms_epi, T, D, tm2, D, [F32, BF], params=[sp["g_ffn"]])

    gate, up, act = _mm("ffn_in", [[(hn3, W["w_gateT"], "nt")], [(hn3, W["w_upT"], "nt")]], [], _swiglu_fwd_epi,
                        T, D_FF, tm4, 256, [BF, BF, BF])
    dy, dyb, loss_part = _mm("ffn_out", [[(act, W["w_down"], "nn")]], [h2, target], _add_loss_epi, T, D, tm, D, [F32, BF], n_acc=1)

    dgate, dup = _mm("ffn_out_bwd", [[(dyb, W["w_down"], "nt")]], [gate, up], _swiglu_bwd_epi, T, D_FF, tm4, 256, [BF, BF])
    gW = {}
    gW["w_gateT"], gW["w_upT"] = _mm("dw_gate_up", [[(dgate, hn3, "tn")], [(dup, hn3, "tn")]], [], _each, D_FF, D, 256, D, [BF, BF])
    gW["w_down"], = _mm("dw_down", [[(act, dyb, "tn")]], [], _ident, D_FF, D, 256, D, [BF])
    tok = push("ffn", gW)
    gs = {}
    dh2, dh2b, gs["g_ffn"] = _mm("ffn_in_bwd", [[(dgate, W["w_gateT"], "nn"), (dup, W["w_upT"], "nn")]], [h2, dy], _rms_bwd_epi,
                                 T, D, tm, D, [F32, BF], params=[tie(sp["g_ffn"], tok)], n_acc=1)

    dxo, = _mm("proj_xo_bwd", [[(dh2b, W["w_xo"], "nt")]], [], _ident, T, D, tm2, D, [BF])
    gW["w_xo"], = _mm("dw_xo", [[(xo, dh2b, "tn")]], [], _ident, D, D, 256, D, [BF])
    dqx, gs["g_xq"], gs["g_xk"], dkv_k, dkv_v = _rw_bwd(
        "xattn_bwd", _xattn_fn, xa_rows, xa_params, [(dxo, XHEAD, 0, True)], T, tm4, 4, [BF], [True, True, True, True])
    dkv = jnp.concatenate([dkv_k[:, :D], dkv_v[:, D:]], axis=1)
    gW["w_xq"], = _mm("dw_xq", [[(hn2, dqx, "tn")]], [], _ident, D, D, 256, D, [BF])
    dmemn, = _mm("proj_xkv_bwd", [[(dkv, W["w_xkvT"], "nn")]], [], _ident, N_MEM, D, N_MEM, 512, [F32])
    gW["w_xkvT"], = _mm("dw_xkv", [[(dkv, memn, "tn")]], [], _ident, 2 * D, D, 512, D, [BF])
    tok = push("xattn", gW)
    gs["g_mem"], = _rw_bwd("rms_mem_bwd", _rms_fn, [(mem, D, 0, False)], [(sp["g_mem"], D, 0, False)], [(dmemn, D, 0, False)],
                           N_MEM, N_MEM, 1, [None], [True])
    dh1, dh1b, gs["g_xattn"] = _mm("proj_xq_bwd", [[(dqx, W["w_xq"], "nt")]], [h1, dh2], _rms_bwd_epi, T, D, tm, D, [F32, BF],
                                   params=[tie(sp["g_xattn"], tok)], n_acc=1)

    dmix, = _mm("proj_out_bwd", [[(dh1b, W["w_out"], "nt")]], [], _ident, T, D, tm2, D, [F32])
    gW["w_out"] = jnp.concatenate(_mm("dw_out", [[(ret, dh1b, "tn")], [(fox, dh1b, "tn")]], [], _each, 4 * LANES, D, 256, D,
                                      [BF, BF]), axis=0)
    tok = push("out", gW)
    doa, doat, dl = _fox_cotangent(dmix, fox, T, tm4)
    dqn, dkn, dfv, dfc4, drc4 = _fox_backward(qa, qat, ka, kat, va, doa, doat, lse + tok[0:1, 0:1], dl, T, tq, tq)
    dfq, dfk, gq2, gk2 = _rw_bwd("fox_prep_bwd", _fox_prep_fn, [(P, LANES, 16, True), (P, LANES, 20, True)],
                                 [(g_fq2, LANES, 0, False), (g_fk2, LANES, 0, False)],
                                 [(dqn, LANES, 0, True), (dkn, LANES, 0, True)], T, tm4, 4, [BF, BF], [True, True])
    gs["g_fox_q"] = gq2[:, :HEAD] + gq2[:, HEAD:]
    gs["g_fox_k"] = gk2[:, :HEAD] + gk2[:, HEAD:]
    dff, dbp = _fgate_bwd_col(ffp, bpad, dfc4, drc4, T)
    gs["b_forget"] = dbp[:, :8]
    drq, drk, drv, drg, dg_ret = _ret_bwd(P, cos4, sin4, g_ret, lg, s0, dmix, T, tb)
    gs["g_ret_out"] = dg_ret
    dsecs = [drq, drk, drv, drg, dfq, dfk, dfv]
    g_secs = list(_mm("dw_in", [[(d, hn1, "tn")] for d in dsecs], [], _each, 512, D, LANES, D, [BF] * len(dsecs)))
    g_ff, = _mm("dw_in_ff", [[(dff, hn1, "tn")]], [], _ident, LANES, D, LANES, D, [BF])
    gW["w_inT"] = jnp.concatenate(g_secs + [g_ff[:8]], axis=0)
    tok = push("in", gW)
    grad_x, _, gs["g_mix"] = _mm("proj_in_bwd", [[(d, w, "nn") for d, w in zip(dsecs, w_secs)] + [(dff, w_ffT, "nn")]], [x, dh1],
                                 _rms_bwd_epi, T, D, tm, D, [F32, BF], params=[tie(sp["g_mix"], tok)], n_acc=1)
    return grad_x, push_small(gs, loss_part)


_CANON = {"w_in": "w_inT", "w_xkv": "w_xkvT", "w_gate": "w_gateT", "w_up": "w_upT"}
_SMALL = (("g_mix", 0, 0, 1024), ("g_xattn", 1, 0, 1024), ("g_mem", 2, 0, 1024), ("g_ffn", 3, 0, 1024),
          ("g_ret_out", 4, 0, 512), ("g_xq", 4, 512, 256), ("g_xk", 4, 768, 256),
          ("g_fox_q", 5, 0, 64), ("g_fox_k", 5, 64, 64), ("b_forget", 5, 128, 8))
_LOSS_AT = (5, 256)


def _pack_small(tree):
    buf = jnp.zeros((SMALL_ROWS, D), F32)
    for name, r, c, n in _SMALL:
        buf = lax.dynamic_update_slice(buf, tree[name].reshape(1, n).astype(F32), (r, c))
    return buf


def _unpack_small(buf, like):
    return {name: buf[r:r + 1, c:c + n].reshape(like[name].shape) for name, r, c, n in _SMALL}


def _canonical(tree, name):
    a = tree[name][0]
    return a.T if W_SHARD[name][1] else a


def _from_canonical(a, name):
    return (a.T if W_SHARD[name][1] else a)[None]


def kernel(x, mem, g_mix, w_in, b_forget, g_ret_out, g_fox_q, g_fox_k, w_out, g_xattn, w_xq, w_xkv, g_mem, g_xq, g_xk, w_xo, g_ffn, w_gate, w_up, w_down, loss_target, m_g_mix, m_w_in, m_b_forget, m_g_ret_out, m_g_fox_q, m_g_fox_k, m_w_out, m_g_xattn, m_w_xq, m_w_xkv, m_g_mem, m_g_xq, m_g_xk, m_w_xo, m_g_ffn, m_w_gate, m_w_up, m_w_down, v_g_mix, v_w_in, v_b_forget, v_g_ret_out, v_g_fox_q, v_g_fox_k, v_w_out, v_g_xattn, v_w_xq, v_w_xkv, v_g_mem, v_g_xq, v_g_xk, v_w_xo, v_g_ffn, v_w_gate, v_w_up, v_w_down):
    names = ("g_mix", "w_in", "b_forget", "g_ret_out", "g_fox_q", "g_fox_k", "w_out", "g_xattn", "w_xq", "w_xkv", "g_mem",
             "g_xq", "g_xk", "w_xo", "g_ffn", "w_gate", "w_up", "w_down")
    w = dict(zip(names, (g_mix, w_in, b_forget, g_ret_out, g_fox_q, g_fox_k, w_out, g_xattn, w_xq, w_xkv, g_mem, g_xq, g_xk,
                         w_xo, g_ffn, w_gate, w_up, w_down)))
    m = dict(zip(names, (m_g_mix, m_w_in, m_b_forget, m_g_ret_out, m_g_fox_q, m_g_fox_k, m_w_out, m_g_xattn, m_w_xq, m_w_xkv,
                         m_g_mem, m_g_xq, m_g_xk, m_w_xo, m_g_ffn, m_w_gate, m_w_up, m_w_down)))
    v = dict(zip(names, (v_g_mix, v_w_in, v_b_forget, v_g_ret_out, v_g_fox_q, v_g_fox_k, v_w_out, v_g_xattn, v_w_xq, v_w_xkv,
                         v_g_mem, v_g_xq, v_g_xk, v_w_xo, v_g_ffn, v_w_gate, v_w_up, v_w_down)))
    small_names = [s[0] for s in _SMALL]
    me = 4 * lax.axis_index("x") + 2 * lax.axis_index("y") + lax.axis_index("c")

    first = _all_gather(_canonical(w, "w_in").astype(BF))
    first, rest = lax.optimization_barrier((first, [_canonical(w, n).astype(BF) for n in GATHER_REST]))
    rest_started = _exchange_start("gather_rest_start", rest, [lax.empty((N_DEV,) + a.shape, BF) for a in rest], scatter=False)

    def fetch_rest(after):
        srcs, lands = _exchange_wait("gather_rest_wait", rest_started, after, scatter=False)
        lands = [lax.dynamic_update_index_in_dim(a, own, me, axis=0) for a, own in zip(lands, srcs)]
        return {_CANON.get(n, n): a.reshape(N_DEV * a.shape[1], D) for n, a in zip(GATHER_REST, lands)}

    pushed = {}

    def push(group, grads):
        srcs = [grads[_CANON.get(n, n)].reshape(N_DEV, W_SHARD[n][0], D) for n in SCATTER_GROUPS[group]]
        pushed[group] = _exchange_start("scatter_%s_start" % group, srcs, [lax.empty(a.shape, BF) for a in srcs], scatter=True)
        return pushed[group][4]

    def push_small(gs, loss_part):
        small = lax.dynamic_update_slice(_pack_small(gs), loss_part[:, :1], _LOSS_AT)
        pushed["small"] = _exchange_start("gather_small_start", [small], [jnp.broadcast_to(small[None], (N_DEV,) + small.shape)],
                                          scatter=False)
        return pushed["small"][4]

    sp = {n: w[n].reshape(1, -1) for n in small_names}
    grad_x, done = _local_step(x[0], mem[0], loss_target[0], sp, first.reshape(N_DEV * W_SHARD["w_in"][0], D),
                               rest_started[4], fetch_rest, push, push_small)

    results, after = {}, done
    for group in ("ffn", "xattn", "out", "small", "in"):
        if group == "small":
            recv_small = _exchange_wait("gather_small_wait", pushed["small"], after, scatter=False)[1][0]
            g_sm, d_sm, m_sm, v_sm = _adamw("adamw_small", recv_small, None, _pack_small(w), _pack_small(m), _pack_small(v))
            after = g_sm
            continue
        sents, recvs = _exchange_wait("scatter_%s_wait" % group, pushed[group], after, scatter=True)
        for name, sent, recv in zip(SCATTER_GROUPS[group], sents, recvs):
            own = lax.dynamic_index_in_dim(sent, me, axis=0, keepdims=False)
            res = _adamw("adamw_" + name, recv, own, *(_canonical(t, name) for t in (w, m, v)))
            results[name] = [_from_canonical(r, name) for r in res]
        after = results[SCATTER_GROUPS[group][-1]][0]
    loss = g_sm[_LOSS_AT[0], _LOSS_AT[1]]

    outs = []
    for k, sm in enumerate((g_sm, d_sm, m_sm, v_sm)):
        tree = _unpack_small(sm, w)
        tree.update({name: res[k] for name, res in results.items()})
        outs += [tree[n] for n in names]
    return (loss, grad_x[None], *outs)
```

```python
import jax
import jax.numpy as jnp
from jax import lax
from jax.experimental import pallas as pl
from jax.experimental.pallas import tpu as pltpu

F32 = jnp.float32
BF = jnp.bfloat16

D = 1024
HEAD = 64
CHUNK = 64
N_MEM = 256
XHEAD = 256
D_FF = 2816
EPS = 1e-6
NEG = -1e30
LANES = 128
N_DEV = 8
V7X_VMEM_BYTES = 64 * 1024 * 1024
VMEM_LIMIT = V7X_VMEM_BYTES - 8 * 1024 * 1024

ADAM_LR, ADAM_B1, ADAM_B2, ADAM_EPS, ADAM_WD, ADAM_STEP = 0.001, 0.9, 0.999, 1e-08, 0.01, 10

W_SHARD = {"w_in": (449, True), "w_out": (128, False), "w_xq": (128, False), "w_xkv": (256, True),
           "w_xo": (128, False), "w_gate": (352, True), "w_up": (352, True), "w_down": (352, False)}
GATHER_REST = ("w_out", "w_xq", "w_xkv", "w_xo", "w_gate", "w_up", "w_down")
SCATTER_GROUPS = {"ffn": ("w_gate", "w_up", "w_down"), "xattn": ("w_xq", "w_xo", "w_xkv"), "out": ("w_out",), "in": ("w_in",)}
SMALL_ROWS = 8

NT = (((1,), (1,)), ((), ()))
NN = (((1,), (0,)), ((), ()))
TN = (((0,), (0,)), ((), ()))
_DIMS = {"nn": NN, "nt": NT, "tn": TN}


def _params(sem):
    return pltpu.CompilerParams(dimension_semantics=sem, vmem_limit_bytes=VMEM_LIMIT)


def _mm(name, products, extras, epilogue, M, N, tm, tn, out_dtypes, params=(), n_acc=0):
    assert n_acc == 0 or tn == N
    flat = [t for p in products for t in p]
    counts = [len(p) for p in products]
    in_specs, args, where, slots = [], [], {}, []

    def operand(arr, spec, kind):
        key = (id(arr), kind)
        if key not in where:
            where[key] = len(args)
            args.append(arr)
            in_specs.append(spec)
        return where[key]

    for a, b, form in flat:
        if form == "tn":
            ia = operand(a, pl.BlockSpec((a.shape[0], tm), lambda i, j: (0, i)), "a_tn")
        else:
            ia = operand(a, pl.BlockSpec((tm, a.shape[1]), lambda i, j: (i, 0)), "a")
        if form == "nt":
            ib = operand(b, pl.BlockSpec((tn, b.shape[1]), lambda i, j: (j, 0)), "b_nt")
        else:
            ib = operand(b, pl.BlockSpec((b.shape[0], tn), lambda i, j: (0, j)), "b")
        slots.append((ia, ib))
    n_mm = len(args)
    for e in extras:
        in_specs.append(pl.BlockSpec((tm, tn), lambda i, j: (i, j)))
        args.append(e)
    for p in params:
        in_specs.append(pl.BlockSpec((1, tn), lambda i, j: (0, j)))
        args.append(p)
    n_in = len(args)
    n_out = len(out_dtypes)

    def body(*refs):
        ins, outs = refs[:n_in], refs[n_in:]
        prods, p = [], 0
        for c in counts:
            acc = None
            for _ in range(c):
                a = ins[slots[p][0]][...].astype(BF)
                b = ins[slots[p][1]][...].astype(BF)
                d = lax.dot_general(a, b, _DIMS[flat[p][2]], preferred_element_type=F32)
                acc = d if acc is None else acc + d
                p += 1
            prods.append(acc)
        ex = [r[...].astype(F32) for r in ins[n_mm:]]
        res = epilogue(*prods, *ex)
        for o, r in zip(outs[:n_out], res[:n_out]):
            o[...] = r.astype(o.dtype)
        for o, r in zip(outs[n_out:], res[n_out:]):
            @pl.when(pl.program_id(0) == 0)
            def _(o=o):
                o[...] = jnp.zeros(o.shape, F32)
            o[...] += r

    return pl.pallas_call(
        body, name=name, grid=(M // tm, N // tn), in_specs=in_specs,
        out_specs=[pl.BlockSpec((tm, tn), lambda i, j: (i, j)) for _ in out_dtypes]
        + [pl.BlockSpec((1, tn), lambda i, j: (0, j)) for _ in range(n_acc)],
        out_shape=[jax.ShapeDtypeStruct((M, N), dt) for dt in out_dtypes] + [jax.ShapeDtypeStruct((1, N), F32)] * n_acc,
        compiler_params=_params(("arbitrary", "arbitrary")),
    )(*args)


def _ident(x):
    return (x,)


def _each(*xs):
    return xs


def _spec(rows, w, off, per_j):
    if per_j:
        return pl.BlockSpec((rows, w), lambda j, i: (i, off + j))
    return pl.BlockSpec((rows, w), lambda j, i: (i, off))


def _pspec(rows, w, off, per_j):
    if per_j:
        return pl.BlockSpec((rows, w), lambda j, i: (0, off + j))
    return pl.BlockSpec((rows, w), lambda j, i: (0, off))


def _rw_fwd(name, fn, rows, params, outs, T, tm, nj, n_acc=0):
    in_specs = [_spec(tm, w, off, pj) for _, w, off, pj in rows] + [_pspec(a.shape[0], w, off, pj) for a, w, off, pj in params]
    args = [r[0] for r in rows] + [p[0] for p in params]
    n_in, n_out = len(args), len(outs)
    out_specs = [pl.BlockSpec((tm, w), lambda j, i: (i, j)) for _, w in outs]
    out_shape = [jax.ShapeDtypeStruct((T, nj * w), dt) for dt, w in outs]
    out_specs += [pl.BlockSpec((1, LANES), lambda j, i: (0, 0)) for _ in range(n_acc)]
    out_shape += [jax.ShapeDtypeStruct((1, LANES), F32) for _ in range(n_acc)]

    def body(*refs):
        vals = [r[...].astype(F32) for r in refs[:n_in]]
        res = fn(*vals)
        orefs = refs[n_in:]
        for k in range(n_out):
            orefs[k][...] = res[k].astype(orefs[k].dtype)
        first = (pl.program_id(0) == 0) & (pl.program_id(1) == 0)
        for k in range(n_acc):
            @pl.when(first)
            def _(k=k):
                orefs[n_out + k][...] = jnp.zeros((1, LANES), F32)
            orefs[n_out + k][...] += res[n_out + k]

    return pl.pallas_call(
        body, name=name, grid=(nj, T // tm), in_specs=in_specs, out_specs=out_specs, out_shape=out_shape,
        compiler_params=_params(("arbitrary", "arbitrary")),
    )(*args)


def _rw_bwd(name, fn, rows, params, cots, T, tm, nj, row_grads, param_grads, resid=None):
    in_specs = ([_spec(tm, w, off, pj) for _, w, off, pj in rows] + [_pspec(a.shape[0], w, off, pj) for a, w, off, pj in params]
                + [_spec(tm, w, off, pj) for _, w, off, pj in cots])
    args = [r[0] for r in rows] + [p[0] for p in params] + [c[0] for c in cots]
    if resid is not None:
        in_specs.append(_spec(tm, rows[0][1], rows[0][2], rows[0][3]))
        args.append(resid)
    nr, npar, nc = len(rows), len(params), len(cots)
    out_specs, out_shape, kinds = [], [], []
    for k, dts in enumerate(row_grads):
        for dt in (dts if isinstance(dts, (list, tuple)) else [dts]):
            if dt is not None:
                w = rows[k][1]
                out_specs.append(pl.BlockSpec((tm, w), lambda j, i: (i, j)))
                out_shape.append(jax.ShapeDtypeStruct((T, nj * w), dt))
                kinds.append(("row", k))
    for k, need in enumerate(param_grads):
        if need:
            a, w, off, pj = params[k]
            out_specs.append(_pspec(a.shape[0], w, off, pj))
            out_shape.append(jax.ShapeDtypeStruct(a.shape, F32))
            kinds.append(("par", k))

    def body(*refs):
        vals = [r[...].astype(F32) for r in refs[:nr + npar]]
        ct = tuple(r[...].astype(F32) for r in refs[nr + npar:nr + npar + nc])
        _, vjp = jax.vjp(lambda *a: tuple(fn(*a)), *vals)
        grads = list(vjp(ct))
        n_in = nr + npar + nc + (resid is not None)
        if resid is not None:
            grads[0] = grads[0] + refs[n_in - 1][...].astype(F32)
        orefs = refs[n_in:]
        j, i = pl.program_id(0), pl.program_id(1)
        for o, (kind, k) in zip(orefs, kinds):
            if kind == "row":
                o[...] = grads[k].astype(o.dtype)
            else:
                first = (i == 0) if params[k][3] else ((i == 0) & (j == 0))

                @pl.when(first)
                def _(o=o):
                    o[...] = jnp.zeros(o.shape, F32)
                o[...] += grads[nr + k]

    return pl.pallas_call(
        body, name=name, grid=(nj, T // tm), in_specs=in_specs, out_specs=out_specs, out_shape=out_shape,
        compiler_params=_params(("arbitrary", "arbitrary")),
    )(*args)


def _rms(x, g):
    return x * lax.rsqrt(jnp.mean(x * x, axis=-1, keepdims=True) + EPS) * g


def _rms_fn(x, g):
    return (_rms(x, g),)


def _lo_mask():
    return lax.broadcasted_iota(jnp.int32, (1, LANES), 1) < HEAD


def _gmean(x, lo):
    s0 = jnp.sum(jnp.where(lo, x, 0.0), axis=-1, keepdims=True)
    s1 = jnp.sum(jnp.where(lo, 0.0, x), axis=-1, keepdims=True)
    return jnp.where(lo, s0, s1) * (1.0 / HEAD)


def _fox_prep_fn(fq, fk, gq, gk):
    lo = _lo_mask()
    qn = fq * lax.rsqrt(_gmean(fq * fq, lo) + EPS) * gq * (HEAD ** -0.5)
    kn = fk * lax.rsqrt(_gmean(fk * fk, lo) + EPS) * gk
    return qn, kn


@jax.custom_vjp
def _swap_halves(x):
    bit = (lax.broadcasted_iota(jnp.int32, (1, LANES), 1) & (HEAD // 2)) == 0
    return jnp.where(bit, pltpu.roll(x, LANES - HEAD // 2, 1), pltpu.roll(x, HEAD // 2, 1))


_swap_halves.defvjp(lambda x: (_swap_halves(x), None), lambda _, g: (_swap_halves(g),))


def _ret_fn(rq, rk, rv, rg, cos, sin, s_in, g, lg):
    tb = rq.shape[0]
    nc = tb // CHUNK
    lo = _lo_mask()
    row = lax.broadcasted_iota(jnp.int32, (LANES, 1), 0) < HEAD
    same_head = row == lo
    q = (rq * cos + _swap_halves(rq) * sin) * (HEAD ** -0.5)
    k = rk * cos + _swap_halves(rk) * sin
    q3, k3, v3 = q.reshape(nc, CHUNK, LANES), k.reshape(nc, CHUNK, LANES), rv.reshape(nc, CHUNK, LANES)
    pos = lax.broadcasted_iota(jnp.int32, (CHUNK, 1), 0).astype(F32)
    q_decay = jnp.exp(lg * (pos + 1.0))
    k_decay = jnp.exp(lg * (CHUNK - 1.0 - pos))
    chunk_decay = jnp.exp(lg * float(CHUNK))
    dist = jnp.abs(lax.broadcasted_iota(jnp.int32, (CHUNK, CHUNK), 0) - lax.broadcasted_iota(jnp.int32, (CHUNK, CHUNK), 1)).astype(F32)
    v3b = v3.astype(BF)
    intra = []
    for hh in range(2):
        hm = lo if hh == 0 else ~lo
        lg_h = lg[:, hh * HEAD:hh * HEAD + 1]
        qm = jnp.where(hm, q3, 0.0).astype(BF)
        sc = jnp.einsum("nid,njd->nij", qm, k3.astype(BF), preferred_element_type=F32) * jnp.exp(lg_h * dist)[None]
        intra.append(jnp.einsum("nij,nje->nie", sc.astype(BF), v3b, preferred_element_type=F32))
    o = jnp.where(lo, intra[0], intra[1])
    kv = jnp.einsum("njd,nje->nde", (k3 * k_decay[None]).astype(BF), v3b, preferred_element_type=F32)
    kv = jnp.where(same_head[None], kv, 0.0)
    state, states = s_in, []
    for n in range(nc):
        states.append(state)
        state = state * chunk_decay + kv[n]
    s_prev = jnp.stack(states, axis=0)
    o = o + jnp.einsum("nid,nde->nie", (q3 * q_decay[None]).astype(BF), s_prev.astype(BF), preferred_element_type=F32)
    o = o.reshape(tb, LANES)
    mu = _gmean(o, lo)
    oc = o - mu
    y = oc * lax.rsqrt(_gmean(oc * oc, lo) + EPS) * g
    return jax.nn.silu(rg) * y, state


def _xattn_fn(qx, gq, gk, kk, vv):
    q = _rms(qx, gq)
    k = _rms(kk, gk)
    logits = lax.dot_general(q.astype(BF), k.astype(BF), NT, preferred_element_type=F32) * (XHEAD ** -0.5)
    p = jax.nn.softmax(logits, axis=-1)
    return (jnp.dot(p.astype(BF), vv.astype(BF), preferred_element_type=F32),)


def _swiglu_fwd_epi(g, u):
    return g, u, jax.nn.silu(g) * u


def _swiglu_bwd_epi(dact, g, u):
    _, vjp = jax.vjp(lambda a, b: jax.nn.silu(a) * b, g, u)
    return vjp(dact)


def _add_rms_epi(acc, resid, g):
    h = acc + resid
    return h, _rms(h, g)


def _add_loss_epi(acc, resid, target):
    err = (acc + resid) - target
    dy = err * (1.0 / D)
    part = jnp.sum(jnp.sum(err * err, axis=0, keepdims=True), axis=1, keepdims=True) * (0.5 / D)
    return dy, dy, jnp.broadcast_to(part, (1, err.shape[1]))


def _rms_bwd_epi(dhn, h, skip, g):
    _, vjp = jax.vjp(_rms, h, g)
    dh, dg = vjp(dhn)
    dh = dh + skip
    return dh, dh, dg


def _ret_fwd(P, cos, sin, g_ret, lg, T, tb):
    nb = T // tb

    def body(rq, rk, rv, rg, c, s, g, l, o_ref, s0_ref, state):
        @pl.when(pl.program_id(1) == 0)
        def _():
            state[...] = jnp.zeros(state.shape, F32)
        s0_ref[0, 0] = state[...]
        out, s_new = _ret_fn(rq[...], rk[...], rv[...], rg[...], c[...], s[...], state[...], g[...], l[...])
        o_ref[...] = out.astype(o_ref.dtype)
        state[...] = s_new

    sec = lambda off: pl.BlockSpec((tb, LANES), lambda j, i: (i, off + j))
    tab = pl.BlockSpec((tb, LANES), lambda j, i: (i, 0))
    par = pl.BlockSpec((1, LANES), lambda j, i: (0, j))
    return pl.pallas_call(
        body, name="ret_fwd", grid=(4, nb),
        in_specs=[sec(0), sec(4), sec(8), sec(12), tab, tab, par, par],
        out_specs=[pl.BlockSpec((tb, LANES), lambda j, i: (i, j)), pl.BlockSpec((1, 1, LANES, LANES), lambda j, i: (j, i, 0, 0))],
        out_shape=[jax.ShapeDtypeStruct((T, 4 * LANES), BF), jax.ShapeDtypeStruct((4, nb, LANES, LANES), F32)],
        scratch_shapes=[pltpu.VMEM((LANES, LANES), F32)],
        compiler_params=_params(("arbitrary", "arbitrary")),
    )(P, P, P, P, cos, sin, g_ret, lg)


def _ret_bwd(P, cos, sin, g_ret, lg, s0, dmix, T, tb):
    nb = T // tb

    def body(rq, rk, rv, rg, c, s, g, l, s0_ref, do, drq, drk, drv, drg, dg, dstate):
        i = pl.program_id(1)

        @pl.when(i == 0)
        def _():
            dstate[...] = jnp.zeros(dstate.shape, F32)
            dg[...] = jnp.zeros(dg.shape, F32)

        cc, ss, ll = c[...], s[...], l[...]
        _, vjp = jax.vjp(lambda a, b, v, gate, st, gg: _ret_fn(a, b, v, gate, cc, ss, st, gg, ll),
                         rq[...], rk[...], rv[...], rg[...], s0_ref[0, 0], g[...])
        ga, gb, gv, ggate, gst, ggain = vjp((do[...], dstate[...]))
        drq[...] = ga.astype(drq.dtype)
        drk[...] = gb.astype(drk.dtype)
        drv[...] = gv.astype(drv.dtype)
        drg[...] = ggate.astype(drg.dtype)
        dstate[...] = gst
        dg[...] += ggain

    rev = lambda i: nb - 1 - i
    sec = lambda off: pl.BlockSpec((tb, LANES), lambda j, i: (rev(i), off + j))
    tab = pl.BlockSpec((tb, LANES), lambda j, i: (rev(i), 0))
    par = pl.BlockSpec((1, LANES), lambda j, i: (0, j))
    outb = pl.BlockSpec((tb, LANES), lambda j, i: (rev(i), j))
    return pl.pallas_call(
        body, name="ret_bwd", grid=(4, nb),
        in_specs=[sec(0), sec(4), sec(8), sec(12), tab, tab, par, par,
                  pl.BlockSpec((1, 1, LANES, LANES), lambda j, i: (j, rev(i), 0, 0)), outb],
        out_specs=[outb, outb, outb, outb, par],
        out_shape=[jax.ShapeDtypeStruct((T, 4 * LANES), BF)] * 4 + [jax.ShapeDtypeStruct((1, 4 * LANES), F32)],
        scratch_shapes=[pltpu.VMEM((LANES, LANES), F32)],
        compiler_params=_params(("arbitrary", "arbitrary")),
    )(P, P, P, P, cos, sin, g_ret, lg, s0, dmix)


_FB = 128


def _tri(lower):
    r = lax.broadcasted_iota(jnp.int32, (_FB, _FB), 0)
    c = lax.broadcasted_iota(jnp.int32, (_FB, _FB), 1)
    return ((r >= c) if lower else (r <= c)).astype(F32)


def _fgate_fwd(ffp, bpad, T):
    def body(ff_ref, b_ref, fc_ref, fr_ref):
        lane = lax.broadcasted_iota(jnp.int32, (1, LANES), 1)
        tri = _tri(True)
        carry = jnp.zeros((1, LANES), F32)
        for blk in range(T // _FB):
            z = ff_ref[blk * _FB:(blk + 1) * _FB, :] + b_ref[...]
            lf = jnp.where(lane < 8, jax.nn.log_sigmoid(z), 0.0)
            f = jnp.dot(tri, lf, precision=lax.Precision.HIGHEST, preferred_element_type=F32) + carry
            carry = f[_FB - 1:_FB, :]
            fc_ref[blk * _FB:(blk + 1) * _FB, :] = f
            fr_ref[:, blk * _FB:(blk + 1) * _FB] = f.T[:8, :]

    return pl.pallas_call(
        body, name="fgate_fwd",
        out_shape=[jax.ShapeDtypeStruct((T, LANES), F32), jax.ShapeDtypeStruct((8, T), F32)],
        compiler_params=pltpu.CompilerParams(vmem_limit_bytes=VMEM_LIMIT),
    )(ffp, bpad)


_BIAS_LANE = HEAD


def _head_bias_col(fc, head):
    lane = lax.broadcasted_iota(jnp.int32, (1, LANES), 1)
    return jnp.sum(jnp.where(lane == head, fc, 0.0), axis=-1, keepdims=True)


def _split3(f):
    hi = f.astype(BF).astype(F32)
    mid = (f - hi).astype(BF).astype(F32)
    lo = ((f - hi) - mid).astype(BF).astype(F32)
    return hi, mid, lo


def _fox_operands(P, fc, g_fq2, g_fk2, T, tm):
    def body(fq_ref, fk_ref, fv_ref, fc_ref, gq_ref, gk_ref, qa_ref, qat_ref, ka_ref, kat_ref, va_ref, vat_ref):
        j = pl.program_id(0)
        lane = lax.broadcasted_iota(jnp.int32, (1, LANES), 1)
        qn, kn = _fox_prep_fn(fq_ref[...], fk_ref[...], gq_ref[...], gk_ref[...])
        v = fv_ref[...]
        fcb = fc_ref[...]
        b = _BIAS_LANE
        for hh in range(2):
            hi, mid, lo = _split3(_head_bias_col(fcb, 2 * j + hh))
            take = (lambda a: a) if hh == 0 else (lambda a: pltpu.roll(a, HEAD, 1))
            qa = jnp.where(lane < HEAD, take(qn), jnp.where(lane == b, hi, jnp.where(lane == b + 1, mid, jnp.where(
                lane == b + 2, lo, jnp.where(lane < b + 6, 1.0, 0.0)))))
            ka = jnp.where(lane < HEAD, take(kn), jnp.where(lane < b + 3, 1.0, jnp.where(lane == b + 3, -hi, jnp.where(
                lane == b + 4, -mid, jnp.where(lane == b + 5, -lo, 0.0)))))
            va = jnp.where(lane < HEAD, take(v), 0.0)
            for val, ref, tref in ((qa, qa_ref, qat_ref), (ka, ka_ref, kat_ref), (va, va_ref, vat_ref)):
                ref[hh] = val.astype(BF)
                tref[hh] = val.T.astype(BF)

    sec = lambda off: pl.BlockSpec((tm, LANES), lambda j, i: (i, off + j))
    par = pl.BlockSpec((1, LANES), lambda j, i: (0, 0))
    nat = pl.BlockSpec((2, tm, LANES), lambda j, i: (j, i, 0))
    trn = pl.BlockSpec((2, LANES, tm), lambda j, i: (j, 0, i))
    return pl.pallas_call(
        body, name="fox_operands", grid=(4, T // tm),
        in_specs=[sec(16), sec(20), sec(24), pl.BlockSpec((tm, LANES), lambda j, i: (i, 0)), par, par],
        out_specs=[nat, trn, nat, trn, nat, trn],
        out_shape=[jax.ShapeDtypeStruct((8, T, LANES), BF), jax.ShapeDtypeStruct((8, LANES, T), BF)] * 3,
        compiler_params=_params(("parallel", "arbitrary")),
    )(P, P, P, fc, g_fq2, g_fk2)


def _fox_forward(qat, ka, vat, T, tq, tk):
    nq, per = T // tq, tq // tk
    assert per == 2
    RC = 64

    def body(qat_ref, ka_ref, vat_ref, o_ref, lse_ref, s_scr, p_scr, a_scr, m_scr, l_scr, acc_scr):
        i = pl.program_id(1)
        sub = lax.broadcasted_iota(jnp.int32, (8, 1), 0)
        row = lax.broadcasted_iota(jnp.int32, (RC, tq), 0)
        col = lax.broadcasted_iota(jnp.int32, (RC, tq), 1)
        m_scr[...] = jnp.full(m_scr.shape, NEG, F32)
        l_scr[...] = jnp.zeros(l_scr.shape, F32)
        acc_scr[...] = jnp.zeros(acc_scr.shape, F32)

        def scores(slot, kb):
            k0 = pl.multiple_of(kb * tk, tk)
            for hh in range(2):
                s_scr[slot, hh] = jnp.dot(ka_ref[hh, pl.ds(k0, tk), :], qat_ref[hh], preferred_element_type=F32)

        def softmax(slot, kb, diagonal):
            shift = kb * tk - i * tq
            for hh in range(2):
                def masked(r):
                    tile = s_scr[slot, hh, r * RC:(r + 1) * RC, :]
                    return jnp.where(row + (r * RC + shift) <= col, tile, NEG) if diagonal else tile

                mx = jnp.max(masked(0), axis=0, keepdims=True)
                for r in range(1, tk // RC):
                    mx = jnp.maximum(mx, jnp.max(masked(r), axis=0, keepdims=True))
                m_old = m_scr[hh, 0:1, :]
                m2 = jnp.maximum(m_old, mx)
                a = jnp.exp(m_old - m2)
                lsum = jnp.zeros((1, tq), F32)
                for r in range(tk // RC):
                    p = jnp.exp(masked(r) - m2)
                    p_scr[slot, hh, r * RC:(r + 1) * RC, :] = p.astype(BF)
                    lsum = lsum + jnp.sum(p, axis=0, keepdims=True)
                m_scr[hh] = jnp.broadcast_to(m2, (8, tq))
                l_scr[hh] = jnp.broadcast_to(a * l_scr[hh, 0:1, :] + lsum, (8, tq))
                a_scr[slot, hh] = jnp.broadcast_to(a, (8, tq))

        def values(slot, kb):
            k0 = pl.multiple_of(kb * tk, tk)
            for hh in range(2):
                pv = jnp.dot(vat_ref[hh, 0:HEAD, pl.ds(k0, tk)], p_scr[slot, hh], preferred_element_type=F32)
                acc_scr[hh] = a_scr[slot, hh, 0:1, :] * acc_scr[hh] + pv

        def pair(kb, diag_first, diag_second, more):
            if more:
                scores(0, kb + 2)
            softmax(1, kb + 1, diag_first)
            values(0, kb)
            if more:
                scores(1, kb + 3)
                softmax(0, kb + 2, diag_second)
            values(1, kb + 1)

        scores(0, 0)
        scores(1, 1)
        softmax(0, 0, True)

        @pl.loop(0, jnp.maximum(i - 1, 0))
        def _(t):
            pair(2 * t, False, False, True)

        @pl.when(i >= 1)
        def _():
            pair(2 * (i - 1), False, True, True)

        pair(2 * i, True, False, False)

        o_ref[...] = jnp.concatenate([acc_scr[hh] / l_scr[hh, 0:1, :] for hh in range(2)], axis=0).T
        lses = [m_scr[hh, 0:1, :] + jnp.log(l_scr[hh, 0:1, :]) for hh in range(2)]
        lse_ref[0] = jnp.where(sub == 0, lses[0], jnp.where(sub == 1, lses[1], 0.0))

    return pl.pallas_call(
        body, name="fox_forward", grid=(4, nq),
        in_specs=[pl.BlockSpec((2, LANES, tq), lambda j, i: (j, 0, i)), pl.BlockSpec((2, T, LANES), lambda j, i: (j, 0, 0)),
                  pl.BlockSpec((2, LANES, T), lambda j, i: (j, 0, 0))],
        out_specs=[pl.BlockSpec((tq, LANES), lambda j, i: (i, j)), pl.BlockSpec((1, 8, tq), lambda j, i: (j, 0, i))],
        out_shape=[jax.ShapeDtypeStruct((T, 4 * LANES), F32), jax.ShapeDtypeStruct((4, 8, T), F32)],
        scratch_shapes=[pltpu.VMEM((2, 2, tk, tq), F32), pltpu.VMEM((2, 2, tk, tq), BF), pltpu.VMEM((2, 2, 8, tq), F32),
                        pltpu.VMEM((2, 8, tq), F32), pltpu.VMEM((2, 8, tq), F32), pltpu.VMEM((2, HEAD, tq), F32)],
        compiler_params=_params(("parallel", "arbitrary")),
    )(qat, ka, vat)


def _fox_cotangent(dmix, fox, T, tm):
    def body(do_ref, o_ref, doa_ref, doat_ref, dl_ref):
        lane = lax.broadcasted_iota(jnp.int32, (1, LANES), 1)
        sub = lax.broadcasted_iota(jnp.int32, (8, 1), 0)
        dob = do_ref[...].astype(BF).astype(F32)
        prod_t = (dob * o_ref[...]).T
        d0 = jnp.sum(prod_t[:HEAD], axis=0, keepdims=True)
        d1 = jnp.sum(prod_t[HEAD:], axis=0, keepdims=True)
        dl_ref[0] = jnp.where(sub == 0, d0, jnp.where(sub == 1, d1, 0.0))
        for hh in range(2):
            val = jnp.where(lane < HEAD, dob if hh == 0 else pltpu.roll(dob, HEAD, 1), 0.0)
            doa_ref[hh] = val.astype(BF)
            doat_ref[hh] = val.T.astype(BF)

    return pl.pallas_call(
        body, name="fox_cotangent", grid=(4, T // tm),
        in_specs=[pl.BlockSpec((tm, LANES), lambda j, i: (i, 4 + j)), pl.BlockSpec((tm, LANES), lambda j, i: (i, j))],
        out_specs=[pl.BlockSpec((2, tm, LANES), lambda j, i: (j, i, 0)), pl.BlockSpec((2, LANES, tm), lambda j, i: (j, 0, i)),
                   pl.BlockSpec((1, 8, tm), lambda j, i: (j, 0, i))],
        out_shape=[jax.ShapeDtypeStruct((8, T, LANES), BF), jax.ShapeDtypeStruct((8, LANES, T), BF),
                   jax.ShapeDtypeStruct((4, 8, T), F32)],
        compiler_params=_params(("parallel", "arbitrary")),
    )(dmix, fox)


def _fox_backward(qa, qat, ka, kat, va, doa, doat, lse, dl, T, tq, tk):
    nq, nk = T // tq, T // tk

    def body(qa_ref, qat_ref, ka_ref, kat_ref, va_ref, doa_ref, doat_ref, lse_ref, dl_ref,
             dq_ref, dk_ref, dv_ref, df_ref, dr_ref, dqt, dk_acc, dv_acc, df_acc, sdp, pds):
        j, kb = pl.program_id(0), pl.program_id(1)
        lane = lax.broadcasted_iota(jnp.int32, (1, LANES), 1)
        first = (kb * tk) // tq

        @pl.when(kb == 0)
        def _():
            dqt[...] = jnp.zeros(dqt.shape, F32)

        dk_acc[...] = jnp.zeros(dk_acc.shape, F32)
        dv_acc[...] = jnp.zeros(dv_acc.shape, F32)
        df_acc[...] = jnp.zeros(df_acc.shape, F32)

        RC = 64
        last = nq - 1

        def products(slot, qi):
            q0 = pl.multiple_of(qi * tq, tq)
            for hh in range(2):
                sdp[slot, hh, 0] = jnp.dot(ka_ref[hh], qat_ref[hh, :, pl.ds(q0, tq)], preferred_element_type=F32)
                sdp[slot, hh, 1] = jnp.dot(va_ref[hh], doat_ref[hh, :, pl.ds(q0, tq)], preferred_element_type=F32)

        def softmax_bwd(slot, qi, diagonal, valid):
            q0 = pl.multiple_of(qi * tq, tq)
            shift = kb * tk - first * tq
            col = lax.broadcasted_iota(jnp.int32, (RC, tq), 1)
            row = lax.broadcasted_iota(jnp.int32, (RC, tq), 0)
            for hh in range(2):
                lse_row = lse_ref[0, hh:hh + 1, pl.ds(q0, tq)]
                dl_row = dl_ref[0, hh:hh + 1, pl.ds(q0, tq)]
                rsum = jnp.zeros((1, tq), F32)
                for r in range(tk // RC):
                    rows = slice(r * RC, (r + 1) * RC)
                    p = jnp.exp(sdp[slot, hh, 0, rows, :] - lse_row)
                    p = jnp.where((row + (r * RC + shift) <= col) if diagonal else valid, p, 0.0)
                    ds = p * (sdp[slot, hh, 1, rows, :] - dl_row)
                    pds[slot, hh, 0, rows, :] = p.astype(BF)
                    pds[slot, hh, 1, rows, :] = ds.astype(BF)
                    rsum = rsum + jnp.sum(ds, axis=0, keepdims=True)
                    part = ds[:, 0:LANES]
                    for c in range(1, tq // LANES):
                        part = part + ds[:, c * LANES:(c + 1) * LANES]
                    df_acc[hh, rows, :] += part
                dqt[hh, HEAD:HEAD + 8, pl.ds(q0, tq)] += jnp.broadcast_to(rsum, (8, tq))

        def accumulate(slot, qi):
            q0 = pl.multiple_of(qi * tq, tq)
            for hh in range(2):
                dv_acc[hh] += jnp.dot(pds[slot, hh, 0], doa_ref[hh, pl.ds(q0, tq), :], preferred_element_type=F32)
                dk_acc[hh] += jnp.dot(pds[slot, hh, 1], qa_ref[hh, pl.ds(q0, tq), :], preferred_element_type=F32)
                dqt[hh, 0:HEAD, pl.ds(q0, tq)] += jnp.dot(kat_ref[hh, 0:HEAD, :], pds[slot, hh, 1], preferred_element_type=F32)

        products(0, first)
        products(1, jnp.minimum(first + 1, last))
        softmax_bwd(0, first, True, None)

        @pl.loop(0, (nq - first + 1) // 2)
        def _(t):
            qi = first + 2 * t
            products(0, jnp.minimum(qi + 2, last))
            softmax_bwd(1, jnp.minimum(qi + 1, last), False, qi + 1 <= last)
            accumulate(0, qi)
            products(1, jnp.minimum(qi + 3, last))
            softmax_bwd(0, jnp.minimum(qi + 2, last), False, qi + 2 <= last)
            accumulate(1, jnp.minimum(qi + 1, last))

        lo = lane < HEAD
        dk_ref[...] = jnp.where(lo, dk_acc[0], pltpu.roll(dk_acc[1], HEAD, 1))
        dv_ref[...] = jnp.where(lo, dv_acc[0], pltpu.roll(dv_acc[1], HEAD, 1)).astype(dv_ref.dtype)
        f0 = -jnp.sum(df_acc[0], axis=1, keepdims=True)
        f1 = -jnp.sum(df_acc[1], axis=1, keepdims=True)
        df_ref[0] = jnp.where(lane == 2 * j, f0, jnp.where(lane == 2 * j + 1, f1, 0.0))

        @pl.when(kb == nk - 1)
        def _():
            for t in range(nq):
                cols = slice(t * tq, (t + 1) * tq)
                dq_ref[cols, :] = jnp.concatenate([dqt[0, 0:HEAD, cols], dqt[1, 0:HEAD, cols]], axis=0).T
                rsum = jnp.concatenate([dqt[0, HEAD:HEAD + 8, cols], dqt[1, HEAD:HEAD + 8, cols],
                                        jnp.zeros((LANES - 16, tq), F32)], axis=0).T
                dr_ref[0, cols, :] = jnp.where(lane == 2 * j, rsum[:, 0:1], jnp.where(lane == 2 * j + 1, rsum[:, 8:9], 0.0))

    nat_full = pl.BlockSpec((2, T, LANES), lambda j, kb: (j, 0, 0))
    trn_full = pl.BlockSpec((2, LANES, T), lambda j, kb: (j, 0, 0))
    nat_blk = pl.BlockSpec((2, tk, LANES), lambda j, kb: (j, kb, 0))
    trn_blk = pl.BlockSpec((2, LANES, tk), lambda j, kb: (j, 0, kb))
    rows = pl.BlockSpec((1, 8, T), lambda j, kb: (j, 0, 0))
    blk = pl.BlockSpec((tk, LANES), lambda j, kb: (kb, j))
    return pl.pallas_call(
        body, name="fox_backward", grid=(4, nk),
        in_specs=[nat_full, trn_full, nat_blk, trn_blk, nat_blk, nat_full, trn_full, rows, rows],
        out_specs=[pl.BlockSpec((T, LANES), lambda j, kb: (0, j)), blk, blk, pl.BlockSpec((1, tk, LANES), lambda j, kb: (j, kb, 0)),
                   pl.BlockSpec((1, T, LANES), lambda j, kb: (j, 0, 0))],
        out_shape=[jax.ShapeDtypeStruct((T, 4 * LANES), F32), jax.ShapeDtypeStruct((T, 4 * LANES), F32),
                   jax.ShapeDtypeStruct((T, 4 * LANES), BF), jax.ShapeDtypeStruct((4, T, LANES), F32),
                   jax.ShapeDtypeStruct((4, T, LANES), F32)],
        scratch_shapes=[pltpu.VMEM((2, HEAD + 8, T), F32), pltpu.VMEM((2, tk, LANES), F32), pltpu.VMEM((2, tk, LANES), F32),
                        pltpu.VMEM((2, tk, LANES), F32), pltpu.VMEM((2, 2, 2, tk, tq), F32), pltpu.VMEM((2, 2, 2, tk, tq), BF)],
        compiler_params=_params(("arbitrary", "arbitrary")),
    )(qa, qat, ka, kat, va, doa, doat, lse, dl)


def _fgate_bwd_col(ffp, bpad, dfc4, drc4, T):
    def body(ff_ref, b_ref, dfc_ref, drc_ref, dff_ref, db_ref):
        lane = lax.broadcasted_iota(jnp.int32, (1, LANES), 1)
        tri = _tri(False)
        carry = jnp.zeros((1, LANES), F32)
        db = jnp.zeros((1, LANES), F32)
        for blk in reversed(range(T // _FB)):
            rows = slice(blk * _FB, (blk + 1) * _FB)
            dcol = dfc_ref[0, rows, :] + drc_ref[0, rows, :]
            for pair in range(1, 4):
                dcol = dcol + (dfc_ref[pair, rows, :] + drc_ref[pair, rows, :])
            dlf = jnp.dot(tri, dcol, precision=lax.Precision.HIGHEST, preferred_element_type=F32) + carry
            carry = dlf[0:1, :]
            z = ff_ref[blk * _FB:(blk + 1) * _FB, :] + b_ref[...]
            dz = jnp.where(lane < 8, dlf * jax.nn.sigmoid(-z), 0.0)
            dff_ref[blk * _FB:(blk + 1) * _FB, :] = dz.astype(dff_ref.dtype)
            db = db + jnp.sum(dz, axis=0, keepdims=True)
        db_ref[...] = db

    return pl.pallas_call(
        body, name="fgate_bwd",
        out_shape=[jax.ShapeDtypeStruct((T, LANES), BF), jax.ShapeDtypeStruct((1, LANES), F32)],
        compiler_params=pltpu.CompilerParams(vmem_limit_bytes=VMEM_LIMIT),
    )(ffp, bpad, dfc4, drc4)


MESH = pl.DeviceIdType.MESH
N_PEERS = N_DEV - 1


def _place():
    return lax.axis_index("x"), lax.axis_index("y"), lax.axis_index("c")


def _all_gather(shard):
    R, W = shard.shape

    def body(x_ref, out_ref, send_sems, recv_sems, local_sem):
        x, y, c = _place()
        me, sibling = (x, y, c), (x, y, 1 - c)
        chips = [(1 - x, y), (x, 1 - y), (1 - x, 1 - y)]

        def slot(px, py, pc):
            return out_ref.at[4 * px + 2 * py + pc]

        def copy(k, block, to, src=None):
            return pltpu.make_async_remote_copy(
                src_ref=slot(*block) if src is None else src, dst_ref=slot(*block),
                send_sem=send_sems.at[k], recv_sem=recv_sems.at[k], device_id=to, device_id_type=MESH)

        mine = pltpu.make_async_copy(x_ref, slot(*me), local_sem)
        mine.start()
        first = [copy(0, me, sibling, src=x_ref)]
        first += [copy(1 + n, me, (*chip, c), src=x_ref) for n, chip in enumerate(chips)]
        for cp in first:
            cp.start()
        passed = [copy(4 + n, (*chip, c), sibling) for n, chip in enumerate(chips)]
        for n, chip in enumerate(chips):
            copy(1 + n, (*chip, c), me).wait_recv()
            passed[n].start()
        copy(0, sibling, me).wait_recv()
        for n, chip in enumerate(chips):
            copy(4 + n, (*chip, 1 - c), me).wait_recv()
        for cp in first + passed:
            cp.wait_send()
        mine.wait()

    return pl.pallas_call(
        body, name="all_gather_weights",
        out_shape=jax.ShapeDtypeStruct((N_DEV, R, W), shard.dtype),
        in_specs=[pl.BlockSpec(memory_space=pl.ANY)], out_specs=pl.BlockSpec(memory_space=pl.ANY),
        scratch_shapes=[pltpu.SemaphoreType.DMA((N_PEERS,)), pltpu.SemaphoreType.DMA((N_PEERS,)), pltpu.SemaphoreType.DMA],
    )(shard)


def _exchange_copies(src_refs, land_refs, send_sems, recv_sems, scatter):
    x, y, c = _place()
    me = 4 * x + 2 * y + c
    copies = []
    for k, (src_ref, land_ref) in enumerate(zip(src_refs, land_refs)):
        for r in range(1, N_DEV):
            px, py, pc = x ^ (r >> 2), y ^ ((r >> 1) & 1), c ^ (r & 1)
            copies.append(pltpu.make_async_remote_copy(
                src_ref=src_ref.at[4 * px + 2 * py + pc] if scatter else src_ref, dst_ref=land_ref.at[me],
                send_sem=send_sems.at[k * N_PEERS + r - 1], recv_sem=recv_sems.at[k * N_PEERS + r - 1],
                device_id=(px, py, pc), device_id_type=MESH))
    return copies


_HBM = pl.BlockSpec(memory_space=pltpu.HBM)
_SEM = pl.BlockSpec(memory_space=pltpu.SEMAPHORE)
_EFFECT = pltpu.SideEffectType.DATAFLOW_SIDE_EFFECTING


def _exchange_start(name, srcs, lands, scatter):
    n = len(srcs)

    def body(*refs):
        send_sems, recv_sems = refs[2 * n], refs[2 * n + 1]
        for cp in _exchange_copies(refs[:n], refs[n:2 * n], send_sems, recv_sems, scatter):
            cp.start()
        token = refs[-1]
        token[...] = jnp.zeros(token.shape, F32)

    arrays = list(srcs) + list(lands)
    out = pl.pallas_call(
        body, name=name,
        out_shape=(pltpu.SemaphoreType.DMA((n * N_PEERS,)), pltpu.SemaphoreType.DMA((n * N_PEERS,)))
        + tuple(pltpu.HBM(a.shape, a.dtype) for a in arrays) + (jax.ShapeDtypeStruct((8, LANES), F32),),
        in_specs=(_HBM,) * (2 * n), out_specs=(_SEM, _SEM) + (_HBM,) * (2 * n) + (pl.BlockSpec(memory_space=pltpu.VMEM),),
        input_output_aliases={k: 2 + k for k in range(2 * n)},
        compiler_params=pltpu.CompilerParams(has_side_effects=_EFFECT),
    )(*(pltpu.with_memory_space_constraint(a, pltpu.HBM) for a in arrays))
    return out[0], out[1], out[2:2 + n], out[2 + n:2 + 2 * n], out[-1]


def _exchange_wait(name, started, after, scatter):
    send_sems, recv_sems, srcs, lands, _ = started
    n = len(srcs)

    def body(*refs):
        copies = _exchange_copies(refs[:n], refs[n:2 * n], refs[2 * n], refs[2 * n + 1], scatter)
        for cp in copies:
            cp.wait_send()
        for cp in copies:
            cp.wait_recv()

    arrays = list(srcs) + list(lands)
    out = pl.pallas_call(
        body, name=name,
        out_shape=tuple(pltpu.HBM(a.shape, a.dtype) for a in arrays),
        in_specs=(_HBM,) * (2 * n) + (_SEM, _SEM, pl.BlockSpec(memory_space=pl.ANY)), out_specs=(_HBM,) * (2 * n),
        input_output_aliases={k: k for k in range(2 * n)},
        compiler_params=pltpu.CompilerParams(has_side_effects=_EFFECT),
    )(*arrays, send_sems, recv_sems, after)
    return out[:n], out[n:]


def _adam_update(g, w, m, v):
    m2 = ADAM_B1 * m + (1.0 - ADAM_B1) * g
    v2 = ADAM_B2 * v + (1.0 - ADAM_B2) * jnp.square(g)
    m_hat = m2 / (1.0 - ADAM_B1 ** ADAM_STEP)
    v_hat = v2 / (1.0 - ADAM_B2 ** ADAM_STEP)
    return g, -ADAM_LR * (m_hat / (jnp.sqrt(v_hat) + ADAM_EPS) + ADAM_WD * w), m2, v2


def _adamw(name, me, slots, sent, w, m, v):
    R, W = w.shape
    steps = max(k for k in (4, 2, 1) if k == 1 or (R % k == 0 and (R // k) % 16 == 0))
    tr = R // steps

    def body(me_ref, s_ref, *refs):
        if sent is not None:
            g = refs[0][0].astype(F32)
            refs = refs[1:]
        else:
            g = jnp.zeros((tr, W), F32)
        for s in range(N_DEV):
            part = s_ref[s].astype(F32)
            g = g + (part if sent is None else jnp.where(me_ref[0] == s, 0.0, part))
        w_ref, m_ref, v_ref = refs[:3]
        for o, r in zip(refs[3:], _adam_update(g, w_ref[...], m_ref[...], v_ref[...])):
            o[...] = r

    rows = pl.BlockSpec((tr, W), lambda i, me_ref: (i, 0))
    in_specs = [pl.BlockSpec((N_DEV, tr, W), lambda i, me_ref: (0, i, 0))]
    args = [slots]
    if sent is not None:
        in_specs.append(pl.BlockSpec((1, tr, W), lambda i, me_ref: (me_ref[0], i, 0)))
        args.append(sent)
    return pl.pallas_call(
        body, name=name,
        grid_spec=pltpu.PrefetchScalarGridSpec(num_scalar_prefetch=1, grid=(steps,), in_specs=in_specs + [rows] * 3,
                                               out_specs=[rows] * 4),
        out_shape=[jax.ShapeDtypeStruct((R, W), F32)] * 4,
        compiler_params=_params(("arbitrary",)),
    )(me, *args, w, m, v)


def _tables(T):
    pos = jnp.arange(T, dtype=F32)
    inv_freq = 10000.0 ** (-jnp.arange(0, HEAD, 2, dtype=F32) / HEAD)
    ang = pos[:, None] * inv_freq[None, :]
    cos, sin = jnp.cos(ang), jnp.sin(ang)
    cos4 = jnp.tile(cos, (1, 4))
    sin4 = jnp.tile(jnp.concatenate([-sin, sin], axis=1), (1, 2))
    log_g = jnp.log(1.0 - 2.0 ** (-5.0 - jnp.arange(8, dtype=F32)))
    return cos4, sin4, jnp.repeat(log_g, HEAD)[None, :]


def _local_step(x, mem, target, sp, w_inT, token, fetch_rest, push, push_small):
    T = x.shape[0]
    tm = min(512, T)
    tq = min(256, T)
    tb = min(1024, T)
    cos4, sin4, lg = _tables(T)
    g_fq2 = jnp.tile(sp["g_fox_q"], (1, 2))
    g_fk2 = jnp.tile(sp["g_fox_k"], (1, 2))
    g_ret = sp["g_ret_out"].reshape(1, 8 * HEAD)
    bpad = jnp.pad(sp["b_forget"], ((0, 0), (0, LANES - 8)))
    w_secs = [w_inT[k * 512:(k + 1) * 512] for k in range(7)]
    w_ffT = jnp.pad(w_inT[3584:3592], ((0, LANES - 8), (0, 0)))
    w_mainT = w_inT[:3584]
    tie = lambda p, tok: p + tok[0:1, 0:1]
    tm2, tm4 = min(1024, T), min(2048, T)

    hn1, = _rw_fwd("rms_mix", _rms_fn, [(x, D, 0, False)], [(tie(sp["g_mix"], token), D, 0, False)], [(BF, D)], T, tm4, 1)
    P, = _mm("proj_in", [[(hn1, w_mainT, "nt")]], [], _ident, T, 3584, tm4, 512, [F32])
    ffp, = _mm("proj_ff", [[(hn1, w_ffT, "nt")]], [], _ident, T, LANES, tm, LANES, [F32])
    ret, s0 = _ret_fwd(P, cos4, sin4, g_ret, lg, T, tb)
    fc, _ = _fgate_fwd(ffp, bpad, T)
    qa, qat, ka, kat, va, vat = _fox_operands(P, fc, g_fq2, g_fk2, T, tm4)
    fox, lse = _fox_forward(qat, ka, vat, T, min(512, T), tq)
    W = fetch_rest(fox)
    w_out_halves = (W["w_out"][:4 * LANES], W["w_out"][4 * LANES:])
    h1, hn2 = _mm("proj_out", [[(ret, w_out_halves[0], "nn"), (fox, w_out_halves[1], "nn")]], [x], _add_rms_epi, T, D, tm2, D,
                  [F32, BF], params=[sp["g_xattn"]])

    qx, = _mm("proj_xq", [[(hn2, W["w_xq"], "nn")]], [], _ident, T, D, tm2, D, [F32])
    memn, = _rw_fwd("rms_mem", _rms_fn, [(mem, D, 0, False)], [(sp["g_mem"], D, 0, False)], [(BF, D)], N_MEM, N_MEM, 1)
    kv, = _mm("proj_xkv", [[(memn, W["w_xkvT"], "nt")]], [], _ident, N_MEM, 2 * D, N_MEM, 512, [F32])
    xa_rows = [(qx, XHEAD, 0, True)]
    xa_params = [(sp["g_xq"], XHEAD, 0, False), (sp["g_xk"], XHEAD, 0, False), (kv, XHEAD, 0, True), (kv, XHEAD, 4, True)]
    xo, = _rw_fwd("xattn_fwd", _xattn_fn, xa_rows, xa_params, [(BF, XHEAD)], T, tm4, 4)
    h2, hn3 = _mm("proj_xo", [[(xo, W["w_xo"], "nn")]], [h1], _add_rms_epi, T, D, tm2, D, [F32, BF], params=[sp["g_ffn"]])

    gate, up, act = _mm("ffn_in", [[(hn3, W["w_gateT"], "nt")], [(hn3, W["w_upT"], "nt")]], [], _swiglu_fwd_epi,
                        T, D_FF, tm4, 256, [BF, BF, BF])
    dy, dyb, loss_part = _mm("ffn_out", [[(act, W["w_down"], "nn")]], [h2, target], _add_loss_epi, T, D, tm, D, [F32, BF], n_acc=1)

    dgate, dup = _mm("ffn_out_bwd", [[(dyb, W["w_down"], "nt")]], [gate, up], _swiglu_bwd_epi, T, D_FF, tm4, 256, [BF, BF])
    gW = {}
    gW["w_gateT"], gW["w_upT"] = _mm("dw_gate_up", [[(dgate, hn3, "tn")], [(dup, hn3, "tn")]], [], _each, D_FF, D, 256, D, [BF, BF])
    gW["w_down"], = _mm("dw_down", [[(act, dyb, "tn")]], [], _ident, D_FF, D, 256, D, [BF])
    tok = push("ffn", gW)
    gs = {}
    dh2, dh2b, gs["g_ffn"] = _mm("ffn_in_bwd", [[(dgate, W["w_gateT"], "nn"), (dup, W["w_upT"], "nn")]], [h2, dy], _rms_bwd_epi,
                                 T, D, min(256, T), D, [F32, BF], params=[tie(sp["g_ffn"], tok)], n_acc=1)

    dxo, = _mm("proj_xo_bwd", [[(dh2b, W["w_xo"], "nt")]], [], _ident, T, D, tm2, D, [BF])
    gW["w_xo"], = _mm("dw_xo", [[(xo, dh2b, "tn")]], [], _ident, D, D, 256, D, [BF])
    dqx, gs["g_xq"], gs["g_xk"], dkv_k, dkv_v = _rw_bwd(
        "xattn_bwd", _xattn_fn, xa_rows, xa_params, [(dxo, XHEAD, 0, True)], T, tm4, 4, [BF], [True, True, True, True])
    dkv = jnp.concatenate([dkv_k[:, :D], dkv_v[:, D:]], axis=1)
    gW["w_xq"], = _mm("dw_xq", [[(hn2, dqx, "tn")]], [], _ident, D, D, 256, D, [BF])
    dmemn, = _mm("proj_xkv_bwd", [[(dkv, W["w_xkvT"], "nn")]], [], _ident, N_MEM, D, N_MEM, 512, [F32])
    gW["w_xkvT"], = _mm("dw_xkv", [[(dkv, memn, "tn")]], [], _ident, 2 * D, D, 512, D, [BF])
    tok = push("xattn", gW)
    gs["g_mem"], = _rw_bwd("rms_mem_bwd", _rms_fn, [(mem, D, 0, False)], [(sp["g_mem"], D, 0, False)], [(dmemn, D, 0, False)],
                           N_MEM, N_MEM, 1, [None], [True])
    dh1, dh1b, gs["g_xattn"] = _mm("proj_xq_bwd", [[(dqx, W["w_xq"], "nt")]], [h1, dh2], _rms_bwd_epi, T, D, tm, D, [F32, BF],
                                   params=[tie(sp["g_xattn"], tok)], n_acc=1)

    dmix, = _mm("proj_out_bwd", [[(dh1b, W["w_out"], "nt")]], [], _ident, T, D, tm2, D, [F32])
    gW["w_out"] = jnp.concatenate(_mm("dw_out", [[(ret, dh1b, "tn")], [(fox, dh1b, "tn")]], [], _each, 4 * LANES, D, 256, D,
                                      [BF, BF]), axis=0)
    tok = push("out", gW)
    doa, doat, dl = _fox_cotangent(dmix, fox, T, tm4)
    dqn, dkn, dfv, dfc4, drc4 = _fox_backward(qa, qat, ka, kat, va, doa, doat, lse + tok[0:1, 0:1], dl, T, tq, tq)
    dfq, dfk, gq2, gk2 = _rw_bwd("fox_prep_bwd", _fox_prep_fn, [(P, LANES, 16, True), (P, LANES, 20, True)],
                                 [(g_fq2, LANES, 0, False), (g_fk2, LANES, 0, False)],
                                 [(dqn, LANES, 0, True), (dkn, LANES, 0, True)], T, tm4, 4, [BF, BF], [True, True])
    gs["g_fox_q"] = gq2[:, :HEAD] + gq2[:, HEAD:]
    gs["g_fox_k"] = gk2[:, :HEAD] + gk2[:, HEAD:]
    dff, dbp = _fgate_bwd_col(ffp, bpad, dfc4, drc4, T)
    gs["b_forget"] = dbp[:, :8]
    drq, drk, drv, drg, dg_ret = _ret_bwd(P, cos4, sin4, g_ret, lg, s0, dmix, T, tb)
    gs["g_ret_out"] = dg_ret
    dsecs = [drq, drk, drv, drg, dfq, dfk, dfv]
    g_secs = list(_mm("dw_in", [[(d, hn1, "tn")] for d in dsecs], [], _each, 512, D, LANES, D, [BF] * len(dsecs)))
    g_ff, = _mm("dw_in_ff", [[(dff, hn1, "tn")]], [], _ident, LANES, D, LANES, D, [BF])
    gW["w_inT"] = jnp.concatenate(g_secs + [g_ff[:8]], axis=0)
    tok = push("in", gW)
    grad_x, _, gs["g_mix"] = _mm("proj_in_bwd", [[(d, w, "nn") for d, w in zip(dsecs, w_secs)] + [(dff, w_ffT, "nn")]], [x, dh1],
                                 _rms_bwd_epi, T, D, tm, D, [F32, BF], params=[tie(sp["g_mix"], tok)], n_acc=1)
    return grad_x, push_small(gs, loss_part)


_CANON = {"w_in": "w_inT", "w_xkv": "w_xkvT", "w_gate": "w_gateT", "w_up": "w_upT"}
_SMALL = (("g_mix", 0, 0, 1024), ("g_xattn", 1, 0, 1024), ("g_mem", 2, 0, 1024), ("g_ffn", 3, 0, 1024),
          ("g_ret_out", 4, 0, 512), ("g_xq", 4, 512, 256), ("g_xk", 4, 768, 256),
          ("g_fox_q", 5, 0, 64), ("g_fox_k", 5, 64, 64), ("b_forget", 5, 128, 8))
_LOSS_AT = (5, 256)


def _pack_small(tree):
    buf = jnp.zeros((SMALL_ROWS, D), F32)
    for name, r, c, n in _SMALL:
        buf = lax.dynamic_update_slice(buf, tree[name].reshape(1, n).astype(F32), (r, c))
    return buf


def _unpack_small(buf, like):
    return {name: buf[r:r + 1, c:c + n].reshape(like[name].shape) for name, r, c, n in _SMALL}


def _canonical(tree, name):
    a = tree[name][0]
    return a.T if W_SHARD[name][1] else a


def _from_canonical(a, name):
    return (a.T if W_SHARD[name][1] else a)[None]


def kernel(x, mem, g_mix, w_in, b_forget, g_ret_out, g_fox_q, g_fox_k, w_out, g_xattn, w_xq, w_xkv, g_mem, g_xq, g_xk, w_xo, g_ffn, w_gate, w_up, w_down, loss_target, m_g_mix, m_w_in, m_b_forget, m_g_ret_out, m_g_fox_q, m_g_fox_k, m_w_out, m_g_xattn, m_w_xq, m_w_xkv, m_g_mem, m_g_xq, m_g_xk, m_w_xo, m_g_ffn, m_w_gate, m_w_up, m_w_down, v_g_mix, v_w_in, v_b_forget, v_g_ret_out, v_g_fox_q, v_g_fox_k, v_w_out, v_g_xattn, v_w_xq, v_w_xkv, v_g_mem, v_g_xq, v_g_xk, v_w_xo, v_g_ffn, v_w_gate, v_w_up, v_w_down):
    names = ("g_mix", "w_in", "b_forget", "g_ret_out", "g_fox_q", "g_fox_k", "w_out", "g_xattn", "w_xq", "w_xkv", "g_mem",
             "g_xq", "g_xk", "w_xo", "g_ffn", "w_gate", "w_up", "w_down")
    w = dict(zip(names, (g_mix, w_in, b_forget, g_ret_out, g_fox_q, g_fox_k, w_out, g_xattn, w_xq, w_xkv, g_mem, g_xq, g_xk,
                         w_xo, g_ffn, w_gate, w_up, w_down)))
    m = dict(zip(names, (m_g_mix, m_w_in, m_b_forget, m_g_ret_out, m_g_fox_q, m_g_fox_k, m_w_out, m_g_xattn, m_w_xq, m_w_xkv,
                         m_g_mem, m_g_xq, m_g_xk, m_w_xo, m_g_ffn, m_w_gate, m_w_up, m_w_down)))
    v = dict(zip(names, (v_g_mix, v_w_in, v_b_forget, v_g_ret_out, v_g_fox_q, v_g_fox_k, v_w_out, v_g_xattn, v_w_xq, v_w_xkv,
                         v_g_mem, v_g_xq, v_g_xk, v_w_xo, v_g_ffn, v_w_gate, v_w_up, v_w_down)))
    small_names = [s[0] for s in _SMALL]
    me = 4 * lax.axis_index("x") + 2 * lax.axis_index("y") + lax.axis_index("c")
    me1 = me.astype(jnp.int32).reshape(1)

    first = _all_gather(_canonical(w, "w_in").astype(BF))
    first, rest = lax.optimization_barrier((first, [_canonical(w, n).astype(BF) for n in GATHER_REST]))
    rest_started = _exchange_start("gather_rest_start", rest, [lax.empty((N_DEV,) + a.shape, BF) for a in rest], scatter=False)

    def fetch_rest(after):
        srcs, lands = _exchange_wait("gather_rest_wait", rest_started, after, scatter=False)
        lands = [lax.dynamic_update_index_in_dim(a, own, me, axis=0) for a, own in zip(lands, srcs)]
        return {_CANON.get(n, n): a.reshape(N_DEV * a.shape[1], D) for n, a in zip(GATHER_REST, lands)}

    pushed = {}

    def push(group, grads):
        srcs = [grads[_CANON.get(n, n)].reshape(N_DEV, W_SHARD[n][0], D) for n in SCATTER_GROUPS[group]]
        pushed[group] = _exchange_start("scatter_%s_start" % group, srcs, [lax.empty(a.shape, BF) for a in srcs], scatter=True)
        return pushed[group][4]

    def push_small(gs, loss_part):
        small = lax.dynamic_update_slice(_pack_small(gs), loss_part[:, :1], _LOSS_AT)
        pushed["small"] = _exchange_start("gather_small_start", [small], [jnp.broadcast_to(small[None], (N_DEV,) + small.shape)],
                                          scatter=False)
        return pushed["small"][4]

    sp = {n: w[n].reshape(1, -1) for n in small_names}
    grad_x, done = _local_step(x[0], mem[0], loss_target[0], sp, first.reshape(N_DEV * W_SHARD["w_in"][0], D),
                               rest_started[4], fetch_rest, push, push_small)

    results, after = {}, done
    for group in ("ffn", "xattn", "out", "small", "in"):
        if group == "small":
            recv_small = _exchange_wait("gather_small_wait", pushed["small"], after, scatter=False)[1][0]
            g_sm, d_sm, m_sm, v_sm = _adamw("adamw_small", me1, recv_small, None, _pack_small(w), _pack_small(m), _pack_small(v))
            after = g_sm
            continue
        sents, recvs = _exchange_wait("scatter_%s_wait" % group, pushed[group], after, scatter=True)
        for name, sent, recv in zip(SCATTER_GROUPS[group], sents, recvs):
            res = _adamw("adamw_" + name, me1, recv, sent, *(_canonical(t, name) for t in (w, m, v)))
            results[name] = [_from_canonical(r, name) for r in res]
        after = results[SCATTER_GROUPS[group][-1]][0]
    loss = g_sm[_LOSS_AT[0], _LOSS_AT[1]]

    outs = []
    for k, sm in enumerate((g_sm, d_sm, m_sm, v_sm)):
        tree = _unpack_small(sm, w)
        tree.update({name: res[k] for name, res in results.items()})
        outs += [tree[n] for n in names]
    return (loss, grad_x[None], *outs)
```

```python
import jax
import jax.numpy as jnp
from jax import lax
from jax.experimental import pallas as pl
from jax.experimental.pallas import tpu as pltpu

F32 = jnp.float32
BF = jnp.bfloat16

D = 1024
HEAD = 64
CHUNK = 64
N_MEM = 256
XHEAD = 256
D_FF = 2816
EPS = 1e-6
NEG = -1e30
LANES = 128
N_DEV = 8
V7X_VMEM_BYTES = 64 * 1024 * 1024
VMEM_LIMIT = V7X_VMEM_BYTES - 8 * 1024 * 1024

ADAM_LR, ADAM_B1, ADAM_B2, ADAM_EPS, ADAM_WD, ADAM_STEP = 0.001, 0.9, 0.999, 1e-08, 0.01, 10

W_SHARD = {"w_in": (449, True), "w_out": (128, False), "w_xq": (128, False), "w_xkv": (256, True),
           "w_xo": (128, False), "w_gate": (352, True), "w_up": (352, True), "w_down": (352, False)}
GATHER_REST = ("w_out", "w_xq", "w_xkv", "w_xo", "w_gate", "w_up", "w_down")
SCATTER_GROUPS = {"ffn": ("w_gate", "w_up", "w_down"), "xattn": ("w_xq", "w_xo", "w_xkv"), "out": ("w_out",), "in": ("w_in",)}
SMALL_ROWS = 8

NT = (((1,), (1,)), ((), ()))
NN = (((1,), (0,)), ((), ()))
TN = (((0,), (0,)), ((), ()))
_DIMS = {"nn": NN, "nt": NT, "tn": TN}


def _params(sem):
    return pltpu.CompilerParams(dimension_semantics=sem, vmem_limit_bytes=VMEM_LIMIT)


def _mm(name, products, extras, epilogue, M, N, tm, tn, out_dtypes, params=(), n_acc=0):
    assert n_acc == 0 or tn == N
    flat = [t for p in products for t in p]
    counts = [len(p) for p in products]
    in_specs, args, where, slots = [], [], {}, []

    def operand(arr, spec, kind):
        key = (id(arr), kind)
        if key not in where:
            where[key] = len(args)
            args.append(arr)
            in_specs.append(spec)
        return where[key]

    for a, b, form in flat:
        if form == "tn":
            ia = operand(a, pl.BlockSpec((a.shape[0], tm), lambda i, j: (0, i)), "a_tn")
        else:
            ia = operand(a, pl.BlockSpec((tm, a.shape[1]), lambda i, j: (i, 0)), "a")
        if form == "nt":
            ib = operand(b, pl.BlockSpec((tn, b.shape[1]), lambda i, j: (j, 0)), "b_nt")
        else:
            ib = operand(b, pl.BlockSpec((b.shape[0], tn), lambda i, j: (0, j)), "b")
        slots.append((ia, ib))
    n_mm = len(args)
    for e in extras:
        in_specs.append(pl.BlockSpec((tm, tn), lambda i, j: (i, j)))
        args.append(e)
    for p in params:
        in_specs.append(pl.BlockSpec((1, tn), lambda i, j: (0, j)))
        args.append(p)
    n_in = len(args)
    n_out = len(out_dtypes)

    def body(*refs):
        ins, outs = refs[:n_in], refs[n_in:]
        prods, p = [], 0
        for c in counts:
            acc = None
            for _ in range(c):
                a = ins[slots[p][0]][...].astype(BF)
                b = ins[slots[p][1]][...].astype(BF)
                d = lax.dot_general(a, b, _DIMS[flat[p][2]], preferred_element_type=F32)
                acc = d if acc is None else acc + d
                p += 1
            prods.append(acc)
        ex = [r[...].astype(F32) for r in ins[n_mm:]]
        res = epilogue(*prods, *ex)
        for o, r in zip(outs[:n_out], res[:n_out]):
            o[...] = r.astype(o.dtype)
        for o, r in zip(outs[n_out:], res[n_out:]):
            @pl.when(pl.program_id(0) == 0)
            def _(o=o):
                o[...] = jnp.zeros(o.shape, F32)
            o[...] += r

    return pl.pallas_call(
        body, name=name, grid=(M // tm, N // tn), in_specs=in_specs,
        out_specs=[pl.BlockSpec((tm, tn), lambda i, j: (i, j)) for _ in out_dtypes]
        + [pl.BlockSpec((1, tn), lambda i, j: (0, j)) for _ in range(n_acc)],
        out_shape=[jax.ShapeDtypeStruct((M, N), dt) for dt in out_dtypes] + [jax.ShapeDtypeStruct((1, N), F32)] * n_acc,
        compiler_params=_params(("arbitrary", "arbitrary")),
    )(*args)


def _ident(x):
    return (x,)


def _each(*xs):
    return xs


def _spec(rows, w, off, per_j):
    if per_j:
        return pl.BlockSpec((rows, w), lambda j, i: (i, off + j))
    return pl.BlockSpec((rows, w), lambda j, i: (i, off))


def _pspec(rows, w, off, per_j):
    if per_j:
        return pl.BlockSpec((rows, w), lambda j, i: (0, off + j))
    return pl.BlockSpec((rows, w), lambda j, i: (0, off))


def _rw_fwd(name, fn, rows, params, outs, T, tm, nj, n_acc=0):
    in_specs = [_spec(tm, w, off, pj) for _, w, off, pj in rows] + [_pspec(a.shape[0], w, off, pj) for a, w, off, pj in params]
    args = [r[0] for r in rows] + [p[0] for p in params]
    n_in, n_out = len(args), len(outs)
    out_specs = [pl.BlockSpec((tm, w), lambda j, i: (i, j)) for _, w in outs]
    out_shape = [jax.ShapeDtypeStruct((T, nj * w), dt) for dt, w in outs]
    out_specs += [pl.BlockSpec((1, LANES), lambda j, i: (0, 0)) for _ in range(n_acc)]
    out_shape += [jax.ShapeDtypeStruct((1, LANES), F32) for _ in range(n_acc)]

    def body(*refs):
        vals = [r[...].astype(F32) for r in refs[:n_in]]
        res = fn(*vals)
        orefs = refs[n_in:]
        for k in range(n_out):
            orefs[k][...] = res[k].astype(orefs[k].dtype)
        first = (pl.program_id(0) == 0) & (pl.program_id(1) == 0)
        for k in range(n_acc):
            @pl.when(first)
            def _(k=k):
                orefs[n_out + k][...] = jnp.zeros((1, LANES), F32)
            orefs[n_out + k][...] += res[n_out + k]

    return pl.pallas_call(
        body, name=name, grid=(nj, T // tm), in_specs=in_specs, out_specs=out_specs, out_shape=out_shape,
        compiler_params=_params(("arbitrary", "arbitrary")),
    )(*args)


def _rw_bwd(name, fn, rows, params, cots, T, tm, nj, row_grads, param_grads, resid=None):
    in_specs = ([_spec(tm, w, off, pj) for _, w, off, pj in rows] + [_pspec(a.shape[0], w, off, pj) for a, w, off, pj in params]
                + [_spec(tm, w, off, pj) for _, w, off, pj in cots])
    args = [r[0] for r in rows] + [p[0] for p in params] + [c[0] for c in cots]
    if resid is not None:
        in_specs.append(_spec(tm, rows[0][1], rows[0][2], rows[0][3]))
        args.append(resid)
    nr, npar, nc = len(rows), len(params), len(cots)
    out_specs, out_shape, kinds = [], [], []
    for k, dts in enumerate(row_grads):
        for dt in (dts if isinstance(dts, (list, tuple)) else [dts]):
            if dt is not None:
                w = rows[k][1]
                out_specs.append(pl.BlockSpec((tm, w), lambda j, i: (i, j)))
                out_shape.append(jax.ShapeDtypeStruct((T, nj * w), dt))
                kinds.append(("row", k))
    for k, need in enumerate(param_grads):
        if need:
            a, w, off, pj = params[k]
            out_specs.append(_pspec(a.shape[0], w, off, pj))
            out_shape.append(jax.ShapeDtypeStruct(a.shape, F32))
            kinds.append(("par", k))

    def body(*refs):
        vals = [r[...].astype(F32) for r in refs[:nr + npar]]
        ct = tuple(r[...].astype(F32) for r in refs[nr + npar:nr + npar + nc])
        _, vjp = jax.vjp(lambda *a: tuple(fn(*a)), *vals)
        grads = list(vjp(ct))
        n_in = nr + npar + nc + (resid is not None)
        if resid is not None:
            grads[0] = grads[0] + refs[n_in - 1][...].astype(F32)
        orefs = refs[n_in:]
        j, i = pl.program_id(0), pl.program_id(1)
        for o, (kind, k) in zip(orefs, kinds):
            if kind == "row":
                o[...] = grads[k].astype(o.dtype)
            else:
                first = (i == 0) if params[k][3] else ((i == 0) & (j == 0))

                @pl.when(first)
                def _(o=o):
                    o[...] = jnp.zeros(o.shape, F32)
                o[...] += grads[nr + k]

    return pl.pallas_call(
        body, name=name, grid=(nj, T // tm), in_specs=in_specs, out_specs=out_specs, out_shape=out_shape,
        compiler_params=_params(("arbitrary", "arbitrary")),
    )(*args)


def _rms(x, g):
    return x * lax.rsqrt(jnp.mean(x * x, axis=-1, keepdims=True) + EPS) * g


def _rms_fn(x, g):
    return (_rms(x, g),)


def _lo_mask():
    return lax.broadcasted_iota(jnp.int32, (1, LANES), 1) < HEAD


def _gmean(x, lo):
    s0 = jnp.sum(jnp.where(lo, x, 0.0), axis=-1, keepdims=True)
    s1 = jnp.sum(jnp.where(lo, 0.0, x), axis=-1, keepdims=True)
    return jnp.where(lo, s0, s1) * (1.0 / HEAD)


def _fox_prep_fn(fq, fk, gq, gk):
    lo = _lo_mask()
    qn = fq * lax.rsqrt(_gmean(fq * fq, lo) + EPS) * gq * (HEAD ** -0.5)
    kn = fk * lax.rsqrt(_gmean(fk * fk, lo) + EPS) * gk
    return qn, kn


@jax.custom_vjp
def _swap_halves(x):
    bit = (lax.broadcasted_iota(jnp.int32, (1, LANES), 1) & (HEAD // 2)) == 0
    return jnp.where(bit, pltpu.roll(x, LANES - HEAD // 2, 1), pltpu.roll(x, HEAD // 2, 1))


_swap_halves.defvjp(lambda x: (_swap_halves(x), None), lambda _, g: (_swap_halves(g),))


def _ret_fn(rq, rk, rv, rg, cos, sin, s_in, g, lg):
    tb = rq.shape[0]
    nc = tb // CHUNK
    lo = _lo_mask()
    row = lax.broadcasted_iota(jnp.int32, (LANES, 1), 0) < HEAD
    same_head = row == lo
    q = (rq * cos + _swap_halves(rq) * sin) * (HEAD ** -0.5)
    k = rk * cos + _swap_halves(rk) * sin
    q3, k3, v3 = q.reshape(nc, CHUNK, LANES), k.reshape(nc, CHUNK, LANES), rv.reshape(nc, CHUNK, LANES)
    pos = lax.broadcasted_iota(jnp.int32, (CHUNK, 1), 0).astype(F32)
    q_decay = jnp.exp(lg * (pos + 1.0))
    k_decay = jnp.exp(lg * (CHUNK - 1.0 - pos))
    chunk_decay = jnp.exp(lg * float(CHUNK))
    dist = jnp.abs(lax.broadcasted_iota(jnp.int32, (CHUNK, CHUNK), 0) - lax.broadcasted_iota(jnp.int32, (CHUNK, CHUNK), 1)).astype(F32)
    v3b = v3.astype(BF)
    intra = []
    for hh in range(2):
        hm = lo if hh == 0 else ~lo
        lg_h = lg[:, hh * HEAD:hh * HEAD + 1]
        qm = jnp.where(hm, q3, 0.0).astype(BF)
        sc = jnp.einsum("nid,njd->nij", qm, k3.astype(BF), preferred_element_type=F32) * jnp.exp(lg_h * dist)[None]
        intra.append(jnp.einsum("nij,nje->nie", sc.astype(BF), v3b, preferred_element_type=F32))
    o = jnp.where(lo, intra[0], intra[1])
    kv = jnp.einsum("njd,nje->nde", (k3 * k_decay[None]).astype(BF), v3b, preferred_element_type=F32)
    kv = jnp.where(same_head[None], kv, 0.0)
    state, states = s_in, []
    for n in range(nc):
        states.append(state)
        state = state * chunk_decay + kv[n]
    s_prev = jnp.stack(states, axis=0)
    o = o + jnp.einsum("nid,nde->nie", (q3 * q_decay[None]).astype(BF), s_prev.astype(BF), preferred_element_type=F32)
    o = o.reshape(tb, LANES)
    mu = _gmean(o, lo)
    oc = o - mu
    y = oc * lax.rsqrt(_gmean(oc * oc, lo) + EPS) * g
    return jax.nn.silu(rg) * y, state


def _xattn_fn(qx, gq, gk, kk, vv):
    q = _rms(qx, gq)
    k = _rms(kk, gk)
    logits = lax.dot_general(q.astype(BF), k.astype(BF), NT, preferred_element_type=F32) * (XHEAD ** -0.5)
    p = jax.nn.softmax(logits, axis=-1)
    return (jnp.dot(p.astype(BF), vv.astype(BF), preferred_element_type=F32),)


def _swiglu_fwd_epi(g, u):
    return g, u, jax.nn.silu(g) * u


def _swiglu_bwd_epi(dact, g, u):
    _, vjp = jax.vjp(lambda a, b: jax.nn.silu(a) * b, g, u)
    return vjp(dact)


def _add_rms_epi(acc, resid, g):
    h = acc + resid
    return h, _rms(h, g)


def _add_loss_epi(acc, resid, target):
    err = (acc + resid) - target
    dy = err * (1.0 / D)
    part = jnp.sum(jnp.sum(err * err, axis=0, keepdims=True), axis=1, keepdims=True) * (0.5 / D)
    return dy, dy, jnp.broadcast_to(part, (1, err.shape[1]))


def _rms_bwd_epi(dhn, h, skip, g):
    _, vjp = jax.vjp(_rms, h, g)
    dh, dg = vjp(dhn)
    dh = dh + skip
    return dh, dh, dg


def _ret_fwd(P, cos, sin, g_ret, lg, T, tb):
    nb = T // tb

    def body(rq, rk, rv, rg, c, s, g, l, o_ref, s0_ref, state):
        @pl.when(pl.program_id(1) == 0)
        def _():
            state[...] = jnp.zeros(state.shape, F32)
        s0_ref[0, 0] = state[...]
        out, s_new = _ret_fn(*(r[...].astype(F32) for r in (rq, rk, rv, rg)), c[...], s[...], state[...], g[...], l[...])
        o_ref[...] = out.astype(o_ref.dtype)
        state[...] = s_new

    sec = lambda off: pl.BlockSpec((tb, LANES), lambda j, i: (i, off + j))
    tab = pl.BlockSpec((tb, LANES), lambda j, i: (i, 0))
    par = pl.BlockSpec((1, LANES), lambda j, i: (0, j))
    return pl.pallas_call(
        body, name="ret_fwd", grid=(4, nb),
        in_specs=[sec(0), sec(4), sec(8), sec(12), tab, tab, par, par],
        out_specs=[pl.BlockSpec((tb, LANES), lambda j, i: (i, j)), pl.BlockSpec((1, 1, LANES, LANES), lambda j, i: (j, i, 0, 0))],
        out_shape=[jax.ShapeDtypeStruct((T, 4 * LANES), BF), jax.ShapeDtypeStruct((4, nb, LANES, LANES), F32)],
        scratch_shapes=[pltpu.VMEM((LANES, LANES), F32)],
        compiler_params=_params(("arbitrary", "arbitrary")),
    )(P, P, P, P, cos, sin, g_ret, lg)


def _ret_bwd(P, cos, sin, g_ret, lg, s0, dmix, T, tb):
    nb = T // tb

    def body(rq, rk, rv, rg, c, s, g, l, s0_ref, do, drq, drk, drv, drg, dg, dstate):
        i = pl.program_id(1)

        @pl.when(i == 0)
        def _():
            dstate[...] = jnp.zeros(dstate.shape, F32)
            dg[...] = jnp.zeros(dg.shape, F32)

        cc, ss, ll = c[...], s[...], l[...]
        _, vjp = jax.vjp(lambda a, b, v, gate, st, gg: _ret_fn(a, b, v, gate, cc, ss, st, gg, ll),
                         *(r[...].astype(F32) for r in (rq, rk, rv, rg)), s0_ref[0, 0], g[...])
        ga, gb, gv, ggate, gst, ggain = vjp((do[...], dstate[...]))
        drq[...] = ga.astype(drq.dtype)
        drk[...] = gb.astype(drk.dtype)
        drv[...] = gv.astype(drv.dtype)
        drg[...] = ggate.astype(drg.dtype)
        dstate[...] = gst
        dg[...] += ggain

    rev = lambda i: nb - 1 - i
    sec = lambda off: pl.BlockSpec((tb, LANES), lambda j, i: (rev(i), off + j))
    tab = pl.BlockSpec((tb, LANES), lambda j, i: (rev(i), 0))
    par = pl.BlockSpec((1, LANES), lambda j, i: (0, j))
    outb = pl.BlockSpec((tb, LANES), lambda j, i: (rev(i), j))
    return pl.pallas_call(
        body, name="ret_bwd", grid=(4, nb),
        in_specs=[sec(0), sec(4), sec(8), sec(12), tab, tab, par, par,
                  pl.BlockSpec((1, 1, LANES, LANES), lambda j, i: (j, rev(i), 0, 0)), outb],
        out_specs=[outb, outb, outb, outb, par],
        out_shape=[jax.ShapeDtypeStruct((T, 4 * LANES), BF)] * 4 + [jax.ShapeDtypeStruct((1, 4 * LANES), F32)],
        scratch_shapes=[pltpu.VMEM((LANES, LANES), F32)],
        compiler_params=_params(("arbitrary", "arbitrary")),
    )(P, P, P, P, cos, sin, g_ret, lg, s0, dmix)


_FB = 128


def _tri(lower):
    r = lax.broadcasted_iota(jnp.int32, (_FB, _FB), 0)
    c = lax.broadcasted_iota(jnp.int32, (_FB, _FB), 1)
    return ((r >= c) if lower else (r <= c)).astype(F32)


def _fgate_fwd(ffp, bpad, T):
    def body(ff_ref, b_ref, fc_ref, fr_ref):
        lane = lax.broadcasted_iota(jnp.int32, (1, LANES), 1)
        tri = _tri(True)
        carry = jnp.zeros((1, LANES), F32)
        for blk in range(T // _FB):
            z = ff_ref[blk * _FB:(blk + 1) * _FB, :] + b_ref[...]
            lf = jnp.where(lane < 8, jax.nn.log_sigmoid(z), 0.0)
            f = jnp.dot(tri, lf, precision=lax.Precision.HIGHEST, preferred_element_type=F32) + carry
            carry = f[_FB - 1:_FB, :]
            fc_ref[blk * _FB:(blk + 1) * _FB, :] = f
            fr_ref[:, blk * _FB:(blk + 1) * _FB] = f.T[:8, :]

    return pl.pallas_call(
        body, name="fgate_fwd",
        out_shape=[jax.ShapeDtypeStruct((T, LANES), F32), jax.ShapeDtypeStruct((8, T), F32)],
        compiler_params=pltpu.CompilerParams(vmem_limit_bytes=VMEM_LIMIT),
    )(ffp, bpad)


_BIAS_LANE = HEAD


def _head_bias_col(fc, head):
    lane = lax.broadcasted_iota(jnp.int32, (1, LANES), 1)
    return jnp.sum(jnp.where(lane == head, fc, 0.0), axis=-1, keepdims=True)


def _split3(f):
    hi = f.astype(BF).astype(F32)
    mid = (f - hi).astype(BF).astype(F32)
    lo = ((f - hi) - mid).astype(BF).astype(F32)
    return hi, mid, lo


def _fox_operands(P, fc, g_fq2, g_fk2, T, tm):
    def body(fq_ref, fk_ref, fv_ref, fc_ref, gq_ref, gk_ref, qa_ref, qat_ref, ka_ref, kat_ref, va_ref, vat_ref):
        j = pl.program_id(0)
        lane = lax.broadcasted_iota(jnp.int32, (1, LANES), 1)
        qn, kn = _fox_prep_fn(fq_ref[...].astype(F32), fk_ref[...].astype(F32), gq_ref[...], gk_ref[...])
        v = fv_ref[...].astype(F32)
        fcb = fc_ref[...]
        b = _BIAS_LANE
        for hh in range(2):
            hi, mid, lo = _split3(_head_bias_col(fcb, 2 * j + hh))
            take = (lambda a: a) if hh == 0 else (lambda a: pltpu.roll(a, HEAD, 1))
            qa = jnp.where(lane < HEAD, take(qn), jnp.where(lane == b, hi, jnp.where(lane == b + 1, mid, jnp.where(
                lane == b + 2, lo, jnp.where(lane < b + 6, 1.0, 0.0)))))
            ka = jnp.where(lane < HEAD, take(kn), jnp.where(lane < b + 3, 1.0, jnp.where(lane == b + 3, -hi, jnp.where(
                lane == b + 4, -mid, jnp.where(lane == b + 5, -lo, 0.0)))))
            va = jnp.where(lane < HEAD, take(v), 0.0)
            for val, ref, tref in ((qa, qa_ref, qat_ref), (ka, ka_ref, kat_ref), (va, va_ref, vat_ref)):
                ref[hh] = val.astype(BF)
                tref[hh] = val.T.astype(BF)

    sec = lambda off: pl.BlockSpec((tm, LANES), lambda j, i: (i, off + j))
    par = pl.BlockSpec((1, LANES), lambda j, i: (0, 0))
    nat = pl.BlockSpec((2, tm, LANES), lambda j, i: (j, i, 0))
    trn = pl.BlockSpec((2, LANES, tm), lambda j, i: (j, 0, i))
    return pl.pallas_call(
        body, name="fox_operands", grid=(4, T // tm),
        in_specs=[sec(16), sec(20), sec(24), pl.BlockSpec((tm, LANES), lambda j, i: (i, 0)), par, par],
        out_specs=[nat, trn, nat, trn, nat, trn],
        out_shape=[jax.ShapeDtypeStruct((8, T, LANES), BF), jax.ShapeDtypeStruct((8, LANES, T), BF)] * 3,
        compiler_params=_params(("parallel", "arbitrary")),
    )(P, P, P, fc, g_fq2, g_fk2)


def _fox_forward(qat, ka, vat, T, tq, tk):
    nq, per = T // tq, tq // tk
    assert per == 2
    RC = 64

    def body(qat_ref, ka_ref, vat_ref, o_ref, lse_ref, s_scr, p_scr, a_scr, m_scr, l_scr, acc_scr):
        i = pl.program_id(1)
        sub = lax.broadcasted_iota(jnp.int32, (8, 1), 0)
        row = lax.broadcasted_iota(jnp.int32, (RC, tq), 0)
        col = lax.broadcasted_iota(jnp.int32, (RC, tq), 1)
        m_scr[...] = jnp.full(m_scr.shape, NEG, F32)
        l_scr[...] = jnp.zeros(l_scr.shape, F32)
        acc_scr[...] = jnp.zeros(acc_scr.shape, F32)

        def scores(slot, kb):
            k0 = pl.multiple_of(kb * tk, tk)
            for hh in range(2):
                s_scr[slot, hh] = jnp.dot(ka_ref[hh, pl.ds(k0, tk), :], qat_ref[hh], preferred_element_type=F32)

        def softmax(slot, kb, diagonal):
            shift = kb * tk - i * tq
            for hh in range(2):
                def masked(r):
                    tile = s_scr[slot, hh, r * RC:(r + 1) * RC, :]
                    return jnp.where(row + (r * RC + shift) <= col, tile, NEG) if diagonal else tile

                mx = jnp.max(masked(0), axis=0, keepdims=True)
                for r in range(1, tk // RC):
                    mx = jnp.maximum(mx, jnp.max(masked(r), axis=0, keepdims=True))
                m_old = m_scr[hh, 0:1, :]
                m2 = jnp.maximum(m_old, mx)
                a = jnp.exp(m_old - m2)
                lsum = jnp.zeros((1, tq), F32)
                for r in range(tk // RC):
                    p = jnp.exp(masked(r) - m2)
                    p_scr[slot, hh, r * RC:(r + 1) * RC, :] = p.astype(BF)
                    lsum = lsum + jnp.sum(p, axis=0, keepdims=True)
                m_scr[hh] = jnp.broadcast_to(m2, (8, tq))
                l_scr[hh] = jnp.broadcast_to(a * l_scr[hh, 0:1, :] + lsum, (8, tq))
                a_scr[slot, hh] = jnp.broadcast_to(a, (8, tq))

        def values(slot, kb):
            k0 = pl.multiple_of(kb * tk, tk)
            for hh in range(2):
                pv = jnp.dot(vat_ref[hh, 0:HEAD, pl.ds(k0, tk)], p_scr[slot, hh], preferred_element_type=F32)
                acc_scr[hh] = a_scr[slot, hh, 0:1, :] * acc_scr[hh] + pv

        def pair(kb, diag_first, diag_second, more):
            if more:
                scores(0, kb + 2)
            softmax(1, kb + 1, diag_first)
            values(0, kb)
            if more:
                scores(1, kb + 3)
                softmax(0, kb + 2, diag_second)
            values(1, kb + 1)

        scores(0, 0)
        scores(1, 1)
        softmax(0, 0, True)

        @pl.loop(0, jnp.maximum(i - 1, 0))
        def _(t):
            pair(2 * t, False, False, True)

        @pl.when(i >= 1)
        def _():
            pair(2 * (i - 1), False, True, True)

        pair(2 * i, True, False, False)

        o_ref[...] = jnp.concatenate([acc_scr[hh] / l_scr[hh, 0:1, :] for hh in range(2)], axis=0).T
        lses = [m_scr[hh, 0:1, :] + jnp.log(l_scr[hh, 0:1, :]) for hh in range(2)]
        lse_ref[0] = jnp.where(sub == 0, lses[0], jnp.where(sub == 1, lses[1], 0.0))

    return pl.pallas_call(
        body, name="fox_forward", grid=(4, nq),
        in_specs=[pl.BlockSpec((2, LANES, tq), lambda j, i: (j, 0, i)), pl.BlockSpec((2, T, LANES), lambda j, i: (j, 0, 0)),
                  pl.BlockSpec((2, LANES, T), lambda j, i: (j, 0, 0))],
        out_specs=[pl.BlockSpec((tq, LANES), lambda j, i: (i, j)), pl.BlockSpec((1, 8, tq), lambda j, i: (j, 0, i))],
        out_shape=[jax.ShapeDtypeStruct((T, 4 * LANES), F32), jax.ShapeDtypeStruct((4, 8, T), F32)],
        scratch_shapes=[pltpu.VMEM((2, 2, tk, tq), F32), pltpu.VMEM((2, 2, tk, tq), BF), pltpu.VMEM((2, 2, 8, tq), F32),
                        pltpu.VMEM((2, 8, tq), F32), pltpu.VMEM((2, 8, tq), F32), pltpu.VMEM((2, HEAD, tq), F32)],
        compiler_params=_params(("parallel", "arbitrary")),
    )(qat, ka, vat)


def _fox_cotangent(dmix, fox, T, tm):
    def body(do_ref, o_ref, doa_ref, doat_ref, dl_ref):
        lane = lax.broadcasted_iota(jnp.int32, (1, LANES), 1)
        sub = lax.broadcasted_iota(jnp.int32, (8, 1), 0)
        dob = do_ref[...].astype(BF).astype(F32)
        prod_t = (dob * o_ref[...]).T
        d0 = jnp.sum(prod_t[:HEAD], axis=0, keepdims=True)
        d1 = jnp.sum(prod_t[HEAD:], axis=0, keepdims=True)
        dl_ref[0] = jnp.where(sub == 0, d0, jnp.where(sub == 1, d1, 0.0))
        for hh in range(2):
            val = jnp.where(lane < HEAD, dob if hh == 0 else pltpu.roll(dob, HEAD, 1), 0.0)
            doa_ref[hh] = val.astype(BF)
            doat_ref[hh] = val.T.astype(BF)

    return pl.pallas_call(
        body, name="fox_cotangent", grid=(4, T // tm),
        in_specs=[pl.BlockSpec((tm, LANES), lambda j, i: (i, 4 + j)), pl.BlockSpec((tm, LANES), lambda j, i: (i, j))],
        out_specs=[pl.BlockSpec((2, tm, LANES), lambda j, i: (j, i, 0)), pl.BlockSpec((2, LANES, tm), lambda j, i: (j, 0, i)),
                   pl.BlockSpec((1, 8, tm), lambda j, i: (j, 0, i))],
        out_shape=[jax.ShapeDtypeStruct((8, T, LANES), BF), jax.ShapeDtypeStruct((8, LANES, T), BF),
                   jax.ShapeDtypeStruct((4, 8, T), F32)],
        compiler_params=_params(("parallel", "arbitrary")),
    )(dmix, fox)


def _fox_backward(qa, qat, ka, kat, va, doa, doat, lse, dl, T, tq, tk):
    nq, nk = T // tq, T // tk

    def body(qa_ref, qat_ref, ka_ref, kat_ref, va_ref, doa_ref, doat_ref, lse_ref, dl_ref,
             dq_ref, dk_ref, dv_ref, df_ref, dr_ref, dqt, dk_acc, dv_acc, df_acc, sdp, pds):
        j, kb = pl.program_id(0), pl.program_id(1)
        lane = lax.broadcasted_iota(jnp.int32, (1, LANES), 1)
        first = (kb * tk) // tq

        @pl.when(kb == 0)
        def _():
            dqt[...] = jnp.zeros(dqt.shape, F32)

        dk_acc[...] = jnp.zeros(dk_acc.shape, F32)
        dv_acc[...] = jnp.zeros(dv_acc.shape, F32)
        df_acc[...] = jnp.zeros(df_acc.shape, F32)

        RC = 64
        last = nq - 1

        def products(slot, qi):
            q0 = pl.multiple_of(qi * tq, tq)
            for hh in range(2):
                sdp[slot, hh, 0] = jnp.dot(ka_ref[hh], qat_ref[hh, :, pl.ds(q0, tq)], preferred_element_type=F32)
                sdp[slot, hh, 1] = jnp.dot(va_ref[hh], doat_ref[hh, :, pl.ds(q0, tq)], preferred_element_type=F32)

        def softmax_bwd(slot, qi, diagonal, valid):
            q0 = pl.multiple_of(qi * tq, tq)
            shift = kb * tk - first * tq
            col = lax.broadcasted_iota(jnp.int32, (RC, tq), 1)
            row = lax.broadcasted_iota(jnp.int32, (RC, tq), 0)
            for hh in range(2):
                lse_row = lse_ref[0, hh:hh + 1, pl.ds(q0, tq)]
                dl_row = dl_ref[0, hh:hh + 1, pl.ds(q0, tq)]
                rsum = jnp.zeros((1, tq), F32)
                for r in range(tk // RC):
                    rows = slice(r * RC, (r + 1) * RC)
                    p = jnp.exp(sdp[slot, hh, 0, rows, :] - lse_row)
                    p = jnp.where((row + (r * RC + shift) <= col) if diagonal else valid, p, 0.0)
                    ds = p * (sdp[slot, hh, 1, rows, :] - dl_row)
                    pds[slot, hh, 0, rows, :] = p.astype(BF)
                    pds[slot, hh, 1, rows, :] = ds.astype(BF)
                    rsum = rsum + jnp.sum(ds, axis=0, keepdims=True)
                    part = ds[:, 0:LANES]
                    for c in range(1, tq // LANES):
                        part = part + ds[:, c * LANES:(c + 1) * LANES]
                    df_acc[hh, rows, :] += part
                dqt[hh, HEAD:HEAD + 8, pl.ds(q0, tq)] += jnp.broadcast_to(rsum, (8, tq))

        def accumulate(slot, qi):
            q0 = pl.multiple_of(qi * tq, tq)
            for hh in range(2):
                dv_acc[hh] += jnp.dot(pds[slot, hh, 0], doa_ref[hh, pl.ds(q0, tq), :], preferred_element_type=F32)
                dk_acc[hh] += jnp.dot(pds[slot, hh, 1], qa_ref[hh, pl.ds(q0, tq), :], preferred_element_type=F32)
                dqt[hh, 0:HEAD, pl.ds(q0, tq)] += jnp.dot(kat_ref[hh, 0:HEAD, :], pds[slot, hh, 1], preferred_element_type=F32)

        products(0, first)
        products(1, jnp.minimum(first + 1, last))
        softmax_bwd(0, first, True, None)

        @pl.loop(0, (nq - first + 1) // 2)
        def _(t):
            qi = first + 2 * t
            products(0, jnp.minimum(qi + 2, last))
            softmax_bwd(1, jnp.minimum(qi + 1, last), False, qi + 1 <= last)
            accumulate(0, qi)
            products(1, jnp.minimum(qi + 3, last))
            softmax_bwd(0, jnp.minimum(qi + 2, last), False, qi + 2 <= last)
            accumulate(1, jnp.minimum(qi + 1, last))

        lo = lane < HEAD
        dk_ref[...] = jnp.where(lo, dk_acc[0], pltpu.roll(dk_acc[1], HEAD, 1))
        dv_ref[...] = jnp.where(lo, dv_acc[0], pltpu.roll(dv_acc[1], HEAD, 1)).astype(dv_ref.dtype)
        f0 = -jnp.sum(df_acc[0], axis=1, keepdims=True)
        f1 = -jnp.sum(df_acc[1], axis=1, keepdims=True)
        df_ref[0] = jnp.where(lane == 2 * j, f0, jnp.where(lane == 2 * j + 1, f1, 0.0))

        @pl.when(kb == nk - 1)
        def _():
            for t in range(nq):
                cols = slice(t * tq, (t + 1) * tq)
                dq_ref[cols, :] = jnp.concatenate([dqt[0, 0:HEAD, cols], dqt[1, 0:HEAD, cols]], axis=0).T
                rsum = jnp.concatenate([dqt[0, HEAD:HEAD + 8, cols], dqt[1, HEAD:HEAD + 8, cols],
                                        jnp.zeros((LANES - 16, tq), F32)], axis=0).T
                dr_ref[0, cols, :] = jnp.where(lane == 2 * j, rsum[:, 0:1], jnp.where(lane == 2 * j + 1, rsum[:, 8:9], 0.0))

    nat_full = pl.BlockSpec((2, T, LANES), lambda j, kb: (j, 0, 0))
    trn_full = pl.BlockSpec((2, LANES, T), lambda j, kb: (j, 0, 0))
    nat_blk = pl.BlockSpec((2, tk, LANES), lambda j, kb: (j, kb, 0))
    trn_blk = pl.BlockSpec((2, LANES, tk), lambda j, kb: (j, 0, kb))
    rows = pl.BlockSpec((1, 8, T), lambda j, kb: (j, 0, 0))
    blk = pl.BlockSpec((tk, LANES), lambda j, kb: (kb, j))
    return pl.pallas_call(
        body, name="fox_backward", grid=(4, nk),
        in_specs=[nat_full, trn_full, nat_blk, trn_blk, nat_blk, nat_full, trn_full, rows, rows],
        out_specs=[pl.BlockSpec((T, LANES), lambda j, kb: (0, j)), blk, blk, pl.BlockSpec((1, tk, LANES), lambda j, kb: (j, kb, 0)),
                   pl.BlockSpec((1, T, LANES), lambda j, kb: (j, 0, 0))],
        out_shape=[jax.ShapeDtypeStruct((T, 4 * LANES), F32), jax.ShapeDtypeStruct((T, 4 * LANES), F32),
                   jax.ShapeDtypeStruct((T, 4 * LANES), BF), jax.ShapeDtypeStruct((4, T, LANES), F32),
                   jax.ShapeDtypeStruct((4, T, LANES), F32)],
        scratch_shapes=[pltpu.VMEM((2, HEAD + 8, T), F32), pltpu.VMEM((2, tk, LANES), F32), pltpu.VMEM((2, tk, LANES), F32),
                        pltpu.VMEM((2, tk, LANES), F32), pltpu.VMEM((2, 2, 2, tk, tq), F32), pltpu.VMEM((2, 2, 2, tk, tq), BF)],
        compiler_params=_params(("arbitrary", "arbitrary")),
    )(qa, qat, ka, kat, va, doa, doat, lse, dl)


def _fgate_bwd_col(ffp, bpad, dfc4, drc4, T):
    def body(ff_ref, b_ref, dfc_ref, drc_ref, dff_ref, db_ref):
        lane = lax.broadcasted_iota(jnp.int32, (1, LANES), 1)
        tri = _tri(False)
        carry = jnp.zeros((1, LANES), F32)
        db = jnp.zeros((1, LANES), F32)
        for blk in reversed(range(T // _FB)):
            rows = slice(blk * _FB, (blk + 1) * _FB)
            dcol = dfc_ref[0, rows, :] + drc_ref[0, rows, :]
            for pair in range(1, 4):
                dcol = dcol + (dfc_ref[pair, rows, :] + drc_ref[pair, rows, :])
            dlf = jnp.dot(tri, dcol, precision=lax.Precision.HIGHEST, preferred_element_type=F32) + carry
            carry = dlf[0:1, :]
            z = ff_ref[blk * _FB:(blk + 1) * _FB, :] + b_ref[...]
            dz = jnp.where(lane < 8, dlf * jax.nn.sigmoid(-z), 0.0)
            dff_ref[blk * _FB:(blk + 1) * _FB, :] = dz.astype(dff_ref.dtype)
            db = db + jnp.sum(dz, axis=0, keepdims=True)
        db_ref[...] = db

    return pl.pallas_call(
        body, name="fgate_bwd",
        out_shape=[jax.ShapeDtypeStruct((T, LANES), BF), jax.ShapeDtypeStruct((1, LANES), F32)],
        compiler_params=pltpu.CompilerParams(vmem_limit_bytes=VMEM_LIMIT),
    )(ffp, bpad, dfc4, drc4)


MESH = pl.DeviceIdType.MESH
N_PEERS = N_DEV - 1


def _place():
    return lax.axis_index("x"), lax.axis_index("y"), lax.axis_index("c")


def _all_gather(shard):
    R, W = shard.shape

    def body(x_ref, out_ref, send_sems, recv_sems, local_sem):
        x, y, c = _place()
        me, sibling = (x, y, c), (x, y, 1 - c)
        chips = [(1 - x, y), (x, 1 - y), (1 - x, 1 - y)]

        def slot(px, py, pc):
            return out_ref.at[4 * px + 2 * py + pc]

        def copy(k, block, to, src=None):
            return pltpu.make_async_remote_copy(
                src_ref=slot(*block) if src is None else src, dst_ref=slot(*block),
                send_sem=send_sems.at[k], recv_sem=recv_sems.at[k], device_id=to, device_id_type=MESH)

        mine = pltpu.make_async_copy(x_ref, slot(*me), local_sem)
        mine.start()
        first = [copy(0, me, sibling, src=x_ref)]
        first += [copy(1 + n, me, (*chip, c), src=x_ref) for n, chip in enumerate(chips)]
        for cp in first:
            cp.start()
        passed = [copy(4 + n, (*chip, c), sibling) for n, chip in enumerate(chips)]
        for n, chip in enumerate(chips):
            copy(1 + n, (*chip, c), me).wait_recv()
            passed[n].start()
        copy(0, sibling, me).wait_recv()
        for n, chip in enumerate(chips):
            copy(4 + n, (*chip, 1 - c), me).wait_recv()
        for cp in first + passed:
            cp.wait_send()
        mine.wait()

    return pl.pallas_call(
        body, name="all_gather_weights",
        out_shape=jax.ShapeDtypeStruct((N_DEV, R, W), shard.dtype),
        in_specs=[pl.BlockSpec(memory_space=pl.ANY)], out_specs=pl.BlockSpec(memory_space=pl.ANY),
        scratch_shapes=[pltpu.SemaphoreType.DMA((N_PEERS,)), pltpu.SemaphoreType.DMA((N_PEERS,)), pltpu.SemaphoreType.DMA],
    )(shard)


def _exchange_copies(src_refs, land_refs, send_sems, recv_sems, scatter):
    x, y, c = _place()
    me = 4 * x + 2 * y + c
    copies = []
    for k, (src_ref, land_ref) in enumerate(zip(src_refs, land_refs)):
        for r in range(1, N_DEV):
            px, py, pc = x ^ (r >> 2), y ^ ((r >> 1) & 1), c ^ (r & 1)
            copies.append(pltpu.make_async_remote_copy(
                src_ref=src_ref.at[4 * px + 2 * py + pc] if scatter else src_ref, dst_ref=land_ref.at[me],
                send_sem=send_sems.at[k * N_PEERS + r - 1], recv_sem=recv_sems.at[k * N_PEERS + r - 1],
                device_id=(px, py, pc), device_id_type=MESH))
    return copies


_HBM = pl.BlockSpec(memory_space=pltpu.HBM)
_SEM = pl.BlockSpec(memory_space=pltpu.SEMAPHORE)
_EFFECT = pltpu.SideEffectType.DATAFLOW_SIDE_EFFECTING


def _exchange_start(name, srcs, lands, scatter):
    n = len(srcs)

    def body(*refs):
        send_sems, recv_sems = refs[2 * n], refs[2 * n + 1]
        for cp in _exchange_copies(refs[:n], refs[n:2 * n], send_sems, recv_sems, scatter):
            cp.start()
        token = refs[-1]
        token[...] = jnp.zeros(token.shape, F32)

    arrays = list(srcs) + list(lands)
    out = pl.pallas_call(
        body, name=name,
        out_shape=(pltpu.SemaphoreType.DMA((n * N_PEERS,)), pltpu.SemaphoreType.DMA((n * N_PEERS,)))
        + tuple(pltpu.HBM(a.shape, a.dtype) for a in arrays) + (jax.ShapeDtypeStruct((8, LANES), F32),),
        in_specs=(_HBM,) * (2 * n), out_specs=(_SEM, _SEM) + (_HBM,) * (2 * n) + (pl.BlockSpec(memory_space=pltpu.VMEM),),
        input_output_aliases={k: 2 + k for k in range(2 * n)},
        compiler_params=pltpu.CompilerParams(has_side_effects=_EFFECT),
    )(*(pltpu.with_memory_space_constraint(a, pltpu.HBM) for a in arrays))
    return out[0], out[1], out[2:2 + n], out[2 + n:2 + 2 * n], out[-1]


def _exchange_wait(name, started, after, scatter):
    send_sems, recv_sems, srcs, lands, _ = started
    n = len(srcs)

    def body(*refs):
        copies = _exchange_copies(refs[:n], refs[n:2 * n], refs[2 * n], refs[2 * n + 1], scatter)
        for cp in copies:
            cp.wait_send()
        for cp in copies:
            cp.wait_recv()

    arrays = list(srcs) + list(lands)
    out = pl.pallas_call(
        body, name=name,
        out_shape=tuple(pltpu.HBM(a.shape, a.dtype) for a in arrays),
        in_specs=(_HBM,) * (2 * n) + (_SEM, _SEM, pl.BlockSpec(memory_space=pl.ANY)), out_specs=(_HBM,) * (2 * n),
        input_output_aliases={k: k for k in range(2 * n)},
        compiler_params=pltpu.CompilerParams(has_side_effects=_EFFECT),
    )(*arrays, send_sems, recv_sems, after)
    return out[:n], out[n:]


def _adam_update(g, w, m, v):
    m2 = ADAM_B1 * m + (1.0 - ADAM_B1) * g
    v2 = ADAM_B2 * v + (1.0 - ADAM_B2) * jnp.square(g)
    m_hat = m2 / (1.0 - ADAM_B1 ** ADAM_STEP)
    v_hat = v2 / (1.0 - ADAM_B2 ** ADAM_STEP)
    return g, -ADAM_LR * (m_hat / (jnp.sqrt(v_hat) + ADAM_EPS) + ADAM_WD * w), m2, v2


def _adamw(name, me, slots, sent, w, m, v):
    R, W = w.shape
    steps = max(k for k in (4, 2, 1) if k == 1 or (R % k == 0 and (R // k) % 16 == 0))
    tr = R // steps

    def body(me_ref, s_ref, *refs):
        if sent is not None:
            g = refs[0][0].astype(F32)
            refs = refs[1:]
        else:
            g = jnp.zeros((tr, W), F32)
        for s in range(N_DEV):
            part = s_ref[s].astype(F32)
            g = g + (part if sent is None else jnp.where(me_ref[0] == s, 0.0, part))
        w_ref, m_ref, v_ref = refs[:3]
        for o, r in zip(refs[3:], _adam_update(g, w_ref[...], m_ref[...], v_ref[...])):
            o[...] = r

    rows = pl.BlockSpec((tr, W), lambda i, me_ref: (i, 0))
    in_specs = [pl.BlockSpec((N_DEV, tr, W), lambda i, me_ref: (0, i, 0))]
    args = [slots]
    if sent is not None:
        in_specs.append(pl.BlockSpec((1, tr, W), lambda i, me_ref: (me_ref[0], i, 0)))
        args.append(sent)
    return pl.pallas_call(
        body, name=name,
        grid_spec=pltpu.PrefetchScalarGridSpec(num_scalar_prefetch=1, grid=(steps,), in_specs=in_specs + [rows] * 3,
                                               out_specs=[rows] * 4),
        out_shape=[jax.ShapeDtypeStruct((R, W), F32)] * 4,
        compiler_params=_params(("arbitrary",)),
    )(me, *args, w, m, v)


def _tables(T):
    pos = jnp.arange(T, dtype=F32)
    inv_freq = 10000.0 ** (-jnp.arange(0, HEAD, 2, dtype=F32) / HEAD)
    ang = pos[:, None] * inv_freq[None, :]
    cos, sin = jnp.cos(ang), jnp.sin(ang)
    cos4 = jnp.tile(cos, (1, 4))
    sin4 = jnp.tile(jnp.concatenate([-sin, sin], axis=1), (1, 2))
    log_g = jnp.log(1.0 - 2.0 ** (-5.0 - jnp.arange(8, dtype=F32)))
    return cos4, sin4, jnp.repeat(log_g, HEAD)[None, :]


def _local_step(x, mem, target, sp, w_inT, token, fetch_rest, push, push_small):
    T = x.shape[0]
    tm = min(512, T)
    tq = min(256, T)
    tb = min(1024, T)
    cos4, sin4, lg = _tables(T)
    g_fq2 = jnp.tile(sp["g_fox_q"], (1, 2))
    g_fk2 = jnp.tile(sp["g_fox_k"], (1, 2))
    g_ret = sp["g_ret_out"].reshape(1, 8 * HEAD)
    bpad = jnp.pad(sp["b_forget"], ((0, 0), (0, LANES - 8)))
    w_secs = [w_inT[k * 512:(k + 1) * 512] for k in range(7)]
    w_ffT = jnp.pad(w_inT[3584:3592], ((0, LANES - 8), (0, 0)))
    w_mainT = w_inT[:3584]
    tie = lambda p, tok: p + tok[0:1, 0:1]
    tm2, tm4 = min(1024, T), min(2048, T)

    hn1, = _rw_fwd("rms_mix", _rms_fn, [(x, D, 0, False)], [(tie(sp["g_mix"], token), D, 0, False)], [(BF, D)], T, tm4, 1)
    P, = _mm("proj_in", [[(hn1, w_mainT, "nt")]], [], _ident, T, 3584, tm4, 512, [BF])
    ffp, = _mm("proj_ff", [[(hn1, w_ffT, "nt")]], [], _ident, T, LANES, tm, LANES, [F32])
    ret, s0 = _ret_fwd(P, cos4, sin4, g_ret, lg, T, tb)
    fc, _ = _fgate_fwd(ffp, bpad, T)
    qa, qat, ka, kat, va, vat = _fox_operands(P, fc, g_fq2, g_fk2, T, tm4)
    fox, lse = _fox_forward(qat, ka, vat, T, min(512, T), tq)
    W = fetch_rest(fox)
    w_out_halves = (W["w_out"][:4 * LANES], W["w_out"][4 * LANES:])
    h1, hn2 = _mm("proj_out", [[(ret, w_out_halves[0], "nn"), (fox, w_out_halves[1], "nn")]], [x], _add_rms_epi, T, D, tm2, D,
                  [F32, BF], params=[sp["g_xattn"]])

    qx, = _mm("proj_xq", [[(hn2, W["w_xq"], "nn")]], [], _ident, T, D, tm2, D, [F32])
    memn, = _rw_fwd("rms_mem", _rms_fn, [(mem, D, 0, False)], [(sp["g_mem"], D, 0, False)], [(BF, D)], N_MEM, N_MEM, 1)
    kv, = _mm("proj_xkv", [[(memn, W["w_xkvT"], "nt")]], [], _ident, N_MEM, 2 * D, N_MEM, 512, [F32])
    xa_rows = [(qx, XHEAD, 0, True)]
    xa_params = [(sp["g_xq"], XHEAD, 0, False), (sp["g_xk"], XHEAD, 0, False), (kv, XHEAD, 0, True), (kv, XHEAD, 4, True)]
    xo, = _rw_fwd("xattn_fwd", _xattn_fn, xa_rows, xa_params, [(BF, XHEAD)], T, tm4, 4)
    h2, hn3 = _mm("proj_xo", [[(xo, W["w_xo"], "nn")]], [h1], _add_rms_epi, T, D, tm2, D, [F32, BF], params=[sp["g_ffn"]])

    gate, up, act = _mm("ffn_in", [[(hn3, W["w_gateT"], "nt")], [(hn3, W["w_upT"], "nt")]], [], _swiglu_fwd_epi,
                        T, D_FF, tm4, 256, [BF, BF, BF])
    dy, dyb, loss_part = _mm("ffn_out", [[(act, W["w_down"], "nn")]], [h2, target], _add_loss_epi, T, D, tm, D, [F32, BF], n_acc=1)

    dgate, dup = _mm("ffn_out_bwd", [[(dyb, W["w_down"], "nt")]], [gate, up], _swiglu_bwd_epi, T, D_FF, tm4, 256, [BF, BF])
    gW = {}
    gW["w_gateT"], gW["w_upT"] = _mm("dw_gate_up", [[(dgate, hn3, "tn")], [(dup, hn3, "tn")]], [], _each, D_FF, D, 256, D, [BF, BF])
    gW["w_down"], = _mm("dw_down", [[(act, dyb, "tn")]], [], _ident, D_FF, D, 256, D, [BF])
    tok = push("ffn", gW)
    gs = {}
    dh2, dh2b, gs["g_ffn"] = _mm("ffn_in_bwd", [[(dgate, W["w_gateT"], "nn"), (dup, W["w_upT"], "nn")]], [h2, dy], _rms_bwd_epi,
                                 T, D, min(256, T), D, [F32, BF], params=[tie(sp["g_ffn"], tok)], n_acc=1)

    dxo, = _mm("proj_xo_bwd", [[(dh2b, W["w_xo"], "nt")]], [], _ident, T, D, tm2, D, [BF])
    gW["w_xo"], = _mm("dw_xo", [[(xo, dh2b, "tn")]], [], _ident, D, D, 256, D, [BF])
    dqx, gs["g_xq"], gs["g_xk"], dkv_k, dkv_v = _rw_bwd(
        "xattn_bwd", _xattn_fn, xa_rows, xa_params, [(dxo, XHEAD, 0, True)], T, tm4, 4, [BF], [True, True, True, True])
    dkv = jnp.concatenate([dkv_k[:, :D], dkv_v[:, D:]], axis=1)
    gW["w_xq"], = _mm("dw_xq", [[(hn2, dqx, "tn")]], [], _ident, D, D, 256, D, [BF])
    dmemn, = _mm("proj_xkv_bwd", [[(dkv, W["w_xkvT"], "nn")]], [], _ident, N_MEM, D, N_MEM, 512, [F32])
    gW["w_xkvT"], = _mm("dw_xkv", [[(dkv, memn, "tn")]], [], _ident, 2 * D, D, 512, D, [BF])
    tok = push("xattn", gW)
    gs["g_mem"], = _rw_bwd("rms_mem_bwd", _rms_fn, [(mem, D, 0, False)], [(sp["g_mem"], D, 0, False)], [(dmemn, D, 0, False)],
                           N_MEM, N_MEM, 1, [None], [True])
    dh1, dh1b, gs["g_xattn"] = _mm("proj_xq_bwd", [[(dqx, W["w_xq"], "nt")]], [h1, dh2], _rms_bwd_epi, T, D, tm, D, [F32, BF],
                                   params=[tie(sp["g_xattn"], tok)], n_acc=1)

    dmix, = _mm("proj_out_bwd", [[(dh1b, W["w_out"], "nt")]], [], _ident, T, D, tm2, D, [F32])
    gW["w_out"] = jnp.concatenate(_mm("dw_out", [[(ret, dh1b, "tn")], [(fox, dh1b, "tn")]], [], _each, 4 * LANES, D, 256, D,
                                      [BF, BF]), axis=0)
    tok = push("out", gW)
    doa, doat, dl = _fox_cotangent(dmix, fox, T, tm4)
    dqn, dkn, dfv, dfc4, drc4 = _fox_backward(qa, qat, ka, kat, va, doa, doat, lse + tok[0:1, 0:1], dl, T, tq, tq)
    dfq, dfk, gq2, gk2 = _rw_bwd("fox_prep_bwd", _fox_prep_fn, [(P, LANES, 16, True), (P, LANES, 20, True)],
                                 [(g_fq2, LANES, 0, False), (g_fk2, LANES, 0, False)],
                                 [(dqn, LANES, 0, True), (dkn, LANES, 0, True)], T, tm4, 4, [BF, BF], [True, True])
    gs["g_fox_q"] = gq2[:, :HEAD] + gq2[:, HEAD:]
    gs["g_fox_k"] = gk2[:, :HEAD] + gk2[:, HEAD:]
    dff, dbp = _fgate_bwd_col(ffp, bpad, dfc4, drc4, T)
    gs["b_forget"] = dbp[:, :8]
    drq, drk, drv, drg, dg_ret = _ret_bwd(P, cos4, sin4, g_ret, lg, s0, dmix, T, tb)
    gs["g_ret_out"] = dg_ret
    dsecs = [drq, drk, drv, drg, dfq, dfk, dfv]
    g_secs = list(_mm("dw_in", [[(d, hn1, "tn")] for d in dsecs], [], _each, 512, D, LANES, D, [BF] * len(dsecs)))
    g_ff, = _mm("dw_in_ff", [[(dff, hn1, "tn")]], [], _ident, LANES, D, LANES, D, [BF])
    gW["w_inT"] = jnp.concatenate(g_secs + [g_ff[:8]], axis=0)
    tok = push("in", gW)
    grad_x, _, gs["g_mix"] = _mm("proj_in_bwd", [[(d, w, "nn") for d, w in zip(dsecs, w_secs)] + [(dff, w_ffT, "nn")]], [x, dh1],
                                 _rms_bwd_epi, T, D, tm, D, [F32, BF], params=[tie(sp["g_mix"], tok)], n_acc=1)
    return grad_x, push_small(gs, loss_part)


_CANON = {"w_in": "w_inT", "w_xkv": "w_xkvT", "w_gate": "w_gateT", "w_up": "w_upT"}
_SMALL = (("g_mix", 0, 0, 1024), ("g_xattn", 1, 0, 1024), ("g_mem", 2, 0, 1024), ("g_ffn", 3, 0, 1024),
          ("g_ret_out", 4, 0, 512), ("g_xq", 4, 512, 256), ("g_xk", 4, 768, 256),
          ("g_fox_q", 5, 0, 64), ("g_fox_k", 5, 64, 64), ("b_forget", 5, 128, 8))
_LOSS_AT = (5, 256)


def _pack_small(tree):
    buf = jnp.zeros((SMALL_ROWS, D), F32)
    for name, r, c, n in _SMALL:
        buf = lax.dynamic_update_slice(buf, tree[name].reshape(1, n).astype(F32), (r, c))
    return buf


def _unpack_small(buf, like):
    return {name: buf[r:r + 1, c:c + n].reshape(like[name].shape) for name, r, c, n in _SMALL}


def _canonical(tree, name):
    a = tree[name][0]
    return a.T if W_SHARD[name][1] else a


def _from_canonical(a, name):
    return (a.T if W_SHARD[name][1] else a)[None]


def kernel(x, mem, g_mix, w_in, b_forget, g_ret_out, g_fox_q, g_fox_k, w_out, g_xattn, w_xq, w_xkv, g_mem, g_xq, g_xk, w_xo, g_ffn, w_gate, w_up, w_down, loss_target, m_g_mix, m_w_in, m_b_forget, m_g_ret_out, m_g_fox_q, m_g_fox_k, m_w_out, m_g_xattn, m_w_xq, m_w_xkv, m_g_mem, m_g_xq, m_g_xk, m_w_xo, m_g_ffn, m_w_gate, m_w_up, m_w_down, v_g_mix, v_w_in, v_b_forget, v_g_ret_out, v_g_fox_q, v_g_fox_k, v_w_out, v_g_xattn, v_w_xq, v_w_xkv, v_g_mem, v_g_xq, v_g_xk, v_w_xo, v_g_ffn, v_w_gate, v_w_up, v_w_down):
    names = ("g_mix", "w_in", "b_forget", "g_ret_out", "g_fox_q", "g_fox_k", "w_out", "g_xattn", "w_xq", "w_xkv", "g_mem",
             "g_xq", "g_xk", "w_xo", "g_ffn", "w_gate", "w_up", "w_down")
    w = dict(zip(names, (g_mix, w_in, b_forget, g_ret_out, g_fox_q, g_fox_k, w_out, g_xattn, w_xq, w_xkv, g_mem, g_xq, g_xk,
                         w_xo, g_ffn, w_gate, w_up, w_down)))
    m = dict(zip(names, (m_g_mix, m_w_in, m_b_forget, m_g_ret_out, m_g_fox_q, m_g_fox_k, m_w_out, m_g_xattn, m_w_xq, m_w_xkv,
                         m_g_mem, m_g_xq, m_g_xk, m_w_xo, m_g_ffn, m_w_gate, m_w_up, m_w_down)))
    v = dict(zip(names, (v_g_mix, v_w_in, v_b_forget, v_g_ret_out, v_g_fox_q, v_g_fox_k, v_w_out, v_g_xattn, v_w_xq, v_w_xkv,
                         v_g_mem, v_g_xq, v_g_xk, v_w_xo, v_g_ffn, v_w_gate, v_w_up, v_w_down)))
    small_names = [s[0] for s in _SMALL]
    me = 4 * lax.axis_index("x") + 2 * lax.axis_index("y") + lax.axis_index("c")
    me1 = me.astype(jnp.int32).reshape(1)

    first = _all_gather(_canonical(w, "w_in").astype(BF))
    first, rest = lax.optimization_barrier((first, [_canonical(w, n).astype(BF) for n in GATHER_REST]))
    rest_started = _exchange_start("gather_rest_start", rest, [lax.empty((N_DEV,) + a.shape, BF) for a in rest], scatter=False)

    def fetch_rest(after):
        srcs, lands = _exchange_wait("gather_rest_wait", rest_started, after, scatter=False)
        lands = [lax.dynamic_update_index_in_dim(a, own, me, axis=0) for a, own in zip(lands, srcs)]
        return {_CANON.get(n, n): a.reshape(N_DEV * a.shape[1], D) for n, a in zip(GATHER_REST, lands)}

    pushed = {}

    def push(group, grads):
        srcs = [grads[_CANON.get(n, n)].reshape(N_DEV, W_SHARD[n][0], D) for n in SCATTER_GROUPS[group]]
        pushed[group] = _exchange_start("scatter_%s_start" % group, srcs, [lax.empty(a.shape, BF) for a in srcs], scatter=True)
        return pushed[group][4]

    def push_small(gs, loss_part):
        small = lax.dynamic_update_slice(_pack_small(gs), loss_part[:, :1], _LOSS_AT)
        pushed["small"] = _exchange_start("gather_small_start", [small], [jnp.broadcast_to(small[None], (N_DEV,) + small.shape)],
                                          scatter=False)
        return pushed["small"][4]

    sp = {n: w[n].reshape(1, -1) for n in small_names}
    grad_x, done = _local_step(x[0], mem[0], loss_target[0], sp, first.reshape(N_DEV * W_SHARD["w_in"][0], D),
                               rest_started[4], fetch_rest, push, push_small)

    results, after = {}, done
    for group in ("ffn", "xattn", "out", "small", "in"):
        if group == "small":
            recv_small = _exchange_wait("gather_small_wait", pushed["small"], after, scatter=False)[1][0]
            g_sm, d_sm, m_sm, v_sm = _adamw("adamw_small", me1, recv_small, None, _pack_small(w), _pack_small(m), _pack_small(v))
            after = g_sm
            continue
        sents, recvs = _exchange_wait("scatter_%s_wait" % group, pushed[group], after, scatter=True)
        for name, sent, recv in zip(SCATTER_GROUPS[group], sents, recvs):
            res = _adamw("adamw_" + name, me1, recv, sent, *(_canonical(t, name) for t in (w, m, v)))
            results[name] = [_from_canonical(r, name) for r in res]
        after = results[SCATTER_GROUPS[group][-1]][0]
    loss = g_sm[_LOSS_AT[0], _LOSS_AT[1]]

    outs = []
    for k, sm in enumerate((g_sm, d_sm, m_sm, v_sm)):
        tree = _unpack_small(sm, w)
        tree.update({name: res[k] for name, res in results.items()})
        outs += [tree[n] for n in names]
    return (loss, grad_x[None], *outs)
```

```python
import jax
import jax.numpy as jnp
from jax import lax
from jax.experimental import pallas as pl
from jax.experimental.pallas import tpu as pltpu

F32 = jnp.float32
BF = jnp.bfloat16

D = 1024
HEAD = 64
CHUNK = 64
N_MEM = 256
XHEAD = 256
D_FF = 2816
EPS = 1e-6
NEG = -1e30
LANES = 128
N_DEV = 8
V7X_VMEM_BYTES = 64 * 1024 * 1024
VMEM_LIMIT = V7X_VMEM_BYTES - 8 * 1024 * 1024

ADAM_LR, ADAM_B1, ADAM_B2, ADAM_EPS, ADAM_WD, ADAM_STEP = 0.001, 0.9, 0.999, 1e-08, 0.01, 10

W_SHARD = {"w_in": (449, True), "w_out": (128, False), "w_xq": (128, False), "w_xkv": (256, True),
           "w_xo": (128, False), "w_gate": (352, True), "w_up": (352, True), "w_down": (352, False)}
GATHER_REST = ("w_out", "w_xq", "w_xkv", "w_xo", "w_gate", "w_up", "w_down")
SCATTER_GROUPS = {"ffn": ("w_gate", "w_up", "w_down"), "xattn": ("w_xq", "w_xo", "w_xkv"), "out": ("w_out",), "in": ("w_in",)}
SMALL_ROWS = 8

NT = (((1,), (1,)), ((), ()))
NN = (((1,), (0,)), ((), ()))
TN = (((0,), (0,)), ((), ()))
_DIMS = {"nn": NN, "nt": NT, "tn": TN}


def _params(sem):
    return pltpu.CompilerParams(dimension_semantics=sem, vmem_limit_bytes=VMEM_LIMIT)


def _mm(name, products, extras, epilogue, M, N, tm, tn, out_dtypes, params=(), n_acc=0):
    assert n_acc == 0 or tn == N
    flat = [t for p in products for t in p]
    counts = [len(p) for p in products]
    in_specs, args, where, slots = [], [], {}, []

    def operand(arr, spec, kind):
        key = (id(arr), kind)
        if key not in where:
            where[key] = len(args)
            args.append(arr)
            in_specs.append(spec)
        return where[key]

    for a, b, form in flat:
        if form == "tn":
            ia = operand(a, pl.BlockSpec((a.shape[0], tm), lambda i, j: (0, i)), "a_tn")
        else:
            ia = operand(a, pl.BlockSpec((tm, a.shape[1]), lambda i, j: (i, 0)), "a")
        if form == "nt":
            ib = operand(b, pl.BlockSpec((tn, b.shape[1]), lambda i, j: (j, 0)), "b_nt")
        else:
            ib = operand(b, pl.BlockSpec((b.shape[0], tn), lambda i, j: (0, j)), "b")
        slots.append((ia, ib))
    n_mm = len(args)
    for e in extras:
        in_specs.append(pl.BlockSpec((tm, tn), lambda i, j: (i, j)))
        args.append(e)
    for p in params:
        in_specs.append(pl.BlockSpec((1, tn), lambda i, j: (0, j)))
        args.append(p)
    n_in = len(args)
    n_out = len(out_dtypes)

    def body(*refs):
        ins, outs = refs[:n_in], refs[n_in:]
        prods, p = [], 0
        for c in counts:
            acc = None
            for _ in range(c):
                a = ins[slots[p][0]][...].astype(BF)
                b = ins[slots[p][1]][...].astype(BF)
                d = lax.dot_general(a, b, _DIMS[flat[p][2]], preferred_element_type=F32)
                acc = d if acc is None else acc + d
                p += 1
            prods.append(acc)
        ex = [r[...].astype(F32) for r in ins[n_mm:]]
        res = epilogue(*prods, *ex)
        for o, r in zip(outs[:n_out], res[:n_out]):
            o[...] = r.astype(o.dtype)
        for o, r in zip(outs[n_out:], res[n_out:]):
            @pl.when(pl.program_id(0) == 0)
            def _(o=o):
                o[...] = jnp.zeros(o.shape, F32)
            o[...] += r

    return pl.pallas_call(
        body, name=name, grid=(M // tm, N // tn), in_specs=in_specs,
        out_specs=[pl.BlockSpec((tm, tn), lambda i, j: (i, j)) for _ in out_dtypes]
        + [pl.BlockSpec((1, tn), lambda i, j: (0, j)) for _ in range(n_acc)],
        out_shape=[jax.ShapeDtypeStruct((M, N), dt) for dt in out_dtypes] + [jax.ShapeDtypeStruct((1, N), F32)] * n_acc,
        compiler_params=_params(("arbitrary", "arbitrary")),
    )(*args)


def _ident(x):
    return (x,)


def _each(*xs):
    return xs


def _spec(rows, w, off, per_j):
    if per_j:
        return pl.BlockSpec((rows, w), lambda j, i: (i, off + j))
    return pl.BlockSpec((rows, w), lambda j, i: (i, off))


def _pspec(rows, w, off, per_j):
    if per_j:
        return pl.BlockSpec((rows, w), lambda j, i: (0, off + j))
    return pl.BlockSpec((rows, w), lambda j, i: (0, off))


def _rw_fwd(name, fn, rows, params, outs, T, tm, nj, n_acc=0):
    in_specs = [_spec(tm, w, off, pj) for _, w, off, pj in rows] + [_pspec(a.shape[0], w, off, pj) for a, w, off, pj in params]
    args = [r[0] for r in rows] + [p[0] for p in params]
    n_in, n_out = len(args), len(outs)
    out_specs = [pl.BlockSpec((tm, w), lambda j, i: (i, j)) for _, w in outs]
    out_shape = [jax.ShapeDtypeStruct((T, nj * w), dt) for dt, w in outs]
    out_specs += [pl.BlockSpec((1, LANES), lambda j, i: (0, 0)) for _ in range(n_acc)]
    out_shape += [jax.ShapeDtypeStruct((1, LANES), F32) for _ in range(n_acc)]

    def body(*refs):
        vals = [r[...].astype(F32) for r in refs[:n_in]]
        res = fn(*vals)
        orefs = refs[n_in:]
        for k in range(n_out):
            orefs[k][...] = res[k].astype(orefs[k].dtype)
        first = (pl.program_id(0) == 0) & (pl.program_id(1) == 0)
        for k in range(n_acc):
            @pl.when(first)
            def _(k=k):
                orefs[n_out + k][...] = jnp.zeros((1, LANES), F32)
            orefs[n_out + k][...] += res[n_out + k]

    return pl.pallas_call(
        body, name=name, grid=(nj, T // tm), in_specs=in_specs, out_specs=out_specs, out_shape=out_shape,
        compiler_params=_params(("arbitrary", "arbitrary")),
    )(*args)


def _rw_bwd(name, fn, rows, params, cots, T, tm, nj, row_grads, param_grads, resid=None):
    in_specs = ([_spec(tm, w, off, pj) for _, w, off, pj in rows] + [_pspec(a.shape[0], w, off, pj) for a, w, off, pj in params]
                + [_spec(tm, w, off, pj) for _, w, off, pj in cots])
    args = [r[0] for r in rows] + [p[0] for p in params] + [c[0] for c in cots]
    if resid is not None:
        in_specs.append(_spec(tm, rows[0][1], rows[0][2], rows[0][3]))
        args.append(resid)
    nr, npar, nc = len(rows), len(params), len(cots)
    out_specs, out_shape, kinds = [], [], []
    for k, dts in enumerate(row_grads):
        for dt in (dts if isinstance(dts, (list, tuple)) else [dts]):
            if dt is not None:
                w = rows[k][1]
                out_specs.append(pl.BlockSpec((tm, w), lambda j, i: (i, j)))
                out_shape.append(jax.ShapeDtypeStruct((T, nj * w), dt))
                kinds.append(("row", k))
    for k, need in enumerate(param_grads):
        if need:
            a, w, off, pj = params[k]
            out_specs.append(_pspec(a.shape[0], w, off, pj))
            out_shape.append(jax.ShapeDtypeStruct(a.shape, F32))
            kinds.append(("par", k))

    def body(*refs):
        vals = [r[...].astype(F32) for r in refs[:nr + npar]]
        ct = tuple(r[...].astype(F32) for r in refs[nr + npar:nr + npar + nc])
        _, vjp = jax.vjp(lambda *a: tuple(fn(*a)), *vals)
        grads = list(vjp(ct))
        n_in = nr + npar + nc + (resid is not None)
        if resid is not None:
            grads[0] = grads[0] + refs[n_in - 1][...].astype(F32)
        orefs = refs[n_in:]
        j, i = pl.program_id(0), pl.program_id(1)
        for o, (kind, k) in zip(orefs, kinds):
            if kind == "row":
                o[...] = grads[k].astype(o.dtype)
            else:
                first = (i == 0) if params[k][3] else ((i == 0) & (j == 0))

                @pl.when(first)
                def _(o=o):
                    o[...] = jnp.zeros(o.shape, F32)
                o[...] += grads[nr + k]

    return pl.pallas_call(
        body, name=name, grid=(nj, T // tm), in_specs=in_specs, out_specs=out_specs, out_shape=out_shape,
        compiler_params=_params(("arbitrary", "arbitrary")),
    )(*args)


def _rms(x, g):
    return x * lax.rsqrt(jnp.mean(x * x, axis=-1, keepdims=True) + EPS) * g


def _rms_fn(x, g):
    return (_rms(x, g),)


def _lo_mask():
    return lax.broadcasted_iota(jnp.int32, (1, LANES), 1) < HEAD


def _gmean(x, lo):
    s0 = jnp.sum(jnp.where(lo, x, 0.0), axis=-1, keepdims=True)
    s1 = jnp.sum(jnp.where(lo, 0.0, x), axis=-1, keepdims=True)
    return jnp.where(lo, s0, s1) * (1.0 / HEAD)


def _fox_prep_fn(fq, fk, gq, gk):
    lo = _lo_mask()
    qn = fq * lax.rsqrt(_gmean(fq * fq, lo) + EPS) * gq * (HEAD ** -0.5)
    kn = fk * lax.rsqrt(_gmean(fk * fk, lo) + EPS) * gk
    return qn, kn


@jax.custom_vjp
def _swap_halves(x):
    bit = (lax.broadcasted_iota(jnp.int32, (1, LANES), 1) & (HEAD // 2)) == 0
    return jnp.where(bit, pltpu.roll(x, LANES - HEAD // 2, 1), pltpu.roll(x, HEAD // 2, 1))


_swap_halves.defvjp(lambda x: (_swap_halves(x), None), lambda _, g: (_swap_halves(g),))


def _ret_fn(rq, rk, rv, rg, cos, sin, s_in, g, lg):
    tb = rq.shape[0]
    nc = tb // CHUNK
    lo = _lo_mask()
    row = lax.broadcasted_iota(jnp.int32, (LANES, 1), 0) < HEAD
    same_head = row == lo
    q = (rq * cos + _swap_halves(rq) * sin) * (HEAD ** -0.5)
    k = rk * cos + _swap_halves(rk) * sin
    q3, k3, v3 = q.reshape(nc, CHUNK, LANES), k.reshape(nc, CHUNK, LANES), rv.reshape(nc, CHUNK, LANES)
    pos = lax.broadcasted_iota(jnp.int32, (CHUNK, 1), 0).astype(F32)
    q_decay = jnp.exp(lg * (pos + 1.0))
    k_decay = jnp.exp(lg * (CHUNK - 1.0 - pos))
    chunk_decay = jnp.exp(lg * float(CHUNK))
    dist = jnp.abs(lax.broadcasted_iota(jnp.int32, (CHUNK, CHUNK), 0) - lax.broadcasted_iota(jnp.int32, (CHUNK, CHUNK), 1)).astype(F32)
    v3b = v3.astype(BF)
    intra = []
    for hh in range(2):
        hm = lo if hh == 0 else ~lo
        lg_h = lg[:, hh * HEAD:hh * HEAD + 1]
        qm = jnp.where(hm, q3, 0.0).astype(BF)
        sc = jnp.einsum("nid,njd->nij", qm, k3.astype(BF), preferred_element_type=F32) * jnp.exp(lg_h * dist)[None]
        intra.append(jnp.einsum("nij,nje->nie", sc.astype(BF), v3b, preferred_element_type=F32))
    o = jnp.where(lo, intra[0], intra[1])
    kv = jnp.einsum("njd,nje->nde", (k3 * k_decay[None]).astype(BF), v3b, preferred_element_type=F32)
    kv = jnp.where(same_head[None], kv, 0.0)
    state, states = s_in, []
    for n in range(nc):
        states.append(state)
        state = state * chunk_decay + kv[n]
    s_prev = jnp.stack(states, axis=0)
    o = o + jnp.einsum("nid,nde->nie", (q3 * q_decay[None]).astype(BF), s_prev.astype(BF), preferred_element_type=F32)
    o = o.reshape(tb, LANES)
    mu = _gmean(o, lo)
    oc = o - mu
    y = oc * lax.rsqrt(_gmean(oc * oc, lo) + EPS) * g
    return jax.nn.silu(rg) * y, state


def _xattn_fn(qx, gq, gk, kk, vv):
    q = _rms(qx, gq)
    k = _rms(kk, gk)
    logits = lax.dot_general(q.astype(BF), k.astype(BF), NT, preferred_element_type=F32) * (XHEAD ** -0.5)
    p = jax.nn.softmax(logits, axis=-1)
    return (jnp.dot(p.astype(BF), vv.astype(BF), preferred_element_type=F32),)


def _swiglu_fwd_epi(g, u):
    return g, u, jax.nn.silu(g) * u


def _swiglu_bwd_epi(dact, g, u):
    _, vjp = jax.vjp(lambda a, b: jax.nn.silu(a) * b, g, u)
    return vjp(dact)


def _add_rms_epi(acc, resid, g):
    h = acc + resid
    return h, _rms(h, g)


def _add_loss_epi(acc, resid, target):
    err = (acc + resid) - target
    dy = err * (1.0 / D)
    part = jnp.sum(jnp.sum(err * err, axis=0, keepdims=True), axis=1, keepdims=True) * (0.5 / D)
    return dy, dy, jnp.broadcast_to(part, (1, err.shape[1]))


def _rms_bwd_epi(dhn, h, skip, g):
    _, vjp = jax.vjp(_rms, h, g)
    dh, dg = vjp(dhn)
    dh = dh + skip
    return dh, dh, dg


def _rms_bwd_first_epi(dhn, h, skip, g):
    return _rms_bwd_epi(dhn, h, skip, g)[1:]


def _ret_fwd(P, cos, sin, g_ret, lg, T, tb):
    nb = T // tb

    def body(rq, rk, rv, rg, c, s, g, l, o_ref, s0_ref, state):
        @pl.when(pl.program_id(1) == 0)
        def _():
            state[...] = jnp.zeros(state.shape, F32)
        s0_ref[0, 0] = state[...]
        out, s_new = _ret_fn(rq[...], rk[...], rv[...], rg[...], c[...], s[...], state[...], g[...], l[...])
        o_ref[...] = out.astype(o_ref.dtype)
        state[...] = s_new

    sec = lambda off: pl.BlockSpec((tb, LANES), lambda j, i: (i, off + j))
    tab = pl.BlockSpec((tb, LANES), lambda j, i: (i, 0))
    par = pl.BlockSpec((1, LANES), lambda j, i: (0, j))
    return pl.pallas_call(
        body, name="ret_fwd", grid=(4, nb),
        in_specs=[sec(0), sec(4), sec(8), sec(12), tab, tab, par, par],
        out_specs=[pl.BlockSpec((tb, LANES), lambda j, i: (i, j)), pl.BlockSpec((1, 1, LANES, LANES), lambda j, i: (j, i, 0, 0))],
        out_shape=[jax.ShapeDtypeStruct((T, 4 * LANES), BF), jax.ShapeDtypeStruct((4, nb, LANES, LANES), F32)],
        scratch_shapes=[pltpu.VMEM((LANES, LANES), F32)],
        compiler_params=_params(("arbitrary", "arbitrary")),
    )(P, P, P, P, cos, sin, g_ret, lg)


def _ret_bwd(P, cos, sin, g_ret, lg, s0, dmix, T, tb):
    nb = T // tb

    def body(rq, rk, rv, rg, c, s, g, l, s0_ref, do, drq, drk, drv, drg, dg, dstate):
        i = pl.program_id(1)

        @pl.when(i == 0)
        def _():
            dstate[...] = jnp.zeros(dstate.shape, F32)
            dg[...] = jnp.zeros(dg.shape, F32)

        cc, ss, ll = c[...], s[...], l[...]
        _, vjp = jax.vjp(lambda a, b, v, gate, st, gg: _ret_fn(a, b, v, gate, cc, ss, st, gg, ll),
                         rq[...], rk[...], rv[...], rg[...], s0_ref[0, 0], g[...])
        ga, gb, gv, ggate, gst, ggain = vjp((do[...], dstate[...]))
        drq[...] = ga.astype(drq.dtype)
        drk[...] = gb.astype(drk.dtype)
        drv[...] = gv.astype(drv.dtype)
        drg[...] = ggate.astype(drg.dtype)
        dstate[...] = gst
        dg[...] += ggain

    rev = lambda i: nb - 1 - i
    sec = lambda off: pl.BlockSpec((tb, LANES), lambda j, i: (rev(i), off + j))
    tab = pl.BlockSpec((tb, LANES), lambda j, i: (rev(i), 0))
    par = pl.BlockSpec((1, LANES), lambda j, i: (0, j))
    outb = pl.BlockSpec((tb, LANES), lambda j, i: (rev(i), j))
    return pl.pallas_call(
        body, name="ret_bwd", grid=(4, nb),
        in_specs=[sec(0), sec(4), sec(8), sec(12), tab, tab, par, par,
                  pl.BlockSpec((1, 1, LANES, LANES), lambda j, i: (j, rev(i), 0, 0)), outb],
        out_specs=[outb, outb, outb, outb, par],
        out_shape=[jax.ShapeDtypeStruct((T, 4 * LANES), BF)] * 4 + [jax.ShapeDtypeStruct((1, 4 * LANES), F32)],
        scratch_shapes=[pltpu.VMEM((LANES, LANES), F32)],
        compiler_params=_params(("arbitrary", "arbitrary")),
    )(P, P, P, P, cos, sin, g_ret, lg, s0, dmix)


_FB = 128


def _tri(lower):
    r = lax.broadcasted_iota(jnp.int32, (_FB, _FB), 0)
    c = lax.broadcasted_iota(jnp.int32, (_FB, _FB), 1)
    return ((r >= c) if lower else (r <= c)).astype(F32)


def _fgate_fwd(ffp, bpad, T):
    def body(ff_ref, b_ref, fc_ref, fr_ref):
        lane = lax.broadcasted_iota(jnp.int32, (1, LANES), 1)
        tri = _tri(True)
        carry = jnp.zeros((1, LANES), F32)
        for blk in range(T // _FB):
            z = ff_ref[blk * _FB:(blk + 1) * _FB, :] + b_ref[...]
            lf = jnp.where(lane < 8, jax.nn.log_sigmoid(z), 0.0)
            f = jnp.dot(tri, lf, precision=lax.Precision.HIGHEST, preferred_element_type=F32) + carry
            carry = f[_FB - 1:_FB, :]
            fc_ref[blk * _FB:(blk + 1) * _FB, :] = f
            fr_ref[:, blk * _FB:(blk + 1) * _FB] = f.T[:8, :]

    return pl.pallas_call(
        body, name="fgate_fwd",
        out_shape=[jax.ShapeDtypeStruct((T, LANES), F32), jax.ShapeDtypeStruct((8, T), F32)],
        compiler_params=pltpu.CompilerParams(vmem_limit_bytes=VMEM_LIMIT),
    )(ffp, bpad)


_BIAS_LANE = HEAD


def _head_bias_col(fc, head):
    lane = lax.broadcasted_iota(jnp.int32, (1, LANES), 1)
    return jnp.sum(jnp.where(lane == head, fc, 0.0), axis=-1, keepdims=True)


def _split3(f):
    hi = f.astype(BF).astype(F32)
    mid = (f - hi).astype(BF).astype(F32)
    lo = ((f - hi) - mid).astype(BF).astype(F32)
    return hi, mid, lo


def _fox_operands(P, fc, g_fq2, g_fk2, T, tm):
    def body(fq_ref, fk_ref, fv_ref, fc_ref, gq_ref, gk_ref, qat_ref, ka_ref, kat_ref, va_ref, vat_ref):
        j = pl.program_id(0)
        lane = lax.broadcasted_iota(jnp.int32, (1, LANES), 1)
        qn, kn = _fox_prep_fn(fq_ref[...], fk_ref[...], gq_ref[...], gk_ref[...])
        v = fv_ref[...]
        fcb = fc_ref[...]
        b = _BIAS_LANE
        for hh in range(2):
            hi, mid, lo = _split3(_head_bias_col(fcb, 2 * j + hh))
            take = (lambda a: a) if hh == 0 else (lambda a: pltpu.roll(a, HEAD, 1))
            qa = jnp.where(lane < HEAD, take(qn), jnp.where(lane == b, hi, jnp.where(lane == b + 1, mid, jnp.where(
                lane == b + 2, lo, jnp.where(lane < b + 6, 1.0, 0.0)))))
            ka = jnp.where(lane < HEAD, take(kn), jnp.where(lane < b + 3, 1.0, jnp.where(lane == b + 3, -hi, jnp.where(
                lane == b + 4, -mid, jnp.where(lane == b + 5, -lo, 0.0)))))
            va = jnp.where(lane < HEAD, take(v), 0.0)
            qat_ref[hh] = qa.T.astype(BF)
            for val, ref, tref in ((ka, ka_ref, kat_ref), (va, va_ref, vat_ref)):
                ref[hh] = val.astype(BF)
                tref[hh] = val.T.astype(BF)

    sec = lambda off: pl.BlockSpec((tm, LANES), lambda j, i: (i, off + j))
    par = pl.BlockSpec((1, LANES), lambda j, i: (0, 0))
    nat = pl.BlockSpec((2, tm, LANES), lambda j, i: (j, i, 0))
    trn = pl.BlockSpec((2, LANES, tm), lambda j, i: (j, 0, i))
    return pl.pallas_call(
        body, name="fox_operands", grid=(4, T // tm),
        in_specs=[sec(16), sec(20), sec(24), pl.BlockSpec((tm, LANES), lambda j, i: (i, 0)), par, par],
        out_specs=[trn, nat, trn, nat, trn],
        out_shape=[jax.ShapeDtypeStruct((8, LANES, T), BF)]
        + [jax.ShapeDtypeStruct((8, T, LANES), BF), jax.ShapeDtypeStruct((8, LANES, T), BF)] * 2,
        compiler_params=_params(("parallel", "arbitrary")),
    )(P, P, P, fc, g_fq2, g_fk2)


def _fox_forward(qat, ka, vat, T, tq, tk):
    nq, per = T // tq, tq // tk
    assert per == 2
    RC = 64

    def body(qat_ref, ka_ref, vat_ref, o_ref, lse_ref, s_scr, p_scr, a_scr, m_scr, l_scr, acc_scr):
        i = pl.program_id(1)
        sub = lax.broadcasted_iota(jnp.int32, (8, 1), 0)
        row = lax.broadcasted_iota(jnp.int32, (RC, tq), 0)
        col = lax.broadcasted_iota(jnp.int32, (RC, tq), 1)
        m_scr[...] = jnp.full(m_scr.shape, NEG, F32)
        l_scr[...] = jnp.zeros(l_scr.shape, F32)
        acc_scr[...] = jnp.zeros(acc_scr.shape, F32)

        def scores(slot, kb):
            k0 = pl.multiple_of(kb * tk, tk)
            for hh in range(2):
                s_scr[slot, hh] = jnp.dot(ka_ref[hh, pl.ds(k0, tk), :], qat_ref[hh], preferred_element_type=F32)

        def softmax(slot, kb, diagonal):
            shift = kb * tk - i * tq
            for hh in range(2):
                def masked(r):
                    tile = s_scr[slot, hh, r * RC:(r + 1) * RC, :]
                    return jnp.where(row + (r * RC + shift) <= col, tile, NEG) if diagonal else tile

                mx = jnp.max(masked(0), axis=0, keepdims=True)
                for r in range(1, tk // RC):
                    mx = jnp.maximum(mx, jnp.max(masked(r), axis=0, keepdims=True))
                m_old = m_scr[hh, 0:1, :]
                m2 = jnp.maximum(m_old, mx)
                a = jnp.exp(m_old - m2)
                lsum = jnp.zeros((1, tq), F32)
                for r in range(tk // RC):
                    p = jnp.exp(masked(r) - m2)
                    p_scr[slot, hh, r * RC:(r + 1) * RC, :] = p.astype(BF)
                    lsum = lsum + jnp.sum(p, axis=0, keepdims=True)
                m_scr[hh] = jnp.broadcast_to(m2, (8, tq))
                l_scr[hh] = jnp.broadcast_to(a * l_scr[hh, 0:1, :] + lsum, (8, tq))
                a_scr[slot, hh] = jnp.broadcast_to(a, (8, tq))

        def values(slot, kb):
            k0 = pl.multiple_of(kb * tk, tk)
            for hh in range(2):
                pv = jnp.dot(vat_ref[hh, 0:HEAD, pl.ds(k0, tk)], p_scr[slot, hh], preferred_element_type=F32)
                acc_scr[hh] = a_scr[slot, hh, 0:1, :] * acc_scr[hh] + pv

        def pair(kb, diag_first, diag_second, more):
            if more:
                scores(0, kb + 2)
            softmax(1, kb + 1, diag_first)
            values(0, kb)
            if more:
                scores(1, kb + 3)
                softmax(0, kb + 2, diag_second)
            values(1, kb + 1)

        scores(0, 0)
        scores(1, 1)
        softmax(0, 0, True)

        @pl.loop(0, jnp.maximum(i - 1, 0))
        def _(t):
            pair(2 * t, False, False, True)

        @pl.when(i >= 1)
        def _():
            pair(2 * (i - 1), False, True, True)

        pair(2 * i, True, False, False)

        o_ref[...] = jnp.concatenate([acc_scr[hh] / l_scr[hh, 0:1, :] for hh in range(2)], axis=0).T
        lses = [m_scr[hh, 0:1, :] + jnp.log(l_scr[hh, 0:1, :]) for hh in range(2)]
        lse_ref[0] = jnp.where(sub == 0, lses[0], jnp.where(sub == 1, lses[1], 0.0))

    return pl.pallas_call(
        body, name="fox_forward", grid=(4, nq),
        in_specs=[pl.BlockSpec((2, LANES, tq), lambda j, i: (j, 0, i)), pl.BlockSpec((2, T, LANES), lambda j, i: (j, 0, 0)),
                  pl.BlockSpec((2, LANES, T), lambda j, i: (j, 0, 0))],
        out_specs=[pl.BlockSpec((tq, LANES), lambda j, i: (i, j)), pl.BlockSpec((1, 8, tq), lambda j, i: (j, 0, i))],
        out_shape=[jax.ShapeDtypeStruct((T, 4 * LANES), F32), jax.ShapeDtypeStruct((4, 8, T), F32)],
        scratch_shapes=[pltpu.VMEM((2, 2, tk, tq), F32), pltpu.VMEM((2, 2, tk, tq), BF), pltpu.VMEM((2, 2, 8, tq), F32),
                        pltpu.VMEM((2, 8, tq), F32), pltpu.VMEM((2, 8, tq), F32), pltpu.VMEM((2, HEAD, tq), F32)],
        compiler_params=_params(("parallel", "arbitrary")),
    )(qat, ka, vat)


def _fox_cotangent(dmix, fox, T, tm):
    def body(do_ref, o_ref, doat_ref, dl_ref):
        lane = lax.broadcasted_iota(jnp.int32, (1, LANES), 1)
        sub = lax.broadcasted_iota(jnp.int32, (8, 1), 0)
        dob = do_ref[...].astype(BF).astype(F32)
        prod_t = (dob * o_ref[...]).T
        d0 = jnp.sum(prod_t[:HEAD], axis=0, keepdims=True)
        d1 = jnp.sum(prod_t[HEAD:], axis=0, keepdims=True)
        dl_ref[0] = jnp.where(sub == 0, d0, jnp.where(sub == 1, d1, 0.0))
        for hh in range(2):
            val = jnp.where(lane < HEAD, dob if hh == 0 else pltpu.roll(dob, HEAD, 1), 0.0)
            doat_ref[hh] = val.T.astype(BF)

    return pl.pallas_call(
        body, name="fox_cotangent", grid=(4, T // tm),
        in_specs=[pl.BlockSpec((tm, LANES), lambda j, i: (i, 4 + j)), pl.BlockSpec((tm, LANES), lambda j, i: (i, j))],
        out_specs=[pl.BlockSpec((2, LANES, tm), lambda j, i: (j, 0, i)), pl.BlockSpec((1, 8, tm), lambda j, i: (j, 0, i))],
        out_shape=[jax.ShapeDtypeStruct((8, LANES, T), BF), jax.ShapeDtypeStruct((4, 8, T), F32)],
        compiler_params=_params(("parallel", "arbitrary")),
    )(dmix, fox)


def _fox_backward(qat, ka, kat, va, doat, lse, dl, T, tq, tk):
    nq, nk = T // tq, T // tk

    def body(qat_ref, ka_ref, kat_ref, va_ref, doat_ref, lse_ref, dl_ref,
             dq_ref, dk_ref, dv_ref, df_ref, dr_ref, dqt, dkt, dvt, df_acc, sdp, pds):
        j, kb = pl.program_id(0), pl.program_id(1)
        lane = lax.broadcasted_iota(jnp.int32, (1, LANES), 1)
        first = (kb * tk) // tq

        @pl.when(kb == 0)
        def _():
            dqt[...] = jnp.zeros(dqt.shape, F32)

        dkt[...] = jnp.zeros(dkt.shape, F32)
        dvt[...] = jnp.zeros(dvt.shape, F32)
        df_acc[...] = jnp.zeros(df_acc.shape, F32)

        RC = 64
        last = nq - 1

        def products(slot, qi):
            q0 = pl.multiple_of(qi * tq, tq)
            for hh in range(2):
                sdp[slot, hh, 0] = jnp.dot(ka_ref[hh], qat_ref[hh, :, pl.ds(q0, tq)], preferred_element_type=F32)
                sdp[slot, hh, 1] = jnp.dot(va_ref[hh], doat_ref[hh, :, pl.ds(q0, tq)], preferred_element_type=F32)

        def softmax_bwd(slot, qi, diagonal, valid):
            q0 = pl.multiple_of(qi * tq, tq)
            shift = kb * tk - first * tq
            col = lax.broadcasted_iota(jnp.int32, (RC, tq), 1)
            row = lax.broadcasted_iota(jnp.int32, (RC, tq), 0)
            for hh in range(2):
                lse_row = lse_ref[0, hh:hh + 1, pl.ds(q0, tq)]
                dl_row = dl_ref[0, hh:hh + 1, pl.ds(q0, tq)]
                rsum = jnp.zeros((1, tq), F32)
                for r in range(tk // RC):
                    rows = slice(r * RC, (r + 1) * RC)
                    p = jnp.exp(sdp[slot, hh, 0, rows, :] - lse_row)
                    p = jnp.where((row + (r * RC + shift) <= col) if diagonal else valid, p, 0.0)
                    ds = p * (sdp[slot, hh, 1, rows, :] - dl_row)
                    pds[slot, hh, 0, rows, :] = p.astype(BF)
                    pds[slot, hh, 1, rows, :] = ds.astype(BF)
                    rsum = rsum + jnp.sum(ds, axis=0, keepdims=True)
                    part = ds[:, 0:LANES]
                    for c in range(1, tq // LANES):
                        part = part + ds[:, c * LANES:(c + 1) * LANES]
                    df_acc[hh, rows, :] += part
                dqt[hh, HEAD:HEAD + 8, pl.ds(q0, tq)] += jnp.broadcast_to(rsum, (8, tq))

        def accumulate(slot, qi):
            q0 = pl.multiple_of(qi * tq, tq)
            for hh in range(2):
                dvt[hh] += lax.dot_general(doat_ref[hh, 0:HEAD, pl.ds(q0, tq)], pds[slot, hh, 0], NT, preferred_element_type=F32)
                dkt[hh] += lax.dot_general(qat_ref[hh, 0:HEAD, pl.ds(q0, tq)], pds[slot, hh, 1], NT, preferred_element_type=F32)
                dqt[hh, 0:HEAD, pl.ds(q0, tq)] += jnp.dot(kat_ref[hh, 0:HEAD, :], pds[slot, hh, 1], preferred_element_type=F32)

        products(0, first)
        products(1, jnp.minimum(first + 1, last))
        softmax_bwd(0, first, True, None)

        @pl.loop(0, (nq - first + 1) // 2)
        def _(t):
            qi = first + 2 * t
            products(0, jnp.minimum(qi + 2, last))
            softmax_bwd(1, jnp.minimum(qi + 1, last), False, qi + 1 <= last)
            accumulate(0, qi)
            products(1, jnp.minimum(qi + 3, last))
            softmax_bwd(0, jnp.minimum(qi + 2, last), False, qi + 2 <= last)
            accumulate(1, jnp.minimum(qi + 1, last))

        dk_ref[...] = jnp.concatenate([dkt[0], dkt[1]], axis=0).T
        dv_ref[...] = jnp.concatenate([dvt[0], dvt[1]], axis=0).T.astype(dv_ref.dtype)
        f0 = -jnp.sum(df_acc[0], axis=1, keepdims=True)
        f1 = -jnp.sum(df_acc[1], axis=1, keepdims=True)
        df_ref[0] = jnp.where(lane == 2 * j, f0, jnp.where(lane == 2 * j + 1, f1, 0.0))

        @pl.when(kb == nk - 1)
        def _():
            for t in range(nq):
                cols = slice(t * tq, (t + 1) * tq)
                dq_ref[cols, :] = jnp.concatenate([dqt[0, 0:HEAD, cols], dqt[1, 0:HEAD, cols]], axis=0).T
                rsum = jnp.concatenate([dqt[0, HEAD:HEAD + 8, cols], dqt[1, HEAD:HEAD + 8, cols],
                                        jnp.zeros((LANES - 16, tq), F32)], axis=0).T
                dr_ref[0, cols, :] = jnp.where(lane == 2 * j, rsum[:, 0:1], jnp.where(lane == 2 * j + 1, rsum[:, 8:9], 0.0))

    trn_full = pl.BlockSpec((2, LANES, T), lambda j, kb: (j, 0, 0))
    nat_blk = pl.BlockSpec((2, tk, LANES), lambda j, kb: (j, kb, 0))
    trn_blk = pl.BlockSpec((2, LANES, tk), lambda j, kb: (j, 0, kb))
    rows = pl.BlockSpec((1, 8, T), lambda j, kb: (j, 0, 0))
    blk = pl.BlockSpec((tk, LANES), lambda j, kb: (kb, j))
    return pl.pallas_call(
        body, name="fox_backward", grid=(4, nk),
        in_specs=[trn_full, nat_blk, trn_blk, nat_blk, trn_full, rows, rows],
        out_specs=[pl.BlockSpec((T, LANES), lambda j, kb: (0, j)), blk, blk, pl.BlockSpec((1, tk, LANES), lambda j, kb: (j, kb, 0)),
                   pl.BlockSpec((1, T, LANES), lambda j, kb: (j, 0, 0))],
        out_shape=[jax.ShapeDtypeStruct((T, 4 * LANES), F32), jax.ShapeDtypeStruct((T, 4 * LANES), F32),
                   jax.ShapeDtypeStruct((T, 4 * LANES), BF), jax.ShapeDtypeStruct((4, T, LANES), F32),
                   jax.ShapeDtypeStruct((4, T, LANES), F32)],
        scratch_shapes=[pltpu.VMEM((2, HEAD + 8, T), F32), pltpu.VMEM((2, HEAD, tk), F32), pltpu.VMEM((2, HEAD, tk), F32),
                        pltpu.VMEM((2, tk, LANES), F32), pltpu.VMEM((2, 2, 2, tk, tq), F32), pltpu.VMEM((2, 2, 2, tk, tq), BF)],
        compiler_params=_params(("arbitrary", "arbitrary")),
    )(qat, ka, kat, va, doat, lse, dl)


def _fgate_bwd_col(ffp, bpad, dfc4, drc4, T):
    def body(ff_ref, b_ref, dfc_ref, drc_ref, dff_ref, db_ref):
        lane = lax.broadcasted_iota(jnp.int32, (1, LANES), 1)
        tri = _tri(False)
        carry = jnp.zeros((1, LANES), F32)
        db = jnp.zeros((1, LANES), F32)
        for blk in reversed(range(T // _FB)):
            rows = slice(blk * _FB, (blk + 1) * _FB)
            dcol = dfc_ref[0, rows, :] + drc_ref[0, rows, :]
            for pair in range(1, 4):
                dcol = dcol + (dfc_ref[pair, rows, :] + drc_ref[pair, rows, :])
            dlf = jnp.dot(tri, dcol, precision=lax.Precision.HIGHEST, preferred_element_type=F32) + carry
            carry = dlf[0:1, :]
            z = ff_ref[blk * _FB:(blk + 1) * _FB, :] + b_ref[...]
            dz = jnp.where(lane < 8, dlf * jax.nn.sigmoid(-z), 0.0)
            dff_ref[blk * _FB:(blk + 1) * _FB, :] = dz.astype(dff_ref.dtype)
            db = db + jnp.sum(dz, axis=0, keepdims=True)
        db_ref[...] = db

    return pl.pallas_call(
        body, name="fgate_bwd",
        out_shape=[jax.ShapeDtypeStruct((T, LANES), BF), jax.ShapeDtypeStruct((1, LANES), F32)],
        compiler_params=pltpu.CompilerParams(vmem_limit_bytes=VMEM_LIMIT),
    )(ffp, bpad, dfc4, drc4)


MESH = pl.DeviceIdType.MESH
N_PEERS = N_DEV - 1


def _place():
    return lax.axis_index("x"), lax.axis_index("y"), lax.axis_index("c")


def _all_gather(shard):
    R, W = shard.shape

    def body(x_ref, out_ref, send_sems, recv_sems, local_sem):
        x, y, c = _place()
        me, sibling = (x, y, c), (x, y, 1 - c)
        chips = [(1 - x, y), (x, 1 - y), (1 - x, 1 - y)]

        def slot(px, py, pc):
            return out_ref.at[4 * px + 2 * py + pc]

        def copy(k, block, to, src=None):
            return pltpu.make_async_remote_copy(
                src_ref=slot(*block) if src is None else src, dst_ref=slot(*block),
                send_sem=send_sems.at[k], recv_sem=recv_sems.at[k], device_id=to, device_id_type=MESH)

        mine = pltpu.make_async_copy(x_ref, slot(*me), local_sem)
        mine.start()
        first = [copy(0, me, sibling, src=x_ref)]
        first += [copy(1 + n, me, (*chip, c), src=x_ref) for n, chip in enumerate(chips)]
        for cp in first:
            cp.start()
        passed = [copy(4 + n, (*chip, c), sibling) for n, chip in enumerate(chips)]
        for n, chip in enumerate(chips):
            copy(1 + n, (*chip, c), me).wait_recv()
            passed[n].start()
        copy(0, sibling, me).wait_recv()
        for n, chip in enumerate(chips):
            copy(4 + n, (*chip, 1 - c), me).wait_recv()
        for cp in first + passed:
            cp.wait_send()
        mine.wait()

    return pl.pallas_call(
        body, name="all_gather_weights",
        out_shape=jax.ShapeDtypeStruct((N_DEV, R, W), shard.dtype),
        in_specs=[pl.BlockSpec(memory_space=pl.ANY)], out_specs=pl.BlockSpec(memory_space=pl.ANY),
        scratch_shapes=[pltpu.SemaphoreType.DMA((N_PEERS,)), pltpu.SemaphoreType.DMA((N_PEERS,)), pltpu.SemaphoreType.DMA],
    )(shard)


def _exchange_copies(src_refs, land_refs, send_sems, recv_sems, scatter):
    x, y, c = _place()
    me = 4 * x + 2 * y + c
    copies = []
    for k, (src_ref, land_ref) in enumerate(zip(src_refs, land_refs)):
        for r in range(1, N_DEV):
            px, py, pc = x ^ (r >> 2), y ^ ((r >> 1) & 1), c ^ (r & 1)
            copies.append(pltpu.make_async_remote_copy(
                src_ref=src_ref.at[4 * px + 2 * py + pc] if scatter else src_ref, dst_ref=land_ref.at[me],
                send_sem=send_sems.at[k * N_PEERS + r - 1], recv_sem=recv_sems.at[k * N_PEERS + r - 1],
                device_id=(px, py, pc), device_id_type=MESH))
    return copies


_HBM = pl.BlockSpec(memory_space=pltpu.HBM)
_SEM = pl.BlockSpec(memory_space=pltpu.SEMAPHORE)
_EFFECT = pltpu.SideEffectType.DATAFLOW_SIDE_EFFECTING


def _exchange_start(name, srcs, lands, scatter):
    n = len(srcs)

    def body(*refs):
        send_sems, recv_sems = refs[2 * n], refs[2 * n + 1]
        for cp in _exchange_copies(refs[:n], refs[n:2 * n], send_sems, recv_sems, scatter):
            cp.start()
        token = refs[-1]
        token[...] = jnp.zeros(token.shape, F32)

    arrays = list(srcs) + list(lands)
    out = pl.pallas_call(
        body, name=name,
        out_shape=(pltpu.SemaphoreType.DMA((n * N_PEERS,)), pltpu.SemaphoreType.DMA((n * N_PEERS,)))
        + tuple(pltpu.HBM(a.shape, a.dtype) for a in arrays) + (jax.ShapeDtypeStruct((8, LANES), F32),),
        in_specs=(_HBM,) * (2 * n), out_specs=(_SEM, _SEM) + (_HBM,) * (2 * n) + (pl.BlockSpec(memory_space=pltpu.VMEM),),
        input_output_aliases={k: 2 + k for k in range(2 * n)},
        compiler_params=pltpu.CompilerParams(has_side_effects=_EFFECT),
    )(*(pltpu.with_memory_space_constraint(a, pltpu.HBM) for a in arrays))
    return out[0], out[1], out[2:2 + n], out[2 + n:2 + 2 * n], out[-1]


def _exchange_wait(name, started, after, scatter):
    send_sems, recv_sems, srcs, lands, _ = started
    n = len(srcs)

    def body(*refs):
        copies = _exchange_copies(refs[:n], refs[n:2 * n], refs[2 * n], refs[2 * n + 1], scatter)
        for cp in copies:
            cp.wait_send()
        for cp in copies:
            cp.wait_recv()

    arrays = list(srcs) + list(lands)
    out = pl.pallas_call(
        body, name=name,
        out_shape=tuple(pltpu.HBM(a.shape, a.dtype) for a in arrays),
        in_specs=(_HBM,) * (2 * n) + (_SEM, _SEM, pl.BlockSpec(memory_space=pl.ANY)), out_specs=(_HBM,) * (2 * n),
        input_output_aliases={k: k for k in range(2 * n)},
        compiler_params=pltpu.CompilerParams(has_side_effects=_EFFECT),
    )(*arrays, send_sems, recv_sems, after)
    return out[:n], out[n:]


def _adam_update(g, w, m, v):
    m2 = ADAM_B1 * m + (1.0 - ADAM_B1) * g
    v2 = ADAM_B2 * v + (1.0 - ADAM_B2) * jnp.square(g)
    m_hat = m2 / (1.0 - ADAM_B1 ** ADAM_STEP)
    v_hat = v2 / (1.0 - ADAM_B2 ** ADAM_STEP)
    return g, -ADAM_LR * (m_hat / (jnp.sqrt(v_hat) + ADAM_EPS) + ADAM_WD * w), m2, v2


def _adamw(name, me, slots, sent, w, m, v):
    R, W = w.shape
    steps = max(k for k in (4, 2, 1) if k == 1 or (R % k == 0 and (R // k) % 16 == 0))
    tr = R // steps

    def body(me_ref, s_ref, *refs):
        if sent is not None:
            g = refs[0][0].astype(F32)
            refs = refs[1:]
        else:
            g = jnp.zeros((tr, W), F32)
        for s in range(N_DEV):
            part = s_ref[s].astype(F32)
            g = g + (part if sent is None else jnp.where(me_ref[0] == s, 0.0, part))
        w_ref, m_ref, v_ref = refs[:3]
        for o, r in zip(refs[3:], _adam_update(g, w_ref[...], m_ref[...], v_ref[...])):
            o[...] = r

    rows = pl.BlockSpec((tr, W), lambda i, me_ref: (i, 0))
    in_specs = [pl.BlockSpec((N_DEV, tr, W), lambda i, me_ref: (0, i, 0))]
    args = [slots]
    if sent is not None:
        in_specs.append(pl.BlockSpec((1, tr, W), lambda i, me_ref: (me_ref[0], i, 0)))
        args.append(sent)
    return pl.pallas_call(
        body, name=name,
        grid_spec=pltpu.PrefetchScalarGridSpec(num_scalar_prefetch=1, grid=(steps,), in_specs=in_specs + [rows] * 3,
                                               out_specs=[rows] * 4),
        out_shape=[jax.ShapeDtypeStruct((R, W), F32)] * 4,
        compiler_params=_params(("arbitrary",)),
    )(me, *args, w, m, v)


def _tables(T):
    pos = jnp.arange(T, dtype=F32)
    inv_freq = 10000.0 ** (-jnp.arange(0, HEAD, 2, dtype=F32) / HEAD)
    ang = pos[:, None] * inv_freq[None, :]
    cos, sin = jnp.cos(ang), jnp.sin(ang)
    cos4 = jnp.tile(cos, (1, 4))
    sin4 = jnp.tile(jnp.concatenate([-sin, sin], axis=1), (1, 2))
    log_g = jnp.log(1.0 - 2.0 ** (-5.0 - jnp.arange(8, dtype=F32)))
    return cos4, sin4, jnp.repeat(log_g, HEAD)[None, :]


def _local_step(x, mem, target, sp, w_inT, token, fetch_rest, push, push_small):
    T = x.shape[0]
    tm = min(512, T)
    tq = min(256, T)
    tb = min(1024, T)
    cos4, sin4, lg = _tables(T)
    g_fq2 = jnp.tile(sp["g_fox_q"], (1, 2))
    g_fk2 = jnp.tile(sp["g_fox_k"], (1, 2))
    g_ret = sp["g_ret_out"].reshape(1, 8 * HEAD)
    bpad = jnp.pad(sp["b_forget"], ((0, 0), (0, LANES - 8)))
    w_secs = [w_inT[k * 512:(k + 1) * 512] for k in range(7)]
    w_ffT = jnp.pad(w_inT[3584:3592], ((0, LANES - 8), (0, 0)))
    w_mainT = w_inT[:3584]
    tie = lambda p, tok: p + tok[0:1, 0:1]
    tm2, tm4 = min(1024, T), min(2048, T)

    hn1, = _rw_fwd("rms_mix", _rms_fn, [(x, D, 0, False)], [(tie(sp["g_mix"], token), D, 0, False)], [(BF, D)], T, tm4, 1)
    P, = _mm("proj_in", [[(hn1, w_mainT, "nt")]], [], _ident, T, 3584, tm4, 512, [F32])
    ffp, = _mm("proj_ff", [[(hn1, w_ffT, "nt")]], [], _ident, T, LANES, tm, LANES, [F32])
    ret, s0 = _ret_fwd(P, cos4, sin4, g_ret, lg, T, tb)
    fc, _ = _fgate_fwd(ffp, bpad, T)
    qat, ka, kat, va, vat = _fox_operands(P, fc, g_fq2, g_fk2, T, tm4)
    fox, lse = _fox_forward(qat, ka, vat, T, min(512, T), tq)
    W = fetch_rest(fox)
    w_out_halves = (W["w_out"][:4 * LANES], W["w_out"][4 * LANES:])
    h1, hn2 = _mm("proj_out", [[(ret, w_out_halves[0], "nn"), (fox, w_out_halves[1], "nn")]], [x], _add_rms_epi, T, D, tm2, D,
                  [F32, BF], params=[sp["g_xattn"]])

    qx, = _mm("proj_xq", [[(hn2, W["w_xq"], "nn")]], [], _ident, T, D, tm2, D, [F32])
    memn, = _rw_fwd("rms_mem", _rms_fn, [(mem, D, 0, False)], [(sp["g_mem"], D, 0, False)], [(BF, D)], N_MEM, N_MEM, 1)
    kv, = _mm("proj_xkv", [[(memn, W["w_xkvT"], "nt")]], [], _ident, N_MEM, 2 * D, N_MEM, 512, [F32])
    xa_rows = [(qx, XHEAD, 0, True)]
    xa_params = [(sp["g_xq"], XHEAD, 0, False), (sp["g_xk"], XHEAD, 0, False), (kv, XHEAD, 0, True), (kv, XHEAD, 4, True)]
    xo, = _rw_fwd("xattn_fwd", _xattn_fn, xa_rows, xa_params, [(BF, XHEAD)], T, tm4, 4)
    h2, hn3 = _mm("proj_xo", [[(xo, W["w_xo"], "nn")]], [h1], _add_rms_epi, T, D, tm2, D, [F32, BF], params=[sp["g_ffn"]])

    gate, up, act = _mm("ffn_in", [[(hn3, W["w_gateT"], "nt")], [(hn3, W["w_upT"], "nt")]], [], _swiglu_fwd_epi,
                        T, D_FF, tm4, 256, [BF, BF, BF])
    dy, dyb, loss_part = _mm("ffn_out", [[(act, W["w_down"], "nn")]], [h2, target], _add_loss_epi, T, D, tm, D, [F32, BF], n_acc=1)

    dgate, dup = _mm("ffn_out_bwd", [[(dyb, W["w_down"], "nt")]], [gate, up], _swiglu_bwd_epi, T, D_FF, tm4, 256, [BF, BF])
    gW = {}
    gW["w_gateT"], gW["w_upT"] = _mm("dw_gate_up", [[(dgate, hn3, "tn")], [(dup, hn3, "tn")]], [], _each, D_FF, D, 256, D, [BF, BF])
    gW["w_down"], = _mm("dw_down", [[(act, dyb, "tn")]], [], _ident, D_FF, D, 256, D, [BF])
    tok = push("ffn", gW)
    gs = {}
    dh2, dh2b, gs["g_ffn"] = _mm("ffn_in_bwd", [[(dgate, W["w_gateT"], "nn"), (dup, W["w_upT"], "nn")]], [h2, dy], _rms_bwd_epi,
                                 T, D, min(256, T), D, [F32, BF], params=[tie(sp["g_ffn"], tok)], n_acc=1)

    dxo, = _mm("proj_xo_bwd", [[(dh2b, W["w_xo"], "nt")]], [], _ident, T, D, tm2, D, [BF])
    gW["w_xo"], = _mm("dw_xo", [[(xo, dh2b, "tn")]], [], _ident, D, D, 256, D, [BF])
    dqx, gs["g_xq"], gs["g_xk"], dkv_k, dkv_v = _rw_bwd(
        "xattn_bwd", _xattn_fn, xa_rows, xa_params, [(dxo, XHEAD, 0, True)], T, tm4, 4, [BF], [True, True, True, True])
    dkv = jnp.concatenate([dkv_k[:, :D], dkv_v[:, D:]], axis=1)
    gW["w_xq"], = _mm("dw_xq", [[(hn2, dqx, "tn")]], [], _ident, D, D, 256, D, [BF])
    dmemn, = _mm("proj_xkv_bwd", [[(dkv, W["w_xkvT"], "nn")]], [], _ident, N_MEM, D, N_MEM, 512, [F32])
    gW["w_xkvT"], = _mm("dw_xkv", [[(dkv, memn, "tn")]], [], _ident, 2 * D, D, 512, D, [BF])
    tok = push("xattn", gW)
    gs["g_mem"], = _rw_bwd("rms_mem_bwd", _rms_fn, [(mem, D, 0, False)], [(sp["g_mem"], D, 0, False)], [(dmemn, D, 0, False)],
                           N_MEM, N_MEM, 1, [None], [True])
    dh1, dh1b, gs["g_xattn"] = _mm("proj_xq_bwd", [[(dqx, W["w_xq"], "nt")]], [h1, dh2], _rms_bwd_epi, T, D, tm, D, [F32, BF],
                                   params=[tie(sp["g_xattn"], tok)], n_acc=1)

    dmix, = _mm("proj_out_bwd", [[(dh1b, W["w_out"], "nt")]], [], _ident, T, D, tm2, D, [F32])
    gW["w_out"] = jnp.concatenate(_mm("dw_out", [[(ret, dh1b, "tn")], [(fox, dh1b, "tn")]], [], _each, 4 * LANES, D, 256, D,
                                      [BF, BF]), axis=0)
    tok = push("out", gW)
    doat, dl = _fox_cotangent(dmix, fox, T, tm4)
    dqn, dkn, dfv, dfc4, drc4 = _fox_backward(qat, ka, kat, va, doat, lse + tok[0:1, 0:1], dl, T, tq, tq)
    dfq, dfk, gq2, gk2 = _rw_bwd("fox_prep_bwd", _fox_prep_fn, [(P, LANES, 16, True), (P, LANES, 20, True)],
                                 [(g_fq2, LANES, 0, False), (g_fk2, LANES, 0, False)],
                                 [(dqn, LANES, 0, True), (dkn, LANES, 0, True)], T, tm4, 4, [BF, BF], [True, True])
    gs["g_fox_q"] = gq2[:, :HEAD] + gq2[:, HEAD:]
    gs["g_fox_k"] = gk2[:, :HEAD] + gk2[:, HEAD:]
    dff, dbp = _fgate_bwd_col(ffp, bpad, dfc4, drc4, T)
    gs["b_forget"] = dbp[:, :8]
    drq, drk, drv, drg, dg_ret = _ret_bwd(P, cos4, sin4, g_ret, lg, s0, dmix, T, tb)
    gs["g_ret_out"] = dg_ret
    dsecs = [drq, drk, drv, drg, dfq, dfk, dfv]
    g_secs = list(_mm("dw_in", [[(d, hn1, "tn")] for d in dsecs], [], _each, 512, D, LANES, D, [BF] * len(dsecs)))
    g_ff, = _mm("dw_in_ff", [[(dff, hn1, "tn")]], [], _ident, LANES, D, LANES, D, [BF])
    gW["w_inT"] = jnp.concatenate(g_secs + [g_ff[:8]], axis=0)
    tok = push("in", gW)
    grad_x, gs["g_mix"] = _mm("proj_in_bwd", [[(d, w, "nn") for d, w in zip(dsecs, w_secs)] + [(dff, w_ffT, "nn")]], [x, dh1],
                              _rms_bwd_first_epi, T, D, tm, D, [F32], params=[tie(sp["g_mix"], tok)], n_acc=1)
    return grad_x, push_small(gs, loss_part)


_CANON = {"w_in": "w_inT", "w_xkv": "w_xkvT", "w_gate": "w_gateT", "w_up": "w_upT"}
_SMALL = (("g_mix", 0, 0, 1024), ("g_xattn", 1, 0, 1024), ("g_mem", 2, 0, 1024), ("g_ffn", 3, 0, 1024),
          ("g_ret_out", 4, 0, 512), ("g_xq", 4, 512, 256), ("g_xk", 4, 768, 256),
          ("g_fox_q", 5, 0, 64), ("g_fox_k", 5, 64, 64), ("b_forget", 5, 128, 8))
_LOSS_AT = (5, 256)


def _pack_small(tree):
    buf = jnp.zeros((SMALL_ROWS, D), F32)
    for name, r, c, n in _SMALL:
        buf = lax.dynamic_update_slice(buf, tree[name].reshape(1, n).astype(F32), (r, c))
    return buf


def _unpack_small(buf, like):
    return {name: buf[r:r + 1, c:c + n].reshape(like[name].shape) for name, r, c, n in _SMALL}


def _canonical(tree, name):
    a = tree[name][0]
    return a.T if W_SHARD[name][1] else a


def _from_canonical(a, name):
    return (a.T if W_SHARD[name][1] else a)[None]


def kernel(x, mem, g_mix, w_in, b_forget, g_ret_out, g_fox_q, g_fox_k, w_out, g_xattn, w_xq, w_xkv, g_mem, g_xq, g_xk, w_xo, g_ffn, w_gate, w_up, w_down, loss_target, m_g_mix, m_w_in, m_b_forget, m_g_ret_out, m_g_fox_q, m_g_fox_k, m_w_out, m_g_xattn, m_w_xq, m_w_xkv, m_g_mem, m_g_xq, m_g_xk, m_w_xo, m_g_ffn, m_w_gate, m_w_up, m_w_down, v_g_mix, v_w_in, v_b_forget, v_g_ret_out, v_g_fox_q, v_g_fox_k, v_w_out, v_g_xattn, v_w_xq, v_w_xkv, v_g_mem, v_g_xq, v_g_xk, v_w_xo, v_g_ffn, v_w_gate, v_w_up, v_w_down):
    names = ("g_mix", "w_in", "b_forget", "g_ret_out", "g_fox_q", "g_fox_k", "w_out", "g_xattn", "w_xq", "w_xkv", "g_mem",
             "g_xq", "g_xk", "w_xo", "g_ffn", "w_gate", "w_up", "w_down")
    w = dict(zip(names, (g_mix, w_in, b_forget, g_ret_out, g_fox_q, g_fox_k, w_out, g_xattn, w_xq, w_xkv, g_mem, g_xq, g_xk,
                         w_xo, g_ffn, w_gate, w_up, w_down)))
    m = dict(zip(names, (m_g_mix, m_w_in, m_b_forget, m_g_ret_out, m_g_fox_q, m_g_fox_k, m_w_out, m_g_xattn, m_w_xq, m_w_xkv,
                         m_g_mem, m_g_xq, m_g_xk, m_w_xo, m_g_ffn, m_w_gate, m_w_up, m_w_down)))
    v = dict(zip(names, (v_g_mix, v_w_in, v_b_forget, v_g_ret_out, v_g_fox_q, v_g_fox_k, v_w_out, v_g_xattn, v_w_xq, v_w_xkv,
                         v_g_mem, v_g_xq, v_g_xk, v_w_xo, v_g_ffn, v_w_gate, v_w_up, v_w_down)))
    small_names = [s[0] for s in _SMALL]
    me = 4 * lax.axis_index("x") + 2 * lax.axis_index("y") + lax.axis_index("c")
    me1 = me.astype(jnp.int32).reshape(1)

    first = _all_gather(_canonical(w, "w_in").astype(BF))
    first, rest = lax.optimization_barrier((first, [_canonical(w, n).astype(BF) for n in GATHER_REST]))
    rest_started = _exchange_start("gather_rest_start", rest, [lax.empty((N_DEV,) + a.shape, BF) for a in rest], scatter=False)

    def fetch_rest(after):
        srcs, lands = _exchange_wait("gather_rest_wait", rest_started, after, scatter=False)
        lands = [lax.dynamic_update_index_in_dim(a, own, me, axis=0) for a, own in zip(lands, srcs)]
        return {_CANON.get(n, n): a.reshape(N_DEV * a.shape[1], D) for n, a in zip(GATHER_REST, lands)}

    pushed = {}

    def push(group, grads):
        srcs = [grads[_CANON.get(n, n)].reshape(N_DEV, W_SHARD[n][0], D) for n in SCATTER_GROUPS[group]]
        pushed[group] = _exchange_start("scatter_%s_start" % group, srcs, [lax.empty(a.shape, BF) for a in srcs], scatter=True)
        return pushed[group][4]

    def push_small(gs, loss_part):
        small = lax.dynamic_update_slice(_pack_small(gs), loss_part[:, :1], _LOSS_AT)
        pushed["small"] = _exchange_start("gather_small_start", [small], [jnp.broadcast_to(small[None], (N_DEV,) + small.shape)],
                                          scatter=False)
        return pushed["small"][4]

    sp = {n: w[n].reshape(1, -1) for n in small_names}
    grad_x, done = _local_step(x[0], mem[0], loss_target[0], sp, first.reshape(N_DEV * W_SHARD["w_in"][0], D),
                               rest_started[4], fetch_rest, push, push_small)

    results, after = {}, done
    for group in ("ffn", "xattn", "out", "small", "in"):
        if group == "small":
            recv_small = _exchange_wait("gather_small_wait", pushed["small"], after, scatter=False)[1][0]
            g_sm, d_sm, m_sm, v_sm = _adamw("adamw_small", me1, recv_small, None, _pack_small(w), _pack_small(m), _pack_small(v))
            after = g_sm
            continue
        sents, recvs = _exchange_wait("scatter_%s_wait" % group, pushed[group], after, scatter=True)
        for name, sent, recv in zip(SCATTER_GROUPS[group], sents, recvs):
            res = _adamw("adamw_" + name, me1, recv, sent, *(_canonical(t, name) for t in (w, m, v)))
            results[name] = [_from_canonical(r, name) for r in res]
        after = results[SCATTER_GROUPS[group][-1]][0]
    loss = g_sm[_LOSS_AT[0], _LOSS_AT[1]]

    outs = []
    for k, sm in enumerate((g_sm, d_sm, m_sm, v_sm)):
        tree = _unpack_small(sm, w)
        tree.update({name: res[k] for name, res in results.items()})
        outs += [tree[n] for n in names]
    return (loss, grad_x[None], *outs)
```

```python
import jax
import jax.numpy as jnp
from jax import lax
from jax.experimental import pallas as pl
from jax.experimental.pallas import tpu as pltpu

F32 = jnp.float32
BF = jnp.bfloat16

D = 1024
HEAD = 64
CHUNK = 64
N_MEM = 256
XHEAD = 256
D_FF = 2816
EPS = 1e-6
NEG = -1e30
LANES = 128
N_DEV = 8
V7X_VMEM_BYTES = 64 * 1024 * 1024
VMEM_LIMIT = V7X_VMEM_BYTES - 8 * 1024 * 1024

ADAM_LR, ADAM_B1, ADAM_B2, ADAM_EPS, ADAM_WD, ADAM_STEP = 0.001, 0.9, 0.999, 1e-08, 0.01, 10

W_SHARD = {"w_in": (449, True), "w_out": (128, False), "w_xq": (128, False), "w_xkv": (256, True),
           "w_xo": (128, False), "w_gate": (352, True), "w_up": (352, True), "w_down": (352, False)}
GATHER_REST = ("w_out", "w_xq", "w_xkv", "w_xo", "w_gate", "w_up", "w_down")
SCATTER_GROUPS = {"ffn": ("w_gate", "w_up", "w_down"), "xattn": ("w_xq", "w_xo", "w_xkv"), "out": ("w_out",), "in": ("w_in",)}
SMALL_ROWS = 8

NT = (((1,), (1,)), ((), ()))
NN = (((1,), (0,)), ((), ()))
TN = (((0,), (0,)), ((), ()))
_DIMS = {"nn": NN, "nt": NT, "tn": TN}


def _params(sem):
    return pltpu.CompilerParams(dimension_semantics=sem, vmem_limit_bytes=VMEM_LIMIT)


def _mm(name, products, extras, epilogue, M, N, tm, tn, out_dtypes, params=(), n_acc=0):
    assert n_acc == 0 or tn == N
    flat = [t for p in products for t in p]
    counts = [len(p) for p in products]
    in_specs, args, where, slots = [], [], {}, []

    def operand(arr, spec, kind):
        key = (id(arr), kind)
        if key not in where:
            where[key] = len(args)
            args.append(arr)
            in_specs.append(spec)
        return where[key]

    for a, b, form in flat:
        if form == "tn":
            ia = operand(a, pl.BlockSpec((a.shape[0], tm), lambda i, j: (0, i)), "a_tn")
        else:
            ia = operand(a, pl.BlockSpec((tm, a.shape[1]), lambda i, j: (i, 0)), "a")
        if form == "nt":
            ib = operand(b, pl.BlockSpec((tn, b.shape[1]), lambda i, j: (j, 0)), "b_nt")
        else:
            ib = operand(b, pl.BlockSpec((b.shape[0], tn), lambda i, j: (0, j)), "b")
        slots.append((ia, ib))
    n_mm = len(args)
    for e in extras:
        in_specs.append(pl.BlockSpec((tm, tn), lambda i, j: (i, j)))
        args.append(e)
    for p in params:
        in_specs.append(pl.BlockSpec((1, tn), lambda i, j: (0, j)))
        args.append(p)
    n_in = len(args)
    n_out = len(out_dtypes)

    def body(*refs):
        ins, outs = refs[:n_in], refs[n_in:]
        prods, p = [], 0
        for c in counts:
            acc = None
            for _ in range(c):
                a = ins[slots[p][0]][...].astype(BF)
                b = ins[slots[p][1]][...].astype(BF)
                d = lax.dot_general(a, b, _DIMS[flat[p][2]], preferred_element_type=F32)
                acc = d if acc is None else acc + d
                p += 1
            prods.append(acc)
        ex = [r[...].astype(F32) for r in ins[n_mm:]]
        res = epilogue(*prods, *ex)
        for o, r in zip(outs[:n_out], res[:n_out]):
            o[...] = r.astype(o.dtype)
        for o, r in zip(outs[n_out:], res[n_out:]):
            @pl.when(pl.program_id(0) == 0)
            def _(o=o):
                o[...] = jnp.zeros(o.shape, F32)
            o[...] += r

    return pl.pallas_call(
        body, name=name, grid=(M // tm, N // tn), in_specs=in_specs,
        out_specs=[pl.BlockSpec((tm, tn), lambda i, j: (i, j)) for _ in out_dtypes]
        + [pl.BlockSpec((1, tn), lambda i, j: (0, j)) for _ in range(n_acc)],
        out_shape=[jax.ShapeDtypeStruct((M, N), dt) for dt in out_dtypes] + [jax.ShapeDtypeStruct((1, N), F32)] * n_acc,
        compiler_params=_params(("arbitrary", "arbitrary")),
    )(*args)


def _ident(x):
    return (x,)


def _each(*xs):
    return xs


def _spec(rows, w, off, per_j):
    if per_j:
        return pl.BlockSpec((rows, w), lambda j, i: (i, off + j))
    return pl.BlockSpec((rows, w), lambda j, i: (i, off))


def _pspec(rows, w, off, per_j):
    if per_j:
        return pl.BlockSpec((rows, w), lambda j, i: (0, off + j))
    return pl.BlockSpec((rows, w), lambda j, i: (0, off))


def _rw_fwd(name, fn, rows, params, outs, T, tm, nj, n_acc=0):
    in_specs = [_spec(tm, w, off, pj) for _, w, off, pj in rows] + [_pspec(a.shape[0], w, off, pj) for a, w, off, pj in params]
    args = [r[0] for r in rows] + [p[0] for p in params]
    n_in, n_out = len(args), len(outs)
    out_specs = [pl.BlockSpec((tm, w), lambda j, i: (i, j)) for _, w in outs]
    out_shape = [jax.ShapeDtypeStruct((T, nj * w), dt) for dt, w in outs]
    out_specs += [pl.BlockSpec((1, LANES), lambda j, i: (0, 0)) for _ in range(n_acc)]
    out_shape += [jax.ShapeDtypeStruct((1, LANES), F32) for _ in range(n_acc)]

    def body(*refs):
        vals = [r[...].astype(F32) for r in refs[:n_in]]
        res = fn(*vals)
        orefs = refs[n_in:]
        for k in range(n_out):
            orefs[k][...] = res[k].astype(orefs[k].dtype)
        first = (pl.program_id(0) == 0) & (pl.program_id(1) == 0)
        for k in range(n_acc):
            @pl.when(first)
            def _(k=k):
                orefs[n_out + k][...] = jnp.zeros((1, LANES), F32)
            orefs[n_out + k][...] += res[n_out + k]

    return pl.pallas_call(
        body, name=name, grid=(nj, T // tm), in_specs=in_specs, out_specs=out_specs, out_shape=out_shape,
        compiler_params=_params(("arbitrary", "arbitrary")),
    )(*args)


def _rw_bwd(name, fn, rows, params, cots, T, tm, nj, row_grads, param_grads, resid=None):
    in_specs = ([_spec(tm, w, off, pj) for _, w, off, pj in rows] + [_pspec(a.shape[0], w, off, pj) for a, w, off, pj in params]
                + [_spec(tm, w, off, pj) for _, w, off, pj in cots])
    args = [r[0] for r in rows] + [p[0] for p in params] + [c[0] for c in cots]
    if resid is not None:
        in_specs.append(_spec(tm, rows[0][1], rows[0][2], rows[0][3]))
        args.append(resid)
    nr, npar, nc = len(rows), len(params), len(cots)
    out_specs, out_shape, kinds = [], [], []
    for k, dts in enumerate(row_grads):
        for dt in (dts if isinstance(dts, (list, tuple)) else [dts]):
            if dt is not None:
                w = rows[k][1]
                out_specs.append(pl.BlockSpec((tm, w), lambda j, i: (i, j)))
                out_shape.append(jax.ShapeDtypeStruct((T, nj * w), dt))
                kinds.append(("row", k))
    for k, need in enumerate(param_grads):
        if need:
            a, w, off, pj = params[k]
            out_specs.append(_pspec(a.shape[0], w, off, pj))
            out_shape.append(jax.ShapeDtypeStruct(a.shape, F32))
            kinds.append(("par", k))

    def body(*refs):
        vals = [r[...].astype(F32) for r in refs[:nr + npar]]
        ct = tuple(r[...].astype(F32) for r in refs[nr + npar:nr + npar + nc])
        _, vjp = jax.vjp(lambda *a: tuple(fn(*a)), *vals)
        grads = list(vjp(ct))
        n_in = nr + npar + nc + (resid is not None)
        if resid is not None:
            grads[0] = grads[0] + refs[n_in - 1][...].astype(F32)
        orefs = refs[n_in:]
        j, i = pl.program_id(0), pl.program_id(1)
        for o, (kind, k) in zip(orefs, kinds):
            if kind == "row":
                o[...] = grads[k].astype(o.dtype)
            else:
                first = (i == 0) if params[k][3] else ((i == 0) & (j == 0))

                @pl.when(first)
                def _(o=o):
                    o[...] = jnp.zeros(o.shape, F32)
                o[...] += grads[nr + k]

    return pl.pallas_call(
        body, name=name, grid=(nj, T // tm), in_specs=in_specs, out_specs=out_specs, out_shape=out_shape,
        compiler_params=_params(("arbitrary", "arbitrary")),
    )(*args)


def _rms(x, g):
    return x * lax.rsqrt(jnp.mean(x * x, axis=-1, keepdims=True) + EPS) * g


def _rms_fn(x, g):
    return (_rms(x, g),)


def _lo_mask():
    return lax.broadcasted_iota(jnp.int32, (1, LANES), 1) < HEAD


def _gmean(x, lo):
    s0 = jnp.sum(jnp.where(lo, x, 0.0), axis=-1, keepdims=True)
    s1 = jnp.sum(jnp.where(lo, 0.0, x), axis=-1, keepdims=True)
    return jnp.where(lo, s0, s1) * (1.0 / HEAD)


def _fox_prep_fn(fq, fk, gq, gk):
    lo = _lo_mask()
    qn = fq * lax.rsqrt(_gmean(fq * fq, lo) + EPS) * gq * (HEAD ** -0.5)
    kn = fk * lax.rsqrt(_gmean(fk * fk, lo) + EPS) * gk
    return qn, kn


@jax.custom_vjp
def _swap_halves(x):
    bit = (lax.broadcasted_iota(jnp.int32, (1, LANES), 1) & (HEAD // 2)) == 0
    return jnp.where(bit, pltpu.roll(x, LANES - HEAD // 2, 1), pltpu.roll(x, HEAD // 2, 1))


_swap_halves.defvjp(lambda x: (_swap_halves(x), None), lambda _, g: (_swap_halves(g),))


def _ret_fn(rq, rk, rv, rg, cos, sin, s_in, g, lg):
    tb = rq.shape[0]
    nc = tb // CHUNK
    lo = _lo_mask()
    row = lax.broadcasted_iota(jnp.int32, (LANES, 1), 0) < HEAD
    same_head = row == lo
    q = (rq * cos + _swap_halves(rq) * sin) * (HEAD ** -0.5)
    k = rk * cos + _swap_halves(rk) * sin
    q3, k3, v3 = q.reshape(nc, CHUNK, LANES), k.reshape(nc, CHUNK, LANES), rv.reshape(nc, CHUNK, LANES)
    pos = lax.broadcasted_iota(jnp.int32, (CHUNK, 1), 0).astype(F32)
    q_decay = jnp.exp(lg * (pos + 1.0))
    k_decay = jnp.exp(lg * (CHUNK - 1.0 - pos))
    chunk_decay = jnp.exp(lg * float(CHUNK))
    dist = jnp.abs(lax.broadcasted_iota(jnp.int32, (CHUNK, CHUNK), 0) - lax.broadcasted_iota(jnp.int32, (CHUNK, CHUNK), 1)).astype(F32)
    v3b = v3.astype(BF)
    intra = []
    for hh in range(2):
        hm = lo if hh == 0 else ~lo
        lg_h = lg[:, hh * HEAD:hh * HEAD + 1]
        qm = jnp.where(hm, q3, 0.0).astype(BF)
        sc = jnp.einsum("nid,njd->nij", qm, k3.astype(BF), preferred_element_type=F32) * jnp.exp(lg_h * dist)[None]
        intra.append(jnp.einsum("nij,nje->nie", sc.astype(BF), v3b, preferred_element_type=F32))
    o = jnp.where(lo, intra[0], intra[1])
    kv = jnp.einsum("njd,nje->nde", (k3 * k_decay[None]).astype(BF), v3b, preferred_element_type=F32)
    kv = jnp.where(same_head[None], kv, 0.0)
    state, states = s_in, []
    for n in range(nc):
        states.append(state)
        state = state * chunk_decay + kv[n]
    s_prev = jnp.stack(states, axis=0)
    o = o + jnp.einsum("nid,nde->nie", (q3 * q_decay[None]).astype(BF), s_prev.astype(BF), preferred_element_type=F32)
    o = o.reshape(tb, LANES)
    mu = _gmean(o, lo)
    oc = o - mu
    y = oc * lax.rsqrt(_gmean(oc * oc, lo) + EPS) * g
    return jax.nn.silu(rg) * y, state


def _xattn_fn(qx, gq, gk, kk, vv):
    q = _rms(qx, gq)
    k = _rms(kk, gk)
    logits = lax.dot_general(q.astype(BF), k.astype(BF), NT, preferred_element_type=F32) * (XHEAD ** -0.5)
    p = jax.nn.softmax(logits, axis=-1)
    return (jnp.dot(p.astype(BF), vv.astype(BF), preferred_element_type=F32),)


def _swiglu_fwd_epi(g, u):
    return g, u, jax.nn.silu(g) * u


def _swiglu_bwd_epi(dact, g, u):
    _, vjp = jax.vjp(lambda a, b: jax.nn.silu(a) * b, g, u)
    return vjp(dact)


def _add_rms_epi(acc, resid, g):
    h = acc + resid
    return h, _rms(h, g)


def _add_loss_epi(acc, resid, target):
    err = (acc + resid) - target
    dy = err * (1.0 / D)
    part = jnp.sum(jnp.sum(err * err, axis=0, keepdims=True), axis=1, keepdims=True) * (0.5 / D)
    return dy, dy, jnp.broadcast_to(part, (1, err.shape[1]))


def _rms_bwd_epi(dhn, h, skip, g):
    _, vjp = jax.vjp(_rms, h, g)
    dh, dg = vjp(dhn)
    dh = dh + skip
    return dh, dh, dg


def _rms_bwd_first_epi(dhn, h, skip, g):
    return _rms_bwd_epi(dhn, h, skip, g)[1:]


def _ret_fwd(P, cos, sin, g_ret, lg, T, tb):
    nb = T // tb

    def body(rq, rk, rv, rg, c, s, g, l, o_ref, s0_ref, state):
        @pl.when(pl.program_id(1) == 0)
        def _():
            state[...] = jnp.zeros(state.shape, F32)
        s0_ref[0, 0] = state[...]
        out, s_new = _ret_fn(rq[...], rk[...], rv[...], rg[...], c[...], s[...], state[...], g[...], l[...])
        o_ref[...] = out.astype(o_ref.dtype)
        state[...] = s_new

    sec = lambda off: pl.BlockSpec((tb, LANES), lambda j, i: (i, off + j))
    tab = pl.BlockSpec((tb, LANES), lambda j, i: (i, 0))
    par = pl.BlockSpec((1, LANES), lambda j, i: (0, j))
    return pl.pallas_call(
        body, name="ret_fwd", grid=(4, nb),
        in_specs=[sec(0), sec(4), sec(8), sec(12), tab, tab, par, par],
        out_specs=[pl.BlockSpec((tb, LANES), lambda j, i: (i, j)), pl.BlockSpec((1, 1, LANES, LANES), lambda j, i: (j, i, 0, 0))],
        out_shape=[jax.ShapeDtypeStruct((T, 4 * LANES), BF), jax.ShapeDtypeStruct((4, nb, LANES, LANES), F32)],
        scratch_shapes=[pltpu.VMEM((LANES, LANES), F32)],
        compiler_params=_params(("arbitrary", "arbitrary")),
    )(P, P, P, P, cos, sin, g_ret, lg)


def _ret_bwd(P, cos, sin, g_ret, lg, s0, dmix, T, tb):
    nb = T // tb

    def body(rq, rk, rv, rg, c, s, g, l, s0_ref, do, drq, drk, drv, drg, dg, dstate):
        i = pl.program_id(1)

        @pl.when(i == 0)
        def _():
            dstate[...] = jnp.zeros(dstate.shape, F32)
            dg[...] = jnp.zeros(dg.shape, F32)

        cc, ss, ll = c[...], s[...], l[...]
        _, vjp = jax.vjp(lambda a, b, v, gate, st, gg: _ret_fn(a, b, v, gate, cc, ss, st, gg, ll),
                         rq[...], rk[...], rv[...], rg[...], s0_ref[0, 0], g[...])
        ga, gb, gv, ggate, gst, ggain = vjp((do[...], dstate[...]))
        drq[...] = ga.astype(drq.dtype)
        drk[...] = gb.astype(drk.dtype)
        drv[...] = gv.astype(drv.dtype)
        drg[...] = ggate.astype(drg.dtype)
        dstate[...] = gst
        dg[...] += ggain

    rev = lambda i: nb - 1 - i
    sec = lambda off: pl.BlockSpec((tb, LANES), lambda j, i: (rev(i), off + j))
    tab = pl.BlockSpec((tb, LANES), lambda j, i: (rev(i), 0))
    par = pl.BlockSpec((1, LANES), lambda j, i: (0, j))
    outb = pl.BlockSpec((tb, LANES), lambda j, i: (rev(i), j))
    return pl.pallas_call(
        body, name="ret_bwd", grid=(4, nb),
        in_specs=[sec(0), sec(4), sec(8), sec(12), tab, tab, par, par,
                  pl.BlockSpec((1, 1, LANES, LANES), lambda j, i: (j, rev(i), 0, 0)), outb],
        out_specs=[outb, outb, outb, outb, par],
        out_shape=[jax.ShapeDtypeStruct((T, 4 * LANES), BF)] * 4 + [jax.ShapeDtypeStruct((1, 4 * LANES), F32)],
        scratch_shapes=[pltpu.VMEM((LANES, LANES), F32)],
        compiler_params=_params(("arbitrary", "arbitrary")),
    )(P, P, P, P, cos, sin, g_ret, lg, s0, dmix)


_FB = 128


def _tri(lower):
    r = lax.broadcasted_iota(jnp.int32, (_FB, _FB), 0)
    c = lax.broadcasted_iota(jnp.int32, (_FB, _FB), 1)
    return ((r >= c) if lower else (r <= c)).astype(F32)


def _fgate_fwd(ffp, bpad, T):
    def body(ff_ref, b_ref, fc_ref, fr_ref):
        lane = lax.broadcasted_iota(jnp.int32, (1, LANES), 1)
        tri = _tri(True)
        carry = jnp.zeros((1, LANES), F32)
        for blk in range(T // _FB):
            z = ff_ref[blk * _FB:(blk + 1) * _FB, :] + b_ref[...]
            lf = jnp.where(lane < 8, jax.nn.log_sigmoid(z), 0.0)
            f = jnp.dot(tri, lf, precision=lax.Precision.HIGHEST, preferred_element_type=F32) + carry
            carry = f[_FB - 1:_FB, :]
            fc_ref[blk * _FB:(blk + 1) * _FB, :] = f
            fr_ref[:, blk * _FB:(blk + 1) * _FB] = f.T[:8, :]

    return pl.pallas_call(
        body, name="fgate_fwd",
        out_shape=[jax.ShapeDtypeStruct((T, LANES), F32), jax.ShapeDtypeStruct((8, T), F32)],
        compiler_params=pltpu.CompilerParams(vmem_limit_bytes=VMEM_LIMIT),
    )(ffp, bpad)


_BIAS_LANE = HEAD


def _head_bias_col(fc, head):
    lane = lax.broadcasted_iota(jnp.int32, (1, LANES), 1)
    return jnp.sum(jnp.where(lane == head, fc, 0.0), axis=-1, keepdims=True)


def _split3(f):
    hi = f.astype(BF).astype(F32)
    mid = (f - hi).astype(BF).astype(F32)
    lo = ((f - hi) - mid).astype(BF).astype(F32)
    return hi, mid, lo


def _fox_operands(P, fc, g_fq2, g_fk2, T, tm):
    def body(fq_ref, fk_ref, fv_ref, fc_ref, gq_ref, gk_ref, qat_ref, ka_ref, kat_ref, va_ref, vat_ref):
        j = pl.program_id(0)
        lane = lax.broadcasted_iota(jnp.int32, (1, LANES), 1)
        qn, kn = _fox_prep_fn(fq_ref[...], fk_ref[...], gq_ref[...], gk_ref[...])
        v = fv_ref[...]
        fcb = fc_ref[...]
        b = _BIAS_LANE
        for hh in range(2):
            hi, mid, lo = _split3(_head_bias_col(fcb, 2 * j + hh))
            take = (lambda a: a) if hh == 0 else (lambda a: pltpu.roll(a, HEAD, 1))
            qa = jnp.where(lane < HEAD, take(qn), jnp.where(lane == b, hi, jnp.where(lane == b + 1, mid, jnp.where(
                lane == b + 2, lo, jnp.where(lane < b + 6, 1.0, 0.0)))))
            ka = jnp.where(lane < HEAD, take(kn), jnp.where(lane < b + 3, 1.0, jnp.where(lane == b + 3, -hi, jnp.where(
                lane == b + 4, -mid, jnp.where(lane == b + 5, -lo, 0.0)))))
            va = jnp.where(lane < HEAD, take(v), 0.0)
            qat_ref[hh] = qa.T.astype(BF)
            for val, ref, tref in ((ka, ka_ref, kat_ref), (va, va_ref, vat_ref)):
                ref[hh] = val.astype(BF)
                tref[hh] = val.T.astype(BF)

    sec = lambda off: pl.BlockSpec((tm, LANES), lambda j, i: (i, off + j))
    par = pl.BlockSpec((1, LANES), lambda j, i: (0, 0))
    nat = pl.BlockSpec((2, tm, LANES), lambda j, i: (j, i, 0))
    trn = pl.BlockSpec((2, LANES, tm), lambda j, i: (j, 0, i))
    return pl.pallas_call(
        body, name="fox_operands", grid=(4, T // tm),
        in_specs=[sec(16), sec(20), sec(24), pl.BlockSpec((tm, LANES), lambda j, i: (i, 0)), par, par],
        out_specs=[trn, nat, trn, nat, trn],
        out_shape=[jax.ShapeDtypeStruct((8, LANES, T), BF)]
        + [jax.ShapeDtypeStruct((8, T, LANES), BF), jax.ShapeDtypeStruct((8, LANES, T), BF)] * 2,
        compiler_params=_params(("parallel", "arbitrary")),
    )(P, P, P, fc, g_fq2, g_fk2)


def _fox_forward(qat, ka, vat, T, tq, tk):
    nq, per = T // tq, tq // tk
    assert per == 2
    RC = 64

    def body(qat_ref, ka_ref, vat_ref, o_ref, lse_ref, s_scr, p_scr, a_scr, m_scr, l_scr, acc_scr):
        i = pl.program_id(1)
        sub = lax.broadcasted_iota(jnp.int32, (8, 1), 0)
        row = lax.broadcasted_iota(jnp.int32, (RC, tq), 0)
        col = lax.broadcasted_iota(jnp.int32, (RC, tq), 1)
        m_scr[...] = jnp.full(m_scr.shape, NEG, F32)
        l_scr[...] = jnp.zeros(l_scr.shape, F32)
        acc_scr[...] = jnp.zeros(acc_scr.shape, F32)

        def scores(slot, kb):
            k0 = pl.multiple_of(kb * tk, tk)
            for hh in range(2):
                s_scr[slot, hh] = jnp.dot(ka_ref[hh, pl.ds(k0, tk), :], qat_ref[hh], preferred_element_type=F32)

        def softmax(slot, kb, diagonal):
            shift = kb * tk - i * tq
            for hh in range(2):
                def masked(r):
                    tile = s_scr[slot, hh, r * RC:(r + 1) * RC, :]
                    return jnp.where(row + (r * RC + shift) <= col, tile, NEG) if diagonal else tile

                mx = jnp.max(masked(0), axis=0, keepdims=True)
                for r in range(1, tk // RC):
                    mx = jnp.maximum(mx, jnp.max(masked(r), axis=0, keepdims=True))
                m_old = m_scr[hh, 0:1, :]
                m2 = jnp.maximum(m_old, mx)
                a = jnp.exp(m_old - m2)
                lsum = jnp.zeros((1, tq), F32)
                for r in range(tk // RC):
                    p = jnp.exp(masked(r) - m2)
                    p_scr[slot, hh, r * RC:(r + 1) * RC, :] = p.astype(BF)
                    lsum = lsum + jnp.sum(p, axis=0, keepdims=True)
                m_scr[hh] = jnp.broadcast_to(m2, (8, tq))
                l_scr[hh] = jnp.broadcast_to(a * l_scr[hh, 0:1, :] + lsum, (8, tq))
                a_scr[slot, hh] = jnp.broadcast_to(a, (8, tq))

        def values(slot, kb):
            k0 = pl.multiple_of(kb * tk, tk)
            for hh in range(2):
                pv = jnp.dot(vat_ref[hh, 0:HEAD, pl.ds(k0, tk)], p_scr[slot, hh], preferred_element_type=F32)
                acc_scr[hh] = a_scr[slot, hh, 0:1, :] * acc_scr[hh] + pv

        def pair(kb, diag_first, diag_second, more):
            if more:
                scores(0, kb + 2)
            softmax(1, kb + 1, diag_first)
            values(0, kb)
            if more:
                scores(1, kb + 3)
                softmax(0, kb + 2, diag_second)
            values(1, kb + 1)

        scores(0, 0)
        scores(1, 1)
        softmax(0, 0, True)

        @pl.loop(0, jnp.maximum(i - 1, 0))
        def _(t):
            pair(2 * t, False, False, True)

        @pl.when(i >= 1)
        def _():
            pair(2 * (i - 1), False, True, True)

        pair(2 * i, True, False, False)

        o_ref[...] = jnp.concatenate([acc_scr[hh] / l_scr[hh, 0:1, :] for hh in range(2)], axis=0).T
        lses = [m_scr[hh, 0:1, :] + jnp.log(l_scr[hh, 0:1, :]) for hh in range(2)]
        lse_ref[0] = jnp.where(sub == 0, lses[0], jnp.where(sub == 1, lses[1], 0.0))

    return pl.pallas_call(
        body, name="fox_forward", grid=(4, nq),
        in_specs=[pl.BlockSpec((2, LANES, tq), lambda j, i: (j, 0, i)), pl.BlockSpec((2, T, LANES), lambda j, i: (j, 0, 0)),
                  pl.BlockSpec((2, LANES, T), lambda j, i: (j, 0, 0))],
        out_specs=[pl.BlockSpec((tq, LANES), lambda j, i: (i, j)), pl.BlockSpec((1, 8, tq), lambda j, i: (j, 0, i))],
        out_shape=[jax.ShapeDtypeStruct((T, 4 * LANES), F32), jax.ShapeDtypeStruct((4, 8, T), F32)],
        scratch_shapes=[pltpu.VMEM((2, 2, tk, tq), F32), pltpu.VMEM((2, 2, tk, tq), BF), pltpu.VMEM((2, 2, 8, tq), F32),
                        pltpu.VMEM((2, 8, tq), F32), pltpu.VMEM((2, 8, tq), F32), pltpu.VMEM((2, HEAD, tq), F32)],
        compiler_params=_params(("parallel", "arbitrary")),
    )(qat, ka, vat)


def _fox_cotangent(dmix, fox, T, tm):
    def body(do_ref, o_ref, doat_ref, dl_ref):
        lane = lax.broadcasted_iota(jnp.int32, (1, LANES), 1)
        sub = lax.broadcasted_iota(jnp.int32, (8, 1), 0)
        dob = do_ref[...].astype(BF).astype(F32)
        prod_t = (dob * o_ref[...]).T
        d0 = jnp.sum(prod_t[:HEAD], axis=0, keepdims=True)
        d1 = jnp.sum(prod_t[HEAD:], axis=0, keepdims=True)
        dl_ref[0] = jnp.where(sub == 0, d0, jnp.where(sub == 1, d1, 0.0))
        for hh in range(2):
            val = jnp.where(lane < HEAD, dob if hh == 0 else pltpu.roll(dob, HEAD, 1), 0.0)
            doat_ref[hh] = val.T.astype(BF)

    return pl.pallas_call(
        body, name="fox_cotangent", grid=(4, T // tm),
        in_specs=[pl.BlockSpec((tm, LANES), lambda j, i: (i, 4 + j)), pl.BlockSpec((tm, LANES), lambda j, i: (i, j))],
        out_specs=[pl.BlockSpec((2, LANES, tm), lambda j, i: (j, 0, i)), pl.BlockSpec((1, 8, tm), lambda j, i: (j, 0, i))],
        out_shape=[jax.ShapeDtypeStruct((8, LANES, T), BF), jax.ShapeDtypeStruct((4, 8, T), F32)],
        compiler_params=_params(("parallel", "arbitrary")),
    )(dmix, fox)


def _fox_backward(qat, ka, kat, va, doat, lse, dl, T, tq, tk):
    nq, nk = T // tq, T // tk

    def body(qat_ref, ka_ref, kat_ref, va_ref, doat_ref, lse_ref, dl_ref,
             dq_ref, dk_ref, dv_ref, df_ref, dr_ref, dqt, dkt, dvt, df_acc, sdp, pds):
        j, kb = pl.program_id(0), pl.program_id(1)
        lane = lax.broadcasted_iota(jnp.int32, (1, LANES), 1)
        first = (kb * tk) // tq

        @pl.when(kb == 0)
        def _():
            dqt[...] = jnp.zeros(dqt.shape, F32)

        dkt[...] = jnp.zeros(dkt.shape, F32)
        dvt[...] = jnp.zeros(dvt.shape, F32)
        df_acc[...] = jnp.zeros(df_acc.shape, F32)

        RC = 64
        last = nq - 1

        def products(slot, qi):
            q0 = pl.multiple_of(qi * tq, tq)
            for hh in range(2):
                sdp[slot, hh, 0] = jnp.dot(ka_ref[hh], qat_ref[hh, :, pl.ds(q0, tq)], preferred_element_type=F32)
                sdp[slot, hh, 1] = jnp.dot(va_ref[hh], doat_ref[hh, :, pl.ds(q0, tq)], preferred_element_type=F32)

        def softmax_bwd(slot, qi, diagonal, valid):
            q0 = pl.multiple_of(qi * tq, tq)
            shift = kb * tk - first * tq
            col = lax.broadcasted_iota(jnp.int32, (RC, tq), 1)
            row = lax.broadcasted_iota(jnp.int32, (RC, tq), 0)
            for hh in range(2):
                lse_row = lse_ref[0, hh:hh + 1, pl.ds(q0, tq)]
                dl_row = dl_ref[0, hh:hh + 1, pl.ds(q0, tq)]
                rsum = jnp.zeros((1, tq), F32)
                for r in range(tk // RC):
                    rows = slice(r * RC, (r + 1) * RC)
                    p = jnp.exp(sdp[slot, hh, 0, rows, :] - lse_row)
                    p = jnp.where((row + (r * RC + shift) <= col) if diagonal else valid, p, 0.0)
                    ds = p * (sdp[slot, hh, 1, rows, :] - dl_row)
                    pds[slot, hh, 0, rows, :] = p.astype(BF)
                    pds[slot, hh, 1, rows, :] = ds.astype(BF)
                    rsum = rsum + jnp.sum(ds, axis=0, keepdims=True)
                    part = ds[:, 0:LANES]
                    for c in range(1, tq // LANES):
                        part = part + ds[:, c * LANES:(c + 1) * LANES]
                    df_acc[hh, rows, :] += part
                dqt[hh, HEAD:HEAD + 8, pl.ds(q0, tq)] += jnp.broadcast_to(rsum, (8, tq))

        def accumulate(slot, qi):
            q0 = pl.multiple_of(qi * tq, tq)
            for hh in range(2):
                dvt[hh] += lax.dot_general(doat_ref[hh, 0:HEAD, pl.ds(q0, tq)], pds[slot, hh, 0], NT, preferred_element_type=F32)
                dkt[hh] += lax.dot_general(qat_ref[hh, 0:HEAD, pl.ds(q0, tq)], pds[slot, hh, 1], NT, preferred_element_type=F32)
                dqt[hh, 0:HEAD, pl.ds(q0, tq)] += jnp.dot(kat_ref[hh, 0:HEAD, :], pds[slot, hh, 1], preferred_element_type=F32)

        products(0, first)
        products(1, jnp.minimum(first + 1, last))
        softmax_bwd(0, first, True, None)

        @pl.loop(0, (nq - first + 1) // 2)
        def _(t):
            qi = first + 2 * t
            products(0, jnp.minimum(qi + 2, last))
            softmax_bwd(1, jnp.minimum(qi + 1, last), False, qi + 1 <= last)
            accumulate(0, qi)
            products(1, jnp.minimum(qi + 3, last))
            softmax_bwd(0, jnp.minimum(qi + 2, last), False, qi + 2 <= last)
            accumulate(1, jnp.minimum(qi + 1, last))

        dk_ref[...] = jnp.concatenate([dkt[0], dkt[1]], axis=0).T
        dv_ref[...] = jnp.concatenate([dvt[0], dvt[1]], axis=0).T.astype(dv_ref.dtype)
        f0 = -jnp.sum(df_acc[0], axis=1, keepdims=True)
        f1 = -jnp.sum(df_acc[1], axis=1, keepdims=True)
        df_ref[0] = jnp.where(lane == 2 * j, f0, jnp.where(lane == 2 * j + 1, f1, 0.0))

        @pl.when(kb == nk - 1)
        def _():
            for t in range(nq):
                cols = slice(t * tq, (t + 1) * tq)
                dq_ref[cols, :] = jnp.concatenate([dqt[0, 0:HEAD, cols], dqt[1, 0:HEAD, cols]], axis=0).T
                rsum = jnp.concatenate([dqt[0, HEAD:HEAD + 8, cols], dqt[1, HEAD:HEAD + 8, cols],
                                        jnp.zeros((LANES - 16, tq), F32)], axis=0).T
                dr_ref[0, cols, :] = jnp.where(lane == 2 * j, rsum[:, 0:1], jnp.where(lane == 2 * j + 1, rsum[:, 8:9], 0.0))

    trn_full = pl.BlockSpec((2, LANES, T), lambda j, kb: (j, 0, 0))
    nat_blk = pl.BlockSpec((2, tk, LANES), lambda j, kb: (j, kb, 0))
    trn_blk = pl.BlockSpec((2, LANES, tk), lambda j, kb: (j, 0, kb))
    rows = pl.BlockSpec((1, 8, T), lambda j, kb: (j, 0, 0))
    blk = pl.BlockSpec((tk, LANES), lambda j, kb: (kb, j))
    return pl.pallas_call(
        body, name="fox_backward", grid=(4, nk),
        in_specs=[trn_full, nat_blk, trn_blk, nat_blk, trn_full, rows, rows],
        out_specs=[pl.BlockSpec((T, LANES), lambda j, kb: (0, j)), blk, blk, pl.BlockSpec((1, tk, LANES), lambda j, kb: (j, kb, 0)),
                   pl.BlockSpec((1, T, LANES), lambda j, kb: (j, 0, 0))],
        out_shape=[jax.ShapeDtypeStruct((T, 4 * LANES), F32), jax.ShapeDtypeStruct((T, 4 * LANES), F32),
                   jax.ShapeDtypeStruct((T, 4 * LANES), BF), jax.ShapeDtypeStruct((4, T, LANES), F32),
                   jax.ShapeDtypeStruct((4, T, LANES), F32)],
        scratch_shapes=[pltpu.VMEM((2, HEAD + 8, T), F32), pltpu.VMEM((2, HEAD, tk), F32), pltpu.VMEM((2, HEAD, tk), F32),
                        pltpu.VMEM((2, tk, LANES), F32), pltpu.VMEM((2, 2, 2, tk, tq), F32), pltpu.VMEM((2, 2, 2, tk, tq), BF)],
        compiler_params=_params(("arbitrary", "arbitrary")),
    )(qat, ka, kat, va, doat, lse, dl)


def _fgate_bwd_col(ffp, bpad, dfc4, drc4, T):
    def body(ff_ref, b_ref, dfc_ref, drc_ref, dff_ref, db_ref):
        lane = lax.broadcasted_iota(jnp.int32, (1, LANES), 1)
        tri = _tri(False)
        carry = jnp.zeros((1, LANES), F32)
        db = jnp.zeros((1, LANES), F32)
        for blk in reversed(range(T // _FB)):
            rows = slice(blk * _FB, (blk + 1) * _FB)
            dcol = dfc_ref[0, rows, :] + drc_ref[0, rows, :]
            for pair in range(1, 4):
                dcol = dcol + (dfc_ref[pair, rows, :] + drc_ref[pair, rows, :])
            dlf = jnp.dot(tri, dcol, precision=lax.Precision.HIGHEST, preferred_element_type=F32) + carry
            carry = dlf[0:1, :]
            z = ff_ref[blk * _FB:(blk + 1) * _FB, :] + b_ref[...]
            dz = jnp.where(lane < 8, dlf * jax.nn.sigmoid(-z), 0.0)
            dff_ref[blk * _FB:(blk + 1) * _FB, :] = dz.astype(dff_ref.dtype)
            db = db + jnp.sum(dz, axis=0, keepdims=True)
        db_ref[...] = db

    return pl.pallas_call(
        body, name="fgate_bwd",
        out_shape=[jax.ShapeDtypeStruct((T, LANES), BF), jax.ShapeDtypeStruct((1, LANES), F32)],
        compiler_params=pltpu.CompilerParams(vmem_limit_bytes=VMEM_LIMIT),
    )(ffp, bpad, dfc4, drc4)


MESH = pl.DeviceIdType.MESH
N_PEERS = N_DEV - 1


def _place():
    return lax.axis_index("x"), lax.axis_index("y"), lax.axis_index("c")


def _all_gather(shard, rows, g, tm):
    R, W = shard.shape
    T = rows.shape[0]
    steps = T // tm

    def body(w_ref, rows_ref, g_ref, out_ref, norm_ref, send_sems, recv_sems, local_sem):
        x, y, c = _place()
        me, sibling = (x, y, c), (x, y, 1 - c)
        chips = [(1 - x, y), (x, 1 - y), (1 - x, 1 - y)]

        def slot(px, py, pc):
            return out_ref.at[4 * px + 2 * py + pc]

        def copy(k, block, to, src=None):
            return pltpu.make_async_remote_copy(
                src_ref=slot(*block) if src is None else src, dst_ref=slot(*block),
                send_sem=send_sems.at[k], recv_sem=recv_sems.at[k], device_id=to, device_id_type=MESH)

        mine = pltpu.make_async_copy(w_ref, slot(*me), local_sem)
        first = [copy(0, me, sibling, src=w_ref)]
        first += [copy(1 + n, me, (*chip, c), src=w_ref) for n, chip in enumerate(chips)]
        passed = [copy(4 + n, (*chip, c), sibling) for n, chip in enumerate(chips)]

        @pl.when(pl.program_id(0) == 0)
        def _():
            mine.start()
            for cp in first:
                cp.start()

        norm_ref[...] = _rms(rows_ref[...], g_ref[...]).astype(norm_ref.dtype)

        @pl.when(pl.program_id(0) == steps - 1)
        def _():
            for n, chip in enumerate(chips):
                copy(1 + n, (*chip, c), me).wait_recv()
                passed[n].start()
            copy(0, sibling, me).wait_recv()
            for n, chip in enumerate(chips):
                copy(4 + n, (*chip, 1 - c), me).wait_recv()
            for cp in first + passed:
                cp.wait_send()
            mine.wait()

    tile = pl.BlockSpec((tm, D), lambda i: (i, 0))
    return pl.pallas_call(
        body, name="all_gather_weights", grid=(steps,),
        out_shape=[jax.ShapeDtypeStruct((N_DEV, R, W), shard.dtype), jax.ShapeDtypeStruct((T, D), BF)],
        in_specs=[pl.BlockSpec(memory_space=pl.ANY), tile, pl.BlockSpec((1, D), lambda i: (0, 0))],
        out_specs=[pl.BlockSpec(memory_space=pl.ANY), tile],
        scratch_shapes=[pltpu.SemaphoreType.DMA((N_PEERS,)), pltpu.SemaphoreType.DMA((N_PEERS,)), pltpu.SemaphoreType.DMA],
        compiler_params=_params(("arbitrary",)),
    )(shard, rows, g)


def _exchange_copies(src_refs, land_refs, send_sems, recv_sems, scatter):
    x, y, c = _place()
    me = 4 * x + 2 * y + c
    copies = []
    for k, (src_ref, land_ref) in enumerate(zip(src_refs, land_refs)):
        for r in range(1, N_DEV):
            px, py, pc = x ^ (r >> 2), y ^ ((r >> 1) & 1), c ^ (r & 1)
            copies.append(pltpu.make_async_remote_copy(
                src_ref=src_ref.at[4 * px + 2 * py + pc] if scatter else src_ref, dst_ref=land_ref.at[me],
                send_sem=send_sems.at[k * N_PEERS + r - 1], recv_sem=recv_sems.at[k * N_PEERS + r - 1],
                device_id=(px, py, pc), device_id_type=MESH))
    return copies


_HBM = pl.BlockSpec(memory_space=pltpu.HBM)
_SEM = pl.BlockSpec(memory_space=pltpu.SEMAPHORE)
_EFFECT = pltpu.SideEffectType.DATAFLOW_SIDE_EFFECTING


def _exchange_start(name, srcs, lands, scatter):
    n = len(srcs)

    def body(*refs):
        send_sems, recv_sems = refs[2 * n], refs[2 * n + 1]
        for cp in _exchange_copies(refs[:n], refs[n:2 * n], send_sems, recv_sems, scatter):
            cp.start()
        token = refs[-1]
        token[...] = jnp.zeros(token.shape, F32)

    arrays = list(srcs) + list(lands)
    out = pl.pallas_call(
        body, name=name,
        out_shape=(pltpu.SemaphoreType.DMA((n * N_PEERS,)), pltpu.SemaphoreType.DMA((n * N_PEERS,)))
        + tuple(pltpu.HBM(a.shape, a.dtype) for a in arrays) + (jax.ShapeDtypeStruct((8, LANES), F32),),
        in_specs=(_HBM,) * (2 * n), out_specs=(_SEM, _SEM) + (_HBM,) * (2 * n) + (pl.BlockSpec(memory_space=pltpu.VMEM),),
        input_output_aliases={k: 2 + k for k in range(2 * n)},
        compiler_params=pltpu.CompilerParams(has_side_effects=_EFFECT),
    )(*(pltpu.with_memory_space_constraint(a, pltpu.HBM) for a in arrays))
    return out[0], out[1], out[2:2 + n], out[2 + n:2 + 2 * n], out[-1]


def _exchange_wait(name, started, after, scatter):
    send_sems, recv_sems, srcs, lands, _ = started
    n = len(srcs)

    def body(*refs):
        copies = _exchange_copies(refs[:n], refs[n:2 * n], refs[2 * n], refs[2 * n + 1], scatter)
        for cp in copies:
            cp.wait_send()
        for cp in copies:
            cp.wait_recv()

    arrays = list(srcs) + list(lands)
    out = pl.pallas_call(
        body, name=name,
        out_shape=tuple(pltpu.HBM(a.shape, a.dtype) for a in arrays),
        in_specs=(_HBM,) * (2 * n) + (_SEM, _SEM, pl.BlockSpec(memory_space=pl.ANY)), out_specs=(_HBM,) * (2 * n),
        input_output_aliases={k: k for k in range(2 * n)},
        compiler_params=pltpu.CompilerParams(has_side_effects=_EFFECT),
    )(*arrays, send_sems, recv_sems, after)
    return out[:n], out[n:]


def _adam_update(g, w, m, v):
    m2 = ADAM_B1 * m + (1.0 - ADAM_B1) * g
    v2 = ADAM_B2 * v + (1.0 - ADAM_B2) * jnp.square(g)
    m_hat = m2 / (1.0 - ADAM_B1 ** ADAM_STEP)
    v_hat = v2 / (1.0 - ADAM_B2 ** ADAM_STEP)
    return g, -ADAM_LR * (m_hat / (jnp.sqrt(v_hat) + ADAM_EPS) + ADAM_WD * w), m2, v2


def _adamw(name, me, slots, sent, w, m, v):
    R, W = w.shape
    steps = max(k for k in (4, 2, 1) if k == 1 or (R % k == 0 and (R // k) % 16 == 0))
    tr = R // steps

    def body(me_ref, s_ref, *refs):
        if sent is not None:
            g = refs[0][0].astype(F32)
            refs = refs[1:]
        else:
            g = jnp.zeros((tr, W), F32)
        for s in range(N_DEV):
            part = s_ref[s].astype(F32)
            g = g + (part if sent is None else jnp.where(me_ref[0] == s, 0.0, part))
        w_ref, m_ref, v_ref = refs[:3]
        for o, r in zip(refs[3:], _adam_update(g, w_ref[...], m_ref[...], v_ref[...])):
            o[...] = r

    rows = pl.BlockSpec((tr, W), lambda i, me_ref: (i, 0))
    in_specs = [pl.BlockSpec((N_DEV, tr, W), lambda i, me_ref: (0, i, 0))]
    args = [slots]
    if sent is not None:
        in_specs.append(pl.BlockSpec((1, tr, W), lambda i, me_ref: (me_ref[0], i, 0)))
        args.append(sent)
    return pl.pallas_call(
        body, name=name,
        grid_spec=pltpu.PrefetchScalarGridSpec(num_scalar_prefetch=1, grid=(steps,), in_specs=in_specs + [rows] * 3,
                                               out_specs=[rows] * 4),
        out_shape=[jax.ShapeDtypeStruct((R, W), F32)] * 4,
        compiler_params=_params(("arbitrary",)),
    )(me, *args, w, m, v)


def _tables(T):
    pos = jnp.arange(T, dtype=F32)
    inv_freq = 10000.0 ** (-jnp.arange(0, HEAD, 2, dtype=F32) / HEAD)
    ang = pos[:, None] * inv_freq[None, :]
    cos, sin = jnp.cos(ang), jnp.sin(ang)
    cos4 = jnp.tile(cos, (1, 4))
    sin4 = jnp.tile(jnp.concatenate([-sin, sin], axis=1), (1, 2))
    log_g = jnp.log(1.0 - 2.0 ** (-5.0 - jnp.arange(8, dtype=F32)))
    return cos4, sin4, jnp.repeat(log_g, HEAD)[None, :]


def _local_step(x, hn1, mem, target, sp, w_inT, token, fetch_rest, push, push_small):
    T = x.shape[0]
    tm = min(512, T)
    tq = min(256, T)
    tb = min(1024, T)
    cos4, sin4, lg = _tables(T)
    g_fq2 = jnp.tile(sp["g_fox_q"], (1, 2))
    g_fk2 = jnp.tile(sp["g_fox_k"], (1, 2))
    g_ret = sp["g_ret_out"].reshape(1, 8 * HEAD)
    bpad = jnp.pad(sp["b_forget"], ((0, 0), (0, LANES - 8)))
    w_secs = [w_inT[k * 512:(k + 1) * 512] for k in range(7)]
    w_ffT = jnp.pad(w_inT[3584:3592], ((0, LANES - 8), (0, 0)))
    w_mainT = w_inT[:3584]
    tie = lambda p, tok: p + tok[0:1, 0:1]
    tm2, tm4 = min(1024, T), min(2048, T)

    hn1, _ = lax.optimization_barrier((hn1, token))
    P, = _mm("proj_in", [[(hn1, w_mainT, "nt")]], [], _ident, T, 3584, tm4, 512, [F32])
    ffp, = _mm("proj_ff", [[(hn1, w_ffT, "nt")]], [], _ident, T, LANES, tm, LANES, [F32])
    ret, s0 = _ret_fwd(P, cos4, sin4, g_ret, lg, T, tb)
    fc, _ = _fgate_fwd(ffp, bpad, T)
    qat, ka, kat, va, vat = _fox_operands(P, fc, g_fq2, g_fk2, T, tm4)
    fox, lse = _fox_forward(qat, ka, vat, T, min(512, T), tq)
    W = fetch_rest(fox)
    w_out_halves = (W["w_out"][:4 * LANES], W["w_out"][4 * LANES:])
    h1, hn2 = _mm("proj_out", [[(ret, w_out_halves[0], "nn"), (fox, w_out_halves[1], "nn")]], [x], _add_rms_epi, T, D, tm2, D,
                  [F32, BF], params=[sp["g_xattn"]])

    qx, = _mm("proj_xq", [[(hn2, W["w_xq"], "nn")]], [], _ident, T, D, tm2, D, [F32])
    memn, = _rw_fwd("rms_mem", _rms_fn, [(mem, D, 0, False)], [(sp["g_mem"], D, 0, False)], [(BF, D)], N_MEM, N_MEM, 1)
    kv, = _mm("proj_xkv", [[(memn, W["w_xkvT"], "nt")]], [], _ident, N_MEM, 2 * D, N_MEM, 512, [F32])
    xa_rows = [(qx, XHEAD, 0, True)]
    xa_params = [(sp["g_xq"], XHEAD, 0, False), (sp["g_xk"], XHEAD, 0, False), (kv, XHEAD, 0, True), (kv, XHEAD, 4, True)]
    xo, = _rw_fwd("xattn_fwd", _xattn_fn, xa_rows, xa_params, [(BF, XHEAD)], T, tm4, 4)
    h2, hn3 = _mm("proj_xo", [[(xo, W["w_xo"], "nn")]], [h1], _add_rms_epi, T, D, tm2, D, [F32, BF], params=[sp["g_ffn"]])

    gate, up, act = _mm("ffn_in", [[(hn3, W["w_gateT"], "nt")], [(hn3, W["w_upT"], "nt")]], [], _swiglu_fwd_epi,
                        T, D_FF, tm4, 256, [BF, BF, BF])
    dy, dyb, loss_part = _mm("ffn_out", [[(act, W["w_down"], "nn")]], [h2, target], _add_loss_epi, T, D, tm, D, [F32, BF], n_acc=1)

    dgate, dup = _mm("ffn_out_bwd", [[(dyb, W["w_down"], "nt")]], [gate, up], _swiglu_bwd_epi, T, D_FF, tm4, 256, [BF, BF])
    gW = {}
    gW["w_gateT"], gW["w_upT"] = _mm("dw_gate_up", [[(dgate, hn3, "tn")], [(dup, hn3, "tn")]], [], _each, D_FF, D, 256, D, [BF, BF])
    gW["w_down"], = _mm("dw_down", [[(act, dyb, "tn")]], [], _ident, D_FF, D, 256, D, [BF])
    tok = push("ffn", gW)
    gs = {}
    dh2, dh2b, gs["g_ffn"] = _mm("ffn_in_bwd", [[(dgate, W["w_gateT"], "nn"), (dup, W["w_upT"], "nn")]], [h2, dy], _rms_bwd_epi,
                                 T, D, min(256, T), D, [F32, BF], params=[tie(sp["g_ffn"], tok)], n_acc=1)

    dxo, = _mm("proj_xo_bwd", [[(dh2b, W["w_xo"], "nt")]], [], _ident, T, D, tm2, D, [BF])
    gW["w_xo"], = _mm("dw_xo", [[(xo, dh2b, "tn")]], [], _ident, D, D, 256, D, [BF])
    dqx, gs["g_xq"], gs["g_xk"], dkv_k, dkv_v = _rw_bwd(
        "xattn_bwd", _xattn_fn, xa_rows, xa_params, [(dxo, XHEAD, 0, True)], T, tm4, 4, [BF], [True, True, True, True])
    dkv = jnp.concatenate([dkv_k[:, :D], dkv_v[:, D:]], axis=1)
    gW["w_xq"], = _mm("dw_xq", [[(hn2, dqx, "tn")]], [], _ident, D, D, 256, D, [BF])
    dmemn, = _mm("proj_xkv_bwd", [[(dkv, W["w_xkvT"], "nn")]], [], _ident, N_MEM, D, N_MEM, 512, [F32])
    gW["w_xkvT"], = _mm("dw_xkv", [[(dkv, memn, "tn")]], [], _ident, 2 * D, D, 512, D, [BF])
    tok = push("xattn", gW)
    gs["g_mem"], = _rw_bwd("rms_mem_bwd", _rms_fn, [(mem, D, 0, False)], [(sp["g_mem"], D, 0, False)], [(dmemn, D, 0, False)],
                           N_MEM, N_MEM, 1, [None], [True])
    dh1, dh1b, gs["g_xattn"] = _mm("proj_xq_bwd", [[(dqx, W["w_xq"], "nt")]], [h1, dh2], _rms_bwd_epi, T, D, tm, D, [F32, BF],
                                   params=[tie(sp["g_xattn"], tok)], n_acc=1)

    dmix, = _mm("proj_out_bwd", [[(dh1b, W["w_out"], "nt")]], [], _ident, T, D, tm2, D, [F32])
    gW["w_out"] = jnp.concatenate(_mm("dw_out", [[(ret, dh1b, "tn")], [(fox, dh1b, "tn")]], [], _each, 4 * LANES, D, 256, D,
                                      [BF, BF]), axis=0)
    tok = push("out", gW)
    doat, dl = _fox_cotangent(dmix, fox, T, tm4)
    dqn, dkn, dfv, dfc4, drc4 = _fox_backward(qat, ka, kat, va, doat, lse + tok[0:1, 0:1], dl, T, tq, tq)
    dfq, dfk, gq2, gk2 = _rw_bwd("fox_prep_bwd", _fox_prep_fn, [(P, LANES, 16, True), (P, LANES, 20, True)],
                                 [(g_fq2, LANES, 0, False), (g_fk2, LANES, 0, False)],
                                 [(dqn, LANES, 0, True), (dkn, LANES, 0, True)], T, tm4, 4, [BF, BF], [True, True])
    gs["g_fox_q"] = gq2[:, :HEAD] + gq2[:, HEAD:]
    gs["g_fox_k"] = gk2[:, :HEAD] + gk2[:, HEAD:]
    dff, dbp = _fgate_bwd_col(ffp, bpad, dfc4, drc4, T)
    gs["b_forget"] = dbp[:, :8]
    drq, drk, drv, drg, dg_ret = _ret_bwd(P, cos4, sin4, g_ret, lg, s0, dmix, T, tb)
    gs["g_ret_out"] = dg_ret
    dsecs = [drq, drk, drv, drg, dfq, dfk, dfv]
    g_secs = list(_mm("dw_in", [[(d, hn1, "tn")] for d in dsecs], [], _each, 512, D, LANES, D, [BF] * len(dsecs)))
    g_ff, = _mm("dw_in_ff", [[(dff, hn1, "tn")]], [], _ident, LANES, D, LANES, D, [BF])
    gW["w_inT"] = jnp.concatenate(g_secs + [g_ff[:8]], axis=0)
    tok = push("in", gW)
    grad_x, gs["g_mix"] = _mm("proj_in_bwd", [[(d, w, "nn") for d, w in zip(dsecs, w_secs)] + [(dff, w_ffT, "nn")]], [x, dh1],
                              _rms_bwd_first_epi, T, D, tm, D, [F32], params=[tie(sp["g_mix"], tok)], n_acc=1)
    return grad_x, push_small(gs, loss_part)


_CANON = {"w_in": "w_inT", "w_xkv": "w_xkvT", "w_gate": "w_gateT", "w_up": "w_upT"}
_SMALL = (("g_mix", 0, 0, 1024), ("g_xattn", 1, 0, 1024), ("g_mem", 2, 0, 1024), ("g_ffn", 3, 0, 1024),
          ("g_ret_out", 4, 0, 512), ("g_xq", 4, 512, 256), ("g_xk", 4, 768, 256),
          ("g_fox_q", 5, 0, 64), ("g_fox_k", 5, 64, 64), ("b_forget", 5, 128, 8))
_LOSS_AT = (5, 256)


def _pack_small(tree):
    buf = jnp.zeros((SMALL_ROWS, D), F32)
    for name, r, c, n in _SMALL:
        buf = lax.dynamic_update_slice(buf, tree[name].reshape(1, n).astype(F32), (r, c))
    return buf


def _unpack_small(buf, like):
    return {name: buf[r:r + 1, c:c + n].reshape(like[name].shape) for name, r, c, n in _SMALL}


def _canonical(tree, name):
    a = tree[name][0]
    return a.T if W_SHARD[name][1] else a


def _from_canonical(a, name):
    return (a.T if W_SHARD[name][1] else a)[None]


def kernel(x, mem, g_mix, w_in, b_forget, g_ret_out, g_fox_q, g_fox_k, w_out, g_xattn, w_xq, w_xkv, g_mem, g_xq, g_xk, w_xo, g_ffn, w_gate, w_up, w_down, loss_target, m_g_mix, m_w_in, m_b_forget, m_g_ret_out, m_g_fox_q, m_g_fox_k, m_w_out, m_g_xattn, m_w_xq, m_w_xkv, m_g_mem, m_g_xq, m_g_xk, m_w_xo, m_g_ffn, m_w_gate, m_w_up, m_w_down, v_g_mix, v_w_in, v_b_forget, v_g_ret_out, v_g_fox_q, v_g_fox_k, v_w_out, v_g_xattn, v_w_xq, v_w_xkv, v_g_mem, v_g_xq, v_g_xk, v_w_xo, v_g_ffn, v_w_gate, v_w_up, v_w_down):
    names = ("g_mix", "w_in", "b_forget", "g_ret_out", "g_fox_q", "g_fox_k", "w_out", "g_xattn", "w_xq", "w_xkv", "g_mem",
             "g_xq", "g_xk", "w_xo", "g_ffn", "w_gate", "w_up", "w_down")
    w = dict(zip(names, (g_mix, w_in, b_forget, g_ret_out, g_fox_q, g_fox_k, w_out, g_xattn, w_xq, w_xkv, g_mem, g_xq, g_xk,
                         w_xo, g_ffn, w_gate, w_up, w_down)))
    m = dict(zip(names, (m_g_mix, m_w_in, m_b_forget, m_g_ret_out, m_g_fox_q, m_g_fox_k, m_w_out, m_g_xattn, m_w_xq, m_w_xkv,
                         m_g_mem, m_g_xq, m_g_xk, m_w_xo, m_g_ffn, m_w_gate, m_w_up, m_w_down)))
    v = dict(zip(names, (v_g_mix, v_w_in, v_b_forget, v_g_ret_out, v_g_fox_q, v_g_fox_k, v_w_out, v_g_xattn, v_w_xq, v_w_xkv,
                         v_g_mem, v_g_xq, v_g_xk, v_w_xo, v_g_ffn, v_w_gate, v_w_up, v_w_down)))
    small_names = [s[0] for s in _SMALL]
    me = 4 * lax.axis_index("x") + 2 * lax.axis_index("y") + lax.axis_index("c")
    me1 = me.astype(jnp.int32).reshape(1)

    sp = {n: w[n].reshape(1, -1) for n in small_names}
    first, hn1 = _all_gather(_canonical(w, "w_in").astype(BF), x[0], sp["g_mix"], min(1024, x.shape[1]))
    first, rest = lax.optimization_barrier((first, [_canonical(w, n).astype(BF) for n in GATHER_REST]))
    rest_started = _exchange_start("gather_rest_start", rest, [lax.empty((N_DEV,) + a.shape, BF) for a in rest], scatter=False)

    def fetch_rest(after):
        srcs, lands = _exchange_wait("gather_rest_wait", rest_started, after, scatter=False)
        lands = [lax.dynamic_update_index_in_dim(a, own, me, axis=0) for a, own in zip(lands, srcs)]
        return {_CANON.get(n, n): a.reshape(N_DEV * a.shape[1], D) for n, a in zip(GATHER_REST, lands)}

    pushed = {}

    def push(group, grads):
        srcs = [grads[_CANON.get(n, n)].reshape(N_DEV, W_SHARD[n][0], D) for n in SCATTER_GROUPS[group]]
        pushed[group] = _exchange_start("scatter_%s_start" % group, srcs, [lax.empty(a.shape, BF) for a in srcs], scatter=True)
        return pushed[group][4]

    def push_small(gs, loss_part):
        small = lax.dynamic_update_slice(_pack_small(gs), loss_part[:, :1], _LOSS_AT)
        pushed["small"] = _exchange_start("gather_small_start", [small], [jnp.broadcast_to(small[None], (N_DEV,) + small.shape)],
                                          scatter=False)
        return pushed["small"][4]

    grad_x, done = _local_step(x[0], hn1, mem[0], loss_target[0], sp, first.reshape(N_DEV * W_SHARD["w_in"][0], D),
                               rest_started[4], fetch_rest, push, push_small)

    results, after = {}, done
    for group in ("ffn", "xattn", "out", "small", "in"):
        if group == "small":
            recv_small = _exchange_wait("gather_small_wait", pushed["small"], after, scatter=False)[1][0]
            g_sm, d_sm, m_sm, v_sm = _adamw("adamw_small", me1, recv_small, None, _pack_small(w), _pack_small(m), _pack_small(v))
            after = g_sm
            continue
        sents, recvs = _exchange_wait("scatter_%s_wait" % group, pushed[group], after, scatter=True)
        for name, sent, recv in zip(SCATTER_GROUPS[group], sents, recvs):
            res = _adamw("adamw_" + name, me1, recv, sent, *(_canonical(t, name) for t in (w, m, v)))
            results[name] = [_from_canonical(r, name) for r in res]
        after = results[SCATTER_GROUPS[group][-1]][0]
    loss = g_sm[_LOSS_AT[0], _LOSS_AT[1]]

    outs = []
    for k, sm in enumerate((g_sm, d_sm, m_sm, v_sm)):
        tree = _unpack_small(sm, w)
        tree.update({name: res[k] for name, res in results.items()})
        outs += [tree[n] for n in names]
    return (loss, grad_x[None], *outs)
```

```python
import jax
import jax.numpy as jnp
from jax import lax
from jax.experimental import pallas as pl
from jax.experimental.pallas import tpu as pltpu

F32 = jnp.float32
BF = jnp.bfloat16

D = 1024
HEAD = 64
CHUNK = 64
N_MEM = 256
XHEAD = 256
D_FF = 2816
EPS = 1e-6
NEG = -1e30
LANES = 128
N_DEV = 8
V7X_VMEM_BYTES = 64 * 1024 * 1024
VMEM_LIMIT = V7X_VMEM_BYTES - 8 * 1024 * 1024

ADAM_LR, ADAM_B1, ADAM_B2, ADAM_EPS, ADAM_WD, ADAM_STEP = 0.001, 0.9, 0.999, 1e-08, 0.01, 10

W_SHARD = {"w_in": (449, True), "w_out": (128, False), "w_xq": (128, False), "w_xkv": (256, True),
           "w_xo": (128, False), "w_gate": (352, True), "w_up": (352, True), "w_down": (352, False)}
GATHER_REST = ("w_out", "w_xq", "w_xkv", "w_xo", "w_gate", "w_up", "w_down")
SCATTER_GROUPS = {"ffn": ("w_gate", "w_up", "w_down"), "xattn": ("w_xq", "w_xo", "w_xkv"), "out": ("w_out",), "in": ("w_in",)}
SMALL_ROWS = 8

NT = (((1,), (1,)), ((), ()))
NN = (((1,), (0,)), ((), ()))
TN = (((0,), (0,)), ((), ()))
_DIMS = {"nn": NN, "nt": NT, "tn": TN}


def _params(sem):
    return pltpu.CompilerParams(dimension_semantics=sem, vmem_limit_bytes=VMEM_LIMIT)


def _mm(name, products, extras, epilogue, M, N, tm, tn, out_dtypes, params=(), n_acc=0):
    assert n_acc == 0 or tn == N
    flat = [t for p in products for t in p]
    counts = [len(p) for p in products]
    in_specs, args, where, slots = [], [], {}, []

    def operand(arr, spec, kind):
        key = (id(arr), kind)
        if key not in where:
            where[key] = len(args)
            args.append(arr)
            in_specs.append(spec)
        return where[key]

    for a, b, form in flat:
        if form == "tn":
            ia = operand(a, pl.BlockSpec((a.shape[0], tm), lambda i, j: (0, i)), "a_tn")
        else:
            ia = operand(a, pl.BlockSpec((tm, a.shape[1]), lambda i, j: (i, 0)), "a")
        if form == "nt":
            ib = operand(b, pl.BlockSpec((tn, b.shape[1]), lambda i, j: (j, 0)), "b_nt")
        else:
            ib = operand(b, pl.BlockSpec((b.shape[0], tn), lambda i, j: (0, j)), "b")
        slots.append((ia, ib))
    n_mm = len(args)
    for e in extras:
        in_specs.append(pl.BlockSpec((tm, tn), lambda i, j: (i, j)))
        args.append(e)
    for p in params:
        in_specs.append(pl.BlockSpec((1, tn), lambda i, j: (0, j)))
        args.append(p)
    n_in = len(args)
    n_out = len(out_dtypes)

    def body(*refs):
        ins, outs = refs[:n_in], refs[n_in:]
        prods, p = [], 0
        for c in counts:
            acc = None
            for _ in range(c):
                a = ins[slots[p][0]][...].astype(BF)
                b = ins[slots[p][1]][...].astype(BF)
                d = lax.dot_general(a, b, _DIMS[flat[p][2]], preferred_element_type=F32)
                acc = d if acc is None else acc + d
                p += 1
            prods.append(acc)
        ex = [r[...].astype(F32) for r in ins[n_mm:]]
        res = epilogue(*prods, *ex)
        for o, r in zip(outs[:n_out], res[:n_out]):
            o[...] = r.astype(o.dtype)
        for o, r in zip(outs[n_out:], res[n_out:]):
            @pl.when(pl.program_id(0) == 0)
            def _(o=o):
                o[...] = jnp.zeros(o.shape, F32)
            o[...] += r

    return pl.pallas_call(
        body, name=name, grid=(M // tm, N // tn), in_specs=in_specs,
        out_specs=[pl.BlockSpec((tm, tn), lambda i, j: (i, j)) for _ in out_dtypes]
        + [pl.BlockSpec((1, tn), lambda i, j: (0, j)) for _ in range(n_acc)],
        out_shape=[jax.ShapeDtypeStruct((M, N), dt) for dt in out_dtypes] + [jax.ShapeDtypeStruct((1, N), F32)] * n_acc,
        compiler_params=_params(("arbitrary", "arbitrary")),
    )(*args)


def _ident(x):
    return (x,)


def _each(*xs):
    return xs


def _spec(rows, w, off, per_j):
    if per_j:
        return pl.BlockSpec((rows, w), lambda j, i: (i, off + j))
    return pl.BlockSpec((rows, w), lambda j, i: (i, off))


def _pspec(rows, w, off, per_j):
    if per_j:
        return pl.BlockSpec((rows, w), lambda j, i: (0, off + j))
    return pl.BlockSpec((rows, w), lambda j, i: (0, off))


def _rw_fwd(name, fn, rows, params, outs, T, tm, nj, n_acc=0):
    in_specs = [_spec(tm, w, off, pj) for _, w, off, pj in rows] + [_pspec(a.shape[0], w, off, pj) for a, w, off, pj in params]
    args = [r[0] for r in rows] + [p[0] for p in params]
    n_in, n_out = len(args), len(outs)
    out_specs = [pl.BlockSpec((tm, w), lambda j, i: (i, j)) for _, w in outs]
    out_shape = [jax.ShapeDtypeStruct((T, nj * w), dt) for dt, w in outs]
    out_specs += [pl.BlockSpec((1, LANES), lambda j, i: (0, 0)) for _ in range(n_acc)]
    out_shape += [jax.ShapeDtypeStruct((1, LANES), F32) for _ in range(n_acc)]

    def body(*refs):
        vals = [r[...].astype(F32) for r in refs[:n_in]]
        res = fn(*vals)
        orefs = refs[n_in:]
        for k in range(n_out):
            orefs[k][...] = res[k].astype(orefs[k].dtype)
        first = (pl.program_id(0) == 0) & (pl.program_id(1) == 0)
        for k in range(n_acc):
            @pl.when(first)
            def _(k=k):
                orefs[n_out + k][...] = jnp.zeros((1, LANES), F32)
            orefs[n_out + k][...] += res[n_out + k]

    return pl.pallas_call(
        body, name=name, grid=(nj, T // tm), in_specs=in_specs, out_specs=out_specs, out_shape=out_shape,
        compiler_params=_params(("arbitrary", "arbitrary")),
    )(*args)


def _rw_bwd(name, fn, rows, params, cots, T, tm, nj, row_grads, param_grads, resid=None):
    in_specs = ([_spec(tm, w, off, pj) for _, w, off, pj in rows] + [_pspec(a.shape[0], w, off, pj) for a, w, off, pj in params]
                + [_spec(tm, w, off, pj) for _, w, off, pj in cots])
    args = [r[0] for r in rows] + [p[0] for p in params] + [c[0] for c in cots]
    if resid is not None:
        in_specs.append(_spec(tm, rows[0][1], rows[0][2], rows[0][3]))
        args.append(resid)
    nr, npar, nc = len(rows), len(params), len(cots)
    out_specs, out_shape, kinds = [], [], []
    for k, dts in enumerate(row_grads):
        for dt in (dts if isinstance(dts, (list, tuple)) else [dts]):
            if dt is not None:
                w = rows[k][1]
                out_specs.append(pl.BlockSpec((tm, w), lambda j, i: (i, j)))
                out_shape.append(jax.ShapeDtypeStruct((T, nj * w), dt))
                kinds.append(("row", k))
    for k, need in enumerate(param_grads):
        if need:
            a, w, off, pj = params[k]
            out_specs.append(_pspec(a.shape[0], w, off, pj))
            out_shape.append(jax.ShapeDtypeStruct(a.shape, F32))
            kinds.append(("par", k))

    def body(*refs):
        vals = [r[...].astype(F32) for r in refs[:nr + npar]]
        ct = tuple(r[...].astype(F32) for r in refs[nr + npar:nr + npar + nc])
        _, vjp = jax.vjp(lambda *a: tuple(fn(*a)), *vals)
        grads = list(vjp(ct))
        n_in = nr + npar + nc + (resid is not None)
        if resid is not None:
            grads[0] = grads[0] + refs[n_in - 1][...].astype(F32)
        orefs = refs[n_in:]
        j, i = pl.program_id(0), pl.program_id(1)
        for o, (kind, k) in zip(orefs, kinds):
            if kind == "row":
                o[...] = grads[k].astype(o.dtype)
            else:
                first = (i == 0) if params[k][3] else ((i == 0) & (j == 0))

                @pl.when(first)
                def _(o=o):
                    o[...] = jnp.zeros(o.shape, F32)
                o[...] += grads[nr + k]

    return pl.pallas_call(
        body, name=name, grid=(nj, T // tm), in_specs=in_specs, out_specs=out_specs, out_shape=out_shape,
        compiler_params=_params(("arbitrary", "arbitrary")),
    )(*args)


def _rms(x, g):
    return x * lax.rsqrt(jnp.mean(x * x, axis=-1, keepdims=True) + EPS) * g


def _rms_fn(x, g):
    return (_rms(x, g),)


def _lo_mask():
    return lax.broadcasted_iota(jnp.int32, (1, LANES), 1) < HEAD


def _gmean(x, lo):
    s0 = jnp.sum(jnp.where(lo, x, 0.0), axis=-1, keepdims=True)
    s1 = jnp.sum(jnp.where(lo, 0.0, x), axis=-1, keepdims=True)
    return jnp.where(lo, s0, s1) * (1.0 / HEAD)


def _fox_prep_fn(fq, fk, gq, gk):
    lo = _lo_mask()
    qn = fq * lax.rsqrt(_gmean(fq * fq, lo) + EPS) * gq * (HEAD ** -0.5)
    kn = fk * lax.rsqrt(_gmean(fk * fk, lo) + EPS) * gk
    return qn, kn


@jax.custom_vjp
def _swap_halves(x):
    bit = (lax.broadcasted_iota(jnp.int32, (1, LANES), 1) & (HEAD // 2)) == 0
    return jnp.where(bit, pltpu.roll(x, LANES - HEAD // 2, 1), pltpu.roll(x, HEAD // 2, 1))


_swap_halves.defvjp(lambda x: (_swap_halves(x), None), lambda _, g: (_swap_halves(g),))


def _ret_fn(rq, rk, rv, rg, cos, sin, s_in, g, lg):
    tb = rq.shape[0]
    nc = tb // CHUNK
    lo = _lo_mask()
    row = lax.broadcasted_iota(jnp.int32, (LANES, 1), 0) < HEAD
    same_head = row == lo
    q = (rq * cos + _swap_halves(rq) * sin) * (HEAD ** -0.5)
    k = rk * cos + _swap_halves(rk) * sin
    q3, k3, v3 = q.reshape(nc, CHUNK, LANES), k.reshape(nc, CHUNK, LANES), rv.reshape(nc, CHUNK, LANES)
    pos = lax.broadcasted_iota(jnp.int32, (CHUNK, 1), 0).astype(F32)
    q_decay = jnp.exp(lg * (pos + 1.0))
    k_decay = jnp.exp(lg * (CHUNK - 1.0 - pos))
    chunk_decay = jnp.exp(lg * float(CHUNK))
    dist = jnp.abs(lax.broadcasted_iota(jnp.int32, (CHUNK, CHUNK), 0) - lax.broadcasted_iota(jnp.int32, (CHUNK, CHUNK), 1)).astype(F32)
    v3b = v3.astype(BF)
    intra = []
    for hh in range(2):
        hm = lo if hh == 0 else ~lo
        lg_h = lg[:, hh * HEAD:hh * HEAD + 1]
        qm = jnp.where(hm, q3, 0.0).astype(BF)
        sc = jnp.einsum("nid,njd->nij", qm, k3.astype(BF), preferred_element_type=F32) * jnp.exp(lg_h * dist)[None]
        intra.append(jnp.einsum("nij,nje->nie", sc.astype(BF), v3b, preferred_element_type=F32))
    o = jnp.where(lo, intra[0], intra[1])
    kv = jnp.einsum("njd,nje->nde", (k3 * k_decay[None]).astype(BF), v3b, preferred_element_type=F32)
    kv = jnp.where(same_head[None], kv, 0.0)
    state, states = s_in, []
    for n in range(nc):
        states.append(state)
        state = state * chunk_decay + kv[n]
    s_prev = jnp.stack(states, axis=0)
    o = o + jnp.einsum("nid,nde->nie", (q3 * q_decay[None]).astype(BF), s_prev.astype(BF), preferred_element_type=F32)
    o = o.reshape(tb, LANES)
    mu = _gmean(o, lo)
    oc = o - mu
    y = oc * lax.rsqrt(_gmean(oc * oc, lo) + EPS) * g
    return jax.nn.silu(rg) * y, state


def _xattn_fn(qx, gq, gk, kk, vv):
    q = _rms(qx, gq)
    k = _rms(kk, gk)
    logits = lax.dot_general(q.astype(BF), k.astype(BF), NT, preferred_element_type=F32) * (XHEAD ** -0.5)
    p = jax.nn.softmax(logits, axis=-1)
    return (jnp.dot(p.astype(BF), vv.astype(BF), preferred_element_type=F32),)


def _swiglu_fwd_epi(g, u):
    return g, u, jax.nn.silu(g) * u


def _swiglu_bwd_epi(dact, g, u):
    _, vjp = jax.vjp(lambda a, b: jax.nn.silu(a) * b, g, u)
    return vjp(dact)


def _add_rms_epi(acc, resid, g):
    h = acc + resid
    return h, _rms(h, g)


def _add_loss_epi(acc, resid, target):
    err = (acc + resid) - target
    dy = err * (1.0 / D)
    part = jnp.sum(jnp.sum(err * err, axis=0, keepdims=True), axis=1, keepdims=True) * (0.5 / D)
    return dy, dy, jnp.broadcast_to(part, (1, err.shape[1]))


def _rms_bwd_epi(dhn, h, skip, g):
    _, vjp = jax.vjp(_rms, h, g)
    dh, dg = vjp(dhn)
    dh = dh + skip
    return dh, dh, dg


def _rms_bwd_first_epi(dhn, h, skip, g):
    return _rms_bwd_epi(dhn, h, skip, g)[1:]


def _ret_fwd(P, cos, sin, g_ret, lg, T, tb):
    nb = T // tb

    def body(rq, rk, rv, rg, c, s, g, l, o_ref, s0_ref, state):
        @pl.when(pl.program_id(1) == 0)
        def _():
            state[...] = jnp.zeros(state.shape, F32)
        s0_ref[0, 0] = state[...]
        out, s_new = _ret_fn(rq[...], rk[...], rv[...], rg[...], c[...], s[...], state[...], g[...], l[...])
        o_ref[...] = out.astype(o_ref.dtype)
        state[...] = s_new

    sec = lambda off: pl.BlockSpec((tb, LANES), lambda j, i: (i, off + j))
    tab = pl.BlockSpec((tb, LANES), lambda j, i: (i, 0))
    par = pl.BlockSpec((1, LANES), lambda j, i: (0, j))
    return pl.pallas_call(
        body, name="ret_fwd", grid=(4, nb),
        in_specs=[sec(0), sec(4), sec(8), sec(12), tab, tab, par, par],
        out_specs=[pl.BlockSpec((tb, LANES), lambda j, i: (i, j)), pl.BlockSpec((1, 1, LANES, LANES), lambda j, i: (j, i, 0, 0))],
        out_shape=[jax.ShapeDtypeStruct((T, 4 * LANES), BF), jax.ShapeDtypeStruct((4, nb, LANES, LANES), F32)],
        scratch_shapes=[pltpu.VMEM((LANES, LANES), F32)],
        compiler_params=_params(("arbitrary", "arbitrary")),
    )(P, P, P, P, cos, sin, g_ret, lg)


def _ret_bwd(P, cos, sin, g_ret, lg, s0, dmix, T, tb):
    nb = T // tb

    def body(rq, rk, rv, rg, c, s, g, l, s0_ref, do, drq, drk, drv, drg, dg, dstate):
        i = pl.program_id(1)

        @pl.when(i == 0)
        def _():
            dstate[...] = jnp.zeros(dstate.shape, F32)
            dg[...] = jnp.zeros(dg.shape, F32)

        cc, ss, ll = c[...], s[...], l[...]
        _, vjp = jax.vjp(lambda a, b, v, gate, st, gg: _ret_fn(a, b, v, gate, cc, ss, st, gg, ll),
                         rq[...], rk[...], rv[...], rg[...], s0_ref[0, 0], g[...])
        ga, gb, gv, ggate, gst, ggain = vjp((do[...], dstate[...]))
        drq[...] = ga.astype(drq.dtype)
        drk[...] = gb.astype(drk.dtype)
        drv[...] = gv.astype(drv.dtype)
        drg[...] = ggate.astype(drg.dtype)
        dstate[...] = gst
        dg[...] += ggain

    rev = lambda i: nb - 1 - i
    sec = lambda off: pl.BlockSpec((tb, LANES), lambda j, i: (rev(i), off + j))
    tab = pl.BlockSpec((tb, LANES), lambda j, i: (rev(i), 0))
    par = pl.BlockSpec((1, LANES), lambda j, i: (0, j))
    outb = pl.BlockSpec((tb, LANES), lambda j, i: (rev(i), j))
    return pl.pallas_call(
        body, name="ret_bwd", grid=(4, nb),
        in_specs=[sec(0), sec(4), sec(8), sec(12), tab, tab, par, par,
                  pl.BlockSpec((1, 1, LANES, LANES), lambda j, i: (j, rev(i), 0, 0)), outb],
        out_specs=[outb, outb, outb, outb, par],
        out_shape=[jax.ShapeDtypeStruct((T, 4 * LANES), BF)] * 4 + [jax.ShapeDtypeStruct((1, 4 * LANES), F32)],
        scratch_shapes=[pltpu.VMEM((LANES, LANES), F32)],
        compiler_params=_params(("arbitrary", "arbitrary")),
    )(P, P, P, P, cos, sin, g_ret, lg, s0, dmix)


_FB = 128


def _tri(lower):
    r = lax.broadcasted_iota(jnp.int32, (_FB, _FB), 0)
    c = lax.broadcasted_iota(jnp.int32, (_FB, _FB), 1)
    return ((r >= c) if lower else (r <= c)).astype(F32)


def _fgate_fwd(ffp, bpad, T):
    def body(ff_ref, b_ref, fc_ref, fr_ref):
        lane = lax.broadcasted_iota(jnp.int32, (1, LANES), 1)
        tri = _tri(True)
        carry = jnp.zeros((1, LANES), F32)
        for blk in range(T // _FB):
            z = ff_ref[blk * _FB:(blk + 1) * _FB, :] + b_ref[...]
            lf = jnp.where(lane < 8, jax.nn.log_sigmoid(z), 0.0)
            f = jnp.dot(tri, lf, precision=lax.Precision.HIGHEST, preferred_element_type=F32) + carry
            carry = f[_FB - 1:_FB, :]
            fc_ref[blk * _FB:(blk + 1) * _FB, :] = f
            fr_ref[:, blk * _FB:(blk + 1) * _FB] = f.T[:8, :]

    return pl.pallas_call(
        body, name="fgate_fwd",
        out_shape=[jax.ShapeDtypeStruct((T, LANES), F32), jax.ShapeDtypeStruct((8, T), F32)],
        compiler_params=pltpu.CompilerParams(vmem_limit_bytes=VMEM_LIMIT),
    )(ffp, bpad)


_BIAS_LANE = HEAD


def _head_bias_col(fc, head):
    lane = lax.broadcasted_iota(jnp.int32, (1, LANES), 1)
    return jnp.sum(jnp.where(lane == head, fc, 0.0), axis=-1, keepdims=True)


def _split3(f):
    hi = f.astype(BF).astype(F32)
    mid = (f - hi).astype(BF).astype(F32)
    lo = ((f - hi) - mid).astype(BF).astype(F32)
    return hi, mid, lo


def _fox_operands(P, fc, g_fq2, g_fk2, T, tm):
    def body(fq_ref, fk_ref, fv_ref, fc_ref, gq_ref, gk_ref, qat_ref, ka_ref, kat_ref, va_ref, vat_ref):
        j = pl.program_id(0)
        lane = lax.broadcasted_iota(jnp.int32, (1, LANES), 1)
        qn, kn = _fox_prep_fn(fq_ref[...], fk_ref[...], gq_ref[...], gk_ref[...])
        v = fv_ref[...]
        fcb = fc_ref[...]
        b = _BIAS_LANE
        for hh in range(2):
            hi, mid, lo = _split3(_head_bias_col(fcb, 2 * j + hh))
            take = (lambda a: a) if hh == 0 else (lambda a: pltpu.roll(a, HEAD, 1))
            qa = jnp.where(lane < HEAD, take(qn), jnp.where(lane == b, hi, jnp.where(lane == b + 1, mid, jnp.where(
                lane == b + 2, lo, jnp.where(lane < b + 6, 1.0, 0.0)))))
            ka = jnp.where(lane < HEAD, take(kn), jnp.where(lane < b + 3, 1.0, jnp.where(lane == b + 3, -hi, jnp.where(
                lane == b + 4, -mid, jnp.where(lane == b + 5, -lo, 0.0)))))
            va = jnp.where(lane < HEAD, take(v), 0.0)
            qat_ref[hh] = qa.T.astype(BF)
            for val, ref, tref in ((ka, ka_ref, kat_ref), (va, va_ref, vat_ref)):
                ref[hh] = val.astype(BF)
                tref[hh] = val.T.astype(BF)

    sec = lambda off: pl.BlockSpec((tm, LANES), lambda j, i: (i, off + j))
    par = pl.BlockSpec((1, LANES), lambda j, i: (0, 0))
    nat = pl.BlockSpec((2, tm, LANES), lambda j, i: (j, i, 0))
    trn = pl.BlockSpec((2, LANES, tm), lambda j, i: (j, 0, i))
    return pl.pallas_call(
        body, name="fox_operands", grid=(4, T // tm),
        in_specs=[sec(16), sec(20), sec(24), pl.BlockSpec((tm, LANES), lambda j, i: (i, 0)), par, par],
        out_specs=[trn, nat, trn, nat, trn],
        out_shape=[jax.ShapeDtypeStruct((8, LANES, T), BF)]
        + [jax.ShapeDtypeStruct((8, T, LANES), BF), jax.ShapeDtypeStruct((8, LANES, T), BF)] * 2,
        compiler_params=_params(("parallel", "arbitrary")),
    )(P, P, P, fc, g_fq2, g_fk2)


def _fox_forward(qat, ka, vat, T, tq, tk):
    nq, per = T // tq, tq // tk
    assert per == 2
    RC = 64

    def body(qat_ref, ka_ref, vat_ref, o_ref, lse_ref, s_scr, p_scr, a_scr, m_scr, l_scr, acc_scr):
        i = pl.program_id(1)
        sub = lax.broadcasted_iota(jnp.int32, (8, 1), 0)
        row = lax.broadcasted_iota(jnp.int32, (RC, tq), 0)
        col = lax.broadcasted_iota(jnp.int32, (RC, tq), 1)
        m_scr[...] = jnp.full(m_scr.shape, NEG, F32)
        l_scr[...] = jnp.zeros(l_scr.shape, F32)
        acc_scr[...] = jnp.zeros(acc_scr.shape, F32)

        def scores(slot, kb):
            k0 = pl.multiple_of(kb * tk, tk)
            for hh in range(2):
                s_scr[slot, hh] = jnp.dot(ka_ref[hh, pl.ds(k0, tk), :], qat_ref[hh], preferred_element_type=F32)

        def softmax(slot, kb, diagonal):
            shift = kb * tk - i * tq
            for hh in range(2):
                def masked(r):
                    tile = s_scr[slot, hh, r * RC:(r + 1) * RC, :]
                    return jnp.where(row + (r * RC + shift) <= col, tile, NEG) if diagonal else tile

                mx = jnp.max(masked(0), axis=0, keepdims=True)
                for r in range(1, tk // RC):
                    mx = jnp.maximum(mx, jnp.max(masked(r), axis=0, keepdims=True))
                m_old = m_scr[hh, 0:1, :]
                m2 = jnp.maximum(m_old, mx)
                a = jnp.exp(m_old - m2)
                lsum = jnp.zeros((1, tq), F32)
                for r in range(tk // RC):
                    p = jnp.exp(masked(r) - m2)
                    p_scr[slot, hh, r * RC:(r + 1) * RC, :] = p.astype(BF)
                    lsum = lsum + jnp.sum(p, axis=0, keepdims=True)
                m_scr[hh] = jnp.broadcast_to(m2, (8, tq))
                l_scr[hh] = jnp.broadcast_to(a * l_scr[hh, 0:1, :] + lsum, (8, tq))
                a_scr[slot, hh] = jnp.broadcast_to(a, (8, tq))

        def values(slot, kb):
            k0 = pl.multiple_of(kb * tk, tk)
            for hh in range(2):
                pv = jnp.dot(vat_ref[hh, 0:HEAD, pl.ds(k0, tk)], p_scr[slot, hh], preferred_element_type=F32)
                acc_scr[hh] = a_scr[slot, hh, 0:1, :] * acc_scr[hh] + pv

        def pair(kb, diag_first, diag_second, more):
            if more:
                scores(0, kb + 2)
            softmax(1, kb + 1, diag_first)
            values(0, kb)
            if more:
                scores(1, kb + 3)
                softmax(0, kb + 2, diag_second)
            values(1, kb + 1)

        scores(0, 0)
        scores(1, 1)
        softmax(0, 0, True)

        @pl.loop(0, jnp.maximum(i - 1, 0))
        def _(t):
            pair(2 * t, False, False, True)

        @pl.when(i >= 1)
        def _():
            pair(2 * (i - 1), False, True, True)

        pair(2 * i, True, False, False)

        o_ref[...] = jnp.concatenate([acc_scr[hh] / l_scr[hh, 0:1, :] for hh in range(2)], axis=0).T
        lses = [m_scr[hh, 0:1, :] + jnp.log(l_scr[hh, 0:1, :]) for hh in range(2)]
        lse_ref[0] = jnp.where(sub == 0, lses[0], jnp.where(sub == 1, lses[1], 0.0))

    return pl.pallas_call(
        body, name="fox_forward", grid=(4, nq),
        in_specs=[pl.BlockSpec((2, LANES, tq), lambda j, i: (j, 0, i)), pl.BlockSpec((2, T, LANES), lambda j, i: (j, 0, 0)),
                  pl.BlockSpec((2, LANES, T), lambda j, i: (j, 0, 0))],
        out_specs=[pl.BlockSpec((tq, LANES), lambda j, i: (i, j)), pl.BlockSpec((1, 8, tq), lambda j, i: (j, 0, i))],
        out_shape=[jax.ShapeDtypeStruct((T, 4 * LANES), F32), jax.ShapeDtypeStruct((4, 8, T), F32)],
        scratch_shapes=[pltpu.VMEM((2, 2, tk, tq), F32), pltpu.VMEM((2, 2, tk, tq), BF), pltpu.VMEM((2, 2, 8, tq), F32),
                        pltpu.VMEM((2, 8, tq), F32), pltpu.VMEM((2, 8, tq), F32), pltpu.VMEM((2, HEAD, tq), F32)],
        compiler_params=_params(("parallel", "arbitrary")),
    )(qat, ka, vat)


def _fox_cotangent(dmix, fox, T, tm):
    def body(do_ref, o_ref, doat_ref, dl_ref):
        lane = lax.broadcasted_iota(jnp.int32, (1, LANES), 1)
        sub = lax.broadcasted_iota(jnp.int32, (8, 1), 0)
        dob = do_ref[...].astype(BF).astype(F32)
        prod_t = (dob * o_ref[...]).T
        d0 = jnp.sum(prod_t[:HEAD], axis=0, keepdims=True)
        d1 = jnp.sum(prod_t[HEAD:], axis=0, keepdims=True)
        dl_ref[0] = jnp.where(sub == 0, d0, jnp.where(sub == 1, d1, 0.0))
        for hh in range(2):
            val = jnp.where(lane < HEAD, dob if hh == 0 else pltpu.roll(dob, HEAD, 1), 0.0)
            doat_ref[hh] = val.T.astype(BF)

    return pl.pallas_call(
        body, name="fox_cotangent", grid=(4, T // tm),
        in_specs=[pl.BlockSpec((tm, LANES), lambda j, i: (i, 4 + j)), pl.BlockSpec((tm, LANES), lambda j, i: (i, j))],
        out_specs=[pl.BlockSpec((2, LANES, tm), lambda j, i: (j, 0, i)), pl.BlockSpec((1, 8, tm), lambda j, i: (j, 0, i))],
        out_shape=[jax.ShapeDtypeStruct((8, LANES, T), BF), jax.ShapeDtypeStruct((4, 8, T), F32)],
        compiler_params=_params(("parallel", "arbitrary")),
    )(dmix, fox)


def _fox_backward(qat, ka, kat, va, doat, lse, dl, T, tq, tk):
    nq, nk = T // tq, T // tk

    def body(qat_ref, ka_ref, kat_ref, va_ref, doat_ref, lse_ref, dl_ref,
             dq_ref, dk_ref, dv_ref, df_ref, dr_ref, dqt, dkt, dvt, df_acc, sdp, pds):
        j, kb = pl.program_id(0), pl.program_id(1)
        lane = lax.broadcasted_iota(jnp.int32, (1, LANES), 1)
        first = (kb * tk) // tq

        @pl.when(kb == 0)
        def _():
            dqt[...] = jnp.zeros(dqt.shape, F32)

        dkt[...] = jnp.zeros(dkt.shape, F32)
        dvt[...] = jnp.zeros(dvt.shape, F32)
        df_acc[...] = jnp.zeros(df_acc.shape, F32)

        RC = 64
        last = nq - 1

        def products(slot, qi):
            q0 = pl.multiple_of(qi * tq, tq)
            for hh in range(2):
                sdp[slot, hh, 0] = jnp.dot(ka_ref[hh], qat_ref[hh, :, pl.ds(q0, tq)], preferred_element_type=F32)
                sdp[slot, hh, 1] = jnp.dot(va_ref[hh], doat_ref[hh, :, pl.ds(q0, tq)], preferred_element_type=F32)

        def softmax_bwd(slot, qi, diagonal, valid):
            q0 = pl.multiple_of(qi * tq, tq)
            shift = kb * tk - first * tq
            col = lax.broadcasted_iota(jnp.int32, (RC, tq), 1)
            row = lax.broadcasted_iota(jnp.int32, (RC, tq), 0)
            for hh in range(2):
                lse_row = lse_ref[0, hh:hh + 1, pl.ds(q0, tq)]
                dl_row = dl_ref[0, hh:hh + 1, pl.ds(q0, tq)]
                rsum = jnp.zeros((1, tq), F32)
                for r in range(tk // RC):
                    rows = slice(r * RC, (r + 1) * RC)
                    p = jnp.exp(sdp[slot, hh, 0, rows, :] - lse_row)
                    p = jnp.where((row + (r * RC + shift) <= col) if diagonal else valid, p, 0.0)
                    ds = p * (sdp[slot, hh, 1, rows, :] - dl_row)
                    pds[slot, hh, 0, rows, :] = p.astype(BF)
                    pds[slot, hh, 1, rows, :] = ds.astype(BF)
                    rsum = rsum + jnp.sum(ds, axis=0, keepdims=True)
                    part = ds[:, 0:LANES]
                    for c in range(1, tq // LANES):
                        part = part + ds[:, c * LANES:(c + 1) * LANES]
                    df_acc[hh, rows, :] += part
                dqt[hh, HEAD:HEAD + 8, pl.ds(q0, tq)] += jnp.broadcast_to(rsum, (8, tq))

        def accumulate(slot, qi):
            q0 = pl.multiple_of(qi * tq, tq)
            for hh in range(2):
                dvt[hh] += lax.dot_general(doat_ref[hh, 0:HEAD, pl.ds(q0, tq)], pds[slot, hh, 0], NT, preferred_element_type=F32)
                dkt[hh] += lax.dot_general(qat_ref[hh, 0:HEAD, pl.ds(q0, tq)], pds[slot, hh, 1], NT, preferred_element_type=F32)
                dqt[hh, 0:HEAD, pl.ds(q0, tq)] += jnp.dot(kat_ref[hh, 0:HEAD, :], pds[slot, hh, 1], preferred_element_type=F32)

        products(0, first)
        products(1, jnp.minimum(first + 1, last))
        softmax_bwd(0, first, True, None)

        @pl.loop(0, (nq - first + 1) // 2)
        def _(t):
            qi = first + 2 * t
            products(0, jnp.minimum(qi + 2, last))
            softmax_bwd(1, jnp.minimum(qi + 1, last), False, qi + 1 <= last)
            accumulate(0, qi)
            products(1, jnp.minimum(qi + 3, last))
            softmax_bwd(0, jnp.minimum(qi + 2, last), False, qi + 2 <= last)
            accumulate(1, jnp.minimum(qi + 1, last))

        dk_ref[...] = jnp.concatenate([dkt[0], dkt[1]], axis=0).T
        dv_ref[...] = jnp.concatenate([dvt[0], dvt[1]], axis=0).T.astype(dv_ref.dtype)
        f0 = -jnp.sum(df_acc[0], axis=1, keepdims=True)
        f1 = -jnp.sum(df_acc[1], axis=1, keepdims=True)
        df_ref[0] = jnp.where(lane == 2 * j, f0, jnp.where(lane == 2 * j + 1, f1, 0.0))

        @pl.when(kb == nk - 1)
        def _():
            for t in range(nq):
                cols = slice(t * tq, (t + 1) * tq)
                dq_ref[cols, :] = jnp.concatenate([dqt[0, 0:HEAD, cols], dqt[1, 0:HEAD, cols]], axis=0).T
                rsum = jnp.concatenate([dqt[0, HEAD:HEAD + 8, cols], dqt[1, HEAD:HEAD + 8, cols],
                                        jnp.zeros((LANES - 16, tq), F32)], axis=0).T
                dr_ref[0, cols, :] = jnp.where(lane == 2 * j, rsum[:, 0:1], jnp.where(lane == 2 * j + 1, rsum[:, 8:9], 0.0))

    trn_full = pl.BlockSpec((2, LANES, T), lambda j, kb: (j, 0, 0))
    nat_blk = pl.BlockSpec((2, tk, LANES), lambda j, kb: (j, kb, 0))
    trn_blk = pl.BlockSpec((2, LANES, tk), lambda j, kb: (j, 0, kb))
    rows = pl.BlockSpec((1, 8, T), lambda j, kb: (j, 0, 0))
    blk = pl.BlockSpec((tk, LANES), lambda j, kb: (kb, j))
    return pl.pallas_call(
        body, name="fox_backward", grid=(4, nk),
        in_specs=[trn_full, nat_blk, trn_blk, nat_blk, trn_full, rows, rows],
        out_specs=[pl.BlockSpec((T, LANES), lambda j, kb: (0, j)), blk, blk, pl.BlockSpec((1, tk, LANES), lambda j, kb: (j, kb, 0)),
                   pl.BlockSpec((1, T, LANES), lambda j, kb: (j, 0, 0))],
        out_shape=[jax.ShapeDtypeStruct((T, 4 * LANES), F32), jax.ShapeDtypeStruct((T, 4 * LANES), F32),
                   jax.ShapeDtypeStruct((T, 4 * LANES), BF), jax.ShapeDtypeStruct((4, T, LANES), F32),
                   jax.ShapeDtypeStruct((4, T, LANES), F32)],
        scratch_shapes=[pltpu.VMEM((2, HEAD + 8, T), F32), pltpu.VMEM((2, HEAD, tk), F32), pltpu.VMEM((2, HEAD, tk), F32),
                        pltpu.VMEM((2, tk, LANES), F32), pltpu.VMEM((2, 2, 2, tk, tq), F32), pltpu.VMEM((2, 2, 2, tk, tq), BF)],
        compiler_params=_params(("arbitrary", "arbitrary")),
    )(qat, ka, kat, va, doat, lse, dl)


def _fgate_bwd_col(ffp, bpad, dfc4, drc4, T):
    def body(ff_ref, b_ref, dfc_ref, drc_ref, dff_ref, db_ref):
        lane = lax.broadcasted_iota(jnp.int32, (1, LANES), 1)
        tri = _tri(False)
        carry = jnp.zeros((1, LANES), F32)
        db = jnp.zeros((1, LANES), F32)
        for blk in reversed(range(T // _FB)):
            rows = slice(blk * _FB, (blk + 1) * _FB)
            dcol = dfc_ref[0, rows, :] + drc_ref[0, rows, :]
            for pair in range(1, 4):
                dcol = dcol + (dfc_ref[pair, rows, :] + drc_ref[pair, rows, :])
            dlf = jnp.dot(tri, dcol, precision=lax.Precision.HIGHEST, preferred_element_type=F32) + carry
            carry = dlf[0:1, :]
            z = ff_ref[blk * _FB:(blk + 1) * _FB, :] + b_ref[...]
            dz = jnp.where(lane < 8, dlf * jax.nn.sigmoid(-z), 0.0)
            dff_ref[blk * _FB:(blk + 1) * _FB, :] = dz.astype(dff_ref.dtype)
            db = db + jnp.sum(dz, axis=0, keepdims=True)
        db_ref[...] = db

    return pl.pallas_call(
        body, name="fgate_bwd",
        out_shape=[jax.ShapeDtypeStruct((T, LANES), BF), jax.ShapeDtypeStruct((1, LANES), F32)],
        compiler_params=pltpu.CompilerParams(vmem_limit_bytes=VMEM_LIMIT),
    )(ffp, bpad, dfc4, drc4)


MESH = pl.DeviceIdType.MESH
N_PEERS = N_DEV - 1


def _place():
    return lax.axis_index("x"), lax.axis_index("y"), lax.axis_index("c")


def _all_gather(shard, rows, g, tm):
    R, W = shard.shape
    T = rows.shape[0]
    steps = T // tm

    def body(w_ref, rows_ref, g_ref, out_ref, norm_ref, send_sems, recv_sems, local_sem):
        x, y, c = _place()
        me, sibling = (x, y, c), (x, y, 1 - c)
        chips = [(1 - x, y), (x, 1 - y), (1 - x, 1 - y)]

        def slot(px, py, pc):
            return out_ref.at[4 * px + 2 * py + pc]

        def copy(k, block, to, src=None):
            return pltpu.make_async_remote_copy(
                src_ref=slot(*block) if src is None else src, dst_ref=slot(*block),
                send_sem=send_sems.at[k], recv_sem=recv_sems.at[k], device_id=to, device_id_type=MESH)

        mine = pltpu.make_async_copy(w_ref, slot(*me), local_sem)
        first = [copy(0, me, sibling, src=w_ref)]
        first += [copy(1 + n, me, (*chip, c), src=w_ref) for n, chip in enumerate(chips)]
        passed = [copy(4 + n, (*chip, c), sibling) for n, chip in enumerate(chips)]

        @pl.when(pl.program_id(0) == 0)
        def _():
            mine.start()
            for cp in first:
                cp.start()

        norm_ref[...] = _rms(rows_ref[...], g_ref[...]).astype(norm_ref.dtype)

        @pl.when(pl.program_id(0) == steps - 1)
        def _():
            for n, chip in enumerate(chips):
                copy(1 + n, (*chip, c), me).wait_recv()
                passed[n].start()
            copy(0, sibling, me).wait_recv()
            for n, chip in enumerate(chips):
                copy(4 + n, (*chip, 1 - c), me).wait_recv()
            for cp in first + passed:
                cp.wait_send()
            mine.wait()

    tile = pl.BlockSpec((tm, D), lambda i: (i, 0))
    return pl.pallas_call(
        body, name="all_gather_weights", grid=(steps,),
        out_shape=[jax.ShapeDtypeStruct((N_DEV, R, W), shard.dtype), jax.ShapeDtypeStruct((T, D), BF)],
        in_specs=[pl.BlockSpec(memory_space=pl.ANY), tile, pl.BlockSpec((1, D), lambda i: (0, 0))],
        out_specs=[pl.BlockSpec(memory_space=pl.ANY), tile],
        scratch_shapes=[pltpu.SemaphoreType.DMA((N_PEERS,)), pltpu.SemaphoreType.DMA((N_PEERS,)), pltpu.SemaphoreType.DMA],
        compiler_params=_params(("arbitrary",)),
    )(shard, rows, g)


def _exchange_copies(src_refs, land_refs, send_sems, recv_sems, scatter):
    x, y, c = _place()
    me = 4 * x + 2 * y + c
    copies = []
    for k, (src_ref, land_ref) in enumerate(zip(src_refs, land_refs)):
        for r in range(1, N_DEV):
            px, py, pc = x ^ (r >> 2), y ^ ((r >> 1) & 1), c ^ (r & 1)
            copies.append(pltpu.make_async_remote_copy(
                src_ref=src_ref.at[4 * px + 2 * py + pc] if scatter else src_ref, dst_ref=land_ref.at[me],
                send_sem=send_sems.at[k * N_PEERS + r - 1], recv_sem=recv_sems.at[k * N_PEERS + r - 1],
                device_id=(px, py, pc), device_id_type=MESH))
    return copies


_HBM = pl.BlockSpec(memory_space=pltpu.HBM)
_SEM = pl.BlockSpec(memory_space=pltpu.SEMAPHORE)
_EFFECT = pltpu.SideEffectType.DATAFLOW_SIDE_EFFECTING


def _exchange_start(name, srcs, lands, scatter):
    n = len(srcs)

    def body(*refs):
        send_sems, recv_sems = refs[2 * n], refs[2 * n + 1]
        for cp in _exchange_copies(refs[:n], refs[n:2 * n], send_sems, recv_sems, scatter):
            cp.start()
        token = refs[-1]
        token[...] = jnp.zeros(token.shape, F32)

    arrays = list(srcs) + list(lands)
    out = pl.pallas_call(
        body, name=name,
        out_shape=(pltpu.SemaphoreType.DMA((n * N_PEERS,)), pltpu.SemaphoreType.DMA((n * N_PEERS,)))
        + tuple(pltpu.HBM(a.shape, a.dtype) for a in arrays) + (jax.ShapeDtypeStruct((8, LANES), F32),),
        in_specs=(_HBM,) * (2 * n), out_specs=(_SEM, _SEM) + (_HBM,) * (2 * n) + (pl.BlockSpec(memory_space=pltpu.VMEM),),
        input_output_aliases={k: 2 + k for k in range(2 * n)},
        compiler_params=pltpu.CompilerParams(has_side_effects=_EFFECT),
    )(*(pltpu.with_memory_space_constraint(a, pltpu.HBM) for a in arrays))
    return out[0], out[1], out[2:2 + n], out[2 + n:2 + 2 * n], out[-1]


def _exchange_wait(name, started, after, scatter):
    send_sems, recv_sems, srcs, lands, _ = started
    n = len(srcs)

    def body(*refs):
        copies = _exchange_copies(refs[:n], refs[n:2 * n], refs[2 * n], refs[2 * n + 1], scatter)
        for cp in copies:
            cp.wait_send()
        for cp in copies:
            cp.wait_recv()

    arrays = list(srcs) + list(lands)
    out = pl.pallas_call(
        body, name=name,
        out_shape=tuple(pltpu.HBM(a.shape, a.dtype) for a in arrays),
        in_specs=(_HBM,) * (2 * n) + (_SEM, _SEM, pl.BlockSpec(memory_space=pl.ANY)), out_specs=(_HBM,) * (2 * n),
        input_output_aliases={k: k for k in range(2 * n)},
        compiler_params=pltpu.CompilerParams(has_side_effects=_EFFECT),
    )(*arrays, send_sems, recv_sems, after)
    return out[:n], out[n:]


def _adam_update(g, w, m, v):
    m2 = ADAM_B1 * m + (1.0 - ADAM_B1) * g
    v2 = ADAM_B2 * v + (1.0 - ADAM_B2) * jnp.square(g)
    m_hat = m2 / (1.0 - ADAM_B1 ** ADAM_STEP)
    v_hat = v2 / (1.0 - ADAM_B2 ** ADAM_STEP)
    return g, -ADAM_LR * (m_hat / (jnp.sqrt(v_hat) + ADAM_EPS) + ADAM_WD * w), m2, v2


def _adamw(name, me, slots, sent, w, m, v):
    R, W = w.shape
    steps = max(k for k in (4, 2, 1) if k == 1 or (R % k == 0 and (R // k) % 16 == 0))
    tr = R // steps

    def body(me_ref, s_ref, *refs):
        if sent is not None:
            g = refs[0][0].astype(F32)
            refs = refs[1:]
        else:
            g = jnp.zeros((tr, W), F32)
        for s in range(N_DEV):
            part = s_ref[s].astype(F32)
            g = g + (part if sent is None else jnp.where(me_ref[0] == s, 0.0, part))
        w_ref, m_ref, v_ref = refs[:3]
        for o, r in zip(refs[3:], _adam_update(g, w_ref[...], m_ref[...], v_ref[...])):
            o[...] = r

    rows = pl.BlockSpec((tr, W), lambda i, me_ref: (i, 0))
    in_specs = [pl.BlockSpec((N_DEV, tr, W), lambda i, me_ref: (0, i, 0))]
    args = [slots]
    if sent is not None:
        in_specs.append(pl.BlockSpec((1, tr, W), lambda i, me_ref: (me_ref[0], i, 0)))
        args.append(sent)
    return pl.pallas_call(
        body, name=name,
        grid_spec=pltpu.PrefetchScalarGridSpec(num_scalar_prefetch=1, grid=(steps,), in_specs=in_specs + [rows] * 3,
                                               out_specs=[rows] * 4),
        out_shape=[jax.ShapeDtypeStruct((R, W), F32)] * 4,
        compiler_params=_params(("arbitrary",)),
    )(me, *args, w, m, v)


def _tables(T):
    pos = jnp.arange(T, dtype=F32)
    inv_freq = 10000.0 ** (-jnp.arange(0, HEAD, 2, dtype=F32) / HEAD)
    ang = pos[:, None] * inv_freq[None, :]
    cos, sin = jnp.cos(ang), jnp.sin(ang)
    cos4 = jnp.tile(cos, (1, 4))
    sin4 = jnp.tile(jnp.concatenate([-sin, sin], axis=1), (1, 2))
    log_g = jnp.log(1.0 - 2.0 ** (-5.0 - jnp.arange(8, dtype=F32)))
    return cos4, sin4, jnp.repeat(log_g, HEAD)[None, :]


def _local_step(x, hn1, mem, target, sp, w_inT, token, fetch_rest, push, push_small):
    T = x.shape[0]
    tm = min(512, T)
    tq = min(256, T)
    tb = min(1024, T)
    cos4, sin4, lg = _tables(T)
    g_fq2 = jnp.tile(sp["g_fox_q"], (1, 2))
    g_fk2 = jnp.tile(sp["g_fox_k"], (1, 2))
    g_ret = sp["g_ret_out"].reshape(1, 8 * HEAD)
    bpad = jnp.pad(sp["b_forget"], ((0, 0), (0, LANES - 8)))
    w_secs = [w_inT[k * 512:(k + 1) * 512] for k in range(7)]
    w_ffT = jnp.pad(w_inT[3584:3592], ((0, LANES - 8), (0, 0)))
    w_mainT = w_inT[:3584]
    tie = lambda p, tok: p + tok[0:1, 0:1]
    tm2, tm4 = min(1024, T), min(2048, T)

    P, = _mm("proj_in", [[(hn1, w_mainT, "nt")]], [], lambda acc, after: (acc,), T, 3584, tm4, 512, [F32],
             params=[jnp.broadcast_to(token[0:1, 0:1], (1, 3584))])
    ffp, = _mm("proj_ff", [[(hn1, w_ffT, "nt")]], [], _ident, T, LANES, tm, LANES, [F32])
    ret, s0 = _ret_fwd(P, cos4, sin4, g_ret, lg, T, tb)
    fc, _ = _fgate_fwd(ffp, bpad, T)
    qat, ka, kat, va, vat = _fox_operands(P, fc, g_fq2, g_fk2, T, tm4)
    fox, lse = _fox_forward(qat, ka, vat, T, min(512, T), tq)
    W = fetch_rest(fox)
    w_out_halves = (W["w_out"][:4 * LANES], W["w_out"][4 * LANES:])
    h1, hn2 = _mm("proj_out", [[(ret, w_out_halves[0], "nn"), (fox, w_out_halves[1], "nn")]], [x], _add_rms_epi, T, D, tm2, D,
                  [F32, BF], params=[sp["g_xattn"]])

    qx, = _mm("proj_xq", [[(hn2, W["w_xq"], "nn")]], [], _ident, T, D, tm2, D, [F32])
    memn, = _rw_fwd("rms_mem", _rms_fn, [(mem, D, 0, False)], [(sp["g_mem"], D, 0, False)], [(BF, D)], N_MEM, N_MEM, 1)
    kv, = _mm("proj_xkv", [[(memn, W["w_xkvT"], "nt")]], [], _ident, N_MEM, 2 * D, N_MEM, 512, [F32])
    xa_rows = [(qx, XHEAD, 0, True)]
    xa_params = [(sp["g_xq"], XHEAD, 0, False), (sp["g_xk"], XHEAD, 0, False), (kv, XHEAD, 0, True), (kv, XHEAD, 4, True)]
    xo, = _rw_fwd("xattn_fwd", _xattn_fn, xa_rows, xa_params, [(BF, XHEAD)], T, tm4, 4)
    h2, hn3 = _mm("proj_xo", [[(xo, W["w_xo"], "nn")]], [h1], _add_rms_epi, T, D, tm2, D, [F32, BF], params=[sp["g_ffn"]])

    gate, up, act = _mm("ffn_in", [[(hn3, W["w_gateT"], "nt")], [(hn3, W["w_upT"], "nt")]], [], _swiglu_fwd_epi,
                        T, D_FF, tm4, 256, [BF, BF, BF])
    dy, dyb, loss_part = _mm("ffn_out", [[(act, W["w_down"], "nn")]], [h2, target], _add_loss_epi, T, D, tm, D, [F32, BF], n_acc=1)

    dgate, dup = _mm("ffn_out_bwd", [[(dyb, W["w_down"], "nt")]], [gate, up], _swiglu_bwd_epi, T, D_FF, tm4, 256, [BF, BF])
    gW = {}
    gW["w_gateT"], gW["w_upT"] = _mm("dw_gate_up", [[(dgate, hn3, "tn")], [(dup, hn3, "tn")]], [], _each, D_FF, D, 256, D, [BF, BF])
    gW["w_down"], = _mm("dw_down", [[(act, dyb, "tn")]], [], _ident, D_FF, D, 256, D, [BF])
    tok = push("ffn", gW)
    gs = {}
    dh2, dh2b, gs["g_ffn"] = _mm("ffn_in_bwd", [[(dgate, W["w_gateT"], "nn"), (dup, W["w_upT"], "nn")]], [h2, dy], _rms_bwd_epi,
                                 T, D, min(256, T), D, [F32, BF], params=[tie(sp["g_ffn"], tok)], n_acc=1)

    dxo, = _mm("proj_xo_bwd", [[(dh2b, W["w_xo"], "nt")]], [], _ident, T, D, tm2, D, [BF])
    gW["w_xo"], = _mm("dw_xo", [[(xo, dh2b, "tn")]], [], _ident, D, D, 256, D, [BF])
    dqx, gs["g_xq"], gs["g_xk"], dkv_k, dkv_v = _rw_bwd(
        "xattn_bwd", _xattn_fn, xa_rows, xa_params, [(dxo, XHEAD, 0, True)], T, tm4, 4, [BF], [True, True, True, True])
    dkv = jnp.concatenate([dkv_k[:, :D], dkv_v[:, D:]], axis=1)
    gW["w_xq"], = _mm("dw_xq", [[(hn2, dqx, "tn")]], [], _ident, D, D, 256, D, [BF])
    dmemn, = _mm("proj_xkv_bwd", [[(dkv, W["w_xkvT"], "nn")]], [], _ident, N_MEM, D, N_MEM, 512, [F32])
    gW["w_xkvT"], = _mm("dw_xkv", [[(dkv, memn, "tn")]], [], _ident, 2 * D, D, 512, D, [BF])
    tok = push("xattn", gW)
    gs["g_mem"], = _rw_bwd("rms_mem_bwd", _rms_fn, [(mem, D, 0, False)], [(sp["g_mem"], D, 0, False)], [(dmemn, D, 0, False)],
                           N_MEM, N_MEM, 1, [None], [True])
    dh1, dh1b, gs["g_xattn"] = _mm("proj_xq_bwd", [[(dqx, W["w_xq"], "nt")]], [h1, dh2], _rms_bwd_epi, T, D, tm, D, [F32, BF],
                                   params=[tie(sp["g_xattn"], tok)], n_acc=1)

    dmix, = _mm("proj_out_bwd", [[(dh1b, W["w_out"], "nt")]], [], _ident, T, D, tm2, D, [F32])
    gW["w_out"] = jnp.concatenate(_mm("dw_out", [[(ret, dh1b, "tn")], [(fox, dh1b, "tn")]], [], _each, 4 * LANES, D, 256, D,
                                      [BF, BF]), axis=0)
    tok = push("out", gW)
    doat, dl = _fox_cotangent(dmix, fox, T, tm4)
    dqn, dkn, dfv, dfc4, drc4 = _fox_backward(qat, ka, kat, va, doat, lse + tok[0:1, 0:1], dl, T, tq, tq)
    dfq, dfk, gq2, gk2 = _rw_bwd("fox_prep_bwd", _fox_prep_fn, [(P, LANES, 16, True), (P, LANES, 20, True)],
                                 [(g_fq2, LANES, 0, False), (g_fk2, LANES, 0, False)],
                                 [(dqn, LANES, 0, True), (dkn, LANES, 0, True)], T, tm4, 4, [BF, BF], [True, True])
    gs["g_fox_q"] = gq2[:, :HEAD] + gq2[:, HEAD:]
    gs["g_fox_k"] = gk2[:, :HEAD] + gk2[:, HEAD:]
    dff, dbp = _fgate_bwd_col(ffp, bpad, dfc4, drc4, T)
    gs["b_forget"] = dbp[:, :8]
    drq, drk, drv, drg, dg_ret = _ret_bwd(P, cos4, sin4, g_ret, lg, s0, dmix, T, tb)
    gs["g_ret_out"] = dg_ret
    dsecs = [drq, drk, drv, drg, dfq, dfk, dfv]
    g_secs = list(_mm("dw_in", [[(d, hn1, "tn")] for d in dsecs], [], _each, 512, D, LANES, D, [BF] * len(dsecs)))
    g_ff, = _mm("dw_in_ff", [[(dff, hn1, "tn")]], [], _ident, LANES, D, LANES, D, [BF])
    gW["w_inT"] = jnp.concatenate(g_secs + [g_ff[:8]], axis=0)
    tok = push("in", gW)
    grad_x, gs["g_mix"] = _mm("proj_in_bwd", [[(d, w, "nn") for d, w in zip(dsecs, w_secs)] + [(dff, w_ffT, "nn")]], [x, dh1],
                              _rms_bwd_first_epi, T, D, tm, D, [F32], params=[tie(sp["g_mix"], tok)], n_acc=1)
    return grad_x, push_small(gs, loss_part)


_CANON = {"w_in": "w_inT", "w_xkv": "w_xkvT", "w_gate": "w_gateT", "w_up": "w_upT"}
_SMALL = (("g_mix", 0, 0, 1024), ("g_xattn", 1, 0, 1024), ("g_mem", 2, 0, 1024), ("g_ffn", 3, 0, 1024),
          ("g_ret_out", 4, 0, 512), ("g_xq", 4, 512, 256), ("g_xk", 4, 768, 256),
          ("g_fox_q", 5, 0, 64), ("g_fox_k", 5, 64, 64), ("b_forget", 5, 128, 8))
_LOSS_AT = (5, 256)


def _pack_small(tree):
    buf = jnp.zeros((SMALL_ROWS, D), F32)
    for name, r, c, n in _SMALL:
        buf = lax.dynamic_update_slice(buf, tree[name].reshape(1, n).astype(F32), (r, c))
    return buf


def _unpack_small(buf, like):
    return {name: buf[r:r + 1, c:c + n].reshape(like[name].shape) for name, r, c, n in _SMALL}


def _canonical(tree, name):
    a = tree[name][0]
    return a.T if W_SHARD[name][1] else a


def _from_canonical(a, name):
    return (a.T if W_SHARD[name][1] else a)[None]


def kernel(x, mem, g_mix, w_in, b_forget, g_ret_out, g_fox_q, g_fox_k, w_out, g_xattn, w_xq, w_xkv, g_mem, g_xq, g_xk, w_xo, g_ffn, w_gate, w_up, w_down, loss_target, m_g_mix, m_w_in, m_b_forget, m_g_ret_out, m_g_fox_q, m_g_fox_k, m_w_out, m_g_xattn, m_w_xq, m_w_xkv, m_g_mem, m_g_xq, m_g_xk, m_w_xo, m_g_ffn, m_w_gate, m_w_up, m_w_down, v_g_mix, v_w_in, v_b_forget, v_g_ret_out, v_g_fox_q, v_g_fox_k, v_w_out, v_g_xattn, v_w_xq, v_w_xkv, v_g_mem, v_g_xq, v_g_xk, v_w_xo, v_g_ffn, v_w_gate, v_w_up, v_w_down):
    names = ("g_mix", "w_in", "b_forget", "g_ret_out", "g_fox_q", "g_fox_k", "w_out", "g_xattn", "w_xq", "w_xkv", "g_mem",
             "g_xq", "g_xk", "w_xo", "g_ffn", "w_gate", "w_up", "w_down")
    w = dict(zip(names, (g_mix, w_in, b_forget, g_ret_out, g_fox_q, g_fox_k, w_out, g_xattn, w_xq, w_xkv, g_mem, g_xq, g_xk,
                         w_xo, g_ffn, w_gate, w_up, w_down)))
    m = dict(zip(names, (m_g_mix, m_w_in, m_b_forget, m_g_ret_out, m_g_fox_q, m_g_fox_k, m_w_out, m_g_xattn, m_w_xq, m_w_xkv,
                         m_g_mem, m_g_xq, m_g_xk, m_w_xo, m_g_ffn, m_w_gate, m_w_up, m_w_down)))
    v = dict(zip(names, (v_g_mix, v_w_in, v_b_forget, v_g_ret_out, v_g_fox_q, v_g_fox_k, v_w_out, v_g_xattn, v_w_xq, v_w_xkv,
                         v_g_mem, v_g_xq, v_g_xk, v_w_xo, v_g_ffn, v_w_gate, v_w_up, v_w_down)))
    small_names = [s[0] for s in _SMALL]
    me = 4 * lax.axis_index("x") + 2 * lax.axis_index("y") + lax.axis_index("c")
    me1 = me.astype(jnp.int32).reshape(1)

    sp = {n: w[n].reshape(1, -1) for n in small_names}
    first, hn1 = _all_gather(_canonical(w, "w_in").astype(BF), x[0], sp["g_mix"], min(1024, x.shape[1]))
    first, rest = lax.optimization_barrier((first, [_canonical(w, n).astype(BF) for n in GATHER_REST]))
    rest_started = _exchange_start("gather_rest_start", rest, [lax.empty((N_DEV,) + a.shape, BF) for a in rest], scatter=False)

    def fetch_rest(after):
        srcs, lands = _exchange_wait("gather_rest_wait", rest_started, after, scatter=False)
        lands = [lax.dynamic_update_index_in_dim(a, own, me, axis=0) for a, own in zip(lands, srcs)]
        return {_CANON.get(n, n): a.reshape(N_DEV * a.shape[1], D) for n, a in zip(GATHER_REST, lands)}

    pushed = {}

    def push(group, grads):
        srcs = [grads[_CANON.get(n, n)].reshape(N_DEV, W_SHARD[n][0], D) for n in SCATTER_GROUPS[group]]
        pushed[group] = _exchange_start("scatter_%s_start" % group, srcs, [lax.empty(a.shape, BF) for a in srcs], scatter=True)
        return pushed[group][4]

    def push_small(gs, loss_part):
        small = lax.dynamic_update_slice(_pack_small(gs), loss_part[:, :1], _LOSS_AT)
        pushed["small"] = _exchange_start("gather_small_start", [small], [jnp.broadcast_to(small[None], (N_DEV,) + small.shape)],
                                          scatter=False)
        return pushed["small"][4]

    grad_x, done = _local_step(x[0], hn1, mem[0], loss_target[0], sp, first.reshape(N_DEV * W_SHARD["w_in"][0], D),
                               rest_started[4], fetch_rest, push, push_small)

    results, after = {}, done
    for group in ("ffn", "xattn", "out", "small", "in"):
        if group == "small":
            recv_small = _exchange_wait("gather_small_wait", pushed["small"], after, scatter=False)[1][0]
            g_sm, d_sm, m_sm, v_sm = _adamw("adamw_small", me1, recv_small, None, _pack_small(w), _pack_small(m), _pack_small(v))
            after = g_sm
            continue
        sents, recvs = _exchange_wait("scatter_%s_wait" % group, pushed[group], after, scatter=True)
        for name, sent, recv in zip(SCATTER_GROUPS[group], sents, recvs):
            res = _adamw("adamw_" + name, me1, recv, sent, *(_canonical(t, name) for t in (w, m, v)))
            results[name] = [_from_canonical(r, name) for r in res]
        after = results[SCATTER_GROUPS[group][-1]][0]
    loss = g_sm[_LOSS_AT[0], _LOSS_AT[1]]

    outs = []
    for k, sm in enumerate((g_sm, d_sm, m_sm, v_sm)):
        tree = _unpack_small(sm, w)
        tree.update({name: res[k] for name, res in results.items()})
        outs += [tree[n] for n in names]
    return (loss, grad_x[None], *outs)
```

```python
import jax
import jax.numpy as jnp
from jax import lax
from jax.experimental import pallas as pl
from jax.experimental.pallas import tpu as pltpu

F32 = jnp.float32
BF = jnp.bfloat16

D = 1024
HEAD = 64
CHUNK = 64
N_MEM = 256
XHEAD = 256
D_FF = 2816
EPS = 1e-6
NEG = -1e30
LANES = 128
N_DEV = 8
V7X_VMEM_BYTES = 64 * 1024 * 1024
VMEM_LIMIT = V7X_VMEM_BYTES - 8 * 1024 * 1024

ADAM_LR, ADAM_B1, ADAM_B2, ADAM_EPS, ADAM_WD, ADAM_STEP = 0.001, 0.9, 0.999, 1e-08, 0.01, 10

W_SHARD = {"w_in": (449, True), "w_out": (128, False), "w_xq": (128, False), "w_xkv": (256, True),
           "w_xo": (128, False), "w_gate": (352, True), "w_up": (352, True), "w_down": (352, False)}
GATHER_REST = {"attn": ("w_out", "w_xq", "w_xkv", "w_xo"), "ffn": ("w_gate", "w_up", "w_down")}
SCATTER_GROUPS = {"ffn": ("w_gate", "w_up", "w_down"), "xattn": ("w_xq", "w_xo", "w_xkv"), "out": ("w_out",), "in": ("w_in",)}
SMALL_ROWS = 8

NT = (((1,), (1,)), ((), ()))
NN = (((1,), (0,)), ((), ()))
TN = (((0,), (0,)), ((), ()))
_DIMS = {"nn": NN, "nt": NT, "tn": TN}


def _params(sem):
    return pltpu.CompilerParams(dimension_semantics=sem, vmem_limit_bytes=VMEM_LIMIT)


def _mm(name, products, extras, epilogue, M, N, tm, tn, out_dtypes, params=(), n_acc=0):
    assert n_acc == 0 or tn == N
    flat = [t for p in products for t in p]
    counts = [len(p) for p in products]
    in_specs, args, where, slots = [], [], {}, []

    def operand(arr, spec, kind):
        key = (id(arr), kind)
        if key not in where:
            where[key] = len(args)
            args.append(arr)
            in_specs.append(spec)
        return where[key]

    for a, b, form in flat:
        if form == "tn":
            ia = operand(a, pl.BlockSpec((a.shape[0], tm), lambda i, j: (0, i)), "a_tn")
        else:
            ia = operand(a, pl.BlockSpec((tm, a.shape[1]), lambda i, j: (i, 0)), "a")
        if form == "nt":
            ib = operand(b, pl.BlockSpec((tn, b.shape[1]), lambda i, j: (j, 0)), "b_nt")
        else:
            ib = operand(b, pl.BlockSpec((b.shape[0], tn), lambda i, j: (0, j)), "b")
        slots.append((ia, ib))
    n_mm = len(args)
    for e in extras:
        in_specs.append(pl.BlockSpec((tm, tn), lambda i, j: (i, j)))
        args.append(e)
    for p in params:
        in_specs.append(pl.BlockSpec((1, tn), lambda i, j: (0, j)))
        args.append(p)
    n_in = len(args)
    n_out = len(out_dtypes)

    def body(*refs):
        ins, outs = refs[:n_in], refs[n_in:]
        prods, p = [], 0
        for c in counts:
            acc = None
            for _ in range(c):
                a = ins[slots[p][0]][...].astype(BF)
                b = ins[slots[p][1]][...].astype(BF)
                d = lax.dot_general(a, b, _DIMS[flat[p][2]], preferred_element_type=F32)
                acc = d if acc is None else acc + d
                p += 1
            prods.append(acc)
        ex = [r[...].astype(F32) for r in ins[n_mm:]]
        res = epilogue(*prods, *ex)
        for o, r in zip(outs[:n_out], res[:n_out]):
            o[...] = r.astype(o.dtype)
        for o, r in zip(outs[n_out:], res[n_out:]):
            @pl.when(pl.program_id(0) == 0)
            def _(o=o):
                o[...] = jnp.zeros(o.shape, F32)
            o[...] += r

    return pl.pallas_call(
        body, name=name, grid=(M // tm, N // tn), in_specs=in_specs,
        out_specs=[pl.BlockSpec((tm, tn), lambda i, j: (i, j)) for _ in out_dtypes]
        + [pl.BlockSpec((1, tn), lambda i, j: (0, j)) for _ in range(n_acc)],
        out_shape=[jax.ShapeDtypeStruct((M, N), dt) for dt in out_dtypes] + [jax.ShapeDtypeStruct((1, N), F32)] * n_acc,
        compiler_params=_params(("arbitrary", "arbitrary")),
    )(*args)


def _ident(x):
    return (x,)


def _each(*xs):
    return xs


def _spec(rows, w, off, per_j):
    if per_j:
        return pl.BlockSpec((rows, w), lambda j, i: (i, off + j))
    return pl.BlockSpec((rows, w), lambda j, i: (i, off))


def _pspec(rows, w, off, per_j):
    if per_j:
        return pl.BlockSpec((rows, w), lambda j, i: (0, off + j))
    return pl.BlockSpec((rows, w), lambda j, i: (0, off))


def _rw_fwd(name, fn, rows, params, outs, T, tm, nj, n_acc=0):
    in_specs = [_spec(tm, w, off, pj) for _, w, off, pj in rows] + [_pspec(a.shape[0], w, off, pj) for a, w, off, pj in params]
    args = [r[0] for r in rows] + [p[0] for p in params]
    n_in, n_out = len(args), len(outs)
    out_specs = [pl.BlockSpec((tm, w), lambda j, i: (i, j)) for _, w in outs]
    out_shape = [jax.ShapeDtypeStruct((T, nj * w), dt) for dt, w in outs]
    out_specs += [pl.BlockSpec((1, LANES), lambda j, i: (0, 0)) for _ in range(n_acc)]
    out_shape += [jax.ShapeDtypeStruct((1, LANES), F32) for _ in range(n_acc)]

    def body(*refs):
        vals = [r[...].astype(F32) for r in refs[:n_in]]
        res = fn(*vals)
        orefs = refs[n_in:]
        for k in range(n_out):
            orefs[k][...] = res[k].astype(orefs[k].dtype)
        first = (pl.program_id(0) == 0) & (pl.program_id(1) == 0)
        for k in range(n_acc):
            @pl.when(first)
            def _(k=k):
                orefs[n_out + k][...] = jnp.zeros((1, LANES), F32)
            orefs[n_out + k][...] += res[n_out + k]

    return pl.pallas_call(
        body, name=name, grid=(nj, T // tm), in_specs=in_specs, out_specs=out_specs, out_shape=out_shape,
        compiler_params=_params(("arbitrary", "arbitrary")),
    )(*args)


def _rw_bwd(name, fn, rows, params, cots, T, tm, nj, row_grads, param_grads, resid=None):
    in_specs = ([_spec(tm, w, off, pj) for _, w, off, pj in rows] + [_pspec(a.shape[0], w, off, pj) for a, w, off, pj in params]
                + [_spec(tm, w, off, pj) for _, w, off, pj in cots])
    args = [r[0] for r in rows] + [p[0] for p in params] + [c[0] for c in cots]
    if resid is not None:
        in_specs.append(_spec(tm, rows[0][1], rows[0][2], rows[0][3]))
        args.append(resid)
    nr, npar, nc = len(rows), len(params), len(cots)
    out_specs, out_shape, kinds = [], [], []
    for k, dts in enumerate(row_grads):
        for dt in (dts if isinstance(dts, (list, tuple)) else [dts]):
            if dt is not None:
                w = rows[k][1]
                out_specs.append(pl.BlockSpec((tm, w), lambda j, i: (i, j)))
                out_shape.append(jax.ShapeDtypeStruct((T, nj * w), dt))
                kinds.append(("row", k))
    for k, need in enumerate(param_grads):
        if need:
            a, w, off, pj = params[k]
            out_specs.append(_pspec(a.shape[0], w, off, pj))
            out_shape.append(jax.ShapeDtypeStruct(a.shape, F32))
            kinds.append(("par", k))

    def body(*refs):
        vals = [r[...].astype(F32) for r in refs[:nr + npar]]
        ct = tuple(r[...].astype(F32) for r in refs[nr + npar:nr + npar + nc])
        _, vjp = jax.vjp(lambda *a: tuple(fn(*a)), *vals)
        grads = list(vjp(ct))
        n_in = nr + npar + nc + (resid is not None)
        if resid is not None:
            grads[0] = grads[0] + refs[n_in - 1][...].astype(F32)
        orefs = refs[n_in:]
        j, i = pl.program_id(0), pl.program_id(1)
        for o, (kind, k) in zip(orefs, kinds):
            if kind == "row":
                o[...] = grads[k].astype(o.dtype)
            else:
                first = (i == 0) if params[k][3] else ((i == 0) & (j == 0))

                @pl.when(first)
                def _(o=o):
                    o[...] = jnp.zeros(o.shape, F32)
                o[...] += grads[nr + k]

    return pl.pallas_call(
        body, name=name, grid=(nj, T // tm), in_specs=in_specs, out_specs=out_specs, out_shape=out_shape,
        compiler_params=_params(("arbitrary", "arbitrary")),
    )(*args)


def _rms(x, g):
    return x * lax.rsqrt(jnp.mean(x * x, axis=-1, keepdims=True) + EPS) * g


def _rms_fn(x, g):
    return (_rms(x, g),)


def _lo_mask():
    return lax.broadcasted_iota(jnp.int32, (1, LANES), 1) < HEAD


def _gmean(x, lo):
    s0 = jnp.sum(jnp.where(lo, x, 0.0), axis=-1, keepdims=True)
    s1 = jnp.sum(jnp.where(lo, 0.0, x), axis=-1, keepdims=True)
    return jnp.where(lo, s0, s1) * (1.0 / HEAD)


def _fox_prep_fn(fq, fk, gq, gk):
    lo = _lo_mask()
    qn = fq * lax.rsqrt(_gmean(fq * fq, lo) + EPS) * gq * (HEAD ** -0.5)
    kn = fk * lax.rsqrt(_gmean(fk * fk, lo) + EPS) * gk
    return qn, kn


@jax.custom_vjp
def _swap_halves(x):
    bit = (lax.broadcasted_iota(jnp.int32, (1, LANES), 1) & (HEAD // 2)) == 0
    return jnp.where(bit, pltpu.roll(x, LANES - HEAD // 2, 1), pltpu.roll(x, HEAD // 2, 1))


_swap_halves.defvjp(lambda x: (_swap_halves(x), None), lambda _, g: (_swap_halves(g),))


def _ret_fn(rq, rk, rv, rg, cos, sin, s_in, g, lg):
    tb = rq.shape[0]
    nc = tb // CHUNK
    lo = _lo_mask()
    row = lax.broadcasted_iota(jnp.int32, (LANES, 1), 0) < HEAD
    same_head = row == lo
    q = (rq * cos + _swap_halves(rq) * sin) * (HEAD ** -0.5)
    k = rk * cos + _swap_halves(rk) * sin
    q3, k3, v3 = q.reshape(nc, CHUNK, LANES), k.reshape(nc, CHUNK, LANES), rv.reshape(nc, CHUNK, LANES)
    pos = lax.broadcasted_iota(jnp.int32, (CHUNK, 1), 0).astype(F32)
    q_decay = jnp.exp(lg * (pos + 1.0))
    k_decay = jnp.exp(lg * (CHUNK - 1.0 - pos))
    chunk_decay = jnp.exp(lg * float(CHUNK))
    dist = jnp.abs(lax.broadcasted_iota(jnp.int32, (CHUNK, CHUNK), 0) - lax.broadcasted_iota(jnp.int32, (CHUNK, CHUNK), 1)).astype(F32)
    v3b = v3.astype(BF)
    intra = []
    for hh in range(2):
        hm = lo if hh == 0 else ~lo
        lg_h = lg[:, hh * HEAD:hh * HEAD + 1]
        qm = jnp.where(hm, q3, 0.0).astype(BF)
        sc = jnp.einsum("nid,njd->nij", qm, k3.astype(BF), preferred_element_type=F32) * jnp.exp(lg_h * dist)[None]
        intra.append(jnp.einsum("nij,nje->nie", sc.astype(BF), v3b, preferred_element_type=F32))
    o = jnp.where(lo, intra[0], intra[1])
    kv = jnp.einsum("njd,nje->nde", (k3 * k_decay[None]).astype(BF), v3b, preferred_element_type=F32)
    kv = jnp.where(same_head[None], kv, 0.0)
    state, states = s_in, []
    for n in range(nc):
        states.append(state)
        state = state * chunk_decay + kv[n]
    s_prev = jnp.stack(states, axis=0)
    o = o + jnp.einsum("nid,nde->nie", (q3 * q_decay[None]).astype(BF), s_prev.astype(BF), preferred_element_type=F32)
    o = o.reshape(tb, LANES)
    mu = _gmean(o, lo)
    oc = o - mu
    y = oc * lax.rsqrt(_gmean(oc * oc, lo) + EPS) * g
    return jax.nn.silu(rg) * y, state


def _xattn_fn(qx, gq, gk, kk, vv):
    q = _rms(qx, gq)
    k = _rms(kk, gk)
    logits = lax.dot_general(q.astype(BF), k.astype(BF), NT, preferred_element_type=F32) * (XHEAD ** -0.5)
    p = jax.nn.softmax(logits, axis=-1)
    return (jnp.dot(p.astype(BF), vv.astype(BF), preferred_element_type=F32),)


def _swiglu_fwd_epi(g, u):
    return g, u, jax.nn.silu(g) * u


def _swiglu_bwd_epi(dact, g, u):
    _, vjp = jax.vjp(lambda a, b: jax.nn.silu(a) * b, g, u)
    return vjp(dact)


def _add_rms_epi(acc, resid, g):
    h = acc + resid
    return h, _rms(h, g)


def _add_loss_epi(acc, resid, target):
    err = (acc + resid) - target
    dy = err * (1.0 / D)
    part = jnp.sum(jnp.sum(err * err, axis=0, keepdims=True), axis=1, keepdims=True) * (0.5 / D)
    return dy, dy, jnp.broadcast_to(part, (1, err.shape[1]))


def _rms_bwd_epi(dhn, h, skip, g):
    _, vjp = jax.vjp(_rms, h, g)
    dh, dg = vjp(dhn)
    dh = dh + skip
    return dh, dh, dg


def _rms_bwd_first_epi(dhn, h, skip, g):
    return _rms_bwd_epi(dhn, h, skip, g)[1:]


def _ret_fwd(P, cos, sin, g_ret, lg, T, tb):
    nb = T // tb

    def body(rq, rk, rv, rg, c, s, g, l, o_ref, s0_ref, state):
        @pl.when(pl.program_id(1) == 0)
        def _():
            state[...] = jnp.zeros(state.shape, F32)
        s0_ref[0, 0] = state[...]
        out, s_new = _ret_fn(rq[...], rk[...], rv[...], rg[...], c[...], s[...], state[...], g[...], l[...])
        o_ref[...] = out.astype(o_ref.dtype)
        state[...] = s_new

    sec = lambda off: pl.BlockSpec((tb, LANES), lambda j, i: (i, off + j))
    tab = pl.BlockSpec((tb, LANES), lambda j, i: (i, 0))
    par = pl.BlockSpec((1, LANES), lambda j, i: (0, j))
    return pl.pallas_call(
        body, name="ret_fwd", grid=(4, nb),
        in_specs=[sec(0), sec(4), sec(8), sec(12), tab, tab, par, par],
        out_specs=[pl.BlockSpec((tb, LANES), lambda j, i: (i, j)), pl.BlockSpec((1, 1, LANES, LANES), lambda j, i: (j, i, 0, 0))],
        out_shape=[jax.ShapeDtypeStruct((T, 4 * LANES), BF), jax.ShapeDtypeStruct((4, nb, LANES, LANES), F32)],
        scratch_shapes=[pltpu.VMEM((LANES, LANES), F32)],
        compiler_params=_params(("arbitrary", "arbitrary")),
    )(P, P, P, P, cos, sin, g_ret, lg)


def _ret_bwd(P, cos, sin, g_ret, lg, s0, dmix, T, tb):
    nb = T // tb

    def body(rq, rk, rv, rg, c, s, g, l, s0_ref, do, drq, drk, drv, drg, dg, dstate):
        i = pl.program_id(1)

        @pl.when(i == 0)
        def _():
            dstate[...] = jnp.zeros(dstate.shape, F32)
            dg[...] = jnp.zeros(dg.shape, F32)

        cc, ss, ll = c[...], s[...], l[...]
        _, vjp = jax.vjp(lambda a, b, v, gate, st, gg: _ret_fn(a, b, v, gate, cc, ss, st, gg, ll),
                         rq[...], rk[...], rv[...], rg[...], s0_ref[0, 0], g[...])
        ga, gb, gv, ggate, gst, ggain = vjp((do[...], dstate[...]))
        drq[...] = ga.astype(drq.dtype)
        drk[...] = gb.astype(drk.dtype)
        drv[...] = gv.astype(drv.dtype)
        drg[...] = ggate.astype(drg.dtype)
        dstate[...] = gst
        dg[...] += ggain

    rev = lambda i: nb - 1 - i
    sec = lambda off: pl.BlockSpec((tb, LANES), lambda j, i: (rev(i), off + j))
    tab = pl.BlockSpec((tb, LANES), lambda j, i: (rev(i), 0))
    par = pl.BlockSpec((1, LANES), lambda j, i: (0, j))
    outb = pl.BlockSpec((tb, LANES), lambda j, i: (rev(i), j))
    return pl.pallas_call(
        body, name="ret_bwd", grid=(4, nb),
        in_specs=[sec(0), sec(4), sec(8), sec(12), tab, tab, par, par,
                  pl.BlockSpec((1, 1, LANES, LANES), lambda j, i: (j, rev(i), 0, 0)), outb],
        out_specs=[outb, outb, outb, outb, par],
        out_shape=[jax.ShapeDtypeStruct((T, 4 * LANES), BF)] * 4 + [jax.ShapeDtypeStruct((1, 4 * LANES), F32)],
        scratch_shapes=[pltpu.VMEM((LANES, LANES), F32)],
        compiler_params=_params(("arbitrary", "arbitrary")),
    )(P, P, P, P, cos, sin, g_ret, lg, s0, dmix)


_FB = 128


def _tri(lower):
    r = lax.broadcasted_iota(jnp.int32, (_FB, _FB), 0)
    c = lax.broadcasted_iota(jnp.int32, (_FB, _FB), 1)
    return ((r >= c) if lower else (r <= c)).astype(F32)


def _fgate_fwd(ffp, bpad, T):
    def body(ff_ref, b_ref, fc_ref, fr_ref):
        lane = lax.broadcasted_iota(jnp.int32, (1, LANES), 1)
        tri = _tri(True)
        carry = jnp.zeros((1, LANES), F32)
        for blk in range(T // _FB):
            z = ff_ref[blk * _FB:(blk + 1) * _FB, :] + b_ref[...]
            lf = jnp.where(lane < 8, jax.nn.log_sigmoid(z), 0.0)
            f = jnp.dot(tri, lf, precision=lax.Precision.HIGHEST, preferred_element_type=F32) + carry
            carry = f[_FB - 1:_FB, :]
            fc_ref[blk * _FB:(blk + 1) * _FB, :] = f
            fr_ref[:, blk * _FB:(blk + 1) * _FB] = f.T[:8, :]

    return pl.pallas_call(
        body, name="fgate_fwd",
        out_shape=[jax.ShapeDtypeStruct((T, LANES), F32), jax.ShapeDtypeStruct((8, T), F32)],
        compiler_params=pltpu.CompilerParams(vmem_limit_bytes=VMEM_LIMIT),
    )(ffp, bpad)


_BIAS_LANE = HEAD


def _head_bias_col(fc, head):
    lane = lax.broadcasted_iota(jnp.int32, (1, LANES), 1)
    return jnp.sum(jnp.where(lane == head, fc, 0.0), axis=-1, keepdims=True)


def _split3(f):
    hi = f.astype(BF).astype(F32)
    mid = (f - hi).astype(BF).astype(F32)
    lo = ((f - hi) - mid).astype(BF).astype(F32)
    return hi, mid, lo


def _fox_operands(P, fc, g_fq2, g_fk2, T, tm):
    def body(fq_ref, fk_ref, fv_ref, fc_ref, gq_ref, gk_ref, qat_ref, ka_ref, kat_ref, va_ref, vat_ref):
        j = pl.program_id(0)
        lane = lax.broadcasted_iota(jnp.int32, (1, LANES), 1)
        qn, kn = _fox_prep_fn(fq_ref[...], fk_ref[...], gq_ref[...], gk_ref[...])
        v = fv_ref[...]
        fcb = fc_ref[...]
        b = _BIAS_LANE
        for hh in range(2):
            hi, mid, lo = _split3(_head_bias_col(fcb, 2 * j + hh))
            take = (lambda a: a) if hh == 0 else (lambda a: pltpu.roll(a, HEAD, 1))
            qa = jnp.where(lane < HEAD, take(qn), jnp.where(lane == b, hi, jnp.where(lane == b + 1, mid, jnp.where(
                lane == b + 2, lo, jnp.where(lane < b + 6, 1.0, 0.0)))))
            ka = jnp.where(lane < HEAD, take(kn), jnp.where(lane < b + 3, 1.0, jnp.where(lane == b + 3, -hi, jnp.where(
                lane == b + 4, -mid, jnp.where(lane == b + 5, -lo, 0.0)))))
            va = jnp.where(lane < HEAD, take(v), 0.0)
            qat_ref[hh] = qa.T.astype(BF)
            for val, ref, tref in ((ka, ka_ref, kat_ref), (va, va_ref, vat_ref)):
                ref[hh] = val.astype(BF)
                tref[hh] = val.T.astype(BF)

    sec = lambda off: pl.BlockSpec((tm, LANES), lambda j, i: (i, off + j))
    par = pl.BlockSpec((1, LANES), lambda j, i: (0, 0))
    nat = pl.BlockSpec((2, tm, LANES), lambda j, i: (j, i, 0))
    trn = pl.BlockSpec((2, LANES, tm), lambda j, i: (j, 0, i))
    return pl.pallas_call(
        body, name="fox_operands", grid=(4, T // tm),
        in_specs=[sec(16), sec(20), sec(24), pl.BlockSpec((tm, LANES), lambda j, i: (i, 0)), par, par],
        out_specs=[trn, nat, trn, nat, trn],
        out_shape=[jax.ShapeDtypeStruct((8, LANES, T), BF)]
        + [jax.ShapeDtypeStruct((8, T, LANES), BF), jax.ShapeDtypeStruct((8, LANES, T), BF)] * 2,
        compiler_params=_params(("parallel", "arbitrary")),
    )(P, P, P, fc, g_fq2, g_fk2)


def _fox_forward(qat, ka, vat, T, tq, tk):
    nq, per = T // tq, tq // tk
    assert per == 2
    RC = 64

    def body(qat_ref, ka_ref, vat_ref, o_ref, lse_ref, s_scr, p_scr, a_scr, m_scr, l_scr, acc_scr):
        i = pl.program_id(1)
        sub = lax.broadcasted_iota(jnp.int32, (8, 1), 0)
        row = lax.broadcasted_iota(jnp.int32, (RC, tq), 0)
        col = lax.broadcasted_iota(jnp.int32, (RC, tq), 1)
        m_scr[...] = jnp.full(m_scr.shape, NEG, F32)
        l_scr[...] = jnp.zeros(l_scr.shape, F32)
        acc_scr[...] = jnp.zeros(acc_scr.shape, F32)

        def scores(slot, kb):
            k0 = pl.multiple_of(kb * tk, tk)
            for hh in range(2):
                s_scr[slot, hh] = jnp.dot(ka_ref[hh, pl.ds(k0, tk), :], qat_ref[hh], preferred_element_type=F32)

        def softmax(slot, kb, diagonal):
            shift = kb * tk - i * tq
            for hh in range(2):
                def masked(r):
                    tile = s_scr[slot, hh, r * RC:(r + 1) * RC, :]
                    return jnp.where(row + (r * RC + shift) <= col, tile, NEG) if diagonal else tile

                mx = jnp.max(masked(0), axis=0, keepdims=True)
                for r in range(1, tk // RC):
                    mx = jnp.maximum(mx, jnp.max(masked(r), axis=0, keepdims=True))
                m_old = m_scr[hh, 0:1, :]
                m2 = jnp.maximum(m_old, mx)
                a = jnp.exp(m_old - m2)
                lsum = jnp.zeros((1, tq), F32)
                for r in range(tk // RC):
                    p = jnp.exp(masked(r) - m2)
                    p_scr[slot, hh, r * RC:(r + 1) * RC, :] = p.astype(BF)
                    lsum = lsum + jnp.sum(p, axis=0, keepdims=True)
                m_scr[hh] = jnp.broadcast_to(m2, (8, tq))
                l_scr[hh] = jnp.broadcast_to(a * l_scr[hh, 0:1, :] + lsum, (8, tq))
                a_scr[slot, hh] = jnp.broadcast_to(a, (8, tq))

        def values(slot, kb):
            k0 = pl.multiple_of(kb * tk, tk)
            for hh in range(2):
                pv = jnp.dot(vat_ref[hh, 0:HEAD, pl.ds(k0, tk)], p_scr[slot, hh], preferred_element_type=F32)
                acc_scr[hh] = a_scr[slot, hh, 0:1, :] * acc_scr[hh] + pv

        def pair(kb, diag_first, diag_second, more):
            if more:
                scores(0, kb + 2)
            softmax(1, kb + 1, diag_first)
            values(0, kb)
            if more:
                scores(1, kb + 3)
                softmax(0, kb + 2, diag_second)
            values(1, kb + 1)

        scores(0, 0)
        scores(1, 1)
        softmax(0, 0, True)

        @pl.loop(0, jnp.maximum(i - 1, 0))
        def _(t):
            pair(2 * t, False, False, True)

        @pl.when(i >= 1)
        def _():
            pair(2 * (i - 1), False, True, True)

        pair(2 * i, True, False, False)

        o_ref[...] = jnp.concatenate([acc_scr[hh] / l_scr[hh, 0:1, :] for hh in range(2)], axis=0).T
        lses = [m_scr[hh, 0:1, :] + jnp.log(l_scr[hh, 0:1, :]) for hh in range(2)]
        lse_ref[0] = jnp.where(sub == 0, lses[0], jnp.where(sub == 1, lses[1], 0.0))

    return pl.pallas_call(
        body, name="fox_forward", grid=(4, nq),
        in_specs=[pl.BlockSpec((2, LANES, tq), lambda j, i: (j, 0, i)), pl.BlockSpec((2, T, LANES), lambda j, i: (j, 0, 0)),
                  pl.BlockSpec((2, LANES, T), lambda j, i: (j, 0, 0))],
        out_specs=[pl.BlockSpec((tq, LANES), lambda j, i: (i, j)), pl.BlockSpec((1, 8, tq), lambda j, i: (j, 0, i))],
        out_shape=[jax.ShapeDtypeStruct((T, 4 * LANES), F32), jax.ShapeDtypeStruct((4, 8, T), F32)],
        scratch_shapes=[pltpu.VMEM((2, 2, tk, tq), F32), pltpu.VMEM((2, 2, tk, tq), BF), pltpu.VMEM((2, 2, 8, tq), F32),
                        pltpu.VMEM((2, 8, tq), F32), pltpu.VMEM((2, 8, tq), F32), pltpu.VMEM((2, HEAD, tq), F32)],
        compiler_params=_params(("parallel", "arbitrary")),
    )(qat, ka, vat)


def _fox_cotangent(dmix, fox, T, tm):
    def body(do_ref, o_ref, doat_ref, dl_ref):
        lane = lax.broadcasted_iota(jnp.int32, (1, LANES), 1)
        sub = lax.broadcasted_iota(jnp.int32, (8, 1), 0)
        dob = do_ref[...].astype(BF).astype(F32)
        prod_t = (dob * o_ref[...]).T
        d0 = jnp.sum(prod_t[:HEAD], axis=0, keepdims=True)
        d1 = jnp.sum(prod_t[HEAD:], axis=0, keepdims=True)
        dl_ref[0] = jnp.where(sub == 0, d0, jnp.where(sub == 1, d1, 0.0))
        for hh in range(2):
            val = jnp.where(lane < HEAD, dob if hh == 0 else pltpu.roll(dob, HEAD, 1), 0.0)
            doat_ref[hh] = val.T.astype(BF)

    return pl.pallas_call(
        body, name="fox_cotangent", grid=(4, T // tm),
        in_specs=[pl.BlockSpec((tm, LANES), lambda j, i: (i, 4 + j)), pl.BlockSpec((tm, LANES), lambda j, i: (i, j))],
        out_specs=[pl.BlockSpec((2, LANES, tm), lambda j, i: (j, 0, i)), pl.BlockSpec((1, 8, tm), lambda j, i: (j, 0, i))],
        out_shape=[jax.ShapeDtypeStruct((8, LANES, T), BF), jax.ShapeDtypeStruct((4, 8, T), F32)],
        compiler_params=_params(("parallel", "arbitrary")),
    )(dmix, fox)


def _fox_backward(qat, ka, kat, va, doat, lse, dl, T, tq, tk):
    nq, nk = T // tq, T // tk

    def body(qat_ref, ka_ref, kat_ref, va_ref, doat_ref, lse_ref, dl_ref,
             dq_ref, dk_ref, dv_ref, df_ref, dr_ref, dqt, dkt, dvt, df_acc, sdp, pds):
        j, kb = pl.program_id(0), pl.program_id(1)
        lane = lax.broadcasted_iota(jnp.int32, (1, LANES), 1)
        first = (kb * tk) // tq

        @pl.when(kb == 0)
        def _():
            dqt[...] = jnp.zeros(dqt.shape, F32)

        dkt[...] = jnp.zeros(dkt.shape, F32)
        dvt[...] = jnp.zeros(dvt.shape, F32)
        df_acc[...] = jnp.zeros(df_acc.shape, F32)

        RC = 64
        last = nq - 1

        def products(slot, qi):
            q0 = pl.multiple_of(qi * tq, tq)
            for hh in range(2):
                sdp[slot, hh, 0] = jnp.dot(ka_ref[hh], qat_ref[hh, :, pl.ds(q0, tq)], preferred_element_type=F32)
                sdp[slot, hh, 1] = jnp.dot(va_ref[hh], doat_ref[hh, :, pl.ds(q0, tq)], preferred_element_type=F32)

        def softmax_bwd(slot, qi, diagonal, valid):
            q0 = pl.multiple_of(qi * tq, tq)
            shift = kb * tk - first * tq
            col = lax.broadcasted_iota(jnp.int32, (RC, tq), 1)
            row = lax.broadcasted_iota(jnp.int32, (RC, tq), 0)
            for hh in range(2):
                lse_row = lse_ref[0, hh:hh + 1, pl.ds(q0, tq)]
                dl_row = dl_ref[0, hh:hh + 1, pl.ds(q0, tq)]
                rsum = jnp.zeros((1, tq), F32)
                for r in range(tk // RC):
                    rows = slice(r * RC, (r + 1) * RC)
                    p = jnp.exp(sdp[slot, hh, 0, rows, :] - lse_row)
                    p = jnp.where((row + (r * RC + shift) <= col) if diagonal else valid, p, 0.0)
                    ds = p * (sdp[slot, hh, 1, rows, :] - dl_row)
                    pds[slot, hh, 0, rows, :] = p.astype(BF)
                    pds[slot, hh, 1, rows, :] = ds.astype(BF)
                    rsum = rsum + jnp.sum(ds, axis=0, keepdims=True)
                    part = ds[:, 0:LANES]
                    for c in range(1, tq // LANES):
                        part = part + ds[:, c * LANES:(c + 1) * LANES]
                    df_acc[hh, rows, :] += part
                dqt[hh, HEAD:HEAD + 8, pl.ds(q0, tq)] += jnp.broadcast_to(rsum, (8, tq))

        def accumulate(slot, qi):
            q0 = pl.multiple_of(qi * tq, tq)
            for hh in range(2):
                dvt[hh] += lax.dot_general(doat_ref[hh, 0:HEAD, pl.ds(q0, tq)], pds[slot, hh, 0], NT, preferred_element_type=F32)
                dkt[hh] += lax.dot_general(qat_ref[hh, 0:HEAD, pl.ds(q0, tq)], pds[slot, hh, 1], NT, preferred_element_type=F32)
                dqt[hh, 0:HEAD, pl.ds(q0, tq)] += jnp.dot(kat_ref[hh, 0:HEAD, :], pds[slot, hh, 1], preferred_element_type=F32)

        products(0, first)
        products(1, jnp.minimum(first + 1, last))
        softmax_bwd(0, first, True, None)

        @pl.loop(0, (nq - first + 1) // 2)
        def _(t):
            qi = first + 2 * t
            products(0, jnp.minimum(qi + 2, last))
            softmax_bwd(1, jnp.minimum(qi + 1, last), False, qi + 1 <= last)
            accumulate(0, qi)
            products(1, jnp.minimum(qi + 3, last))
            softmax_bwd(0, jnp.minimum(qi + 2, last), False, qi + 2 <= last)
            accumulate(1, jnp.minimum(qi + 1, last))

        dk_ref[...] = jnp.concatenate([dkt[0], dkt[1]], axis=0).T
        dv_ref[...] = jnp.concatenate([dvt[0], dvt[1]], axis=0).T.astype(dv_ref.dtype)
        f0 = -jnp.sum(df_acc[0], axis=1, keepdims=True)
        f1 = -jnp.sum(df_acc[1], axis=1, keepdims=True)
        df_ref[0] = jnp.where(lane == 2 * j, f0, jnp.where(lane == 2 * j + 1, f1, 0.0))

        @pl.when(kb == nk - 1)
        def _():
            for t in range(nq):
                cols = slice(t * tq, (t + 1) * tq)
                dq_ref[cols, :] = jnp.concatenate([dqt[0, 0:HEAD, cols], dqt[1, 0:HEAD, cols]], axis=0).T
                rsum = jnp.concatenate([dqt[0, HEAD:HEAD + 8, cols], dqt[1, HEAD:HEAD + 8, cols],
                                        jnp.zeros((LANES - 16, tq), F32)], axis=0).T
                dr_ref[0, cols, :] = jnp.where(lane == 2 * j, rsum[:, 0:1], jnp.where(lane == 2 * j + 1, rsum[:, 8:9], 0.0))

    trn_full = pl.BlockSpec((2, LANES, T), lambda j, kb: (j, 0, 0))
    nat_blk = pl.BlockSpec((2, tk, LANES), lambda j, kb: (j, kb, 0))
    trn_blk = pl.BlockSpec((2, LANES, tk), lambda j, kb: (j, 0, kb))
    rows = pl.BlockSpec((1, 8, T), lambda j, kb: (j, 0, 0))
    blk = pl.BlockSpec((tk, LANES), lambda j, kb: (kb, j))
    return pl.pallas_call(
        body, name="fox_backward", grid=(4, nk),
        in_specs=[trn_full, nat_blk, trn_blk, nat_blk, trn_full, rows, rows],
        out_specs=[pl.BlockSpec((T, LANES), lambda j, kb: (0, j)), blk, blk, pl.BlockSpec((1, tk, LANES), lambda j, kb: (j, kb, 0)),
                   pl.BlockSpec((1, T, LANES), lambda j, kb: (j, 0, 0))],
        out_shape=[jax.ShapeDtypeStruct((T, 4 * LANES), F32), jax.ShapeDtypeStruct((T, 4 * LANES), F32),
                   jax.ShapeDtypeStruct((T, 4 * LANES), BF), jax.ShapeDtypeStruct((4, T, LANES), F32),
                   jax.ShapeDtypeStruct((4, T, LANES), F32)],
        scratch_shapes=[pltpu.VMEM((2, HEAD + 8, T), F32), pltpu.VMEM((2, HEAD, tk), F32), pltpu.VMEM((2, HEAD, tk), F32),
                        pltpu.VMEM((2, tk, LANES), F32), pltpu.VMEM((2, 2, 2, tk, tq), F32), pltpu.VMEM((2, 2, 2, tk, tq), BF)],
        compiler_params=_params(("arbitrary", "arbitrary")),
    )(qat, ka, kat, va, doat, lse, dl)


def _fgate_bwd_col(ffp, bpad, dfc4, drc4, T):
    def body(ff_ref, b_ref, dfc_ref, drc_ref, dff_ref, db_ref):
        lane = lax.broadcasted_iota(jnp.int32, (1, LANES), 1)
        tri = _tri(False)
        carry = jnp.zeros((1, LANES), F32)
        db = jnp.zeros((1, LANES), F32)
        for blk in reversed(range(T // _FB)):
            rows = slice(blk * _FB, (blk + 1) * _FB)
            dcol = dfc_ref[0, rows, :] + drc_ref[0, rows, :]
            for pair in range(1, 4):
                dcol = dcol + (dfc_ref[pair, rows, :] + drc_ref[pair, rows, :])
            dlf = jnp.dot(tri, dcol, precision=lax.Precision.HIGHEST, preferred_element_type=F32) + carry
            carry = dlf[0:1, :]
            z = ff_ref[blk * _FB:(blk + 1) * _FB, :] + b_ref[...]
            dz = jnp.where(lane < 8, dlf * jax.nn.sigmoid(-z), 0.0)
            dff_ref[blk * _FB:(blk + 1) * _FB, :] = dz.astype(dff_ref.dtype)
            db = db + jnp.sum(dz, axis=0, keepdims=True)
        db_ref[...] = db

    return pl.pallas_call(
        body, name="fgate_bwd",
        out_shape=[jax.ShapeDtypeStruct((T, LANES), BF), jax.ShapeDtypeStruct((1, LANES), F32)],
        compiler_params=pltpu.CompilerParams(vmem_limit_bytes=VMEM_LIMIT),
    )(ffp, bpad, dfc4, drc4)


MESH = pl.DeviceIdType.MESH
N_PEERS = N_DEV - 1


def _place():
    return lax.axis_index("x"), lax.axis_index("y"), lax.axis_index("c")


def _all_gather(shard, rows, g, tm):
    R, W = shard.shape
    T = rows.shape[0]
    steps = T // tm

    def body(w_ref, rows_ref, g_ref, out_ref, norm_ref, send_sems, recv_sems, local_sem):
        x, y, c = _place()
        me, sibling = (x, y, c), (x, y, 1 - c)
        chips = [(1 - x, y), (x, 1 - y), (1 - x, 1 - y)]

        def slot(px, py, pc):
            return out_ref.at[4 * px + 2 * py + pc]

        def copy(k, block, to, src=None):
            return pltpu.make_async_remote_copy(
                src_ref=slot(*block) if src is None else src, dst_ref=slot(*block),
                send_sem=send_sems.at[k], recv_sem=recv_sems.at[k], device_id=to, device_id_type=MESH)

        mine = pltpu.make_async_copy(w_ref, slot(*me), local_sem)
        first = [copy(0, me, sibling, src=w_ref)]
        first += [copy(1 + n, me, (*chip, c), src=w_ref) for n, chip in enumerate(chips)]
        passed = [copy(4 + n, (*chip, c), sibling) for n, chip in enumerate(chips)]

        @pl.when(pl.program_id(0) == 0)
        def _():
            mine.start()
            for cp in first:
                cp.start()

        norm_ref[...] = _rms(rows_ref[...], g_ref[...]).astype(norm_ref.dtype)

        @pl.when(pl.program_id(0) == steps - 1)
        def _():
            for n, chip in enumerate(chips):
                copy(1 + n, (*chip, c), me).wait_recv()
                passed[n].start()
            copy(0, sibling, me).wait_recv()
            for n, chip in enumerate(chips):
                copy(4 + n, (*chip, 1 - c), me).wait_recv()
            for cp in first + passed:
                cp.wait_send()
            mine.wait()

    tile = pl.BlockSpec((tm, D), lambda i: (i, 0))
    return pl.pallas_call(
        body, name="all_gather_weights", grid=(steps,),
        out_shape=[jax.ShapeDtypeStruct((N_DEV, R, W), shard.dtype), jax.ShapeDtypeStruct((T, D), BF)],
        in_specs=[pl.BlockSpec(memory_space=pl.ANY), tile, pl.BlockSpec((1, D), lambda i: (0, 0))],
        out_specs=[pl.BlockSpec(memory_space=pl.ANY), tile],
        scratch_shapes=[pltpu.SemaphoreType.DMA((N_PEERS,)), pltpu.SemaphoreType.DMA((N_PEERS,)), pltpu.SemaphoreType.DMA],
        compiler_params=_params(("arbitrary",)),
    )(shard, rows, g)


def _exchange_copies(src_refs, land_refs, send_sems, recv_sems, scatter):
    x, y, c = _place()
    me = 4 * x + 2 * y + c
    copies = []
    for k, (src_ref, land_ref) in enumerate(zip(src_refs, land_refs)):
        for r in range(1, N_DEV):
            px, py, pc = x ^ (r >> 2), y ^ ((r >> 1) & 1), c ^ (r & 1)
            copies.append(pltpu.make_async_remote_copy(
                src_ref=src_ref.at[4 * px + 2 * py + pc] if scatter else src_ref, dst_ref=land_ref.at[me],
                send_sem=send_sems.at[k * N_PEERS + r - 1], recv_sem=recv_sems.at[k * N_PEERS + r - 1],
                device_id=(px, py, pc), device_id_type=MESH))
    return copies


_HBM = pl.BlockSpec(memory_space=pltpu.HBM)
_SEM = pl.BlockSpec(memory_space=pltpu.SEMAPHORE)
_EFFECT = pltpu.SideEffectType.DATAFLOW_SIDE_EFFECTING


def _exchange_start(name, srcs, lands, scatter):
    n = len(srcs)

    def body(*refs):
        send_sems, recv_sems = refs[2 * n], refs[2 * n + 1]
        for cp in _exchange_copies(refs[:n], refs[n:2 * n], send_sems, recv_sems, scatter):
            cp.start()
        token = refs[-1]
        token[...] = jnp.zeros(token.shape, F32)

    arrays = list(srcs) + list(lands)
    out = pl.pallas_call(
        body, name=name,
        out_shape=(pltpu.SemaphoreType.DMA((n * N_PEERS,)), pltpu.SemaphoreType.DMA((n * N_PEERS,)))
        + tuple(pltpu.HBM(a.shape, a.dtype) for a in arrays) + (jax.ShapeDtypeStruct((8, LANES), F32),),
        in_specs=(_HBM,) * (2 * n), out_specs=(_SEM, _SEM) + (_HBM,) * (2 * n) + (pl.BlockSpec(memory_space=pltpu.VMEM),),
        input_output_aliases={k: 2 + k for k in range(2 * n)},
        compiler_params=pltpu.CompilerParams(has_side_effects=_EFFECT),
    )(*(pltpu.with_memory_space_constraint(a, pltpu.HBM) for a in arrays))
    return out[0], out[1], out[2:2 + n], out[2 + n:2 + 2 * n], out[-1]


def _exchange_wait(name, started, after, scatter):
    send_sems, recv_sems, srcs, lands, _ = started
    n = len(srcs)

    def body(*refs):
        copies = _exchange_copies(refs[:n], refs[n:2 * n], refs[2 * n], refs[2 * n + 1], scatter)
        for cp in copies:
            cp.wait_send()
        for cp in copies:
            cp.wait_recv()

    arrays = list(srcs) + list(lands)
    out = pl.pallas_call(
        body, name=name,
        out_shape=tuple(pltpu.HBM(a.shape, a.dtype) for a in arrays),
        in_specs=(_HBM,) * (2 * n) + (_SEM, _SEM, pl.BlockSpec(memory_space=pl.ANY)), out_specs=(_HBM,) * (2 * n),
        input_output_aliases={k: k for k in range(2 * n)},
        compiler_params=pltpu.CompilerParams(has_side_effects=_EFFECT),
    )(*arrays, send_sems, recv_sems, after)
    return out[:n], out[n:]


def _adam_update(g, w, m, v):
    m2 = ADAM_B1 * m + (1.0 - ADAM_B1) * g
    v2 = ADAM_B2 * v + (1.0 - ADAM_B2) * jnp.square(g)
    m_hat = m2 / (1.0 - ADAM_B1 ** ADAM_STEP)
    v_hat = v2 / (1.0 - ADAM_B2 ** ADAM_STEP)
    return g, -ADAM_LR * (m_hat / (jnp.sqrt(v_hat) + ADAM_EPS) + ADAM_WD * w), m2, v2


def _adamw(name, me, slots, sent, w, m, v):
    R, W = w.shape
    steps = max(k for k in (4, 2, 1) if k == 1 or (R % k == 0 and (R // k) % 16 == 0))
    tr = R // steps

    def body(me_ref, s_ref, *refs):
        if sent is not None:
            g = refs[0][0].astype(F32)
            refs = refs[1:]
        else:
            g = jnp.zeros((tr, W), F32)
        for s in range(N_DEV):
            part = s_ref[s].astype(F32)
            g = g + (part if sent is None else jnp.where(me_ref[0] == s, 0.0, part))
        w_ref, m_ref, v_ref = refs[:3]
        for o, r in zip(refs[3:], _adam_update(g, w_ref[...], m_ref[...], v_ref[...])):
            o[...] = r

    rows = pl.BlockSpec((tr, W), lambda i, me_ref: (i, 0))
    in_specs = [pl.BlockSpec((N_DEV, tr, W), lambda i, me_ref: (0, i, 0))]
    args = [slots]
    if sent is not None:
        in_specs.append(pl.BlockSpec((1, tr, W), lambda i, me_ref: (me_ref[0], i, 0)))
        args.append(sent)
    return pl.pallas_call(
        body, name=name,
        grid_spec=pltpu.PrefetchScalarGridSpec(num_scalar_prefetch=1, grid=(steps,), in_specs=in_specs + [rows] * 3,
                                               out_specs=[rows] * 4),
        out_shape=[jax.ShapeDtypeStruct((R, W), F32)] * 4,
        compiler_params=_params(("arbitrary",)),
    )(me, *args, w, m, v)


def _tables(T):
    pos = jnp.arange(T, dtype=F32)
    inv_freq = 10000.0 ** (-jnp.arange(0, HEAD, 2, dtype=F32) / HEAD)
    ang = pos[:, None] * inv_freq[None, :]
    cos, sin = jnp.cos(ang), jnp.sin(ang)
    cos4 = jnp.tile(cos, (1, 4))
    sin4 = jnp.tile(jnp.concatenate([-sin, sin], axis=1), (1, 2))
    log_g = jnp.log(1.0 - 2.0 ** (-5.0 - jnp.arange(8, dtype=F32)))
    return cos4, sin4, jnp.repeat(log_g, HEAD)[None, :]


def _local_step(x, hn1, mem, target, sp, w_inT, token, fetch_rest, push, push_small):
    T = x.shape[0]
    tm = min(512, T)
    tq = min(256, T)
    tb = min(1024, T)
    cos4, sin4, lg = _tables(T)
    g_fq2 = jnp.tile(sp["g_fox_q"], (1, 2))
    g_fk2 = jnp.tile(sp["g_fox_k"], (1, 2))
    g_ret = sp["g_ret_out"].reshape(1, 8 * HEAD)
    bpad = jnp.pad(sp["b_forget"], ((0, 0), (0, LANES - 8)))
    w_secs = [w_inT[k * 512:(k + 1) * 512] for k in range(7)]
    w_ffT = jnp.pad(w_inT[3584:3592], ((0, LANES - 8), (0, 0)))
    w_mainT = w_inT[:3584]
    tie = lambda p, tok: p + tok[0:1, 0:1]
    tm2, tm4 = min(1024, T), min(2048, T)

    P, = _mm("proj_in", [[(hn1, w_mainT, "nt")]], [], lambda acc, after: (acc,), T, 3584, tm4, 512, [F32],
             params=[jnp.broadcast_to(token[0:1, 0:1], (1, 3584))])
    ffp, = _mm("proj_ff", [[(hn1, w_ffT, "nt")]], [], _ident, T, LANES, tm, LANES, [F32])
    ret, s0 = _ret_fwd(P, cos4, sin4, g_ret, lg, T, tb)
    fc, _ = _fgate_fwd(ffp, bpad, T)
    qat, ka, kat, va, vat = _fox_operands(P, fc, g_fq2, g_fk2, T, tm4)
    fox, lse = _fox_forward(qat, ka, vat, T, min(512, T), tq)
    W = fetch_rest("attn", fox)
    w_out_halves = (W["w_out"][:4 * LANES], W["w_out"][4 * LANES:])
    h1, hn2 = _mm("proj_out", [[(ret, w_out_halves[0], "nn"), (fox, w_out_halves[1], "nn")]], [x], _add_rms_epi, T, D, tm2, D,
                  [F32, BF], params=[sp["g_xattn"]])

    qx, = _mm("proj_xq", [[(hn2, W["w_xq"], "nn")]], [], _ident, T, D, tm2, D, [F32])
    memn, = _rw_fwd("rms_mem", _rms_fn, [(mem, D, 0, False)], [(sp["g_mem"], D, 0, False)], [(BF, D)], N_MEM, N_MEM, 1)
    kv, = _mm("proj_xkv", [[(memn, W["w_xkvT"], "nt")]], [], _ident, N_MEM, 2 * D, N_MEM, 512, [F32])
    xa_rows = [(qx, XHEAD, 0, True)]
    xa_params = [(sp["g_xq"], XHEAD, 0, False), (sp["g_xk"], XHEAD, 0, False), (kv, XHEAD, 0, True), (kv, XHEAD, 4, True)]
    xo, = _rw_fwd("xattn_fwd", _xattn_fn, xa_rows, xa_params, [(BF, XHEAD)], T, tm4, 4)
    h2, hn3 = _mm("proj_xo", [[(xo, W["w_xo"], "nn")]], [h1], _add_rms_epi, T, D, tm2, D, [F32, BF], params=[sp["g_ffn"]])

    W.update(fetch_rest("ffn", hn3))
    gate, up, act = _mm("ffn_in", [[(hn3, W["w_gateT"], "nt")], [(hn3, W["w_upT"], "nt")]], [], _swiglu_fwd_epi,
                        T, D_FF, tm4, 256, [BF, BF, BF])
    dy, dyb, loss_part = _mm("ffn_out", [[(act, W["w_down"], "nn")]], [h2, target], _add_loss_epi, T, D, tm, D, [F32, BF], n_acc=1)

    dgate, dup = _mm("ffn_out_bwd", [[(dyb, W["w_down"], "nt")]], [gate, up], _swiglu_bwd_epi, T, D_FF, tm4, 256, [BF, BF])
    gW = {}
    gW["w_gateT"], gW["w_upT"] = _mm("dw_gate_up", [[(dgate, hn3, "tn")], [(dup, hn3, "tn")]], [], _each, D_FF, D, 256, D, [BF, BF])
    gW["w_down"], = _mm("dw_down", [[(act, dyb, "tn")]], [], _ident, D_FF, D, 256, D, [BF])
    tok = push("ffn", gW)
    gs = {}
    dh2, dh2b, gs["g_ffn"] = _mm("ffn_in_bwd", [[(dgate, W["w_gateT"], "nn"), (dup, W["w_upT"], "nn")]], [h2, dy], _rms_bwd_epi,
                                 T, D, min(256, T), D, [F32, BF], params=[tie(sp["g_ffn"], tok)], n_acc=1)

    dxo, = _mm("proj_xo_bwd", [[(dh2b, W["w_xo"], "nt")]], [], _ident, T, D, tm2, D, [BF])
    gW["w_xo"], = _mm("dw_xo", [[(xo, dh2b, "tn")]], [], _ident, D, D, 256, D, [BF])
    dqx, gs["g_xq"], gs["g_xk"], dkv_k, dkv_v = _rw_bwd(
        "xattn_bwd", _xattn_fn, xa_rows, xa_params, [(dxo, XHEAD, 0, True)], T, tm4, 4, [BF], [True, True, True, True])
    dkv = jnp.concatenate([dkv_k[:, :D], dkv_v[:, D:]], axis=1)
    gW["w_xq"], = _mm("dw_xq", [[(hn2, dqx, "tn")]], [], _ident, D, D, 256, D, [BF])
    dmemn, = _mm("proj_xkv_bwd", [[(dkv, W["w_xkvT"], "nn")]], [], _ident, N_MEM, D, N_MEM, 512, [F32])
    gW["w_xkvT"], = _mm("dw_xkv", [[(dkv, memn, "tn")]], [], _ident, 2 * D, D, 512, D, [BF])
    tok = push("xattn", gW)
    gs["g_mem"], = _rw_bwd("rms_mem_bwd", _rms_fn, [(mem, D, 0, False)], [(sp["g_mem"], D, 0, False)], [(dmemn, D, 0, False)],
                           N_MEM, N_MEM, 1, [None], [True])
    dh1, dh1b, gs["g_xattn"] = _mm("proj_xq_bwd", [[(dqx, W["w_xq"], "nt")]], [h1, dh2], _rms_bwd_epi, T, D, tm, D, [F32, BF],
                                   params=[tie(sp["g_xattn"], tok)], n_acc=1)

    dmix, = _mm("proj_out_bwd", [[(dh1b, W["w_out"], "nt")]], [], _ident, T, D, tm2, D, [F32])
    gW["w_out"] = jnp.concatenate(_mm("dw_out", [[(ret, dh1b, "tn")], [(fox, dh1b, "tn")]], [], _each, 4 * LANES, D, 256, D,
                                      [BF, BF]), axis=0)
    tok = push("out", gW)
    doat, dl = _fox_cotangent(dmix, fox, T, tm4)
    dqn, dkn, dfv, dfc4, drc4 = _fox_backward(qat, ka, kat, va, doat, lse + tok[0:1, 0:1], dl, T, tq, tq)
    dfq, dfk, gq2, gk2 = _rw_bwd("fox_prep_bwd", _fox_prep_fn, [(P, LANES, 16, True), (P, LANES, 20, True)],
                                 [(g_fq2, LANES, 0, False), (g_fk2, LANES, 0, False)],
                                 [(dqn, LANES, 0, True), (dkn, LANES, 0, True)], T, tm4, 4, [BF, BF], [True, True])
    gs["g_fox_q"] = gq2[:, :HEAD] + gq2[:, HEAD:]
    gs["g_fox_k"] = gk2[:, :HEAD] + gk2[:, HEAD:]
    dff, dbp = _fgate_bwd_col(ffp, bpad, dfc4, drc4, T)
    gs["b_forget"] = dbp[:, :8]
    drq, drk, drv, drg, dg_ret = _ret_bwd(P, cos4, sin4, g_ret, lg, s0, dmix, T, tb)
    gs["g_ret_out"] = dg_ret
    dsecs = [drq, drk, drv, drg, dfq, dfk, dfv]
    g_secs = list(_mm("dw_in", [[(d, hn1, "tn")] for d in dsecs], [], _each, 512, D, LANES, D, [BF] * len(dsecs)))
    g_ff, = _mm("dw_in_ff", [[(dff, hn1, "tn")]], [], _ident, LANES, D, LANES, D, [BF])
    gW["w_inT"] = jnp.concatenate(g_secs + [g_ff[:8]], axis=0)
    tok = push("in", gW)
    grad_x, gs["g_mix"] = _mm("proj_in_bwd", [[(d, w, "nn") for d, w in zip(dsecs, w_secs)] + [(dff, w_ffT, "nn")]], [x, dh1],
                              _rms_bwd_first_epi, T, D, tm, D, [F32], params=[tie(sp["g_mix"], tok)], n_acc=1)
    return grad_x, push_small(gs, loss_part)


_CANON = {"w_in": "w_inT", "w_xkv": "w_xkvT", "w_gate": "w_gateT", "w_up": "w_upT"}
_SMALL = (("g_mix", 0, 0, 1024), ("g_xattn", 1, 0, 1024), ("g_mem", 2, 0, 1024), ("g_ffn", 3, 0, 1024),
          ("g_ret_out", 4, 0, 512), ("g_xq", 4, 512, 256), ("g_xk", 4, 768, 256),
          ("g_fox_q", 5, 0, 64), ("g_fox_k", 5, 64, 64), ("b_forget", 5, 128, 8))
_LOSS_AT = (5, 256)


def _pack_small(tree):
    buf = jnp.zeros((SMALL_ROWS, D), F32)
    for name, r, c, n in _SMALL:
        buf = lax.dynamic_update_slice(buf, tree[name].reshape(1, n).astype(F32), (r, c))
    return buf


def _unpack_small(buf, like):
    return {name: buf[r:r + 1, c:c + n].reshape(like[name].shape) for name, r, c, n in _SMALL}


def _canonical(tree, name):
    a = tree[name][0]
    return a.T if W_SHARD[name][1] else a


def _from_canonical(a, name):
    return (a.T if W_SHARD[name][1] else a)[None]


def kernel(x, mem, g_mix, w_in, b_forget, g_ret_out, g_fox_q, g_fox_k, w_out, g_xattn, w_xq, w_xkv, g_mem, g_xq, g_xk, w_xo, g_ffn, w_gate, w_up, w_down, loss_target, m_g_mix, m_w_in, m_b_forget, m_g_ret_out, m_g_fox_q, m_g_fox_k, m_w_out, m_g_xattn, m_w_xq, m_w_xkv, m_g_mem, m_g_xq, m_g_xk, m_w_xo, m_g_ffn, m_w_gate, m_w_up, m_w_down, v_g_mix, v_w_in, v_b_forget, v_g_ret_out, v_g_fox_q, v_g_fox_k, v_w_out, v_g_xattn, v_w_xq, v_w_xkv, v_g_mem, v_g_xq, v_g_xk, v_w_xo, v_g_ffn, v_w_gate, v_w_up, v_w_down):
    names = ("g_mix", "w_in", "b_forget", "g_ret_out", "g_fox_q", "g_fox_k", "w_out", "g_xattn", "w_xq", "w_xkv", "g_mem",
             "g_xq", "g_xk", "w_xo", "g_ffn", "w_gate", "w_up", "w_down")
    w = dict(zip(names, (g_mix, w_in, b_forget, g_ret_out, g_fox_q, g_fox_k, w_out, g_xattn, w_xq, w_xkv, g_mem, g_xq, g_xk,
                         w_xo, g_ffn, w_gate, w_up, w_down)))
    m = dict(zip(names, (m_g_mix, m_w_in, m_b_forget, m_g_ret_out, m_g_fox_q, m_g_fox_k, m_w_out, m_g_xattn, m_w_xq, m_w_xkv,
                         m_g_mem, m_g_xq, m_g_xk, m_w_xo, m_g_ffn, m_w_gate, m_w_up, m_w_down)))
    v = dict(zip(names, (v_g_mix, v_w_in, v_b_forget, v_g_ret_out, v_g_fox_q, v_g_fox_k, v_w_out, v_g_xattn, v_w_xq, v_w_xkv,
                         v_g_mem, v_g_xq, v_g_xk, v_w_xo, v_g_ffn, v_w_gate, v_w_up, v_w_down)))
    small_names = [s[0] for s in _SMALL]
    me = 4 * lax.axis_index("x") + 2 * lax.axis_index("y") + lax.axis_index("c")
    me1 = me.astype(jnp.int32).reshape(1)

    sp = {n: w[n].reshape(1, -1) for n in small_names}
    first, hn1 = _all_gather(_canonical(w, "w_in").astype(BF), x[0], sp["g_mix"], min(1024, x.shape[1]))
    first, rests = lax.optimization_barrier((first, {g: [_canonical(w, n).astype(BF) for n in ns] for g, ns in GATHER_REST.items()}))
    rest_started = {g: _exchange_start("gather_%s_start" % g, rests[g], [lax.empty((N_DEV,) + a.shape, BF) for a in rests[g]],
                                       scatter=False) for g in GATHER_REST}
    after = rest_started["attn"][4] + rest_started["ffn"][4]

    def fetch_rest(group, after):
        srcs, lands = _exchange_wait("gather_%s_wait" % group, rest_started[group], after, scatter=False)
        lands = [lax.dynamic_update_index_in_dim(a, own, me, axis=0) for a, own in zip(lands, srcs)]
        return {_CANON.get(n, n): a.reshape(N_DEV * a.shape[1], D) for n, a in zip(GATHER_REST[group], lands)}

    pushed = {}

    def push(group, grads):
        srcs = [grads[_CANON.get(n, n)].reshape(N_DEV, W_SHARD[n][0], D) for n in SCATTER_GROUPS[group]]
        pushed[group] = _exchange_start("scatter_%s_start" % group, srcs, [lax.empty(a.shape, BF) for a in srcs], scatter=True)
        return pushed[group][4]

    def push_small(gs, loss_part):
        small = lax.dynamic_update_slice(_pack_small(gs), loss_part[:, :1], _LOSS_AT)
        pushed["small"] = _exchange_start("gather_small_start", [small], [jnp.broadcast_to(small[None], (N_DEV,) + small.shape)],
                                          scatter=False)
        return pushed["small"][4]

    grad_x, done = _local_step(x[0], hn1, mem[0], loss_target[0], sp, first.reshape(N_DEV * W_SHARD["w_in"][0], D),
                               after, fetch_rest, push, push_small)

    results, after = {}, done
    for group in ("ffn", "xattn", "out", "small", "in"):
        if group == "small":
            recv_small = _exchange_wait("gather_small_wait", pushed["small"], after, scatter=False)[1][0]
            g_sm, d_sm, m_sm, v_sm = _adamw("adamw_small", me1, recv_small, None, _pack_small(w), _pack_small(m), _pack_small(v))
            after = g_sm
            continue
        sents, recvs = _exchange_wait("scatter_%s_wait" % group, pushed[group], after, scatter=True)
        for name, sent, recv in zip(SCATTER_GROUPS[group], sents, recvs):
            res = _adamw("adamw_" + name, me1, recv, sent, *(_canonical(t, name) for t in (w, m, v)))
            results[name] = [_from_canonical(r, name) for r in res]
        after = results[SCATTER_GROUPS[group][-1]][0]
    loss = g_sm[_LOSS_AT[0], _LOSS_AT[1]]

    outs = []
    for k, sm in enumerate((g_sm, d_sm, m_sm, v_sm)):
        tree = _unpack_small(sm, w)
        tree.update({name: res[k] for name, res in results.items()})
        outs += [tree[n] for n in names]
    return (loss, grad_x[None], *outs)
```

```python
import jax
import jax.numpy as jnp
from jax import lax
from jax.experimental import pallas as pl
from jax.experimental.pallas import tpu as pltpu

F32 = jnp.float32
BF = jnp.bfloat16

D = 1024
HEAD = 64
CHUNK = 64
N_MEM = 256
XHEAD = 256
D_FF = 2816
EPS = 1e-6
NEG = -1e30
LANES = 128
N_DEV = 8
V7X_VMEM_BYTES = 64 * 1024 * 1024
VMEM_LIMIT = V7X_VMEM_BYTES - 8 * 1024 * 1024

ADAM_LR, ADAM_B1, ADAM_B2, ADAM_EPS, ADAM_WD, ADAM_STEP = 0.001, 0.9, 0.999, 1e-08, 0.01, 10

W_SHARD = {"w_in": (449, True), "w_out": (128, False), "w_xq": (128, False), "w_xkv": (256, True),
           "w_xo": (128, False), "w_gate": (352, True), "w_up": (352, True), "w_down": (352, False)}
GATHER_REST = {"attn": ("w_out", "w_xq", "w_xkv", "w_xo"), "ffn": ("w_gate", "w_up", "w_down")}
SCATTER_GROUPS = {"ffn": ("w_gate", "w_up", "w_down"), "xattn": ("w_xq", "w_xo", "w_xkv"), "out": ("w_out",), "in": ("w_in",)}
SMALL_ROWS = 8

NT = (((1,), (1,)), ((), ()))
NN = (((1,), (0,)), ((), ()))
TN = (((0,), (0,)), ((), ()))
_DIMS = {"nn": NN, "nt": NT, "tn": TN}


def _params(sem):
    return pltpu.CompilerParams(dimension_semantics=sem, vmem_limit_bytes=VMEM_LIMIT)


def _mm(name, products, extras, epilogue, M, N, tm, tn, out_dtypes, params=(), n_acc=0):
    assert n_acc == 0 or tn == N
    flat = [t for p in products for t in p]
    counts = [len(p) for p in products]
    in_specs, args, where, slots = [], [], {}, []

    def operand(arr, spec, kind):
        key = (id(arr), kind)
        if key not in where:
            where[key] = len(args)
            args.append(arr)
            in_specs.append(spec)
        return where[key]

    for a, b, form, *at in flat:
        if form == "tn":
            ia = operand(a, pl.BlockSpec((a.shape[0], tm), lambda i, j: (0, i)), "a_tn")
        else:
            ia = operand(a, pl.BlockSpec((tm, a.shape[1]), lambda i, j: (i, 0)), "a")
        if form == "nt":
            ib = operand(b, pl.BlockSpec((tn, b.shape[1]), lambda i, j: (j, 0)), "b_nt")
        elif form == "nn":
            k = at[0] if at else 0
            ib = operand(b, pl.BlockSpec((a.shape[1], tn), lambda i, j, k=k: (k, j)), "b%d" % k)
        else:
            ib = operand(b, pl.BlockSpec((b.shape[0], tn), lambda i, j: (0, j)), "b")
        slots.append((ia, ib))
    n_mm = len(args)
    for e in extras:
        in_specs.append(pl.BlockSpec((tm, tn), lambda i, j: (i, j)))
        args.append(e)
    for p in params:
        in_specs.append(pl.BlockSpec((1, tn), lambda i, j: (0, j)))
        args.append(p)
    n_in = len(args)
    n_out = len(out_dtypes)

    def body(*refs):
        ins, outs = refs[:n_in], refs[n_in:]
        prods, p = [], 0
        for c in counts:
            acc = None
            for _ in range(c):
                a = ins[slots[p][0]][...].astype(BF)
                b = ins[slots[p][1]][...].astype(BF)
                d = lax.dot_general(a, b, _DIMS[flat[p][2]], preferred_element_type=F32)
                acc = d if acc is None else acc + d
                p += 1
            prods.append(acc)
        ex = [r[...].astype(F32) for r in ins[n_mm:]]
        res = epilogue(*prods, *ex)
        for o, r in zip(outs[:n_out], res[:n_out]):
            o[...] = r.astype(o.dtype)
        for o, r in zip(outs[n_out:], res[n_out:]):
            @pl.when(pl.program_id(0) == 0)
            def _(o=o):
                o[...] = jnp.zeros(o.shape, F32)
            o[...] += r

    return pl.pallas_call(
        body, name=name, grid=(M // tm, N // tn), in_specs=in_specs,
        out_specs=[pl.BlockSpec((tm, tn), lambda i, j: (i, j)) for _ in out_dtypes]
        + [pl.BlockSpec((1, tn), lambda i, j: (0, j)) for _ in range(n_acc)],
        out_shape=[jax.ShapeDtypeStruct((M, N), dt) for dt in out_dtypes] + [jax.ShapeDtypeStruct((1, N), F32)] * n_acc,
        compiler_params=_params(("arbitrary", "arbitrary")),
    )(*args)


def _ident(x):
    return (x,)


def _each(*xs):
    return xs


def _spec(rows, w, off, per_j):
    if per_j:
        return pl.BlockSpec((rows, w), lambda j, i: (i, off + j))
    return pl.BlockSpec((rows, w), lambda j, i: (i, off))


def _pspec(rows, w, off, per_j):
    if per_j:
        return pl.BlockSpec((rows, w), lambda j, i: (0, off + j))
    return pl.BlockSpec((rows, w), lambda j, i: (0, off))


def _rw_fwd(name, fn, rows, params, outs, T, tm, nj, n_acc=0):
    in_specs = [_spec(tm, w, off, pj) for _, w, off, pj in rows] + [_pspec(a.shape[0], w, off, pj) for a, w, off, pj in params]
    args = [r[0] for r in rows] + [p[0] for p in params]
    n_in, n_out = len(args), len(outs)
    out_specs = [pl.BlockSpec((tm, w), lambda j, i: (i, j)) for _, w in outs]
    out_shape = [jax.ShapeDtypeStruct((T, nj * w), dt) for dt, w in outs]
    out_specs += [pl.BlockSpec((1, LANES), lambda j, i: (0, 0)) for _ in range(n_acc)]
    out_shape += [jax.ShapeDtypeStruct((1, LANES), F32) for _ in range(n_acc)]

    def body(*refs):
        vals = [r[...].astype(F32) for r in refs[:n_in]]
        res = fn(*vals)
        orefs = refs[n_in:]
        for k in range(n_out):
            orefs[k][...] = res[k].astype(orefs[k].dtype)
        first = (pl.program_id(0) == 0) & (pl.program_id(1) == 0)
        for k in range(n_acc):
            @pl.when(first)
            def _(k=k):
                orefs[n_out + k][...] = jnp.zeros((1, LANES), F32)
            orefs[n_out + k][...] += res[n_out + k]

    return pl.pallas_call(
        body, name=name, grid=(nj, T // tm), in_specs=in_specs, out_specs=out_specs, out_shape=out_shape,
        compiler_params=_params(("arbitrary", "arbitrary")),
    )(*args)


def _rw_bwd(name, fn, rows, params, cots, T, tm, nj, row_grads, param_grads, resid=None):
    in_specs = ([_spec(tm, w, off, pj) for _, w, off, pj in rows] + [_pspec(a.shape[0], w, off, pj) for a, w, off, pj in params]
                + [_spec(tm, w, off, pj) for _, w, off, pj in cots])
    args = [r[0] for r in rows] + [p[0] for p in params] + [c[0] for c in cots]
    if resid is not None:
        in_specs.append(_spec(tm, rows[0][1], rows[0][2], rows[0][3]))
        args.append(resid)
    nr, npar, nc = len(rows), len(params), len(cots)
    out_specs, out_shape, kinds = [], [], []
    for k, dts in enumerate(row_grads):
        for dt in (dts if isinstance(dts, (list, tuple)) else [dts]):
            if dt is not None:
                w = rows[k][1]
                out_specs.append(pl.BlockSpec((tm, w), lambda j, i: (i, j)))
                out_shape.append(jax.ShapeDtypeStruct((T, nj * w), dt))
                kinds.append(("row", k))
    for k, need in enumerate(param_grads):
        if need:
            a, w, off, pj = params[k]
            out_specs.append(_pspec(a.shape[0], w, off, pj))
            out_shape.append(jax.ShapeDtypeStruct(a.shape, F32))
            kinds.append(("par", k))

    def body(*refs):
        vals = [r[...].astype(F32) for r in refs[:nr + npar]]
        ct = tuple(r[...].astype(F32) for r in refs[nr + npar:nr + npar + nc])
        _, vjp = jax.vjp(lambda *a: tuple(fn(*a)), *vals)
        grads = list(vjp(ct))
        n_in = nr + npar + nc + (resid is not None)
        if resid is not None:
            grads[0] = grads[0] + refs[n_in - 1][...].astype(F32)
        orefs = refs[n_in:]
        j, i = pl.program_id(0), pl.program_id(1)
        for o, (kind, k) in zip(orefs, kinds):
            if kind == "row":
                o[...] = grads[k].astype(o.dtype)
            else:
                first = (i == 0) if params[k][3] else ((i == 0) & (j == 0))

                @pl.when(first)
                def _(o=o):
                    o[...] = jnp.zeros(o.shape, F32)
                o[...] += grads[nr + k]

    return pl.pallas_call(
        body, name=name, grid=(nj, T // tm), in_specs=in_specs, out_specs=out_specs, out_shape=out_shape,
        compiler_params=_params(("arbitrary", "arbitrary")),
    )(*args)


def _rms(x, g):
    return x * lax.rsqrt(jnp.mean(x * x, axis=-1, keepdims=True) + EPS) * g


def _rms_fn(x, g):
    return (_rms(x, g),)


def _lo_mask():
    return lax.broadcasted_iota(jnp.int32, (1, LANES), 1) < HEAD


def _gmean(x, lo):
    s0 = jnp.sum(jnp.where(lo, x, 0.0), axis=-1, keepdims=True)
    s1 = jnp.sum(jnp.where(lo, 0.0, x), axis=-1, keepdims=True)
    return jnp.where(lo, s0, s1) * (1.0 / HEAD)


def _fox_prep_fn(fq, fk, gq, gk):
    lo = _lo_mask()
    qn = fq * lax.rsqrt(_gmean(fq * fq, lo) + EPS) * gq * (HEAD ** -0.5)
    kn = fk * lax.rsqrt(_gmean(fk * fk, lo) + EPS) * gk
    return qn, kn


@jax.custom_vjp
def _swap_halves(x):
    bit = (lax.broadcasted_iota(jnp.int32, (1, LANES), 1) & (HEAD // 2)) == 0
    return jnp.where(bit, pltpu.roll(x, LANES - HEAD // 2, 1), pltpu.roll(x, HEAD // 2, 1))


_swap_halves.defvjp(lambda x: (_swap_halves(x), None), lambda _, g: (_swap_halves(g),))


def _ret_fn(rq, rk, rv, rg, cos, sin, s_in, g, lg):
    tb = rq.shape[0]
    nc = tb // CHUNK
    lo = _lo_mask()
    row = lax.broadcasted_iota(jnp.int32, (LANES, 1), 0) < HEAD
    same_head = row == lo
    q = (rq * cos + _swap_halves(rq) * sin) * (HEAD ** -0.5)
    k = rk * cos + _swap_halves(rk) * sin
    q3, k3, v3 = q.reshape(nc, CHUNK, LANES), k.reshape(nc, CHUNK, LANES), rv.reshape(nc, CHUNK, LANES)
    pos = lax.broadcasted_iota(jnp.int32, (CHUNK, 1), 0).astype(F32)
    q_decay = jnp.exp(lg * (pos + 1.0))
    k_decay = jnp.exp(lg * (CHUNK - 1.0 - pos))
    chunk_decay = jnp.exp(lg * float(CHUNK))
    dist = jnp.abs(lax.broadcasted_iota(jnp.int32, (CHUNK, CHUNK), 0) - lax.broadcasted_iota(jnp.int32, (CHUNK, CHUNK), 1)).astype(F32)
    v3b = v3.astype(BF)
    intra = []
    for hh in range(2):
        hm = lo if hh == 0 else ~lo
        lg_h = lg[:, hh * HEAD:hh * HEAD + 1]
        qm = jnp.where(hm, q3, 0.0).astype(BF)
        sc = jnp.einsum("nid,njd->nij", qm, k3.astype(BF), preferred_element_type=F32) * jnp.exp(lg_h * dist)[None]
        intra.append(jnp.einsum("nij,nje->nie", sc.astype(BF), v3b, preferred_element_type=F32))
    o = jnp.where(lo, intra[0], intra[1])
    kv = jnp.einsum("njd,nje->nde", (k3 * k_decay[None]).astype(BF), v3b, preferred_element_type=F32)
    kv = jnp.where(same_head[None], kv, 0.0)
    state, states = s_in, []
    for n in range(nc):
        states.append(state)
        state = state * chunk_decay + kv[n]
    s_prev = jnp.stack(states, axis=0)
    o = o + jnp.einsum("nid,nde->nie", (q3 * q_decay[None]).astype(BF), s_prev.astype(BF), preferred_element_type=F32)
    o = o.reshape(tb, LANES)
    mu = _gmean(o, lo)
    oc = o - mu
    y = oc * lax.rsqrt(_gmean(oc * oc, lo) + EPS) * g
    return jax.nn.silu(rg) * y, state


def _xattn_fn(qx, gq, gk, kk, vv):
    q = _rms(qx, gq)
    k = _rms(kk, gk)
    logits = lax.dot_general(q.astype(BF), k.astype(BF), NT, preferred_element_type=F32) * (XHEAD ** -0.5)
    p = jax.nn.softmax(logits, axis=-1)
    return (jnp.dot(p.astype(BF), vv.astype(BF), preferred_element_type=F32),)


def _swiglu_fwd_epi(g, u):
    return g, u, jax.nn.silu(g) * u


def _swiglu_bwd_epi(dact, g, u):
    _, vjp = jax.vjp(lambda a, b: jax.nn.silu(a) * b, g, u)
    return vjp(dact)


def _add_rms_epi(acc, resid, g):
    h = acc + resid
    return h, _rms(h, g)


def _add_loss_epi(acc, resid, target):
    err = (acc + resid) - target
    dy = err * (1.0 / D)
    part = jnp.sum(jnp.sum(err * err, axis=0, keepdims=True), axis=1, keepdims=True) * (0.5 / D)
    return dy, dy, jnp.broadcast_to(part, (1, err.shape[1]))


def _rms_bwd_epi(dhn, h, skip, g):
    _, vjp = jax.vjp(_rms, h, g)
    dh, dg = vjp(dhn)
    dh = dh + skip
    return dh, dh, dg


def _rms_bwd_first_epi(dhn, h, skip, g):
    return _rms_bwd_epi(dhn, h, skip, g)[1:]


def _ret_fwd(P, cos, sin, g_ret, lg, T, tb):
    nb = T // tb

    def body(rq, rk, rv, rg, c, s, g, l, o_ref, s0_ref, state):
        @pl.when(pl.program_id(1) == 0)
        def _():
            state[...] = jnp.zeros(state.shape, F32)
        s0_ref[0, 0] = state[...]
        out, s_new = _ret_fn(rq[...], rk[...], rv[...], rg[...], c[...], s[...], state[...], g[...], l[...])
        o_ref[...] = out.astype(o_ref.dtype)
        state[...] = s_new

    sec = lambda off: pl.BlockSpec((tb, LANES), lambda j, i: (i, off + j))
    tab = pl.BlockSpec((tb, LANES), lambda j, i: (i, 0))
    par = pl.BlockSpec((1, LANES), lambda j, i: (0, j))
    return pl.pallas_call(
        body, name="ret_fwd", grid=(4, nb),
        in_specs=[sec(0), sec(4), sec(8), sec(12), tab, tab, par, par],
        out_specs=[pl.BlockSpec((tb, LANES), lambda j, i: (i, j)), pl.BlockSpec((1, 1, LANES, LANES), lambda j, i: (j, i, 0, 0))],
        out_shape=[jax.ShapeDtypeStruct((T, 4 * LANES), BF), jax.ShapeDtypeStruct((4, nb, LANES, LANES), F32)],
        scratch_shapes=[pltpu.VMEM((LANES, LANES), F32)],
        compiler_params=_params(("arbitrary", "arbitrary")),
    )(P, P, P, P, cos, sin, g_ret, lg)


def _ret_bwd(P, cos, sin, g_ret, lg, s0, dmix, T, tb):
    nb = T // tb

    def body(rq, rk, rv, rg, c, s, g, l, s0_ref, do, drq, drk, drv, drg, dg, dstate):
        i = pl.program_id(1)

        @pl.when(i == 0)
        def _():
            dstate[...] = jnp.zeros(dstate.shape, F32)
            dg[...] = jnp.zeros(dg.shape, F32)

        cc, ss, ll = c[...], s[...], l[...]
        _, vjp = jax.vjp(lambda a, b, v, gate, st, gg: _ret_fn(a, b, v, gate, cc, ss, st, gg, ll),
                         rq[...], rk[...], rv[...], rg[...], s0_ref[0, 0], g[...])
        ga, gb, gv, ggate, gst, ggain = vjp((do[...], dstate[...]))
        drq[...] = ga.astype(drq.dtype)
        drk[...] = gb.astype(drk.dtype)
        drv[...] = gv.astype(drv.dtype)
        drg[...] = ggate.astype(drg.dtype)
        dstate[...] = gst
        dg[...] += ggain

    rev = lambda i: nb - 1 - i
    sec = lambda off: pl.BlockSpec((tb, LANES), lambda j, i: (rev(i), off + j))
    tab = pl.BlockSpec((tb, LANES), lambda j, i: (rev(i), 0))
    par = pl.BlockSpec((1, LANES), lambda j, i: (0, j))
    outb = pl.BlockSpec((tb, LANES), lambda j, i: (rev(i), j))
    return pl.pallas_call(
        body, name="ret_bwd", grid=(4, nb),
        in_specs=[sec(0), sec(4), sec(8), sec(12), tab, tab, par, par,
                  pl.BlockSpec((1, 1, LANES, LANES), lambda j, i: (j, rev(i), 0, 0)), outb],
        out_specs=[outb, outb, outb, outb, par],
        out_shape=[jax.ShapeDtypeStruct((T, 4 * LANES), BF)] * 4 + [jax.ShapeDtypeStruct((1, 4 * LANES), F32)],
        scratch_shapes=[pltpu.VMEM((LANES, LANES), F32)],
        compiler_params=_params(("arbitrary", "arbitrary")),
    )(P, P, P, P, cos, sin, g_ret, lg, s0, dmix)


_FB = 128


def _tri(lower):
    r = lax.broadcasted_iota(jnp.int32, (_FB, _FB), 0)
    c = lax.broadcasted_iota(jnp.int32, (_FB, _FB), 1)
    return ((r >= c) if lower else (r <= c)).astype(F32)


def _fgate_fwd(ffp, bpad, T):
    def body(ff_ref, b_ref, fc_ref, fr_ref):
        lane = lax.broadcasted_iota(jnp.int32, (1, LANES), 1)
        tri = _tri(True)
        carry = jnp.zeros((1, LANES), F32)
        for blk in range(T // _FB):
            z = ff_ref[blk * _FB:(blk + 1) * _FB, :] + b_ref[...]
            lf = jnp.where(lane < 8, jax.nn.log_sigmoid(z), 0.0)
            f = jnp.dot(tri, lf, precision=lax.Precision.HIGHEST, preferred_element_type=F32) + carry
            carry = f[_FB - 1:_FB, :]
            fc_ref[blk * _FB:(blk + 1) * _FB, :] = f
            fr_ref[:, blk * _FB:(blk + 1) * _FB] = f.T[:8, :]

    return pl.pallas_call(
        body, name="fgate_fwd",
        out_shape=[jax.ShapeDtypeStruct((T, LANES), F32), jax.ShapeDtypeStruct((8, T), F32)],
        compiler_params=pltpu.CompilerParams(vmem_limit_bytes=VMEM_LIMIT),
    )(ffp, bpad)


_BIAS_LANE = HEAD


def _head_bias_col(fc, head):
    lane = lax.broadcasted_iota(jnp.int32, (1, LANES), 1)
    return jnp.sum(jnp.where(lane == head, fc, 0.0), axis=-1, keepdims=True)


def _split3(f):
    hi = f.astype(BF).astype(F32)
    mid = (f - hi).astype(BF).astype(F32)
    lo = ((f - hi) - mid).astype(BF).astype(F32)
    return hi, mid, lo


def _fox_operands(P, fc, g_fq2, g_fk2, T, tm):
    def body(fq_ref, fk_ref, fv_ref, fc_ref, gq_ref, gk_ref, qat_ref, ka_ref, kat_ref, va_ref, vat_ref):
        j = pl.program_id(0)
        lane = lax.broadcasted_iota(jnp.int32, (1, LANES), 1)
        qn, kn = _fox_prep_fn(fq_ref[...], fk_ref[...], gq_ref[...], gk_ref[...])
        v = fv_ref[...]
        fcb = fc_ref[...]
        b = _BIAS_LANE
        for hh in range(2):
            hi, mid, lo = _split3(_head_bias_col(fcb, 2 * j + hh))
            take = (lambda a: a) if hh == 0 else (lambda a: pltpu.roll(a, HEAD, 1))
            qa = jnp.where(lane < HEAD, take(qn), jnp.where(lane == b, hi, jnp.where(lane == b + 1, mid, jnp.where(
                lane == b + 2, lo, jnp.where(lane < b + 6, 1.0, 0.0)))))
            ka = jnp.where(lane < HEAD, take(kn), jnp.where(lane < b + 3, 1.0, jnp.where(lane == b + 3, -hi, jnp.where(
                lane == b + 4, -mid, jnp.where(lane == b + 5, -lo, 0.0)))))
            va = jnp.where(lane < HEAD, take(v), 0.0)
            qat_ref[hh] = qa.T.astype(BF)
            for val, ref, tref in ((ka, ka_ref, kat_ref), (va, va_ref, vat_ref)):
                ref[hh] = val.astype(BF)
                tref[hh] = val.T.astype(BF)

    sec = lambda off: pl.BlockSpec((tm, LANES), lambda j, i: (i, off + j))
    par = pl.BlockSpec((1, LANES), lambda j, i: (0, 0))
    nat = pl.BlockSpec((2, tm, LANES), lambda j, i: (j, i, 0))
    trn = pl.BlockSpec((2, LANES, tm), lambda j, i: (j, 0, i))
    return pl.pallas_call(
        body, name="fox_operands", grid=(4, T // tm),
        in_specs=[sec(16), sec(20), sec(24), pl.BlockSpec((tm, LANES), lambda j, i: (i, 0)), par, par],
        out_specs=[trn, nat, trn, nat, trn],
        out_shape=[jax.ShapeDtypeStruct((8, LANES, T), BF)]
        + [jax.ShapeDtypeStruct((8, T, LANES), BF), jax.ShapeDtypeStruct((8, LANES, T), BF)] * 2,
        compiler_params=_params(("parallel", "arbitrary")),
    )(P, P, P, fc, g_fq2, g_fk2)


def _fox_forward(qat, ka, vat, T, tq, tk):
    nq, per = T // tq, tq // tk
    assert per == 2
    RC = 64

    def body(qat_ref, ka_ref, vat_ref, o_ref, lse_ref, s_scr, p_scr, a_scr, m_scr, l_scr, acc_scr):
        i = pl.program_id(1)
        sub = lax.broadcasted_iota(jnp.int32, (8, 1), 0)
        row = lax.broadcasted_iota(jnp.int32, (RC, tq), 0)
        col = lax.broadcasted_iota(jnp.int32, (RC, tq), 1)
        m_scr[...] = jnp.full(m_scr.shape, NEG, F32)
        l_scr[...] = jnp.zeros(l_scr.shape, F32)
        acc_scr[...] = jnp.zeros(acc_scr.shape, F32)

        def scores(slot, kb):
            k0 = pl.multiple_of(kb * tk, tk)
            for hh in range(2):
                s_scr[slot, hh] = jnp.dot(ka_ref[hh, pl.ds(k0, tk), :], qat_ref[hh], preferred_element_type=F32)

        def softmax(slot, kb, diagonal):
            shift = kb * tk - i * tq
            for hh in range(2):
                def masked(r):
                    tile = s_scr[slot, hh, r * RC:(r + 1) * RC, :]
                    return jnp.where(row + (r * RC + shift) <= col, tile, NEG) if diagonal else tile

                mx = jnp.max(masked(0), axis=0, keepdims=True)
                for r in range(1, tk // RC):
                    mx = jnp.maximum(mx, jnp.max(masked(r), axis=0, keepdims=True))
                m_old = m_scr[hh, 0:1, :]
                m2 = jnp.maximum(m_old, mx)
                a = jnp.exp(m_old - m2)
                lsum = jnp.zeros((1, tq), F32)
                for r in range(tk // RC):
                    p = jnp.exp(masked(r) - m2)
                    p_scr[slot, hh, r * RC:(r + 1) * RC, :] = p.astype(BF)
                    lsum = lsum + jnp.sum(p, axis=0, keepdims=True)
                m_scr[hh] = jnp.broadcast_to(m2, (8, tq))
                l_scr[hh] = jnp.broadcast_to(a * l_scr[hh, 0:1, :] + lsum, (8, tq))
                a_scr[slot, hh] = jnp.broadcast_to(a, (8, tq))

        def values(slot, kb):
            k0 = pl.multiple_of(kb * tk, tk)
            for hh in range(2):
                pv = jnp.dot(vat_ref[hh, 0:HEAD, pl.ds(k0, tk)], p_scr[slot, hh], preferred_element_type=F32)
                acc_scr[hh] = a_scr[slot, hh, 0:1, :] * acc_scr[hh] + pv

        def pair(kb, diag_first, diag_second, more):
            if more:
                scores(0, kb + 2)
            softmax(1, kb + 1, diag_first)
            values(0, kb)
            if more:
                scores(1, kb + 3)
                softmax(0, kb + 2, diag_second)
            values(1, kb + 1)

        scores(0, 0)
        scores(1, 1)
        softmax(0, 0, True)

        @pl.loop(0, jnp.maximum(i - 1, 0))
        def _(t):
            pair(2 * t, False, False, True)

        @pl.when(i >= 1)
        def _():
            pair(2 * (i - 1), False, True, True)

        pair(2 * i, True, False, False)

        o_ref[...] = jnp.concatenate([acc_scr[hh] / l_scr[hh, 0:1, :] for hh in range(2)], axis=0).T
        lses = [m_scr[hh, 0:1, :] + jnp.log(l_scr[hh, 0:1, :]) for hh in range(2)]
        lse_ref[0] = jnp.where(sub == 0, lses[0], jnp.where(sub == 1, lses[1], 0.0))

    return pl.pallas_call(
        body, name="fox_forward", grid=(4, nq),
        in_specs=[pl.BlockSpec((2, LANES, tq), lambda j, i: (j, 0, i)), pl.BlockSpec((2, T, LANES), lambda j, i: (j, 0, 0)),
                  pl.BlockSpec((2, LANES, T), lambda j, i: (j, 0, 0))],
        out_specs=[pl.BlockSpec((tq, LANES), lambda j, i: (i, j)), pl.BlockSpec((1, 8, tq), lambda j, i: (j, 0, i))],
        out_shape=[jax.ShapeDtypeStruct((T, 4 * LANES), F32), jax.ShapeDtypeStruct((4, 8, T), F32)],
        scratch_shapes=[pltpu.VMEM((2, 2, tk, tq), F32), pltpu.VMEM((2, 2, tk, tq), BF), pltpu.VMEM((2, 2, 8, tq), F32),
                        pltpu.VMEM((2, 8, tq), F32), pltpu.VMEM((2, 8, tq), F32), pltpu.VMEM((2, HEAD, tq), F32)],
        compiler_params=_params(("parallel", "arbitrary")),
    )(qat, ka, vat)


def _fox_cotangent(dmix, fox, T, tm):
    def body(do_ref, o_ref, doat_ref, dl_ref):
        lane = lax.broadcasted_iota(jnp.int32, (1, LANES), 1)
        sub = lax.broadcasted_iota(jnp.int32, (8, 1), 0)
        dob = do_ref[...].astype(BF).astype(F32)
        prod_t = (dob * o_ref[...]).T
        d0 = jnp.sum(prod_t[:HEAD], axis=0, keepdims=True)
        d1 = jnp.sum(prod_t[HEAD:], axis=0, keepdims=True)
        dl_ref[0] = jnp.where(sub == 0, d0, jnp.where(sub == 1, d1, 0.0))
        for hh in range(2):
            val = jnp.where(lane < HEAD, dob if hh == 0 else pltpu.roll(dob, HEAD, 1), 0.0)
            doat_ref[hh] = val.T.astype(BF)

    return pl.pallas_call(
        body, name="fox_cotangent", grid=(4, T // tm),
        in_specs=[pl.BlockSpec((tm, LANES), lambda j, i: (i, 4 + j)), pl.BlockSpec((tm, LANES), lambda j, i: (i, j))],
        out_specs=[pl.BlockSpec((2, LANES, tm), lambda j, i: (j, 0, i)), pl.BlockSpec((1, 8, tm), lambda j, i: (j, 0, i))],
        out_shape=[jax.ShapeDtypeStruct((8, LANES, T), BF), jax.ShapeDtypeStruct((4, 8, T), F32)],
        compiler_params=_params(("parallel", "arbitrary")),
    )(dmix, fox)


def _fox_backward(qat, ka, kat, va, doat, lse, dl, T, tq, tk):
    nq, nk = T // tq, T // tk

    def body(qat_ref, ka_ref, kat_ref, va_ref, doat_ref, lse_ref, dl_ref,
             dq_ref, dk_ref, dv_ref, df_ref, dr_ref, dqt, dkt, dvt, df_acc, sdp, pds):
        j, kb = pl.program_id(0), pl.program_id(1)
        lane = lax.broadcasted_iota(jnp.int32, (1, LANES), 1)
        first = (kb * tk) // tq

        @pl.when(kb == 0)
        def _():
            dqt[...] = jnp.zeros(dqt.shape, F32)

        dkt[...] = jnp.zeros(dkt.shape, F32)
        dvt[...] = jnp.zeros(dvt.shape, F32)
        df_acc[...] = jnp.zeros(df_acc.shape, F32)

        RC = 64
        last = nq - 1

        def products(slot, qi):
            q0 = pl.multiple_of(qi * tq, tq)
            for hh in range(2):
                sdp[slot, hh, 0] = jnp.dot(ka_ref[hh], qat_ref[hh, :, pl.ds(q0, tq)], preferred_element_type=F32)
                sdp[slot, hh, 1] = jnp.dot(va_ref[hh], doat_ref[hh, :, pl.ds(q0, tq)], preferred_element_type=F32)

        def softmax_bwd(slot, qi, diagonal, valid):
            q0 = pl.multiple_of(qi * tq, tq)
            shift = kb * tk - first * tq
            col = lax.broadcasted_iota(jnp.int32, (RC, tq), 1)
            row = lax.broadcasted_iota(jnp.int32, (RC, tq), 0)
            for hh in range(2):
                lse_row = lse_ref[0, hh:hh + 1, pl.ds(q0, tq)]
                dl_row = dl_ref[0, hh:hh + 1, pl.ds(q0, tq)]
                rsum = jnp.zeros((1, tq), F32)
                for r in range(tk // RC):
                    rows = slice(r * RC, (r + 1) * RC)
                    p = jnp.exp(sdp[slot, hh, 0, rows, :] - lse_row)
                    p = jnp.where((row + (r * RC + shift) <= col) if diagonal else valid, p, 0.0)
                    ds = p * (sdp[slot, hh, 1, rows, :] - dl_row)
                    pds[slot, hh, 0, rows, :] = p.astype(BF)
                    pds[slot, hh, 1, rows, :] = ds.astype(BF)
                    rsum = rsum + jnp.sum(ds, axis=0, keepdims=True)
                    part = ds[:, 0:LANES]
                    for c in range(1, tq // LANES):
                        part = part + ds[:, c * LANES:(c + 1) * LANES]
                    df_acc[hh, rows, :] += part
                dqt[hh, HEAD:HEAD + 8, pl.ds(q0, tq)] += jnp.broadcast_to(rsum, (8, tq))

        def accumulate(slot, qi):
            q0 = pl.multiple_of(qi * tq, tq)
            for hh in range(2):
                dvt[hh] += lax.dot_general(doat_ref[hh, 0:HEAD, pl.ds(q0, tq)], pds[slot, hh, 0], NT, preferred_element_type=F32)
                dkt[hh] += lax.dot_general(qat_ref[hh, 0:HEAD, pl.ds(q0, tq)], pds[slot, hh, 1], NT, preferred_element_type=F32)
                dqt[hh, 0:HEAD, pl.ds(q0, tq)] += jnp.dot(kat_ref[hh, 0:HEAD, :], pds[slot, hh, 1], preferred_element_type=F32)

        products(0, first)
        products(1, jnp.minimum(first + 1, last))
        softmax_bwd(0, first, True, None)

        @pl.loop(0, (nq - first + 1) // 2)
        def _(t):
            qi = first + 2 * t
            products(0, jnp.minimum(qi + 2, last))
            softmax_bwd(1, jnp.minimum(qi + 1, last), False, qi + 1 <= last)
            accumulate(0, qi)
            products(1, jnp.minimum(qi + 3, last))
            softmax_bwd(0, jnp.minimum(qi + 2, last), False, qi + 2 <= last)
            accumulate(1, jnp.minimum(qi + 1, last))

        dk_ref[...] = jnp.concatenate([dkt[0], dkt[1]], axis=0).T
        dv_ref[...] = jnp.concatenate([dvt[0], dvt[1]], axis=0).T.astype(dv_ref.dtype)
        f0 = -jnp.sum(df_acc[0], axis=1, keepdims=True)
        f1 = -jnp.sum(df_acc[1], axis=1, keepdims=True)
        df_ref[0] = jnp.where(lane == 2 * j, f0, jnp.where(lane == 2 * j + 1, f1, 0.0))

        @pl.when(kb == nk - 1)
        def _():
            for t in range(nq):
                cols = slice(t * tq, (t + 1) * tq)
                dq_ref[cols, :] = jnp.concatenate([dqt[0, 0:HEAD, cols], dqt[1, 0:HEAD, cols]], axis=0).T
                rsum = jnp.concatenate([dqt[0, HEAD:HEAD + 8, cols], dqt[1, HEAD:HEAD + 8, cols],
                                        jnp.zeros((LANES - 16, tq), F32)], axis=0).T
                dr_ref[0, cols, :] = jnp.where(lane == 2 * j, rsum[:, 0:1], jnp.where(lane == 2 * j + 1, rsum[:, 8:9], 0.0))

    trn_full = pl.BlockSpec((2, LANES, T), lambda j, kb: (j, 0, 0))
    nat_blk = pl.BlockSpec((2, tk, LANES), lambda j, kb: (j, kb, 0))
    trn_blk = pl.BlockSpec((2, LANES, tk), lambda j, kb: (j, 0, kb))
    rows = pl.BlockSpec((1, 8, T), lambda j, kb: (j, 0, 0))
    blk = pl.BlockSpec((tk, LANES), lambda j, kb: (kb, j))
    return pl.pallas_call(
        body, name="fox_backward", grid=(4, nk),
        in_specs=[trn_full, nat_blk, trn_blk, nat_blk, trn_full, rows, rows],
        out_specs=[pl.BlockSpec((T, LANES), lambda j, kb: (0, j)), blk, blk, pl.BlockSpec((1, tk, LANES), lambda j, kb: (j, kb, 0)),
                   pl.BlockSpec((1, T, LANES), lambda j, kb: (j, 0, 0))],
        out_shape=[jax.ShapeDtypeStruct((T, 4 * LANES), F32), jax.ShapeDtypeStruct((T, 4 * LANES), F32),
                   jax.ShapeDtypeStruct((T, 4 * LANES), BF), jax.ShapeDtypeStruct((4, T, LANES), F32),
                   jax.ShapeDtypeStruct((4, T, LANES), F32)],
        scratch_shapes=[pltpu.VMEM((2, HEAD + 8, T), F32), pltpu.VMEM((2, HEAD, tk), F32), pltpu.VMEM((2, HEAD, tk), F32),
                        pltpu.VMEM((2, tk, LANES), F32), pltpu.VMEM((2, 2, 2, tk, tq), F32), pltpu.VMEM((2, 2, 2, tk, tq), BF)],
        compiler_params=_params(("arbitrary", "arbitrary")),
    )(qat, ka, kat, va, doat, lse, dl)


def _fgate_bwd_col(ffp, bpad, dfc4, drc4, T):
    def body(ff_ref, b_ref, dfc_ref, drc_ref, dff_ref, db_ref):
        lane = lax.broadcasted_iota(jnp.int32, (1, LANES), 1)
        tri = _tri(False)
        carry = jnp.zeros((1, LANES), F32)
        db = jnp.zeros((1, LANES), F32)
        for blk in reversed(range(T // _FB)):
            rows = slice(blk * _FB, (blk + 1) * _FB)
            dcol = dfc_ref[0, rows, :] + drc_ref[0, rows, :]
            for pair in range(1, 4):
                dcol = dcol + (dfc_ref[pair, rows, :] + drc_ref[pair, rows, :])
            dlf = jnp.dot(tri, dcol, precision=lax.Precision.HIGHEST, preferred_element_type=F32) + carry
            carry = dlf[0:1, :]
            z = ff_ref[blk * _FB:(blk + 1) * _FB, :] + b_ref[...]
            dz = jnp.where(lane < 8, dlf * jax.nn.sigmoid(-z), 0.0)
            dff_ref[blk * _FB:(blk + 1) * _FB, :] = dz.astype(dff_ref.dtype)
            db = db + jnp.sum(dz, axis=0, keepdims=True)
        db_ref[...] = db

    return pl.pallas_call(
        body, name="fgate_bwd",
        out_shape=[jax.ShapeDtypeStruct((T, LANES), BF), jax.ShapeDtypeStruct((1, LANES), F32)],
        compiler_params=pltpu.CompilerParams(vmem_limit_bytes=VMEM_LIMIT),
    )(ffp, bpad, dfc4, drc4)


MESH = pl.DeviceIdType.MESH
N_PEERS = N_DEV - 1


def _place():
    return lax.axis_index("x"), lax.axis_index("y"), lax.axis_index("c")


def _all_gather(shard, rows, g, tm):
    R, W = shard.shape
    T = rows.shape[0]
    steps = T // tm

    def body(w_ref, rows_ref, g_ref, out_ref, norm_ref, send_sems, recv_sems, local_sem):
        x, y, c = _place()
        me, sibling = (x, y, c), (x, y, 1 - c)
        chips = [(1 - x, y), (x, 1 - y), (1 - x, 1 - y)]

        def slot(px, py, pc):
            return out_ref.at[4 * px + 2 * py + pc]

        def copy(k, block, to, src=None):
            return pltpu.make_async_remote_copy(
                src_ref=slot(*block) if src is None else src, dst_ref=slot(*block),
                send_sem=send_sems.at[k], recv_sem=recv_sems.at[k], device_id=to, device_id_type=MESH)

        mine = pltpu.make_async_copy(w_ref, slot(*me), local_sem)
        first = [copy(0, me, sibling, src=w_ref)]
        first += [copy(1 + n, me, (*chip, c), src=w_ref) for n, chip in enumerate(chips)]
        passed = [copy(4 + n, (*chip, c), sibling) for n, chip in enumerate(chips)]

        @pl.when(pl.program_id(0) == 0)
        def _():
            mine.start()
            for cp in first:
                cp.start()

        norm_ref[...] = _rms(rows_ref[...], g_ref[...]).astype(norm_ref.dtype)

        @pl.when(pl.program_id(0) == steps - 1)
        def _():
            for n, chip in enumerate(chips):
                copy(1 + n, (*chip, c), me).wait_recv()
                passed[n].start()
            copy(0, sibling, me).wait_recv()
            for n, chip in enumerate(chips):
                copy(4 + n, (*chip, 1 - c), me).wait_recv()
            for cp in first + passed:
                cp.wait_send()
            mine.wait()

    tile = pl.BlockSpec((tm, D), lambda i: (i, 0))
    return pl.pallas_call(
        body, name="all_gather_weights", grid=(steps,),
        out_shape=[jax.ShapeDtypeStruct((N_DEV, R, W), shard.dtype), jax.ShapeDtypeStruct((T, D), BF)],
        in_specs=[pl.BlockSpec(memory_space=pl.ANY), tile, pl.BlockSpec((1, D), lambda i: (0, 0))],
        out_specs=[pl.BlockSpec(memory_space=pl.ANY), tile],
        scratch_shapes=[pltpu.SemaphoreType.DMA((N_PEERS,)), pltpu.SemaphoreType.DMA((N_PEERS,)), pltpu.SemaphoreType.DMA],
        compiler_params=_params(("arbitrary",)),
    )(shard, rows, g)


def _exchange_copies(src_refs, land_refs, send_sems, recv_sems, scatter):
    x, y, c = _place()
    me = 4 * x + 2 * y + c
    copies = []
    for k, (src_ref, land_ref) in enumerate(zip(src_refs, land_refs)):
        for r in range(1, N_DEV):
            px, py, pc = x ^ (r >> 2), y ^ ((r >> 1) & 1), c ^ (r & 1)
            copies.append(pltpu.make_async_remote_copy(
                src_ref=src_ref.at[4 * px + 2 * py + pc] if scatter else src_ref, dst_ref=land_ref.at[me],
                send_sem=send_sems.at[k * N_PEERS + r - 1], recv_sem=recv_sems.at[k * N_PEERS + r - 1],
                device_id=(px, py, pc), device_id_type=MESH))
    return copies


_HBM = pl.BlockSpec(memory_space=pltpu.HBM)
_SEM = pl.BlockSpec(memory_space=pltpu.SEMAPHORE)
_EFFECT = pltpu.SideEffectType.DATAFLOW_SIDE_EFFECTING


def _exchange_start(name, srcs, lands, scatter):
    n = len(srcs)

    def body(*refs):
        send_sems, recv_sems = refs[2 * n], refs[2 * n + 1]
        for cp in _exchange_copies(refs[:n], refs[n:2 * n], send_sems, recv_sems, scatter):
            cp.start()
        token = refs[-1]
        token[...] = jnp.zeros(token.shape, F32)

    arrays = list(srcs) + list(lands)
    out = pl.pallas_call(
        body, name=name,
        out_shape=(pltpu.SemaphoreType.DMA((n * N_PEERS,)), pltpu.SemaphoreType.DMA((n * N_PEERS,)))
        + tuple(pltpu.HBM(a.shape, a.dtype) for a in arrays) + (jax.ShapeDtypeStruct((8, LANES), F32),),
        in_specs=(_HBM,) * (2 * n), out_specs=(_SEM, _SEM) + (_HBM,) * (2 * n) + (pl.BlockSpec(memory_space=pltpu.VMEM),),
        input_output_aliases={k: 2 + k for k in range(2 * n)},
        compiler_params=pltpu.CompilerParams(has_side_effects=_EFFECT),
    )(*(pltpu.with_memory_space_constraint(a, pltpu.HBM) for a in arrays))
    return out[0], out[1], out[2:2 + n], out[2 + n:2 + 2 * n], out[-1]


def _exchange_wait(name, started, after, scatter):
    send_sems, recv_sems, srcs, lands, _ = started
    n = len(srcs)

    def body(*refs):
        copies = _exchange_copies(refs[:n], refs[n:2 * n], refs[2 * n], refs[2 * n + 1], scatter)
        for cp in copies:
            cp.wait_send()
        for cp in copies:
            cp.wait_recv()

    arrays = list(srcs) + list(lands)
    out = pl.pallas_call(
        body, name=name,
        out_shape=tuple(pltpu.HBM(a.shape, a.dtype) for a in arrays),
        in_specs=(_HBM,) * (2 * n) + (_SEM, _SEM, pl.BlockSpec(memory_space=pl.ANY)), out_specs=(_HBM,) * (2 * n),
        input_output_aliases={k: k for k in range(2 * n)},
        compiler_params=pltpu.CompilerParams(has_side_effects=_EFFECT),
    )(*arrays, send_sems, recv_sems, after)
    return out[:n], out[n:]


def _adam_update(g, w, m, v):
    m2 = ADAM_B1 * m + (1.0 - ADAM_B1) * g
    v2 = ADAM_B2 * v + (1.0 - ADAM_B2) * jnp.square(g)
    m_hat = m2 / (1.0 - ADAM_B1 ** ADAM_STEP)
    v_hat = v2 / (1.0 - ADAM_B2 ** ADAM_STEP)
    return g, -ADAM_LR * (m_hat / (jnp.sqrt(v_hat) + ADAM_EPS) + ADAM_WD * w), m2, v2


def _adamw(name, me, slots, sent, w, m, v):
    R, W = w.shape
    steps = max(k for k in (4, 2, 1) if k == 1 or (R % k == 0 and (R // k) % 16 == 0))
    tr = R // steps

    def body(me_ref, s_ref, *refs):
        if sent is not None:
            g = refs[0][0].astype(F32)
            refs = refs[1:]
        else:
            g = jnp.zeros((tr, W), F32)
        for s in range(N_DEV):
            part = s_ref[s].astype(F32)
            g = g + (part if sent is None else jnp.where(me_ref[0] == s, 0.0, part))
        w_ref, m_ref, v_ref = refs[:3]
        for o, r in zip(refs[3:], _adam_update(g, w_ref[...], m_ref[...], v_ref[...])):
            o[...] = r

    rows = pl.BlockSpec((tr, W), lambda i, me_ref: (i, 0))
    in_specs = [pl.BlockSpec((N_DEV, tr, W), lambda i, me_ref: (0, i, 0))]
    args = [slots]
    if sent is not None:
        in_specs.append(pl.BlockSpec((1, tr, W), lambda i, me_ref: (me_ref[0], i, 0)))
        args.append(sent)
    return pl.pallas_call(
        body, name=name,
        grid_spec=pltpu.PrefetchScalarGridSpec(num_scalar_prefetch=1, grid=(steps,), in_specs=in_specs + [rows] * 3,
                                               out_specs=[rows] * 4),
        out_shape=[jax.ShapeDtypeStruct((R, W), F32)] * 4,
        compiler_params=_params(("arbitrary",)),
    )(me, *args, w, m, v)


def _tables(T):
    pos = jnp.arange(T, dtype=F32)
    inv_freq = 10000.0 ** (-jnp.arange(0, HEAD, 2, dtype=F32) / HEAD)
    ang = pos[:, None] * inv_freq[None, :]
    cos, sin = jnp.cos(ang), jnp.sin(ang)
    cos4 = jnp.tile(cos, (1, 4))
    sin4 = jnp.tile(jnp.concatenate([-sin, sin], axis=1), (1, 2))
    log_g = jnp.log(1.0 - 2.0 ** (-5.0 - jnp.arange(8, dtype=F32)))
    return cos4, sin4, jnp.repeat(log_g, HEAD)[None, :]


def _local_step(x, hn1, mem, target, sp, w_inT, token, fetch_rest, push, push_small):
    T = x.shape[0]
    tm = min(512, T)
    tq = min(256, T)
    tb = min(1024, T)
    cos4, sin4, lg = _tables(T)
    g_fq2 = jnp.tile(sp["g_fox_q"], (1, 2))
    g_fk2 = jnp.tile(sp["g_fox_k"], (1, 2))
    g_ret = sp["g_ret_out"].reshape(1, 8 * HEAD)
    bpad = jnp.pad(sp["b_forget"], ((0, 0), (0, LANES - 8)))
    w_ffT = jnp.pad(w_inT[3584:3592], ((0, LANES - 8), (0, 0)))
    tie = lambda p, tok: p + tok[0:1, 0:1]
    tm2, tm4 = min(1024, T), min(2048, T)

    P, = _mm("proj_in", [[(hn1, w_inT, "nt")]], [], lambda acc, after: (acc,), T, 3584, tm4, 512, [F32],
             params=[jnp.broadcast_to(token[0:1, 0:1], (1, 3584))])
    ffp, = _mm("proj_ff", [[(hn1, w_ffT, "nt")]], [], _ident, T, LANES, tm, LANES, [F32])
    ret, s0 = _ret_fwd(P, cos4, sin4, g_ret, lg, T, tb)
    fc, _ = _fgate_fwd(ffp, bpad, T)
    qat, ka, kat, va, vat = _fox_operands(P, fc, g_fq2, g_fk2, T, tm4)
    fox, lse = _fox_forward(qat, ka, vat, T, min(512, T), tq)
    W = fetch_rest("attn", fox)
    h1, hn2 = _mm("proj_out", [[(ret, W["w_out"], "nn", 0), (fox, W["w_out"], "nn", 1)]], [x], _add_rms_epi, T, D, tm2, D,
                  [F32, BF], params=[sp["g_xattn"]])

    qx, = _mm("proj_xq", [[(hn2, W["w_xq"], "nn")]], [], _ident, T, D, tm2, D, [F32])
    memn, = _rw_fwd("rms_mem", _rms_fn, [(mem, D, 0, False)], [(sp["g_mem"], D, 0, False)], [(BF, D)], N_MEM, N_MEM, 1)
    kv, = _mm("proj_xkv", [[(memn, W["w_xkvT"], "nt")]], [], _ident, N_MEM, 2 * D, N_MEM, 512, [F32])
    xa_rows = [(qx, XHEAD, 0, True)]
    xa_params = [(sp["g_xq"], XHEAD, 0, False), (sp["g_xk"], XHEAD, 0, False), (kv, XHEAD, 0, True), (kv, XHEAD, 4, True)]
    xo, = _rw_fwd("xattn_fwd", _xattn_fn, xa_rows, xa_params, [(BF, XHEAD)], T, tm4, 4)
    h2, hn3 = _mm("proj_xo", [[(xo, W["w_xo"], "nn")]], [h1], _add_rms_epi, T, D, tm2, D, [F32, BF], params=[sp["g_ffn"]])

    W.update(fetch_rest("ffn", hn3))
    gate, up, act = _mm("ffn_in", [[(hn3, W["w_gateT"], "nt")], [(hn3, W["w_upT"], "nt")]], [], _swiglu_fwd_epi,
                        T, D_FF, tm4, 256, [BF, BF, BF])
    dy, dyb, loss_part = _mm("ffn_out", [[(act, W["w_down"], "nn")]], [h2, target], _add_loss_epi, T, D, tm, D, [F32, BF], n_acc=1)

    dgate, dup = _mm("ffn_out_bwd", [[(dyb, W["w_down"], "nt")]], [gate, up], _swiglu_bwd_epi, T, D_FF, tm4, 256, [BF, BF])
    gW = {}
    gW["w_gateT"], gW["w_upT"] = _mm("dw_gate_up", [[(dgate, hn3, "tn")], [(dup, hn3, "tn")]], [], _each, D_FF, D, 256, D, [BF, BF])
    gW["w_down"], = _mm("dw_down", [[(act, dyb, "tn")]], [], _ident, D_FF, D, 256, D, [BF])
    tok = push("ffn", gW)
    gs = {}
    dh2, dh2b, gs["g_ffn"] = _mm("ffn_in_bwd", [[(dgate, W["w_gateT"], "nn"), (dup, W["w_upT"], "nn")]], [h2, dy], _rms_bwd_epi,
                                 T, D, min(256, T), D, [F32, BF], params=[tie(sp["g_ffn"], tok)], n_acc=1)

    dxo, = _mm("proj_xo_bwd", [[(dh2b, W["w_xo"], "nt")]], [], _ident, T, D, tm2, D, [BF])
    gW["w_xo"], = _mm("dw_xo", [[(xo, dh2b, "tn")]], [], _ident, D, D, 256, D, [BF])
    dqx, gs["g_xq"], gs["g_xk"], dkv_k, dkv_v = _rw_bwd(
        "xattn_bwd", _xattn_fn, xa_rows, xa_params, [(dxo, XHEAD, 0, True)], T, tm4, 4, [BF], [True, True, True, True])
    dkv = jnp.concatenate([dkv_k[:, :D], dkv_v[:, D:]], axis=1)
    gW["w_xq"], = _mm("dw_xq", [[(hn2, dqx, "tn")]], [], _ident, D, D, 256, D, [BF])
    dmemn, = _mm("proj_xkv_bwd", [[(dkv, W["w_xkvT"], "nn")]], [], _ident, N_MEM, D, N_MEM, 512, [F32])
    gW["w_xkvT"], = _mm("dw_xkv", [[(dkv, memn, "tn")]], [], _ident, 2 * D, D, 512, D, [BF])
    tok = push("xattn", gW)
    gs["g_mem"], = _rw_bwd("rms_mem_bwd", _rms_fn, [(mem, D, 0, False)], [(sp["g_mem"], D, 0, False)], [(dmemn, D, 0, False)],
                           N_MEM, N_MEM, 1, [None], [True])
    dh1, dh1b, gs["g_xattn"] = _mm("proj_xq_bwd", [[(dqx, W["w_xq"], "nt")]], [h1, dh2], _rms_bwd_epi, T, D, tm, D, [F32, BF],
                                   params=[tie(sp["g_xattn"], tok)], n_acc=1)

    dmix, = _mm("proj_out_bwd", [[(dh1b, W["w_out"], "nt")]], [], _ident, T, D, tm2, D, [F32])
    gW["w_out"] = jnp.concatenate(_mm("dw_out", [[(ret, dh1b, "tn")], [(fox, dh1b, "tn")]], [], _each, 4 * LANES, D, 256, D,
                                      [BF, BF]), axis=0)
    tok = push("out", gW)
    doat, dl = _fox_cotangent(dmix, fox, T, tm4)
    dqn, dkn, dfv, dfc4, drc4 = _fox_backward(qat, ka, kat, va, doat, lse + tok[0:1, 0:1], dl, T, tq, tq)
    dfq, dfk, gq2, gk2 = _rw_bwd("fox_prep_bwd", _fox_prep_fn, [(P, LANES, 16, True), (P, LANES, 20, True)],
                                 [(g_fq2, LANES, 0, False), (g_fk2, LANES, 0, False)],
                                 [(dqn, LANES, 0, True), (dkn, LANES, 0, True)], T, tm4, 4, [BF, BF], [True, True])
    gs["g_fox_q"] = gq2[:, :HEAD] + gq2[:, HEAD:]
    gs["g_fox_k"] = gk2[:, :HEAD] + gk2[:, HEAD:]
    dff, dbp = _fgate_bwd_col(ffp, bpad, dfc4, drc4, T)
    gs["b_forget"] = dbp[:, :8]
    drq, drk, drv, drg, dg_ret = _ret_bwd(P, cos4, sin4, g_ret, lg, s0, dmix, T, tb)
    gs["g_ret_out"] = dg_ret
    dsecs = [drq, drk, drv, drg, dfq, dfk, dfv]
    g_secs = list(_mm("dw_in", [[(d, hn1, "tn")] for d in dsecs], [], _each, 512, D, LANES, D, [BF] * len(dsecs)))
    g_ff, = _mm("dw_in_ff", [[(dff, hn1, "tn")]], [], _ident, LANES, D, LANES, D, [BF])
    gW["w_inT"] = jnp.concatenate(g_secs + [g_ff[:8]], axis=0)
    tok = push("in", gW)
    grad_x, gs["g_mix"] = _mm("proj_in_bwd", [[(d, w_inT, "nn", k) for k, d in enumerate(dsecs)] + [(dff, w_ffT, "nn")]], [x, dh1],
                              _rms_bwd_first_epi, T, D, tm, D, [F32], params=[tie(sp["g_mix"], tok)], n_acc=1)
    return grad_x, push_small(gs, loss_part)


_CANON = {"w_in": "w_inT", "w_xkv": "w_xkvT", "w_gate": "w_gateT", "w_up": "w_upT"}
_SMALL = (("g_mix", 0, 0, 1024), ("g_xattn", 1, 0, 1024), ("g_mem", 2, 0, 1024), ("g_ffn", 3, 0, 1024),
          ("g_ret_out", 4, 0, 512), ("g_xq", 4, 512, 256), ("g_xk", 4, 768, 256),
          ("g_fox_q", 5, 0, 64), ("g_fox_k", 5, 64, 64), ("b_forget", 5, 128, 8))
_LOSS_AT = (5, 256)


def _pack_small(tree):
    buf = jnp.zeros((SMALL_ROWS, D), F32)
    for name, r, c, n in _SMALL:
        buf = lax.dynamic_update_slice(buf, tree[name].reshape(1, n).astype(F32), (r, c))
    return buf


def _unpack_small(buf, like):
    return {name: buf[r:r + 1, c:c + n].reshape(like[name].shape) for name, r, c, n in _SMALL}


def _canonical(tree, name):
    a = tree[name][0]
    return a.T if W_SHARD[name][1] else a


def _from_canonical(a, name):
    return (a.T if W_SHARD[name][1] else a)[None]


def kernel(x, mem, g_mix, w_in, b_forget, g_ret_out, g_fox_q, g_fox_k, w_out, g_xattn, w_xq, w_xkv, g_mem, g_xq, g_xk, w_xo, g_ffn, w_gate, w_up, w_down, loss_target, m_g_mix, m_w_in, m_b_forget, m_g_ret_out, m_g_fox_q, m_g_fox_k, m_w_out, m_g_xattn, m_w_xq, m_w_xkv, m_g_mem, m_g_xq, m_g_xk, m_w_xo, m_g_ffn, m_w_gate, m_w_up, m_w_down, v_g_mix, v_w_in, v_b_forget, v_g_ret_out, v_g_fox_q, v_g_fox_k, v_w_out, v_g_xattn, v_w_xq, v_w_xkv, v_g_mem, v_g_xq, v_g_xk, v_w_xo, v_g_ffn, v_w_gate, v_w_up, v_w_down):
    names = ("g_mix", "w_in", "b_forget", "g_ret_out", "g_fox_q", "g_fox_k", "w_out", "g_xattn", "w_xq", "w_xkv", "g_mem",
             "g_xq", "g_xk", "w_xo", "g_ffn", "w_gate", "w_up", "w_down")
    w = dict(zip(names, (g_mix, w_in, b_forget, g_ret_out, g_fox_q, g_fox_k, w_out, g_xattn, w_xq, w_xkv, g_mem, g_xq, g_xk,
                         w_xo, g_ffn, w_gate, w_up, w_down)))
    m = dict(zip(names, (m_g_mix, m_w_in, m_b_forget, m_g_ret_out, m_g_fox_q, m_g_fox_k, m_w_out, m_g_xattn, m_w_xq, m_w_xkv,
                         m_g_mem, m_g_xq, m_g_xk, m_w_xo, m_g_ffn, m_w_gate, m_w_up, m_w_down)))
    v = dict(zip(names, (v_g_mix, v_w_in, v_b_forget, v_g_ret_out, v_g_fox_q, v_g_fox_k, v_w_out, v_g_xattn, v_w_xq, v_w_xkv,
                         v_g_mem, v_g_xq, v_g_xk, v_w_xo, v_g_ffn, v_w_gate, v_w_up, v_w_down)))
    small_names = [s[0] for s in _SMALL]
    me = 4 * lax.axis_index("x") + 2 * lax.axis_index("y") + lax.axis_index("c")
    me1 = me.astype(jnp.int32).reshape(1)

    sp = {n: w[n].reshape(1, -1) for n in small_names}
    first, hn1 = _all_gather(_canonical(w, "w_in").astype(BF), x[0], sp["g_mix"], min(1024, x.shape[1]))
    first, rests = lax.optimization_barrier((first, {g: [_canonical(w, n).astype(BF) for n in ns] for g, ns in GATHER_REST.items()}))
    rest_started = {g: _exchange_start("gather_%s_start" % g, rests[g], [lax.empty((N_DEV,) + a.shape, BF) for a in rests[g]],
                                       scatter=False) for g in GATHER_REST}
    after = rest_started["attn"][4] + rest_started["ffn"][4]

    def fetch_rest(group, after):
        srcs, lands = _exchange_wait("gather_%s_wait" % group, rest_started[group], after, scatter=False)
        lands = [lax.dynamic_update_index_in_dim(a, own, me, axis=0) for a, own in zip(lands, srcs)]
        return {_CANON.get(n, n): a.reshape(N_DEV * a.shape[1], D) for n, a in zip(GATHER_REST[group], lands)}

    pushed = {}

    def push(group, grads):
        srcs = [grads[_CANON.get(n, n)].reshape(N_DEV, W_SHARD[n][0], D) for n in SCATTER_GROUPS[group]]
        pushed[group] = _exchange_start("scatter_%s_start" % group, srcs, [lax.empty(a.shape, BF) for a in srcs], scatter=True)
        return pushed[group][4]

    def push_small(gs, loss_part):
        small = lax.dynamic_update_slice(_pack_small(gs), loss_part[:, :1], _LOSS_AT)
        pushed["small"] = _exchange_start("gather_small_start", [small], [jnp.broadcast_to(small[None], (N_DEV,) + small.shape)],
                                          scatter=False)
        return pushed["small"][4]

    grad_x, done = _local_step(x[0], hn1, mem[0], loss_target[0], sp, first.reshape(N_DEV * W_SHARD["w_in"][0], D),
                               after, fetch_rest, push, push_small)

    results, after = {}, done
    for group in ("ffn", "xattn", "out", "small", "in"):
        if group == "small":
            recv_small = _exchange_wait("gather_small_wait", pushed["small"], after, scatter=False)[1][0]
            g_sm, d_sm, m_sm, v_sm = _adamw("adamw_small", me1, recv_small, None, _pack_small(w), _pack_small(m), _pack_small(v))
            after = g_sm
            continue
        sents, recvs = _exchange_wait("scatter_%s_wait" % group, pushed[group], after, scatter=True)
        for name, sent, recv in zip(SCATTER_GROUPS[group], sents, recvs):
            res = _adamw("adamw_" + name, me1, recv, sent, *(_canonical(t, name) for t in (w, m, v)))
            results[name] = [_from_canonical(r, name) for r in res]
        after = results[SCATTER_GROUPS[group][-1]][0]
    loss = g_sm[_LOSS_AT[0], _LOSS_AT[1]]

    outs = []
    for k, sm in enumerate((g_sm, d_sm, m_sm, v_sm)):
        tree = _unpack_small(sm, w)
        tree.update({name: res[k] for name, res in results.items()})
        outs += [tree[n] for n in names]
    return (loss, grad_x[None], *outs)
```

```python
import jax
import jax.numpy as jnp
from jax import lax
from jax.experimental import pallas as pl
from jax.experimental.pallas import tpu as pltpu

F32 = jnp.float32
BF = jnp.bfloat16

D = 1024
HEAD = 64
CHUNK = 64
N_MEM = 256
XHEAD = 256
D_FF = 2816
EPS = 1e-6
NEG = -1e30
LANES = 128
N_DEV = 8
V7X_VMEM_BYTES = 64 * 1024 * 1024
VMEM_LIMIT = V7X_VMEM_BYTES - 8 * 1024 * 1024

ADAM_LR, ADAM_B1, ADAM_B2, ADAM_EPS, ADAM_WD, ADAM_STEP = 0.001, 0.9, 0.999, 1e-08, 0.01, 10

W_SHARD = {"w_in": (449, True), "w_out": (128, False), "w_xq": (128, False), "w_xkv": (256, True),
           "w_xo": (128, False), "w_gate": (352, True), "w_up": (352, True), "w_down": (352, False)}
GATHER_REST = {"attn": ("w_out", "w_xq", "w_xkv", "w_xo"), "ffn": ("w_gate", "w_up", "w_down")}
SCATTER_GROUPS = {"ffn": ("w_gate", "w_up", "w_down"), "xattn": ("w_xq", "w_xo", "w_xkv"), "out": ("w_out",), "in": ("w_in",)}
SMALL_ROWS = 8

NT = (((1,), (1,)), ((), ()))
NN = (((1,), (0,)), ((), ()))
TN = (((0,), (0,)), ((), ()))
_DIMS = {"nn": NN, "nt": NT, "tn": TN}


def _params(sem):
    return pltpu.CompilerParams(dimension_semantics=sem, vmem_limit_bytes=VMEM_LIMIT)


def _mm(name, products, extras, epilogue, M, N, tm, tn, out_dtypes, params=(), n_acc=0):
    assert n_acc == 0 or tn == N
    flat = [t for p in products for t in p]
    counts = [len(p) for p in products]
    in_specs, args, where, slots = [], [], {}, []

    def operand(arr, spec, kind):
        key = (id(arr), kind)
        if key not in where:
            where[key] = len(args)
            args.append(arr)
            in_specs.append(spec)
        return where[key]

    for a, b, form, *at in flat:
        if form == "tn":
            ia = operand(a, pl.BlockSpec((a.shape[0], tm), lambda i, j: (0, i)), "a_tn")
        else:
            ia = operand(a, pl.BlockSpec((tm, a.shape[1]), lambda i, j: (i, 0)), "a")
        if form == "nt":
            ib = operand(b, pl.BlockSpec((tn, b.shape[1]), lambda i, j: (j, 0)), "b_nt")
        elif form == "nn":
            k = at[0] if at else 0
            ib = operand(b, pl.BlockSpec((a.shape[1], tn), lambda i, j, k=k: (k, j)), "b%d" % k)
        else:
            ib = operand(b, pl.BlockSpec((b.shape[0], tn), lambda i, j: (0, j)), "b")
        slots.append((ia, ib))
    n_mm = len(args)
    for e in extras:
        in_specs.append(pl.BlockSpec((tm, tn), lambda i, j: (i, j)))
        args.append(e)
    for p in params:
        in_specs.append(pl.BlockSpec((1, tn), lambda i, j: (0, j)))
        args.append(p)
    n_in = len(args)
    n_out = len(out_dtypes)

    def body(*refs):
        ins, outs = refs[:n_in], refs[n_in:]
        prods, p = [], 0
        for c in counts:
            acc = None
            for _ in range(c):
                a = ins[slots[p][0]][...].astype(BF)
                b = ins[slots[p][1]][...].astype(BF)
                d = lax.dot_general(a, b, _DIMS[flat[p][2]], preferred_element_type=F32)
                acc = d if acc is None else acc + d
                p += 1
            prods.append(acc)
        ex = [r[...].astype(F32) for r in ins[n_mm:]]
        res = epilogue(*prods, *ex)
        for o, r in zip(outs[:n_out], res[:n_out]):
            o[...] = r.astype(o.dtype)
        for o, r in zip(outs[n_out:], res[n_out:]):
            @pl.when(pl.program_id(0) == 0)
            def _(o=o):
                o[...] = jnp.zeros(o.shape, F32)
            o[...] += r

    return pl.pallas_call(
        body, name=name, grid=(M // tm, N // tn), in_specs=in_specs,
        out_specs=[pl.BlockSpec((tm, tn), lambda i, j: (i, j)) for _ in out_dtypes]
        + [pl.BlockSpec((1, tn), lambda i, j: (0, j)) for _ in range(n_acc)],
        out_shape=[jax.ShapeDtypeStruct((M, N), dt) for dt in out_dtypes] + [jax.ShapeDtypeStruct((1, N), F32)] * n_acc,
        compiler_params=_params(("arbitrary", "arbitrary")),
    )(*args)


def _ident(x):
    return (x,)


def _each(*xs):
    return xs


def _spec(rows, w, off, per_j):
    if per_j:
        return pl.BlockSpec((rows, w), lambda j, i: (i, off + j))
    return pl.BlockSpec((rows, w), lambda j, i: (i, off))


def _pspec(rows, w, off, per_j):
    if per_j:
        return pl.BlockSpec((rows, w), lambda j, i: (0, off + j))
    return pl.BlockSpec((rows, w), lambda j, i: (0, off))


def _rw_fwd(name, fn, rows, params, outs, T, tm, nj, n_acc=0):
    in_specs = [_spec(tm, w, off, pj) for _, w, off, pj in rows] + [_pspec(a.shape[0], w, off, pj) for a, w, off, pj in params]
    args = [r[0] for r in rows] + [p[0] for p in params]
    n_in, n_out = len(args), len(outs)
    out_specs = [pl.BlockSpec((tm, w), lambda j, i: (i, j)) for _, w in outs]
    out_shape = [jax.ShapeDtypeStruct((T, nj * w), dt) for dt, w in outs]
    out_specs += [pl.BlockSpec((1, LANES), lambda j, i: (0, 0)) for _ in range(n_acc)]
    out_shape += [jax.ShapeDtypeStruct((1, LANES), F32) for _ in range(n_acc)]

    def body(*refs):
        vals = [r[...].astype(F32) for r in refs[:n_in]]
        res = fn(*vals)
        orefs = refs[n_in:]
        for k in range(n_out):
            orefs[k][...] = res[k].astype(orefs[k].dtype)
        first = (pl.program_id(0) == 0) & (pl.program_id(1) == 0)
        for k in range(n_acc):
            @pl.when(first)
            def _(k=k):
                orefs[n_out + k][...] = jnp.zeros((1, LANES), F32)
            orefs[n_out + k][...] += res[n_out + k]

    return pl.pallas_call(
        body, name=name, grid=(nj, T // tm), in_specs=in_specs, out_specs=out_specs, out_shape=out_shape,
        compiler_params=_params(("arbitrary", "arbitrary")),
    )(*args)


def _rw_bwd(name, fn, rows, params, cots, T, tm, nj, row_grads, param_grads, resid=None):
    in_specs = ([_spec(tm, w, off, pj) for _, w, off, pj in rows] + [_pspec(a.shape[0], w, off, pj) for a, w, off, pj in params]
                + [_spec(tm, w, off, pj) for _, w, off, pj in cots])
    args = [r[0] for r in rows] + [p[0] for p in params] + [c[0] for c in cots]
    if resid is not None:
        in_specs.append(_spec(tm, rows[0][1], rows[0][2], rows[0][3]))
        args.append(resid)
    nr, npar, nc = len(rows), len(params), len(cots)
    out_specs, out_shape, kinds = [], [], []
    for k, dts in enumerate(row_grads):
        for dt in (dts if isinstance(dts, (list, tuple)) else [dts]):
            if dt is not None:
                w = rows[k][1]
                out_specs.append(pl.BlockSpec((tm, w), lambda j, i: (i, j)))
                out_shape.append(jax.ShapeDtypeStruct((T, nj * w), dt))
                kinds.append(("row", k))
    for k, need in enumerate(param_grads):
        if need:
            a, w, off, pj = params[k]
            out_specs.append(_pspec(a.shape[0], w, off, pj))
            out_shape.append(jax.ShapeDtypeStruct(a.shape, F32))
            kinds.append(("par", k))

    def body(*refs):
        vals = [r[...].astype(F32) for r in refs[:nr + npar]]
        ct = tuple(r[...].astype(F32) for r in refs[nr + npar:nr + npar + nc])
        _, vjp = jax.vjp(lambda *a: tuple(fn(*a)), *vals)
        grads = list(vjp(ct))
        n_in = nr + npar + nc + (resid is not None)
        if resid is not None:
            grads[0] = grads[0] + refs[n_in - 1][...].astype(F32)
        orefs = refs[n_in:]
        j, i = pl.program_id(0), pl.program_id(1)
        for o, (kind, k) in zip(orefs, kinds):
            if kind == "row":
                o[...] = grads[k].astype(o.dtype)
            else:
                first = (i == 0) if params[k][3] else ((i == 0) & (j == 0))

                @pl.when(first)
                def _(o=o):
                    o[...] = jnp.zeros(o.shape, F32)
                o[...] += grads[nr + k]

    return pl.pallas_call(
        body, name=name, grid=(nj, T // tm), in_specs=in_specs, out_specs=out_specs, out_shape=out_shape,
        compiler_params=_params(("arbitrary", "arbitrary")),
    )(*args)


def _rms(x, g):
    return x * lax.rsqrt(jnp.mean(x * x, axis=-1, keepdims=True) + EPS) * g


def _rms_fn(x, g):
    return (_rms(x, g),)


def _lo_mask():
    return lax.broadcasted_iota(jnp.int32, (1, LANES), 1) < HEAD


def _gmean(x, lo):
    s0 = jnp.sum(jnp.where(lo, x, 0.0), axis=-1, keepdims=True)
    s1 = jnp.sum(jnp.where(lo, 0.0, x), axis=-1, keepdims=True)
    return jnp.where(lo, s0, s1) * (1.0 / HEAD)


def _fox_prep_fn(fq, fk, gq, gk):
    lo = _lo_mask()
    qn = fq * lax.rsqrt(_gmean(fq * fq, lo) + EPS) * gq * (HEAD ** -0.5)
    kn = fk * lax.rsqrt(_gmean(fk * fk, lo) + EPS) * gk
    return qn, kn


@jax.custom_vjp
def _swap_halves(x):
    bit = (lax.broadcasted_iota(jnp.int32, (1, LANES), 1) & (HEAD // 2)) == 0
    return jnp.where(bit, pltpu.roll(x, LANES - HEAD // 2, 1), pltpu.roll(x, HEAD // 2, 1))


_swap_halves.defvjp(lambda x: (_swap_halves(x), None), lambda _, g: (_swap_halves(g),))


def _ret_fn(rq, rk, rv, rg, cos, sin, s_in, g, lg):
    tb = rq.shape[0]
    nc = tb // CHUNK
    lo = _lo_mask()
    row = lax.broadcasted_iota(jnp.int32, (LANES, 1), 0) < HEAD
    same_head = row == lo
    q = (rq * cos + _swap_halves(rq) * sin) * (HEAD ** -0.5)
    k = rk * cos + _swap_halves(rk) * sin
    q3, k3, v3 = q.reshape(nc, CHUNK, LANES), k.reshape(nc, CHUNK, LANES), rv.reshape(nc, CHUNK, LANES)
    pos = lax.broadcasted_iota(jnp.int32, (CHUNK, 1), 0).astype(F32)
    q_decay = jnp.exp(lg * (pos + 1.0))
    k_decay = jnp.exp(lg * (CHUNK - 1.0 - pos))
    chunk_decay = jnp.exp(lg * float(CHUNK))
    dist = jnp.abs(lax.broadcasted_iota(jnp.int32, (CHUNK, CHUNK), 0) - lax.broadcasted_iota(jnp.int32, (CHUNK, CHUNK), 1)).astype(F32)
    v3b = v3.astype(BF)
    intra = []
    for hh in range(2):
        hm = lo if hh == 0 else ~lo
        lg_h = lg[:, hh * HEAD:hh * HEAD + 1]
        qm = jnp.where(hm, q3, 0.0).astype(BF)
        sc = jnp.einsum("nid,njd->nij", qm, k3.astype(BF), preferred_element_type=F32) * jnp.exp(lg_h * dist)[None]
        intra.append(jnp.einsum("nij,nje->nie", sc.astype(BF), v3b, preferred_element_type=F32))
    o = jnp.where(lo, intra[0], intra[1])
    kv = jnp.einsum("njd,nje->nde", (k3 * k_decay[None]).astype(BF), v3b, preferred_element_type=F32)
    kv = jnp.where(same_head[None], kv, 0.0)
    state, states = s_in, []
    for n in range(nc):
        states.append(state)
        state = state * chunk_decay + kv[n]
    s_prev = jnp.stack(states, axis=0)
    o = o + jnp.einsum("nid,nde->nie", (q3 * q_decay[None]).astype(BF), s_prev.astype(BF), preferred_element_type=F32)
    o = o.reshape(tb, LANES)
    mu = _gmean(o, lo)
    oc = o - mu
    y = oc * lax.rsqrt(_gmean(oc * oc, lo) + EPS) * g
    return jax.nn.silu(rg) * y, state


def _xattn_fn(qx, gq, gk, kk, vv):
    q = _rms(qx, gq)
    k = _rms(kk, gk)
    logits = lax.dot_general(q.astype(BF), k.astype(BF), NT, preferred_element_type=F32) * (XHEAD ** -0.5)
    p = jax.nn.softmax(logits, axis=-1)
    return (jnp.dot(p.astype(BF), vv.astype(BF), preferred_element_type=F32),)


def _swiglu_fwd_epi(g, u):
    return g, u, jax.nn.silu(g) * u


def _swiglu_bwd_epi(dact, g, u):
    _, vjp = jax.vjp(lambda a, b: jax.nn.silu(a) * b, g, u)
    return vjp(dact)


def _add_rms_epi(acc, resid, g):
    h = acc + resid
    return h, _rms(h, g)


def _add_loss_epi(acc, resid, target):
    err = (acc + resid) - target
    dy = err * (1.0 / D)
    part = jnp.sum(jnp.sum(err * err, axis=0, keepdims=True), axis=1, keepdims=True) * (0.5 / D)
    return dy, dy, jnp.broadcast_to(part, (1, err.shape[1]))


def _rms_bwd_epi(dhn, h, skip, g):
    _, vjp = jax.vjp(_rms, h, g)
    dh, dg = vjp(dhn)
    dh = dh + skip
    return dh, dh, dg


def _rms_bwd_first_epi(dhn, h, skip, g):
    return _rms_bwd_epi(dhn, h, skip, g)[1:]


def _ret_fwd(P, cos, sin, g_ret, lg, T, tb):
    nb = T // tb

    def body(rq, rk, rv, rg, c, s, g, l, o_ref, s0_ref, state):
        @pl.when(pl.program_id(1) == 0)
        def _():
            state[...] = jnp.zeros(state.shape, F32)
        s0_ref[0, 0] = state[...]
        out, s_new = _ret_fn(rq[...], rk[...], rv[...], rg[...], c[...], s[...], state[...], g[...], l[...])
        o_ref[...] = out.astype(o_ref.dtype)
        state[...] = s_new

    sec = lambda off: pl.BlockSpec((tb, LANES), lambda j, i: (i, off + j))
    tab = pl.BlockSpec((tb, LANES), lambda j, i: (i, 0))
    par = pl.BlockSpec((1, LANES), lambda j, i: (0, j))
    return pl.pallas_call(
        body, name="ret_fwd", grid=(4, nb),
        in_specs=[sec(0), sec(4), sec(8), sec(12), tab, tab, par, par],
        out_specs=[pl.BlockSpec((tb, LANES), lambda j, i: (i, j)), pl.BlockSpec((1, 1, LANES, LANES), lambda j, i: (j, i, 0, 0))],
        out_shape=[jax.ShapeDtypeStruct((T, 4 * LANES), BF), jax.ShapeDtypeStruct((4, nb, LANES, LANES), F32)],
        scratch_shapes=[pltpu.VMEM((LANES, LANES), F32)],
        compiler_params=_params(("arbitrary", "arbitrary")),
    )(P, P, P, P, cos, sin, g_ret, lg)


def _ret_bwd(P, cos, sin, g_ret, lg, s0, dmix, T, tb):
    nb = T // tb

    def body(rq, rk, rv, rg, c, s, g, l, s0_ref, do, drq, drk, drv, drg, dg, dstate):
        i = pl.program_id(1)

        @pl.when(i == 0)
        def _():
            dstate[...] = jnp.zeros(dstate.shape, F32)
            dg[...] = jnp.zeros(dg.shape, F32)

        cc, ss, ll = c[...], s[...], l[...]
        _, vjp = jax.vjp(lambda a, b, v, gate, st, gg: _ret_fn(a, b, v, gate, cc, ss, st, gg, ll),
                         rq[...], rk[...], rv[...], rg[...], s0_ref[0, 0], g[...])
        ga, gb, gv, ggate, gst, ggain = vjp((do[...], dstate[...]))
        drq[...] = ga.astype(drq.dtype)
        drk[...] = gb.astype(drk.dtype)
        drv[...] = gv.astype(drv.dtype)
        drg[...] = ggate.astype(drg.dtype)
        dstate[...] = gst
        dg[...] += ggain

    rev = lambda i: nb - 1 - i
    sec = lambda off: pl.BlockSpec((tb, LANES), lambda j, i: (rev(i), off + j))
    tab = pl.BlockSpec((tb, LANES), lambda j, i: (rev(i), 0))
    par = pl.BlockSpec((1, LANES), lambda j, i: (0, j))
    outb = pl.BlockSpec((tb, LANES), lambda j, i: (rev(i), j))
    return pl.pallas_call(
        body, name="ret_bwd", grid=(4, nb),
        in_specs=[sec(0), sec(4), sec(8), sec(12), tab, tab, par, par,
                  pl.BlockSpec((1, 1, LANES, LANES), lambda j, i: (j, rev(i), 0, 0)), outb],
        out_specs=[outb, outb, outb, outb, par],
        out_shape=[jax.ShapeDtypeStruct((T, 4 * LANES), BF)] * 4 + [jax.ShapeDtypeStruct((1, 4 * LANES), F32)],
        scratch_shapes=[pltpu.VMEM((LANES, LANES), F32)],
        compiler_params=_params(("arbitrary", "arbitrary")),
    )(P, P, P, P, cos, sin, g_ret, lg, s0, dmix)


_FB = 128


def _tri(lower):
    r = lax.broadcasted_iota(jnp.int32, (_FB, _FB), 0)
    c = lax.broadcasted_iota(jnp.int32, (_FB, _FB), 1)
    return ((r >= c) if lower else (r <= c)).astype(F32)


def _fgate_fwd(ffp, bpad, T):
    def body(ff_ref, b_ref, fc_ref, fr_ref):
        lane = lax.broadcasted_iota(jnp.int32, (1, LANES), 1)
        tri = _tri(True)
        carry = jnp.zeros((1, LANES), F32)
        for blk in range(T // _FB):
            z = ff_ref[blk * _FB:(blk + 1) * _FB, :] + b_ref[...]
            lf = jnp.where(lane < 8, jax.nn.log_sigmoid(z), 0.0)
            f = jnp.dot(tri, lf, precision=lax.Precision.HIGHEST, preferred_element_type=F32) + carry
            carry = f[_FB - 1:_FB, :]
            fc_ref[blk * _FB:(blk + 1) * _FB, :] = f
            fr_ref[:, blk * _FB:(blk + 1) * _FB] = f.T[:8, :]

    return pl.pallas_call(
        body, name="fgate_fwd",
        out_shape=[jax.ShapeDtypeStruct((T, LANES), F32), jax.ShapeDtypeStruct((8, T), F32)],
        compiler_params=pltpu.CompilerParams(vmem_limit_bytes=VMEM_LIMIT),
    )(ffp, bpad)


_BIAS_LANE = HEAD


def _head_bias_col(fc, head):
    lane = lax.broadcasted_iota(jnp.int32, (1, LANES), 1)
    return jnp.sum(jnp.where(lane == head, fc, 0.0), axis=-1, keepdims=True)


def _split3(f):
    hi = f.astype(BF).astype(F32)
    mid = (f - hi).astype(BF).astype(F32)
    lo = ((f - hi) - mid).astype(BF).astype(F32)
    return hi, mid, lo


def _fox_operands(P, fc, g_fq2, g_fk2, T, tm):
    def body(fq_ref, fk_ref, fv_ref, fc_ref, gq_ref, gk_ref, qat_ref, ka_ref, kat_ref, va_ref, vat_ref):
        j = pl.program_id(0)
        lane = lax.broadcasted_iota(jnp.int32, (1, LANES), 1)
        qn, kn = _fox_prep_fn(fq_ref[...], fk_ref[...], gq_ref[...], gk_ref[...])
        v = fv_ref[...]
        fcb = fc_ref[...]
        b = _BIAS_LANE
        for hh in range(2):
            hi, mid, lo = _split3(_head_bias_col(fcb, 2 * j + hh))
            take = (lambda a: a) if hh == 0 else (lambda a: pltpu.roll(a, HEAD, 1))
            qa = jnp.where(lane < HEAD, take(qn), jnp.where(lane == b, hi, jnp.where(lane == b + 1, mid, jnp.where(
                lane == b + 2, lo, jnp.where(lane < b + 6, 1.0, 0.0)))))
            ka = jnp.where(lane < HEAD, take(kn), jnp.where(lane < b + 3, 1.0, jnp.where(lane == b + 3, -hi, jnp.where(
                lane == b + 4, -mid, jnp.where(lane == b + 5, -lo, 0.0)))))
            va = jnp.where(lane < HEAD, take(v), 0.0)
            qat_ref[hh] = qa.T.astype(BF)
            for val, ref, tref in ((ka, ka_ref, kat_ref), (va, va_ref, vat_ref)):
                ref[hh] = val.astype(BF)
                tref[hh] = val.T.astype(BF)

    sec = lambda off: pl.BlockSpec((tm, LANES), lambda j, i: (i, off + j))
    par = pl.BlockSpec((1, LANES), lambda j, i: (0, 0))
    nat = pl.BlockSpec((2, tm, LANES), lambda j, i: (j, i, 0))
    trn = pl.BlockSpec((2, LANES, tm), lambda j, i: (j, 0, i))
    return pl.pallas_call(
        body, name="fox_operands", grid=(4, T // tm),
        in_specs=[sec(16), sec(20), sec(24), pl.BlockSpec((tm, LANES), lambda j, i: (i, 0)), par, par],
        out_specs=[trn, nat, trn, nat, trn],
        out_shape=[jax.ShapeDtypeStruct((8, LANES, T), BF)]
        + [jax.ShapeDtypeStruct((8, T, LANES), BF), jax.ShapeDtypeStruct((8, LANES, T), BF)] * 2,
        compiler_params=_params(("parallel", "arbitrary")),
    )(P, P, P, fc, g_fq2, g_fk2)


def _fox_forward(qat, ka, vat, T, tq, tk):
    nq, per = T // tq, tq // tk
    assert per == 2
    RC = 64

    def body(qat_ref, ka_ref, vat_ref, o_ref, lse_ref, s_scr, p_scr, a_scr, m_scr, l_scr, acc_scr):
        i = pl.program_id(1)
        sub = lax.broadcasted_iota(jnp.int32, (8, 1), 0)
        row = lax.broadcasted_iota(jnp.int32, (RC, tq), 0)
        col = lax.broadcasted_iota(jnp.int32, (RC, tq), 1)
        m_scr[...] = jnp.full(m_scr.shape, NEG, F32)
        l_scr[...] = jnp.zeros(l_scr.shape, F32)
        acc_scr[...] = jnp.zeros(acc_scr.shape, F32)

        def scores(slot, kb):
            k0 = pl.multiple_of(kb * tk, tk)
            for hh in range(2):
                s_scr[slot, hh] = jnp.dot(ka_ref[hh, pl.ds(k0, tk), :], qat_ref[hh], preferred_element_type=F32)

        def softmax(slot, kb, diagonal):
            shift = kb * tk - i * tq
            for hh in range(2):
                def masked(r):
                    tile = s_scr[slot, hh, r * RC:(r + 1) * RC, :]
                    return jnp.where(row + (r * RC + shift) <= col, tile, NEG) if diagonal else tile

                mx = jnp.max(masked(0), axis=0, keepdims=True)
                for r in range(1, tk // RC):
                    mx = jnp.maximum(mx, jnp.max(masked(r), axis=0, keepdims=True))
                m_old = m_scr[hh, 0:1, :]
                m2 = jnp.maximum(m_old, mx)
                a = jnp.exp(m_old - m2)
                lsum = jnp.zeros((1, tq), F32)
                for r in range(tk // RC):
                    p = jnp.exp(masked(r) - m2)
                    p_scr[slot, hh, r * RC:(r + 1) * RC, :] = p.astype(BF)
                    lsum = lsum + jnp.sum(p, axis=0, keepdims=True)
                m_scr[hh] = jnp.broadcast_to(m2, (8, tq))
                l_scr[hh] = jnp.broadcast_to(a * l_scr[hh, 0:1, :] + lsum, (8, tq))
                a_scr[slot, hh] = jnp.broadcast_to(a, (8, tq))

        def values(slot, kb):
            k0 = pl.multiple_of(kb * tk, tk)
            for hh in range(2):
                pv = jnp.dot(vat_ref[hh, 0:HEAD, pl.ds(k0, tk)], p_scr[slot, hh], preferred_element_type=F32)
                acc_scr[hh] = a_scr[slot, hh, 0:1, :] * acc_scr[hh] + pv

        def pair(kb, diag_first, diag_second, more):
            if more:
                scores(0, kb + 2)
            softmax(1, kb + 1, diag_first)
            values(0, kb)
            if more:
                scores(1, kb + 3)
                softmax(0, kb + 2, diag_second)
            values(1, kb + 1)

        scores(0, 0)
        scores(1, 1)
        softmax(0, 0, True)

        @pl.loop(0, jnp.maximum(i - 1, 0))
        def _(t):
            pair(2 * t, False, False, True)

        @pl.when(i >= 1)
        def _():
            pair(2 * (i - 1), False, True, True)

        pair(2 * i, True, False, False)

        o_ref[...] = jnp.concatenate([acc_scr[hh] / l_scr[hh, 0:1, :] for hh in range(2)], axis=0).T
        lses = [m_scr[hh, 0:1, :] + jnp.log(l_scr[hh, 0:1, :]) for hh in range(2)]
        lse_ref[0] = jnp.where(sub == 0, lses[0], jnp.where(sub == 1, lses[1], 0.0))

    return pl.pallas_call(
        body, name="fox_forward", grid=(4, nq),
        in_specs=[pl.BlockSpec((2, LANES, tq), lambda j, i: (j, 0, i)), pl.BlockSpec((2, T, LANES), lambda j, i: (j, 0, 0)),
                  pl.BlockSpec((2, LANES, T), lambda j, i: (j, 0, 0))],
        out_specs=[pl.BlockSpec((tq, LANES), lambda j, i: (i, j)), pl.BlockSpec((1, 8, tq), lambda j, i: (j, 0, i))],
        out_shape=[jax.ShapeDtypeStruct((T, 4 * LANES), F32), jax.ShapeDtypeStruct((4, 8, T), F32)],
        scratch_shapes=[pltpu.VMEM((2, 2, tk, tq), F32), pltpu.VMEM((2, 2, tk, tq), BF), pltpu.VMEM((2, 2, 8, tq), F32),
                        pltpu.VMEM((2, 8, tq), F32), pltpu.VMEM((2, 8, tq), F32), pltpu.VMEM((2, HEAD, tq), F32)],
        compiler_params=_params(("parallel", "arbitrary")),
    )(qat, ka, vat)


def _fox_cotangent(dmix, fox, T, tm):
    def body(do_ref, o_ref, doat_ref, dl_ref):
        lane = lax.broadcasted_iota(jnp.int32, (1, LANES), 1)
        sub = lax.broadcasted_iota(jnp.int32, (8, 1), 0)
        dob = do_ref[...].astype(BF).astype(F32)
        prod_t = (dob * o_ref[...]).T
        d0 = jnp.sum(prod_t[:HEAD], axis=0, keepdims=True)
        d1 = jnp.sum(prod_t[HEAD:], axis=0, keepdims=True)
        dl_ref[0] = jnp.where(sub == 0, d0, jnp.where(sub == 1, d1, 0.0))
        for hh in range(2):
            val = jnp.where(lane < HEAD, dob if hh == 0 else pltpu.roll(dob, HEAD, 1), 0.0)
            doat_ref[hh] = val.T.astype(BF)

    return pl.pallas_call(
        body, name="fox_cotangent", grid=(4, T // tm),
        in_specs=[pl.BlockSpec((tm, LANES), lambda j, i: (i, 4 + j)), pl.BlockSpec((tm, LANES), lambda j, i: (i, j))],
        out_specs=[pl.BlockSpec((2, LANES, tm), lambda j, i: (j, 0, i)), pl.BlockSpec((1, 8, tm), lambda j, i: (j, 0, i))],
        out_shape=[jax.ShapeDtypeStruct((8, LANES, T), BF), jax.ShapeDtypeStruct((4, 8, T), F32)],
        compiler_params=_params(("parallel", "arbitrary")),
    )(dmix, fox)


def _fox_backward(qat, ka, kat, va, doat, lse, dl, T, tq, tk):
    nq, nk = T // tq, T // tk

    def body(qat_ref, ka_ref, kat_ref, va_ref, doat_ref, lse_ref, dl_ref,
             dq_ref, dk_ref, dv_ref, df_ref, dr_ref, dqt, dkt, dvt, df_acc, sdp, pds):
        j, kb = pl.program_id(0), pl.program_id(1)
        lane = lax.broadcasted_iota(jnp.int32, (1, LANES), 1)
        first = (kb * tk) // tq

        @pl.when(kb == 0)
        def _():
            dqt[...] = jnp.zeros(dqt.shape, F32)

        dkt[...] = jnp.zeros(dkt.shape, F32)
        dvt[...] = jnp.zeros(dvt.shape, F32)
        df_acc[...] = jnp.zeros(df_acc.shape, F32)

        RC = 64
        last = nq - 1

        def products(slot, qi):
            q0 = pl.multiple_of(qi * tq, tq)
            for hh in range(2):
                sdp[slot, hh, 0] = jnp.dot(ka_ref[hh], qat_ref[hh, :, pl.ds(q0, tq)], preferred_element_type=F32)
                sdp[slot, hh, 1] = jnp.dot(va_ref[hh], doat_ref[hh, :, pl.ds(q0, tq)], preferred_element_type=F32)

        def softmax_bwd(slot, qi, diagonal, valid):
            q0 = pl.multiple_of(qi * tq, tq)
            shift = kb * tk - first * tq
            col = lax.broadcasted_iota(jnp.int32, (RC, tq), 1)
            row = lax.broadcasted_iota(jnp.int32, (RC, tq), 0)
            for hh in range(2):
                lse_row = lse_ref[0, hh:hh + 1, pl.ds(q0, tq)]
                dl_row = dl_ref[0, hh:hh + 1, pl.ds(q0, tq)]
                rsum = jnp.zeros((1, tq), F32)
                for r in range(tk // RC):
                    rows = slice(r * RC, (r + 1) * RC)
                    p = jnp.exp(sdp[slot, hh, 0, rows, :] - lse_row)
                    p = jnp.where((row + (r * RC + shift) <= col) if diagonal else valid, p, 0.0)
                    ds = p * (sdp[slot, hh, 1, rows, :] - dl_row)
                    pds[slot, hh, 0, rows, :] = p.astype(BF)
                    pds[slot, hh, 1, rows, :] = ds.astype(BF)
                    rsum = rsum + jnp.sum(ds, axis=0, keepdims=True)
                    part = ds[:, 0:LANES]
                    for c in range(1, tq // LANES):
                        part = part + ds[:, c * LANES:(c + 1) * LANES]
                    df_acc[hh, rows, :] += part
                dqt[hh, HEAD:HEAD + 8, pl.ds(q0, tq)] += jnp.broadcast_to(rsum, (8, tq))

        def accumulate(slot, qi):
            q0 = pl.multiple_of(qi * tq, tq)
            for hh in range(2):
                dvt[hh] += lax.dot_general(doat_ref[hh, 0:HEAD, pl.ds(q0, tq)], pds[slot, hh, 0], NT, preferred_element_type=F32)
                dkt[hh] += lax.dot_general(qat_ref[hh, 0:HEAD, pl.ds(q0, tq)], pds[slot, hh, 1], NT, preferred_element_type=F32)
                dqt[hh, 0:HEAD, pl.ds(q0, tq)] += jnp.dot(kat_ref[hh, 0:HEAD, :], pds[slot, hh, 1], preferred_element_type=F32)

        products(0, first)
        products(1, jnp.minimum(first + 1, last))
        softmax_bwd(0, first, True, None)

        @pl.loop(0, (nq - first + 1) // 2)
        def _(t):
            qi = first + 2 * t
            products(0, jnp.minimum(qi + 2, last))
            softmax_bwd(1, jnp.minimum(qi + 1, last), False, qi + 1 <= last)
            accumulate(0, qi)
            products(1, jnp.minimum(qi + 3, last))
            softmax_bwd(0, jnp.minimum(qi + 2, last), False, qi + 2 <= last)
            accumulate(1, jnp.minimum(qi + 1, last))

        dk_ref[...] = jnp.concatenate([dkt[0], dkt[1]], axis=0).T
        dv_ref[...] = jnp.concatenate([dvt[0], dvt[1]], axis=0).T.astype(dv_ref.dtype)
        f0 = -jnp.sum(df_acc[0], axis=1, keepdims=True)
        f1 = -jnp.sum(df_acc[1], axis=1, keepdims=True)
        df_ref[0] = jnp.where(lane == 2 * j, f0, jnp.where(lane == 2 * j + 1, f1, 0.0))

        @pl.when(kb == nk - 1)
        def _():
            for t in range(nq):
                cols = slice(t * tq, (t + 1) * tq)
                dq_ref[cols, :] = jnp.concatenate([dqt[0, 0:HEAD, cols], dqt[1, 0:HEAD, cols]], axis=0).T
                rsum = jnp.concatenate([dqt[0, HEAD:HEAD + 8, cols], dqt[1, HEAD:HEAD + 8, cols],
                                        jnp.zeros((LANES - 16, tq), F32)], axis=0).T
                dr_ref[0, cols, :] = jnp.where(lane == 2 * j, rsum[:, 0:1], jnp.where(lane == 2 * j + 1, rsum[:, 8:9], 0.0))

    trn_full = pl.BlockSpec((2, LANES, T), lambda j, kb: (j, 0, 0))
    nat_blk = pl.BlockSpec((2, tk, LANES), lambda j, kb: (j, kb, 0))
    trn_blk = pl.BlockSpec((2, LANES, tk), lambda j, kb: (j, 0, kb))
    rows = pl.BlockSpec((1, 8, T), lambda j, kb: (j, 0, 0))
    blk = pl.BlockSpec((tk, LANES), lambda j, kb: (kb, j))
    return pl.pallas_call(
        body, name="fox_backward", grid=(4, nk),
        in_specs=[trn_full, nat_blk, trn_blk, nat_blk, trn_full, rows, rows],
        out_specs=[pl.BlockSpec((T, LANES), lambda j, kb: (0, j)), blk, blk, pl.BlockSpec((1, tk, LANES), lambda j, kb: (j, kb, 0)),
                   pl.BlockSpec((1, T, LANES), lambda j, kb: (j, 0, 0))],
        out_shape=[jax.ShapeDtypeStruct((T, 4 * LANES), F32), jax.ShapeDtypeStruct((T, 4 * LANES), F32),
                   jax.ShapeDtypeStruct((T, 4 * LANES), BF), jax.ShapeDtypeStruct((4, T, LANES), F32),
                   jax.ShapeDtypeStruct((4, T, LANES), F32)],
        scratch_shapes=[pltpu.VMEM((2, HEAD + 8, T), F32), pltpu.VMEM((2, HEAD, tk), F32), pltpu.VMEM((2, HEAD, tk), F32),
                        pltpu.VMEM((2, tk, LANES), F32), pltpu.VMEM((2, 2, 2, tk, tq), F32), pltpu.VMEM((2, 2, 2, tk, tq), BF)],
        compiler_params=_params(("arbitrary", "arbitrary")),
    )(qat, ka, kat, va, doat, lse, dl)


def _fgate_bwd_col(ffp, bpad, dfc4, drc4, T):
    def body(ff_ref, b_ref, dfc_ref, drc_ref, dff_ref, db_ref):
        lane = lax.broadcasted_iota(jnp.int32, (1, LANES), 1)
        tri = _tri(False)
        carry = jnp.zeros((1, LANES), F32)
        db = jnp.zeros((1, LANES), F32)
        for blk in reversed(range(T // _FB)):
            rows = slice(blk * _FB, (blk + 1) * _FB)
            dcol = dfc_ref[0, rows, :] + drc_ref[0, rows, :]
            for pair in range(1, 4):
                dcol = dcol + (dfc_ref[pair, rows, :] + drc_ref[pair, rows, :])
            dlf = jnp.dot(tri, dcol, precision=lax.Precision.HIGHEST, preferred_element_type=F32) + carry
            carry = dlf[0:1, :]
            z = ff_ref[blk * _FB:(blk + 1) * _FB, :] + b_ref[...]
            dz = jnp.where(lane < 8, dlf * jax.nn.sigmoid(-z), 0.0)
            dff_ref[blk * _FB:(blk + 1) * _FB, :] = dz.astype(dff_ref.dtype)
            db = db + jnp.sum(dz, axis=0, keepdims=True)
        db_ref[...] = db

    return pl.pallas_call(
        body, name="fgate_bwd",
        out_shape=[jax.ShapeDtypeStruct((T, LANES), BF), jax.ShapeDtypeStruct((1, LANES), F32)],
        compiler_params=pltpu.CompilerParams(vmem_limit_bytes=VMEM_LIMIT),
    )(ffp, bpad, dfc4, drc4)


MESH = pl.DeviceIdType.MESH
N_PEERS = N_DEV - 1


def _place():
    return lax.axis_index("x"), lax.axis_index("y"), lax.axis_index("c")


def _all_gather(shard, rows, g, tm):
    R, W = shard.shape
    T = rows.shape[0]
    steps = T // tm

    def body(w_ref, rows_ref, g_ref, out_ref, norm_ref, send_sems, recv_sems, local_sem):
        x, y, c = _place()
        me, sibling = (x, y, c), (x, y, 1 - c)
        chips = [(1 - x, y), (x, 1 - y), (1 - x, 1 - y)]

        def slot(px, py, pc):
            return out_ref.at[4 * px + 2 * py + pc]

        def copy(k, block, to, src=None):
            return pltpu.make_async_remote_copy(
                src_ref=slot(*block) if src is None else src, dst_ref=slot(*block),
                send_sem=send_sems.at[k], recv_sem=recv_sems.at[k], device_id=to, device_id_type=MESH)

        mine = pltpu.make_async_copy(w_ref, slot(*me), local_sem)
        first = [copy(0, me, sibling, src=w_ref)]
        first += [copy(1 + n, me, (*chip, c), src=w_ref) for n, chip in enumerate(chips)]
        passed = [copy(4 + n, (*chip, c), sibling) for n, chip in enumerate(chips)]

        @pl.when(pl.program_id(0) == 0)
        def _():
            mine.start()
            for cp in first:
                cp.start()

        norm_ref[...] = _rms(rows_ref[...], g_ref[...]).astype(norm_ref.dtype)

        @pl.when(pl.program_id(0) == steps - 1)
        def _():
            for n, chip in enumerate(chips):
                copy(1 + n, (*chip, c), me).wait_recv()
                passed[n].start()
            copy(0, sibling, me).wait_recv()
            for n, chip in enumerate(chips):
                copy(4 + n, (*chip, 1 - c), me).wait_recv()
            for cp in first + passed:
                cp.wait_send()
            mine.wait()

    tile = pl.BlockSpec((tm, D), lambda i: (i, 0))
    return pl.pallas_call(
        body, name="all_gather_weights", grid=(steps,),
        out_shape=[jax.ShapeDtypeStruct((N_DEV, R, W), shard.dtype), jax.ShapeDtypeStruct((T, D), BF)],
        in_specs=[pl.BlockSpec(memory_space=pl.ANY), tile, pl.BlockSpec((1, D), lambda i: (0, 0))],
        out_specs=[pl.BlockSpec(memory_space=pl.ANY), tile],
        scratch_shapes=[pltpu.SemaphoreType.DMA((N_PEERS,)), pltpu.SemaphoreType.DMA((N_PEERS,)), pltpu.SemaphoreType.DMA],
        compiler_params=_params(("arbitrary",)),
    )(shard, rows, g)


def _exchange_copies(src_refs, land_refs, send_sems, recv_sems, scatter):
    x, y, c = _place()
    me = 4 * x + 2 * y + c
    copies = []
    for k, (src_ref, land_ref) in enumerate(zip(src_refs, land_refs)):
        for r in range(1, N_DEV):
            px, py, pc = x ^ (r >> 2), y ^ ((r >> 1) & 1), c ^ (r & 1)
            copies.append(pltpu.make_async_remote_copy(
                src_ref=src_ref.at[4 * px + 2 * py + pc] if scatter else src_ref, dst_ref=land_ref.at[me],
                send_sem=send_sems.at[k * N_PEERS + r - 1], recv_sem=recv_sems.at[k * N_PEERS + r - 1],
                device_id=(px, py, pc), device_id_type=MESH))
    return copies


_HBM = pl.BlockSpec(memory_space=pltpu.HBM)
_SEM = pl.BlockSpec(memory_space=pltpu.SEMAPHORE)
_EFFECT = pltpu.SideEffectType.DATAFLOW_SIDE_EFFECTING


def _exchange_start(name, srcs, lands, scatter):
    n = len(srcs)

    def body(*refs):
        send_sems, recv_sems, token, own_sems = refs[2 * n], refs[2 * n + 1], refs[4 * n + 2], refs[-1]
        for cp in _exchange_copies(refs[:n], refs[n:2 * n], send_sems, recv_sems, scatter):
            cp.start()
        token[...] = jnp.zeros(token.shape, F32)
        if not scatter:
            x, y, c = _place()
            own = [pltpu.make_async_copy(refs[k], refs[n + k].at[4 * x + 2 * y + c], own_sems.at[k]) for k in range(n)]
            for cp in own:
                cp.start()
            for cp in own:
                cp.wait()

    arrays = list(srcs) + list(lands)
    out = pl.pallas_call(
        body, name=name,
        out_shape=(pltpu.SemaphoreType.DMA((n * N_PEERS,)), pltpu.SemaphoreType.DMA((n * N_PEERS,)))
        + tuple(pltpu.HBM(a.shape, a.dtype) for a in arrays) + (jax.ShapeDtypeStruct((8, LANES), F32),),
        in_specs=(_HBM,) * (2 * n), out_specs=(_SEM, _SEM) + (_HBM,) * (2 * n) + (pl.BlockSpec(memory_space=pltpu.VMEM),),
        input_output_aliases={k: 2 + k for k in range(2 * n)},
        scratch_shapes=[] if scatter else [pltpu.SemaphoreType.DMA((n,))],
        compiler_params=pltpu.CompilerParams(has_side_effects=_EFFECT),
    )(*(pltpu.with_memory_space_constraint(a, pltpu.HBM) for a in arrays))
    return out[0], out[1], out[2:2 + n], out[2 + n:2 + 2 * n], out[-1]


def _exchange_wait(name, started, after, scatter):
    send_sems, recv_sems, srcs, lands, _ = started
    n = len(srcs)

    def body(*refs):
        copies = _exchange_copies(refs[:n], refs[n:2 * n], refs[2 * n], refs[2 * n + 1], scatter)
        for cp in copies:
            cp.wait_send()
        for cp in copies:
            cp.wait_recv()

    arrays = list(srcs) + list(lands)
    out = pl.pallas_call(
        body, name=name,
        out_shape=tuple(pltpu.HBM(a.shape, a.dtype) for a in arrays),
        in_specs=(_HBM,) * (2 * n) + (_SEM, _SEM, pl.BlockSpec(memory_space=pl.ANY)), out_specs=(_HBM,) * (2 * n),
        input_output_aliases={k: k for k in range(2 * n)},
        compiler_params=pltpu.CompilerParams(has_side_effects=_EFFECT),
    )(*arrays, send_sems, recv_sems, after)
    return out[:n], out[n:]


def _adam_update(g, w, m, v):
    m2 = ADAM_B1 * m + (1.0 - ADAM_B1) * g
    v2 = ADAM_B2 * v + (1.0 - ADAM_B2) * jnp.square(g)
    m_hat = m2 / (1.0 - ADAM_B1 ** ADAM_STEP)
    v_hat = v2 / (1.0 - ADAM_B2 ** ADAM_STEP)
    return g, -ADAM_LR * (m_hat / (jnp.sqrt(v_hat) + ADAM_EPS) + ADAM_WD * w), m2, v2


def _adamw(name, me, slots, sent, w, m, v):
    R, W = w.shape
    steps = max(k for k in (4, 2, 1) if k == 1 or (R % k == 0 and (R // k) % 16 == 0))
    tr = R // steps

    def body(me_ref, s_ref, *refs):
        if sent is not None:
            g = refs[0][0].astype(F32)
            refs = refs[1:]
        else:
            g = jnp.zeros((tr, W), F32)
        for s in range(N_DEV):
            part = s_ref[s].astype(F32)
            g = g + (part if sent is None else jnp.where(me_ref[0] == s, 0.0, part))
        w_ref, m_ref, v_ref = refs[:3]
        for o, r in zip(refs[3:], _adam_update(g, w_ref[...], m_ref[...], v_ref[...])):
            o[...] = r

    rows = pl.BlockSpec((tr, W), lambda i, me_ref: (i, 0))
    in_specs = [pl.BlockSpec((N_DEV, tr, W), lambda i, me_ref: (0, i, 0))]
    args = [slots]
    if sent is not None:
        in_specs.append(pl.BlockSpec((1, tr, W), lambda i, me_ref: (me_ref[0], i, 0)))
        args.append(sent)
    return pl.pallas_call(
        body, name=name,
        grid_spec=pltpu.PrefetchScalarGridSpec(num_scalar_prefetch=1, grid=(steps,), in_specs=in_specs + [rows] * 3,
                                               out_specs=[rows] * 4),
        out_shape=[jax.ShapeDtypeStruct((R, W), F32)] * 4,
        compiler_params=_params(("arbitrary",)),
    )(me, *args, w, m, v)


def _tables(T):
    pos = jnp.arange(T, dtype=F32)
    inv_freq = 10000.0 ** (-jnp.arange(0, HEAD, 2, dtype=F32) / HEAD)
    ang = pos[:, None] * inv_freq[None, :]
    cos, sin = jnp.cos(ang), jnp.sin(ang)
    cos4 = jnp.tile(cos, (1, 4))
    sin4 = jnp.tile(jnp.concatenate([-sin, sin], axis=1), (1, 2))
    log_g = jnp.log(1.0 - 2.0 ** (-5.0 - jnp.arange(8, dtype=F32)))
    return cos4, sin4, jnp.repeat(log_g, HEAD)[None, :]


def _local_step(x, hn1, mem, target, sp, w_inT, token, fetch_rest, push, push_small):
    T = x.shape[0]
    tm = min(512, T)
    tq = min(256, T)
    tb = min(1024, T)
    cos4, sin4, lg = _tables(T)
    g_fq2 = jnp.tile(sp["g_fox_q"], (1, 2))
    g_fk2 = jnp.tile(sp["g_fox_k"], (1, 2))
    g_ret = sp["g_ret_out"].reshape(1, 8 * HEAD)
    bpad = jnp.pad(sp["b_forget"], ((0, 0), (0, LANES - 8)))
    w_ffT = jnp.pad(w_inT[3584:3592], ((0, LANES - 8), (0, 0)))
    tie = lambda p, tok: p + tok[0:1, 0:1]
    tm2, tm4 = min(1024, T), min(2048, T)

    P, = _mm("proj_in", [[(hn1, w_inT, "nt")]], [], lambda acc, after: (acc,), T, 3584, tm4, 512, [F32],
             params=[jnp.broadcast_to(token[0:1, 0:1], (1, 3584))])
    ffp, = _mm("proj_ff", [[(hn1, w_ffT, "nt")]], [], _ident, T, LANES, tm, LANES, [F32])
    ret, s0 = _ret_fwd(P, cos4, sin4, g_ret, lg, T, tb)
    fc, _ = _fgate_fwd(ffp, bpad, T)
    qat, ka, kat, va, vat = _fox_operands(P, fc, g_fq2, g_fk2, T, tm4)
    fox, lse = _fox_forward(qat, ka, vat, T, min(512, T), tq)
    W = fetch_rest("attn", fox)
    h1, hn2 = _mm("proj_out", [[(ret, W["w_out"], "nn", 0), (fox, W["w_out"], "nn", 1)]], [x], _add_rms_epi, T, D, tm2, D,
                  [F32, BF], params=[sp["g_xattn"]])

    qx, = _mm("proj_xq", [[(hn2, W["w_xq"], "nn")]], [], _ident, T, D, tm2, D, [F32])
    memn, = _rw_fwd("rms_mem", _rms_fn, [(mem, D, 0, False)], [(sp["g_mem"], D, 0, False)], [(BF, D)], N_MEM, N_MEM, 1)
    kv, = _mm("proj_xkv", [[(memn, W["w_xkvT"], "nt")]], [], _ident, N_MEM, 2 * D, N_MEM, 512, [F32])
    xa_rows = [(qx, XHEAD, 0, True)]
    xa_params = [(sp["g_xq"], XHEAD, 0, False), (sp["g_xk"], XHEAD, 0, False), (kv, XHEAD, 0, True), (kv, XHEAD, 4, True)]
    xo, = _rw_fwd("xattn_fwd", _xattn_fn, xa_rows, xa_params, [(BF, XHEAD)], T, tm4, 4)
    h2, hn3 = _mm("proj_xo", [[(xo, W["w_xo"], "nn")]], [h1], _add_rms_epi, T, D, tm2, D, [F32, BF], params=[sp["g_ffn"]])

    W.update(fetch_rest("ffn", hn3))
    gate, up, act = _mm("ffn_in", [[(hn3, W["w_gateT"], "nt")], [(hn3, W["w_upT"], "nt")]], [], _swiglu_fwd_epi,
                        T, D_FF, tm4, 256, [BF, BF, BF])
    dy, dyb, loss_part = _mm("ffn_out", [[(act, W["w_down"], "nn")]], [h2, target], _add_loss_epi, T, D, tm, D, [F32, BF], n_acc=1)

    dgate, dup = _mm("ffn_out_bwd", [[(dyb, W["w_down"], "nt")]], [gate, up], _swiglu_bwd_epi, T, D_FF, tm4, 256, [BF, BF])
    gW = {}
    gW["w_gateT"], gW["w_upT"] = _mm("dw_gate_up", [[(dgate, hn3, "tn")], [(dup, hn3, "tn")]], [], _each, D_FF, D, 256, D, [BF, BF])
    gW["w_down"], = _mm("dw_down", [[(act, dyb, "tn")]], [], _ident, D_FF, D, 256, D, [BF])
    tok = push("ffn", gW)
    gs = {}
    dh2, dh2b, gs["g_ffn"] = _mm("ffn_in_bwd", [[(dgate, W["w_gateT"], "nn"), (dup, W["w_upT"], "nn")]], [h2, dy], _rms_bwd_epi,
                                 T, D, min(256, T), D, [F32, BF], params=[tie(sp["g_ffn"], tok)], n_acc=1)

    dxo, = _mm("proj_xo_bwd", [[(dh2b, W["w_xo"], "nt")]], [], _ident, T, D, tm2, D, [BF])
    gW["w_xo"], = _mm("dw_xo", [[(xo, dh2b, "tn")]], [], _ident, D, D, 256, D, [BF])
    dqx, gs["g_xq"], gs["g_xk"], dkv_k, dkv_v = _rw_bwd(
        "xattn_bwd", _xattn_fn, xa_rows, xa_params, [(dxo, XHEAD, 0, True)], T, tm4, 4, [BF], [True, True, True, True])
    dkv = jnp.concatenate([dkv_k[:, :D], dkv_v[:, D:]], axis=1)
    gW["w_xq"], = _mm("dw_xq", [[(hn2, dqx, "tn")]], [], _ident, D, D, 256, D, [BF])
    dmemn, = _mm("proj_xkv_bwd", [[(dkv, W["w_xkvT"], "nn")]], [], _ident, N_MEM, D, N_MEM, 512, [F32])
    gW["w_xkvT"], = _mm("dw_xkv", [[(dkv, memn, "tn")]], [], _ident, 2 * D, D, 512, D, [BF])
    tok = push("xattn", gW)
    gs["g_mem"], = _rw_bwd("rms_mem_bwd", _rms_fn, [(mem, D, 0, False)], [(sp["g_mem"], D, 0, False)], [(dmemn, D, 0, False)],
                           N_MEM, N_MEM, 1, [None], [True])
    dh1, dh1b, gs["g_xattn"] = _mm("proj_xq_bwd", [[(dqx, W["w_xq"], "nt")]], [h1, dh2], _rms_bwd_epi, T, D, tm, D, [F32, BF],
                                   params=[tie(sp["g_xattn"], tok)], n_acc=1)

    dmix, = _mm("proj_out_bwd", [[(dh1b, W["w_out"], "nt")]], [], _ident, T, D, tm2, D, [F32])
    gW["w_out"] = jnp.concatenate(_mm("dw_out", [[(ret, dh1b, "tn")], [(fox, dh1b, "tn")]], [], _each, 4 * LANES, D, 256, D,
                                      [BF, BF]), axis=0)
    tok = push("out", gW)
    doat, dl = _fox_cotangent(dmix, fox, T, tm4)
    dqn, dkn, dfv, dfc4, drc4 = _fox_backward(qat, ka, kat, va, doat, lse + tok[0:1, 0:1], dl, T, tq, tq)
    dfq, dfk, gq2, gk2 = _rw_bwd("fox_prep_bwd", _fox_prep_fn, [(P, LANES, 16, True), (P, LANES, 20, True)],
                                 [(g_fq2, LANES, 0, False), (g_fk2, LANES, 0, False)],
                                 [(dqn, LANES, 0, True), (dkn, LANES, 0, True)], T, tm4, 4, [BF, BF], [True, True])
    gs["g_fox_q"] = gq2[:, :HEAD] + gq2[:, HEAD:]
    gs["g_fox_k"] = gk2[:, :HEAD] + gk2[:, HEAD:]
    dff, dbp = _fgate_bwd_col(ffp, bpad, dfc4, drc4, T)
    gs["b_forget"] = dbp[:, :8]
    drq, drk, drv, drg, dg_ret = _ret_bwd(P, cos4, sin4, g_ret, lg, s0, dmix, T, tb)
    gs["g_ret_out"] = dg_ret
    dsecs = [drq, drk, drv, drg, dfq, dfk, dfv]
    g_secs = list(_mm("dw_in", [[(d, hn1, "tn")] for d in dsecs], [], _each, 512, D, LANES, D, [BF] * len(dsecs)))
    g_ff, = _mm("dw_in_ff", [[(dff, hn1, "tn")]], [], _ident, LANES, D, LANES, D, [BF])
    gW["w_inT"] = jnp.concatenate(g_secs + [g_ff[:8]], axis=0)
    tok = push("in", gW)
    grad_x, gs["g_mix"] = _mm("proj_in_bwd", [[(d, w_inT, "nn", k) for k, d in enumerate(dsecs)] + [(dff, w_ffT, "nn")]], [x, dh1],
                              _rms_bwd_first_epi, T, D, tm, D, [F32], params=[tie(sp["g_mix"], tok)], n_acc=1)
    return grad_x, push_small(gs, loss_part)


_CANON = {"w_in": "w_inT", "w_xkv": "w_xkvT", "w_gate": "w_gateT", "w_up": "w_upT"}
_SMALL = (("g_mix", 0, 0, 1024), ("g_xattn", 1, 0, 1024), ("g_mem", 2, 0, 1024), ("g_ffn", 3, 0, 1024),
          ("g_ret_out", 4, 0, 512), ("g_xq", 4, 512, 256), ("g_xk", 4, 768, 256),
          ("g_fox_q", 5, 0, 64), ("g_fox_k", 5, 64, 64), ("b_forget", 5, 128, 8))
_LOSS_AT = (5, 256)


def _pack_small(tree):
    buf = jnp.zeros((SMALL_ROWS, D), F32)
    for name, r, c, n in _SMALL:
        buf = lax.dynamic_update_slice(buf, tree[name].reshape(1, n).astype(F32), (r, c))
    return buf


def _unpack_small(buf, like):
    return {name: buf[r:r + 1, c:c + n].reshape(like[name].shape) for name, r, c, n in _SMALL}


def _canonical(tree, name):
    a = tree[name][0]
    return a.T if W_SHARD[name][1] else a


def _from_canonical(a, name):
    return (a.T if W_SHARD[name][1] else a)[None]


def kernel(x, mem, g_mix, w_in, b_forget, g_ret_out, g_fox_q, g_fox_k, w_out, g_xattn, w_xq, w_xkv, g_mem, g_xq, g_xk, w_xo, g_ffn, w_gate, w_up, w_down, loss_target, m_g_mix, m_w_in, m_b_forget, m_g_ret_out, m_g_fox_q, m_g_fox_k, m_w_out, m_g_xattn, m_w_xq, m_w_xkv, m_g_mem, m_g_xq, m_g_xk, m_w_xo, m_g_ffn, m_w_gate, m_w_up, m_w_down, v_g_mix, v_w_in, v_b_forget, v_g_ret_out, v_g_fox_q, v_g_fox_k, v_w_out, v_g_xattn, v_w_xq, v_w_xkv, v_g_mem, v_g_xq, v_g_xk, v_w_xo, v_g_ffn, v_w_gate, v_w_up, v_w_down):
    names = ("g_mix", "w_in", "b_forget", "g_ret_out", "g_fox_q", "g_fox_k", "w_out", "g_xattn", "w_xq", "w_xkv", "g_mem",
             "g_xq", "g_xk", "w_xo", "g_ffn", "w_gate", "w_up", "w_down")
    w = dict(zip(names, (g_mix, w_in, b_forget, g_ret_out, g_fox_q, g_fox_k, w_out, g_xattn, w_xq, w_xkv, g_mem, g_xq, g_xk,
                         w_xo, g_ffn, w_gate, w_up, w_down)))
    m = dict(zip(names, (m_g_mix, m_w_in, m_b_forget, m_g_ret_out, m_g_fox_q, m_g_fox_k, m_w_out, m_g_xattn, m_w_xq, m_w_xkv,
                         m_g_mem, m_g_xq, m_g_xk, m_w_xo, m_g_ffn, m_w_gate, m_w_up, m_w_down)))
    v = dict(zip(names, (v_g_mix, v_w_in, v_b_forget, v_g_ret_out, v_g_fox_q, v_g_fox_k, v_w_out, v_g_xattn, v_w_xq, v_w_xkv,
                         v_g_mem, v_g_xq, v_g_xk, v_w_xo, v_g_ffn, v_w_gate, v_w_up, v_w_down)))
    small_names = [s[0] for s in _SMALL]
    me = 4 * lax.axis_index("x") + 2 * lax.axis_index("y") + lax.axis_index("c")
    me1 = me.astype(jnp.int32).reshape(1)

    sp = {n: w[n].reshape(1, -1) for n in small_names}
    first, hn1 = _all_gather(_canonical(w, "w_in").astype(BF), x[0], sp["g_mix"], min(1024, x.shape[1]))
    first, rests = lax.optimization_barrier((first, {g: [_canonical(w, n).astype(BF) for n in ns] for g, ns in GATHER_REST.items()}))
    rest_started = {g: _exchange_start("gather_%s_start" % g, rests[g], [lax.empty((N_DEV,) + a.shape, BF) for a in rests[g]],
                                       scatter=False) for g in GATHER_REST}
    after = rest_started["attn"][4] + rest_started["ffn"][4]

    def fetch_rest(group, after):
        lands = _exchange_wait("gather_%s_wait" % group, rest_started[group], after, scatter=False)[1]
        return {_CANON.get(n, n): a.reshape(N_DEV * a.shape[1], D) for n, a in zip(GATHER_REST[group], lands)}

    pushed = {}

    def push(group, grads):
        srcs = [grads[_CANON.get(n, n)].reshape(N_DEV, W_SHARD[n][0], D) for n in SCATTER_GROUPS[group]]
        pushed[group] = _exchange_start("scatter_%s_start" % group, srcs, [lax.empty(a.shape, BF) for a in srcs], scatter=True)
        return pushed[group][4]

    def push_small(gs, loss_part):
        small = lax.dynamic_update_slice(_pack_small(gs), loss_part[:, :1], _LOSS_AT)
        pushed["small"] = _exchange_start("gather_small_start", [small], [lax.empty((N_DEV,) + small.shape, F32)], scatter=False)
        return pushed["small"][4]

    grad_x, done = _local_step(x[0], hn1, mem[0], loss_target[0], sp, first.reshape(N_DEV * W_SHARD["w_in"][0], D),
                               after, fetch_rest, push, push_small)

    results, after = {}, done
    for group in ("ffn", "xattn", "out", "small", "in"):
        if group == "small":
            recv_small = _exchange_wait("gather_small_wait", pushed["small"], after, scatter=False)[1][0]
            g_sm, d_sm, m_sm, v_sm = _adamw("adamw_small", me1, recv_small, None, _pack_small(w), _pack_small(m), _pack_small(v))
            after = g_sm
            continue
        sents, recvs = _exchange_wait("scatter_%s_wait" % group, pushed[group], after, scatter=True)
        for name, sent, recv in zip(SCATTER_GROUPS[group], sents, recvs):
            res = _adamw("adamw_" + name, me1, recv, sent, *(_canonical(t, name) for t in (w, m, v)))
            results[name] = [_from_canonical(r, name) for r in res]
        after = results[SCATTER_GROUPS[group][-1]][0]
    loss = g_sm[_LOSS_AT[0], _LOSS_AT[1]]

    outs = []
    for k, sm in enumerate((g_sm, d_sm, m_sm, v_sm)):
        tree = _unpack_small(sm, w)
        tree.update({name: res[k] for name, res in results.items()})
        outs += [tree[n] for n in names]
    return (loss, grad_x[None], *outs)
```

```python
import jax
import jax.numpy as jnp
from jax import lax
from jax.experimental import pallas as pl
from jax.experimental.pallas import tpu as pltpu

F32 = jnp.float32
BF = jnp.bfloat16

D = 1024
HEAD = 64
CHUNK = 64
N_MEM = 256
XHEAD = 256
D_FF = 2816
EPS = 1e-6
NEG = -1e30
LANES = 128
N_DEV = 8
V7X_VMEM_BYTES = 64 * 1024 * 1024
VMEM_LIMIT = V7X_VMEM_BYTES - 8 * 1024 * 1024

ADAM_LR, ADAM_B1, ADAM_B2, ADAM_EPS, ADAM_WD, ADAM_STEP = 0.001, 0.9, 0.999, 1e-08, 0.01, 10

W_SHARD = {"w_in": (449, True), "w_out": (128, False), "w_xq": (128, False), "w_xkv": (256, True),
           "w_xo": (128, False), "w_gate": (352, True), "w_up": (352, True), "w_down": (352, False)}
GATHER_REST = {"attn": ("w_out", "w_xq", "w_xkv", "w_xo"), "ffn": ("w_gate", "w_up", "w_down")}
SCATTER_GROUPS = {"ffn": ("w_gate", "w_up", "w_down"), "xattn": ("w_xq", "w_xo", "w_xkv"), "out": ("w_out",), "in": ("w_in",)}
SMALL_ROWS = 8

NT = (((1,), (1,)), ((), ()))
NN = (((1,), (0,)), ((), ()))
TN = (((0,), (0,)), ((), ()))
_DIMS = {"nn": NN, "nt": NT, "tn": TN}


def _params(sem):
    return pltpu.CompilerParams(dimension_semantics=sem, vmem_limit_bytes=VMEM_LIMIT)


def _mm(name, products, extras, epilogue, M, N, tm, tn, out_dtypes, params=(), n_acc=0):
    assert n_acc == 0 or tn == N
    flat = [t for p in products for t in p]
    counts = [len(p) for p in products]
    in_specs, args, where, slots = [], [], {}, []

    def operand(arr, spec, kind):
        key = (id(arr), kind)
        if key not in where:
            where[key] = len(args)
            args.append(arr)
            in_specs.append(spec)
        return where[key]

    for a, b, form, *at in flat:
        if form == "tn":
            ia = operand(a, pl.BlockSpec((a.shape[0], tm), lambda i, j: (0, i)), "a_tn")
        else:
            ia = operand(a, pl.BlockSpec((tm, a.shape[1]), lambda i, j: (i, 0)), "a")
        if form == "nt":
            ib = operand(b, pl.BlockSpec((tn, b.shape[1]), lambda i, j: (j, 0)), "b_nt")
        elif form == "nn":
            k = at[0] if at else 0
            ib = operand(b, pl.BlockSpec((a.shape[1], tn), lambda i, j, k=k: (k, j)), "b%d" % k)
        else:
            ib = operand(b, pl.BlockSpec((b.shape[0], tn), lambda i, j: (0, j)), "b")
        slots.append((ia, ib))
    n_mm = len(args)
    for e in extras:
        in_specs.append(pl.BlockSpec((tm, tn), lambda i, j: (i, j)))
        args.append(e)
    for p in params:
        in_specs.append(pl.BlockSpec((1, tn), lambda i, j: (0, j)))
        args.append(p)
    n_in = len(args)
    n_out = len(out_dtypes)

    def body(*refs):
        ins, outs = refs[:n_in], refs[n_in:]
        prods, p = [], 0
        for c in counts:
            acc = None
            for _ in range(c):
                a = ins[slots[p][0]][...].astype(BF)
                b = ins[slots[p][1]][...].astype(BF)
                d = lax.dot_general(a, b, _DIMS[flat[p][2]], preferred_element_type=F32)
                acc = d if acc is None else acc + d
                p += 1
            prods.append(acc)
        ex = [r[...].astype(F32) for r in ins[n_mm:]]
        res = epilogue(*prods, *ex)
        for o, r in zip(outs[:n_out], res[:n_out]):
            o[...] = r.astype(o.dtype)
        for o, r in zip(outs[n_out:], res[n_out:]):
            @pl.when(pl.program_id(0) == 0)
            def _(o=o):
                o[...] = jnp.zeros(o.shape, F32)
            o[...] += r

    return pl.pallas_call(
        body, name=name, grid=(M // tm, N // tn), in_specs=in_specs,
        out_specs=[pl.BlockSpec((tm, tn), lambda i, j: (i, j)) for _ in out_dtypes]
        + [pl.BlockSpec((1, tn), lambda i, j: (0, j)) for _ in range(n_acc)],
        out_shape=[jax.ShapeDtypeStruct((M, N), dt) for dt in out_dtypes] + [jax.ShapeDtypeStruct((1, N), F32)] * n_acc,
        compiler_params=_params(("arbitrary", "arbitrary")),
    )(*args)


def _ident(x):
    return (x,)


def _each(*xs):
    return xs


def _spec(rows, w, off, per_j):
    if per_j:
        return pl.BlockSpec((rows, w), lambda j, i: (i, off + j))
    return pl.BlockSpec((rows, w), lambda j, i: (i, off))


def _pspec(rows, w, off, per_j):
    if per_j:
        return pl.BlockSpec((rows, w), lambda j, i: (0, off + j))
    return pl.BlockSpec((rows, w), lambda j, i: (0, off))


def _rw_fwd(name, fn, rows, params, outs, T, tm, nj, n_acc=0):
    in_specs = [_spec(tm, w, off, pj) for _, w, off, pj in rows] + [_pspec(a.shape[0], w, off, pj) for a, w, off, pj in params]
    args = [r[0] for r in rows] + [p[0] for p in params]
    n_in, n_out = len(args), len(outs)
    out_specs = [pl.BlockSpec((tm, w), lambda j, i: (i, j)) for _, w in outs]
    out_shape = [jax.ShapeDtypeStruct((T, nj * w), dt) for dt, w in outs]
    out_specs += [pl.BlockSpec((1, LANES), lambda j, i: (0, 0)) for _ in range(n_acc)]
    out_shape += [jax.ShapeDtypeStruct((1, LANES), F32) for _ in range(n_acc)]

    def body(*refs):
        vals = [r[...].astype(F32) for r in refs[:n_in]]
        res = fn(*vals)
        orefs = refs[n_in:]
        for k in range(n_out):
            orefs[k][...] = res[k].astype(orefs[k].dtype)
        first = (pl.program_id(0) == 0) & (pl.program_id(1) == 0)
        for k in range(n_acc):
            @pl.when(first)
            def _(k=k):
                orefs[n_out + k][...] = jnp.zeros((1, LANES), F32)
            orefs[n_out + k][...] += res[n_out + k]

    return pl.pallas_call(
        body, name=name, grid=(nj, T // tm), in_specs=in_specs, out_specs=out_specs, out_shape=out_shape,
        compiler_params=_params(("arbitrary", "arbitrary")),
    )(*args)


def _rw_bwd(name, fn, rows, params, cots, T, tm, nj, row_grads, param_grads, resid=None):
    in_specs = ([_spec(tm, w, off, pj) for _, w, off, pj in rows] + [_pspec(a.shape[0], w, off, pj) for a, w, off, pj in params]
                + [_spec(tm, w, off, pj) for _, w, off, pj in cots])
    args = [r[0] for r in rows] + [p[0] for p in params] + [c[0] for c in cots]
    if resid is not None:
        in_specs.append(_spec(tm, rows[0][1], rows[0][2], rows[0][3]))
        args.append(resid)
    nr, npar, nc = len(rows), len(params), len(cots)
    out_specs, out_shape, kinds = [], [], []
    for k, dts in enumerate(row_grads):
        for dt in (dts if isinstance(dts, (list, tuple)) else [dts]):
            if dt is not None:
                w = rows[k][1]
                out_specs.append(pl.BlockSpec((tm, w), lambda j, i: (i, j)))
                out_shape.append(jax.ShapeDtypeStruct((T, nj * w), dt))
                kinds.append(("row", k))
    for k, need in enumerate(param_grads):
        if need:
            a, w, off, pj = params[k]
            out_specs.append(_pspec(a.shape[0], w, off, pj))
            out_shape.append(jax.ShapeDtypeStruct(a.shape, F32))
            kinds.append(("par", k))

    def body(*refs):
        vals = [r[...].astype(F32) for r in refs[:nr + npar]]
        ct = tuple(r[...].astype(F32) for r in refs[nr + npar:nr + npar + nc])
        _, vjp = jax.vjp(lambda *a: tuple(fn(*a)), *vals)
        grads = list(vjp(ct))
        n_in = nr + npar + nc + (resid is not None)
        if resid is not None:
            grads[0] = grads[0] + refs[n_in - 1][...].astype(F32)
        orefs = refs[n_in:]
        j, i = pl.program_id(0), pl.program_id(1)
        for o, (kind, k) in zip(orefs, kinds):
            if kind == "row":
                o[...] = grads[k].astype(o.dtype)
            else:
                first = (i == 0) if params[k][3] else ((i == 0) & (j == 0))

                @pl.when(first)
                def _(o=o):
                    o[...] = jnp.zeros(o.shape, F32)
                o[...] += grads[nr + k]

    return pl.pallas_call(
        body, name=name, grid=(nj, T // tm), in_specs=in_specs, out_specs=out_specs, out_shape=out_shape,
        compiler_params=_params(("arbitrary", "arbitrary")),
    )(*args)


def _rms(x, g):
    return x * lax.rsqrt(jnp.mean(x * x, axis=-1, keepdims=True) + EPS) * g


def _rms_fn(x, g):
    return (_rms(x, g),)


def _lo_mask():
    return lax.broadcasted_iota(jnp.int32, (1, LANES), 1) < HEAD


def _gmean(x, lo):
    s0 = jnp.sum(jnp.where(lo, x, 0.0), axis=-1, keepdims=True)
    s1 = jnp.sum(jnp.where(lo, 0.0, x), axis=-1, keepdims=True)
    return jnp.where(lo, s0, s1) * (1.0 / HEAD)


def _fox_prep_fn(fq, fk, gq, gk):
    lo = _lo_mask()
    qn = fq * lax.rsqrt(_gmean(fq * fq, lo) + EPS) * gq * (HEAD ** -0.5)
    kn = fk * lax.rsqrt(_gmean(fk * fk, lo) + EPS) * gk
    return qn, kn


@jax.custom_vjp
def _swap_halves(x):
    bit = (lax.broadcasted_iota(jnp.int32, (1, LANES), 1) & (HEAD // 2)) == 0
    return jnp.where(bit, pltpu.roll(x, LANES - HEAD // 2, 1), pltpu.roll(x, HEAD // 2, 1))


_swap_halves.defvjp(lambda x: (_swap_halves(x), None), lambda _, g: (_swap_halves(g),))


def _ret_fn(rq, rk, rv, rg, cos, sin, s_in, g, lg):
    tb = rq.shape[0]
    nc = tb // CHUNK
    lo = _lo_mask()
    row = lax.broadcasted_iota(jnp.int32, (LANES, 1), 0) < HEAD
    same_head = row == lo
    q = (rq * cos + _swap_halves(rq) * sin) * (HEAD ** -0.5)
    k = rk * cos + _swap_halves(rk) * sin
    q3, k3, v3 = q.reshape(nc, CHUNK, LANES), k.reshape(nc, CHUNK, LANES), rv.reshape(nc, CHUNK, LANES)
    pos = lax.broadcasted_iota(jnp.int32, (CHUNK, 1), 0).astype(F32)
    q_decay = jnp.exp(lg * (pos + 1.0))
    k_decay = jnp.exp(lg * (CHUNK - 1.0 - pos))
    chunk_decay = jnp.exp(lg * float(CHUNK))
    dist = jnp.abs(lax.broadcasted_iota(jnp.int32, (CHUNK, CHUNK), 0) - lax.broadcasted_iota(jnp.int32, (CHUNK, CHUNK), 1)).astype(F32)
    v3b = v3.astype(BF)
    intra = []
    for hh in range(2):
        hm = lo if hh == 0 else ~lo
        lg_h = lg[:, hh * HEAD:hh * HEAD + 1]
        qm = jnp.where(hm, q3, 0.0).astype(BF)
        sc = jnp.einsum("nid,njd->nij", qm, k3.astype(BF), preferred_element_type=F32) * jnp.exp(lg_h * dist)[None]
        intra.append(jnp.einsum("nij,nje->nie", sc.astype(BF), v3b, preferred_element_type=F32))
    o = jnp.where(lo, intra[0], intra[1])
    kv = jnp.einsum("njd,nje->nde", (k3 * k_decay[None]).astype(BF), v3b, preferred_element_type=F32)
    kv = jnp.where(same_head[None], kv, 0.0)
    state, states = s_in, []
    for n in range(nc):
        states.append(state)
        state = state * chunk_decay + kv[n]
    s_prev = jnp.stack(states, axis=0)
    o = o + jnp.einsum("nid,nde->nie", (q3 * q_decay[None]).astype(BF), s_prev.astype(BF), preferred_element_type=F32)
    o = o.reshape(tb, LANES)
    mu = _gmean(o, lo)
    oc = o - mu
    y = oc * lax.rsqrt(_gmean(oc * oc, lo) + EPS) * g
    return jax.nn.silu(rg) * y, state


def _xattn_fn(qx, gq, gk, kk, vv):
    q = _rms(qx, gq)
    k = _rms(kk, gk)
    logits = lax.dot_general(q.astype(BF), k.astype(BF), NT, preferred_element_type=F32) * (XHEAD ** -0.5)
    p = jax.nn.softmax(logits, axis=-1)
    return (jnp.dot(p.astype(BF), vv.astype(BF), preferred_element_type=F32),)


def _swiglu_fwd_epi(g, u):
    return g, u, jax.nn.silu(g) * u


def _swiglu_bwd_epi(dact, g, u):
    _, vjp = jax.vjp(lambda a, b: jax.nn.silu(a) * b, g, u)
    return vjp(dact)


def _add_rms_epi(acc, resid, g):
    h = acc + resid
    return h, _rms(h, g)


def _add_loss_epi(acc, resid, target):
    err = (acc + resid) - target
    dy = err * (1.0 / D)
    part = jnp.sum(jnp.sum(err * err, axis=0, keepdims=True), axis=1, keepdims=True) * (0.5 / D)
    return dy, dy, jnp.broadcast_to(part, (1, err.shape[1]))


def _rms_bwd_epi(dhn, h, skip, g):
    _, vjp = jax.vjp(_rms, h, g)
    dh, dg = vjp(dhn)
    dh = dh + skip
    return dh, dh, dg


def _rms_bwd_first_epi(dhn, h, skip, g):
    return _rms_bwd_epi(dhn, h, skip, g)[1:]


def _ret_fwd(P, cos, sin, g_ret, lg, T, tb):
    nb = T // tb

    def body(rq, rk, rv, rg, c, s, g, l, o_ref, s0_ref, state):
        @pl.when(pl.program_id(1) == 0)
        def _():
            state[...] = jnp.zeros(state.shape, F32)
        s0_ref[0, 0] = state[...]
        out, s_new = _ret_fn(rq[...], rk[...], rv[...], rg[...], c[...], s[...], state[...], g[...], l[...])
        o_ref[...] = out.astype(o_ref.dtype)
        state[...] = s_new

    sec = lambda off: pl.BlockSpec((tb, LANES), lambda j, i: (i, off + j))
    tab = pl.BlockSpec((tb, LANES), lambda j, i: (i, 0))
    par = pl.BlockSpec((1, LANES), lambda j, i: (0, j))
    return pl.pallas_call(
        body, name="ret_fwd", grid=(4, nb),
        in_specs=[sec(0), sec(4), sec(8), sec(12), tab, tab, par, par],
        out_specs=[pl.BlockSpec((tb, LANES), lambda j, i: (i, j)), pl.BlockSpec((1, 1, LANES, LANES), lambda j, i: (j, i, 0, 0))],
        out_shape=[jax.ShapeDtypeStruct((T, 4 * LANES), BF), jax.ShapeDtypeStruct((4, nb, LANES, LANES), F32)],
        scratch_shapes=[pltpu.VMEM((LANES, LANES), F32)],
        compiler_params=_params(("arbitrary", "arbitrary")),
    )(P, P, P, P, cos, sin, g_ret, lg)


def _ret_bwd(P, cos, sin, g_ret, lg, s0, dmix, T, tb):
    nb = T // tb

    def body(rq, rk, rv, rg, c, s, g, l, s0_ref, do, drq, drk, drv, drg, dg, dstate):
        i = pl.program_id(1)

        @pl.when(i == 0)
        def _():
            dstate[...] = jnp.zeros(dstate.shape, F32)
            dg[...] = jnp.zeros(dg.shape, F32)

        cc, ss, ll = c[...], s[...], l[...]
        _, vjp = jax.vjp(lambda a, b, v, gate, st, gg: _ret_fn(a, b, v, gate, cc, ss, st, gg, ll),
                         rq[...], rk[...], rv[...], rg[...], s0_ref[0, 0], g[...])
        ga, gb, gv, ggate, gst, ggain = vjp((do[...], dstate[...]))
        drq[...] = ga.astype(drq.dtype)
        drk[...] = gb.astype(drk.dtype)
        drv[...] = gv.astype(drv.dtype)
        drg[...] = ggate.astype(drg.dtype)
        dstate[...] = gst
        dg[...] += ggain

    rev = lambda i: nb - 1 - i
    sec = lambda off: pl.BlockSpec((tb, LANES), lambda j, i: (rev(i), off + j))
    tab = pl.BlockSpec((tb, LANES), lambda j, i: (rev(i), 0))
    par = pl.BlockSpec((1, LANES), lambda j, i: (0, j))
    outb = pl.BlockSpec((tb, LANES), lambda j, i: (rev(i), j))
    return pl.pallas_call(
        body, name="ret_bwd", grid=(4, nb),
        in_specs=[sec(0), sec(4), sec(8), sec(12), tab, tab, par, par,
                  pl.BlockSpec((1, 1, LANES, LANES), lambda j, i: (j, rev(i), 0, 0)), outb],
        out_specs=[outb, outb, outb, outb, par],
        out_shape=[jax.ShapeDtypeStruct((T, 4 * LANES), BF)] * 4 + [jax.ShapeDtypeStruct((1, 4 * LANES), F32)],
        scratch_shapes=[pltpu.VMEM((LANES, LANES), F32)],
        compiler_params=_params(("arbitrary", "arbitrary")),
    )(P, P, P, P, cos, sin, g_ret, lg, s0, dmix)


_FB = 128


def _tri(lower):
    r = lax.broadcasted_iota(jnp.int32, (_FB, _FB), 0)
    c = lax.broadcasted_iota(jnp.int32, (_FB, _FB), 1)
    return ((r >= c) if lower else (r <= c)).astype(F32)


def _fgate_fwd(ffp, bpad, T):
    def body(ff_ref, b_ref, fc_ref, fr_ref):
        lane = lax.broadcasted_iota(jnp.int32, (1, LANES), 1)
        tri = _tri(True)
        carry = jnp.zeros((1, LANES), F32)
        for blk in range(T // _FB):
            z = ff_ref[blk * _FB:(blk + 1) * _FB, :] + b_ref[...]
            lf = jnp.where(lane < 8, jax.nn.log_sigmoid(z), 0.0)
            f = jnp.dot(tri, lf, precision=lax.Precision.HIGHEST, preferred_element_type=F32) + carry
            carry = f[_FB - 1:_FB, :]
            fc_ref[blk * _FB:(blk + 1) * _FB, :] = f
            fr_ref[:, blk * _FB:(blk + 1) * _FB] = f.T[:8, :]

    return pl.pallas_call(
        body, name="fgate_fwd",
        out_shape=[jax.ShapeDtypeStruct((T, LANES), F32), jax.ShapeDtypeStruct((8, T), F32)],
        compiler_params=pltpu.CompilerParams(vmem_limit_bytes=VMEM_LIMIT),
    )(ffp, bpad)


_BIAS_LANE = HEAD


def _head_bias_col(fc, head):
    lane = lax.broadcasted_iota(jnp.int32, (1, LANES), 1)
    return jnp.sum(jnp.where(lane == head, fc, 0.0), axis=-1, keepdims=True)


def _split3(f):
    hi = f.astype(BF).astype(F32)
    mid = (f - hi).astype(BF).astype(F32)
    lo = ((f - hi) - mid).astype(BF).astype(F32)
    return hi, mid, lo


def _fox_operands(P, fc, g_fq2, g_fk2, T, tm):
    def body(fq_ref, fk_ref, fv_ref, fc_ref, gq_ref, gk_ref, qat_ref, ka_ref, kat_ref, va_ref, vat_ref):
        j = pl.program_id(0)
        lane = lax.broadcasted_iota(jnp.int32, (1, LANES), 1)
        qn, kn = _fox_prep_fn(fq_ref[...], fk_ref[...], gq_ref[...], gk_ref[...])
        v = fv_ref[...]
        fcb = fc_ref[...]
        b = _BIAS_LANE
        for hh in range(2):
            hi, mid, lo = _split3(_head_bias_col(fcb, 2 * j + hh))
            take = (lambda a: a) if hh == 0 else (lambda a: pltpu.roll(a, HEAD, 1))
            qa = jnp.where(lane < HEAD, take(qn), jnp.where(lane == b, hi, jnp.where(lane == b + 1, mid, jnp.where(
                lane == b + 2, lo, jnp.where(lane < b + 6, 1.0, 0.0)))))
            ka = jnp.where(lane < HEAD, take(kn), jnp.where(lane < b + 3, 1.0, jnp.where(lane == b + 3, -hi, jnp.where(
                lane == b + 4, -mid, jnp.where(lane == b + 5, -lo, 0.0)))))
            va = jnp.where(lane < HEAD, take(v), 0.0)
            qat_ref[hh] = qa.T.astype(BF)
            for val, ref, tref in ((ka, ka_ref, kat_ref), (va, va_ref, vat_ref)):
                ref[hh] = val.astype(BF)
                tref[hh] = val.T.astype(BF)

    sec = lambda off: pl.BlockSpec((tm, LANES), lambda j, i: (i, off + j))
    par = pl.BlockSpec((1, LANES), lambda j, i: (0, 0))
    nat = pl.BlockSpec((2, tm, LANES), lambda j, i: (j, i, 0))
    trn = pl.BlockSpec((2, LANES, tm), lambda j, i: (j, 0, i))
    return pl.pallas_call(
        body, name="fox_operands", grid=(4, T // tm),
        in_specs=[sec(16), sec(20), sec(24), pl.BlockSpec((tm, LANES), lambda j, i: (i, 0)), par, par],
        out_specs=[trn, nat, trn, nat, trn],
        out_shape=[jax.ShapeDtypeStruct((8, LANES, T), BF)]
        + [jax.ShapeDtypeStruct((8, T, LANES), BF), jax.ShapeDtypeStruct((8, LANES, T), BF)] * 2,
        compiler_params=_params(("parallel", "arbitrary")),
    )(P, P, P, fc, g_fq2, g_fk2)


def _fox_forward(qat, ka, vat, T, tq, tk):
    nq, per = T // tq, tq // tk
    assert per == 2
    RC = 64

    def body(qat_ref, ka_ref, vat_ref, o_ref, lse_ref, s_scr, p_scr, a_scr, m_scr, l_scr, acc_scr):
        i = pl.program_id(1)
        sub = lax.broadcasted_iota(jnp.int32, (8, 1), 0)
        row = lax.broadcasted_iota(jnp.int32, (RC, tq), 0)
        col = lax.broadcasted_iota(jnp.int32, (RC, tq), 1)
        m_scr[...] = jnp.full(m_scr.shape, NEG, F32)
        l_scr[...] = jnp.zeros(l_scr.shape, F32)
        acc_scr[...] = jnp.zeros(acc_scr.shape, F32)

        def scores(slot, kb):
            k0 = pl.multiple_of(kb * tk, tk)
            for hh in range(2):
                s_scr[slot, hh] = jnp.dot(ka_ref[hh, pl.ds(k0, tk), :], qat_ref[hh], preferred_element_type=F32)

        def softmax(slot, kb, diagonal):
            shift = kb * tk - i * tq
            for hh in range(2):
                def masked(r):
                    tile = s_scr[slot, hh, r * RC:(r + 1) * RC, :]
                    return jnp.where(row + (r * RC + shift) <= col, tile, NEG) if diagonal else tile

                mx = jnp.max(masked(0), axis=0, keepdims=True)
                for r in range(1, tk // RC):
                    mx = jnp.maximum(mx, jnp.max(masked(r), axis=0, keepdims=True))
                m_old = m_scr[hh, 0:1, :]
                m2 = jnp.maximum(m_old, mx)
                a = jnp.exp(m_old - m2)
                lsum = jnp.zeros((1, tq), F32)
                for r in range(tk // RC):
                    p = jnp.exp(masked(r) - m2)
                    p_scr[slot, hh, r * RC:(r + 1) * RC, :] = p.astype(BF)
                    lsum = lsum + jnp.sum(p, axis=0, keepdims=True)
                m_scr[hh] = jnp.broadcast_to(m2, (8, tq))
                l_scr[hh] = jnp.broadcast_to(a * l_scr[hh, 0:1, :] + lsum, (8, tq))
                a_scr[slot, hh] = jnp.broadcast_to(a, (8, tq))

        def values(slot, kb):
            k0 = pl.multiple_of(kb * tk, tk)
            for hh in range(2):
                pv = jnp.dot(vat_ref[hh, 0:HEAD, pl.ds(k0, tk)], p_scr[slot, hh], preferred_element_type=F32)
                acc_scr[hh] = a_scr[slot, hh, 0:1, :] * acc_scr[hh] + pv

        def pair(kb, diag_first, diag_second, more):
            if more:
                scores(0, kb + 2)
            softmax(1, kb + 1, diag_first)
            values(0, kb)
            if more:
                scores(1, kb + 3)
                softmax(0, kb + 2, diag_second)
            values(1, kb + 1)

        scores(0, 0)
        scores(1, 1)
        softmax(0, 0, True)

        @pl.loop(0, jnp.maximum(i - 1, 0))
        def _(t):
            pair(2 * t, False, False, True)

        @pl.when(i >= 1)
        def _():
            pair(2 * (i - 1), False, True, True)

        pair(2 * i, True, False, False)

        o_ref[...] = jnp.concatenate([acc_scr[hh] / l_scr[hh, 0:1, :] for hh in range(2)], axis=0).T
        lses = [m_scr[hh, 0:1, :] + jnp.log(l_scr[hh, 0:1, :]) for hh in range(2)]
        lse_ref[0] = jnp.where(sub == 0, lses[0], jnp.where(sub == 1, lses[1], 0.0))

    return pl.pallas_call(
        body, name="fox_forward", grid=(4, nq),
        in_specs=[pl.BlockSpec((2, LANES, tq), lambda j, i: (j, 0, i)), pl.BlockSpec((2, T, LANES), lambda j, i: (j, 0, 0)),
                  pl.BlockSpec((2, LANES, T), lambda j, i: (j, 0, 0))],
        out_specs=[pl.BlockSpec((tq, LANES), lambda j, i: (i, j)), pl.BlockSpec((1, 8, tq), lambda j, i: (j, 0, i))],
        out_shape=[jax.ShapeDtypeStruct((T, 4 * LANES), F32), jax.ShapeDtypeStruct((4, 8, T), F32)],
        scratch_shapes=[pltpu.VMEM((2, 2, tk, tq), F32), pltpu.VMEM((2, 2, tk, tq), BF), pltpu.VMEM((2, 2, 8, tq), F32),
                        pltpu.VMEM((2, 8, tq), F32), pltpu.VMEM((2, 8, tq), F32), pltpu.VMEM((2, HEAD, tq), F32)],
        compiler_params=_params(("parallel", "arbitrary")),
    )(qat, ka, vat)


def _fox_cotangent(dmix, fox, T, tm):
    def body(do_ref, o_ref, doat_ref, dl_ref):
        lane = lax.broadcasted_iota(jnp.int32, (1, LANES), 1)
        sub = lax.broadcasted_iota(jnp.int32, (8, 1), 0)
        dob = do_ref[...].astype(BF).astype(F32)
        prod_t = (dob * o_ref[...]).T
        d0 = jnp.sum(prod_t[:HEAD], axis=0, keepdims=True)
        d1 = jnp.sum(prod_t[HEAD:], axis=0, keepdims=True)
        dl_ref[0] = jnp.where(sub == 0, d0, jnp.where(sub == 1, d1, 0.0))
        for hh in range(2):
            val = jnp.where(lane < HEAD, dob if hh == 0 else pltpu.roll(dob, HEAD, 1), 0.0)
            doat_ref[hh] = val.T.astype(BF)

    return pl.pallas_call(
        body, name="fox_cotangent", grid=(4, T // tm),
        in_specs=[pl.BlockSpec((tm, LANES), lambda j, i: (i, 4 + j)), pl.BlockSpec((tm, LANES), lambda j, i: (i, j))],
        out_specs=[pl.BlockSpec((2, LANES, tm), lambda j, i: (j, 0, i)), pl.BlockSpec((1, 8, tm), lambda j, i: (j, 0, i))],
        out_shape=[jax.ShapeDtypeStruct((8, LANES, T), BF), jax.ShapeDtypeStruct((4, 8, T), F32)],
        compiler_params=_params(("parallel", "arbitrary")),
    )(dmix, fox)


def _fox_backward(qat, ka, kat, va, doat, lse, dl, T, tq, tk):
    nq, nk = T // tq, T // tk

    def body(qat_ref, ka_ref, kat_ref, va_ref, doat_ref, lse_ref, dl_ref,
             dq_ref, dk_ref, dv_ref, df_ref, dr_ref, dqt, dkt, dvt, df_acc, sdp, pds):
        j, kb = pl.program_id(0), pl.program_id(1)
        lane = lax.broadcasted_iota(jnp.int32, (1, LANES), 1)
        first = (kb * tk) // tq

        @pl.when(kb == 0)
        def _():
            dqt[...] = jnp.zeros(dqt.shape, F32)

        dkt[...] = jnp.zeros(dkt.shape, F32)
        dvt[...] = jnp.zeros(dvt.shape, F32)
        df_acc[...] = jnp.zeros(df_acc.shape, F32)

        RC = 64
        last = nq - 1

        def products(slot, qi):
            q0 = pl.multiple_of(qi * tq, tq)
            for hh in range(2):
                sdp[slot, hh, 0] = jnp.dot(ka_ref[hh], qat_ref[hh, :, pl.ds(q0, tq)], preferred_element_type=F32)
                sdp[slot, hh, 1] = jnp.dot(va_ref[hh], doat_ref[hh, :, pl.ds(q0, tq)], preferred_element_type=F32)

        def softmax_bwd(slot, qi, diagonal, valid):
            q0 = pl.multiple_of(qi * tq, tq)
            shift = kb * tk - first * tq
            col = lax.broadcasted_iota(jnp.int32, (RC, tq), 1)
            row = lax.broadcasted_iota(jnp.int32, (RC, tq), 0)
            for hh in range(2):
                lse_row = lse_ref[0, hh:hh + 1, pl.ds(q0, tq)]
                dl_row = dl_ref[0, hh:hh + 1, pl.ds(q0, tq)]
                rsum = jnp.zeros((1, tq), F32)
                for r in range(tk // RC):
                    rows = slice(r * RC, (r + 1) * RC)
                    p = jnp.exp(sdp[slot, hh, 0, rows, :] - lse_row)
                    p = jnp.where((row + (r * RC + shift) <= col) if diagonal else valid, p, 0.0)
                    ds = p * (sdp[slot, hh, 1, rows, :] - dl_row)
                    pds[slot, hh, 0, rows, :] = p.astype(BF)
                    pds[slot, hh, 1, rows, :] = ds.astype(BF)
                    rsum = rsum + jnp.sum(ds, axis=0, keepdims=True)
                    part = ds[:, 0:LANES]
                    for c in range(1, tq // LANES):
                        part = part + ds[:, c * LANES:(c + 1) * LANES]
                    df_acc[hh, rows, :] += part
                dqt[hh, HEAD:HEAD + 8, pl.ds(q0, tq)] += jnp.broadcast_to(rsum, (8, tq))

        def accumulate(slot, qi):
            q0 = pl.multiple_of(qi * tq, tq)
            for hh in range(2):
                dvt[hh] += lax.dot_general(doat_ref[hh, 0:HEAD, pl.ds(q0, tq)], pds[slot, hh, 0], NT, preferred_element_type=F32)
                dkt[hh] += lax.dot_general(qat_ref[hh, 0:HEAD, pl.ds(q0, tq)], pds[slot, hh, 1], NT, preferred_element_type=F32)
                dqt[hh, 0:HEAD, pl.ds(q0, tq)] += jnp.dot(kat_ref[hh, 0:HEAD, :], pds[slot, hh, 1], preferred_element_type=F32)

        products(0, first)
        products(1, jnp.minimum(first + 1, last))
        softmax_bwd(0, first, True, None)

        @pl.loop(0, (nq - first + 1) // 2)
        def _(t):
            qi = first + 2 * t
            products(0, jnp.minimum(qi + 2, last))
            softmax_bwd(1, jnp.minimum(qi + 1, last), False, qi + 1 <= last)
            accumulate(0, qi)
            products(1, jnp.minimum(qi + 3, last))
            softmax_bwd(0, jnp.minimum(qi + 2, last), False, qi + 2 <= last)
            accumulate(1, jnp.minimum(qi + 1, last))

        dk_ref[...] = jnp.concatenate([dkt[0], dkt[1]], axis=0).T
        dv_ref[...] = jnp.concatenate([dvt[0], dvt[1]], axis=0).T.astype(dv_ref.dtype)
        f0 = -jnp.sum(df_acc[0], axis=1, keepdims=True)
        f1 = -jnp.sum(df_acc[1], axis=1, keepdims=True)
        df_ref[0] = jnp.where(lane == 2 * j, f0, jnp.where(lane == 2 * j + 1, f1, 0.0))

        @pl.when(kb == nk - 1)
        def _():
            for t in range(nq):
                cols = slice(t * tq, (t + 1) * tq)
                dq_ref[cols, :] = jnp.concatenate([dqt[0, 0:HEAD, cols], dqt[1, 0:HEAD, cols]], axis=0).T
                rsum = jnp.concatenate([dqt[0, HEAD:HEAD + 8, cols], dqt[1, HEAD:HEAD + 8, cols],
                                        jnp.zeros((LANES - 16, tq), F32)], axis=0).T
                dr_ref[0, cols, :] = jnp.where(lane == 2 * j, rsum[:, 0:1], jnp.where(lane == 2 * j + 1, rsum[:, 8:9], 0.0))

    trn_full = pl.BlockSpec((2, LANES, T), lambda j, kb: (j, 0, 0))
    nat_blk = pl.BlockSpec((2, tk, LANES), lambda j, kb: (j, kb, 0))
    trn_blk = pl.BlockSpec((2, LANES, tk), lambda j, kb: (j, 0, kb))
    rows = pl.BlockSpec((1, 8, T), lambda j, kb: (j, 0, 0))
    blk = pl.BlockSpec((tk, LANES), lambda j, kb: (kb, j))
    return pl.pallas_call(
        body, name="fox_backward", grid=(4, nk),
        in_specs=[trn_full, nat_blk, trn_blk, nat_blk, trn_full, rows, rows],
        out_specs=[pl.BlockSpec((T, LANES), lambda j, kb: (0, j)), blk, blk, pl.BlockSpec((1, tk, LANES), lambda j, kb: (j, kb, 0)),
                   pl.BlockSpec((1, T, LANES), lambda j, kb: (j, 0, 0))],
        out_shape=[jax.ShapeDtypeStruct((T, 4 * LANES), F32), jax.ShapeDtypeStruct((T, 4 * LANES), F32),
                   jax.ShapeDtypeStruct((T, 4 * LANES), BF), jax.ShapeDtypeStruct((4, T, LANES), F32),
                   jax.ShapeDtypeStruct((4, T, LANES), F32)],
        scratch_shapes=[pltpu.VMEM((2, HEAD + 8, T), F32), pltpu.VMEM((2, HEAD, tk), F32), pltpu.VMEM((2, HEAD, tk), F32),
                        pltpu.VMEM((2, tk, LANES), F32), pltpu.VMEM((2, 2, 2, tk, tq), F32), pltpu.VMEM((2, 2, 2, tk, tq), BF)],
        compiler_params=_params(("arbitrary", "arbitrary")),
    )(qat, ka, kat, va, doat, lse, dl)


def _fgate_bwd_col(ffp, bpad, dfc4, drc4, T):
    def body(ff_ref, b_ref, dfc_ref, drc_ref, dff_ref, db_ref):
        lane = lax.broadcasted_iota(jnp.int32, (1, LANES), 1)
        tri = _tri(False)
        carry = jnp.zeros((1, LANES), F32)
        db = jnp.zeros((1, LANES), F32)
        for blk in reversed(range(T // _FB)):
            rows = slice(blk * _FB, (blk + 1) * _FB)
            dcol = dfc_ref[0, rows, :] + drc_ref[0, rows, :]
            for pair in range(1, 4):
                dcol = dcol + (dfc_ref[pair, rows, :] + drc_ref[pair, rows, :])
            dlf = jnp.dot(tri, dcol, precision=lax.Precision.HIGHEST, preferred_element_type=F32) + carry
            carry = dlf[0:1, :]
            z = ff_ref[blk * _FB:(blk + 1) * _FB, :] + b_ref[...]
            dz = jnp.where(lane < 8, dlf * jax.nn.sigmoid(-z), 0.0)
            dff_ref[blk * _FB:(blk + 1) * _FB, :] = dz.astype(dff_ref.dtype)
            db = db + jnp.sum(dz, axis=0, keepdims=True)
        db_ref[...] = db

    return pl.pallas_call(
        body, name="fgate_bwd",
        out_shape=[jax.ShapeDtypeStruct((T, LANES), BF), jax.ShapeDtypeStruct((1, LANES), F32)],
        compiler_params=pltpu.CompilerParams(vmem_limit_bytes=VMEM_LIMIT),
    )(ffp, bpad, dfc4, drc4)


MESH = pl.DeviceIdType.MESH
N_PEERS = N_DEV - 1


def _place():
    return lax.axis_index("x"), lax.axis_index("y"), lax.axis_index("c")


def _all_gather(shard, rows, g, tm):
    R, W = shard.shape
    T = rows.shape[0]
    steps = T // tm

    def body(w_ref, rows_ref, g_ref, out_ref, norm_ref, send_sems, recv_sems, local_sem):
        x, y, c = _place()
        me, sibling = (x, y, c), (x, y, 1 - c)
        chips = [(1 - x, y), (x, 1 - y), (1 - x, 1 - y)]

        def slot(px, py, pc):
            return out_ref.at[4 * px + 2 * py + pc]

        def copy(k, block, to, src=None):
            return pltpu.make_async_remote_copy(
                src_ref=slot(*block) if src is None else src, dst_ref=slot(*block),
                send_sem=send_sems.at[k], recv_sem=recv_sems.at[k], device_id=to, device_id_type=MESH)

        mine = pltpu.make_async_copy(w_ref, slot(*me), local_sem)
        first = [copy(0, me, sibling, src=w_ref)]
        first += [copy(1 + n, me, (*chip, c), src=w_ref) for n, chip in enumerate(chips)]
        passed = [copy(4 + n, (*chip, c), sibling) for n, chip in enumerate(chips)]

        @pl.when(pl.program_id(0) == 0)
        def _():
            mine.start()
            for cp in first:
                cp.start()

        norm_ref[...] = _rms(rows_ref[...], g_ref[...]).astype(norm_ref.dtype)

        @pl.when(pl.program_id(0) == steps - 1)
        def _():
            for n, chip in enumerate(chips):
                copy(1 + n, (*chip, c), me).wait_recv()
                passed[n].start()
            copy(0, sibling, me).wait_recv()
            for n, chip in enumerate(chips):
                copy(4 + n, (*chip, 1 - c), me).wait_recv()
            for cp in first + passed:
                cp.wait_send()
            mine.wait()

    tile = pl.BlockSpec((tm, D), lambda i: (i, 0))
    return pl.pallas_call(
        body, name="all_gather_weights", grid=(steps,),
        out_shape=[jax.ShapeDtypeStruct((N_DEV, R, W), shard.dtype), jax.ShapeDtypeStruct((T, D), BF)],
        in_specs=[pl.BlockSpec(memory_space=pl.ANY), tile, pl.BlockSpec((1, D), lambda i: (0, 0))],
        out_specs=[pl.BlockSpec(memory_space=pl.ANY), tile],
        scratch_shapes=[pltpu.SemaphoreType.DMA((N_PEERS,)), pltpu.SemaphoreType.DMA((N_PEERS,)), pltpu.SemaphoreType.DMA],
        compiler_params=_params(("arbitrary",)),
    )(shard, rows, g)


def _exchange_copies(src_refs, land_refs, send_sems, recv_sems, scatter):
    x, y, c = _place()
    me = 4 * x + 2 * y + c
    copies = []
    for k, (src_ref, land_ref) in enumerate(zip(src_refs, land_refs)):
        for r in range(1, N_DEV):
            px, py, pc = x ^ (r >> 2), y ^ ((r >> 1) & 1), c ^ (r & 1)
            copies.append(pltpu.make_async_remote_copy(
                src_ref=src_ref.at[4 * px + 2 * py + pc] if scatter else src_ref, dst_ref=land_ref.at[me],
                send_sem=send_sems.at[k * N_PEERS + r - 1], recv_sem=recv_sems.at[k * N_PEERS + r - 1],
                device_id=(px, py, pc), device_id_type=MESH))
    return copies


_HBM = pl.BlockSpec(memory_space=pltpu.HBM)
_SEM = pl.BlockSpec(memory_space=pltpu.SEMAPHORE)
_EFFECT = pltpu.SideEffectType.DATAFLOW_SIDE_EFFECTING


def _exchange_start(name, srcs, lands, scatter):
    n = len(srcs)

    def body(*refs):
        send_sems, recv_sems = refs[2 * n], refs[2 * n + 1]
        for cp in _exchange_copies(refs[:n], refs[n:2 * n], send_sems, recv_sems, scatter):
            cp.start()
        token = refs[-1]
        token[...] = jnp.zeros(token.shape, F32)

    arrays = list(srcs) + list(lands)
    out = pl.pallas_call(
        body, name=name,
        out_shape=(pltpu.SemaphoreType.DMA((n * N_PEERS,)), pltpu.SemaphoreType.DMA((n * N_PEERS,)))
        + tuple(pltpu.HBM(a.shape, a.dtype) for a in arrays) + (jax.ShapeDtypeStruct((8, LANES), F32),),
        in_specs=(_HBM,) * (2 * n), out_specs=(_SEM, _SEM) + (_HBM,) * (2 * n) + (pl.BlockSpec(memory_space=pltpu.VMEM),),
        input_output_aliases={k: 2 + k for k in range(2 * n)},
        compiler_params=pltpu.CompilerParams(has_side_effects=_EFFECT),
    )(*(pltpu.with_memory_space_constraint(a, pltpu.HBM) for a in arrays))
    return out[0], out[1], out[2:2 + n], out[2 + n:2 + 2 * n], out[-1]


def _exchange_wait(name, started, after, scatter):
    send_sems, recv_sems, srcs, lands, _ = started
    n = len(srcs)

    def body(*refs):
        copies = _exchange_copies(refs[:n], refs[n:2 * n], refs[2 * n], refs[2 * n + 1], scatter)
        for cp in copies:
            cp.wait_send()
        for cp in copies:
            cp.wait_recv()
        if not scatter:
            x, y, c = _place()
            own = [pltpu.make_async_copy(refs[k], refs[n + k].at[4 * x + 2 * y + c], refs[-1].at[k]) for k in range(n)]
            for cp in own:
                cp.start()
            for cp in own:
                cp.wait()

    arrays = list(srcs) + list(lands)
    out = pl.pallas_call(
        body, name=name,
        out_shape=tuple(pltpu.HBM(a.shape, a.dtype) for a in arrays),
        in_specs=(_HBM,) * (2 * n) + (_SEM, _SEM, pl.BlockSpec(memory_space=pl.ANY)), out_specs=(_HBM,) * (2 * n),
        input_output_aliases={k: k for k in range(2 * n)},
        scratch_shapes=[] if scatter else [pltpu.SemaphoreType.DMA((n,))],
        compiler_params=pltpu.CompilerParams(has_side_effects=_EFFECT),
    )(*arrays, send_sems, recv_sems, after)
    return out[:n], out[n:]


def _adam_update(g, w, m, v):
    m2 = ADAM_B1 * m + (1.0 - ADAM_B1) * g
    v2 = ADAM_B2 * v + (1.0 - ADAM_B2) * jnp.square(g)
    m_hat = m2 / (1.0 - ADAM_B1 ** ADAM_STEP)
    v_hat = v2 / (1.0 - ADAM_B2 ** ADAM_STEP)
    return g, -ADAM_LR * (m_hat / (jnp.sqrt(v_hat) + ADAM_EPS) + ADAM_WD * w), m2, v2


def _adamw(name, me, slots, sent, w, m, v):
    R, W = w.shape
    steps = max(k for k in (4, 2, 1) if k == 1 or (R % k == 0 and (R // k) % 16 == 0))
    tr = R // steps

    def body(me_ref, s_ref, *refs):
        if sent is not None:
            g = refs[0][0].astype(F32)
            refs = refs[1:]
        else:
            g = jnp.zeros((tr, W), F32)
        for s in range(N_DEV):
            part = s_ref[s].astype(F32)
            g = g + (part if sent is None else jnp.where(me_ref[0] == s, 0.0, part))
        w_ref, m_ref, v_ref = refs[:3]
        for o, r in zip(refs[3:], _adam_update(g, w_ref[...], m_ref[...], v_ref[...])):
            o[...] = r

    rows = pl.BlockSpec((tr, W), lambda i, me_ref: (i, 0))
    in_specs = [pl.BlockSpec((N_DEV, tr, W), lambda i, me_ref: (0, i, 0))]
    args = [slots]
    if sent is not None:
        in_specs.append(pl.BlockSpec((1, tr, W), lambda i, me_ref: (me_ref[0], i, 0)))
        args.append(sent)
    return pl.pallas_call(
        body, name=name,
        grid_spec=pltpu.PrefetchScalarGridSpec(num_scalar_prefetch=1, grid=(steps,), in_specs=in_specs + [rows] * 3,
                                               out_specs=[rows] * 4),
        out_shape=[jax.ShapeDtypeStruct((R, W), F32)] * 4,
        compiler_params=_params(("arbitrary",)),
    )(me, *args, w, m, v)


def _tables(T):
    pos = jnp.arange(T, dtype=F32)
    inv_freq = 10000.0 ** (-jnp.arange(0, HEAD, 2, dtype=F32) / HEAD)
    ang = pos[:, None] * inv_freq[None, :]
    cos, sin = jnp.cos(ang), jnp.sin(ang)
    cos4 = jnp.tile(cos, (1, 4))
    sin4 = jnp.tile(jnp.concatenate([-sin, sin], axis=1), (1, 2))
    log_g = jnp.log(1.0 - 2.0 ** (-5.0 - jnp.arange(8, dtype=F32)))
    return cos4, sin4, jnp.repeat(log_g, HEAD)[None, :]


def _local_step(x, hn1, mem, target, sp, w_inT, token, fetch_rest, push, push_small):
    T = x.shape[0]
    tm = min(512, T)
    tq = min(256, T)
    tb = min(1024, T)
    cos4, sin4, lg = _tables(T)
    g_fq2 = jnp.tile(sp["g_fox_q"], (1, 2))
    g_fk2 = jnp.tile(sp["g_fox_k"], (1, 2))
    g_ret = sp["g_ret_out"].reshape(1, 8 * HEAD)
    bpad = jnp.pad(sp["b_forget"], ((0, 0), (0, LANES - 8)))
    w_ffT = jnp.pad(w_inT[3584:3592], ((0, LANES - 8), (0, 0)))
    tie = lambda p, tok: p + tok[0:1, 0:1]
    tm2, tm4 = min(1024, T), min(2048, T)

    P, = _mm("proj_in", [[(hn1, w_inT, "nt")]], [], lambda acc, after: (acc,), T, 3584, tm4, 512, [F32],
             params=[jnp.broadcast_to(token[0:1, 0:1], (1, 3584))])
    ffp, = _mm("proj_ff", [[(hn1, w_ffT, "nt")]], [], _ident, T, LANES, tm, LANES, [F32])
    ret, s0 = _ret_fwd(P, cos4, sin4, g_ret, lg, T, tb)
    fc, _ = _fgate_fwd(ffp, bpad, T)
    qat, ka, kat, va, vat = _fox_operands(P, fc, g_fq2, g_fk2, T, tm4)
    fox, lse = _fox_forward(qat, ka, vat, T, min(512, T), tq)
    W = fetch_rest("attn", fox)
    h1, hn2 = _mm("proj_out", [[(ret, W["w_out"], "nn", 0), (fox, W["w_out"], "nn", 1)]], [x], _add_rms_epi, T, D, tm2, D,
                  [F32, BF], params=[sp["g_xattn"]])

    qx, = _mm("proj_xq", [[(hn2, W["w_xq"], "nn")]], [], _ident, T, D, tm2, D, [F32])
    memn, = _rw_fwd("rms_mem", _rms_fn, [(mem, D, 0, False)], [(sp["g_mem"], D, 0, False)], [(BF, D)], N_MEM, N_MEM, 1)
    kv, = _mm("proj_xkv", [[(memn, W["w_xkvT"], "nt")]], [], _ident, N_MEM, 2 * D, N_MEM, 512, [F32])
    xa_rows = [(qx, XHEAD, 0, True)]
    xa_params = [(sp["g_xq"], XHEAD, 0, False), (sp["g_xk"], XHEAD, 0, False), (kv, XHEAD, 0, True), (kv, XHEAD, 4, True)]
    xo, = _rw_fwd("xattn_fwd", _xattn_fn, xa_rows, xa_params, [(BF, XHEAD)], T, tm4, 4)
    h2, hn3 = _mm("proj_xo", [[(xo, W["w_xo"], "nn")]], [h1], _add_rms_epi, T, D, tm2, D, [F32, BF], params=[sp["g_ffn"]])

    W.update(fetch_rest("ffn", hn3))
    gate, up, act = _mm("ffn_in", [[(hn3, W["w_gateT"], "nt")], [(hn3, W["w_upT"], "nt")]], [], _swiglu_fwd_epi,
                        T, D_FF, tm4, 256, [BF, BF, BF])
    dy, dyb, loss_part = _mm("ffn_out", [[(act, W["w_down"], "nn")]], [h2, target], _add_loss_epi, T, D, tm, D, [F32, BF], n_acc=1)

    dgate, dup = _mm("ffn_out_bwd", [[(dyb, W["w_down"], "nt")]], [gate, up], _swiglu_bwd_epi, T, D_FF, tm4, 256, [BF, BF])
    gW = {}
    gW["w_gateT"], gW["w_upT"] = _mm("dw_gate_up", [[(dgate, hn3, "tn")], [(dup, hn3, "tn")]], [], _each, D_FF, D, 256, D, [BF, BF])
    gW["w_down"], = _mm("dw_down", [[(act, dyb, "tn")]], [], _ident, D_FF, D, 256, D, [BF])
    tok = push("ffn", gW)
    gs = {}
    dh2, dh2b, gs["g_ffn"] = _mm("ffn_in_bwd", [[(dgate, W["w_gateT"], "nn"), (dup, W["w_upT"], "nn")]], [h2, dy], _rms_bwd_epi,
                                 T, D, min(256, T), D, [F32, BF], params=[tie(sp["g_ffn"], tok)], n_acc=1)

    dxo, = _mm("proj_xo_bwd", [[(dh2b, W["w_xo"], "nt")]], [], _ident, T, D, tm2, D, [BF])
    gW["w_xo"], = _mm("dw_xo", [[(xo, dh2b, "tn")]], [], _ident, D, D, 256, D, [BF])
    dqx, gs["g_xq"], gs["g_xk"], dkv_k, dkv_v = _rw_bwd(
        "xattn_bwd", _xattn_fn, xa_rows, xa_params, [(dxo, XHEAD, 0, True)], T, tm4, 4, [BF], [True, True, True, True])
    dkv = jnp.concatenate([dkv_k[:, :D], dkv_v[:, D:]], axis=1)
    gW["w_xq"], = _mm("dw_xq", [[(hn2, dqx, "tn")]], [], _ident, D, D, 256, D, [BF])
    dmemn, = _mm("proj_xkv_bwd", [[(dkv, W["w_xkvT"], "nn")]], [], _ident, N_MEM, D, N_MEM, 512, [F32])
    gW["w_xkvT"], = _mm("dw_xkv", [[(dkv, memn, "tn")]], [], _ident, 2 * D, D, 512, D, [BF])
    tok = push("xattn", gW)
    gs["g_mem"], = _rw_bwd("rms_mem_bwd", _rms_fn, [(mem, D, 0, False)], [(sp["g_mem"], D, 0, False)], [(dmemn, D, 0, False)],
                           N_MEM, N_MEM, 1, [None], [True])
    dh1, dh1b, gs["g_xattn"] = _mm("proj_xq_bwd", [[(dqx, W["w_xq"], "nt")]], [h1, dh2], _rms_bwd_epi, T, D, tm, D, [F32, BF],
                                   params=[tie(sp["g_xattn"], tok)], n_acc=1)

    dmix, = _mm("proj_out_bwd", [[(dh1b, W["w_out"], "nt")]], [], _ident, T, D, tm2, D, [F32])
    gW["w_out"] = jnp.concatenate(_mm("dw_out", [[(ret, dh1b, "tn")], [(fox, dh1b, "tn")]], [], _each, 4 * LANES, D, 256, D,
                                      [BF, BF]), axis=0)
    tok = push("out", gW)
    doat, dl = _fox_cotangent(dmix, fox, T, tm4)
    dqn, dkn, dfv, dfc4, drc4 = _fox_backward(qat, ka, kat, va, doat, lse + tok[0:1, 0:1], dl, T, tq, tq)
    dfq, dfk, gq2, gk2 = _rw_bwd("fox_prep_bwd", _fox_prep_fn, [(P, LANES, 16, True), (P, LANES, 20, True)],
                                 [(g_fq2, LANES, 0, False), (g_fk2, LANES, 0, False)],
                                 [(dqn, LANES, 0, True), (dkn, LANES, 0, True)], T, tm4, 4, [BF, BF], [True, True])
    gs["g_fox_q"] = gq2[:, :HEAD] + gq2[:, HEAD:]
    gs["g_fox_k"] = gk2[:, :HEAD] + gk2[:, HEAD:]
    dff, dbp = _fgate_bwd_col(ffp, bpad, dfc4, drc4, T)
    gs["b_forget"] = dbp[:, :8]
    drq, drk, drv, drg, dg_ret = _ret_bwd(P, cos4, sin4, g_ret, lg, s0, dmix, T, tb)
    gs["g_ret_out"] = dg_ret
    dsecs = [drq, drk, drv, drg, dfq, dfk, dfv]
    g_secs = list(_mm("dw_in", [[(d, hn1, "tn")] for d in dsecs], [], _each, 512, D, LANES, D, [BF] * len(dsecs)))
    g_ff, = _mm("dw_in_ff", [[(dff, hn1, "tn")]], [], _ident, LANES, D, LANES, D, [BF])
    gW["w_inT"] = jnp.concatenate(g_secs + [g_ff[:8]], axis=0)
    tok = push("in", gW)
    grad_x, gs["g_mix"] = _mm("proj_in_bwd", [[(d, w_inT, "nn", k) for k, d in enumerate(dsecs)] + [(dff, w_ffT, "nn")]], [x, dh1],
                              _rms_bwd_first_epi, T, D, tm, D, [F32], params=[tie(sp["g_mix"], tok)], n_acc=1)
    return grad_x, push_small(gs, loss_part)


_CANON = {"w_in": "w_inT", "w_xkv": "w_xkvT", "w_gate": "w_gateT", "w_up": "w_upT"}
_SMALL = (("g_mix", 0, 0, 1024), ("g_xattn", 1, 0, 1024), ("g_mem", 2, 0, 1024), ("g_ffn", 3, 0, 1024),
          ("g_ret_out", 4, 0, 512), ("g_xq", 4, 512, 256), ("g_xk", 4, 768, 256),
          ("g_fox_q", 5, 0, 64), ("g_fox_k", 5, 64, 64), ("b_forget", 5, 128, 8))
_LOSS_AT = (5, 256)


def _pack_small(tree):
    buf = jnp.zeros((SMALL_ROWS, D), F32)
    for name, r, c, n in _SMALL:
        buf = lax.dynamic_update_slice(buf, tree[name].reshape(1, n).astype(F32), (r, c))
    return buf


def _unpack_small(buf, like):
    return {name: buf[r:r + 1, c:c + n].reshape(like[name].shape) for name, r, c, n in _SMALL}


def _canonical(tree, name):
    a = tree[name][0]
    return a.T if W_SHARD[name][1] else a


def _from_canonical(a, name):
    return (a.T if W_SHARD[name][1] else a)[None]


def kernel(x, mem, g_mix, w_in, b_forget, g_ret_out, g_fox_q, g_fox_k, w_out, g_xattn, w_xq, w_xkv, g_mem, g_xq, g_xk, w_xo, g_ffn, w_gate, w_up, w_down, loss_target, m_g_mix, m_w_in, m_b_forget, m_g_ret_out, m_g_fox_q, m_g_fox_k, m_w_out, m_g_xattn, m_w_xq, m_w_xkv, m_g_mem, m_g_xq, m_g_xk, m_w_xo, m_g_ffn, m_w_gate, m_w_up, m_w_down, v_g_mix, v_w_in, v_b_forget, v_g_ret_out, v_g_fox_q, v_g_fox_k, v_w_out, v_g_xattn, v_w_xq, v_w_xkv, v_g_mem, v_g_xq, v_g_xk, v_w_xo, v_g_ffn, v_w_gate, v_w_up, v_w_down):
    names = ("g_mix", "w_in", "b_forget", "g_ret_out", "g_fox_q", "g_fox_k", "w_out", "g_xattn", "w_xq", "w_xkv", "g_mem",
             "g_xq", "g_xk", "w_xo", "g_ffn", "w_gate", "w_up", "w_down")
    w = dict(zip(names, (g_mix, w_in, b_forget, g_ret_out, g_fox_q, g_fox_k, w_out, g_xattn, w_xq, w_xkv, g_mem, g_xq, g_xk,
                         w_xo, g_ffn, w_gate, w_up, w_down)))
    m = dict(zip(names, (m_g_mix, m_w_in, m_b_forget, m_g_ret_out, m_g_fox_q, m_g_fox_k, m_w_out, m_g_xattn, m_w_xq, m_w_xkv,
                         m_g_mem, m_g_xq, m_g_xk, m_w_xo, m_g_ffn, m_w_gate, m_w_up, m_w_down)))
    v = dict(zip(names, (v_g_mix, v_w_in, v_b_forget, v_g_ret_out, v_g_fox_q, v_g_fox_k, v_w_out, v_g_xattn, v_w_xq, v_w_xkv,
                         v_g_mem, v_g_xq, v_g_xk, v_w_xo, v_g_ffn, v_w_gate, v_w_up, v_w_down)))
    small_names = [s[0] for s in _SMALL]
    me = 4 * lax.axis_index("x") + 2 * lax.axis_index("y") + lax.axis_index("c")
    me1 = me.astype(jnp.int32).reshape(1)

    sp = {n: w[n].reshape(1, -1) for n in small_names}
    first, hn1 = _all_gather(_canonical(w, "w_in").astype(BF), x[0], sp["g_mix"], min(1024, x.shape[1]))
    first, rests = lax.optimization_barrier((first, {g: [_canonical(w, n).astype(BF) for n in ns] for g, ns in GATHER_REST.items()}))
    rest_started = {g: _exchange_start("gather_%s_start" % g, rests[g], [lax.empty((N_DEV,) + a.shape, BF) for a in rests[g]],
                                       scatter=False) for g in GATHER_REST}
    after = rest_started["attn"][4] + rest_started["ffn"][4]

    def fetch_rest(group, after):
        lands = _exchange_wait("gather_%s_wait" % group, rest_started[group], after, scatter=False)[1]
        return {_CANON.get(n, n): a.reshape(N_DEV * a.shape[1], D) for n, a in zip(GATHER_REST[group], lands)}

    pushed = {}

    def push(group, grads):
        srcs = [grads[_CANON.get(n, n)].reshape(N_DEV, W_SHARD[n][0], D) for n in SCATTER_GROUPS[group]]
        pushed[group] = _exchange_start("scatter_%s_start" % group, srcs, [lax.empty(a.shape, BF) for a in srcs], scatter=True)
        return pushed[group][4]

    def push_small(gs, loss_part):
        small = lax.dynamic_update_slice(_pack_small(gs), loss_part[:, :1], _LOSS_AT)
        pushed["small"] = _exchange_start("gather_small_start", [small], [lax.empty((N_DEV,) + small.shape, F32)], scatter=False)
        return pushed["small"][4]

    grad_x, done = _local_step(x[0], hn1, mem[0], loss_target[0], sp, first.reshape(N_DEV * W_SHARD["w_in"][0], D),
                               after, fetch_rest, push, push_small)

    results, after = {}, done
    for group in ("ffn", "xattn", "out", "small", "in"):
        if group == "small":
            recv_small = _exchange_wait("gather_small_wait", pushed["small"], after, scatter=False)[1][0]
            g_sm, d_sm, m_sm, v_sm = _adamw("adamw_small", me1, recv_small, None, _pack_small(w), _pack_small(m), _pack_small(v))
            after = g_sm
            continue
        sents, recvs = _exchange_wait("scatter_%s_wait" % group, pushed[group], after, scatter=True)
        for name, sent, recv in zip(SCATTER_GROUPS[group], sents, recvs):
            res = _adamw("adamw_" + name, me1, recv, sent, *(_canonical(t, name) for t in (w, m, v)))
            results[name] = [_from_canonical(r, name) for r in res]
        after = results[SCATTER_GROUPS[group][-1]][0]
    loss = g_sm[_LOSS_AT[0], _LOSS_AT[1]]

    outs = []
    for k, sm in enumerate((g_sm, d_sm, m_sm, v_sm)):
        tree = _unpack_small(sm, w)
        tree.update({name: res[k] for name, res in results.items()})
        outs += [tree[n] for n in names]
    return (loss, grad_x[None], *outs)
```

```python
import jax
import jax.numpy as jnp
from jax import lax
from jax.experimental import pallas as pl
from jax.experimental.pallas import tpu as pltpu

F32 = jnp.float32
BF = jnp.bfloat16

D = 1024
HEAD = 64
CHUNK = 64
N_MEM = 256
XHEAD = 256
D_FF = 2816
EPS = 1e-6
NEG = -1e30
LANES = 128
N_DEV = 8
V7X_VMEM_BYTES = 64 * 1024 * 1024
VMEM_LIMIT = V7X_VMEM_BYTES - 8 * 1024 * 1024

ADAM_LR, ADAM_B1, ADAM_B2, ADAM_EPS, ADAM_WD, ADAM_STEP = 0.001, 0.9, 0.999, 1e-08, 0.01, 10

W_SHARD = {"w_in": (449, True), "w_out": (128, False), "w_xq": (128, False), "w_xkv": (256, True),
           "w_xo": (128, False), "w_gate": (352, True), "w_up": (352, True), "w_down": (352, False)}
GATHER_REST = {"attn": ("w_out", "w_xq", "w_xkv", "w_xo"), "ffn": ("w_gate", "w_up", "w_down")}
SCATTER_GROUPS = {"ffn": ("w_gate", "w_up", "w_down"), "xattn": ("w_xq", "w_xo", "w_xkv"), "out": ("w_out",), "in": ("w_in",)}
SMALL_ROWS = 8

NT = (((1,), (1,)), ((), ()))
NN = (((1,), (0,)), ((), ()))
TN = (((0,), (0,)), ((), ()))
_DIMS = {"nn": NN, "nt": NT, "tn": TN}


def _params(sem):
    return pltpu.CompilerParams(dimension_semantics=sem, vmem_limit_bytes=VMEM_LIMIT)


def _mm(name, products, extras, epilogue, M, N, tm, tn, out_dtypes, params=(), n_acc=0):
    assert n_acc == 0 or tn == N
    flat = [t for p in products for t in p]
    counts = [len(p) for p in products]
    in_specs, args, where, slots = [], [], {}, []

    def operand(arr, spec, kind):
        key = (id(arr), kind)
        if key not in where:
            where[key] = len(args)
            args.append(arr)
            in_specs.append(spec)
        return where[key]

    for a, b, form, *at in flat:
        if form == "tn":
            ia = operand(a, pl.BlockSpec((a.shape[0], tm), lambda i, j: (0, i)), "a_tn")
        else:
            ia = operand(a, pl.BlockSpec((tm, a.shape[1]), lambda i, j: (i, 0)), "a")
        if form == "nt":
            ib = operand(b, pl.BlockSpec((tn, b.shape[1]), lambda i, j: (j, 0)), "b_nt")
        elif form == "nn":
            k = at[0] if at else 0
            ib = operand(b, pl.BlockSpec((a.shape[1], tn), lambda i, j, k=k: (k, j)), "b%d" % k)
        else:
            ib = operand(b, pl.BlockSpec((b.shape[0], tn), lambda i, j: (0, j)), "b")
        slots.append((ia, ib))
    n_mm = len(args)
    for e in extras:
        in_specs.append(pl.BlockSpec((tm, tn), lambda i, j: (i, j)))
        args.append(e)
    for p in params:
        in_specs.append(pl.BlockSpec((1, tn), lambda i, j: (0, j)))
        args.append(p)
    n_in = len(args)
    n_out = len(out_dtypes)

    def body(*refs):
        ins, outs = refs[:n_in], refs[n_in:]
        prods, p = [], 0
        for c in counts:
            acc = None
            for _ in range(c):
                a = ins[slots[p][0]][...].astype(BF)
                b = ins[slots[p][1]][...].astype(BF)
                d = lax.dot_general(a, b, _DIMS[flat[p][2]], preferred_element_type=F32)
                acc = d if acc is None else acc + d
                p += 1
            prods.append(acc)
        ex = [r[...].astype(F32) for r in ins[n_mm:]]
        res = epilogue(*prods, *ex)
        for o, r in zip(outs[:n_out], res[:n_out]):
            o[...] = r.astype(o.dtype)
        for o, r in zip(outs[n_out:], res[n_out:]):
            @pl.when(pl.program_id(0) == 0)
            def _(o=o):
                o[...] = jnp.zeros(o.shape, F32)
            o[...] += r

    return pl.pallas_call(
        body, name=name, grid=(M // tm, N // tn), in_specs=in_specs,
        out_specs=[pl.BlockSpec((tm, tn), lambda i, j: (i, j)) for _ in out_dtypes]
        + [pl.BlockSpec((1, tn), lambda i, j: (0, j)) for _ in range(n_acc)],
        out_shape=[jax.ShapeDtypeStruct((M, N), dt) for dt in out_dtypes] + [jax.ShapeDtypeStruct((1, N), F32)] * n_acc,
        compiler_params=_params(("arbitrary", "arbitrary")),
    )(*args)


def _ident(x):
    return (x,)


def _each(*xs):
    return xs


def _spec(rows, w, off, per_j):
    if per_j:
        return pl.BlockSpec((rows, w), lambda j, i: (i, off + j))
    return pl.BlockSpec((rows, w), lambda j, i: (i, off))


def _pspec(rows, w, off, per_j):
    if per_j:
        return pl.BlockSpec((rows, w), lambda j, i: (0, off + j))
    return pl.BlockSpec((rows, w), lambda j, i: (0, off))


def _rw_fwd(name, fn, rows, params, outs, T, tm, nj, n_acc=0):
    in_specs = [_spec(tm, w, off, pj) for _, w, off, pj in rows] + [_pspec(a.shape[0], w, off, pj) for a, w, off, pj in params]
    args = [r[0] for r in rows] + [p[0] for p in params]
    n_in, n_out = len(args), len(outs)
    out_specs = [pl.BlockSpec((tm, w), lambda j, i: (i, j)) for _, w in outs]
    out_shape = [jax.ShapeDtypeStruct((T, nj * w), dt) for dt, w in outs]
    out_specs += [pl.BlockSpec((1, LANES), lambda j, i: (0, 0)) for _ in range(n_acc)]
    out_shape += [jax.ShapeDtypeStruct((1, LANES), F32) for _ in range(n_acc)]

    def body(*refs):
        vals = [r[...].astype(F32) for r in refs[:n_in]]
        res = fn(*vals)
        orefs = refs[n_in:]
        for k in range(n_out):
            orefs[k][...] = res[k].astype(orefs[k].dtype)
        first = (pl.program_id(0) == 0) & (pl.program_id(1) == 0)
        for k in range(n_acc):
            @pl.when(first)
            def _(k=k):
                orefs[n_out + k][...] = jnp.zeros((1, LANES), F32)
            orefs[n_out + k][...] += res[n_out + k]

    return pl.pallas_call(
        body, name=name, grid=(nj, T // tm), in_specs=in_specs, out_specs=out_specs, out_shape=out_shape,
        compiler_params=_params(("arbitrary", "arbitrary")),
    )(*args)


def _rw_bwd(name, fn, rows, params, cots, T, tm, nj, row_grads, param_grads, resid=None):
    in_specs = ([_spec(tm, w, off, pj) for _, w, off, pj in rows] + [_pspec(a.shape[0], w, off, pj) for a, w, off, pj in params]
                + [_spec(tm, w, off, pj) for _, w, off, pj in cots])
    args = [r[0] for r in rows] + [p[0] for p in params] + [c[0] for c in cots]
    if resid is not None:
        in_specs.append(_spec(tm, rows[0][1], rows[0][2], rows[0][3]))
        args.append(resid)
    nr, npar, nc = len(rows), len(params), len(cots)
    out_specs, out_shape, kinds = [], [], []
    for k, dts in enumerate(row_grads):
        for dt in (dts if isinstance(dts, (list, tuple)) else [dts]):
            if dt is not None:
                w = rows[k][1]
                out_specs.append(pl.BlockSpec((tm, w), lambda j, i: (i, j)))
                out_shape.append(jax.ShapeDtypeStruct((T, nj * w), dt))
                kinds.append(("row", k))
    for k, need in enumerate(param_grads):
        if need:
            a, w, off, pj = params[k]
            out_specs.append(_pspec(a.shape[0], w, off, pj))
            out_shape.append(jax.ShapeDtypeStruct(a.shape, F32))
            kinds.append(("par", k))

    def body(*refs):
        vals = [r[...].astype(F32) for r in refs[:nr + npar]]
        ct = tuple(r[...].astype(F32) for r in refs[nr + npar:nr + npar + nc])
        _, vjp = jax.vjp(lambda *a: tuple(fn(*a)), *vals)
        grads = list(vjp(ct))
        n_in = nr + npar + nc + (resid is not None)
        if resid is not None:
            grads[0] = grads[0] + refs[n_in - 1][...].astype(F32)
        orefs = refs[n_in:]
        j, i = pl.program_id(0), pl.program_id(1)
        for o, (kind, k) in zip(orefs, kinds):
            if kind == "row":
                o[...] = grads[k].astype(o.dtype)
            else:
                first = (i == 0) if params[k][3] else ((i == 0) & (j == 0))

                @pl.when(first)
                def _(o=o):
                    o[...] = jnp.zeros(o.shape, F32)
                o[...] += grads[nr + k]

    return pl.pallas_call(
        body, name=name, grid=(nj, T // tm), in_specs=in_specs, out_specs=out_specs, out_shape=out_shape,
        compiler_params=_params(("arbitrary", "arbitrary")),
    )(*args)


def _rms(x, g):
    return x * lax.rsqrt(jnp.mean(x * x, axis=-1, keepdims=True) + EPS) * g


def _rms_fn(x, g):
    return (_rms(x, g),)


def _lo_mask():
    return lax.broadcasted_iota(jnp.int32, (1, LANES), 1) < HEAD


def _gmean(x, lo):
    s0 = jnp.sum(jnp.where(lo, x, 0.0), axis=-1, keepdims=True)
    s1 = jnp.sum(jnp.where(lo, 0.0, x), axis=-1, keepdims=True)
    return jnp.where(lo, s0, s1) * (1.0 / HEAD)


def _fox_prep_fn(fq, fk, gq, gk):
    lo = _lo_mask()
    qn = fq * lax.rsqrt(_gmean(fq * fq, lo) + EPS) * gq * (HEAD ** -0.5)
    kn = fk * lax.rsqrt(_gmean(fk * fk, lo) + EPS) * gk
    return qn, kn


@jax.custom_vjp
def _swap_halves(x):
    bit = (lax.broadcasted_iota(jnp.int32, (1, LANES), 1) & (HEAD // 2)) == 0
    return jnp.where(bit, pltpu.roll(x, LANES - HEAD // 2, 1), pltpu.roll(x, HEAD // 2, 1))


_swap_halves.defvjp(lambda x: (_swap_halves(x), None), lambda _, g: (_swap_halves(g),))


def _ret_fn(rq, rk, rv, rg, cos, sin, s_in, g, lg):
    tb = rq.shape[0]
    nc = tb // CHUNK
    lo = _lo_mask()
    row = lax.broadcasted_iota(jnp.int32, (LANES, 1), 0) < HEAD
    same_head = row == lo
    q = (rq * cos + _swap_halves(rq) * sin) * (HEAD ** -0.5)
    k = rk * cos + _swap_halves(rk) * sin
    q3, k3, v3 = q.reshape(nc, CHUNK, LANES), k.reshape(nc, CHUNK, LANES), rv.reshape(nc, CHUNK, LANES)
    pos = lax.broadcasted_iota(jnp.int32, (CHUNK, 1), 0).astype(F32)
    q_decay = jnp.exp(lg * (pos + 1.0))
    k_decay = jnp.exp(lg * (CHUNK - 1.0 - pos))
    chunk_decay = jnp.exp(lg * float(CHUNK))
    dist = jnp.abs(lax.broadcasted_iota(jnp.int32, (CHUNK, CHUNK), 0) - lax.broadcasted_iota(jnp.int32, (CHUNK, CHUNK), 1)).astype(F32)
    v3b = v3.astype(BF)
    intra = []
    for hh in range(2):
        hm = lo if hh == 0 else ~lo
        lg_h = lg[:, hh * HEAD:hh * HEAD + 1]
        qm = jnp.where(hm, q3, 0.0).astype(BF)
        sc = jnp.einsum("nid,njd->nij", qm, k3.astype(BF), preferred_element_type=F32) * jnp.exp(lg_h * dist)[None]
        intra.append(jnp.einsum("nij,nje->nie", sc.astype(BF), v3b, preferred_element_type=F32))
    o = jnp.where(lo, intra[0], intra[1])
    kv = jnp.einsum("njd,nje->nde", (k3 * k_decay[None]).astype(BF), v3b, preferred_element_type=F32)
    kv = jnp.where(same_head[None], kv, 0.0)
    state, states = s_in, []
    for n in range(nc):
        states.append(state)
        state = state * chunk_decay + kv[n]
    s_prev = jnp.stack(states, axis=0)
    o = o + jnp.einsum("nid,nde->nie", (q3 * q_decay[None]).astype(BF), s_prev.astype(BF), preferred_element_type=F32)
    o = o.reshape(tb, LANES)
    mu = _gmean(o, lo)
    oc = o - mu
    y = oc * lax.rsqrt(_gmean(oc * oc, lo) + EPS) * g
    return jax.nn.silu(rg) * y, state


def _xattn_fn(qx, gq, gk, kk, vv):
    q = _rms(qx, gq)
    k = _rms(kk, gk)
    logits = lax.dot_general(q.astype(BF), k.astype(BF), NT, preferred_element_type=F32) * (XHEAD ** -0.5)
    p = jax.nn.softmax(logits, axis=-1)
    return (jnp.dot(p.astype(BF), vv.astype(BF), preferred_element_type=F32),)


def _swiglu_fwd_epi(g, u):
    return g, u, jax.nn.silu(g) * u


def _swiglu_bwd_epi(dact, g, u):
    _, vjp = jax.vjp(lambda a, b: jax.nn.silu(a) * b, g, u)
    return vjp(dact)


def _ffn_out_bwd(dyb, w_down, gate, up, act, T, tm, tn):
    n_col = D_FF // tn

    def body(dy_ref, w_ref, g_ref, u_ref, a_ref, dg_ref, du_ref, dw_ref, acc, stage, sem):
        i, j = pl.program_id(0), pl.program_id(1)

        @pl.when((i == 0) & (j == 0))
        def _():
            acc[...] = jnp.zeros(acc.shape, F32)

        dy = dy_ref[...]
        dact = lax.dot_general(dy, w_ref[...], NT, preferred_element_type=F32)
        dg, du = _swiglu_bwd_epi(dact, g_ref[...].astype(F32), u_ref[...].astype(F32))
        dg_ref[...] = dg.astype(dg_ref.dtype)
        du_ref[...] = du.astype(du_ref.dtype)
        acc[j] += lax.dot_general(a_ref[...], dy, TN, preferred_element_type=F32)

        @pl.when(i == T // tm - 1)
        def _():
            stage[...] = acc[j].astype(stage.dtype)
            done = pltpu.make_async_copy(stage, dw_ref.at[pl.ds(pl.multiple_of(j * tn, tn), tn)], sem)
            done.start()
            done.wait()

    tile = pl.BlockSpec((tm, tn), lambda i, j: (i, j))
    return pl.pallas_call(
        body, name="ffn_out_bwd", grid=(T // tm, n_col),
        in_specs=[pl.BlockSpec((tm, D), lambda i, j: (i, 0)), pl.BlockSpec((tn, D), lambda i, j: (j, 0)), tile, tile, tile],
        out_specs=[tile, tile, pl.BlockSpec(memory_space=pl.ANY)],
        out_shape=[jax.ShapeDtypeStruct((T, D_FF), BF)] * 2 + [jax.ShapeDtypeStruct((D_FF, D), BF)],
        scratch_shapes=[pltpu.VMEM((n_col, tn, D), F32), pltpu.VMEM((tn, D), BF), pltpu.SemaphoreType.DMA],
        compiler_params=_params(("arbitrary", "arbitrary")),
    )(dyb, w_down, gate, up, act)


def _add_rms_epi(acc, resid, g):
    h = acc + resid
    return h, _rms(h, g)


def _add_loss_epi(acc, resid, target):
    err = (acc + resid) - target
    dy = err * (1.0 / D)
    part = jnp.sum(jnp.sum(err * err, axis=0, keepdims=True), axis=1, keepdims=True) * (0.5 / D)
    return dy, dy, jnp.broadcast_to(part, (1, err.shape[1]))


def _rms_bwd_epi(dhn, h, skip, g):
    _, vjp = jax.vjp(_rms, h, g)
    dh, dg = vjp(dhn)
    dh = dh + skip
    return dh, dh, dg


def _rms_bwd_first_epi(dhn, h, skip, g):
    return _rms_bwd_epi(dhn, h, skip, g)[1:]


def _ret_fwd(P, cos, sin, g_ret, lg, T, tb):
    nb = T // tb

    def body(rq, rk, rv, rg, c, s, g, l, o_ref, s0_ref, state):
        @pl.when(pl.program_id(1) == 0)
        def _():
            state[...] = jnp.zeros(state.shape, F32)
        s0_ref[0, 0] = state[...]
        out, s_new = _ret_fn(rq[...], rk[...], rv[...], rg[...], c[...], s[...], state[...], g[...], l[...])
        o_ref[...] = out.astype(o_ref.dtype)
        state[...] = s_new

    sec = lambda off: pl.BlockSpec((tb, LANES), lambda j, i: (i, off + j))
    tab = pl.BlockSpec((tb, LANES), lambda j, i: (i, 0))
    par = pl.BlockSpec((1, LANES), lambda j, i: (0, j))
    return pl.pallas_call(
        body, name="ret_fwd", grid=(4, nb),
        in_specs=[sec(0), sec(4), sec(8), sec(12), tab, tab, par, par],
        out_specs=[pl.BlockSpec((tb, LANES), lambda j, i: (i, j)), pl.BlockSpec((1, 1, LANES, LANES), lambda j, i: (j, i, 0, 0))],
        out_shape=[jax.ShapeDtypeStruct((T, 4 * LANES), BF), jax.ShapeDtypeStruct((4, nb, LANES, LANES), F32)],
        scratch_shapes=[pltpu.VMEM((LANES, LANES), F32)],
        compiler_params=_params(("arbitrary", "arbitrary")),
    )(P, P, P, P, cos, sin, g_ret, lg)


def _ret_bwd(P, cos, sin, g_ret, lg, s0, dmix, T, tb):
    nb = T // tb

    def body(rq, rk, rv, rg, c, s, g, l, s0_ref, do, drq, drk, drv, drg, dg, dstate):
        i = pl.program_id(1)

        @pl.when(i == 0)
        def _():
            dstate[...] = jnp.zeros(dstate.shape, F32)
            dg[...] = jnp.zeros(dg.shape, F32)

        cc, ss, ll = c[...], s[...], l[...]
        _, vjp = jax.vjp(lambda a, b, v, gate, st, gg: _ret_fn(a, b, v, gate, cc, ss, st, gg, ll),
                         rq[...], rk[...], rv[...], rg[...], s0_ref[0, 0], g[...])
        ga, gb, gv, ggate, gst, ggain = vjp((do[...], dstate[...]))
        drq[...] = ga.astype(drq.dtype)
        drk[...] = gb.astype(drk.dtype)
        drv[...] = gv.astype(drv.dtype)
        drg[...] = ggate.astype(drg.dtype)
        dstate[...] = gst
        dg[...] += ggain

    rev = lambda i: nb - 1 - i
    sec = lambda off: pl.BlockSpec((tb, LANES), lambda j, i: (rev(i), off + j))
    tab = pl.BlockSpec((tb, LANES), lambda j, i: (rev(i), 0))
    par = pl.BlockSpec((1, LANES), lambda j, i: (0, j))
    outb = pl.BlockSpec((tb, LANES), lambda j, i: (rev(i), j))
    return pl.pallas_call(
        body, name="ret_bwd", grid=(4, nb),
        in_specs=[sec(0), sec(4), sec(8), sec(12), tab, tab, par, par,
                  pl.BlockSpec((1, 1, LANES, LANES), lambda j, i: (j, rev(i), 0, 0)), outb],
        out_specs=[outb, outb, outb, outb, par],
        out_shape=[jax.ShapeDtypeStruct((T, 4 * LANES), BF)] * 4 + [jax.ShapeDtypeStruct((1, 4 * LANES), F32)],
        scratch_shapes=[pltpu.VMEM((LANES, LANES), F32)],
        compiler_params=_params(("arbitrary", "arbitrary")),
    )(P, P, P, P, cos, sin, g_ret, lg, s0, dmix)


_FB = 128


def _tri(lower):
    r = lax.broadcasted_iota(jnp.int32, (_FB, _FB), 0)
    c = lax.broadcasted_iota(jnp.int32, (_FB, _FB), 1)
    return ((r >= c) if lower else (r <= c)).astype(F32)


def _fgate_fwd(ffp, bpad, T):
    def body(ff_ref, b_ref, fc_ref, fr_ref):
        lane = lax.broadcasted_iota(jnp.int32, (1, LANES), 1)
        tri = _tri(True)
        carry = jnp.zeros((1, LANES), F32)
        for blk in range(T // _FB):
            z = ff_ref[blk * _FB:(blk + 1) * _FB, :] + b_ref[...]
            lf = jnp.where(lane < 8, jax.nn.log_sigmoid(z), 0.0)
            f = jnp.dot(tri, lf, precision=lax.Precision.HIGHEST, preferred_element_type=F32) + carry
            carry = f[_FB - 1:_FB, :]
            fc_ref[blk * _FB:(blk + 1) * _FB, :] = f
            fr_ref[:, blk * _FB:(blk + 1) * _FB] = f.T[:8, :]

    return pl.pallas_call(
        body, name="fgate_fwd",
        out_shape=[jax.ShapeDtypeStruct((T, LANES), F32), jax.ShapeDtypeStruct((8, T), F32)],
        compiler_params=pltpu.CompilerParams(vmem_limit_bytes=VMEM_LIMIT),
    )(ffp, bpad)


_BIAS_LANE = HEAD


def _head_bias_col(fc, head):
    lane = lax.broadcasted_iota(jnp.int32, (1, LANES), 1)
    return jnp.sum(jnp.where(lane == head, fc, 0.0), axis=-1, keepdims=True)


def _split3(f):
    hi = f.astype(BF).astype(F32)
    mid = (f - hi).astype(BF).astype(F32)
    lo = ((f - hi) - mid).astype(BF).astype(F32)
    return hi, mid, lo


def _fox_operands(P, fc, g_fq2, g_fk2, T, tm):
    def body(fq_ref, fk_ref, fv_ref, fc_ref, gq_ref, gk_ref, qat_ref, ka_ref, kat_ref, va_ref, vat_ref):
        j = pl.program_id(0)
        lane = lax.broadcasted_iota(jnp.int32, (1, LANES), 1)
        qn, kn = _fox_prep_fn(fq_ref[...], fk_ref[...], gq_ref[...], gk_ref[...])
        v = fv_ref[...]
        fcb = fc_ref[...]
        b = _BIAS_LANE
        for hh in range(2):
            hi, mid, lo = _split3(_head_bias_col(fcb, 2 * j + hh))
            take = (lambda a: a) if hh == 0 else (lambda a: pltpu.roll(a, HEAD, 1))
            qa = jnp.where(lane < HEAD, take(qn), jnp.where(lane == b, hi, jnp.where(lane == b + 1, mid, jnp.where(
                lane == b + 2, lo, jnp.where(lane < b + 6, 1.0, 0.0)))))
            ka = jnp.where(lane < HEAD, take(kn), jnp.where(lane < b + 3, 1.0, jnp.where(lane == b + 3, -hi, jnp.where(
                lane == b + 4, -mid, jnp.where(lane == b + 5, -lo, 0.0)))))
            va = jnp.where(lane < HEAD, take(v), 0.0)
            qat_ref[hh] = qa.T.astype(BF)
            for val, ref, tref in ((ka, ka_ref, kat_ref), (va, va_ref, vat_ref)):
                ref[hh] = val.astype(BF)
                tref[hh] = val.T.astype(BF)

    sec = lambda off: pl.BlockSpec((tm, LANES), lambda j, i: (i, off + j))
    par = pl.BlockSpec((1, LANES), lambda j, i: (0, 0))
    nat = pl.BlockSpec((2, tm, LANES), lambda j, i: (j, i, 0))
    trn = pl.BlockSpec((2, LANES, tm), lambda j, i: (j, 0, i))
    return pl.pallas_call(
        body, name="fox_operands", grid=(4, T // tm),
        in_specs=[sec(16), sec(20), sec(24), pl.BlockSpec((tm, LANES), lambda j, i: (i, 0)), par, par],
        out_specs=[trn, nat, trn, nat, trn],
        out_shape=[jax.ShapeDtypeStruct((8, LANES, T), BF)]
        + [jax.ShapeDtypeStruct((8, T, LANES), BF), jax.ShapeDtypeStruct((8, LANES, T), BF)] * 2,
        compiler_params=_params(("parallel", "arbitrary")),
    )(P, P, P, fc, g_fq2, g_fk2)


def _fox_forward(qat, ka, vat, T, tq, tk):
    nq, per = T // tq, tq // tk
    assert per == 2
    RC = 64

    def body(qat_ref, ka_ref, vat_ref, o_ref, lse_ref, s_scr, p_scr, a_scr, m_scr, l_scr, acc_scr):
        i = pl.program_id(1)
        sub = lax.broadcasted_iota(jnp.int32, (8, 1), 0)
        row = lax.broadcasted_iota(jnp.int32, (RC, tq), 0)
        col = lax.broadcasted_iota(jnp.int32, (RC, tq), 1)
        m_scr[...] = jnp.full(m_scr.shape, NEG, F32)
        l_scr[...] = jnp.zeros(l_scr.shape, F32)
        acc_scr[...] = jnp.zeros(acc_scr.shape, F32)

        def scores(slot, kb):
            k0 = pl.multiple_of(kb * tk, tk)
            for hh in range(2):
                s_scr[slot, hh] = jnp.dot(ka_ref[hh, pl.ds(k0, tk), :], qat_ref[hh], preferred_element_type=F32)

        def softmax(slot, kb, diagonal):
            shift = kb * tk - i * tq
            for hh in range(2):
                def masked(r):
                    tile = s_scr[slot, hh, r * RC:(r + 1) * RC, :]
                    return jnp.where(row + (r * RC + shift) <= col, tile, NEG) if diagonal else tile

                mx = jnp.max(masked(0), axis=0, keepdims=True)
                for r in range(1, tk // RC):
                    mx = jnp.maximum(mx, jnp.max(masked(r), axis=0, keepdims=True))
                m_old = m_scr[hh, 0:1, :]
                m2 = jnp.maximum(m_old, mx)
                a = jnp.exp(m_old - m2)
                lsum = jnp.zeros((1, tq), F32)
                for r in range(tk // RC):
                    p = jnp.exp(masked(r) - m2)
                    p_scr[slot, hh, r * RC:(r + 1) * RC, :] = p.astype(BF)
                    lsum = lsum + jnp.sum(p, axis=0, keepdims=True)
                m_scr[hh] = jnp.broadcast_to(m2, (8, tq))
                l_scr[hh] = jnp.broadcast_to(a * l_scr[hh, 0:1, :] + lsum, (8, tq))
                a_scr[slot, hh] = jnp.broadcast_to(a, (8, tq))

        def values(slot, kb):
            k0 = pl.multiple_of(kb * tk, tk)
            for hh in range(2):
                pv = jnp.dot(vat_ref[hh, 0:HEAD, pl.ds(k0, tk)], p_scr[slot, hh], preferred_element_type=F32)
                acc_scr[hh] = a_scr[slot, hh, 0:1, :] * acc_scr[hh] + pv

        def pair(kb, diag_first, diag_second, more):
            if more:
                scores(0, kb + 2)
            softmax(1, kb + 1, diag_first)
            values(0, kb)
            if more:
                scores(1, kb + 3)
                softmax(0, kb + 2, diag_second)
            values(1, kb + 1)

        scores(0, 0)
        scores(1, 1)
        softmax(0, 0, True)

        @pl.loop(0, jnp.maximum(i - 1, 0))
        def _(t):
            pair(2 * t, False, False, True)

        @pl.when(i >= 1)
        def _():
            pair(2 * (i - 1), False, True, True)

        pair(2 * i, True, False, False)

        o_ref[...] = jnp.concatenate([acc_scr[hh] / l_scr[hh, 0:1, :] for hh in range(2)], axis=0).T
        lses = [m_scr[hh, 0:1, :] + jnp.log(l_scr[hh, 0:1, :]) for hh in range(2)]
        lse_ref[0] = jnp.where(sub == 0, lses[0], jnp.where(sub == 1, lses[1], 0.0))

    return pl.pallas_call(
        body, name="fox_forward", grid=(4, nq),
        in_specs=[pl.BlockSpec((2, LANES, tq), lambda j, i: (j, 0, i)), pl.BlockSpec((2, T, LANES), lambda j, i: (j, 0, 0)),
                  pl.BlockSpec((2, LANES, T), lambda j, i: (j, 0, 0))],
        out_specs=[pl.BlockSpec((tq, LANES), lambda j, i: (i, j)), pl.BlockSpec((1, 8, tq), lambda j, i: (j, 0, i))],
        out_shape=[jax.ShapeDtypeStruct((T, 4 * LANES), F32), jax.ShapeDtypeStruct((4, 8, T), F32)],
        scratch_shapes=[pltpu.VMEM((2, 2, tk, tq), F32), pltpu.VMEM((2, 2, tk, tq), BF), pltpu.VMEM((2, 2, 8, tq), F32),
                        pltpu.VMEM((2, 8, tq), F32), pltpu.VMEM((2, 8, tq), F32), pltpu.VMEM((2, HEAD, tq), F32)],
        compiler_params=_params(("parallel", "arbitrary")),
    )(qat, ka, vat)


def _fox_cotangent(dmix, fox, T, tm):
    def body(do_ref, o_ref, doat_ref, dl_ref):
        lane = lax.broadcasted_iota(jnp.int32, (1, LANES), 1)
        sub = lax.broadcasted_iota(jnp.int32, (8, 1), 0)
        dob = do_ref[...].astype(BF).astype(F32)
        prod_t = (dob * o_ref[...]).T
        d0 = jnp.sum(prod_t[:HEAD], axis=0, keepdims=True)
        d1 = jnp.sum(prod_t[HEAD:], axis=0, keepdims=True)
        dl_ref[0] = jnp.where(sub == 0, d0, jnp.where(sub == 1, d1, 0.0))
        for hh in range(2):
            val = jnp.where(lane < HEAD, dob if hh == 0 else pltpu.roll(dob, HEAD, 1), 0.0)
            doat_ref[hh] = val.T.astype(BF)

    return pl.pallas_call(
        body, name="fox_cotangent", grid=(4, T // tm),
        in_specs=[pl.BlockSpec((tm, LANES), lambda j, i: (i, 4 + j)), pl.BlockSpec((tm, LANES), lambda j, i: (i, j))],
        out_specs=[pl.BlockSpec((2, LANES, tm), lambda j, i: (j, 0, i)), pl.BlockSpec((1, 8, tm), lambda j, i: (j, 0, i))],
        out_shape=[jax.ShapeDtypeStruct((8, LANES, T), BF), jax.ShapeDtypeStruct((4, 8, T), F32)],
        compiler_params=_params(("parallel", "arbitrary")),
    )(dmix, fox)


def _fox_backward(qat, ka, kat, va, doat, lse, dl, T, tq, tk):
    nq, nk = T // tq, T // tk

    def body(qat_ref, ka_ref, kat_ref, va_ref, doat_ref, lse_ref, dl_ref,
             dq_ref, dk_ref, dv_ref, df_ref, dr_ref, dqt, dkt, dvt, df_acc, sdp, pds):
        j, kb = pl.program_id(0), pl.program_id(1)
        lane = lax.broadcasted_iota(jnp.int32, (1, LANES), 1)
        first = (kb * tk) // tq

        @pl.when(kb == 0)
        def _():
            dqt[...] = jnp.zeros(dqt.shape, F32)

        dkt[...] = jnp.zeros(dkt.shape, F32)
        dvt[...] = jnp.zeros(dvt.shape, F32)
        df_acc[...] = jnp.zeros(df_acc.shape, F32)

        RC = 64
        last = nq - 1

        def products(slot, qi):
            q0 = pl.multiple_of(qi * tq, tq)
            for hh in range(2):
                sdp[slot, hh, 0] = jnp.dot(ka_ref[hh], qat_ref[hh, :, pl.ds(q0, tq)], preferred_element_type=F32)
                sdp[slot, hh, 1] = jnp.dot(va_ref[hh], doat_ref[hh, :, pl.ds(q0, tq)], preferred_element_type=F32)

        def softmax_bwd(slot, qi, diagonal, valid):
            q0 = pl.multiple_of(qi * tq, tq)
            shift = kb * tk - first * tq
            col = lax.broadcasted_iota(jnp.int32, (RC, tq), 1)
            row = lax.broadcasted_iota(jnp.int32, (RC, tq), 0)
            for hh in range(2):
                lse_row = lse_ref[0, hh:hh + 1, pl.ds(q0, tq)]
                dl_row = dl_ref[0, hh:hh + 1, pl.ds(q0, tq)]
                rsum = jnp.zeros((1, tq), F32)
                for r in range(tk // RC):
                    rows = slice(r * RC, (r + 1) * RC)
                    p = jnp.exp(sdp[slot, hh, 0, rows, :] - lse_row)
                    p = jnp.where((row + (r * RC + shift) <= col) if diagonal else valid, p, 0.0)
                    ds = p * (sdp[slot, hh, 1, rows, :] - dl_row)
                    pds[slot, hh, 0, rows, :] = p.astype(BF)
                    pds[slot, hh, 1, rows, :] = ds.astype(BF)
                    rsum = rsum + jnp.sum(ds, axis=0, keepdims=True)
                    part = ds[:, 0:LANES]
                    for c in range(1, tq // LANES):
                        part = part + ds[:, c * LANES:(c + 1) * LANES]
                    df_acc[hh, rows, :] += part
                dqt[hh, HEAD:HEAD + 8, pl.ds(q0, tq)] += jnp.broadcast_to(rsum, (8, tq))

        def accumulate(slot, qi):
            q0 = pl.multiple_of(qi * tq, tq)
            for hh in range(2):
                dvt[hh] += lax.dot_general(doat_ref[hh, 0:HEAD, pl.ds(q0, tq)], pds[slot, hh, 0], NT, preferred_element_type=F32)
                dkt[hh] += lax.dot_general(qat_ref[hh, 0:HEAD, pl.ds(q0, tq)], pds[slot, hh, 1], NT, preferred_element_type=F32)
                dqt[hh, 0:HEAD, pl.ds(q0, tq)] += jnp.dot(kat_ref[hh, 0:HEAD, :], pds[slot, hh, 1], preferred_element_type=F32)

        products(0, first)
        products(1, jnp.minimum(first + 1, last))
        softmax_bwd(0, first, True, None)

        @pl.loop(0, (nq - first + 1) // 2)
        def _(t):
            qi = first + 2 * t
            products(0, jnp.minimum(qi + 2, last))
            softmax_bwd(1, jnp.minimum(qi + 1, last), False, qi + 1 <= last)
            accumulate(0, qi)
            products(1, jnp.minimum(qi + 3, last))
            softmax_bwd(0, jnp.minimum(qi + 2, last), False, qi + 2 <= last)
            accumulate(1, jnp.minimum(qi + 1, last))

        dk_ref[...] = jnp.concatenate([dkt[0], dkt[1]], axis=0).T
        dv_ref[...] = jnp.concatenate([dvt[0], dvt[1]], axis=0).T.astype(dv_ref.dtype)
        f0 = -jnp.sum(df_acc[0], axis=1, keepdims=True)
        f1 = -jnp.sum(df_acc[1], axis=1, keepdims=True)
        df_ref[0] = jnp.where(lane == 2 * j, f0, jnp.where(lane == 2 * j + 1, f1, 0.0))

        @pl.when(kb == nk - 1)
        def _():
            for t in range(nq):
                cols = slice(t * tq, (t + 1) * tq)
                dq_ref[cols, :] = jnp.concatenate([dqt[0, 0:HEAD, cols], dqt[1, 0:HEAD, cols]], axis=0).T
                rsum = jnp.concatenate([dqt[0, HEAD:HEAD + 8, cols], dqt[1, HEAD:HEAD + 8, cols],
                                        jnp.zeros((LANES - 16, tq), F32)], axis=0).T
                dr_ref[0, cols, :] = jnp.where(lane == 2 * j, rsum[:, 0:1], jnp.where(lane == 2 * j + 1, rsum[:, 8:9], 0.0))

    trn_full = pl.BlockSpec((2, LANES, T), lambda j, kb: (j, 0, 0))
    nat_blk = pl.BlockSpec((2, tk, LANES), lambda j, kb: (j, kb, 0))
    trn_blk = pl.BlockSpec((2, LANES, tk), lambda j, kb: (j, 0, kb))
    rows = pl.BlockSpec((1, 8, T), lambda j, kb: (j, 0, 0))
    blk = pl.BlockSpec((tk, LANES), lambda j, kb: (kb, j))
    return pl.pallas_call(
        body, name="fox_backward", grid=(4, nk),
        in_specs=[trn_full, nat_blk, trn_blk, nat_blk, trn_full, rows, rows],
        out_specs=[pl.BlockSpec((T, LANES), lambda j, kb: (0, j)), blk, blk, pl.BlockSpec((1, tk, LANES), lambda j, kb: (j, kb, 0)),
                   pl.BlockSpec((1, T, LANES), lambda j, kb: (j, 0, 0))],
        out_shape=[jax.ShapeDtypeStruct((T, 4 * LANES), F32), jax.ShapeDtypeStruct((T, 4 * LANES), F32),
                   jax.ShapeDtypeStruct((T, 4 * LANES), BF), jax.ShapeDtypeStruct((4, T, LANES), F32),
                   jax.ShapeDtypeStruct((4, T, LANES), F32)],
        scratch_shapes=[pltpu.VMEM((2, HEAD + 8, T), F32), pltpu.VMEM((2, HEAD, tk), F32), pltpu.VMEM((2, HEAD, tk), F32),
                        pltpu.VMEM((2, tk, LANES), F32), pltpu.VMEM((2, 2, 2, tk, tq), F32), pltpu.VMEM((2, 2, 2, tk, tq), BF)],
        compiler_params=_params(("arbitrary", "arbitrary")),
    )(qat, ka, kat, va, doat, lse, dl)


def _fgate_bwd_col(ffp, bpad, dfc4, drc4, T):
    def body(ff_ref, b_ref, dfc_ref, drc_ref, dff_ref, db_ref):
        lane = lax.broadcasted_iota(jnp.int32, (1, LANES), 1)
        tri = _tri(False)
        carry = jnp.zeros((1, LANES), F32)
        db = jnp.zeros((1, LANES), F32)
        for blk in reversed(range(T // _FB)):
            rows = slice(blk * _FB, (blk + 1) * _FB)
            dcol = dfc_ref[0, rows, :] + drc_ref[0, rows, :]
            for pair in range(1, 4):
                dcol = dcol + (dfc_ref[pair, rows, :] + drc_ref[pair, rows, :])
            dlf = jnp.dot(tri, dcol, precision=lax.Precision.HIGHEST, preferred_element_type=F32) + carry
            carry = dlf[0:1, :]
            z = ff_ref[blk * _FB:(blk + 1) * _FB, :] + b_ref[...]
            dz = jnp.where(lane < 8, dlf * jax.nn.sigmoid(-z), 0.0)
            dff_ref[blk * _FB:(blk + 1) * _FB, :] = dz.astype(dff_ref.dtype)
            db = db + jnp.sum(dz, axis=0, keepdims=True)
        db_ref[...] = db

    return pl.pallas_call(
        body, name="fgate_bwd",
        out_shape=[jax.ShapeDtypeStruct((T, LANES), BF), jax.ShapeDtypeStruct((1, LANES), F32)],
        compiler_params=pltpu.CompilerParams(vmem_limit_bytes=VMEM_LIMIT),
    )(ffp, bpad, dfc4, drc4)


MESH = pl.DeviceIdType.MESH
N_PEERS = N_DEV - 1


def _place():
    return lax.axis_index("x"), lax.axis_index("y"), lax.axis_index("c")


def _all_gather(shard, rows, g, tm):
    R, W = shard.shape
    T = rows.shape[0]
    steps = T // tm

    def body(w_ref, rows_ref, g_ref, out_ref, norm_ref, send_sems, recv_sems, local_sem):
        x, y, c = _place()
        me, sibling = (x, y, c), (x, y, 1 - c)
        chips = [(1 - x, y), (x, 1 - y), (1 - x, 1 - y)]

        def slot(px, py, pc):
            return out_ref.at[4 * px + 2 * py + pc]

        def copy(k, block, to, src=None):
            return pltpu.make_async_remote_copy(
                src_ref=slot(*block) if src is None else src, dst_ref=slot(*block),
                send_sem=send_sems.at[k], recv_sem=recv_sems.at[k], device_id=to, device_id_type=MESH)

        mine = pltpu.make_async_copy(w_ref, slot(*me), local_sem)
        first = [copy(0, me, sibling, src=w_ref)]
        first += [copy(1 + n, me, (*chip, c), src=w_ref) for n, chip in enumerate(chips)]
        passed = [copy(4 + n, (*chip, c), sibling) for n, chip in enumerate(chips)]

        @pl.when(pl.program_id(0) == 0)
        def _():
            mine.start()
            for cp in first:
                cp.start()

        norm_ref[...] = _rms(rows_ref[...], g_ref[...]).astype(norm_ref.dtype)

        @pl.when(pl.program_id(0) == steps - 1)
        def _():
            for n, chip in enumerate(chips):
                copy(1 + n, (*chip, c), me).wait_recv()
                passed[n].start()
            copy(0, sibling, me).wait_recv()
            for n, chip in enumerate(chips):
                copy(4 + n, (*chip, 1 - c), me).wait_recv()
            for cp in first + passed:
                cp.wait_send()
            mine.wait()

    tile = pl.BlockSpec((tm, D), lambda i: (i, 0))
    return pl.pallas_call(
        body, name="all_gather_weights", grid=(steps,),
        out_shape=[jax.ShapeDtypeStruct((N_DEV, R, W), shard.dtype), jax.ShapeDtypeStruct((T, D), BF)],
        in_specs=[pl.BlockSpec(memory_space=pl.ANY), tile, pl.BlockSpec((1, D), lambda i: (0, 0))],
        out_specs=[pl.BlockSpec(memory_space=pl.ANY), tile],
        scratch_shapes=[pltpu.SemaphoreType.DMA((N_PEERS,)), pltpu.SemaphoreType.DMA((N_PEERS,)), pltpu.SemaphoreType.DMA],
        compiler_params=_params(("arbitrary",)),
    )(shard, rows, g)


def _exchange_copies(src_refs, land_refs, send_sems, recv_sems, scatter):
    x, y, c = _place()
    me = 4 * x + 2 * y + c
    copies = []
    for k, (src_ref, land_ref) in enumerate(zip(src_refs, land_refs)):
        for r in range(1, N_DEV):
            px, py, pc = x ^ (r >> 2), y ^ ((r >> 1) & 1), c ^ (r & 1)
            copies.append(pltpu.make_async_remote_copy(
                src_ref=src_ref.at[4 * px + 2 * py + pc] if scatter else src_ref, dst_ref=land_ref.at[me],
                send_sem=send_sems.at[k * N_PEERS + r - 1], recv_sem=recv_sems.at[k * N_PEERS + r - 1],
                device_id=(px, py, pc), device_id_type=MESH))
    return copies


_HBM = pl.BlockSpec(memory_space=pltpu.HBM)
_SEM = pl.BlockSpec(memory_space=pltpu.SEMAPHORE)
_EFFECT = pltpu.SideEffectType.DATAFLOW_SIDE_EFFECTING


def _exchange_start(name, srcs, lands, scatter):
    n = len(srcs)

    def body(*refs):
        send_sems, recv_sems = refs[2 * n], refs[2 * n + 1]
        for cp in _exchange_copies(refs[:n], refs[n:2 * n], send_sems, recv_sems, scatter):
            cp.start()
        token = refs[-1]
        token[...] = jnp.zeros(token.shape, F32)

    arrays = list(srcs) + list(lands)
    out = pl.pallas_call(
        body, name=name,
        out_shape=(pltpu.SemaphoreType.DMA((n * N_PEERS,)), pltpu.SemaphoreType.DMA((n * N_PEERS,)))
        + tuple(pltpu.HBM(a.shape, a.dtype) for a in arrays) + (jax.ShapeDtypeStruct((8, LANES), F32),),
        in_specs=(_HBM,) * (2 * n), out_specs=(_SEM, _SEM) + (_HBM,) * (2 * n) + (pl.BlockSpec(memory_space=pltpu.VMEM),),
        input_output_aliases={k: 2 + k for k in range(2 * n)},
        compiler_params=pltpu.CompilerParams(has_side_effects=_EFFECT),
    )(*(pltpu.with_memory_space_constraint(a, pltpu.HBM) for a in arrays))
    return out[0], out[1], out[2:2 + n], out[2 + n:2 + 2 * n], out[-1]


def _exchange_wait(name, started, after, scatter):
    send_sems, recv_sems, srcs, lands, _ = started
    n = len(srcs)

    def body(*refs):
        copies = _exchange_copies(refs[:n], refs[n:2 * n], refs[2 * n], refs[2 * n + 1], scatter)
        for cp in copies:
            cp.wait_send()
        for cp in copies:
            cp.wait_recv()

    arrays = list(srcs) + list(lands)
    out = pl.pallas_call(
        body, name=name,
        out_shape=tuple(pltpu.HBM(a.shape, a.dtype) for a in arrays),
        in_specs=(_HBM,) * (2 * n) + (_SEM, _SEM, pl.BlockSpec(memory_space=pl.ANY)), out_specs=(_HBM,) * (2 * n),
        input_output_aliases={k: k for k in range(2 * n)},
        compiler_params=pltpu.CompilerParams(has_side_effects=_EFFECT),
    )(*arrays, send_sems, recv_sems, after)
    return out[:n], out[n:]


def _adam_update(g, w, m, v):
    m2 = ADAM_B1 * m + (1.0 - ADAM_B1) * g
    v2 = ADAM_B2 * v + (1.0 - ADAM_B2) * jnp.square(g)
    m_hat = m2 / (1.0 - ADAM_B1 ** ADAM_STEP)
    v_hat = v2 / (1.0 - ADAM_B2 ** ADAM_STEP)
    return g, -ADAM_LR * (m_hat / (jnp.sqrt(v_hat) + ADAM_EPS) + ADAM_WD * w), m2, v2


def _adamw(name, me, slots, sent, w, m, v):
    R, W = w.shape
    steps = max(k for k in (4, 2, 1) if k == 1 or (R % k == 0 and (R // k) % 16 == 0))
    tr = R // steps

    def body(me_ref, s_ref, *refs):
        if sent is not None:
            g = refs[0][0].astype(F32)
            refs = refs[1:]
        else:
            g = jnp.zeros((tr, W), F32)
        for s in range(N_DEV):
            part = s_ref[s].astype(F32)
            g = g + (part if sent is None else jnp.where(me_ref[0] == s, 0.0, part))
        w_ref, m_ref, v_ref = refs[:3]
        for o, r in zip(refs[3:], _adam_update(g, w_ref[...], m_ref[...], v_ref[...])):
            o[...] = r

    rows = pl.BlockSpec((tr, W), lambda i, me_ref: (i, 0))
    in_specs = [pl.BlockSpec((N_DEV, tr, W), lambda i, me_ref: (0, i, 0))]
    args = [slots]
    if sent is not None:
        in_specs.append(pl.BlockSpec((1, tr, W), lambda i, me_ref: (me_ref[0], i, 0)))
        args.append(sent)
    return pl.pallas_call(
        body, name=name,
        grid_spec=pltpu.PrefetchScalarGridSpec(num_scalar_prefetch=1, grid=(steps,), in_specs=in_specs + [rows] * 3,
                                               out_specs=[rows] * 4),
        out_shape=[jax.ShapeDtypeStruct((R, W), F32)] * 4,
        compiler_params=_params(("arbitrary",)),
    )(me, *args, w, m, v)


def _tables(T):
    pos = jnp.arange(T, dtype=F32)
    inv_freq = 10000.0 ** (-jnp.arange(0, HEAD, 2, dtype=F32) / HEAD)
    ang = pos[:, None] * inv_freq[None, :]
    cos, sin = jnp.cos(ang), jnp.sin(ang)
    cos4 = jnp.tile(cos, (1, 4))
    sin4 = jnp.tile(jnp.concatenate([-sin, sin], axis=1), (1, 2))
    log_g = jnp.log(1.0 - 2.0 ** (-5.0 - jnp.arange(8, dtype=F32)))
    return cos4, sin4, jnp.repeat(log_g, HEAD)[None, :]


def _local_step(x, hn1, mem, target, sp, w_inT, token, fetch_rest, push, push_small):
    T = x.shape[0]
    tm = min(512, T)
    tq = min(256, T)
    tb = min(1024, T)
    cos4, sin4, lg = _tables(T)
    g_fq2 = jnp.tile(sp["g_fox_q"], (1, 2))
    g_fk2 = jnp.tile(sp["g_fox_k"], (1, 2))
    g_ret = sp["g_ret_out"].reshape(1, 8 * HEAD)
    bpad = jnp.pad(sp["b_forget"], ((0, 0), (0, LANES - 8)))
    w_ffT = jnp.pad(w_inT[3584:3592], ((0, LANES - 8), (0, 0)))
    tie = lambda p, tok: p + tok[0:1, 0:1]
    tm2, tm4 = min(1024, T), min(2048, T)

    P, = _mm("proj_in", [[(hn1, w_inT, "nt")]], [], lambda acc, after: (acc,), T, 3584, tm4, 512, [F32],
             params=[jnp.broadcast_to(token[0:1, 0:1], (1, 3584))])
    ffp, = _mm("proj_ff", [[(hn1, w_ffT, "nt")]], [], _ident, T, LANES, tm, LANES, [F32])
    ret, s0 = _ret_fwd(P, cos4, sin4, g_ret, lg, T, tb)
    fc, _ = _fgate_fwd(ffp, bpad, T)
    qat, ka, kat, va, vat = _fox_operands(P, fc, g_fq2, g_fk2, T, tm4)
    fox, lse = _fox_forward(qat, ka, vat, T, min(512, T), tq)
    W = fetch_rest("attn", fox)
    h1, hn2 = _mm("proj_out", [[(ret, W["w_out"], "nn", 0), (fox, W["w_out"], "nn", 1)]], [x], _add_rms_epi, T, D, tm2, D,
                  [F32, BF], params=[sp["g_xattn"]])

    qx, = _mm("proj_xq", [[(hn2, W["w_xq"], "nn")]], [], _ident, T, D, tm2, D, [F32])
    memn, = _rw_fwd("rms_mem", _rms_fn, [(mem, D, 0, False)], [(sp["g_mem"], D, 0, False)], [(BF, D)], N_MEM, N_MEM, 1)
    kv, = _mm("proj_xkv", [[(memn, W["w_xkvT"], "nt")]], [], _ident, N_MEM, 2 * D, N_MEM, 512, [F32])
    xa_rows = [(qx, XHEAD, 0, True)]
    xa_params = [(sp["g_xq"], XHEAD, 0, False), (sp["g_xk"], XHEAD, 0, False), (kv, XHEAD, 0, True), (kv, XHEAD, 4, True)]
    xo, = _rw_fwd("xattn_fwd", _xattn_fn, xa_rows, xa_params, [(BF, XHEAD)], T, tm4, 4)
    h2, hn3 = _mm("proj_xo", [[(xo, W["w_xo"], "nn")]], [h1], _add_rms_epi, T, D, tm2, D, [F32, BF], params=[sp["g_ffn"]])

    W.update(fetch_rest("ffn", hn3))
    gate, up, act = _mm("ffn_in", [[(hn3, W["w_gateT"], "nt")], [(hn3, W["w_upT"], "nt")]], [], _swiglu_fwd_epi,
                        T, D_FF, tm4, 256, [BF, BF, BF])
    dy, dyb, loss_part = _mm("ffn_out", [[(act, W["w_down"], "nn")]], [h2, target], _add_loss_epi, T, D, tm, D, [F32, BF], n_acc=1)

    gW = {}
    dgate, dup, gW["w_down"] = _ffn_out_bwd(dyb, W["w_down"], gate, up, act, T, tm4, 256)
    gW["w_gateT"], gW["w_upT"] = _mm("dw_gate_up", [[(dgate, hn3, "tn")], [(dup, hn3, "tn")]], [], _each, D_FF, D, 256, D, [BF, BF])
    tok = push("ffn", gW)
    gs = {}
    dh2, dh2b, gs["g_ffn"] = _mm("ffn_in_bwd", [[(dgate, W["w_gateT"], "nn"), (dup, W["w_upT"], "nn")]], [h2, dy], _rms_bwd_epi,
                                 T, D, min(256, T), D, [F32, BF], params=[tie(sp["g_ffn"], tok)], n_acc=1)

    dxo, = _mm("proj_xo_bwd", [[(dh2b, W["w_xo"], "nt")]], [], _ident, T, D, tm2, D, [BF])
    gW["w_xo"], = _mm("dw_xo", [[(xo, dh2b, "tn")]], [], _ident, D, D, 256, D, [BF])
    dqx, gs["g_xq"], gs["g_xk"], dkv_k, dkv_v = _rw_bwd(
        "xattn_bwd", _xattn_fn, xa_rows, xa_params, [(dxo, XHEAD, 0, True)], T, tm4, 4, [BF], [True, True, True, True])
    dkv = jnp.concatenate([dkv_k[:, :D], dkv_v[:, D:]], axis=1)
    gW["w_xq"], = _mm("dw_xq", [[(hn2, dqx, "tn")]], [], _ident, D, D, 256, D, [BF])
    dmemn, = _mm("proj_xkv_bwd", [[(dkv, W["w_xkvT"], "nn")]], [], _ident, N_MEM, D, N_MEM, 512, [F32])
    gW["w_xkvT"], = _mm("dw_xkv", [[(dkv, memn, "tn")]], [], _ident, 2 * D, D, 512, D, [BF])
    tok = push("xattn", gW)
    gs["g_mem"], = _rw_bwd("rms_mem_bwd", _rms_fn, [(mem, D, 0, False)], [(sp["g_mem"], D, 0, False)], [(dmemn, D, 0, False)],
                           N_MEM, N_MEM, 1, [None], [True])
    dh1, dh1b, gs["g_xattn"] = _mm("proj_xq_bwd", [[(dqx, W["w_xq"], "nt")]], [h1, dh2], _rms_bwd_epi, T, D, tm, D, [F32, BF],
                                   params=[tie(sp["g_xattn"], tok)], n_acc=1)

    dmix, = _mm("proj_out_bwd", [[(dh1b, W["w_out"], "nt")]], [], _ident, T, D, tm2, D, [F32])
    gW["w_out"] = jnp.concatenate(_mm("dw_out", [[(ret, dh1b, "tn")], [(fox, dh1b, "tn")]], [], _each, 4 * LANES, D, 256, D,
                                      [BF, BF]), axis=0)
    tok = push("out", gW)
    doat, dl = _fox_cotangent(dmix, fox, T, tm4)
    dqn, dkn, dfv, dfc4, drc4 = _fox_backward(qat, ka, kat, va, doat, lse + tok[0:1, 0:1], dl, T, tq, tq)
    dfq, dfk, gq2, gk2 = _rw_bwd("fox_prep_bwd", _fox_prep_fn, [(P, LANES, 16, True), (P, LANES, 20, True)],
                                 [(g_fq2, LANES, 0, False), (g_fk2, LANES, 0, False)],
                                 [(dqn, LANES, 0, True), (dkn, LANES, 0, True)], T, tm4, 4, [BF, BF], [True, True])
    gs["g_fox_q"] = gq2[:, :HEAD] + gq2[:, HEAD:]
    gs["g_fox_k"] = gk2[:, :HEAD] + gk2[:, HEAD:]
    dff, dbp = _fgate_bwd_col(ffp, bpad, dfc4, drc4, T)
    gs["b_forget"] = dbp[:, :8]
    drq, drk, drv, drg, dg_ret = _ret_bwd(P, cos4, sin4, g_ret, lg, s0, dmix, T, tb)
    gs["g_ret_out"] = dg_ret
    dsecs = [drq, drk, drv, drg, dfq, dfk, dfv]
    g_secs = list(_mm("dw_in", [[(d, hn1, "tn")] for d in dsecs], [], _each, 512, D, LANES, D, [BF] * len(dsecs)))
    g_ff, = _mm("dw_in_ff", [[(dff, hn1, "tn")]], [], _ident, LANES, D, LANES, D, [BF])
    gW["w_inT"] = jnp.concatenate(g_secs + [g_ff[:8]], axis=0)
    tok = push("in", gW)
    grad_x, gs["g_mix"] = _mm("proj_in_bwd", [[(d, w_inT, "nn", k) for k, d in enumerate(dsecs)] + [(dff, w_ffT, "nn")]], [x, dh1],
                              _rms_bwd_first_epi, T, D, tm, D, [F32], params=[tie(sp["g_mix"], tok)], n_acc=1)
    return grad_x, push_small(gs, loss_part)


_CANON = {"w_in": "w_inT", "w_xkv": "w_xkvT", "w_gate": "w_gateT", "w_up": "w_upT"}
_SMALL = (("g_mix", 0, 0, 1024), ("g_xattn", 1, 0, 1024), ("g_mem", 2, 0, 1024), ("g_ffn", 3, 0, 1024),
          ("g_ret_out", 4, 0, 512), ("g_xq", 4, 512, 256), ("g_xk", 4, 768, 256),
          ("g_fox_q", 5, 0, 64), ("g_fox_k", 5, 64, 64), ("b_forget", 5, 128, 8))
_LOSS_AT = (5, 256)


def _pack_small(tree):
    buf = jnp.zeros((SMALL_ROWS, D), F32)
    for name, r, c, n in _SMALL:
        buf = lax.dynamic_update_slice(buf, tree[name].reshape(1, n).astype(F32), (r, c))
    return buf


def _unpack_small(buf, like):
    return {name: buf[r:r + 1, c:c + n].reshape(like[name].shape) for name, r, c, n in _SMALL}


def _canonical(tree, name):
    a = tree[name][0]
    return a.T if W_SHARD[name][1] else a


def _from_canonical(a, name):
    return (a.T if W_SHARD[name][1] else a)[None]


def kernel(x, mem, g_mix, w_in, b_forget, g_ret_out, g_fox_q, g_fox_k, w_out, g_xattn, w_xq, w_xkv, g_mem, g_xq, g_xk, w_xo, g_ffn, w_gate, w_up, w_down, loss_target, m_g_mix, m_w_in, m_b_forget, m_g_ret_out, m_g_fox_q, m_g_fox_k, m_w_out, m_g_xattn, m_w_xq, m_w_xkv, m_g_mem, m_g_xq, m_g_xk, m_w_xo, m_g_ffn, m_w_gate, m_w_up, m_w_down, v_g_mix, v_w_in, v_b_forget, v_g_ret_out, v_g_fox_q, v_g_fox_k, v_w_out, v_g_xattn, v_w_xq, v_w_xkv, v_g_mem, v_g_xq, v_g_xk, v_w_xo, v_g_ffn, v_w_gate, v_w_up, v_w_down):
    names = ("g_mix", "w_in", "b_forget", "g_ret_out", "g_fox_q", "g_fox_k", "w_out", "g_xattn", "w_xq", "w_xkv", "g_mem",
             "g_xq", "g_xk", "w_xo", "g_ffn", "w_gate", "w_up", "w_down")
    w = dict(zip(names, (g_mix, w_in, b_forget, g_ret_out, g_fox_q, g_fox_k, w_out, g_xattn, w_xq, w_xkv, g_mem, g_xq, g_xk,
                         w_xo, g_ffn, w_gate, w_up, w_down)))
    m = dict(zip(names, (m_g_mix, m_w_in, m_b_forget, m_g_ret_out, m_g_fox_q, m_g_fox_k, m_w_out, m_g_xattn, m_w_xq, m_w_xkv,
                         m_g_mem, m_g_xq, m_g_xk, m_w_xo, m_g_ffn, m_w_gate, m_w_up, m_w_down)))
    v = dict(zip(names, (v_g_mix, v_w_in, v_b_forget, v_g_ret_out, v_g_fox_q, v_g_fox_k, v_w_out, v_g_xattn, v_w_xq, v_w_xkv,
                         v_g_mem, v_g_xq, v_g_xk, v_w_xo, v_g_ffn, v_w_gate, v_w_up, v_w_down)))
    small_names = [s[0] for s in _SMALL]
    me = 4 * lax.axis_index("x") + 2 * lax.axis_index("y") + lax.axis_index("c")
    me1 = me.astype(jnp.int32).reshape(1)

    sp = {n: w[n].reshape(1, -1) for n in small_names}
    first, hn1 = _all_gather(_canonical(w, "w_in").astype(BF), x[0], sp["g_mix"], min(1024, x.shape[1]))
    first, rests = lax.optimization_barrier((first, {g: [_canonical(w, n).astype(BF) for n in ns] for g, ns in GATHER_REST.items()}))
    rest_started = {g: _exchange_start("gather_%s_start" % g, rests[g], [lax.empty((N_DEV,) + a.shape, BF) for a in rests[g]],
                                       scatter=False) for g in GATHER_REST}
    after = rest_started["attn"][4] + rest_started["ffn"][4]

    def fetch_rest(group, after):
        srcs, lands = _exchange_wait("gather_%s_wait" % group, rest_started[group], after, scatter=False)
        lands = [lax.dynamic_update_index_in_dim(a, own, me, axis=0) for a, own in zip(lands, srcs)]
        return {_CANON.get(n, n): a.reshape(N_DEV * a.shape[1], D) for n, a in zip(GATHER_REST[group], lands)}

    pushed = {}

    def push(group, grads):
        srcs = [grads[_CANON.get(n, n)].reshape(N_DEV, W_SHARD[n][0], D) for n in SCATTER_GROUPS[group]]
        pushed[group] = _exchange_start("scatter_%s_start" % group, srcs, [lax.empty(a.shape, BF) for a in srcs], scatter=True)
        return pushed[group][4]

    def push_small(gs, loss_part):
        small = lax.dynamic_update_slice(_pack_small(gs), loss_part[:, :1], _LOSS_AT)
        pushed["small"] = _exchange_start("gather_small_start", [small], [jnp.broadcast_to(small[None], (N_DEV,) + small.shape)],
                                          scatter=False)
        return pushed["small"][4]

    grad_x, done = _local_step(x[0], hn1, mem[0], loss_target[0], sp, first.reshape(N_DEV * W_SHARD["w_in"][0], D),
                               after, fetch_rest, push, push_small)

    results, after = {}, done
    for group in ("ffn", "xattn", "out", "small", "in"):
        if group == "small":
            recv_small = _exchange_wait("gather_small_wait", pushed["small"], after, scatter=False)[1][0]
            g_sm, d_sm, m_sm, v_sm = _adamw("adamw_small", me1, recv_small, None, _pack_small(w), _pack_small(m), _pack_small(v))
            after = g_sm
            continue
        sents, recvs = _exchange_wait("scatter_%s_wait" % group, pushed[group], after, scatter=True)
        for name, sent, recv in zip(SCATTER_GROUPS[group], sents, recvs):
            res = _adamw("adamw_" + name, me1, recv, sent, *(_canonical(t, name) for t in (w, m, v)))
            results[name] = [_from_canonical(r, name) for r in res]
        after = results[SCATTER_GROUPS[group][-1]][0]
    loss = g_sm[_LOSS_AT[0], _LOSS_AT[1]]

    outs = []
    for k, sm in enumerate((g_sm, d_sm, m_sm, v_sm)):
        tree = _unpack_small(sm, w)
        tree.update({name: res[k] for name, res in results.items()})
        outs += [tree[n] for n in names]
    return (loss, grad_x[None], *outs)
```
